```python
import jax, jax.numpy as jnp
from jax import lax
import numpy as np

D_MODEL = 1024
BATCH = 8
SEQ = 4096
DEPTH = 2

A_WIDTH = D_MODEL // 2
A_GROUPS = 4
A_GROUP_DIM = A_WIDTH // A_GROUPS
CHUNK = 128
B_WIDTH = D_MODEL // 2
B_HEAD_DIM = 64
B_HEADS = B_WIDTH // B_HEAD_DIM
DILATED_PATTERNS = ((128, 1), (512, 4), (2048, 16))
Q_BLOCK = 128
ROPE_THETA = 10000.0
AB_IN = 2 * A_WIDTH + 3 * B_WIDTH
CONV_WIDTH = 31
FFN_DIM = 2816
FFN_CONV_WIDTH = 3
EPS = 1e-6
NEG = -1e30
N_EVEN = (DEPTH + 1) // 2
N_ODD = DEPTH // 2

kernel_name = "hybrid_gmlp_dilated_conformer_convffn"


def rms_norm(x, g):
    xf = x.astype(jnp.float32)
    y = xf * lax.rsqrt(jnp.mean(xf * xf, axis=-1, keepdims=True) + EPS)
    return (y * g.astype(jnp.float32)).astype(x.dtype)


def layer_norm(x, g):
    xf = x.astype(jnp.float32)
    mu = jnp.mean(xf, axis=-1, keepdims=True)
    var = jnp.mean(jnp.square(xf - mu), axis=-1, keepdims=True)
    return ((xf - mu) * lax.rsqrt(var + EPS) * g.astype(jnp.float32)).astype(x.dtype)


def causal_depthwise_conv(x, w, b):
    k = w.shape[0]
    y = lax.conv_general_dilated(
        x, w[:, None, :].astype(x.dtype), window_strides=(1,), padding=((k - 1, 0),),
        dimension_numbers=("NWC", "WIO", "NWC"), feature_group_count=x.shape[-1])
    return y + b


def rotary(x, positions):
    e = x.shape[-1]
    inv_freq = 1.0 / (ROPE_THETA ** (jnp.arange(0, e, 2, dtype=jnp.float32) / e))
    ang = positions.astype(jnp.float32)[..., None] * inv_freq
    cos, sin = jnp.cos(ang)[:, :, None, :], jnp.sin(ang)[:, :, None, :]
    xf = x.astype(jnp.float32)
    x1, x2 = xf[..., : e // 2], xf[..., e // 2:]
    return jnp.concatenate([x1 * cos - x2 * sin, x2 * cos + x1 * sin], -1).astype(x.dtype)


def dilated_branch(q, k, v, window, dilation):
    bn, s, h, e = q.shape
    L = s // dilation
    w = window // dilation
    qb_len = min(Q_BLOCK, L)
    nb = L // qb_len

    def to_sub(t):
        return t.reshape(bn, L, dilation, h, e).transpose(0, 2, 3, 1, 4)

    qs, ks, vs = to_sub(q), to_sub(k), to_sub(v)
    pad = ((0, 0), (0, 0), (0, 0), (w, 0), (0, 0))
    kp, vp = jnp.pad(ks, pad), jnp.pad(vs, pad)
    starts = jnp.arange(nb) * qb_len
    j = jnp.arange(w + qb_len)
    idx = starts[:, None] + j[None, :]
    kb = jnp.take(kp, idx, axis=3)
    vb = jnp.take(vp, idx, axis=3)
    qb = qs.reshape(bn, dilation, h, nb, qb_len, e)
    sc = jnp.einsum("brhnqe,brhnke->brhnqk", qb, kb).astype(jnp.float32) * (e ** -0.5)
    i = jnp.arange(qb_len)
    dist = i[:, None] + w - j[None, :]
    kpos = starts[:, None, None] + j[None, None, :] - w
    mask = (dist >= 0)[None] & (dist <= w)[None] & (kpos >= 0)
    sc = jnp.where(mask, sc, NEG)
    m = jnp.max(sc, axis=-1, keepdims=True)
    p = jnp.exp(sc - m)
    den = jnp.sum(p, axis=-1, keepdims=True)
    o = jnp.einsum("brhnqk,brhnke->brhnqe", p, vb.astype(jnp.float32)) / den
    lse = (m + jnp.log(den))[..., 0]
    o = o.reshape(bn, dilation, h, L, e).transpose(0, 3, 1, 2, 4).reshape(bn, s, h, e)
    lse = lse.reshape(bn, dilation, h, L).transpose(0, 3, 1, 2).reshape(bn, s, h)
    return o, lse


def mixer_ab(h, positions, w_in, a_vnorm_g, a_spatial_w, a_spatial_b, q_norm_g, k_norm_g, w_out):
    bn, s, _ = h.shape
    z = h @ w_in
    ua, va, q, k, v = jnp.split(
        z, [A_WIDTH, 2 * A_WIDTH, 2 * A_WIDTH + B_WIDTH, 2 * A_WIDTH + 2 * B_WIDTH], axis=-1)
    nc = s // CHUNK
    ua = jax.nn.gelu(ua, approximate=False).reshape(bn, nc, CHUNK, A_GROUPS, A_GROUP_DIM)
    va = layer_norm(jax.nn.gelu(va, approximate=False).reshape(bn, s, A_GROUPS, A_GROUP_DIM), a_vnorm_g)
    va = va.reshape(bn, nc, CHUNK, A_GROUPS, A_GROUP_DIM)
    causal = jnp.tril(jnp.ones((CHUNK, CHUNK), dtype=bool))
    ws = jnp.where(causal[None], a_spatial_w, 0.0).astype(va.dtype)
    f = jnp.einsum("gts,bcsgd->bctgd", ws, va) + a_spatial_b.T[None, None, :, :, None]
    ya = (ua * f).reshape(bn, s, A_WIDTH)
    q = rotary(rms_norm(q.reshape(bn, s, B_HEADS, B_HEAD_DIM), q_norm_g), positions)
    k = rotary(rms_norm(k.reshape(bn, s, B_HEADS, B_HEAD_DIM), k_norm_g), positions)
    v = v.reshape(bn, s, B_HEADS, B_HEAD_DIM)
    outs, lses = [], []
    for window, dilation in DILATED_PATTERNS:
        o, lse = dilated_branch(q, k, v, window, dilation)
        outs.append(o)
        lses.append(lse)
    wts = jax.nn.softmax(jnp.stack(lses, 0), axis=0)
    yb = jnp.sum(wts[..., None] * jnp.stack(outs, 0), axis=0)
    yb = yb.astype(h.dtype).reshape(bn, s, B_WIDTH)
    return jnp.concatenate([ya, yb], axis=-1) @ w_out


def conformer_conv(h, pw1_w, pw1_b, dw_w, dw_b, ln_g, ln_b, pw2_w, pw2_b):
    a, g = jnp.split(h @ pw1_w + pw1_b, 2, axis=-1)
    y = a * jax.nn.sigmoid(g)
    y = causal_depthwise_conv(y, dw_w, dw_b)
    y = jax.nn.silu(layer_norm(y, ln_g) + ln_b)
    return y @ pw2_w + pw2_b


def conv_ffn(h, up_w, dw_w, dw_b, down_w):
    z = causal_depthwise_conv(h @ up_w, dw_w, dw_b)
    a, b = jnp.split(z, 2, axis=-1)
    return (jax.nn.silu(a) * b) @ down_w


def modulate(x, g, shift, scale):
    return rms_norm(x, g) * (1.0 + scale[:, None, :]) + shift[:, None, :]


def _fwd_setup_inputs(seed: int = 0) -> dict:
    key = jax.random.key(seed)
    ks = jax.random.split(key, 32)
    D, F = D_MODEL, FFN_DIM

    def nrm(k, shape, scale):
        return jax.random.normal(k, shape, jnp.float32) * scale

    start = jax.random.randint(ks[2], (BATCH, 1), 0, 1024, dtype=jnp.int32)
    positions = (start + jnp.arange(SEQ, dtype=jnp.int32)[None, :]).astype(jnp.int32)
    return {
        "x": nrm(ks[0], (BATCH, SEQ, D), 1.0),
        "c": nrm(ks[1], (BATCH, D), 1.0),
        "positions": positions,
        "ada_w": nrm(ks[3], (DEPTH, D, 6 * D), 0.5 * D ** -0.5),
        "ada_b": nrm(ks[4], (DEPTH, 6 * D), 0.01),
        "norm_mix_g": 1.0 + nrm(ks[5], (DEPTH, D), 0.02),
        "norm_ffn_g": 1.0 + nrm(ks[6], (DEPTH, D), 0.02),
        "ab_w_in": nrm(ks[7], (N_EVEN, D, AB_IN), D ** -0.5),
        "a_vnorm_g": 1.0 + nrm(ks[8], (N_EVEN, A_GROUPS, A_GROUP_DIM), 0.02),
        "a_spatial_w": nrm(ks[9], (N_EVEN, A_GROUPS, CHUNK, CHUNK), CHUNK ** -0.5),
        "a_spatial_b": 1.0 + nrm(ks[10], (N_EVEN, A_GROUPS, CHUNK), 0.1),
        "b_q_norm_g": 1.0 + nrm(ks[11], (N_EVEN, B_HEAD_DIM), 0.02),
        "b_k_norm_g": 1.0 + nrm(ks[12], (N_EVEN, B_HEAD_DIM), 0.02),
        "ab_w_out": nrm(ks[13], (N_EVEN, A_WIDTH + B_WIDTH, D), (A_WIDTH + B_WIDTH) ** -0.5),
        "conv_pw1_w": nrm(ks[14], (N_ODD, D, 2 * D), D ** -0.5),
        "conv_pw1_b": nrm(ks[15], (N_ODD, 2 * D), 0.01),
        "conv_dw_w": nrm(ks[16], (N_ODD, CONV_WIDTH, D), CONV_WIDTH ** -0.5),
        "conv_dw_b": nrm(ks[17], (N_ODD, D), 0.01),
        "conv_ln_g": 1.0 + nrm(ks[18], (N_ODD, D), 0.02),
        "conv_ln_b": nrm(ks[19], (N_ODD, D), 0.01),
        "conv_pw2_w": nrm(ks[20], (N_ODD, D, D), D ** -0.5),
        "conv_pw2_b": nrm(ks[21], (N_ODD, D), 0.01),
        "ffn_up_w": nrm(ks[22], (DEPTH, D, 2 * F), D ** -0.5),
        "ffn_dw_w": nrm(ks[23], (DEPTH, FFN_CONV_WIDTH, 2 * F), FFN_CONV_WIDTH ** -0.5),
        "ffn_dw_b": nrm(ks[24], (DEPTH, 2 * F), 0.01),
        "ffn_down_w": nrm(ks[25], (DEPTH, F, D), F ** -0.5),
    }


def _fwd_reference(x, c, positions, ada_w, ada_b, norm_mix_g, norm_ffn_g, ab_w_in, a_vnorm_g,
              a_spatial_w, a_spatial_b, b_q_norm_g, b_k_norm_g, ab_w_out, conv_pw1_w,
              conv_pw1_b, conv_dw_w, conv_dw_b, conv_ln_g, conv_ln_b, conv_pw2_w, conv_pw2_b,
              ffn_up_w, ffn_dw_w, ffn_dw_b, ffn_down_w):
    c_act = jax.nn.silu(c)
    for layer in range(DEPTH):
        mod = c_act @ ada_w[layer] + ada_b[layer]
        sh_m, sc_m, g_m, sh_f, sc_f, g_f = jnp.split(mod, 6, axis=-1)
        h = modulate(x, norm_mix_g[layer], sh_m, sc_m)
        li = layer // 2
        if layer % 2 == 0:
            y = mixer_ab(h, positions, ab_w_in[li], a_vnorm_g[li], a_spatial_w[li],
                         a_spatial_b[li], b_q_norm_g[li], b_k_norm_g[li], ab_w_out[li])
        else:
            y = conformer_conv(h, conv_pw1_w[li], conv_pw1_b[li], conv_dw_w[li], conv_dw_b[li],
                               conv_ln_g[li], conv_ln_b[li], conv_pw2_w[li], conv_pw2_b[li])
        x = x + g_m[:, None, :] * y
        h = modulate(x, norm_ffn_g[layer], sh_f, sc_f)
        x = x + g_f[:, None, :] * conv_ffn(h, ffn_up_w[layer], ffn_dw_w[layer],
                                           ffn_dw_b[layer], ffn_down_w[layer])
    return x


import jax as _jax
import jax.numpy as _jnp

TWIN_FORMAT = 'train_step'
FWD_PARAMS = ['x', 'c', 'positions', 'ada_w', 'ada_b', 'norm_mix_g', 'norm_ffn_g', 'ab_w_in', 'a_vnorm_g', 'a_spatial_w', 'a_spatial_b', 'b_q_norm_g', 'b_k_norm_g', 'ab_w_out', 'conv_pw1_w', 'conv_pw1_b', 'conv_dw_w', 'conv_dw_b', 'conv_ln_g', 'conv_ln_b', 'conv_pw2_w', 'conv_pw2_b', 'ffn_up_w', 'ffn_dw_w', 'ffn_dw_b', 'ffn_down_w']
TWIN_WEIGHTS = ['ada_w', 'ada_b', 'norm_mix_g', 'norm_ffn_g', 'ab_w_in', 'a_vnorm_g', 'a_spatial_w', 'a_spatial_b', 'b_q_norm_g', 'b_k_norm_g', 'ab_w_out', 'conv_pw1_w', 'conv_pw1_b', 'conv_dw_w', 'conv_dw_b', 'conv_ln_g', 'conv_ln_b', 'conv_pw2_w', 'conv_pw2_b', 'ffn_up_w', 'ffn_dw_w', 'ffn_dw_b', 'ffn_down_w']
TWIN_DIFF_INPUT = 'x'
TWIN_INPUTS = ['x', 'c', 'positions', 'ada_w', 'ada_b', 'norm_mix_g', 'norm_ffn_g', 'ab_w_in', 'a_vnorm_g', 'a_spatial_w', 'a_spatial_b', 'b_q_norm_g', 'b_k_norm_g', 'ab_w_out', 'conv_pw1_w', 'conv_pw1_b', 'conv_dw_w', 'conv_dw_b', 'conv_ln_g', 'conv_ln_b', 'conv_pw2_w', 'conv_pw2_b', 'ffn_up_w', 'ffn_dw_w', 'ffn_dw_b', 'ffn_down_w', 'loss_target', 'm_ada_w', 'm_ada_b', 'm_norm_mix_g', 'm_norm_ffn_g', 'm_ab_w_in', 'm_a_vnorm_g', 'm_a_spatial_w', 'm_a_spatial_b', 'm_b_q_norm_g', 'm_b_k_norm_g', 'm_ab_w_out', 'm_conv_pw1_w', 'm_conv_pw1_b', 'm_conv_dw_w', 'm_conv_dw_b', 'm_conv_ln_g', 'm_conv_ln_b', 'm_conv_pw2_w', 'm_conv_pw2_b', 'm_ffn_up_w', 'm_ffn_dw_w', 'm_ffn_dw_b', 'm_ffn_down_w', 'v_ada_w', 'v_ada_b', 'v_norm_mix_g', 'v_norm_ffn_g', 'v_ab_w_in', 'v_a_vnorm_g', 'v_a_spatial_w', 'v_a_spatial_b', 'v_b_q_norm_g', 'v_b_k_norm_g', 'v_ab_w_out', 'v_conv_pw1_w', 'v_conv_pw1_b', 'v_conv_dw_w', 'v_conv_dw_b', 'v_conv_ln_g', 'v_conv_ln_b', 'v_conv_pw2_w', 'v_conv_pw2_b', 'v_ffn_up_w', 'v_ffn_dw_w', 'v_ffn_dw_b', 'v_ffn_down_w']
TWIN_OUTPUTS = ['loss', 'grad_x', 'grad_ada_w', 'grad_ada_b', 'grad_norm_mix_g', 'grad_norm_ffn_g', 'grad_ab_w_in', 'grad_a_vnorm_g', 'grad_a_spatial_w', 'grad_a_spatial_b', 'grad_b_q_norm_g', 'grad_b_k_norm_g', 'grad_ab_w_out', 'grad_conv_pw1_w', 'grad_conv_pw1_b', 'grad_conv_dw_w', 'grad_conv_dw_b', 'grad_conv_ln_g', 'grad_conv_ln_b', 'grad_conv_pw2_w', 'grad_conv_pw2_b', 'grad_ffn_up_w', 'grad_ffn_dw_w', 'grad_ffn_dw_b', 'grad_ffn_down_w', 'delta_ada_w', 'delta_ada_b', 'delta_norm_mix_g', 'delta_norm_ffn_g', 'delta_ab_w_in', 'delta_a_vnorm_g', 'delta_a_spatial_w', 'delta_a_spatial_b', 'delta_b_q_norm_g', 'delta_b_k_norm_g', 'delta_ab_w_out', 'delta_conv_pw1_w', 'delta_conv_pw1_b', 'delta_conv_dw_w', 'delta_conv_dw_b', 'delta_conv_ln_g', 'delta_conv_ln_b', 'delta_conv_pw2_w', 'delta_conv_pw2_b', 'delta_ffn_up_w', 'delta_ffn_dw_w', 'delta_ffn_dw_b', 'delta_ffn_down_w', 'new_m_ada_w', 'new_m_ada_b', 'new_m_norm_mix_g', 'new_m_norm_ffn_g', 'new_m_ab_w_in', 'new_m_a_vnorm_g', 'new_m_a_spatial_w', 'new_m_a_spatial_b', 'new_m_b_q_norm_g', 'new_m_b_k_norm_g', 'new_m_ab_w_out', 'new_m_conv_pw1_w', 'new_m_conv_pw1_b', 'new_m_conv_dw_w', 'new_m_conv_dw_b', 'new_m_conv_ln_g', 'new_m_conv_ln_b', 'new_m_conv_pw2_w', 'new_m_conv_pw2_b', 'new_m_ffn_up_w', 'new_m_ffn_dw_w', 'new_m_ffn_dw_b', 'new_m_ffn_down_w', 'new_v_ada_w', 'new_v_ada_b', 'new_v_norm_mix_g', 'new_v_norm_ffn_g', 'new_v_ab_w_in', 'new_v_a_vnorm_g', 'new_v_a_spatial_w', 'new_v_a_spatial_b', 'new_v_b_q_norm_g', 'new_v_b_k_norm_g', 'new_v_ab_w_out', 'new_v_conv_pw1_w', 'new_v_conv_pw1_b', 'new_v_conv_dw_w', 'new_v_conv_dw_b', 'new_v_conv_ln_g', 'new_v_conv_ln_b', 'new_v_conv_pw2_w', 'new_v_conv_pw2_b', 'new_v_ffn_up_w', 'new_v_ffn_dw_w', 'new_v_ffn_dw_b', 'new_v_ffn_down_w']
TWIN_LEAF_KINDS = {'loss': 'loss', 'grad_x': 'grad_x', 'grad_ada_w': 'grad_w', 'grad_ada_b': 'grad_w', 'grad_norm_mix_g': 'grad_w', 'grad_norm_ffn_g': 'grad_w', 'grad_ab_w_in': 'grad_w', 'grad_a_vnorm_g': 'grad_w', 'grad_a_spatial_w': 'grad_w', 'grad_a_spatial_b': 'grad_w', 'grad_b_q_norm_g': 'grad_w', 'grad_b_k_norm_g': 'grad_w', 'grad_ab_w_out': 'grad_w', 'grad_conv_pw1_w': 'grad_w', 'grad_conv_pw1_b': 'grad_w', 'grad_conv_dw_w': 'grad_w', 'grad_conv_dw_b': 'grad_w', 'grad_conv_ln_g': 'grad_w', 'grad_conv_ln_b': 'grad_w', 'grad_conv_pw2_w': 'grad_w', 'grad_conv_pw2_b': 'grad_w', 'grad_ffn_up_w': 'grad_w', 'grad_ffn_dw_w': 'grad_w', 'grad_ffn_dw_b': 'grad_w', 'grad_ffn_down_w': 'grad_w', 'delta_ada_w': 'delta_w', 'delta_ada_b': 'delta_w', 'delta_norm_mix_g': 'delta_w', 'delta_norm_ffn_g': 'delta_w', 'delta_ab_w_in': 'delta_w', 'delta_a_vnorm_g': 'delta_w', 'delta_a_spatial_w': 'delta_w', 'delta_a_spatial_b': 'delta_w', 'delta_b_q_norm_g': 'delta_w', 'delta_b_k_norm_g': 'delta_w', 'delta_ab_w_out': 'delta_w', 'delta_conv_pw1_w': 'delta_w', 'delta_conv_pw1_b': 'delta_w', 'delta_conv_dw_w': 'delta_w', 'delta_conv_dw_b': 'delta_w', 'delta_conv_ln_g': 'delta_w', 'delta_conv_ln_b': 'delta_w', 'delta_conv_pw2_w': 'delta_w', 'delta_conv_pw2_b': 'delta_w', 'delta_ffn_up_w': 'delta_w', 'delta_ffn_dw_w': 'delta_w', 'delta_ffn_dw_b': 'delta_w', 'delta_ffn_down_w': 'delta_w', 'new_m_ada_w': 'new_m', 'new_m_ada_b': 'new_m', 'new_m_norm_mix_g': 'new_m', 'new_m_norm_ffn_g': 'new_m', 'new_m_ab_w_in': 'new_m', 'new_m_a_vnorm_g': 'new_m', 'new_m_a_spatial_w': 'new_m', 'new_m_a_spatial_b': 'new_m', 'new_m_b_q_norm_g': 'new_m', 'new_m_b_k_norm_g': 'new_m', 'new_m_ab_w_out': 'new_m', 'new_m_conv_pw1_w': 'new_m', 'new_m_conv_pw1_b': 'new_m', 'new_m_conv_dw_w': 'new_m', 'new_m_conv_dw_b': 'new_m', 'new_m_conv_ln_g': 'new_m', 'new_m_conv_ln_b': 'new_m', 'new_m_conv_pw2_w': 'new_m', 'new_m_conv_pw2_b': 'new_m', 'new_m_ffn_up_w': 'new_m', 'new_m_ffn_dw_w': 'new_m', 'new_m_ffn_dw_b': 'new_m', 'new_m_ffn_down_w': 'new_m', 'new_v_ada_w': 'new_v', 'new_v_ada_b': 'new_v', 'new_v_norm_mix_g': 'new_v', 'new_v_norm_ffn_g': 'new_v', 'new_v_ab_w_in': 'new_v', 'new_v_a_vnorm_g': 'new_v', 'new_v_a_spatial_w': 'new_v', 'new_v_a_spatial_b': 'new_v', 'new_v_b_q_norm_g': 'new_v', 'new_v_b_k_norm_g': 'new_v', 'new_v_ab_w_out': 'new_v', 'new_v_conv_pw1_w': 'new_v', 'new_v_conv_pw1_b': 'new_v', 'new_v_conv_dw_w': 'new_v', 'new_v_conv_dw_b': 'new_v', 'new_v_conv_ln_g': 'new_v', 'new_v_conv_ln_b': 'new_v', 'new_v_conv_pw2_w': 'new_v', 'new_v_conv_pw2_b': 'new_v', 'new_v_ffn_up_w': 'new_v', 'new_v_ffn_dw_w': 'new_v', 'new_v_ffn_dw_b': 'new_v', 'new_v_ffn_down_w': 'new_v'}


def _forward(args):
    return _fwd_reference(*[args[k] for k in FWD_PARAMS])


def _output_shape():
    out = _jax.eval_shape(lambda: _forward(_fwd_setup_inputs(0)))
    return out.shape, out.dtype

N_MICROBATCH = 1
ADAM_LR = 0.001
ADAM_B1 = 0.9
ADAM_B2 = 0.999
ADAM_EPS = 1e-08
ADAM_WD = 0.01
ADAM_STEP = 10
PER_EXAMPLE_BATCH_AXIS = {'x': 0, 'c': 0, 'positions': 0, 'loss_target': 0}
SHARED_INPUTS = []
_WEIGHT_DTYPES = {'ada_w': _jnp.float32, 'ada_b': _jnp.float32, 'norm_mix_g': _jnp.float32, 'norm_ffn_g': _jnp.float32, 'ab_w_in': _jnp.float32, 'a_vnorm_g': _jnp.float32, 'a_spatial_w': _jnp.float32, 'a_spatial_b': _jnp.float32, 'b_q_norm_g': _jnp.float32, 'b_k_norm_g': _jnp.float32, 'ab_w_out': _jnp.float32, 'conv_pw1_w': _jnp.float32, 'conv_pw1_b': _jnp.float32, 'conv_dw_w': _jnp.float32, 'conv_dw_b': _jnp.float32, 'conv_ln_g': _jnp.float32, 'conv_ln_b': _jnp.float32, 'conv_pw2_w': _jnp.float32, 'conv_pw2_b': _jnp.float32, 'ffn_up_w': _jnp.float32, 'ffn_dw_w': _jnp.float32, 'ffn_dw_b': _jnp.float32, 'ffn_down_w': _jnp.float32}
MOMENT_SCALE = {'ada_w': 1.066296e+00, 'ada_b': 2.283187e+00, 'norm_mix_g': 8.814080e-01, 'norm_ffn_g': 3.645744e+00, 'ab_w_in': 1.871547e-01, 'a_vnorm_g': 7.870017e-01, 'a_spatial_w': 8.301434e-02, 'a_spatial_b': 1.596303e+00, 'b_q_norm_g': 1.464427e-01, 'b_k_norm_g': 1.462349e-01, 'ab_w_out': 3.300660e-01, 'conv_pw1_w': 8.203198e-02, 'conv_pw1_b': 3.777639e-01, 'conv_dw_w': 1.198918e-01, 'conv_dw_b': 8.179281e-01, 'conv_ln_g': 1.412754e+00, 'conv_ln_b': 9.735997e-01, 'conv_pw2_w': 2.039247e-01, 'conv_pw2_b': 9.910237e-01, 'ffn_up_w': 1.272506e-01, 'ffn_dw_w': 5.446847e-01, 'ffn_dw_b': 4.513674e-01, 'ffn_down_w': 9.730546e-02}


def _to_microbatches(a, axis):
    t = _jnp.moveaxis(a, axis, 0)
    t = t.reshape((N_MICROBATCH, t.shape[0] // N_MICROBATCH) + t.shape[1:])
    return _jnp.moveaxis(t, 1, axis + 1)


def setup_inputs(seed: int = 0) -> dict:
    inp = _fwd_setup_inputs(seed)
    key = _jax.random.fold_in(_jax.random.key(seed), 7919)
    shape, _ = _output_shape()
    out = dict(inp)
    out["loss_target"] = _jax.random.normal(_jax.random.fold_in(key, 0), shape, _jnp.float32)
    for i, name in enumerate(TWIN_WEIGHTS):
        w = inp[name].astype(_jnp.float32)
        if MOMENT_SCALE is None:
            s = _jnp.sqrt(_jnp.mean(_jnp.square(w)) + 1e-30)
        else:
            s = MOMENT_SCALE[name]
        km, kv = _jax.random.split(_jax.random.fold_in(key, i + 1))
        out[name] = w
        out["m_" + name] = s * _jax.random.normal(km, w.shape, _jnp.float32)
        out["v_" + name] = (s * s) * _jax.random.uniform(kv, w.shape, _jnp.float32, 0.5, 1.5)
    if N_MICROBATCH > 1:
        for name, axis in PER_EXAMPLE_BATCH_AXIS.items():
            out[name] = _to_microbatches(out[name], axis)
    return {'x': out['x'], 'c': out['c'], 'positions': out['positions'], 'ada_w': out['ada_w'], 'ada_b': out['ada_b'], 'norm_mix_g': out['norm_mix_g'], 'norm_ffn_g': out['norm_ffn_g'], 'ab_w_in': out['ab_w_in'], 'a_vnorm_g': out['a_vnorm_g'], 'a_spatial_w': out['a_spatial_w'], 'a_spatial_b': out['a_spatial_b'], 'b_q_norm_g': out['b_q_norm_g'], 'b_k_norm_g': out['b_k_norm_g'], 'ab_w_out': out['ab_w_out'], 'conv_pw1_w': out['conv_pw1_w'], 'conv_pw1_b': out['conv_pw1_b'], 'conv_dw_w': out['conv_dw_w'], 'conv_dw_b': out['conv_dw_b'], 'conv_ln_g': out['conv_ln_g'], 'conv_ln_b': out['conv_ln_b'], 'conv_pw2_w': out['conv_pw2_w'], 'conv_pw2_b': out['conv_pw2_b'], 'ffn_up_w': out['ffn_up_w'], 'ffn_dw_w': out['ffn_dw_w'], 'ffn_dw_b': out['ffn_dw_b'], 'ffn_down_w': out['ffn_down_w'], 'loss_target': out['loss_target'], 'm_ada_w': out['m_ada_w'], 'm_ada_b': out['m_ada_b'], 'm_norm_mix_g': out['m_norm_mix_g'], 'm_norm_ffn_g': out['m_norm_ffn_g'], 'm_ab_w_in': out['m_ab_w_in'], 'm_a_vnorm_g': out['m_a_vnorm_g'], 'm_a_spatial_w': out['m_a_spatial_w'], 'm_a_spatial_b': out['m_a_spatial_b'], 'm_b_q_norm_g': out['m_b_q_norm_g'], 'm_b_k_norm_g': out['m_b_k_norm_g'], 'm_ab_w_out': out['m_ab_w_out'], 'm_conv_pw1_w': out['m_conv_pw1_w'], 'm_conv_pw1_b': out['m_conv_pw1_b'], 'm_conv_dw_w': out['m_conv_dw_w'], 'm_conv_dw_b': out['m_conv_dw_b'], 'm_conv_ln_g': out['m_conv_ln_g'], 'm_conv_ln_b': out['m_conv_ln_b'], 'm_conv_pw2_w': out['m_conv_pw2_w'], 'm_conv_pw2_b': out['m_conv_pw2_b'], 'm_ffn_up_w': out['m_ffn_up_w'], 'm_ffn_dw_w': out['m_ffn_dw_w'], 'm_ffn_dw_b': out['m_ffn_dw_b'], 'm_ffn_down_w': out['m_ffn_down_w'], 'v_ada_w': out['v_ada_w'], 'v_ada_b': out['v_ada_b'], 'v_norm_mix_g': out['v_norm_mix_g'], 'v_norm_ffn_g': out['v_norm_ffn_g'], 'v_ab_w_in': out['v_ab_w_in'], 'v_a_vnorm_g': out['v_a_vnorm_g'], 'v_a_spatial_w': out['v_a_spatial_w'], 'v_a_spatial_b': out['v_a_spatial_b'], 'v_b_q_norm_g': out['v_b_q_norm_g'], 'v_b_k_norm_g': out['v_b_k_norm_g'], 'v_ab_w_out': out['v_ab_w_out'], 'v_conv_pw1_w': out['v_conv_pw1_w'], 'v_conv_pw1_b': out['v_conv_pw1_b'], 'v_conv_dw_w': out['v_conv_dw_w'], 'v_conv_dw_b': out['v_conv_dw_b'], 'v_conv_ln_g': out['v_conv_ln_g'], 'v_conv_ln_b': out['v_conv_ln_b'], 'v_conv_pw2_w': out['v_conv_pw2_w'], 'v_conv_pw2_b': out['v_conv_pw2_b'], 'v_ffn_up_w': out['v_ffn_up_w'], 'v_ffn_dw_w': out['v_ffn_dw_w'], 'v_ffn_dw_b': out['v_ffn_dw_b'], 'v_ffn_down_w': out['v_ffn_down_w']}


def _loss(weights, diff, rest, loss_target):
    with _jax.named_scope("forward"):
        args = {**rest, TWIN_DIFF_INPUT: diff, **{k: w.astype(_WEIGHT_DTYPES[k]) for k, w in weights.items()}}
        y = _forward(args)
    with _jax.named_scope("loss_head"):
        err = _jnp.square(y.astype(_jnp.float32) - loss_target)
        return 0.5 * _jnp.sum(_jnp.mean(err, axis=-1)) if err.ndim else 0.5 * err


def _adamw(w, g, m, v):
    m = ADAM_B1 * m + (1.0 - ADAM_B1) * g
    v = ADAM_B2 * v + (1.0 - ADAM_B2) * _jnp.square(g)
    m_hat = m / (1.0 - ADAM_B1 ** ADAM_STEP)
    v_hat = v / (1.0 - ADAM_B2 ** ADAM_STEP)
    delta = -ADAM_LR * (m_hat / (_jnp.sqrt(v_hat) + ADAM_EPS) + ADAM_WD * w)
    return delta, m, v


def reference(x, c, positions, ada_w, ada_b, norm_mix_g, norm_ffn_g, ab_w_in, a_vnorm_g, a_spatial_w, a_spatial_b, b_q_norm_g, b_k_norm_g, ab_w_out, conv_pw1_w, conv_pw1_b, conv_dw_w, conv_dw_b, conv_ln_g, conv_ln_b, conv_pw2_w, conv_pw2_b, ffn_up_w, ffn_dw_w, ffn_dw_b, ffn_down_w, loss_target, m_ada_w, m_ada_b, m_norm_mix_g, m_norm_ffn_g, m_ab_w_in, m_a_vnorm_g, m_a_spatial_w, m_a_spatial_b, m_b_q_norm_g, m_b_k_norm_g, m_ab_w_out, m_conv_pw1_w, m_conv_pw1_b, m_conv_dw_w, m_conv_dw_b, m_conv_ln_g, m_conv_ln_b, m_conv_pw2_w, m_conv_pw2_b, m_ffn_up_w, m_ffn_dw_w, m_ffn_dw_b, m_ffn_down_w, v_ada_w, v_ada_b, v_norm_mix_g, v_norm_ffn_g, v_ab_w_in, v_a_vnorm_g, v_a_spatial_w, v_a_spatial_b, v_b_q_norm_g, v_b_k_norm_g, v_ab_w_out, v_conv_pw1_w, v_conv_pw1_b, v_conv_dw_w, v_conv_dw_b, v_conv_ln_g, v_conv_ln_b, v_conv_pw2_w, v_conv_pw2_b, v_ffn_up_w, v_ffn_dw_w, v_ffn_dw_b, v_ffn_down_w):
    given = dict(x=x, c=c, positions=positions, ada_w=ada_w, ada_b=ada_b, norm_mix_g=norm_mix_g, norm_ffn_g=norm_ffn_g, ab_w_in=ab_w_in, a_vnorm_g=a_vnorm_g, a_spatial_w=a_spatial_w, a_spatial_b=a_spatial_b, b_q_norm_g=b_q_norm_g, b_k_norm_g=b_k_norm_g, ab_w_out=ab_w_out, conv_pw1_w=conv_pw1_w, conv_pw1_b=conv_pw1_b, conv_dw_w=conv_dw_w, conv_dw_b=conv_dw_b, conv_ln_g=conv_ln_g, conv_ln_b=conv_ln_b, conv_pw2_w=conv_pw2_w, conv_pw2_b=conv_pw2_b, ffn_up_w=ffn_up_w, ffn_dw_w=ffn_dw_w, ffn_dw_b=ffn_dw_b, ffn_down_w=ffn_down_w, loss_target=loss_target, m_ada_w=m_ada_w, m_ada_b=m_ada_b, m_norm_mix_g=m_norm_mix_g, m_norm_ffn_g=m_norm_ffn_g, m_ab_w_in=m_ab_w_in, m_a_vnorm_g=m_a_vnorm_g, m_a_spatial_w=m_a_spatial_w, m_a_spatial_b=m_a_spatial_b, m_b_q_norm_g=m_b_q_norm_g, m_b_k_norm_g=m_b_k_norm_g, m_ab_w_out=m_ab_w_out, m_conv_pw1_w=m_conv_pw1_w, m_conv_pw1_b=m_conv_pw1_b, m_conv_dw_w=m_conv_dw_w, m_conv_dw_b=m_conv_dw_b, m_conv_ln_g=m_conv_ln_g, m_conv_ln_b=m_conv_ln_b, m_conv_pw2_w=m_conv_pw2_w, m_conv_pw2_b=m_conv_pw2_b, m_ffn_up_w=m_ffn_up_w, m_ffn_dw_w=m_ffn_dw_w, m_ffn_dw_b=m_ffn_dw_b, m_ffn_down_w=m_ffn_down_w, v_ada_w=v_ada_w, v_ada_b=v_ada_b, v_norm_mix_g=v_norm_mix_g, v_norm_ffn_g=v_norm_ffn_g, v_ab_w_in=v_ab_w_in, v_a_vnorm_g=v_a_vnorm_g, v_a_spatial_w=v_a_spatial_w, v_a_spatial_b=v_a_spatial_b, v_b_q_norm_g=v_b_q_norm_g, v_b_k_norm_g=v_b_k_norm_g, v_ab_w_out=v_ab_w_out, v_conv_pw1_w=v_conv_pw1_w, v_conv_pw1_b=v_conv_pw1_b, v_conv_dw_w=v_conv_dw_w, v_conv_dw_b=v_conv_dw_b, v_conv_ln_g=v_conv_ln_g, v_conv_ln_b=v_conv_ln_b, v_conv_pw2_w=v_conv_pw2_w, v_conv_pw2_b=v_conv_pw2_b, v_ffn_up_w=v_ffn_up_w, v_ffn_dw_w=v_ffn_dw_w, v_ffn_dw_b=v_ffn_dw_b, v_ffn_down_w=v_ffn_down_w)
    weights = {n: given[n] for n in TWIN_WEIGHTS}
    shared = {n: given[n] for n in SHARED_INPUTS}
    per_example = {n: given[n] for n in ['x', 'c', 'positions']}
    grad_fn = _jax.value_and_grad(_loss, argnums=(0, 1))

    def one_microbatch(ex, loss_target):
        ex = dict(ex)
        diff = ex.pop(TWIN_DIFF_INPUT)
        return grad_fn(weights, diff, {**shared, **ex}, loss_target)

    if N_MICROBATCH == 1:
        loss, (grad_w, grad_x) = one_microbatch(per_example, given["loss_target"])
    else:
        def body(carry, xs):
            loss_sum, grad_sum = carry
            l_k, (gw_k, gx_k) = one_microbatch(xs[0], xs[1])
            with _jax.named_scope("update"):
                return (loss_sum + l_k, _jax.tree.map(_jnp.add, grad_sum, gw_k)), gx_k

        init = (_jnp.zeros((), _jnp.float32), _jax.tree.map(_jnp.zeros_like, weights))
        (loss, grad_w), grad_x = _jax.lax.scan(body, init, (per_example, given["loss_target"]))
    with _jax.named_scope("update"):
        delta_w, new_m, new_v = {}, {}, {}
        for n in TWIN_WEIGHTS:
            delta_w[n], new_m[n], new_v[n] = _adamw(weights[n], grad_w[n], given["m_" + n], given["v_" + n])
    return (loss, grad_x, *[grad_w[n] for n in TWIN_WEIGHTS], *[delta_w[n] for n in TWIN_WEIGHTS],
            *[new_m[n] for n in TWIN_WEIGHTS], *[new_v[n] for n in TWIN_WEIGHTS])
```

```python
import functools
import math

import jax
import jax.numpy as jnp
from jax import lax
from jax.experimental import pallas as pl
from jax.experimental.pallas import tpu as pltpu

F32 = jnp.float32
BF16 = jnp.bfloat16
MESH = pl.DeviceIdType.MESH

D_MODEL = 1024
A_WIDTH = 512
A_GROUPS = 4
GROUP_DIM = 128
CHUNK = 128
B_WIDTH = 512
HEADS = 8
HEAD_DIM = 64
PATTERNS = ((128, 1), (512, 4), (2048, 16))
Q_BLOCK = 128
ROPE_THETA = 10000.0
AB_IN = 2560
CONV_WIDTH = 31
FFN_DIM = 2816
FFN_CONV_WIDTH = 3
EPS = 1e-6
NEG = -1e30
N_DEV = 8
ADAM_LR, ADAM_B1, ADAM_B2, ADAM_EPS, ADAM_WD, ADAM_STEP = 0.001, 0.9, 0.999, 1e-08, 0.01, 10

V7X_VMEM_LIMIT = 56 * 2**20
BF16_ROWS = 16
FFN_HALO = 16
CONV_HALO = 32

_NN = (((1,), (0,)), ((), ()))
_NT = (((1,), (1,)), ((), ()))
_TN = (((0,), (0,)), ((), ()))


def _tile(n, prefs=(512, 256, 128)):
    for t in prefs:
        if n % t == 0:
            return t
    return n


def _row_tile(n, cap=512):
    best = n
    for t in range(8, min(n, cap) + 1, 8):
        if n % t == 0:
            best = t
    return best if best <= cap else n


def _params(*sem):
    return pltpu.CompilerParams(dimension_semantics=sem, vmem_limit_bytes=V7X_VMEM_LIMIT)


def _dot(a, b, dims):
    return lax.dot_general(a, b, dims, preferred_element_type=F32)


def _sigmoid(x):
    return 1.0 / (1.0 + jnp.exp(-x))


def _gelu(x):
    return 0.5 * x * (1.0 + lax.erf(x * (2.0 ** -0.5)))


def _gelu_grad(x):
    return 0.5 * (1.0 + lax.erf(x * (2.0 ** -0.5))) + x * jnp.exp(-0.5 * x * x) * (1.0 / math.sqrt(2.0 * math.pi))


def _colsum(v):
    return jnp.sum(v, axis=0, keepdims=True)


def _matmul(a, b, mode, out_dtype, name, bias=None, resid=None):
    if mode == "nn":
        (m, k), (_, n) = a.shape, b.shape
    elif mode == "nt":
        (m, k), (n, _) = a.shape, b.shape
    else:
        (k, m), (_, n) = a.shape, b.shape
    tm, tn = _tile(m), _tile(n)
    dims = {"nn": _NN, "nt": _NT, "tn": _TN}[mode]
    a_spec = pl.BlockSpec((k, tm), lambda i, j: (0, i)) if mode == "tn" else pl.BlockSpec((tm, k), lambda i, j: (i, 0))
    b_spec = pl.BlockSpec((tn, k), lambda i, j: (j, 0)) if mode == "nt" else pl.BlockSpec((k, tn), lambda i, j: (0, j))
    in_specs, args = [a_spec, b_spec], [a, b]
    row_spec = pl.BlockSpec((1, tn), lambda i, j: (0, j))
    tile_spec = pl.BlockSpec((tm, tn), lambda i, j: (i, j))
    if bias is not None:
        in_specs.append(row_spec)
        args.append(bias)
    if resid is not None:
        in_specs += [tile_spec, row_spec]
        args += list(resid)
    out_shape = [jax.ShapeDtypeStruct((m, n), out_dtype)]
    out_specs = [tile_spec]
    if resid is not None:
        out_shape.append(jax.ShapeDtypeStruct((m, n), F32))
        out_specs.append(tile_spec)

    def body(*refs):
        a_ref, b_ref = refs[0], refs[1]
        pos = 2
        acc = _dot(a_ref[...], b_ref[...], dims)
        if bias is not None:
            acc = acc + refs[pos][...]
            pos += 1
        if resid is not None:
            x_ref, g_ref = refs[pos], refs[pos + 1]
            pos += 2
        refs[pos][...] = acc.astype(out_dtype)
        if resid is not None:
            refs[pos + 1][...] = x_ref[...] + g_ref[...] * acc

    outs = pl.pallas_call(
        body, name=name, grid=(m // tm, n // tn), in_specs=in_specs, out_specs=out_specs, out_shape=out_shape,
        compiler_params=_params("parallel", "parallel"),
    )(*args)
    return outs if resid is not None else outs[0]


def _modnorm(x, g, sc, sh, name):
    t, d = x.shape
    tm = _tile(t)
    row = pl.BlockSpec((1, d), lambda i: (0, 0))
    blk = pl.BlockSpec((tm, d), lambda i: (i, 0))

    def body(x_ref, g_ref, sc_ref, sh_ref, o_ref):
        x = x_ref[...]
        r = lax.rsqrt(jnp.mean(x * x, axis=-1, keepdims=True) + EPS)
        o_ref[...] = ((x * r) * g_ref[...] * (1.0 + sc_ref[...]) + sh_ref[...]).astype(BF16)

    return pl.pallas_call(
        body, name=name, grid=(t // tm,), in_specs=[blk, row, row, row], out_specs=blk,
        out_shape=jax.ShapeDtypeStruct((t, d), BF16), compiler_params=_params("parallel"),
    )(x, g, sc, sh)


def _modnorm_bwd(x, dh, g, sc, dres, name):
    t, d = x.shape
    tm = _tile(t)
    row = pl.BlockSpec((1, d), lambda i: (0, 0))
    blk = pl.BlockSpec((tm, d), lambda i: (i, 0))

    def body(x_ref, dh_ref, g_ref, sc_ref, dres_ref, dx_ref, dw_ref, dsh_ref):
        @pl.when(pl.program_id(0) == 0)
        def _():
            dw_ref[...] = jnp.zeros_like(dw_ref)
            dsh_ref[...] = jnp.zeros_like(dsh_ref)

        x = x_ref[...]
        dh = dh_ref[...].astype(F32)
        r = lax.rsqrt(jnp.mean(x * x, axis=-1, keepdims=True) + EPS)
        xn = x * r
        dxn = dh * (g_ref[...] * (1.0 + sc_ref[...]))
        dx_ref[...] = dres_ref[...] + r * (dxn - xn * jnp.mean(dxn * xn, axis=-1, keepdims=True))
        dw_ref[...] += _colsum(dh * xn)
        dsh_ref[...] += _colsum(dh)

    return pl.pallas_call(
        body, name=name, grid=(t // tm,), in_specs=[blk, blk, row, row, blk], out_specs=[blk, row, row],
        out_shape=[jax.ShapeDtypeStruct((t, d), F32), jax.ShapeDtypeStruct((1, d), F32), jax.ShapeDtypeStruct((1, d), F32)],
        compiler_params=_params("arbitrary"),
    )(x, dh, g, sc, dres)


def _gate_bwd(dxn, y, gate, name):
    t, d = dxn.shape
    tm = _tile(t)
    row = pl.BlockSpec((1, d), lambda i: (0, 0))
    blk = pl.BlockSpec((tm, d), lambda i: (i, 0))

    def body(dxn_ref, y_ref, g_ref, dy_ref, dg_ref):
        @pl.when(pl.program_id(0) == 0)
        def _():
            dg_ref[...] = jnp.zeros_like(dg_ref)

        dxn = dxn_ref[...]
        dy_ref[...] = (dxn * g_ref[...]).astype(BF16)
        dg_ref[...] += _colsum(dxn * y_ref[...])

    return pl.pallas_call(
        body, name=name, grid=(t // tm,), in_specs=[blk, blk, row], out_specs=[blk, row],
        out_shape=[jax.ShapeDtypeStruct((t, d), BF16), jax.ShapeDtypeStruct((1, d), F32)],
        compiler_params=_params("arbitrary"),
    )(dxn, y, gate)


def _loss_head(y, target, name):
    t, d = y.shape
    tm = _tile(t)
    blk = pl.BlockSpec((tm, d), lambda i: (i, 0))
    one = pl.BlockSpec((1, 1), lambda i: (0, 0))

    def body(y_ref, t_ref, dy_ref, loss_ref, acc_ref):
        @pl.when(pl.program_id(0) == 0)
        def _():
            acc_ref[...] = jnp.zeros_like(acc_ref)

        e = y_ref[...] - t_ref[...]
        dy_ref[...] = e * (1.0 / d)
        acc_ref[...] += _colsum(e * e)

        @pl.when(pl.program_id(0) == pl.num_programs(0) - 1)
        def _():
            loss_ref[...] = jnp.sum(acc_ref[...], axis=1, keepdims=True) * (0.5 / d)

    return pl.pallas_call(
        body, name=name, grid=(t // tm,), in_specs=[blk, blk], out_specs=[blk, one],
        out_shape=[jax.ShapeDtypeStruct((t, d), F32), jax.ShapeDtypeStruct((1, 1), F32)],
        scratch_shapes=[pltpu.VMEM((1, d), F32)], compiler_params=_params("arbitrary"),
    )(y, target)


def _group_norm(vg, gain):
    mu = jnp.mean(vg, axis=-1, keepdims=True)
    xc = vg - mu
    rstd = lax.rsqrt(jnp.mean(xc * xc, axis=-1, keepdims=True) + EPS)
    xhat = xc * rstd
    return xhat, rstd, xhat * gain


def _gmlp_fwd(z, gain, wtril, bias_exp, name):
    t = z.shape[0]
    zu = pl.BlockSpec((CHUNK, A_WIDTH), lambda i: (i, 0))
    zv = pl.BlockSpec((CHUNK, A_WIDTH), lambda i: (i, 1))
    full2 = lambda shp: pl.BlockSpec(shp, lambda i: (0, 0))
    w_spec = pl.BlockSpec((A_GROUPS, CHUNK, CHUNK), lambda i: (0, 0, 0))

    def body(zu_ref, zv_ref, gain_ref, w_ref, b_ref, ya_ref):
        ua = _gelu(zu_ref[...].astype(F32))
        vg = _gelu(zv_ref[...].astype(F32))
        for g in range(A_GROUPS):
            sl = slice(g * GROUP_DIM, (g + 1) * GROUP_DIM)
            _, _, vn = _group_norm(vg[:, sl], gain_ref[:, sl])
            f = _dot(w_ref[g], vn.astype(BF16), _NN) + b_ref[:, sl]
            ya_ref[:, sl] = (ua[:, sl] * f).astype(BF16)

    return pl.pallas_call(
        body, name=name, grid=(t // CHUNK,),
        in_specs=[zu, zv, full2((1, A_WIDTH)), w_spec, full2((CHUNK, A_WIDTH))], out_specs=zu,
        out_shape=jax.ShapeDtypeStruct((t, A_WIDTH), BF16), compiler_params=_params("parallel"),
    )(z, z, gain, wtril, bias_exp)


def _gmlp_bwd(z, dcat, gain, wtril, wtril_t, bias_exp, name):
    t = z.shape[0]
    zu = pl.BlockSpec((CHUNK, A_WIDTH), lambda i: (i, 0))
    zv = pl.BlockSpec((CHUNK, A_WIDTH), lambda i: (i, 1))
    full2 = lambda shp: pl.BlockSpec(shp, lambda i: (0, 0))
    w_spec = pl.BlockSpec((A_GROUPS, CHUNK, CHUNK), lambda i: (0, 0, 0))
    dz_spec = pl.BlockSpec((CHUNK, 2 * A_WIDTH), lambda i: (i, 0))

    def body(zu_ref, zv_ref, dya_ref, gain_ref, w_ref, wt_ref, b_ref, dz_ref, dw_ref, dgain_ref, dbias_ref):
        @pl.when(pl.program_id(0) == 0)
        def _():
            dw_ref[...] = jnp.zeros_like(dw_ref)
            dgain_ref[...] = jnp.zeros_like(dgain_ref)
            dbias_ref[...] = jnp.zeros_like(dbias_ref)

        zu_v = zu_ref[...].astype(F32)
        zv_v = zv_ref[...].astype(F32)
        dya = dya_ref[...].astype(F32)
        ua = _gelu(zu_v)
        vg = _gelu(zv_v)
        row = lax.broadcasted_iota(jnp.int32, (CHUNK, CHUNK), 0)
        col = lax.broadcasted_iota(jnp.int32, (CHUNK, CHUNK), 1)
        for g in range(A_GROUPS):
            sl = slice(g * GROUP_DIM, (g + 1) * GROUP_DIM)
            gain_g = gain_ref[:, sl]
            xhat, rstd, vn = _group_norm(vg[:, sl], gain_g)
            vn16 = vn.astype(BF16)
            f = _dot(w_ref[g], vn16, _NN) + b_ref[:, sl]
            df = dya[:, sl] * ua[:, sl]
            df16 = df.astype(BF16)
            dz_ref[:, sl] = (dya[:, sl] * f * _gelu_grad(zu_v[:, sl])).astype(BF16)
            dw_ref[g] += jnp.where(row >= col, _dot(df16, vn16, _NT), 0.0)
            dvn = _dot(wt_ref[g], df16, _NN)
            dgain_ref[:, sl] += _colsum(dvn * xhat)
            dxh = dvn * gain_g
            dvg = rstd * (dxh - jnp.mean(dxh, axis=-1, keepdims=True) - xhat * jnp.mean(dxh * xhat, axis=-1, keepdims=True))
            dz_ref[:, A_WIDTH + g * GROUP_DIM:A_WIDTH + (g + 1) * GROUP_DIM] = (dvg * _gelu_grad(zv_v[:, sl])).astype(BF16)
            dbias_ref[:, sl] += df

    return pl.pallas_call(
        body, name=name, grid=(t // CHUNK,),
        in_specs=[zu, zv, zu, full2((1, A_WIDTH)), w_spec, w_spec, full2((CHUNK, A_WIDTH))],
        out_specs=[dz_spec, w_spec, full2((1, A_WIDTH)), full2((CHUNK, A_WIDTH))],
        out_shape=[jax.ShapeDtypeStruct((t, 2 * A_WIDTH), BF16), jax.ShapeDtypeStruct((A_GROUPS, CHUNK, CHUNK), F32),
                   jax.ShapeDtypeStruct((1, A_WIDTH), F32), jax.ShapeDtypeStruct((CHUNK, A_WIDTH), F32)],
        compiler_params=_params("arbitrary"),
    )(z, z, dcat, gain, wtril, wtril_t, bias_exp)


def _rope_tables(pos, inv_freq, sign, name):
    t = pos.shape[0]
    tm = _tile(t)
    row = pl.BlockSpec((1, B_WIDTH), lambda i: (0, 0))
    blk = pl.BlockSpec((tm, B_WIDTH), lambda i: (i, 0))

    def body(pos_ref, f_ref, s_ref, cos_ref, sin_ref):
        ang = pos_ref[...] * f_ref[...]
        cos_ref[...] = jnp.cos(ang)
        sin_ref[...] = jnp.sin(ang) * s_ref[...]

    return pl.pallas_call(
        body, name=name, grid=(t // tm,), in_specs=[pl.BlockSpec((tm, 1), lambda i: (i, 0)), row, row],
        out_specs=[blk, blk], out_shape=[jax.ShapeDtypeStruct((t, B_WIDTH), F32)] * 2,
        compiler_params=_params("parallel"),
    )(pos, inv_freq, sign)


def _head_sum(v, seg):
    return lax.dot_general(v, seg, _NN, precision=lax.Precision.HIGHEST, preferred_element_type=F32)


def _swap_halves(v):
    lane = lax.broadcasted_iota(jnp.int32, v.shape, 1)
    return jnp.where((lane & (HEAD_DIM - 1)) < HEAD_DIM // 2,pltpu.roll(v, B_WIDTH - HEAD_DIM // 2, 1), pltpu.roll(v, HEAD_DIM // 2, 1))


def _qk_prep(z, cos_t, sin_t, gq, gk, seg, name):
    t = z.shape[0]
    tm = _tile(t, (256, 128))
    col = lambda c: pl.BlockSpec((tm, B_WIDTH), lambda i: (i, c))
    row = pl.BlockSpec((1, B_WIDTH), lambda i: (0, 0))
    blk = col(0)

    def body(q_ref, k_ref, v_ref, cos_ref, sin_ref, gq_ref, gk_ref, seg_ref, qo_ref, ko_ref, vo_ref):
        def norm_rot(x, g):
            r = lax.rsqrt(_head_sum(x * x, seg_ref[...]) * (1.0 / HEAD_DIM) + EPS)
            xn = x * r * g
            return xn * cos_ref[...] + _swap_halves(xn) * sin_ref[...]

        qo_ref[...] = norm_rot(q_ref[...].astype(F32), gq_ref[...]).astype(BF16)
        ko_ref[...] = norm_rot(k_ref[...].astype(F32), gk_ref[...]).astype(BF16)
        vo_ref[...] = v_ref[...].astype(BF16)

    return pl.pallas_call(
        body, name=name, grid=(t // tm,),
        in_specs=[col(2), col(3), col(4), blk, blk, row, row, pl.BlockSpec((B_WIDTH, B_WIDTH), lambda i: (0, 0))],
        out_specs=[blk, blk, blk], out_shape=[jax.ShapeDtypeStruct((t, B_WIDTH), BF16)] * 3,
        compiler_params=_params("parallel"),
    )(z, z, z, cos_t, sin_t, gq, gk, seg)


def _qk_prep_bwd(z, dqs, dks, dvs, cos_t, sin_t, gq, gk, seg, name):
    t = z.shape[0]
    tm = _tile(t, (256, 128))
    col = lambda c: pl.BlockSpec((tm, B_WIDTH), lambda i: (i, c))
    row = pl.BlockSpec((1, B_WIDTH), lambda i: (0, 0))
    blk = col(0)
    nb = len(dqs)

    def body(*refs):
        q_ref, k_ref = refs[0], refs[1]
        dq_refs, dk_refs, dv_refs = refs[2:2 + nb], refs[2 + nb:2 + 2 * nb], refs[2 + 2 * nb:2 + 3 * nb]
        cos_ref, sin_ref, gq_ref, gk_ref, seg_ref, dz_ref, dgq_ref, dgk_ref = refs[2 + 3 * nb:]

        @pl.when(pl.program_id(0) == 0)
        def _():
            dgq_ref[...] = jnp.zeros_like(dgq_ref)
            dgk_ref[...] = jnp.zeros_like(dgk_ref)

        def back(x, d_refs, g, dg_ref):
            dout = d_refs[0][...]
            for r_ in d_refs[1:]:
                dout = dout + r_[...]
            dy = dout * cos_ref[...] + _swap_halves(dout * sin_ref[...])
            r = lax.rsqrt(_head_sum(x * x, seg_ref[...]) * (1.0 / HEAD_DIM) + EPS)
            xn = x * r
            dg_ref[...] += _colsum(dy * xn)
            dxn = dy * g
            return r * (dxn - xn * (_head_sum(dxn * xn, seg_ref[...]) * (1.0 / HEAD_DIM)))

        dz_ref[:, 0:B_WIDTH] = back(q_ref[...].astype(F32), dq_refs, gq_ref[...], dgq_ref).astype(BF16)
        dz_ref[:, B_WIDTH:2 * B_WIDTH] = back(k_ref[...].astype(F32), dk_refs, gk_ref[...], dgk_ref).astype(BF16)
        dv = dv_refs[0][...]
        for r_ in dv_refs[1:]:
            dv = dv + r_[...]
        dz_ref[:, 2 * B_WIDTH:3 * B_WIDTH] = dv.astype(BF16)

    return pl.pallas_call(
        body, name=name, grid=(t // tm,),
        in_specs=[col(2), col(3)] + [blk] * (3 * nb) + [blk, blk, row, row, pl.BlockSpec((B_WIDTH, B_WIDTH), lambda i: (0, 0))],
        out_specs=[pl.BlockSpec((tm, 3 * B_WIDTH), lambda i: (i, 0)), row, row],
        out_shape=[jax.ShapeDtypeStruct((t, 3 * B_WIDTH), BF16), jax.ShapeDtypeStruct((1, B_WIDTH), F32),
                   jax.ShapeDtypeStruct((1, B_WIDTH), F32)],
        compiler_params=_params("arbitrary"),
    )(z, z, *dqs, *dks, *dvs, cos_t, sin_t, gq, gk, seg)


def _subseq(a, dil):
    return a.reshape(a.shape[0] // dil, dil * a.shape[1])


def _attn_fwd(q, k, v, dil, name):
    t = q.shape[0]
    nb = t // dil // Q_BLOCK
    cur = pl.BlockSpec((Q_BLOCK, B_WIDTH), lambda r, i: (i, r))
    prev = pl.BlockSpec((Q_BLOCK, B_WIDTH), lambda r, i: (jnp.maximum(i - 1, 0), r))

    def body(q_ref, kp_ref, kc_ref, vp_ref, vc_ref, o_ref, lse_ref):
        i = pl.program_id(1)
        q = q_ref[...]
        kk = jnp.concatenate([kp_ref[...], kc_ref[...]], axis=0)
        vv = jnp.concatenate([vp_ref[...], vc_ref[...]], axis=0)
        a = lax.broadcasted_iota(jnp.int32, (Q_BLOCK, 2 * Q_BLOCK), 0)
        j = lax.broadcasted_iota(jnp.int32, (Q_BLOCK, 2 * Q_BLOCK), 1)
        dist = a + Q_BLOCK - j
        mask = (dist >= 0) & (dist <= Q_BLOCK) & ((j >= Q_BLOCK) | (i > 0))
        for h in range(HEADS):
            sl = slice(h * HEAD_DIM, (h + 1) * HEAD_DIM)
            s = jnp.where(mask, _dot(q[:, sl], kk[:, sl], _NT) * (HEAD_DIM ** -0.5), NEG)
            m = jnp.max(s, axis=-1, keepdims=True)
            p = jnp.exp(s - m)
            den = jnp.sum(p, axis=-1, keepdims=True)
            o_ref[:, sl] = _dot(p.astype(BF16), vv[:, sl], _NN) / den
            lse_ref[:, sl] = jnp.broadcast_to(m + jnp.log(den), (Q_BLOCK, HEAD_DIM))

    o, lse = pl.pallas_call(
        body, name=name, grid=(dil, nb), in_specs=[cur, prev, cur, prev, cur], out_specs=[cur, cur],
        out_shape=[jax.ShapeDtypeStruct((t // dil, dil * B_WIDTH), F32)] * 2,
        compiler_params=_params("parallel", "parallel"),
    )(_subseq(q, dil), _subseq(k, dil), _subseq(k, dil), _subseq(v, dil), _subseq(v, dil))
    return o.reshape(t, B_WIDTH), lse.reshape(t, B_WIDTH)


def _attn_merge(outs, lses, name):
    t = outs[0].shape[0]
    tm = _tile(t)
    blk = pl.BlockSpec((tm, B_WIDTH), lambda i: (i, 0))
    nb = len(outs)

    def body(*refs):
        o_refs, l_refs, yb_ref, lse_ref = refs[:nb], refs[nb:2 * nb], refs[2 * nb], refs[2 * nb + 1]
        ls = [r[...] for r in l_refs]
        m = functools.reduce(jnp.maximum, ls)
        tot = m + jnp.log(sum(jnp.exp(l - m) for l in ls))
        yb_ref[...] = sum(jnp.exp(l - tot) * o[...] for l, o in zip(ls, o_refs)).astype(BF16)
        lse_ref[...] = tot

    return pl.pallas_call(
        body, name=name, grid=(t // tm,), in_specs=[blk] * (2 * nb), out_specs=[blk, blk],
        out_shape=[jax.ShapeDtypeStruct((t, B_WIDTH), BF16), jax.ShapeDtypeStruct((t, B_WIDTH), F32)],
        compiler_params=_params("parallel"),
    )(*outs, *lses)


def _attn_bwd_q(q, k, v, do, o, lse, dil, name):
    t = q.shape[0]
    nb = t // dil // Q_BLOCK
    cur = pl.BlockSpec((Q_BLOCK, B_WIDTH), lambda r, i: (i, r))
    prev = pl.BlockSpec((Q_BLOCK, B_WIDTH), lambda r, i: (jnp.maximum(i - 1, 0), r))

    def body(q_ref, kp_ref, kc_ref, vp_ref, vc_ref, do_ref, o_ref, lse_ref, dq_ref):
        i = pl.program_id(1)
        q = q_ref[...]
        kk = jnp.concatenate([kp_ref[...], kc_ref[...]], axis=0)
        vv = jnp.concatenate([vp_ref[...], vc_ref[...]], axis=0)
        do = do_ref[...]
        dof = do.astype(F32)
        of = o_ref[...].astype(F32)
        a = lax.broadcasted_iota(jnp.int32, (Q_BLOCK, 2 * Q_BLOCK), 0)
        j = lax.broadcasted_iota(jnp.int32, (Q_BLOCK, 2 * Q_BLOCK), 1)
        dist = a + Q_BLOCK - j
        mask = (dist >= 0) & (dist <= Q_BLOCK) & ((j >= Q_BLOCK) | (i > 0))
        for h in range(HEADS):
            sl = slice(h * HEAD_DIM, (h + 1) * HEAD_DIM)
            s = jnp.where(mask, _dot(q[:, sl], kk[:, sl], _NT) * (HEAD_DIM ** -0.5), NEG)
            p = jnp.exp(s - lse_ref[:, h * HEAD_DIM:h * HEAD_DIM + 1])
            dp = _dot(do[:, sl], vv[:, sl], _NT)
            delta = jnp.sum(dof[:, sl] * of[:, sl], axis=-1, keepdims=True)
            ds = p * (dp - delta) * (HEAD_DIM ** -0.5)
            dq_ref[:, sl] = _dot(ds.astype(BF16), kk[:, sl], _NN)

    dq = pl.pallas_call(
        body, name=name, grid=(dil, nb), in_specs=[cur, prev, cur, prev, cur, cur, cur, cur], out_specs=cur,
        out_shape=jax.ShapeDtypeStruct((t // dil, dil * B_WIDTH), F32),
        compiler_params=_params("parallel", "parallel"),
    )(_subseq(q, dil), _subseq(k, dil), _subseq(k, dil), _subseq(v, dil), _subseq(v, dil),
      _subseq(do, dil), _subseq(o, dil), _subseq(lse, dil))
    return dq.reshape(t, B_WIDTH)


def _attn_bwd_kv(q, k, v, do, o, lse, dil, name):
    t = q.shape[0]
    nb = t // dil // Q_BLOCK
    cur = pl.BlockSpec((Q_BLOCK, B_WIDTH), lambda r, j: (j, r))
    nxt = pl.BlockSpec((Q_BLOCK, B_WIDTH), lambda r, j: (jnp.minimum(j + 1, nb - 1), r))

    def body(k_ref, v_ref, qc_ref, qn_ref, doc_ref, don_ref, oc_ref, on_ref, lc_ref, ln_ref, dk_ref, dv_ref):
        jb = pl.program_id(1)
        kb = k_ref[...]
        vb = v_ref[...]
        qq = jnp.concatenate([qc_ref[...], qn_ref[...]], axis=0)
        dd = jnp.concatenate([doc_ref[...], don_ref[...]], axis=0)
        ddf = dd.astype(F32)
        oo = jnp.concatenate([oc_ref[...], on_ref[...]], axis=0).astype(F32)
        ll = jnp.concatenate([lc_ref[...], ln_ref[...]], axis=0)
        a = lax.broadcasted_iota(jnp.int32, (2 * Q_BLOCK, Q_BLOCK), 0)
        b = lax.broadcasted_iota(jnp.int32, (2 * Q_BLOCK, Q_BLOCK), 1)
        dist = a - b
        mask = (dist >= 0) & (dist <= Q_BLOCK) & ((a < Q_BLOCK) | (jb < nb - 1))
        for h in range(HEADS):
            sl = slice(h * HEAD_DIM, (h + 1) * HEAD_DIM)
            s = jnp.where(mask, _dot(qq[:, sl], kb[:, sl], _NT) * (HEAD_DIM ** -0.5), NEG)
            p = jnp.exp(s - ll[:, h * HEAD_DIM:h * HEAD_DIM + 1])
            dv_ref[:, sl] = _dot(p.astype(BF16), dd[:, sl], _TN)
            dp = _dot(dd[:, sl], vb[:, sl], _NT)
            delta = jnp.sum(ddf[:, sl] * oo[:, sl], axis=-1, keepdims=True)
            ds = p * (dp - delta) * (HEAD_DIM ** -0.5)
            dk_ref[:, sl] = _dot(ds.astype(BF16), qq[:, sl], _TN)

    qs, dos, os_, ls = _subseq(q, dil), _subseq(do, dil), _subseq(o, dil), _subseq(lse, dil)
    dk, dv = pl.pallas_call(
        body, name=name, grid=(dil, nb), in_specs=[cur, cur, cur, nxt, cur, nxt, cur, nxt, cur, nxt], out_specs=[cur, cur],
        out_shape=[jax.ShapeDtypeStruct((t // dil, dil * B_WIDTH), F32)] * 2,
        compiler_params=_params("parallel", "parallel"),
    )(_subseq(k, dil), _subseq(v, dil), qs, qs, dos, dos, os_, os_, ls, ls)
    return dk.reshape(t, B_WIDTH), dv.reshape(t, B_WIDTH)


FFN_TN = 256
FFN_NB = FFN_DIM // FFN_TN


def _ffn_window(main_ref, halo_ref, scr_ref, first):
    scr_ref[0:FFN_HALO, :] = jnp.where(first, 0.0, halo_ref[...].astype(F32))
    scr_ref[FFN_HALO:, :] = main_ref[...].astype(F32)


def _ffn_conv(scr_ref, w_ref, b_ref, tm):
    acc = b_ref[...] + w_ref[FFN_CONV_WIDTH - 1:FFN_CONV_WIDTH, :] * scr_ref[pl.ds(FFN_HALO, tm), :]
    for k in range(FFN_CONV_WIDTH - 1):
        acc = acc + w_ref[k:k + 1, :] * scr_ref[pl.ds(FFN_HALO - (FFN_CONV_WIDTH - 1) + k, tm), :]
    return acc


def _ffn_act(u, dw_w, dw_b, name):
    t = u.shape[0]
    tm = _tile(t)
    hb = tm // FFN_HALO
    main = lambda off: pl.BlockSpec((tm, FFN_TN), lambda i, j: (i, j + off))
    halo = lambda off: pl.BlockSpec((FFN_HALO, FFN_TN), lambda i, j: (jnp.maximum(i * hb - 1, 0), j + off))
    wsp = lambda off: pl.BlockSpec((FFN_CONV_WIDTH, FFN_TN), lambda i, j: (0, j + off))
    bsp = lambda off: pl.BlockSpec((1, FFN_TN), lambda i, j: (0, j + off))

    def body(ua_ref, uah_ref, ub_ref, ubh_ref, wa_ref, wb_ref, ba_ref, bb_ref, o_ref, sa_ref, sb_ref):
        first = pl.program_id(0) == 0
        _ffn_window(ua_ref, uah_ref, sa_ref, first)
        _ffn_window(ub_ref, ubh_ref, sb_ref, first)
        za = _ffn_conv(sa_ref, wa_ref, ba_ref, tm)
        zb = _ffn_conv(sb_ref, wb_ref, bb_ref, tm)
        o_ref[...] = (za * _sigmoid(za) * zb).astype(BF16)

    return pl.pallas_call(
        body, name=name, grid=(t // tm, FFN_NB),
        in_specs=[main(0), halo(0), main(FFN_NB), halo(FFN_NB), wsp(0), wsp(FFN_NB), bsp(0), bsp(FFN_NB)],
        out_specs=pl.BlockSpec((tm, FFN_TN), lambda i, j: (i, j)),
        out_shape=jax.ShapeDtypeStruct((t, FFN_DIM), BF16),
        scratch_shapes=[pltpu.VMEM((tm + FFN_HALO, FFN_TN), F32)] * 2,
        compiler_params=_params("parallel", "parallel"),
    )(u, u, u, u, dw_w, dw_w, dw_b, dw_b)


def _ffn_act_bwd(u, dact, dw_w, dw_b, name):
    t = u.shape[0]
    tm = _tile(t)
    hb = tm // FFN_HALO
    nb2 = 2 * FFN_NB
    own = lambda jj: jj
    par = lambda jj: (jj + FFN_NB) % nb2
    main = lambda f: pl.BlockSpec((tm, FFN_TN), lambda jj, i: (i, f(jj)))
    halo = lambda f: pl.BlockSpec((FFN_HALO, FFN_TN), lambda jj, i: (jnp.maximum(i * hb - 1, 0), f(jj)))
    wsp = lambda f: pl.BlockSpec((FFN_CONV_WIDTH, FFN_TN), lambda jj, i: (0, f(jj)))
    bsp = lambda f: pl.BlockSpec((1, FFN_TN), lambda jj, i: (0, f(jj)))

    def body(uo_ref, uoh_ref, up_ref, uph_ref, da_ref, wo_ref, wp_ref, bo_ref, bp_ref, dz_ref, dw_ref, db_ref, so_ref, sp_ref):
        jj, i = pl.program_id(0), pl.program_id(1)

        @pl.when(i == 0)
        def _():
            dw_ref[...] = jnp.zeros_like(dw_ref)
            db_ref[...] = jnp.zeros_like(db_ref)

        _ffn_window(uo_ref, uoh_ref, so_ref, i == 0)
        _ffn_window(up_ref, uph_ref, sp_ref, i == 0)
        z_own = _ffn_conv(so_ref, wo_ref, bo_ref, tm)
        z_par = _ffn_conv(sp_ref, wp_ref, bp_ref, tm)
        dact = da_ref[...].astype(F32)
        is_a = jj < FFN_NB
        za = jnp.where(is_a, z_own, z_par)
        zb = jnp.where(is_a, z_par, z_own)
        sg = _sigmoid(za)
        d_a = dact * zb * (sg * (1.0 + za * (1.0 - sg)))
        d_b = dact * (za * sg)
        dz = jnp.where(is_a, d_a, d_b).astype(BF16)
        dz_ref[...] = dz
        dzf = dz.astype(F32)
        db_ref[...] += _colsum(dzf)
        for k in range(FFN_CONV_WIDTH):
            dw_ref[k:k + 1, :] += _colsum(dzf * so_ref[pl.ds(FFN_HALO - (FFN_CONV_WIDTH - 1) + k, tm), :])

    return pl.pallas_call(
        body, name=name, grid=(nb2, t // tm),
        in_specs=[main(own), halo(own), main(par), halo(par), pl.BlockSpec((tm, FFN_TN), lambda jj, i: (i, jj % FFN_NB)),
                  wsp(own), wsp(par), bsp(own), bsp(par)],
        out_specs=[main(own), wsp(own), bsp(own)],
        out_shape=[jax.ShapeDtypeStruct((t, 2 * FFN_DIM), BF16), jax.ShapeDtypeStruct((FFN_CONV_WIDTH, 2 * FFN_DIM), F32),
                   jax.ShapeDtypeStruct((1, 2 * FFN_DIM), F32)],
        scratch_shapes=[pltpu.VMEM((tm + FFN_HALO, FFN_TN), F32)] * 2,
        compiler_params=_params("parallel", "arbitrary"),
    )(u, u, u, u, dact, dw_w, dw_w, dw_b, dw_b)


def _conv_transpose(dz, w, width, halo_rows, tn, name):
    t, n = dz.shape
    tm = _tile(t, (256, 128))
    hb = tm // halo_rows
    last_halo = t // halo_rows - 1
    nt = t // tm
    main = pl.BlockSpec((tm, tn), lambda i, j: (i, j))
    nxt = pl.BlockSpec((halo_rows, tn), lambda i, j: (jnp.minimum((i + 1) * hb, last_halo), j))

    def body(dz_ref, dzn_ref, w_ref, du_ref, scr_ref):
        scr_ref[0:tm, :] = dz_ref[...].astype(F32)
        scr_ref[tm:, :] = jnp.where(pl.program_id(0) == nt - 1, 0.0, dzn_ref[...].astype(F32))
        acc = w_ref[width - 1:width, :] * scr_ref[pl.ds(0, tm), :]
        for k in range(width - 1):
            acc = acc + w_ref[k:k + 1, :] * scr_ref[pl.ds(width - 1 - k, tm), :]
        du_ref[...] = acc.astype(du_ref.dtype)

    return pl.pallas_call(
        body, name=name, grid=(nt, n // tn),
        in_specs=[main, nxt, pl.BlockSpec((width, tn), lambda i, j: (0, j))], out_specs=main,
        out_shape=jax.ShapeDtypeStruct((t, n), BF16 if dz.dtype == BF16 else F32),
        scratch_shapes=[pltpu.VMEM((tm + halo_rows, tn), F32)],
        compiler_params=_params("parallel", "parallel"),
    )(dz, dz, w)


CONV_TM = 256


def _glu_window(pa_ref, pah_ref, pg_ref, pgh_ref, scr_ref, first):
    ah, gh = pah_ref[...].astype(F32), pgh_ref[...].astype(F32)
    scr_ref[0:CONV_HALO, :] = jnp.where(first, 0.0, ah * _sigmoid(gh))
    scr_ref[CONV_HALO:, :] = pa_ref[...].astype(F32) * _sigmoid(pg_ref[...].astype(F32))


def _dw_conv31(scr_ref, w_ref, b_ref, tm):
    acc = b_ref[...] + w_ref[CONV_WIDTH - 1:CONV_WIDTH, :] * scr_ref[pl.ds(CONV_HALO, tm), :]
    for k in range(CONV_WIDTH - 1):
        acc = acc + w_ref[k:k + 1, :] * scr_ref[pl.ds(CONV_HALO - (CONV_WIDTH - 1) + k, tm), :]
    return acc


def _conformer_specs(t):
    tm = _tile(t, (CONV_TM, 128))
    hb = tm // CONV_HALO
    d = D_MODEL
    main = lambda c: pl.BlockSpec((tm, d), lambda i: (i, c))
    halo = lambda c: pl.BlockSpec((CONV_HALO, d), lambda i: (jnp.maximum(i * hb - 1, 0), c))
    row = pl.BlockSpec((1, d), lambda i: (0, 0))
    wsp = pl.BlockSpec((CONV_WIDTH, d), lambda i: (0, 0))
    return tm, main, halo, row, wsp


def _conformer_mid(p, dw_w, dw_b, ln_g, ln_b, name):
    t = p.shape[0]
    tm, main, halo, row, wsp = _conformer_specs(t)

    def body(pa_ref, pah_ref, pg_ref, pgh_ref, w_ref, b_ref, g_ref, lb_ref, o_ref, scr_ref):
        _glu_window(pa_ref, pah_ref, pg_ref, pgh_ref, scr_ref, pl.program_id(0) == 0)
        dc = _dw_conv31(scr_ref, w_ref, b_ref, tm)
        mu = jnp.mean(dc, axis=-1, keepdims=True)
        xc = dc - mu
        ln = xc * lax.rsqrt(jnp.mean(xc * xc, axis=-1, keepdims=True) + EPS) * g_ref[...] + lb_ref[...]
        o_ref[...] = (ln * _sigmoid(ln)).astype(BF16)

    return pl.pallas_call(
        body, name=name, grid=(t // tm,), in_specs=[main(0), halo(0), main(1), halo(1), wsp, row, row, row],
        out_specs=main(0), out_shape=jax.ShapeDtypeStruct((t, D_MODEL), BF16),
        scratch_shapes=[pltpu.VMEM((tm + CONV_HALO, D_MODEL), F32)], compiler_params=_params("parallel"),
    )(p, p, p, p, dw_w, dw_b, ln_g, ln_b)


def _conformer_mid_bwd(p, ds, dw_w, dw_b, ln_g, ln_b, name):
    t = p.shape[0]
    tm, main, halo, row, wsp = _conformer_specs(t)

    def body(pa_ref, pah_ref, pg_ref, pgh_ref, ds_ref, w_ref, b_ref, g_ref, lb_ref,
             ddc_ref, dw_ref, db_ref, dg_ref, dlb_ref, scr_ref):
        @pl.when(pl.program_id(0) == 0)
        def _():
            for r_ in (dw_ref, db_ref, dg_ref, dlb_ref):
                r_[...] = jnp.zeros_like(r_)

        _glu_window(pa_ref, pah_ref, pg_ref, pgh_ref, scr_ref, pl.program_id(0) == 0)
        dc = _dw_conv31(scr_ref, w_ref, b_ref, tm)
        mu = jnp.mean(dc, axis=-1, keepdims=True)
        xc = dc - mu
        rstd = lax.rsqrt(jnp.mean(xc * xc, axis=-1, keepdims=True) + EPS)
        xhat = xc * rstd
        ln = xhat * g_ref[...] + lb_ref[...]
        sg = _sigmoid(ln)
        dln = ds_ref[...].astype(F32) * (sg * (1.0 + ln * (1.0 - sg)))
        dg_ref[...] += _colsum(dln * xhat)
        dlb_ref[...] += _colsum(dln)
        dxh = dln * g_ref[...]
        ddc = rstd * (dxh - jnp.mean(dxh, axis=-1, keepdims=True) - xhat * jnp.mean(dxh * xhat, axis=-1, keepdims=True))
        ddc_ref[...] = ddc
        db_ref[...] += _colsum(ddc)
        for k in range(CONV_WIDTH):
            dw_ref[k:k + 1, :] += _colsum(ddc * scr_ref[pl.ds(CONV_HALO - (CONV_WIDTH - 1) + k, tm), :])

    return pl.pallas_call(
        body, name=name, grid=(t // tm,), in_specs=[main(0), halo(0), main(1), halo(1), main(0), wsp, row, row, row],
        out_specs=[main(0), wsp, row, row, row],
        out_shape=[jax.ShapeDtypeStruct((t, D_MODEL), F32), jax.ShapeDtypeStruct((CONV_WIDTH, D_MODEL), F32)]
        + [jax.ShapeDtypeStruct((1, D_MODEL), F32)] * 3,
        scratch_shapes=[pltpu.VMEM((tm + CONV_HALO, D_MODEL), F32)], compiler_params=_params("arbitrary"),
    )(p, p, p, p, ds, dw_w, dw_b, ln_g, ln_b)


def _glu_bwd(p, dglu, name):
    t = p.shape[0]
    d = D_MODEL
    tm = _tile(t)
    col = lambda c: pl.BlockSpec((tm, d), lambda i: (i, c))
    wide = pl.BlockSpec((tm, 2 * d), lambda i: (i, 0))
    row = pl.BlockSpec((1, 2 * d), lambda i: (0, 0))

    def body(pa_ref, pg_ref, dglu_ref, dp_ref, db_ref):
        @pl.when(pl.program_id(0) == 0)
        def _():
            db_ref[...] = jnp.zeros_like(db_ref)

        a, g, dglu = pa_ref[...].astype(F32), pg_ref[...].astype(F32), dglu_ref[...].astype(F32)
        sg = _sigmoid(g)
        da = (dglu * sg).astype(BF16)
        dg = (dglu * a * sg * (1.0 - sg)).astype(BF16)
        dp_ref[:, 0:d] = da
        dp_ref[:, d:2 * d] = dg
        db_ref[:, 0:d] += _colsum(da.astype(F32))
        db_ref[:, d:2 * d] += _colsum(dg.astype(F32))

    return pl.pallas_call(
        body, name=name, grid=(t // tm,), in_specs=[col(0), col(1), col(0)], out_specs=[wide, row],
        out_shape=[jax.ShapeDtypeStruct((t, 2 * d), BF16), jax.ShapeDtypeStruct((1, 2 * d), F32)],
        compiler_params=_params("arbitrary"),
    )(p, p, dglu)


def _colsum_call(a, name):
    t, n = a.shape
    tm = _tile(t)

    def body(a_ref, o_ref):
        @pl.when(pl.program_id(0) == 0)
        def _():
            o_ref[...] = jnp.zeros_like(o_ref)

        o_ref[...] += _colsum(a_ref[...].astype(F32))

    return pl.pallas_call(
        body, name=name, grid=(t // tm,), in_specs=[pl.BlockSpec((tm, n), lambda i: (i, 0))],
        out_specs=pl.BlockSpec((1, n), lambda i: (0, 0)), out_shape=jax.ShapeDtypeStruct((1, n), F32),
        compiler_params=_params("arbitrary"),
    )(a)


def _ada_fwd(c_all, w, name):
    rows, d = c_all.shape
    n = w.shape[1]
    tn = _tile(n, (256, 128))

    def body(c_ref, w_ref, o_ref):
        c = c_ref[...]
        o_ref[...] = _dot((c * _sigmoid(c)).astype(BF16), w_ref[...].astype(BF16), _NN)

    return pl.pallas_call(
        body, name=name, grid=(n // tn,),
        in_specs=[pl.BlockSpec((rows, d), lambda j: (0, 0)), pl.BlockSpec((d, tn), lambda j: (0, j))],
        out_specs=pl.BlockSpec((rows, tn), lambda j: (0, j)), out_shape=jax.ShapeDtypeStruct((rows, n), F32),
        compiler_params=_params("parallel"),
    )(c_all, w)


def _ada_bwd(c_all, dmod, name):
    rows, d = c_all.shape
    n = dmod.shape[1]
    tn = _tile(n, (256, 128))

    def body(c_ref, g_ref, o_ref):
        c = c_ref[...]
        o_ref[...] = _dot((c * _sigmoid(c)).astype(BF16), g_ref[...].astype(BF16), _TN)

    return pl.pallas_call(
        body, name=name, grid=(n // tn,),
        in_specs=[pl.BlockSpec((rows, d), lambda j: (0, 0)), pl.BlockSpec((rows, tn), lambda j: (0, j))],
        out_specs=pl.BlockSpec((d, tn), lambda j: (0, j)), out_shape=jax.ShapeDtypeStruct((d, n), F32),
        compiler_params=_params("parallel"),
    )(c_all, dmod)


def _sum_slots(a, name):
    s, r, c = a.shape
    tr = _row_tile(r, 256)

    def body(a_ref, o_ref):
        acc = a_ref[0].astype(F32)
        for k in range(1, s):
            acc = acc + a_ref[k].astype(F32)
        o_ref[...] = acc

    return pl.pallas_call(
        body, name=name, grid=(r // tr,), in_specs=[pl.BlockSpec((s, tr, c), lambda i: (0, i, 0))],
        out_specs=pl.BlockSpec((tr, c), lambda i: (i, 0)), out_shape=jax.ShapeDtypeStruct((r, c), F32),
        compiler_params=_params("parallel"),
    )(a)


def _add_bf16(a, b, name):
    r, c = a.shape
    tr = _row_tile(r, 512)
    blk = pl.BlockSpec((tr, c), lambda i: (i, 0))

    def body(a_ref, b_ref, o_ref):
        o_ref[...] = (a_ref[...].astype(F32) + b_ref[...].astype(F32)).astype(BF16)

    return pl.pallas_call(
        body, name=name, grid=(r // tr,), in_specs=[blk, blk], out_specs=blk,
        out_shape=jax.ShapeDtypeStruct((r, c), BF16), compiler_params=_params("parallel"),
    )(a, b)


def _adamw(w, g, m, v, name):
    r, c = w.shape
    tr = _row_tile(r, 256)
    blk = pl.BlockSpec((tr, c), lambda i: (i, 0))
    c1 = 1.0 / (1.0 - ADAM_B1 ** ADAM_STEP)
    c2 = 1.0 / (1.0 - ADAM_B2 ** ADAM_STEP)

    def body(w_ref, g_ref, m_ref, v_ref, d_ref, nm_ref, nv_ref):
        g_ = g_ref[...]
        nm = ADAM_B1 * m_ref[...] + (1.0 - ADAM_B1) * g_
        nv = ADAM_B2 * v_ref[...] + (1.0 - ADAM_B2) * (g_ * g_)
        d_ref[...] = -ADAM_LR * ((nm * c1) / (jnp.sqrt(nv * c2) + ADAM_EPS) + ADAM_WD * w_ref[...])
        nm_ref[...] = nm
        nv_ref[...] = nv

    return pl.pallas_call(
        body, name=name, grid=(r // tr,), in_specs=[blk] * 4, out_specs=[blk] * 3,
        out_shape=[jax.ShapeDtypeStruct((r, c), F32)] * 3, compiler_params=_params("parallel"),
    )(w, g, m, v)


def _mesh_pos():
    return lax.axis_index("x"), lax.axis_index("y"), lax.axis_index("c")


def _all_gather_vmem(x_shard, name):
    m_per, n = x_shard.shape

    def body(x_ref, out_ref, send_sems, recv_sems, local_sem):
        x, y, c = _mesh_pos()
        me, sibling = (x, y, c), (x, y, 1 - c)
        chips = [(1 - x, y), (x, 1 - y), (1 - x, 1 - y)]

        def rows(px, py, pc):
            return out_ref.at[pl.ds((4 * px + 2 * py + pc) * m_per, m_per), :]

        def copy(k, block, to, src=None):
            return pltpu.make_async_remote_copy(
                src_ref=rows(*block) if src is None else src, dst_ref=rows(*block),
                send_sem=send_sems.at[k], recv_sem=recv_sems.at[k], device_id=to, device_id_type=MESH)

        mine = pltpu.make_async_copy(x_ref, rows(*me), local_sem)
        mine.start()
        first = [copy(0, me, sibling, src=x_ref)]
        first += [copy(1 + j, me, (*chip, c), src=x_ref) for j, chip in enumerate(chips)]
        for cp in first:
            cp.start()
        passed = [copy(4 + j, (*chip, c), sibling) for j, chip in enumerate(chips)]
        for j, chip in enumerate(chips):
            copy(1 + j, (*chip, c), me).wait_recv()
            passed[j].start()
        copy(0, sibling, me).wait_recv()
        for j, chip in enumerate(chips):
            copy(4 + j, (*chip, 1 - c), me).wait_recv()
        for cp in first + passed:
            cp.wait_send()
        mine.wait()

    return pl.pallas_call(
        body, name=name, out_shape=jax.ShapeDtypeStruct((N_DEV * m_per, n), x_shard.dtype),
        in_specs=[pl.BlockSpec(memory_space=pltpu.VMEM)], out_specs=pl.BlockSpec(memory_space=pltpu.VMEM),
        scratch_shapes=[pltpu.SemaphoreType.DMA((7,)), pltpu.SemaphoreType.DMA((7,)), pltpu.SemaphoreType.DMA],
    )(x_shard)


def _block_of(ref, idx, stacked):
    return ref.at[:, idx] if stacked else ref.at[idx]


def _all_gather_hbm(shards, name):
    n = len(shards)
    stacked = [s.ndim == 3 for s in shards]
    out_shape = [jax.ShapeDtypeStruct((s.shape[0], N_DEV) + s.shape[1:] if st else (N_DEV,) + s.shape, s.dtype)
                 for s, st in zip(shards, stacked)]

    def body(*refs):
        x_refs, out_refs = refs[:n], refs[n:2 * n]
        send_sems, recv_sems, local_sems = refs[2 * n:]
        x, y, c = _mesh_pos()
        me, sibling = (x, y, c), (x, y, 1 - c)
        chips = [(1 - x, y), (x, 1 - y), (1 - x, 1 - y)]

        def blk(a, p):
            return _block_of(out_refs[a], 4 * p[0] + 2 * p[1] + p[2], stacked[a])

        def copy(a, k, block, to, src=None):
            return pltpu.make_async_remote_copy(
                src_ref=blk(a, block) if src is None else src, dst_ref=blk(a, block),
                send_sem=send_sems.at[7 * a + k], recv_sem=recv_sems.at[7 * a + k], device_id=to, device_id_type=MESH)

        mine = [pltpu.make_async_copy(x_refs[a], blk(a, me), local_sems.at[a]) for a in range(n)]
        for cp in mine:
            cp.start()
        first = []
        for a in range(n):
            first.append(copy(a, 0, me, sibling, src=x_refs[a]))
            first += [copy(a, 1 + j, me, (*chip, c), src=x_refs[a]) for j, chip in enumerate(chips)]
        for cp in first:
            cp.start()
        passed = []
        for j, chip in enumerate(chips):
            for a in range(n):
                copy(a, 1 + j, (*chip, c), me).wait_recv()
                fwd = copy(a, 4 + j, (*chip, c), sibling)
                fwd.start()
                passed.append(fwd)
        for a in range(n):
            copy(a, 0, sibling, me).wait_recv()
            for j, chip in enumerate(chips):
                copy(a, 4 + j, (*chip, 1 - c), me).wait_recv()
        for cp in first + passed:
            cp.wait_send()
        for cp in mine:
            cp.wait()

    any_spec = pl.BlockSpec(memory_space=pl.ANY)
    return pl.pallas_call(
        body, name=name, out_shape=out_shape, in_specs=[any_spec] * n, out_specs=[any_spec] * n,
        scratch_shapes=[pltpu.SemaphoreType.DMA((7 * n,)), pltpu.SemaphoreType.DMA((7 * n,)), pltpu.SemaphoreType.DMA((n,))],
    )(*shards)


def _exchange_halves(grads, name):
    n = len(grads)
    out_shape = [jax.ShapeDtypeStruct((4,) + g.shape[2:], g.dtype) for g in grads] * 2

    def body(*refs):
        g_refs, keep_refs, recv_refs = refs[:n], refs[n:2 * n], refs[2 * n:3 * n]
        send_sems, recv_sems, local_sems = refs[3 * n:]
        x, y, c = _mesh_pos()
        sibling = (x, y, 1 - c)

        def half(a, cc):
            return g_refs[a].at[:, cc]

        local = [pltpu.make_async_copy(half(a, c), keep_refs[a], local_sems.at[a]) for a in range(n)]
        remote = [pltpu.make_async_remote_copy(
            src_ref=half(a, 1 - c), dst_ref=recv_refs[a], send_sem=send_sems.at[a], recv_sem=recv_sems.at[a],
            device_id=sibling, device_id_type=MESH) for a in range(n)]
        for cp in local + remote:
            cp.start()
        for cp in remote:
            cp.wait()
        for cp in local:
            cp.wait()

    any_spec = pl.BlockSpec(memory_space=pl.ANY)
    outs = pl.pallas_call(
        body, name=name, out_shape=out_shape, in_specs=[any_spec] * n, out_specs=[any_spec] * (2 * n),
        scratch_shapes=[pltpu.SemaphoreType.DMA((n,)), pltpu.SemaphoreType.DMA((n,)), pltpu.SemaphoreType.DMA((n,))],
    )(*grads)
    return outs[:n], outs[n:]


def _exchange_chips(parts, name):
    n = len(parts)
    stacked = [p.ndim == 4 for p in parts]
    out_shape = [jax.ShapeDtypeStruct(p.shape, p.dtype) for p in parts]

    def body(*refs):
        p_refs, o_refs = refs[:n], refs[n:2 * n]
        send_sems, recv_sems, local_sems = refs[2 * n:]
        x, y, c = _mesh_pos()
        my_q = 2 * x + y
        chips = [(1 - x, y), (x, 1 - y), (1 - x, 1 - y)]

        local = [pltpu.make_async_copy(_block_of(p_refs[a], my_q, stacked[a]), _block_of(o_refs[a], my_q, stacked[a]),
                                       local_sems.at[a]) for a in range(n)]
        remote = []
        for a in range(n):
            for j, chip in enumerate(chips):
                q = 2 * chip[0] + chip[1]
                remote.append(pltpu.make_async_remote_copy(
                    src_ref=_block_of(p_refs[a], q, stacked[a]), dst_ref=_block_of(o_refs[a], my_q, stacked[a]),
                    send_sem=send_sems.at[3 * a + j], recv_sem=recv_sems.at[3 * a + j],
                    device_id=(*chip, c), device_id_type=MESH))
        for cp in local + remote:
            cp.start()
        for cp in remote:
            cp.wait()
        for cp in local:
            cp.wait()

    any_spec = pl.BlockSpec(memory_space=pl.ANY)
    return pl.pallas_call(
        body, name=name, out_shape=out_shape, in_specs=[any_spec] * n, out_specs=[any_spec] * n,
        scratch_shapes=[pltpu.SemaphoreType.DMA((3 * n,)), pltpu.SemaphoreType.DMA((3 * n,)), pltpu.SemaphoreType.DMA((n,))],
    )(*parts)


def _ffn_forward(x, mod, norm_g, w, tag):
    sh, sc, gate = mod
    h = _modnorm(x, norm_g, sc, sh, f"{tag}_norm")
    u = _matmul(h, w["up_t"], "nt", BF16, f"{tag}_up")
    act = _ffn_act(u, w["dw_w"], w["dw_b"], f"{tag}_act")
    y, x_new = _matmul(act, w["down"], "nn", F32, f"{tag}_down", resid=(x, gate))
    return x_new, (x, h, u, act, y)


def _ffn_backward(dx_new, saved, mod, norm_g, w, tag):
    x, h, u, act, y = saved
    _, sc, gate = mod
    dy, d_gate = _gate_bwd(dx_new, y, gate, f"{tag}_gate_bwd")
    d_down = _matmul(act, dy, "tn", BF16, f"{tag}_down_dw")
    dact = _matmul(dy, w["down"], "nt", BF16, f"{tag}_down_dx")
    dz, d_dw_w, d_dw_b = _ffn_act_bwd(u, dact, w["dw_w"], w["dw_b"], f"{tag}_act_bwd")
    du = _conv_transpose(dz, w["dw_w"], FFN_CONV_WIDTH, FFN_HALO, 512, f"{tag}_conv_bwd")
    d_up_t = _matmul(du, h, "tn", BF16, f"{tag}_up_dw")
    dh = _matmul(du, w["up_t"], "nn", F32, f"{tag}_up_dx")
    dx, d_w, d_sh = _modnorm_bwd(x, dh, norm_g, sc, dx_new, f"{tag}_norm_bwd")
    return dx, dict(up_t=d_up_t, down=d_down, dw_w=d_dw_w, dw_b=d_dw_b, norm_g=d_w * (1.0 + sc),
                    sh=d_sh, sc=d_w * norm_g, gate=d_gate)


def _mixer_forward(x, mod, norm_g, w, rope, tag):
    sh, sc, gate = mod
    h = _modnorm(x, norm_g, sc, sh, f"{tag}_norm")
    z = _matmul(h, w["w_in_t"], "nt", BF16, f"{tag}_in")
    ya = _gmlp_fwd(z, w["gain"], w["wtril"], w["bias_exp"], f"{tag}_gmlp")
    q, k, v = _qk_prep(z, rope[0], rope[1], w["gq"], w["gk"], w["seg"], f"{tag}_qk")
    outs, lses = [], []
    for _, dil in PATTERNS:
        o, l = _attn_fwd(q, k, v, dil, f"{tag}_attn_d{dil}")
        outs.append(o)
        lses.append(l)
    yb, lse = _attn_merge(outs, lses, f"{tag}_merge")
    cat = jnp.concatenate([ya, yb], axis=1)
    y, x_new = _matmul(cat, w["w_out"], "nn", F32, f"{tag}_out", resid=(x, gate))
    return x_new, (x, h, z, q, k, v, yb, lse, cat, y)


def _mixer_backward(dx_new, saved, mod, norm_g, w, rope, tag):
    x, h, z, q, k, v, yb, lse, cat, y = saved
    _, sc, gate = mod
    dy, d_gate = _gate_bwd(dx_new, y, gate, f"{tag}_gate_bwd")
    d_w_out = _matmul(cat, dy, "tn", BF16, f"{tag}_out_dw")
    dcat = _matmul(dy, w["w_out"], "nt", BF16, f"{tag}_out_dx")
    dz_a, d_sp_w, d_gain, d_bias_exp = _gmlp_bwd(z, dcat, w["gain"], w["wtril"], w["wtril_t"], w["bias_exp"], f"{tag}_gmlp_bwd")
    dyb = dcat[:, A_WIDTH:]
    dqs, dks, dvs = [], [], []
    for _, dil in PATTERNS:
        dqs.append(_attn_bwd_q(q, k, v, dyb, yb, lse, dil, f"{tag}_attn_dq_d{dil}"))
        dk, dv = _attn_bwd_kv(q, k, v, dyb, yb, lse, dil, f"{tag}_attn_dkv_d{dil}")
        dks.append(dk)
        dvs.append(dv)
    dz_qkv, d_gq, d_gk = _qk_prep_bwd(z, dqs, dks, dvs, rope[0], rope[1], w["gq"], w["gk"], w["seg"], f"{tag}_qk_bwd")
    dz = jnp.concatenate([dz_a, dz_qkv], axis=1)
    d_w_in_t = _matmul(dz, h, "tn", BF16, f"{tag}_in_dw")
    dh = _matmul(dz, w["w_in_t"], "nn", F32, f"{tag}_in_dx")
    dx, d_w, d_sh = _modnorm_bwd(x, dh, norm_g, sc, dx_new, f"{tag}_norm_bwd")
    return dx, dict(
        w_in_t=d_w_in_t, w_out=d_w_out, vnorm_g=d_gain.reshape(A_GROUPS, GROUP_DIM), spatial_w=d_sp_w,
        spatial_b=d_bias_exp.reshape(CHUNK, A_GROUPS, GROUP_DIM).sum(-1).T,
        q_norm_g=d_gq.reshape(HEADS, HEAD_DIM).sum(0), k_norm_g=d_gk.reshape(HEADS, HEAD_DIM).sum(0),
        norm_g=d_w * (1.0 + sc), sh=d_sh, sc=d_w * norm_g, gate=d_gate)


def _conformer_forward(x, mod, norm_g, w, tag):
    sh, sc, gate = mod
    h = _modnorm(x, norm_g, sc, sh, f"{tag}_norm")
    p = _matmul(h, w["pw1_t"], "nt", BF16, f"{tag}_pw1", bias=w["pw1_b"])
    s = _conformer_mid(p, w["dw_w"], w["dw_b"], w["ln_g"], w["ln_b"], f"{tag}_mid")
    y, x_new = _matmul(s, w["pw2"], "nn", F32, f"{tag}_pw2", bias=w["pw2_b"], resid=(x, gate))
    return x_new, (x, h, p, s, y)


def _conformer_backward(dx_new, saved, mod, norm_g, w, tag):
    x, h, p, s, y = saved
    _, sc, gate = mod
    dy, d_gate = _gate_bwd(dx_new, y, gate, f"{tag}_gate_bwd")
    d_pw2 = _matmul(s, dy, "tn", BF16, f"{tag}_pw2_dw")
    d_pw2_b = _colsum_call(dy, f"{tag}_pw2_db")
    ds = _matmul(dy, w["pw2"], "nt", BF16, f"{tag}_pw2_dx")
    ddc, d_dw_w, d_dw_b, d_ln_g, d_ln_b = _conformer_mid_bwd(p, ds, w["dw_w"], w["dw_b"], w["ln_g"], w["ln_b"], f"{tag}_mid_bwd")
    dglu = _conv_transpose(ddc, w["dw_w"], CONV_WIDTH, CONV_HALO, 512, f"{tag}_conv_bwd")
    dp, d_pw1_b = _glu_bwd(p, dglu, f"{tag}_glu_bwd")
    d_pw1_t = _matmul(dp, h, "tn", BF16, f"{tag}_pw1_dw")
    dh = _matmul(dp, w["pw1_t"], "nn", F32, f"{tag}_pw1_dx")
    dx, d_w, d_sh = _modnorm_bwd(x, dh, norm_g, sc, dx_new, f"{tag}_norm_bwd")
    return dx, dict(pw1_t=d_pw1_t, pw1_b=d_pw1_b, dw_w=d_dw_w, dw_b=d_dw_b, ln_g=d_ln_g, ln_b=d_ln_b, pw2=d_pw2,
                    pw2_b=d_pw2_b, norm_g=d_w * (1.0 + sc), sh=d_sh, sc=d_w * norm_g, gate=d_gate)


def _local_step(x, target, pos, mod, norm_mix_g, norm_ffn_g, mixer_w, conv_w, ffn_w):
    d = D_MODEL
    inv_freq = 1.0 / (ROPE_THETA ** (jnp.arange(0, HEAD_DIM, 2, dtype=F32) / HEAD_DIM))
    inv_freq = jnp.tile(inv_freq, 2 * HEADS)[None, :]
    sign = jnp.tile(jnp.concatenate([-jnp.ones(HEAD_DIM // 2, F32), jnp.ones(HEAD_DIM // 2, F32)]), HEADS)[None, :]
    rope = _rope_tables(pos, inv_freq, sign, "rope_tables")
    mods = [[mod[l:l + 1, i * d:(i + 1) * d] for i in range(6)] for l in range(2)]
    mix = [(m[0], m[1], m[2]) for m in mods]
    ffn = [(m[3], m[4], m[5]) for m in mods]
    gm = [norm_mix_g[l:l + 1] for l in range(2)]
    gf = [norm_ffn_g[l:l + 1] for l in range(2)]

    x1, s_mix = _mixer_forward(x, mix[0], gm[0], mixer_w, rope, "l0_mix")
    x2, s_ffn0 = _ffn_forward(x1, ffn[0], gf[0], ffn_w[0], "l0_ffn")
    x3, s_conv = _conformer_forward(x2, mix[1], gm[1], conv_w, "l1_conv")
    x4, s_ffn1 = _ffn_forward(x3, ffn[1], gf[1], ffn_w[1], "l1_ffn")
    dx, loss = _loss_head(x4, target, "loss_head")
    dx, g_ffn1 = _ffn_backward(dx, s_ffn1, ffn[1], gf[1], ffn_w[1], "l1_ffn")
    dx, g_conv = _conformer_backward(dx, s_conv, mix[1], gm[1], conv_w, "l1_conv")
    dx, g_ffn0 = _ffn_backward(dx, s_ffn0, ffn[0], gf[0], ffn_w[0], "l0_ffn")
    dx, g_mix = _mixer_backward(dx, s_mix, mix[0], gm[0], mixer_w, rope, "l0_mix")
    blocks = [g_mix, g_ffn0, g_conv, g_ffn1]
    dmod = jnp.stack([jnp.concatenate([a["sh"], a["sc"], a["gate"], b["sh"], b["sc"], b["gate"]], axis=1)[0]
                      for a, b in ((g_mix, g_ffn0), (g_conv, g_ffn1))])
    return loss, dx, dmod, blocks


def _pack(arrs, rows=None):
    flat = jnp.concatenate([a.reshape(-1).astype(F32) for a in arrs])
    n = flat.shape[0]
    if rows is None:
        cols = 1024
        rows = -(-n // (8 * cols)) * 8
    else:
        cols = -(-n // (rows * 128)) * 128
    return jnp.pad(flat, (0, rows * cols - n)).reshape(rows, cols)


def _unpack(flat, shapes):
    out, off = [], 0
    for shp in shapes:
        n = math.prod(shp)
        out.append(flat[off:off + n].reshape(shp))
        off += n
    return out


def _take_block(a, idx, size, axis):
    return lax.dynamic_slice_in_dim(a, idx * size, size, axis)


def kernel(x, c, positions, ada_w, ada_b, norm_mix_g, norm_ffn_g, ab_w_in, a_vnorm_g, a_spatial_w, a_spatial_b, b_q_norm_g, b_k_norm_g, ab_w_out, conv_pw1_w, conv_pw1_b, conv_dw_w, conv_dw_b, conv_ln_g, conv_ln_b, conv_pw2_w, conv_pw2_b, ffn_up_w, ffn_dw_w, ffn_dw_b, ffn_down_w, loss_target, m_ada_w, m_ada_b, m_norm_mix_g, m_norm_ffn_g, m_ab_w_in, m_a_vnorm_g, m_a_spatial_w, m_a_spatial_b, m_b_q_norm_g, m_b_k_norm_g, m_ab_w_out, m_conv_pw1_w, m_conv_pw1_b, m_conv_dw_w, m_conv_dw_b, m_conv_ln_g, m_conv_ln_b, m_conv_pw2_w, m_conv_pw2_b, m_ffn_up_w, m_ffn_dw_w, m_ffn_dw_b, m_ffn_down_w, v_ada_w, v_ada_b, v_norm_mix_g, v_norm_ffn_g, v_ab_w_in, v_a_vnorm_g, v_a_spatial_w, v_a_spatial_b, v_b_q_norm_g, v_b_k_norm_g, v_ab_w_out, v_conv_pw1_w, v_conv_pw1_b, v_conv_dw_w, v_conv_dw_b, v_conv_ln_g, v_conv_ln_b, v_conv_pw2_w, v_conv_pw2_b, v_ffn_up_w, v_ffn_dw_w, v_ffn_dw_b, v_ffn_down_w):
    weights = dict(ada_w=ada_w, ada_b=ada_b, norm_mix_g=norm_mix_g, norm_ffn_g=norm_ffn_g, ab_w_in=ab_w_in, a_vnorm_g=a_vnorm_g, a_spatial_w=a_spatial_w, a_spatial_b=a_spatial_b, b_q_norm_g=b_q_norm_g, b_k_norm_g=b_k_norm_g, ab_w_out=ab_w_out, conv_pw1_w=conv_pw1_w, conv_pw1_b=conv_pw1_b, conv_dw_w=conv_dw_w, conv_dw_b=conv_dw_b, conv_ln_g=conv_ln_g, conv_ln_b=conv_ln_b, conv_pw2_w=conv_pw2_w, conv_pw2_b=conv_pw2_b, ffn_up_w=ffn_up_w, ffn_dw_w=ffn_dw_w, ffn_dw_b=ffn_dw_b, ffn_down_w=ffn_down_w)
    mom1 = dict(ada_w=m_ada_w, ada_b=m_ada_b, norm_mix_g=m_norm_mix_g, norm_ffn_g=m_norm_ffn_g, ab_w_in=m_ab_w_in, a_vnorm_g=m_a_vnorm_g, a_spatial_w=m_a_spatial_w, a_spatial_b=m_a_spatial_b, b_q_norm_g=m_b_q_norm_g, b_k_norm_g=m_b_k_norm_g, ab_w_out=m_ab_w_out, conv_pw1_w=m_conv_pw1_w, conv_pw1_b=m_conv_pw1_b, conv_dw_w=m_conv_dw_w, conv_dw_b=m_conv_dw_b, conv_ln_g=m_conv_ln_g, conv_ln_b=m_conv_ln_b, conv_pw2_w=m_conv_pw2_w, conv_pw2_b=m_conv_pw2_b, ffn_up_w=m_ffn_up_w, ffn_dw_w=m_ffn_dw_w, ffn_dw_b=m_ffn_dw_b, ffn_down_w=m_ffn_down_w)
    mom2 = dict(ada_w=v_ada_w, ada_b=v_ada_b, norm_mix_g=v_norm_mix_g, norm_ffn_g=v_norm_ffn_g, ab_w_in=v_ab_w_in, a_vnorm_g=v_a_vnorm_g, a_spatial_w=v_a_spatial_w, a_spatial_b=v_a_spatial_b, b_q_norm_g=v_b_q_norm_g, b_k_norm_g=v_b_k_norm_g, ab_w_out=v_ab_w_out, conv_pw1_w=v_conv_pw1_w, conv_pw1_b=v_conv_pw1_b, conv_dw_w=v_conv_dw_w, conv_dw_b=v_conv_dw_b, conv_ln_g=v_conv_ln_g, conv_ln_b=v_conv_ln_b, conv_pw2_w=v_conv_pw2_w, conv_pw2_b=v_conv_pw2_b, ffn_up_w=v_ffn_up_w, ffn_dw_w=v_ffn_dw_w, ffn_dw_b=v_ffn_dw_b, ffn_down_w=v_ffn_down_w)
    order = list(weights)
    d, f2 = D_MODEL, 2 * FFN_DIM
    t = x.shape[1]
    me = 4 * lax.axis_index("x") + 2 * lax.axis_index("y") + lax.axis_index("c")
    for window, dil in PATTERNS:
        assert window // dil == Q_BLOCK and t % (dil * Q_BLOCK) == 0

    small_in = [c[0], conv_pw1_b[0], conv_dw_w[0], conv_dw_b[0], conv_ln_g[0], conv_ln_b[0], conv_pw2_b[0], ffn_dw_w]
    g1 = _all_gather_vmem(_pack(small_in, rows=8), "gather_small").reshape(N_DEV, -1)
    c_all, pw1_b, dw_w, dw_b, ln_g, ln_b, pw2_b, fdw_w = [
        jnp.stack(parts) for parts in zip(*[_unpack(g1[k], [a.shape for a in small_in]) for k in range(N_DEV)])]
    pw1_b, dw_b, ln_g, ln_b, pw2_b = [a.reshape(1, -1) for a in (pw1_b, dw_b, ln_g, ln_b, pw2_b)]
    dw_w = dw_w.transpose(1, 0, 2).reshape(CONV_WIDTH, d)
    fdw_w = fdw_w.transpose(1, 2, 0, 3).reshape(2, FFN_CONV_WIDTH, f2)

    shards = [ab_w_in[0].T, ab_w_out[0], conv_pw1_w[0].T, conv_pw2_w[0], ffn_up_w[0].T, ffn_up_w[1].T, ffn_down_w[0], ffn_down_w[1]]
    full = _all_gather_hbm([s.astype(BF16) for s in shards], "gather_weights")
    w_in_t, w_out, pw1_t, pw2, up_t0, up_t1, down0, down1 = [a.reshape(-1, d) for a in full]

    c16 = jnp.pad(c_all, ((0, 2 * N_DEV - c_all.shape[0]), (0, 0)))
    part = jnp.concatenate([_ada_fwd(c16, ada_w[l], f"ada_fwd{l}")[:N_DEV] for l in range(2)], axis=1)
    g2 = _all_gather_vmem(part, "gather_mod").reshape(N_DEV, N_DEV, 2, -1)
    mod = lax.dynamic_index_in_dim(g2, me, axis=1, keepdims=False).transpose(1, 0, 2).reshape(2, 6 * d) + ada_b

    causal = jnp.tril(jnp.ones((CHUNK, CHUNK), bool))
    wtril = jnp.where(causal[None], a_spatial_w[0], 0.0)
    mixer_w = dict(
        w_in_t=w_in_t, w_out=w_out, gain=a_vnorm_g[0].reshape(1, A_WIDTH), wtril=wtril.astype(BF16),
        wtril_t=wtril.transpose(0, 2, 1).astype(BF16),
        bias_exp=jnp.repeat(a_spatial_b[0].T, GROUP_DIM, axis=1),
        gq=jnp.tile(b_q_norm_g[0], HEADS)[None, :], gk=jnp.tile(b_k_norm_g[0], HEADS)[None, :],
        seg=jnp.kron(jnp.eye(HEADS, dtype=F32), jnp.ones((HEAD_DIM, HEAD_DIM), F32)))
    conv_w = dict(pw1_t=pw1_t, pw1_b=pw1_b, dw_w=dw_w, dw_b=dw_b, ln_g=ln_g, ln_b=ln_b, pw2=pw2, pw2_b=pw2_b)
    ffn_w = [dict(up_t=u_, down=dn, dw_w=fdw_w[l], dw_b=ffn_dw_b[l:l + 1]) for l, (u_, dn) in enumerate(((up_t0, down0), (up_t1, down1)))]

    loss, dx, dmod, (g_mix, g_ffn0, g_conv, g_ffn1) = _local_step(
        x[0], loss_target[0], positions[0].astype(F32)[:, None], mod, norm_mix_g, norm_ffn_g, mixer_w, conv_w, ffn_w)

    big = [g_mix["w_in_t"], g_mix["w_out"], g_conv["pw1_t"], g_conv["pw2"], g_ffn0["up_t"], g_ffn1["up_t"], g_ffn0["down"], g_ffn1["down"]]
    kept, got = _exchange_halves([g.reshape(4, 2, g.shape[0] // N_DEV, d) for g in big], "reduce_cores")
    parts = [_add_bf16(a.reshape(-1, d), b.reshape(-1, d), f"reduce_cores_add{i}").reshape(a.shape)
             for i, (a, b) in enumerate(zip(kept, got))]
    slots = _exchange_chips(parts, "reduce_chips")
    r_in_t, r_out, r_pw1_t, r_pw2, r_up_t0, r_up_t1, r_down0, r_down1 = [
        _sum_slots(s, f"reduce_chips_sum{i}") for i, s in enumerate(slots)]

    small_g = [
        dmod, jnp.concatenate([g_mix["norm_g"], g_conv["norm_g"]]), jnp.concatenate([g_ffn0["norm_g"], g_ffn1["norm_g"]]),
        g_mix["vnorm_g"], g_mix["spatial_w"], g_mix["spatial_b"], g_mix["q_norm_g"], g_mix["k_norm_g"],
        g_conv["pw1_b"], g_conv["dw_w"], g_conv["dw_b"], g_conv["ln_g"], g_conv["ln_b"], g_conv["pw2_b"],
        jnp.stack([g_ffn0["dw_w"], g_ffn1["dw_w"]]), jnp.concatenate([g_ffn0["dw_b"], g_ffn1["dw_b"]])]
    packed = _pack(small_g, rows=8)
    g3 = _all_gather_vmem(packed, "gather_small_grads").reshape(N_DEV, 8, -1)
    total = _unpack(_sum_slots(g3, "sum_small_grads").reshape(-1), [a.shape for a in small_g])
    (s_dmod, s_mix_g, s_ffn_g, s_vnorm, s_sp_w, s_sp_b, s_gq, s_gk, s_pw1_b, s_dw_w, s_dw_b, s_ln_g, s_ln_b,
     s_pw2_b, s_fdw_w, s_fdw_b) = total
    dmod_all = g3.reshape(N_DEV, -1)[:, :2 * 6 * d].reshape(N_DEV, 2, 6 * d)
    n_ada = ada_w.shape[2]
    dmod16 = jnp.pad(_take_block(dmod_all, me, n_ada, 2), ((0, N_DEV), (0, 0), (0, 0)))
    g_ada_w = jnp.stack([_ada_bwd(c16, dmod16[:, l], f"ada_bwd{l}") for l in range(2)])

    grads = dict(
        ada_w=g_ada_w, ada_b=s_dmod, norm_mix_g=s_mix_g, norm_ffn_g=s_ffn_g, ab_w_in=r_in_t.T[None],
        a_vnorm_g=s_vnorm[None], a_spatial_w=s_sp_w[None], a_spatial_b=s_sp_b[None], b_q_norm_g=s_gq[None],
        b_k_norm_g=s_gk[None], ab_w_out=r_out[None], conv_pw1_w=r_pw1_t.T[None],
        conv_pw1_b=_take_block(s_pw1_b, me, conv_pw1_b.shape[1], 1),
        conv_dw_w=_take_block(s_dw_w, me, conv_dw_w.shape[2], 1)[None],
        conv_dw_b=_take_block(s_dw_b, me, conv_dw_b.shape[1], 1), conv_ln_g=_take_block(s_ln_g, me, conv_ln_g.shape[1], 1),
        conv_ln_b=_take_block(s_ln_b, me, conv_ln_b.shape[1], 1), conv_pw2_w=r_pw2[None],
        conv_pw2_b=_take_block(s_pw2_b, me, conv_pw2_b.shape[1], 1),
        ffn_up_w=jnp.stack([r_up_t0.T, r_up_t1.T]), ffn_dw_w=_take_block(s_fdw_w, me, ffn_dw_w.shape[2], 2),
        ffn_dw_b=s_fdw_b, ffn_down_w=jnp.stack([r_down0, r_down1]))

    large = ("ada_w", "ab_w_in", "ab_w_out", "conv_pw1_w", "conv_pw2_w", "ffn_up_w", "ffn_down_w")
    delta, new_m, new_v = {}, {}, {}
    for name in large:
        shp = weights[name].shape
        two_d = lambda a: a.reshape(-1, shp[-1])
        res = _adamw(two_d(weights[name]), two_d(grads[name]), two_d(mom1[name]), two_d(mom2[name]), f"adamw_{name}")
        delta[name], new_m[name], new_v[name] = [r.reshape(shp) for r in res]
    small = [n for n in order if n not in large]
    shapes = [weights[n].shape for n in small]
    res = _adamw(*[_pack([src[n] for n in small]) for src in (weights, grads, mom1, mom2)], "adamw_small")
    for dst, r in zip((delta, new_m, new_v), res):
        for n, a in zip(small, _unpack(r.reshape(-1), shapes)):
            dst[n] = a

    loss = lax.psum(loss[0, 0], ("x", "y", "c"))
    return (loss, dx[None], *[grads[n] for n in order], *[delta[n] for n in order],
            *[new_m[n] for n in order], *[new_v[n] for n in order])
```

```python
import functools
import math

import jax
import jax.numpy as jnp
from jax import lax
from jax.experimental import pallas as pl
from jax.experimental.pallas import tpu as pltpu

F32 = jnp.float32
BF16 = jnp.bfloat16
MESH = pl.DeviceIdType.MESH

D_MODEL = 1024
A_WIDTH = 512
A_GROUPS = 4
GROUP_DIM = 128
CHUNK = 128
B_WIDTH = 512
HEADS = 8
HEAD_DIM = 64
PATTERNS = ((128, 1), (512, 4), (2048, 16))
Q_BLOCK = 128
ROPE_THETA = 10000.0
AB_IN = 2560
CONV_WIDTH = 31
FFN_DIM = 2816
FFN_CONV_WIDTH = 3
EPS = 1e-6
NEG = -1e30
N_DEV = 8
ADAM_LR, ADAM_B1, ADAM_B2, ADAM_EPS, ADAM_WD, ADAM_STEP = 0.001, 0.9, 0.999, 1e-08, 0.01, 10

V7X_VMEM_LIMIT = 56 * 2**20
BF16_ROWS = 16
FFN_HALO = 16
CONV_HALO = 32

_NN = (((1,), (0,)), ((), ()))
_NT = (((1,), (1,)), ((), ()))
_TN = (((0,), (0,)), ((), ()))


def _tile(n, prefs=(512, 256, 128)):
    for t in prefs:
        if n % t == 0:
            return t
    return n


def _row_tile(n, cap=512):
    best = n
    for t in range(8, min(n, cap) + 1, 8):
        if n % t == 0:
            best = t
    return best if best <= cap else n


def _params(*sem):
    return pltpu.CompilerParams(dimension_semantics=sem, vmem_limit_bytes=V7X_VMEM_LIMIT)


def _dot(a, b, dims):
    return lax.dot_general(a, b, dims, preferred_element_type=F32)


def _sigmoid(x):
    return 1.0 / (1.0 + jnp.exp(-x))


def _gelu(x):
    return 0.5 * x * (1.0 + lax.erf(x * (2.0 ** -0.5)))


def _gelu_grad(x):
    return 0.5 * (1.0 + lax.erf(x * (2.0 ** -0.5))) + x * jnp.exp(-0.5 * x * x) * (1.0 / math.sqrt(2.0 * math.pi))


def _colsum(v):
    return jnp.sum(v, axis=0, keepdims=True)


MATMUL_VMEM_BUDGET = 40 * 2**20


def _matmul_tiles(m, n, k, out_bytes, with_resid):
    def options(dim):
        opts = [t for t in (1024, 512, 256, 128) if dim % t == 0]
        return opts + [dim] if dim <= 4096 and dim not in opts else opts

    best = None
    for tm in options(m):
        for tn in options(n):
            need = 4 * (tm * k + k * tn) + tm * tn * (4 + 2 * out_bytes) + (24 * tm * tn if with_resid else 0)
            if need <= MATMUL_VMEM_BUDGET and (best is None or tm * tn / (tm + tn) > best[0]):
                best = (tm * tn / (tm + tn), tm, tn)
    return best[1], best[2]


def _matmul_tn_acc(a, b, name, tk=512):
    squeeze = a.ndim == 2
    a3 = a[None] if squeeze else a
    p_, t, m = a3.shape
    n = b.shape[1]
    nk = t // tk

    def body(a_ref, b_ref, o_ref, acc_ref):
        kt = pl.program_id(1)

        @pl.when(kt == 0)
        def _():
            acc_ref[...] = jnp.zeros_like(acc_ref)

        acc_ref[...] += _dot(a_ref[...], b_ref[...], _TN)

        @pl.when(kt == nk - 1)
        def _():
            o_ref[...] = acc_ref[...].astype(BF16)

    out = pl.pallas_call(
        body, name=name, grid=(p_, nk),
        in_specs=[pl.BlockSpec((None, tk, m), lambda p, kt: (p, kt, 0)), pl.BlockSpec((tk, n), lambda p, kt: (kt, 0))],
        out_specs=pl.BlockSpec((None, m, n), lambda p, kt: (p, 0, 0)), out_shape=jax.ShapeDtypeStruct((p_, m, n), BF16),
        scratch_shapes=[pltpu.VMEM((m, n), F32)], compiler_params=_params("parallel", "arbitrary"),
    )(a3, b)
    return out[0] if squeeze else out


def _matmul(a, b, mode, out_dtype, name, bias=None, resid=None):
    if mode == "nn":
        (m, k), (_, n) = a.shape, b.shape
    elif mode == "nt":
        (m, k), (n, _) = a.shape, b.shape
    else:
        (k, m), (_, n) = a.shape, b.shape
    tm, tn = _matmul_tiles(m, n, k, jnp.dtype(out_dtype).itemsize, resid is not None)
    dims = {"nn": _NN, "nt": _NT, "tn": _TN}[mode]
    a_spec = pl.BlockSpec((k, tm), lambda i, j: (0, i)) if mode == "tn" else pl.BlockSpec((tm, k), lambda i, j: (i, 0))
    b_spec = pl.BlockSpec((tn, k), lambda i, j: (j, 0)) if mode == "nt" else pl.BlockSpec((k, tn), lambda i, j: (0, j))
    in_specs, args = [a_spec, b_spec], [a, b]
    row_spec = pl.BlockSpec((1, tn), lambda i, j: (0, j))
    tile_spec = pl.BlockSpec((tm, tn), lambda i, j: (i, j))
    if bias is not None:
        in_specs.append(row_spec)
        args.append(bias)
    if resid is not None:
        in_specs += [tile_spec, row_spec]
        args += list(resid)
    out_shape = [jax.ShapeDtypeStruct((m, n), out_dtype)]
    out_specs = [tile_spec]
    if resid is not None:
        out_shape.append(jax.ShapeDtypeStruct((m, n), F32))
        out_specs.append(tile_spec)

    def body(*refs):
        a_ref, b_ref = refs[0], refs[1]
        pos = 2
        acc = _dot(a_ref[...], b_ref[...], dims)
        if bias is not None:
            acc = acc + refs[pos][...]
            pos += 1
        if resid is not None:
            x_ref, g_ref = refs[pos], refs[pos + 1]
            pos += 2
        refs[pos][...] = acc.astype(out_dtype)
        if resid is not None:
            refs[pos + 1][...] = x_ref[...] + g_ref[...] * acc

    outs = pl.pallas_call(
        body, name=name, grid=(m // tm, n // tn), in_specs=in_specs, out_specs=out_specs, out_shape=out_shape,
        compiler_params=_params("parallel", "parallel"),
    )(*args)
    return outs if resid is not None else outs[0]


def _modnorm(x, g, sc, sh, name):
    t, d = x.shape
    tm = _tile(t)
    row = pl.BlockSpec((1, d), lambda i: (0, 0))
    blk = pl.BlockSpec((tm, d), lambda i: (i, 0))

    def body(x_ref, g_ref, sc_ref, sh_ref, o_ref):
        x = x_ref[...]
        r = lax.rsqrt(jnp.mean(x * x, axis=-1, keepdims=True) + EPS)
        o_ref[...] = ((x * r) * g_ref[...] * (1.0 + sc_ref[...]) + sh_ref[...]).astype(BF16)

    return pl.pallas_call(
        body, name=name, grid=(t // tm,), in_specs=[blk, row, row, row], out_specs=blk,
        out_shape=jax.ShapeDtypeStruct((t, d), BF16), compiler_params=_params("parallel"),
    )(x, g, sc, sh)


def _modnorm_bwd(x, dh, g, sc, dres, name):
    t, d = x.shape
    tm = _tile(t)
    row = pl.BlockSpec((1, d), lambda i: (0, 0))
    blk = pl.BlockSpec((tm, d), lambda i: (i, 0))

    def body(x_ref, dh_ref, g_ref, sc_ref, dres_ref, dx_ref, dw_ref, dsh_ref):
        @pl.when(pl.program_id(0) == 0)
        def _():
            dw_ref[...] = jnp.zeros_like(dw_ref)
            dsh_ref[...] = jnp.zeros_like(dsh_ref)

        x = x_ref[...]
        dh = dh_ref[...].astype(F32)
        r = lax.rsqrt(jnp.mean(x * x, axis=-1, keepdims=True) + EPS)
        xn = x * r
        dxn = dh * (g_ref[...] * (1.0 + sc_ref[...]))
        dx_ref[...] = dres_ref[...] + r * (dxn - xn * jnp.mean(dxn * xn, axis=-1, keepdims=True))
        dw_ref[...] += _colsum(dh * xn)
        dsh_ref[...] += _colsum(dh)

    return pl.pallas_call(
        body, name=name, grid=(t // tm,), in_specs=[blk, blk, row, row, blk], out_specs=[blk, row, row],
        out_shape=[jax.ShapeDtypeStruct((t, d), F32), jax.ShapeDtypeStruct((1, d), F32), jax.ShapeDtypeStruct((1, d), F32)],
        compiler_params=_params("arbitrary"),
    )(x, dh, g, sc, dres)


def _gate_bwd(dxn, y, gate, name):
    t, d = dxn.shape
    tm = _tile(t)
    row = pl.BlockSpec((1, d), lambda i: (0, 0))
    blk = pl.BlockSpec((tm, d), lambda i: (i, 0))

    def body(dxn_ref, y_ref, g_ref, dy_ref, dg_ref):
        @pl.when(pl.program_id(0) == 0)
        def _():
            dg_ref[...] = jnp.zeros_like(dg_ref)

        dxn = dxn_ref[...]
        dy_ref[...] = (dxn * g_ref[...]).astype(BF16)
        dg_ref[...] += _colsum(dxn * y_ref[...])

    return pl.pallas_call(
        body, name=name, grid=(t // tm,), in_specs=[blk, blk, row], out_specs=[blk, row],
        out_shape=[jax.ShapeDtypeStruct((t, d), BF16), jax.ShapeDtypeStruct((1, d), F32)],
        compiler_params=_params("arbitrary"),
    )(dxn, y, gate)


def _loss_head(y, target, name):
    t, d = y.shape
    tm = _tile(t)
    blk = pl.BlockSpec((tm, d), lambda i: (i, 0))
    one = pl.BlockSpec((1, 1), lambda i: (0, 0))

    def body(y_ref, t_ref, dy_ref, loss_ref, acc_ref):
        @pl.when(pl.program_id(0) == 0)
        def _():
            acc_ref[...] = jnp.zeros_like(acc_ref)

        e = y_ref[...] - t_ref[...]
        dy_ref[...] = e * (1.0 / d)
        acc_ref[...] += _colsum(e * e)

        @pl.when(pl.program_id(0) == pl.num_programs(0) - 1)
        def _():
            loss_ref[...] = jnp.sum(acc_ref[...], axis=1, keepdims=True) * (0.5 / d)

    return pl.pallas_call(
        body, name=name, grid=(t // tm,), in_specs=[blk, blk], out_specs=[blk, one],
        out_shape=[jax.ShapeDtypeStruct((t, d), F32), jax.ShapeDtypeStruct((1, 1), F32)],
        scratch_shapes=[pltpu.VMEM((1, d), F32)], compiler_params=_params("arbitrary"),
    )(y, target)


def _group_norm(vg, gain):
    mu = jnp.mean(vg, axis=-1, keepdims=True)
    xc = vg - mu
    rstd = lax.rsqrt(jnp.mean(xc * xc, axis=-1, keepdims=True) + EPS)
    xhat = xc * rstd
    return xhat, rstd, xhat * gain


def _gmlp_fwd(z, gain, wtril, bias_exp, name):
    t = z.shape[0]
    zu = pl.BlockSpec((CHUNK, A_WIDTH), lambda i: (i, 0))
    zv = pl.BlockSpec((CHUNK, A_WIDTH), lambda i: (i, 1))
    full2 = lambda shp: pl.BlockSpec(shp, lambda i: (0, 0))
    w_spec = pl.BlockSpec((A_GROUPS, CHUNK, CHUNK), lambda i: (0, 0, 0))

    def body(zu_ref, zv_ref, gain_ref, w_ref, b_ref, ya_ref):
        ua = _gelu(zu_ref[...].astype(F32))
        vg = _gelu(zv_ref[...].astype(F32))
        for g in range(A_GROUPS):
            sl = slice(g * GROUP_DIM, (g + 1) * GROUP_DIM)
            _, _, vn = _group_norm(vg[:, sl], gain_ref[:, sl])
            f = _dot(w_ref[g], vn.astype(BF16), _NN) + b_ref[:, sl]
            ya_ref[:, sl] = (ua[:, sl] * f).astype(BF16)

    return pl.pallas_call(
        body, name=name, grid=(t // CHUNK,),
        in_specs=[zu, zv, full2((1, A_WIDTH)), w_spec, full2((CHUNK, A_WIDTH))], out_specs=zu,
        out_shape=jax.ShapeDtypeStruct((t, A_WIDTH), BF16), compiler_params=_params("parallel"),
    )(z, z, gain, wtril, bias_exp)


def _gmlp_bwd(z, dcat, gain, wtril, wtril_t, bias_exp, name):
    t = z.shape[0]
    zu = pl.BlockSpec((CHUNK, A_WIDTH), lambda i: (i, 0))
    zv = pl.BlockSpec((CHUNK, A_WIDTH), lambda i: (i, 1))
    full2 = lambda shp: pl.BlockSpec(shp, lambda i: (0, 0))
    w_spec = pl.BlockSpec((A_GROUPS, CHUNK, CHUNK), lambda i: (0, 0, 0))
    dz_spec = pl.BlockSpec((CHUNK, 2 * A_WIDTH), lambda i: (i, 0))

    def body(zu_ref, zv_ref, dya_ref, gain_ref, w_ref, wt_ref, b_ref, dz_ref, dw_ref, dgain_ref, dbias_ref):
        @pl.when(pl.program_id(0) == 0)
        def _():
            dw_ref[...] = jnp.zeros_like(dw_ref)
            dgain_ref[...] = jnp.zeros_like(dgain_ref)
            dbias_ref[...] = jnp.zeros_like(dbias_ref)

        zu_v = zu_ref[...].astype(F32)
        zv_v = zv_ref[...].astype(F32)
        dya = dya_ref[...].astype(F32)
        ua = _gelu(zu_v)
        vg = _gelu(zv_v)
        row = lax.broadcasted_iota(jnp.int32, (CHUNK, CHUNK), 0)
        col = lax.broadcasted_iota(jnp.int32, (CHUNK, CHUNK), 1)
        for g in range(A_GROUPS):
            sl = slice(g * GROUP_DIM, (g + 1) * GROUP_DIM)
            gain_g = gain_ref[:, sl]
            xhat, rstd, vn = _group_norm(vg[:, sl], gain_g)
            vn16 = vn.astype(BF16)
            f = _dot(w_ref[g], vn16, _NN) + b_ref[:, sl]
            df = dya[:, sl] * ua[:, sl]
            df16 = df.astype(BF16)
            dz_ref[:, sl] = (dya[:, sl] * f * _gelu_grad(zu_v[:, sl])).astype(BF16)
            dw_ref[g] += jnp.where(row >= col, _dot(df16, vn16, _NT), 0.0)
            dvn = _dot(wt_ref[g], df16, _NN)
            dgain_ref[:, sl] += _colsum(dvn * xhat)
            dxh = dvn * gain_g
            dvg = rstd * (dxh - jnp.mean(dxh, axis=-1, keepdims=True) - xhat * jnp.mean(dxh * xhat, axis=-1, keepdims=True))
            dz_ref[:, A_WIDTH + g * GROUP_DIM:A_WIDTH + (g + 1) * GROUP_DIM] = (dvg * _gelu_grad(zv_v[:, sl])).astype(BF16)
            dbias_ref[:, sl] += df

    return pl.pallas_call(
        body, name=name, grid=(t // CHUNK,),
        in_specs=[zu, zv, zu, full2((1, A_WIDTH)), w_spec, w_spec, full2((CHUNK, A_WIDTH))],
        out_specs=[dz_spec, w_spec, full2((1, A_WIDTH)), full2((CHUNK, A_WIDTH))],
        out_shape=[jax.ShapeDtypeStruct((t, 2 * A_WIDTH), BF16), jax.ShapeDtypeStruct((A_GROUPS, CHUNK, CHUNK), F32),
                   jax.ShapeDtypeStruct((1, A_WIDTH), F32), jax.ShapeDtypeStruct((CHUNK, A_WIDTH), F32)],
        compiler_params=_params("arbitrary"),
    )(z, z, dcat, gain, wtril, wtril_t, bias_exp)


def _rope_tables(pos, inv_freq, sign, name):
    t = pos.shape[0]
    tm = _tile(t)
    row = pl.BlockSpec((1, B_WIDTH), lambda i: (0, 0))
    blk = pl.BlockSpec((tm, B_WIDTH), lambda i: (i, 0))

    def body(pos_ref, f_ref, s_ref, cos_ref, sin_ref):
        ang = pos_ref[...] * f_ref[...]
        cos_ref[...] = jnp.cos(ang)
        sin_ref[...] = jnp.sin(ang) * s_ref[...]

    return pl.pallas_call(
        body, name=name, grid=(t // tm,), in_specs=[pl.BlockSpec((tm, 1), lambda i: (i, 0)), row, row],
        out_specs=[blk, blk], out_shape=[jax.ShapeDtypeStruct((t, B_WIDTH), F32)] * 2,
        compiler_params=_params("parallel"),
    )(pos, inv_freq, sign)


def _head_sum(v, seg):
    return lax.dot_general(v, seg, _NN, precision=lax.Precision.HIGHEST, preferred_element_type=F32)


def _swap_halves(v):
    lane = lax.broadcasted_iota(jnp.int32, v.shape, 1)
    return jnp.where((lane & (HEAD_DIM - 1)) < HEAD_DIM // 2,pltpu.roll(v, B_WIDTH - HEAD_DIM // 2, 1), pltpu.roll(v, HEAD_DIM // 2, 1))


def _qk_prep(z, cos_t, sin_t, gq, gk, seg, name):
    t = z.shape[0]
    tm = _tile(t, (256, 128))
    col = lambda c: pl.BlockSpec((tm, B_WIDTH), lambda i: (i, c))
    row = pl.BlockSpec((1, B_WIDTH), lambda i: (0, 0))
    blk = col(0)

    def body(q_ref, k_ref, v_ref, cos_ref, sin_ref, gq_ref, gk_ref, seg_ref, qo_ref, ko_ref, vo_ref):
        def norm_rot(x, g):
            r = lax.rsqrt(_head_sum(x * x, seg_ref[...]) * (1.0 / HEAD_DIM) + EPS)
            xn = x * r * g
            return xn * cos_ref[...] + _swap_halves(xn) * sin_ref[...]

        qo_ref[...] = norm_rot(q_ref[...].astype(F32), gq_ref[...]).astype(BF16)
        ko_ref[...] = norm_rot(k_ref[...].astype(F32), gk_ref[...]).astype(BF16)
        vo_ref[...] = v_ref[...].astype(BF16)

    return pl.pallas_call(
        body, name=name, grid=(t // tm,),
        in_specs=[col(2), col(3), col(4), blk, blk, row, row, pl.BlockSpec((B_WIDTH, B_WIDTH), lambda i: (0, 0))],
        out_specs=[blk, blk, blk], out_shape=[jax.ShapeDtypeStruct((t, B_WIDTH), BF16)] * 3,
        compiler_params=_params("parallel"),
    )(z, z, z, cos_t, sin_t, gq, gk, seg)


def _qk_prep_bwd(z, dqs, dks, dvs, cos_t, sin_t, gq, gk, seg, name):
    t = z.shape[0]
    tm = _tile(t, (256, 128))
    col = lambda c: pl.BlockSpec((tm, B_WIDTH), lambda i: (i, c))
    row = pl.BlockSpec((1, B_WIDTH), lambda i: (0, 0))
    blk = col(0)
    nb = len(dqs)

    def body(*refs):
        q_ref, k_ref = refs[0], refs[1]
        dq_refs, dk_refs, dv_refs = refs[2:2 + nb], refs[2 + nb:2 + 2 * nb], refs[2 + 2 * nb:2 + 3 * nb]
        cos_ref, sin_ref, gq_ref, gk_ref, seg_ref, dz_ref, dgq_ref, dgk_ref = refs[2 + 3 * nb:]

        @pl.when(pl.program_id(0) == 0)
        def _():
            dgq_ref[...] = jnp.zeros_like(dgq_ref)
            dgk_ref[...] = jnp.zeros_like(dgk_ref)

        def back(x, d_refs, g, dg_ref):
            dout = d_refs[0][...]
            for r_ in d_refs[1:]:
                dout = dout + r_[...]
            dy = dout * cos_ref[...] + _swap_halves(dout * sin_ref[...])
            r = lax.rsqrt(_head_sum(x * x, seg_ref[...]) * (1.0 / HEAD_DIM) + EPS)
            xn = x * r
            dg_ref[...] += _colsum(dy * xn)
            dxn = dy * g
            return r * (dxn - xn * (_head_sum(dxn * xn, seg_ref[...]) * (1.0 / HEAD_DIM)))

        dz_ref[:, 0:B_WIDTH] = back(q_ref[...].astype(F32), dq_refs, gq_ref[...], dgq_ref).astype(BF16)
        dz_ref[:, B_WIDTH:2 * B_WIDTH] = back(k_ref[...].astype(F32), dk_refs, gk_ref[...], dgk_ref).astype(BF16)
        dv = dv_refs[0][...]
        for r_ in dv_refs[1:]:
            dv = dv + r_[...]
        dz_ref[:, 2 * B_WIDTH:3 * B_WIDTH] = dv.astype(BF16)

    return pl.pallas_call(
        body, name=name, grid=(t // tm,),
        in_specs=[col(2), col(3)] + [blk] * (3 * nb) + [blk, blk, row, row, pl.BlockSpec((B_WIDTH, B_WIDTH), lambda i: (0, 0))],
        out_specs=[pl.BlockSpec((tm, 3 * B_WIDTH), lambda i: (i, 0)), row, row],
        out_shape=[jax.ShapeDtypeStruct((t, 3 * B_WIDTH), BF16), jax.ShapeDtypeStruct((1, B_WIDTH), F32),
                   jax.ShapeDtypeStruct((1, B_WIDTH), F32)],
        compiler_params=_params("arbitrary"),
    )(z, z, *dqs, *dks, *dvs, cos_t, sin_t, gq, gk, seg)


def _subseq(a, dil):
    return a.reshape(a.shape[0] // dil, dil * a.shape[1])


def _attn_fwd(q, k, v, dil, name):
    t = q.shape[0]
    nb = t // dil // Q_BLOCK
    cur = pl.BlockSpec((Q_BLOCK, B_WIDTH), lambda r, i: (i, r))
    prev = pl.BlockSpec((Q_BLOCK, B_WIDTH), lambda r, i: (jnp.maximum(i - 1, 0), r))

    def body(q_ref, kp_ref, kc_ref, vp_ref, vc_ref, o_ref, lse_ref):
        i = pl.program_id(1)
        q = q_ref[...]
        kk = jnp.concatenate([kp_ref[...], kc_ref[...]], axis=0)
        vv = jnp.concatenate([vp_ref[...], vc_ref[...]], axis=0)
        a = lax.broadcasted_iota(jnp.int32, (Q_BLOCK, 2 * Q_BLOCK), 0)
        j = lax.broadcasted_iota(jnp.int32, (Q_BLOCK, 2 * Q_BLOCK), 1)
        dist = a + Q_BLOCK - j
        mask = (dist >= 0) & (dist <= Q_BLOCK) & ((j >= Q_BLOCK) | (i > 0))
        for h in range(HEADS):
            sl = slice(h * HEAD_DIM, (h + 1) * HEAD_DIM)
            s = jnp.where(mask, _dot(q[:, sl], kk[:, sl], _NT) * (HEAD_DIM ** -0.5), NEG)
            m = jnp.max(s, axis=-1, keepdims=True)
            p = jnp.exp(s - m)
            den = jnp.sum(p, axis=-1, keepdims=True)
            o_ref[:, sl] = _dot(p.astype(BF16), vv[:, sl], _NN) / den
            lse_ref[:, sl] = jnp.broadcast_to(m + jnp.log(den), (Q_BLOCK, HEAD_DIM))

    o, lse = pl.pallas_call(
        body, name=name, grid=(dil, nb), in_specs=[cur, prev, cur, prev, cur], out_specs=[cur, cur],
        out_shape=[jax.ShapeDtypeStruct((t // dil, dil * B_WIDTH), F32)] * 2,
        compiler_params=_params("parallel", "parallel"),
    )(_subseq(q, dil), _subseq(k, dil), _subseq(k, dil), _subseq(v, dil), _subseq(v, dil))
    return o.reshape(t, B_WIDTH), lse.reshape(t, B_WIDTH)


def _attn_merge(outs, lses, name):
    t = outs[0].shape[0]
    tm = _tile(t)
    blk = pl.BlockSpec((tm, B_WIDTH), lambda i: (i, 0))
    nb = len(outs)

    def body(*refs):
        o_refs, l_refs, yb_ref, lse_ref = refs[:nb], refs[nb:2 * nb], refs[2 * nb], refs[2 * nb + 1]
        ls = [r[...] for r in l_refs]
        m = functools.reduce(jnp.maximum, ls)
        tot = m + jnp.log(sum(jnp.exp(l - m) for l in ls))
        yb_ref[...] = sum(jnp.exp(l - tot) * o[...] for l, o in zip(ls, o_refs)).astype(BF16)
        lse_ref[...] = tot

    return pl.pallas_call(
        body, name=name, grid=(t // tm,), in_specs=[blk] * (2 * nb), out_specs=[blk, blk],
        out_shape=[jax.ShapeDtypeStruct((t, B_WIDTH), BF16), jax.ShapeDtypeStruct((t, B_WIDTH), F32)],
        compiler_params=_params("parallel"),
    )(*outs, *lses)


def _attn_bwd_q(q, k, v, do, o, lse, dil, name):
    t = q.shape[0]
    nb = t // dil // Q_BLOCK
    cur = pl.BlockSpec((Q_BLOCK, B_WIDTH), lambda r, i: (i, r))
    prev = pl.BlockSpec((Q_BLOCK, B_WIDTH), lambda r, i: (jnp.maximum(i - 1, 0), r))

    def body(q_ref, kp_ref, kc_ref, vp_ref, vc_ref, do_ref, o_ref, lse_ref, dq_ref):
        i = pl.program_id(1)
        q = q_ref[...]
        kk = jnp.concatenate([kp_ref[...], kc_ref[...]], axis=0)
        vv = jnp.concatenate([vp_ref[...], vc_ref[...]], axis=0)
        do = do_ref[...]
        dof = do.astype(F32)
        of = o_ref[...].astype(F32)
        a = lax.broadcasted_iota(jnp.int32, (Q_BLOCK, 2 * Q_BLOCK), 0)
        j = lax.broadcasted_iota(jnp.int32, (Q_BLOCK, 2 * Q_BLOCK), 1)
        dist = a + Q_BLOCK - j
        mask = (dist >= 0) & (dist <= Q_BLOCK) & ((j >= Q_BLOCK) | (i > 0))
        for h in range(HEADS):
            sl = slice(h * HEAD_DIM, (h + 1) * HEAD_DIM)
            s = jnp.where(mask, _dot(q[:, sl], kk[:, sl], _NT) * (HEAD_DIM ** -0.5), NEG)
            p = jnp.exp(s - lse_ref[:, h * HEAD_DIM:h * HEAD_DIM + 1])
            dp = _dot(do[:, sl], vv[:, sl], _NT)
            delta = jnp.sum(dof[:, sl] * of[:, sl], axis=-1, keepdims=True)
            ds = p * (dp - delta) * (HEAD_DIM ** -0.5)
            dq_ref[:, sl] = _dot(ds.astype(BF16), kk[:, sl], _NN)

    dq = pl.pallas_call(
        body, name=name, grid=(dil, nb), in_specs=[cur, prev, cur, prev, cur, cur, cur, cur], out_specs=cur,
        out_shape=jax.ShapeDtypeStruct((t // dil, dil * B_WIDTH), F32),
        compiler_params=_params("parallel", "parallel"),
    )(_subseq(q, dil), _subseq(k, dil), _subseq(k, dil), _subseq(v, dil), _subseq(v, dil),
      _subseq(do, dil), _subseq(o, dil), _subseq(lse, dil))
    return dq.reshape(t, B_WIDTH)


def _attn_bwd_kv(q, k, v, do, o, lse, dil, name):
    t = q.shape[0]
    nb = t // dil // Q_BLOCK
    cur = pl.BlockSpec((Q_BLOCK, B_WIDTH), lambda r, j: (j, r))
    nxt = pl.BlockSpec((Q_BLOCK, B_WIDTH), lambda r, j: (jnp.minimum(j + 1, nb - 1), r))

    def body(k_ref, v_ref, qc_ref, qn_ref, doc_ref, don_ref, oc_ref, on_ref, lc_ref, ln_ref, dk_ref, dv_ref):
        jb = pl.program_id(1)
        kb = k_ref[...]
        vb = v_ref[...]
        qq = jnp.concatenate([qc_ref[...], qn_ref[...]], axis=0)
        dd = jnp.concatenate([doc_ref[...], don_ref[...]], axis=0)
        ddf = dd.astype(F32)
        oo = jnp.concatenate([oc_ref[...], on_ref[...]], axis=0).astype(F32)
        ll = jnp.concatenate([lc_ref[...], ln_ref[...]], axis=0)
        a = lax.broadcasted_iota(jnp.int32, (2 * Q_BLOCK, Q_BLOCK), 0)
        b = lax.broadcasted_iota(jnp.int32, (2 * Q_BLOCK, Q_BLOCK), 1)
        dist = a - b
        mask = (dist >= 0) & (dist <= Q_BLOCK) & ((a < Q_BLOCK) | (jb < nb - 1))
        for h in range(HEADS):
            sl = slice(h * HEAD_DIM, (h + 1) * HEAD_DIM)
            s = jnp.where(mask, _dot(qq[:, sl], kb[:, sl], _NT) * (HEAD_DIM ** -0.5), NEG)
            p = jnp.exp(s - ll[:, h * HEAD_DIM:h * HEAD_DIM + 1])
            dv_ref[:, sl] = _dot(p.astype(BF16), dd[:, sl], _TN)
            dp = _dot(dd[:, sl], vb[:, sl], _NT)
            delta = jnp.sum(ddf[:, sl] * oo[:, sl], axis=-1, keepdims=True)
            ds = p * (dp - delta) * (HEAD_DIM ** -0.5)
            dk_ref[:, sl] = _dot(ds.astype(BF16), qq[:, sl], _TN)

    qs, dos, os_, ls = _subseq(q, dil), _subseq(do, dil), _subseq(o, dil), _subseq(lse, dil)
    dk, dv = pl.pallas_call(
        body, name=name, grid=(dil, nb), in_specs=[cur, cur, cur, nxt, cur, nxt, cur, nxt, cur, nxt], out_specs=[cur, cur],
        out_shape=[jax.ShapeDtypeStruct((t // dil, dil * B_WIDTH), F32)] * 2,
        compiler_params=_params("parallel", "parallel"),
    )(_subseq(k, dil), _subseq(v, dil), qs, qs, dos, dos, os_, os_, ls, ls)
    return dk.reshape(t, B_WIDTH), dv.reshape(t, B_WIDTH)


FFN_TN = 256


def _ffn_up(h, up_t, name):
    t, k = h.shape
    tm = _tile(t)

    def body(h_ref, w_ref, o_ref):
        o_ref[...] = _dot(h_ref[...], w_ref[...], _NT).astype(BF16)

    return pl.pallas_call(
        body, name=name, grid=(2, t // tm),
        in_specs=[pl.BlockSpec((tm, k), lambda p, i: (i, 0)), pl.BlockSpec((None, FFN_DIM, k), lambda p, i: (p, 0, 0))],
        out_specs=pl.BlockSpec((None, tm, FFN_DIM), lambda p, i: (p, i, 0)),
        out_shape=jax.ShapeDtypeStruct((2, t, FFN_DIM), BF16), compiler_params=_params("parallel", "parallel"),
    )(h, up_t.reshape(2, FFN_DIM, k))


def _ffn_up_dx(du, up_t, name):
    t = du.shape[1]
    k = up_t.shape[1]
    tm = _tile(t)

    def body(a_ref, b_ref, o_ref):
        o_ref[...] = _dot(a_ref[0], b_ref[0], _NN) + _dot(a_ref[1], b_ref[1], _NN)

    return pl.pallas_call(
        body, name=name, grid=(t // tm,),
        in_specs=[pl.BlockSpec((2, tm, FFN_DIM), lambda i: (0, i, 0)), pl.BlockSpec((2, FFN_DIM, k), lambda i: (0, 0, 0))],
        out_specs=pl.BlockSpec((tm, k), lambda i: (i, 0)), out_shape=jax.ShapeDtypeStruct((t, k), F32),
        compiler_params=_params("parallel"),
    )(du, up_t.reshape(2, FFN_DIM, k))


def _ffn_conv(scr_ref, w_ref, b_ref, p, rows):
    acc = b_ref[p] + w_ref[p, FFN_CONV_WIDTH - 1:FFN_CONV_WIDTH, :] * scr_ref[p, pl.ds(FFN_HALO, rows), :]
    for k in range(FFN_CONV_WIDTH - 1):
        acc = acc + w_ref[p, k:k + 1, :] * scr_ref[p, pl.ds(FFN_HALO - (FFN_CONV_WIDTH - 1) + k, rows), :]
    return acc


def _ffn_act(u, dw_w, dw_b, name):
    t = u.shape[1]
    tm = _tile(t)
    hb = tm // FFN_HALO
    main = pl.BlockSpec((2, tm, FFN_TN), lambda i, j: (0, i, j))
    halo = pl.BlockSpec((2, FFN_HALO, FFN_TN), lambda i, j: (0, jnp.maximum(i * hb - 1, 0), j))
    wsp = pl.BlockSpec((2, FFN_CONV_WIDTH, FFN_TN), lambda i, j: (0, 0, j))
    bsp = pl.BlockSpec((2, 1, FFN_TN), lambda i, j: (0, 0, j))

    def body(u_ref, uh_ref, w_ref, b_ref, o_ref, scr_ref):
        first = pl.program_id(0) == 0
        for p in range(2):
            scr_ref[p, 0:FFN_HALO, :] = jnp.where(first, 0.0, uh_ref[p].astype(F32))
            scr_ref[p, FFN_HALO:, :] = u_ref[p].astype(F32)
        za = _ffn_conv(scr_ref, w_ref, b_ref, 0, tm)
        zb = _ffn_conv(scr_ref, w_ref, b_ref, 1, tm)
        o_ref[...] = (za * _sigmoid(za) * zb).astype(BF16)

    return pl.pallas_call(
        body, name=name, grid=(t // tm, FFN_DIM // FFN_TN), in_specs=[main, halo, wsp, bsp],
        out_specs=pl.BlockSpec((tm, FFN_TN), lambda i, j: (i, j)), out_shape=jax.ShapeDtypeStruct((t, FFN_DIM), BF16),
        scratch_shapes=[pltpu.VMEM((2, tm + FFN_HALO, FFN_TN), F32)], compiler_params=_params("parallel", "parallel"),
    )(u, u, dw_w, dw_b)


def _ffn_act_bwd(u, dact, dw_w, dw_b, name):
    t = u.shape[1]
    tm = _tile(t)
    hb = tm // FFN_HALO
    nt = t // tm
    last_halo = t // FFN_HALO - 1
    rows = tm + FFN_HALO
    prev_i = lambda i: jnp.maximum(i * hb - 1, 0)
    next_i = lambda i: jnp.minimum((i + 1) * hb, last_halo)
    main = pl.BlockSpec((2, tm, FFN_TN), lambda j, i: (0, i, j))
    prev = pl.BlockSpec((2, FFN_HALO, FFN_TN), lambda j, i: (0, prev_i(i), j))
    nxt = pl.BlockSpec((2, FFN_HALO, FFN_TN), lambda j, i: (0, next_i(i), j))
    wsp = pl.BlockSpec((2, FFN_CONV_WIDTH, FFN_TN), lambda j, i: (0, 0, j))
    bsp = pl.BlockSpec((2, 1, FFN_TN), lambda j, i: (0, 0, j))

    def body(u_ref, up_ref, un_ref, da_ref, dan_ref, w_ref, b_ref, du_ref, dw_ref, db_ref, su_ref, sd_ref):
        i = pl.program_id(1)

        @pl.when(i == 0)
        def _():
            dw_ref[...] = jnp.zeros_like(dw_ref)
            db_ref[...] = jnp.zeros_like(db_ref)

        first, last = i == 0, i == nt - 1
        for p in range(2):
            su_ref[p, 0:FFN_HALO, :] = jnp.where(first, 0.0, up_ref[p].astype(F32))
            su_ref[p, FFN_HALO:FFN_HALO + tm, :] = u_ref[p].astype(F32)
            su_ref[p, FFN_HALO + tm:, :] = jnp.where(last, 0.0, un_ref[p].astype(F32))
        za = _ffn_conv(su_ref, w_ref, b_ref, 0, rows)
        zb = _ffn_conv(su_ref, w_ref, b_ref, 1, rows)
        dact = jnp.concatenate([da_ref[...].astype(F32), jnp.where(last, 0.0, dan_ref[...].astype(F32))], axis=0)
        sg = _sigmoid(za)
        sd_ref[0] = dact * zb * (sg * (1.0 + za * (1.0 - sg)))
        sd_ref[1] = dact * (za * sg)
        for p in range(2):
            dz = sd_ref[p, 0:tm, :]
            db_ref[p] += _colsum(dz)
            for k in range(FFN_CONV_WIDTH):
                dw_ref[p, k:k + 1, :] += _colsum(dz * su_ref[p, pl.ds(FFN_HALO - (FFN_CONV_WIDTH - 1) + k, tm), :])
            acc = w_ref[p, FFN_CONV_WIDTH - 1:FFN_CONV_WIDTH, :] * dz
            for k in range(FFN_CONV_WIDTH - 1):
                acc = acc + w_ref[p, k:k + 1, :] * sd_ref[p, pl.ds(FFN_CONV_WIDTH - 1 - k, tm), :]
            du_ref[p] = acc.astype(BF16)

    return pl.pallas_call(
        body, name=name, grid=(FFN_DIM // FFN_TN, nt),
        in_specs=[main, prev, nxt, pl.BlockSpec((tm, FFN_TN), lambda j, i: (i, j)),
                  pl.BlockSpec((FFN_HALO, FFN_TN), lambda j, i: (next_i(i), j)), wsp, bsp],
        out_specs=[main, wsp, bsp],
        out_shape=[jax.ShapeDtypeStruct((2, t, FFN_DIM), BF16), jax.ShapeDtypeStruct((2, FFN_CONV_WIDTH, FFN_DIM), F32),
                   jax.ShapeDtypeStruct((2, 1, FFN_DIM), F32)],
        scratch_shapes=[pltpu.VMEM((2, tm + 2 * FFN_HALO, FFN_TN), F32), pltpu.VMEM((2, rows, FFN_TN), F32)],
        compiler_params=_params("parallel", "arbitrary"),
    )(u, u, u, dact, dact, dw_w, dw_b)


def _conv_transpose(dz, w, width, halo_rows, tn, name):
    t, n = dz.shape
    tm = _tile(t, (256, 128))
    hb = tm // halo_rows
    last_halo = t // halo_rows - 1
    nt = t // tm
    main = pl.BlockSpec((tm, tn), lambda i, j: (i, j))
    nxt = pl.BlockSpec((halo_rows, tn), lambda i, j: (jnp.minimum((i + 1) * hb, last_halo), j))

    def body(dz_ref, dzn_ref, w_ref, du_ref, scr_ref):
        scr_ref[0:tm, :] = dz_ref[...].astype(F32)
        scr_ref[tm:, :] = jnp.where(pl.program_id(0) == nt - 1, 0.0, dzn_ref[...].astype(F32))
        acc = w_ref[width - 1:width, :] * scr_ref[pl.ds(0, tm), :]
        for k in range(width - 1):
            acc = acc + w_ref[k:k + 1, :] * scr_ref[pl.ds(width - 1 - k, tm), :]
        du_ref[...] = acc.astype(du_ref.dtype)

    return pl.pallas_call(
        body, name=name, grid=(nt, n // tn),
        in_specs=[main, nxt, pl.BlockSpec((width, tn), lambda i, j: (0, j))], out_specs=main,
        out_shape=jax.ShapeDtypeStruct((t, n), BF16 if dz.dtype == BF16 else F32),
        scratch_shapes=[pltpu.VMEM((tm + halo_rows, tn), F32)],
        compiler_params=_params("parallel", "parallel"),
    )(dz, dz, w)


CONV_TM = 256


def _glu_window(pa_ref, pah_ref, pg_ref, pgh_ref, scr_ref, first):
    ah, gh = pah_ref[...].astype(F32), pgh_ref[...].astype(F32)
    scr_ref[0:CONV_HALO, :] = jnp.where(first, 0.0, ah * _sigmoid(gh))
    scr_ref[CONV_HALO:, :] = pa_ref[...].astype(F32) * _sigmoid(pg_ref[...].astype(F32))


def _dw_conv31(scr_ref, w_ref, b_ref, tm):
    acc = b_ref[...] + w_ref[CONV_WIDTH - 1:CONV_WIDTH, :] * scr_ref[pl.ds(CONV_HALO, tm), :]
    for k in range(CONV_WIDTH - 1):
        acc = acc + w_ref[k:k + 1, :] * scr_ref[pl.ds(CONV_HALO - (CONV_WIDTH - 1) + k, tm), :]
    return acc


def _conformer_specs(t):
    tm = _tile(t, (CONV_TM, 128))
    hb = tm // CONV_HALO
    d = D_MODEL
    main = lambda c: pl.BlockSpec((tm, d), lambda i: (i, c))
    halo = lambda c: pl.BlockSpec((CONV_HALO, d), lambda i: (jnp.maximum(i * hb - 1, 0), c))
    row = pl.BlockSpec((1, d), lambda i: (0, 0))
    wsp = pl.BlockSpec((CONV_WIDTH, d), lambda i: (0, 0))
    return tm, main, halo, row, wsp


def _conformer_mid(p, dw_w, dw_b, ln_g, ln_b, name):
    t = p.shape[0]
    tm, main, halo, row, wsp = _conformer_specs(t)

    def body(pa_ref, pah_ref, pg_ref, pgh_ref, w_ref, b_ref, g_ref, lb_ref, o_ref, scr_ref):
        _glu_window(pa_ref, pah_ref, pg_ref, pgh_ref, scr_ref, pl.program_id(0) == 0)
        dc = _dw_conv31(scr_ref, w_ref, b_ref, tm)
        mu = jnp.mean(dc, axis=-1, keepdims=True)
        xc = dc - mu
        ln = xc * lax.rsqrt(jnp.mean(xc * xc, axis=-1, keepdims=True) + EPS) * g_ref[...] + lb_ref[...]
        o_ref[...] = (ln * _sigmoid(ln)).astype(BF16)

    return pl.pallas_call(
        body, name=name, grid=(t // tm,), in_specs=[main(0), halo(0), main(1), halo(1), wsp, row, row, row],
        out_specs=main(0), out_shape=jax.ShapeDtypeStruct((t, D_MODEL), BF16),
        scratch_shapes=[pltpu.VMEM((tm + CONV_HALO, D_MODEL), F32)], compiler_params=_params("parallel"),
    )(p, p, p, p, dw_w, dw_b, ln_g, ln_b)


def _conformer_mid_bwd(p, ds, dw_w, dw_b, ln_g, ln_b, name):
    t = p.shape[0]
    tm, main, halo, row, wsp = _conformer_specs(t)

    def body(pa_ref, pah_ref, pg_ref, pgh_ref, ds_ref, w_ref, b_ref, g_ref, lb_ref,
             ddc_ref, dw_ref, db_ref, dg_ref, dlb_ref, scr_ref):
        @pl.when(pl.program_id(0) == 0)
        def _():
            for r_ in (dw_ref, db_ref, dg_ref, dlb_ref):
                r_[...] = jnp.zeros_like(r_)

        _glu_window(pa_ref, pah_ref, pg_ref, pgh_ref, scr_ref, pl.program_id(0) == 0)
        dc = _dw_conv31(scr_ref, w_ref, b_ref, tm)
        mu = jnp.mean(dc, axis=-1, keepdims=True)
        xc = dc - mu
        rstd = lax.rsqrt(jnp.mean(xc * xc, axis=-1, keepdims=True) + EPS)
        xhat = xc * rstd
        ln = xhat * g_ref[...] + lb_ref[...]
        sg = _sigmoid(ln)
        dln = ds_ref[...].astype(F32) * (sg * (1.0 + ln * (1.0 - sg)))
        dg_ref[...] += _colsum(dln * xhat)
        dlb_ref[...] += _colsum(dln)
        dxh = dln * g_ref[...]
        ddc = rstd * (dxh - jnp.mean(dxh, axis=-1, keepdims=True) - xhat * jnp.mean(dxh * xhat, axis=-1, keepdims=True))
        ddc_ref[...] = ddc
        db_ref[...] += _colsum(ddc)
        for k in range(CONV_WIDTH):
            dw_ref[k:k + 1, :] += _colsum(ddc * scr_ref[pl.ds(CONV_HALO - (CONV_WIDTH - 1) + k, tm), :])

    return pl.pallas_call(
        body, name=name, grid=(t // tm,), in_specs=[main(0), halo(0), main(1), halo(1), main(0), wsp, row, row, row],
        out_specs=[main(0), wsp, row, row, row],
        out_shape=[jax.ShapeDtypeStruct((t, D_MODEL), F32), jax.ShapeDtypeStruct((CONV_WIDTH, D_MODEL), F32)]
        + [jax.ShapeDtypeStruct((1, D_MODEL), F32)] * 3,
        scratch_shapes=[pltpu.VMEM((tm + CONV_HALO, D_MODEL), F32)], compiler_params=_params("arbitrary"),
    )(p, p, p, p, ds, dw_w, dw_b, ln_g, ln_b)


def _glu_bwd(p, dglu, name):
    t = p.shape[0]
    d = D_MODEL
    tm = _tile(t)
    col = lambda c: pl.BlockSpec((tm, d), lambda i: (i, c))
    wide = pl.BlockSpec((tm, 2 * d), lambda i: (i, 0))
    row = pl.BlockSpec((1, 2 * d), lambda i: (0, 0))

    def body(pa_ref, pg_ref, dglu_ref, dp_ref, db_ref):
        @pl.when(pl.program_id(0) == 0)
        def _():
            db_ref[...] = jnp.zeros_like(db_ref)

        a, g, dglu = pa_ref[...].astype(F32), pg_ref[...].astype(F32), dglu_ref[...].astype(F32)
        sg = _sigmoid(g)
        da = (dglu * sg).astype(BF16)
        dg = (dglu * a * sg * (1.0 - sg)).astype(BF16)
        dp_ref[:, 0:d] = da
        dp_ref[:, d:2 * d] = dg
        db_ref[:, 0:d] += _colsum(da.astype(F32))
        db_ref[:, d:2 * d] += _colsum(dg.astype(F32))

    return pl.pallas_call(
        body, name=name, grid=(t // tm,), in_specs=[col(0), col(1), col(0)], out_specs=[wide, row],
        out_shape=[jax.ShapeDtypeStruct((t, 2 * d), BF16), jax.ShapeDtypeStruct((1, 2 * d), F32)],
        compiler_params=_params("arbitrary"),
    )(p, p, dglu)


def _colsum_call(a, name):
    t, n = a.shape
    tm = _tile(t)

    def body(a_ref, o_ref):
        @pl.when(pl.program_id(0) == 0)
        def _():
            o_ref[...] = jnp.zeros_like(o_ref)

        o_ref[...] += _colsum(a_ref[...].astype(F32))

    return pl.pallas_call(
        body, name=name, grid=(t // tm,), in_specs=[pl.BlockSpec((tm, n), lambda i: (i, 0))],
        out_specs=pl.BlockSpec((1, n), lambda i: (0, 0)), out_shape=jax.ShapeDtypeStruct((1, n), F32),
        compiler_params=_params("arbitrary"),
    )(a)


def _ada_fwd(c_all, w, name):
    rows, d = c_all.shape
    n = w.shape[1]
    tn = _tile(n, (256, 128))

    def body(c_ref, w_ref, o_ref):
        c = c_ref[...]
        o_ref[...] = _dot((c * _sigmoid(c)).astype(BF16), w_ref[...].astype(BF16), _NN)

    return pl.pallas_call(
        body, name=name, grid=(n // tn,),
        in_specs=[pl.BlockSpec((rows, d), lambda j: (0, 0)), pl.BlockSpec((d, tn), lambda j: (0, j))],
        out_specs=pl.BlockSpec((rows, tn), lambda j: (0, j)), out_shape=jax.ShapeDtypeStruct((rows, n), F32),
        compiler_params=_params("parallel"),
    )(c_all, w)


def _ada_bwd(c_all, dmod, name):
    rows, d = c_all.shape
    n = dmod.shape[1]
    tn = _tile(n, (256, 128))

    def body(c_ref, g_ref, o_ref):
        c = c_ref[...]
        o_ref[...] = _dot((c * _sigmoid(c)).astype(BF16), g_ref[...].astype(BF16), _TN)

    return pl.pallas_call(
        body, name=name, grid=(n // tn,),
        in_specs=[pl.BlockSpec((rows, d), lambda j: (0, 0)), pl.BlockSpec((rows, tn), lambda j: (0, j))],
        out_specs=pl.BlockSpec((d, tn), lambda j: (0, j)), out_shape=jax.ShapeDtypeStruct((d, n), F32),
        compiler_params=_params("parallel"),
    )(c_all, dmod)


def _sum_slots(a, name):
    s, r, c = a.shape
    tr = _row_tile(r, 256)

    def body(a_ref, o_ref):
        acc = a_ref[0].astype(F32)
        for k in range(1, s):
            acc = acc + a_ref[k].astype(F32)
        o_ref[...] = acc

    return pl.pallas_call(
        body, name=name, grid=(r // tr,), in_specs=[pl.BlockSpec((s, tr, c), lambda i: (0, i, 0))],
        out_specs=pl.BlockSpec((tr, c), lambda i: (i, 0)), out_shape=jax.ShapeDtypeStruct((r, c), F32),
        compiler_params=_params("parallel"),
    )(a)


def _adamw(w, g, m, v, name):
    r, c = w.shape
    tr = _row_tile(r, 256)
    blk = pl.BlockSpec((tr, c), lambda i: (i, 0))
    c1 = 1.0 / (1.0 - ADAM_B1 ** ADAM_STEP)
    c2 = 1.0 / (1.0 - ADAM_B2 ** ADAM_STEP)

    def body(w_ref, g_ref, m_ref, v_ref, d_ref, nm_ref, nv_ref):
        g_ = g_ref[...]
        nm = ADAM_B1 * m_ref[...] + (1.0 - ADAM_B1) * g_
        nv = ADAM_B2 * v_ref[...] + (1.0 - ADAM_B2) * (g_ * g_)
        d_ref[...] = -ADAM_LR * ((nm * c1) / (jnp.sqrt(nv * c2) + ADAM_EPS) + ADAM_WD * w_ref[...])
        nm_ref[...] = nm
        nv_ref[...] = nv

    return pl.pallas_call(
        body, name=name, grid=(r // tr,), in_specs=[blk] * 4, out_specs=[blk] * 3,
        out_shape=[jax.ShapeDtypeStruct((r, c), F32)] * 3, compiler_params=_params("parallel"),
    )(w, g, m, v)


def _mesh_pos():
    return lax.axis_index("x"), lax.axis_index("y"), lax.axis_index("c")


def _all_gather_vmem(x_shard, name):
    m_per, n = x_shard.shape

    def body(x_ref, out_ref, send_sems, recv_sems, local_sem):
        x, y, c = _mesh_pos()
        me, sibling = (x, y, c), (x, y, 1 - c)
        chips = [(1 - x, y), (x, 1 - y), (1 - x, 1 - y)]

        def rows(px, py, pc):
            return out_ref.at[pl.ds((4 * px + 2 * py + pc) * m_per, m_per), :]

        def copy(k, block, to, src=None):
            return pltpu.make_async_remote_copy(
                src_ref=rows(*block) if src is None else src, dst_ref=rows(*block),
                send_sem=send_sems.at[k], recv_sem=recv_sems.at[k], device_id=to, device_id_type=MESH)

        mine = pltpu.make_async_copy(x_ref, rows(*me), local_sem)
        mine.start()
        first = [copy(0, me, sibling, src=x_ref)]
        first += [copy(1 + j, me, (*chip, c), src=x_ref) for j, chip in enumerate(chips)]
        for cp in first:
            cp.start()
        passed = [copy(4 + j, (*chip, c), sibling) for j, chip in enumerate(chips)]
        for j, chip in enumerate(chips):
            copy(1 + j, (*chip, c), me).wait_recv()
            passed[j].start()
        copy(0, sibling, me).wait_recv()
        for j, chip in enumerate(chips):
            copy(4 + j, (*chip, 1 - c), me).wait_recv()
        for cp in first + passed:
            cp.wait_send()
        mine.wait()

    return pl.pallas_call(
        body, name=name, out_shape=jax.ShapeDtypeStruct((N_DEV * m_per, n), x_shard.dtype),
        in_specs=[pl.BlockSpec(memory_space=pltpu.VMEM)], out_specs=pl.BlockSpec(memory_space=pltpu.VMEM),
        scratch_shapes=[pltpu.SemaphoreType.DMA((7,)), pltpu.SemaphoreType.DMA((7,)), pltpu.SemaphoreType.DMA],
    )(x_shard)


def _all_gather_hbm(shards, name):
    n = len(shards)
    out_shape = [jax.ShapeDtypeStruct((N_DEV,) + s.shape, s.dtype) for s in shards]

    def body(*refs):
        x_refs, out_refs = refs[:n], refs[n:2 * n]
        send_sems, recv_sems, local_sems = refs[2 * n:]
        x, y, c = _mesh_pos()
        me, sibling = (x, y, c), (x, y, 1 - c)
        chips = [(1 - x, y), (x, 1 - y), (1 - x, 1 - y)]

        def blk(a, p):
            return out_refs[a].at[4 * p[0] + 2 * p[1] + p[2]]

        def copy(a, k, block, to, src=None):
            return pltpu.make_async_remote_copy(
                src_ref=blk(a, block) if src is None else src, dst_ref=blk(a, block),
                send_sem=send_sems.at[7 * a + k], recv_sem=recv_sems.at[7 * a + k], device_id=to, device_id_type=MESH)

        mine = [pltpu.make_async_copy(x_refs[a], blk(a, me), local_sems.at[a]) for a in range(n)]
        for cp in mine:
            cp.start()
        first = []
        for a in range(n):
            first.append(copy(a, 0, me, sibling, src=x_refs[a]))
            first += [copy(a, 1 + j, me, (*chip, c), src=x_refs[a]) for j, chip in enumerate(chips)]
        for cp in first:
            cp.start()
        passed = []
        for j, chip in enumerate(chips):
            for a in range(n):
                copy(a, 1 + j, (*chip, c), me).wait_recv()
                fwd = copy(a, 4 + j, (*chip, c), sibling)
                fwd.start()
                passed.append(fwd)
        for a in range(n):
            copy(a, 0, sibling, me).wait_recv()
            for j, chip in enumerate(chips):
                copy(a, 4 + j, (*chip, 1 - c), me).wait_recv()
        for cp in first + passed:
            cp.wait_send()
        for cp in mine:
            cp.wait()

    any_spec = pl.BlockSpec(memory_space=pl.ANY)
    return pl.pallas_call(
        body, name=name, out_shape=out_shape, in_specs=[any_spec] * n, out_specs=[any_spec] * n,
        scratch_shapes=[pltpu.SemaphoreType.DMA((7 * n,)), pltpu.SemaphoreType.DMA((7 * n,)), pltpu.SemaphoreType.DMA((n,))],
    )(*shards)


def _exchange_halves(grads, name):
    n = len(grads)
    out_shape = [jax.ShapeDtypeStruct((4,) + g.shape[2:], g.dtype) for g in grads]

    def body(*refs):
        g_refs, recv_refs = refs[:n], refs[n:2 * n]
        send_sems, recv_sems = refs[2 * n:]
        x, y, c = _mesh_pos()
        copies = [pltpu.make_async_remote_copy(
            src_ref=g_refs[a].at[q, 1 - c], dst_ref=recv_refs[a].at[q], send_sem=send_sems.at[4 * a + q],
            recv_sem=recv_sems.at[4 * a + q], device_id=(x, y, 1 - c), device_id_type=MESH)
            for a in range(n) for q in range(4)]
        for cp in copies:
            cp.start()
        for cp in copies:
            cp.wait()

    any_spec = pl.BlockSpec(memory_space=pl.ANY)
    return pl.pallas_call(
        body, name=name, out_shape=out_shape, in_specs=[any_spec] * n, out_specs=[any_spec] * n,
        scratch_shapes=[pltpu.SemaphoreType.DMA((4 * n,)), pltpu.SemaphoreType.DMA((4 * n,))],
    )(*grads)


def _add_own_half(g, got, c, name):
    _, _, r, w = g.shape
    tr = _row_tile(r, 512)

    def body(c_ref, g_ref, got_ref, o_ref):
        o_ref[...] = (g_ref[...].astype(F32) + got_ref[...].astype(F32)).astype(BF16)

    slot = pl.BlockSpec((None, tr, w), lambda q, i, c_ref: (q, i, 0))
    return pl.pallas_call(
        body, name=name, out_shape=jax.ShapeDtypeStruct((4, r, w), BF16),
        grid_spec=pltpu.PrefetchScalarGridSpec(
            num_scalar_prefetch=1, grid=(4, r // tr),
            in_specs=[pl.BlockSpec((None, None, tr, w), lambda q, i, c_ref: (q, c_ref[0], i, 0)), slot], out_specs=slot),
        compiler_params=_params("parallel", "parallel"),
    )(c, g, got)


def _exchange_chips(parts, name):
    n = len(parts)
    out_shape = [jax.ShapeDtypeStruct(p.shape, p.dtype) for p in parts]

    def body(*refs):
        p_refs, o_refs = refs[:n], refs[n:2 * n]
        send_sems, recv_sems, local_sems = refs[2 * n:]
        x, y, c = _mesh_pos()
        my_q = 2 * x + y
        chips = [(1 - x, y), (x, 1 - y), (1 - x, 1 - y)]

        local = [pltpu.make_async_copy(p_refs[a].at[my_q], o_refs[a].at[my_q], local_sems.at[a]) for a in range(n)]
        remote = []
        for a in range(n):
            for j, chip in enumerate(chips):
                q = 2 * chip[0] + chip[1]
                remote.append(pltpu.make_async_remote_copy(
                    src_ref=p_refs[a].at[q], dst_ref=o_refs[a].at[my_q],
                    send_sem=send_sems.at[3 * a + j], recv_sem=recv_sems.at[3 * a + j],
                    device_id=(*chip, c), device_id_type=MESH))
        for cp in local + remote:
            cp.start()
        for cp in remote:
            cp.wait()
        for cp in local:
            cp.wait()

    any_spec = pl.BlockSpec(memory_space=pl.ANY)
    return pl.pallas_call(
        body, name=name, out_shape=out_shape, in_specs=[any_spec] * n, out_specs=[any_spec] * n,
        scratch_shapes=[pltpu.SemaphoreType.DMA((3 * n,)), pltpu.SemaphoreType.DMA((3 * n,)), pltpu.SemaphoreType.DMA((n,))],
    )(*parts)


def _ffn_forward(x, mod, norm_g, w, tag):
    sh, sc, gate = mod
    h = _modnorm(x, norm_g, sc, sh, f"{tag}_norm")
    u = _ffn_up(h, w["up_t"], f"{tag}_up")
    act = _ffn_act(u, w["dw_w"], w["dw_b"], f"{tag}_act")
    y, x_new = _matmul(act, w["down"], "nn", F32, f"{tag}_down", resid=(x, gate))
    return x_new, (x, h, u, act, y)


def _ffn_backward(dx_new, saved, mod, norm_g, w, tag):
    x, h, u, act, y = saved
    _, sc, gate = mod
    dy, d_gate = _gate_bwd(dx_new, y, gate, f"{tag}_gate_bwd")
    d_down = _matmul_tn_acc(act, dy, f"{tag}_down_dw")
    dact = _matmul(dy, w["down"], "nt", BF16, f"{tag}_down_dx")
    du, d_dw_w, d_dw_b = _ffn_act_bwd(u, dact, w["dw_w"], w["dw_b"], f"{tag}_act_bwd")
    d_up_t = _matmul_tn_acc(du, h, f"{tag}_up_dw").reshape(2 * FFN_DIM, -1)
    dh = _ffn_up_dx(du, w["up_t"], f"{tag}_up_dx")
    dx, d_w, d_sh = _modnorm_bwd(x, dh, norm_g, sc, dx_new, f"{tag}_norm_bwd")
    return dx, dict(up_t=d_up_t, down=d_down, dw_w=d_dw_w.transpose(1, 0, 2).reshape(FFN_CONV_WIDTH, 2 * FFN_DIM),
                    dw_b=d_dw_b.reshape(1, 2 * FFN_DIM), norm_g=d_w * (1.0 + sc), sh=d_sh, sc=d_w * norm_g, gate=d_gate)


def _mixer_forward(x, mod, norm_g, w, rope, tag):
    sh, sc, gate = mod
    h = _modnorm(x, norm_g, sc, sh, f"{tag}_norm")
    z = _matmul(h, w["w_in_t"], "nt", BF16, f"{tag}_in")
    ya = _gmlp_fwd(z, w["gain"], w["wtril"], w["bias_exp"], f"{tag}_gmlp")
    q, k, v = _qk_prep(z, rope[0], rope[1], w["gq"], w["gk"], w["seg"], f"{tag}_qk")
    outs, lses = [], []
    for _, dil in PATTERNS:
        o, l = _attn_fwd(q, k, v, dil, f"{tag}_attn_d{dil}")
        outs.append(o)
        lses.append(l)
    yb, lse = _attn_merge(outs, lses, f"{tag}_merge")
    cat = jnp.concatenate([ya, yb], axis=1)
    y, x_new = _matmul(cat, w["w_out"], "nn", F32, f"{tag}_out", resid=(x, gate))
    return x_new, (x, h, z, q, k, v, yb, lse, cat, y)


def _mixer_backward(dx_new, saved, mod, norm_g, w, rope, tag):
    x, h, z, q, k, v, yb, lse, cat, y = saved
    _, sc, gate = mod
    dy, d_gate = _gate_bwd(dx_new, y, gate, f"{tag}_gate_bwd")
    d_w_out = _matmul_tn_acc(cat, dy, f"{tag}_out_dw")
    dcat = _matmul(dy, w["w_out"], "nt", BF16, f"{tag}_out_dx")
    dz_a, d_sp_w, d_gain, d_bias_exp = _gmlp_bwd(z, dcat, w["gain"], w["wtril"], w["wtril_t"], w["bias_exp"], f"{tag}_gmlp_bwd")
    dyb = dcat[:, A_WIDTH:]
    dqs, dks, dvs = [], [], []
    for _, dil in PATTERNS:
        dqs.append(_attn_bwd_q(q, k, v, dyb, yb, lse, dil, f"{tag}_attn_dq_d{dil}"))
        dk, dv = _attn_bwd_kv(q, k, v, dyb, yb, lse, dil, f"{tag}_attn_dkv_d{dil}")
        dks.append(dk)
        dvs.append(dv)
    dz_qkv, d_gq, d_gk = _qk_prep_bwd(z, dqs, dks, dvs, rope[0], rope[1], w["gq"], w["gk"], w["seg"], f"{tag}_qk_bwd")
    dz = jnp.concatenate([dz_a, dz_qkv], axis=1)
    d_w_in_t = _matmul_tn_acc(dz, h, f"{tag}_in_dw")
    dh = _matmul(dz, w["w_in_t"], "nn", F32, f"{tag}_in_dx")
    dx, d_w, d_sh = _modnorm_bwd(x, dh, norm_g, sc, dx_new, f"{tag}_norm_bwd")
    return dx, dict(
        w_in_t=d_w_in_t, w_out=d_w_out, vnorm_g=d_gain.reshape(A_GROUPS, GROUP_DIM), spatial_w=d_sp_w,
        spatial_b=d_bias_exp.reshape(CHUNK, A_GROUPS, GROUP_DIM).sum(-1).T,
        q_norm_g=d_gq.reshape(HEADS, HEAD_DIM).sum(0), k_norm_g=d_gk.reshape(HEADS, HEAD_DIM).sum(0),
        norm_g=d_w * (1.0 + sc), sh=d_sh, sc=d_w * norm_g, gate=d_gate)


def _conformer_forward(x, mod, norm_g, w, tag):
    sh, sc, gate = mod
    h = _modnorm(x, norm_g, sc, sh, f"{tag}_norm")
    p = _matmul(h, w["pw1_t"], "nt", BF16, f"{tag}_pw1", bias=w["pw1_b"])
    s = _conformer_mid(p, w["dw_w"], w["dw_b"], w["ln_g"], w["ln_b"], f"{tag}_mid")
    y, x_new = _matmul(s, w["pw2"], "nn", F32, f"{tag}_pw2", bias=w["pw2_b"], resid=(x, gate))
    return x_new, (x, h, p, s, y)


def _conformer_backward(dx_new, saved, mod, norm_g, w, tag):
    x, h, p, s, y = saved
    _, sc, gate = mod
    dy, d_gate = _gate_bwd(dx_new, y, gate, f"{tag}_gate_bwd")
    d_pw2 = _matmul_tn_acc(s, dy, f"{tag}_pw2_dw")
    d_pw2_b = _colsum_call(dy, f"{tag}_pw2_db")
    ds = _matmul(dy, w["pw2"], "nt", BF16, f"{tag}_pw2_dx")
    ddc, d_dw_w, d_dw_b, d_ln_g, d_ln_b = _conformer_mid_bwd(p, ds, w["dw_w"], w["dw_b"], w["ln_g"], w["ln_b"], f"{tag}_mid_bwd")
    dglu = _conv_transpose(ddc, w["dw_w"], CONV_WIDTH, CONV_HALO, 512, f"{tag}_conv_bwd")
    dp, d_pw1_b = _glu_bwd(p, dglu, f"{tag}_glu_bwd")
    d_pw1_t = _matmul_tn_acc(dp, h, f"{tag}_pw1_dw")
    dh = _matmul(dp, w["pw1_t"], "nn", F32, f"{tag}_pw1_dx")
    dx, d_w, d_sh = _modnorm_bwd(x, dh, norm_g, sc, dx_new, f"{tag}_norm_bwd")
    return dx, dict(pw1_t=d_pw1_t, pw1_b=d_pw1_b, dw_w=d_dw_w, dw_b=d_dw_b, ln_g=d_ln_g, ln_b=d_ln_b, pw2=d_pw2,
                    pw2_b=d_pw2_b, norm_g=d_w * (1.0 + sc), sh=d_sh, sc=d_w * norm_g, gate=d_gate)


def _local_step(x, target, pos, mod, norm_mix_g, norm_ffn_g, mixer_w, conv_w, ffn_w):
    d = D_MODEL
    inv_freq = 1.0 / (ROPE_THETA ** (jnp.arange(0, HEAD_DIM, 2, dtype=F32) / HEAD_DIM))
    inv_freq = jnp.tile(inv_freq, 2 * HEADS)[None, :]
    sign = jnp.tile(jnp.concatenate([-jnp.ones(HEAD_DIM // 2, F32), jnp.ones(HEAD_DIM // 2, F32)]), HEADS)[None, :]
    rope = _rope_tables(pos, inv_freq, sign, "rope_tables")
    mods = [[mod[l:l + 1, i * d:(i + 1) * d] for i in range(6)] for l in range(2)]
    mix = [(m[0], m[1], m[2]) for m in mods]
    ffn = [(m[3], m[4], m[5]) for m in mods]
    gm = [norm_mix_g[l:l + 1] for l in range(2)]
    gf = [norm_ffn_g[l:l + 1] for l in range(2)]

    x1, s_mix = _mixer_forward(x, mix[0], gm[0], mixer_w, rope, "l0_mix")
    x2, s_ffn0 = _ffn_forward(x1, ffn[0], gf[0], ffn_w[0], "l0_ffn")
    x3, s_conv = _conformer_forward(x2, mix[1], gm[1], conv_w, "l1_conv")
    x4, s_ffn1 = _ffn_forward(x3, ffn[1], gf[1], ffn_w[1], "l1_ffn")
    dx, loss = _loss_head(x4, target, "loss_head")
    dx, g_ffn1 = _ffn_backward(dx, s_ffn1, ffn[1], gf[1], ffn_w[1], "l1_ffn")
    dx, g_conv = _conformer_backward(dx, s_conv, mix[1], gm[1], conv_w, "l1_conv")
    dx, g_ffn0 = _ffn_backward(dx, s_ffn0, ffn[0], gf[0], ffn_w[0], "l0_ffn")
    dx, g_mix = _mixer_backward(dx, s_mix, mix[0], gm[0], mixer_w, rope, "l0_mix")
    blocks = [g_mix, g_ffn0, g_conv, g_ffn1]
    dmod = jnp.stack([jnp.concatenate([a["sh"], a["sc"], a["gate"], b["sh"], b["sc"], b["gate"]], axis=1)[0]
                      for a, b in ((g_mix, g_ffn0), (g_conv, g_ffn1))])
    return loss, dx, dmod, blocks


def _pack(arrs, rows=None):
    flat = jnp.concatenate([a.reshape(-1).astype(F32) for a in arrs])
    n = flat.shape[0]
    if rows is None:
        cols = 1024
        rows = -(-n // (8 * cols)) * 8
    else:
        cols = -(-n // (rows * 128)) * 128
    return jnp.pad(flat, (0, rows * cols - n)).reshape(rows, cols)


def _unpack(flat, shapes):
    out, off = [], 0
    for shp in shapes:
        n = math.prod(shp)
        out.append(flat[off:off + n].reshape(shp))
        off += n
    return out


def _take_block(a, idx, size, axis):
    return lax.dynamic_slice_in_dim(a, idx * size, size, axis)


def kernel(x, c, positions, ada_w, ada_b, norm_mix_g, norm_ffn_g, ab_w_in, a_vnorm_g, a_spatial_w, a_spatial_b, b_q_norm_g, b_k_norm_g, ab_w_out, conv_pw1_w, conv_pw1_b, conv_dw_w, conv_dw_b, conv_ln_g, conv_ln_b, conv_pw2_w, conv_pw2_b, ffn_up_w, ffn_dw_w, ffn_dw_b, ffn_down_w, loss_target, m_ada_w, m_ada_b, m_norm_mix_g, m_norm_ffn_g, m_ab_w_in, m_a_vnorm_g, m_a_spatial_w, m_a_spatial_b, m_b_q_norm_g, m_b_k_norm_g, m_ab_w_out, m_conv_pw1_w, m_conv_pw1_b, m_conv_dw_w, m_conv_dw_b, m_conv_ln_g, m_conv_ln_b, m_conv_pw2_w, m_conv_pw2_b, m_ffn_up_w, m_ffn_dw_w, m_ffn_dw_b, m_ffn_down_w, v_ada_w, v_ada_b, v_norm_mix_g, v_norm_ffn_g, v_ab_w_in, v_a_vnorm_g, v_a_spatial_w, v_a_spatial_b, v_b_q_norm_g, v_b_k_norm_g, v_ab_w_out, v_conv_pw1_w, v_conv_pw1_b, v_conv_dw_w, v_conv_dw_b, v_conv_ln_g, v_conv_ln_b, v_conv_pw2_w, v_conv_pw2_b, v_ffn_up_w, v_ffn_dw_w, v_ffn_dw_b, v_ffn_down_w):
    weights = dict(ada_w=ada_w, ada_b=ada_b, norm_mix_g=norm_mix_g, norm_ffn_g=norm_ffn_g, ab_w_in=ab_w_in, a_vnorm_g=a_vnorm_g, a_spatial_w=a_spatial_w, a_spatial_b=a_spatial_b, b_q_norm_g=b_q_norm_g, b_k_norm_g=b_k_norm_g, ab_w_out=ab_w_out, conv_pw1_w=conv_pw1_w, conv_pw1_b=conv_pw1_b, conv_dw_w=conv_dw_w, conv_dw_b=conv_dw_b, conv_ln_g=conv_ln_g, conv_ln_b=conv_ln_b, conv_pw2_w=conv_pw2_w, conv_pw2_b=conv_pw2_b, ffn_up_w=ffn_up_w, ffn_dw_w=ffn_dw_w, ffn_dw_b=ffn_dw_b, ffn_down_w=ffn_down_w)
    mom1 = dict(ada_w=m_ada_w, ada_b=m_ada_b, norm_mix_g=m_norm_mix_g, norm_ffn_g=m_norm_ffn_g, ab_w_in=m_ab_w_in, a_vnorm_g=m_a_vnorm_g, a_spatial_w=m_a_spatial_w, a_spatial_b=m_a_spatial_b, b_q_norm_g=m_b_q_norm_g, b_k_norm_g=m_b_k_norm_g, ab_w_out=m_ab_w_out, conv_pw1_w=m_conv_pw1_w, conv_pw1_b=m_conv_pw1_b, conv_dw_w=m_conv_dw_w, conv_dw_b=m_conv_dw_b, conv_ln_g=m_conv_ln_g, conv_ln_b=m_conv_ln_b, conv_pw2_w=m_conv_pw2_w, conv_pw2_b=m_conv_pw2_b, ffn_up_w=m_ffn_up_w, ffn_dw_w=m_ffn_dw_w, ffn_dw_b=m_ffn_dw_b, ffn_down_w=m_ffn_down_w)
    mom2 = dict(ada_w=v_ada_w, ada_b=v_ada_b, norm_mix_g=v_norm_mix_g, norm_ffn_g=v_norm_ffn_g, ab_w_in=v_ab_w_in, a_vnorm_g=v_a_vnorm_g, a_spatial_w=v_a_spatial_w, a_spatial_b=v_a_spatial_b, b_q_norm_g=v_b_q_norm_g, b_k_norm_g=v_b_k_norm_g, ab_w_out=v_ab_w_out, conv_pw1_w=v_conv_pw1_w, conv_pw1_b=v_conv_pw1_b, conv_dw_w=v_conv_dw_w, conv_dw_b=v_conv_dw_b, conv_ln_g=v_conv_ln_g, conv_ln_b=v_conv_ln_b, conv_pw2_w=v_conv_pw2_w, conv_pw2_b=v_conv_pw2_b, ffn_up_w=v_ffn_up_w, ffn_dw_w=v_ffn_dw_w, ffn_dw_b=v_ffn_dw_b, ffn_down_w=v_ffn_down_w)
    order = list(weights)
    d, f2 = D_MODEL, 2 * FFN_DIM
    t = x.shape[1]
    me = 4 * lax.axis_index("x") + 2 * lax.axis_index("y") + lax.axis_index("c")
    for window, dil in PATTERNS:
        assert window // dil == Q_BLOCK and t % (dil * Q_BLOCK) == 0

    small_in = [c[0], conv_pw1_b[0], conv_dw_w[0], conv_dw_b[0], conv_ln_g[0], conv_ln_b[0], conv_pw2_b[0], ffn_dw_w]
    g1 = _all_gather_vmem(_pack(small_in, rows=8), "gather_small").reshape(N_DEV, -1)
    c_all, pw1_b, dw_w, dw_b, ln_g, ln_b, pw2_b, fdw_w = [
        jnp.stack(parts) for parts in zip(*[_unpack(g1[k], [a.shape for a in small_in]) for k in range(N_DEV)])]
    pw1_b, dw_b, ln_g, ln_b, pw2_b = [a.reshape(1, -1) for a in (pw1_b, dw_b, ln_g, ln_b, pw2_b)]
    dw_w = dw_w.transpose(1, 0, 2).reshape(CONV_WIDTH, d)
    fdw_w = fdw_w.transpose(1, 2, 0, 3).reshape(2, FFN_CONV_WIDTH, f2)

    shards = [ab_w_in[0].T, ab_w_out[0], conv_pw1_w[0].T, conv_pw2_w[0], ffn_up_w[0].T, ffn_up_w[1].T, ffn_down_w[0], ffn_down_w[1]]
    full = _all_gather_hbm([s.astype(BF16) for s in shards], "gather_weights")
    w_in_t, w_out, pw1_t, pw2, up_t0, up_t1, down0, down1 = [a.reshape(-1, d) for a in full]

    c16 = jnp.pad(c_all, ((0, 2 * N_DEV - c_all.shape[0]), (0, 0)))
    part = jnp.concatenate([_ada_fwd(c16, ada_w[l], f"ada_fwd{l}")[:N_DEV] for l in range(2)], axis=1)
    g2 = _all_gather_vmem(part, "gather_mod").reshape(N_DEV, N_DEV, 2, -1)
    mod = lax.dynamic_index_in_dim(g2, me, axis=1, keepdims=False).transpose(1, 0, 2).reshape(2, 6 * d) + ada_b

    causal = jnp.tril(jnp.ones((CHUNK, CHUNK), bool))
    wtril = jnp.where(causal[None], a_spatial_w[0], 0.0)
    mixer_w = dict(
        w_in_t=w_in_t, w_out=w_out, gain=a_vnorm_g[0].reshape(1, A_WIDTH), wtril=wtril.astype(BF16),
        wtril_t=wtril.transpose(0, 2, 1).astype(BF16),
        bias_exp=jnp.repeat(a_spatial_b[0].T, GROUP_DIM, axis=1),
        gq=jnp.tile(b_q_norm_g[0], HEADS)[None, :], gk=jnp.tile(b_k_norm_g[0], HEADS)[None, :],
        seg=jnp.kron(jnp.eye(HEADS, dtype=F32), jnp.ones((HEAD_DIM, HEAD_DIM), F32)))
    conv_w = dict(pw1_t=pw1_t, pw1_b=pw1_b, dw_w=dw_w, dw_b=dw_b, ln_g=ln_g, ln_b=ln_b, pw2=pw2, pw2_b=pw2_b)
    ffn_w = [dict(up_t=u_, down=dn, dw_w=fdw_w[l].reshape(FFN_CONV_WIDTH, 2, FFN_DIM).transpose(1, 0, 2),
                  dw_b=ffn_dw_b[l].reshape(2, 1, FFN_DIM)) for l, (u_, dn) in enumerate(((up_t0, down0), (up_t1, down1)))]

    loss, dx, dmod, (g_mix, g_ffn0, g_conv, g_ffn1) = _local_step(
        x[0], loss_target[0], positions[0].astype(F32)[:, None], mod, norm_mix_g, norm_ffn_g, mixer_w, conv_w, ffn_w)

    big = [g_mix["w_in_t"], g_mix["w_out"], g_conv["pw1_t"], g_conv["pw2"], g_ffn0["up_t"], g_ffn1["up_t"], g_ffn0["down"], g_ffn1["down"]]
    big = [g.reshape(4, 2, g.shape[0] // N_DEV, d) for g in big]
    got = _exchange_halves(big, "reduce_cores")
    my_core = lax.axis_index("c").astype(jnp.int32).reshape(1)
    parts = [_add_own_half(g, r, my_core, f"reduce_cores_add{i}") for i, (g, r) in enumerate(zip(big, got))]
    slots = _exchange_chips(parts, "reduce_chips")
    r_in_t, r_out, r_pw1_t, r_pw2, r_up_t0, r_up_t1, r_down0, r_down1 = [
        _sum_slots(s, f"reduce_chips_sum{i}") for i, s in enumerate(slots)]

    small_g = [
        dmod, jnp.concatenate([g_mix["norm_g"], g_conv["norm_g"]]), jnp.concatenate([g_ffn0["norm_g"], g_ffn1["norm_g"]]),
        g_mix["vnorm_g"], g_mix["spatial_w"], g_mix["spatial_b"], g_mix["q_norm_g"], g_mix["k_norm_g"],
        g_conv["pw1_b"], g_conv["dw_w"], g_conv["dw_b"], g_conv["ln_g"], g_conv["ln_b"], g_conv["pw2_b"],
        jnp.stack([g_ffn0["dw_w"], g_ffn1["dw_w"]]), jnp.concatenate([g_ffn0["dw_b"], g_ffn1["dw_b"]])]
    packed = _pack(small_g, rows=8)
    g3 = _all_gather_vmem(packed, "gather_small_grads").reshape(N_DEV, 8, -1)
    total = _unpack(_sum_slots(g3, "sum_small_grads").reshape(-1), [a.shape for a in small_g])
    (s_dmod, s_mix_g, s_ffn_g, s_vnorm, s_sp_w, s_sp_b, s_gq, s_gk, s_pw1_b, s_dw_w, s_dw_b, s_ln_g, s_ln_b,
     s_pw2_b, s_fdw_w, s_fdw_b) = total
    dmod_all = g3.reshape(N_DEV, -1)[:, :2 * 6 * d].reshape(N_DEV, 2, 6 * d)
    n_ada = ada_w.shape[2]
    dmod16 = jnp.pad(_take_block(dmod_all, me, n_ada, 2), ((0, N_DEV), (0, 0), (0, 0)))
    g_ada_w = jnp.stack([_ada_bwd(c16, dmod16[:, l], f"ada_bwd{l}") for l in range(2)])

    grads = dict(
        ada_w=g_ada_w, ada_b=s_dmod, norm_mix_g=s_mix_g, norm_ffn_g=s_ffn_g, ab_w_in=r_in_t.T[None],
        a_vnorm_g=s_vnorm[None], a_spatial_w=s_sp_w[None], a_spatial_b=s_sp_b[None], b_q_norm_g=s_gq[None],
        b_k_norm_g=s_gk[None], ab_w_out=r_out[None], conv_pw1_w=r_pw1_t.T[None],
        conv_pw1_b=_take_block(s_pw1_b, me, conv_pw1_b.shape[1], 1),
        conv_dw_w=_take_block(s_dw_w, me, conv_dw_w.shape[2], 1)[None],
        conv_dw_b=_take_block(s_dw_b, me, conv_dw_b.shape[1], 1), conv_ln_g=_take_block(s_ln_g, me, conv_ln_g.shape[1], 1),
        conv_ln_b=_take_block(s_ln_b, me, conv_ln_b.shape[1], 1), conv_pw2_w=r_pw2[None],
        conv_pw2_b=_take_block(s_pw2_b, me, conv_pw2_b.shape[1], 1),
        ffn_up_w=jnp.stack([r_up_t0.T, r_up_t1.T]), ffn_dw_w=_take_block(s_fdw_w, me, ffn_dw_w.shape[2], 2),
        ffn_dw_b=s_fdw_b, ffn_down_w=jnp.stack([r_down0, r_down1]))

    large = ("ada_w", "ab_w_in", "ab_w_out", "conv_pw1_w", "conv_pw2_w", "ffn_up_w", "ffn_down_w")
    delta, new_m, new_v = {}, {}, {}
    for name in large:
        shp = weights[name].shape
        two_d = lambda a: a.reshape(-1, shp[-1])
        res = _adamw(two_d(weights[name]), two_d(grads[name]), two_d(mom1[name]), two_d(mom2[name]), f"adamw_{name}")
        delta[name], new_m[name], new_v[name] = [r.reshape(shp) for r in res]
    small = [n for n in order if n not in large]
    shapes = [weights[n].shape for n in small]
    res = _adamw(*[_pack([src[n] for n in small]) for src in (weights, grads, mom1, mom2)], "adamw_small")
    for dst, r in zip((delta, new_m, new_v), res):
        for n, a in zip(small, _unpack(r.reshape(-1), shapes)):
            dst[n] = a

    loss = lax.psum(loss[0, 0], ("x", "y", "c"))
    return (loss, dx[None], *[grads[n] for n in order], *[delta[n] for n in order],
            *[new_m[n] for n in order], *[new_v[n] for n in order])
```

```python
import functools
import math

import jax
import jax.numpy as jnp
from jax import lax
from jax.experimental import pallas as pl
from jax.experimental.pallas import tpu as pltpu

F32 = jnp.float32
BF16 = jnp.bfloat16
MESH = pl.DeviceIdType.MESH

D_MODEL = 1024
A_WIDTH = 512
A_GROUPS = 4
GROUP_DIM = 128
CHUNK = 128
B_WIDTH = 512
HEADS = 8
HEAD_DIM = 64
PATTERNS = ((128, 1), (512, 4), (2048, 16))
Q_BLOCK = 128
ROPE_THETA = 10000.0
AB_IN = 2560
CONV_WIDTH = 31
FFN_DIM = 2816
FFN_CONV_WIDTH = 3
EPS = 1e-6
NEG = -1e30
N_DEV = 8
ADAM_LR, ADAM_B1, ADAM_B2, ADAM_EPS, ADAM_WD, ADAM_STEP = 0.001, 0.9, 0.999, 1e-08, 0.01, 10

V7X_VMEM_LIMIT = 56 * 2**20
BF16_ROWS = 16
FFN_HALO = 16
CONV_HALO = 32

_NN = (((1,), (0,)), ((), ()))
_NT = (((1,), (1,)), ((), ()))
_TN = (((0,), (0,)), ((), ()))


def _tile(n, prefs=(512, 256, 128)):
    for t in prefs:
        if n % t == 0:
            return t
    return n


def _row_tile(n, cap=512):
    best = n
    for t in range(8, min(n, cap) + 1, 8):
        if n % t == 0:
            best = t
    return best if best <= cap else n


def _params(*sem):
    return pltpu.CompilerParams(dimension_semantics=sem, vmem_limit_bytes=V7X_VMEM_LIMIT)


def _dot(a, b, dims):
    return lax.dot_general(a, b, dims, preferred_element_type=F32)


def _sigmoid(x):
    return 1.0 / (1.0 + jnp.exp(-x))


def _gelu(x):
    return 0.5 * x * (1.0 + lax.erf(x * (2.0 ** -0.5)))


def _gelu_grad(x):
    return 0.5 * (1.0 + lax.erf(x * (2.0 ** -0.5))) + x * jnp.exp(-0.5 * x * x) * (1.0 / math.sqrt(2.0 * math.pi))


def _colsum(v):
    return jnp.sum(v, axis=0, keepdims=True)


MATMUL_VMEM_BUDGET = 40 * 2**20


def _matmul_tiles(m, n, k, out_bytes, with_resid):
    def options(dim):
        opts = [t for t in (1024, 512, 256, 128) if dim % t == 0]
        return opts + [dim] if dim <= 4096 and dim not in opts else opts

    best = None
    for tm in options(m):
        for tn in options(n):
            need = 4 * (tm * k + k * tn) + tm * tn * (4 + 2 * out_bytes) + (24 * tm * tn if with_resid else 0)
            if need <= MATMUL_VMEM_BUDGET and (best is None or tm * tn / (tm + tn) > best[0]):
                best = (tm * tn / (tm + tn), tm, tn)
    return best[1], best[2]


def _matmul_tn_acc(a, b, name, tk=512):
    squeeze = a.ndim == 2
    a3 = a[None] if squeeze else a
    p_, t, m = a3.shape
    n = b.shape[1]
    nk = t // tk

    def body(a_ref, b_ref, o_ref, acc_ref):
        kt = pl.program_id(1)

        @pl.when(kt == 0)
        def _():
            acc_ref[...] = jnp.zeros_like(acc_ref)

        acc_ref[...] += _dot(a_ref[...], b_ref[...], _TN)

        @pl.when(kt == nk - 1)
        def _():
            o_ref[...] = acc_ref[...].astype(BF16)

    out = pl.pallas_call(
        body, name=name, grid=(p_, nk),
        in_specs=[pl.BlockSpec((None, tk, m), lambda p, kt: (p, kt, 0)), pl.BlockSpec((tk, n), lambda p, kt: (kt, 0))],
        out_specs=pl.BlockSpec((None, m, n), lambda p, kt: (p, 0, 0)), out_shape=jax.ShapeDtypeStruct((p_, m, n), BF16),
        scratch_shapes=[pltpu.VMEM((m, n), F32)], compiler_params=_params("parallel", "arbitrary"),
    )(a3, b)
    return out[0] if squeeze else out


def _matmul(a, b, mode, out_dtype, name, bias=None, resid=None):
    if mode == "nn":
        (m, k), (_, n) = a.shape, b.shape
    elif mode == "nt":
        (m, k), (n, _) = a.shape, b.shape
    else:
        (k, m), (_, n) = a.shape, b.shape
    tm, tn = _matmul_tiles(m, n, k, jnp.dtype(out_dtype).itemsize, resid is not None)
    dims = {"nn": _NN, "nt": _NT, "tn": _TN}[mode]
    a_spec = pl.BlockSpec((k, tm), lambda i, j: (0, i)) if mode == "tn" else pl.BlockSpec((tm, k), lambda i, j: (i, 0))
    b_spec = pl.BlockSpec((tn, k), lambda i, j: (j, 0)) if mode == "nt" else pl.BlockSpec((k, tn), lambda i, j: (0, j))
    in_specs, args = [a_spec, b_spec], [a, b]
    row_spec = pl.BlockSpec((1, tn), lambda i, j: (0, j))
    tile_spec = pl.BlockSpec((tm, tn), lambda i, j: (i, j))
    if bias is not None:
        in_specs.append(row_spec)
        args.append(bias)
    if resid is not None:
        in_specs += [tile_spec, row_spec]
        args += list(resid)
    out_shape = [jax.ShapeDtypeStruct((m, n), out_dtype)]
    out_specs = [tile_spec]
    if resid is not None:
        out_shape.append(jax.ShapeDtypeStruct((m, n), F32))
        out_specs.append(tile_spec)

    def body(*refs):
        a_ref, b_ref = refs[0], refs[1]
        pos = 2
        acc = _dot(a_ref[...], b_ref[...], dims)
        if bias is not None:
            acc = acc + refs[pos][...]
            pos += 1
        if resid is not None:
            x_ref, g_ref = refs[pos], refs[pos + 1]
            pos += 2
        refs[pos][...] = acc.astype(out_dtype)
        if resid is not None:
            refs[pos + 1][...] = x_ref[...] + g_ref[...] * acc

    outs = pl.pallas_call(
        body, name=name, grid=(m // tm, n // tn), in_specs=in_specs, out_specs=out_specs, out_shape=out_shape,
        compiler_params=_params("parallel", "parallel"),
    )(*args)
    return outs if resid is not None else outs[0]


def _modnorm(x, g, sc, sh, name):
    t, d = x.shape
    tm = _tile(t)
    row = pl.BlockSpec((1, d), lambda i: (0, 0))
    blk = pl.BlockSpec((tm, d), lambda i: (i, 0))

    def body(x_ref, g_ref, sc_ref, sh_ref, o_ref):
        x = x_ref[...]
        r = lax.rsqrt(jnp.mean(x * x, axis=-1, keepdims=True) + EPS)
        o_ref[...] = ((x * r) * g_ref[...] * (1.0 + sc_ref[...]) + sh_ref[...]).astype(BF16)

    return pl.pallas_call(
        body, name=name, grid=(t // tm,), in_specs=[blk, row, row, row], out_specs=blk,
        out_shape=jax.ShapeDtypeStruct((t, d), BF16), compiler_params=_params("parallel"),
    )(x, g, sc, sh)


def _modnorm_bwd(x, dh, g, sc, dres, name):
    t, d = x.shape
    tm = _tile(t)
    row = pl.BlockSpec((1, d), lambda i: (0, 0))
    blk = pl.BlockSpec((tm, d), lambda i: (i, 0))

    def body(x_ref, dh_ref, g_ref, sc_ref, dres_ref, dx_ref, dw_ref, dsh_ref):
        @pl.when(pl.program_id(0) == 0)
        def _():
            dw_ref[...] = jnp.zeros_like(dw_ref)
            dsh_ref[...] = jnp.zeros_like(dsh_ref)

        x = x_ref[...]
        dh = dh_ref[...].astype(F32)
        r = lax.rsqrt(jnp.mean(x * x, axis=-1, keepdims=True) + EPS)
        xn = x * r
        dxn = dh * (g_ref[...] * (1.0 + sc_ref[...]))
        dx_ref[...] = dres_ref[...] + r * (dxn - xn * jnp.mean(dxn * xn, axis=-1, keepdims=True))
        dw_ref[...] += _colsum(dh * xn)
        dsh_ref[...] += _colsum(dh)

    return pl.pallas_call(
        body, name=name, grid=(t // tm,), in_specs=[blk, blk, row, row, blk], out_specs=[blk, row, row],
        out_shape=[jax.ShapeDtypeStruct((t, d), F32), jax.ShapeDtypeStruct((1, d), F32), jax.ShapeDtypeStruct((1, d), F32)],
        compiler_params=_params("arbitrary"),
    )(x, dh, g, sc, dres)


def _gate_bwd(dxn, y, gate, name):
    t, d = dxn.shape
    tm = _tile(t)
    row = pl.BlockSpec((1, d), lambda i: (0, 0))
    blk = pl.BlockSpec((tm, d), lambda i: (i, 0))

    def body(dxn_ref, y_ref, g_ref, dy_ref, dg_ref):
        @pl.when(pl.program_id(0) == 0)
        def _():
            dg_ref[...] = jnp.zeros_like(dg_ref)

        dxn = dxn_ref[...]
        dy_ref[...] = (dxn * g_ref[...]).astype(BF16)
        dg_ref[...] += _colsum(dxn * y_ref[...])

    return pl.pallas_call(
        body, name=name, grid=(t // tm,), in_specs=[blk, blk, row], out_specs=[blk, row],
        out_shape=[jax.ShapeDtypeStruct((t, d), BF16), jax.ShapeDtypeStruct((1, d), F32)],
        compiler_params=_params("arbitrary"),
    )(dxn, y, gate)


def _loss_head(y, target, name):
    t, d = y.shape
    tm = _tile(t)
    blk = pl.BlockSpec((tm, d), lambda i: (i, 0))
    one = pl.BlockSpec((1, 1), lambda i: (0, 0))

    def body(y_ref, t_ref, dy_ref, loss_ref, acc_ref):
        @pl.when(pl.program_id(0) == 0)
        def _():
            acc_ref[...] = jnp.zeros_like(acc_ref)

        e = y_ref[...] - t_ref[...]
        dy_ref[...] = e * (1.0 / d)
        acc_ref[...] += _colsum(e * e)

        @pl.when(pl.program_id(0) == pl.num_programs(0) - 1)
        def _():
            loss_ref[...] = jnp.sum(acc_ref[...], axis=1, keepdims=True) * (0.5 / d)

    return pl.pallas_call(
        body, name=name, grid=(t // tm,), in_specs=[blk, blk], out_specs=[blk, one],
        out_shape=[jax.ShapeDtypeStruct((t, d), F32), jax.ShapeDtypeStruct((1, 1), F32)],
        scratch_shapes=[pltpu.VMEM((1, d), F32)], compiler_params=_params("arbitrary"),
    )(y, target)


def _group_norm(vg, gain):
    mu = jnp.mean(vg, axis=-1, keepdims=True)
    xc = vg - mu
    rstd = lax.rsqrt(jnp.mean(xc * xc, axis=-1, keepdims=True) + EPS)
    xhat = xc * rstd
    return xhat, rstd, xhat * gain


def _gmlp_fwd(z, gain, wtril, bias_exp, name):
    t = z.shape[0]
    zu = pl.BlockSpec((CHUNK, A_WIDTH), lambda i: (i, 0))
    zv = pl.BlockSpec((CHUNK, A_WIDTH), lambda i: (i, 1))
    full2 = lambda shp: pl.BlockSpec(shp, lambda i: (0, 0))
    w_spec = pl.BlockSpec((A_GROUPS, CHUNK, CHUNK), lambda i: (0, 0, 0))

    def body(zu_ref, zv_ref, gain_ref, w_ref, b_ref, ya_ref):
        ua = _gelu(zu_ref[...].astype(F32))
        vg = _gelu(zv_ref[...].astype(F32))
        for g in range(A_GROUPS):
            sl = slice(g * GROUP_DIM, (g + 1) * GROUP_DIM)
            _, _, vn = _group_norm(vg[:, sl], gain_ref[:, sl])
            f = _dot(w_ref[g], vn.astype(BF16), _NN) + b_ref[:, sl]
            ya_ref[:, sl] = (ua[:, sl] * f).astype(BF16)

    return pl.pallas_call(
        body, name=name, grid=(t // CHUNK,),
        in_specs=[zu, zv, full2((1, A_WIDTH)), w_spec, full2((CHUNK, A_WIDTH))], out_specs=zu,
        out_shape=jax.ShapeDtypeStruct((t, A_WIDTH), BF16), compiler_params=_params("parallel"),
    )(z, z, gain, wtril, bias_exp)


def _gmlp_bwd(z, dcat, gain, wtril, wtril_t, bias_exp, name):
    t = z.shape[0]
    zu = pl.BlockSpec((CHUNK, A_WIDTH), lambda i: (i, 0))
    zv = pl.BlockSpec((CHUNK, A_WIDTH), lambda i: (i, 1))
    full2 = lambda shp: pl.BlockSpec(shp, lambda i: (0, 0))
    w_spec = pl.BlockSpec((A_GROUPS, CHUNK, CHUNK), lambda i: (0, 0, 0))
    dz_spec = pl.BlockSpec((CHUNK, 2 * A_WIDTH), lambda i: (i, 0))

    def body(zu_ref, zv_ref, dya_ref, gain_ref, w_ref, wt_ref, b_ref, dz_ref, dw_ref, dgain_ref, dbias_ref):
        @pl.when(pl.program_id(0) == 0)
        def _():
            dw_ref[...] = jnp.zeros_like(dw_ref)
            dgain_ref[...] = jnp.zeros_like(dgain_ref)
            dbias_ref[...] = jnp.zeros_like(dbias_ref)

        zu_v = zu_ref[...].astype(F32)
        zv_v = zv_ref[...].astype(F32)
        dya = dya_ref[...].astype(F32)
        ua = _gelu(zu_v)
        vg = _gelu(zv_v)
        row = lax.broadcasted_iota(jnp.int32, (CHUNK, CHUNK), 0)
        col = lax.broadcasted_iota(jnp.int32, (CHUNK, CHUNK), 1)
        for g in range(A_GROUPS):
            sl = slice(g * GROUP_DIM, (g + 1) * GROUP_DIM)
            gain_g = gain_ref[:, sl]
            xhat, rstd, vn = _group_norm(vg[:, sl], gain_g)
            vn16 = vn.astype(BF16)
            f = _dot(w_ref[g], vn16, _NN) + b_ref[:, sl]
            df = dya[:, sl] * ua[:, sl]
            df16 = df.astype(BF16)
            dz_ref[:, sl] = (dya[:, sl] * f * _gelu_grad(zu_v[:, sl])).astype(BF16)
            dw_ref[g] += jnp.where(row >= col, _dot(df16, vn16, _NT), 0.0)
            dvn = _dot(wt_ref[g], df16, _NN)
            dgain_ref[:, sl] += _colsum(dvn * xhat)
            dxh = dvn * gain_g
            dvg = rstd * (dxh - jnp.mean(dxh, axis=-1, keepdims=True) - xhat * jnp.mean(dxh * xhat, axis=-1, keepdims=True))
            dz_ref[:, A_WIDTH + g * GROUP_DIM:A_WIDTH + (g + 1) * GROUP_DIM] = (dvg * _gelu_grad(zv_v[:, sl])).astype(BF16)
            dbias_ref[:, sl] += df

    return pl.pallas_call(
        body, name=name, grid=(t // CHUNK,),
        in_specs=[zu, zv, zu, full2((1, A_WIDTH)), w_spec, w_spec, full2((CHUNK, A_WIDTH))],
        out_specs=[dz_spec, w_spec, full2((1, A_WIDTH)), full2((CHUNK, A_WIDTH))],
        out_shape=[jax.ShapeDtypeStruct((t, 2 * A_WIDTH), BF16), jax.ShapeDtypeStruct((A_GROUPS, CHUNK, CHUNK), F32),
                   jax.ShapeDtypeStruct((1, A_WIDTH), F32), jax.ShapeDtypeStruct((CHUNK, A_WIDTH), F32)],
        compiler_params=_params("arbitrary"),
    )(z, z, dcat, gain, wtril, wtril_t, bias_exp)


def _rope_tables(pos, inv_freq, sign, name):
    t = pos.shape[0]
    tm = _tile(t)
    row = pl.BlockSpec((1, B_WIDTH), lambda i: (0, 0))
    blk = pl.BlockSpec((tm, B_WIDTH), lambda i: (i, 0))

    def body(pos_ref, f_ref, s_ref, cos_ref, sin_ref):
        ang = pos_ref[...] * f_ref[...]
        cos_ref[...] = jnp.cos(ang)
        sin_ref[...] = jnp.sin(ang) * s_ref[...]

    return pl.pallas_call(
        body, name=name, grid=(t // tm,), in_specs=[pl.BlockSpec((tm, 1), lambda i: (i, 0)), row, row],
        out_specs=[blk, blk], out_shape=[jax.ShapeDtypeStruct((t, B_WIDTH), F32)] * 2,
        compiler_params=_params("parallel"),
    )(pos, inv_freq, sign)


def _head_sum(v, seg):
    return lax.dot_general(v, seg, _NN, precision=lax.Precision.HIGHEST, preferred_element_type=F32)


def _swap_halves(v):
    lane = lax.broadcasted_iota(jnp.int32, v.shape, 1)
    return jnp.where((lane & (HEAD_DIM - 1)) < HEAD_DIM // 2,pltpu.roll(v, B_WIDTH - HEAD_DIM // 2, 1), pltpu.roll(v, HEAD_DIM // 2, 1))


def _qk_prep(z, cos_t, sin_t, gq, gk, seg, name):
    t = z.shape[0]
    tm = _tile(t, (256, 128))
    col = lambda c: pl.BlockSpec((tm, B_WIDTH), lambda i: (i, c))
    row = pl.BlockSpec((1, B_WIDTH), lambda i: (0, 0))
    blk = col(0)

    def body(q_ref, k_ref, v_ref, cos_ref, sin_ref, gq_ref, gk_ref, seg_ref, qo_ref, ko_ref, vo_ref):
        def norm_rot(x, g):
            r = lax.rsqrt(_head_sum(x * x, seg_ref[...]) * (1.0 / HEAD_DIM) + EPS)
            xn = x * r * g
            return xn * cos_ref[...] + _swap_halves(xn) * sin_ref[...]

        qo_ref[...] = norm_rot(q_ref[...].astype(F32), gq_ref[...]).astype(BF16)
        ko_ref[...] = norm_rot(k_ref[...].astype(F32), gk_ref[...]).astype(BF16)
        vo_ref[...] = v_ref[...].astype(BF16)

    return pl.pallas_call(
        body, name=name, grid=(t // tm,),
        in_specs=[col(2), col(3), col(4), blk, blk, row, row, pl.BlockSpec((B_WIDTH, B_WIDTH), lambda i: (0, 0))],
        out_specs=[blk, blk, blk], out_shape=[jax.ShapeDtypeStruct((t, B_WIDTH), BF16)] * 3,
        compiler_params=_params("parallel"),
    )(z, z, z, cos_t, sin_t, gq, gk, seg)


def _qk_prep_bwd(z, dqs, dks, dvs, cos_t, sin_t, gq, gk, seg, name):
    t = z.shape[0]
    tm = _tile(t, (256, 128))
    col = lambda c: pl.BlockSpec((tm, B_WIDTH), lambda i: (i, c))
    row = pl.BlockSpec((1, B_WIDTH), lambda i: (0, 0))
    blk = col(0)
    nb = len(dqs)

    def body(*refs):
        q_ref, k_ref = refs[0], refs[1]
        dq_refs, dk_refs, dv_refs = refs[2:2 + nb], refs[2 + nb:2 + 2 * nb], refs[2 + 2 * nb:2 + 3 * nb]
        cos_ref, sin_ref, gq_ref, gk_ref, seg_ref, dz_ref, dgq_ref, dgk_ref = refs[2 + 3 * nb:]

        @pl.when(pl.program_id(0) == 0)
        def _():
            dgq_ref[...] = jnp.zeros_like(dgq_ref)
            dgk_ref[...] = jnp.zeros_like(dgk_ref)

        def back(x, d_refs, g, dg_ref):
            dout = d_refs[0][...]
            for r_ in d_refs[1:]:
                dout = dout + r_[...]
            dy = dout * cos_ref[...] + _swap_halves(dout * sin_ref[...])
            r = lax.rsqrt(_head_sum(x * x, seg_ref[...]) * (1.0 / HEAD_DIM) + EPS)
            xn = x * r
            dg_ref[...] += _colsum(dy * xn)
            dxn = dy * g
            return r * (dxn - xn * (_head_sum(dxn * xn, seg_ref[...]) * (1.0 / HEAD_DIM)))

        dz_ref[:, 0:B_WIDTH] = back(q_ref[...].astype(F32), dq_refs, gq_ref[...], dgq_ref).astype(BF16)
        dz_ref[:, B_WIDTH:2 * B_WIDTH] = back(k_ref[...].astype(F32), dk_refs, gk_ref[...], dgk_ref).astype(BF16)
        dv = dv_refs[0][...]
        for r_ in dv_refs[1:]:
            dv = dv + r_[...]
        dz_ref[:, 2 * B_WIDTH:3 * B_WIDTH] = dv.astype(BF16)

    return pl.pallas_call(
        body, name=name, grid=(t // tm,),
        in_specs=[col(2), col(3)] + [blk] * (3 * nb) + [blk, blk, row, row, pl.BlockSpec((B_WIDTH, B_WIDTH), lambda i: (0, 0))],
        out_specs=[pl.BlockSpec((tm, 3 * B_WIDTH), lambda i: (i, 0)), row, row],
        out_shape=[jax.ShapeDtypeStruct((t, 3 * B_WIDTH), BF16), jax.ShapeDtypeStruct((1, B_WIDTH), F32),
                   jax.ShapeDtypeStruct((1, B_WIDTH), F32)],
        compiler_params=_params("arbitrary"),
    )(z, z, *dqs, *dks, *dvs, cos_t, sin_t, gq, gk, seg)


def _subseq(a, dil):
    return a.reshape(a.shape[0] // dil, dil * a.shape[1])


def _attn_fwd(q, k, v, dil, name):
    t = q.shape[0]
    nb = t // dil // Q_BLOCK
    cur = pl.BlockSpec((Q_BLOCK, B_WIDTH), lambda r, i: (i, r))
    prev = pl.BlockSpec((Q_BLOCK, B_WIDTH), lambda r, i: (jnp.maximum(i - 1, 0), r))

    def body(q_ref, kp_ref, kc_ref, vp_ref, vc_ref, o_ref, lse_ref):
        i = pl.program_id(1)
        q = q_ref[...]
        kk = jnp.concatenate([kp_ref[...], kc_ref[...]], axis=0)
        vv = jnp.concatenate([vp_ref[...], vc_ref[...]], axis=0)
        a = lax.broadcasted_iota(jnp.int32, (Q_BLOCK, 2 * Q_BLOCK), 0)
        j = lax.broadcasted_iota(jnp.int32, (Q_BLOCK, 2 * Q_BLOCK), 1)
        dist = a + Q_BLOCK - j
        mask = (dist >= 0) & (dist <= Q_BLOCK) & ((j >= Q_BLOCK) | (i > 0))
        for h in range(HEADS):
            sl = slice(h * HEAD_DIM, (h + 1) * HEAD_DIM)
            s = jnp.where(mask, _dot(q[:, sl], kk[:, sl], _NT) * (HEAD_DIM ** -0.5), NEG)
            m = jnp.max(s, axis=-1, keepdims=True)
            p = jnp.exp(s - m)
            den = jnp.sum(p, axis=-1, keepdims=True)
            o_ref[:, sl] = _dot(p.astype(BF16), vv[:, sl], _NN) / den
            lse_ref[:, sl] = jnp.broadcast_to(m + jnp.log(den), (Q_BLOCK, HEAD_DIM))

    o, lse = pl.pallas_call(
        body, name=name, grid=(dil, nb), in_specs=[cur, prev, cur, prev, cur], out_specs=[cur, cur],
        out_shape=[jax.ShapeDtypeStruct((t // dil, dil * B_WIDTH), F32)] * 2,
        compiler_params=_params("parallel", "parallel"),
    )(_subseq(q, dil), _subseq(k, dil), _subseq(k, dil), _subseq(v, dil), _subseq(v, dil))
    return o.reshape(t, B_WIDTH), lse.reshape(t, B_WIDTH)


def _attn_merge(outs, lses, name):
    t = outs[0].shape[0]
    tm = _tile(t)
    blk = pl.BlockSpec((tm, B_WIDTH), lambda i: (i, 0))
    nb = len(outs)

    def body(*refs):
        o_refs, l_refs, yb_ref, lse_ref = refs[:nb], refs[nb:2 * nb], refs[2 * nb], refs[2 * nb + 1]
        ls = [r[...] for r in l_refs]
        m = functools.reduce(jnp.maximum, ls)
        tot = m + jnp.log(sum(jnp.exp(l - m) for l in ls))
        yb_ref[...] = sum(jnp.exp(l - tot) * o[...] for l, o in zip(ls, o_refs)).astype(BF16)
        lse_ref[...] = tot

    return pl.pallas_call(
        body, name=name, grid=(t // tm,), in_specs=[blk] * (2 * nb), out_specs=[blk, blk],
        out_shape=[jax.ShapeDtypeStruct((t, B_WIDTH), BF16), jax.ShapeDtypeStruct((t, B_WIDTH), F32)],
        compiler_params=_params("parallel"),
    )(*outs, *lses)


def _attn_bwd_q(q, k, v, do, o, lse, dil, name):
    t = q.shape[0]
    nb = t // dil // Q_BLOCK
    cur = pl.BlockSpec((Q_BLOCK, B_WIDTH), lambda r, i: (i, r))
    prev = pl.BlockSpec((Q_BLOCK, B_WIDTH), lambda r, i: (jnp.maximum(i - 1, 0), r))

    def body(q_ref, kp_ref, kc_ref, vp_ref, vc_ref, do_ref, o_ref, lse_ref, dq_ref):
        i = pl.program_id(1)
        q = q_ref[...]
        kk = jnp.concatenate([kp_ref[...], kc_ref[...]], axis=0)
        vv = jnp.concatenate([vp_ref[...], vc_ref[...]], axis=0)
        do = do_ref[...]
        dof = do.astype(F32)
        of = o_ref[...].astype(F32)
        a = lax.broadcasted_iota(jnp.int32, (Q_BLOCK, 2 * Q_BLOCK), 0)
        j = lax.broadcasted_iota(jnp.int32, (Q_BLOCK, 2 * Q_BLOCK), 1)
        dist = a + Q_BLOCK - j
        mask = (dist >= 0) & (dist <= Q_BLOCK) & ((j >= Q_BLOCK) | (i > 0))
        for h in range(HEADS):
            sl = slice(h * HEAD_DIM, (h + 1) * HEAD_DIM)
            s = jnp.where(mask, _dot(q[:, sl], kk[:, sl], _NT) * (HEAD_DIM ** -0.5), NEG)
            p = jnp.exp(s - lse_ref[:, h * HEAD_DIM:h * HEAD_DIM + 1])
            dp = _dot(do[:, sl], vv[:, sl], _NT)
            delta = jnp.sum(dof[:, sl] * of[:, sl], axis=-1, keepdims=True)
            ds = p * (dp - delta) * (HEAD_DIM ** -0.5)
            dq_ref[:, sl] = _dot(ds.astype(BF16), kk[:, sl], _NN)

    dq = pl.pallas_call(
        body, name=name, grid=(dil, nb), in_specs=[cur, prev, cur, prev, cur, cur, cur, cur], out_specs=cur,
        out_shape=jax.ShapeDtypeStruct((t // dil, dil * B_WIDTH), F32),
        compiler_params=_params("parallel", "parallel"),
    )(_subseq(q, dil), _subseq(k, dil), _subseq(k, dil), _subseq(v, dil), _subseq(v, dil),
      _subseq(do, dil), _subseq(o, dil), _subseq(lse, dil))
    return dq.reshape(t, B_WIDTH)


def _attn_bwd_kv(q, k, v, do, o, lse, dil, name):
    t = q.shape[0]
    nb = t // dil // Q_BLOCK
    cur = pl.BlockSpec((Q_BLOCK, B_WIDTH), lambda r, j: (j, r))
    nxt = pl.BlockSpec((Q_BLOCK, B_WIDTH), lambda r, j: (jnp.minimum(j + 1, nb - 1), r))

    def body(k_ref, v_ref, qc_ref, qn_ref, doc_ref, don_ref, oc_ref, on_ref, lc_ref, ln_ref, dk_ref, dv_ref):
        jb = pl.program_id(1)
        kb = k_ref[...]
        vb = v_ref[...]
        qq = jnp.concatenate([qc_ref[...], qn_ref[...]], axis=0)
        dd = jnp.concatenate([doc_ref[...], don_ref[...]], axis=0)
        ddf = dd.astype(F32)
        oo = jnp.concatenate([oc_ref[...], on_ref[...]], axis=0).astype(F32)
        ll = jnp.concatenate([lc_ref[...], ln_ref[...]], axis=0)
        a = lax.broadcasted_iota(jnp.int32, (2 * Q_BLOCK, Q_BLOCK), 0)
        b = lax.broadcasted_iota(jnp.int32, (2 * Q_BLOCK, Q_BLOCK), 1)
        dist = a - b
        mask = (dist >= 0) & (dist <= Q_BLOCK) & ((a < Q_BLOCK) | (jb < nb - 1))
        for h in range(HEADS):
            sl = slice(h * HEAD_DIM, (h + 1) * HEAD_DIM)
            s = jnp.where(mask, _dot(qq[:, sl], kb[:, sl], _NT) * (HEAD_DIM ** -0.5), NEG)
            p = jnp.exp(s - ll[:, h * HEAD_DIM:h * HEAD_DIM + 1])
            dv_ref[:, sl] = _dot(p.astype(BF16), dd[:, sl], _TN)
            dp = _dot(dd[:, sl], vb[:, sl], _NT)
            delta = jnp.sum(ddf[:, sl] * oo[:, sl], axis=-1, keepdims=True)
            ds = p * (dp - delta) * (HEAD_DIM ** -0.5)
            dk_ref[:, sl] = _dot(ds.astype(BF16), qq[:, sl], _TN)

    qs, dos, os_, ls = _subseq(q, dil), _subseq(do, dil), _subseq(o, dil), _subseq(lse, dil)
    dk, dv = pl.pallas_call(
        body, name=name, grid=(dil, nb), in_specs=[cur, cur, cur, nxt, cur, nxt, cur, nxt, cur, nxt], out_specs=[cur, cur],
        out_shape=[jax.ShapeDtypeStruct((t // dil, dil * B_WIDTH), F32)] * 2,
        compiler_params=_params("parallel", "parallel"),
    )(_subseq(k, dil), _subseq(v, dil), qs, qs, dos, dos, os_, os_, ls, ls)
    return dk.reshape(t, B_WIDTH), dv.reshape(t, B_WIDTH)


FFN_TN = 256


def _ffn_up(h, up_t, name):
    t, k = h.shape
    tm = _tile(t)

    def body(h_ref, w_ref, o_ref):
        o_ref[...] = _dot(h_ref[...], w_ref[...], _NT).astype(BF16)

    return pl.pallas_call(
        body, name=name, grid=(2, t // tm),
        in_specs=[pl.BlockSpec((tm, k), lambda p, i: (i, 0)), pl.BlockSpec((None, FFN_DIM, k), lambda p, i: (p, 0, 0))],
        out_specs=pl.BlockSpec((None, tm, FFN_DIM), lambda p, i: (p, i, 0)),
        out_shape=jax.ShapeDtypeStruct((2, t, FFN_DIM), BF16), compiler_params=_params("parallel", "parallel"),
    )(h, up_t.reshape(2, FFN_DIM, k))


def _ffn_up_dx(du, up_t, name):
    t = du.shape[1]
    k = up_t.shape[1]
    tm = _tile(t)

    def body(a_ref, b_ref, o_ref):
        o_ref[...] = _dot(a_ref[0], b_ref[0], _NN) + _dot(a_ref[1], b_ref[1], _NN)

    return pl.pallas_call(
        body, name=name, grid=(t // tm,),
        in_specs=[pl.BlockSpec((2, tm, FFN_DIM), lambda i: (0, i, 0)), pl.BlockSpec((2, FFN_DIM, k), lambda i: (0, 0, 0))],
        out_specs=pl.BlockSpec((tm, k), lambda i: (i, 0)), out_shape=jax.ShapeDtypeStruct((t, k), F32),
        compiler_params=_params("parallel"),
    )(du, up_t.reshape(2, FFN_DIM, k))


def _ffn_conv(scr_ref, w_ref, b_ref, p, rows):
    acc = b_ref[p] + w_ref[p, FFN_CONV_WIDTH - 1:FFN_CONV_WIDTH, :] * scr_ref[p, pl.ds(FFN_HALO, rows), :]
    for k in range(FFN_CONV_WIDTH - 1):
        acc = acc + w_ref[p, k:k + 1, :] * scr_ref[p, pl.ds(FFN_HALO - (FFN_CONV_WIDTH - 1) + k, rows), :]
    return acc


def _ffn_act(u, dw_w, dw_b, name):
    t = u.shape[1]
    tm = _tile(t)
    hb = tm // FFN_HALO
    main = pl.BlockSpec((2, tm, FFN_TN), lambda i, j: (0, i, j))
    halo = pl.BlockSpec((2, FFN_HALO, FFN_TN), lambda i, j: (0, jnp.maximum(i * hb - 1, 0), j))
    wsp = pl.BlockSpec((2, FFN_CONV_WIDTH, FFN_TN), lambda i, j: (0, 0, j))
    bsp = pl.BlockSpec((2, 1, FFN_TN), lambda i, j: (0, 0, j))

    def body(u_ref, uh_ref, w_ref, b_ref, o_ref, scr_ref):
        first = pl.program_id(0) == 0
        for p in range(2):
            scr_ref[p, 0:FFN_HALO, :] = jnp.where(first, 0.0, uh_ref[p].astype(F32))
            scr_ref[p, FFN_HALO:, :] = u_ref[p].astype(F32)
        za = _ffn_conv(scr_ref, w_ref, b_ref, 0, tm)
        zb = _ffn_conv(scr_ref, w_ref, b_ref, 1, tm)
        o_ref[...] = (za * _sigmoid(za) * zb).astype(BF16)

    return pl.pallas_call(
        body, name=name, grid=(t // tm, FFN_DIM // FFN_TN), in_specs=[main, halo, wsp, bsp],
        out_specs=pl.BlockSpec((tm, FFN_TN), lambda i, j: (i, j)), out_shape=jax.ShapeDtypeStruct((t, FFN_DIM), BF16),
        scratch_shapes=[pltpu.VMEM((2, tm + FFN_HALO, FFN_TN), F32)], compiler_params=_params("parallel", "parallel"),
    )(u, u, dw_w, dw_b)


def _ffn_act_bwd(u, dact, dw_w, dw_b, name):
    t = u.shape[1]
    tm = _tile(t)
    hb = tm // FFN_HALO
    nt = t // tm
    last_halo = t // FFN_HALO - 1
    rows = tm + FFN_HALO
    prev_i = lambda i: jnp.maximum(i * hb - 1, 0)
    next_i = lambda i: jnp.minimum((i + 1) * hb, last_halo)
    main = pl.BlockSpec((2, tm, FFN_TN), lambda j, i: (0, i, j))
    prev = pl.BlockSpec((2, FFN_HALO, FFN_TN), lambda j, i: (0, prev_i(i), j))
    nxt = pl.BlockSpec((2, FFN_HALO, FFN_TN), lambda j, i: (0, next_i(i), j))
    wsp = pl.BlockSpec((2, FFN_CONV_WIDTH, FFN_TN), lambda j, i: (0, 0, j))
    bsp = pl.BlockSpec((2, 1, FFN_TN), lambda j, i: (0, 0, j))

    def body(u_ref, up_ref, un_ref, da_ref, dan_ref, w_ref, b_ref, du_ref, dw_ref, db_ref, su_ref, sd_ref):
        i = pl.program_id(1)

        @pl.when(i == 0)
        def _():
            dw_ref[...] = jnp.zeros_like(dw_ref)
            db_ref[...] = jnp.zeros_like(db_ref)

        first, last = i == 0, i == nt - 1
        for p in range(2):
            su_ref[p, 0:FFN_HALO, :] = jnp.where(first, 0.0, up_ref[p].astype(F32))
            su_ref[p, FFN_HALO:FFN_HALO + tm, :] = u_ref[p].astype(F32)
            su_ref[p, FFN_HALO + tm:, :] = jnp.where(last, 0.0, un_ref[p].astype(F32))
        za = _ffn_conv(su_ref, w_ref, b_ref, 0, rows)
        zb = _ffn_conv(su_ref, w_ref, b_ref, 1, rows)
        dact = jnp.concatenate([da_ref[...].astype(F32), jnp.where(last, 0.0, dan_ref[...].astype(F32))], axis=0)
        sg = _sigmoid(za)
        sd_ref[0] = dact * zb * (sg * (1.0 + za * (1.0 - sg)))
        sd_ref[1] = dact * (za * sg)
        for p in range(2):
            dz = sd_ref[p, 0:tm, :]
            db_ref[p] += _colsum(dz)
            for k in range(FFN_CONV_WIDTH):
                dw_ref[p, k:k + 1, :] += _colsum(dz * su_ref[p, pl.ds(FFN_HALO - (FFN_CONV_WIDTH - 1) + k, tm), :])
            acc = w_ref[p, FFN_CONV_WIDTH - 1:FFN_CONV_WIDTH, :] * dz
            for k in range(FFN_CONV_WIDTH - 1):
                acc = acc + w_ref[p, k:k + 1, :] * sd_ref[p, pl.ds(FFN_CONV_WIDTH - 1 - k, tm), :]
            du_ref[p] = acc.astype(BF16)

    return pl.pallas_call(
        body, name=name, grid=(FFN_DIM // FFN_TN, nt),
        in_specs=[main, prev, nxt, pl.BlockSpec((tm, FFN_TN), lambda j, i: (i, j)),
                  pl.BlockSpec((FFN_HALO, FFN_TN), lambda j, i: (next_i(i), j)), wsp, bsp],
        out_specs=[main, wsp, bsp],
        out_shape=[jax.ShapeDtypeStruct((2, t, FFN_DIM), BF16), jax.ShapeDtypeStruct((2, FFN_CONV_WIDTH, FFN_DIM), F32),
                   jax.ShapeDtypeStruct((2, 1, FFN_DIM), F32)],
        scratch_shapes=[pltpu.VMEM((2, tm + 2 * FFN_HALO, FFN_TN), F32), pltpu.VMEM((2, rows, FFN_TN), F32)],
        compiler_params=_params("parallel", "arbitrary"),
    )(u, u, u, dact, dact, dw_w, dw_b)


def _conv_transpose(dz, w, width, halo_rows, tn, name):
    t, n = dz.shape
    tm = _tile(t, (256, 128))
    hb = tm // halo_rows
    last_halo = t // halo_rows - 1
    nt = t // tm
    main = pl.BlockSpec((tm, tn), lambda i, j: (i, j))
    nxt = pl.BlockSpec((halo_rows, tn), lambda i, j: (jnp.minimum((i + 1) * hb, last_halo), j))

    def body(dz_ref, dzn_ref, w_ref, du_ref, scr_ref):
        scr_ref[0:tm, :] = dz_ref[...].astype(F32)
        scr_ref[tm:, :] = jnp.where(pl.program_id(0) == nt - 1, 0.0, dzn_ref[...].astype(F32))
        acc = w_ref[width - 1:width, :] * scr_ref[pl.ds(0, tm), :]
        for k in range(width - 1):
            acc = acc + w_ref[k:k + 1, :] * scr_ref[pl.ds(width - 1 - k, tm), :]
        du_ref[...] = acc.astype(du_ref.dtype)

    return pl.pallas_call(
        body, name=name, grid=(nt, n // tn),
        in_specs=[main, nxt, pl.BlockSpec((width, tn), lambda i, j: (0, j))], out_specs=main,
        out_shape=jax.ShapeDtypeStruct((t, n), BF16 if dz.dtype == BF16 else F32),
        scratch_shapes=[pltpu.VMEM((tm + halo_rows, tn), F32)],
        compiler_params=_params("parallel", "parallel"),
    )(dz, dz, w)


CONV_TM = 256


def _glu_window(pa_ref, pah_ref, pg_ref, pgh_ref, scr_ref, first):
    ah, gh = pah_ref[...].astype(F32), pgh_ref[...].astype(F32)
    scr_ref[0:CONV_HALO, :] = jnp.where(first, 0.0, ah * _sigmoid(gh))
    scr_ref[CONV_HALO:, :] = pa_ref[...].astype(F32) * _sigmoid(pg_ref[...].astype(F32))


def _dw_conv31(scr_ref, w_ref, b_ref, tm):
    acc = b_ref[...] + w_ref[CONV_WIDTH - 1:CONV_WIDTH, :] * scr_ref[pl.ds(CONV_HALO, tm), :]
    for k in range(CONV_WIDTH - 1):
        acc = acc + w_ref[k:k + 1, :] * scr_ref[pl.ds(CONV_HALO - (CONV_WIDTH - 1) + k, tm), :]
    return acc


def _conformer_specs(t):
    tm = _tile(t, (CONV_TM, 128))
    hb = tm // CONV_HALO
    d = D_MODEL
    main = lambda c: pl.BlockSpec((tm, d), lambda i: (i, c))
    halo = lambda c: pl.BlockSpec((CONV_HALO, d), lambda i: (jnp.maximum(i * hb - 1, 0), c))
    row = pl.BlockSpec((1, d), lambda i: (0, 0))
    wsp = pl.BlockSpec((CONV_WIDTH, d), lambda i: (0, 0))
    return tm, main, halo, row, wsp


def _conformer_mid(p, dw_w, dw_b, ln_g, ln_b, name):
    t = p.shape[0]
    tm, main, halo, row, wsp = _conformer_specs(t)

    def body(pa_ref, pah_ref, pg_ref, pgh_ref, w_ref, b_ref, g_ref, lb_ref, o_ref, scr_ref):
        _glu_window(pa_ref, pah_ref, pg_ref, pgh_ref, scr_ref, pl.program_id(0) == 0)
        dc = _dw_conv31(scr_ref, w_ref, b_ref, tm)
        mu = jnp.mean(dc, axis=-1, keepdims=True)
        xc = dc - mu
        ln = xc * lax.rsqrt(jnp.mean(xc * xc, axis=-1, keepdims=True) + EPS) * g_ref[...] + lb_ref[...]
        o_ref[...] = (ln * _sigmoid(ln)).astype(BF16)

    return pl.pallas_call(
        body, name=name, grid=(t // tm,), in_specs=[main(0), halo(0), main(1), halo(1), wsp, row, row, row],
        out_specs=main(0), out_shape=jax.ShapeDtypeStruct((t, D_MODEL), BF16),
        scratch_shapes=[pltpu.VMEM((tm + CONV_HALO, D_MODEL), F32)], compiler_params=_params("parallel"),
    )(p, p, p, p, dw_w, dw_b, ln_g, ln_b)


def _conformer_mid_bwd(p, ds, dw_w, dw_b, ln_g, ln_b, name):
    t = p.shape[0]
    tm, main, halo, row, wsp = _conformer_specs(t)

    def body(pa_ref, pah_ref, pg_ref, pgh_ref, ds_ref, w_ref, b_ref, g_ref, lb_ref,
             ddc_ref, dw_ref, db_ref, dg_ref, dlb_ref, scr_ref):
        @pl.when(pl.program_id(0) == 0)
        def _():
            for r_ in (dw_ref, db_ref, dg_ref, dlb_ref):
                r_[...] = jnp.zeros_like(r_)

        _glu_window(pa_ref, pah_ref, pg_ref, pgh_ref, scr_ref, pl.program_id(0) == 0)
        dc = _dw_conv31(scr_ref, w_ref, b_ref, tm)
        mu = jnp.mean(dc, axis=-1, keepdims=True)
        xc = dc - mu
        rstd = lax.rsqrt(jnp.mean(xc * xc, axis=-1, keepdims=True) + EPS)
        xhat = xc * rstd
        ln = xhat * g_ref[...] + lb_ref[...]
        sg = _sigmoid(ln)
        dln = ds_ref[...].astype(F32) * (sg * (1.0 + ln * (1.0 - sg)))
        dg_ref[...] += _colsum(dln * xhat)
        dlb_ref[...] += _colsum(dln)
        dxh = dln * g_ref[...]
        ddc = rstd * (dxh - jnp.mean(dxh, axis=-1, keepdims=True) - xhat * jnp.mean(dxh * xhat, axis=-1, keepdims=True))
        ddc_ref[...] = ddc
        db_ref[...] += _colsum(ddc)
        for k in range(CONV_WIDTH):
            dw_ref[k:k + 1, :] += _colsum(ddc * scr_ref[pl.ds(CONV_HALO - (CONV_WIDTH - 1) + k, tm), :])

    return pl.pallas_call(
        body, name=name, grid=(t // tm,), in_specs=[main(0), halo(0), main(1), halo(1), main(0), wsp, row, row, row],
        out_specs=[main(0), wsp, row, row, row],
        out_shape=[jax.ShapeDtypeStruct((t, D_MODEL), F32), jax.ShapeDtypeStruct((CONV_WIDTH, D_MODEL), F32)]
        + [jax.ShapeDtypeStruct((1, D_MODEL), F32)] * 3,
        scratch_shapes=[pltpu.VMEM((tm + CONV_HALO, D_MODEL), F32)], compiler_params=_params("arbitrary"),
    )(p, p, p, p, ds, dw_w, dw_b, ln_g, ln_b)


def _glu_bwd(p, dglu, name):
    t = p.shape[0]
    d = D_MODEL
    tm = _tile(t)
    col = lambda c: pl.BlockSpec((tm, d), lambda i: (i, c))
    wide = pl.BlockSpec((tm, 2 * d), lambda i: (i, 0))
    row = pl.BlockSpec((1, 2 * d), lambda i: (0, 0))

    def body(pa_ref, pg_ref, dglu_ref, dp_ref, db_ref):
        @pl.when(pl.program_id(0) == 0)
        def _():
            db_ref[...] = jnp.zeros_like(db_ref)

        a, g, dglu = pa_ref[...].astype(F32), pg_ref[...].astype(F32), dglu_ref[...].astype(F32)
        sg = _sigmoid(g)
        da = (dglu * sg).astype(BF16)
        dg = (dglu * a * sg * (1.0 - sg)).astype(BF16)
        dp_ref[:, 0:d] = da
        dp_ref[:, d:2 * d] = dg
        db_ref[:, 0:d] += _colsum(da.astype(F32))
        db_ref[:, d:2 * d] += _colsum(dg.astype(F32))

    return pl.pallas_call(
        body, name=name, grid=(t // tm,), in_specs=[col(0), col(1), col(0)], out_specs=[wide, row],
        out_shape=[jax.ShapeDtypeStruct((t, 2 * d), BF16), jax.ShapeDtypeStruct((1, 2 * d), F32)],
        compiler_params=_params("arbitrary"),
    )(p, p, dglu)


def _colsum_call(a, name):
    t, n = a.shape
    tm = _tile(t)

    def body(a_ref, o_ref):
        @pl.when(pl.program_id(0) == 0)
        def _():
            o_ref[...] = jnp.zeros_like(o_ref)

        o_ref[...] += _colsum(a_ref[...].astype(F32))

    return pl.pallas_call(
        body, name=name, grid=(t // tm,), in_specs=[pl.BlockSpec((tm, n), lambda i: (i, 0))],
        out_specs=pl.BlockSpec((1, n), lambda i: (0, 0)), out_shape=jax.ShapeDtypeStruct((1, n), F32),
        compiler_params=_params("arbitrary"),
    )(a)


def _ada_fwd(c_all, w, name):
    rows, d = c_all.shape
    n = w.shape[1]
    tn = _tile(n, (256, 128))

    def body(c_ref, w_ref, o_ref):
        c = c_ref[...]
        o_ref[...] = _dot((c * _sigmoid(c)).astype(BF16), w_ref[...].astype(BF16), _NN)

    return pl.pallas_call(
        body, name=name, grid=(n // tn,),
        in_specs=[pl.BlockSpec((rows, d), lambda j: (0, 0)), pl.BlockSpec((d, tn), lambda j: (0, j))],
        out_specs=pl.BlockSpec((rows, tn), lambda j: (0, j)), out_shape=jax.ShapeDtypeStruct((rows, n), F32),
        compiler_params=_params("parallel"),
    )(c_all, w)


def _ada_bwd(c_all, dmod, name):
    rows, d = c_all.shape
    n = dmod.shape[1]
    tn = _tile(n, (256, 128))

    def body(c_ref, g_ref, o_ref):
        c = c_ref[...]
        o_ref[...] = _dot((c * _sigmoid(c)).astype(BF16), g_ref[...].astype(BF16), _TN)

    return pl.pallas_call(
        body, name=name, grid=(n // tn,),
        in_specs=[pl.BlockSpec((rows, d), lambda j: (0, 0)), pl.BlockSpec((rows, tn), lambda j: (0, j))],
        out_specs=pl.BlockSpec((d, tn), lambda j: (0, j)), out_shape=jax.ShapeDtypeStruct((d, n), F32),
        compiler_params=_params("parallel"),
    )(c_all, dmod)


def _sum_slots(a, name):
    s, r, c = a.shape
    tr = _row_tile(r, 256)

    def body(a_ref, o_ref):
        acc = a_ref[0].astype(F32)
        for k in range(1, s):
            acc = acc + a_ref[k].astype(F32)
        o_ref[...] = acc

    return pl.pallas_call(
        body, name=name, grid=(r // tr,), in_specs=[pl.BlockSpec((s, tr, c), lambda i: (0, i, 0))],
        out_specs=pl.BlockSpec((tr, c), lambda i: (i, 0)), out_shape=jax.ShapeDtypeStruct((r, c), F32),
        compiler_params=_params("parallel"),
    )(a)


def _adamw(w, g, m, v, name):
    r, c = w.shape
    tr = _row_tile(r, 256)
    blk = pl.BlockSpec((tr, c), lambda i: (i, 0))
    c1 = 1.0 / (1.0 - ADAM_B1 ** ADAM_STEP)
    c2 = 1.0 / (1.0 - ADAM_B2 ** ADAM_STEP)

    def body(w_ref, g_ref, m_ref, v_ref, d_ref, nm_ref, nv_ref):
        g_ = g_ref[...]
        nm = ADAM_B1 * m_ref[...] + (1.0 - ADAM_B1) * g_
        nv = ADAM_B2 * v_ref[...] + (1.0 - ADAM_B2) * (g_ * g_)
        d_ref[...] = -ADAM_LR * ((nm * c1) / (jnp.sqrt(nv * c2) + ADAM_EPS) + ADAM_WD * w_ref[...])
        nm_ref[...] = nm
        nv_ref[...] = nv

    return pl.pallas_call(
        body, name=name, grid=(r // tr,), in_specs=[blk] * 4, out_specs=[blk] * 3,
        out_shape=[jax.ShapeDtypeStruct((r, c), F32)] * 3, compiler_params=_params("parallel"),
    )(w, g, m, v)


def _mesh_pos():
    return lax.axis_index("x"), lax.axis_index("y"), lax.axis_index("c")


def _all_gather_vmem(x_shard, name):
    m_per, n = x_shard.shape

    def body(x_ref, out_ref, send_sems, recv_sems, local_sem):
        x, y, c = _mesh_pos()
        me, sibling = (x, y, c), (x, y, 1 - c)
        chips = [(1 - x, y), (x, 1 - y), (1 - x, 1 - y)]

        def rows(px, py, pc):
            return out_ref.at[pl.ds((4 * px + 2 * py + pc) * m_per, m_per), :]

        def copy(k, block, to, src=None):
            return pltpu.make_async_remote_copy(
                src_ref=rows(*block) if src is None else src, dst_ref=rows(*block),
                send_sem=send_sems.at[k], recv_sem=recv_sems.at[k], device_id=to, device_id_type=MESH)

        mine = pltpu.make_async_copy(x_ref, rows(*me), local_sem)
        mine.start()
        first = [copy(0, me, sibling, src=x_ref)]
        first += [copy(1 + j, me, (*chip, c), src=x_ref) for j, chip in enumerate(chips)]
        for cp in first:
            cp.start()
        passed = [copy(4 + j, (*chip, c), sibling) for j, chip in enumerate(chips)]
        for j, chip in enumerate(chips):
            copy(1 + j, (*chip, c), me).wait_recv()
            passed[j].start()
        copy(0, sibling, me).wait_recv()
        for j, chip in enumerate(chips):
            copy(4 + j, (*chip, 1 - c), me).wait_recv()
        for cp in first + passed:
            cp.wait_send()
        mine.wait()

    return pl.pallas_call(
        body, name=name, out_shape=jax.ShapeDtypeStruct((N_DEV * m_per, n), x_shard.dtype),
        in_specs=[pl.BlockSpec(memory_space=pltpu.VMEM)], out_specs=pl.BlockSpec(memory_space=pltpu.VMEM),
        scratch_shapes=[pltpu.SemaphoreType.DMA((7,)), pltpu.SemaphoreType.DMA((7,)), pltpu.SemaphoreType.DMA],
    )(x_shard)


def _all_gather_hbm(shards, name):
    n = len(shards)
    out_shape = [jax.ShapeDtypeStruct((N_DEV,) + s.shape, s.dtype) for s in shards]

    def body(*refs):
        x_refs, out_refs = refs[:n], refs[n:2 * n]
        send_sems, recv_sems, local_sems = refs[2 * n:]
        x, y, c = _mesh_pos()
        me, sibling = (x, y, c), (x, y, 1 - c)
        chips = [(1 - x, y), (x, 1 - y), (1 - x, 1 - y)]

        def blk(a, p):
            return out_refs[a].at[4 * p[0] + 2 * p[1] + p[2]]

        def copy(a, k, block, to, src=None):
            return pltpu.make_async_remote_copy(
                src_ref=blk(a, block) if src is None else src, dst_ref=blk(a, block),
                send_sem=send_sems.at[7 * a + k], recv_sem=recv_sems.at[7 * a + k], device_id=to, device_id_type=MESH)

        mine = [pltpu.make_async_copy(x_refs[a], blk(a, me), local_sems.at[a]) for a in range(n)]
        for cp in mine:
            cp.start()
        first = []
        for a in range(n):
            first.append(copy(a, 0, me, sibling, src=x_refs[a]))
            first += [copy(a, 1 + j, me, (*chip, c), src=x_refs[a]) for j, chip in enumerate(chips)]
        for cp in first:
            cp.start()
        passed = []
        for j, chip in enumerate(chips):
            for a in range(n):
                copy(a, 1 + j, (*chip, c), me).wait_recv()
                fwd = copy(a, 4 + j, (*chip, c), sibling)
                fwd.start()
                passed.append(fwd)
        for a in range(n):
            copy(a, 0, sibling, me).wait_recv()
            for j, chip in enumerate(chips):
                copy(a, 4 + j, (*chip, 1 - c), me).wait_recv()
        for cp in first + passed:
            cp.wait_send()
        for cp in mine:
            cp.wait()

    any_spec = pl.BlockSpec(memory_space=pl.ANY)
    return pl.pallas_call(
        body, name=name, out_shape=out_shape, in_specs=[any_spec] * n, out_specs=[any_spec] * n,
        scratch_shapes=[pltpu.SemaphoreType.DMA((7 * n,)), pltpu.SemaphoreType.DMA((7 * n,)), pltpu.SemaphoreType.DMA((n,))],
    )(*shards)


def _peers(x, y, c):
    flip = lambda v, f: 1 - v if f else v
    return [(flip(x, m & 4), flip(y, m & 2), flip(c, m & 1)) for m in range(1, N_DEV)]


def _dev_index(p):
    return 4 * p[0] + 2 * p[1] + p[2]


def _push_copies(src_refs, land_refs, send_sems, recv_sems, scatter, receive):
    x, y, c = _mesh_pos()
    me = _dev_index((x, y, c))
    copies = []
    for a, (src, land) in enumerate(zip(src_refs, land_refs)):
        for k, p in enumerate(_peers(x, y, c)):
            copies.append(pltpu.make_async_remote_copy(
                src_ref=src.at[_dev_index(p)] if scatter else src, dst_ref=land.at[_dev_index(p) if receive else me],
                send_sem=send_sems.at[7 * a + k], recv_sem=recv_sems.at[7 * a + k], device_id=p, device_id_type=MESH))
    return copies


_HBM = pl.BlockSpec(memory_space=pltpu.HBM)
_SEM = pl.BlockSpec(memory_space=pltpu.SEMAPHORE)
_EFFECT = pltpu.SideEffectType.DATAFLOW_SIDE_EFFECTING


def _pushes_start(srcs, lands, scatter, name):
    n = len(srcs)

    def body(*refs):
        src_refs, land_refs = refs[:n], refs[n:2 * n]
        send_sems, recv_sems = refs[2 * n], refs[2 * n + 1]
        token = refs[-1]
        for cp in _push_copies(src_refs, land_refs, send_sems, recv_sems, scatter, receive=False):
            cp.start()
        token[...] = jnp.zeros_like(token)

    hbm = lambda a: pltpu.HBM(a.shape, a.dtype)
    sems = pltpu.SemaphoreType.DMA((7 * n,))
    outs = pl.pallas_call(
        body, name=name,
        out_shape=(sems, sems, *[hbm(a) for a in srcs], *[hbm(a) for a in lands], jax.ShapeDtypeStruct((8, 128), F32)),
        in_specs=[_HBM] * (2 * n), out_specs=(_SEM, _SEM, *[_HBM] * (2 * n), pl.BlockSpec(memory_space=pltpu.VMEM)),
        input_output_aliases={i: 2 + i for i in range(2 * n)},
        compiler_params=pltpu.CompilerParams(has_side_effects=_EFFECT),
    )(*[pltpu.with_memory_space_constraint(a, pltpu.HBM) for a in (*srcs, *lands)])
    return (outs[0], outs[1], outs[2:2 + n], outs[2 + n:2 + 2 * n], scatter), outs[-1]


def _pushes_wait(handle, after, name):
    send_sems, recv_sems, srcs, lands, scatter = handle
    n = len(srcs)

    def body(*refs):
        src_refs, land_refs = refs[:n], refs[n:2 * n]
        for cp in _push_copies(src_refs, land_refs, refs[2 * n], refs[2 * n + 1], scatter, receive=True):
            cp.wait_send()
            cp.wait_recv()

    hbm = lambda a: pltpu.HBM(a.shape, a.dtype)
    outs = pl.pallas_call(
        body, name=name, out_shape=tuple(hbm(a) for a in (*srcs, *lands)),
        in_specs=[_HBM] * (2 * n) + [_SEM, _SEM, pl.BlockSpec(memory_space=pl.ANY)], out_specs=tuple([_HBM] * (2 * n)),
        input_output_aliases={i: i for i in range(2 * n)},
        compiler_params=pltpu.CompilerParams(has_side_effects=_EFFECT),
    )(*srcs, *lands, send_sems, recv_sems, after)
    return outs[n:]


def _own_block_placed(block, rows_shape, dtype, me):
    return lax.dynamic_update_slice(lax.empty((N_DEV,) + rows_shape, dtype), block[None], (me, 0, 0))


def _ffn_forward(x, mod, norm_g, w, tag):
    sh, sc, gate = mod
    h = _modnorm(x, norm_g, sc, sh, f"{tag}_norm")
    u = _ffn_up(h, w["up_t"], f"{tag}_up")
    act = _ffn_act(u, w["dw_w"], w["dw_b"], f"{tag}_act")
    y, x_new = _matmul(act, w["down"], "nn", F32, f"{tag}_down", resid=(x, gate))
    return x_new, (x, h, u, act, y)


def _ffn_backward(dx_new, saved, mod, norm_g, w, tag):
    x, h, u, act, y = saved
    _, sc, gate = mod
    dy, d_gate = _gate_bwd(dx_new, y, gate, f"{tag}_gate_bwd")
    d_down = _matmul_tn_acc(act, dy, f"{tag}_down_dw")
    dact = _matmul(dy, w["down"], "nt", BF16, f"{tag}_down_dx")
    du, d_dw_w, d_dw_b = _ffn_act_bwd(u, dact, w["dw_w"], w["dw_b"], f"{tag}_act_bwd")
    d_up_t = _matmul_tn_acc(du, h, f"{tag}_up_dw").reshape(2 * FFN_DIM, -1)
    dh = _ffn_up_dx(du, w["up_t"], f"{tag}_up_dx")
    dx, d_w, d_sh = _modnorm_bwd(x, dh, norm_g, sc, dx_new, f"{tag}_norm_bwd")
    return dx, dict(up_t=d_up_t, down=d_down, dw_w=d_dw_w.transpose(1, 0, 2).reshape(FFN_CONV_WIDTH, 2 * FFN_DIM),
                    dw_b=d_dw_b.reshape(1, 2 * FFN_DIM), norm_g=d_w * (1.0 + sc), sh=d_sh, sc=d_w * norm_g, gate=d_gate)


def _mixer_forward(x, mod, norm_g, w, rope, tag):
    sh, sc, gate = mod
    h = _modnorm(x, norm_g, sc, sh, f"{tag}_norm")
    z = _matmul(h, w["w_in_t"], "nt", BF16, f"{tag}_in")
    ya = _gmlp_fwd(z, w["gain"], w["wtril"], w["bias_exp"], f"{tag}_gmlp")
    q, k, v = _qk_prep(z, rope[0], rope[1], w["gq"], w["gk"], w["seg"], f"{tag}_qk")
    outs, lses = [], []
    for _, dil in PATTERNS:
        o, l = _attn_fwd(q, k, v, dil, f"{tag}_attn_d{dil}")
        outs.append(o)
        lses.append(l)
    yb, lse = _attn_merge(outs, lses, f"{tag}_merge")
    cat = jnp.concatenate([ya, yb], axis=1)
    y, x_new = _matmul(cat, w["w_out"], "nn", F32, f"{tag}_out", resid=(x, gate))
    return x_new, (x, h, z, q, k, v, yb, lse, cat, y)


def _mixer_backward(dx_new, saved, mod, norm_g, w, rope, tag):
    x, h, z, q, k, v, yb, lse, cat, y = saved
    _, sc, gate = mod
    dy, d_gate = _gate_bwd(dx_new, y, gate, f"{tag}_gate_bwd")
    d_w_out = _matmul_tn_acc(cat, dy, f"{tag}_out_dw")
    dcat = _matmul(dy, w["w_out"], "nt", BF16, f"{tag}_out_dx")
    dz_a, d_sp_w, d_gain, d_bias_exp = _gmlp_bwd(z, dcat, w["gain"], w["wtril"], w["wtril_t"], w["bias_exp"], f"{tag}_gmlp_bwd")
    dyb = dcat[:, A_WIDTH:]
    dqs, dks, dvs = [], [], []
    for _, dil in PATTERNS:
        dqs.append(_attn_bwd_q(q, k, v, dyb, yb, lse, dil, f"{tag}_attn_dq_d{dil}"))
        dk, dv = _attn_bwd_kv(q, k, v, dyb, yb, lse, dil, f"{tag}_attn_dkv_d{dil}")
        dks.append(dk)
        dvs.append(dv)
    dz_qkv, d_gq, d_gk = _qk_prep_bwd(z, dqs, dks, dvs, rope[0], rope[1], w["gq"], w["gk"], w["seg"], f"{tag}_qk_bwd")
    dz = jnp.concatenate([dz_a, dz_qkv], axis=1)
    d_w_in_t = _matmul_tn_acc(dz, h, f"{tag}_in_dw")
    dh = _matmul(dz, w["w_in_t"], "nn", F32, f"{tag}_in_dx")
    dx, d_w, d_sh = _modnorm_bwd(x, dh, norm_g, sc, dx_new, f"{tag}_norm_bwd")
    return dx, dict(
        w_in_t=d_w_in_t, w_out=d_w_out, vnorm_g=d_gain.reshape(A_GROUPS, GROUP_DIM), spatial_w=d_sp_w,
        spatial_b=d_bias_exp.reshape(CHUNK, A_GROUPS, GROUP_DIM).sum(-1).T,
        q_norm_g=d_gq.reshape(HEADS, HEAD_DIM).sum(0), k_norm_g=d_gk.reshape(HEADS, HEAD_DIM).sum(0),
        norm_g=d_w * (1.0 + sc), sh=d_sh, sc=d_w * norm_g, gate=d_gate)


def _conformer_forward(x, mod, norm_g, w, tag):
    sh, sc, gate = mod
    h = _modnorm(x, norm_g, sc, sh, f"{tag}_norm")
    p = _matmul(h, w["pw1_t"], "nt", BF16, f"{tag}_pw1", bias=w["pw1_b"])
    s = _conformer_mid(p, w["dw_w"], w["dw_b"], w["ln_g"], w["ln_b"], f"{tag}_mid")
    y, x_new = _matmul(s, w["pw2"], "nn", F32, f"{tag}_pw2", bias=w["pw2_b"], resid=(x, gate))
    return x_new, (x, h, p, s, y)


def _conformer_backward(dx_new, saved, mod, norm_g, w, tag):
    x, h, p, s, y = saved
    _, sc, gate = mod
    dy, d_gate = _gate_bwd(dx_new, y, gate, f"{tag}_gate_bwd")
    d_pw2 = _matmul_tn_acc(s, dy, f"{tag}_pw2_dw")
    d_pw2_b = _colsum_call(dy, f"{tag}_pw2_db")
    ds = _matmul(dy, w["pw2"], "nt", BF16, f"{tag}_pw2_dx")
    ddc, d_dw_w, d_dw_b, d_ln_g, d_ln_b = _conformer_mid_bwd(p, ds, w["dw_w"], w["dw_b"], w["ln_g"], w["ln_b"], f"{tag}_mid_bwd")
    dglu = _conv_transpose(ddc, w["dw_w"], CONV_WIDTH, CONV_HALO, 512, f"{tag}_conv_bwd")
    dp, d_pw1_b = _glu_bwd(p, dglu, f"{tag}_glu_bwd")
    d_pw1_t = _matmul_tn_acc(dp, h, f"{tag}_pw1_dw")
    dh = _matmul(dp, w["pw1_t"], "nn", F32, f"{tag}_pw1_dx")
    dx, d_w, d_sh = _modnorm_bwd(x, dh, norm_g, sc, dx_new, f"{tag}_norm_bwd")
    return dx, dict(pw1_t=d_pw1_t, pw1_b=d_pw1_b, dw_w=d_dw_w, dw_b=d_dw_b, ln_g=d_ln_g, ln_b=d_ln_b, pw2=d_pw2,
                    pw2_b=d_pw2_b, norm_g=d_w * (1.0 + sc), sh=d_sh, sc=d_w * norm_g, gate=d_gate)


def _local_step(x, target, pos, mod, norm_mix_g, norm_ffn_g, mixer_w, conv_w, ffn_w, fetch, emit):
    d = D_MODEL

    def tied(m, token):
        return m if token is None else (m[0], m[1], m[2] + token[0:1, 0:1])

    inv_freq = 1.0 / (ROPE_THETA ** (jnp.arange(0, HEAD_DIM, 2, dtype=F32) / HEAD_DIM))
    inv_freq = jnp.tile(inv_freq, 2 * HEADS)[None, :]
    sign = jnp.tile(jnp.concatenate([-jnp.ones(HEAD_DIM // 2, F32), jnp.ones(HEAD_DIM // 2, F32)]), HEADS)[None, :]
    rope = _rope_tables(pos, inv_freq, sign, "rope_tables")
    mods = [[mod[l:l + 1, i * d:(i + 1) * d] for i in range(6)] for l in range(2)]
    mix = [(m[0], m[1], m[2]) for m in mods]
    ffn = [(m[3], m[4], m[5]) for m in mods]
    gm = [norm_mix_g[l:l + 1] for l in range(2)]
    gf = [norm_ffn_g[l:l + 1] for l in range(2)]

    mixer_w = {**mixer_w, **fetch("l0_mix", x)}
    x1, s_mix = _mixer_forward(x, mix[0], gm[0], mixer_w, rope, "l0_mix")
    ffn_w0 = {**ffn_w[0], **fetch("l0_ffn", x1)}
    x2, s_ffn0 = _ffn_forward(x1, ffn[0], gf[0], ffn_w0, "l0_ffn")
    conv_w = {**conv_w, **fetch("l1_conv", x2)}
    x3, s_conv = _conformer_forward(x2, mix[1], gm[1], conv_w, "l1_conv")
    ffn_w1 = {**ffn_w[1], **fetch("l1_ffn", x3)}
    x4, s_ffn1 = _ffn_forward(x3, ffn[1], gf[1], ffn_w1, "l1_ffn")
    dx, loss = _loss_head(x4, target, "loss_head")
    dx, g_ffn1 = _ffn_backward(dx, s_ffn1, ffn[1], gf[1], ffn_w1, "l1_ffn")
    token = emit("l1_ffn", [g_ffn1.pop("up_t"), g_ffn1.pop("down")])
    dx, g_conv = _conformer_backward(dx, s_conv, tied(mix[1], token), gm[1], conv_w, "l1_conv")
    token = emit("l1_conv", [g_conv.pop("pw1_t"), g_conv.pop("pw2")])
    dx, g_ffn0 = _ffn_backward(dx, s_ffn0, tied(ffn[0], token), gf[0], ffn_w0, "l0_ffn")
    token = emit("l0_ffn", [g_ffn0.pop("up_t"), g_ffn0.pop("down")])
    dx, g_mix = _mixer_backward(dx, s_mix, tied(mix[0], token), gm[0], mixer_w, rope, "l0_mix")
    emit("l0_mix", [g_mix.pop("w_in_t"), g_mix.pop("w_out")])
    blocks = [g_mix, g_ffn0, g_conv, g_ffn1]
    dmod = jnp.stack([jnp.concatenate([a["sh"], a["sc"], a["gate"], b["sh"], b["sc"], b["gate"]], axis=1)[0]
                      for a, b in ((g_mix, g_ffn0), (g_conv, g_ffn1))])
    return loss, dx, dmod, blocks


def _pack(arrs, rows=None):
    flat = jnp.concatenate([a.reshape(-1).astype(F32) for a in arrs])
    n = flat.shape[0]
    if rows is None:
        cols = 1024
        rows = -(-n // (8 * cols)) * 8
    else:
        cols = -(-n // (rows * 128)) * 128
    return jnp.pad(flat, (0, rows * cols - n)).reshape(rows, cols)


def _unpack(flat, shapes):
    out, off = [], 0
    for shp in shapes:
        n = math.prod(shp)
        out.append(flat[off:off + n].reshape(shp))
        off += n
    return out


def _take_block(a, idx, size, axis):
    return lax.dynamic_slice_in_dim(a, idx * size, size, axis)


def kernel(x, c, positions, ada_w, ada_b, norm_mix_g, norm_ffn_g, ab_w_in, a_vnorm_g, a_spatial_w, a_spatial_b, b_q_norm_g, b_k_norm_g, ab_w_out, conv_pw1_w, conv_pw1_b, conv_dw_w, conv_dw_b, conv_ln_g, conv_ln_b, conv_pw2_w, conv_pw2_b, ffn_up_w, ffn_dw_w, ffn_dw_b, ffn_down_w, loss_target, m_ada_w, m_ada_b, m_norm_mix_g, m_norm_ffn_g, m_ab_w_in, m_a_vnorm_g, m_a_spatial_w, m_a_spatial_b, m_b_q_norm_g, m_b_k_norm_g, m_ab_w_out, m_conv_pw1_w, m_conv_pw1_b, m_conv_dw_w, m_conv_dw_b, m_conv_ln_g, m_conv_ln_b, m_conv_pw2_w, m_conv_pw2_b, m_ffn_up_w, m_ffn_dw_w, m_ffn_dw_b, m_ffn_down_w, v_ada_w, v_ada_b, v_norm_mix_g, v_norm_ffn_g, v_ab_w_in, v_a_vnorm_g, v_a_spatial_w, v_a_spatial_b, v_b_q_norm_g, v_b_k_norm_g, v_ab_w_out, v_conv_pw1_w, v_conv_pw1_b, v_conv_dw_w, v_conv_dw_b, v_conv_ln_g, v_conv_ln_b, v_conv_pw2_w, v_conv_pw2_b, v_ffn_up_w, v_ffn_dw_w, v_ffn_dw_b, v_ffn_down_w):
    weights = dict(ada_w=ada_w, ada_b=ada_b, norm_mix_g=norm_mix_g, norm_ffn_g=norm_ffn_g, ab_w_in=ab_w_in, a_vnorm_g=a_vnorm_g, a_spatial_w=a_spatial_w, a_spatial_b=a_spatial_b, b_q_norm_g=b_q_norm_g, b_k_norm_g=b_k_norm_g, ab_w_out=ab_w_out, conv_pw1_w=conv_pw1_w, conv_pw1_b=conv_pw1_b, conv_dw_w=conv_dw_w, conv_dw_b=conv_dw_b, conv_ln_g=conv_ln_g, conv_ln_b=conv_ln_b, conv_pw2_w=conv_pw2_w, conv_pw2_b=conv_pw2_b, ffn_up_w=ffn_up_w, ffn_dw_w=ffn_dw_w, ffn_dw_b=ffn_dw_b, ffn_down_w=ffn_down_w)
    mom1 = dict(ada_w=m_ada_w, ada_b=m_ada_b, norm_mix_g=m_norm_mix_g, norm_ffn_g=m_norm_ffn_g, ab_w_in=m_ab_w_in, a_vnorm_g=m_a_vnorm_g, a_spatial_w=m_a_spatial_w, a_spatial_b=m_a_spatial_b, b_q_norm_g=m_b_q_norm_g, b_k_norm_g=m_b_k_norm_g, ab_w_out=m_ab_w_out, conv_pw1_w=m_conv_pw1_w, conv_pw1_b=m_conv_pw1_b, conv_dw_w=m_conv_dw_w, conv_dw_b=m_conv_dw_b, conv_ln_g=m_conv_ln_g, conv_ln_b=m_conv_ln_b, conv_pw2_w=m_conv_pw2_w, conv_pw2_b=m_conv_pw2_b, ffn_up_w=m_ffn_up_w, ffn_dw_w=m_ffn_dw_w, ffn_dw_b=m_ffn_dw_b, ffn_down_w=m_ffn_down_w)
    mom2 = dict(ada_w=v_ada_w, ada_b=v_ada_b, norm_mix_g=v_norm_mix_g, norm_ffn_g=v_norm_ffn_g, ab_w_in=v_ab_w_in, a_vnorm_g=v_a_vnorm_g, a_spatial_w=v_a_spatial_w, a_spatial_b=v_a_spatial_b, b_q_norm_g=v_b_q_norm_g, b_k_norm_g=v_b_k_norm_g, ab_w_out=v_ab_w_out, conv_pw1_w=v_conv_pw1_w, conv_pw1_b=v_conv_pw1_b, conv_dw_w=v_conv_dw_w, conv_dw_b=v_conv_dw_b, conv_ln_g=v_conv_ln_g, conv_ln_b=v_conv_ln_b, conv_pw2_w=v_conv_pw2_w, conv_pw2_b=v_conv_pw2_b, ffn_up_w=v_ffn_up_w, ffn_dw_w=v_ffn_dw_w, ffn_dw_b=v_ffn_dw_b, ffn_down_w=v_ffn_down_w)
    order = list(weights)
    d, f2 = D_MODEL, 2 * FFN_DIM
    t = x.shape[1]
    me = 4 * lax.axis_index("x") + 2 * lax.axis_index("y") + lax.axis_index("c")
    for window, dil in PATTERNS:
        assert window // dil == Q_BLOCK and t % (dil * Q_BLOCK) == 0

    small_in = [c[0], conv_pw1_b[0], conv_dw_w[0], conv_dw_b[0], conv_ln_g[0], conv_ln_b[0], conv_pw2_b[0], ffn_dw_w]
    g1 = _all_gather_vmem(_pack(small_in, rows=8), "gather_small").reshape(N_DEV, -1)
    c_all, pw1_b, dw_w, dw_b, ln_g, ln_b, pw2_b, fdw_w = [
        jnp.stack(parts) for parts in zip(*[_unpack(g1[k], [a.shape for a in small_in]) for k in range(N_DEV)])]
    pw1_b, dw_b, ln_g, ln_b, pw2_b = [a.reshape(1, -1) for a in (pw1_b, dw_b, ln_g, ln_b, pw2_b)]
    dw_w = dw_w.transpose(1, 0, 2).reshape(CONV_WIDTH, d)
    fdw_w = fdw_w.transpose(1, 2, 0, 3).reshape(2, FFN_CONV_WIDTH, f2)

    stages = dict(l0_mix=[ab_w_in[0].T, ab_w_out[0]], l0_ffn=[ffn_up_w[0].T, ffn_down_w[0]],
                  l1_conv=[conv_pw1_w[0].T, conv_pw2_w[0]], l1_ffn=[ffn_up_w[1].T, ffn_down_w[1]])
    stages = {k: [s.astype(BF16) for s in v] for k, v in stages.items()}
    names = dict(l0_mix=("w_in_t", "w_out"), l0_ffn=("up_t", "down"), l1_conv=("pw1_t", "pw2"), l1_ffn=("up_t", "down"))
    ready = {"l0_mix": [a.reshape(-1, d) for a in _all_gather_hbm(stages["l0_mix"], "gather_mixer_weights")]}
    stages, _ = lax.optimization_barrier((stages, ready))
    arriving, tokens = {}, []
    for stage, group in (("l0_ffn", ("l0_ffn",)), ("l1_conv", ("l1_conv", "l1_ffn"))):
        srcs = [s for g in group for s in stages[g]]
        arriving[stage], token = _pushes_start(
            srcs, [_own_block_placed(s, s.shape, BF16, me) for s in srcs], False, f"gather_{stage}_start")
        tokens.append(token)

    def fetch(stage, after):
        if stage in arriving:
            full = [a.reshape(-1, d) for a in _pushes_wait(arriving[stage], after, f"gather_{stage}_wait")]
            ready[stage] = full[:2]
            if stage == "l1_conv":
                ready["l1_ffn"] = full[2:]
        return dict(zip(names[stage], ready[stage]))

    c16 = jnp.pad(c_all, ((0, 2 * N_DEV - c_all.shape[0]), (0, 0)))
    part = jnp.concatenate([_ada_fwd(c16, ada_w[l], f"ada_fwd{l}")[:N_DEV] for l in range(2)], axis=1)
    g2 = _all_gather_vmem(part, "gather_mod").reshape(N_DEV, N_DEV, 2, -1)
    mod = lax.dynamic_index_in_dim(g2, me, axis=1, keepdims=False).transpose(1, 0, 2).reshape(2, 6 * d) + ada_b
    mod = mod + tokens[0][0:1, 0:1] + tokens[1][0:1, 0:1]

    causal = jnp.tril(jnp.ones((CHUNK, CHUNK), bool))
    wtril = jnp.where(causal[None], a_spatial_w[0], 0.0)
    mixer_w = dict(
        gain=a_vnorm_g[0].reshape(1, A_WIDTH), wtril=wtril.astype(BF16),
        wtril_t=wtril.transpose(0, 2, 1).astype(BF16),
        bias_exp=jnp.repeat(a_spatial_b[0].T, GROUP_DIM, axis=1),
        gq=jnp.tile(b_q_norm_g[0], HEADS)[None, :], gk=jnp.tile(b_k_norm_g[0], HEADS)[None, :],
        seg=jnp.kron(jnp.eye(HEADS, dtype=F32), jnp.ones((HEAD_DIM, HEAD_DIM), F32)))
    conv_w = dict(pw1_b=pw1_b, dw_w=dw_w, dw_b=dw_b, ln_g=ln_g, ln_b=ln_b, pw2_b=pw2_b)
    ffn_w = [dict(dw_w=fdw_w[l].reshape(FFN_CONV_WIDTH, 2, FFN_DIM).transpose(1, 0, 2), dw_b=ffn_dw_b[l].reshape(2, 1, FFN_DIM))
             for l in range(2)]

    leaving = {}

    def emit(stage, grads):
        blocks = [g.reshape(N_DEV, g.shape[0] // N_DEV, d) for g in grads]
        lands = [_own_block_placed(lax.dynamic_index_in_dim(b, me, 0, keepdims=False), b.shape[1:], BF16, me) for b in blocks]
        leaving[stage], token = _pushes_start(blocks, lands, True, f"reduce_{stage}_start")
        return token

    loss, dx, dmod, (g_mix, g_ffn0, g_conv, g_ffn1) = _local_step(
        x[0], loss_target[0], positions[0].astype(F32)[:, None], mod, norm_mix_g, norm_ffn_g, mixer_w, conv_w, ffn_w,
        fetch, emit)

    reduced = {}
    for stage in ("l1_ffn", "l1_conv", "l0_ffn", "l0_mix"):
        lands = _pushes_wait(leaving[stage], dx, f"reduce_{stage}_wait")
        reduced[stage] = [_sum_slots(a, f"reduce_{stage}_sum{i}") for i, a in enumerate(lands)]
    (r_in_t, r_out), (r_up_t0, r_down0) = reduced["l0_mix"], reduced["l0_ffn"]
    (r_pw1_t, r_pw2), (r_up_t1, r_down1) = reduced["l1_conv"], reduced["l1_ffn"]

    small_g = [
        dmod, jnp.concatenate([g_mix["norm_g"], g_conv["norm_g"]]), jnp.concatenate([g_ffn0["norm_g"], g_ffn1["norm_g"]]),
        g_mix["vnorm_g"], g_mix["spatial_w"], g_mix["spatial_b"], g_mix["q_norm_g"], g_mix["k_norm_g"],
        g_conv["pw1_b"], g_conv["dw_w"], g_conv["dw_b"], g_conv["ln_g"], g_conv["ln_b"], g_conv["pw2_b"],
        jnp.stack([g_ffn0["dw_w"], g_ffn1["dw_w"]]), jnp.concatenate([g_ffn0["dw_b"], g_ffn1["dw_b"]])]
    packed = _pack(small_g, rows=8)
    g3 = _all_gather_vmem(packed, "gather_small_grads").reshape(N_DEV, 8, -1)
    total = _unpack(_sum_slots(g3, "sum_small_grads").reshape(-1), [a.shape for a in small_g])
    (s_dmod, s_mix_g, s_ffn_g, s_vnorm, s_sp_w, s_sp_b, s_gq, s_gk, s_pw1_b, s_dw_w, s_dw_b, s_ln_g, s_ln_b,
     s_pw2_b, s_fdw_w, s_fdw_b) = total
    dmod_all = g3.reshape(N_DEV, -1)[:, :2 * 6 * d].reshape(N_DEV, 2, 6 * d)
    n_ada = ada_w.shape[2]
    dmod16 = jnp.pad(_take_block(dmod_all, me, n_ada, 2), ((0, N_DEV), (0, 0), (0, 0)))
    g_ada_w = jnp.stack([_ada_bwd(c16, dmod16[:, l], f"ada_bwd{l}") for l in range(2)])

    grads = dict(
        ada_w=g_ada_w, ada_b=s_dmod, norm_mix_g=s_mix_g, norm_ffn_g=s_ffn_g, ab_w_in=r_in_t.T[None],
        a_vnorm_g=s_vnorm[None], a_spatial_w=s_sp_w[None], a_spatial_b=s_sp_b[None], b_q_norm_g=s_gq[None],
        b_k_norm_g=s_gk[None], ab_w_out=r_out[None], conv_pw1_w=r_pw1_t.T[None],
        conv_pw1_b=_take_block(s_pw1_b, me, conv_pw1_b.shape[1], 1),
        conv_dw_w=_take_block(s_dw_w, me, conv_dw_w.shape[2], 1)[None],
        conv_dw_b=_take_block(s_dw_b, me, conv_dw_b.shape[1], 1), conv_ln_g=_take_block(s_ln_g, me, conv_ln_g.shape[1], 1),
        conv_ln_b=_take_block(s_ln_b, me, conv_ln_b.shape[1], 1), conv_pw2_w=r_pw2[None],
        conv_pw2_b=_take_block(s_pw2_b, me, conv_pw2_b.shape[1], 1),
        ffn_up_w=jnp.stack([r_up_t0.T, r_up_t1.T]), ffn_dw_w=_take_block(s_fdw_w, me, ffn_dw_w.shape[2], 2),
        ffn_dw_b=s_fdw_b, ffn_down_w=jnp.stack([r_down0, r_down1]))

    large = ("ada_w", "ab_w_in", "ab_w_out", "conv_pw1_w", "conv_pw2_w", "ffn_up_w", "ffn_down_w")
    delta, new_m, new_v = {}, {}, {}
    for name in large:
        shp = weights[name].shape
        two_d = lambda a: a.reshape(-1, shp[-1])
        res = _adamw(two_d(weights[name]), two_d(grads[name]), two_d(mom1[name]), two_d(mom2[name]), f"adamw_{name}")
        delta[name], new_m[name], new_v[name] = [r.reshape(shp) for r in res]
    small = [n for n in order if n not in large]
    shapes = [weights[n].shape for n in small]
    res = _adamw(*[_pack([src[n] for n in small]) for src in (weights, grads, mom1, mom2)], "adamw_small")
    for dst, r in zip((delta, new_m, new_v), res):
        for n, a in zip(small, _unpack(r.reshape(-1), shapes)):
            dst[n] = a

    loss = lax.psum(loss[0, 0], ("x", "y", "c"))
    return (loss, dx[None], *[grads[n] for n in order], *[delta[n] for n in order],
            *[new_m[n] for n in order], *[new_v[n] for n in order])
```

```python
import functools
import math

import jax
import jax.numpy as jnp
from jax import lax
from jax.experimental import pallas as pl
from jax.experimental.pallas import tpu as pltpu

F32 = jnp.float32
BF16 = jnp.bfloat16
MESH = pl.DeviceIdType.MESH

D_MODEL = 1024
A_WIDTH = 512
A_GROUPS = 4
GROUP_DIM = 128
CHUNK = 128
B_WIDTH = 512
HEADS = 8
HEAD_DIM = 64
PATTERNS = ((128, 1), (512, 4), (2048, 16))
Q_BLOCK = 128
ROPE_THETA = 10000.0
AB_IN = 2560
CONV_WIDTH = 31
FFN_DIM = 2816
FFN_CONV_WIDTH = 3
EPS = 1e-6
NEG = -1e30
N_DEV = 8
ADAM_LR, ADAM_B1, ADAM_B2, ADAM_EPS, ADAM_WD, ADAM_STEP = 0.001, 0.9, 0.999, 1e-08, 0.01, 10

V7X_VMEM_LIMIT = 56 * 2**20
BF16_ROWS = 16
FFN_HALO = 16
CONV_HALO = 32

_NN = (((1,), (0,)), ((), ()))
_NT = (((1,), (1,)), ((), ()))
_TN = (((0,), (0,)), ((), ()))


def _tile(n, prefs=(512, 256, 128)):
    for t in prefs:
        if n % t == 0:
            return t
    return n


def _row_tile(n, cap=512):
    best = n
    for t in range(8, min(n, cap) + 1, 8):
        if n % t == 0:
            best = t
    return best if best <= cap else n


def _params(*sem):
    return pltpu.CompilerParams(dimension_semantics=sem, vmem_limit_bytes=V7X_VMEM_LIMIT)


def _dot(a, b, dims):
    return lax.dot_general(a, b, dims, preferred_element_type=F32)


def _sigmoid(x):
    return 1.0 / (1.0 + jnp.exp(-x))


def _gelu(x):
    return 0.5 * x * (1.0 + lax.erf(x * (2.0 ** -0.5)))


def _gelu_grad(x):
    return 0.5 * (1.0 + lax.erf(x * (2.0 ** -0.5))) + x * jnp.exp(-0.5 * x * x) * (1.0 / math.sqrt(2.0 * math.pi))


def _colsum(v):
    return jnp.sum(v, axis=0, keepdims=True)


MATMUL_VMEM_BUDGET = 40 * 2**20


def _matmul_tiles(m, n, k, out_bytes, with_resid):
    def options(dim):
        opts = [t for t in (1024, 512, 256, 128) if dim % t == 0]
        return opts + [dim] if dim <= 4096 and dim not in opts else opts

    best = None
    for tm in options(m):
        for tn in options(n):
            need = 4 * (tm * k + k * tn) + tm * tn * (4 + 2 * out_bytes) + (24 * tm * tn if with_resid else 0)
            if need <= MATMUL_VMEM_BUDGET and (best is None or tm * tn / (tm + tn) > best[0]):
                best = (tm * tn / (tm + tn), tm, tn)
    return best[1], best[2]


def _matmul_tn_acc(a, b, name, tk=512):
    squeeze = a.ndim == 2
    a3 = a[None] if squeeze else a
    p_, t, m = a3.shape
    n = b.shape[1]
    nk = t // tk

    def body(a_ref, b_ref, o_ref, acc_ref):
        kt = pl.program_id(1)

        @pl.when(kt == 0)
        def _():
            acc_ref[...] = jnp.zeros_like(acc_ref)

        acc_ref[...] += _dot(a_ref[...], b_ref[...], _TN)

        @pl.when(kt == nk - 1)
        def _():
            o_ref[...] = acc_ref[...].astype(BF16)

    out = pl.pallas_call(
        body, name=name, grid=(p_, nk),
        in_specs=[pl.BlockSpec((None, tk, m), lambda p, kt: (p, kt, 0)), pl.BlockSpec((tk, n), lambda p, kt: (kt, 0))],
        out_specs=pl.BlockSpec((None, m, n), lambda p, kt: (p, 0, 0)), out_shape=jax.ShapeDtypeStruct((p_, m, n), BF16),
        scratch_shapes=[pltpu.VMEM((m, n), F32)], compiler_params=_params("parallel", "arbitrary"),
    )(a3, b)
    return out[0] if squeeze else out


def _matmul(a, b, mode, out_dtype, name, bias=None, resid=None):
    if mode == "nn":
        (m, k), (_, n) = a.shape, b.shape
    elif mode == "nt":
        (m, k), (n, _) = a.shape, b.shape
    else:
        (k, m), (_, n) = a.shape, b.shape
    tm, tn = _matmul_tiles(m, n, k, jnp.dtype(out_dtype).itemsize, resid is not None)
    dims = {"nn": _NN, "nt": _NT, "tn": _TN}[mode]
    a_spec = pl.BlockSpec((k, tm), lambda i, j: (0, i)) if mode == "tn" else pl.BlockSpec((tm, k), lambda i, j: (i, 0))
    b_spec = pl.BlockSpec((tn, k), lambda i, j: (j, 0)) if mode == "nt" else pl.BlockSpec((k, tn), lambda i, j: (0, j))
    in_specs, args = [a_spec, b_spec], [a, b]
    row_spec = pl.BlockSpec((1, tn), lambda i, j: (0, j))
    tile_spec = pl.BlockSpec((tm, tn), lambda i, j: (i, j))
    if bias is not None:
        in_specs.append(row_spec)
        args.append(bias)
    if resid is not None:
        in_specs += [tile_spec, row_spec]
        args += list(resid)
    out_shape = [jax.ShapeDtypeStruct((m, n), out_dtype)]
    out_specs = [tile_spec]
    if resid is not None:
        out_shape.append(jax.ShapeDtypeStruct((m, n), F32))
        out_specs.append(tile_spec)

    def body(*refs):
        a_ref, b_ref = refs[0], refs[1]
        pos = 2
        acc = _dot(a_ref[...], b_ref[...], dims)
        if bias is not None:
            acc = acc + refs[pos][...]
            pos += 1
        if resid is not None:
            x_ref, g_ref = refs[pos], refs[pos + 1]
            pos += 2
        refs[pos][...] = acc.astype(out_dtype)
        if resid is not None:
            refs[pos + 1][...] = x_ref[...] + g_ref[...] * acc

    outs = pl.pallas_call(
        body, name=name, grid=(m // tm, n // tn), in_specs=in_specs, out_specs=out_specs, out_shape=out_shape,
        compiler_params=_params("parallel", "parallel"),
    )(*args)
    return outs if resid is not None else outs[0]


def _modnorm(x, g, sc, sh, name):
    t, d = x.shape
    tm = _tile(t)
    row = pl.BlockSpec((1, d), lambda i: (0, 0))
    blk = pl.BlockSpec((tm, d), lambda i: (i, 0))

    def body(x_ref, g_ref, sc_ref, sh_ref, o_ref):
        x = x_ref[...]
        r = lax.rsqrt(jnp.mean(x * x, axis=-1, keepdims=True) + EPS)
        o_ref[...] = ((x * r) * g_ref[...] * (1.0 + sc_ref[...]) + sh_ref[...]).astype(BF16)

    return pl.pallas_call(
        body, name=name, grid=(t // tm,), in_specs=[blk, row, row, row], out_specs=blk,
        out_shape=jax.ShapeDtypeStruct((t, d), BF16), compiler_params=_params("parallel"),
    )(x, g, sc, sh)


def _modnorm_bwd(x, dh, g, sc, dres, name):
    t, d = x.shape
    tm = _tile(t)
    row = pl.BlockSpec((1, d), lambda i: (0, 0))
    blk = pl.BlockSpec((tm, d), lambda i: (i, 0))

    def body(x_ref, dh_ref, g_ref, sc_ref, dres_ref, dx_ref, dw_ref, dsh_ref):
        @pl.when(pl.program_id(0) == 0)
        def _():
            dw_ref[...] = jnp.zeros_like(dw_ref)
            dsh_ref[...] = jnp.zeros_like(dsh_ref)

        x = x_ref[...]
        dh = dh_ref[...].astype(F32)
        r = lax.rsqrt(jnp.mean(x * x, axis=-1, keepdims=True) + EPS)
        xn = x * r
        dxn = dh * (g_ref[...] * (1.0 + sc_ref[...]))
        dx_ref[...] = dres_ref[...] + r * (dxn - xn * jnp.mean(dxn * xn, axis=-1, keepdims=True))
        dw_ref[...] += _colsum(dh * xn)
        dsh_ref[...] += _colsum(dh)

    return pl.pallas_call(
        body, name=name, grid=(t // tm,), in_specs=[blk, blk, row, row, blk], out_specs=[blk, row, row],
        out_shape=[jax.ShapeDtypeStruct((t, d), F32), jax.ShapeDtypeStruct((1, d), F32), jax.ShapeDtypeStruct((1, d), F32)],
        compiler_params=_params("arbitrary"),
    )(x, dh, g, sc, dres)


def _gate_bwd(dxn, y, gate, name):
    t, d = dxn.shape
    tm = _tile(t)
    row = pl.BlockSpec((1, d), lambda i: (0, 0))
    blk = pl.BlockSpec((tm, d), lambda i: (i, 0))

    def body(dxn_ref, y_ref, g_ref, dy_ref, dg_ref):
        @pl.when(pl.program_id(0) == 0)
        def _():
            dg_ref[...] = jnp.zeros_like(dg_ref)

        dxn = dxn_ref[...]
        dy_ref[...] = (dxn * g_ref[...]).astype(BF16)
        dg_ref[...] += _colsum(dxn * y_ref[...])

    return pl.pallas_call(
        body, name=name, grid=(t // tm,), in_specs=[blk, blk, row], out_specs=[blk, row],
        out_shape=[jax.ShapeDtypeStruct((t, d), BF16), jax.ShapeDtypeStruct((1, d), F32)],
        compiler_params=_params("arbitrary"),
    )(dxn, y, gate)


def _loss_head(y, target, name):
    t, d = y.shape
    tm = _tile(t)
    blk = pl.BlockSpec((tm, d), lambda i: (i, 0))
    one = pl.BlockSpec((1, 1), lambda i: (0, 0))

    def body(y_ref, t_ref, dy_ref, loss_ref, acc_ref):
        @pl.when(pl.program_id(0) == 0)
        def _():
            acc_ref[...] = jnp.zeros_like(acc_ref)

        e = y_ref[...] - t_ref[...]
        dy_ref[...] = e * (1.0 / d)
        acc_ref[...] += _colsum(e * e)

        @pl.when(pl.program_id(0) == pl.num_programs(0) - 1)
        def _():
            loss_ref[...] = jnp.sum(acc_ref[...], axis=1, keepdims=True) * (0.5 / d)

    return pl.pallas_call(
        body, name=name, grid=(t // tm,), in_specs=[blk, blk], out_specs=[blk, one],
        out_shape=[jax.ShapeDtypeStruct((t, d), F32), jax.ShapeDtypeStruct((1, 1), F32)],
        scratch_shapes=[pltpu.VMEM((1, d), F32)], compiler_params=_params("arbitrary"),
    )(y, target)


def _group_norm(vg, gain):
    mu = jnp.mean(vg, axis=-1, keepdims=True)
    xc = vg - mu
    rstd = lax.rsqrt(jnp.mean(xc * xc, axis=-1, keepdims=True) + EPS)
    xhat = xc * rstd
    return xhat, rstd, xhat * gain


def _gmlp_fwd(z, gain, wtril, bias_exp, name):
    t = z.shape[0]
    zu = pl.BlockSpec((CHUNK, A_WIDTH), lambda i: (i, 0))
    zv = pl.BlockSpec((CHUNK, A_WIDTH), lambda i: (i, 1))
    full2 = lambda shp: pl.BlockSpec(shp, lambda i: (0, 0))
    w_spec = pl.BlockSpec((A_GROUPS, CHUNK, CHUNK), lambda i: (0, 0, 0))

    def body(zu_ref, zv_ref, gain_ref, w_ref, b_ref, ya_ref):
        ua = _gelu(zu_ref[...].astype(F32))
        vg = _gelu(zv_ref[...].astype(F32))
        for g in range(A_GROUPS):
            sl = slice(g * GROUP_DIM, (g + 1) * GROUP_DIM)
            _, _, vn = _group_norm(vg[:, sl], gain_ref[:, sl])
            f = _dot(w_ref[g], vn.astype(BF16), _NN) + b_ref[:, sl]
            ya_ref[:, sl] = (ua[:, sl] * f).astype(BF16)

    return pl.pallas_call(
        body, name=name, grid=(t // CHUNK,),
        in_specs=[zu, zv, full2((1, A_WIDTH)), w_spec, full2((CHUNK, A_WIDTH))], out_specs=zu,
        out_shape=jax.ShapeDtypeStruct((t, A_WIDTH), BF16), compiler_params=_params("parallel"),
    )(z, z, gain, wtril, bias_exp)


def _gmlp_bwd(z, dcat, gain, wtril, wtril_t, bias_exp, name):
    t = z.shape[0]
    zu = pl.BlockSpec((CHUNK, A_WIDTH), lambda i: (i, 0))
    zv = pl.BlockSpec((CHUNK, A_WIDTH), lambda i: (i, 1))
    full2 = lambda shp: pl.BlockSpec(shp, lambda i: (0, 0))
    w_spec = pl.BlockSpec((A_GROUPS, CHUNK, CHUNK), lambda i: (0, 0, 0))
    dz_spec = pl.BlockSpec((CHUNK, 2 * A_WIDTH), lambda i: (i, 0))

    def body(zu_ref, zv_ref, dya_ref, gain_ref, w_ref, wt_ref, b_ref, dz_ref, dw_ref, dgain_ref, dbias_ref):
        @pl.when(pl.program_id(0) == 0)
        def _():
            dw_ref[...] = jnp.zeros_like(dw_ref)
            dgain_ref[...] = jnp.zeros_like(dgain_ref)
            dbias_ref[...] = jnp.zeros_like(dbias_ref)

        zu_v = zu_ref[...].astype(F32)
        zv_v = zv_ref[...].astype(F32)
        dya = dya_ref[...].astype(F32)
        ua = _gelu(zu_v)
        vg = _gelu(zv_v)
        row = lax.broadcasted_iota(jnp.int32, (CHUNK, CHUNK), 0)
        col = lax.broadcasted_iota(jnp.int32, (CHUNK, CHUNK), 1)
        for g in range(A_GROUPS):
            sl = slice(g * GROUP_DIM, (g + 1) * GROUP_DIM)
            gain_g = gain_ref[:, sl]
            xhat, rstd, vn = _group_norm(vg[:, sl], gain_g)
            vn16 = vn.astype(BF16)
            f = _dot(w_ref[g], vn16, _NN) + b_ref[:, sl]
            df = dya[:, sl] * ua[:, sl]
            df16 = df.astype(BF16)
            dz_ref[:, sl] = (dya[:, sl] * f * _gelu_grad(zu_v[:, sl])).astype(BF16)
            dw_ref[g] += jnp.where(row >= col, _dot(df16, vn16, _NT), 0.0)
            dvn = _dot(wt_ref[g], df16, _NN)
            dgain_ref[:, sl] += _colsum(dvn * xhat)
            dxh = dvn * gain_g
            dvg = rstd * (dxh - jnp.mean(dxh, axis=-1, keepdims=True) - xhat * jnp.mean(dxh * xhat, axis=-1, keepdims=True))
            dz_ref[:, A_WIDTH + g * GROUP_DIM:A_WIDTH + (g + 1) * GROUP_DIM] = (dvg * _gelu_grad(zv_v[:, sl])).astype(BF16)
            dbias_ref[:, sl] += df

    return pl.pallas_call(
        body, name=name, grid=(t // CHUNK,),
        in_specs=[zu, zv, zu, full2((1, A_WIDTH)), w_spec, w_spec, full2((CHUNK, A_WIDTH))],
        out_specs=[dz_spec, w_spec, full2((1, A_WIDTH)), full2((CHUNK, A_WIDTH))],
        out_shape=[jax.ShapeDtypeStruct((t, 2 * A_WIDTH), BF16), jax.ShapeDtypeStruct((A_GROUPS, CHUNK, CHUNK), F32),
                   jax.ShapeDtypeStruct((1, A_WIDTH), F32), jax.ShapeDtypeStruct((CHUNK, A_WIDTH), F32)],
        compiler_params=_params("arbitrary"),
    )(z, z, dcat, gain, wtril, wtril_t, bias_exp)


def _rope_tables(pos, inv_freq, sign, name):
    t = pos.shape[0]
    tm = _tile(t)
    row = pl.BlockSpec((1, B_WIDTH), lambda i: (0, 0))
    blk = pl.BlockSpec((tm, B_WIDTH), lambda i: (i, 0))

    def body(pos_ref, f_ref, s_ref, cos_ref, sin_ref):
        ang = pos_ref[...] * f_ref[...]
        cos_ref[...] = jnp.cos(ang)
        sin_ref[...] = jnp.sin(ang) * s_ref[...]

    return pl.pallas_call(
        body, name=name, grid=(t // tm,), in_specs=[pl.BlockSpec((tm, 1), lambda i: (i, 0)), row, row],
        out_specs=[blk, blk], out_shape=[jax.ShapeDtypeStruct((t, B_WIDTH), F32)] * 2,
        compiler_params=_params("parallel"),
    )(pos, inv_freq, sign)


def _head_sum(v, seg):
    return lax.dot_general(v, seg, _NN, precision=lax.Precision.HIGHEST, preferred_element_type=F32)


def _swap_halves(v):
    lane = lax.broadcasted_iota(jnp.int32, v.shape, 1)
    return jnp.where((lane & (HEAD_DIM - 1)) < HEAD_DIM // 2,pltpu.roll(v, B_WIDTH - HEAD_DIM // 2, 1), pltpu.roll(v, HEAD_DIM // 2, 1))


def _qk_prep(z, cos_t, sin_t, gq, gk, seg, name):
    t = z.shape[0]
    tm = _tile(t, (256, 128))
    col = lambda c: pl.BlockSpec((tm, B_WIDTH), lambda i: (i, c))
    row = pl.BlockSpec((1, B_WIDTH), lambda i: (0, 0))
    blk = col(0)

    def body(q_ref, k_ref, v_ref, cos_ref, sin_ref, gq_ref, gk_ref, seg_ref, qo_ref, ko_ref, vo_ref):
        def norm_rot(x, g):
            r = lax.rsqrt(_head_sum(x * x, seg_ref[...]) * (1.0 / HEAD_DIM) + EPS)
            xn = x * r * g
            return xn * cos_ref[...] + _swap_halves(xn) * sin_ref[...]

        qo_ref[...] = norm_rot(q_ref[...].astype(F32), gq_ref[...]).astype(BF16)
        ko_ref[...] = norm_rot(k_ref[...].astype(F32), gk_ref[...]).astype(BF16)
        vo_ref[...] = v_ref[...].astype(BF16)

    return pl.pallas_call(
        body, name=name, grid=(t // tm,),
        in_specs=[col(2), col(3), col(4), blk, blk, row, row, pl.BlockSpec((B_WIDTH, B_WIDTH), lambda i: (0, 0))],
        out_specs=[blk, blk, blk], out_shape=[jax.ShapeDtypeStruct((t, B_WIDTH), BF16)] * 3,
        compiler_params=_params("parallel"),
    )(z, z, z, cos_t, sin_t, gq, gk, seg)


def _qk_prep_bwd(z, dqs, dks, dvs, cos_t, sin_t, gq, gk, seg, name):
    t = z.shape[0]
    tm = _tile(t, (256, 128))
    col = lambda c: pl.BlockSpec((tm, B_WIDTH), lambda i: (i, c))
    row = pl.BlockSpec((1, B_WIDTH), lambda i: (0, 0))
    blk = col(0)
    nb = len(dqs)

    def body(*refs):
        q_ref, k_ref = refs[0], refs[1]
        dq_refs, dk_refs, dv_refs = refs[2:2 + nb], refs[2 + nb:2 + 2 * nb], refs[2 + 2 * nb:2 + 3 * nb]
        cos_ref, sin_ref, gq_ref, gk_ref, seg_ref, dz_ref, dgq_ref, dgk_ref = refs[2 + 3 * nb:]

        @pl.when(pl.program_id(0) == 0)
        def _():
            dgq_ref[...] = jnp.zeros_like(dgq_ref)
            dgk_ref[...] = jnp.zeros_like(dgk_ref)

        def back(x, d_refs, g, dg_ref):
            dout = d_refs[0][...]
            for r_ in d_refs[1:]:
                dout = dout + r_[...]
            dy = dout * cos_ref[...] + _swap_halves(dout * sin_ref[...])
            r = lax.rsqrt(_head_sum(x * x, seg_ref[...]) * (1.0 / HEAD_DIM) + EPS)
            xn = x * r
            dg_ref[...] += _colsum(dy * xn)
            dxn = dy * g
            return r * (dxn - xn * (_head_sum(dxn * xn, seg_ref[...]) * (1.0 / HEAD_DIM)))

        dz_ref[:, 0:B_WIDTH] = back(q_ref[...].astype(F32), dq_refs, gq_ref[...], dgq_ref).astype(BF16)
        dz_ref[:, B_WIDTH:2 * B_WIDTH] = back(k_ref[...].astype(F32), dk_refs, gk_ref[...], dgk_ref).astype(BF16)
        dv = dv_refs[0][...]
        for r_ in dv_refs[1:]:
            dv = dv + r_[...]
        dz_ref[:, 2 * B_WIDTH:3 * B_WIDTH] = dv.astype(BF16)

    return pl.pallas_call(
        body, name=name, grid=(t // tm,),
        in_specs=[col(2), col(3)] + [blk] * (3 * nb) + [blk, blk, row, row, pl.BlockSpec((B_WIDTH, B_WIDTH), lambda i: (0, 0))],
        out_specs=[pl.BlockSpec((tm, 3 * B_WIDTH), lambda i: (i, 0)), row, row],
        out_shape=[jax.ShapeDtypeStruct((t, 3 * B_WIDTH), BF16), jax.ShapeDtypeStruct((1, B_WIDTH), F32),
                   jax.ShapeDtypeStruct((1, B_WIDTH), F32)],
        compiler_params=_params("arbitrary"),
    )(z, z, *dqs, *dks, *dvs, cos_t, sin_t, gq, gk, seg)


def _subseq(a, dil):
    return a.reshape(a.shape[0] // dil, dil * a.shape[1])


def _attn_fwd(q, k, v, dil, name):
    t = q.shape[0]
    nb = t // dil // Q_BLOCK
    cur = pl.BlockSpec((Q_BLOCK, B_WIDTH), lambda r, i: (i, r))
    prev = pl.BlockSpec((Q_BLOCK, B_WIDTH), lambda r, i: (jnp.maximum(i - 1, 0), r))

    def body(q_ref, kp_ref, kc_ref, vp_ref, vc_ref, o_ref, lse_ref):
        i = pl.program_id(1)
        q = q_ref[...]
        kk = jnp.concatenate([kp_ref[...], kc_ref[...]], axis=0)
        vv = jnp.concatenate([vp_ref[...], vc_ref[...]], axis=0)
        a = lax.broadcasted_iota(jnp.int32, (Q_BLOCK, 2 * Q_BLOCK), 0)
        j = lax.broadcasted_iota(jnp.int32, (Q_BLOCK, 2 * Q_BLOCK), 1)
        dist = a + Q_BLOCK - j
        mask = (dist >= 0) & (dist <= Q_BLOCK) & ((j >= Q_BLOCK) | (i > 0))
        for h in range(HEADS):
            sl = slice(h * HEAD_DIM, (h + 1) * HEAD_DIM)
            s = jnp.where(mask, _dot(q[:, sl], kk[:, sl], _NT) * (HEAD_DIM ** -0.5), NEG)
            m = jnp.max(s, axis=-1, keepdims=True)
            p = jnp.exp(s - m)
            den = jnp.sum(p, axis=-1, keepdims=True)
            o_ref[:, sl] = _dot(p.astype(BF16), vv[:, sl], _NN) / den
            lse_ref[:, sl] = jnp.broadcast_to(m + jnp.log(den), (Q_BLOCK, HEAD_DIM))

    o, lse = pl.pallas_call(
        body, name=name, grid=(dil, nb), in_specs=[cur, prev, cur, prev, cur], out_specs=[cur, cur],
        out_shape=[jax.ShapeDtypeStruct((t // dil, dil * B_WIDTH), F32)] * 2,
        compiler_params=_params("parallel", "parallel"),
    )(_subseq(q, dil), _subseq(k, dil), _subseq(k, dil), _subseq(v, dil), _subseq(v, dil))
    return o.reshape(t, B_WIDTH), lse.reshape(t, B_WIDTH)


def _attn_merge(outs, lses, name):
    t = outs[0].shape[0]
    tm = _tile(t)
    blk = pl.BlockSpec((tm, B_WIDTH), lambda i: (i, 0))
    nb = len(outs)

    def body(*refs):
        o_refs, l_refs, yb_ref, lse_ref = refs[:nb], refs[nb:2 * nb], refs[2 * nb], refs[2 * nb + 1]
        ls = [r[...] for r in l_refs]
        m = functools.reduce(jnp.maximum, ls)
        tot = m + jnp.log(sum(jnp.exp(l - m) for l in ls))
        yb_ref[...] = sum(jnp.exp(l - tot) * o[...] for l, o in zip(ls, o_refs)).astype(BF16)
        lse_ref[...] = tot

    return pl.pallas_call(
        body, name=name, grid=(t // tm,), in_specs=[blk] * (2 * nb), out_specs=[blk, blk],
        out_shape=[jax.ShapeDtypeStruct((t, B_WIDTH), BF16), jax.ShapeDtypeStruct((t, B_WIDTH), F32)],
        compiler_params=_params("parallel"),
    )(*outs, *lses)


def _attn_bwd(q, k, v, do, o, lse, dil, name):
    t = q.shape[0]
    nb = t // dil // Q_BLOCK
    blk = lambda f: pl.BlockSpec((Q_BLOCK, B_WIDTH), lambda r, i: (f(i), r))
    cur = blk(lambda i: jnp.minimum(i, nb - 1))
    prev = blk(lambda i: jnp.clip(i - 1, 0, nb - 1))
    scale = HEAD_DIM ** -0.5

    def body(q_ref, kp_ref, kc_ref, vp_ref, vc_ref, do_ref, o_ref, lse_ref, dq_ref, dk_ref, dv_ref,
             ck_ref, cv_ref, tk_ref, tv_ref):
        i = pl.program_id(1)

        @pl.when(i == 0)
        def _():
            ck_ref[...] = jnp.zeros_like(ck_ref)
            cv_ref[...] = jnp.zeros_like(cv_ref)

        @pl.when(i < nb)
        def _():
            q = q_ref[...]
            kk = jnp.concatenate([kp_ref[...], kc_ref[...]], axis=0)
            vv = jnp.concatenate([vp_ref[...], vc_ref[...]], axis=0)
            do = do_ref[...]
            dof = do.astype(F32)
            of = o_ref[...].astype(F32)
            a = lax.broadcasted_iota(jnp.int32, (Q_BLOCK, 2 * Q_BLOCK), 0)
            j = lax.broadcasted_iota(jnp.int32, (Q_BLOCK, 2 * Q_BLOCK), 1)
            dist = a + Q_BLOCK - j
            mask = (dist >= 0) & (dist <= Q_BLOCK) & ((j >= Q_BLOCK) | (i > 0))
            for h in range(HEADS):
                sl = slice(h * HEAD_DIM, (h + 1) * HEAD_DIM)
                s = jnp.where(mask, _dot(q[:, sl], kk[:, sl], _NT) * scale, NEG)
                p = jnp.exp(s - lse_ref[:, h * HEAD_DIM:h * HEAD_DIM + 1])
                dp = _dot(do[:, sl], vv[:, sl], _NT)
                delta = jnp.sum(dof[:, sl] * of[:, sl], axis=-1, keepdims=True)
                ds = (p * (dp - delta) * scale).astype(BF16)
                dq_ref[:, sl] = _dot(ds, kk[:, sl], _NN)
                dv_t = _dot(do[:, sl], p.astype(BF16), _TN)
                dk_t = _dot(q[:, sl], ds, _TN)
                tk_ref[sl, :] = ck_ref[sl, :] + dk_t[:, :Q_BLOCK]
                tv_ref[sl, :] = cv_ref[sl, :] + dv_t[:, :Q_BLOCK]
                ck_ref[sl, :] = dk_t[:, Q_BLOCK:]
                cv_ref[sl, :] = dv_t[:, Q_BLOCK:]

        @pl.when(i == nb)
        def _():
            tk_ref[...] = ck_ref[...]
            tv_ref[...] = cv_ref[...]

        @pl.when(i >= 1)
        def _():
            dk_ref[...] = tk_ref[...].T
            dv_ref[...] = tv_ref[...].T

    sub = lambda a_: _subseq(a_, dil)
    shape = jax.ShapeDtypeStruct((t // dil, dil * B_WIDTH), F32)
    dq, dk, dv = pl.pallas_call(
        body, name=name, grid=(dil, nb + 1), in_specs=[cur, prev, cur, prev, cur, cur, cur, cur],
        out_specs=[cur, prev, prev], out_shape=[shape] * 3,
        scratch_shapes=[pltpu.VMEM((B_WIDTH, Q_BLOCK), F32)] * 4,
        compiler_params=_params("parallel", "arbitrary"),
    )(sub(q), sub(k), sub(k), sub(v), sub(v), sub(do), sub(o), sub(lse))
    return dq.reshape(t, B_WIDTH), dk.reshape(t, B_WIDTH), dv.reshape(t, B_WIDTH)


FFN_TN = 256
FFN_FWD_CHUNK = 256
FFN_BWD_CHUNK = 128


def _ffn_up(h, up_t, name):
    t, k = h.shape
    tm = _tile(t)

    def body(h_ref, w_ref, o_ref):
        o_ref[...] = _dot(h_ref[...], w_ref[...], _NT).astype(BF16)

    return pl.pallas_call(
        body, name=name, grid=(2, t // tm),
        in_specs=[pl.BlockSpec((tm, k), lambda p, i: (i, 0)), pl.BlockSpec((None, FFN_DIM, k), lambda p, i: (p, 0, 0))],
        out_specs=pl.BlockSpec((None, tm, FFN_DIM), lambda p, i: (p, i, 0)),
        out_shape=jax.ShapeDtypeStruct((2, t, FFN_DIM), BF16), compiler_params=_params("parallel", "parallel"),
    )(h, up_t.reshape(2, FFN_DIM, k))


def _ffn_up_dx(du, up_t, name):
    t = du.shape[1]
    k = up_t.shape[1]
    tm = _tile(t)

    def body(a_ref, b_ref, o_ref):
        o_ref[...] = _dot(a_ref[0], b_ref[0], _NN) + _dot(a_ref[1], b_ref[1], _NN)

    return pl.pallas_call(
        body, name=name, grid=(t // tm,),
        in_specs=[pl.BlockSpec((2, tm, FFN_DIM), lambda i: (0, i, 0)), pl.BlockSpec((2, FFN_DIM, k), lambda i: (0, 0, 0))],
        out_specs=pl.BlockSpec((tm, k), lambda i: (i, 0)), out_shape=jax.ShapeDtypeStruct((t, k), F32),
        compiler_params=_params("parallel"),
    )(du, up_t.reshape(2, FFN_DIM, k))


def _ffn_conv(win, w_ref, b_ref, p):
    x = win.astype(F32)
    x0, x1, x2 = x[FFN_HALO:], pltpu.roll(x, 1, 0)[FFN_HALO:], pltpu.roll(x, 2, 0)[FFN_HALO:]
    return x0, b_ref[p] + w_ref[p, 2:3, :] * x0 + w_ref[p, 1:2, :] * x1 + w_ref[p, 0:1, :] * x2


def _zero_if(cond, v):
    return jnp.where(cond, 0, v).astype(v.dtype)


def _ffn_act(u, dw_w, dw_b, name):
    t = u.shape[1]
    tm = _tile(t)
    chunk = min(FFN_FWD_CHUNK, tm)
    hb = tm // FFN_HALO
    main = pl.BlockSpec((2, tm, FFN_TN), lambda i, j: (0, i, j))
    halo = pl.BlockSpec((2, FFN_HALO, FFN_TN), lambda i, j: (0, jnp.maximum(i * hb - 1, 0), j))
    wsp = pl.BlockSpec((2, FFN_CONV_WIDTH, FFN_TN), lambda i, j: (0, 0, j))
    bsp = pl.BlockSpec((2, 1, FFN_TN), lambda i, j: (0, 0, j))

    def body(u_ref, uh_ref, w_ref, b_ref, o_ref):
        first = pl.program_id(0) == 0

        def emit(rows, wins):
            za, zb = _ffn_conv(wins[0], w_ref, b_ref, 0)[1], _ffn_conv(wins[1], w_ref, b_ref, 1)[1]
            o_ref[rows, :] = (za * _sigmoid(za) * zb).astype(BF16)

        emit(pl.ds(0, chunk), [jnp.concatenate([_zero_if(first, uh_ref[p]), u_ref[p, 0:chunk, :]], axis=0) for p in range(2)])

        def step(c, carry):
            s = pl.multiple_of(c * chunk, chunk)
            emit(pl.ds(s, chunk), [u_ref[p, pl.ds(s - FFN_HALO, chunk + FFN_HALO), :] for p in range(2)])
            return carry

        lax.fori_loop(1, tm // chunk, step, 0)

    return pl.pallas_call(
        body, name=name, grid=(t // tm, FFN_DIM // FFN_TN), in_specs=[main, halo, wsp, bsp],
        out_specs=pl.BlockSpec((tm, FFN_TN), lambda i, j: (i, j)), out_shape=jax.ShapeDtypeStruct((t, FFN_DIM), BF16),
        compiler_params=_params("parallel", "parallel"),
    )(u, u, dw_w, dw_b)


def _fold8(v):
    return jnp.sum(v.reshape(v.shape[0] // 8, 8, v.shape[1]), axis=0)


def _ffn_act_bwd(u, dact, dw_w, dw_b, name):
    t = u.shape[1]
    tm = _tile(t)
    chunk = min(FFN_BWD_CHUNK, tm // 2)
    halo = FFN_HALO
    hb = tm // halo
    nt = t // tm
    last_halo = t // halo - 1
    prev_i = lambda i: jnp.maximum(i * hb - 1, 0)
    next_i = lambda i: jnp.minimum((i + 1) * hb, last_halo)
    main = pl.BlockSpec((2, tm, FFN_TN), lambda j, i: (0, i, j))
    prev = pl.BlockSpec((2, halo, FFN_TN), lambda j, i: (0, prev_i(i), j))
    nxt = pl.BlockSpec((2, halo, FFN_TN), lambda j, i: (0, next_i(i), j))
    wsp = pl.BlockSpec((2, FFN_CONV_WIDTH, FFN_TN), lambda j, i: (0, 0, j))
    bsp = pl.BlockSpec((2, 1, FFN_TN), lambda j, i: (0, 0, j))

    def body(u_ref, up_ref, un_ref, da_ref, dan_ref, w_ref, b_ref, du_ref, dw_ref, db_ref, acc_ref):
        i = pl.program_id(1)
        first, last = i == 0, i == nt - 1
        acc_ref[...] = jnp.zeros_like(acc_ref)

        def emit(rows, wins, dact):
            n = chunk + halo
            (ua, za), (ub, zb) = _ffn_conv(wins[0], w_ref, b_ref, 0), _ffn_conv(wins[1], w_ref, b_ref, 1)
            dact = dact.astype(F32)
            sg = _sigmoid(za)
            dzs = (dact * zb * (sg * (1.0 + za * (1.0 - sg))), dact * (za * sg))
            for p, (dz, um) in enumerate(zip(dzs, (ua, ub))):
                ahead = (dz[:chunk], pltpu.roll(dz, n - 1, 0)[:chunk], pltpu.roll(dz, n - 2, 0)[:chunk])
                um = um[:chunk]
                acc_ref[p, FFN_CONV_WIDTH] += _fold8(ahead[0])
                du = None
                for j, dzj in enumerate(ahead):
                    k = FFN_CONV_WIDTH - 1 - j
                    acc_ref[p, k] += _fold8(dzj * um)
                    term = w_ref[p, k:k + 1, :] * dzj
                    du = term if du is None else du + term
                du_ref[p, rows, :] = du.astype(BF16)

        emit(pl.ds(0, chunk),
             [jnp.concatenate([_zero_if(first, up_ref[p]), u_ref[p, 0:chunk + halo, :]], axis=0) for p in range(2)],
             da_ref[0:chunk + halo, :])

        def step(c, carry):
            s = pl.multiple_of(c * chunk, chunk)
            emit(pl.ds(s, chunk), [u_ref[p, pl.ds(s - halo, chunk + 2 * halo), :] for p in range(2)],
                 da_ref[pl.ds(s, chunk + halo), :])
            return carry

        lax.fori_loop(1, tm // chunk - 1, step, 0)
        s = tm - chunk
        emit(pl.ds(s, chunk),
             [jnp.concatenate([u_ref[p, s - halo:tm, :], _zero_if(last, un_ref[p])], axis=0) for p in range(2)],
             jnp.concatenate([da_ref[s:tm, :], _zero_if(last, dan_ref[...])], axis=0))

        @pl.when(i == 0)
        def _():
            dw_ref[...] = jnp.zeros_like(dw_ref)
            db_ref[...] = jnp.zeros_like(db_ref)

        for p in range(2):
            for k in range(FFN_CONV_WIDTH):
                dw_ref[p, k:k + 1, :] += _colsum(acc_ref[p, k])
            db_ref[p] += _colsum(acc_ref[p, FFN_CONV_WIDTH])

    return pl.pallas_call(
        body, name=name, grid=(FFN_DIM // FFN_TN, nt),
        in_specs=[main, prev, nxt, pl.BlockSpec((tm, FFN_TN), lambda j, i: (i, j)),
                  pl.BlockSpec((halo, FFN_TN), lambda j, i: (next_i(i), j)), wsp, bsp],
        out_specs=[main, wsp, bsp],
        out_shape=[jax.ShapeDtypeStruct((2, t, FFN_DIM), BF16), jax.ShapeDtypeStruct((2, FFN_CONV_WIDTH, FFN_DIM), F32),
                   jax.ShapeDtypeStruct((2, 1, FFN_DIM), F32)],
        scratch_shapes=[pltpu.VMEM((2, FFN_CONV_WIDTH + 1, 8, FFN_TN), F32)],
        compiler_params=_params("parallel", "arbitrary"),
    )(u, u, u, dact, dact, dw_w, dw_b)


def _conv_transpose(dz, w, width, halo_rows, tn, name):
    t, n = dz.shape
    tm = _tile(t, (256, 128))
    hb = tm // halo_rows
    last_halo = t // halo_rows - 1
    nt = t // tm
    main = pl.BlockSpec((tm, tn), lambda i, j: (i, j))
    nxt = pl.BlockSpec((halo_rows, tn), lambda i, j: (jnp.minimum((i + 1) * hb, last_halo), j))

    def body(dz_ref, dzn_ref, w_ref, du_ref, scr_ref):
        scr_ref[0:tm, :] = dz_ref[...].astype(F32)
        scr_ref[tm:, :] = jnp.where(pl.program_id(0) == nt - 1, 0.0, dzn_ref[...].astype(F32))
        acc = w_ref[width - 1:width, :] * scr_ref[pl.ds(0, tm), :]
        for k in range(width - 1):
            acc = acc + w_ref[k:k + 1, :] * scr_ref[pl.ds(width - 1 - k, tm), :]
        du_ref[...] = acc.astype(du_ref.dtype)

    return pl.pallas_call(
        body, name=name, grid=(nt, n // tn),
        in_specs=[main, nxt, pl.BlockSpec((width, tn), lambda i, j: (0, j))], out_specs=main,
        out_shape=jax.ShapeDtypeStruct((t, n), BF16 if dz.dtype == BF16 else F32),
        scratch_shapes=[pltpu.VMEM((tm + halo_rows, tn), F32)],
        compiler_params=_params("parallel", "parallel"),
    )(dz, dz, w)


CONV_TM = 256


def _glu_window(pa_ref, pah_ref, pg_ref, pgh_ref, scr_ref, first):
    ah, gh = pah_ref[...].astype(F32), pgh_ref[...].astype(F32)
    scr_ref[0:CONV_HALO, :] = jnp.where(first, 0.0, ah * _sigmoid(gh))
    scr_ref[CONV_HALO:, :] = pa_ref[...].astype(F32) * _sigmoid(pg_ref[...].astype(F32))


def _dw_conv31(scr_ref, w_ref, b_ref, tm):
    acc = b_ref[...] + w_ref[CONV_WIDTH - 1:CONV_WIDTH, :] * scr_ref[pl.ds(CONV_HALO, tm), :]
    for k in range(CONV_WIDTH - 1):
        acc = acc + w_ref[k:k + 1, :] * scr_ref[pl.ds(CONV_HALO - (CONV_WIDTH - 1) + k, tm), :]
    return acc


def _conformer_specs(t):
    tm = _tile(t, (CONV_TM, 128))
    hb = tm // CONV_HALO
    d = D_MODEL
    main = lambda c: pl.BlockSpec((tm, d), lambda i: (i, c))
    halo = lambda c: pl.BlockSpec((CONV_HALO, d), lambda i: (jnp.maximum(i * hb - 1, 0), c))
    row = pl.BlockSpec((1, d), lambda i: (0, 0))
    wsp = pl.BlockSpec((CONV_WIDTH, d), lambda i: (0, 0))
    return tm, main, halo, row, wsp


def _conformer_mid(p, dw_w, dw_b, ln_g, ln_b, name):
    t = p.shape[0]
    tm, main, halo, row, wsp = _conformer_specs(t)

    def body(pa_ref, pah_ref, pg_ref, pgh_ref, w_ref, b_ref, g_ref, lb_ref, o_ref, scr_ref):
        _glu_window(pa_ref, pah_ref, pg_ref, pgh_ref, scr_ref, pl.program_id(0) == 0)
        dc = _dw_conv31(scr_ref, w_ref, b_ref, tm)
        mu = jnp.mean(dc, axis=-1, keepdims=True)
        xc = dc - mu
        ln = xc * lax.rsqrt(jnp.mean(xc * xc, axis=-1, keepdims=True) + EPS) * g_ref[...] + lb_ref[...]
        o_ref[...] = (ln * _sigmoid(ln)).astype(BF16)

    return pl.pallas_call(
        body, name=name, grid=(t // tm,), in_specs=[main(0), halo(0), main(1), halo(1), wsp, row, row, row],
        out_specs=main(0), out_shape=jax.ShapeDtypeStruct((t, D_MODEL), BF16),
        scratch_shapes=[pltpu.VMEM((tm + CONV_HALO, D_MODEL), F32)], compiler_params=_params("parallel"),
    )(p, p, p, p, dw_w, dw_b, ln_g, ln_b)


def _conformer_mid_bwd(p, ds, dw_w, dw_b, ln_g, ln_b, name):
    t = p.shape[0]
    tm, main, halo, row, wsp = _conformer_specs(t)

    def body(pa_ref, pah_ref, pg_ref, pgh_ref, ds_ref, w_ref, b_ref, g_ref, lb_ref,
             ddc_ref, dw_ref, db_ref, dg_ref, dlb_ref, scr_ref):
        @pl.when(pl.program_id(0) == 0)
        def _():
            for r_ in (dw_ref, db_ref, dg_ref, dlb_ref):
                r_[...] = jnp.zeros_like(r_)

        _glu_window(pa_ref, pah_ref, pg_ref, pgh_ref, scr_ref, pl.program_id(0) == 0)
        dc = _dw_conv31(scr_ref, w_ref, b_ref, tm)
        mu = jnp.mean(dc, axis=-1, keepdims=True)
        xc = dc - mu
        rstd = lax.rsqrt(jnp.mean(xc * xc, axis=-1, keepdims=True) + EPS)
        xhat = xc * rstd
        ln = xhat * g_ref[...] + lb_ref[...]
        sg = _sigmoid(ln)
        dln = ds_ref[...].astype(F32) * (sg * (1.0 + ln * (1.0 - sg)))
        dg_ref[...] += _colsum(dln * xhat)
        dlb_ref[...] += _colsum(dln)
        dxh = dln * g_ref[...]
        ddc = rstd * (dxh - jnp.mean(dxh, axis=-1, keepdims=True) - xhat * jnp.mean(dxh * xhat, axis=-1, keepdims=True))
        ddc_ref[...] = ddc
        db_ref[...] += _colsum(ddc)
        for k in range(CONV_WIDTH):
            dw_ref[k:k + 1, :] += _colsum(ddc * scr_ref[pl.ds(CONV_HALO - (CONV_WIDTH - 1) + k, tm), :])

    return pl.pallas_call(
        body, name=name, grid=(t // tm,), in_specs=[main(0), halo(0), main(1), halo(1), main(0), wsp, row, row, row],
        out_specs=[main(0), wsp, row, row, row],
        out_shape=[jax.ShapeDtypeStruct((t, D_MODEL), F32), jax.ShapeDtypeStruct((CONV_WIDTH, D_MODEL), F32)]
        + [jax.ShapeDtypeStruct((1, D_MODEL), F32)] * 3,
        scratch_shapes=[pltpu.VMEM((tm + CONV_HALO, D_MODEL), F32)], compiler_params=_params("arbitrary"),
    )(p, p, p, p, ds, dw_w, dw_b, ln_g, ln_b)


def _glu_bwd(p, dglu, name):
    t = p.shape[0]
    d = D_MODEL
    tm = _tile(t)
    col = lambda c: pl.BlockSpec((tm, d), lambda i: (i, c))
    wide = pl.BlockSpec((tm, 2 * d), lambda i: (i, 0))
    row = pl.BlockSpec((1, 2 * d), lambda i: (0, 0))

    def body(pa_ref, pg_ref, dglu_ref, dp_ref, db_ref):
        @pl.when(pl.program_id(0) == 0)
        def _():
            db_ref[...] = jnp.zeros_like(db_ref)

        a, g, dglu = pa_ref[...].astype(F32), pg_ref[...].astype(F32), dglu_ref[...].astype(F32)
        sg = _sigmoid(g)
        da = (dglu * sg).astype(BF16)
        dg = (dglu * a * sg * (1.0 - sg)).astype(BF16)
        dp_ref[:, 0:d] = da
        dp_ref[:, d:2 * d] = dg
        db_ref[:, 0:d] += _colsum(da.astype(F32))
        db_ref[:, d:2 * d] += _colsum(dg.astype(F32))

    return pl.pallas_call(
        body, name=name, grid=(t // tm,), in_specs=[col(0), col(1), col(0)], out_specs=[wide, row],
        out_shape=[jax.ShapeDtypeStruct((t, 2 * d), BF16), jax.ShapeDtypeStruct((1, 2 * d), F32)],
        compiler_params=_params("arbitrary"),
    )(p, p, dglu)


def _colsum_call(a, name):
    t, n = a.shape
    tm = _tile(t)

    def body(a_ref, o_ref):
        @pl.when(pl.program_id(0) == 0)
        def _():
            o_ref[...] = jnp.zeros_like(o_ref)

        o_ref[...] += _colsum(a_ref[...].astype(F32))

    return pl.pallas_call(
        body, name=name, grid=(t // tm,), in_specs=[pl.BlockSpec((tm, n), lambda i: (i, 0))],
        out_specs=pl.BlockSpec((1, n), lambda i: (0, 0)), out_shape=jax.ShapeDtypeStruct((1, n), F32),
        compiler_params=_params("arbitrary"),
    )(a)


def _ada_fwd(c_all, w, name):
    rows, d = c_all.shape
    n = w.shape[1]
    tn = _tile(n, (256, 128))

    def body(c_ref, w_ref, o_ref):
        c = c_ref[...]
        o_ref[...] = _dot((c * _sigmoid(c)).astype(BF16), w_ref[...].astype(BF16), _NN)

    return pl.pallas_call(
        body, name=name, grid=(n // tn,),
        in_specs=[pl.BlockSpec((rows, d), lambda j: (0, 0)), pl.BlockSpec((d, tn), lambda j: (0, j))],
        out_specs=pl.BlockSpec((rows, tn), lambda j: (0, j)), out_shape=jax.ShapeDtypeStruct((rows, n), F32),
        compiler_params=_params("parallel"),
    )(c_all, w)


def _ada_bwd(c_all, dmod, name):
    rows, d = c_all.shape
    n = dmod.shape[1]
    tn = _tile(n, (256, 128))

    def body(c_ref, g_ref, o_ref):
        c = c_ref[...]
        o_ref[...] = _dot((c * _sigmoid(c)).astype(BF16), g_ref[...].astype(BF16), _TN)

    return pl.pallas_call(
        body, name=name, grid=(n // tn,),
        in_specs=[pl.BlockSpec((rows, d), lambda j: (0, 0)), pl.BlockSpec((rows, tn), lambda j: (0, j))],
        out_specs=pl.BlockSpec((d, tn), lambda j: (0, j)), out_shape=jax.ShapeDtypeStruct((d, n), F32),
        compiler_params=_params("parallel"),
    )(c_all, dmod)


def _sum_slots(a, name):
    s, r, c = a.shape
    tr = _row_tile(r, 256)

    def body(a_ref, o_ref):
        acc = a_ref[0].astype(F32)
        for k in range(1, s):
            acc = acc + a_ref[k].astype(F32)
        o_ref[...] = acc

    return pl.pallas_call(
        body, name=name, grid=(r // tr,), in_specs=[pl.BlockSpec((s, tr, c), lambda i: (0, i, 0))],
        out_specs=pl.BlockSpec((tr, c), lambda i: (i, 0)), out_shape=jax.ShapeDtypeStruct((r, c), F32),
        compiler_params=_params("parallel"),
    )(a)


def _adamw(w, g, m, v, name):
    r, c = w.shape
    tr = _row_tile(r, 256)
    blk = pl.BlockSpec((tr, c), lambda i: (i, 0))
    c1 = 1.0 / (1.0 - ADAM_B1 ** ADAM_STEP)
    c2 = 1.0 / (1.0 - ADAM_B2 ** ADAM_STEP)

    def body(w_ref, g_ref, m_ref, v_ref, d_ref, nm_ref, nv_ref):
        g_ = g_ref[...]
        nm = ADAM_B1 * m_ref[...] + (1.0 - ADAM_B1) * g_
        nv = ADAM_B2 * v_ref[...] + (1.0 - ADAM_B2) * (g_ * g_)
        d_ref[...] = -ADAM_LR * ((nm * c1) / (jnp.sqrt(nv * c2) + ADAM_EPS) + ADAM_WD * w_ref[...])
        nm_ref[...] = nm
        nv_ref[...] = nv

    return pl.pallas_call(
        body, name=name, grid=(r // tr,), in_specs=[blk] * 4, out_specs=[blk] * 3,
        out_shape=[jax.ShapeDtypeStruct((r, c), F32)] * 3, compiler_params=_params("parallel"),
    )(w, g, m, v)


def _mesh_pos():
    return lax.axis_index("x"), lax.axis_index("y"), lax.axis_index("c")


def _all_gather_vmem(x_shard, name):
    m_per, n = x_shard.shape

    def body(x_ref, out_ref, send_sems, recv_sems, local_sem):
        x, y, c = _mesh_pos()
        me, sibling = (x, y, c), (x, y, 1 - c)
        chips = [(1 - x, y), (x, 1 - y), (1 - x, 1 - y)]

        def rows(px, py, pc):
            return out_ref.at[pl.ds((4 * px + 2 * py + pc) * m_per, m_per), :]

        def copy(k, block, to, src=None):
            return pltpu.make_async_remote_copy(
                src_ref=rows(*block) if src is None else src, dst_ref=rows(*block),
                send_sem=send_sems.at[k], recv_sem=recv_sems.at[k], device_id=to, device_id_type=MESH)

        mine = pltpu.make_async_copy(x_ref, rows(*me), local_sem)
        mine.start()
        first = [copy(0, me, sibling, src=x_ref)]
        first += [copy(1 + j, me, (*chip, c), src=x_ref) for j, chip in enumerate(chips)]
        for cp in first:
            cp.start()
        passed = [copy(4 + j, (*chip, c), sibling) for j, chip in enumerate(chips)]
        for j, chip in enumerate(chips):
            copy(1 + j, (*chip, c), me).wait_recv()
            passed[j].start()
        copy(0, sibling, me).wait_recv()
        for j, chip in enumerate(chips):
            copy(4 + j, (*chip, 1 - c), me).wait_recv()
        for cp in first + passed:
            cp.wait_send()
        mine.wait()

    return pl.pallas_call(
        body, name=name, out_shape=jax.ShapeDtypeStruct((N_DEV * m_per, n), x_shard.dtype),
        in_specs=[pl.BlockSpec(memory_space=pltpu.VMEM)], out_specs=pl.BlockSpec(memory_space=pltpu.VMEM),
        scratch_shapes=[pltpu.SemaphoreType.DMA((7,)), pltpu.SemaphoreType.DMA((7,)), pltpu.SemaphoreType.DMA],
    )(x_shard)


def _all_gather_hbm(shards, name):
    n = len(shards)
    out_shape = [jax.ShapeDtypeStruct((N_DEV,) + s.shape, s.dtype) for s in shards]

    def body(*refs):
        x_refs, out_refs = refs[:n], refs[n:2 * n]
        send_sems, recv_sems, local_sems = refs[2 * n:]
        x, y, c = _mesh_pos()
        me, sibling = (x, y, c), (x, y, 1 - c)
        chips = [(1 - x, y), (x, 1 - y), (1 - x, 1 - y)]

        def blk(a, p):
            return out_refs[a].at[4 * p[0] + 2 * p[1] + p[2]]

        def copy(a, k, block, to, src=None):
            return pltpu.make_async_remote_copy(
                src_ref=blk(a, block) if src is None else src, dst_ref=blk(a, block),
                send_sem=send_sems.at[7 * a + k], recv_sem=recv_sems.at[7 * a + k], device_id=to, device_id_type=MESH)

        mine = [pltpu.make_async_copy(x_refs[a], blk(a, me), local_sems.at[a]) for a in range(n)]
        for cp in mine:
            cp.start()
        first = []
        for a in range(n):
            first.append(copy(a, 0, me, sibling, src=x_refs[a]))
            first += [copy(a, 1 + j, me, (*chip, c), src=x_refs[a]) for j, chip in enumerate(chips)]
        for cp in first:
            cp.start()
        passed = []
        for j, chip in enumerate(chips):
            for a in range(n):
                copy(a, 1 + j, (*chip, c), me).wait_recv()
                fwd = copy(a, 4 + j, (*chip, c), sibling)
                fwd.start()
                passed.append(fwd)
        for a in range(n):
            copy(a, 0, sibling, me).wait_recv()
            for j, chip in enumerate(chips):
                copy(a, 4 + j, (*chip, 1 - c), me).wait_recv()
        for cp in first + passed:
            cp.wait_send()
        for cp in mine:
            cp.wait()

    any_spec = pl.BlockSpec(memory_space=pl.ANY)
    return pl.pallas_call(
        body, name=name, out_shape=out_shape, in_specs=[any_spec] * n, out_specs=[any_spec] * n,
        scratch_shapes=[pltpu.SemaphoreType.DMA((7 * n,)), pltpu.SemaphoreType.DMA((7 * n,)), pltpu.SemaphoreType.DMA((n,))],
    )(*shards)


def _peers(x, y, c):
    flip = lambda v, f: 1 - v if f else v
    return [(flip(x, m & 4), flip(y, m & 2), flip(c, m & 1)) for m in range(1, N_DEV)]


def _dev_index(p):
    return 4 * p[0] + 2 * p[1] + p[2]


def _push_copies(src_refs, land_refs, send_sems, recv_sems, scatter, receive):
    x, y, c = _mesh_pos()
    me = _dev_index((x, y, c))
    copies = []
    for a, (src, land) in enumerate(zip(src_refs, land_refs)):
        for k, p in enumerate(_peers(x, y, c)):
            copies.append(pltpu.make_async_remote_copy(
                src_ref=src.at[_dev_index(p)] if scatter else src, dst_ref=land.at[_dev_index(p) if receive else me],
                send_sem=send_sems.at[7 * a + k], recv_sem=recv_sems.at[7 * a + k], device_id=p, device_id_type=MESH))
    return copies


_HBM = pl.BlockSpec(memory_space=pltpu.HBM)
_SEM = pl.BlockSpec(memory_space=pltpu.SEMAPHORE)
_EFFECT = pltpu.SideEffectType.DATAFLOW_SIDE_EFFECTING


def _pushes_start(srcs, lands, scatter, name):
    n = len(srcs)

    def body(*refs):
        src_refs, land_refs = refs[:n], refs[n:2 * n]
        send_sems, recv_sems = refs[2 * n], refs[2 * n + 1]
        token = refs[-1]
        for cp in _push_copies(src_refs, land_refs, send_sems, recv_sems, scatter, receive=False):
            cp.start()
        token[...] = jnp.zeros_like(token)

    hbm = lambda a: pltpu.HBM(a.shape, a.dtype)
    sems = pltpu.SemaphoreType.DMA((7 * n,))
    outs = pl.pallas_call(
        body, name=name,
        out_shape=(sems, sems, *[hbm(a) for a in srcs], *[hbm(a) for a in lands], jax.ShapeDtypeStruct((8, 128), F32)),
        in_specs=[_HBM] * (2 * n), out_specs=(_SEM, _SEM, *[_HBM] * (2 * n), pl.BlockSpec(memory_space=pltpu.VMEM)),
        input_output_aliases={i: 2 + i for i in range(2 * n)},
        compiler_params=pltpu.CompilerParams(has_side_effects=_EFFECT),
    )(*[pltpu.with_memory_space_constraint(a, pltpu.HBM) for a in (*srcs, *lands)])
    return (outs[0], outs[1], outs[2:2 + n], outs[2 + n:2 + 2 * n], scatter), outs[-1]


def _pushes_wait(handle, after, name):
    send_sems, recv_sems, srcs, lands, scatter = handle
    n = len(srcs)

    def body(*refs):
        src_refs, land_refs = refs[:n], refs[n:2 * n]
        for cp in _push_copies(src_refs, land_refs, refs[2 * n], refs[2 * n + 1], scatter, receive=True):
            cp.wait_send()
            cp.wait_recv()

    hbm = lambda a: pltpu.HBM(a.shape, a.dtype)
    outs = pl.pallas_call(
        body, name=name, out_shape=tuple(hbm(a) for a in (*srcs, *lands)),
        in_specs=[_HBM] * (2 * n) + [_SEM, _SEM, pl.BlockSpec(memory_space=pl.ANY)], out_specs=tuple([_HBM] * (2 * n)),
        input_output_aliases={i: i for i in range(2 * n)},
        compiler_params=pltpu.CompilerParams(has_side_effects=_EFFECT),
    )(*srcs, *lands, send_sems, recv_sems, after)
    return outs[n:]


def _own_block_placed(block, rows_shape, dtype, me):
    return lax.dynamic_update_slice(lax.empty((N_DEV,) + rows_shape, dtype), block[None], (me, 0, 0))


def _ffn_forward(x, mod, norm_g, w, tag):
    sh, sc, gate = mod
    h = _modnorm(x, norm_g, sc, sh, f"{tag}_norm")
    u = _ffn_up(h, w["up_t"], f"{tag}_up")
    act = _ffn_act(u, w["dw_w"], w["dw_b"], f"{tag}_act")
    y, x_new = _matmul(act, w["down"], "nn", F32, f"{tag}_down", resid=(x, gate))
    return x_new, (x, h, u, act, y)


def _ffn_backward(dx_new, saved, mod, norm_g, w, tag):
    x, h, u, act, y = saved
    _, sc, gate = mod
    dy, d_gate = _gate_bwd(dx_new, y, gate, f"{tag}_gate_bwd")
    d_down = _matmul_tn_acc(act, dy, f"{tag}_down_dw")
    dact = _matmul(dy, w["down"], "nt", BF16, f"{tag}_down_dx")
    du, d_dw_w, d_dw_b = _ffn_act_bwd(u, dact, w["dw_w"], w["dw_b"], f"{tag}_act_bwd")
    d_up_t = _matmul_tn_acc(du, h, f"{tag}_up_dw").reshape(2 * FFN_DIM, -1)
    dh = _ffn_up_dx(du, w["up_t"], f"{tag}_up_dx")
    dx, d_w, d_sh = _modnorm_bwd(x, dh, norm_g, sc, dx_new, f"{tag}_norm_bwd")
    return dx, dict(up_t=d_up_t, down=d_down, dw_w=d_dw_w.transpose(1, 0, 2).reshape(FFN_CONV_WIDTH, 2 * FFN_DIM),
                    dw_b=d_dw_b.reshape(1, 2 * FFN_DIM), norm_g=d_w * (1.0 + sc), sh=d_sh, sc=d_w * norm_g, gate=d_gate)


def _mixer_forward(x, mod, norm_g, w, rope, tag):
    sh, sc, gate = mod
    h = _modnorm(x, norm_g, sc, sh, f"{tag}_norm")
    z = _matmul(h, w["w_in_t"], "nt", BF16, f"{tag}_in")
    ya = _gmlp_fwd(z, w["gain"], w["wtril"], w["bias_exp"], f"{tag}_gmlp")
    q, k, v = _qk_prep(z, rope[0], rope[1], w["gq"], w["gk"], w["seg"], f"{tag}_qk")
    outs, lses = [], []
    for _, dil in PATTERNS:
        o, l = _attn_fwd(q, k, v, dil, f"{tag}_attn_d{dil}")
        outs.append(o)
        lses.append(l)
    yb, lse = _attn_merge(outs, lses, f"{tag}_merge")
    cat = jnp.concatenate([ya, yb], axis=1)
    y, x_new = _matmul(cat, w["w_out"], "nn", F32, f"{tag}_out", resid=(x, gate))
    return x_new, (x, h, z, q, k, v, yb, lse, cat, y)


def _mixer_backward(dx_new, saved, mod, norm_g, w, rope, tag):
    x, h, z, q, k, v, yb, lse, cat, y = saved
    _, sc, gate = mod
    dy, d_gate = _gate_bwd(dx_new, y, gate, f"{tag}_gate_bwd")
    d_w_out = _matmul_tn_acc(cat, dy, f"{tag}_out_dw")
    dcat = _matmul(dy, w["w_out"], "nt", BF16, f"{tag}_out_dx")
    dz_a, d_sp_w, d_gain, d_bias_exp = _gmlp_bwd(z, dcat, w["gain"], w["wtril"], w["wtril_t"], w["bias_exp"], f"{tag}_gmlp_bwd")
    dyb = dcat[:, A_WIDTH:]
    dqs, dks, dvs = [], [], []
    for _, dil in PATTERNS:
        dq, dk, dv = _attn_bwd(q, k, v, dyb, yb, lse, dil, f"{tag}_attn_bwd_d{dil}")
        dqs.append(dq)
        dks.append(dk)
        dvs.append(dv)
    dz_qkv, d_gq, d_gk = _qk_prep_bwd(z, dqs, dks, dvs, rope[0], rope[1], w["gq"], w["gk"], w["seg"], f"{tag}_qk_bwd")
    dz = jnp.concatenate([dz_a, dz_qkv], axis=1)
    d_w_in_t = _matmul_tn_acc(dz, h, f"{tag}_in_dw")
    dh = _matmul(dz, w["w_in_t"], "nn", F32, f"{tag}_in_dx")
    dx, d_w, d_sh = _modnorm_bwd(x, dh, norm_g, sc, dx_new, f"{tag}_norm_bwd")
    return dx, dict(
        w_in_t=d_w_in_t, w_out=d_w_out, vnorm_g=d_gain.reshape(A_GROUPS, GROUP_DIM), spatial_w=d_sp_w,
        spatial_b=d_bias_exp.reshape(CHUNK, A_GROUPS, GROUP_DIM).sum(-1).T,
        q_norm_g=d_gq.reshape(HEADS, HEAD_DIM).sum(0), k_norm_g=d_gk.reshape(HEADS, HEAD_DIM).sum(0),
        norm_g=d_w * (1.0 + sc), sh=d_sh, sc=d_w * norm_g, gate=d_gate)


def _conformer_forward(x, mod, norm_g, w, tag):
    sh, sc, gate = mod
    h = _modnorm(x, norm_g, sc, sh, f"{tag}_norm")
    p = _matmul(h, w["pw1_t"], "nt", BF16, f"{tag}_pw1", bias=w["pw1_b"])
    s = _conformer_mid(p, w["dw_w"], w["dw_b"], w["ln_g"], w["ln_b"], f"{tag}_mid")
    y, x_new = _matmul(s, w["pw2"], "nn", F32, f"{tag}_pw2", bias=w["pw2_b"], resid=(x, gate))
    return x_new, (x, h, p, s, y)


def _conformer_backward(dx_new, saved, mod, norm_g, w, tag):
    x, h, p, s, y = saved
    _, sc, gate = mod
    dy, d_gate = _gate_bwd(dx_new, y, gate, f"{tag}_gate_bwd")
    d_pw2 = _matmul_tn_acc(s, dy, f"{tag}_pw2_dw")
    d_pw2_b = _colsum_call(dy, f"{tag}_pw2_db")
    ds = _matmul(dy, w["pw2"], "nt", BF16, f"{tag}_pw2_dx")
    ddc, d_dw_w, d_dw_b, d_ln_g, d_ln_b = _conformer_mid_bwd(p, ds, w["dw_w"], w["dw_b"], w["ln_g"], w["ln_b"], f"{tag}_mid_bwd")
    dglu = _conv_transpose(ddc, w["dw_w"], CONV_WIDTH, CONV_HALO, 512, f"{tag}_conv_bwd")
    dp, d_pw1_b = _glu_bwd(p, dglu, f"{tag}_glu_bwd")
    d_pw1_t = _matmul_tn_acc(dp, h, f"{tag}_pw1_dw")
    dh = _matmul(dp, w["pw1_t"], "nn", F32, f"{tag}_pw1_dx")
    dx, d_w, d_sh = _modnorm_bwd(x, dh, norm_g, sc, dx_new, f"{tag}_norm_bwd")
    return dx, dict(pw1_t=d_pw1_t, pw1_b=d_pw1_b, dw_w=d_dw_w, dw_b=d_dw_b, ln_g=d_ln_g, ln_b=d_ln_b, pw2=d_pw2,
                    pw2_b=d_pw2_b, norm_g=d_w * (1.0 + sc), sh=d_sh, sc=d_w * norm_g, gate=d_gate)


def _local_step(x, target, pos, mod, norm_mix_g, norm_ffn_g, mixer_w, conv_w, ffn_w, fetch, emit):
    d = D_MODEL

    def tied(m, token):
        return m if token is None else (m[0], m[1], m[2] + token[0:1, 0:1])

    inv_freq = 1.0 / (ROPE_THETA ** (jnp.arange(0, HEAD_DIM, 2, dtype=F32) / HEAD_DIM))
    inv_freq = jnp.tile(inv_freq, 2 * HEADS)[None, :]
    sign = jnp.tile(jnp.concatenate([-jnp.ones(HEAD_DIM // 2, F32), jnp.ones(HEAD_DIM // 2, F32)]), HEADS)[None, :]
    rope = _rope_tables(pos, inv_freq, sign, "rope_tables")
    mods = [[mod[l:l + 1, i * d:(i + 1) * d] for i in range(6)] for l in range(2)]
    mix = [(m[0], m[1], m[2]) for m in mods]
    ffn = [(m[3], m[4], m[5]) for m in mods]
    gm = [norm_mix_g[l:l + 1] for l in range(2)]
    gf = [norm_ffn_g[l:l + 1] for l in range(2)]

    mixer_w = {**mixer_w, **fetch("l0_mix", x)}
    x1, s_mix = _mixer_forward(x, mix[0], gm[0], mixer_w, rope, "l0_mix")
    ffn_w0 = {**ffn_w[0], **fetch("l0_ffn", x1)}
    x2, s_ffn0 = _ffn_forward(x1, ffn[0], gf[0], ffn_w0, "l0_ffn")
    conv_w = {**conv_w, **fetch("l1_conv", x2)}
    x3, s_conv = _conformer_forward(x2, mix[1], gm[1], conv_w, "l1_conv")
    ffn_w1 = {**ffn_w[1], **fetch("l1_ffn", x3)}
    x4, s_ffn1 = _ffn_forward(x3, ffn[1], gf[1], ffn_w1, "l1_ffn")
    dx, loss = _loss_head(x4, target, "loss_head")
    dx, g_ffn1 = _ffn_backward(dx, s_ffn1, ffn[1], gf[1], ffn_w1, "l1_ffn")
    token = emit("l1_ffn", [g_ffn1.pop("up_t"), g_ffn1.pop("down")])
    dx, g_conv = _conformer_backward(dx, s_conv, tied(mix[1], token), gm[1], conv_w, "l1_conv")
    token = emit("l1_conv", [g_conv.pop("pw1_t"), g_conv.pop("pw2")])
    dx, g_ffn0 = _ffn_backward(dx, s_ffn0, tied(ffn[0], token), gf[0], ffn_w0, "l0_ffn")
    token = emit("l0_ffn", [g_ffn0.pop("up_t"), g_ffn0.pop("down")])
    dx, g_mix = _mixer_backward(dx, s_mix, tied(mix[0], token), gm[0], mixer_w, rope, "l0_mix")
    emit("l0_mix", [g_mix.pop("w_in_t"), g_mix.pop("w_out")])
    blocks = [g_mix, g_ffn0, g_conv, g_ffn1]
    dmod = jnp.stack([jnp.concatenate([a["sh"], a["sc"], a["gate"], b["sh"], b["sc"], b["gate"]], axis=1)[0]
                      for a, b in ((g_mix, g_ffn0), (g_conv, g_ffn1))])
    return loss, dx, dmod, blocks


def _pack(arrs, rows=None):
    flat = jnp.concatenate([a.reshape(-1).astype(F32) for a in arrs])
    n = flat.shape[0]
    if rows is None:
        cols = 1024
        rows = -(-n // (8 * cols)) * 8
    else:
        cols = -(-n // (rows * 128)) * 128
    return jnp.pad(flat, (0, rows * cols - n)).reshape(rows, cols)


def _unpack(flat, shapes):
    out, off = [], 0
    for shp in shapes:
        n = math.prod(shp)
        out.append(flat[..., off:off + n].reshape(flat.shape[:-1] + tuple(shp)))
        off += n
    return out


def _take_block(a, idx, size, axis):
    return lax.dynamic_slice_in_dim(a, idx * size, size, axis)


def kernel(x, c, positions, ada_w, ada_b, norm_mix_g, norm_ffn_g, ab_w_in, a_vnorm_g, a_spatial_w, a_spatial_b, b_q_norm_g, b_k_norm_g, ab_w_out, conv_pw1_w, conv_pw1_b, conv_dw_w, conv_dw_b, conv_ln_g, conv_ln_b, conv_pw2_w, conv_pw2_b, ffn_up_w, ffn_dw_w, ffn_dw_b, ffn_down_w, loss_target, m_ada_w, m_ada_b, m_norm_mix_g, m_norm_ffn_g, m_ab_w_in, m_a_vnorm_g, m_a_spatial_w, m_a_spatial_b, m_b_q_norm_g, m_b_k_norm_g, m_ab_w_out, m_conv_pw1_w, m_conv_pw1_b, m_conv_dw_w, m_conv_dw_b, m_conv_ln_g, m_conv_ln_b, m_conv_pw2_w, m_conv_pw2_b, m_ffn_up_w, m_ffn_dw_w, m_ffn_dw_b, m_ffn_down_w, v_ada_w, v_ada_b, v_norm_mix_g, v_norm_ffn_g, v_ab_w_in, v_a_vnorm_g, v_a_spatial_w, v_a_spatial_b, v_b_q_norm_g, v_b_k_norm_g, v_ab_w_out, v_conv_pw1_w, v_conv_pw1_b, v_conv_dw_w, v_conv_dw_b, v_conv_ln_g, v_conv_ln_b, v_conv_pw2_w, v_conv_pw2_b, v_ffn_up_w, v_ffn_dw_w, v_ffn_dw_b, v_ffn_down_w):
    weights = dict(ada_w=ada_w, ada_b=ada_b, norm_mix_g=norm_mix_g, norm_ffn_g=norm_ffn_g, ab_w_in=ab_w_in, a_vnorm_g=a_vnorm_g, a_spatial_w=a_spatial_w, a_spatial_b=a_spatial_b, b_q_norm_g=b_q_norm_g, b_k_norm_g=b_k_norm_g, ab_w_out=ab_w_out, conv_pw1_w=conv_pw1_w, conv_pw1_b=conv_pw1_b, conv_dw_w=conv_dw_w, conv_dw_b=conv_dw_b, conv_ln_g=conv_ln_g, conv_ln_b=conv_ln_b, conv_pw2_w=conv_pw2_w, conv_pw2_b=conv_pw2_b, ffn_up_w=ffn_up_w, ffn_dw_w=ffn_dw_w, ffn_dw_b=ffn_dw_b, ffn_down_w=ffn_down_w)
    mom1 = dict(ada_w=m_ada_w, ada_b=m_ada_b, norm_mix_g=m_norm_mix_g, norm_ffn_g=m_norm_ffn_g, ab_w_in=m_ab_w_in, a_vnorm_g=m_a_vnorm_g, a_spatial_w=m_a_spatial_w, a_spatial_b=m_a_spatial_b, b_q_norm_g=m_b_q_norm_g, b_k_norm_g=m_b_k_norm_g, ab_w_out=m_ab_w_out, conv_pw1_w=m_conv_pw1_w, conv_pw1_b=m_conv_pw1_b, conv_dw_w=m_conv_dw_w, conv_dw_b=m_conv_dw_b, conv_ln_g=m_conv_ln_g, conv_ln_b=m_conv_ln_b, conv_pw2_w=m_conv_pw2_w, conv_pw2_b=m_conv_pw2_b, ffn_up_w=m_ffn_up_w, ffn_dw_w=m_ffn_dw_w, ffn_dw_b=m_ffn_dw_b, ffn_down_w=m_ffn_down_w)
    mom2 = dict(ada_w=v_ada_w, ada_b=v_ada_b, norm_mix_g=v_norm_mix_g, norm_ffn_g=v_norm_ffn_g, ab_w_in=v_ab_w_in, a_vnorm_g=v_a_vnorm_g, a_spatial_w=v_a_spatial_w, a_spatial_b=v_a_spatial_b, b_q_norm_g=v_b_q_norm_g, b_k_norm_g=v_b_k_norm_g, ab_w_out=v_ab_w_out, conv_pw1_w=v_conv_pw1_w, conv_pw1_b=v_conv_pw1_b, conv_dw_w=v_conv_dw_w, conv_dw_b=v_conv_dw_b, conv_ln_g=v_conv_ln_g, conv_ln_b=v_conv_ln_b, conv_pw2_w=v_conv_pw2_w, conv_pw2_b=v_conv_pw2_b, ffn_up_w=v_ffn_up_w, ffn_dw_w=v_ffn_dw_w, ffn_dw_b=v_ffn_dw_b, ffn_down_w=v_ffn_down_w)
    order = list(weights)
    d, f2 = D_MODEL, 2 * FFN_DIM
    t = x.shape[1]
    me = 4 * lax.axis_index("x") + 2 * lax.axis_index("y") + lax.axis_index("c")
    for window, dil in PATTERNS:
        assert window // dil == Q_BLOCK and t % (dil * Q_BLOCK) == 0

    small_in = [c[0], conv_pw1_b[0], conv_dw_w[0], conv_dw_b[0], conv_ln_g[0], conv_ln_b[0], conv_pw2_b[0], ffn_dw_w]
    g1 = _all_gather_vmem(_pack(small_in, rows=8), "gather_small").reshape(N_DEV, -1)
    c_all, pw1_b, dw_w, dw_b, ln_g, ln_b, pw2_b, fdw_w = _unpack(g1, [a.shape for a in small_in])
    pw1_b, dw_b, ln_g, ln_b, pw2_b = [a.reshape(1, -1) for a in (pw1_b, dw_b, ln_g, ln_b, pw2_b)]
    dw_w = dw_w.transpose(1, 0, 2).reshape(CONV_WIDTH, d)
    fdw_w = fdw_w.transpose(1, 2, 0, 3).reshape(2, FFN_CONV_WIDTH, f2)

    stages = dict(l0_mix=[ab_w_in[0].T, ab_w_out[0]], l0_ffn=[ffn_up_w[0].T, ffn_down_w[0]],
                  l1_conv=[conv_pw1_w[0].T, conv_pw2_w[0]], l1_ffn=[ffn_up_w[1].T, ffn_down_w[1]])
    stages = {k: [s.astype(BF16) for s in v] for k, v in stages.items()}
    names = dict(l0_mix=("w_in_t", "w_out"), l0_ffn=("up_t", "down"), l1_conv=("pw1_t", "pw2"), l1_ffn=("up_t", "down"))
    ready = {"l0_mix": [a.reshape(-1, d) for a in _all_gather_hbm(stages["l0_mix"], "gather_mixer_weights")]}
    stages, _ = lax.optimization_barrier((stages, ready))
    arriving, tokens = {}, []
    for stage, group in (("l0_ffn", ("l0_ffn",)), ("l1_conv", ("l1_conv", "l1_ffn"))):
        srcs = [s for g in group for s in stages[g]]
        arriving[stage], token = _pushes_start(
            srcs, [_own_block_placed(s, s.shape, BF16, me) for s in srcs], False, f"gather_{stage}_start")
        tokens.append(token)

    def fetch(stage, after):
        if stage in arriving:
            full = [a.reshape(-1, d) for a in _pushes_wait(arriving[stage], after, f"gather_{stage}_wait")]
            ready[stage] = full[:2]
            if stage == "l1_conv":
                ready["l1_ffn"] = full[2:]
        return dict(zip(names[stage], ready[stage]))

    c16 = jnp.pad(c_all, ((0, 2 * N_DEV - c_all.shape[0]), (0, 0)))
    part = jnp.concatenate([_ada_fwd(c16, ada_w[l], f"ada_fwd{l}")[:N_DEV] for l in range(2)], axis=1)
    g2 = _all_gather_vmem(part, "gather_mod").reshape(N_DEV, N_DEV, 2, -1)
    mod = lax.dynamic_index_in_dim(g2, me, axis=1, keepdims=False).transpose(1, 0, 2).reshape(2, 6 * d) + ada_b
    mod = mod + tokens[0][0:1, 0:1] + tokens[1][0:1, 0:1]

    causal = jnp.tril(jnp.ones((CHUNK, CHUNK), bool))
    wtril = jnp.where(causal[None], a_spatial_w[0], 0.0)
    mixer_w = dict(
        gain=a_vnorm_g[0].reshape(1, A_WIDTH), wtril=wtril.astype(BF16),
        wtril_t=wtril.transpose(0, 2, 1).astype(BF16),
        bias_exp=jnp.repeat(a_spatial_b[0].T, GROUP_DIM, axis=1),
        gq=jnp.tile(b_q_norm_g[0], HEADS)[None, :], gk=jnp.tile(b_k_norm_g[0], HEADS)[None, :],
        seg=jnp.kron(jnp.eye(HEADS, dtype=F32), jnp.ones((HEAD_DIM, HEAD_DIM), F32)))
    conv_w = dict(pw1_b=pw1_b, dw_w=dw_w, dw_b=dw_b, ln_g=ln_g, ln_b=ln_b, pw2_b=pw2_b)
    ffn_w = [dict(dw_w=fdw_w[l].reshape(FFN_CONV_WIDTH, 2, FFN_DIM).transpose(1, 0, 2), dw_b=ffn_dw_b[l].reshape(2, 1, FFN_DIM))
             for l in range(2)]

    leaving = {}

    def emit(stage, grads):
        blocks = [g.reshape(N_DEV, g.shape[0] // N_DEV, d) for g in grads]
        lands = [_own_block_placed(lax.dynamic_index_in_dim(b, me, 0, keepdims=False), b.shape[1:], BF16, me) for b in blocks]
        leaving[stage], token = _pushes_start(blocks, lands, True, f"reduce_{stage}_start")
        return token

    loss, dx, dmod, (g_mix, g_ffn0, g_conv, g_ffn1) = _local_step(
        x[0], loss_target[0], positions[0].astype(F32)[:, None], mod, norm_mix_g, norm_ffn_g, mixer_w, conv_w, ffn_w,
        fetch, emit)

    def reduced(stage, after):
        lands = _pushes_wait(leaving[stage], after, f"reduce_{stage}_wait")
        return [_sum_slots(a, f"reduce_{stage}_sum{i}") for i, a in enumerate(lands)]

    (r_up_t1, r_down1), (r_pw1_t, r_pw2), (r_up_t0, r_down0) = [reduced(s, dx) for s in ("l1_ffn", "l1_conv", "l0_ffn")]

    small_g = [
        dmod, jnp.concatenate([g_mix["norm_g"], g_conv["norm_g"]]), jnp.concatenate([g_ffn0["norm_g"], g_ffn1["norm_g"]]),
        g_mix["vnorm_g"], g_mix["spatial_w"], g_mix["spatial_b"], g_mix["q_norm_g"], g_mix["k_norm_g"],
        g_conv["pw1_b"], g_conv["dw_w"], g_conv["dw_b"], g_conv["ln_g"], g_conv["ln_b"], g_conv["pw2_b"],
        jnp.stack([g_ffn0["dw_w"], g_ffn1["dw_w"]]), jnp.concatenate([g_ffn0["dw_b"], g_ffn1["dw_b"]])]
    packed = _pack(small_g, rows=8)
    g3 = _all_gather_vmem(packed, "gather_small_grads").reshape(N_DEV, 8, -1)
    total = _unpack(_sum_slots(g3, "sum_small_grads").reshape(-1), [a.shape for a in small_g])
    (s_dmod, s_mix_g, s_ffn_g, s_vnorm, s_sp_w, s_sp_b, s_gq, s_gk, s_pw1_b, s_dw_w, s_dw_b, s_ln_g, s_ln_b,
     s_pw2_b, s_fdw_w, s_fdw_b) = total
    dmod_all = g3.reshape(N_DEV, -1)[:, :2 * 6 * d].reshape(N_DEV, 2, 6 * d)
    n_ada = ada_w.shape[2]
    dmod16 = jnp.pad(_take_block(dmod_all, me, n_ada, 2), ((0, N_DEV), (0, 0), (0, 0)))
    g_ada_w = jnp.stack([_ada_bwd(c16, dmod16[:, l], f"ada_bwd{l}") for l in range(2)])

    grads = dict(
        ada_w=g_ada_w, ada_b=s_dmod, norm_mix_g=s_mix_g, norm_ffn_g=s_ffn_g,
        a_vnorm_g=s_vnorm[None], a_spatial_w=s_sp_w[None], a_spatial_b=s_sp_b[None], b_q_norm_g=s_gq[None],
        b_k_norm_g=s_gk[None], conv_pw1_w=r_pw1_t.T[None],
        conv_pw1_b=_take_block(s_pw1_b, me, conv_pw1_b.shape[1], 1),
        conv_dw_w=_take_block(s_dw_w, me, conv_dw_w.shape[2], 1)[None],
        conv_dw_b=_take_block(s_dw_b, me, conv_dw_b.shape[1], 1), conv_ln_g=_take_block(s_ln_g, me, conv_ln_g.shape[1], 1),
        conv_ln_b=_take_block(s_ln_b, me, conv_ln_b.shape[1], 1), conv_pw2_w=r_pw2[None],
        conv_pw2_b=_take_block(s_pw2_b, me, conv_pw2_b.shape[1], 1),
        ffn_up_w=jnp.stack([r_up_t0.T, r_up_t1.T]), ffn_dw_w=_take_block(s_fdw_w, me, ffn_dw_w.shape[2], 2),
        ffn_dw_b=s_fdw_b, ffn_down_w=jnp.stack([r_down0, r_down1]))

    large = ("ada_w", "conv_pw1_w", "conv_pw2_w", "ffn_up_w", "ffn_down_w", "ab_w_in", "ab_w_out")
    delta, new_m, new_v = {}, {}, {}
    for name in large:
        if name == "ab_w_in":
            r_in_t, r_out = reduced("l0_mix", new_v["ffn_down_w"])
            grads.update(ab_w_in=r_in_t.T[None], ab_w_out=r_out[None])
        shp = weights[name].shape
        two_d = lambda a: a.reshape(-1, shp[-1])
        res = _adamw(two_d(weights[name]), two_d(grads[name]), two_d(mom1[name]), two_d(mom2[name]), f"adamw_{name}")
        delta[name], new_m[name], new_v[name] = [r.reshape(shp) for r in res]
    small = [n for n in order if n not in large]
    shapes = [weights[n].shape for n in small]
    res = _adamw(*[_pack([src[n] for n in small]) for src in (weights, grads, mom1, mom2)], "adamw_small")
    for dst, r in zip((delta, new_m, new_v), res):
        for n, a in zip(small, _unpack(r.reshape(-1), shapes)):
            dst[n] = a

    loss = lax.psum(loss[0, 0], ("x", "y", "c"))
    return (loss, dx[None], *[grads[n] for n in order], *[delta[n] for n in order],
            *[new_m[n] for n in order], *[new_v[n] for n in order])
```

```python
import functools
import math

import jax
import jax.numpy as jnp
from jax import lax
from jax.experimental import pallas as pl
from jax.experimental.pallas import tpu as pltpu

F32 = jnp.float32
BF16 = jnp.bfloat16
MESH = pl.DeviceIdType.MESH

D_MODEL = 1024
A_WIDTH = 512
A_GROUPS = 4
GROUP_DIM = 128
CHUNK = 128
B_WIDTH = 512
HEADS = 8
HEAD_DIM = 64
PATTERNS = ((128, 1), (512, 4), (2048, 16))
Q_BLOCK = 128
ROPE_THETA = 10000.0
AB_IN = 2560
CONV_WIDTH = 31
FFN_DIM = 2816
FFN_CONV_WIDTH = 3
EPS = 1e-6
NEG = -1e30
N_DEV = 8
ADAM_LR, ADAM_B1, ADAM_B2, ADAM_EPS, ADAM_WD, ADAM_STEP = 0.001, 0.9, 0.999, 1e-08, 0.01, 10

V7X_VMEM_LIMIT = 56 * 2**20
BF16_ROWS = 16
FFN_HALO = 16
CONV_HALO = 32

_NN = (((1,), (0,)), ((), ()))
_NT = (((1,), (1,)), ((), ()))
_TN = (((0,), (0,)), ((), ()))


def _tile(n, prefs=(512, 256, 128)):
    for t in prefs:
        if n % t == 0:
            return t
    return n


def _row_tile(n, cap=512):
    best = n
    for t in range(8, min(n, cap) + 1, 8):
        if n % t == 0:
            best = t
    return best if best <= cap else n


def _params(*sem):
    return pltpu.CompilerParams(dimension_semantics=sem, vmem_limit_bytes=V7X_VMEM_LIMIT)


def _dot(a, b, dims):
    return lax.dot_general(a, b, dims, preferred_element_type=F32)


def _sigmoid(x):
    return 1.0 / (1.0 + jnp.exp(-x))


def _gelu(x):
    return 0.5 * x * (1.0 + lax.erf(x * (2.0 ** -0.5)))


def _gelu_grad(x):
    return 0.5 * (1.0 + lax.erf(x * (2.0 ** -0.5))) + x * jnp.exp(-0.5 * x * x) * (1.0 / math.sqrt(2.0 * math.pi))


def _colsum(v):
    return jnp.sum(v, axis=0, keepdims=True)


MATMUL_VMEM_BUDGET = 40 * 2**20


def _matmul_tiles(m, n, k, out_bytes, with_resid):
    def options(dim):
        opts = [t for t in (1024, 512, 256, 128) if dim % t == 0]
        return opts + [dim] if dim <= 4096 and dim not in opts else opts

    best = None
    for tm in options(m):
        for tn in options(n):
            need = 4 * (tm * k + k * tn) + tm * tn * (4 + 2 * out_bytes) + (24 * tm * tn if with_resid else 0)
            if need <= MATMUL_VMEM_BUDGET and (best is None or tm * tn / (tm + tn) > best[0]):
                best = (tm * tn / (tm + tn), tm, tn)
    return best[1], best[2]


def _matmul_tn_acc(a, b, name, tk=512):
    squeeze = a.ndim == 2
    a3 = a[None] if squeeze else a
    p_, t, m = a3.shape
    n = b.shape[1]
    nk = t // tk

    def body(a_ref, b_ref, o_ref, acc_ref):
        kt = pl.program_id(1)

        @pl.when(kt == 0)
        def _():
            acc_ref[...] = jnp.zeros_like(acc_ref)

        acc_ref[...] += _dot(a_ref[...], b_ref[...], _TN)

        @pl.when(kt == nk - 1)
        def _():
            o_ref[...] = acc_ref[...].astype(BF16)

    out = pl.pallas_call(
        body, name=name, grid=(p_, nk),
        in_specs=[pl.BlockSpec((None, tk, m), lambda p, kt: (p, kt, 0)), pl.BlockSpec((tk, n), lambda p, kt: (kt, 0))],
        out_specs=pl.BlockSpec((None, m, n), lambda p, kt: (p, 0, 0)), out_shape=jax.ShapeDtypeStruct((p_, m, n), BF16),
        scratch_shapes=[pltpu.VMEM((m, n), F32)], compiler_params=_params("parallel", "arbitrary"),
    )(a3, b)
    return out[0] if squeeze else out


def _matmul(a, b, mode, out_dtype, name, bias=None, resid=None):
    if mode == "nn":
        (m, k), (_, n) = a.shape, b.shape
    elif mode == "nt":
        (m, k), (n, _) = a.shape, b.shape
    else:
        (k, m), (_, n) = a.shape, b.shape
    tm, tn = _matmul_tiles(m, n, k, jnp.dtype(out_dtype).itemsize, resid is not None)
    dims = {"nn": _NN, "nt": _NT, "tn": _TN}[mode]
    a_spec = pl.BlockSpec((k, tm), lambda i, j: (0, i)) if mode == "tn" else pl.BlockSpec((tm, k), lambda i, j: (i, 0))
    b_spec = pl.BlockSpec((tn, k), lambda i, j: (j, 0)) if mode == "nt" else pl.BlockSpec((k, tn), lambda i, j: (0, j))
    in_specs, args = [a_spec, b_spec], [a, b]
    row_spec = pl.BlockSpec((1, tn), lambda i, j: (0, j))
    tile_spec = pl.BlockSpec((tm, tn), lambda i, j: (i, j))
    if bias is not None:
        in_specs.append(row_spec)
        args.append(bias)
    if resid is not None:
        in_specs += [tile_spec, row_spec]
        args += list(resid)
    out_shape = [jax.ShapeDtypeStruct((m, n), out_dtype)]
    out_specs = [tile_spec]
    if resid is not None:
        out_shape.append(jax.ShapeDtypeStruct((m, n), F32))
        out_specs.append(tile_spec)

    def body(*refs):
        a_ref, b_ref = refs[0], refs[1]
        pos = 2
        acc = _dot(a_ref[...], b_ref[...], dims)
        if bias is not None:
            acc = acc + refs[pos][...]
            pos += 1
        if resid is not None:
            x_ref, g_ref = refs[pos], refs[pos + 1]
            pos += 2
        refs[pos][...] = acc.astype(out_dtype)
        if resid is not None:
            refs[pos + 1][...] = x_ref[...] + g_ref[...] * acc

    outs = pl.pallas_call(
        body, name=name, grid=(m // tm, n // tn), in_specs=in_specs, out_specs=out_specs, out_shape=out_shape,
        compiler_params=_params("parallel", "parallel"),
    )(*args)
    return outs if resid is not None else outs[0]


def _modnorm(x, g, sc, sh, name):
    t, d = x.shape
    tm = _tile(t)
    row = pl.BlockSpec((1, d), lambda i: (0, 0))
    blk = pl.BlockSpec((tm, d), lambda i: (i, 0))

    def body(x_ref, g_ref, sc_ref, sh_ref, o_ref):
        x = x_ref[...]
        r = lax.rsqrt(jnp.mean(x * x, axis=-1, keepdims=True) + EPS)
        o_ref[...] = ((x * r) * g_ref[...] * (1.0 + sc_ref[...]) + sh_ref[...]).astype(BF16)

    return pl.pallas_call(
        body, name=name, grid=(t // tm,), in_specs=[blk, row, row, row], out_specs=blk,
        out_shape=jax.ShapeDtypeStruct((t, d), BF16), compiler_params=_params("parallel"),
    )(x, g, sc, sh)


def _modnorm_bwd(x, dh, g, sc, dres, name):
    t, d = x.shape
    tm = _tile(t)
    row = pl.BlockSpec((1, d), lambda i: (0, 0))
    blk = pl.BlockSpec((tm, d), lambda i: (i, 0))

    def body(x_ref, dh_ref, g_ref, sc_ref, dres_ref, dx_ref, dw_ref, dsh_ref):
        @pl.when(pl.program_id(0) == 0)
        def _():
            dw_ref[...] = jnp.zeros_like(dw_ref)
            dsh_ref[...] = jnp.zeros_like(dsh_ref)

        x = x_ref[...]
        dh = dh_ref[...].astype(F32)
        r = lax.rsqrt(jnp.mean(x * x, axis=-1, keepdims=True) + EPS)
        xn = x * r
        dxn = dh * (g_ref[...] * (1.0 + sc_ref[...]))
        dx_ref[...] = dres_ref[...] + r * (dxn - xn * jnp.mean(dxn * xn, axis=-1, keepdims=True))
        dw_ref[...] += _colsum(dh * xn)
        dsh_ref[...] += _colsum(dh)

    return pl.pallas_call(
        body, name=name, grid=(t // tm,), in_specs=[blk, blk, row, row, blk], out_specs=[blk, row, row],
        out_shape=[jax.ShapeDtypeStruct((t, d), F32), jax.ShapeDtypeStruct((1, d), F32), jax.ShapeDtypeStruct((1, d), F32)],
        compiler_params=_params("arbitrary"),
    )(x, dh, g, sc, dres)


def _gate_bwd(dxn, y, gate, name):
    t, d = dxn.shape
    tm = _tile(t)
    row = pl.BlockSpec((1, d), lambda i: (0, 0))
    blk = pl.BlockSpec((tm, d), lambda i: (i, 0))

    def body(dxn_ref, y_ref, g_ref, dy_ref, dg_ref):
        @pl.when(pl.program_id(0) == 0)
        def _():
            dg_ref[...] = jnp.zeros_like(dg_ref)

        dxn = dxn_ref[...]
        dy_ref[...] = (dxn * g_ref[...]).astype(BF16)
        dg_ref[...] += _colsum(dxn * y_ref[...])

    return pl.pallas_call(
        body, name=name, grid=(t // tm,), in_specs=[blk, blk, row], out_specs=[blk, row],
        out_shape=[jax.ShapeDtypeStruct((t, d), BF16), jax.ShapeDtypeStruct((1, d), F32)],
        compiler_params=_params("arbitrary"),
    )(dxn, y, gate)


def _loss_head(y, target, name):
    t, d = y.shape
    tm = _tile(t)
    blk = pl.BlockSpec((tm, d), lambda i: (i, 0))
    one = pl.BlockSpec((1, 1), lambda i: (0, 0))

    def body(y_ref, t_ref, dy_ref, loss_ref, acc_ref):
        @pl.when(pl.program_id(0) == 0)
        def _():
            acc_ref[...] = jnp.zeros_like(acc_ref)

        e = y_ref[...] - t_ref[...]
        dy_ref[...] = e * (1.0 / d)
        acc_ref[...] += _colsum(e * e)

        @pl.when(pl.program_id(0) == pl.num_programs(0) - 1)
        def _():
            loss_ref[...] = jnp.sum(acc_ref[...], axis=1, keepdims=True) * (0.5 / d)

    return pl.pallas_call(
        body, name=name, grid=(t // tm,), in_specs=[blk, blk], out_specs=[blk, one],
        out_shape=[jax.ShapeDtypeStruct((t, d), F32), jax.ShapeDtypeStruct((1, 1), F32)],
        scratch_shapes=[pltpu.VMEM((1, d), F32)], compiler_params=_params("arbitrary"),
    )(y, target)


def _group_norm(vg, gain):
    mu = jnp.mean(vg, axis=-1, keepdims=True)
    xc = vg - mu
    rstd = lax.rsqrt(jnp.mean(xc * xc, axis=-1, keepdims=True) + EPS)
    xhat = xc * rstd
    return xhat, rstd, xhat * gain


def _gmlp_fwd(z, gain, wtril, bias_exp, name):
    t = z.shape[0]
    zu = pl.BlockSpec((CHUNK, A_WIDTH), lambda i: (i, 0))
    zv = pl.BlockSpec((CHUNK, A_WIDTH), lambda i: (i, 1))
    full2 = lambda shp: pl.BlockSpec(shp, lambda i: (0, 0))
    w_spec = pl.BlockSpec((A_GROUPS, CHUNK, CHUNK), lambda i: (0, 0, 0))

    def body(zu_ref, zv_ref, gain_ref, w_ref, b_ref, ya_ref):
        ua = _gelu(zu_ref[...].astype(F32))
        vg = _gelu(zv_ref[...].astype(F32))
        for g in range(A_GROUPS):
            sl = slice(g * GROUP_DIM, (g + 1) * GROUP_DIM)
            _, _, vn = _group_norm(vg[:, sl], gain_ref[:, sl])
            f = _dot(w_ref[g], vn.astype(BF16), _NN) + b_ref[:, sl]
            ya_ref[:, sl] = (ua[:, sl] * f).astype(BF16)

    return pl.pallas_call(
        body, name=name, grid=(t // CHUNK,),
        in_specs=[zu, zv, full2((1, A_WIDTH)), w_spec, full2((CHUNK, A_WIDTH))], out_specs=zu,
        out_shape=jax.ShapeDtypeStruct((t, A_WIDTH), BF16), compiler_params=_params("parallel"),
    )(z, z, gain, wtril, bias_exp)


def _gmlp_bwd(z, dcat, gain, wtril, wtril_t, bias_exp, name):
    t = z.shape[0]
    zu = pl.BlockSpec((CHUNK, A_WIDTH), lambda i: (i, 0))
    zv = pl.BlockSpec((CHUNK, A_WIDTH), lambda i: (i, 1))
    full2 = lambda shp: pl.BlockSpec(shp, lambda i: (0, 0))
    w_spec = pl.BlockSpec((A_GROUPS, CHUNK, CHUNK), lambda i: (0, 0, 0))
    dz_spec = pl.BlockSpec((CHUNK, 2 * A_WIDTH), lambda i: (i, 0))

    def body(zu_ref, zv_ref, dya_ref, gain_ref, w_ref, wt_ref, b_ref, dz_ref, dw_ref, dgain_ref, dbias_ref):
        @pl.when(pl.program_id(0) == 0)
        def _():
            dw_ref[...] = jnp.zeros_like(dw_ref)
            dgain_ref[...] = jnp.zeros_like(dgain_ref)
            dbias_ref[...] = jnp.zeros_like(dbias_ref)

        zu_v = zu_ref[...].astype(F32)
        zv_v = zv_ref[...].astype(F32)
        dya = dya_ref[...].astype(F32)
        ua = _gelu(zu_v)
        vg = _gelu(zv_v)
        row = lax.broadcasted_iota(jnp.int32, (CHUNK, CHUNK), 0)
        col = lax.broadcasted_iota(jnp.int32, (CHUNK, CHUNK), 1)
        for g in range(A_GROUPS):
            sl = slice(g * GROUP_DIM, (g + 1) * GROUP_DIM)
            gain_g = gain_ref[:, sl]
            xhat, rstd, vn = _group_norm(vg[:, sl], gain_g)
            vn16 = vn.astype(BF16)
            f = _dot(w_ref[g], vn16, _NN) + b_ref[:, sl]
            df = dya[:, sl] * ua[:, sl]
            df16 = df.astype(BF16)
            dz_ref[:, sl] = (dya[:, sl] * f * _gelu_grad(zu_v[:, sl])).astype(BF16)
            dw_ref[g] += jnp.where(row >= col, _dot(df16, vn16, _NT), 0.0)
            dvn = _dot(wt_ref[g], df16, _NN)
            dgain_ref[:, sl] += _colsum(dvn * xhat)
            dxh = dvn * gain_g
            dvg = rstd * (dxh - jnp.mean(dxh, axis=-1, keepdims=True) - xhat * jnp.mean(dxh * xhat, axis=-1, keepdims=True))
            dz_ref[:, A_WIDTH + g * GROUP_DIM:A_WIDTH + (g + 1) * GROUP_DIM] = (dvg * _gelu_grad(zv_v[:, sl])).astype(BF16)
            dbias_ref[:, sl] += df

    return pl.pallas_call(
        body, name=name, grid=(t // CHUNK,),
        in_specs=[zu, zv, zu, full2((1, A_WIDTH)), w_spec, w_spec, full2((CHUNK, A_WIDTH))],
        out_specs=[dz_spec, w_spec, full2((1, A_WIDTH)), full2((CHUNK, A_WIDTH))],
        out_shape=[jax.ShapeDtypeStruct((t, 2 * A_WIDTH), BF16), jax.ShapeDtypeStruct((A_GROUPS, CHUNK, CHUNK), F32),
                   jax.ShapeDtypeStruct((1, A_WIDTH), F32), jax.ShapeDtypeStruct((CHUNK, A_WIDTH), F32)],
        compiler_params=_params("arbitrary"),
    )(z, z, dcat, gain, wtril, wtril_t, bias_exp)


def _rope_tables(pos, inv_freq, sign, name):
    t = pos.shape[0]
    tm = _tile(t)
    row = pl.BlockSpec((1, B_WIDTH), lambda i: (0, 0))
    blk = pl.BlockSpec((tm, B_WIDTH), lambda i: (i, 0))

    def body(pos_ref, f_ref, s_ref, cos_ref, sin_ref):
        ang = pos_ref[...] * f_ref[...]
        cos_ref[...] = jnp.cos(ang)
        sin_ref[...] = jnp.sin(ang) * s_ref[...]

    return pl.pallas_call(
        body, name=name, grid=(t // tm,), in_specs=[pl.BlockSpec((tm, 1), lambda i: (i, 0)), row, row],
        out_specs=[blk, blk], out_shape=[jax.ShapeDtypeStruct((t, B_WIDTH), F32)] * 2,
        compiler_params=_params("parallel"),
    )(pos, inv_freq, sign)


def _head_sum(v, seg):
    return lax.dot_general(v, seg, _NN, precision=lax.Precision.HIGHEST, preferred_element_type=F32)


def _swap_halves(v):
    lane = lax.broadcasted_iota(jnp.int32, v.shape, 1)
    return jnp.where((lane & (HEAD_DIM - 1)) < HEAD_DIM // 2,pltpu.roll(v, B_WIDTH - HEAD_DIM // 2, 1), pltpu.roll(v, HEAD_DIM // 2, 1))


def _qk_prep(z, cos_t, sin_t, gq, gk, seg, name):
    t = z.shape[0]
    tm = _tile(t, (256, 128))
    col = lambda c: pl.BlockSpec((tm, B_WIDTH), lambda i: (i, c))
    row = pl.BlockSpec((1, B_WIDTH), lambda i: (0, 0))
    blk = col(0)

    def body(q_ref, k_ref, v_ref, cos_ref, sin_ref, gq_ref, gk_ref, seg_ref, qo_ref, ko_ref, vo_ref):
        def norm_rot(x, g):
            r = lax.rsqrt(_head_sum(x * x, seg_ref[...]) * (1.0 / HEAD_DIM) + EPS)
            xn = x * r * g
            return xn * cos_ref[...] + _swap_halves(xn) * sin_ref[...]

        qo_ref[...] = norm_rot(q_ref[...].astype(F32), gq_ref[...]).astype(BF16)
        ko_ref[...] = norm_rot(k_ref[...].astype(F32), gk_ref[...]).astype(BF16)
        vo_ref[...] = v_ref[...].astype(BF16)

    return pl.pallas_call(
        body, name=name, grid=(t // tm,),
        in_specs=[col(2), col(3), col(4), blk, blk, row, row, pl.BlockSpec((B_WIDTH, B_WIDTH), lambda i: (0, 0))],
        out_specs=[blk, blk, blk], out_shape=[jax.ShapeDtypeStruct((t, B_WIDTH), BF16)] * 3,
        compiler_params=_params("parallel"),
    )(z, z, z, cos_t, sin_t, gq, gk, seg)


def _qk_prep_bwd(z, dqs, dks, dvs, cos_t, sin_t, gq, gk, seg, name):
    t = z.shape[0]
    tm = _tile(t, (256, 128))
    col = lambda c: pl.BlockSpec((tm, B_WIDTH), lambda i: (i, c))
    row = pl.BlockSpec((1, B_WIDTH), lambda i: (0, 0))
    blk = col(0)
    nb = len(dqs)

    def body(*refs):
        q_ref, k_ref = refs[0], refs[1]
        dq_refs, dk_refs, dv_refs = refs[2:2 + nb], refs[2 + nb:2 + 2 * nb], refs[2 + 2 * nb:2 + 3 * nb]
        cos_ref, sin_ref, gq_ref, gk_ref, seg_ref, dz_ref, dgq_ref, dgk_ref = refs[2 + 3 * nb:]

        @pl.when(pl.program_id(0) == 0)
        def _():
            dgq_ref[...] = jnp.zeros_like(dgq_ref)
            dgk_ref[...] = jnp.zeros_like(dgk_ref)

        def back(x, d_refs, g, dg_ref):
            dout = d_refs[0][...]
            for r_ in d_refs[1:]:
                dout = dout + r_[...]
            dy = dout * cos_ref[...] + _swap_halves(dout * sin_ref[...])
            r = lax.rsqrt(_head_sum(x * x, seg_ref[...]) * (1.0 / HEAD_DIM) + EPS)
            xn = x * r
            dg_ref[...] += _colsum(dy * xn)
            dxn = dy * g
            return r * (dxn - xn * (_head_sum(dxn * xn, seg_ref[...]) * (1.0 / HEAD_DIM)))

        dz_ref[:, 0:B_WIDTH] = back(q_ref[...].astype(F32), dq_refs, gq_ref[...], dgq_ref).astype(BF16)
        dz_ref[:, B_WIDTH:2 * B_WIDTH] = back(k_ref[...].astype(F32), dk_refs, gk_ref[...], dgk_ref).astype(BF16)
        dv = dv_refs[0][...]
        for r_ in dv_refs[1:]:
            dv = dv + r_[...]
        dz_ref[:, 2 * B_WIDTH:3 * B_WIDTH] = dv.astype(BF16)

    return pl.pallas_call(
        body, name=name, grid=(t // tm,),
        in_specs=[col(2), col(3)] + [blk] * (3 * nb) + [blk, blk, row, row, pl.BlockSpec((B_WIDTH, B_WIDTH), lambda i: (0, 0))],
        out_specs=[pl.BlockSpec((tm, 3 * B_WIDTH), lambda i: (i, 0)), row, row],
        out_shape=[jax.ShapeDtypeStruct((t, 3 * B_WIDTH), BF16), jax.ShapeDtypeStruct((1, B_WIDTH), F32),
                   jax.ShapeDtypeStruct((1, B_WIDTH), F32)],
        compiler_params=_params("arbitrary"),
    )(z, z, *dqs, *dks, *dvs, cos_t, sin_t, gq, gk, seg)


def _subseq(a, dil):
    return a.reshape(a.shape[0] // dil, dil * a.shape[1])


def _attn_fwd(q, k, v, dil, name):
    t = q.shape[0]
    nb = t // dil // Q_BLOCK
    cur = pl.BlockSpec((Q_BLOCK, B_WIDTH), lambda r, i: (i, r))
    prev = pl.BlockSpec((Q_BLOCK, B_WIDTH), lambda r, i: (jnp.maximum(i - 1, 0), r))

    def body(q_ref, kp_ref, kc_ref, vp_ref, vc_ref, o_ref, lse_ref):
        i = pl.program_id(1)
        q = q_ref[...]
        kk = jnp.concatenate([kp_ref[...], kc_ref[...]], axis=0)
        vv = jnp.concatenate([vp_ref[...], vc_ref[...]], axis=0)
        a = lax.broadcasted_iota(jnp.int32, (Q_BLOCK, 2 * Q_BLOCK), 0)
        j = lax.broadcasted_iota(jnp.int32, (Q_BLOCK, 2 * Q_BLOCK), 1)
        dist = a + Q_BLOCK - j
        mask = (dist >= 0) & (dist <= Q_BLOCK) & ((j >= Q_BLOCK) | (i > 0))
        for h in range(HEADS):
            sl = slice(h * HEAD_DIM, (h + 1) * HEAD_DIM)
            s = jnp.where(mask, _dot(q[:, sl], kk[:, sl], _NT) * (HEAD_DIM ** -0.5), NEG)
            m = jnp.max(s, axis=-1, keepdims=True)
            p = jnp.exp(s - m)
            den = jnp.sum(p, axis=-1, keepdims=True)
            o_ref[:, sl] = _dot(p.astype(BF16), vv[:, sl], _NN) / den
            lse_ref[:, sl] = jnp.broadcast_to(m + jnp.log(den), (Q_BLOCK, HEAD_DIM))

    o, lse = pl.pallas_call(
        body, name=name, grid=(dil, nb), in_specs=[cur, prev, cur, prev, cur], out_specs=[cur, cur],
        out_shape=[jax.ShapeDtypeStruct((t // dil, dil * B_WIDTH), F32)] * 2,
        compiler_params=_params("parallel", "parallel"),
    )(_subseq(q, dil), _subseq(k, dil), _subseq(k, dil), _subseq(v, dil), _subseq(v, dil))
    return o.reshape(t, B_WIDTH), lse.reshape(t, B_WIDTH)


def _attn_merge(outs, lses, name):
    t = outs[0].shape[0]
    tm = _tile(t)
    blk = pl.BlockSpec((tm, B_WIDTH), lambda i: (i, 0))
    nb = len(outs)

    def body(*refs):
        o_refs, l_refs, yb_ref, lse_ref = refs[:nb], refs[nb:2 * nb], refs[2 * nb], refs[2 * nb + 1]
        ls = [r[...] for r in l_refs]
        m = functools.reduce(jnp.maximum, ls)
        tot = m + jnp.log(sum(jnp.exp(l - m) for l in ls))
        yb_ref[...] = sum(jnp.exp(l - tot) * o[...] for l, o in zip(ls, o_refs)).astype(BF16)
        lse_ref[...] = tot

    return pl.pallas_call(
        body, name=name, grid=(t // tm,), in_specs=[blk] * (2 * nb), out_specs=[blk, blk],
        out_shape=[jax.ShapeDtypeStruct((t, B_WIDTH), BF16), jax.ShapeDtypeStruct((t, B_WIDTH), F32)],
        compiler_params=_params("parallel"),
    )(*outs, *lses)


def _attn_bwd(q, k, v, do, o, lse, dil, name):
    t = q.shape[0]
    nb = t // dil // Q_BLOCK
    blk = lambda f: pl.BlockSpec((Q_BLOCK, B_WIDTH), lambda r, i: (f(i), r))
    cur = blk(lambda i: jnp.minimum(i, nb - 1))
    prev = blk(lambda i: jnp.clip(i - 1, 0, nb - 1))
    scale = HEAD_DIM ** -0.5

    def body(q_ref, kp_ref, kc_ref, vp_ref, vc_ref, do_ref, o_ref, lse_ref, dq_ref, dk_ref, dv_ref,
             ck_ref, cv_ref, tk_ref, tv_ref):
        i = pl.program_id(1)

        @pl.when(i == 0)
        def _():
            ck_ref[...] = jnp.zeros_like(ck_ref)
            cv_ref[...] = jnp.zeros_like(cv_ref)

        @pl.when(i < nb)
        def _():
            q = q_ref[...]
            kk = jnp.concatenate([kp_ref[...], kc_ref[...]], axis=0)
            vv = jnp.concatenate([vp_ref[...], vc_ref[...]], axis=0)
            do = do_ref[...]
            dof = do.astype(F32)
            of = o_ref[...].astype(F32)
            a = lax.broadcasted_iota(jnp.int32, (Q_BLOCK, 2 * Q_BLOCK), 0)
            j = lax.broadcasted_iota(jnp.int32, (Q_BLOCK, 2 * Q_BLOCK), 1)
            dist = a + Q_BLOCK - j
            mask = (dist >= 0) & (dist <= Q_BLOCK) & ((j >= Q_BLOCK) | (i > 0))
            for h in range(HEADS):
                sl = slice(h * HEAD_DIM, (h + 1) * HEAD_DIM)
                s = jnp.where(mask, _dot(q[:, sl], kk[:, sl], _NT) * scale, NEG)
                p = jnp.exp(s - lse_ref[:, h * HEAD_DIM:h * HEAD_DIM + 1])
                dp = _dot(do[:, sl], vv[:, sl], _NT)
                delta = jnp.sum(dof[:, sl] * of[:, sl], axis=-1, keepdims=True)
                ds = (p * (dp - delta) * scale).astype(BF16)
                dq_ref[:, sl] = _dot(ds, kk[:, sl], _NN)
                dv_t = _dot(do[:, sl], p.astype(BF16), _TN)
                dk_t = _dot(q[:, sl], ds, _TN)
                tk_ref[sl, :] = ck_ref[sl, :] + dk_t[:, :Q_BLOCK]
                tv_ref[sl, :] = cv_ref[sl, :] + dv_t[:, :Q_BLOCK]
                ck_ref[sl, :] = dk_t[:, Q_BLOCK:]
                cv_ref[sl, :] = dv_t[:, Q_BLOCK:]

        @pl.when(i == nb)
        def _():
            tk_ref[...] = ck_ref[...]
            tv_ref[...] = cv_ref[...]

        @pl.when(i >= 1)
        def _():
            dk_ref[...] = tk_ref[...].T
            dv_ref[...] = tv_ref[...].T

    sub = lambda a_: _subseq(a_, dil)
    shape = jax.ShapeDtypeStruct((t // dil, dil * B_WIDTH), F32)
    dq, dk, dv = pl.pallas_call(
        body, name=name, grid=(dil, nb + 1), in_specs=[cur, prev, cur, prev, cur, cur, cur, cur],
        out_specs=[cur, prev, prev], out_shape=[shape] * 3,
        scratch_shapes=[pltpu.VMEM((B_WIDTH, Q_BLOCK), F32)] * 4,
        compiler_params=_params("parallel", "arbitrary"),
    )(sub(q), sub(k), sub(k), sub(v), sub(v), sub(do), sub(o), sub(lse))
    return dq.reshape(t, B_WIDTH), dk.reshape(t, B_WIDTH), dv.reshape(t, B_WIDTH)


FFN_TN = 256
FFN_FWD_CHUNK = 256
FFN_BWD_CHUNK = 128


def _ffn_up(h, up_t, name):
    t, k = h.shape
    tm = _tile(t)

    def body(h_ref, w_ref, o_ref):
        o_ref[...] = _dot(h_ref[...], w_ref[...], _NT).astype(BF16)

    return pl.pallas_call(
        body, name=name, grid=(2, t // tm),
        in_specs=[pl.BlockSpec((tm, k), lambda p, i: (i, 0)), pl.BlockSpec((None, FFN_DIM, k), lambda p, i: (p, 0, 0))],
        out_specs=pl.BlockSpec((None, tm, FFN_DIM), lambda p, i: (p, i, 0)),
        out_shape=jax.ShapeDtypeStruct((2, t, FFN_DIM), BF16), compiler_params=_params("parallel", "parallel"),
    )(h, up_t.reshape(2, FFN_DIM, k))


def _ffn_up_dx(du, up_t, name):
    t = du.shape[1]
    k = up_t.shape[1]
    tm = _tile(t)

    def body(a_ref, b_ref, o_ref):
        o_ref[...] = _dot(a_ref[0], b_ref[0], _NN) + _dot(a_ref[1], b_ref[1], _NN)

    return pl.pallas_call(
        body, name=name, grid=(t // tm,),
        in_specs=[pl.BlockSpec((2, tm, FFN_DIM), lambda i: (0, i, 0)), pl.BlockSpec((2, FFN_DIM, k), lambda i: (0, 0, 0))],
        out_specs=pl.BlockSpec((tm, k), lambda i: (i, 0)), out_shape=jax.ShapeDtypeStruct((t, k), F32),
        compiler_params=_params("parallel"),
    )(du, up_t.reshape(2, FFN_DIM, k))


def _ffn_conv(win, w_ref, b_ref, p):
    x = win.astype(F32)
    x0, x1, x2 = x[FFN_HALO:], pltpu.roll(x, 1, 0)[FFN_HALO:], pltpu.roll(x, 2, 0)[FFN_HALO:]
    return x0, b_ref[p] + w_ref[p, 2:3, :] * x0 + w_ref[p, 1:2, :] * x1 + w_ref[p, 0:1, :] * x2


def _zero_if(cond, v):
    return jnp.where(cond, 0, v).astype(v.dtype)


def _ffn_act(u, dw_w, dw_b, name):
    t = u.shape[1]
    tm = _tile(t)
    chunk = min(FFN_FWD_CHUNK, tm)
    hb = tm // FFN_HALO
    main = pl.BlockSpec((2, tm, FFN_TN), lambda i, j: (0, i, j))
    halo = pl.BlockSpec((2, FFN_HALO, FFN_TN), lambda i, j: (0, jnp.maximum(i * hb - 1, 0), j))
    wsp = pl.BlockSpec((2, FFN_CONV_WIDTH, FFN_TN), lambda i, j: (0, 0, j))
    bsp = pl.BlockSpec((2, 1, FFN_TN), lambda i, j: (0, 0, j))

    def body(u_ref, uh_ref, w_ref, b_ref, o_ref):
        first = pl.program_id(0) == 0

        def emit(rows, wins):
            za, zb = _ffn_conv(wins[0], w_ref, b_ref, 0)[1], _ffn_conv(wins[1], w_ref, b_ref, 1)[1]
            o_ref[rows, :] = (za * _sigmoid(za) * zb).astype(BF16)

        emit(pl.ds(0, chunk), [jnp.concatenate([_zero_if(first, uh_ref[p]), u_ref[p, 0:chunk, :]], axis=0) for p in range(2)])

        def step(c, carry):
            s = pl.multiple_of(c * chunk, chunk)
            emit(pl.ds(s, chunk), [u_ref[p, pl.ds(s - FFN_HALO, chunk + FFN_HALO), :] for p in range(2)])
            return carry

        lax.fori_loop(1, tm // chunk, step, 0)

    return pl.pallas_call(
        body, name=name, grid=(t // tm, FFN_DIM // FFN_TN), in_specs=[main, halo, wsp, bsp],
        out_specs=pl.BlockSpec((tm, FFN_TN), lambda i, j: (i, j)), out_shape=jax.ShapeDtypeStruct((t, FFN_DIM), BF16),
        compiler_params=_params("parallel", "parallel"),
    )(u, u, dw_w, dw_b)


def _fold8(v):
    return jnp.sum(v.reshape(v.shape[0] // 8, 8, v.shape[1]), axis=0)


def _ffn_act_bwd(u, dact, dw_w, dw_b, name):
    t = u.shape[1]
    tm = _tile(t)
    chunk = min(FFN_BWD_CHUNK, tm // 2)
    halo = FFN_HALO
    hb = tm // halo
    nt = t // tm
    last_halo = t // halo - 1
    prev_i = lambda i: jnp.maximum(i * hb - 1, 0)
    next_i = lambda i: jnp.minimum((i + 1) * hb, last_halo)
    main = pl.BlockSpec((2, tm, FFN_TN), lambda j, i: (0, i, j))
    prev = pl.BlockSpec((2, halo, FFN_TN), lambda j, i: (0, prev_i(i), j))
    nxt = pl.BlockSpec((2, halo, FFN_TN), lambda j, i: (0, next_i(i), j))
    wsp = pl.BlockSpec((2, FFN_CONV_WIDTH, FFN_TN), lambda j, i: (0, 0, j))
    bsp = pl.BlockSpec((2, 1, FFN_TN), lambda j, i: (0, 0, j))

    def body(u_ref, up_ref, un_ref, da_ref, dan_ref, w_ref, b_ref, du_ref, dw_ref, db_ref, acc_ref):
        i = pl.program_id(1)
        first, last = i == 0, i == nt - 1
        acc_ref[...] = jnp.zeros_like(acc_ref)

        def emit(rows, wins, dact):
            n = chunk + halo
            (ua, za), (ub, zb) = _ffn_conv(wins[0], w_ref, b_ref, 0), _ffn_conv(wins[1], w_ref, b_ref, 1)
            dact = dact.astype(F32)
            sg = _sigmoid(za)
            dzs = (dact * zb * (sg * (1.0 + za * (1.0 - sg))), dact * (za * sg))
            for p, (dz, um) in enumerate(zip(dzs, (ua, ub))):
                ahead = (dz[:chunk], pltpu.roll(dz, n - 1, 0)[:chunk], pltpu.roll(dz, n - 2, 0)[:chunk])
                um = um[:chunk]
                acc_ref[p, FFN_CONV_WIDTH] += _fold8(ahead[0])
                du = None
                for j, dzj in enumerate(ahead):
                    k = FFN_CONV_WIDTH - 1 - j
                    acc_ref[p, k] += _fold8(dzj * um)
                    term = w_ref[p, k:k + 1, :] * dzj
                    du = term if du is None else du + term
                du_ref[p, rows, :] = du.astype(BF16)

        emit(pl.ds(0, chunk),
             [jnp.concatenate([_zero_if(first, up_ref[p]), u_ref[p, 0:chunk + halo, :]], axis=0) for p in range(2)],
             da_ref[0:chunk + halo, :])

        def step(c, carry):
            s = pl.multiple_of(c * chunk, chunk)
            emit(pl.ds(s, chunk), [u_ref[p, pl.ds(s - halo, chunk + 2 * halo), :] for p in range(2)],
                 da_ref[pl.ds(s, chunk + halo), :])
            return carry

        lax.fori_loop(1, tm // chunk - 1, step, 0)
        s = tm - chunk
        emit(pl.ds(s, chunk),
             [jnp.concatenate([u_ref[p, s - halo:tm, :], _zero_if(last, un_ref[p])], axis=0) for p in range(2)],
             jnp.concatenate([da_ref[s:tm, :], _zero_if(last, dan_ref[...])], axis=0))

        @pl.when(i == 0)
        def _():
            dw_ref[...] = jnp.zeros_like(dw_ref)
            db_ref[...] = jnp.zeros_like(db_ref)

        for p in range(2):
            for k in range(FFN_CONV_WIDTH):
                dw_ref[p, k:k + 1, :] += _colsum(acc_ref[p, k])
            db_ref[p] += _colsum(acc_ref[p, FFN_CONV_WIDTH])

    return pl.pallas_call(
        body, name=name, grid=(FFN_DIM // FFN_TN, nt),
        in_specs=[main, prev, nxt, pl.BlockSpec((tm, FFN_TN), lambda j, i: (i, j)),
                  pl.BlockSpec((halo, FFN_TN), lambda j, i: (next_i(i), j)), wsp, bsp],
        out_specs=[main, wsp, bsp],
        out_shape=[jax.ShapeDtypeStruct((2, t, FFN_DIM), BF16), jax.ShapeDtypeStruct((2, FFN_CONV_WIDTH, FFN_DIM), F32),
                   jax.ShapeDtypeStruct((2, 1, FFN_DIM), F32)],
        scratch_shapes=[pltpu.VMEM((2, FFN_CONV_WIDTH + 1, 8, FFN_TN), F32)],
        compiler_params=_params("parallel", "arbitrary"),
    )(u, u, u, dact, dact, dw_w, dw_b)


CONV_TM = 256
CONV_ROWS = 128
CONV_LANES = 128


def _glu_window(pa_ref, pah_ref, pg_ref, pgh_ref, scr_ref, first):
    ah, gh = pah_ref[...].astype(F32), pgh_ref[...].astype(F32)
    scr_ref[0:CONV_HALO, :] = jnp.where(first, 0.0, ah * _sigmoid(gh))
    scr_ref[CONV_HALO:, :] = pa_ref[...].astype(F32) * _sigmoid(pg_ref[...].astype(F32))


def _tap_slabs(win, rows, ahead):
    n = win.shape[0]
    for s in range(8):
        ws = win if s == 0 else pltpu.roll(win, n - s if ahead else s, 0)
        for q in range(CONV_HALO // 8):
            o = 8 * q + s
            if o < CONV_WIDTH:
                start = 8 * q if ahead else CONV_HALO - 8 * q
                yield CONV_WIDTH - 1 - o, ws[start:start + rows]


def _conformer_specs(t):
    tm = _tile(t, (CONV_TM, 128))
    hb = tm // CONV_HALO
    d = D_MODEL
    main = lambda c: pl.BlockSpec((tm, d), lambda i: (i, c))
    halo = lambda c: pl.BlockSpec((CONV_HALO, d), lambda i: (jnp.maximum(i * hb - 1, 0), c))
    row = pl.BlockSpec((1, d), lambda i: (0, 0))
    wsp = pl.BlockSpec((CONV_WIDTH, d), lambda i: (0, 0))
    return tm, main, halo, row, wsp


def _conformer_mid(p, dw_w, dw_b, ln_g, ln_b, name):
    t = p.shape[0]
    tm, main, halo, row, wsp = _conformer_specs(t)
    d, lanes = D_MODEL, CONV_LANES

    def body(pa_ref, pah_ref, pg_ref, pgh_ref, w_ref, b_ref, g_ref, lb_ref, o_ref, dc_ref, scr_ref):
        _glu_window(pa_ref, pah_ref, pg_ref, pgh_ref, scr_ref, pl.program_id(0) == 0)
        for c in range(d // lanes):
            ls = slice(c * lanes, (c + 1) * lanes)
            acc = jnp.broadcast_to(b_ref[:, ls], (tm, lanes))
            for k, slab in _tap_slabs(scr_ref[:, ls], tm, False):
                acc = acc + w_ref[k:k + 1, ls] * slab
            dc_ref[:, ls] = acc

        def norm(r, carry):
            r0 = pl.multiple_of(r * 32, 32)
            dc = dc_ref[pl.ds(r0, 32), :]
            xc = dc - jnp.mean(dc, axis=-1, keepdims=True)
            ln = xc * lax.rsqrt(jnp.mean(xc * xc, axis=-1, keepdims=True) + EPS) * g_ref[...] + lb_ref[...]
            o_ref[pl.ds(r0, 32), :] = (ln * _sigmoid(ln)).astype(BF16)
            return carry

        lax.fori_loop(0, tm // 32, norm, 0)

    return pl.pallas_call(
        body, name=name, grid=(t // tm,), in_specs=[main(0), halo(0), main(1), halo(1), wsp, row, row, row],
        out_specs=[main(0), main(0)], out_shape=[jax.ShapeDtypeStruct((t, d), BF16), jax.ShapeDtypeStruct((t, d), F32)],
        scratch_shapes=[pltpu.VMEM((tm + CONV_HALO, d), F32)], compiler_params=_params("parallel"),
    )(p, p, p, p, dw_w, dw_b, ln_g, ln_b)


def _conformer_mid_bwd(p, dc, ds, ln_g, ln_b, name):
    t = p.shape[0]
    tm, main, halo, row, wsp = _conformer_specs(t)
    d, nt = D_MODEL, t // tm
    rows, lanes = CONV_ROWS, CONV_LANES

    def body(pa_ref, pah_ref, pg_ref, pgh_ref, dc_ref, ds_ref, g_ref, lb_ref,
             ddc_ref, dw_ref, db_ref, dg_ref, dlb_ref, scr_ref, wacc_ref, racc_ref):
        i = pl.program_id(0)

        @pl.when(i == 0)
        def _():
            wacc_ref[...] = jnp.zeros_like(wacc_ref)
            racc_ref[...] = jnp.zeros_like(racc_ref)

        _glu_window(pa_ref, pah_ref, pg_ref, pgh_ref, scr_ref, i == 0)

        def norm_bwd(r, carry):
            r0 = pl.multiple_of(r * 32, 32)
            dcv = dc_ref[pl.ds(r0, 32), :]
            xc = dcv - jnp.mean(dcv, axis=-1, keepdims=True)
            rstd = lax.rsqrt(jnp.mean(xc * xc, axis=-1, keepdims=True) + EPS)
            xhat = xc * rstd
            ln = xhat * g_ref[...] + lb_ref[...]
            sg = _sigmoid(ln)
            dln = ds_ref[pl.ds(r0, 32), :].astype(F32) * (sg * (1.0 + ln * (1.0 - sg)))
            dxh = dln * g_ref[...]
            ddc = rstd * (dxh - jnp.mean(dxh, axis=-1, keepdims=True) - xhat * jnp.mean(dxh * xhat, axis=-1, keepdims=True))
            ddc_ref[pl.ds(r0, 32), :] = ddc
            racc_ref[0] += _fold8(dln * xhat)
            racc_ref[1] += _fold8(dln)
            racc_ref[2] += _fold8(ddc)
            return carry

        lax.fori_loop(0, tm // 32, norm_bwd, 0)

        for c in range(d // lanes):
            ls = slice(c * lanes, (c + 1) * lanes)

            def taps(r, carry, ls=ls):
                r0 = pl.multiple_of(r * rows, rows)
                ddc = ddc_ref[pl.ds(r0, rows), ls]
                for k, slab in _tap_slabs(scr_ref[pl.ds(r0, rows + CONV_HALO), ls], rows, False):
                    wacc_ref[k, :, ls] += _fold8(ddc * slab)
                return carry

            lax.fori_loop(0, tm // rows, taps, 0)

        @pl.when(i == nt - 1)
        def _():
            for k in range(CONV_WIDTH):
                dw_ref[k:k + 1, :] = _colsum(wacc_ref[k])
            dg_ref[...] = _colsum(racc_ref[0])
            dlb_ref[...] = _colsum(racc_ref[1])
            db_ref[...] = _colsum(racc_ref[2])

    return pl.pallas_call(
        body, name=name, grid=(nt,), in_specs=[main(0), halo(0), main(1), halo(1), main(0), main(0), row, row],
        out_specs=[main(0), wsp, row, row, row],
        out_shape=[jax.ShapeDtypeStruct((t, d), F32), jax.ShapeDtypeStruct((CONV_WIDTH, d), F32)]
        + [jax.ShapeDtypeStruct((1, d), F32)] * 3,
        scratch_shapes=[pltpu.VMEM((tm + CONV_HALO, d), F32), pltpu.VMEM((CONV_WIDTH, 8, d), F32), pltpu.VMEM((3, 8, d), F32)],
        compiler_params=_params("arbitrary"),
    )(p, p, p, p, dc, ds, ln_g, ln_b)


def _conformer_glu_bwd(p, ddc, dw_w, name):
    t = p.shape[0]
    d = D_MODEL
    tm = _tile(t, (CONV_TM, 128))
    hb = tm // CONV_HALO
    nt = t // tm
    last_halo = t // CONV_HALO - 1
    rows, lanes = CONV_ROWS, CONV_LANES
    col = lambda c: pl.BlockSpec((tm, d), lambda i: (i, c))
    nxt = pl.BlockSpec((CONV_HALO, d), lambda i: (jnp.minimum((i + 1) * hb, last_halo), 0))

    def body(pa_ref, pg_ref, ddc_ref, ddcn_ref, w_ref, dp_ref, db_ref, scr_ref, acc_ref):
        i = pl.program_id(0)

        @pl.when(i == 0)
        def _():
            acc_ref[...] = jnp.zeros_like(acc_ref)

        scr_ref[0:tm, :] = ddc_ref[...]
        scr_ref[tm:, :] = _zero_if(i == nt - 1, ddcn_ref[...])
        for c in range(d // lanes):
            ls = slice(c * lanes, (c + 1) * lanes)
            gs = slice(d + c * lanes, d + (c + 1) * lanes)

            def taps(r, carry, ls=ls, gs=gs):
                r0 = pl.multiple_of(r * rows, rows)
                dglu = None
                for k, slab in _tap_slabs(scr_ref[pl.ds(r0, rows + CONV_HALO), ls], rows, True):
                    term = w_ref[k:k + 1, ls] * slab
                    dglu = term if dglu is None else dglu + term
                a = pa_ref[pl.ds(r0, rows), ls].astype(F32)
                sg = _sigmoid(pg_ref[pl.ds(r0, rows), ls].astype(F32))
                da = (dglu * sg).astype(BF16)
                dg = (dglu * a * sg * (1.0 - sg)).astype(BF16)
                dp_ref[pl.ds(r0, rows), ls] = da
                dp_ref[pl.ds(r0, rows), gs] = dg
                acc_ref[:, ls] += _fold8(da.astype(F32))
                acc_ref[:, gs] += _fold8(dg.astype(F32))
                return carry

            lax.fori_loop(0, tm // rows, taps, 0)

        @pl.when(i == nt - 1)
        def _():
            db_ref[...] = _colsum(acc_ref[...])

    return pl.pallas_call(
        body, name=name, grid=(nt,),
        in_specs=[col(0), col(1), col(0), nxt, pl.BlockSpec((CONV_WIDTH, d), lambda i: (0, 0))],
        out_specs=[pl.BlockSpec((tm, 2 * d), lambda i: (i, 0)), pl.BlockSpec((1, 2 * d), lambda i: (0, 0))],
        out_shape=[jax.ShapeDtypeStruct((t, 2 * d), BF16), jax.ShapeDtypeStruct((1, 2 * d), F32)],
        scratch_shapes=[pltpu.VMEM((tm + CONV_HALO, d), F32), pltpu.VMEM((8, 2 * d), F32)],
        compiler_params=_params("arbitrary"),
    )(p, p, ddc, ddc, dw_w)


def _colsum_call(a, name):
    t, n = a.shape
    tm = _tile(t)

    def body(a_ref, o_ref):
        @pl.when(pl.program_id(0) == 0)
        def _():
            o_ref[...] = jnp.zeros_like(o_ref)

        o_ref[...] += _colsum(a_ref[...].astype(F32))

    return pl.pallas_call(
        body, name=name, grid=(t // tm,), in_specs=[pl.BlockSpec((tm, n), lambda i: (i, 0))],
        out_specs=pl.BlockSpec((1, n), lambda i: (0, 0)), out_shape=jax.ShapeDtypeStruct((1, n), F32),
        compiler_params=_params("arbitrary"),
    )(a)


def _ada_fwd(c_all, w, name):
    rows, d = c_all.shape
    n = w.shape[1]
    tn = _tile(n, (256, 128))

    def body(c_ref, w_ref, o_ref):
        c = c_ref[...]
        o_ref[...] = _dot((c * _sigmoid(c)).astype(BF16), w_ref[...].astype(BF16), _NN)

    return pl.pallas_call(
        body, name=name, grid=(n // tn,),
        in_specs=[pl.BlockSpec((rows, d), lambda j: (0, 0)), pl.BlockSpec((d, tn), lambda j: (0, j))],
        out_specs=pl.BlockSpec((rows, tn), lambda j: (0, j)), out_shape=jax.ShapeDtypeStruct((rows, n), F32),
        compiler_params=_params("parallel"),
    )(c_all, w)


def _ada_bwd(c_all, dmod, name):
    rows, d = c_all.shape
    n = dmod.shape[1]
    tn = _tile(n, (256, 128))

    def body(c_ref, g_ref, o_ref):
        c = c_ref[...]
        o_ref[...] = _dot((c * _sigmoid(c)).astype(BF16), g_ref[...].astype(BF16), _TN)

    return pl.pallas_call(
        body, name=name, grid=(n // tn,),
        in_specs=[pl.BlockSpec((rows, d), lambda j: (0, 0)), pl.BlockSpec((rows, tn), lambda j: (0, j))],
        out_specs=pl.BlockSpec((d, tn), lambda j: (0, j)), out_shape=jax.ShapeDtypeStruct((d, n), F32),
        compiler_params=_params("parallel"),
    )(c_all, dmod)


def _sum_slots(a, name):
    s, r, c = a.shape
    tr = _row_tile(r, 256)

    def body(a_ref, o_ref):
        acc = a_ref[0].astype(F32)
        for k in range(1, s):
            acc = acc + a_ref[k].astype(F32)
        o_ref[...] = acc

    return pl.pallas_call(
        body, name=name, grid=(r // tr,), in_specs=[pl.BlockSpec((s, tr, c), lambda i: (0, i, 0))],
        out_specs=pl.BlockSpec((tr, c), lambda i: (i, 0)), out_shape=jax.ShapeDtypeStruct((r, c), F32),
        compiler_params=_params("parallel"),
    )(a)


def _adamw_update(w, g, m, v):
    nm = ADAM_B1 * m + (1.0 - ADAM_B1) * g
    nv = ADAM_B2 * v + (1.0 - ADAM_B2) * (g * g)
    m_hat = nm * (1.0 / (1.0 - ADAM_B1 ** ADAM_STEP))
    v_hat = nv * (1.0 / (1.0 - ADAM_B2 ** ADAM_STEP))
    return -ADAM_LR * (m_hat / (jnp.sqrt(v_hat) + ADAM_EPS) + ADAM_WD * w), nm, nv


def _adamw(w, g, m, v, name):
    l, r, c = w.shape
    tr = _row_tile(r, 256)
    blk = pl.BlockSpec((None, tr, c), lambda k, i: (k, i, 0))

    def body(w_ref, g_ref, m_ref, v_ref, d_ref, nm_ref, nv_ref):
        d_ref[...], nm_ref[...], nv_ref[...] = _adamw_update(w_ref[...], g_ref[...], m_ref[...], v_ref[...])

    return pl.pallas_call(
        body, name=name, grid=(l, r // tr), in_specs=[blk] * 4, out_specs=[blk] * 3,
        out_shape=[jax.ShapeDtypeStruct(w.shape, F32)] * 3, compiler_params=_params("parallel", "parallel"),
    )(w, g, m, v)


def _adamw_small(ws, gs, ms, vs, name):
    n = len(ws)
    two_d = lambda a: a.reshape(-1, a.shape[-1])

    def body(*refs):
        ins, outs = refs[:4 * n], refs[4 * n:]
        for a in range(n):
            outs[a][...], outs[n + a][...], outs[2 * n + a][...] = _adamw_update(*[ins[k * n + a][...] for k in range(4)])

    res = pl.pallas_call(
        body, name=name, out_shape=[jax.ShapeDtypeStruct(two_d(w).shape, F32) for w in ws] * 3,
    )(*[two_d(a) for a in (*ws, *gs, *ms, *vs)])
    return [[res[k * n + a].reshape(ws[a].shape) for a in range(n)] for k in range(3)]


def _mesh_pos():
    return lax.axis_index("x"), lax.axis_index("y"), lax.axis_index("c")


def _all_gather_vmem(x_shard, name):
    m_per, n = x_shard.shape

    def body(x_ref, out_ref, send_sems, recv_sems, local_sem):
        x, y, c = _mesh_pos()
        me, sibling = (x, y, c), (x, y, 1 - c)
        chips = [(1 - x, y), (x, 1 - y), (1 - x, 1 - y)]

        def rows(px, py, pc):
            return out_ref.at[pl.ds((4 * px + 2 * py + pc) * m_per, m_per), :]

        def copy(k, block, to, src=None):
            return pltpu.make_async_remote_copy(
                src_ref=rows(*block) if src is None else src, dst_ref=rows(*block),
                send_sem=send_sems.at[k], recv_sem=recv_sems.at[k], device_id=to, device_id_type=MESH)

        mine = pltpu.make_async_copy(x_ref, rows(*me), local_sem)
        mine.start()
        first = [copy(0, me, sibling, src=x_ref)]
        first += [copy(1 + j, me, (*chip, c), src=x_ref) for j, chip in enumerate(chips)]
        for cp in first:
            cp.start()
        passed = [copy(4 + j, (*chip, c), sibling) for j, chip in enumerate(chips)]
        for j, chip in enumerate(chips):
            copy(1 + j, (*chip, c), me).wait_recv()
            passed[j].start()
        copy(0, sibling, me).wait_recv()
        for j, chip in enumerate(chips):
            copy(4 + j, (*chip, 1 - c), me).wait_recv()
        for cp in first + passed:
            cp.wait_send()
        mine.wait()

    return pl.pallas_call(
        body, name=name, out_shape=jax.ShapeDtypeStruct((N_DEV * m_per, n), x_shard.dtype),
        in_specs=[pl.BlockSpec(memory_space=pltpu.VMEM)], out_specs=pl.BlockSpec(memory_space=pltpu.VMEM),
        scratch_shapes=[pltpu.SemaphoreType.DMA((7,)), pltpu.SemaphoreType.DMA((7,)), pltpu.SemaphoreType.DMA],
    )(x_shard)


def _all_gather_hbm(shards, name):
    n = len(shards)
    out_shape = [jax.ShapeDtypeStruct((N_DEV,) + s.shape, s.dtype) for s in shards]

    def body(*refs):
        x_refs, out_refs = refs[:n], refs[n:2 * n]
        send_sems, recv_sems, local_sems = refs[2 * n:]
        x, y, c = _mesh_pos()
        me, sibling = (x, y, c), (x, y, 1 - c)
        chips = [(1 - x, y), (x, 1 - y), (1 - x, 1 - y)]

        def blk(a, p):
            return out_refs[a].at[4 * p[0] + 2 * p[1] + p[2]]

        def copy(a, k, block, to, src=None):
            return pltpu.make_async_remote_copy(
                src_ref=blk(a, block) if src is None else src, dst_ref=blk(a, block),
                send_sem=send_sems.at[7 * a + k], recv_sem=recv_sems.at[7 * a + k], device_id=to, device_id_type=MESH)

        mine = [pltpu.make_async_copy(x_refs[a], blk(a, me), local_sems.at[a]) for a in range(n)]
        for cp in mine:
            cp.start()
        first = []
        for a in range(n):
            first.append(copy(a, 0, me, sibling, src=x_refs[a]))
            first += [copy(a, 1 + j, me, (*chip, c), src=x_refs[a]) for j, chip in enumerate(chips)]
        for cp in first:
            cp.start()
        passed = []
        for j, chip in enumerate(chips):
            for a in range(n):
                copy(a, 1 + j, (*chip, c), me).wait_recv()
                fwd = copy(a, 4 + j, (*chip, c), sibling)
                fwd.start()
                passed.append(fwd)
        for a in range(n):
            copy(a, 0, sibling, me).wait_recv()
            for j, chip in enumerate(chips):
                copy(a, 4 + j, (*chip, 1 - c), me).wait_recv()
        for cp in first + passed:
            cp.wait_send()
        for cp in mine:
            cp.wait()

    any_spec = pl.BlockSpec(memory_space=pl.ANY)
    return pl.pallas_call(
        body, name=name, out_shape=out_shape, in_specs=[any_spec] * n, out_specs=[any_spec] * n,
        scratch_shapes=[pltpu.SemaphoreType.DMA((7 * n,)), pltpu.SemaphoreType.DMA((7 * n,)), pltpu.SemaphoreType.DMA((n,))],
    )(*shards)


def _peers(x, y, c):
    flip = lambda v, f: 1 - v if f else v
    return [(flip(x, m & 4), flip(y, m & 2), flip(c, m & 1)) for m in range(1, N_DEV)]


def _dev_index(p):
    return 4 * p[0] + 2 * p[1] + p[2]


def _push_copies(src_refs, land_refs, send_sems, recv_sems, scatter, receive):
    x, y, c = _mesh_pos()
    me = _dev_index((x, y, c))
    copies = []
    for a, (src, land) in enumerate(zip(src_refs, land_refs)):
        for k, p in enumerate(_peers(x, y, c)):
            copies.append(pltpu.make_async_remote_copy(
                src_ref=src.at[_dev_index(p)] if scatter else src, dst_ref=land.at[_dev_index(p) if receive else me],
                send_sem=send_sems.at[7 * a + k], recv_sem=recv_sems.at[7 * a + k], device_id=p, device_id_type=MESH))
    return copies


_HBM = pl.BlockSpec(memory_space=pltpu.HBM)
_SEM = pl.BlockSpec(memory_space=pltpu.SEMAPHORE)
_EFFECT = pltpu.SideEffectType.DATAFLOW_SIDE_EFFECTING


def _pushes_start(srcs, lands, scatter, name):
    n = len(srcs)

    def body(*refs):
        src_refs, land_refs = refs[:n], refs[n:2 * n]
        send_sems, recv_sems = refs[2 * n], refs[2 * n + 1]
        token = refs[-1]
        for cp in _push_copies(src_refs, land_refs, send_sems, recv_sems, scatter, receive=False):
            cp.start()
        token[...] = jnp.zeros_like(token)

    hbm = lambda a: pltpu.HBM(a.shape, a.dtype)
    sems = pltpu.SemaphoreType.DMA((7 * n,))
    outs = pl.pallas_call(
        body, name=name,
        out_shape=(sems, sems, *[hbm(a) for a in srcs], *[hbm(a) for a in lands], jax.ShapeDtypeStruct((8, 128), F32)),
        in_specs=[_HBM] * (2 * n), out_specs=(_SEM, _SEM, *[_HBM] * (2 * n), pl.BlockSpec(memory_space=pltpu.VMEM)),
        input_output_aliases={i: 2 + i for i in range(2 * n)},
        compiler_params=pltpu.CompilerParams(has_side_effects=_EFFECT),
    )(*[pltpu.with_memory_space_constraint(a, pltpu.HBM) for a in (*srcs, *lands)])
    return (outs[0], outs[1], outs[2:2 + n], outs[2 + n:2 + 2 * n], scatter), outs[-1]


def _pushes_wait(handle, after, name):
    send_sems, recv_sems, srcs, lands, scatter = handle
    n = len(srcs)

    def body(*refs):
        src_refs, land_refs = refs[:n], refs[n:2 * n]
        for cp in _push_copies(src_refs, land_refs, refs[2 * n], refs[2 * n + 1], scatter, receive=True):
            cp.wait_send()
            cp.wait_recv()

    hbm = lambda a: pltpu.HBM(a.shape, a.dtype)
    outs = pl.pallas_call(
        body, name=name, out_shape=tuple(hbm(a) for a in (*srcs, *lands)),
        in_specs=[_HBM] * (2 * n) + [_SEM, _SEM, pl.BlockSpec(memory_space=pl.ANY)], out_specs=tuple([_HBM] * (2 * n)),
        input_output_aliases={i: i for i in range(2 * n)},
        compiler_params=pltpu.CompilerParams(has_side_effects=_EFFECT),
    )(*srcs, *lands, send_sems, recv_sems, after)
    return outs[n:]


def _landing_zones(srcs, scatter, name):
    n = len(srcs)

    def body(*refs):
        src_refs, land_refs, sems = refs[:n], refs[n:2 * n], refs[2 * n]
        me = _dev_index(_mesh_pos())
        copies = [pltpu.make_async_copy(src.at[me] if scatter else src, land.at[me], sems.at[a])
                  for a, (src, land) in enumerate(zip(src_refs, land_refs))]
        for cp in copies:
            cp.start()
        for cp in copies:
            cp.wait()

    any_spec = pl.BlockSpec(memory_space=pl.ANY)
    return pl.pallas_call(
        body, name=name, out_shape=[jax.ShapeDtypeStruct((N_DEV,) + s.shape[-2:], s.dtype) for s in srcs],
        in_specs=[any_spec] * n, out_specs=[any_spec] * n, scratch_shapes=[pltpu.SemaphoreType.DMA((n,))],
    )(*srcs)


def _ffn_forward(x, mod, norm_g, w, tag):
    sh, sc, gate = mod
    h = _modnorm(x, norm_g, sc, sh, f"{tag}_norm")
    u = _ffn_up(h, w["up_t"], f"{tag}_up")
    act = _ffn_act(u, w["dw_w"], w["dw_b"], f"{tag}_act")
    y, x_new = _matmul(act, w["down"], "nn", F32, f"{tag}_down", resid=(x, gate))
    return x_new, (x, h, u, act, y)


def _behind(row, token):
    return row if token is None else row + token[0:1, 0:1]


def _ffn_backward(dx_new, saved, mod, norm_g, w, tag, emit):
    x, h, u, act, y = saved
    _, sc, gate = mod
    dy, d_gate = _gate_bwd(dx_new, y, gate, f"{tag}_gate_bwd")
    d_down = _matmul_tn_acc(act, dy, f"{tag}_down_dw")
    dact = _matmul(dy, w["down"], "nt", BF16, f"{tag}_down_dx")
    du, d_dw_w, d_dw_b = _ffn_act_bwd(u, dact, w["dw_w"], w["dw_b"], f"{tag}_act_bwd")
    d_up_t = _matmul_tn_acc(du, h, f"{tag}_up_dw").reshape(2 * FFN_DIM, -1)
    token = emit([d_up_t, d_down])
    dh = _ffn_up_dx(du, w["up_t"], f"{tag}_up_dx")
    dx, d_w, d_sh = _modnorm_bwd(x, dh, norm_g, _behind(sc, token), dx_new, f"{tag}_norm_bwd")
    return dx, dict(dw_w=d_dw_w.transpose(1, 0, 2).reshape(FFN_CONV_WIDTH, 2 * FFN_DIM),
                    dw_b=d_dw_b.reshape(1, 2 * FFN_DIM), norm_g=d_w * (1.0 + sc), sh=d_sh, sc=d_w * norm_g, gate=d_gate)


def _mixer_forward(x, mod, norm_g, w, rope, tag):
    sh, sc, gate = mod
    h = _modnorm(x, norm_g, sc, sh, f"{tag}_norm")
    z = _matmul(h, w["w_in_t"], "nt", BF16, f"{tag}_in")
    ya = _gmlp_fwd(z, w["gain"], w["wtril"], w["bias_exp"], f"{tag}_gmlp")
    q, k, v = _qk_prep(z, rope[0], rope[1], w["gq"], w["gk"], w["seg"], f"{tag}_qk")
    outs, lses = [], []
    for _, dil in PATTERNS:
        o, l = _attn_fwd(q, k, v, dil, f"{tag}_attn_d{dil}")
        outs.append(o)
        lses.append(l)
    yb, lse = _attn_merge(outs, lses, f"{tag}_merge")
    cat = jnp.concatenate([ya, yb], axis=1)
    y, x_new = _matmul(cat, w["w_out"], "nn", F32, f"{tag}_out", resid=(x, gate))
    return x_new, (x, h, z, q, k, v, yb, lse, cat, y)


def _mixer_backward(dx_new, saved, mod, norm_g, w, rope, tag, emit):
    x, h, z, q, k, v, yb, lse, cat, y = saved
    _, sc, gate = mod
    dy, d_gate = _gate_bwd(dx_new, y, gate, f"{tag}_gate_bwd")
    d_w_out = _matmul_tn_acc(cat, dy, f"{tag}_out_dw")
    dcat = _matmul(dy, w["w_out"], "nt", BF16, f"{tag}_out_dx")
    dz_a, d_sp_w, d_gain, d_bias_exp = _gmlp_bwd(z, dcat, w["gain"], w["wtril"], w["wtril_t"], w["bias_exp"], f"{tag}_gmlp_bwd")
    dyb = dcat[:, A_WIDTH:]
    dqs, dks, dvs = [], [], []
    for _, dil in PATTERNS:
        dq, dk, dv = _attn_bwd(q, k, v, dyb, yb, lse, dil, f"{tag}_attn_bwd_d{dil}")
        dqs.append(dq)
        dks.append(dk)
        dvs.append(dv)
    dz_qkv, d_gq, d_gk = _qk_prep_bwd(z, dqs, dks, dvs, rope[0], rope[1], w["gq"], w["gk"], w["seg"], f"{tag}_qk_bwd")
    dz = jnp.concatenate([dz_a, dz_qkv], axis=1)
    d_w_in_t = _matmul_tn_acc(dz, h, f"{tag}_in_dw")
    token = emit([d_w_in_t, d_w_out])
    dh = _matmul(dz, w["w_in_t"], "nn", F32, f"{tag}_in_dx")
    dx, d_w, d_sh = _modnorm_bwd(x, dh, norm_g, _behind(sc, token), dx_new, f"{tag}_norm_bwd")
    return dx, dict(
        vnorm_g=d_gain.reshape(A_GROUPS, GROUP_DIM), spatial_w=d_sp_w,
        spatial_b=d_bias_exp.reshape(CHUNK, A_GROUPS, GROUP_DIM).sum(-1).T,
        q_norm_g=d_gq.reshape(HEADS, HEAD_DIM).sum(0), k_norm_g=d_gk.reshape(HEADS, HEAD_DIM).sum(0),
        norm_g=d_w * (1.0 + sc), sh=d_sh, sc=d_w * norm_g, gate=d_gate)


def _conformer_forward(x, mod, norm_g, w, tag):
    sh, sc, gate = mod
    h = _modnorm(x, norm_g, sc, sh, f"{tag}_norm")
    p = _matmul(h, w["pw1_t"], "nt", BF16, f"{tag}_pw1", bias=w["pw1_b"])
    s, dc = _conformer_mid(p, w["dw_w"], w["dw_b"], w["ln_g"], w["ln_b"], f"{tag}_mid")
    y, x_new = _matmul(s, w["pw2"], "nn", F32, f"{tag}_pw2", bias=w["pw2_b"], resid=(x, gate))
    return x_new, (x, h, p, dc, s, y)


def _conformer_backward(dx_new, saved, mod, norm_g, w, tag, emit):
    x, h, p, dc, s, y = saved
    _, sc, gate = mod
    dy, d_gate = _gate_bwd(dx_new, y, gate, f"{tag}_gate_bwd")
    d_pw2 = _matmul_tn_acc(s, dy, f"{tag}_pw2_dw")
    d_pw2_b = _colsum_call(dy, f"{tag}_pw2_db")
    ds = _matmul(dy, w["pw2"], "nt", BF16, f"{tag}_pw2_dx")
    ddc, d_dw_w, d_dw_b, d_ln_g, d_ln_b = _conformer_mid_bwd(p, dc, ds, w["ln_g"], w["ln_b"], f"{tag}_mid_bwd")
    dp, d_pw1_b = _conformer_glu_bwd(p, ddc, w["dw_w"], f"{tag}_glu_bwd")
    d_pw1_t = _matmul_tn_acc(dp, h, f"{tag}_pw1_dw")
    token = emit([d_pw1_t, d_pw2])
    dh = _matmul(dp, w["pw1_t"], "nn", F32, f"{tag}_pw1_dx")
    dx, d_w, d_sh = _modnorm_bwd(x, dh, norm_g, _behind(sc, token), dx_new, f"{tag}_norm_bwd")
    return dx, dict(pw1_b=d_pw1_b, dw_w=d_dw_w, dw_b=d_dw_b, ln_g=d_ln_g, ln_b=d_ln_b, pw2_b=d_pw2_b, norm_g=d_w * (1.0 + sc), sh=d_sh, sc=d_w * norm_g, gate=d_gate)


def _local_step(x, target, pos, mod, norm_mix_g, norm_ffn_g, mixer_w, conv_w, ffn_w, fetch, emit):
    d = D_MODEL
    inv_freq = 1.0 / (ROPE_THETA ** (jnp.arange(0, HEAD_DIM, 2, dtype=F32) / HEAD_DIM))
    inv_freq = jnp.tile(inv_freq, 2 * HEADS)[None, :]
    sign = jnp.tile(jnp.concatenate([-jnp.ones(HEAD_DIM // 2, F32), jnp.ones(HEAD_DIM // 2, F32)]), HEADS)[None, :]
    rope = _rope_tables(pos, inv_freq, sign, "rope_tables")
    mods = [[mod[l:l + 1, i * d:(i + 1) * d] for i in range(6)] for l in range(2)]
    mix = [(m[0], m[1], m[2]) for m in mods]
    ffn = [(m[3], m[4], m[5]) for m in mods]
    gm = [norm_mix_g[l:l + 1] for l in range(2)]
    gf = [norm_ffn_g[l:l + 1] for l in range(2)]

    mixer_w = {**mixer_w, **fetch("l0_mix", x)}
    x1, s_mix = _mixer_forward(x, mix[0], gm[0], mixer_w, rope, "l0_mix")
    ffn_w0 = {**ffn_w[0], **fetch("l0_ffn", x1)}
    x2, s_ffn0 = _ffn_forward(x1, ffn[0], gf[0], ffn_w0, "l0_ffn")
    conv_w = {**conv_w, **fetch("l1_conv", x2)}
    x3, s_conv = _conformer_forward(x2, mix[1], gm[1], conv_w, "l1_conv")
    ffn_w1 = {**ffn_w[1], **fetch("l1_ffn", x3)}
    x4, s_ffn1 = _ffn_forward(x3, ffn[1], gf[1], ffn_w1, "l1_ffn")
    dx, loss = _loss_head(x4, target, "loss_head")
    dx, g_ffn1 = _ffn_backward(dx, s_ffn1, ffn[1], gf[1], ffn_w1, "l1_ffn", functools.partial(emit, "l1_ffn"))
    dx, g_conv = _conformer_backward(dx, s_conv, mix[1], gm[1], conv_w, "l1_conv", functools.partial(emit, "l1_conv"))
    dx, g_ffn0 = _ffn_backward(dx, s_ffn0, ffn[0], gf[0], ffn_w0, "l0_ffn", functools.partial(emit, "l0_ffn"))
    dx, g_mix = _mixer_backward(dx, s_mix, mix[0], gm[0], mixer_w, rope, "l0_mix", functools.partial(emit, "l0_mix"))
    blocks = [g_mix, g_ffn0, g_conv, g_ffn1]
    dmod = jnp.stack([jnp.concatenate([a["sh"], a["sc"], a["gate"], b["sh"], b["sc"], b["gate"]], axis=1)[0]
                      for a, b in ((g_mix, g_ffn0), (g_conv, g_ffn1))])
    return loss, dx, dmod, blocks


def _pack(arrs, rows=8):
    flat = jnp.concatenate([a.reshape(-1).astype(F32) for a in arrs])
    n = flat.shape[0]
    cols = -(-n // (rows * 128)) * 128
    return jnp.pad(flat, (0, rows * cols - n)).reshape(rows, cols)


def _unpack(flat, shapes):
    out, off = [], 0
    for shp in shapes:
        n = math.prod(shp)
        out.append(flat[..., off:off + n].reshape(flat.shape[:-1] + tuple(shp)))
        off += n
    return out


def _take_block(a, idx, size, axis):
    return lax.dynamic_slice_in_dim(a, idx * size, size, axis)


def kernel(x, c, positions, ada_w, ada_b, norm_mix_g, norm_ffn_g, ab_w_in, a_vnorm_g, a_spatial_w, a_spatial_b, b_q_norm_g, b_k_norm_g, ab_w_out, conv_pw1_w, conv_pw1_b, conv_dw_w, conv_dw_b, conv_ln_g, conv_ln_b, conv_pw2_w, conv_pw2_b, ffn_up_w, ffn_dw_w, ffn_dw_b, ffn_down_w, loss_target, m_ada_w, m_ada_b, m_norm_mix_g, m_norm_ffn_g, m_ab_w_in, m_a_vnorm_g, m_a_spatial_w, m_a_spatial_b, m_b_q_norm_g, m_b_k_norm_g, m_ab_w_out, m_conv_pw1_w, m_conv_pw1_b, m_conv_dw_w, m_conv_dw_b, m_conv_ln_g, m_conv_ln_b, m_conv_pw2_w, m_conv_pw2_b, m_ffn_up_w, m_ffn_dw_w, m_ffn_dw_b, m_ffn_down_w, v_ada_w, v_ada_b, v_norm_mix_g, v_norm_ffn_g, v_ab_w_in, v_a_vnorm_g, v_a_spatial_w, v_a_spatial_b, v_b_q_norm_g, v_b_k_norm_g, v_ab_w_out, v_conv_pw1_w, v_conv_pw1_b, v_conv_dw_w, v_conv_dw_b, v_conv_ln_g, v_conv_ln_b, v_conv_pw2_w, v_conv_pw2_b, v_ffn_up_w, v_ffn_dw_w, v_ffn_dw_b, v_ffn_down_w):
    weights = dict(ada_w=ada_w, ada_b=ada_b, norm_mix_g=norm_mix_g, norm_ffn_g=norm_ffn_g, ab_w_in=ab_w_in, a_vnorm_g=a_vnorm_g, a_spatial_w=a_spatial_w, a_spatial_b=a_spatial_b, b_q_norm_g=b_q_norm_g, b_k_norm_g=b_k_norm_g, ab_w_out=ab_w_out, conv_pw1_w=conv_pw1_w, conv_pw1_b=conv_pw1_b, conv_dw_w=conv_dw_w, conv_dw_b=conv_dw_b, conv_ln_g=conv_ln_g, conv_ln_b=conv_ln_b, conv_pw2_w=conv_pw2_w, conv_pw2_b=conv_pw2_b, ffn_up_w=ffn_up_w, ffn_dw_w=ffn_dw_w, ffn_dw_b=ffn_dw_b, ffn_down_w=ffn_down_w)
    mom1 = dict(ada_w=m_ada_w, ada_b=m_ada_b, norm_mix_g=m_norm_mix_g, norm_ffn_g=m_norm_ffn_g, ab_w_in=m_ab_w_in, a_vnorm_g=m_a_vnorm_g, a_spatial_w=m_a_spatial_w, a_spatial_b=m_a_spatial_b, b_q_norm_g=m_b_q_norm_g, b_k_norm_g=m_b_k_norm_g, ab_w_out=m_ab_w_out, conv_pw1_w=m_conv_pw1_w, conv_pw1_b=m_conv_pw1_b, conv_dw_w=m_conv_dw_w, conv_dw_b=m_conv_dw_b, conv_ln_g=m_conv_ln_g, conv_ln_b=m_conv_ln_b, conv_pw2_w=m_conv_pw2_w, conv_pw2_b=m_conv_pw2_b, ffn_up_w=m_ffn_up_w, ffn_dw_w=m_ffn_dw_w, ffn_dw_b=m_ffn_dw_b, ffn_down_w=m_ffn_down_w)
    mom2 = dict(ada_w=v_ada_w, ada_b=v_ada_b, norm_mix_g=v_norm_mix_g, norm_ffn_g=v_norm_ffn_g, ab_w_in=v_ab_w_in, a_vnorm_g=v_a_vnorm_g, a_spatial_w=v_a_spatial_w, a_spatial_b=v_a_spatial_b, b_q_norm_g=v_b_q_norm_g, b_k_norm_g=v_b_k_norm_g, ab_w_out=v_ab_w_out, conv_pw1_w=v_conv_pw1_w, conv_pw1_b=v_conv_pw1_b, conv_dw_w=v_conv_dw_w, conv_dw_b=v_conv_dw_b, conv_ln_g=v_conv_ln_g, conv_ln_b=v_conv_ln_b, conv_pw2_w=v_conv_pw2_w, conv_pw2_b=v_conv_pw2_b, ffn_up_w=v_ffn_up_w, ffn_dw_w=v_ffn_dw_w, ffn_dw_b=v_ffn_dw_b, ffn_down_w=v_ffn_down_w)
    order = list(weights)
    d, f2 = D_MODEL, 2 * FFN_DIM
    t = x.shape[1]
    me = 4 * lax.axis_index("x") + 2 * lax.axis_index("y") + lax.axis_index("c")
    for window, dil in PATTERNS:
        assert window // dil == Q_BLOCK and t % (dil * Q_BLOCK) == 0

    small_in = [c[0], conv_pw1_b[0], conv_dw_w[0], conv_dw_b[0], conv_ln_g[0], conv_ln_b[0], conv_pw2_b[0], ffn_dw_w]
    g1 = _all_gather_vmem(_pack(small_in, rows=8), "gather_small").reshape(N_DEV, -1)
    c_all, pw1_b, dw_w, dw_b, ln_g, ln_b, pw2_b, fdw_w = _unpack(g1, [a.shape for a in small_in])
    pw1_b, dw_b, ln_g, ln_b, pw2_b = [a.reshape(1, -1) for a in (pw1_b, dw_b, ln_g, ln_b, pw2_b)]
    dw_w = dw_w.transpose(1, 0, 2).reshape(CONV_WIDTH, d)
    fdw_w = fdw_w.transpose(1, 2, 0, 3).reshape(2, FFN_CONV_WIDTH, f2)

    stages = dict(l0_mix=[ab_w_in[0].T, ab_w_out[0]], l0_ffn=[ffn_up_w[0].T, ffn_down_w[0]],
                  l1_conv=[conv_pw1_w[0].T, conv_pw2_w[0]], l1_ffn=[ffn_up_w[1].T, ffn_down_w[1]])
    stages = {k: [s.astype(BF16) for s in v] for k, v in stages.items()}
    names = dict(l0_mix=("w_in_t", "w_out"), l0_ffn=("up_t", "down"), l1_conv=("pw1_t", "pw2"), l1_ffn=("up_t", "down"))
    ready = {"l0_mix": [a.reshape(-1, d) for a in _all_gather_hbm(stages["l0_mix"], "gather_mixer_weights")]}
    stages, _ = lax.optimization_barrier((stages, ready))
    arriving, tokens = {}, []
    for stage, group in (("l0_ffn", ("l0_ffn",)), ("l1_conv", ("l1_conv", "l1_ffn"))):
        srcs = [s for g in group for s in stages[g]]
        arriving[stage], token = _pushes_start(
            srcs, _landing_zones(srcs, False, f"gather_{stage}_zones"), False, f"gather_{stage}_start")
        tokens.append(token)

    def fetch(stage, after):
        if stage in arriving:
            full = [a.reshape(-1, d) for a in _pushes_wait(arriving[stage], after, f"gather_{stage}_wait")]
            ready[stage] = full[:2]
            if stage == "l1_conv":
                ready["l1_ffn"] = full[2:]
        return dict(zip(names[stage], ready[stage]))

    c16 = jnp.pad(c_all, ((0, 2 * N_DEV - c_all.shape[0]), (0, 0)))
    part = jnp.concatenate([_ada_fwd(c16, ada_w[l], f"ada_fwd{l}")[:N_DEV] for l in range(2)], axis=1)
    g2 = _all_gather_vmem(part, "gather_mod").reshape(N_DEV, N_DEV, 2, -1)
    mod = lax.dynamic_index_in_dim(g2, me, axis=1, keepdims=False).transpose(1, 0, 2).reshape(2, 6 * d) + ada_b
    mod = mod + tokens[0][0:1, 0:1] + tokens[1][0:1, 0:1]

    causal = jnp.tril(jnp.ones((CHUNK, CHUNK), bool))
    wtril = jnp.where(causal[None], a_spatial_w[0], 0.0)
    mixer_w = dict(
        gain=a_vnorm_g[0].reshape(1, A_WIDTH), wtril=wtril.astype(BF16),
        wtril_t=wtril.transpose(0, 2, 1).astype(BF16),
        bias_exp=jnp.repeat(a_spatial_b[0].T, GROUP_DIM, axis=1),
        gq=jnp.tile(b_q_norm_g[0], HEADS)[None, :], gk=jnp.tile(b_k_norm_g[0], HEADS)[None, :],
        seg=jnp.kron(jnp.eye(HEADS, dtype=F32), jnp.ones((HEAD_DIM, HEAD_DIM), F32)))
    conv_w = dict(pw1_b=pw1_b, dw_w=dw_w, dw_b=dw_b, ln_g=ln_g, ln_b=ln_b, pw2_b=pw2_b)
    ffn_w = [dict(dw_w=fdw_w[l].reshape(FFN_CONV_WIDTH, 2, FFN_DIM).transpose(1, 0, 2), dw_b=ffn_dw_b[l].reshape(2, 1, FFN_DIM))
             for l in range(2)]

    leaving = {}

    def emit(stage, grads):
        blocks = [g.reshape(N_DEV, g.shape[0] // N_DEV, d) for g in grads]
        leaving[stage], token = _pushes_start(
            blocks, _landing_zones(blocks, True, f"reduce_{stage}_zones"), True, f"reduce_{stage}_start")
        return token

    loss, dx, dmod, (g_mix, g_ffn0, g_conv, g_ffn1) = _local_step(
        x[0], loss_target[0], positions[0].astype(F32)[:, None], mod, norm_mix_g, norm_ffn_g, mixer_w, conv_w, ffn_w,
        fetch, emit)

    def reduced(stage, after):
        lands = _pushes_wait(leaving[stage], after, f"reduce_{stage}_wait")
        return [_sum_slots(a, f"reduce_{stage}_sum{i}") for i, a in enumerate(lands)]

    (r_up_t1, r_down1), (r_pw1_t, r_pw2), (r_up_t0, r_down0) = [reduced(s, dx) for s in ("l1_ffn", "l1_conv", "l0_ffn")]

    small_g = [
        dmod, jnp.concatenate([g_mix["norm_g"], g_conv["norm_g"]]), jnp.concatenate([g_ffn0["norm_g"], g_ffn1["norm_g"]]),
        g_mix["vnorm_g"], g_mix["spatial_w"], g_mix["spatial_b"], g_mix["q_norm_g"], g_mix["k_norm_g"],
        g_conv["pw1_b"], g_conv["dw_w"], g_conv["dw_b"], g_conv["ln_g"], g_conv["ln_b"], g_conv["pw2_b"],
        jnp.stack([g_ffn0["dw_w"], g_ffn1["dw_w"]]), jnp.concatenate([g_ffn0["dw_b"], g_ffn1["dw_b"]])]
    packed = _pack(small_g, rows=8)
    g3 = _all_gather_vmem(packed, "gather_small_grads").reshape(N_DEV, 8, -1)
    total = _unpack(_sum_slots(g3, "sum_small_grads").reshape(-1), [a.shape for a in small_g])
    (s_dmod, s_mix_g, s_ffn_g, s_vnorm, s_sp_w, s_sp_b, s_gq, s_gk, s_pw1_b, s_dw_w, s_dw_b, s_ln_g, s_ln_b,
     s_pw2_b, s_fdw_w, s_fdw_b) = total
    dmod_all = g3.reshape(N_DEV, -1)[:, :2 * 6 * d].reshape(N_DEV, 2, 6 * d)
    n_ada = ada_w.shape[2]
    dmod16 = jnp.pad(_take_block(dmod_all, me, n_ada, 2), ((0, N_DEV), (0, 0), (0, 0)))
    g_ada_w = jnp.stack([_ada_bwd(c16, dmod16[:, l], f"ada_bwd{l}") for l in range(2)])

    grads = dict(
        ada_w=g_ada_w, ada_b=s_dmod, norm_mix_g=s_mix_g, norm_ffn_g=s_ffn_g,
        a_vnorm_g=s_vnorm[None], a_spatial_w=s_sp_w[None], a_spatial_b=s_sp_b[None], b_q_norm_g=s_gq[None],
        b_k_norm_g=s_gk[None], conv_pw1_w=r_pw1_t.T[None],
        conv_pw1_b=_take_block(s_pw1_b, me, conv_pw1_b.shape[1], 1),
        conv_dw_w=_take_block(s_dw_w, me, conv_dw_w.shape[2], 1)[None],
        conv_dw_b=_take_block(s_dw_b, me, conv_dw_b.shape[1], 1), conv_ln_g=_take_block(s_ln_g, me, conv_ln_g.shape[1], 1),
        conv_ln_b=_take_block(s_ln_b, me, conv_ln_b.shape[1], 1), conv_pw2_w=r_pw2[None],
        conv_pw2_b=_take_block(s_pw2_b, me, conv_pw2_b.shape[1], 1),
        ffn_up_w=jnp.stack([r_up_t0.T, r_up_t1.T]), ffn_dw_w=_take_block(s_fdw_w, me, ffn_dw_w.shape[2], 2),
        ffn_dw_b=s_fdw_b, ffn_down_w=jnp.stack([r_down0, r_down1]))

    large = ("ada_w", "conv_pw1_w", "conv_pw2_w", "ffn_up_w", "ffn_down_w", "ab_w_in", "ab_w_out")
    delta, new_m, new_v = {}, {}, {}
    for name in large:
        if name == "ab_w_in":
            r_in_t, r_out = reduced("l0_mix", new_v["ffn_down_w"])
            grads.update(ab_w_in=r_in_t.T[None], ab_w_out=r_out[None])
        delta[name], new_m[name], new_v[name] = _adamw(weights[name], grads[name], mom1[name], mom2[name], f"adamw_{name}")
    small = [n for n in order if n not in large]
    res = _adamw_small(*[[src[n] for n in small] for src in (weights, grads, mom1, mom2)], "adamw_small")
    for dst, arrs in zip((delta, new_m, new_v), res):
        dst.update(zip(small, arrs))

    loss = lax.psum(loss[0, 0], ("x", "y", "c"))
    return (loss, dx[None], *[grads[n] for n in order], *[delta[n] for n in order],
            *[new_m[n] for n in order], *[new_v[n] for n in order])
```

```python
import functools
import math

import jax
import jax.numpy as jnp
from jax import lax
from jax.experimental import pallas as pl
from jax.experimental.pallas import tpu as pltpu

F32 = jnp.float32
BF16 = jnp.bfloat16
MESH = pl.DeviceIdType.MESH

D_MODEL = 1024
A_WIDTH = 512
A_GROUPS = 4
GROUP_DIM = 128
CHUNK = 128
B_WIDTH = 512
HEADS = 8
HEAD_DIM = 64
PATTERNS = ((128, 1), (512, 4), (2048, 16))
Q_BLOCK = 128
ROPE_THETA = 10000.0
AB_IN = 2560
CONV_WIDTH = 31
FFN_DIM = 2816
FFN_CONV_WIDTH = 3
EPS = 1e-6
NEG = -1e30
N_DEV = 8
ADAM_LR, ADAM_B1, ADAM_B2, ADAM_EPS, ADAM_WD, ADAM_STEP = 0.001, 0.9, 0.999, 1e-08, 0.01, 10

V7X_VMEM_LIMIT = 56 * 2**20
BF16_ROWS = 16
FFN_HALO = 16
CONV_HALO = 32

_NN = (((1,), (0,)), ((), ()))
_NT = (((1,), (1,)), ((), ()))
_TN = (((0,), (0,)), ((), ()))


def _tile(n, prefs=(512, 256, 128)):
    for t in prefs:
        if n % t == 0:
            return t
    return n


def _row_tile(n, cap=512):
    best = n
    for t in range(8, min(n, cap) + 1, 8):
        if n % t == 0:
            best = t
    return best if best <= cap else n


def _params(*sem):
    return pltpu.CompilerParams(dimension_semantics=sem, vmem_limit_bytes=V7X_VMEM_LIMIT)


def _dot(a, b, dims):
    return lax.dot_general(a, b, dims, preferred_element_type=F32)


def _sigmoid(x):
    return 1.0 / (1.0 + jnp.exp(-x))


def _gelu(x):
    return 0.5 * x * (1.0 + lax.erf(x * (2.0 ** -0.5)))


def _gelu_grad(x):
    return 0.5 * (1.0 + lax.erf(x * (2.0 ** -0.5))) + x * jnp.exp(-0.5 * x * x) * (1.0 / math.sqrt(2.0 * math.pi))


def _colsum(v):
    return jnp.sum(v, axis=0, keepdims=True)


MATMUL_VMEM_BUDGET = 40 * 2**20


def _matmul_tiles(m, n, k, out_bytes, with_resid):
    def options(dim):
        opts = [t for t in (1024, 512, 256, 128) if dim % t == 0]
        return opts + [dim] if dim <= 4096 and dim not in opts else opts

    best = None
    for tm in options(m):
        for tn in options(n):
            need = 4 * (tm * k + k * tn) + tm * tn * (4 + 2 * out_bytes) + (24 * tm * tn if with_resid else 0)
            if need <= MATMUL_VMEM_BUDGET and (best is None or tm * tn / (tm + tn) > best[0]):
                best = (tm * tn / (tm + tn), tm, tn)
    return best[1], best[2]


def _matmul_tn_acc(a, b, name, tk=512):
    squeeze = a.ndim == 2
    a3 = a[None] if squeeze else a
    p_, t, m = a3.shape
    n = b.shape[1]
    nk = t // tk

    def body(a_ref, b_ref, o_ref, acc_ref):
        kt = pl.program_id(1)

        @pl.when(kt == 0)
        def _():
            acc_ref[...] = jnp.zeros_like(acc_ref)

        acc_ref[...] += _dot(a_ref[...], b_ref[...], _TN)

        @pl.when(kt == nk - 1)
        def _():
            o_ref[...] = acc_ref[...].astype(BF16)

    out = pl.pallas_call(
        body, name=name, grid=(p_, nk),
        in_specs=[pl.BlockSpec((None, tk, m), lambda p, kt: (p, kt, 0)), pl.BlockSpec((tk, n), lambda p, kt: (kt, 0))],
        out_specs=pl.BlockSpec((None, m, n), lambda p, kt: (p, 0, 0)), out_shape=jax.ShapeDtypeStruct((p_, m, n), BF16),
        scratch_shapes=[pltpu.VMEM((m, n), F32)], compiler_params=_params("parallel", "arbitrary"),
    )(a3, b)
    return out[0] if squeeze else out


def _matmul(a, b, mode, out_dtype, name, bias=None, resid=None):
    if mode == "nn":
        (m, k), (_, n) = a.shape, b.shape
    elif mode == "nt":
        (m, k), (n, _) = a.shape, b.shape
    else:
        (k, m), (_, n) = a.shape, b.shape
    tm, tn = _matmul_tiles(m, n, k, jnp.dtype(out_dtype).itemsize, resid is not None)
    dims = {"nn": _NN, "nt": _NT, "tn": _TN}[mode]
    a_spec = pl.BlockSpec((k, tm), lambda i, j: (0, i)) if mode == "tn" else pl.BlockSpec((tm, k), lambda i, j: (i, 0))
    b_spec = pl.BlockSpec((tn, k), lambda i, j: (j, 0)) if mode == "nt" else pl.BlockSpec((k, tn), lambda i, j: (0, j))
    in_specs, args = [a_spec, b_spec], [a, b]
    row_spec = pl.BlockSpec((1, tn), lambda i, j: (0, j))
    tile_spec = pl.BlockSpec((tm, tn), lambda i, j: (i, j))
    if bias is not None:
        in_specs.append(row_spec)
        args.append(bias)
    if resid is not None:
        in_specs += [tile_spec, row_spec]
        args += list(resid)
    out_shape = [jax.ShapeDtypeStruct((m, n), out_dtype)]
    out_specs = [tile_spec]
    if resid is not None:
        out_shape.append(jax.ShapeDtypeStruct((m, n), F32))
        out_specs.append(tile_spec)

    def body(*refs):
        a_ref, b_ref = refs[0], refs[1]
        pos = 2
        acc = _dot(a_ref[...], b_ref[...], dims)
        if bias is not None:
            acc = acc + refs[pos][...]
            pos += 1
        if resid is not None:
            x_ref, g_ref = refs[pos], refs[pos + 1]
            pos += 2
        refs[pos][...] = acc.astype(out_dtype)
        if resid is not None:
            refs[pos + 1][...] = x_ref[...] + g_ref[...] * acc

    outs = pl.pallas_call(
        body, name=name, grid=(m // tm, n // tn), in_specs=in_specs, out_specs=out_specs, out_shape=out_shape,
        compiler_params=_params("parallel", "parallel"),
    )(*args)
    return outs if resid is not None else outs[0]


def _modnorm(x, g, sc, sh, name):
    t, d = x.shape
    tm = _tile(t)
    row = pl.BlockSpec((1, d), lambda i: (0, 0))
    blk = pl.BlockSpec((tm, d), lambda i: (i, 0))

    def body(x_ref, g_ref, sc_ref, sh_ref, o_ref):
        x = x_ref[...]
        r = lax.rsqrt(jnp.mean(x * x, axis=-1, keepdims=True) + EPS)
        o_ref[...] = ((x * r) * g_ref[...] * (1.0 + sc_ref[...]) + sh_ref[...]).astype(BF16)

    return pl.pallas_call(
        body, name=name, grid=(t // tm,), in_specs=[blk, row, row, row], out_specs=blk,
        out_shape=jax.ShapeDtypeStruct((t, d), BF16), compiler_params=_params("parallel"),
    )(x, g, sc, sh)


def _modnorm_bwd(x, dh, g, sc, dres, name):
    t, d = x.shape
    tm = _tile(t)
    row = pl.BlockSpec((1, d), lambda i: (0, 0))
    blk = pl.BlockSpec((tm, d), lambda i: (i, 0))

    def body(x_ref, dh_ref, g_ref, sc_ref, dres_ref, dx_ref, dw_ref, dsh_ref):
        @pl.when(pl.program_id(0) == 0)
        def _():
            dw_ref[...] = jnp.zeros_like(dw_ref)
            dsh_ref[...] = jnp.zeros_like(dsh_ref)

        x = x_ref[...]
        dh = dh_ref[...].astype(F32)
        r = lax.rsqrt(jnp.mean(x * x, axis=-1, keepdims=True) + EPS)
        xn = x * r
        dxn = dh * (g_ref[...] * (1.0 + sc_ref[...]))
        dx_ref[...] = dres_ref[...] + r * (dxn - xn * jnp.mean(dxn * xn, axis=-1, keepdims=True))
        dw_ref[...] += _colsum(dh * xn)
        dsh_ref[...] += _colsum(dh)

    return pl.pallas_call(
        body, name=name, grid=(t // tm,), in_specs=[blk, blk, row, row, blk], out_specs=[blk, row, row],
        out_shape=[jax.ShapeDtypeStruct((t, d), F32), jax.ShapeDtypeStruct((1, d), F32), jax.ShapeDtypeStruct((1, d), F32)],
        compiler_params=_params("arbitrary"),
    )(x, dh, g, sc, dres)


def _gate_bwd(dxn, y, gate, name):
    t, d = dxn.shape
    tm = _tile(t)
    row = pl.BlockSpec((1, d), lambda i: (0, 0))
    blk = pl.BlockSpec((tm, d), lambda i: (i, 0))

    def body(dxn_ref, y_ref, g_ref, dy_ref, dg_ref):
        @pl.when(pl.program_id(0) == 0)
        def _():
            dg_ref[...] = jnp.zeros_like(dg_ref)

        dxn = dxn_ref[...]
        dy_ref[...] = (dxn * g_ref[...]).astype(BF16)
        dg_ref[...] += _colsum(dxn * y_ref[...])

    return pl.pallas_call(
        body, name=name, grid=(t // tm,), in_specs=[blk, blk, row], out_specs=[blk, row],
        out_shape=[jax.ShapeDtypeStruct((t, d), BF16), jax.ShapeDtypeStruct((1, d), F32)],
        compiler_params=_params("arbitrary"),
    )(dxn, y, gate)


def _loss_head(y, target, name):
    t, d = y.shape
    tm = _tile(t)
    blk = pl.BlockSpec((tm, d), lambda i: (i, 0))
    one = pl.BlockSpec((1, 1), lambda i: (0, 0))

    def body(y_ref, t_ref, dy_ref, loss_ref, acc_ref):
        @pl.when(pl.program_id(0) == 0)
        def _():
            acc_ref[...] = jnp.zeros_like(acc_ref)

        e = y_ref[...] - t_ref[...]
        dy_ref[...] = e * (1.0 / d)
        acc_ref[...] += _colsum(e * e)

        @pl.when(pl.program_id(0) == pl.num_programs(0) - 1)
        def _():
            loss_ref[...] = jnp.sum(acc_ref[...], axis=1, keepdims=True) * (0.5 / d)

    return pl.pallas_call(
        body, name=name, grid=(t // tm,), in_specs=[blk, blk], out_specs=[blk, one],
        out_shape=[jax.ShapeDtypeStruct((t, d), F32), jax.ShapeDtypeStruct((1, 1), F32)],
        scratch_shapes=[pltpu.VMEM((1, d), F32)], compiler_params=_params("arbitrary"),
    )(y, target)


def _group_norm(vg, gain):
    mu = jnp.mean(vg, axis=-1, keepdims=True)
    xc = vg - mu
    rstd = lax.rsqrt(jnp.mean(xc * xc, axis=-1, keepdims=True) + EPS)
    xhat = xc * rstd
    return xhat, rstd, xhat * gain


def _gmlp_fwd(z, gain, wtril, bias_exp, name):
    t = z.shape[0]
    zu = pl.BlockSpec((CHUNK, A_WIDTH), lambda i: (i, 0))
    zv = pl.BlockSpec((CHUNK, A_WIDTH), lambda i: (i, 1))
    full2 = lambda shp: pl.BlockSpec(shp, lambda i: (0, 0))
    w_spec = pl.BlockSpec((A_GROUPS, CHUNK, CHUNK), lambda i: (0, 0, 0))

    def body(zu_ref, zv_ref, gain_ref, w_ref, b_ref, ya_ref):
        ua = _gelu(zu_ref[...].astype(F32))
        vg = _gelu(zv_ref[...].astype(F32))
        for g in range(A_GROUPS):
            sl = slice(g * GROUP_DIM, (g + 1) * GROUP_DIM)
            _, _, vn = _group_norm(vg[:, sl], gain_ref[:, sl])
            f = _dot(w_ref[g], vn.astype(BF16), _NN) + b_ref[:, sl]
            ya_ref[:, sl] = (ua[:, sl] * f).astype(BF16)

    return pl.pallas_call(
        body, name=name, grid=(t // CHUNK,),
        in_specs=[zu, zv, full2((1, A_WIDTH)), w_spec, full2((CHUNK, A_WIDTH))], out_specs=zu,
        out_shape=jax.ShapeDtypeStruct((t, A_WIDTH), BF16), compiler_params=_params("parallel"),
    )(z, z, gain, wtril, bias_exp)


def _gmlp_bwd(z, dcat, gain, wtril, wtril_t, bias_exp, name):
    t = z.shape[0]
    zu = pl.BlockSpec((CHUNK, A_WIDTH), lambda i: (i, 0))
    zv = pl.BlockSpec((CHUNK, A_WIDTH), lambda i: (i, 1))
    full2 = lambda shp: pl.BlockSpec(shp, lambda i: (0, 0))
    w_spec = pl.BlockSpec((A_GROUPS, CHUNK, CHUNK), lambda i: (0, 0, 0))
    dz_spec = pl.BlockSpec((CHUNK, 2 * A_WIDTH), lambda i: (i, 0))

    def body(zu_ref, zv_ref, dya_ref, gain_ref, w_ref, wt_ref, b_ref, dz_ref, dw_ref, dgain_ref, dbias_ref):
        @pl.when(pl.program_id(0) == 0)
        def _():
            dw_ref[...] = jnp.zeros_like(dw_ref)
            dgain_ref[...] = jnp.zeros_like(dgain_ref)
            dbias_ref[...] = jnp.zeros_like(dbias_ref)

        zu_v = zu_ref[...].astype(F32)
        zv_v = zv_ref[...].astype(F32)
        dya = dya_ref[...].astype(F32)
        ua = _gelu(zu_v)
        vg = _gelu(zv_v)
        row = lax.broadcasted_iota(jnp.int32, (CHUNK, CHUNK), 0)
        col = lax.broadcasted_iota(jnp.int32, (CHUNK, CHUNK), 1)
        for g in range(A_GROUPS):
            sl = slice(g * GROUP_DIM, (g + 1) * GROUP_DIM)
            gain_g = gain_ref[:, sl]
            xhat, rstd, vn = _group_norm(vg[:, sl], gain_g)
            vn16 = vn.astype(BF16)
            f = _dot(w_ref[g], vn16, _NN) + b_ref[:, sl]
            df = dya[:, sl] * ua[:, sl]
            df16 = df.astype(BF16)
            dz_ref[:, sl] = (dya[:, sl] * f * _gelu_grad(zu_v[:, sl])).astype(BF16)
            dw_ref[g] += jnp.where(row >= col, _dot(df16, vn16, _NT), 0.0)
            dvn = _dot(wt_ref[g], df16, _NN)
            dgain_ref[:, sl] += _colsum(dvn * xhat)
            dxh = dvn * gain_g
            dvg = rstd * (dxh - jnp.mean(dxh, axis=-1, keepdims=True) - xhat * jnp.mean(dxh * xhat, axis=-1, keepdims=True))
            dz_ref[:, A_WIDTH + g * GROUP_DIM:A_WIDTH + (g + 1) * GROUP_DIM] = (dvg * _gelu_grad(zv_v[:, sl])).astype(BF16)
            dbias_ref[:, sl] += df

    return pl.pallas_call(
        body, name=name, grid=(t // CHUNK,),
        in_specs=[zu, zv, zu, full2((1, A_WIDTH)), w_spec, w_spec, full2((CHUNK, A_WIDTH))],
        out_specs=[dz_spec, w_spec, full2((1, A_WIDTH)), full2((CHUNK, A_WIDTH))],
        out_shape=[jax.ShapeDtypeStruct((t, 2 * A_WIDTH), BF16), jax.ShapeDtypeStruct((A_GROUPS, CHUNK, CHUNK), F32),
                   jax.ShapeDtypeStruct((1, A_WIDTH), F32), jax.ShapeDtypeStruct((CHUNK, A_WIDTH), F32)],
        compiler_params=_params("arbitrary"),
    )(z, z, dcat, gain, wtril, wtril_t, bias_exp)


def _rope_tables(pos, inv_freq, sign, name):
    t = pos.shape[0]
    tm = _tile(t)
    row = pl.BlockSpec((1, B_WIDTH), lambda i: (0, 0))
    blk = pl.BlockSpec((tm, B_WIDTH), lambda i: (i, 0))

    def body(pos_ref, f_ref, s_ref, cos_ref, sin_ref):
        ang = pos_ref[...] * f_ref[...]
        cos_ref[...] = jnp.cos(ang)
        sin_ref[...] = jnp.sin(ang) * s_ref[...]

    return pl.pallas_call(
        body, name=name, grid=(t // tm,), in_specs=[pl.BlockSpec((tm, 1), lambda i: (i, 0)), row, row],
        out_specs=[blk, blk], out_shape=[jax.ShapeDtypeStruct((t, B_WIDTH), F32)] * 2,
        compiler_params=_params("parallel"),
    )(pos, inv_freq, sign)


def _head_sum(v, seg):
    return lax.dot_general(v, seg, _NN, precision=lax.Precision.HIGHEST, preferred_element_type=F32)


def _swap_halves(v):
    lane = lax.broadcasted_iota(jnp.int32, v.shape, 1)
    return jnp.where((lane & (HEAD_DIM - 1)) < HEAD_DIM // 2,pltpu.roll(v, B_WIDTH - HEAD_DIM // 2, 1), pltpu.roll(v, HEAD_DIM // 2, 1))


def _qk_prep(z, cos_t, sin_t, gq, gk, seg, name):
    t = z.shape[0]
    tm = _tile(t, (256, 128))
    col = lambda c: pl.BlockSpec((tm, B_WIDTH), lambda i: (i, c))
    row = pl.BlockSpec((1, B_WIDTH), lambda i: (0, 0))
    blk = col(0)

    def body(q_ref, k_ref, v_ref, cos_ref, sin_ref, gq_ref, gk_ref, seg_ref, qo_ref, ko_ref, vo_ref):
        def norm_rot(x, g):
            r = lax.rsqrt(_head_sum(x * x, seg_ref[...]) * (1.0 / HEAD_DIM) + EPS)
            xn = x * r * g
            return xn * cos_ref[...] + _swap_halves(xn) * sin_ref[...]

        qo_ref[...] = norm_rot(q_ref[...].astype(F32), gq_ref[...]).astype(BF16)
        ko_ref[...] = norm_rot(k_ref[...].astype(F32), gk_ref[...]).astype(BF16)
        vo_ref[...] = v_ref[...].astype(BF16)

    return pl.pallas_call(
        body, name=name, grid=(t // tm,),
        in_specs=[col(2), col(3), col(4), blk, blk, row, row, pl.BlockSpec((B_WIDTH, B_WIDTH), lambda i: (0, 0))],
        out_specs=[blk, blk, blk], out_shape=[jax.ShapeDtypeStruct((t, B_WIDTH), BF16)] * 3,
        compiler_params=_params("parallel"),
    )(z, z, z, cos_t, sin_t, gq, gk, seg)


def _qk_prep_bwd(z, dqs, dks, dvs, cos_t, sin_t, gq, gk, seg, name):
    t = z.shape[0]
    tm = _tile(t, (256, 128))
    col = lambda c: pl.BlockSpec((tm, B_WIDTH), lambda i: (i, c))
    row = pl.BlockSpec((1, B_WIDTH), lambda i: (0, 0))
    blk = col(0)
    nb = len(dqs)

    def body(*refs):
        q_ref, k_ref = refs[0], refs[1]
        dq_refs, dk_refs, dv_refs = refs[2:2 + nb], refs[2 + nb:2 + 2 * nb], refs[2 + 2 * nb:2 + 3 * nb]
        cos_ref, sin_ref, gq_ref, gk_ref, seg_ref, dz_ref, dgq_ref, dgk_ref = refs[2 + 3 * nb:]

        @pl.when(pl.program_id(0) == 0)
        def _():
            dgq_ref[...] = jnp.zeros_like(dgq_ref)
            dgk_ref[...] = jnp.zeros_like(dgk_ref)

        def back(x, d_refs, g, dg_ref):
            dout = d_refs[0][...]
            for r_ in d_refs[1:]:
                dout = dout + r_[...]
            dy = dout * cos_ref[...] + _swap_halves(dout * sin_ref[...])
            r = lax.rsqrt(_head_sum(x * x, seg_ref[...]) * (1.0 / HEAD_DIM) + EPS)
            xn = x * r
            dg_ref[...] += _colsum(dy * xn)
            dxn = dy * g
            return r * (dxn - xn * (_head_sum(dxn * xn, seg_ref[...]) * (1.0 / HEAD_DIM)))

        dz_ref[:, 0:B_WIDTH] = back(q_ref[...].astype(F32), dq_refs, gq_ref[...], dgq_ref).astype(BF16)
        dz_ref[:, B_WIDTH:2 * B_WIDTH] = back(k_ref[...].astype(F32), dk_refs, gk_ref[...], dgk_ref).astype(BF16)
        dv = dv_refs[0][...]
        for r_ in dv_refs[1:]:
            dv = dv + r_[...]
        dz_ref[:, 2 * B_WIDTH:3 * B_WIDTH] = dv.astype(BF16)

    return pl.pallas_call(
        body, name=name, grid=(t // tm,),
        in_specs=[col(2), col(3)] + [blk] * (3 * nb) + [blk, blk, row, row, pl.BlockSpec((B_WIDTH, B_WIDTH), lambda i: (0, 0))],
        out_specs=[pl.BlockSpec((tm, 3 * B_WIDTH), lambda i: (i, 0)), row, row],
        out_shape=[jax.ShapeDtypeStruct((t, 3 * B_WIDTH), BF16), jax.ShapeDtypeStruct((1, B_WIDTH), F32),
                   jax.ShapeDtypeStruct((1, B_WIDTH), F32)],
        compiler_params=_params("arbitrary"),
    )(z, z, *dqs, *dks, *dvs, cos_t, sin_t, gq, gk, seg)


def _subseq(a, dil):
    return a.reshape(a.shape[0] // dil, dil * a.shape[1])


def _attn_fwd(q, k, v, dil, name):
    t = q.shape[0]
    nb = t // dil // Q_BLOCK
    cur = pl.BlockSpec((Q_BLOCK, B_WIDTH), lambda r, i: (i, r))
    prev = pl.BlockSpec((Q_BLOCK, B_WIDTH), lambda r, i: (jnp.maximum(i - 1, 0), r))

    def body(q_ref, kp_ref, kc_ref, vp_ref, vc_ref, o_ref, lse_ref):
        i = pl.program_id(1)
        q = q_ref[...]
        kk = jnp.concatenate([kp_ref[...], kc_ref[...]], axis=0)
        vv = jnp.concatenate([vp_ref[...], vc_ref[...]], axis=0)
        a = lax.broadcasted_iota(jnp.int32, (Q_BLOCK, 2 * Q_BLOCK), 0)
        j = lax.broadcasted_iota(jnp.int32, (Q_BLOCK, 2 * Q_BLOCK), 1)
        dist = a + Q_BLOCK - j
        mask = (dist >= 0) & (dist <= Q_BLOCK) & ((j >= Q_BLOCK) | (i > 0))
        for h in range(HEADS):
            sl = slice(h * HEAD_DIM, (h + 1) * HEAD_DIM)
            s = jnp.where(mask, _dot(q[:, sl], kk[:, sl], _NT) * (HEAD_DIM ** -0.5), NEG)
            m = jnp.max(s, axis=-1, keepdims=True)
            p = jnp.exp(s - m)
            den = jnp.sum(p, axis=-1, keepdims=True)
            o_ref[:, sl] = _dot(p.astype(BF16), vv[:, sl], _NN) / den
            lse_ref[:, sl] = jnp.broadcast_to(m + jnp.log(den), (Q_BLOCK, HEAD_DIM))

    o, lse = pl.pallas_call(
        body, name=name, grid=(dil, nb), in_specs=[cur, prev, cur, prev, cur], out_specs=[cur, cur],
        out_shape=[jax.ShapeDtypeStruct((t // dil, dil * B_WIDTH), F32)] * 2,
        compiler_params=_params("parallel", "parallel"),
    )(_subseq(q, dil), _subseq(k, dil), _subseq(k, dil), _subseq(v, dil), _subseq(v, dil))
    return o.reshape(t, B_WIDTH), lse.reshape(t, B_WIDTH)


def _attn_merge(outs, lses, name):
    t = outs[0].shape[0]
    tm = _tile(t)
    blk = pl.BlockSpec((tm, B_WIDTH), lambda i: (i, 0))
    nb = len(outs)

    def body(*refs):
        o_refs, l_refs, yb_ref, lse_ref = refs[:nb], refs[nb:2 * nb], refs[2 * nb], refs[2 * nb + 1]
        ls = [r[...] for r in l_refs]
        m = functools.reduce(jnp.maximum, ls)
        tot = m + jnp.log(sum(jnp.exp(l - m) for l in ls))
        yb_ref[...] = sum(jnp.exp(l - tot) * o[...] for l, o in zip(ls, o_refs)).astype(BF16)
        lse_ref[...] = tot

    return pl.pallas_call(
        body, name=name, grid=(t // tm,), in_specs=[blk] * (2 * nb), out_specs=[blk, blk],
        out_shape=[jax.ShapeDtypeStruct((t, B_WIDTH), BF16), jax.ShapeDtypeStruct((t, B_WIDTH), F32)],
        compiler_params=_params("parallel"),
    )(*outs, *lses)


def _attn_bwd(q, k, v, do, o, lse, dil, name):
    t = q.shape[0]
    nb = t // dil // Q_BLOCK
    blk = lambda f: pl.BlockSpec((Q_BLOCK, B_WIDTH), lambda r, i: (f(i), r))
    cur = blk(lambda i: jnp.minimum(i, nb - 1))
    prev = blk(lambda i: jnp.clip(i - 1, 0, nb - 1))
    scale = HEAD_DIM ** -0.5

    def body(q_ref, kp_ref, kc_ref, vp_ref, vc_ref, do_ref, o_ref, lse_ref, dq_ref, dk_ref, dv_ref,
             ck_ref, cv_ref, tk_ref, tv_ref):
        i = pl.program_id(1)

        @pl.when(i == 0)
        def _():
            ck_ref[...] = jnp.zeros_like(ck_ref)
            cv_ref[...] = jnp.zeros_like(cv_ref)

        @pl.when(i < nb)
        def _():
            q = q_ref[...]
            kk = jnp.concatenate([kp_ref[...], kc_ref[...]], axis=0)
            vv = jnp.concatenate([vp_ref[...], vc_ref[...]], axis=0)
            do = do_ref[...]
            dof = do.astype(F32)
            of = o_ref[...].astype(F32)
            a = lax.broadcasted_iota(jnp.int32, (Q_BLOCK, 2 * Q_BLOCK), 0)
            j = lax.broadcasted_iota(jnp.int32, (Q_BLOCK, 2 * Q_BLOCK), 1)
            dist = a + Q_BLOCK - j
            mask = (dist >= 0) & (dist <= Q_BLOCK) & ((j >= Q_BLOCK) | (i > 0))
            for h in range(HEADS):
                sl = slice(h * HEAD_DIM, (h + 1) * HEAD_DIM)
                s = jnp.where(mask, _dot(q[:, sl], kk[:, sl], _NT) * scale, NEG)
                p = jnp.exp(s - lse_ref[:, h * HEAD_DIM:h * HEAD_DIM + 1])
                dp = _dot(do[:, sl], vv[:, sl], _NT)
                delta = jnp.sum(dof[:, sl] * of[:, sl], axis=-1, keepdims=True)
                ds = (p * (dp - delta) * scale).astype(BF16)
                dq_ref[:, sl] = _dot(ds, kk[:, sl], _NN)
                dv_t = _dot(do[:, sl], p.astype(BF16), _TN)
                dk_t = _dot(q[:, sl], ds, _TN)
                tk_ref[sl, :] = ck_ref[sl, :] + dk_t[:, :Q_BLOCK]
                tv_ref[sl, :] = cv_ref[sl, :] + dv_t[:, :Q_BLOCK]
                ck_ref[sl, :] = dk_t[:, Q_BLOCK:]
                cv_ref[sl, :] = dv_t[:, Q_BLOCK:]

        @pl.when(i == nb)
        def _():
            tk_ref[...] = ck_ref[...]
            tv_ref[...] = cv_ref[...]

        @pl.when(i >= 1)
        def _():
            dk_ref[...] = tk_ref[...].T
            dv_ref[...] = tv_ref[...].T

    sub = lambda a_: _subseq(a_, dil)
    shape = jax.ShapeDtypeStruct((t // dil, dil * B_WIDTH), F32)
    dq, dk, dv = pl.pallas_call(
        body, name=name, grid=(dil, nb + 1), in_specs=[cur, prev, cur, prev, cur, cur, cur, cur],
        out_specs=[cur, prev, prev], out_shape=[shape] * 3,
        scratch_shapes=[pltpu.VMEM((B_WIDTH, Q_BLOCK), F32)] * 4,
        compiler_params=_params("parallel", "arbitrary"),
    )(sub(q), sub(k), sub(k), sub(v), sub(v), sub(do), sub(o), sub(lse))
    return dq.reshape(t, B_WIDTH), dk.reshape(t, B_WIDTH), dv.reshape(t, B_WIDTH)


FFN_TN = 256
FFN_FWD_CHUNK = 256
FFN_BWD_CHUNK = 128


def _ffn_up(h, up_t, name):
    t, k = h.shape
    tm = _tile(t)

    def body(h_ref, w_ref, o_ref):
        o_ref[...] = _dot(h_ref[...], w_ref[...], _NT).astype(BF16)

    return pl.pallas_call(
        body, name=name, grid=(2, t // tm),
        in_specs=[pl.BlockSpec((tm, k), lambda p, i: (i, 0)), pl.BlockSpec((None, FFN_DIM, k), lambda p, i: (p, 0, 0))],
        out_specs=pl.BlockSpec((None, tm, FFN_DIM), lambda p, i: (p, i, 0)),
        out_shape=jax.ShapeDtypeStruct((2, t, FFN_DIM), BF16), compiler_params=_params("parallel", "parallel"),
    )(h, up_t.reshape(2, FFN_DIM, k))


def _ffn_up_dx(du, up_t, name):
    t = du.shape[1]
    k = up_t.shape[1]
    tm = _tile(t)

    def body(a_ref, b_ref, o_ref):
        o_ref[...] = _dot(a_ref[0], b_ref[0], _NN) + _dot(a_ref[1], b_ref[1], _NN)

    return pl.pallas_call(
        body, name=name, grid=(t // tm,),
        in_specs=[pl.BlockSpec((2, tm, FFN_DIM), lambda i: (0, i, 0)), pl.BlockSpec((2, FFN_DIM, k), lambda i: (0, 0, 0))],
        out_specs=pl.BlockSpec((tm, k), lambda i: (i, 0)), out_shape=jax.ShapeDtypeStruct((t, k), F32),
        compiler_params=_params("parallel"),
    )(du, up_t.reshape(2, FFN_DIM, k))


def _ffn_conv(win, w_ref, b_ref, p):
    x = win.astype(F32)
    x0, x1, x2 = x[FFN_HALO:], pltpu.roll(x, 1, 0)[FFN_HALO:], pltpu.roll(x, 2, 0)[FFN_HALO:]
    return x0, b_ref[p] + w_ref[p, 2:3, :] * x0 + w_ref[p, 1:2, :] * x1 + w_ref[p, 0:1, :] * x2


def _zero_if(cond, v):
    return jnp.where(cond, 0, v).astype(v.dtype)


def _ffn_act(u, dw_w, dw_b, name):
    t = u.shape[1]
    tm = _tile(t)
    chunk = min(FFN_FWD_CHUNK, tm)
    hb = tm // FFN_HALO
    main = pl.BlockSpec((2, tm, FFN_TN), lambda i, j: (0, i, j))
    halo = pl.BlockSpec((2, FFN_HALO, FFN_TN), lambda i, j: (0, jnp.maximum(i * hb - 1, 0), j))
    wsp = pl.BlockSpec((2, FFN_CONV_WIDTH, FFN_TN), lambda i, j: (0, 0, j))
    bsp = pl.BlockSpec((2, 1, FFN_TN), lambda i, j: (0, 0, j))

    def body(u_ref, uh_ref, w_ref, b_ref, o_ref):
        first = pl.program_id(0) == 0

        def emit(rows, wins):
            za, zb = _ffn_conv(wins[0], w_ref, b_ref, 0)[1], _ffn_conv(wins[1], w_ref, b_ref, 1)[1]
            o_ref[rows, :] = (za * _sigmoid(za) * zb).astype(BF16)

        emit(pl.ds(0, chunk), [jnp.concatenate([_zero_if(first, uh_ref[p]), u_ref[p, 0:chunk, :]], axis=0) for p in range(2)])

        def step(c, carry):
            s = pl.multiple_of(c * chunk, chunk)
            emit(pl.ds(s, chunk), [u_ref[p, pl.ds(s - FFN_HALO, chunk + FFN_HALO), :] for p in range(2)])
            return carry

        lax.fori_loop(1, tm // chunk, step, 0)

    return pl.pallas_call(
        body, name=name, grid=(t // tm, FFN_DIM // FFN_TN), in_specs=[main, halo, wsp, bsp],
        out_specs=pl.BlockSpec((tm, FFN_TN), lambda i, j: (i, j)), out_shape=jax.ShapeDtypeStruct((t, FFN_DIM), BF16),
        compiler_params=_params("parallel", "parallel"),
    )(u, u, dw_w, dw_b)


def _fold8(v):
    return jnp.sum(v.reshape(v.shape[0] // 8, 8, v.shape[1]), axis=0)


def _ffn_act_bwd(u, dact, dw_w, dw_b, name):
    t = u.shape[1]
    tm = _tile(t)
    chunk = min(FFN_BWD_CHUNK, tm // 2)
    halo = FFN_HALO
    hb = tm // halo
    nt = t // tm
    last_halo = t // halo - 1
    prev_i = lambda i: jnp.maximum(i * hb - 1, 0)
    next_i = lambda i: jnp.minimum((i + 1) * hb, last_halo)
    main = pl.BlockSpec((2, tm, FFN_TN), lambda j, i: (0, i, j))
    prev = pl.BlockSpec((2, halo, FFN_TN), lambda j, i: (0, prev_i(i), j))
    nxt = pl.BlockSpec((2, halo, FFN_TN), lambda j, i: (0, next_i(i), j))
    wsp = pl.BlockSpec((2, FFN_CONV_WIDTH, FFN_TN), lambda j, i: (0, 0, j))
    bsp = pl.BlockSpec((2, 1, FFN_TN), lambda j, i: (0, 0, j))

    def body(u_ref, up_ref, un_ref, da_ref, dan_ref, w_ref, b_ref, du_ref, dw_ref, db_ref, acc_ref):
        i = pl.program_id(1)
        first, last = i == 0, i == nt - 1
        acc_ref[...] = jnp.zeros_like(acc_ref)

        def emit(rows, wins, dact):
            n = chunk + halo
            (ua, za), (ub, zb) = _ffn_conv(wins[0], w_ref, b_ref, 0), _ffn_conv(wins[1], w_ref, b_ref, 1)
            dact = dact.astype(F32)
            sg = _sigmoid(za)
            dzs = (dact * zb * (sg * (1.0 + za * (1.0 - sg))), dact * (za * sg))
            for p, (dz, um) in enumerate(zip(dzs, (ua, ub))):
                ahead = (dz[:chunk], pltpu.roll(dz, n - 1, 0)[:chunk], pltpu.roll(dz, n - 2, 0)[:chunk])
                um = um[:chunk]
                acc_ref[p, FFN_CONV_WIDTH] += _fold8(ahead[0])
                du = None
                for j, dzj in enumerate(ahead):
                    k = FFN_CONV_WIDTH - 1 - j
                    acc_ref[p, k] += _fold8(dzj * um)
                    term = w_ref[p, k:k + 1, :] * dzj
                    du = term if du is None else du + term
                du_ref[p, rows, :] = du.astype(BF16)

        emit(pl.ds(0, chunk),
             [jnp.concatenate([_zero_if(first, up_ref[p]), u_ref[p, 0:chunk + halo, :]], axis=0) for p in range(2)],
             da_ref[0:chunk + halo, :])

        def step(c, carry):
            s = pl.multiple_of(c * chunk, chunk)
            emit(pl.ds(s, chunk), [u_ref[p, pl.ds(s - halo, chunk + 2 * halo), :] for p in range(2)],
                 da_ref[pl.ds(s, chunk + halo), :])
            return carry

        lax.fori_loop(1, tm // chunk - 1, step, 0)
        s = tm - chunk
        emit(pl.ds(s, chunk),
             [jnp.concatenate([u_ref[p, s - halo:tm, :], _zero_if(last, un_ref[p])], axis=0) for p in range(2)],
             jnp.concatenate([da_ref[s:tm, :], _zero_if(last, dan_ref[...])], axis=0))

        @pl.when(i == 0)
        def _():
            dw_ref[...] = jnp.zeros_like(dw_ref)
            db_ref[...] = jnp.zeros_like(db_ref)

        for p in range(2):
            for k in range(FFN_CONV_WIDTH):
                dw_ref[p, k:k + 1, :] += _colsum(acc_ref[p, k])
            db_ref[p] += _colsum(acc_ref[p, FFN_CONV_WIDTH])

    return pl.pallas_call(
        body, name=name, grid=(FFN_DIM // FFN_TN, nt),
        in_specs=[main, prev, nxt, pl.BlockSpec((tm, FFN_TN), lambda j, i: (i, j)),
                  pl.BlockSpec((halo, FFN_TN), lambda j, i: (next_i(i), j)), wsp, bsp],
        out_specs=[main, wsp, bsp],
        out_shape=[jax.ShapeDtypeStruct((2, t, FFN_DIM), BF16), jax.ShapeDtypeStruct((2, FFN_CONV_WIDTH, FFN_DIM), F32),
                   jax.ShapeDtypeStruct((2, 1, FFN_DIM), F32)],
        scratch_shapes=[pltpu.VMEM((2, FFN_CONV_WIDTH + 1, 8, FFN_TN), F32)],
        compiler_params=_params("parallel", "arbitrary"),
    )(u, u, u, dact, dact, dw_w, dw_b)


CONV_TM = 256
CONV_ROWS = 128
CONV_LANES = 128


def _glu_window(pa_ref, pah_ref, pg_ref, pgh_ref, scr_ref, first):
    ah, gh = pah_ref[...].astype(F32), pgh_ref[...].astype(F32)
    scr_ref[0:CONV_HALO, :] = jnp.where(first, 0.0, ah * _sigmoid(gh))
    scr_ref[CONV_HALO:, :] = pa_ref[...].astype(F32) * _sigmoid(pg_ref[...].astype(F32))


def _tap_slabs(win, rows, ahead):
    n = win.shape[0]
    for s in range(8):
        ws = win if s == 0 else pltpu.roll(win, n - s if ahead else s, 0)
        for q in range(CONV_HALO // 8):
            o = 8 * q + s
            if o < CONV_WIDTH:
                start = 8 * q if ahead else CONV_HALO - 8 * q
                yield CONV_WIDTH - 1 - o, ws[start:start + rows]


def _conformer_specs(t):
    tm = _tile(t, (CONV_TM, 128))
    hb = tm // CONV_HALO
    d = D_MODEL
    main = lambda c: pl.BlockSpec((tm, d), lambda i: (i, c))
    halo = lambda c: pl.BlockSpec((CONV_HALO, d), lambda i: (jnp.maximum(i * hb - 1, 0), c))
    row = pl.BlockSpec((1, d), lambda i: (0, 0))
    wsp = pl.BlockSpec((CONV_WIDTH, d), lambda i: (0, 0))
    return tm, main, halo, row, wsp


def _conformer_mid(p, dw_w, dw_b, ln_g, ln_b, name):
    t = p.shape[0]
    tm, main, halo, row, wsp = _conformer_specs(t)
    d, lanes = D_MODEL, CONV_LANES

    def body(pa_ref, pah_ref, pg_ref, pgh_ref, w_ref, b_ref, g_ref, lb_ref, o_ref, dc_ref, scr_ref):
        _glu_window(pa_ref, pah_ref, pg_ref, pgh_ref, scr_ref, pl.program_id(0) == 0)
        for c in range(d // lanes):
            ls = slice(c * lanes, (c + 1) * lanes)
            acc = jnp.broadcast_to(b_ref[:, ls], (tm, lanes))
            for k, slab in _tap_slabs(scr_ref[:, ls], tm, False):
                acc = acc + w_ref[k:k + 1, ls] * slab
            dc_ref[:, ls] = acc

        def norm(r, carry):
            r0 = pl.multiple_of(r * 32, 32)
            dc = dc_ref[pl.ds(r0, 32), :]
            xc = dc - jnp.mean(dc, axis=-1, keepdims=True)
            ln = xc * lax.rsqrt(jnp.mean(xc * xc, axis=-1, keepdims=True) + EPS) * g_ref[...] + lb_ref[...]
            o_ref[pl.ds(r0, 32), :] = (ln * _sigmoid(ln)).astype(BF16)
            return carry

        lax.fori_loop(0, tm // 32, norm, 0)

    return pl.pallas_call(
        body, name=name, grid=(t // tm,), in_specs=[main(0), halo(0), main(1), halo(1), wsp, row, row, row],
        out_specs=[main(0), main(0)], out_shape=[jax.ShapeDtypeStruct((t, d), BF16), jax.ShapeDtypeStruct((t, d), F32)],
        scratch_shapes=[pltpu.VMEM((tm + CONV_HALO, d), F32)], compiler_params=_params("parallel"),
    )(p, p, p, p, dw_w, dw_b, ln_g, ln_b)


def _conformer_mid_bwd(p, dc, ds, ln_g, ln_b, name):
    t = p.shape[0]
    tm, main, halo, row, wsp = _conformer_specs(t)
    d, nt = D_MODEL, t // tm
    rows, lanes = CONV_ROWS, CONV_LANES

    def body(pa_ref, pah_ref, pg_ref, pgh_ref, dc_ref, ds_ref, g_ref, lb_ref,
             ddc_ref, dw_ref, db_ref, dg_ref, dlb_ref, scr_ref, wacc_ref, racc_ref):
        i = pl.program_id(0)

        @pl.when(i == 0)
        def _():
            wacc_ref[...] = jnp.zeros_like(wacc_ref)
            racc_ref[...] = jnp.zeros_like(racc_ref)

        _glu_window(pa_ref, pah_ref, pg_ref, pgh_ref, scr_ref, i == 0)

        def norm_bwd(r, carry):
            r0 = pl.multiple_of(r * 32, 32)
            dcv = dc_ref[pl.ds(r0, 32), :]
            xc = dcv - jnp.mean(dcv, axis=-1, keepdims=True)
            rstd = lax.rsqrt(jnp.mean(xc * xc, axis=-1, keepdims=True) + EPS)
            xhat = xc * rstd
            ln = xhat * g_ref[...] + lb_ref[...]
            sg = _sigmoid(ln)
            dln = ds_ref[pl.ds(r0, 32), :].astype(F32) * (sg * (1.0 + ln * (1.0 - sg)))
            dxh = dln * g_ref[...]
            ddc = rstd * (dxh - jnp.mean(dxh, axis=-1, keepdims=True) - xhat * jnp.mean(dxh * xhat, axis=-1, keepdims=True))
            ddc_ref[pl.ds(r0, 32), :] = ddc
            racc_ref[0] += _fold8(dln * xhat)
            racc_ref[1] += _fold8(dln)
            racc_ref[2] += _fold8(ddc)
            return carry

        lax.fori_loop(0, tm // 32, norm_bwd, 0)

        for c in range(d // lanes):
            ls = slice(c * lanes, (c + 1) * lanes)

            def taps(r, carry, ls=ls):
                r0 = pl.multiple_of(r * rows, rows)
                ddc = ddc_ref[pl.ds(r0, rows), ls]
                for k, slab in _tap_slabs(scr_ref[pl.ds(r0, rows + CONV_HALO), ls], rows, False):
                    wacc_ref[k, :, ls] += _fold8(ddc * slab)
                return carry

            lax.fori_loop(0, tm // rows, taps, 0)

        @pl.when(i == nt - 1)
        def _():
            for k in range(CONV_WIDTH):
                dw_ref[k:k + 1, :] = _colsum(wacc_ref[k])
            dg_ref[...] = _colsum(racc_ref[0])
            dlb_ref[...] = _colsum(racc_ref[1])
            db_ref[...] = _colsum(racc_ref[2])

    return pl.pallas_call(
        body, name=name, grid=(nt,), in_specs=[main(0), halo(0), main(1), halo(1), main(0), main(0), row, row],
        out_specs=[main(0), wsp, row, row, row],
        out_shape=[jax.ShapeDtypeStruct((t, d), F32), jax.ShapeDtypeStruct((CONV_WIDTH, d), F32)]
        + [jax.ShapeDtypeStruct((1, d), F32)] * 3,
        scratch_shapes=[pltpu.VMEM((tm + CONV_HALO, d), F32), pltpu.VMEM((CONV_WIDTH, 8, d), F32), pltpu.VMEM((3, 8, d), F32)],
        compiler_params=_params("arbitrary"),
    )(p, p, p, p, dc, ds, ln_g, ln_b)


def _conformer_glu_bwd(p, ddc, dw_w, name):
    t = p.shape[0]
    d = D_MODEL
    tm = _tile(t, (CONV_TM, 128))
    hb = tm // CONV_HALO
    nt = t // tm
    last_halo = t // CONV_HALO - 1
    rows, lanes = CONV_ROWS, CONV_LANES
    col = lambda c: pl.BlockSpec((tm, d), lambda i: (i, c))
    nxt = pl.BlockSpec((CONV_HALO, d), lambda i: (jnp.minimum((i + 1) * hb, last_halo), 0))

    def body(pa_ref, pg_ref, ddc_ref, ddcn_ref, w_ref, dp_ref, db_ref, scr_ref, acc_ref):
        i = pl.program_id(0)

        @pl.when(i == 0)
        def _():
            acc_ref[...] = jnp.zeros_like(acc_ref)

        scr_ref[0:tm, :] = ddc_ref[...]
        scr_ref[tm:, :] = _zero_if(i == nt - 1, ddcn_ref[...])
        for c in range(d // lanes):
            ls = slice(c * lanes, (c + 1) * lanes)
            gs = slice(d + c * lanes, d + (c + 1) * lanes)

            def taps(r, carry, ls=ls, gs=gs):
                r0 = pl.multiple_of(r * rows, rows)
                dglu = None
                for k, slab in _tap_slabs(scr_ref[pl.ds(r0, rows + CONV_HALO), ls], rows, True):
                    term = w_ref[k:k + 1, ls] * slab
                    dglu = term if dglu is None else dglu + term
                a = pa_ref[pl.ds(r0, rows), ls].astype(F32)
                sg = _sigmoid(pg_ref[pl.ds(r0, rows), ls].astype(F32))
                da = (dglu * sg).astype(BF16)
                dg = (dglu * a * sg * (1.0 - sg)).astype(BF16)
                dp_ref[pl.ds(r0, rows), ls] = da
                dp_ref[pl.ds(r0, rows), gs] = dg
                acc_ref[:, ls] += _fold8(da.astype(F32))
                acc_ref[:, gs] += _fold8(dg.astype(F32))
                return carry

            lax.fori_loop(0, tm // rows, taps, 0)

        @pl.when(i == nt - 1)
        def _():
            db_ref[...] = _colsum(acc_ref[...])

    return pl.pallas_call(
        body, name=name, grid=(nt,),
        in_specs=[col(0), col(1), col(0), nxt, pl.BlockSpec((CONV_WIDTH, d), lambda i: (0, 0))],
        out_specs=[pl.BlockSpec((tm, 2 * d), lambda i: (i, 0)), pl.BlockSpec((1, 2 * d), lambda i: (0, 0))],
        out_shape=[jax.ShapeDtypeStruct((t, 2 * d), BF16), jax.ShapeDtypeStruct((1, 2 * d), F32)],
        scratch_shapes=[pltpu.VMEM((tm + CONV_HALO, d), F32), pltpu.VMEM((8, 2 * d), F32)],
        compiler_params=_params("arbitrary"),
    )(p, p, ddc, ddc, dw_w)


def _colsum_call(a, name):
    t, n = a.shape
    tm = _tile(t)

    def body(a_ref, o_ref):
        @pl.when(pl.program_id(0) == 0)
        def _():
            o_ref[...] = jnp.zeros_like(o_ref)

        o_ref[...] += _colsum(a_ref[...].astype(F32))

    return pl.pallas_call(
        body, name=name, grid=(t // tm,), in_specs=[pl.BlockSpec((tm, n), lambda i: (i, 0))],
        out_specs=pl.BlockSpec((1, n), lambda i: (0, 0)), out_shape=jax.ShapeDtypeStruct((1, n), F32),
        compiler_params=_params("arbitrary"),
    )(a)


def _ada_fwd(c_all, w, name):
    rows, d = c_all.shape
    n = w.shape[1]
    tn = _tile(n, (256, 128))

    def body(c_ref, w_ref, o_ref):
        c = c_ref[...]
        o_ref[...] = _dot((c * _sigmoid(c)).astype(BF16), w_ref[...].astype(BF16), _NN)

    return pl.pallas_call(
        body, name=name, grid=(n // tn,),
        in_specs=[pl.BlockSpec((rows, d), lambda j: (0, 0)), pl.BlockSpec((d, tn), lambda j: (0, j))],
        out_specs=pl.BlockSpec((rows, tn), lambda j: (0, j)), out_shape=jax.ShapeDtypeStruct((rows, n), F32),
        compiler_params=_params("parallel"),
    )(c_all, w)


def _ada_bwd(c_all, dmod, name):
    rows, d = c_all.shape
    n = dmod.shape[1]
    tn = _tile(n, (256, 128))

    def body(c_ref, g_ref, o_ref):
        c = c_ref[...]
        o_ref[...] = _dot((c * _sigmoid(c)).astype(BF16), g_ref[...].astype(BF16), _TN)

    return pl.pallas_call(
        body, name=name, grid=(n // tn,),
        in_specs=[pl.BlockSpec((rows, d), lambda j: (0, 0)), pl.BlockSpec((rows, tn), lambda j: (0, j))],
        out_specs=pl.BlockSpec((d, tn), lambda j: (0, j)), out_shape=jax.ShapeDtypeStruct((d, n), F32),
        compiler_params=_params("parallel"),
    )(c_all, dmod)


def _sum_slots(a, name):
    s, r, c = a.shape
    tr = _row_tile(r, 256)

    def body(a_ref, o_ref):
        acc = a_ref[0].astype(F32)
        for k in range(1, s):
            acc = acc + a_ref[k].astype(F32)
        o_ref[...] = acc

    return pl.pallas_call(
        body, name=name, grid=(r // tr,), in_specs=[pl.BlockSpec((s, tr, c), lambda i: (0, i, 0))],
        out_specs=pl.BlockSpec((tr, c), lambda i: (i, 0)), out_shape=jax.ShapeDtypeStruct((r, c), F32),
        compiler_params=_params("parallel"),
    )(a)


def _sum_with_own(blocks, land, me, name):
    s, r, c = land.shape
    tr = _row_tile(r, 256)
    slot = lambda k: pl.BlockSpec((None, tr, c), lambda i, me_ref: ((me_ref[0] + k) % s, i, 0))

    def body(me_ref, own_ref, *refs):
        o_ref = refs[-1]
        acc = own_ref[...].astype(F32)
        for ref in refs[:-1]:
            acc = acc + ref[...].astype(F32)
        o_ref[...] = acc

    return pl.pallas_call(
        body, name=name, out_shape=jax.ShapeDtypeStruct((r, c), F32),
        grid_spec=pltpu.PrefetchScalarGridSpec(
            num_scalar_prefetch=1, grid=(r // tr,), in_specs=[slot(0)] + [slot(k) for k in range(1, s)],
            out_specs=pl.BlockSpec((tr, c), lambda i, me_ref: (i, 0))),
        compiler_params=_params("parallel"),
    )(me, blocks, *[land] * (s - 1))


def _adamw_update(w, g, m, v):
    nm = ADAM_B1 * m + (1.0 - ADAM_B1) * g
    nv = ADAM_B2 * v + (1.0 - ADAM_B2) * (g * g)
    m_hat = nm * (1.0 / (1.0 - ADAM_B1 ** ADAM_STEP))
    v_hat = nv * (1.0 / (1.0 - ADAM_B2 ** ADAM_STEP))
    return -ADAM_LR * (m_hat / (jnp.sqrt(v_hat) + ADAM_EPS) + ADAM_WD * w), nm, nv


def _adamw(w, g, m, v, name):
    l, r, c = w.shape
    tr = _row_tile(r, 256)
    blk = pl.BlockSpec((None, tr, c), lambda k, i: (k, i, 0))

    def body(w_ref, g_ref, m_ref, v_ref, d_ref, nm_ref, nv_ref):
        d_ref[...], nm_ref[...], nv_ref[...] = _adamw_update(w_ref[...], g_ref[...], m_ref[...], v_ref[...])

    return pl.pallas_call(
        body, name=name, grid=(l, r // tr), in_specs=[blk] * 4, out_specs=[blk] * 3,
        out_shape=[jax.ShapeDtypeStruct(w.shape, F32)] * 3, compiler_params=_params("parallel", "parallel"),
    )(w, g, m, v)


def _adamw_small(ws, gs, ms, vs, name):
    n = len(ws)
    two_d = lambda a: a.reshape(-1, a.shape[-1])

    def body(*refs):
        ins, outs = refs[:4 * n], refs[4 * n:]
        for a in range(n):
            outs[a][...], outs[n + a][...], outs[2 * n + a][...] = _adamw_update(*[ins[k * n + a][...] for k in range(4)])

    res = pl.pallas_call(
        body, name=name, out_shape=[jax.ShapeDtypeStruct(two_d(w).shape, F32) for w in ws] * 3,
    )(*[two_d(a) for a in (*ws, *gs, *ms, *vs)])
    return [[res[k * n + a].reshape(ws[a].shape) for a in range(n)] for k in range(3)]


def _mesh_pos():
    return lax.axis_index("x"), lax.axis_index("y"), lax.axis_index("c")


def _all_gather_vmem(x_shard, name):
    m_per, n = x_shard.shape

    def body(x_ref, out_ref, send_sems, recv_sems, local_sem):
        x, y, c = _mesh_pos()
        me, sibling = (x, y, c), (x, y, 1 - c)
        chips = [(1 - x, y), (x, 1 - y), (1 - x, 1 - y)]

        def rows(px, py, pc):
            return out_ref.at[pl.ds((4 * px + 2 * py + pc) * m_per, m_per), :]

        def copy(k, block, to, src=None):
            return pltpu.make_async_remote_copy(
                src_ref=rows(*block) if src is None else src, dst_ref=rows(*block),
                send_sem=send_sems.at[k], recv_sem=recv_sems.at[k], device_id=to, device_id_type=MESH)

        mine = pltpu.make_async_copy(x_ref, rows(*me), local_sem)
        mine.start()
        first = [copy(0, me, sibling, src=x_ref)]
        first += [copy(1 + j, me, (*chip, c), src=x_ref) for j, chip in enumerate(chips)]
        for cp in first:
            cp.start()
        passed = [copy(4 + j, (*chip, c), sibling) for j, chip in enumerate(chips)]
        for j, chip in enumerate(chips):
            copy(1 + j, (*chip, c), me).wait_recv()
            passed[j].start()
        copy(0, sibling, me).wait_recv()
        for j, chip in enumerate(chips):
            copy(4 + j, (*chip, 1 - c), me).wait_recv()
        for cp in first + passed:
            cp.wait_send()
        mine.wait()

    return pl.pallas_call(
        body, name=name, out_shape=jax.ShapeDtypeStruct((N_DEV * m_per, n), x_shard.dtype),
        in_specs=[pl.BlockSpec(memory_space=pltpu.VMEM)], out_specs=pl.BlockSpec(memory_space=pltpu.VMEM),
        scratch_shapes=[pltpu.SemaphoreType.DMA((7,)), pltpu.SemaphoreType.DMA((7,)), pltpu.SemaphoreType.DMA],
    )(x_shard)


def _all_gather_hbm(shards, name):
    n = len(shards)
    out_shape = [jax.ShapeDtypeStruct((N_DEV,) + s.shape, s.dtype) for s in shards]

    def body(*refs):
        x_refs, out_refs = refs[:n], refs[n:2 * n]
        send_sems, recv_sems, local_sems = refs[2 * n:]
        x, y, c = _mesh_pos()
        me, sibling = (x, y, c), (x, y, 1 - c)
        chips = [(1 - x, y), (x, 1 - y), (1 - x, 1 - y)]

        def blk(a, p):
            return out_refs[a].at[4 * p[0] + 2 * p[1] + p[2]]

        def copy(a, k, block, to, src=None):
            return pltpu.make_async_remote_copy(
                src_ref=blk(a, block) if src is None else src, dst_ref=blk(a, block),
                send_sem=send_sems.at[7 * a + k], recv_sem=recv_sems.at[7 * a + k], device_id=to, device_id_type=MESH)

        mine = [pltpu.make_async_copy(x_refs[a], blk(a, me), local_sems.at[a]) for a in range(n)]
        for cp in mine:
            cp.start()
        first = []
        for a in range(n):
            first.append(copy(a, 0, me, sibling, src=x_refs[a]))
            first += [copy(a, 1 + j, me, (*chip, c), src=x_refs[a]) for j, chip in enumerate(chips)]
        for cp in first:
            cp.start()
        passed = []
        for j, chip in enumerate(chips):
            for a in range(n):
                copy(a, 1 + j, (*chip, c), me).wait_recv()
                fwd = copy(a, 4 + j, (*chip, c), sibling)
                fwd.start()
                passed.append(fwd)
        for a in range(n):
            copy(a, 0, sibling, me).wait_recv()
            for j, chip in enumerate(chips):
                copy(a, 4 + j, (*chip, 1 - c), me).wait_recv()
        for cp in first + passed:
            cp.wait_send()
        for cp in mine:
            cp.wait()

    return pl.pallas_call(
        body, name=name, out_shape=out_shape, in_specs=[pl.BlockSpec(memory_space=pltpu.VMEM)] * n,
        out_specs=[pl.BlockSpec(memory_space=pl.ANY)] * n,
        scratch_shapes=[pltpu.SemaphoreType.DMA((7 * n,)), pltpu.SemaphoreType.DMA((7 * n,)), pltpu.SemaphoreType.DMA((n,))],
    )(*shards)


def _peers(x, y, c):
    flip = lambda v, f: 1 - v if f else v
    return [(flip(x, m & 4), flip(y, m & 2), flip(c, m & 1)) for m in range(1, N_DEV)]


def _dev_index(p):
    return 4 * p[0] + 2 * p[1] + p[2]


def _push_copies(src_refs, land_refs, send_sems, recv_sems, scatter, receive):
    x, y, c = _mesh_pos()
    me = _dev_index((x, y, c))
    copies = []
    for a, (src, land) in enumerate(zip(src_refs, land_refs)):
        for k, p in enumerate(_peers(x, y, c)):
            copies.append(pltpu.make_async_remote_copy(
                src_ref=src.at[_dev_index(p)] if scatter else src, dst_ref=land.at[_dev_index(p) if receive else me],
                send_sem=send_sems.at[7 * a + k], recv_sem=recv_sems.at[7 * a + k], device_id=p, device_id_type=MESH))
    return copies


_HBM = pl.BlockSpec(memory_space=pltpu.HBM)
_SEM = pl.BlockSpec(memory_space=pltpu.SEMAPHORE)
_EFFECT = pltpu.SideEffectType.DATAFLOW_SIDE_EFFECTING


def _pushes_start(srcs, lands, scatter, name):
    n = len(srcs)

    def body(*refs):
        src_refs, land_refs = refs[:n], refs[n:2 * n]
        send_sems, recv_sems = refs[2 * n], refs[2 * n + 1]
        token = refs[-1]
        for cp in _push_copies(src_refs, land_refs, send_sems, recv_sems, scatter, receive=False):
            cp.start()
        token[...] = jnp.zeros_like(token)

    hbm = lambda a: pltpu.HBM(a.shape, a.dtype)
    sems = pltpu.SemaphoreType.DMA((7 * n,))
    outs = pl.pallas_call(
        body, name=name,
        out_shape=(sems, sems, *[hbm(a) for a in srcs], *[hbm(a) for a in lands], jax.ShapeDtypeStruct((8, 128), F32)),
        in_specs=[_HBM] * (2 * n), out_specs=(_SEM, _SEM, *[_HBM] * (2 * n), pl.BlockSpec(memory_space=pltpu.VMEM)),
        input_output_aliases={i: 2 + i for i in range(2 * n)},
        compiler_params=pltpu.CompilerParams(has_side_effects=_EFFECT),
    )(*[pltpu.with_memory_space_constraint(a, pltpu.HBM) for a in (*srcs, *lands)])
    return (outs[0], outs[1], outs[2:2 + n], outs[2 + n:2 + 2 * n], scatter), outs[-1]


def _pushes_wait(handle, after, name):
    send_sems, recv_sems, srcs, lands, scatter = handle
    n = len(srcs)

    def body(*refs):
        src_refs, land_refs = refs[:n], refs[n:2 * n]
        for cp in _push_copies(src_refs, land_refs, refs[2 * n], refs[2 * n + 1], scatter, receive=True):
            cp.wait_send()
            cp.wait_recv()

    hbm = lambda a: pltpu.HBM(a.shape, a.dtype)
    outs = pl.pallas_call(
        body, name=name, out_shape=tuple(hbm(a) for a in (*srcs, *lands)),
        in_specs=[_HBM] * (2 * n) + [_SEM, _SEM, pl.BlockSpec(memory_space=pl.ANY)], out_specs=tuple([_HBM] * (2 * n)),
        input_output_aliases={i: i for i in range(2 * n)},
        compiler_params=pltpu.CompilerParams(has_side_effects=_EFFECT),
    )(*srcs, *lands, send_sems, recv_sems, after)
    return outs[:n], outs[n:]


def _landing_zones(srcs, name):
    n = len(srcs)

    def body(*refs):
        src_refs, land_refs, bufs, sems = refs[:n], refs[n:2 * n], refs[2 * n:3 * n], refs[3 * n]
        me = _dev_index(_mesh_pos())
        load = [pltpu.make_async_copy(src, buf, sems.at[a]) for a, (src, buf) in enumerate(zip(src_refs, bufs))]
        store = [pltpu.make_async_copy(buf, land.at[me], sems.at[a]) for a, (buf, land) in enumerate(zip(bufs, land_refs))]
        for cp in load:
            cp.start()
        for ld, st in zip(load, store):
            ld.wait()
            st.start()
        for cp in store:
            cp.wait()

    any_spec = pl.BlockSpec(memory_space=pl.ANY)
    return pl.pallas_call(
        body, name=name, out_shape=[jax.ShapeDtypeStruct((N_DEV,) + s.shape, s.dtype) for s in srcs],
        in_specs=[any_spec] * n, out_specs=[any_spec] * n,
        scratch_shapes=[pltpu.VMEM(s.shape, s.dtype) for s in srcs] + [pltpu.SemaphoreType.DMA((n,))],
        compiler_params=pltpu.CompilerParams(vmem_limit_bytes=V7X_VMEM_LIMIT),
    )(*srcs)


def _ffn_forward(x, mod, norm_g, w, tag):
    sh, sc, gate = mod
    h = _modnorm(x, norm_g, sc, sh, f"{tag}_norm")
    u = _ffn_up(h, w["up_t"], f"{tag}_up")
    act = _ffn_act(u, w["dw_w"], w["dw_b"], f"{tag}_act")
    y, x_new = _matmul(act, w["down"], "nn", F32, f"{tag}_down", resid=(x, gate))
    return x_new, (x, h, u, act, y)


def _behind(row, token):
    return row if token is None else row + token[0:1, 0:1]


def _ffn_backward(dx_new, saved, mod, norm_g, w, tag, emit):
    x, h, u, act, y = saved
    _, sc, gate = mod
    dy, d_gate = _gate_bwd(dx_new, y, gate, f"{tag}_gate_bwd")
    d_down = _matmul_tn_acc(act, dy, f"{tag}_down_dw")
    dact = _matmul(dy, w["down"], "nt", BF16, f"{tag}_down_dx")
    du, d_dw_w, d_dw_b = _ffn_act_bwd(u, dact, w["dw_w"], w["dw_b"], f"{tag}_act_bwd")
    d_up_t = _matmul_tn_acc(du, h, f"{tag}_up_dw").reshape(2 * FFN_DIM, -1)
    token = emit([d_up_t, d_down])
    dh = _ffn_up_dx(du, w["up_t"], f"{tag}_up_dx")
    dx, d_w, d_sh = _modnorm_bwd(x, dh, norm_g, _behind(sc, token), dx_new, f"{tag}_norm_bwd")
    return dx, dict(dw_w=d_dw_w.transpose(1, 0, 2).reshape(FFN_CONV_WIDTH, 2 * FFN_DIM),
                    dw_b=d_dw_b.reshape(1, 2 * FFN_DIM), norm_g=d_w * (1.0 + sc), sh=d_sh, sc=d_w * norm_g, gate=d_gate)


def _mixer_forward(x, mod, norm_g, w, rope, tag):
    sh, sc, gate = mod
    h = _modnorm(x, norm_g, sc, sh, f"{tag}_norm")
    z = _matmul(h, w["w_in_t"], "nt", BF16, f"{tag}_in")
    ya = _gmlp_fwd(z, w["gain"], w["wtril"], w["bias_exp"], f"{tag}_gmlp")
    q, k, v = _qk_prep(z, rope[0], rope[1], w["gq"], w["gk"], w["seg"], f"{tag}_qk")
    outs, lses = [], []
    for _, dil in PATTERNS:
        o, l = _attn_fwd(q, k, v, dil, f"{tag}_attn_d{dil}")
        outs.append(o)
        lses.append(l)
    yb, lse = _attn_merge(outs, lses, f"{tag}_merge")
    cat = jnp.concatenate([ya, yb], axis=1)
    y, x_new = _matmul(cat, w["w_out"], "nn", F32, f"{tag}_out", resid=(x, gate))
    return x_new, (x, h, z, q, k, v, yb, lse, cat, y)


def _mixer_backward(dx_new, saved, mod, norm_g, w, rope, tag, emit):
    x, h, z, q, k, v, yb, lse, cat, y = saved
    _, sc, gate = mod
    dy, d_gate = _gate_bwd(dx_new, y, gate, f"{tag}_gate_bwd")
    d_w_out = _matmul_tn_acc(cat, dy, f"{tag}_out_dw")
    dcat = _matmul(dy, w["w_out"], "nt", BF16, f"{tag}_out_dx")
    dz_a, d_sp_w, d_gain, d_bias_exp = _gmlp_bwd(z, dcat, w["gain"], w["wtril"], w["wtril_t"], w["bias_exp"], f"{tag}_gmlp_bwd")
    dyb = dcat[:, A_WIDTH:]
    dqs, dks, dvs = [], [], []
    for _, dil in PATTERNS:
        dq, dk, dv = _attn_bwd(q, k, v, dyb, yb, lse, dil, f"{tag}_attn_bwd_d{dil}")
        dqs.append(dq)
        dks.append(dk)
        dvs.append(dv)
    dz_qkv, d_gq, d_gk = _qk_prep_bwd(z, dqs, dks, dvs, rope[0], rope[1], w["gq"], w["gk"], w["seg"], f"{tag}_qk_bwd")
    dz = jnp.concatenate([dz_a, dz_qkv], axis=1)
    d_w_in_t = _matmul_tn_acc(dz, h, f"{tag}_in_dw")
    token = emit([d_w_in_t, d_w_out])
    dh = _matmul(dz, w["w_in_t"], "nn", F32, f"{tag}_in_dx")
    dx, d_w, d_sh = _modnorm_bwd(x, dh, norm_g, _behind(sc, token), dx_new, f"{tag}_norm_bwd")
    return dx, dict(
        vnorm_g=d_gain.reshape(A_GROUPS, GROUP_DIM), spatial_w=d_sp_w,
        spatial_b=d_bias_exp.reshape(CHUNK, A_GROUPS, GROUP_DIM).sum(-1).T,
        q_norm_g=d_gq.reshape(HEADS, HEAD_DIM).sum(0), k_norm_g=d_gk.reshape(HEADS, HEAD_DIM).sum(0),
        norm_g=d_w * (1.0 + sc), sh=d_sh, sc=d_w * norm_g, gate=d_gate)


def _conformer_forward(x, mod, norm_g, w, tag):
    sh, sc, gate = mod
    h = _modnorm(x, norm_g, sc, sh, f"{tag}_norm")
    p = _matmul(h, w["pw1_t"], "nt", BF16, f"{tag}_pw1", bias=w["pw1_b"])
    s, dc = _conformer_mid(p, w["dw_w"], w["dw_b"], w["ln_g"], w["ln_b"], f"{tag}_mid")
    y, x_new = _matmul(s, w["pw2"], "nn", F32, f"{tag}_pw2", bias=w["pw2_b"], resid=(x, gate))
    return x_new, (x, h, p, dc, s, y)


def _conformer_backward(dx_new, saved, mod, norm_g, w, tag, emit):
    x, h, p, dc, s, y = saved
    _, sc, gate = mod
    dy, d_gate = _gate_bwd(dx_new, y, gate, f"{tag}_gate_bwd")
    d_pw2 = _matmul_tn_acc(s, dy, f"{tag}_pw2_dw")
    d_pw2_b = _colsum_call(dy, f"{tag}_pw2_db")
    ds = _matmul(dy, w["pw2"], "nt", BF16, f"{tag}_pw2_dx")
    ddc, d_dw_w, d_dw_b, d_ln_g, d_ln_b = _conformer_mid_bwd(p, dc, ds, w["ln_g"], w["ln_b"], f"{tag}_mid_bwd")
    dp, d_pw1_b = _conformer_glu_bwd(p, ddc, w["dw_w"], f"{tag}_glu_bwd")
    d_pw1_t = _matmul_tn_acc(dp, h, f"{tag}_pw1_dw")
    token = emit([d_pw1_t, d_pw2])
    dh = _matmul(dp, w["pw1_t"], "nn", F32, f"{tag}_pw1_dx")
    dx, d_w, d_sh = _modnorm_bwd(x, dh, norm_g, _behind(sc, token), dx_new, f"{tag}_norm_bwd")
    return dx, dict(pw1_b=d_pw1_b, dw_w=d_dw_w, dw_b=d_dw_b, ln_g=d_ln_g, ln_b=d_ln_b, pw2_b=d_pw2_b, norm_g=d_w * (1.0 + sc), sh=d_sh, sc=d_w * norm_g, gate=d_gate)


def _local_step(x, target, pos, mod, norm_mix_g, norm_ffn_g, mixer_w, conv_w, ffn_w, fetch, emit):
    d = D_MODEL
    inv_freq = 1.0 / (ROPE_THETA ** (jnp.arange(0, HEAD_DIM, 2, dtype=F32) / HEAD_DIM))
    inv_freq = jnp.tile(inv_freq, 2 * HEADS)[None, :]
    sign = jnp.tile(jnp.concatenate([-jnp.ones(HEAD_DIM // 2, F32), jnp.ones(HEAD_DIM // 2, F32)]), HEADS)[None, :]
    rope = _rope_tables(pos, inv_freq, sign, "rope_tables")
    mods = [[mod[l:l + 1, i * d:(i + 1) * d] for i in range(6)] for l in range(2)]
    mix = [(m[0], m[1], m[2]) for m in mods]
    ffn = [(m[3], m[4], m[5]) for m in mods]
    gm = [norm_mix_g[l:l + 1] for l in range(2)]
    gf = [norm_ffn_g[l:l + 1] for l in range(2)]

    mixer_w = {**mixer_w, **fetch("l0_mix", x)}
    x1, s_mix = _mixer_forward(x, mix[0], gm[0], mixer_w, rope, "l0_mix")
    ffn_w0 = {**ffn_w[0], **fetch("l0_ffn", x1)}
    x2, s_ffn0 = _ffn_forward(x1, ffn[0], gf[0], ffn_w0, "l0_ffn")
    conv_w = {**conv_w, **fetch("l1_conv", x2)}
    x3, s_conv = _conformer_forward(x2, mix[1], gm[1], conv_w, "l1_conv")
    ffn_w1 = {**ffn_w[1], **fetch("l1_ffn", x3)}
    x4, s_ffn1 = _ffn_forward(x3, ffn[1], gf[1], ffn_w1, "l1_ffn")
    dx, loss = _loss_head(x4, target, "loss_head")
    dx, g_ffn1 = _ffn_backward(dx, s_ffn1, ffn[1], gf[1], ffn_w1, "l1_ffn", functools.partial(emit, "l1_ffn"))
    dx, g_conv = _conformer_backward(dx, s_conv, mix[1], gm[1], conv_w, "l1_conv", functools.partial(emit, "l1_conv"))
    dx, g_ffn0 = _ffn_backward(dx, s_ffn0, ffn[0], gf[0], ffn_w0, "l0_ffn", functools.partial(emit, "l0_ffn"))
    dx, g_mix = _mixer_backward(dx, s_mix, mix[0], gm[0], mixer_w, rope, "l0_mix", functools.partial(emit, "l0_mix"))
    blocks = [g_mix, g_ffn0, g_conv, g_ffn1]
    dmod = jnp.stack([jnp.concatenate([a["sh"], a["sc"], a["gate"], b["sh"], b["sc"], b["gate"]], axis=1)[0]
                      for a, b in ((g_mix, g_ffn0), (g_conv, g_ffn1))])
    return loss, dx, dmod, blocks


def _pack(arrs, rows=8):
    flat = jnp.concatenate([a.reshape(-1).astype(F32) for a in arrs])
    n = flat.shape[0]
    cols = -(-n // (rows * 128)) * 128
    return jnp.pad(flat, (0, rows * cols - n)).reshape(rows, cols)


def _unpack(flat, shapes):
    out, off = [], 0
    for shp in shapes:
        n = math.prod(shp)
        out.append(flat[..., off:off + n].reshape(flat.shape[:-1] + tuple(shp)))
        off += n
    return out


def _take_block(a, idx, size, axis):
    return lax.dynamic_slice_in_dim(a, idx * size, size, axis)


def kernel(x, c, positions, ada_w, ada_b, norm_mix_g, norm_ffn_g, ab_w_in, a_vnorm_g, a_spatial_w, a_spatial_b, b_q_norm_g, b_k_norm_g, ab_w_out, conv_pw1_w, conv_pw1_b, conv_dw_w, conv_dw_b, conv_ln_g, conv_ln_b, conv_pw2_w, conv_pw2_b, ffn_up_w, ffn_dw_w, ffn_dw_b, ffn_down_w, loss_target, m_ada_w, m_ada_b, m_norm_mix_g, m_norm_ffn_g, m_ab_w_in, m_a_vnorm_g, m_a_spatial_w, m_a_spatial_b, m_b_q_norm_g, m_b_k_norm_g, m_ab_w_out, m_conv_pw1_w, m_conv_pw1_b, m_conv_dw_w, m_conv_dw_b, m_conv_ln_g, m_conv_ln_b, m_conv_pw2_w, m_conv_pw2_b, m_ffn_up_w, m_ffn_dw_w, m_ffn_dw_b, m_ffn_down_w, v_ada_w, v_ada_b, v_norm_mix_g, v_norm_ffn_g, v_ab_w_in, v_a_vnorm_g, v_a_spatial_w, v_a_spatial_b, v_b_q_norm_g, v_b_k_norm_g, v_ab_w_out, v_conv_pw1_w, v_conv_pw1_b, v_conv_dw_w, v_conv_dw_b, v_conv_ln_g, v_conv_ln_b, v_conv_pw2_w, v_conv_pw2_b, v_ffn_up_w, v_ffn_dw_w, v_ffn_dw_b, v_ffn_down_w):
    weights = dict(ada_w=ada_w, ada_b=ada_b, norm_mix_g=norm_mix_g, norm_ffn_g=norm_ffn_g, ab_w_in=ab_w_in, a_vnorm_g=a_vnorm_g, a_spatial_w=a_spatial_w, a_spatial_b=a_spatial_b, b_q_norm_g=b_q_norm_g, b_k_norm_g=b_k_norm_g, ab_w_out=ab_w_out, conv_pw1_w=conv_pw1_w, conv_pw1_b=conv_pw1_b, conv_dw_w=conv_dw_w, conv_dw_b=conv_dw_b, conv_ln_g=conv_ln_g, conv_ln_b=conv_ln_b, conv_pw2_w=conv_pw2_w, conv_pw2_b=conv_pw2_b, ffn_up_w=ffn_up_w, ffn_dw_w=ffn_dw_w, ffn_dw_b=ffn_dw_b, ffn_down_w=ffn_down_w)
    mom1 = dict(ada_w=m_ada_w, ada_b=m_ada_b, norm_mix_g=m_norm_mix_g, norm_ffn_g=m_norm_ffn_g, ab_w_in=m_ab_w_in, a_vnorm_g=m_a_vnorm_g, a_spatial_w=m_a_spatial_w, a_spatial_b=m_a_spatial_b, b_q_norm_g=m_b_q_norm_g, b_k_norm_g=m_b_k_norm_g, ab_w_out=m_ab_w_out, conv_pw1_w=m_conv_pw1_w, conv_pw1_b=m_conv_pw1_b, conv_dw_w=m_conv_dw_w, conv_dw_b=m_conv_dw_b, conv_ln_g=m_conv_ln_g, conv_ln_b=m_conv_ln_b, conv_pw2_w=m_conv_pw2_w, conv_pw2_b=m_conv_pw2_b, ffn_up_w=m_ffn_up_w, ffn_dw_w=m_ffn_dw_w, ffn_dw_b=m_ffn_dw_b, ffn_down_w=m_ffn_down_w)
    mom2 = dict(ada_w=v_ada_w, ada_b=v_ada_b, norm_mix_g=v_norm_mix_g, norm_ffn_g=v_norm_ffn_g, ab_w_in=v_ab_w_in, a_vnorm_g=v_a_vnorm_g, a_spatial_w=v_a_spatial_w, a_spatial_b=v_a_spatial_b, b_q_norm_g=v_b_q_norm_g, b_k_norm_g=v_b_k_norm_g, ab_w_out=v_ab_w_out, conv_pw1_w=v_conv_pw1_w, conv_pw1_b=v_conv_pw1_b, conv_dw_w=v_conv_dw_w, conv_dw_b=v_conv_dw_b, conv_ln_g=v_conv_ln_g, conv_ln_b=v_conv_ln_b, conv_pw2_w=v_conv_pw2_w, conv_pw2_b=v_conv_pw2_b, ffn_up_w=v_ffn_up_w, ffn_dw_w=v_ffn_dw_w, ffn_dw_b=v_ffn_dw_b, ffn_down_w=v_ffn_down_w)
    order = list(weights)
    d, f2 = D_MODEL, 2 * FFN_DIM
    t = x.shape[1]
    me = 4 * lax.axis_index("x") + 2 * lax.axis_index("y") + lax.axis_index("c")
    for window, dil in PATTERNS:
        assert window // dil == Q_BLOCK and t % (dil * Q_BLOCK) == 0

    small_in = [c[0], conv_pw1_b[0], conv_dw_w[0], conv_dw_b[0], conv_ln_g[0], conv_ln_b[0], conv_pw2_b[0], ffn_dw_w]
    g1 = _all_gather_vmem(_pack(small_in, rows=8), "gather_small").reshape(N_DEV, -1)
    c_all, pw1_b, dw_w, dw_b, ln_g, ln_b, pw2_b, fdw_w = _unpack(g1, [a.shape for a in small_in])
    pw1_b, dw_b, ln_g, ln_b, pw2_b = [a.reshape(1, -1) for a in (pw1_b, dw_b, ln_g, ln_b, pw2_b)]
    dw_w = dw_w.transpose(1, 0, 2).reshape(CONV_WIDTH, d)
    fdw_w = fdw_w.transpose(1, 2, 0, 3).reshape(2, FFN_CONV_WIDTH, f2)

    stages = dict(l0_mix=[ab_w_in[0].T, ab_w_out[0]], l0_ffn=[ffn_up_w[0].T, ffn_down_w[0]],
                  l1_conv=[conv_pw1_w[0].T, conv_pw2_w[0]], l1_ffn=[ffn_up_w[1].T, ffn_down_w[1]])
    stages = {k: [s.astype(BF16) for s in v] for k, v in stages.items()}
    names = dict(l0_mix=("w_in_t", "w_out"), l0_ffn=("up_t", "down"), l1_conv=("pw1_t", "pw2"), l1_ffn=("up_t", "down"))
    ready = {"l0_mix": [a.reshape(-1, d) for a in _all_gather_hbm(stages["l0_mix"], "gather_mixer_weights")]}
    stages, _ = lax.optimization_barrier((stages, ready))
    arriving, tokens = {}, []
    for stage, group in (("l0_ffn", ("l0_ffn",)), ("l1_conv", ("l1_conv", "l1_ffn"))):
        srcs = [s for g in group for s in stages[g]]
        arriving[stage], token = _pushes_start(
            srcs, _landing_zones(srcs, f"gather_{stage}_zones"), False, f"gather_{stage}_start")
        tokens.append(token)

    def fetch(stage, after):
        if stage in arriving:
            full = [a.reshape(-1, d) for a in _pushes_wait(arriving[stage], after, f"gather_{stage}_wait")[1]]
            ready[stage] = full[:2]
            if stage == "l1_conv":
                ready["l1_ffn"] = full[2:]
        return dict(zip(names[stage], ready[stage]))

    c16 = jnp.pad(c_all, ((0, 2 * N_DEV - c_all.shape[0]), (0, 0)))
    part = jnp.concatenate([_ada_fwd(c16, ada_w[l], f"ada_fwd{l}")[:N_DEV] for l in range(2)], axis=1)
    g2 = _all_gather_vmem(part, "gather_mod").reshape(N_DEV, N_DEV, 2, -1)
    mod = lax.dynamic_index_in_dim(g2, me, axis=1, keepdims=False).transpose(1, 0, 2).reshape(2, 6 * d) + ada_b
    mod = mod + tokens[0][0:1, 0:1] + tokens[1][0:1, 0:1]

    causal = jnp.tril(jnp.ones((CHUNK, CHUNK), bool))
    wtril = jnp.where(causal[None], a_spatial_w[0], 0.0)
    mixer_w = dict(
        gain=a_vnorm_g[0].reshape(1, A_WIDTH), wtril=wtril.astype(BF16),
        wtril_t=wtril.transpose(0, 2, 1).astype(BF16),
        bias_exp=jnp.repeat(a_spatial_b[0].T, GROUP_DIM, axis=1),
        gq=jnp.tile(b_q_norm_g[0], HEADS)[None, :], gk=jnp.tile(b_k_norm_g[0], HEADS)[None, :],
        seg=jnp.kron(jnp.eye(HEADS, dtype=F32), jnp.ones((HEAD_DIM, HEAD_DIM), F32)))
    conv_w = dict(pw1_b=pw1_b, dw_w=dw_w, dw_b=dw_b, ln_g=ln_g, ln_b=ln_b, pw2_b=pw2_b)
    ffn_w = [dict(dw_w=fdw_w[l].reshape(FFN_CONV_WIDTH, 2, FFN_DIM).transpose(1, 0, 2), dw_b=ffn_dw_b[l].reshape(2, 1, FFN_DIM))
             for l in range(2)]

    leaving = {}

    def emit(stage, grads):
        blocks = [g.reshape(N_DEV, g.shape[0] // N_DEV, d) for g in grads]
        leaving[stage], token = _pushes_start(
            blocks, [lax.empty(b.shape, b.dtype) for b in blocks], True, f"reduce_{stage}_start")
        return token

    loss, dx, dmod, (g_mix, g_ffn0, g_conv, g_ffn1) = _local_step(
        x[0], loss_target[0], positions[0].astype(F32)[:, None], mod, norm_mix_g, norm_ffn_g, mixer_w, conv_w, ffn_w,
        fetch, emit)

    me_op = me.astype(jnp.int32).reshape(1)

    def reduced(stage, after):
        blocks, lands = _pushes_wait(leaving[stage], after, f"reduce_{stage}_wait")
        return [_sum_with_own(b, a, me_op, f"reduce_{stage}_sum{i}") for i, (b, a) in enumerate(zip(blocks, lands))]

    (r_up_t1, r_down1), (r_pw1_t, r_pw2), (r_up_t0, r_down0) = [reduced(s, dx) for s in ("l1_ffn", "l1_conv", "l0_ffn")]

    small_g = [
        dmod, jnp.concatenate([g_mix["norm_g"], g_conv["norm_g"]]), jnp.concatenate([g_ffn0["norm_g"], g_ffn1["norm_g"]]),
        g_mix["vnorm_g"], g_mix["spatial_w"], g_mix["spatial_b"], g_mix["q_norm_g"], g_mix["k_norm_g"],
        g_conv["pw1_b"], g_conv["dw_w"], g_conv["dw_b"], g_conv["ln_g"], g_conv["ln_b"], g_conv["pw2_b"],
        jnp.stack([g_ffn0["dw_w"], g_ffn1["dw_w"]]), jnp.concatenate([g_ffn0["dw_b"], g_ffn1["dw_b"]])]
    packed = _pack(small_g, rows=8)
    g3 = _all_gather_vmem(packed, "gather_small_grads").reshape(N_DEV, 8, -1)
    total = _unpack(_sum_slots(g3, "sum_small_grads").reshape(-1), [a.shape for a in small_g])
    (s_dmod, s_mix_g, s_ffn_g, s_vnorm, s_sp_w, s_sp_b, s_gq, s_gk, s_pw1_b, s_dw_w, s_dw_b, s_ln_g, s_ln_b,
     s_pw2_b, s_fdw_w, s_fdw_b) = total
    dmod_all = g3.reshape(N_DEV, -1)[:, :2 * 6 * d].reshape(N_DEV, 2, 6 * d)
    n_ada = ada_w.shape[2]
    dmod16 = jnp.pad(_take_block(dmod_all, me, n_ada, 2), ((0, N_DEV), (0, 0), (0, 0)))
    g_ada_w = jnp.stack([_ada_bwd(c16, dmod16[:, l], f"ada_bwd{l}") for l in range(2)])

    grads = dict(
        ada_w=g_ada_w, ada_b=s_dmod, norm_mix_g=s_mix_g, norm_ffn_g=s_ffn_g,
        a_vnorm_g=s_vnorm[None], a_spatial_w=s_sp_w[None], a_spatial_b=s_sp_b[None], b_q_norm_g=s_gq[None],
        b_k_norm_g=s_gk[None], conv_pw1_w=r_pw1_t.T[None],
        conv_pw1_b=_take_block(s_pw1_b, me, conv_pw1_b.shape[1], 1),
        conv_dw_w=_take_block(s_dw_w, me, conv_dw_w.shape[2], 1)[None],
        conv_dw_b=_take_block(s_dw_b, me, conv_dw_b.shape[1], 1), conv_ln_g=_take_block(s_ln_g, me, conv_ln_g.shape[1], 1),
        conv_ln_b=_take_block(s_ln_b, me, conv_ln_b.shape[1], 1), conv_pw2_w=r_pw2[None],
        conv_pw2_b=_take_block(s_pw2_b, me, conv_pw2_b.shape[1], 1),
        ffn_up_w=jnp.stack([r_up_t0.T, r_up_t1.T]), ffn_dw_w=_take_block(s_fdw_w, me, ffn_dw_w.shape[2], 2),
        ffn_dw_b=s_fdw_b, ffn_down_w=jnp.stack([r_down0, r_down1]))

    large = ("ada_w", "conv_pw1_w", "conv_pw2_w", "ffn_up_w", "ffn_down_w", "ab_w_in", "ab_w_out")
    delta, new_m, new_v = {}, {}, {}
    for name in large:
        if name == "ab_w_in":
            r_in_t, r_out = reduced("l0_mix", new_v["ffn_down_w"])
            grads.update(ab_w_in=r_in_t.T[None], ab_w_out=r_out[None])
        delta[name], new_m[name], new_v[name] = _adamw(weights[name], grads[name], mom1[name], mom2[name], f"adamw_{name}")
    small = [n for n in order if n not in large]
    res = _adamw_small(*[[src[n] for n in small] for src in (weights, grads, mom1, mom2)], "adamw_small")
    for dst, arrs in zip((delta, new_m, new_v), res):
        dst.update(zip(small, arrs))

    loss = lax.psum(loss[0, 0], ("x", "y", "c"))
    return (loss, dx[None], *[grads[n] for n in order], *[delta[n] for n in order],
            *[new_m[n] for n in order], *[new_v[n] for n in order])
```

```python
import functools
import math

import jax
import jax.numpy as jnp
from jax import lax
from jax.experimental import pallas as pl
from jax.experimental.pallas import tpu as pltpu

F32 = jnp.float32
BF16 = jnp.bfloat16
MESH = pl.DeviceIdType.MESH

D_MODEL = 1024
A_WIDTH = 512
A_GROUPS = 4
GROUP_DIM = 128
CHUNK = 128
B_WIDTH = 512
HEADS = 8
HEAD_DIM = 64
PATTERNS = ((128, 1), (512, 4), (2048, 16))
Q_BLOCK = 128
ROPE_THETA = 10000.0
AB_IN = 2560
CONV_WIDTH = 31
FFN_DIM = 2816
FFN_CONV_WIDTH = 3
EPS = 1e-6
NEG = -1e30
N_DEV = 8
ADAM_LR, ADAM_B1, ADAM_B2, ADAM_EPS, ADAM_WD, ADAM_STEP = 0.001, 0.9, 0.999, 1e-08, 0.01, 10

V7X_VMEM_LIMIT = 56 * 2**20
BF16_ROWS = 16
FFN_HALO = 16
CONV_HALO = 32

_NN = (((1,), (0,)), ((), ()))
_NT = (((1,), (1,)), ((), ()))
_TN = (((0,), (0,)), ((), ()))


def _tile(n, prefs=(512, 256, 128)):
    for t in prefs:
        if n % t == 0:
            return t
    return n


def _row_tile(n, cap=512):
    best = n
    for t in range(8, min(n, cap) + 1, 8):
        if n % t == 0:
            best = t
    return best if best <= cap else n


def _params(*sem):
    return pltpu.CompilerParams(dimension_semantics=sem, vmem_limit_bytes=V7X_VMEM_LIMIT)


def _dot(a, b, dims):
    return lax.dot_general(a, b, dims, preferred_element_type=F32)


def _sigmoid(x):
    return 1.0 / (1.0 + jnp.exp(-x))


def _gelu(x):
    return 0.5 * x * (1.0 + lax.erf(x * (2.0 ** -0.5)))


def _gelu_grad(x):
    return 0.5 * (1.0 + lax.erf(x * (2.0 ** -0.5))) + x * jnp.exp(-0.5 * x * x) * (1.0 / math.sqrt(2.0 * math.pi))


def _colsum(v):
    return jnp.sum(v, axis=0, keepdims=True)


MATMUL_VMEM_BUDGET = 40 * 2**20


def _matmul_tiles(m, n, k, out_bytes, with_resid):
    def options(dim):
        opts = [t for t in (1024, 512, 256, 128) if dim % t == 0]
        return opts + [dim] if dim <= 4096 and dim not in opts else opts

    best = None
    for tm in options(m):
        for tn in options(n):
            need = 4 * (tm * k + k * tn) + tm * tn * (4 + 2 * out_bytes) + (24 * tm * tn if with_resid else 0)
            if need <= MATMUL_VMEM_BUDGET and (best is None or tm * tn / (tm + tn) > best[0]):
                best = (tm * tn / (tm + tn), tm, tn)
    return best[1], best[2]


def _matmul_tn_acc(a, b, name, tk=512):
    squeeze = a.ndim == 2
    a3 = a[None] if squeeze else a
    p_, t, m = a3.shape
    n = b.shape[1]
    nk = t // tk

    def body(a_ref, b_ref, o_ref, acc_ref):
        kt = pl.program_id(1)

        @pl.when(kt == 0)
        def _():
            acc_ref[...] = jnp.zeros_like(acc_ref)

        acc_ref[...] += _dot(a_ref[...], b_ref[...], _TN)

        @pl.when(kt == nk - 1)
        def _():
            o_ref[...] = acc_ref[...].astype(BF16)

    out = pl.pallas_call(
        body, name=name, grid=(p_, nk),
        in_specs=[pl.BlockSpec((None, tk, m), lambda p, kt: (p, kt, 0)), pl.BlockSpec((tk, n), lambda p, kt: (kt, 0))],
        out_specs=pl.BlockSpec((None, m, n), lambda p, kt: (p, 0, 0)), out_shape=jax.ShapeDtypeStruct((p_, m, n), BF16),
        scratch_shapes=[pltpu.VMEM((m, n), F32)], compiler_params=_params("parallel", "arbitrary"),
    )(a3, b)
    return out[0] if squeeze else out


def _matmul(a, b, mode, out_dtype, name, bias=None, resid=None):
    if mode == "nn":
        (m, k), (_, n) = a.shape, b.shape
    elif mode == "nt":
        (m, k), (n, _) = a.shape, b.shape
    else:
        (k, m), (_, n) = a.shape, b.shape
    tm, tn = _matmul_tiles(m, n, k, jnp.dtype(out_dtype).itemsize, resid is not None)
    dims = {"nn": _NN, "nt": _NT, "tn": _TN}[mode]
    a_spec = pl.BlockSpec((k, tm), lambda i, j: (0, i)) if mode == "tn" else pl.BlockSpec((tm, k), lambda i, j: (i, 0))
    b_spec = pl.BlockSpec((tn, k), lambda i, j: (j, 0)) if mode == "nt" else pl.BlockSpec((k, tn), lambda i, j: (0, j))
    in_specs, args = [a_spec, b_spec], [a, b]
    row_spec = pl.BlockSpec((1, tn), lambda i, j: (0, j))
    tile_spec = pl.BlockSpec((tm, tn), lambda i, j: (i, j))
    if bias is not None:
        in_specs.append(row_spec)
        args.append(bias)
    if resid is not None:
        in_specs += [tile_spec, row_spec]
        args += list(resid)
    out_shape = [jax.ShapeDtypeStruct((m, n), out_dtype)]
    out_specs = [tile_spec]
    if resid is not None:
        out_shape.append(jax.ShapeDtypeStruct((m, n), F32))
        out_specs.append(tile_spec)

    def body(*refs):
        a_ref, b_ref = refs[0], refs[1]
        pos = 2
        acc = _dot(a_ref[...], b_ref[...], dims)
        if bias is not None:
            acc = acc + refs[pos][...]
            pos += 1
        if resid is not None:
            x_ref, g_ref = refs[pos], refs[pos + 1]
            pos += 2
        refs[pos][...] = acc.astype(out_dtype)
        if resid is not None:
            refs[pos + 1][...] = x_ref[...] + g_ref[...] * acc

    outs = pl.pallas_call(
        body, name=name, grid=(m // tm, n // tn), in_specs=in_specs, out_specs=out_specs, out_shape=out_shape,
        compiler_params=_params("parallel", "parallel"),
    )(*args)
    return outs if resid is not None else outs[0]


def _modnorm(x, g, sc, sh, name):
    t, d = x.shape
    tm = _tile(t)
    row = pl.BlockSpec((1, d), lambda i: (0, 0))
    blk = pl.BlockSpec((tm, d), lambda i: (i, 0))

    def body(x_ref, g_ref, sc_ref, sh_ref, o_ref):
        x = x_ref[...]
        r = lax.rsqrt(jnp.mean(x * x, axis=-1, keepdims=True) + EPS)
        o_ref[...] = ((x * r) * g_ref[...] * (1.0 + sc_ref[...]) + sh_ref[...]).astype(BF16)

    return pl.pallas_call(
        body, name=name, grid=(t // tm,), in_specs=[blk, row, row, row], out_specs=blk,
        out_shape=jax.ShapeDtypeStruct((t, d), BF16), compiler_params=_params("parallel"),
    )(x, g, sc, sh)


def _modnorm_bwd(x, dh, g, sc, dres, name):
    t, d = x.shape
    tm = _tile(t)
    row = pl.BlockSpec((1, d), lambda i: (0, 0))
    blk = pl.BlockSpec((tm, d), lambda i: (i, 0))

    def body(x_ref, dh_ref, g_ref, sc_ref, dres_ref, dx_ref, dw_ref, dsh_ref):
        @pl.when(pl.program_id(0) == 0)
        def _():
            dw_ref[...] = jnp.zeros_like(dw_ref)
            dsh_ref[...] = jnp.zeros_like(dsh_ref)

        x = x_ref[...]
        dh = dh_ref[...].astype(F32)
        r = lax.rsqrt(jnp.mean(x * x, axis=-1, keepdims=True) + EPS)
        xn = x * r
        dxn = dh * (g_ref[...] * (1.0 + sc_ref[...]))
        dx_ref[...] = dres_ref[...] + r * (dxn - xn * jnp.mean(dxn * xn, axis=-1, keepdims=True))
        dw_ref[...] += _colsum(dh * xn)
        dsh_ref[...] += _colsum(dh)

    return pl.pallas_call(
        body, name=name, grid=(t // tm,), in_specs=[blk, blk, row, row, blk], out_specs=[blk, row, row],
        out_shape=[jax.ShapeDtypeStruct((t, d), F32), jax.ShapeDtypeStruct((1, d), F32), jax.ShapeDtypeStruct((1, d), F32)],
        compiler_params=_params("arbitrary"),
    )(x, dh, g, sc, dres)


def _gate_bwd(dxn, y, gate, name):
    t, d = dxn.shape
    tm = _tile(t)
    row = pl.BlockSpec((1, d), lambda i: (0, 0))
    blk = pl.BlockSpec((tm, d), lambda i: (i, 0))

    def body(dxn_ref, y_ref, g_ref, dy_ref, dg_ref):
        @pl.when(pl.program_id(0) == 0)
        def _():
            dg_ref[...] = jnp.zeros_like(dg_ref)

        dxn = dxn_ref[...]
        dy_ref[...] = (dxn * g_ref[...]).astype(BF16)
        dg_ref[...] += _colsum(dxn * y_ref[...])

    return pl.pallas_call(
        body, name=name, grid=(t // tm,), in_specs=[blk, blk, row], out_specs=[blk, row],
        out_shape=[jax.ShapeDtypeStruct((t, d), BF16), jax.ShapeDtypeStruct((1, d), F32)],
        compiler_params=_params("arbitrary"),
    )(dxn, y, gate)


def _loss_head(y, target, name):
    t, d = y.shape
    tm = _tile(t)
    blk = pl.BlockSpec((tm, d), lambda i: (i, 0))
    one = pl.BlockSpec((1, 1), lambda i: (0, 0))

    def body(y_ref, t_ref, dy_ref, loss_ref, acc_ref):
        @pl.when(pl.program_id(0) == 0)
        def _():
            acc_ref[...] = jnp.zeros_like(acc_ref)

        e = y_ref[...] - t_ref[...]
        dy_ref[...] = e * (1.0 / d)
        acc_ref[...] += _colsum(e * e)

        @pl.when(pl.program_id(0) == pl.num_programs(0) - 1)
        def _():
            loss_ref[...] = jnp.sum(acc_ref[...], axis=1, keepdims=True) * (0.5 / d)

    return pl.pallas_call(
        body, name=name, grid=(t // tm,), in_specs=[blk, blk], out_specs=[blk, one],
        out_shape=[jax.ShapeDtypeStruct((t, d), F32), jax.ShapeDtypeStruct((1, 1), F32)],
        scratch_shapes=[pltpu.VMEM((1, d), F32)], compiler_params=_params("arbitrary"),
    )(y, target)


def _group_norm(vg, gain):
    mu = jnp.mean(vg, axis=-1, keepdims=True)
    xc = vg - mu
    rstd = lax.rsqrt(jnp.mean(xc * xc, axis=-1, keepdims=True) + EPS)
    xhat = xc * rstd
    return xhat, rstd, xhat * gain


def _gmlp_fwd(z, gain, wtril, bias_exp, name):
    t = z.shape[0]
    zu = pl.BlockSpec((CHUNK, A_WIDTH), lambda i: (i, 0))
    zv = pl.BlockSpec((CHUNK, A_WIDTH), lambda i: (i, 1))
    full2 = lambda shp: pl.BlockSpec(shp, lambda i: (0, 0))
    w_spec = pl.BlockSpec((A_GROUPS, CHUNK, CHUNK), lambda i: (0, 0, 0))

    def body(zu_ref, zv_ref, gain_ref, w_ref, b_ref, ya_ref):
        ua = _gelu(zu_ref[...].astype(F32))
        vg = _gelu(zv_ref[...].astype(F32))
        for g in range(A_GROUPS):
            sl = slice(g * GROUP_DIM, (g + 1) * GROUP_DIM)
            _, _, vn = _group_norm(vg[:, sl], gain_ref[:, sl])
            f = _dot(w_ref[g], vn.astype(BF16), _NN) + b_ref[:, sl]
            ya_ref[:, sl] = (ua[:, sl] * f).astype(BF16)

    return pl.pallas_call(
        body, name=name, grid=(t // CHUNK,),
        in_specs=[zu, zv, full2((1, A_WIDTH)), w_spec, full2((CHUNK, A_WIDTH))], out_specs=zu,
        out_shape=jax.ShapeDtypeStruct((t, A_WIDTH), BF16), compiler_params=_params("parallel"),
    )(z, z, gain, wtril, bias_exp)


def _gmlp_bwd(z, dcat, gain, wtril, wtril_t, bias_exp, name):
    t = z.shape[0]
    zu = pl.BlockSpec((CHUNK, A_WIDTH), lambda i: (i, 0))
    zv = pl.BlockSpec((CHUNK, A_WIDTH), lambda i: (i, 1))
    full2 = lambda shp: pl.BlockSpec(shp, lambda i: (0, 0))
    w_spec = pl.BlockSpec((A_GROUPS, CHUNK, CHUNK), lambda i: (0, 0, 0))
    dz_spec = pl.BlockSpec((CHUNK, 2 * A_WIDTH), lambda i: (i, 0))

    def body(zu_ref, zv_ref, dya_ref, gain_ref, w_ref, wt_ref, b_ref, dz_ref, dw_ref, dgain_ref, dbias_ref):
        @pl.when(pl.program_id(0) == 0)
        def _():
            dw_ref[...] = jnp.zeros_like(dw_ref)
            dgain_ref[...] = jnp.zeros_like(dgain_ref)
            dbias_ref[...] = jnp.zeros_like(dbias_ref)

        zu_v = zu_ref[...].astype(F32)
        zv_v = zv_ref[...].astype(F32)
        dya = dya_ref[...].astype(F32)
        ua = _gelu(zu_v)
        vg = _gelu(zv_v)
        row = lax.broadcasted_iota(jnp.int32, (CHUNK, CHUNK), 0)
        col = lax.broadcasted_iota(jnp.int32, (CHUNK, CHUNK), 1)
        for g in range(A_GROUPS):
            sl = slice(g * GROUP_DIM, (g + 1) * GROUP_DIM)
            gain_g = gain_ref[:, sl]
            xhat, rstd, vn = _group_norm(vg[:, sl], gain_g)
            vn16 = vn.astype(BF16)
            f = _dot(w_ref[g], vn16, _NN) + b_ref[:, sl]
            df = dya[:, sl] * ua[:, sl]
            df16 = df.astype(BF16)
            dz_ref[:, sl] = (dya[:, sl] * f * _gelu_grad(zu_v[:, sl])).astype(BF16)
            dw_ref[g] += jnp.where(row >= col, _dot(df16, vn16, _NT), 0.0)
            dvn = _dot(wt_ref[g], df16, _NN)
            dgain_ref[:, sl] += _colsum(dvn * xhat)
            dxh = dvn * gain_g
            dvg = rstd * (dxh - jnp.mean(dxh, axis=-1, keepdims=True) - xhat * jnp.mean(dxh * xhat, axis=-1, keepdims=True))
            dz_ref[:, A_WIDTH + g * GROUP_DIM:A_WIDTH + (g + 1) * GROUP_DIM] = (dvg * _gelu_grad(zv_v[:, sl])).astype(BF16)
            dbias_ref[:, sl] += df

    return pl.pallas_call(
        body, name=name, grid=(t // CHUNK,),
        in_specs=[zu, zv, zu, full2((1, A_WIDTH)), w_spec, w_spec, full2((CHUNK, A_WIDTH))],
        out_specs=[dz_spec, w_spec, full2((1, A_WIDTH)), full2((CHUNK, A_WIDTH))],
        out_shape=[jax.ShapeDtypeStruct((t, 2 * A_WIDTH), BF16), jax.ShapeDtypeStruct((A_GROUPS, CHUNK, CHUNK), F32),
                   jax.ShapeDtypeStruct((1, A_WIDTH), F32), jax.ShapeDtypeStruct((CHUNK, A_WIDTH), F32)],
        compiler_params=_params("arbitrary"),
    )(z, z, dcat, gain, wtril, wtril_t, bias_exp)


def _rope_tables(pos, inv_freq, sign, name):
    t = pos.shape[0]
    tm = _tile(t)
    row = pl.BlockSpec((1, B_WIDTH), lambda i: (0, 0))
    blk = pl.BlockSpec((tm, B_WIDTH), lambda i: (i, 0))

    def body(pos_ref, f_ref, s_ref, cos_ref, sin_ref):
        ang = pos_ref[...] * f_ref[...]
        cos_ref[...] = jnp.cos(ang)
        sin_ref[...] = jnp.sin(ang) * s_ref[...]

    return pl.pallas_call(
        body, name=name, grid=(t // tm,), in_specs=[pl.BlockSpec((tm, 1), lambda i: (i, 0)), row, row],
        out_specs=[blk, blk], out_shape=[jax.ShapeDtypeStruct((t, B_WIDTH), F32)] * 2,
        compiler_params=_params("parallel"),
    )(pos, inv_freq, sign)


def _head_sum(v, seg):
    return lax.dot_general(v, seg, _NN, precision=lax.Precision.HIGHEST, preferred_element_type=F32)


def _swap_halves(v):
    lane = lax.broadcasted_iota(jnp.int32, v.shape, 1)
    return jnp.where((lane & (HEAD_DIM - 1)) < HEAD_DIM // 2,pltpu.roll(v, B_WIDTH - HEAD_DIM // 2, 1), pltpu.roll(v, HEAD_DIM // 2, 1))


def _qk_prep(z, cos_t, sin_t, gq, gk, seg, name):
    t = z.shape[0]
    tm = _tile(t, (256, 128))
    col = lambda c: pl.BlockSpec((tm, B_WIDTH), lambda i: (i, c))
    row = pl.BlockSpec((1, B_WIDTH), lambda i: (0, 0))
    blk = col(0)

    def body(q_ref, k_ref, v_ref, cos_ref, sin_ref, gq_ref, gk_ref, seg_ref, qo_ref, ko_ref, vo_ref):
        def norm_rot(x, g):
            r = lax.rsqrt(_head_sum(x * x, seg_ref[...]) * (1.0 / HEAD_DIM) + EPS)
            xn = x * r * g
            return xn * cos_ref[...] + _swap_halves(xn) * sin_ref[...]

        qo_ref[...] = norm_rot(q_ref[...].astype(F32), gq_ref[...]).astype(BF16)
        ko_ref[...] = norm_rot(k_ref[...].astype(F32), gk_ref[...]).astype(BF16)
        vo_ref[...] = v_ref[...].astype(BF16)

    return pl.pallas_call(
        body, name=name, grid=(t // tm,),
        in_specs=[col(2), col(3), col(4), blk, blk, row, row, pl.BlockSpec((B_WIDTH, B_WIDTH), lambda i: (0, 0))],
        out_specs=[blk, blk, blk], out_shape=[jax.ShapeDtypeStruct((t, B_WIDTH), BF16)] * 3,
        compiler_params=_params("parallel"),
    )(z, z, z, cos_t, sin_t, gq, gk, seg)


def _qk_prep_bwd(z, dqs, dks, dvs, cos_t, sin_t, gq, gk, seg, name):
    t = z.shape[0]
    tm = _tile(t, (256, 128))
    col = lambda c: pl.BlockSpec((tm, B_WIDTH), lambda i: (i, c))
    row = pl.BlockSpec((1, B_WIDTH), lambda i: (0, 0))
    blk = col(0)
    nb = len(dqs)

    def body(*refs):
        q_ref, k_ref = refs[0], refs[1]
        dq_refs, dk_refs, dv_refs = refs[2:2 + nb], refs[2 + nb:2 + 2 * nb], refs[2 + 2 * nb:2 + 3 * nb]
        cos_ref, sin_ref, gq_ref, gk_ref, seg_ref, dz_ref, dgq_ref, dgk_ref = refs[2 + 3 * nb:]

        @pl.when(pl.program_id(0) == 0)
        def _():
            dgq_ref[...] = jnp.zeros_like(dgq_ref)
            dgk_ref[...] = jnp.zeros_like(dgk_ref)

        def back(x, d_refs, g, dg_ref):
            dout = d_refs[0][...]
            for r_ in d_refs[1:]:
                dout = dout + r_[...]
            dy = dout * cos_ref[...] + _swap_halves(dout * sin_ref[...])
            r = lax.rsqrt(_head_sum(x * x, seg_ref[...]) * (1.0 / HEAD_DIM) + EPS)
            xn = x * r
            dg_ref[...] += _colsum(dy * xn)
            dxn = dy * g
            return r * (dxn - xn * (_head_sum(dxn * xn, seg_ref[...]) * (1.0 / HEAD_DIM)))

        dz_ref[:, 0:B_WIDTH] = back(q_ref[...].astype(F32), dq_refs, gq_ref[...], dgq_ref).astype(BF16)
        dz_ref[:, B_WIDTH:2 * B_WIDTH] = back(k_ref[...].astype(F32), dk_refs, gk_ref[...], dgk_ref).astype(BF16)
        dv = dv_refs[0][...]
        for r_ in dv_refs[1:]:
            dv = dv + r_[...]
        dz_ref[:, 2 * B_WIDTH:3 * B_WIDTH] = dv.astype(BF16)

    return pl.pallas_call(
        body, name=name, grid=(t // tm,),
        in_specs=[col(2), col(3)] + [blk] * (3 * nb) + [blk, blk, row, row, pl.BlockSpec((B_WIDTH, B_WIDTH), lambda i: (0, 0))],
        out_specs=[pl.BlockSpec((tm, 3 * B_WIDTH), lambda i: (i, 0)), row, row],
        out_shape=[jax.ShapeDtypeStruct((t, 3 * B_WIDTH), BF16), jax.ShapeDtypeStruct((1, B_WIDTH), F32),
                   jax.ShapeDtypeStruct((1, B_WIDTH), F32)],
        compiler_params=_params("arbitrary"),
    )(z, z, *dqs, *dks, *dvs, cos_t, sin_t, gq, gk, seg)


def _subseq(a, dil):
    return a.reshape(a.shape[0] // dil, dil * a.shape[1])


def _attn_fwd(q, k, v, dil, name):
    t = q.shape[0]
    nb = t // dil // Q_BLOCK
    cur = pl.BlockSpec((Q_BLOCK, B_WIDTH), lambda r, i: (i, r))
    prev = pl.BlockSpec((Q_BLOCK, B_WIDTH), lambda r, i: (jnp.maximum(i - 1, 0), r))

    def body(q_ref, kp_ref, kc_ref, vp_ref, vc_ref, o_ref, lse_ref):
        i = pl.program_id(1)
        q = q_ref[...]
        kk = jnp.concatenate([kp_ref[...], kc_ref[...]], axis=0)
        vv = jnp.concatenate([vp_ref[...], vc_ref[...]], axis=0)
        a = lax.broadcasted_iota(jnp.int32, (Q_BLOCK, 2 * Q_BLOCK), 0)
        j = lax.broadcasted_iota(jnp.int32, (Q_BLOCK, 2 * Q_BLOCK), 1)
        dist = a + Q_BLOCK - j
        mask = (dist >= 0) & (dist <= Q_BLOCK) & ((j >= Q_BLOCK) | (i > 0))
        for h in range(HEADS):
            sl = slice(h * HEAD_DIM, (h + 1) * HEAD_DIM)
            s = jnp.where(mask, _dot(q[:, sl], kk[:, sl], _NT) * (HEAD_DIM ** -0.5), NEG)
            m = jnp.max(s, axis=-1, keepdims=True)
            p = jnp.exp(s - m)
            den = jnp.sum(p, axis=-1, keepdims=True)
            o_ref[:, sl] = _dot(p.astype(BF16), vv[:, sl], _NN) / den
            lse_ref[:, sl] = jnp.broadcast_to(m + jnp.log(den), (Q_BLOCK, HEAD_DIM))

    o, lse = pl.pallas_call(
        body, name=name, grid=(dil, nb), in_specs=[cur, prev, cur, prev, cur], out_specs=[cur, cur],
        out_shape=[jax.ShapeDtypeStruct((t // dil, dil * B_WIDTH), F32)] * 2,
        compiler_params=_params("parallel", "parallel"),
    )(_subseq(q, dil), _subseq(k, dil), _subseq(k, dil), _subseq(v, dil), _subseq(v, dil))
    return o.reshape(t, B_WIDTH), lse.reshape(t, B_WIDTH)


def _attn_merge(outs, lses, name):
    t = outs[0].shape[0]
    tm = _tile(t)
    blk = pl.BlockSpec((tm, B_WIDTH), lambda i: (i, 0))
    nb = len(outs)

    def body(*refs):
        o_refs, l_refs, yb_ref, lse_ref = refs[:nb], refs[nb:2 * nb], refs[2 * nb], refs[2 * nb + 1]
        ls = [r[...] for r in l_refs]
        m = functools.reduce(jnp.maximum, ls)
        tot = m + jnp.log(sum(jnp.exp(l - m) for l in ls))
        yb_ref[...] = sum(jnp.exp(l - tot) * o[...] for l, o in zip(ls, o_refs)).astype(BF16)
        lse_ref[...] = tot

    return pl.pallas_call(
        body, name=name, grid=(t // tm,), in_specs=[blk] * (2 * nb), out_specs=[blk, blk],
        out_shape=[jax.ShapeDtypeStruct((t, B_WIDTH), BF16), jax.ShapeDtypeStruct((t, B_WIDTH), F32)],
        compiler_params=_params("parallel"),
    )(*outs, *lses)


def _attn_bwd(q, k, v, do, o, lse, dil, name):
    t = q.shape[0]
    nb = t // dil // Q_BLOCK
    blk = lambda f: pl.BlockSpec((Q_BLOCK, B_WIDTH), lambda r, i: (f(i), r))
    cur = blk(lambda i: jnp.minimum(i, nb - 1))
    prev = blk(lambda i: jnp.clip(i - 1, 0, nb - 1))
    scale = HEAD_DIM ** -0.5

    def body(q_ref, kp_ref, kc_ref, vp_ref, vc_ref, do_ref, o_ref, lse_ref, dq_ref, dk_ref, dv_ref,
             ck_ref, cv_ref, tk_ref, tv_ref):
        i = pl.program_id(1)

        @pl.when(i == 0)
        def _():
            ck_ref[...] = jnp.zeros_like(ck_ref)
            cv_ref[...] = jnp.zeros_like(cv_ref)

        @pl.when(i < nb)
        def _():
            q = q_ref[...]
            kk = jnp.concatenate([kp_ref[...], kc_ref[...]], axis=0)
            vv = jnp.concatenate([vp_ref[...], vc_ref[...]], axis=0)
            do = do_ref[...]
            dof = do.astype(F32)
            of = o_ref[...].astype(F32)
            a = lax.broadcasted_iota(jnp.int32, (Q_BLOCK, 2 * Q_BLOCK), 0)
            j = lax.broadcasted_iota(jnp.int32, (Q_BLOCK, 2 * Q_BLOCK), 1)
            dist = a + Q_BLOCK - j
            mask = (dist >= 0) & (dist <= Q_BLOCK) & ((j >= Q_BLOCK) | (i > 0))
            for h in range(HEADS):
                sl = slice(h * HEAD_DIM, (h + 1) * HEAD_DIM)
                s = jnp.where(mask, _dot(q[:, sl], kk[:, sl], _NT) * scale, NEG)
                p = jnp.exp(s - lse_ref[:, h * HEAD_DIM:h * HEAD_DIM + 1])
                dp = _dot(do[:, sl], vv[:, sl], _NT)
                delta = jnp.sum(dof[:, sl] * of[:, sl], axis=-1, keepdims=True)
                ds = (p * (dp - delta) * scale).astype(BF16)
                dq_ref[:, sl] = _dot(ds, kk[:, sl], _NN)
                dv_t = _dot(do[:, sl], p.astype(BF16), _TN)
                dk_t = _dot(q[:, sl], ds, _TN)
                tk_ref[sl, :] = ck_ref[sl, :] + dk_t[:, :Q_BLOCK]
                tv_ref[sl, :] = cv_ref[sl, :] + dv_t[:, :Q_BLOCK]
                ck_ref[sl, :] = dk_t[:, Q_BLOCK:]
                cv_ref[sl, :] = dv_t[:, Q_BLOCK:]

        @pl.when(i == nb)
        def _():
            tk_ref[...] = ck_ref[...]
            tv_ref[...] = cv_ref[...]

        @pl.when(i >= 1)
        def _():
            dk_ref[...] = tk_ref[...].T
            dv_ref[...] = tv_ref[...].T

    sub = lambda a_: _subseq(a_, dil)
    shape = jax.ShapeDtypeStruct((t // dil, dil * B_WIDTH), F32)
    dq, dk, dv = pl.pallas_call(
        body, name=name, grid=(dil, nb + 1), in_specs=[cur, prev, cur, prev, cur, cur, cur, cur],
        out_specs=[cur, prev, prev], out_shape=[shape] * 3,
        scratch_shapes=[pltpu.VMEM((B_WIDTH, Q_BLOCK), F32)] * 4,
        compiler_params=_params("parallel", "arbitrary"),
    )(sub(q), sub(k), sub(k), sub(v), sub(v), sub(do), sub(o), sub(lse))
    return dq.reshape(t, B_WIDTH), dk.reshape(t, B_WIDTH), dv.reshape(t, B_WIDTH)


FFN_TN = 256
FFN_FWD_CHUNK = 256
FFN_BWD_CHUNK = 128


def _ffn_up(h, up_t, name):
    t, k = h.shape
    tm = _tile(t)

    def body(h_ref, w_ref, o_ref):
        o_ref[...] = _dot(h_ref[...], w_ref[...], _NT).astype(BF16)

    return pl.pallas_call(
        body, name=name, grid=(2, t // tm),
        in_specs=[pl.BlockSpec((tm, k), lambda p, i: (i, 0)), pl.BlockSpec((None, FFN_DIM, k), lambda p, i: (p, 0, 0))],
        out_specs=pl.BlockSpec((None, tm, FFN_DIM), lambda p, i: (p, i, 0)),
        out_shape=jax.ShapeDtypeStruct((2, t, FFN_DIM), BF16), compiler_params=_params("parallel", "parallel"),
    )(h, up_t.reshape(2, FFN_DIM, k))


def _ffn_up_dx(du, up_t, name):
    t = du.shape[1]
    k = up_t.shape[1]
    tm = _tile(t)

    def body(a_ref, b_ref, o_ref):
        o_ref[...] = _dot(a_ref[0], b_ref[0], _NN) + _dot(a_ref[1], b_ref[1], _NN)

    return pl.pallas_call(
        body, name=name, grid=(t // tm,),
        in_specs=[pl.BlockSpec((2, tm, FFN_DIM), lambda i: (0, i, 0)), pl.BlockSpec((2, FFN_DIM, k), lambda i: (0, 0, 0))],
        out_specs=pl.BlockSpec((tm, k), lambda i: (i, 0)), out_shape=jax.ShapeDtypeStruct((t, k), F32),
        compiler_params=_params("parallel"),
    )(du, up_t.reshape(2, FFN_DIM, k))


def _ffn_conv(win, w_ref, b_ref, p):
    x = win.astype(F32)
    x0, x1, x2 = x[FFN_HALO:], pltpu.roll(x, 1, 0)[FFN_HALO:], pltpu.roll(x, 2, 0)[FFN_HALO:]
    return x0, b_ref[p] + w_ref[p, 2:3, :] * x0 + w_ref[p, 1:2, :] * x1 + w_ref[p, 0:1, :] * x2


def _zero_if(cond, v):
    return jnp.where(cond, 0, v).astype(v.dtype)


def _ffn_act(u, dw_w, dw_b, name):
    t = u.shape[1]
    tm = _tile(t)
    chunk = min(FFN_FWD_CHUNK, tm)
    hb = tm // FFN_HALO
    main = pl.BlockSpec((2, tm, FFN_TN), lambda i, j: (0, i, j))
    halo = pl.BlockSpec((2, FFN_HALO, FFN_TN), lambda i, j: (0, jnp.maximum(i * hb - 1, 0), j))
    wsp = pl.BlockSpec((2, FFN_CONV_WIDTH, FFN_TN), lambda i, j: (0, 0, j))
    bsp = pl.BlockSpec((2, 1, FFN_TN), lambda i, j: (0, 0, j))

    def body(u_ref, uh_ref, w_ref, b_ref, o_ref):
        first = pl.program_id(0) == 0

        def emit(rows, wins):
            za, zb = _ffn_conv(wins[0], w_ref, b_ref, 0)[1], _ffn_conv(wins[1], w_ref, b_ref, 1)[1]
            o_ref[rows, :] = (za * _sigmoid(za) * zb).astype(BF16)

        emit(pl.ds(0, chunk), [jnp.concatenate([_zero_if(first, uh_ref[p]), u_ref[p, 0:chunk, :]], axis=0) for p in range(2)])

        def step(c, carry):
            s = pl.multiple_of(c * chunk, chunk)
            emit(pl.ds(s, chunk), [u_ref[p, pl.ds(s - FFN_HALO, chunk + FFN_HALO), :] for p in range(2)])
            return carry

        lax.fori_loop(1, tm // chunk, step, 0)

    return pl.pallas_call(
        body, name=name, grid=(t // tm, FFN_DIM // FFN_TN), in_specs=[main, halo, wsp, bsp],
        out_specs=pl.BlockSpec((tm, FFN_TN), lambda i, j: (i, j)), out_shape=jax.ShapeDtypeStruct((t, FFN_DIM), BF16),
        compiler_params=_params("parallel", "parallel"),
    )(u, u, dw_w, dw_b)


def _fold8(v):
    return jnp.sum(v.reshape(v.shape[0] // 8, 8, v.shape[1]), axis=0)


def _ffn_act_bwd(u, dact, dw_w, dw_b, name):
    t = u.shape[1]
    tm = _tile(t)
    chunk = min(FFN_BWD_CHUNK, tm // 2)
    halo = FFN_HALO
    hb = tm // halo
    nt = t // tm
    last_halo = t // halo - 1
    prev_i = lambda i: jnp.maximum(i * hb - 1, 0)
    next_i = lambda i: jnp.minimum((i + 1) * hb, last_halo)
    main = pl.BlockSpec((2, tm, FFN_TN), lambda j, i: (0, i, j))
    prev = pl.BlockSpec((2, halo, FFN_TN), lambda j, i: (0, prev_i(i), j))
    nxt = pl.BlockSpec((2, halo, FFN_TN), lambda j, i: (0, next_i(i), j))
    wsp = pl.BlockSpec((2, FFN_CONV_WIDTH, FFN_TN), lambda j, i: (0, 0, j))
    bsp = pl.BlockSpec((2, 1, FFN_TN), lambda j, i: (0, 0, j))

    def body(u_ref, up_ref, un_ref, da_ref, dan_ref, w_ref, b_ref, du_ref, dw_ref, db_ref, acc_ref):
        i = pl.program_id(1)
        first, last = i == 0, i == nt - 1
        acc_ref[...] = jnp.zeros_like(acc_ref)

        def emit(rows, wins, dact):
            n = chunk + halo
            (ua, za), (ub, zb) = _ffn_conv(wins[0], w_ref, b_ref, 0), _ffn_conv(wins[1], w_ref, b_ref, 1)
            dact = dact.astype(F32)
            sg = _sigmoid(za)
            dzs = (dact * zb * (sg * (1.0 + za * (1.0 - sg))), dact * (za * sg))
            for p, (dz, um) in enumerate(zip(dzs, (ua, ub))):
                ahead = (dz[:chunk], pltpu.roll(dz, n - 1, 0)[:chunk], pltpu.roll(dz, n - 2, 0)[:chunk])
                um = um[:chunk]
                acc_ref[p, FFN_CONV_WIDTH] += _fold8(ahead[0])
                du = None
                for j, dzj in enumerate(ahead):
                    k = FFN_CONV_WIDTH - 1 - j
                    acc_ref[p, k] += _fold8(dzj * um)
                    term = w_ref[p, k:k + 1, :] * dzj
                    du = term if du is None else du + term
                du_ref[p, rows, :] = du.astype(BF16)

        emit(pl.ds(0, chunk),
             [jnp.concatenate([_zero_if(first, up_ref[p]), u_ref[p, 0:chunk + halo, :]], axis=0) for p in range(2)],
             da_ref[0:chunk + halo, :])

        def step(c, carry):
            s = pl.multiple_of(c * chunk, chunk)
            emit(pl.ds(s, chunk), [u_ref[p, pl.ds(s - halo, chunk + 2 * halo), :] for p in range(2)],
                 da_ref[pl.ds(s, chunk + halo), :])
            return carry

        lax.fori_loop(1, tm // chunk - 1, step, 0)
        s = tm - chunk
        emit(pl.ds(s, chunk),
             [jnp.concatenate([u_ref[p, s - halo:tm, :], _zero_if(last, un_ref[p])], axis=0) for p in range(2)],
             jnp.concatenate([da_ref[s:tm, :], _zero_if(last, dan_ref[...])], axis=0))

        @pl.when(i == 0)
        def _():
            dw_ref[...] = jnp.zeros_like(dw_ref)
            db_ref[...] = jnp.zeros_like(db_ref)

        for p in range(2):
            for k in range(FFN_CONV_WIDTH):
                dw_ref[p, k:k + 1, :] += _colsum(acc_ref[p, k])
            db_ref[p] += _colsum(acc_ref[p, FFN_CONV_WIDTH])

    return pl.pallas_call(
        body, name=name, grid=(FFN_DIM // FFN_TN, nt),
        in_specs=[main, prev, nxt, pl.BlockSpec((tm, FFN_TN), lambda j, i: (i, j)),
                  pl.BlockSpec((halo, FFN_TN), lambda j, i: (next_i(i), j)), wsp, bsp],
        out_specs=[main, wsp, bsp],
        out_shape=[jax.ShapeDtypeStruct((2, t, FFN_DIM), BF16), jax.ShapeDtypeStruct((2, FFN_CONV_WIDTH, FFN_DIM), F32),
                   jax.ShapeDtypeStruct((2, 1, FFN_DIM), F32)],
        scratch_shapes=[pltpu.VMEM((2, FFN_CONV_WIDTH + 1, 8, FFN_TN), F32)],
        compiler_params=_params("parallel", "arbitrary"),
    )(u, u, u, dact, dact, dw_w, dw_b)


CONV_TM = 256
CONV_ROWS = 128
CONV_LANES = 128


def _glu_window(pa_ref, pah_ref, pg_ref, pgh_ref, scr_ref, first):
    ah, gh = pah_ref[...].astype(F32), pgh_ref[...].astype(F32)
    scr_ref[0:CONV_HALO, :] = jnp.where(first, 0.0, ah * _sigmoid(gh))
    scr_ref[CONV_HALO:, :] = pa_ref[...].astype(F32) * _sigmoid(pg_ref[...].astype(F32))


def _tap_slabs(win, rows, ahead):
    n = win.shape[0]
    for s in range(8):
        ws = win if s == 0 else pltpu.roll(win, n - s if ahead else s, 0)
        for q in range(CONV_HALO // 8):
            o = 8 * q + s
            if o < CONV_WIDTH:
                start = 8 * q if ahead else CONV_HALO - 8 * q
                yield CONV_WIDTH - 1 - o, ws[start:start + rows]


def _conformer_specs(t):
    tm = _tile(t, (CONV_TM, 128))
    hb = tm // CONV_HALO
    d = D_MODEL
    main = lambda c: pl.BlockSpec((tm, d), lambda i: (i, c))
    halo = lambda c: pl.BlockSpec((CONV_HALO, d), lambda i: (jnp.maximum(i * hb - 1, 0), c))
    row = pl.BlockSpec((1, d), lambda i: (0, 0))
    wsp = pl.BlockSpec((CONV_WIDTH, d), lambda i: (0, 0))
    return tm, main, halo, row, wsp


def _conformer_mid(p, dw_w, dw_b, ln_g, ln_b, name):
    t = p.shape[0]
    tm, main, halo, row, wsp = _conformer_specs(t)
    d, lanes = D_MODEL, CONV_LANES

    def body(pa_ref, pah_ref, pg_ref, pgh_ref, w_ref, b_ref, g_ref, lb_ref, o_ref, dc_ref, scr_ref):
        _glu_window(pa_ref, pah_ref, pg_ref, pgh_ref, scr_ref, pl.program_id(0) == 0)
        for c in range(d // lanes):
            ls = slice(c * lanes, (c + 1) * lanes)
            acc = jnp.broadcast_to(b_ref[:, ls], (tm, lanes))
            for k, slab in _tap_slabs(scr_ref[:, ls], tm, False):
                acc = acc + w_ref[k:k + 1, ls] * slab
            dc_ref[:, ls] = acc

        def norm(r, carry):
            r0 = pl.multiple_of(r * 32, 32)
            dc = dc_ref[pl.ds(r0, 32), :]
            xc = dc - jnp.mean(dc, axis=-1, keepdims=True)
            ln = xc * lax.rsqrt(jnp.mean(xc * xc, axis=-1, keepdims=True) + EPS) * g_ref[...] + lb_ref[...]
            o_ref[pl.ds(r0, 32), :] = (ln * _sigmoid(ln)).astype(BF16)
            return carry

        lax.fori_loop(0, tm // 32, norm, 0)

    return pl.pallas_call(
        body, name=name, grid=(t // tm,), in_specs=[main(0), halo(0), main(1), halo(1), wsp, row, row, row],
        out_specs=[main(0), main(0)], out_shape=[jax.ShapeDtypeStruct((t, d), BF16), jax.ShapeDtypeStruct((t, d), F32)],
        scratch_shapes=[pltpu.VMEM((tm + CONV_HALO, d), F32)], compiler_params=_params("parallel"),
    )(p, p, p, p, dw_w, dw_b, ln_g, ln_b)


def _conformer_mid_bwd(p, dc, ds, ln_g, ln_b, name):
    t = p.shape[0]
    tm, main, halo, row, wsp = _conformer_specs(t)
    d, nt = D_MODEL, t // tm
    rows, lanes = CONV_ROWS, CONV_LANES

    def body(pa_ref, pah_ref, pg_ref, pgh_ref, dc_ref, ds_ref, g_ref, lb_ref,
             ddc_ref, dw_ref, db_ref, dg_ref, dlb_ref, scr_ref, wacc_ref, racc_ref):
        i = pl.program_id(0)

        @pl.when(i == 0)
        def _():
            wacc_ref[...] = jnp.zeros_like(wacc_ref)
            racc_ref[...] = jnp.zeros_like(racc_ref)

        _glu_window(pa_ref, pah_ref, pg_ref, pgh_ref, scr_ref, i == 0)

        def norm_bwd(r, carry):
            r0 = pl.multiple_of(r * 32, 32)
            dcv = dc_ref[pl.ds(r0, 32), :]
            xc = dcv - jnp.mean(dcv, axis=-1, keepdims=True)
            rstd = lax.rsqrt(jnp.mean(xc * xc, axis=-1, keepdims=True) + EPS)
            xhat = xc * rstd
            ln = xhat * g_ref[...] + lb_ref[...]
            sg = _sigmoid(ln)
            dln = ds_ref[pl.ds(r0, 32), :].astype(F32) * (sg * (1.0 + ln * (1.0 - sg)))
            dxh = dln * g_ref[...]
            ddc = rstd * (dxh - jnp.mean(dxh, axis=-1, keepdims=True) - xhat * jnp.mean(dxh * xhat, axis=-1, keepdims=True))
            ddc_ref[pl.ds(r0, 32), :] = ddc
            racc_ref[0] += _fold8(dln * xhat)
            racc_ref[1] += _fold8(dln)
            racc_ref[2] += _fold8(ddc)
            return carry

        lax.fori_loop(0, tm // 32, norm_bwd, 0)

        for c in range(d // lanes):
            ls = slice(c * lanes, (c + 1) * lanes)

            def taps(r, carry, ls=ls):
                r0 = pl.multiple_of(r * rows, rows)
                ddc = ddc_ref[pl.ds(r0, rows), ls]
                for k, slab in _tap_slabs(scr_ref[pl.ds(r0, rows + CONV_HALO), ls], rows, False):
                    wacc_ref[k, :, ls] += _fold8(ddc * slab)
                return carry

            lax.fori_loop(0, tm // rows, taps, 0)

        @pl.when(i == nt - 1)
        def _():
            for k in range(CONV_WIDTH):
                dw_ref[k:k + 1, :] = _colsum(wacc_ref[k])
            dg_ref[...] = _colsum(racc_ref[0])
            dlb_ref[...] = _colsum(racc_ref[1])
            db_ref[...] = _colsum(racc_ref[2])

    return pl.pallas_call(
        body, name=name, grid=(nt,), in_specs=[main(0), halo(0), main(1), halo(1), main(0), main(0), row, row],
        out_specs=[main(0), wsp, row, row, row],
        out_shape=[jax.ShapeDtypeStruct((t, d), F32), jax.ShapeDtypeStruct((CONV_WIDTH, d), F32)]
        + [jax.ShapeDtypeStruct((1, d), F32)] * 3,
        scratch_shapes=[pltpu.VMEM((tm + CONV_HALO, d), F32), pltpu.VMEM((CONV_WIDTH, 8, d), F32), pltpu.VMEM((3, 8, d), F32)],
        compiler_params=_params("arbitrary"),
    )(p, p, p, p, dc, ds, ln_g, ln_b)


def _conformer_glu_bwd(p, ddc, dw_w, name):
    t = p.shape[0]
    d = D_MODEL
    tm = _tile(t, (CONV_TM, 128))
    hb = tm // CONV_HALO
    nt = t // tm
    last_halo = t // CONV_HALO - 1
    rows, lanes = CONV_ROWS, CONV_LANES
    col = lambda c: pl.BlockSpec((tm, d), lambda i: (i, c))
    nxt = pl.BlockSpec((CONV_HALO, d), lambda i: (jnp.minimum((i + 1) * hb, last_halo), 0))

    def body(pa_ref, pg_ref, ddc_ref, ddcn_ref, w_ref, dp_ref, db_ref, scr_ref, acc_ref):
        i = pl.program_id(0)

        @pl.when(i == 0)
        def _():
            acc_ref[...] = jnp.zeros_like(acc_ref)

        scr_ref[0:tm, :] = ddc_ref[...]
        scr_ref[tm:, :] = _zero_if(i == nt - 1, ddcn_ref[...])
        for c in range(d // lanes):
            ls = slice(c * lanes, (c + 1) * lanes)
            gs = slice(d + c * lanes, d + (c + 1) * lanes)

            def taps(r, carry, ls=ls, gs=gs):
                r0 = pl.multiple_of(r * rows, rows)
                dglu = None
                for k, slab in _tap_slabs(scr_ref[pl.ds(r0, rows + CONV_HALO), ls], rows, True):
                    term = w_ref[k:k + 1, ls] * slab
                    dglu = term if dglu is None else dglu + term
                a = pa_ref[pl.ds(r0, rows), ls].astype(F32)
                sg = _sigmoid(pg_ref[pl.ds(r0, rows), ls].astype(F32))
                da = (dglu * sg).astype(BF16)
                dg = (dglu * a * sg * (1.0 - sg)).astype(BF16)
                dp_ref[pl.ds(r0, rows), ls] = da
                dp_ref[pl.ds(r0, rows), gs] = dg
                acc_ref[:, ls] += _fold8(da.astype(F32))
                acc_ref[:, gs] += _fold8(dg.astype(F32))
                return carry

            lax.fori_loop(0, tm // rows, taps, 0)

        @pl.when(i == nt - 1)
        def _():
            db_ref[...] = _colsum(acc_ref[...])

    return pl.pallas_call(
        body, name=name, grid=(nt,),
        in_specs=[col(0), col(1), col(0), nxt, pl.BlockSpec((CONV_WIDTH, d), lambda i: (0, 0))],
        out_specs=[pl.BlockSpec((tm, 2 * d), lambda i: (i, 0)), pl.BlockSpec((1, 2 * d), lambda i: (0, 0))],
        out_shape=[jax.ShapeDtypeStruct((t, 2 * d), BF16), jax.ShapeDtypeStruct((1, 2 * d), F32)],
        scratch_shapes=[pltpu.VMEM((tm + CONV_HALO, d), F32), pltpu.VMEM((8, 2 * d), F32)],
        compiler_params=_params("arbitrary"),
    )(p, p, ddc, ddc, dw_w)


def _colsum_call(a, name):
    t, n = a.shape
    tm = _tile(t)

    def body(a_ref, o_ref):
        @pl.when(pl.program_id(0) == 0)
        def _():
            o_ref[...] = jnp.zeros_like(o_ref)

        o_ref[...] += _colsum(a_ref[...].astype(F32))

    return pl.pallas_call(
        body, name=name, grid=(t // tm,), in_specs=[pl.BlockSpec((tm, n), lambda i: (i, 0))],
        out_specs=pl.BlockSpec((1, n), lambda i: (0, 0)), out_shape=jax.ShapeDtypeStruct((1, n), F32),
        compiler_params=_params("arbitrary"),
    )(a)


def _ada_fwd(c_all, w, name):
    rows, d = c_all.shape
    n = w.shape[1]
    tn = _tile(n, (256, 128))

    def body(c_ref, w_ref, o_ref):
        c = c_ref[...]
        o_ref[...] = _dot((c * _sigmoid(c)).astype(BF16), w_ref[...].astype(BF16), _NN)

    return pl.pallas_call(
        body, name=name, grid=(n // tn,),
        in_specs=[pl.BlockSpec((rows, d), lambda j: (0, 0)), pl.BlockSpec((d, tn), lambda j: (0, j))],
        out_specs=pl.BlockSpec((rows, tn), lambda j: (0, j)), out_shape=jax.ShapeDtypeStruct((rows, n), F32),
        compiler_params=_params("parallel"),
    )(c_all, w)


def _ada_bwd(c_all, dmod, name):
    rows, d = c_all.shape
    n = dmod.shape[1]
    tn = _tile(n, (256, 128))

    def body(c_ref, g_ref, o_ref):
        c = c_ref[...]
        o_ref[...] = _dot((c * _sigmoid(c)).astype(BF16), g_ref[...].astype(BF16), _TN)

    return pl.pallas_call(
        body, name=name, grid=(n // tn,),
        in_specs=[pl.BlockSpec((rows, d), lambda j: (0, 0)), pl.BlockSpec((rows, tn), lambda j: (0, j))],
        out_specs=pl.BlockSpec((d, tn), lambda j: (0, j)), out_shape=jax.ShapeDtypeStruct((d, n), F32),
        compiler_params=_params("parallel"),
    )(c_all, dmod)


def _sum_slots(a, name):
    s, r, c = a.shape
    tr = _row_tile(r, 256)

    def body(a_ref, o_ref):
        acc = a_ref[0].astype(F32)
        for k in range(1, s):
            acc = acc + a_ref[k].astype(F32)
        o_ref[...] = acc

    return pl.pallas_call(
        body, name=name, grid=(r // tr,), in_specs=[pl.BlockSpec((s, tr, c), lambda i: (0, i, 0))],
        out_specs=pl.BlockSpec((tr, c), lambda i: (i, 0)), out_shape=jax.ShapeDtypeStruct((r, c), F32),
        compiler_params=_params("parallel"),
    )(a)


def _sum_with_own(blocks, land, me, name):
    s, r, c = land.shape
    tr = _row_tile(r, 256)
    slot = lambda k: pl.BlockSpec((None, tr, c), lambda i, me_ref: ((me_ref[0] + k) % s, i, 0))

    def body(me_ref, own_ref, *refs):
        o_ref = refs[-1]
        acc = own_ref[...].astype(F32)
        for ref in refs[:-1]:
            acc = acc + ref[...].astype(F32)
        o_ref[...] = acc

    return pl.pallas_call(
        body, name=name, out_shape=jax.ShapeDtypeStruct((r, c), F32),
        grid_spec=pltpu.PrefetchScalarGridSpec(
            num_scalar_prefetch=1, grid=(r // tr,), in_specs=[slot(0)] + [slot(k) for k in range(1, s)],
            out_specs=pl.BlockSpec((tr, c), lambda i, me_ref: (i, 0))),
        compiler_params=_params("parallel"),
    )(me, blocks, *[land] * (s - 1))


def _adamw_update(w, g, m, v):
    nm = ADAM_B1 * m + (1.0 - ADAM_B1) * g
    nv = ADAM_B2 * v + (1.0 - ADAM_B2) * (g * g)
    m_hat = nm * (1.0 / (1.0 - ADAM_B1 ** ADAM_STEP))
    v_hat = nv * (1.0 / (1.0 - ADAM_B2 ** ADAM_STEP))
    return -ADAM_LR * (m_hat / (jnp.sqrt(v_hat) + ADAM_EPS) + ADAM_WD * w), nm, nv


def _adamw(w, g, m, v, name):
    l, r, c = w.shape
    tr = _row_tile(r, 256)
    blk = pl.BlockSpec((None, tr, c), lambda k, i: (k, i, 0))

    def body(w_ref, g_ref, m_ref, v_ref, d_ref, nm_ref, nv_ref):
        d_ref[...], nm_ref[...], nv_ref[...] = _adamw_update(w_ref[...], g_ref[...], m_ref[...], v_ref[...])

    return pl.pallas_call(
        body, name=name, grid=(l, r // tr), in_specs=[blk] * 4, out_specs=[blk] * 3,
        out_shape=[jax.ShapeDtypeStruct(w.shape, F32)] * 3, compiler_params=_params("parallel", "parallel"),
    )(w, g, m, v)


def _adamw_small(ws, gs, ms, vs, name):
    n = len(ws)
    two_d = lambda a: a.reshape(-1, a.shape[-1])

    def body(*refs):
        ins, outs = refs[:4 * n], refs[4 * n:]
        for a in range(n):
            outs[a][...], outs[n + a][...], outs[2 * n + a][...] = _adamw_update(*[ins[k * n + a][...] for k in range(4)])

    res = pl.pallas_call(
        body, name=name, out_shape=[jax.ShapeDtypeStruct(two_d(w).shape, F32) for w in ws] * 3,
    )(*[two_d(a) for a in (*ws, *gs, *ms, *vs)])
    return [[res[k * n + a].reshape(ws[a].shape) for a in range(n)] for k in range(3)]


def _mesh_pos():
    return lax.axis_index("x"), lax.axis_index("y"), lax.axis_index("c")


def _all_gather_vmem(x_shard, name):
    m_per, n = x_shard.shape

    def body(x_ref, out_ref, send_sems, recv_sems, local_sem):
        x, y, c = _mesh_pos()
        me, sibling = (x, y, c), (x, y, 1 - c)
        chips = [(1 - x, y), (x, 1 - y), (1 - x, 1 - y)]

        def rows(px, py, pc):
            return out_ref.at[pl.ds((4 * px + 2 * py + pc) * m_per, m_per), :]

        def copy(k, block, to, src=None):
            return pltpu.make_async_remote_copy(
                src_ref=rows(*block) if src is None else src, dst_ref=rows(*block),
                send_sem=send_sems.at[k], recv_sem=recv_sems.at[k], device_id=to, device_id_type=MESH)

        mine = pltpu.make_async_copy(x_ref, rows(*me), local_sem)
        mine.start()
        first = [copy(0, me, sibling, src=x_ref)]
        first += [copy(1 + j, me, (*chip, c), src=x_ref) for j, chip in enumerate(chips)]
        for cp in first:
            cp.start()
        passed = [copy(4 + j, (*chip, c), sibling) for j, chip in enumerate(chips)]
        for j, chip in enumerate(chips):
            copy(1 + j, (*chip, c), me).wait_recv()
            passed[j].start()
        copy(0, sibling, me).wait_recv()
        for j, chip in enumerate(chips):
            copy(4 + j, (*chip, 1 - c), me).wait_recv()
        for cp in first + passed:
            cp.wait_send()
        mine.wait()

    return pl.pallas_call(
        body, name=name, out_shape=jax.ShapeDtypeStruct((N_DEV * m_per, n), x_shard.dtype),
        in_specs=[pl.BlockSpec(memory_space=pltpu.VMEM)], out_specs=pl.BlockSpec(memory_space=pltpu.VMEM),
        scratch_shapes=[pltpu.SemaphoreType.DMA((7,)), pltpu.SemaphoreType.DMA((7,)), pltpu.SemaphoreType.DMA],
    )(x_shard)


def _all_gather_hbm(shards, name):
    n = len(shards)
    out_shape = [jax.ShapeDtypeStruct((N_DEV,) + s.shape, s.dtype) for s in shards]

    def body(*refs):
        x_refs, out_refs = refs[:n], refs[n:2 * n]
        send_sems, recv_sems, local_sems = refs[2 * n:]
        x, y, c = _mesh_pos()
        me, sibling = (x, y, c), (x, y, 1 - c)
        chips = [(1 - x, y), (x, 1 - y), (1 - x, 1 - y)]

        def blk(a, p):
            return out_refs[a].at[4 * p[0] + 2 * p[1] + p[2]]

        def copy(a, k, block, to, src=None):
            return pltpu.make_async_remote_copy(
                src_ref=blk(a, block) if src is None else src, dst_ref=blk(a, block),
                send_sem=send_sems.at[7 * a + k], recv_sem=recv_sems.at[7 * a + k], device_id=to, device_id_type=MESH)

        mine = [pltpu.make_async_copy(x_refs[a], blk(a, me), local_sems.at[a]) for a in range(n)]
        for cp in mine:
            cp.start()
        first = []
        for a in range(n):
            first.append(copy(a, 0, me, sibling, src=x_refs[a]))
            first += [copy(a, 1 + j, me, (*chip, c), src=x_refs[a]) for j, chip in enumerate(chips)]
        for cp in first:
            cp.start()
        passed = []
        for j, chip in enumerate(chips):
            for a in range(n):
                copy(a, 1 + j, (*chip, c), me).wait_recv()
                fwd = copy(a, 4 + j, (*chip, c), sibling)
                fwd.start()
                passed.append(fwd)
        for a in range(n):
            copy(a, 0, sibling, me).wait_recv()
            for j, chip in enumerate(chips):
                copy(a, 4 + j, (*chip, 1 - c), me).wait_recv()
        for cp in first + passed:
            cp.wait_send()
        for cp in mine:
            cp.wait()

    return pl.pallas_call(
        body, name=name, out_shape=out_shape, in_specs=[pl.BlockSpec(memory_space=pltpu.VMEM)] * n,
        out_specs=[pl.BlockSpec(memory_space=pl.ANY)] * n,
        scratch_shapes=[pltpu.SemaphoreType.DMA((7 * n,)), pltpu.SemaphoreType.DMA((7 * n,)), pltpu.SemaphoreType.DMA((n,))],
    )(*shards)


def _peers(x, y, c):
    flip = lambda v, f: 1 - v if f else v
    return [(flip(x, m & 4), flip(y, m & 2), flip(c, m & 1)) for m in range(1, N_DEV)]


def _dev_index(p):
    return 4 * p[0] + 2 * p[1] + p[2]


def _push_copies(src_refs, land_refs, send_sems, recv_sems, scatter, receive):
    x, y, c = _mesh_pos()
    me = _dev_index((x, y, c))
    copies = []
    for a, (src, land) in enumerate(zip(src_refs, land_refs)):
        for k, p in enumerate(_peers(x, y, c)):
            copies.append(pltpu.make_async_remote_copy(
                src_ref=src.at[_dev_index(p)] if scatter else src, dst_ref=land.at[_dev_index(p) if receive else me],
                send_sem=send_sems.at[7 * a + k], recv_sem=recv_sems.at[7 * a + k], device_id=p, device_id_type=MESH))
    return copies


_HBM = pl.BlockSpec(memory_space=pltpu.HBM)
_SEM = pl.BlockSpec(memory_space=pltpu.SEMAPHORE)
_EFFECT = pltpu.SideEffectType.DATAFLOW_SIDE_EFFECTING


def _pushes_start(srcs, lands, scatter, name):
    n = len(srcs)

    def body(*refs):
        src_refs, land_refs = refs[:n], refs[n:2 * n]
        send_sems, recv_sems = refs[2 * n], refs[2 * n + 1]
        token = refs[-1]
        for cp in _push_copies(src_refs, land_refs, send_sems, recv_sems, scatter, receive=False):
            cp.start()
        token[...] = jnp.zeros_like(token)

    hbm = lambda a: pltpu.HBM(a.shape, a.dtype)
    sems = pltpu.SemaphoreType.DMA((7 * n,))
    outs = pl.pallas_call(
        body, name=name,
        out_shape=(sems, sems, *[hbm(a) for a in srcs], *[hbm(a) for a in lands], jax.ShapeDtypeStruct((8, 128), F32)),
        in_specs=[_HBM] * (2 * n), out_specs=(_SEM, _SEM, *[_HBM] * (2 * n), pl.BlockSpec(memory_space=pltpu.VMEM)),
        input_output_aliases={i: 2 + i for i in range(2 * n)},
        compiler_params=pltpu.CompilerParams(has_side_effects=_EFFECT),
    )(*[pltpu.with_memory_space_constraint(a, pltpu.HBM) for a in (*srcs, *lands)])
    return (outs[0], outs[1], outs[2:2 + n], outs[2 + n:2 + 2 * n], scatter), outs[-1]


def _pushes_wait(handle, after, name):
    send_sems, recv_sems, srcs, lands, scatter = handle
    n = len(srcs)

    def body(*refs):
        src_refs, land_refs = refs[:n], refs[n:2 * n]
        for cp in _push_copies(src_refs, land_refs, refs[2 * n], refs[2 * n + 1], scatter, receive=True):
            cp.wait_send()
            cp.wait_recv()

    hbm = lambda a: pltpu.HBM(a.shape, a.dtype)
    outs = pl.pallas_call(
        body, name=name, out_shape=tuple(hbm(a) for a in (*srcs, *lands)),
        in_specs=[_HBM] * (2 * n) + [_SEM, _SEM, pl.BlockSpec(memory_space=pl.ANY)], out_specs=tuple([_HBM] * (2 * n)),
        input_output_aliases={i: i for i in range(2 * n)},
        compiler_params=pltpu.CompilerParams(has_side_effects=_EFFECT),
    )(*srcs, *lands, send_sems, recv_sems, after)
    return outs[:n], outs[n:]


def _landing_zones(srcs, name):
    n = len(srcs)

    def body(*refs):
        src_refs, land_refs, bufs, sems = refs[:n], refs[n:2 * n], refs[2 * n:3 * n], refs[3 * n]
        me = _dev_index(_mesh_pos())
        load = [pltpu.make_async_copy(src, buf, sems.at[a]) for a, (src, buf) in enumerate(zip(src_refs, bufs))]
        store = [pltpu.make_async_copy(buf, land.at[me], sems.at[a]) for a, (buf, land) in enumerate(zip(bufs, land_refs))]
        for cp in load:
            cp.start()
        for ld, st in zip(load, store):
            ld.wait()
            st.start()
        for cp in store:
            cp.wait()

    any_spec = pl.BlockSpec(memory_space=pl.ANY)
    return pl.pallas_call(
        body, name=name, out_shape=[jax.ShapeDtypeStruct((N_DEV,) + s.shape, s.dtype) for s in srcs],
        in_specs=[any_spec] * n, out_specs=[any_spec] * n,
        scratch_shapes=[pltpu.VMEM(s.shape, s.dtype) for s in srcs] + [pltpu.SemaphoreType.DMA((n,))],
        compiler_params=pltpu.CompilerParams(vmem_limit_bytes=V7X_VMEM_LIMIT),
    )(*srcs)


def _ffn_forward(x, mod, norm_g, w, tag):
    sh, sc, gate = mod
    h = _modnorm(x, norm_g, sc, sh, f"{tag}_norm")
    u = _ffn_up(h, w["up_t"], f"{tag}_up")
    act = _ffn_act(u, w["dw_w"], w["dw_b"], f"{tag}_act")
    y, x_new = _matmul(act, w["down"], "nn", F32, f"{tag}_down", resid=(x, gate))
    return x_new, (x, h, u, act, y)


def _behind(row, token):
    return row if token is None else row + token[0:1, 0:1]


def _ffn_backward(dx_new, saved, mod, norm_g, w, tag, emit):
    x, h, u, act, y = saved
    _, sc, gate = mod
    dy, d_gate = _gate_bwd(dx_new, y, gate, f"{tag}_gate_bwd")
    d_down = _matmul_tn_acc(act, dy, f"{tag}_down_dw")
    dact = _matmul(dy, w["down"], "nt", BF16, f"{tag}_down_dx")
    du, d_dw_w, d_dw_b = _ffn_act_bwd(u, dact, w["dw_w"], w["dw_b"], f"{tag}_act_bwd")
    d_up_t = _matmul_tn_acc(du, h, f"{tag}_up_dw").reshape(2 * FFN_DIM, -1)
    token = emit([d_up_t, d_down])
    dh = _ffn_up_dx(du, w["up_t"], f"{tag}_up_dx")
    dx, d_w, d_sh = _modnorm_bwd(x, dh, norm_g, _behind(sc, token), dx_new, f"{tag}_norm_bwd")
    return dx, dict(dw_w=d_dw_w.transpose(1, 0, 2).reshape(FFN_CONV_WIDTH, 2 * FFN_DIM),
                    dw_b=d_dw_b.reshape(1, 2 * FFN_DIM), norm_g=d_w * (1.0 + sc), sh=d_sh, sc=d_w * norm_g, gate=d_gate)


def _mixer_forward(x, mod, norm_g, w, rope, tag):
    sh, sc, gate = mod
    h = _modnorm(x, norm_g, sc, sh, f"{tag}_norm")
    z = _matmul(h, w["w_in_t"], "nt", BF16, f"{tag}_in")
    ya = _gmlp_fwd(z, w["gain"], w["wtril"], w["bias_exp"], f"{tag}_gmlp")
    q, k, v = _qk_prep(z, rope[0], rope[1], w["gq"], w["gk"], w["seg"], f"{tag}_qk")
    outs, lses = [], []
    for _, dil in PATTERNS:
        o, l = _attn_fwd(q, k, v, dil, f"{tag}_attn_d{dil}")
        outs.append(o)
        lses.append(l)
    yb, lse = _attn_merge(outs, lses, f"{tag}_merge")
    cat = jnp.concatenate([ya, yb], axis=1)
    y, x_new = _matmul(cat, w["w_out"], "nn", F32, f"{tag}_out", resid=(x, gate))
    return x_new, (x, h, z, q, k, v, yb, lse, cat, y)


def _mixer_backward(dx_new, saved, mod, norm_g, w, rope, tag, emit):
    x, h, z, q, k, v, yb, lse, cat, y = saved
    _, sc, gate = mod
    dy, d_gate = _gate_bwd(dx_new, y, gate, f"{tag}_gate_bwd")
    d_w_out = _matmul_tn_acc(cat, dy, f"{tag}_out_dw")
    dcat = _matmul(dy, w["w_out"], "nt", BF16, f"{tag}_out_dx")
    dz_a, d_sp_w, d_gain, d_bias_exp = _gmlp_bwd(z, dcat, w["gain"], w["wtril"], w["wtril_t"], w["bias_exp"], f"{tag}_gmlp_bwd")
    dyb = dcat[:, A_WIDTH:]
    dqs, dks, dvs = [], [], []
    for _, dil in PATTERNS:
        dq, dk, dv = _attn_bwd(q, k, v, dyb, yb, lse, dil, f"{tag}_attn_bwd_d{dil}")
        dqs.append(dq)
        dks.append(dk)
        dvs.append(dv)
    dz_qkv, d_gq, d_gk = _qk_prep_bwd(z, dqs, dks, dvs, rope[0], rope[1], w["gq"], w["gk"], w["seg"], f"{tag}_qk_bwd")
    dz = jnp.concatenate([dz_a, dz_qkv], axis=1)
    d_w_in_t = _matmul_tn_acc(dz, h, f"{tag}_in_dw")
    token = emit([d_w_in_t, d_w_out])
    dh = _matmul(dz, w["w_in_t"], "nn", F32, f"{tag}_in_dx")
    dx, d_w, d_sh = _modnorm_bwd(x, dh, norm_g, _behind(sc, token), dx_new, f"{tag}_norm_bwd")
    return dx, dict(
        vnorm_g=d_gain.reshape(A_GROUPS, GROUP_DIM), spatial_w=d_sp_w,
        spatial_b=d_bias_exp.reshape(CHUNK, A_GROUPS, GROUP_DIM).sum(-1).T,
        q_norm_g=d_gq.reshape(HEADS, HEAD_DIM).sum(0), k_norm_g=d_gk.reshape(HEADS, HEAD_DIM).sum(0),
        norm_g=d_w * (1.0 + sc), sh=d_sh, sc=d_w * norm_g, gate=d_gate)


def _conformer_forward(x, mod, norm_g, w, tag):
    sh, sc, gate = mod
    h = _modnorm(x, norm_g, sc, sh, f"{tag}_norm")
    p = _matmul(h, w["pw1_t"], "nt", BF16, f"{tag}_pw1", bias=w["pw1_b"])
    s, dc = _conformer_mid(p, w["dw_w"], w["dw_b"], w["ln_g"], w["ln_b"], f"{tag}_mid")
    y, x_new = _matmul(s, w["pw2"], "nn", F32, f"{tag}_pw2", bias=w["pw2_b"], resid=(x, gate))
    return x_new, (x, h, p, dc, s, y)


def _conformer_backward(dx_new, saved, mod, norm_g, w, tag, emit):
    x, h, p, dc, s, y = saved
    _, sc, gate = mod
    dy, d_gate = _gate_bwd(dx_new, y, gate, f"{tag}_gate_bwd")
    d_pw2 = _matmul_tn_acc(s, dy, f"{tag}_pw2_dw")
    d_pw2_b = _colsum_call(dy, f"{tag}_pw2_db")
    ds = _matmul(dy, w["pw2"], "nt", BF16, f"{tag}_pw2_dx")
    ddc, d_dw_w, d_dw_b, d_ln_g, d_ln_b = _conformer_mid_bwd(p, dc, ds, w["ln_g"], w["ln_b"], f"{tag}_mid_bwd")
    dp, d_pw1_b = _conformer_glu_bwd(p, ddc, w["dw_w"], f"{tag}_glu_bwd")
    d_pw1_t = _matmul_tn_acc(dp, h, f"{tag}_pw1_dw")
    token = emit([d_pw1_t, d_pw2])
    dh = _matmul(dp, w["pw1_t"], "nn", F32, f"{tag}_pw1_dx")
    dx, d_w, d_sh = _modnorm_bwd(x, dh, norm_g, _behind(sc, token), dx_new, f"{tag}_norm_bwd")
    return dx, dict(pw1_b=d_pw1_b, dw_w=d_dw_w, dw_b=d_dw_b, ln_g=d_ln_g, ln_b=d_ln_b, pw2_b=d_pw2_b, norm_g=d_w * (1.0 + sc), sh=d_sh, sc=d_w * norm_g, gate=d_gate)


def _local_step(x, target, pos, mod, norm_mix_g, norm_ffn_g, mixer_w, conv_w, ffn_w, fetch, emit):
    d = D_MODEL
    inv_freq = 1.0 / (ROPE_THETA ** (jnp.arange(0, HEAD_DIM, 2, dtype=F32) / HEAD_DIM))
    inv_freq = jnp.tile(inv_freq, 2 * HEADS)[None, :]
    sign = jnp.tile(jnp.concatenate([-jnp.ones(HEAD_DIM // 2, F32), jnp.ones(HEAD_DIM // 2, F32)]), HEADS)[None, :]
    rope = _rope_tables(pos, inv_freq, sign, "rope_tables")
    mods = [[mod[l:l + 1, i * d:(i + 1) * d] for i in range(6)] for l in range(2)]
    mix = [(m[0], m[1], m[2]) for m in mods]
    ffn = [(m[3], m[4], m[5]) for m in mods]
    gm = [norm_mix_g[l:l + 1] for l in range(2)]
    gf = [norm_ffn_g[l:l + 1] for l in range(2)]

    mixer_w = {**mixer_w, **fetch("l0_mix", x)}
    x1, s_mix = _mixer_forward(x, mix[0], gm[0], mixer_w, rope, "l0_mix")
    ffn_w0 = {**ffn_w[0], **fetch("l0_ffn", x1)}
    x2, s_ffn0 = _ffn_forward(x1, ffn[0], gf[0], ffn_w0, "l0_ffn")
    conv_w = {**conv_w, **fetch("l1_conv", x2)}
    x3, s_conv = _conformer_forward(x2, mix[1], gm[1], conv_w, "l1_conv")
    ffn_w1 = {**ffn_w[1], **fetch("l1_ffn", x3)}
    x4, s_ffn1 = _ffn_forward(x3, ffn[1], gf[1], ffn_w1, "l1_ffn")
    dx, loss = _loss_head(x4, target, "loss_head")
    dx, g_ffn1 = _ffn_backward(dx, s_ffn1, ffn[1], gf[1], ffn_w1, "l1_ffn", functools.partial(emit, "l1_ffn"))
    dx, g_conv = _conformer_backward(dx, s_conv, mix[1], gm[1], conv_w, "l1_conv", functools.partial(emit, "l1_conv"))
    dx, g_ffn0 = _ffn_backward(dx, s_ffn0, ffn[0], gf[0], ffn_w0, "l0_ffn", functools.partial(emit, "l0_ffn"))
    dx, g_mix = _mixer_backward(dx, s_mix, mix[0], gm[0], mixer_w, rope, "l0_mix", functools.partial(emit, "l0_mix"))
    blocks = [g_mix, g_ffn0, g_conv, g_ffn1]
    dmod = jnp.stack([jnp.concatenate([a["sh"], a["sc"], a["gate"], b["sh"], b["sc"], b["gate"]], axis=1)[0]
                      for a, b in ((g_mix, g_ffn0), (g_conv, g_ffn1))])
    return loss, dx, dmod, blocks


def _pack(arrs, rows=8):
    flat = jnp.concatenate([a.reshape(-1).astype(F32) for a in arrs])
    n = flat.shape[0]
    cols = -(-n // (rows * 128)) * 128
    return jnp.pad(flat, (0, rows * cols - n)).reshape(rows, cols)


def _unpack(flat, shapes):
    out, off = [], 0
    for shp in shapes:
        n = math.prod(shp)
        out.append(flat[..., off:off + n].reshape(flat.shape[:-1] + tuple(shp)))
        off += n
    return out


def _take_block(a, idx, size, axis):
    return lax.dynamic_slice_in_dim(a, idx * size, size, axis)


def kernel(x, c, positions, ada_w, ada_b, norm_mix_g, norm_ffn_g, ab_w_in, a_vnorm_g, a_spatial_w, a_spatial_b, b_q_norm_g, b_k_norm_g, ab_w_out, conv_pw1_w, conv_pw1_b, conv_dw_w, conv_dw_b, conv_ln_g, conv_ln_b, conv_pw2_w, conv_pw2_b, ffn_up_w, ffn_dw_w, ffn_dw_b, ffn_down_w, loss_target, m_ada_w, m_ada_b, m_norm_mix_g, m_norm_ffn_g, m_ab_w_in, m_a_vnorm_g, m_a_spatial_w, m_a_spatial_b, m_b_q_norm_g, m_b_k_norm_g, m_ab_w_out, m_conv_pw1_w, m_conv_pw1_b, m_conv_dw_w, m_conv_dw_b, m_conv_ln_g, m_conv_ln_b, m_conv_pw2_w, m_conv_pw2_b, m_ffn_up_w, m_ffn_dw_w, m_ffn_dw_b, m_ffn_down_w, v_ada_w, v_ada_b, v_norm_mix_g, v_norm_ffn_g, v_ab_w_in, v_a_vnorm_g, v_a_spatial_w, v_a_spatial_b, v_b_q_norm_g, v_b_k_norm_g, v_ab_w_out, v_conv_pw1_w, v_conv_pw1_b, v_conv_dw_w, v_conv_dw_b, v_conv_ln_g, v_conv_ln_b, v_conv_pw2_w, v_conv_pw2_b, v_ffn_up_w, v_ffn_dw_w, v_ffn_dw_b, v_ffn_down_w):
    weights = dict(ada_w=ada_w, ada_b=ada_b, norm_mix_g=norm_mix_g, norm_ffn_g=norm_ffn_g, ab_w_in=ab_w_in, a_vnorm_g=a_vnorm_g, a_spatial_w=a_spatial_w, a_spatial_b=a_spatial_b, b_q_norm_g=b_q_norm_g, b_k_norm_g=b_k_norm_g, ab_w_out=ab_w_out, conv_pw1_w=conv_pw1_w, conv_pw1_b=conv_pw1_b, conv_dw_w=conv_dw_w, conv_dw_b=conv_dw_b, conv_ln_g=conv_ln_g, conv_ln_b=conv_ln_b, conv_pw2_w=conv_pw2_w, conv_pw2_b=conv_pw2_b, ffn_up_w=ffn_up_w, ffn_dw_w=ffn_dw_w, ffn_dw_b=ffn_dw_b, ffn_down_w=ffn_down_w)
    mom1 = dict(ada_w=m_ada_w, ada_b=m_ada_b, norm_mix_g=m_norm_mix_g, norm_ffn_g=m_norm_ffn_g, ab_w_in=m_ab_w_in, a_vnorm_g=m_a_vnorm_g, a_spatial_w=m_a_spatial_w, a_spatial_b=m_a_spatial_b, b_q_norm_g=m_b_q_norm_g, b_k_norm_g=m_b_k_norm_g, ab_w_out=m_ab_w_out, conv_pw1_w=m_conv_pw1_w, conv_pw1_b=m_conv_pw1_b, conv_dw_w=m_conv_dw_w, conv_dw_b=m_conv_dw_b, conv_ln_g=m_conv_ln_g, conv_ln_b=m_conv_ln_b, conv_pw2_w=m_conv_pw2_w, conv_pw2_b=m_conv_pw2_b, ffn_up_w=m_ffn_up_w, ffn_dw_w=m_ffn_dw_w, ffn_dw_b=m_ffn_dw_b, ffn_down_w=m_ffn_down_w)
    mom2 = dict(ada_w=v_ada_w, ada_b=v_ada_b, norm_mix_g=v_norm_mix_g, norm_ffn_g=v_norm_ffn_g, ab_w_in=v_ab_w_in, a_vnorm_g=v_a_vnorm_g, a_spatial_w=v_a_spatial_w, a_spatial_b=v_a_spatial_b, b_q_norm_g=v_b_q_norm_g, b_k_norm_g=v_b_k_norm_g, ab_w_out=v_ab_w_out, conv_pw1_w=v_conv_pw1_w, conv_pw1_b=v_conv_pw1_b, conv_dw_w=v_conv_dw_w, conv_dw_b=v_conv_dw_b, conv_ln_g=v_conv_ln_g, conv_ln_b=v_conv_ln_b, conv_pw2_w=v_conv_pw2_w, conv_pw2_b=v_conv_pw2_b, ffn_up_w=v_ffn_up_w, ffn_dw_w=v_ffn_dw_w, ffn_dw_b=v_ffn_dw_b, ffn_down_w=v_ffn_down_w)
    order = list(weights)
    d, f2 = D_MODEL, 2 * FFN_DIM
    t = x.shape[1]
    me = 4 * lax.axis_index("x") + 2 * lax.axis_index("y") + lax.axis_index("c")
    for window, dil in PATTERNS:
        assert window // dil == Q_BLOCK and t % (dil * Q_BLOCK) == 0

    small_in = [c[0], conv_pw1_b[0], conv_dw_w[0], conv_dw_b[0], conv_ln_g[0], conv_ln_b[0], conv_pw2_b[0], ffn_dw_w]
    g1 = _all_gather_vmem(_pack(small_in, rows=8), "gather_small").reshape(N_DEV, -1)
    c_all, pw1_b, dw_w, dw_b, ln_g, ln_b, pw2_b, fdw_w = _unpack(g1, [a.shape for a in small_in])
    pw1_b, dw_b, ln_g, ln_b, pw2_b = [a.reshape(1, -1) for a in (pw1_b, dw_b, ln_g, ln_b, pw2_b)]
    dw_w = dw_w.transpose(1, 0, 2).reshape(CONV_WIDTH, d)
    fdw_w = fdw_w.transpose(1, 2, 0, 3).reshape(2, FFN_CONV_WIDTH, f2)

    c16 = jnp.pad(c_all, ((0, 2 * N_DEV - c_all.shape[0]), (0, 0)))
    part = jnp.concatenate([_ada_fwd(c16, ada_w[l], f"ada_fwd{l}")[:N_DEV] for l in range(2)], axis=1)
    g2 = _all_gather_vmem(part, "gather_mod").reshape(N_DEV, N_DEV, 2, -1)
    mod = lax.dynamic_index_in_dim(g2, me, axis=1, keepdims=False).transpose(1, 0, 2).reshape(2, 6 * d) + ada_b

    stages = dict(l0_mix=[ab_w_in[0].T, ab_w_out[0]], l0_ffn=[ffn_up_w[0].T, ffn_down_w[0]],
                  l1_conv=[conv_pw1_w[0].T, conv_pw2_w[0]], l1_ffn=[ffn_up_w[1].T, ffn_down_w[1]])
    stages = {k: [s.astype(BF16) for s in v] for k, v in stages.items()}
    names = dict(l0_mix=("w_in_t", "w_out"), l0_ffn=("up_t", "down"), l1_conv=("pw1_t", "pw2"), l1_ffn=("up_t", "down"))
    ready = {"l0_mix": [a.reshape(-1, d) for a in _all_gather_hbm(stages["l0_mix"], "gather_mixer_weights")]}
    behind = (ready, mod)
    arriving = {}
    for stage, group in (("l0_ffn", ("l0_ffn",)), ("l1_conv", ("l1_conv", "l1_ffn"))):
        srcs, _ = lax.optimization_barrier(([s for g in group for s in stages[g]], behind))
        arriving[stage], behind = _pushes_start(
            srcs, _landing_zones(srcs, f"gather_{stage}_zones"), False, f"gather_{stage}_start")
        mod = mod + behind[0:1, 0:1]

    def fetch(stage, after):
        if stage in arriving:
            full = [a.reshape(-1, d) for a in _pushes_wait(arriving[stage], after, f"gather_{stage}_wait")[1]]
            ready[stage] = full[:2]
            if stage == "l1_conv":
                ready["l1_ffn"] = full[2:]
        return dict(zip(names[stage], ready[stage]))

    causal = jnp.tril(jnp.ones((CHUNK, CHUNK), bool))
    wtril = jnp.where(causal[None], a_spatial_w[0], 0.0)
    mixer_w = dict(
        gain=a_vnorm_g[0].reshape(1, A_WIDTH), wtril=wtril.astype(BF16),
        wtril_t=wtril.transpose(0, 2, 1).astype(BF16),
        bias_exp=jnp.repeat(a_spatial_b[0].T, GROUP_DIM, axis=1),
        gq=jnp.tile(b_q_norm_g[0], HEADS)[None, :], gk=jnp.tile(b_k_norm_g[0], HEADS)[None, :],
        seg=jnp.kron(jnp.eye(HEADS, dtype=F32), jnp.ones((HEAD_DIM, HEAD_DIM), F32)))
    conv_w = dict(pw1_b=pw1_b, dw_w=dw_w, dw_b=dw_b, ln_g=ln_g, ln_b=ln_b, pw2_b=pw2_b)
    ffn_w = [dict(dw_w=fdw_w[l].reshape(FFN_CONV_WIDTH, 2, FFN_DIM).transpose(1, 0, 2), dw_b=ffn_dw_b[l].reshape(2, 1, FFN_DIM))
             for l in range(2)]

    leaving = {}

    def emit(stage, grads):
        blocks = [g.reshape(N_DEV, g.shape[0] // N_DEV, d) for g in grads]
        leaving[stage], token = _pushes_start(
            blocks, [lax.empty(b.shape, b.dtype) for b in blocks], True, f"reduce_{stage}_start")
        return token

    loss, dx, dmod, (g_mix, g_ffn0, g_conv, g_ffn1) = _local_step(
        x[0], loss_target[0], positions[0].astype(F32)[:, None], mod, norm_mix_g, norm_ffn_g, mixer_w, conv_w, ffn_w,
        fetch, emit)

    me_op = me.astype(jnp.int32).reshape(1)

    def reduced(stage, after):
        blocks, lands = _pushes_wait(leaving[stage], after, f"reduce_{stage}_wait")
        return [_sum_with_own(b, a, me_op, f"reduce_{stage}_sum{i}") for i, (b, a) in enumerate(zip(blocks, lands))]

    (r_up_t1, r_down1), (r_pw1_t, r_pw2), (r_up_t0, r_down0) = [reduced(s, dx) for s in ("l1_ffn", "l1_conv", "l0_ffn")]

    small_g = [
        dmod, jnp.concatenate([g_mix["norm_g"], g_conv["norm_g"]]), jnp.concatenate([g_ffn0["norm_g"], g_ffn1["norm_g"]]),
        g_mix["vnorm_g"], g_mix["spatial_w"], g_mix["spatial_b"], g_mix["q_norm_g"], g_mix["k_norm_g"],
        g_conv["pw1_b"], g_conv["dw_w"], g_conv["dw_b"], g_conv["ln_g"], g_conv["ln_b"], g_conv["pw2_b"],
        jnp.stack([g_ffn0["dw_w"], g_ffn1["dw_w"]]), jnp.concatenate([g_ffn0["dw_b"], g_ffn1["dw_b"]])]
    packed = _pack(small_g, rows=8)
    g3 = _all_gather_vmem(packed, "gather_small_grads").reshape(N_DEV, 8, -1)
    total = _unpack(_sum_slots(g3, "sum_small_grads").reshape(-1), [a.shape for a in small_g])
    (s_dmod, s_mix_g, s_ffn_g, s_vnorm, s_sp_w, s_sp_b, s_gq, s_gk, s_pw1_b, s_dw_w, s_dw_b, s_ln_g, s_ln_b,
     s_pw2_b, s_fdw_w, s_fdw_b) = total
    dmod_all = g3.reshape(N_DEV, -1)[:, :2 * 6 * d].reshape(N_DEV, 2, 6 * d)
    n_ada = ada_w.shape[2]
    dmod16 = jnp.pad(_take_block(dmod_all, me, n_ada, 2), ((0, N_DEV), (0, 0), (0, 0)))
    g_ada_w = jnp.stack([_ada_bwd(c16, dmod16[:, l], f"ada_bwd{l}") for l in range(2)])

    grads = dict(
        ada_w=g_ada_w, ada_b=s_dmod, norm_mix_g=s_mix_g, norm_ffn_g=s_ffn_g,
        a_vnorm_g=s_vnorm[None], a_spatial_w=s_sp_w[None], a_spatial_b=s_sp_b[None], b_q_norm_g=s_gq[None],
        b_k_norm_g=s_gk[None], conv_pw1_w=r_pw1_t.T[None],
        conv_pw1_b=_take_block(s_pw1_b, me, conv_pw1_b.shape[1], 1),
        conv_dw_w=_take_block(s_dw_w, me, conv_dw_w.shape[2], 1)[None],
        conv_dw_b=_take_block(s_dw_b, me, conv_dw_b.shape[1], 1), conv_ln_g=_take_block(s_ln_g, me, conv_ln_g.shape[1], 1),
        conv_ln_b=_take_block(s_ln_b, me, conv_ln_b.shape[1], 1), conv_pw2_w=r_pw2[None],
        conv_pw2_b=_take_block(s_pw2_b, me, conv_pw2_b.shape[1], 1),
        ffn_up_w=jnp.stack([r_up_t0.T, r_up_t1.T]), ffn_dw_w=_take_block(s_fdw_w, me, ffn_dw_w.shape[2], 2),
        ffn_dw_b=s_fdw_b, ffn_down_w=jnp.stack([r_down0, r_down1]))

    large = ("ada_w", "conv_pw1_w", "conv_pw2_w", "ffn_up_w", "ffn_down_w", "ab_w_in", "ab_w_out")
    delta, new_m, new_v = {}, {}, {}
    for name in large:
        if name == "ab_w_in":
            r_in_t, r_out = reduced("l0_mix", new_v["ffn_down_w"])
            grads.update(ab_w_in=r_in_t.T[None], ab_w_out=r_out[None])
        delta[name], new_m[name], new_v[name] = _adamw(weights[name], grads[name], mom1[name], mom2[name], f"adamw_{name}")
    small = [n for n in order if n not in large]
    res = _adamw_small(*[[src[n] for n in small] for src in (weights, grads, mom1, mom2)], "adamw_small")
    for dst, arrs in zip((delta, new_m, new_v), res):
        dst.update(zip(small, arrs))

    loss = lax.psum(loss[0, 0], ("x", "y", "c"))
    return (loss, dx[None], *[grads[n] for n in order], *[delta[n] for n in order],
            *[new_m[n] for n in order], *[new_v[n] for n in order])
```

```python
import functools
import math

import jax
import jax.numpy as jnp
from jax import lax
from jax.experimental import pallas as pl
from jax.experimental.pallas import tpu as pltpu

F32 = jnp.float32
BF16 = jnp.bfloat16
MESH = pl.DeviceIdType.MESH

D_MODEL = 1024
A_WIDTH = 512
A_GROUPS = 4
GROUP_DIM = 128
CHUNK = 128
B_WIDTH = 512
HEADS = 8
HEAD_DIM = 64
PATTERNS = ((128, 1), (512, 4), (2048, 16))
Q_BLOCK = 128
ROPE_THETA = 10000.0
AB_IN = 2560
CONV_WIDTH = 31
FFN_DIM = 2816
FFN_CONV_WIDTH = 3
EPS = 1e-6
NEG = -1e30
N_DEV = 8
ADAM_LR, ADAM_B1, ADAM_B2, ADAM_EPS, ADAM_WD, ADAM_STEP = 0.001, 0.9, 0.999, 1e-08, 0.01, 10

V7X_VMEM_LIMIT = 56 * 2**20
BF16_ROWS = 16
FFN_HALO = 16
CONV_HALO = 32

_NN = (((1,), (0,)), ((), ()))
_NT = (((1,), (1,)), ((), ()))
_TN = (((0,), (0,)), ((), ()))


def _tile(n, prefs=(512, 256, 128)):
    for t in prefs:
        if n % t == 0:
            return t
    return n


def _row_tile(n, cap=512):
    best = n
    for t in range(8, min(n, cap) + 1, 8):
        if n % t == 0:
            best = t
    return best if best <= cap else n


def _params(*sem):
    return pltpu.CompilerParams(dimension_semantics=sem, vmem_limit_bytes=V7X_VMEM_LIMIT)


def _dot(a, b, dims):
    return lax.dot_general(a, b, dims, preferred_element_type=F32)


def _sigmoid(x):
    return 1.0 / (1.0 + jnp.exp(-x))


def _gelu(x):
    return 0.5 * x * (1.0 + lax.erf(x * (2.0 ** -0.5)))


def _gelu_grad(x):
    return 0.5 * (1.0 + lax.erf(x * (2.0 ** -0.5))) + x * jnp.exp(-0.5 * x * x) * (1.0 / math.sqrt(2.0 * math.pi))


def _colsum(v):
    return jnp.sum(v, axis=0, keepdims=True)


MATMUL_VMEM_BUDGET = 40 * 2**20


def _matmul_tiles(m, n, k, out_bytes, with_resid):
    def options(dim):
        opts = [t for t in (1024, 512, 256, 128) if dim % t == 0]
        return opts + [dim] if dim <= 4096 and dim not in opts else opts

    best = None
    for tm in options(m):
        for tn in options(n):
            need = 4 * (tm * k + k * tn) + tm * tn * (4 + 2 * out_bytes) + (24 * tm * tn if with_resid else 0)
            if need <= MATMUL_VMEM_BUDGET and (best is None or tm * tn / (tm + tn) > best[0]):
                best = (tm * tn / (tm + tn), tm, tn)
    return best[1], best[2]


def _matmul_tn_acc(a, b, name, tk=512):
    squeeze = a.ndim == 2
    a3 = a[None] if squeeze else a
    p_, t, m = a3.shape
    n = b.shape[1]
    nk = t // tk

    def body(a_ref, b_ref, o_ref, acc_ref):
        kt = pl.program_id(1)

        @pl.when(kt == 0)
        def _():
            acc_ref[...] = jnp.zeros_like(acc_ref)

        acc_ref[...] += _dot(a_ref[...], b_ref[...], _TN)

        @pl.when(kt == nk - 1)
        def _():
            o_ref[...] = acc_ref[...].astype(BF16)

    out = pl.pallas_call(
        body, name=name, grid=(p_, nk),
        in_specs=[pl.BlockSpec((None, tk, m), lambda p, kt: (p, kt, 0)), pl.BlockSpec((tk, n), lambda p, kt: (kt, 0))],
        out_specs=pl.BlockSpec((None, m, n), lambda p, kt: (p, 0, 0)), out_shape=jax.ShapeDtypeStruct((p_, m, n), BF16),
        scratch_shapes=[pltpu.VMEM((m, n), F32)], compiler_params=_params("parallel", "arbitrary"),
    )(a3, b)
    return out[0] if squeeze else out


def _matmul(a, b, mode, out_dtype, name, bias=None, resid=None):
    if mode == "nn":
        (m, k), (_, n) = a.shape, b.shape
    elif mode == "nt":
        (m, k), (n, _) = a.shape, b.shape
    else:
        (k, m), (_, n) = a.shape, b.shape
    tm, tn = _matmul_tiles(m, n, k, jnp.dtype(out_dtype).itemsize, resid is not None)
    dims = {"nn": _NN, "nt": _NT, "tn": _TN}[mode]
    a_spec = pl.BlockSpec((k, tm), lambda i, j: (0, i)) if mode == "tn" else pl.BlockSpec((tm, k), lambda i, j: (i, 0))
    b_spec = pl.BlockSpec((tn, k), lambda i, j: (j, 0)) if mode == "nt" else pl.BlockSpec((k, tn), lambda i, j: (0, j))
    in_specs, args = [a_spec, b_spec], [a, b]
    row_spec = pl.BlockSpec((1, tn), lambda i, j: (0, j))
    tile_spec = pl.BlockSpec((tm, tn), lambda i, j: (i, j))
    if bias is not None:
        in_specs.append(row_spec)
        args.append(bias)
    if resid is not None:
        in_specs += [tile_spec, row_spec]
        args += list(resid)
    out_shape = [jax.ShapeDtypeStruct((m, n), out_dtype)]
    out_specs = [tile_spec]
    if resid is not None:
        out_shape.append(jax.ShapeDtypeStruct((m, n), F32))
        out_specs.append(tile_spec)

    def body(*refs):
        a_ref, b_ref = refs[0], refs[1]
        pos = 2
        acc = _dot(a_ref[...], b_ref[...], dims)
        if bias is not None:
            acc = acc + refs[pos][...]
            pos += 1
        if resid is not None:
            x_ref, g_ref = refs[pos], refs[pos + 1]
            pos += 2
        refs[pos][...] = acc.astype(out_dtype)
        if resid is not None:
            refs[pos + 1][...] = x_ref[...] + g_ref[...] * acc

    outs = pl.pallas_call(
        body, name=name, grid=(m // tm, n // tn), in_specs=in_specs, out_specs=out_specs, out_shape=out_shape,
        compiler_params=_params("parallel", "parallel"),
    )(*args)
    return outs if resid is not None else outs[0]


def _modnorm(x, g, sc, sh, name):
    t, d = x.shape
    tm = _tile(t)
    row = pl.BlockSpec((1, d), lambda i: (0, 0))
    blk = pl.BlockSpec((tm, d), lambda i: (i, 0))

    def body(x_ref, g_ref, sc_ref, sh_ref, o_ref):
        x = x_ref[...]
        r = lax.rsqrt(jnp.mean(x * x, axis=-1, keepdims=True) + EPS)
        o_ref[...] = ((x * r) * g_ref[...] * (1.0 + sc_ref[...]) + sh_ref[...]).astype(BF16)

    return pl.pallas_call(
        body, name=name, grid=(t // tm,), in_specs=[blk, row, row, row], out_specs=blk,
        out_shape=jax.ShapeDtypeStruct((t, d), BF16), compiler_params=_params("parallel"),
    )(x, g, sc, sh)


def _modnorm_bwd(x, dh, g, sc, dres, name):
    t, d = x.shape
    tm = _tile(t)
    row = pl.BlockSpec((1, d), lambda i: (0, 0))
    blk = pl.BlockSpec((tm, d), lambda i: (i, 0))

    def body(x_ref, dh_ref, g_ref, sc_ref, dres_ref, dx_ref, dw_ref, dsh_ref):
        @pl.when(pl.program_id(0) == 0)
        def _():
            dw_ref[...] = jnp.zeros_like(dw_ref)
            dsh_ref[...] = jnp.zeros_like(dsh_ref)

        x = x_ref[...]
        dh = dh_ref[...].astype(F32)
        r = lax.rsqrt(jnp.mean(x * x, axis=-1, keepdims=True) + EPS)
        xn = x * r
        dxn = dh * (g_ref[...] * (1.0 + sc_ref[...]))
        dx_ref[...] = dres_ref[...] + r * (dxn - xn * jnp.mean(dxn * xn, axis=-1, keepdims=True))
        dw_ref[...] += _colsum(dh * xn)
        dsh_ref[...] += _colsum(dh)

    return pl.pallas_call(
        body, name=name, grid=(t // tm,), in_specs=[blk, blk, row, row, blk], out_specs=[blk, row, row],
        out_shape=[jax.ShapeDtypeStruct((t, d), F32), jax.ShapeDtypeStruct((1, d), F32), jax.ShapeDtypeStruct((1, d), F32)],
        compiler_params=_params("arbitrary"),
    )(x, dh, g, sc, dres)


def _gate_bwd(dxn, y, gate, name):
    t, d = dxn.shape
    tm = _tile(t)
    row = pl.BlockSpec((1, d), lambda i: (0, 0))
    blk = pl.BlockSpec((tm, d), lambda i: (i, 0))

    def body(dxn_ref, y_ref, g_ref, dy_ref, dg_ref):
        @pl.when(pl.program_id(0) == 0)
        def _():
            dg_ref[...] = jnp.zeros_like(dg_ref)

        dxn = dxn_ref[...]
        dy_ref[...] = (dxn * g_ref[...]).astype(BF16)
        dg_ref[...] += _colsum(dxn * y_ref[...])

    return pl.pallas_call(
        body, name=name, grid=(t // tm,), in_specs=[blk, blk, row], out_specs=[blk, row],
        out_shape=[jax.ShapeDtypeStruct((t, d), BF16), jax.ShapeDtypeStruct((1, d), F32)],
        compiler_params=_params("arbitrary"),
    )(dxn, y, gate)


def _loss_head(y, target, name):
    t, d = y.shape
    tm = _tile(t)
    blk = pl.BlockSpec((tm, d), lambda i: (i, 0))
    one = pl.BlockSpec((1, 1), lambda i: (0, 0))

    def body(y_ref, t_ref, dy_ref, loss_ref, acc_ref):
        @pl.when(pl.program_id(0) == 0)
        def _():
            acc_ref[...] = jnp.zeros_like(acc_ref)

        e = y_ref[...] - t_ref[...]
        dy_ref[...] = e * (1.0 / d)
        acc_ref[...] += _colsum(e * e)

        @pl.when(pl.program_id(0) == pl.num_programs(0) - 1)
        def _():
            loss_ref[...] = jnp.sum(acc_ref[...], axis=1, keepdims=True) * (0.5 / d)

    return pl.pallas_call(
        body, name=name, grid=(t // tm,), in_specs=[blk, blk], out_specs=[blk, one],
        out_shape=[jax.ShapeDtypeStruct((t, d), F32), jax.ShapeDtypeStruct((1, 1), F32)],
        scratch_shapes=[pltpu.VMEM((1, d), F32)], compiler_params=_params("arbitrary"),
    )(y, target)


def _group_norm(vg, gain):
    mu = jnp.mean(vg, axis=-1, keepdims=True)
    xc = vg - mu
    rstd = lax.rsqrt(jnp.mean(xc * xc, axis=-1, keepdims=True) + EPS)
    xhat = xc * rstd
    return xhat, rstd, xhat * gain


def _gmlp_fwd(z, gain, wtril, bias_exp, name):
    t = z.shape[0]
    zu = pl.BlockSpec((CHUNK, A_WIDTH), lambda i: (i, 0))
    zv = pl.BlockSpec((CHUNK, A_WIDTH), lambda i: (i, 1))
    full2 = lambda shp: pl.BlockSpec(shp, lambda i: (0, 0))
    w_spec = pl.BlockSpec((A_GROUPS, CHUNK, CHUNK), lambda i: (0, 0, 0))

    def body(zu_ref, zv_ref, gain_ref, w_ref, b_ref, ya_ref):
        ua = _gelu(zu_ref[...].astype(F32))
        vg = _gelu(zv_ref[...].astype(F32))
        for g in range(A_GROUPS):
            sl = slice(g * GROUP_DIM, (g + 1) * GROUP_DIM)
            _, _, vn = _group_norm(vg[:, sl], gain_ref[:, sl])
            f = _dot(w_ref[g], vn.astype(BF16), _NN) + b_ref[:, sl]
            ya_ref[:, sl] = (ua[:, sl] * f).astype(BF16)

    return pl.pallas_call(
        body, name=name, grid=(t // CHUNK,),
        in_specs=[zu, zv, full2((1, A_WIDTH)), w_spec, full2((CHUNK, A_WIDTH))], out_specs=zu,
        out_shape=jax.ShapeDtypeStruct((t, A_WIDTH), BF16), compiler_params=_params("parallel"),
    )(z, z, gain, wtril, bias_exp)


def _gmlp_bwd(z, dcat, gain, wtril, wtril_t, bias_exp, name):
    t = z.shape[0]
    zu = pl.BlockSpec((CHUNK, A_WIDTH), lambda i: (i, 0))
    zv = pl.BlockSpec((CHUNK, A_WIDTH), lambda i: (i, 1))
    full2 = lambda shp: pl.BlockSpec(shp, lambda i: (0, 0))
    w_spec = pl.BlockSpec((A_GROUPS, CHUNK, CHUNK), lambda i: (0, 0, 0))
    dz_spec = pl.BlockSpec((CHUNK, 2 * A_WIDTH), lambda i: (i, 0))

    def body(zu_ref, zv_ref, dya_ref, gain_ref, w_ref, wt_ref, b_ref, dz_ref, dw_ref, dgain_ref, dbias_ref):
        @pl.when(pl.program_id(0) == 0)
        def _():
            dw_ref[...] = jnp.zeros_like(dw_ref)
            dgain_ref[...] = jnp.zeros_like(dgain_ref)
            dbias_ref[...] = jnp.zeros_like(dbias_ref)

        zu_v = zu_ref[...].astype(F32)
        zv_v = zv_ref[...].astype(F32)
        dya = dya_ref[...].astype(F32)
        ua = _gelu(zu_v)
        vg = _gelu(zv_v)
        row = lax.broadcasted_iota(jnp.int32, (CHUNK, CHUNK), 0)
        col = lax.broadcasted_iota(jnp.int32, (CHUNK, CHUNK), 1)
        for g in range(A_GROUPS):
            sl = slice(g * GROUP_DIM, (g + 1) * GROUP_DIM)
            gain_g = gain_ref[:, sl]
            xhat, rstd, vn = _group_norm(vg[:, sl], gain_g)
            vn16 = vn.astype(BF16)
            f = _dot(w_ref[g], vn16, _NN) + b_ref[:, sl]
            df = dya[:, sl] * ua[:, sl]
            df16 = df.astype(BF16)
            dz_ref[:, sl] = (dya[:, sl] * f * _gelu_grad(zu_v[:, sl])).astype(BF16)
            dw_ref[g] += jnp.where(row >= col, _dot(df16, vn16, _NT), 0.0)
            dvn = _dot(wt_ref[g], df16, _NN)
            dgain_ref[:, sl] += _colsum(dvn * xhat)
            dxh = dvn * gain_g
            dvg = rstd * (dxh - jnp.mean(dxh, axis=-1, keepdims=True) - xhat * jnp.mean(dxh * xhat, axis=-1, keepdims=True))
            dz_ref[:, A_WIDTH + g * GROUP_DIM:A_WIDTH + (g + 1) * GROUP_DIM] = (dvg * _gelu_grad(zv_v[:, sl])).astype(BF16)
            dbias_ref[:, sl] += df

    return pl.pallas_call(
        body, name=name, grid=(t // CHUNK,),
        in_specs=[zu, zv, zu, full2((1, A_WIDTH)), w_spec, w_spec, full2((CHUNK, A_WIDTH))],
        out_specs=[dz_spec, w_spec, full2((1, A_WIDTH)), full2((CHUNK, A_WIDTH))],
        out_shape=[jax.ShapeDtypeStruct((t, 2 * A_WIDTH), BF16), jax.ShapeDtypeStruct((A_GROUPS, CHUNK, CHUNK), F32),
                   jax.ShapeDtypeStruct((1, A_WIDTH), F32), jax.ShapeDtypeStruct((CHUNK, A_WIDTH), F32)],
        compiler_params=_params("arbitrary"),
    )(z, z, dcat, gain, wtril, wtril_t, bias_exp)


def _rope_tables(pos, inv_freq, sign, name):
    t = pos.shape[0]
    tm = _tile(t)
    row = pl.BlockSpec((1, B_WIDTH), lambda i: (0, 0))
    blk = pl.BlockSpec((tm, B_WIDTH), lambda i: (i, 0))

    def body(pos_ref, f_ref, s_ref, cos_ref, sin_ref):
        ang = pos_ref[...] * f_ref[...]
        cos_ref[...] = jnp.cos(ang)
        sin_ref[...] = jnp.sin(ang) * s_ref[...]

    return pl.pallas_call(
        body, name=name, grid=(t // tm,), in_specs=[pl.BlockSpec((tm, 1), lambda i: (i, 0)), row, row],
        out_specs=[blk, blk], out_shape=[jax.ShapeDtypeStruct((t, B_WIDTH), F32)] * 2,
        compiler_params=_params("parallel"),
    )(pos, inv_freq, sign)


def _head_sum(v, seg):
    hi = v.astype(BF16)
    lo = (v - hi.astype(F32)).astype(BF16)
    return _dot(hi, seg, _NN) + _dot(lo, seg, _NN)


def _swap_halves(v):
    lane = lax.broadcasted_iota(jnp.int32, v.shape, 1)
    return jnp.where((lane & (HEAD_DIM - 1)) < HEAD_DIM // 2,pltpu.roll(v, B_WIDTH - HEAD_DIM // 2, 1), pltpu.roll(v, HEAD_DIM // 2, 1))


DILATIONS = tuple(dil for _, dil in PATTERNS)
SUBSEQ_TM = 256
LANES = 128


def _subseq_shape(t, dil):
    return (t // dil, dil * B_WIDTH)


def _subseq_spec(tm, dil):
    return pl.BlockSpec((tm // dil, dil * B_WIDTH), lambda i: (i, 0))


def _to_subseq(x, scr_ref, dil):
    if dil == 1:
        return x
    tm, w = x.shape
    for c in range(w // LANES):
        scr_ref[c * tm:(c + 1) * tm, :] = x[:, c * LANES:(c + 1) * LANES]
    return jnp.concatenate([scr_ref[pl.ds(c * tm + r, tm // dil, stride=dil), :]
                            for r in range(dil) for c in range(w // LANES)], axis=1)


def _from_subseq(y, scr_ref, dil):
    if dil == 1:
        return y
    n, w = y.shape[0], y.shape[1] // dil
    tm = n * dil
    for r in range(dil):
        for c in range(w // LANES):
            scr_ref[pl.ds(c * tm + r, n, stride=dil), :] = y[:, r * w + c * LANES:r * w + (c + 1) * LANES]
    return jnp.concatenate([scr_ref[c * tm:(c + 1) * tm, :] for c in range(w // LANES)], axis=1)


def _subseq_scratch(tm):
    return pltpu.VMEM((B_WIDTH // LANES * tm, LANES), F32)


def _qk_prep(z, cos_t, sin_t, gq, gk, seg, name):
    t = z.shape[0]
    tm = _tile(t, (SUBSEQ_TM,))
    col = lambda c: pl.BlockSpec((tm, B_WIDTH), lambda i: (i, c))
    row = pl.BlockSpec((1, B_WIDTH), lambda i: (0, 0))
    blk = col(0)
    nd = len(DILATIONS)

    def body(q_ref, k_ref, v_ref, cos_ref, sin_ref, gq_ref, gk_ref, seg_ref, *rest):
        out_refs, scr_ref = rest[:-1], rest[-1]

        def norm_rot(x, g):
            r = lax.rsqrt(_head_sum(x * x, seg_ref[...]) * (1.0 / HEAD_DIM) + EPS)
            xn = x * r * g
            return xn * cos_ref[...] + _swap_halves(xn) * sin_ref[...]

        vals = (norm_rot(q_ref[...].astype(F32), gq_ref[...]), norm_rot(k_ref[...].astype(F32), gk_ref[...]),
                v_ref[...].astype(F32))
        for a, val in enumerate(vals):
            for b, dil in enumerate(DILATIONS):
                out_refs[a * nd + b][...] = _to_subseq(val, scr_ref, dil).astype(BF16)

    outs = pl.pallas_call(
        body, name=name, grid=(t // tm,),
        in_specs=[col(2), col(3), col(4), blk, blk, row, row, pl.BlockSpec((B_WIDTH, B_WIDTH), lambda i: (0, 0))],
        out_specs=[_subseq_spec(tm, dil) for _ in range(3) for dil in DILATIONS],
        out_shape=[jax.ShapeDtypeStruct(_subseq_shape(t, dil), BF16) for _ in range(3) for dil in DILATIONS],
        scratch_shapes=[_subseq_scratch(tm)], compiler_params=_params("parallel"),
    )(z, z, z, cos_t, sin_t, gq, gk, seg)
    return outs[:nd], outs[nd:2 * nd], outs[2 * nd:]


def _qk_prep_bwd(z, dqs, dks, dvs, cos_t, sin_t, gq, gk, seg, name):
    t = z.shape[0]
    tm = _tile(t, (SUBSEQ_TM,))
    col = lambda c: pl.BlockSpec((tm, B_WIDTH), lambda i: (i, c))
    row = pl.BlockSpec((1, B_WIDTH), lambda i: (0, 0))
    blk = col(0)
    nb = len(DILATIONS)
    subs = [_subseq_spec(tm, dil) for dil in DILATIONS]

    def body(*refs):
        q_ref, k_ref = refs[0], refs[1]
        dq_refs, dk_refs, dv_refs = refs[2:2 + nb], refs[2 + nb:2 + 2 * nb], refs[2 + 2 * nb:2 + 3 * nb]
        cos_ref, sin_ref, gq_ref, gk_ref, seg_ref, dz_ref, dgq_ref, dgk_ref, scr_ref = refs[2 + 3 * nb:]

        @pl.when(pl.program_id(0) == 0)
        def _():
            dgq_ref[...] = jnp.zeros_like(dgq_ref)
            dgk_ref[...] = jnp.zeros_like(dgk_ref)

        def total(d_refs):
            return sum(_from_subseq(r_[...], scr_ref, dil) for r_, dil in zip(d_refs, DILATIONS))

        def back(x, d_refs, g, dg_ref):
            dout = total(d_refs)
            dy = dout * cos_ref[...] + _swap_halves(dout * sin_ref[...])
            r = lax.rsqrt(_head_sum(x * x, seg_ref[...]) * (1.0 / HEAD_DIM) + EPS)
            xn = x * r
            dg_ref[...] += _colsum(dy * xn)
            dxn = dy * g
            return r * (dxn - xn * (_head_sum(dxn * xn, seg_ref[...]) * (1.0 / HEAD_DIM)))

        dz_ref[:, 0:B_WIDTH] = back(q_ref[...].astype(F32), dq_refs, gq_ref[...], dgq_ref).astype(BF16)
        dz_ref[:, B_WIDTH:2 * B_WIDTH] = back(k_ref[...].astype(F32), dk_refs, gk_ref[...], dgk_ref).astype(BF16)
        dz_ref[:, 2 * B_WIDTH:3 * B_WIDTH] = total(dv_refs).astype(BF16)

    return pl.pallas_call(
        body, name=name, grid=(t // tm,),
        in_specs=[col(2), col(3)] + subs * 3 + [blk, blk, row, row, pl.BlockSpec((B_WIDTH, B_WIDTH), lambda i: (0, 0))],
        out_specs=[pl.BlockSpec((tm, 3 * B_WIDTH), lambda i: (i, 0)), row, row],
        out_shape=[jax.ShapeDtypeStruct((t, 3 * B_WIDTH), BF16), jax.ShapeDtypeStruct((1, B_WIDTH), F32),
                   jax.ShapeDtypeStruct((1, B_WIDTH), F32)],
        scratch_shapes=[_subseq_scratch(tm)], compiler_params=_params("arbitrary"),
    )(z, z, *dqs, *dks, *dvs, cos_t, sin_t, gq, gk, seg)


def _subseq_views(x, col, name):
    t = x.shape[0]
    tm = _tile(t, (SUBSEQ_TM,))

    def body(x_ref, *rest):
        out_refs, scr_ref = rest[:-1], rest[-1]
        val = x_ref[...].astype(F32)
        for o_ref, dil in zip(out_refs, DILATIONS):
            o_ref[...] = _to_subseq(val, scr_ref, dil).astype(o_ref.dtype)

    return pl.pallas_call(
        body, name=name, grid=(t // tm,), in_specs=[pl.BlockSpec((tm, B_WIDTH), lambda i: (i, col))],
        out_specs=[_subseq_spec(tm, dil) for dil in DILATIONS],
        out_shape=[jax.ShapeDtypeStruct(_subseq_shape(t, dil), x.dtype) for dil in DILATIONS],
        scratch_shapes=[_subseq_scratch(tm)], compiler_params=_params("parallel"),
    )(x)


def _attn_fwd(q, k, v, dil, name):
    t = q.shape[0] * dil
    nb = t // dil // Q_BLOCK
    cur = pl.BlockSpec((Q_BLOCK, B_WIDTH), lambda r, i: (i, r))
    prev = pl.BlockSpec((Q_BLOCK, B_WIDTH), lambda r, i: (jnp.maximum(i - 1, 0), r))

    def body(q_ref, kp_ref, kc_ref, vp_ref, vc_ref, o_ref, lse_ref):
        i = pl.program_id(1)
        q = q_ref[...]
        kk = jnp.concatenate([kp_ref[...], kc_ref[...]], axis=0)
        vv = jnp.concatenate([vp_ref[...], vc_ref[...]], axis=0)
        a = lax.broadcasted_iota(jnp.int32, (Q_BLOCK, 2 * Q_BLOCK), 0)
        j = lax.broadcasted_iota(jnp.int32, (Q_BLOCK, 2 * Q_BLOCK), 1)
        dist = a + Q_BLOCK - j
        mask = (dist >= 0) & (dist <= Q_BLOCK) & ((j >= Q_BLOCK) | (i > 0))
        for h in range(HEADS):
            sl = slice(h * HEAD_DIM, (h + 1) * HEAD_DIM)
            s = jnp.where(mask, _dot(q[:, sl], kk[:, sl], _NT) * (HEAD_DIM ** -0.5), NEG)
            m = jnp.max(s, axis=-1, keepdims=True)
            p = jnp.exp(s - m)
            den = jnp.sum(p, axis=-1, keepdims=True)
            o_ref[:, sl] = _dot(p.astype(BF16), vv[:, sl], _NN) / den
            lse_ref[:, sl] = jnp.broadcast_to(m + jnp.log(den), (Q_BLOCK, HEAD_DIM))

    return pl.pallas_call(
        body, name=name, grid=(dil, nb), in_specs=[cur, prev, cur, prev, cur], out_specs=[cur, cur],
        out_shape=[jax.ShapeDtypeStruct(_subseq_shape(t, dil), F32)] * 2,
        compiler_params=_params("parallel", "parallel"),
    )(q, k, k, v, v)


def _attn_merge(outs, lses, name):
    nb = len(DILATIONS)
    t = outs[0].shape[0] * DILATIONS[0]
    tm = _tile(t, (SUBSEQ_TM,))
    subs = [_subseq_spec(tm, dil) for dil in DILATIONS]

    def body(*refs):
        o_refs, l_refs = refs[:nb], refs[nb:2 * nb]
        yb_refs, lse_refs, scr_ref = refs[2 * nb:3 * nb], refs[3 * nb:4 * nb], refs[4 * nb]
        ls = [_from_subseq(r[...], scr_ref, dil) for r, dil in zip(l_refs, DILATIONS)]
        m = functools.reduce(jnp.maximum, ls)
        tot = m + jnp.log(sum(jnp.exp(l - m) for l in ls))
        yb = sum(jnp.exp(l - tot) * _from_subseq(o[...], scr_ref, dil) for l, o, dil in zip(ls, o_refs, DILATIONS))
        yb = yb.astype(BF16).astype(F32)
        for yb_ref, lse_ref, dil in zip(yb_refs, lse_refs, DILATIONS):
            yb_ref[...] = _to_subseq(yb, scr_ref, dil).astype(BF16)
            lse_ref[...] = _to_subseq(tot, scr_ref, dil)

    outs_ = pl.pallas_call(
        body, name=name, grid=(t // tm,), in_specs=subs * 2, out_specs=subs * 2,
        out_shape=[jax.ShapeDtypeStruct(_subseq_shape(t, dil), BF16) for dil in DILATIONS]
        + [jax.ShapeDtypeStruct(_subseq_shape(t, dil), F32) for dil in DILATIONS],
        scratch_shapes=[_subseq_scratch(tm)], compiler_params=_params("parallel"),
    )(*outs, *lses)
    return outs_[:nb], outs_[nb:]


def _attn_bwd(q, k, v, do, o, lse, dil, name):
    t = q.shape[0] * dil
    nb = t // dil // Q_BLOCK
    blk = lambda f: pl.BlockSpec((Q_BLOCK, B_WIDTH), lambda r, i: (f(i), r))
    cur = blk(lambda i: jnp.minimum(i, nb - 1))
    prev = blk(lambda i: jnp.clip(i - 1, 0, nb - 1))
    scale = HEAD_DIM ** -0.5

    def body(q_ref, kp_ref, kc_ref, vp_ref, vc_ref, do_ref, o_ref, lse_ref, dq_ref, dk_ref, dv_ref,
             ck_ref, cv_ref, tk_ref, tv_ref):
        i = pl.program_id(1)

        @pl.when(i == 0)
        def _():
            ck_ref[...] = jnp.zeros_like(ck_ref)
            cv_ref[...] = jnp.zeros_like(cv_ref)

        @pl.when(i < nb)
        def _():
            q = q_ref[...]
            kk = jnp.concatenate([kp_ref[...], kc_ref[...]], axis=0)
            vv = jnp.concatenate([vp_ref[...], vc_ref[...]], axis=0)
            do = do_ref[...]
            dof = do.astype(F32)
            of = o_ref[...].astype(F32)
            a = lax.broadcasted_iota(jnp.int32, (Q_BLOCK, 2 * Q_BLOCK), 0)
            j = lax.broadcasted_iota(jnp.int32, (Q_BLOCK, 2 * Q_BLOCK), 1)
            dist = a + Q_BLOCK - j
            mask = (dist >= 0) & (dist <= Q_BLOCK) & ((j >= Q_BLOCK) | (i > 0))
            for h in range(HEADS):
                sl = slice(h * HEAD_DIM, (h + 1) * HEAD_DIM)
                s = jnp.where(mask, _dot(q[:, sl], kk[:, sl], _NT) * scale, NEG)
                p = jnp.exp(s - lse_ref[:, h * HEAD_DIM:h * HEAD_DIM + 1])
                dp = _dot(do[:, sl], vv[:, sl], _NT)
                delta = jnp.sum(dof[:, sl] * of[:, sl], axis=-1, keepdims=True)
                ds = (p * (dp - delta) * scale).astype(BF16)
                dq_ref[:, sl] = _dot(ds, kk[:, sl], _NN)
                dv_t = _dot(do[:, sl], p.astype(BF16), _TN)
                dk_t = _dot(q[:, sl], ds, _TN)
                tk_ref[sl, :] = ck_ref[sl, :] + dk_t[:, :Q_BLOCK]
                tv_ref[sl, :] = cv_ref[sl, :] + dv_t[:, :Q_BLOCK]
                ck_ref[sl, :] = dk_t[:, Q_BLOCK:]
                cv_ref[sl, :] = dv_t[:, Q_BLOCK:]

        @pl.when(i == nb)
        def _():
            tk_ref[...] = ck_ref[...]
            tv_ref[...] = cv_ref[...]

        @pl.when(i >= 1)
        def _():
            dk_ref[...] = tk_ref[...].T
            dv_ref[...] = tv_ref[...].T

    return pl.pallas_call(
        body, name=name, grid=(dil, nb + 1), in_specs=[cur, prev, cur, prev, cur, cur, cur, cur],
        out_specs=[cur, prev, prev], out_shape=[jax.ShapeDtypeStruct(_subseq_shape(t, dil), F32)] * 3,
        scratch_shapes=[pltpu.VMEM((B_WIDTH, Q_BLOCK), F32)] * 4,
        compiler_params=_params("parallel", "arbitrary"),
    )(q, k, k, v, v, do, o, lse)


FFN_TN = 256
FFN_FWD_CHUNK = 256
FFN_BWD_CHUNK = 128


def _ffn_up(h, up_t, name):
    t, k = h.shape
    tm = _tile(t)

    def body(h_ref, w_ref, o_ref):
        o_ref[...] = _dot(h_ref[...], w_ref[...], _NT).astype(BF16)

    return pl.pallas_call(
        body, name=name, grid=(2, t // tm),
        in_specs=[pl.BlockSpec((tm, k), lambda p, i: (i, 0)), pl.BlockSpec((None, FFN_DIM, k), lambda p, i: (p, 0, 0))],
        out_specs=pl.BlockSpec((None, tm, FFN_DIM), lambda p, i: (p, i, 0)),
        out_shape=jax.ShapeDtypeStruct((2, t, FFN_DIM), BF16), compiler_params=_params("parallel", "parallel"),
    )(h, up_t.reshape(2, FFN_DIM, k))


def _ffn_up_dx(du, up_t, name):
    t = du.shape[1]
    k = up_t.shape[1]
    tm = _tile(t)

    def body(a_ref, b_ref, o_ref):
        o_ref[...] = _dot(a_ref[0], b_ref[0], _NN) + _dot(a_ref[1], b_ref[1], _NN)

    return pl.pallas_call(
        body, name=name, grid=(t // tm,),
        in_specs=[pl.BlockSpec((2, tm, FFN_DIM), lambda i: (0, i, 0)), pl.BlockSpec((2, FFN_DIM, k), lambda i: (0, 0, 0))],
        out_specs=pl.BlockSpec((tm, k), lambda i: (i, 0)), out_shape=jax.ShapeDtypeStruct((t, k), F32),
        compiler_params=_params("parallel"),
    )(du, up_t.reshape(2, FFN_DIM, k))


def _ffn_conv(win, w_ref, b_ref, p):
    x = win.astype(F32)
    x0, x1, x2 = x[FFN_HALO:], pltpu.roll(x, 1, 0)[FFN_HALO:], pltpu.roll(x, 2, 0)[FFN_HALO:]
    return x0, b_ref[p] + w_ref[p, 2:3, :] * x0 + w_ref[p, 1:2, :] * x1 + w_ref[p, 0:1, :] * x2


def _zero_if(cond, v):
    return jnp.where(cond, 0, v).astype(v.dtype)


def _ffn_act(u, dw_w, dw_b, name):
    t = u.shape[1]
    tm = _tile(t)
    chunk = min(FFN_FWD_CHUNK, tm)
    hb = tm // FFN_HALO
    main = pl.BlockSpec((2, tm, FFN_TN), lambda i, j: (0, i, j))
    halo = pl.BlockSpec((2, FFN_HALO, FFN_TN), lambda i, j: (0, jnp.maximum(i * hb - 1, 0), j))
    wsp = pl.BlockSpec((2, FFN_CONV_WIDTH, FFN_TN), lambda i, j: (0, 0, j))
    bsp = pl.BlockSpec((2, 1, FFN_TN), lambda i, j: (0, 0, j))

    def body(u_ref, uh_ref, w_ref, b_ref, o_ref):
        first = pl.program_id(0) == 0

        def emit(rows, wins):
            za, zb = _ffn_conv(wins[0], w_ref, b_ref, 0)[1], _ffn_conv(wins[1], w_ref, b_ref, 1)[1]
            o_ref[rows, :] = (za * _sigmoid(za) * zb).astype(BF16)

        emit(pl.ds(0, chunk), [jnp.concatenate([_zero_if(first, uh_ref[p]), u_ref[p, 0:chunk, :]], axis=0) for p in range(2)])

        def step(c, carry):
            s = pl.multiple_of(c * chunk, chunk)
            emit(pl.ds(s, chunk), [u_ref[p, pl.ds(s - FFN_HALO, chunk + FFN_HALO), :] for p in range(2)])
            return carry

        lax.fori_loop(1, tm // chunk, step, 0)

    return pl.pallas_call(
        body, name=name, grid=(t // tm, FFN_DIM // FFN_TN), in_specs=[main, halo, wsp, bsp],
        out_specs=pl.BlockSpec((tm, FFN_TN), lambda i, j: (i, j)), out_shape=jax.ShapeDtypeStruct((t, FFN_DIM), BF16),
        compiler_params=_params("parallel", "parallel"),
    )(u, u, dw_w, dw_b)


def _fold8(v):
    return jnp.sum(v.reshape(v.shape[0] // 8, 8, v.shape[1]), axis=0)


def _ffn_act_bwd(u, dact, dw_w, dw_b, name):
    t = u.shape[1]
    tm = _tile(t)
    chunk = min(FFN_BWD_CHUNK, tm // 2)
    halo = FFN_HALO
    hb = tm // halo
    nt = t // tm
    last_halo = t // halo - 1
    prev_i = lambda i: jnp.maximum(i * hb - 1, 0)
    next_i = lambda i: jnp.minimum((i + 1) * hb, last_halo)
    main = pl.BlockSpec((2, tm, FFN_TN), lambda j, i: (0, i, j))
    prev = pl.BlockSpec((2, halo, FFN_TN), lambda j, i: (0, prev_i(i), j))
    nxt = pl.BlockSpec((2, halo, FFN_TN), lambda j, i: (0, next_i(i), j))
    wsp = pl.BlockSpec((2, FFN_CONV_WIDTH, FFN_TN), lambda j, i: (0, 0, j))
    bsp = pl.BlockSpec((2, 1, FFN_TN), lambda j, i: (0, 0, j))

    def body(u_ref, up_ref, un_ref, da_ref, dan_ref, w_ref, b_ref, du_ref, dw_ref, db_ref, acc_ref):
        i = pl.program_id(1)
        first, last = i == 0, i == nt - 1
        acc_ref[...] = jnp.zeros_like(acc_ref)

        def emit(rows, wins, dact):
            n = chunk + halo
            (ua, za), (ub, zb) = _ffn_conv(wins[0], w_ref, b_ref, 0), _ffn_conv(wins[1], w_ref, b_ref, 1)
            dact = dact.astype(F32)
            sg = _sigmoid(za)
            dzs = (dact * zb * (sg * (1.0 + za * (1.0 - sg))), dact * (za * sg))
            for p, (dz, um) in enumerate(zip(dzs, (ua, ub))):
                ahead = (dz[:chunk], pltpu.roll(dz, n - 1, 0)[:chunk], pltpu.roll(dz, n - 2, 0)[:chunk])
                um = um[:chunk]
                acc_ref[p, FFN_CONV_WIDTH] += _fold8(ahead[0])
                du = None
                for j, dzj in enumerate(ahead):
                    k = FFN_CONV_WIDTH - 1 - j
                    acc_ref[p, k] += _fold8(dzj * um)
                    term = w_ref[p, k:k + 1, :] * dzj
                    du = term if du is None else du + term
                du_ref[p, rows, :] = du.astype(BF16)

        emit(pl.ds(0, chunk),
             [jnp.concatenate([_zero_if(first, up_ref[p]), u_ref[p, 0:chunk + halo, :]], axis=0) for p in range(2)],
             da_ref[0:chunk + halo, :])

        def step(c, carry):
            s = pl.multiple_of(c * chunk, chunk)
            emit(pl.ds(s, chunk), [u_ref[p, pl.ds(s - halo, chunk + 2 * halo), :] for p in range(2)],
                 da_ref[pl.ds(s, chunk + halo), :])
            return carry

        lax.fori_loop(1, tm // chunk - 1, step, 0)
        s = tm - chunk
        emit(pl.ds(s, chunk),
             [jnp.concatenate([u_ref[p, s - halo:tm, :], _zero_if(last, un_ref[p])], axis=0) for p in range(2)],
             jnp.concatenate([da_ref[s:tm, :], _zero_if(last, dan_ref[...])], axis=0))

        @pl.when(i == 0)
        def _():
            dw_ref[...] = jnp.zeros_like(dw_ref)
            db_ref[...] = jnp.zeros_like(db_ref)

        for p in range(2):
            for k in range(FFN_CONV_WIDTH):
                dw_ref[p, k:k + 1, :] += _colsum(acc_ref[p, k])
            db_ref[p] += _colsum(acc_ref[p, FFN_CONV_WIDTH])

    return pl.pallas_call(
        body, name=name, grid=(FFN_DIM // FFN_TN, nt),
        in_specs=[main, prev, nxt, pl.BlockSpec((tm, FFN_TN), lambda j, i: (i, j)),
                  pl.BlockSpec((halo, FFN_TN), lambda j, i: (next_i(i), j)), wsp, bsp],
        out_specs=[main, wsp, bsp],
        out_shape=[jax.ShapeDtypeStruct((2, t, FFN_DIM), BF16), jax.ShapeDtypeStruct((2, FFN_CONV_WIDTH, FFN_DIM), F32),
                   jax.ShapeDtypeStruct((2, 1, FFN_DIM), F32)],
        scratch_shapes=[pltpu.VMEM((2, FFN_CONV_WIDTH + 1, 8, FFN_TN), F32)],
        compiler_params=_params("parallel", "arbitrary"),
    )(u, u, u, dact, dact, dw_w, dw_b)


CONV_TM = 256
CONV_ROWS = 128
CONV_LANES = 128


def _glu_window(pa_ref, pah_ref, pg_ref, pgh_ref, scr_ref, first):
    ah, gh = pah_ref[...].astype(F32), pgh_ref[...].astype(F32)
    scr_ref[0:CONV_HALO, :] = jnp.where(first, 0.0, ah * _sigmoid(gh))
    scr_ref[CONV_HALO:, :] = pa_ref[...].astype(F32) * _sigmoid(pg_ref[...].astype(F32))


def _tap_slabs(win, rows, ahead):
    n = win.shape[0]
    for s in range(8):
        ws = win if s == 0 else pltpu.roll(win, n - s if ahead else s, 0)
        for q in range(CONV_HALO // 8):
            o = 8 * q + s
            if o < CONV_WIDTH:
                start = 8 * q if ahead else CONV_HALO - 8 * q
                yield CONV_WIDTH - 1 - o, ws[start:start + rows]


def _conformer_specs(t):
    tm = _tile(t, (CONV_TM, 128))
    hb = tm // CONV_HALO
    d = D_MODEL
    main = lambda c: pl.BlockSpec((tm, d), lambda i: (i, c))
    halo = lambda c: pl.BlockSpec((CONV_HALO, d), lambda i: (jnp.maximum(i * hb - 1, 0), c))
    row = pl.BlockSpec((1, d), lambda i: (0, 0))
    wsp = pl.BlockSpec((CONV_WIDTH, d), lambda i: (0, 0))
    return tm, main, halo, row, wsp


def _conformer_mid(p, dw_w, dw_b, ln_g, ln_b, name):
    t = p.shape[0]
    tm, main, halo, row, wsp = _conformer_specs(t)
    d, lanes = D_MODEL, CONV_LANES

    def body(pa_ref, pah_ref, pg_ref, pgh_ref, w_ref, b_ref, g_ref, lb_ref, o_ref, dc_ref, scr_ref):
        _glu_window(pa_ref, pah_ref, pg_ref, pgh_ref, scr_ref, pl.program_id(0) == 0)
        for c in range(d // lanes):
            ls = slice(c * lanes, (c + 1) * lanes)
            acc = jnp.broadcast_to(b_ref[:, ls], (tm, lanes))
            for k, slab in _tap_slabs(scr_ref[:, ls], tm, False):
                acc = acc + w_ref[k:k + 1, ls] * slab
            dc_ref[:, ls] = acc

        def norm(r, carry):
            r0 = pl.multiple_of(r * 32, 32)
            dc = dc_ref[pl.ds(r0, 32), :]
            xc = dc - jnp.mean(dc, axis=-1, keepdims=True)
            ln = xc * lax.rsqrt(jnp.mean(xc * xc, axis=-1, keepdims=True) + EPS) * g_ref[...] + lb_ref[...]
            o_ref[pl.ds(r0, 32), :] = (ln * _sigmoid(ln)).astype(BF16)
            return carry

        lax.fori_loop(0, tm // 32, norm, 0)

    return pl.pallas_call(
        body, name=name, grid=(t // tm,), in_specs=[main(0), halo(0), main(1), halo(1), wsp, row, row, row],
        out_specs=[main(0), main(0)], out_shape=[jax.ShapeDtypeStruct((t, d), BF16), jax.ShapeDtypeStruct((t, d), F32)],
        scratch_shapes=[pltpu.VMEM((tm + CONV_HALO, d), F32)], compiler_params=_params("parallel"),
    )(p, p, p, p, dw_w, dw_b, ln_g, ln_b)


def _conformer_mid_bwd(p, dc, ds, ln_g, ln_b, name):
    t = p.shape[0]
    tm, main, halo, row, wsp = _conformer_specs(t)
    d, nt = D_MODEL, t // tm
    rows, lanes = CONV_ROWS, CONV_LANES

    def body(pa_ref, pah_ref, pg_ref, pgh_ref, dc_ref, ds_ref, g_ref, lb_ref,
             ddc_ref, dw_ref, db_ref, dg_ref, dlb_ref, scr_ref, wacc_ref, racc_ref):
        i = pl.program_id(0)

        @pl.when(i == 0)
        def _():
            wacc_ref[...] = jnp.zeros_like(wacc_ref)
            racc_ref[...] = jnp.zeros_like(racc_ref)

        _glu_window(pa_ref, pah_ref, pg_ref, pgh_ref, scr_ref, i == 0)

        def norm_bwd(r, carry):
            r0 = pl.multiple_of(r * 32, 32)
            dcv = dc_ref[pl.ds(r0, 32), :]
            xc = dcv - jnp.mean(dcv, axis=-1, keepdims=True)
            rstd = lax.rsqrt(jnp.mean(xc * xc, axis=-1, keepdims=True) + EPS)
            xhat = xc * rstd
            ln = xhat * g_ref[...] + lb_ref[...]
            sg = _sigmoid(ln)
            dln = ds_ref[pl.ds(r0, 32), :].astype(F32) * (sg * (1.0 + ln * (1.0 - sg)))
            dxh = dln * g_ref[...]
            ddc = rstd * (dxh - jnp.mean(dxh, axis=-1, keepdims=True) - xhat * jnp.mean(dxh * xhat, axis=-1, keepdims=True))
            ddc_ref[pl.ds(r0, 32), :] = ddc
            racc_ref[0] += _fold8(dln * xhat)
            racc_ref[1] += _fold8(dln)
            racc_ref[2] += _fold8(ddc)
            return carry

        lax.fori_loop(0, tm // 32, norm_bwd, 0)

        for c in range(d // lanes):
            ls = slice(c * lanes, (c + 1) * lanes)

            def taps(r, carry, ls=ls):
                r0 = pl.multiple_of(r * rows, rows)
                ddc = ddc_ref[pl.ds(r0, rows), ls]
                for k, slab in _tap_slabs(scr_ref[pl.ds(r0, rows + CONV_HALO), ls], rows, False):
                    wacc_ref[k, :, ls] += _fold8(ddc * slab)
                return carry

            lax.fori_loop(0, tm // rows, taps, 0)

        @pl.when(i == nt - 1)
        def _():
            for k in range(CONV_WIDTH):
                dw_ref[k:k + 1, :] = _colsum(wacc_ref[k])
            dg_ref[...] = _colsum(racc_ref[0])
            dlb_ref[...] = _colsum(racc_ref[1])
            db_ref[...] = _colsum(racc_ref[2])

    return pl.pallas_call(
        body, name=name, grid=(nt,), in_specs=[main(0), halo(0), main(1), halo(1), main(0), main(0), row, row],
        out_specs=[main(0), wsp, row, row, row],
        out_shape=[jax.ShapeDtypeStruct((t, d), F32), jax.ShapeDtypeStruct((CONV_WIDTH, d), F32)]
        + [jax.ShapeDtypeStruct((1, d), F32)] * 3,
        scratch_shapes=[pltpu.VMEM((tm + CONV_HALO, d), F32), pltpu.VMEM((CONV_WIDTH, 8, d), F32), pltpu.VMEM((3, 8, d), F32)],
        compiler_params=_params("arbitrary"),
    )(p, p, p, p, dc, ds, ln_g, ln_b)


def _conformer_glu_bwd(p, ddc, dw_w, name):
    t = p.shape[0]
    d = D_MODEL
    tm = _tile(t, (CONV_TM, 128))
    hb = tm // CONV_HALO
    nt = t // tm
    last_halo = t // CONV_HALO - 1
    rows, lanes = CONV_ROWS, CONV_LANES
    col = lambda c: pl.BlockSpec((tm, d), lambda i: (i, c))
    nxt = pl.BlockSpec((CONV_HALO, d), lambda i: (jnp.minimum((i + 1) * hb, last_halo), 0))

    def body(pa_ref, pg_ref, ddc_ref, ddcn_ref, w_ref, dp_ref, db_ref, scr_ref, acc_ref):
        i = pl.program_id(0)

        @pl.when(i == 0)
        def _():
            acc_ref[...] = jnp.zeros_like(acc_ref)

        scr_ref[0:tm, :] = ddc_ref[...]
        scr_ref[tm:, :] = _zero_if(i == nt - 1, ddcn_ref[...])
        for c in range(d // lanes):
            ls = slice(c * lanes, (c + 1) * lanes)
            gs = slice(d + c * lanes, d + (c + 1) * lanes)

            def taps(r, carry, ls=ls, gs=gs):
                r0 = pl.multiple_of(r * rows, rows)
                dglu = None
                for k, slab in _tap_slabs(scr_ref[pl.ds(r0, rows + CONV_HALO), ls], rows, True):
                    term = w_ref[k:k + 1, ls] * slab
                    dglu = term if dglu is None else dglu + term
                a = pa_ref[pl.ds(r0, rows), ls].astype(F32)
                sg = _sigmoid(pg_ref[pl.ds(r0, rows), ls].astype(F32))
                da = (dglu * sg).astype(BF16)
                dg = (dglu * a * sg * (1.0 - sg)).astype(BF16)
                dp_ref[pl.ds(r0, rows), ls] = da
                dp_ref[pl.ds(r0, rows), gs] = dg
                acc_ref[:, ls] += _fold8(da.astype(F32))
                acc_ref[:, gs] += _fold8(dg.astype(F32))
                return carry

            lax.fori_loop(0, tm // rows, taps, 0)

        @pl.when(i == nt - 1)
        def _():
            db_ref[...] = _colsum(acc_ref[...])

    return pl.pallas_call(
        body, name=name, grid=(nt,),
        in_specs=[col(0), col(1), col(0), nxt, pl.BlockSpec((CONV_WIDTH, d), lambda i: (0, 0))],
        out_specs=[pl.BlockSpec((tm, 2 * d), lambda i: (i, 0)), pl.BlockSpec((1, 2 * d), lambda i: (0, 0))],
        out_shape=[jax.ShapeDtypeStruct((t, 2 * d), BF16), jax.ShapeDtypeStruct((1, 2 * d), F32)],
        scratch_shapes=[pltpu.VMEM((tm + CONV_HALO, d), F32), pltpu.VMEM((8, 2 * d), F32)],
        compiler_params=_params("arbitrary"),
    )(p, p, ddc, ddc, dw_w)


def _colsum_call(a, name):
    t, n = a.shape
    tm = _tile(t)

    def body(a_ref, o_ref):
        @pl.when(pl.program_id(0) == 0)
        def _():
            o_ref[...] = jnp.zeros_like(o_ref)

        o_ref[...] += _colsum(a_ref[...].astype(F32))

    return pl.pallas_call(
        body, name=name, grid=(t // tm,), in_specs=[pl.BlockSpec((tm, n), lambda i: (i, 0))],
        out_specs=pl.BlockSpec((1, n), lambda i: (0, 0)), out_shape=jax.ShapeDtypeStruct((1, n), F32),
        compiler_params=_params("arbitrary"),
    )(a)


def _ada_fwd(c_all, w, name):
    rows, d = c_all.shape
    n = w.shape[1]
    tn = _tile(n, (256, 128))

    def body(c_ref, w_ref, o_ref):
        c = c_ref[...]
        o_ref[...] = _dot((c * _sigmoid(c)).astype(BF16), w_ref[...].astype(BF16), _NN)

    return pl.pallas_call(
        body, name=name, grid=(n // tn,),
        in_specs=[pl.BlockSpec((rows, d), lambda j: (0, 0)), pl.BlockSpec((d, tn), lambda j: (0, j))],
        out_specs=pl.BlockSpec((rows, tn), lambda j: (0, j)), out_shape=jax.ShapeDtypeStruct((rows, n), F32),
        compiler_params=_params("parallel"),
    )(c_all, w)


def _ada_bwd(c_all, dmod, name):
    rows, d = c_all.shape
    n = dmod.shape[1]
    tn = _tile(n, (256, 128))

    def body(c_ref, g_ref, o_ref):
        c = c_ref[...]
        o_ref[...] = _dot((c * _sigmoid(c)).astype(BF16), g_ref[...].astype(BF16), _TN)

    return pl.pallas_call(
        body, name=name, grid=(n // tn,),
        in_specs=[pl.BlockSpec((rows, d), lambda j: (0, 0)), pl.BlockSpec((rows, tn), lambda j: (0, j))],
        out_specs=pl.BlockSpec((d, tn), lambda j: (0, j)), out_shape=jax.ShapeDtypeStruct((d, n), F32),
        compiler_params=_params("parallel"),
    )(c_all, dmod)


def _sum_slots(a, name):
    s, r, c = a.shape
    tr = _row_tile(r, 256)

    def body(a_ref, o_ref):
        acc = a_ref[0].astype(F32)
        for k in range(1, s):
            acc = acc + a_ref[k].astype(F32)
        o_ref[...] = acc

    return pl.pallas_call(
        body, name=name, grid=(r // tr,), in_specs=[pl.BlockSpec((s, tr, c), lambda i: (0, i, 0))],
        out_specs=pl.BlockSpec((tr, c), lambda i: (i, 0)), out_shape=jax.ShapeDtypeStruct((r, c), F32),
        compiler_params=_params("parallel"),
    )(a)


def _sum_with_own(blocks, land, me, name):
    s, r, c = land.shape
    tr = _row_tile(r, 256)
    slot = lambda k: pl.BlockSpec((None, tr, c), lambda i, me_ref: ((me_ref[0] + k) % s, i, 0))

    def body(me_ref, own_ref, *refs):
        o_ref = refs[-1]
        acc = own_ref[...].astype(F32)
        for ref in refs[:-1]:
            acc = acc + ref[...].astype(F32)
        o_ref[...] = acc

    return pl.pallas_call(
        body, name=name, out_shape=jax.ShapeDtypeStruct((r, c), F32),
        grid_spec=pltpu.PrefetchScalarGridSpec(
            num_scalar_prefetch=1, grid=(r // tr,), in_specs=[slot(0)] + [slot(k) for k in range(1, s)],
            out_specs=pl.BlockSpec((tr, c), lambda i, me_ref: (i, 0))),
        compiler_params=_params("parallel"),
    )(me, blocks, *[land] * (s - 1))


def _adamw_update(w, g, m, v):
    nm = ADAM_B1 * m + (1.0 - ADAM_B1) * g
    nv = ADAM_B2 * v + (1.0 - ADAM_B2) * (g * g)
    m_hat = nm * (1.0 / (1.0 - ADAM_B1 ** ADAM_STEP))
    v_hat = nv * (1.0 / (1.0 - ADAM_B2 ** ADAM_STEP))
    return -ADAM_LR * (m_hat / (jnp.sqrt(v_hat) + ADAM_EPS) + ADAM_WD * w), nm, nv


def _adamw(w, g, m, v, name):
    l, r, c = w.shape
    tr = _row_tile(r, 256)
    blk = pl.BlockSpec((None, tr, c), lambda k, i: (k, i, 0))

    def body(w_ref, g_ref, m_ref, v_ref, d_ref, nm_ref, nv_ref):
        d_ref[...], nm_ref[...], nv_ref[...] = _adamw_update(w_ref[...], g_ref[...], m_ref[...], v_ref[...])

    return pl.pallas_call(
        body, name=name, grid=(l, r // tr), in_specs=[blk] * 4, out_specs=[blk] * 3,
        out_shape=[jax.ShapeDtypeStruct(w.shape, F32)] * 3, compiler_params=_params("parallel", "parallel"),
    )(w, g, m, v)


def _adamw_small(ws, gs, ms, vs, name):
    n = len(ws)
    two_d = lambda a: a.reshape(-1, a.shape[-1])

    def body(*refs):
        ins, outs = refs[:4 * n], refs[4 * n:]
        for a in range(n):
            outs[a][...], outs[n + a][...], outs[2 * n + a][...] = _adamw_update(*[ins[k * n + a][...] for k in range(4)])

    res = pl.pallas_call(
        body, name=name, out_shape=[jax.ShapeDtypeStruct(two_d(w).shape, F32) for w in ws] * 3,
    )(*[two_d(a) for a in (*ws, *gs, *ms, *vs)])
    return [[res[k * n + a].reshape(ws[a].shape) for a in range(n)] for k in range(3)]


def _mesh_pos():
    return lax.axis_index("x"), lax.axis_index("y"), lax.axis_index("c")


def _all_gather_vmem(x_shard, name):
    m_per, n = x_shard.shape

    def body(x_ref, out_ref, send_sems, recv_sems, local_sem):
        x, y, c = _mesh_pos()
        me, sibling = (x, y, c), (x, y, 1 - c)
        chips = [(1 - x, y), (x, 1 - y), (1 - x, 1 - y)]

        def rows(px, py, pc):
            return out_ref.at[pl.ds((4 * px + 2 * py + pc) * m_per, m_per), :]

        def copy(k, block, to, src=None):
            return pltpu.make_async_remote_copy(
                src_ref=rows(*block) if src is None else src, dst_ref=rows(*block),
                send_sem=send_sems.at[k], recv_sem=recv_sems.at[k], device_id=to, device_id_type=MESH)

        mine = pltpu.make_async_copy(x_ref, rows(*me), local_sem)
        mine.start()
        first = [copy(0, me, sibling, src=x_ref)]
        first += [copy(1 + j, me, (*chip, c), src=x_ref) for j, chip in enumerate(chips)]
        for cp in first:
            cp.start()
        passed = [copy(4 + j, (*chip, c), sibling) for j, chip in enumerate(chips)]
        for j, chip in enumerate(chips):
            copy(1 + j, (*chip, c), me).wait_recv()
            passed[j].start()
        copy(0, sibling, me).wait_recv()
        for j, chip in enumerate(chips):
            copy(4 + j, (*chip, 1 - c), me).wait_recv()
        for cp in first + passed:
            cp.wait_send()
        mine.wait()

    return pl.pallas_call(
        body, name=name, out_shape=jax.ShapeDtypeStruct((N_DEV * m_per, n), x_shard.dtype),
        in_specs=[pl.BlockSpec(memory_space=pltpu.VMEM)], out_specs=pl.BlockSpec(memory_space=pltpu.VMEM),
        scratch_shapes=[pltpu.SemaphoreType.DMA((7,)), pltpu.SemaphoreType.DMA((7,)), pltpu.SemaphoreType.DMA],
    )(x_shard)


def _all_gather_hbm(shards, name):
    n = len(shards)
    out_shape = [jax.ShapeDtypeStruct((N_DEV,) + s.shape, s.dtype) for s in shards]

    def body(*refs):
        x_refs, out_refs = refs[:n], refs[n:2 * n]
        send_sems, recv_sems, local_sems = refs[2 * n:]
        x, y, c = _mesh_pos()
        me, sibling = (x, y, c), (x, y, 1 - c)
        chips = [(1 - x, y), (x, 1 - y), (1 - x, 1 - y)]

        def blk(a, p):
            return out_refs[a].at[4 * p[0] + 2 * p[1] + p[2]]

        def copy(a, k, block, to, src=None):
            return pltpu.make_async_remote_copy(
                src_ref=blk(a, block) if src is None else src, dst_ref=blk(a, block),
                send_sem=send_sems.at[7 * a + k], recv_sem=recv_sems.at[7 * a + k], device_id=to, device_id_type=MESH)

        mine = [pltpu.make_async_copy(x_refs[a], blk(a, me), local_sems.at[a]) for a in range(n)]
        for cp in mine:
            cp.start()
        first = []
        for a in range(n):
            first.append(copy(a, 0, me, sibling, src=x_refs[a]))
            first += [copy(a, 1 + j, me, (*chip, c), src=x_refs[a]) for j, chip in enumerate(chips)]
        for cp in first:
            cp.start()
        passed = []
        for j, chip in enumerate(chips):
            for a in range(n):
                copy(a, 1 + j, (*chip, c), me).wait_recv()
                fwd = copy(a, 4 + j, (*chip, c), sibling)
                fwd.start()
                passed.append(fwd)
        for a in range(n):
            copy(a, 0, sibling, me).wait_recv()
            for j, chip in enumerate(chips):
                copy(a, 4 + j, (*chip, 1 - c), me).wait_recv()
        for cp in first + passed:
            cp.wait_send()
        for cp in mine:
            cp.wait()

    return pl.pallas_call(
        body, name=name, out_shape=out_shape, in_specs=[pl.BlockSpec(memory_space=pltpu.VMEM)] * n,
        out_specs=[pl.BlockSpec(memory_space=pl.ANY)] * n,
        scratch_shapes=[pltpu.SemaphoreType.DMA((7 * n,)), pltpu.SemaphoreType.DMA((7 * n,)), pltpu.SemaphoreType.DMA((n,))],
    )(*shards)


def _peers(x, y, c):
    flip = lambda v, f: 1 - v if f else v
    return [(flip(x, m & 4), flip(y, m & 2), flip(c, m & 1)) for m in range(1, N_DEV)]


def _dev_index(p):
    return 4 * p[0] + 2 * p[1] + p[2]


def _push_copies(src_refs, land_refs, send_sems, recv_sems, scatter, receive):
    x, y, c = _mesh_pos()
    me = _dev_index((x, y, c))
    copies = []
    for a, (src, land) in enumerate(zip(src_refs, land_refs)):
        for k, p in enumerate(_peers(x, y, c)):
            copies.append(pltpu.make_async_remote_copy(
                src_ref=src.at[_dev_index(p)] if scatter else src, dst_ref=land.at[_dev_index(p) if receive else me],
                send_sem=send_sems.at[7 * a + k], recv_sem=recv_sems.at[7 * a + k], device_id=p, device_id_type=MESH))
    return copies


_HBM = pl.BlockSpec(memory_space=pltpu.HBM)
_SEM = pl.BlockSpec(memory_space=pltpu.SEMAPHORE)
_EFFECT = pltpu.SideEffectType.DATAFLOW_SIDE_EFFECTING


def _pushes_start(srcs, lands, scatter, name):
    n = len(srcs)

    def body(*refs):
        src_refs, land_refs = refs[:n], refs[n:2 * n]
        send_sems, recv_sems = refs[2 * n], refs[2 * n + 1]
        token = refs[-1]
        for cp in _push_copies(src_refs, land_refs, send_sems, recv_sems, scatter, receive=False):
            cp.start()
        token[...] = jnp.zeros_like(token)

    hbm = lambda a: pltpu.HBM(a.shape, a.dtype)
    sems = pltpu.SemaphoreType.DMA((7 * n,))
    outs = pl.pallas_call(
        body, name=name,
        out_shape=(sems, sems, *[hbm(a) for a in srcs], *[hbm(a) for a in lands], jax.ShapeDtypeStruct((8, 128), F32)),
        in_specs=[_HBM] * (2 * n), out_specs=(_SEM, _SEM, *[_HBM] * (2 * n), pl.BlockSpec(memory_space=pltpu.VMEM)),
        input_output_aliases={i: 2 + i for i in range(2 * n)},
        compiler_params=pltpu.CompilerParams(has_side_effects=_EFFECT),
    )(*[pltpu.with_memory_space_constraint(a, pltpu.HBM) for a in (*srcs, *lands)])
    return (outs[0], outs[1], outs[2:2 + n], outs[2 + n:2 + 2 * n], scatter), outs[-1]


def _pushes_wait(handle, after, name):
    send_sems, recv_sems, srcs, lands, scatter = handle
    n = len(srcs)

    def body(*refs):
        src_refs, land_refs = refs[:n], refs[n:2 * n]
        for cp in _push_copies(src_refs, land_refs, refs[2 * n], refs[2 * n + 1], scatter, receive=True):
            cp.wait_send()
            cp.wait_recv()

    hbm = lambda a: pltpu.HBM(a.shape, a.dtype)
    outs = pl.pallas_call(
        body, name=name, out_shape=tuple(hbm(a) for a in (*srcs, *lands)),
        in_specs=[_HBM] * (2 * n) + [_SEM, _SEM, pl.BlockSpec(memory_space=pl.ANY)], out_specs=tuple([_HBM] * (2 * n)),
        input_output_aliases={i: i for i in range(2 * n)},
        compiler_params=pltpu.CompilerParams(has_side_effects=_EFFECT),
    )(*srcs, *lands, send_sems, recv_sems, after)
    return outs[:n], outs[n:]


def _landing_zones(srcs, name):
    n = len(srcs)

    def body(*refs):
        src_refs, land_refs, bufs, sems = refs[:n], refs[n:2 * n], refs[2 * n:3 * n], refs[3 * n]
        me = _dev_index(_mesh_pos())
        load = [pltpu.make_async_copy(src, buf, sems.at[a]) for a, (src, buf) in enumerate(zip(src_refs, bufs))]
        store = [pltpu.make_async_copy(buf, land.at[me], sems.at[a]) for a, (buf, land) in enumerate(zip(bufs, land_refs))]
        for cp in load:
            cp.start()
        for ld, st in zip(load, store):
            ld.wait()
            st.start()
        for cp in store:
            cp.wait()

    any_spec = pl.BlockSpec(memory_space=pl.ANY)
    return pl.pallas_call(
        body, name=name, out_shape=[jax.ShapeDtypeStruct((N_DEV,) + s.shape, s.dtype) for s in srcs],
        in_specs=[any_spec] * n, out_specs=[any_spec] * n,
        scratch_shapes=[pltpu.VMEM(s.shape, s.dtype) for s in srcs] + [pltpu.SemaphoreType.DMA((n,))],
        compiler_params=pltpu.CompilerParams(vmem_limit_bytes=V7X_VMEM_LIMIT),
    )(*srcs)


def _ffn_forward(x, mod, norm_g, w, tag):
    sh, sc, gate = mod
    h = _modnorm(x, norm_g, sc, sh, f"{tag}_norm")
    u = _ffn_up(h, w["up_t"], f"{tag}_up")
    act = _ffn_act(u, w["dw_w"], w["dw_b"], f"{tag}_act")
    y, x_new = _matmul(act, w["down"], "nn", F32, f"{tag}_down", resid=(x, gate))
    return x_new, (x, h, u, act, y)


def _behind(row, token):
    return row if token is None else row + token[0:1, 0:1]


def _ffn_backward(dx_new, saved, mod, norm_g, w, tag, emit):
    x, h, u, act, y = saved
    _, sc, gate = mod
    dy, d_gate = _gate_bwd(dx_new, y, gate, f"{tag}_gate_bwd")
    d_down = _matmul_tn_acc(act, dy, f"{tag}_down_dw")
    dact = _matmul(dy, w["down"], "nt", BF16, f"{tag}_down_dx")
    du, d_dw_w, d_dw_b = _ffn_act_bwd(u, dact, w["dw_w"], w["dw_b"], f"{tag}_act_bwd")
    d_up_t = _matmul_tn_acc(du, h, f"{tag}_up_dw").reshape(2 * FFN_DIM, -1)
    token = emit([d_up_t, d_down])
    dh = _ffn_up_dx(du, w["up_t"], f"{tag}_up_dx")
    dx, d_w, d_sh = _modnorm_bwd(x, dh, norm_g, _behind(sc, token), dx_new, f"{tag}_norm_bwd")
    return dx, dict(dw_w=d_dw_w.transpose(1, 0, 2).reshape(FFN_CONV_WIDTH, 2 * FFN_DIM),
                    dw_b=d_dw_b.reshape(1, 2 * FFN_DIM), norm_g=d_w * (1.0 + sc), sh=d_sh, sc=d_w * norm_g, gate=d_gate)


def _mixer_forward(x, mod, norm_g, w, rope, tag):
    sh, sc, gate = mod
    h = _modnorm(x, norm_g, sc, sh, f"{tag}_norm")
    z = _matmul(h, w["w_in_t"], "nt", BF16, f"{tag}_in")
    ya = _gmlp_fwd(z, w["gain"], w["wtril"], w["bias_exp"], f"{tag}_gmlp")
    q, k, v = _qk_prep(z, rope[0], rope[1], w["gq"], w["gk"], w["seg"], f"{tag}_qk")
    outs, lses = zip(*[_attn_fwd(q[b], k[b], v[b], dil, f"{tag}_attn_d{dil}") for b, dil in enumerate(DILATIONS)])
    yb, lse = _attn_merge(outs, lses, f"{tag}_merge")
    cat = jnp.concatenate([ya, yb[0]], axis=1)
    y, x_new = _matmul(cat, w["w_out"], "nn", F32, f"{tag}_out", resid=(x, gate))
    return x_new, (x, h, z, q, k, v, yb, lse, cat, y)


def _mixer_backward(dx_new, saved, mod, norm_g, w, rope, tag, emit):
    x, h, z, q, k, v, yb, lse, cat, y = saved
    _, sc, gate = mod
    dy, d_gate = _gate_bwd(dx_new, y, gate, f"{tag}_gate_bwd")
    d_w_out = _matmul_tn_acc(cat, dy, f"{tag}_out_dw")
    dcat = _matmul(dy, w["w_out"], "nt", BF16, f"{tag}_out_dx")
    dz_a, d_sp_w, d_gain, d_bias_exp = _gmlp_bwd(z, dcat, w["gain"], w["wtril"], w["wtril_t"], w["bias_exp"], f"{tag}_gmlp_bwd")
    dyb = _subseq_views(dcat, A_WIDTH // B_WIDTH, f"{tag}_dyb_views")
    dqs, dks, dvs = zip(*[_attn_bwd(q[b], k[b], v[b], dyb[b], yb[b], lse[b], dil, f"{tag}_attn_bwd_d{dil}")
                          for b, dil in enumerate(DILATIONS)])
    dz_qkv, d_gq, d_gk = _qk_prep_bwd(z, dqs, dks, dvs, rope[0], rope[1], w["gq"], w["gk"], w["seg"], f"{tag}_qk_bwd")
    dz = jnp.concatenate([dz_a, dz_qkv], axis=1)
    d_w_in_t = _matmul_tn_acc(dz, h, f"{tag}_in_dw")
    token = emit([d_w_in_t, d_w_out])
    dh = _matmul(dz, w["w_in_t"], "nn", F32, f"{tag}_in_dx")
    dx, d_w, d_sh = _modnorm_bwd(x, dh, norm_g, _behind(sc, token), dx_new, f"{tag}_norm_bwd")
    return dx, dict(
        vnorm_g=d_gain.reshape(A_GROUPS, GROUP_DIM), spatial_w=d_sp_w,
        spatial_b=d_bias_exp.reshape(CHUNK, A_GROUPS, GROUP_DIM).sum(-1).T,
        q_norm_g=d_gq.reshape(HEADS, HEAD_DIM).sum(0), k_norm_g=d_gk.reshape(HEADS, HEAD_DIM).sum(0),
        norm_g=d_w * (1.0 + sc), sh=d_sh, sc=d_w * norm_g, gate=d_gate)


def _conformer_forward(x, mod, norm_g, w, tag):
    sh, sc, gate = mod
    h = _modnorm(x, norm_g, sc, sh, f"{tag}_norm")
    p = _matmul(h, w["pw1_t"], "nt", BF16, f"{tag}_pw1", bias=w["pw1_b"])
    s, dc = _conformer_mid(p, w["dw_w"], w["dw_b"], w["ln_g"], w["ln_b"], f"{tag}_mid")
    y, x_new = _matmul(s, w["pw2"], "nn", F32, f"{tag}_pw2", bias=w["pw2_b"], resid=(x, gate))
    return x_new, (x, h, p, dc, s, y)


def _conformer_backward(dx_new, saved, mod, norm_g, w, tag, emit):
    x, h, p, dc, s, y = saved
    _, sc, gate = mod
    dy, d_gate = _gate_bwd(dx_new, y, gate, f"{tag}_gate_bwd")
    d_pw2 = _matmul_tn_acc(s, dy, f"{tag}_pw2_dw")
    d_pw2_b = _colsum_call(dy, f"{tag}_pw2_db")
    ds = _matmul(dy, w["pw2"], "nt", BF16, f"{tag}_pw2_dx")
    ddc, d_dw_w, d_dw_b, d_ln_g, d_ln_b = _conformer_mid_bwd(p, dc, ds, w["ln_g"], w["ln_b"], f"{tag}_mid_bwd")
    dp, d_pw1_b = _conformer_glu_bwd(p, ddc, w["dw_w"], f"{tag}_glu_bwd")
    d_pw1_t = _matmul_tn_acc(dp, h, f"{tag}_pw1_dw")
    token = emit([d_pw1_t, d_pw2])
    dh = _matmul(dp, w["pw1_t"], "nn", F32, f"{tag}_pw1_dx")
    dx, d_w, d_sh = _modnorm_bwd(x, dh, norm_g, _behind(sc, token), dx_new, f"{tag}_norm_bwd")
    return dx, dict(pw1_b=d_pw1_b, dw_w=d_dw_w, dw_b=d_dw_b, ln_g=d_ln_g, ln_b=d_ln_b, pw2_b=d_pw2_b, norm_g=d_w * (1.0 + sc), sh=d_sh, sc=d_w * norm_g, gate=d_gate)


def _local_step(x, target, pos, mod, norm_mix_g, norm_ffn_g, mixer_w, conv_w, ffn_w, fetch, emit):
    d = D_MODEL
    inv_freq = 1.0 / (ROPE_THETA ** (jnp.arange(0, HEAD_DIM, 2, dtype=F32) / HEAD_DIM))
    inv_freq = jnp.tile(inv_freq, 2 * HEADS)[None, :]
    sign = jnp.tile(jnp.concatenate([-jnp.ones(HEAD_DIM // 2, F32), jnp.ones(HEAD_DIM // 2, F32)]), HEADS)[None, :]
    rope = _rope_tables(pos, inv_freq, sign, "rope_tables")
    mods = [[mod[l:l + 1, i * d:(i + 1) * d] for i in range(6)] for l in range(2)]
    mix = [(m[0], m[1], m[2]) for m in mods]
    ffn = [(m[3], m[4], m[5]) for m in mods]
    gm = [norm_mix_g[l:l + 1] for l in range(2)]
    gf = [norm_ffn_g[l:l + 1] for l in range(2)]

    mixer_w = {**mixer_w, **fetch("l0_mix", x)}
    x1, s_mix = _mixer_forward(x, mix[0], gm[0], mixer_w, rope, "l0_mix")
    ffn_w0 = {**ffn_w[0], **fetch("l0_ffn", x1)}
    x2, s_ffn0 = _ffn_forward(x1, ffn[0], gf[0], ffn_w0, "l0_ffn")
    conv_w = {**conv_w, **fetch("l1_conv", x2)}
    x3, s_conv = _conformer_forward(x2, mix[1], gm[1], conv_w, "l1_conv")
    ffn_w1 = {**ffn_w[1], **fetch("l1_ffn", x3)}
    x4, s_ffn1 = _ffn_forward(x3, ffn[1], gf[1], ffn_w1, "l1_ffn")
    dx, loss = _loss_head(x4, target, "loss_head")
    dx, g_ffn1 = _ffn_backward(dx, s_ffn1, ffn[1], gf[1], ffn_w1, "l1_ffn", functools.partial(emit, "l1_ffn"))
    dx, g_conv = _conformer_backward(dx, s_conv, mix[1], gm[1], conv_w, "l1_conv", functools.partial(emit, "l1_conv"))
    dx, g_ffn0 = _ffn_backward(dx, s_ffn0, ffn[0], gf[0], ffn_w0, "l0_ffn", functools.partial(emit, "l0_ffn"))
    dx, g_mix = _mixer_backward(dx, s_mix, mix[0], gm[0], mixer_w, rope, "l0_mix", functools.partial(emit, "l0_mix"))
    blocks = [g_mix, g_ffn0, g_conv, g_ffn1]
    dmod = jnp.stack([jnp.concatenate([a["sh"], a["sc"], a["gate"], b["sh"], b["sc"], b["gate"]], axis=1)[0]
                      for a, b in ((g_mix, g_ffn0), (g_conv, g_ffn1))])
    return loss, dx, dmod, blocks


def _pack(arrs, rows=8):
    flat = jnp.concatenate([a.reshape(-1).astype(F32) for a in arrs])
    n = flat.shape[0]
    cols = -(-n // (rows * 128)) * 128
    return jnp.pad(flat, (0, rows * cols - n)).reshape(rows, cols)


def _unpack(flat, shapes):
    out, off = [], 0
    for shp in shapes:
        n = math.prod(shp)
        out.append(flat[..., off:off + n].reshape(flat.shape[:-1] + tuple(shp)))
        off += n
    return out


def _take_block(a, idx, size, axis):
    return lax.dynamic_slice_in_dim(a, idx * size, size, axis)


def kernel(x, c, positions, ada_w, ada_b, norm_mix_g, norm_ffn_g, ab_w_in, a_vnorm_g, a_spatial_w, a_spatial_b, b_q_norm_g, b_k_norm_g, ab_w_out, conv_pw1_w, conv_pw1_b, conv_dw_w, conv_dw_b, conv_ln_g, conv_ln_b, conv_pw2_w, conv_pw2_b, ffn_up_w, ffn_dw_w, ffn_dw_b, ffn_down_w, loss_target, m_ada_w, m_ada_b, m_norm_mix_g, m_norm_ffn_g, m_ab_w_in, m_a_vnorm_g, m_a_spatial_w, m_a_spatial_b, m_b_q_norm_g, m_b_k_norm_g, m_ab_w_out, m_conv_pw1_w, m_conv_pw1_b, m_conv_dw_w, m_conv_dw_b, m_conv_ln_g, m_conv_ln_b, m_conv_pw2_w, m_conv_pw2_b, m_ffn_up_w, m_ffn_dw_w, m_ffn_dw_b, m_ffn_down_w, v_ada_w, v_ada_b, v_norm_mix_g, v_norm_ffn_g, v_ab_w_in, v_a_vnorm_g, v_a_spatial_w, v_a_spatial_b, v_b_q_norm_g, v_b_k_norm_g, v_ab_w_out, v_conv_pw1_w, v_conv_pw1_b, v_conv_dw_w, v_conv_dw_b, v_conv_ln_g, v_conv_ln_b, v_conv_pw2_w, v_conv_pw2_b, v_ffn_up_w, v_ffn_dw_w, v_ffn_dw_b, v_ffn_down_w):
    weights = dict(ada_w=ada_w, ada_b=ada_b, norm_mix_g=norm_mix_g, norm_ffn_g=norm_ffn_g, ab_w_in=ab_w_in, a_vnorm_g=a_vnorm_g, a_spatial_w=a_spatial_w, a_spatial_b=a_spatial_b, b_q_norm_g=b_q_norm_g, b_k_norm_g=b_k_norm_g, ab_w_out=ab_w_out, conv_pw1_w=conv_pw1_w, conv_pw1_b=conv_pw1_b, conv_dw_w=conv_dw_w, conv_dw_b=conv_dw_b, conv_ln_g=conv_ln_g, conv_ln_b=conv_ln_b, conv_pw2_w=conv_pw2_w, conv_pw2_b=conv_pw2_b, ffn_up_w=ffn_up_w, ffn_dw_w=ffn_dw_w, ffn_dw_b=ffn_dw_b, ffn_down_w=ffn_down_w)
    mom1 = dict(ada_w=m_ada_w, ada_b=m_ada_b, norm_mix_g=m_norm_mix_g, norm_ffn_g=m_norm_ffn_g, ab_w_in=m_ab_w_in, a_vnorm_g=m_a_vnorm_g, a_spatial_w=m_a_spatial_w, a_spatial_b=m_a_spatial_b, b_q_norm_g=m_b_q_norm_g, b_k_norm_g=m_b_k_norm_g, ab_w_out=m_ab_w_out, conv_pw1_w=m_conv_pw1_w, conv_pw1_b=m_conv_pw1_b, conv_dw_w=m_conv_dw_w, conv_dw_b=m_conv_dw_b, conv_ln_g=m_conv_ln_g, conv_ln_b=m_conv_ln_b, conv_pw2_w=m_conv_pw2_w, conv_pw2_b=m_conv_pw2_b, ffn_up_w=m_ffn_up_w, ffn_dw_w=m_ffn_dw_w, ffn_dw_b=m_ffn_dw_b, ffn_down_w=m_ffn_down_w)
    mom2 = dict(ada_w=v_ada_w, ada_b=v_ada_b, norm_mix_g=v_norm_mix_g, norm_ffn_g=v_norm_ffn_g, ab_w_in=v_ab_w_in, a_vnorm_g=v_a_vnorm_g, a_spatial_w=v_a_spatial_w, a_spatial_b=v_a_spatial_b, b_q_norm_g=v_b_q_norm_g, b_k_norm_g=v_b_k_norm_g, ab_w_out=v_ab_w_out, conv_pw1_w=v_conv_pw1_w, conv_pw1_b=v_conv_pw1_b, conv_dw_w=v_conv_dw_w, conv_dw_b=v_conv_dw_b, conv_ln_g=v_conv_ln_g, conv_ln_b=v_conv_ln_b, conv_pw2_w=v_conv_pw2_w, conv_pw2_b=v_conv_pw2_b, ffn_up_w=v_ffn_up_w, ffn_dw_w=v_ffn_dw_w, ffn_dw_b=v_ffn_dw_b, ffn_down_w=v_ffn_down_w)
    order = list(weights)
    d, f2 = D_MODEL, 2 * FFN_DIM
    t = x.shape[1]
    me = 4 * lax.axis_index("x") + 2 * lax.axis_index("y") + lax.axis_index("c")
    for window, dil in PATTERNS:
        assert window // dil == Q_BLOCK and t % (dil * Q_BLOCK) == 0

    small_in = [c[0], conv_pw1_b[0], conv_dw_w[0], conv_dw_b[0], conv_ln_g[0], conv_ln_b[0], conv_pw2_b[0], ffn_dw_w]
    g1 = _all_gather_vmem(_pack(small_in, rows=8), "gather_small").reshape(N_DEV, -1)
    c_all, pw1_b, dw_w, dw_b, ln_g, ln_b, pw2_b, fdw_w = _unpack(g1, [a.shape for a in small_in])
    pw1_b, dw_b, ln_g, ln_b, pw2_b = [a.reshape(1, -1) for a in (pw1_b, dw_b, ln_g, ln_b, pw2_b)]
    dw_w = dw_w.transpose(1, 0, 2).reshape(CONV_WIDTH, d)
    fdw_w = fdw_w.transpose(1, 2, 0, 3).reshape(2, FFN_CONV_WIDTH, f2)

    c16 = jnp.pad(c_all, ((0, 2 * N_DEV - c_all.shape[0]), (0, 0)))
    part = jnp.concatenate([_ada_fwd(c16, ada_w[l], f"ada_fwd{l}")[:N_DEV] for l in range(2)], axis=1)
    g2 = _all_gather_vmem(part, "gather_mod").reshape(N_DEV, N_DEV, 2, -1)
    mod = lax.dynamic_index_in_dim(g2, me, axis=1, keepdims=False).transpose(1, 0, 2).reshape(2, 6 * d) + ada_b

    stages = dict(l0_mix=[ab_w_in[0].T, ab_w_out[0]], l0_ffn=[ffn_up_w[0].T, ffn_down_w[0]],
                  l1_conv=[conv_pw1_w[0].T, conv_pw2_w[0]], l1_ffn=[ffn_up_w[1].T, ffn_down_w[1]])
    stages = {k: [s.astype(BF16) for s in v] for k, v in stages.items()}
    names = dict(l0_mix=("w_in_t", "w_out"), l0_ffn=("up_t", "down"), l1_conv=("pw1_t", "pw2"), l1_ffn=("up_t", "down"))
    ready = {"l0_mix": [a.reshape(-1, d) for a in _all_gather_hbm(stages["l0_mix"], "gather_mixer_weights")]}
    behind = (ready, mod)
    arriving = {}
    for stage, group in (("l0_ffn", ("l0_ffn",)), ("l1_conv", ("l1_conv", "l1_ffn"))):
        srcs, _ = lax.optimization_barrier(([s for g in group for s in stages[g]], behind))
        arriving[stage], behind = _pushes_start(
            srcs, _landing_zones(srcs, f"gather_{stage}_zones"), False, f"gather_{stage}_start")
        mod = mod + behind[0:1, 0:1]

    def fetch(stage, after):
        if stage in arriving:
            full = [a.reshape(-1, d) for a in _pushes_wait(arriving[stage], after, f"gather_{stage}_wait")[1]]
            ready[stage] = full[:2]
            if stage == "l1_conv":
                ready["l1_ffn"] = full[2:]
        return dict(zip(names[stage], ready[stage]))

    causal = jnp.tril(jnp.ones((CHUNK, CHUNK), bool))
    wtril = jnp.where(causal[None], a_spatial_w[0], 0.0)
    mixer_w = dict(
        gain=a_vnorm_g[0].reshape(1, A_WIDTH), wtril=wtril.astype(BF16),
        wtril_t=wtril.transpose(0, 2, 1).astype(BF16),
        bias_exp=jnp.repeat(a_spatial_b[0].T, GROUP_DIM, axis=1),
        gq=jnp.tile(b_q_norm_g[0], HEADS)[None, :], gk=jnp.tile(b_k_norm_g[0], HEADS)[None, :],
        seg=jnp.kron(jnp.eye(HEADS, dtype=BF16), jnp.ones((HEAD_DIM, HEAD_DIM), BF16)))
    conv_w = dict(pw1_b=pw1_b, dw_w=dw_w, dw_b=dw_b, ln_g=ln_g, ln_b=ln_b, pw2_b=pw2_b)
    ffn_w = [dict(dw_w=fdw_w[l].reshape(FFN_CONV_WIDTH, 2, FFN_DIM).transpose(1, 0, 2), dw_b=ffn_dw_b[l].reshape(2, 1, FFN_DIM))
             for l in range(2)]

    leaving = {}

    def emit(stage, grads):
        blocks = [g.reshape(N_DEV, g.shape[0] // N_DEV, d) for g in grads]
        leaving[stage], token = _pushes_start(
            blocks, [lax.empty(b.shape, b.dtype) for b in blocks], True, f"reduce_{stage}_start")
        return token

    loss, dx, dmod, (g_mix, g_ffn0, g_conv, g_ffn1) = _local_step(
        x[0], loss_target[0], positions[0].astype(F32)[:, None], mod, norm_mix_g, norm_ffn_g, mixer_w, conv_w, ffn_w,
        fetch, emit)

    me_op = me.astype(jnp.int32).reshape(1)

    def reduced(stage, after):
        blocks, lands = _pushes_wait(leaving[stage], after, f"reduce_{stage}_wait")
        return [_sum_with_own(b, a, me_op, f"reduce_{stage}_sum{i}") for i, (b, a) in enumerate(zip(blocks, lands))]

    (r_up_t1, r_down1), (r_pw1_t, r_pw2), (r_up_t0, r_down0) = [reduced(s, dx) for s in ("l1_ffn", "l1_conv", "l0_ffn")]

    small_g = [
        dmod, jnp.concatenate([g_mix["norm_g"], g_conv["norm_g"]]), jnp.concatenate([g_ffn0["norm_g"], g_ffn1["norm_g"]]),
        g_mix["vnorm_g"], g_mix["spatial_w"], g_mix["spatial_b"], g_mix["q_norm_g"], g_mix["k_norm_g"],
        g_conv["pw1_b"], g_conv["dw_w"], g_conv["dw_b"], g_conv["ln_g"], g_conv["ln_b"], g_conv["pw2_b"],
        jnp.stack([g_ffn0["dw_w"], g_ffn1["dw_w"]]), jnp.concatenate([g_ffn0["dw_b"], g_ffn1["dw_b"]])]
    packed = _pack(small_g, rows=8)
    g3 = _all_gather_vmem(packed, "gather_small_grads").reshape(N_DEV, 8, -1)
    total = _unpack(_sum_slots(g3, "sum_small_grads").reshape(-1), [a.shape for a in small_g])
    (s_dmod, s_mix_g, s_ffn_g, s_vnorm, s_sp_w, s_sp_b, s_gq, s_gk, s_pw1_b, s_dw_w, s_dw_b, s_ln_g, s_ln_b,
     s_pw2_b, s_fdw_w, s_fdw_b) = total
    dmod_all = g3.reshape(N_DEV, -1)[:, :2 * 6 * d].reshape(N_DEV, 2, 6 * d)
    n_ada = ada_w.shape[2]
    dmod16 = jnp.pad(_take_block(dmod_all, me, n_ada, 2), ((0, N_DEV), (0, 0), (0, 0)))
    g_ada_w = jnp.stack([_ada_bwd(c16, dmod16[:, l], f"ada_bwd{l}") for l in range(2)])

    grads = dict(
        ada_w=g_ada_w, ada_b=s_dmod, norm_mix_g=s_mix_g, norm_ffn_g=s_ffn_g,
        a_vnorm_g=s_vnorm[None], a_spatial_w=s_sp_w[None], a_spatial_b=s_sp_b[None], b_q_norm_g=s_gq[None],
        b_k_norm_g=s_gk[None], conv_pw1_w=r_pw1_t.T[None],
        conv_pw1_b=_take_block(s_pw1_b, me, conv_pw1_b.shape[1], 1),
        conv_dw_w=_take_block(s_dw_w, me, conv_dw_w.shape[2], 1)[None],
        conv_dw_b=_take_block(s_dw_b, me, conv_dw_b.shape[1], 1), conv_ln_g=_take_block(s_ln_g, me, conv_ln_g.shape[1], 1),
        conv_ln_b=_take_block(s_ln_b, me, conv_ln_b.shape[1], 1), conv_pw2_w=r_pw2[None],
        conv_pw2_b=_take_block(s_pw2_b, me, conv_pw2_b.shape[1], 1),
        ffn_up_w=jnp.stack([r_up_t0.T, r_up_t1.T]), ffn_dw_w=_take_block(s_fdw_w, me, ffn_dw_w.shape[2], 2),
        ffn_dw_b=s_fdw_b, ffn_down_w=jnp.stack([r_down0, r_down1]))

    large = ("ada_w", "conv_pw1_w", "conv_pw2_w", "ffn_up_w", "ffn_down_w", "ab_w_in", "ab_w_out")
    delta, new_m, new_v = {}, {}, {}
    for name in large:
        if name == "ab_w_in":
            r_in_t, r_out = reduced("l0_mix", new_v["ffn_down_w"])
            grads.update(ab_w_in=r_in_t.T[None], ab_w_out=r_out[None])
        delta[name], new_m[name], new_v[name] = _adamw(weights[name], grads[name], mom1[name], mom2[name], f"adamw_{name}")
    small = [n for n in order if n not in large]
    res = _adamw_small(*[[src[n] for n in small] for src in (weights, grads, mom1, mom2)], "adamw_small")
    for dst, arrs in zip((delta, new_m, new_v), res):
        dst.update(zip(small, arrs))

    loss = lax.psum(loss[0, 0], ("x", "y", "c"))
    return (loss, dx[None], *[grads[n] for n in order], *[delta[n] for n in order],
            *[new_m[n] for n in order], *[new_v[n] for n in order])
```

```python
import functools
import math

import jax
import jax.numpy as jnp
from jax import lax
from jax.experimental import pallas as pl
from jax.experimental.pallas import tpu as pltpu

F32 = jnp.float32
BF16 = jnp.bfloat16
MESH = pl.DeviceIdType.MESH

D_MODEL = 1024
A_WIDTH = 512
A_GROUPS = 4
GROUP_DIM = 128
CHUNK = 128
B_WIDTH = 512
HEADS = 8
HEAD_DIM = 64
PATTERNS = ((128, 1), (512, 4), (2048, 16))
Q_BLOCK = 128
ROPE_THETA = 10000.0
AB_IN = 2560
CONV_WIDTH = 31
FFN_DIM = 2816
FFN_CONV_WIDTH = 3
EPS = 1e-6
NEG = -1e30
N_DEV = 8
ADAM_LR, ADAM_B1, ADAM_B2, ADAM_EPS, ADAM_WD, ADAM_STEP = 0.001, 0.9, 0.999, 1e-08, 0.01, 10

V7X_VMEM_LIMIT = 56 * 2**20
BF16_ROWS = 16
FFN_HALO = 16
CONV_HALO = 32

_NN = (((1,), (0,)), ((), ()))
_NT = (((1,), (1,)), ((), ()))
_TN = (((0,), (0,)), ((), ()))


def _tile(n, prefs=(512, 256, 128)):
    for t in prefs:
        if n % t == 0:
            return t
    return n


def _row_tile(n, cap=512):
    best = n
    for t in range(8, min(n, cap) + 1, 8):
        if n % t == 0:
            best = t
    return best if best <= cap else n


def _params(*sem):
    return pltpu.CompilerParams(dimension_semantics=sem, vmem_limit_bytes=V7X_VMEM_LIMIT)


def _dot(a, b, dims):
    return lax.dot_general(a, b, dims, preferred_element_type=F32)


def _sigmoid(x):
    return 1.0 / (1.0 + jnp.exp(-x))


def _gelu(x):
    return 0.5 * x * (1.0 + lax.erf(x * (2.0 ** -0.5)))


def _gelu_grad(x):
    return 0.5 * (1.0 + lax.erf(x * (2.0 ** -0.5))) + x * jnp.exp(-0.5 * x * x) * (1.0 / math.sqrt(2.0 * math.pi))


def _colsum(v):
    return jnp.sum(v, axis=0, keepdims=True)


MATMUL_VMEM_BUDGET = 40 * 2**20


def _matmul_tiles(m, n, k, out_bytes, with_resid):
    def options(dim):
        opts = [t for t in (1024, 512, 256, 128) if dim % t == 0]
        return opts + [dim] if dim <= 4096 and dim not in opts else opts

    best = None
    for tm in options(m):
        for tn in options(n):
            need = 4 * (tm * k + k * tn) + tm * tn * (4 + 2 * out_bytes) + (24 * tm * tn if with_resid else 0)
            if need <= MATMUL_VMEM_BUDGET and (best is None or tm * tn / (tm + tn) > best[0]):
                best = (tm * tn / (tm + tn), tm, tn)
    return best[1], best[2]


def _matmul_tn_acc(a, b, name, tk=512):
    squeeze = a.ndim == 2
    a3 = a[None] if squeeze else a
    p_, t, m = a3.shape
    n = b.shape[1]
    nk = t // tk

    def body(a_ref, b_ref, o_ref, acc_ref):
        kt = pl.program_id(1)

        @pl.when(kt == 0)
        def _():
            acc_ref[...] = jnp.zeros_like(acc_ref)

        acc_ref[...] += _dot(a_ref[...], b_ref[...], _TN)

        @pl.when(kt == nk - 1)
        def _():
            o_ref[...] = acc_ref[...].astype(BF16)

    out = pl.pallas_call(
        body, name=name, grid=(p_, nk),
        in_specs=[pl.BlockSpec((None, tk, m), lambda p, kt: (p, kt, 0)), pl.BlockSpec((tk, n), lambda p, kt: (kt, 0))],
        out_specs=pl.BlockSpec((None, m, n), lambda p, kt: (p, 0, 0)), out_shape=jax.ShapeDtypeStruct((p_, m, n), BF16),
        scratch_shapes=[pltpu.VMEM((m, n), F32)], compiler_params=_params("parallel", "arbitrary"),
    )(a3, b)
    return out[0] if squeeze else out


def _matmul(a, b, mode, out_dtype, name, bias=None, resid=None):
    if mode == "nn":
        (m, k), (_, n) = a.shape, b.shape
    elif mode == "nt":
        (m, k), (n, _) = a.shape, b.shape
    else:
        (k, m), (_, n) = a.shape, b.shape
    tm, tn = _matmul_tiles(m, n, k, jnp.dtype(out_dtype).itemsize, resid is not None)
    dims = {"nn": _NN, "nt": _NT, "tn": _TN}[mode]
    a_spec = pl.BlockSpec((k, tm), lambda i, j: (0, i)) if mode == "tn" else pl.BlockSpec((tm, k), lambda i, j: (i, 0))
    b_spec = pl.BlockSpec((tn, k), lambda i, j: (j, 0)) if mode == "nt" else pl.BlockSpec((k, tn), lambda i, j: (0, j))
    in_specs, args = [a_spec, b_spec], [a, b]
    row_spec = pl.BlockSpec((1, tn), lambda i, j: (0, j))
    tile_spec = pl.BlockSpec((tm, tn), lambda i, j: (i, j))
    if bias is not None:
        in_specs.append(row_spec)
        args.append(bias)
    if resid is not None:
        in_specs += [tile_spec, row_spec]
        args += list(resid)
    out_shape = [jax.ShapeDtypeStruct((m, n), out_dtype)]
    out_specs = [tile_spec]
    if resid is not None:
        out_shape.append(jax.ShapeDtypeStruct((m, n), F32))
        out_specs.append(tile_spec)

    def body(*refs):
        a_ref, b_ref = refs[0], refs[1]
        pos = 2
        acc = _dot(a_ref[...], b_ref[...], dims)
        if bias is not None:
            acc = acc + refs[pos][...]
            pos += 1
        if resid is not None:
            x_ref, g_ref = refs[pos], refs[pos + 1]
            pos += 2
        refs[pos][...] = acc.astype(out_dtype)
        if resid is not None:
            refs[pos + 1][...] = x_ref[...] + g_ref[...] * acc

    outs = pl.pallas_call(
        body, name=name, grid=(m // tm, n // tn), in_specs=in_specs, out_specs=out_specs, out_shape=out_shape,
        compiler_params=_params("parallel", "parallel"),
    )(*args)
    return outs if resid is not None else outs[0]


def _modnorm(x, g, sc, sh, name):
    t, d = x.shape
    tm = _tile(t)
    row = pl.BlockSpec((1, d), lambda i: (0, 0))
    blk = pl.BlockSpec((tm, d), lambda i: (i, 0))

    def body(x_ref, g_ref, sc_ref, sh_ref, o_ref):
        x = x_ref[...]
        r = lax.rsqrt(jnp.mean(x * x, axis=-1, keepdims=True) + EPS)
        o_ref[...] = ((x * r) * g_ref[...] * (1.0 + sc_ref[...]) + sh_ref[...]).astype(BF16)

    return pl.pallas_call(
        body, name=name, grid=(t // tm,), in_specs=[blk, row, row, row], out_specs=blk,
        out_shape=jax.ShapeDtypeStruct((t, d), BF16), compiler_params=_params("parallel"),
    )(x, g, sc, sh)


def _modnorm_bwd(x, dh, g, sc, dres, name):
    t, d = x.shape
    tm = _tile(t)
    row = pl.BlockSpec((1, d), lambda i: (0, 0))
    blk = pl.BlockSpec((tm, d), lambda i: (i, 0))

    def body(x_ref, dh_ref, g_ref, sc_ref, dres_ref, dx_ref, dw_ref, dsh_ref):
        @pl.when(pl.program_id(0) == 0)
        def _():
            dw_ref[...] = jnp.zeros_like(dw_ref)
            dsh_ref[...] = jnp.zeros_like(dsh_ref)

        x = x_ref[...]
        dh = dh_ref[...].astype(F32)
        r = lax.rsqrt(jnp.mean(x * x, axis=-1, keepdims=True) + EPS)
        xn = x * r
        dxn = dh * (g_ref[...] * (1.0 + sc_ref[...]))
        dx_ref[...] = dres_ref[...] + r * (dxn - xn * jnp.mean(dxn * xn, axis=-1, keepdims=True))
        dw_ref[...] += _colsum(dh * xn)
        dsh_ref[...] += _colsum(dh)

    return pl.pallas_call(
        body, name=name, grid=(t // tm,), in_specs=[blk, blk, row, row, blk], out_specs=[blk, row, row],
        out_shape=[jax.ShapeDtypeStruct((t, d), F32), jax.ShapeDtypeStruct((1, d), F32), jax.ShapeDtypeStruct((1, d), F32)],
        compiler_params=_params("arbitrary"),
    )(x, dh, g, sc, dres)


def _gate_bwd(dxn, y, gate, name):
    t, d = dxn.shape
    tm = _tile(t)
    row = pl.BlockSpec((1, d), lambda i: (0, 0))
    blk = pl.BlockSpec((tm, d), lambda i: (i, 0))

    def body(dxn_ref, y_ref, g_ref, dy_ref, dg_ref):
        @pl.when(pl.program_id(0) == 0)
        def _():
            dg_ref[...] = jnp.zeros_like(dg_ref)

        dxn = dxn_ref[...]
        dy_ref[...] = (dxn * g_ref[...]).astype(BF16)
        dg_ref[...] += _colsum(dxn * y_ref[...])

    return pl.pallas_call(
        body, name=name, grid=(t // tm,), in_specs=[blk, blk, row], out_specs=[blk, row],
        out_shape=[jax.ShapeDtypeStruct((t, d), BF16), jax.ShapeDtypeStruct((1, d), F32)],
        compiler_params=_params("arbitrary"),
    )(dxn, y, gate)


def _loss_head(y, target, name):
    t, d = y.shape
    tm = _tile(t)
    blk = pl.BlockSpec((tm, d), lambda i: (i, 0))
    one = pl.BlockSpec((1, 1), lambda i: (0, 0))

    def body(y_ref, t_ref, dy_ref, loss_ref, acc_ref):
        @pl.when(pl.program_id(0) == 0)
        def _():
            acc_ref[...] = jnp.zeros_like(acc_ref)

        e = y_ref[...] - t_ref[...]
        dy_ref[...] = e * (1.0 / d)
        acc_ref[...] += _colsum(e * e)

        @pl.when(pl.program_id(0) == pl.num_programs(0) - 1)
        def _():
            loss_ref[...] = jnp.sum(acc_ref[...], axis=1, keepdims=True) * (0.5 / d)

    return pl.pallas_call(
        body, name=name, grid=(t // tm,), in_specs=[blk, blk], out_specs=[blk, one],
        out_shape=[jax.ShapeDtypeStruct((t, d), F32), jax.ShapeDtypeStruct((1, 1), F32)],
        scratch_shapes=[pltpu.VMEM((1, d), F32)], compiler_params=_params("arbitrary"),
    )(y, target)


def _group_norm(vg, gain):
    mu = jnp.mean(vg, axis=-1, keepdims=True)
    xc = vg - mu
    rstd = lax.rsqrt(jnp.mean(xc * xc, axis=-1, keepdims=True) + EPS)
    xhat = xc * rstd
    return xhat, rstd, xhat * gain


def _gmlp_fwd(z, gain, wtril, bias_exp, name):
    t = z.shape[0]
    zu = pl.BlockSpec((CHUNK, A_WIDTH), lambda i: (i, 0))
    zv = pl.BlockSpec((CHUNK, A_WIDTH), lambda i: (i, 1))
    full2 = lambda shp: pl.BlockSpec(shp, lambda i: (0, 0))
    w_spec = pl.BlockSpec((A_GROUPS, CHUNK, CHUNK), lambda i: (0, 0, 0))

    def body(zu_ref, zv_ref, gain_ref, w_ref, b_ref, ya_ref):
        ua = _gelu(zu_ref[...].astype(F32))
        vg = _gelu(zv_ref[...].astype(F32))
        for g in range(A_GROUPS):
            sl = slice(g * GROUP_DIM, (g + 1) * GROUP_DIM)
            _, _, vn = _group_norm(vg[:, sl], gain_ref[:, sl])
            f = _dot(w_ref[g], vn.astype(BF16), _NN) + b_ref[:, sl]
            ya_ref[:, sl] = (ua[:, sl] * f).astype(BF16)

    return pl.pallas_call(
        body, name=name, grid=(t // CHUNK,),
        in_specs=[zu, zv, full2((1, A_WIDTH)), w_spec, full2((CHUNK, A_WIDTH))], out_specs=zu,
        out_shape=jax.ShapeDtypeStruct((t, A_WIDTH), BF16), compiler_params=_params("parallel"),
    )(z, z, gain, wtril, bias_exp)


def _gmlp_bwd(z, dcat, gain, wtril, wtril_t, bias_exp, name):
    t = z.shape[0]
    zu = pl.BlockSpec((CHUNK, A_WIDTH), lambda i: (i, 0))
    zv = pl.BlockSpec((CHUNK, A_WIDTH), lambda i: (i, 1))
    full2 = lambda shp: pl.BlockSpec(shp, lambda i: (0, 0))
    w_spec = pl.BlockSpec((A_GROUPS, CHUNK, CHUNK), lambda i: (0, 0, 0))
    dz_spec = pl.BlockSpec((CHUNK, 2 * A_WIDTH), lambda i: (i, 0))

    def body(zu_ref, zv_ref, dya_ref, gain_ref, w_ref, wt_ref, b_ref, dz_ref, dw_ref, dgain_ref, dbias_ref):
        @pl.when(pl.program_id(0) == 0)
        def _():
            dw_ref[...] = jnp.zeros_like(dw_ref)
            dgain_ref[...] = jnp.zeros_like(dgain_ref)
            dbias_ref[...] = jnp.zeros_like(dbias_ref)

        zu_v = zu_ref[...].astype(F32)
        zv_v = zv_ref[...].astype(F32)
        dya = dya_ref[...].astype(F32)
        ua = _gelu(zu_v)
        vg = _gelu(zv_v)
        row = lax.broadcasted_iota(jnp.int32, (CHUNK, CHUNK), 0)
        col = lax.broadcasted_iota(jnp.int32, (CHUNK, CHUNK), 1)
        for g in range(A_GROUPS):
            sl = slice(g * GROUP_DIM, (g + 1) * GROUP_DIM)
            gain_g = gain_ref[:, sl]
            xhat, rstd, vn = _group_norm(vg[:, sl], gain_g)
            vn16 = vn.astype(BF16)
            f = _dot(w_ref[g], vn16, _NN) + b_ref[:, sl]
            df = dya[:, sl] * ua[:, sl]
            df16 = df.astype(BF16)
            dz_ref[:, sl] = (dya[:, sl] * f * _gelu_grad(zu_v[:, sl])).astype(BF16)
            dw_ref[g] += jnp.where(row >= col, _dot(df16, vn16, _NT), 0.0)
            dvn = _dot(wt_ref[g], df16, _NN)
            dgain_ref[:, sl] += _colsum(dvn * xhat)
            dxh = dvn * gain_g
            dvg = rstd * (dxh - jnp.mean(dxh, axis=-1, keepdims=True) - xhat * jnp.mean(dxh * xhat, axis=-1, keepdims=True))
            dz_ref[:, A_WIDTH + g * GROUP_DIM:A_WIDTH + (g + 1) * GROUP_DIM] = (dvg * _gelu_grad(zv_v[:, sl])).astype(BF16)
            dbias_ref[:, sl] += df

    return pl.pallas_call(
        body, name=name, grid=(t // CHUNK,),
        in_specs=[zu, zv, zu, full2((1, A_WIDTH)), w_spec, w_spec, full2((CHUNK, A_WIDTH))],
        out_specs=[dz_spec, w_spec, full2((1, A_WIDTH)), full2((CHUNK, A_WIDTH))],
        out_shape=[jax.ShapeDtypeStruct((t, 2 * A_WIDTH), BF16), jax.ShapeDtypeStruct((A_GROUPS, CHUNK, CHUNK), F32),
                   jax.ShapeDtypeStruct((1, A_WIDTH), F32), jax.ShapeDtypeStruct((CHUNK, A_WIDTH), F32)],
        compiler_params=_params("arbitrary"),
    )(z, z, dcat, gain, wtril, wtril_t, bias_exp)


def _rope_tables(pos, inv_freq, sign, name):
    t = pos.shape[0]
    tm = _tile(t)
    row = pl.BlockSpec((1, B_WIDTH), lambda i: (0, 0))
    blk = pl.BlockSpec((tm, B_WIDTH), lambda i: (i, 0))

    def body(pos_ref, f_ref, s_ref, cos_ref, sin_ref):
        ang = pos_ref[...] * f_ref[:, 0:LANES]
        cos_ref[...] = jnp.tile(jnp.cos(ang), (1, B_WIDTH // LANES))
        sin_ref[...] = jnp.tile(jnp.sin(ang) * s_ref[:, 0:LANES], (1, B_WIDTH // LANES))

    return pl.pallas_call(
        body, name=name, grid=(t // tm,), in_specs=[pl.BlockSpec((tm, 1), lambda i: (i, 0)), row, row],
        out_specs=[blk, blk], out_shape=[jax.ShapeDtypeStruct((t, B_WIDTH), F32)] * 2,
        compiler_params=_params("parallel"),
    )(pos, inv_freq, sign)


def _head_sum(v, seg):
    hi = v.astype(BF16)
    lo = (v - hi.astype(F32)).astype(BF16)
    return _dot(hi, seg, _NN) + _dot(lo, seg, _NN)


def _swap_halves(v):
    lane = lax.broadcasted_iota(jnp.int32, v.shape, 1)
    return jnp.where((lane & (HEAD_DIM - 1)) < HEAD_DIM // 2,pltpu.roll(v, B_WIDTH - HEAD_DIM // 2, 1), pltpu.roll(v, HEAD_DIM // 2, 1))


DILATIONS = tuple(dil for _, dil in PATTERNS)
SUBSEQ_TM = 256
LANES = 128


def _subseq_shape(t, dil):
    return (t // dil, dil * B_WIDTH)


def _subseq_spec(tm, dil):
    return pl.BlockSpec((tm // dil, dil * B_WIDTH), lambda i: (i, 0))


def _to_subseq(x, scr_ref, dil):
    if dil == 1:
        return x
    tm, w = x.shape
    for c in range(w // LANES):
        scr_ref[c * tm:(c + 1) * tm, :] = x[:, c * LANES:(c + 1) * LANES]
    return jnp.concatenate([scr_ref[pl.ds(c * tm + r, tm // dil, stride=dil), :]
                            for r in range(dil) for c in range(w // LANES)], axis=1)


def _from_subseq(y, scr_ref, dil):
    if dil == 1:
        return y
    n, w = y.shape[0], y.shape[1] // dil
    tm = n * dil
    for r in range(dil):
        for c in range(w // LANES):
            scr_ref[pl.ds(c * tm + r, n, stride=dil), :] = y[:, r * w + c * LANES:r * w + (c + 1) * LANES]
    return jnp.concatenate([scr_ref[c * tm:(c + 1) * tm, :] for c in range(w // LANES)], axis=1)


def _subseq_scratch(tm):
    return pltpu.VMEM((B_WIDTH // LANES * tm, LANES), F32)


def _qk_prep(z, cos_t, sin_t, gq, gk, seg, name):
    t = z.shape[0]
    tm = _tile(t, (SUBSEQ_TM,))
    col = lambda c: pl.BlockSpec((tm, B_WIDTH), lambda i: (i, c))
    row = pl.BlockSpec((1, B_WIDTH), lambda i: (0, 0))
    blk = col(0)
    nd = len(DILATIONS)

    def body(q_ref, k_ref, v_ref, cos_ref, sin_ref, gq_ref, gk_ref, seg_ref, *rest):
        out_refs, scr_ref = rest[:-1], rest[-1]

        def norm_rot(x, g):
            r = lax.rsqrt(_head_sum(x * x, seg_ref[...]) * (1.0 / HEAD_DIM) + EPS)
            xn = x * r * g
            return xn * cos_ref[...] + _swap_halves(xn) * sin_ref[...]

        vals = (norm_rot(q_ref[...].astype(F32), gq_ref[...]), norm_rot(k_ref[...].astype(F32), gk_ref[...]),
                v_ref[...].astype(F32))
        for a, val in enumerate(vals):
            for b, dil in enumerate(DILATIONS):
                out_refs[a * nd + b][...] = _to_subseq(val, scr_ref, dil).astype(BF16)

    outs = pl.pallas_call(
        body, name=name, grid=(t // tm,),
        in_specs=[col(2), col(3), col(4), blk, blk, row, row, pl.BlockSpec((B_WIDTH, B_WIDTH), lambda i: (0, 0))],
        out_specs=[_subseq_spec(tm, dil) for _ in range(3) for dil in DILATIONS],
        out_shape=[jax.ShapeDtypeStruct(_subseq_shape(t, dil), BF16) for _ in range(3) for dil in DILATIONS],
        scratch_shapes=[_subseq_scratch(tm)], compiler_params=_params("parallel"),
    )(z, z, z, cos_t, sin_t, gq, gk, seg)
    return outs[:nd], outs[nd:2 * nd], outs[2 * nd:]


def _qk_prep_bwd(z, dqs, dks, dvs, cos_t, sin_t, gq, gk, seg, name):
    t = z.shape[0]
    tm = _tile(t, (SUBSEQ_TM,))
    col = lambda c: pl.BlockSpec((tm, B_WIDTH), lambda i: (i, c))
    row = pl.BlockSpec((1, B_WIDTH), lambda i: (0, 0))
    blk = col(0)
    nb = len(DILATIONS)
    subs = [_subseq_spec(tm, dil) for dil in DILATIONS]

    def body(*refs):
        q_ref, k_ref = refs[0], refs[1]
        dq_refs, dk_refs, dv_refs = refs[2:2 + nb], refs[2 + nb:2 + 2 * nb], refs[2 + 2 * nb:2 + 3 * nb]
        cos_ref, sin_ref, gq_ref, gk_ref, seg_ref, dz_ref, dgq_ref, dgk_ref, scr_ref = refs[2 + 3 * nb:]

        @pl.when(pl.program_id(0) == 0)
        def _():
            dgq_ref[...] = jnp.zeros_like(dgq_ref)
            dgk_ref[...] = jnp.zeros_like(dgk_ref)

        def total(d_refs):
            return sum(_from_subseq(r_[...], scr_ref, dil) for r_, dil in zip(d_refs, DILATIONS))

        def back(x, d_refs, g, dg_ref):
            dout = total(d_refs)
            dy = dout * cos_ref[...] + _swap_halves(dout * sin_ref[...])
            r = lax.rsqrt(_head_sum(x * x, seg_ref[...]) * (1.0 / HEAD_DIM) + EPS)
            xn = x * r
            dg_ref[...] += _colsum(dy * xn)
            dxn = dy * g
            return r * (dxn - xn * (_head_sum(dxn * xn, seg_ref[...]) * (1.0 / HEAD_DIM)))

        dz_ref[:, 0:B_WIDTH] = back(q_ref[...].astype(F32), dq_refs, gq_ref[...], dgq_ref).astype(BF16)
        dz_ref[:, B_WIDTH:2 * B_WIDTH] = back(k_ref[...].astype(F32), dk_refs, gk_ref[...], dgk_ref).astype(BF16)
        dz_ref[:, 2 * B_WIDTH:3 * B_WIDTH] = total(dv_refs).astype(BF16)

    return pl.pallas_call(
        body, name=name, grid=(t // tm,),
        in_specs=[col(2), col(3)] + subs * 3 + [blk, blk, row, row, pl.BlockSpec((B_WIDTH, B_WIDTH), lambda i: (0, 0))],
        out_specs=[pl.BlockSpec((tm, 3 * B_WIDTH), lambda i: (i, 0)), row, row],
        out_shape=[jax.ShapeDtypeStruct((t, 3 * B_WIDTH), BF16), jax.ShapeDtypeStruct((1, B_WIDTH), F32),
                   jax.ShapeDtypeStruct((1, B_WIDTH), F32)],
        scratch_shapes=[_subseq_scratch(tm)], compiler_params=_params("arbitrary"),
    )(z, z, *dqs, *dks, *dvs, cos_t, sin_t, gq, gk, seg)


def _subseq_views(x, col, name):
    t = x.shape[0]
    tm = _tile(t, (SUBSEQ_TM,))

    def body(x_ref, *rest):
        out_refs, scr_ref = rest[:-1], rest[-1]
        val = x_ref[...].astype(F32)
        for o_ref, dil in zip(out_refs, DILATIONS):
            o_ref[...] = _to_subseq(val, scr_ref, dil).astype(o_ref.dtype)

    return pl.pallas_call(
        body, name=name, grid=(t // tm,), in_specs=[pl.BlockSpec((tm, B_WIDTH), lambda i: (i, col))],
        out_specs=[_subseq_spec(tm, dil) for dil in DILATIONS],
        out_shape=[jax.ShapeDtypeStruct(_subseq_shape(t, dil), x.dtype) for dil in DILATIONS],
        scratch_shapes=[_subseq_scratch(tm)], compiler_params=_params("parallel"),
    )(x)


def _attn_fwd(q, k, v, dil, name):
    t = q.shape[0] * dil
    nb = t // dil // Q_BLOCK
    cur = pl.BlockSpec((Q_BLOCK, B_WIDTH), lambda r, i: (i, r))
    prev = pl.BlockSpec((Q_BLOCK, B_WIDTH), lambda r, i: (jnp.maximum(i - 1, 0), r))

    def body(q_ref, kp_ref, kc_ref, vp_ref, vc_ref, o_ref, lse_ref):
        i = pl.program_id(1)
        q = q_ref[...]
        kk = jnp.concatenate([kp_ref[...], kc_ref[...]], axis=0)
        vv = jnp.concatenate([vp_ref[...], vc_ref[...]], axis=0)
        a = lax.broadcasted_iota(jnp.int32, (Q_BLOCK, 2 * Q_BLOCK), 0)
        j = lax.broadcasted_iota(jnp.int32, (Q_BLOCK, 2 * Q_BLOCK), 1)
        dist = a + Q_BLOCK - j
        mask = (dist >= 0) & (dist <= Q_BLOCK) & ((j >= Q_BLOCK) | (i > 0))
        for h in range(HEADS):
            sl = slice(h * HEAD_DIM, (h + 1) * HEAD_DIM)
            s = jnp.where(mask, _dot(q[:, sl], kk[:, sl], _NT) * (HEAD_DIM ** -0.5), NEG)
            m = jnp.max(s, axis=-1, keepdims=True)
            p = jnp.exp(s - m)
            den = jnp.sum(p, axis=-1, keepdims=True)
            o_ref[:, sl] = _dot(p.astype(BF16), vv[:, sl], _NN) / den
            lse_ref[:, sl] = jnp.broadcast_to(m + jnp.log(den), (Q_BLOCK, HEAD_DIM))

    return pl.pallas_call(
        body, name=name, grid=(dil, nb), in_specs=[cur, prev, cur, prev, cur], out_specs=[cur, cur],
        out_shape=[jax.ShapeDtypeStruct(_subseq_shape(t, dil), F32)] * 2,
        compiler_params=_params("parallel", "parallel"),
    )(q, k, k, v, v)


def _attn_merge(outs, lses, name):
    nb = len(DILATIONS)
    t = outs[0].shape[0] * DILATIONS[0]
    tm = _tile(t, (SUBSEQ_TM,))
    subs = [_subseq_spec(tm, dil) for dil in DILATIONS]

    def body(*refs):
        o_refs, l_refs = refs[:nb], refs[nb:2 * nb]
        yb_refs, lse_refs, scr_ref = refs[2 * nb:3 * nb], refs[3 * nb:4 * nb], refs[4 * nb]
        ls = [_from_subseq(r[...], scr_ref, dil) for r, dil in zip(l_refs, DILATIONS)]
        m = functools.reduce(jnp.maximum, ls)
        tot = m + jnp.log(sum(jnp.exp(l - m) for l in ls))
        yb = sum(jnp.exp(l - tot) * _from_subseq(o[...], scr_ref, dil) for l, o, dil in zip(ls, o_refs, DILATIONS))
        yb = yb.astype(BF16).astype(F32)
        for yb_ref, lse_ref, dil in zip(yb_refs, lse_refs, DILATIONS):
            yb_ref[...] = _to_subseq(yb, scr_ref, dil).astype(BF16)
            lse_ref[...] = _to_subseq(tot, scr_ref, dil)

    outs_ = pl.pallas_call(
        body, name=name, grid=(t // tm,), in_specs=subs * 2, out_specs=subs * 2,
        out_shape=[jax.ShapeDtypeStruct(_subseq_shape(t, dil), BF16) for dil in DILATIONS]
        + [jax.ShapeDtypeStruct(_subseq_shape(t, dil), F32) for dil in DILATIONS],
        scratch_shapes=[_subseq_scratch(tm)], compiler_params=_params("parallel"),
    )(*outs, *lses)
    return outs_[:nb], outs_[nb:]


def _attn_bwd(q, k, v, do, o, lse, dil, name):
    t = q.shape[0] * dil
    nb = t // dil // Q_BLOCK
    blk = lambda f: pl.BlockSpec((Q_BLOCK, B_WIDTH), lambda r, i: (f(i), r))
    cur = blk(lambda i: jnp.minimum(i, nb - 1))
    prev = blk(lambda i: jnp.clip(i - 1, 0, nb - 1))
    scale = HEAD_DIM ** -0.5

    def body(q_ref, kp_ref, kc_ref, vp_ref, vc_ref, do_ref, o_ref, lse_ref, dq_ref, dk_ref, dv_ref,
             ck_ref, cv_ref, tk_ref, tv_ref):
        i = pl.program_id(1)

        @pl.when(i == 0)
        def _():
            ck_ref[...] = jnp.zeros_like(ck_ref)
            cv_ref[...] = jnp.zeros_like(cv_ref)

        @pl.when(i < nb)
        def _():
            q = q_ref[...]
            kk = jnp.concatenate([kp_ref[...], kc_ref[...]], axis=0)
            vv = jnp.concatenate([vp_ref[...], vc_ref[...]], axis=0)
            do = do_ref[...]
            dof = do.astype(F32)
            of = o_ref[...].astype(F32)
            a = lax.broadcasted_iota(jnp.int32, (Q_BLOCK, 2 * Q_BLOCK), 0)
            j = lax.broadcasted_iota(jnp.int32, (Q_BLOCK, 2 * Q_BLOCK), 1)
            dist = a + Q_BLOCK - j
            mask = (dist >= 0) & (dist <= Q_BLOCK) & ((j >= Q_BLOCK) | (i > 0))
            for h in range(HEADS):
                sl = slice(h * HEAD_DIM, (h + 1) * HEAD_DIM)
                s = jnp.where(mask, _dot(q[:, sl], kk[:, sl], _NT) * scale, NEG)
                p = jnp.exp(s - lse_ref[:, h * HEAD_DIM:h * HEAD_DIM + 1])
                dp = _dot(do[:, sl], vv[:, sl], _NT)
                delta = jnp.sum(dof[:, sl] * of[:, sl], axis=-1, keepdims=True)
                ds = (p * (dp - delta) * scale).astype(BF16)
                dq_ref[:, sl] = _dot(ds, kk[:, sl], _NN)
                dv_t = _dot(do[:, sl], p.astype(BF16), _TN)
                dk_t = _dot(q[:, sl], ds, _TN)
                tk_ref[sl, :] = ck_ref[sl, :] + dk_t[:, :Q_BLOCK]
                tv_ref[sl, :] = cv_ref[sl, :] + dv_t[:, :Q_BLOCK]
                ck_ref[sl, :] = dk_t[:, Q_BLOCK:]
                cv_ref[sl, :] = dv_t[:, Q_BLOCK:]

        @pl.when(i == nb)
        def _():
            tk_ref[...] = ck_ref[...]
            tv_ref[...] = cv_ref[...]

        @pl.when(i >= 1)
        def _():
            dk_ref[...] = tk_ref[...].T
            dv_ref[...] = tv_ref[...].T

    return pl.pallas_call(
        body, name=name, grid=(dil, nb + 1), in_specs=[cur, prev, cur, prev, cur, cur, cur, cur],
        out_specs=[cur, prev, prev], out_shape=[jax.ShapeDtypeStruct(_subseq_shape(t, dil), F32)] * 3,
        scratch_shapes=[pltpu.VMEM((B_WIDTH, Q_BLOCK), F32)] * 4,
        compiler_params=_params("parallel", "arbitrary"),
    )(q, k, k, v, v, do, o, lse)


FFN_TN = 256
FFN_FWD_CHUNK = 256
FFN_BWD_CHUNK = 128


def _ffn_up(h, up_t, name):
    t, k = h.shape
    tm = _tile(t)

    def body(h_ref, w_ref, o_ref):
        o_ref[...] = _dot(h_ref[...], w_ref[...], _NT).astype(BF16)

    return pl.pallas_call(
        body, name=name, grid=(2, t // tm),
        in_specs=[pl.BlockSpec((tm, k), lambda p, i: (i, 0)), pl.BlockSpec((None, FFN_DIM, k), lambda p, i: (p, 0, 0))],
        out_specs=pl.BlockSpec((None, tm, FFN_DIM), lambda p, i: (p, i, 0)),
        out_shape=jax.ShapeDtypeStruct((2, t, FFN_DIM), BF16), compiler_params=_params("parallel", "parallel"),
    )(h, up_t.reshape(2, FFN_DIM, k))


def _ffn_up_dx(du, up_t, name):
    t = du.shape[1]
    k = up_t.shape[1]
    tm = _tile(t)

    def body(a_ref, b_ref, o_ref):
        o_ref[...] = _dot(a_ref[0], b_ref[0], _NN) + _dot(a_ref[1], b_ref[1], _NN)

    return pl.pallas_call(
        body, name=name, grid=(t // tm,),
        in_specs=[pl.BlockSpec((2, tm, FFN_DIM), lambda i: (0, i, 0)), pl.BlockSpec((2, FFN_DIM, k), lambda i: (0, 0, 0))],
        out_specs=pl.BlockSpec((tm, k), lambda i: (i, 0)), out_shape=jax.ShapeDtypeStruct((t, k), F32),
        compiler_params=_params("parallel"),
    )(du, up_t.reshape(2, FFN_DIM, k))


def _ffn_conv(win, w_ref, b_ref, p):
    x = win.astype(F32)
    x0, x1, x2 = x[FFN_HALO:], pltpu.roll(x, 1, 0)[FFN_HALO:], pltpu.roll(x, 2, 0)[FFN_HALO:]
    return b_ref[p] + w_ref[p, 2:3, :] * x0 + w_ref[p, 1:2, :] * x1 + w_ref[p, 0:1, :] * x2


def _zero_if(cond, v):
    return jnp.where(cond, 0, v).astype(v.dtype)


def _ffn_act(u, dw_w, dw_b, name):
    t = u.shape[1]
    tm = _tile(t)
    chunk = min(FFN_FWD_CHUNK, tm)
    hb = tm // FFN_HALO
    main = pl.BlockSpec((2, tm, FFN_TN), lambda i, j: (0, i, j))
    halo = pl.BlockSpec((2, FFN_HALO, FFN_TN), lambda i, j: (0, jnp.maximum(i * hb - 1, 0), j))
    wsp = pl.BlockSpec((2, FFN_CONV_WIDTH, FFN_TN), lambda i, j: (0, 0, j))
    bsp = pl.BlockSpec((2, 1, FFN_TN), lambda i, j: (0, 0, j))

    def body(u_ref, uh_ref, w_ref, b_ref, o_ref, z_ref):
        first = pl.program_id(0) == 0

        def emit(rows, wins):
            za, zb = _ffn_conv(wins[0], w_ref, b_ref, 0), _ffn_conv(wins[1], w_ref, b_ref, 1)
            o_ref[rows, :] = (za * _sigmoid(za) * zb).astype(BF16)
            z_ref[0, rows, :] = za.astype(BF16)
            z_ref[1, rows, :] = zb.astype(BF16)

        emit(pl.ds(0, chunk), [jnp.concatenate([_zero_if(first, uh_ref[p]), u_ref[p, 0:chunk, :]], axis=0) for p in range(2)])

        def step(c, carry):
            s = pl.multiple_of(c * chunk, chunk)
            emit(pl.ds(s, chunk), [u_ref[p, pl.ds(s - FFN_HALO, chunk + FFN_HALO), :] for p in range(2)])
            return carry

        lax.fori_loop(1, tm // chunk, step, 0)

    return pl.pallas_call(
        body, name=name, grid=(t // tm, FFN_DIM // FFN_TN), in_specs=[main, halo, wsp, bsp],
        out_specs=[pl.BlockSpec((tm, FFN_TN), lambda i, j: (i, j)), main],
        out_shape=[jax.ShapeDtypeStruct((t, FFN_DIM), BF16), jax.ShapeDtypeStruct((2, t, FFN_DIM), BF16)],
        compiler_params=_params("parallel", "parallel"),
    )(u, u, dw_w, dw_b)


def _fold8(v):
    return jnp.sum(v.reshape(v.shape[0] // 8, 8, v.shape[1]), axis=0)


def _ffn_act_bwd(u, z, dact, dw_w, name):
    t = u.shape[1]
    tm = _tile(t)
    chunk = min(FFN_BWD_CHUNK, tm // 2)
    halo = FFN_HALO
    hb = tm // halo
    nt = t // tm
    last_halo = t // halo - 1
    next_i = lambda i: jnp.minimum((i + 1) * hb, last_halo)
    main = pl.BlockSpec((2, tm, FFN_TN), lambda j, i: (0, i, j))
    nxt = pl.BlockSpec((2, halo, FFN_TN), lambda j, i: (0, next_i(i), j))
    wsp = pl.BlockSpec((2, FFN_CONV_WIDTH, FFN_TN), lambda j, i: (0, 0, j))
    bsp = pl.BlockSpec((2, 1, FFN_TN), lambda j, i: (0, 0, j))

    def body(u_ref, z_ref, zn_ref, da_ref, dan_ref, w_ref, du_ref, dw_ref, db_ref, acc_ref):
        i = pl.program_id(1)
        last = i == nt - 1
        acc_ref[...] = jnp.zeros_like(acc_ref)

        def emit(rows, zs, dact):
            n = chunk + halo
            za, zb, dact = zs[0].astype(F32), zs[1].astype(F32), dact.astype(F32)
            sg = _sigmoid(za)
            dzs = (dact * zb * (sg * (1.0 + za * (1.0 - sg))), dact * (za * sg))
            for p, dz in enumerate(dzs):
                ahead = (dz[:chunk], pltpu.roll(dz, n - 1, 0)[:chunk], pltpu.roll(dz, n - 2, 0)[:chunk])
                um = u_ref[p, rows, :].astype(F32)
                acc_ref[p, FFN_CONV_WIDTH] += _fold8(ahead[0])
                du = None
                for j, dzj in enumerate(ahead):
                    k = FFN_CONV_WIDTH - 1 - j
                    acc_ref[p, k] += _fold8(dzj * um)
                    term = w_ref[p, k:k + 1, :] * dzj
                    du = term if du is None else du + term
                du_ref[p, rows, :] = du.astype(BF16)

        def step(c, carry):
            s = pl.multiple_of(c * chunk, chunk)
            emit(pl.ds(s, chunk), [z_ref[p, pl.ds(s, chunk + halo), :] for p in range(2)], da_ref[pl.ds(s, chunk + halo), :])
            return carry

        lax.fori_loop(0, tm // chunk - 1, step, 0)
        s = tm - chunk
        emit(pl.ds(s, chunk),
             [jnp.concatenate([z_ref[p, s:tm, :], zn_ref[p]], axis=0) for p in range(2)],
             jnp.concatenate([da_ref[s:tm, :], _zero_if(last, dan_ref[...])], axis=0))

        @pl.when(i == 0)
        def _():
            dw_ref[...] = jnp.zeros_like(dw_ref)
            db_ref[...] = jnp.zeros_like(db_ref)

        for p in range(2):
            for k in range(FFN_CONV_WIDTH):
                dw_ref[p, k:k + 1, :] += _colsum(acc_ref[p, k])
            db_ref[p] += _colsum(acc_ref[p, FFN_CONV_WIDTH])

    return pl.pallas_call(
        body, name=name, grid=(FFN_DIM // FFN_TN, nt),
        in_specs=[main, main, nxt, pl.BlockSpec((tm, FFN_TN), lambda j, i: (i, j)),
                  pl.BlockSpec((halo, FFN_TN), lambda j, i: (next_i(i), j)), wsp],
        out_specs=[main, wsp, bsp],
        out_shape=[jax.ShapeDtypeStruct((2, t, FFN_DIM), BF16), jax.ShapeDtypeStruct((2, FFN_CONV_WIDTH, FFN_DIM), F32),
                   jax.ShapeDtypeStruct((2, 1, FFN_DIM), F32)],
        scratch_shapes=[pltpu.VMEM((2, FFN_CONV_WIDTH + 1, 8, FFN_TN), F32)],
        compiler_params=_params("parallel", "arbitrary"),
    )(u, z, z, dact, dact, dw_w)


CONV_TM = 256
CONV_ROWS = 128
CONV_LANES = 128


def _glu_window(pa_ref, pah_ref, pg_ref, pgh_ref, scr_ref, first):
    ah, gh = pah_ref[...].astype(F32), pgh_ref[...].astype(F32)
    scr_ref[0:CONV_HALO, :] = jnp.where(first, 0.0, ah * _sigmoid(gh))
    scr_ref[CONV_HALO:, :] = pa_ref[...].astype(F32) * _sigmoid(pg_ref[...].astype(F32))


def _tap_slabs(win, rows, ahead):
    n = win.shape[0]
    for s in range(8):
        ws = win if s == 0 else pltpu.roll(win, n - s if ahead else s, 0)
        for q in range(CONV_HALO // 8):
            o = 8 * q + s
            if o < CONV_WIDTH:
                start = 8 * q if ahead else CONV_HALO - 8 * q
                yield CONV_WIDTH - 1 - o, ws[start:start + rows]


def _conformer_specs(t):
    tm = _tile(t, (CONV_TM, 128))
    hb = tm // CONV_HALO
    d = D_MODEL
    main = lambda c: pl.BlockSpec((tm, d), lambda i: (i, c))
    halo = lambda c: pl.BlockSpec((CONV_HALO, d), lambda i: (jnp.maximum(i * hb - 1, 0), c))
    row = pl.BlockSpec((1, d), lambda i: (0, 0))
    wsp = pl.BlockSpec((CONV_WIDTH, d), lambda i: (0, 0))
    return tm, main, halo, row, wsp


def _conformer_mid(p, dw_w, dw_b, ln_g, ln_b, name):
    t = p.shape[0]
    tm, main, halo, row, wsp = _conformer_specs(t)
    d, lanes = D_MODEL, CONV_LANES

    def body(pa_ref, pah_ref, pg_ref, pgh_ref, w_ref, b_ref, g_ref, lb_ref, o_ref, dc_ref, scr_ref):
        _glu_window(pa_ref, pah_ref, pg_ref, pgh_ref, scr_ref, pl.program_id(0) == 0)
        for c in range(d // lanes):
            ls = slice(c * lanes, (c + 1) * lanes)
            acc = jnp.broadcast_to(b_ref[:, ls], (tm, lanes))
            for k, slab in _tap_slabs(scr_ref[:, ls], tm, False):
                acc = acc + w_ref[k:k + 1, ls] * slab
            dc_ref[:, ls] = acc

        def norm(r, carry):
            r0 = pl.multiple_of(r * 32, 32)
            dc = dc_ref[pl.ds(r0, 32), :]
            xc = dc - jnp.mean(dc, axis=-1, keepdims=True)
            ln = xc * lax.rsqrt(jnp.mean(xc * xc, axis=-1, keepdims=True) + EPS) * g_ref[...] + lb_ref[...]
            o_ref[pl.ds(r0, 32), :] = (ln * _sigmoid(ln)).astype(BF16)
            return carry

        lax.fori_loop(0, tm // 32, norm, 0)

    return pl.pallas_call(
        body, name=name, grid=(t // tm,), in_specs=[main(0), halo(0), main(1), halo(1), wsp, row, row, row],
        out_specs=[main(0), main(0)], out_shape=[jax.ShapeDtypeStruct((t, d), BF16), jax.ShapeDtypeStruct((t, d), F32)],
        scratch_shapes=[pltpu.VMEM((tm + CONV_HALO, d), F32)], compiler_params=_params("parallel"),
    )(p, p, p, p, dw_w, dw_b, ln_g, ln_b)


def _conformer_mid_bwd(p, dc, ds, ln_g, ln_b, name):
    t = p.shape[0]
    tm, main, halo, row, wsp = _conformer_specs(t)
    d, nt = D_MODEL, t // tm
    rows, lanes = CONV_ROWS, CONV_LANES

    def body(pa_ref, pah_ref, pg_ref, pgh_ref, dc_ref, ds_ref, g_ref, lb_ref,
             ddc_ref, dw_ref, db_ref, dg_ref, dlb_ref, scr_ref, wacc_ref, racc_ref):
        i = pl.program_id(0)

        @pl.when(i == 0)
        def _():
            wacc_ref[...] = jnp.zeros_like(wacc_ref)
            racc_ref[...] = jnp.zeros_like(racc_ref)

        _glu_window(pa_ref, pah_ref, pg_ref, pgh_ref, scr_ref, i == 0)

        def norm_bwd(r, carry):
            r0 = pl.multiple_of(r * 32, 32)
            dcv = dc_ref[pl.ds(r0, 32), :]
            xc = dcv - jnp.mean(dcv, axis=-1, keepdims=True)
            rstd = lax.rsqrt(jnp.mean(xc * xc, axis=-1, keepdims=True) + EPS)
            xhat = xc * rstd
            ln = xhat * g_ref[...] + lb_ref[...]
            sg = _sigmoid(ln)
            dln = ds_ref[pl.ds(r0, 32), :].astype(F32) * (sg * (1.0 + ln * (1.0 - sg)))
            dxh = dln * g_ref[...]
            ddc = rstd * (dxh - jnp.mean(dxh, axis=-1, keepdims=True) - xhat * jnp.mean(dxh * xhat, axis=-1, keepdims=True))
            ddc_ref[pl.ds(r0, 32), :] = ddc
            racc_ref[0] += _fold8(dln * xhat)
            racc_ref[1] += _fold8(dln)
            racc_ref[2] += _fold8(ddc)
            return carry

        lax.fori_loop(0, tm // 32, norm_bwd, 0)

        for c in range(d // lanes):
            ls = slice(c * lanes, (c + 1) * lanes)

            def taps(r, carry, ls=ls):
                r0 = pl.multiple_of(r * rows, rows)
                ddc = ddc_ref[pl.ds(r0, rows), ls]
                for k, slab in _tap_slabs(scr_ref[pl.ds(r0, rows + CONV_HALO), ls], rows, False):
                    wacc_ref[k, :, ls] += _fold8(ddc * slab)
                return carry

            lax.fori_loop(0, tm // rows, taps, 0)

        @pl.when(i == nt - 1)
        def _():
            for k in range(CONV_WIDTH):
                dw_ref[k:k + 1, :] = _colsum(wacc_ref[k])
            dg_ref[...] = _colsum(racc_ref[0])
            dlb_ref[...] = _colsum(racc_ref[1])
            db_ref[...] = _colsum(racc_ref[2])

    return pl.pallas_call(
        body, name=name, grid=(nt,), in_specs=[main(0), halo(0), main(1), halo(1), main(0), main(0), row, row],
        out_specs=[main(0), wsp, row, row, row],
        out_shape=[jax.ShapeDtypeStruct((t, d), F32), jax.ShapeDtypeStruct((CONV_WIDTH, d), F32)]
        + [jax.ShapeDtypeStruct((1, d), F32)] * 3,
        scratch_shapes=[pltpu.VMEM((tm + CONV_HALO, d), F32), pltpu.VMEM((CONV_WIDTH, 8, d), F32), pltpu.VMEM((3, 8, d), F32)],
        compiler_params=_params("arbitrary"),
    )(p, p, p, p, dc, ds, ln_g, ln_b)


def _conformer_glu_bwd(p, ddc, dw_w, name):
    t = p.shape[0]
    d = D_MODEL
    tm = _tile(t, (CONV_TM, 128))
    hb = tm // CONV_HALO
    nt = t // tm
    last_halo = t // CONV_HALO - 1
    rows, lanes = CONV_ROWS, CONV_LANES
    col = lambda c: pl.BlockSpec((tm, d), lambda i: (i, c))
    nxt = pl.BlockSpec((CONV_HALO, d), lambda i: (jnp.minimum((i + 1) * hb, last_halo), 0))

    def body(pa_ref, pg_ref, ddc_ref, ddcn_ref, w_ref, dp_ref, db_ref, scr_ref, acc_ref):
        i = pl.program_id(0)

        @pl.when(i == 0)
        def _():
            acc_ref[...] = jnp.zeros_like(acc_ref)

        scr_ref[0:tm, :] = ddc_ref[...]
        scr_ref[tm:, :] = _zero_if(i == nt - 1, ddcn_ref[...])
        for c in range(d // lanes):
            ls = slice(c * lanes, (c + 1) * lanes)
            gs = slice(d + c * lanes, d + (c + 1) * lanes)

            def taps(r, carry, ls=ls, gs=gs):
                r0 = pl.multiple_of(r * rows, rows)
                dglu = None
                for k, slab in _tap_slabs(scr_ref[pl.ds(r0, rows + CONV_HALO), ls], rows, True):
                    term = w_ref[k:k + 1, ls] * slab
                    dglu = term if dglu is None else dglu + term
                a = pa_ref[pl.ds(r0, rows), ls].astype(F32)
                sg = _sigmoid(pg_ref[pl.ds(r0, rows), ls].astype(F32))
                da = (dglu * sg).astype(BF16)
                dg = (dglu * a * sg * (1.0 - sg)).astype(BF16)
                dp_ref[pl.ds(r0, rows), ls] = da
                dp_ref[pl.ds(r0, rows), gs] = dg
                acc_ref[:, ls] += _fold8(da.astype(F32))
                acc_ref[:, gs] += _fold8(dg.astype(F32))
                return carry

            lax.fori_loop(0, tm // rows, taps, 0)

        @pl.when(i == nt - 1)
        def _():
            db_ref[...] = _colsum(acc_ref[...])

    return pl.pallas_call(
        body, name=name, grid=(nt,),
        in_specs=[col(0), col(1), col(0), nxt, pl.BlockSpec((CONV_WIDTH, d), lambda i: (0, 0))],
        out_specs=[pl.BlockSpec((tm, 2 * d), lambda i: (i, 0)), pl.BlockSpec((1, 2 * d), lambda i: (0, 0))],
        out_shape=[jax.ShapeDtypeStruct((t, 2 * d), BF16), jax.ShapeDtypeStruct((1, 2 * d), F32)],
        scratch_shapes=[pltpu.VMEM((tm + CONV_HALO, d), F32), pltpu.VMEM((8, 2 * d), F32)],
        compiler_params=_params("arbitrary"),
    )(p, p, ddc, ddc, dw_w)


def _colsum_call(a, name):
    t, n = a.shape
    tm = _tile(t)

    def body(a_ref, o_ref):
        @pl.when(pl.program_id(0) == 0)
        def _():
            o_ref[...] = jnp.zeros_like(o_ref)

        o_ref[...] += _colsum(a_ref[...].astype(F32))

    return pl.pallas_call(
        body, name=name, grid=(t // tm,), in_specs=[pl.BlockSpec((tm, n), lambda i: (i, 0))],
        out_specs=pl.BlockSpec((1, n), lambda i: (0, 0)), out_shape=jax.ShapeDtypeStruct((1, n), F32),
        compiler_params=_params("arbitrary"),
    )(a)


def _ada_fwd(c_all, w, name):
    rows, d = c_all.shape
    n = w.shape[1]
    tn = _tile(n, (256, 128))

    def body(c_ref, w_ref, o_ref):
        c = c_ref[...]
        o_ref[...] = _dot((c * _sigmoid(c)).astype(BF16), w_ref[...].astype(BF16), _NN)

    return pl.pallas_call(
        body, name=name, grid=(n // tn,),
        in_specs=[pl.BlockSpec((rows, d), lambda j: (0, 0)), pl.BlockSpec((d, tn), lambda j: (0, j))],
        out_specs=pl.BlockSpec((rows, tn), lambda j: (0, j)), out_shape=jax.ShapeDtypeStruct((rows, n), F32),
        compiler_params=_params("parallel"),
    )(c_all, w)


def _ada_bwd(c_all, dmod, name):
    rows, d = c_all.shape
    n = dmod.shape[1]
    tn = _tile(n, (256, 128))

    def body(c_ref, g_ref, o_ref):
        c = c_ref[...]
        o_ref[...] = _dot((c * _sigmoid(c)).astype(BF16), g_ref[...].astype(BF16), _TN)

    return pl.pallas_call(
        body, name=name, grid=(n // tn,),
        in_specs=[pl.BlockSpec((rows, d), lambda j: (0, 0)), pl.BlockSpec((rows, tn), lambda j: (0, j))],
        out_specs=pl.BlockSpec((d, tn), lambda j: (0, j)), out_shape=jax.ShapeDtypeStruct((d, n), F32),
        compiler_params=_params("parallel"),
    )(c_all, dmod)


def _sum_slots(a, name):
    s, r, c = a.shape
    tr = _row_tile(r, 256)

    def body(a_ref, o_ref):
        acc = a_ref[0].astype(F32)
        for k in range(1, s):
            acc = acc + a_ref[k].astype(F32)
        o_ref[...] = acc

    return pl.pallas_call(
        body, name=name, grid=(r // tr,), in_specs=[pl.BlockSpec((s, tr, c), lambda i: (0, i, 0))],
        out_specs=pl.BlockSpec((tr, c), lambda i: (i, 0)), out_shape=jax.ShapeDtypeStruct((r, c), F32),
        compiler_params=_params("parallel"),
    )(a)


def _sum_with_own(blocks, land, me, name):
    s, r, c = land.shape
    tr = _row_tile(r, 256)
    slot = lambda k: pl.BlockSpec((None, tr, c), lambda i, me_ref: ((me_ref[0] + k) % s, i, 0))

    def body(me_ref, own_ref, *refs):
        o_ref = refs[-1]
        acc = own_ref[...].astype(F32)
        for ref in refs[:-1]:
            acc = acc + ref[...].astype(F32)
        o_ref[...] = acc

    return pl.pallas_call(
        body, name=name, out_shape=jax.ShapeDtypeStruct((r, c), F32),
        grid_spec=pltpu.PrefetchScalarGridSpec(
            num_scalar_prefetch=1, grid=(r // tr,), in_specs=[slot(0)] + [slot(k) for k in range(1, s)],
            out_specs=pl.BlockSpec((tr, c), lambda i, me_ref: (i, 0))),
        compiler_params=_params("parallel"),
    )(me, blocks, *[land] * (s - 1))


def _adamw_update(w, g, m, v):
    nm = ADAM_B1 * m + (1.0 - ADAM_B1) * g
    nv = ADAM_B2 * v + (1.0 - ADAM_B2) * (g * g)
    m_hat = nm * (1.0 / (1.0 - ADAM_B1 ** ADAM_STEP))
    v_hat = nv * (1.0 / (1.0 - ADAM_B2 ** ADAM_STEP))
    return -ADAM_LR * (m_hat / (jnp.sqrt(v_hat) + ADAM_EPS) + ADAM_WD * w), nm, nv


def _adamw(w, g, m, v, name):
    l, r, c = w.shape
    tr = _row_tile(r, 256)
    blk = pl.BlockSpec((None, tr, c), lambda k, i: (k, i, 0))

    def body(w_ref, g_ref, m_ref, v_ref, d_ref, nm_ref, nv_ref):
        d_ref[...], nm_ref[...], nv_ref[...] = _adamw_update(w_ref[...], g_ref[...], m_ref[...], v_ref[...])

    return pl.pallas_call(
        body, name=name, grid=(l, r // tr), in_specs=[blk] * 4, out_specs=[blk] * 3,
        out_shape=[jax.ShapeDtypeStruct(w.shape, F32)] * 3, compiler_params=_params("parallel", "parallel"),
    )(w, g, m, v)


def _adamw_small(ws, gs, ms, vs, name):
    n = len(ws)
    two_d = lambda a: a.reshape(-1, a.shape[-1])

    def body(*refs):
        ins, outs = refs[:4 * n], refs[4 * n:]
        for a in range(n):
            outs[a][...], outs[n + a][...], outs[2 * n + a][...] = _adamw_update(*[ins[k * n + a][...] for k in range(4)])

    res = pl.pallas_call(
        body, name=name, out_shape=[jax.ShapeDtypeStruct(two_d(w).shape, F32) for w in ws] * 3,
    )(*[two_d(a) for a in (*ws, *gs, *ms, *vs)])
    return [[res[k * n + a].reshape(ws[a].shape) for a in range(n)] for k in range(3)]


def _mesh_pos():
    return lax.axis_index("x"), lax.axis_index("y"), lax.axis_index("c")


def _all_gather_vmem(x_shard, name):
    m_per, n = x_shard.shape

    def body(x_ref, out_ref, send_sems, recv_sems, local_sem):
        x, y, c = _mesh_pos()
        me, sibling = (x, y, c), (x, y, 1 - c)
        chips = [(1 - x, y), (x, 1 - y), (1 - x, 1 - y)]

        def rows(px, py, pc):
            return out_ref.at[pl.ds((4 * px + 2 * py + pc) * m_per, m_per), :]

        def copy(k, block, to, src=None):
            return pltpu.make_async_remote_copy(
                src_ref=rows(*block) if src is None else src, dst_ref=rows(*block),
                send_sem=send_sems.at[k], recv_sem=recv_sems.at[k], device_id=to, device_id_type=MESH)

        mine = pltpu.make_async_copy(x_ref, rows(*me), local_sem)
        mine.start()
        first = [copy(0, me, sibling, src=x_ref)]
        first += [copy(1 + j, me, (*chip, c), src=x_ref) for j, chip in enumerate(chips)]
        for cp in first:
            cp.start()
        passed = [copy(4 + j, (*chip, c), sibling) for j, chip in enumerate(chips)]
        for j, chip in enumerate(chips):
            copy(1 + j, (*chip, c), me).wait_recv()
            passed[j].start()
        copy(0, sibling, me).wait_recv()
        for j, chip in enumerate(chips):
            copy(4 + j, (*chip, 1 - c), me).wait_recv()
        for cp in first + passed:
            cp.wait_send()
        mine.wait()

    return pl.pallas_call(
        body, name=name, out_shape=jax.ShapeDtypeStruct((N_DEV * m_per, n), x_shard.dtype),
        in_specs=[pl.BlockSpec(memory_space=pltpu.VMEM)], out_specs=pl.BlockSpec(memory_space=pltpu.VMEM),
        scratch_shapes=[pltpu.SemaphoreType.DMA((7,)), pltpu.SemaphoreType.DMA((7,)), pltpu.SemaphoreType.DMA],
    )(x_shard)


def _all_gather_hbm(shards, name):
    n = len(shards)
    out_shape = [jax.ShapeDtypeStruct((N_DEV,) + s.shape, s.dtype) for s in shards]

    def body(*refs):
        x_refs, out_refs = refs[:n], refs[n:2 * n]
        send_sems, recv_sems, local_sems = refs[2 * n:]
        x, y, c = _mesh_pos()
        me, sibling = (x, y, c), (x, y, 1 - c)
        chips = [(1 - x, y), (x, 1 - y), (1 - x, 1 - y)]

        def blk(a, p):
            return out_refs[a].at[4 * p[0] + 2 * p[1] + p[2]]

        def copy(a, k, block, to, src=None):
            return pltpu.make_async_remote_copy(
                src_ref=blk(a, block) if src is None else src, dst_ref=blk(a, block),
                send_sem=send_sems.at[7 * a + k], recv_sem=recv_sems.at[7 * a + k], device_id=to, device_id_type=MESH)

        mine = [pltpu.make_async_copy(x_refs[a], blk(a, me), local_sems.at[a]) for a in range(n)]
        for cp in mine:
            cp.start()
        first = []
        for a in range(n):
            first.append(copy(a, 0, me, sibling, src=x_refs[a]))
            first += [copy(a, 1 + j, me, (*chip, c), src=x_refs[a]) for j, chip in enumerate(chips)]
        for cp in first:
            cp.start()
        passed = []
        for j, chip in enumerate(chips):
            for a in range(n):
                copy(a, 1 + j, (*chip, c), me).wait_recv()
                fwd = copy(a, 4 + j, (*chip, c), sibling)
                fwd.start()
                passed.append(fwd)
        for a in range(n):
            copy(a, 0, sibling, me).wait_recv()
            for j, chip in enumerate(chips):
                copy(a, 4 + j, (*chip, 1 - c), me).wait_recv()
        for cp in first + passed:
            cp.wait_send()
        for cp in mine:
            cp.wait()

    return pl.pallas_call(
        body, name=name, out_shape=out_shape, in_specs=[pl.BlockSpec(memory_space=pltpu.VMEM)] * n,
        out_specs=[pl.BlockSpec(memory_space=pl.ANY)] * n,
        scratch_shapes=[pltpu.SemaphoreType.DMA((7 * n,)), pltpu.SemaphoreType.DMA((7 * n,)), pltpu.SemaphoreType.DMA((n,))],
    )(*shards)


def _peers(x, y, c):
    flip = lambda v, f: 1 - v if f else v
    return [(flip(x, m & 4), flip(y, m & 2), flip(c, m & 1)) for m in range(1, N_DEV)]


def _dev_index(p):
    return 4 * p[0] + 2 * p[1] + p[2]


def _push_copies(src_refs, land_refs, send_sems, recv_sems, scatter, receive):
    x, y, c = _mesh_pos()
    me = _dev_index((x, y, c))
    copies = []
    for a, (src, land) in enumerate(zip(src_refs, land_refs)):
        for k, p in enumerate(_peers(x, y, c)):
            copies.append(pltpu.make_async_remote_copy(
                src_ref=src.at[_dev_index(p)] if scatter else src, dst_ref=land.at[_dev_index(p) if receive else me],
                send_sem=send_sems.at[7 * a + k], recv_sem=recv_sems.at[7 * a + k], device_id=p, device_id_type=MESH))
    return copies


_HBM = pl.BlockSpec(memory_space=pltpu.HBM)
_SEM = pl.BlockSpec(memory_space=pltpu.SEMAPHORE)
_EFFECT = pltpu.SideEffectType.DATAFLOW_SIDE_EFFECTING


def _pushes_start(srcs, lands, scatter, name):
    n = len(srcs)

    def body(*refs):
        src_refs, land_refs = refs[:n], refs[n:2 * n]
        send_sems, recv_sems = refs[2 * n], refs[2 * n + 1]
        token = refs[-1]
        for cp in _push_copies(src_refs, land_refs, send_sems, recv_sems, scatter, receive=False):
            cp.start()
        token[...] = jnp.zeros_like(token)

    hbm = lambda a: pltpu.HBM(a.shape, a.dtype)
    sems = pltpu.SemaphoreType.DMA((7 * n,))
    outs = pl.pallas_call(
        body, name=name,
        out_shape=(sems, sems, *[hbm(a) for a in srcs], *[hbm(a) for a in lands], jax.ShapeDtypeStruct((8, 128), F32)),
        in_specs=[_HBM] * (2 * n), out_specs=(_SEM, _SEM, *[_HBM] * (2 * n), pl.BlockSpec(memory_space=pltpu.VMEM)),
        input_output_aliases={i: 2 + i for i in range(2 * n)},
        compiler_params=pltpu.CompilerParams(has_side_effects=_EFFECT),
    )(*[pltpu.with_memory_space_constraint(a, pltpu.HBM) for a in (*srcs, *lands)])
    return (outs[0], outs[1], outs[2:2 + n], outs[2 + n:2 + 2 * n], scatter), outs[-1]


def _pushes_wait(handle, after, name):
    send_sems, recv_sems, srcs, lands, scatter = handle
    n = len(srcs)

    def body(*refs):
        src_refs, land_refs = refs[:n], refs[n:2 * n]
        for cp in _push_copies(src_refs, land_refs, refs[2 * n], refs[2 * n + 1], scatter, receive=True):
            cp.wait_send()
            cp.wait_recv()

    hbm = lambda a: pltpu.HBM(a.shape, a.dtype)
    outs = pl.pallas_call(
        body, name=name, out_shape=tuple(hbm(a) for a in (*srcs, *lands)),
        in_specs=[_HBM] * (2 * n) + [_SEM, _SEM, pl.BlockSpec(memory_space=pl.ANY)], out_specs=tuple([_HBM] * (2 * n)),
        input_output_aliases={i: i for i in range(2 * n)},
        compiler_params=pltpu.CompilerParams(has_side_effects=_EFFECT),
    )(*srcs, *lands, send_sems, recv_sems, after)
    return outs[:n], outs[n:]


def _landing_zones(srcs, name):
    n = len(srcs)

    def body(*refs):
        src_refs, land_refs, bufs, sems = refs[:n], refs[n:2 * n], refs[2 * n:3 * n], refs[3 * n]
        me = _dev_index(_mesh_pos())
        load = [pltpu.make_async_copy(src, buf, sems.at[a]) for a, (src, buf) in enumerate(zip(src_refs, bufs))]
        store = [pltpu.make_async_copy(buf, land.at[me], sems.at[a]) for a, (buf, land) in enumerate(zip(bufs, land_refs))]
        for cp in load:
            cp.start()
        for ld, st in zip(load, store):
            ld.wait()
            st.start()
        for cp in store:
            cp.wait()

    any_spec = pl.BlockSpec(memory_space=pl.ANY)
    return pl.pallas_call(
        body, name=name, out_shape=[jax.ShapeDtypeStruct((N_DEV,) + s.shape, s.dtype) for s in srcs],
        in_specs=[any_spec] * n, out_specs=[any_spec] * n,
        scratch_shapes=[pltpu.VMEM(s.shape, s.dtype) for s in srcs] + [pltpu.SemaphoreType.DMA((n,))],
        compiler_params=pltpu.CompilerParams(vmem_limit_bytes=V7X_VMEM_LIMIT),
    )(*srcs)


def _ffn_forward(x, mod, norm_g, w, tag):
    sh, sc, gate = mod
    h = _modnorm(x, norm_g, sc, sh, f"{tag}_norm")
    u = _ffn_up(h, w["up_t"], f"{tag}_up")
    act, z = _ffn_act(u, w["dw_w"], w["dw_b"], f"{tag}_act")
    y, x_new = _matmul(act, w["down"], "nn", F32, f"{tag}_down", resid=(x, gate))
    return x_new, (x, h, u, z, act, y)


def _behind(row, token):
    return row if token is None else row + token[0:1, 0:1]


def _ffn_backward(dx_new, saved, mod, norm_g, w, tag, emit):
    x, h, u, z, act, y = saved
    _, sc, gate = mod
    dy, d_gate = _gate_bwd(dx_new, y, gate, f"{tag}_gate_bwd")
    d_down = _matmul_tn_acc(act, dy, f"{tag}_down_dw")
    dact = _matmul(dy, w["down"], "nt", BF16, f"{tag}_down_dx")
    du, d_dw_w, d_dw_b = _ffn_act_bwd(u, z, dact, w["dw_w"], f"{tag}_act_bwd")
    d_up_t = _matmul_tn_acc(du, h, f"{tag}_up_dw").reshape(2 * FFN_DIM, -1)
    token = emit([d_up_t, d_down])
    dh = _ffn_up_dx(du, w["up_t"], f"{tag}_up_dx")
    dx, d_w, d_sh = _modnorm_bwd(x, dh, norm_g, _behind(sc, token), dx_new, f"{tag}_norm_bwd")
    return dx, dict(dw_w=d_dw_w.transpose(1, 0, 2).reshape(FFN_CONV_WIDTH, 2 * FFN_DIM),
                    dw_b=d_dw_b.reshape(1, 2 * FFN_DIM), norm_g=d_w * (1.0 + sc), sh=d_sh, sc=d_w * norm_g, gate=d_gate)


def _mixer_forward(x, mod, norm_g, w, rope, tag):
    sh, sc, gate = mod
    h = _modnorm(x, norm_g, sc, sh, f"{tag}_norm")
    z = _matmul(h, w["w_in_t"], "nt", BF16, f"{tag}_in")
    ya = _gmlp_fwd(z, w["gain"], w["wtril"], w["bias_exp"], f"{tag}_gmlp")
    q, k, v = _qk_prep(z, rope[0], rope[1], w["gq"], w["gk"], w["seg"], f"{tag}_qk")
    outs, lses = zip(*[_attn_fwd(q[b], k[b], v[b], dil, f"{tag}_attn_d{dil}") for b, dil in enumerate(DILATIONS)])
    yb, lse = _attn_merge(outs, lses, f"{tag}_merge")
    cat = jnp.concatenate([ya, yb[0]], axis=1)
    y, x_new = _matmul(cat, w["w_out"], "nn", F32, f"{tag}_out", resid=(x, gate))
    return x_new, (x, h, z, q, k, v, yb, lse, cat, y)


def _mixer_backward(dx_new, saved, mod, norm_g, w, rope, tag, emit):
    x, h, z, q, k, v, yb, lse, cat, y = saved
    _, sc, gate = mod
    dy, d_gate = _gate_bwd(dx_new, y, gate, f"{tag}_gate_bwd")
    d_w_out = _matmul_tn_acc(cat, dy, f"{tag}_out_dw")
    dcat = _matmul(dy, w["w_out"], "nt", BF16, f"{tag}_out_dx")
    dz_a, d_sp_w, d_gain, d_bias_exp = _gmlp_bwd(z, dcat, w["gain"], w["wtril"], w["wtril_t"], w["bias_exp"], f"{tag}_gmlp_bwd")
    dyb = _subseq_views(dcat, A_WIDTH // B_WIDTH, f"{tag}_dyb_views")
    dqs, dks, dvs = zip(*[_attn_bwd(q[b], k[b], v[b], dyb[b], yb[b], lse[b], dil, f"{tag}_attn_bwd_d{dil}")
                          for b, dil in enumerate(DILATIONS)])
    dz_qkv, d_gq, d_gk = _qk_prep_bwd(z, dqs, dks, dvs, rope[0], rope[1], w["gq"], w["gk"], w["seg"], f"{tag}_qk_bwd")
    dz = jnp.concatenate([dz_a, dz_qkv], axis=1)
    d_w_in_t = _matmul_tn_acc(dz, h, f"{tag}_in_dw")
    token = emit([d_w_in_t, d_w_out])
    dh = _matmul(dz, w["w_in_t"], "nn", F32, f"{tag}_in_dx")
    dx, d_w, d_sh = _modnorm_bwd(x, dh, norm_g, _behind(sc, token), dx_new, f"{tag}_norm_bwd")
    return dx, dict(
        vnorm_g=d_gain.reshape(A_GROUPS, GROUP_DIM), spatial_w=d_sp_w,
        spatial_b=d_bias_exp.reshape(CHUNK, A_GROUPS, GROUP_DIM).sum(-1).T,
        q_norm_g=d_gq.reshape(HEADS, HEAD_DIM).sum(0), k_norm_g=d_gk.reshape(HEADS, HEAD_DIM).sum(0),
        norm_g=d_w * (1.0 + sc), sh=d_sh, sc=d_w * norm_g, gate=d_gate)


def _conformer_forward(x, mod, norm_g, w, tag):
    sh, sc, gate = mod
    h = _modnorm(x, norm_g, sc, sh, f"{tag}_norm")
    p = _matmul(h, w["pw1_t"], "nt", BF16, f"{tag}_pw1", bias=w["pw1_b"])
    s, dc = _conformer_mid(p, w["dw_w"], w["dw_b"], w["ln_g"], w["ln_b"], f"{tag}_mid")
    y, x_new = _matmul(s, w["pw2"], "nn", F32, f"{tag}_pw2", bias=w["pw2_b"], resid=(x, gate))
    return x_new, (x, h, p, dc, s, y)


def _conformer_backward(dx_new, saved, mod, norm_g, w, tag, emit):
    x, h, p, dc, s, y = saved
    _, sc, gate = mod
    dy, d_gate = _gate_bwd(dx_new, y, gate, f"{tag}_gate_bwd")
    d_pw2 = _matmul_tn_acc(s, dy, f"{tag}_pw2_dw")
    d_pw2_b = _colsum_call(dy, f"{tag}_pw2_db")
    ds = _matmul(dy, w["pw2"], "nt", BF16, f"{tag}_pw2_dx")
    ddc, d_dw_w, d_dw_b, d_ln_g, d_ln_b = _conformer_mid_bwd(p, dc, ds, w["ln_g"], w["ln_b"], f"{tag}_mid_bwd")
    dp, d_pw1_b = _conformer_glu_bwd(p, ddc, w["dw_w"], f"{tag}_glu_bwd")
    d_pw1_t = _matmul_tn_acc(dp, h, f"{tag}_pw1_dw")
    token = emit([d_pw1_t, d_pw2])
    dh = _matmul(dp, w["pw1_t"], "nn", F32, f"{tag}_pw1_dx")
    dx, d_w, d_sh = _modnorm_bwd(x, dh, norm_g, _behind(sc, token), dx_new, f"{tag}_norm_bwd")
    return dx, dict(pw1_b=d_pw1_b, dw_w=d_dw_w, dw_b=d_dw_b, ln_g=d_ln_g, ln_b=d_ln_b, pw2_b=d_pw2_b, norm_g=d_w * (1.0 + sc), sh=d_sh, sc=d_w * norm_g, gate=d_gate)


def _local_step(x, target, pos, mod, norm_mix_g, norm_ffn_g, mixer_w, conv_w, ffn_w, fetch, emit):
    d = D_MODEL
    inv_freq = 1.0 / (ROPE_THETA ** (jnp.arange(0, HEAD_DIM, 2, dtype=F32) / HEAD_DIM))
    inv_freq = jnp.tile(inv_freq, 2 * HEADS)[None, :]
    sign = jnp.tile(jnp.concatenate([-jnp.ones(HEAD_DIM // 2, F32), jnp.ones(HEAD_DIM // 2, F32)]), HEADS)[None, :]
    rope = _rope_tables(pos, inv_freq, sign, "rope_tables")
    mods = [[mod[l:l + 1, i * d:(i + 1) * d] for i in range(6)] for l in range(2)]
    mix = [(m[0], m[1], m[2]) for m in mods]
    ffn = [(m[3], m[4], m[5]) for m in mods]
    gm = [norm_mix_g[l:l + 1] for l in range(2)]
    gf = [norm_ffn_g[l:l + 1] for l in range(2)]

    mixer_w = {**mixer_w, **fetch("l0_mix", x)}
    x1, s_mix = _mixer_forward(x, mix[0], gm[0], mixer_w, rope, "l0_mix")
    ffn_w0 = {**ffn_w[0], **fetch("l0_ffn", x1)}
    x2, s_ffn0 = _ffn_forward(x1, ffn[0], gf[0], ffn_w0, "l0_ffn")
    conv_w = {**conv_w, **fetch("l1_conv", x2)}
    x3, s_conv = _conformer_forward(x2, mix[1], gm[1], conv_w, "l1_conv")
    ffn_w1 = {**ffn_w[1], **fetch("l1_ffn", x3)}
    x4, s_ffn1 = _ffn_forward(x3, ffn[1], gf[1], ffn_w1, "l1_ffn")
    dx, loss = _loss_head(x4, target, "loss_head")
    dx, g_ffn1 = _ffn_backward(dx, s_ffn1, ffn[1], gf[1], ffn_w1, "l1_ffn", functools.partial(emit, "l1_ffn"))
    dx, g_conv = _conformer_backward(dx, s_conv, mix[1], gm[1], conv_w, "l1_conv", functools.partial(emit, "l1_conv"))
    dx, g_ffn0 = _ffn_backward(dx, s_ffn0, ffn[0], gf[0], ffn_w0, "l0_ffn", functools.partial(emit, "l0_ffn"))
    dx, g_mix = _mixer_backward(dx, s_mix, mix[0], gm[0], mixer_w, rope, "l0_mix", functools.partial(emit, "l0_mix"))
    blocks = [g_mix, g_ffn0, g_conv, g_ffn1]
    dmod = jnp.stack([jnp.concatenate([a["sh"], a["sc"], a["gate"], b["sh"], b["sc"], b["gate"]], axis=1)[0]
                      for a, b in ((g_mix, g_ffn0), (g_conv, g_ffn1))])
    return loss, dx, dmod, blocks


def _pack(arrs, rows=8):
    flat = jnp.concatenate([a.reshape(-1).astype(F32) for a in arrs])
    n = flat.shape[0]
    cols = -(-n // (rows * 128)) * 128
    return jnp.pad(flat, (0, rows * cols - n)).reshape(rows, cols)


def _unpack(flat, shapes):
    out, off = [], 0
    for shp in shapes:
        n = math.prod(shp)
        out.append(flat[..., off:off + n].reshape(flat.shape[:-1] + tuple(shp)))
        off += n
    return out


def _take_block(a, idx, size, axis):
    return lax.dynamic_slice_in_dim(a, idx * size, size, axis)


def kernel(x, c, positions, ada_w, ada_b, norm_mix_g, norm_ffn_g, ab_w_in, a_vnorm_g, a_spatial_w, a_spatial_b, b_q_norm_g, b_k_norm_g, ab_w_out, conv_pw1_w, conv_pw1_b, conv_dw_w, conv_dw_b, conv_ln_g, conv_ln_b, conv_pw2_w, conv_pw2_b, ffn_up_w, ffn_dw_w, ffn_dw_b, ffn_down_w, loss_target, m_ada_w, m_ada_b, m_norm_mix_g, m_norm_ffn_g, m_ab_w_in, m_a_vnorm_g, m_a_spatial_w, m_a_spatial_b, m_b_q_norm_g, m_b_k_norm_g, m_ab_w_out, m_conv_pw1_w, m_conv_pw1_b, m_conv_dw_w, m_conv_dw_b, m_conv_ln_g, m_conv_ln_b, m_conv_pw2_w, m_conv_pw2_b, m_ffn_up_w, m_ffn_dw_w, m_ffn_dw_b, m_ffn_down_w, v_ada_w, v_ada_b, v_norm_mix_g, v_norm_ffn_g, v_ab_w_in, v_a_vnorm_g, v_a_spatial_w, v_a_spatial_b, v_b_q_norm_g, v_b_k_norm_g, v_ab_w_out, v_conv_pw1_w, v_conv_pw1_b, v_conv_dw_w, v_conv_dw_b, v_conv_ln_g, v_conv_ln_b, v_conv_pw2_w, v_conv_pw2_b, v_ffn_up_w, v_ffn_dw_w, v_ffn_dw_b, v_ffn_down_w):
    weights = dict(ada_w=ada_w, ada_b=ada_b, norm_mix_g=norm_mix_g, norm_ffn_g=norm_ffn_g, ab_w_in=ab_w_in, a_vnorm_g=a_vnorm_g, a_spatial_w=a_spatial_w, a_spatial_b=a_spatial_b, b_q_norm_g=b_q_norm_g, b_k_norm_g=b_k_norm_g, ab_w_out=ab_w_out, conv_pw1_w=conv_pw1_w, conv_pw1_b=conv_pw1_b, conv_dw_w=conv_dw_w, conv_dw_b=conv_dw_b, conv_ln_g=conv_ln_g, conv_ln_b=conv_ln_b, conv_pw2_w=conv_pw2_w, conv_pw2_b=conv_pw2_b, ffn_up_w=ffn_up_w, ffn_dw_w=ffn_dw_w, ffn_dw_b=ffn_dw_b, ffn_down_w=ffn_down_w)
    mom1 = dict(ada_w=m_ada_w, ada_b=m_ada_b, norm_mix_g=m_norm_mix_g, norm_ffn_g=m_norm_ffn_g, ab_w_in=m_ab_w_in, a_vnorm_g=m_a_vnorm_g, a_spatial_w=m_a_spatial_w, a_spatial_b=m_a_spatial_b, b_q_norm_g=m_b_q_norm_g, b_k_norm_g=m_b_k_norm_g, ab_w_out=m_ab_w_out, conv_pw1_w=m_conv_pw1_w, conv_pw1_b=m_conv_pw1_b, conv_dw_w=m_conv_dw_w, conv_dw_b=m_conv_dw_b, conv_ln_g=m_conv_ln_g, conv_ln_b=m_conv_ln_b, conv_pw2_w=m_conv_pw2_w, conv_pw2_b=m_conv_pw2_b, ffn_up_w=m_ffn_up_w, ffn_dw_w=m_ffn_dw_w, ffn_dw_b=m_ffn_dw_b, ffn_down_w=m_ffn_down_w)
    mom2 = dict(ada_w=v_ada_w, ada_b=v_ada_b, norm_mix_g=v_norm_mix_g, norm_ffn_g=v_norm_ffn_g, ab_w_in=v_ab_w_in, a_vnorm_g=v_a_vnorm_g, a_spatial_w=v_a_spatial_w, a_spatial_b=v_a_spatial_b, b_q_norm_g=v_b_q_norm_g, b_k_norm_g=v_b_k_norm_g, ab_w_out=v_ab_w_out, conv_pw1_w=v_conv_pw1_w, conv_pw1_b=v_conv_pw1_b, conv_dw_w=v_conv_dw_w, conv_dw_b=v_conv_dw_b, conv_ln_g=v_conv_ln_g, conv_ln_b=v_conv_ln_b, conv_pw2_w=v_conv_pw2_w, conv_pw2_b=v_conv_pw2_b, ffn_up_w=v_ffn_up_w, ffn_dw_w=v_ffn_dw_w, ffn_dw_b=v_ffn_dw_b, ffn_down_w=v_ffn_down_w)
    order = list(weights)
    d, f2 = D_MODEL, 2 * FFN_DIM
    t = x.shape[1]
    me = 4 * lax.axis_index("x") + 2 * lax.axis_index("y") + lax.axis_index("c")
    for window, dil in PATTERNS:
        assert window // dil == Q_BLOCK and t % (dil * Q_BLOCK) == 0

    small_in = [c[0], conv_pw1_b[0], conv_dw_w[0], conv_dw_b[0], conv_ln_g[0], conv_ln_b[0], conv_pw2_b[0], ffn_dw_w]
    g1 = _all_gather_vmem(_pack(small_in, rows=8), "gather_small").reshape(N_DEV, -1)
    c_all, pw1_b, dw_w, dw_b, ln_g, ln_b, pw2_b, fdw_w = _unpack(g1, [a.shape for a in small_in])
    pw1_b, dw_b, ln_g, ln_b, pw2_b = [a.reshape(1, -1) for a in (pw1_b, dw_b, ln_g, ln_b, pw2_b)]
    dw_w = dw_w.transpose(1, 0, 2).reshape(CONV_WIDTH, d)
    fdw_w = fdw_w.transpose(1, 2, 0, 3).reshape(2, FFN_CONV_WIDTH, f2)

    c16 = jnp.pad(c_all, ((0, 2 * N_DEV - c_all.shape[0]), (0, 0)))
    part = jnp.concatenate([_ada_fwd(c16, ada_w[l], f"ada_fwd{l}")[:N_DEV] for l in range(2)], axis=1)
    g2 = _all_gather_vmem(part, "gather_mod").reshape(N_DEV, N_DEV, 2, -1)
    mod = lax.dynamic_index_in_dim(g2, me, axis=1, keepdims=False).transpose(1, 0, 2).reshape(2, 6 * d) + ada_b

    stages = dict(l0_mix=[ab_w_in[0].T, ab_w_out[0]], l0_ffn=[ffn_up_w[0].T, ffn_down_w[0]],
                  l1_conv=[conv_pw1_w[0].T, conv_pw2_w[0]], l1_ffn=[ffn_up_w[1].T, ffn_down_w[1]])
    stages = {k: [s.astype(BF16) for s in v] for k, v in stages.items()}
    names = dict(l0_mix=("w_in_t", "w_out"), l0_ffn=("up_t", "down"), l1_conv=("pw1_t", "pw2"), l1_ffn=("up_t", "down"))
    ready = {"l0_mix": [a.reshape(-1, d) for a in _all_gather_hbm(stages["l0_mix"], "gather_mixer_weights")]}
    behind = (ready, mod)
    arriving = {}
    for stage, group in (("l0_ffn", ("l0_ffn",)), ("l1_conv", ("l1_conv", "l1_ffn"))):
        srcs, _ = lax.optimization_barrier(([s for g in group for s in stages[g]], behind))
        arriving[stage], behind = _pushes_start(
            srcs, _landing_zones(srcs, f"gather_{stage}_zones"), False, f"gather_{stage}_start")
        mod = mod + behind[0:1, 0:1]

    def fetch(stage, after):
        if stage in arriving:
            full = [a.reshape(-1, d) for a in _pushes_wait(arriving[stage], after, f"gather_{stage}_wait")[1]]
            ready[stage] = full[:2]
            if stage == "l1_conv":
                ready["l1_ffn"] = full[2:]
        return dict(zip(names[stage], ready[stage]))

    causal = jnp.tril(jnp.ones((CHUNK, CHUNK), bool))
    wtril = jnp.where(causal[None], a_spatial_w[0], 0.0)
    mixer_w = dict(
        gain=a_vnorm_g[0].reshape(1, A_WIDTH), wtril=wtril.astype(BF16),
        wtril_t=wtril.transpose(0, 2, 1).astype(BF16),
        bias_exp=jnp.repeat(a_spatial_b[0].T, GROUP_DIM, axis=1),
        gq=jnp.tile(b_q_norm_g[0], HEADS)[None, :], gk=jnp.tile(b_k_norm_g[0], HEADS)[None, :],
        seg=jnp.kron(jnp.eye(HEADS, dtype=BF16), jnp.ones((HEAD_DIM, HEAD_DIM), BF16)))
    conv_w = dict(pw1_b=pw1_b, dw_w=dw_w, dw_b=dw_b, ln_g=ln_g, ln_b=ln_b, pw2_b=pw2_b)
    ffn_w = [dict(dw_w=fdw_w[l].reshape(FFN_CONV_WIDTH, 2, FFN_DIM).transpose(1, 0, 2), dw_b=ffn_dw_b[l].reshape(2, 1, FFN_DIM))
             for l in range(2)]

    leaving = {}

    def emit(stage, grads):
        blocks = [g.reshape(N_DEV, g.shape[0] // N_DEV, d) for g in grads]
        leaving[stage], token = _pushes_start(
            blocks, [lax.empty(b.shape, b.dtype) for b in blocks], True, f"reduce_{stage}_start")
        return token

    loss, dx, dmod, (g_mix, g_ffn0, g_conv, g_ffn1) = _local_step(
        x[0], loss_target[0], positions[0].astype(F32)[:, None], mod, norm_mix_g, norm_ffn_g, mixer_w, conv_w, ffn_w,
        fetch, emit)

    me_op = me.astype(jnp.int32).reshape(1)

    def reduced(stage, after):
        blocks, lands = _pushes_wait(leaving[stage], after, f"reduce_{stage}_wait")
        return [_sum_with_own(b, a, me_op, f"reduce_{stage}_sum{i}") for i, (b, a) in enumerate(zip(blocks, lands))]

    (r_up_t1, r_down1), (r_pw1_t, r_pw2), (r_up_t0, r_down0) = [reduced(s, dx) for s in ("l1_ffn", "l1_conv", "l0_ffn")]

    small_g = [
        dmod, jnp.concatenate([g_mix["norm_g"], g_conv["norm_g"]]), jnp.concatenate([g_ffn0["norm_g"], g_ffn1["norm_g"]]),
        g_mix["vnorm_g"], g_mix["spatial_w"], g_mix["spatial_b"], g_mix["q_norm_g"], g_mix["k_norm_g"],
        g_conv["pw1_b"], g_conv["dw_w"], g_conv["dw_b"], g_conv["ln_g"], g_conv["ln_b"], g_conv["pw2_b"],
        jnp.stack([g_ffn0["dw_w"], g_ffn1["dw_w"]]), jnp.concatenate([g_ffn0["dw_b"], g_ffn1["dw_b"]])]
    packed = _pack(small_g, rows=8)
    g3 = _all_gather_vmem(packed, "gather_small_grads").reshape(N_DEV, 8, -1)
    total = _unpack(_sum_slots(g3, "sum_small_grads").reshape(-1), [a.shape for a in small_g])
    (s_dmod, s_mix_g, s_ffn_g, s_vnorm, s_sp_w, s_sp_b, s_gq, s_gk, s_pw1_b, s_dw_w, s_dw_b, s_ln_g, s_ln_b,
     s_pw2_b, s_fdw_w, s_fdw_b) = total
    dmod_all = g3.reshape(N_DEV, -1)[:, :2 * 6 * d].reshape(N_DEV, 2, 6 * d)
    n_ada = ada_w.shape[2]
    dmod16 = jnp.pad(_take_block(dmod_all, me, n_ada, 2), ((0, N_DEV), (0, 0), (0, 0)))
    g_ada_w = jnp.stack([_ada_bwd(c16, dmod16[:, l], f"ada_bwd{l}") for l in range(2)])

    grads = dict(
        ada_w=g_ada_w, ada_b=s_dmod, norm_mix_g=s_mix_g, norm_ffn_g=s_ffn_g,
        a_vnorm_g=s_vnorm[None], a_spatial_w=s_sp_w[None], a_spatial_b=s_sp_b[None], b_q_norm_g=s_gq[None],
        b_k_norm_g=s_gk[None], conv_pw1_w=r_pw1_t.T[None],
        conv_pw1_b=_take_block(s_pw1_b, me, conv_pw1_b.shape[1], 1),
        conv_dw_w=_take_block(s_dw_w, me, conv_dw_w.shape[2], 1)[None],
        conv_dw_b=_take_block(s_dw_b, me, conv_dw_b.shape[1], 1), conv_ln_g=_take_block(s_ln_g, me, conv_ln_g.shape[1], 1),
        conv_ln_b=_take_block(s_ln_b, me, conv_ln_b.shape[1], 1), conv_pw2_w=r_pw2[None],
        conv_pw2_b=_take_block(s_pw2_b, me, conv_pw2_b.shape[1], 1),
        ffn_up_w=jnp.stack([r_up_t0.T, r_up_t1.T]), ffn_dw_w=_take_block(s_fdw_w, me, ffn_dw_w.shape[2], 2),
        ffn_dw_b=s_fdw_b, ffn_down_w=jnp.stack([r_down0, r_down1]))

    large = ("ada_w", "conv_pw1_w", "conv_pw2_w", "ffn_up_w", "ffn_down_w", "ab_w_in", "ab_w_out")
    delta, new_m, new_v = {}, {}, {}
    for name in large:
        if name == "ab_w_in":
            r_in_t, r_out = reduced("l0_mix", new_v["ffn_down_w"])
            grads.update(ab_w_in=r_in_t.T[None], ab_w_out=r_out[None])
        delta[name], new_m[name], new_v[name] = _adamw(weights[name], grads[name], mom1[name], mom2[name], f"adamw_{name}")
    small = [n for n in order if n not in large]
    res = _adamw_small(*[[src[n] for n in small] for src in (weights, grads, mom1, mom2)], "adamw_small")
    for dst, arrs in zip((delta, new_m, new_v), res):
        dst.update(zip(small, arrs))

    loss = lax.psum(loss[0, 0], ("x", "y", "c"))
    return (loss, dx[None], *[grads[n] for n in order], *[delta[n] for n in order],
            *[new_m[n] for n in order], *[new_v[n] for n in order])
```

```python
import functools
import math

import jax
import jax.numpy as jnp
from jax import lax
from jax.experimental import pallas as pl
from jax.experimental.pallas import tpu as pltpu

F32 = jnp.float32
BF16 = jnp.bfloat16
MESH = pl.DeviceIdType.MESH

D_MODEL = 1024
A_WIDTH = 512
A_GROUPS = 4
GROUP_DIM = 128
CHUNK = 128
B_WIDTH = 512
HEADS = 8
HEAD_DIM = 64
PATTERNS = ((128, 1), (512, 4), (2048, 16))
Q_BLOCK = 128
ROPE_THETA = 10000.0
AB_IN = 2560
CONV_WIDTH = 31
FFN_DIM = 2816
FFN_CONV_WIDTH = 3
EPS = 1e-6
NEG = -1e30
N_DEV = 8
ADAM_LR, ADAM_B1, ADAM_B2, ADAM_EPS, ADAM_WD, ADAM_STEP = 0.001, 0.9, 0.999, 1e-08, 0.01, 10

V7X_VMEM_LIMIT = 56 * 2**20
BF16_ROWS = 16
FFN_HALO = 16
CONV_HALO = 32

_NN = (((1,), (0,)), ((), ()))
_NT = (((1,), (1,)), ((), ()))
_TN = (((0,), (0,)), ((), ()))


def _tile(n, prefs=(512, 256, 128)):
    for t in prefs:
        if n % t == 0:
            return t
    return n


def _row_tile(n, cap=512):
    best = n
    for t in range(8, min(n, cap) + 1, 8):
        if n % t == 0:
            best = t
    return best if best <= cap else n


def _params(*sem):
    return pltpu.CompilerParams(dimension_semantics=sem, vmem_limit_bytes=V7X_VMEM_LIMIT)


def _dot(a, b, dims):
    return lax.dot_general(a, b, dims, preferred_element_type=F32)


def _sigmoid(x):
    return 1.0 / (1.0 + jnp.exp(-x))


def _gelu(x):
    return 0.5 * x * (1.0 + lax.erf(x * (2.0 ** -0.5)))


def _gelu_grad(x):
    return 0.5 * (1.0 + lax.erf(x * (2.0 ** -0.5))) + x * jnp.exp(-0.5 * x * x) * (1.0 / math.sqrt(2.0 * math.pi))


def _colsum(v):
    return jnp.sum(v, axis=0, keepdims=True)


MATMUL_VMEM_BUDGET = 40 * 2**20


def _matmul_tiles(m, n, k, out_bytes, with_resid):
    def options(dim):
        opts = [t for t in (1024, 512, 256, 128) if dim % t == 0]
        return opts + [dim] if dim <= 4096 and dim not in opts else opts

    best = None
    for tm in options(m):
        for tn in options(n):
            need = 4 * (tm * k + k * tn) + tm * tn * (4 + 2 * out_bytes) + (24 * tm * tn if with_resid else 0)
            if need <= MATMUL_VMEM_BUDGET and (best is None or tm * tn / (tm + tn) > best[0]):
                best = (tm * tn / (tm + tn), tm, tn)
    return best[1], best[2]


def _matmul_tn_acc(a, b, name, tk=512):
    squeeze = a.ndim == 2
    a3 = a[None] if squeeze else a
    p_, t, m = a3.shape
    n = b.shape[1]
    nk = t // tk

    def body(a_ref, b_ref, o_ref, acc_ref):
        kt = pl.program_id(1)

        @pl.when(kt == 0)
        def _():
            acc_ref[...] = jnp.zeros_like(acc_ref)

        acc_ref[...] += _dot(a_ref[...], b_ref[...], _TN)

        @pl.when(kt == nk - 1)
        def _():
            o_ref[...] = acc_ref[...].astype(BF16)

    out = pl.pallas_call(
        body, name=name, grid=(p_, nk),
        in_specs=[pl.BlockSpec((None, tk, m), lambda p, kt: (p, kt, 0)), pl.BlockSpec((tk, n), lambda p, kt: (kt, 0))],
        out_specs=pl.BlockSpec((None, m, n), lambda p, kt: (p, 0, 0)), out_shape=jax.ShapeDtypeStruct((p_, m, n), BF16),
        scratch_shapes=[pltpu.VMEM((m, n), F32)], compiler_params=_params("parallel", "arbitrary"),
    )(a3, b)
    return out[0] if squeeze else out


def _matmul(a, b, mode, out_dtype, name, bias=None, resid=None):
    if mode == "nn":
        (m, k), (_, n) = a.shape, b.shape
    elif mode == "nt":
        (m, k), (n, _) = a.shape, b.shape
    else:
        (k, m), (_, n) = a.shape, b.shape
    tm, tn = _matmul_tiles(m, n, k, jnp.dtype(out_dtype).itemsize, resid is not None)
    dims = {"nn": _NN, "nt": _NT, "tn": _TN}[mode]
    a_spec = pl.BlockSpec((k, tm), lambda i, j: (0, i)) if mode == "tn" else pl.BlockSpec((tm, k), lambda i, j: (i, 0))
    b_spec = pl.BlockSpec((tn, k), lambda i, j: (j, 0)) if mode == "nt" else pl.BlockSpec((k, tn), lambda i, j: (0, j))
    in_specs, args = [a_spec, b_spec], [a, b]
    row_spec = pl.BlockSpec((1, tn), lambda i, j: (0, j))
    tile_spec = pl.BlockSpec((tm, tn), lambda i, j: (i, j))
    if bias is not None:
        in_specs.append(row_spec)
        args.append(bias)
    if resid is not None:
        in_specs += [tile_spec, row_spec]
        args += list(resid)
    out_shape = [jax.ShapeDtypeStruct((m, n), out_dtype)]
    out_specs = [tile_spec]
    if resid is not None:
        out_shape.append(jax.ShapeDtypeStruct((m, n), F32))
        out_specs.append(tile_spec)

    def body(*refs):
        a_ref, b_ref = refs[0], refs[1]
        pos = 2
        acc = _dot(a_ref[...], b_ref[...], dims)
        if bias is not None:
            acc = acc + refs[pos][...]
            pos += 1
        if resid is not None:
            x_ref, g_ref = refs[pos], refs[pos + 1]
            pos += 2
        refs[pos][...] = acc.astype(out_dtype)
        if resid is not None:
            refs[pos + 1][...] = x_ref[...] + g_ref[...] * acc

    outs = pl.pallas_call(
        body, name=name, grid=(m // tm, n // tn), in_specs=in_specs, out_specs=out_specs, out_shape=out_shape,
        compiler_params=_params("parallel", "parallel"),
    )(*args)
    return outs if resid is not None else outs[0]


def _modnorm(x, g, sc, sh, name):
    t, d = x.shape
    tm = _tile(t)
    row = pl.BlockSpec((1, d), lambda i: (0, 0))
    blk = pl.BlockSpec((tm, d), lambda i: (i, 0))

    def body(x_ref, g_ref, sc_ref, sh_ref, o_ref):
        x = x_ref[...]
        r = lax.rsqrt(jnp.mean(x * x, axis=-1, keepdims=True) + EPS)
        o_ref[...] = ((x * r) * g_ref[...] * (1.0 + sc_ref[...]) + sh_ref[...]).astype(BF16)

    return pl.pallas_call(
        body, name=name, grid=(t // tm,), in_specs=[blk, row, row, row], out_specs=blk,
        out_shape=jax.ShapeDtypeStruct((t, d), BF16), compiler_params=_params("parallel"),
    )(x, g, sc, sh)


def _modnorm_bwd(x, dh, g, sc, dres, name):
    t, d = x.shape
    tm = _tile(t)
    row = pl.BlockSpec((1, d), lambda i: (0, 0))
    blk = pl.BlockSpec((tm, d), lambda i: (i, 0))

    def body(x_ref, dh_ref, g_ref, sc_ref, dres_ref, dx_ref, dw_ref, dsh_ref):
        @pl.when(pl.program_id(0) == 0)
        def _():
            dw_ref[...] = jnp.zeros_like(dw_ref)
            dsh_ref[...] = jnp.zeros_like(dsh_ref)

        x = x_ref[...]
        dh = dh_ref[...].astype(F32)
        r = lax.rsqrt(jnp.mean(x * x, axis=-1, keepdims=True) + EPS)
        xn = x * r
        dxn = dh * (g_ref[...] * (1.0 + sc_ref[...]))
        dx_ref[...] = dres_ref[...] + r * (dxn - xn * jnp.mean(dxn * xn, axis=-1, keepdims=True))
        dw_ref[...] += _colsum(dh * xn)
        dsh_ref[...] += _colsum(dh)

    return pl.pallas_call(
        body, name=name, grid=(t // tm,), in_specs=[blk, blk, row, row, blk], out_specs=[blk, row, row],
        out_shape=[jax.ShapeDtypeStruct((t, d), F32), jax.ShapeDtypeStruct((1, d), F32), jax.ShapeDtypeStruct((1, d), F32)],
        compiler_params=_params("arbitrary"),
    )(x, dh, g, sc, dres)


def _gate_bwd(dxn, y, gate, name):
    t, d = dxn.shape
    tm = _tile(t)
    row = pl.BlockSpec((1, d), lambda i: (0, 0))
    blk = pl.BlockSpec((tm, d), lambda i: (i, 0))

    def body(dxn_ref, y_ref, g_ref, dy_ref, dg_ref):
        @pl.when(pl.program_id(0) == 0)
        def _():
            dg_ref[...] = jnp.zeros_like(dg_ref)

        dxn = dxn_ref[...]
        dy_ref[...] = (dxn * g_ref[...]).astype(BF16)
        dg_ref[...] += _colsum(dxn * y_ref[...])

    return pl.pallas_call(
        body, name=name, grid=(t // tm,), in_specs=[blk, blk, row], out_specs=[blk, row],
        out_shape=[jax.ShapeDtypeStruct((t, d), BF16), jax.ShapeDtypeStruct((1, d), F32)],
        compiler_params=_params("arbitrary"),
    )(dxn, y, gate)


def _loss_head(y, target, name):
    t, d = y.shape
    tm = _tile(t)
    blk = pl.BlockSpec((tm, d), lambda i: (i, 0))
    one = pl.BlockSpec((1, 1), lambda i: (0, 0))

    def body(y_ref, t_ref, dy_ref, loss_ref, acc_ref):
        @pl.when(pl.program_id(0) == 0)
        def _():
            acc_ref[...] = jnp.zeros_like(acc_ref)

        e = y_ref[...] - t_ref[...]
        dy_ref[...] = e * (1.0 / d)
        acc_ref[...] += _colsum(e * e)

        @pl.when(pl.program_id(0) == pl.num_programs(0) - 1)
        def _():
            loss_ref[...] = jnp.sum(acc_ref[...], axis=1, keepdims=True) * (0.5 / d)

    return pl.pallas_call(
        body, name=name, grid=(t // tm,), in_specs=[blk, blk], out_specs=[blk, one],
        out_shape=[jax.ShapeDtypeStruct((t, d), F32), jax.ShapeDtypeStruct((1, 1), F32)],
        scratch_shapes=[pltpu.VMEM((1, d), F32)], compiler_params=_params("arbitrary"),
    )(y, target)


def _group_norm(vg, gain):
    mu = jnp.mean(vg, axis=-1, keepdims=True)
    xc = vg - mu
    rstd = lax.rsqrt(jnp.mean(xc * xc, axis=-1, keepdims=True) + EPS)
    xhat = xc * rstd
    return xhat, rstd, xhat * gain


def _gmlp_fwd(z, gain, wtril, bias_exp, name):
    t = z.shape[0]
    zu = pl.BlockSpec((CHUNK, A_WIDTH), lambda i: (i, 0))
    zv = pl.BlockSpec((CHUNK, A_WIDTH), lambda i: (i, 1))
    full2 = lambda shp: pl.BlockSpec(shp, lambda i: (0, 0))
    w_spec = pl.BlockSpec((A_GROUPS, CHUNK, CHUNK), lambda i: (0, 0, 0))

    def body(zu_ref, zv_ref, gain_ref, w_ref, b_ref, ya_ref):
        ua = _gelu(zu_ref[...].astype(F32))
        vg = _gelu(zv_ref[...].astype(F32))
        for g in range(A_GROUPS):
            sl = slice(g * GROUP_DIM, (g + 1) * GROUP_DIM)
            _, _, vn = _group_norm(vg[:, sl], gain_ref[:, sl])
            f = _dot(w_ref[g], vn.astype(BF16), _NN) + b_ref[:, sl]
            ya_ref[:, sl] = (ua[:, sl] * f).astype(BF16)

    return pl.pallas_call(
        body, name=name, grid=(t // CHUNK,),
        in_specs=[zu, zv, full2((1, A_WIDTH)), w_spec, full2((CHUNK, A_WIDTH))], out_specs=zu,
        out_shape=jax.ShapeDtypeStruct((t, A_WIDTH + B_WIDTH), BF16), compiler_params=_params("parallel"),
    )(z, z, gain, wtril, bias_exp)


def _gmlp_bwd(z, dcat, gain, wtril, wtril_t, bias_exp, name):
    t = z.shape[0]
    zu = pl.BlockSpec((CHUNK, A_WIDTH), lambda i: (i, 0))
    zv = pl.BlockSpec((CHUNK, A_WIDTH), lambda i: (i, 1))
    full2 = lambda shp: pl.BlockSpec(shp, lambda i: (0, 0))
    w_spec = pl.BlockSpec((A_GROUPS, CHUNK, CHUNK), lambda i: (0, 0, 0))
    dz_spec = pl.BlockSpec((CHUNK, 2 * A_WIDTH), lambda i: (i, 0))

    def body(zu_ref, zv_ref, dya_ref, gain_ref, w_ref, wt_ref, b_ref, dz_ref, dw_ref, dgain_ref, dbias_ref):
        @pl.when(pl.program_id(0) == 0)
        def _():
            dw_ref[...] = jnp.zeros_like(dw_ref)
            dgain_ref[...] = jnp.zeros_like(dgain_ref)
            dbias_ref[...] = jnp.zeros_like(dbias_ref)

        zu_v = zu_ref[...].astype(F32)
        zv_v = zv_ref[...].astype(F32)
        dya = dya_ref[...].astype(F32)
        ua = _gelu(zu_v)
        vg = _gelu(zv_v)
        row = lax.broadcasted_iota(jnp.int32, (CHUNK, CHUNK), 0)
        col = lax.broadcasted_iota(jnp.int32, (CHUNK, CHUNK), 1)
        for g in range(A_GROUPS):
            sl = slice(g * GROUP_DIM, (g + 1) * GROUP_DIM)
            gain_g = gain_ref[:, sl]
            xhat, rstd, vn = _group_norm(vg[:, sl], gain_g)
            vn16 = vn.astype(BF16)
            f = _dot(w_ref[g], vn16, _NN) + b_ref[:, sl]
            df = dya[:, sl] * ua[:, sl]
            df16 = df.astype(BF16)
            dz_ref[:, sl] = (dya[:, sl] * f * _gelu_grad(zu_v[:, sl])).astype(BF16)
            dw_ref[g] += jnp.where(row >= col, _dot(df16, vn16, _NT), 0.0)
            dvn = _dot(wt_ref[g], df16, _NN)
            dgain_ref[:, sl] += _colsum(dvn * xhat)
            dxh = dvn * gain_g
            dvg = rstd * (dxh - jnp.mean(dxh, axis=-1, keepdims=True) - xhat * jnp.mean(dxh * xhat, axis=-1, keepdims=True))
            dz_ref[:, A_WIDTH + g * GROUP_DIM:A_WIDTH + (g + 1) * GROUP_DIM] = (dvg * _gelu_grad(zv_v[:, sl])).astype(BF16)
            dbias_ref[:, sl] += df

    return pl.pallas_call(
        body, name=name, grid=(t // CHUNK,),
        in_specs=[zu, zv, zu, full2((1, A_WIDTH)), w_spec, w_spec, full2((CHUNK, A_WIDTH))],
        out_specs=[dz_spec, w_spec, full2((1, A_WIDTH)), full2((CHUNK, A_WIDTH))],
        out_shape=[jax.ShapeDtypeStruct((t, 2 * A_WIDTH), BF16), jax.ShapeDtypeStruct((A_GROUPS, CHUNK, CHUNK), F32),
                   jax.ShapeDtypeStruct((1, A_WIDTH), F32), jax.ShapeDtypeStruct((CHUNK, A_WIDTH), F32)],
        compiler_params=_params("arbitrary"),
    )(z, z, dcat, gain, wtril, wtril_t, bias_exp)


def _rope_tables(pos, inv_freq, sign, name):
    t = pos.shape[0]
    tm = _tile(t)
    row = pl.BlockSpec((1, B_WIDTH), lambda i: (0, 0))
    blk = pl.BlockSpec((tm, B_WIDTH), lambda i: (i, 0))

    def body(pos_ref, f_ref, s_ref, cos_ref, sin_ref):
        ang = pos_ref[...] * f_ref[:, 0:LANES]
        cos_ref[...] = jnp.tile(jnp.cos(ang), (1, B_WIDTH // LANES))
        sin_ref[...] = jnp.tile(jnp.sin(ang) * s_ref[:, 0:LANES], (1, B_WIDTH // LANES))

    return pl.pallas_call(
        body, name=name, grid=(t // tm,), in_specs=[pl.BlockSpec((tm, 1), lambda i: (i, 0)), row, row],
        out_specs=[blk, blk], out_shape=[jax.ShapeDtypeStruct((t, B_WIDTH), F32)] * 2,
        compiler_params=_params("parallel"),
    )(pos, inv_freq, sign)


def _head_sum(v, seg):
    hi = v.astype(BF16)
    lo = (v - hi.astype(F32)).astype(BF16)
    return _dot(hi, seg, _NN) + _dot(lo, seg, _NN)


def _swap_halves(v):
    lane = lax.broadcasted_iota(jnp.int32, v.shape, 1)
    return jnp.where((lane & (HEAD_DIM - 1)) < HEAD_DIM // 2,pltpu.roll(v, B_WIDTH - HEAD_DIM // 2, 1), pltpu.roll(v, HEAD_DIM // 2, 1))


DILATIONS = tuple(dil for _, dil in PATTERNS)
SUBSEQ_TM = 256
LANES = 128


def _subseq_shape(t, dil):
    return (t // dil, dil * B_WIDTH)


def _subseq_spec(tm, dil):
    return pl.BlockSpec((tm // dil, dil * B_WIDTH), lambda i: (i, 0))


def _to_subseq(x, scr_ref, dil):
    if dil == 1:
        return x
    tm, w = x.shape
    for c in range(w // LANES):
        scr_ref[c * tm:(c + 1) * tm, :] = x[:, c * LANES:(c + 1) * LANES]
    return jnp.concatenate([scr_ref[pl.ds(c * tm + r, tm // dil, stride=dil), :]
                            for r in range(dil) for c in range(w // LANES)], axis=1)


def _from_subseq(y, scr_ref, dil):
    if dil == 1:
        return y
    n, w = y.shape[0], y.shape[1] // dil
    tm = n * dil
    for r in range(dil):
        for c in range(w // LANES):
            scr_ref[pl.ds(c * tm + r, n, stride=dil), :] = y[:, r * w + c * LANES:r * w + (c + 1) * LANES]
    return jnp.concatenate([scr_ref[c * tm:(c + 1) * tm, :] for c in range(w // LANES)], axis=1)


def _subseq_scratch(tm):
    return pltpu.VMEM((B_WIDTH // LANES * tm, LANES), F32)


def _qk_prep(z, cos_t, sin_t, gq, gk, seg, name):
    t = z.shape[0]
    tm = _tile(t, (SUBSEQ_TM,))
    col = lambda c: pl.BlockSpec((tm, B_WIDTH), lambda i: (i, c))
    row = pl.BlockSpec((1, B_WIDTH), lambda i: (0, 0))
    blk = col(0)
    nd = len(DILATIONS)

    def body(q_ref, k_ref, v_ref, cos_ref, sin_ref, gq_ref, gk_ref, seg_ref, *rest):
        out_refs, scr_ref = rest[:-1], rest[-1]

        def norm_rot(x, g):
            r = lax.rsqrt(_head_sum(x * x, seg_ref[...]) * (1.0 / HEAD_DIM) + EPS)
            xn = x * r * g
            return xn * cos_ref[...] + _swap_halves(xn) * sin_ref[...]

        vals = (norm_rot(q_ref[...].astype(F32), gq_ref[...]), norm_rot(k_ref[...].astype(F32), gk_ref[...]),
                v_ref[...].astype(F32))
        for a, val in enumerate(vals):
            for b, dil in enumerate(DILATIONS):
                out_refs[a * nd + b][...] = _to_subseq(val, scr_ref, dil).astype(BF16)

    outs = pl.pallas_call(
        body, name=name, grid=(t // tm,),
        in_specs=[col(2), col(3), col(4), blk, blk, row, row, pl.BlockSpec((B_WIDTH, B_WIDTH), lambda i: (0, 0))],
        out_specs=[_subseq_spec(tm, dil) for _ in range(3) for dil in DILATIONS],
        out_shape=[jax.ShapeDtypeStruct(_subseq_shape(t, dil), BF16) for _ in range(3) for dil in DILATIONS],
        scratch_shapes=[_subseq_scratch(tm)], compiler_params=_params("parallel"),
    )(z, z, z, cos_t, sin_t, gq, gk, seg)
    return outs[:nd], outs[nd:2 * nd], outs[2 * nd:]


def _qk_prep_bwd(z, dqs, dks, dvs, cos_t, sin_t, gq, gk, seg, name):
    t = z.shape[0]
    tm = _tile(t, (SUBSEQ_TM,))
    col = lambda c: pl.BlockSpec((tm, B_WIDTH), lambda i: (i, c))
    row = pl.BlockSpec((1, B_WIDTH), lambda i: (0, 0))
    blk = col(0)
    nb = len(DILATIONS)
    subs = [_subseq_spec(tm, dil) for dil in DILATIONS]

    def body(*refs):
        q_ref, k_ref = refs[0], refs[1]
        dq_refs, dk_refs, dv_refs = refs[2:2 + nb], refs[2 + nb:2 + 2 * nb], refs[2 + 2 * nb:2 + 3 * nb]
        cos_ref, sin_ref, gq_ref, gk_ref, seg_ref, dz_ref, dgq_ref, dgk_ref, scr_ref = refs[2 + 3 * nb:]

        @pl.when(pl.program_id(0) == 0)
        def _():
            dgq_ref[...] = jnp.zeros_like(dgq_ref)
            dgk_ref[...] = jnp.zeros_like(dgk_ref)

        def total(d_refs):
            return sum(_from_subseq(r_[...], scr_ref, dil) for r_, dil in zip(d_refs, DILATIONS))

        def back(x, d_refs, g, dg_ref):
            dout = total(d_refs)
            dy = dout * cos_ref[...] + _swap_halves(dout * sin_ref[...])
            r = lax.rsqrt(_head_sum(x * x, seg_ref[...]) * (1.0 / HEAD_DIM) + EPS)
            xn = x * r
            dg_ref[...] += _colsum(dy * xn)
            dxn = dy * g
            return r * (dxn - xn * (_head_sum(dxn * xn, seg_ref[...]) * (1.0 / HEAD_DIM)))

        dz_ref[:, 0:B_WIDTH] = back(q_ref[...].astype(F32), dq_refs, gq_ref[...], dgq_ref).astype(BF16)
        dz_ref[:, B_WIDTH:2 * B_WIDTH] = back(k_ref[...].astype(F32), dk_refs, gk_ref[...], dgk_ref).astype(BF16)
        dz_ref[:, 2 * B_WIDTH:3 * B_WIDTH] = total(dv_refs).astype(BF16)

    return pl.pallas_call(
        body, name=name, grid=(t // tm,),
        in_specs=[col(2), col(3)] + subs * 3 + [blk, blk, row, row, pl.BlockSpec((B_WIDTH, B_WIDTH), lambda i: (0, 0))],
        out_specs=[pl.BlockSpec((tm, 3 * B_WIDTH), lambda i: (i, 0)), row, row],
        out_shape=[jax.ShapeDtypeStruct((t, 3 * B_WIDTH), BF16), jax.ShapeDtypeStruct((1, B_WIDTH), F32),
                   jax.ShapeDtypeStruct((1, B_WIDTH), F32)],
        scratch_shapes=[_subseq_scratch(tm)], compiler_params=_params("arbitrary"),
    )(z, z, *dqs, *dks, *dvs, cos_t, sin_t, gq, gk, seg)


def _subseq_views(x, col, name):
    t = x.shape[0]
    tm = _tile(t, (SUBSEQ_TM,))

    def body(x_ref, *rest):
        out_refs, scr_ref = rest[:-1], rest[-1]
        val = x_ref[...].astype(F32)
        for o_ref, dil in zip(out_refs, DILATIONS):
            o_ref[...] = _to_subseq(val, scr_ref, dil).astype(o_ref.dtype)

    return pl.pallas_call(
        body, name=name, grid=(t // tm,), in_specs=[pl.BlockSpec((tm, B_WIDTH), lambda i: (i, col))],
        out_specs=[_subseq_spec(tm, dil) for dil in DILATIONS],
        out_shape=[jax.ShapeDtypeStruct(_subseq_shape(t, dil), x.dtype) for dil in DILATIONS],
        scratch_shapes=[_subseq_scratch(tm)], compiler_params=_params("parallel"),
    )(x)


def _attn_fwd(q, k, v, dil, name):
    t = q.shape[0] * dil
    nb = t // dil // Q_BLOCK
    cur = pl.BlockSpec((Q_BLOCK, B_WIDTH), lambda r, i: (i, r))
    prev = pl.BlockSpec((Q_BLOCK, B_WIDTH), lambda r, i: (jnp.maximum(i - 1, 0), r))

    def body(q_ref, kp_ref, kc_ref, vp_ref, vc_ref, o_ref, lse_ref):
        i = pl.program_id(1)
        q = q_ref[...]
        kk = jnp.concatenate([kp_ref[...], kc_ref[...]], axis=0)
        vv = jnp.concatenate([vp_ref[...], vc_ref[...]], axis=0)
        a = lax.broadcasted_iota(jnp.int32, (Q_BLOCK, 2 * Q_BLOCK), 0)
        j = lax.broadcasted_iota(jnp.int32, (Q_BLOCK, 2 * Q_BLOCK), 1)
        dist = a + Q_BLOCK - j
        mask = (dist >= 0) & (dist <= Q_BLOCK) & ((j >= Q_BLOCK) | (i > 0))
        for h in range(HEADS):
            sl = slice(h * HEAD_DIM, (h + 1) * HEAD_DIM)
            s = jnp.where(mask, _dot(q[:, sl], kk[:, sl], _NT) * (HEAD_DIM ** -0.5), NEG)
            m = jnp.max(s, axis=-1, keepdims=True)
            p = jnp.exp(s - m)
            den = jnp.sum(p, axis=-1, keepdims=True)
            o_ref[:, sl] = _dot(p.astype(BF16), vv[:, sl], _NN) / den
            lse_ref[:, sl] = jnp.broadcast_to(m + jnp.log(den), (Q_BLOCK, HEAD_DIM))

    return pl.pallas_call(
        body, name=name, grid=(dil, nb), in_specs=[cur, prev, cur, prev, cur], out_specs=[cur, cur],
        out_shape=[jax.ShapeDtypeStruct(_subseq_shape(t, dil), F32)] * 2,
        compiler_params=_params("parallel", "parallel"),
    )(q, k, k, v, v)


def _attn_merge(outs, lses, cat, name):
    nb = len(DILATIONS)
    t = cat.shape[0]
    tm = _tile(t, (SUBSEQ_TM,))
    subs = [_subseq_spec(tm, dil) for dil in DILATIONS]

    def body(*refs):
        o_refs, l_refs = refs[:nb], refs[nb:2 * nb]
        yb_refs, lse_refs, cat_ref, scr_ref = refs[2 * nb + 1:3 * nb + 1], refs[3 * nb + 1:4 * nb + 1], refs[4 * nb + 1], refs[4 * nb + 2]
        ls = [_from_subseq(r[...], scr_ref, dil) for r, dil in zip(l_refs, DILATIONS)]
        m = functools.reduce(jnp.maximum, ls)
        tot = m + jnp.log(sum(jnp.exp(l - m) for l in ls))
        yb = sum(jnp.exp(l - tot) * _from_subseq(o[...], scr_ref, dil) for l, o, dil in zip(ls, o_refs, DILATIONS))
        cat_ref[...] = yb.astype(BF16)
        yb = yb.astype(BF16).astype(F32)
        for yb_ref, lse_ref, dil in zip(yb_refs, lse_refs, DILATIONS):
            yb_ref[...] = _to_subseq(yb, scr_ref, dil).astype(BF16)
            lse_ref[...] = _to_subseq(tot, scr_ref, dil)

    outs_ = pl.pallas_call(
        body, name=name, grid=(t // tm,), in_specs=subs * 2 + [pl.BlockSpec(memory_space=pl.ANY)],
        out_specs=subs * 2 + [pl.BlockSpec((tm, B_WIDTH), lambda i: (i, A_WIDTH // B_WIDTH))],
        out_shape=[jax.ShapeDtypeStruct(_subseq_shape(t, dil), BF16) for dil in DILATIONS]
        + [jax.ShapeDtypeStruct(_subseq_shape(t, dil), F32) for dil in DILATIONS] + [jax.ShapeDtypeStruct(cat.shape, BF16)],
        input_output_aliases={2 * nb: 2 * nb}, scratch_shapes=[_subseq_scratch(tm)], compiler_params=_params("parallel"),
    )(*outs, *lses, cat)
    return outs_[:nb], outs_[nb:2 * nb], outs_[2 * nb]


def _attn_bwd(q, k, v, do, o, lse, dil, name):
    t = q.shape[0] * dil
    nb = t // dil // Q_BLOCK
    blk = lambda f: pl.BlockSpec((Q_BLOCK, B_WIDTH), lambda r, i: (f(i), r))
    cur = blk(lambda i: jnp.minimum(i, nb - 1))
    prev = blk(lambda i: jnp.clip(i - 1, 0, nb - 1))
    scale = HEAD_DIM ** -0.5

    def body(q_ref, kp_ref, kc_ref, vp_ref, vc_ref, do_ref, o_ref, lse_ref, dq_ref, dk_ref, dv_ref,
             ck_ref, cv_ref, tk_ref, tv_ref):
        i = pl.program_id(1)

        @pl.when(i == 0)
        def _():
            ck_ref[...] = jnp.zeros_like(ck_ref)
            cv_ref[...] = jnp.zeros_like(cv_ref)

        @pl.when(i < nb)
        def _():
            q = q_ref[...]
            kk = jnp.concatenate([kp_ref[...], kc_ref[...]], axis=0)
            vv = jnp.concatenate([vp_ref[...], vc_ref[...]], axis=0)
            do = do_ref[...]
            dof = do.astype(F32)
            of = o_ref[...].astype(F32)
            a = lax.broadcasted_iota(jnp.int32, (Q_BLOCK, 2 * Q_BLOCK), 0)
            j = lax.broadcasted_iota(jnp.int32, (Q_BLOCK, 2 * Q_BLOCK), 1)
            dist = a + Q_BLOCK - j
            mask = (dist >= 0) & (dist <= Q_BLOCK) & ((j >= Q_BLOCK) | (i > 0))
            for h in range(HEADS):
                sl = slice(h * HEAD_DIM, (h + 1) * HEAD_DIM)
                s = jnp.where(mask, _dot(q[:, sl], kk[:, sl], _NT) * scale, NEG)
                p = jnp.exp(s - lse_ref[:, h * HEAD_DIM:h * HEAD_DIM + 1])
                dp = _dot(do[:, sl], vv[:, sl], _NT)
                delta = jnp.sum(dof[:, sl] * of[:, sl], axis=-1, keepdims=True)
                ds = (p * (dp - delta) * scale).astype(BF16)
                dq_ref[:, sl] = _dot(ds, kk[:, sl], _NN)
                dv_t = _dot(do[:, sl], p.astype(BF16), _TN)
                dk_t = _dot(q[:, sl], ds, _TN)
                tk_ref[sl, :] = ck_ref[sl, :] + dk_t[:, :Q_BLOCK]
                tv_ref[sl, :] = cv_ref[sl, :] + dv_t[:, :Q_BLOCK]
                ck_ref[sl, :] = dk_t[:, Q_BLOCK:]
                cv_ref[sl, :] = dv_t[:, Q_BLOCK:]

        @pl.when(i == nb)
        def _():
            tk_ref[...] = ck_ref[...]
            tv_ref[...] = cv_ref[...]

        @pl.when(i >= 1)
        def _():
            dk_ref[...] = tk_ref[...].T
            dv_ref[...] = tv_ref[...].T

    return pl.pallas_call(
        body, name=name, grid=(dil, nb + 1), in_specs=[cur, prev, cur, prev, cur, cur, cur, cur],
        out_specs=[cur, prev, prev], out_shape=[jax.ShapeDtypeStruct(_subseq_shape(t, dil), F32)] * 3,
        scratch_shapes=[pltpu.VMEM((B_WIDTH, Q_BLOCK), F32)] * 4,
        compiler_params=_params("parallel", "arbitrary"),
    )(q, k, k, v, v, do, o, lse)


FFN_TN = 256
FFN_FWD_CHUNK = 256
FFN_BWD_CHUNK = 128


def _ffn_up(h, up_t, name):
    t, k = h.shape
    tm = _tile(t)

    def body(h_ref, w_ref, o_ref):
        o_ref[...] = _dot(h_ref[...], w_ref[...], _NT).astype(BF16)

    return pl.pallas_call(
        body, name=name, grid=(2, t // tm),
        in_specs=[pl.BlockSpec((tm, k), lambda p, i: (i, 0)), pl.BlockSpec((None, FFN_DIM, k), lambda p, i: (p, 0, 0))],
        out_specs=pl.BlockSpec((None, tm, FFN_DIM), lambda p, i: (p, i, 0)),
        out_shape=jax.ShapeDtypeStruct((2, t, FFN_DIM), BF16), compiler_params=_params("parallel", "parallel"),
    )(h, up_t.reshape(2, FFN_DIM, k))


def _ffn_up_dx(du, up_t, name):
    t = du.shape[1]
    k = up_t.shape[1]
    tm = _tile(t)

    def body(a_ref, b_ref, o_ref):
        o_ref[...] = _dot(a_ref[0], b_ref[0], _NN) + _dot(a_ref[1], b_ref[1], _NN)

    return pl.pallas_call(
        body, name=name, grid=(t // tm,),
        in_specs=[pl.BlockSpec((2, tm, FFN_DIM), lambda i: (0, i, 0)), pl.BlockSpec((2, FFN_DIM, k), lambda i: (0, 0, 0))],
        out_specs=pl.BlockSpec((tm, k), lambda i: (i, 0)), out_shape=jax.ShapeDtypeStruct((t, k), F32),
        compiler_params=_params("parallel"),
    )(du, up_t.reshape(2, FFN_DIM, k))


def _ffn_conv(win, w_ref, b_ref, p):
    x = win.astype(F32)
    x0, x1, x2 = x[FFN_HALO:], pltpu.roll(x, 1, 0)[FFN_HALO:], pltpu.roll(x, 2, 0)[FFN_HALO:]
    return b_ref[p] + w_ref[p, 2:3, :] * x0 + w_ref[p, 1:2, :] * x1 + w_ref[p, 0:1, :] * x2


def _zero_if(cond, v):
    return jnp.where(cond, 0, v).astype(v.dtype)


def _ffn_act(u, dw_w, dw_b, name):
    t = u.shape[1]
    tm = _tile(t)
    chunk = min(FFN_FWD_CHUNK, tm)
    hb = tm // FFN_HALO
    main = pl.BlockSpec((2, tm, FFN_TN), lambda i, j: (0, i, j))
    halo = pl.BlockSpec((2, FFN_HALO, FFN_TN), lambda i, j: (0, jnp.maximum(i * hb - 1, 0), j))
    wsp = pl.BlockSpec((2, FFN_CONV_WIDTH, FFN_TN), lambda i, j: (0, 0, j))
    bsp = pl.BlockSpec((2, 1, FFN_TN), lambda i, j: (0, 0, j))

    def body(u_ref, uh_ref, w_ref, b_ref, o_ref, z_ref):
        first = pl.program_id(0) == 0

        def emit(rows, wins):
            za, zb = _ffn_conv(wins[0], w_ref, b_ref, 0), _ffn_conv(wins[1], w_ref, b_ref, 1)
            o_ref[rows, :] = (za * _sigmoid(za) * zb).astype(BF16)
            z_ref[0, rows, :] = za.astype(BF16)
            z_ref[1, rows, :] = zb.astype(BF16)

        emit(pl.ds(0, chunk), [jnp.concatenate([_zero_if(first, uh_ref[p]), u_ref[p, 0:chunk, :]], axis=0) for p in range(2)])

        def step(c, carry):
            s = pl.multiple_of(c * chunk, chunk)
            emit(pl.ds(s, chunk), [u_ref[p, pl.ds(s - FFN_HALO, chunk + FFN_HALO), :] for p in range(2)])
            return carry

        lax.fori_loop(1, tm // chunk, step, 0)

    return pl.pallas_call(
        body, name=name, grid=(t // tm, FFN_DIM // FFN_TN), in_specs=[main, halo, wsp, bsp],
        out_specs=[pl.BlockSpec((tm, FFN_TN), lambda i, j: (i, j)), main],
        out_shape=[jax.ShapeDtypeStruct((t, FFN_DIM), BF16), jax.ShapeDtypeStruct((2, t, FFN_DIM), BF16)],
        compiler_params=_params("parallel", "parallel"),
    )(u, u, dw_w, dw_b)


def _fold8(v):
    return jnp.sum(v.reshape(v.shape[0] // 8, 8, v.shape[1]), axis=0)


def _ffn_act_bwd(u, z, dact, dw_w, name):
    t = u.shape[1]
    tm = _tile(t)
    chunk = min(FFN_BWD_CHUNK, tm // 2)
    halo = FFN_HALO
    hb = tm // halo
    nt = t // tm
    last_halo = t // halo - 1
    next_i = lambda i: jnp.minimum((i + 1) * hb, last_halo)
    main = pl.BlockSpec((2, tm, FFN_TN), lambda j, i: (0, i, j))
    nxt = pl.BlockSpec((2, halo, FFN_TN), lambda j, i: (0, next_i(i), j))
    wsp = pl.BlockSpec((2, FFN_CONV_WIDTH, FFN_TN), lambda j, i: (0, 0, j))
    bsp = pl.BlockSpec((2, 1, FFN_TN), lambda j, i: (0, 0, j))

    def body(u_ref, z_ref, zn_ref, da_ref, dan_ref, w_ref, du_ref, dw_ref, db_ref, acc_ref):
        i = pl.program_id(1)
        last = i == nt - 1
        acc_ref[...] = jnp.zeros_like(acc_ref)

        def emit(rows, zs, dact):
            n = chunk + halo
            za, zb, dact = zs[0].astype(F32), zs[1].astype(F32), dact.astype(F32)
            sg = _sigmoid(za)
            dzs = (dact * zb * (sg * (1.0 + za * (1.0 - sg))), dact * (za * sg))
            for p, dz in enumerate(dzs):
                ahead = (dz[:chunk], pltpu.roll(dz, n - 1, 0)[:chunk], pltpu.roll(dz, n - 2, 0)[:chunk])
                um = u_ref[p, rows, :].astype(F32)
                acc_ref[p, FFN_CONV_WIDTH] += _fold8(ahead[0])
                du = None
                for j, dzj in enumerate(ahead):
                    k = FFN_CONV_WIDTH - 1 - j
                    acc_ref[p, k] += _fold8(dzj * um)
                    term = w_ref[p, k:k + 1, :] * dzj
                    du = term if du is None else du + term
                du_ref[p, rows, :] = du.astype(BF16)

        def step(c, carry):
            s = pl.multiple_of(c * chunk, chunk)
            emit(pl.ds(s, chunk), [z_ref[p, pl.ds(s, chunk + halo), :] for p in range(2)], da_ref[pl.ds(s, chunk + halo), :])
            return carry

        lax.fori_loop(0, tm // chunk - 1, step, 0)
        s = tm - chunk
        emit(pl.ds(s, chunk),
             [jnp.concatenate([z_ref[p, s:tm, :], zn_ref[p]], axis=0) for p in range(2)],
             jnp.concatenate([da_ref[s:tm, :], _zero_if(last, dan_ref[...])], axis=0))

        @pl.when(i == 0)
        def _():
            dw_ref[...] = jnp.zeros_like(dw_ref)
            db_ref[...] = jnp.zeros_like(db_ref)

        for p in range(2):
            for k in range(FFN_CONV_WIDTH):
                dw_ref[p, k:k + 1, :] += _colsum(acc_ref[p, k])
            db_ref[p] += _colsum(acc_ref[p, FFN_CONV_WIDTH])

    return pl.pallas_call(
        body, name=name, grid=(FFN_DIM // FFN_TN, nt),
        in_specs=[main, main, nxt, pl.BlockSpec((tm, FFN_TN), lambda j, i: (i, j)),
                  pl.BlockSpec((halo, FFN_TN), lambda j, i: (next_i(i), j)), wsp],
        out_specs=[main, wsp, bsp],
        out_shape=[jax.ShapeDtypeStruct((2, t, FFN_DIM), BF16), jax.ShapeDtypeStruct((2, FFN_CONV_WIDTH, FFN_DIM), F32),
                   jax.ShapeDtypeStruct((2, 1, FFN_DIM), F32)],
        scratch_shapes=[pltpu.VMEM((2, FFN_CONV_WIDTH + 1, 8, FFN_TN), F32)],
        compiler_params=_params("parallel", "arbitrary"),
    )(u, z, z, dact, dact, dw_w)


CONV_TM = 256
CONV_ROWS = 128
CONV_LANES = 128


def _glu_window(pa_ref, pah_ref, pg_ref, pgh_ref, scr_ref, first):
    ah, gh = pah_ref[...].astype(F32), pgh_ref[...].astype(F32)
    scr_ref[0:CONV_HALO, :] = jnp.where(first, 0.0, ah * _sigmoid(gh))
    scr_ref[CONV_HALO:, :] = pa_ref[...].astype(F32) * _sigmoid(pg_ref[...].astype(F32))


def _tap_slabs(win, rows, ahead):
    n = win.shape[0]
    for s in range(8):
        ws = win if s == 0 else pltpu.roll(win, n - s if ahead else s, 0)
        for q in range(CONV_HALO // 8):
            o = 8 * q + s
            if o < CONV_WIDTH:
                start = 8 * q if ahead else CONV_HALO - 8 * q
                yield CONV_WIDTH - 1 - o, ws[start:start + rows]


def _conformer_specs(t):
    tm = _tile(t, (CONV_TM, 128))
    hb = tm // CONV_HALO
    d = D_MODEL
    main = lambda c: pl.BlockSpec((tm, d), lambda i: (i, c))
    halo = lambda c: pl.BlockSpec((CONV_HALO, d), lambda i: (jnp.maximum(i * hb - 1, 0), c))
    row = pl.BlockSpec((1, d), lambda i: (0, 0))
    wsp = pl.BlockSpec((CONV_WIDTH, d), lambda i: (0, 0))
    return tm, main, halo, row, wsp


def _conformer_mid(p, dw_w, dw_b, ln_g, ln_b, name):
    t = p.shape[0]
    tm, main, halo, row, wsp = _conformer_specs(t)
    d, lanes = D_MODEL, CONV_LANES

    def body(pa_ref, pah_ref, pg_ref, pgh_ref, w_ref, b_ref, g_ref, lb_ref, o_ref, dc_ref, scr_ref):
        _glu_window(pa_ref, pah_ref, pg_ref, pgh_ref, scr_ref, pl.program_id(0) == 0)
        for c in range(d // lanes):
            ls = slice(c * lanes, (c + 1) * lanes)
            acc = jnp.broadcast_to(b_ref[:, ls], (tm, lanes))
            for k, slab in _tap_slabs(scr_ref[:, ls], tm, False):
                acc = acc + w_ref[k:k + 1, ls] * slab
            dc_ref[:, ls] = acc

        def norm(r, carry):
            r0 = pl.multiple_of(r * 32, 32)
            dc = dc_ref[pl.ds(r0, 32), :]
            xc = dc - jnp.mean(dc, axis=-1, keepdims=True)
            ln = xc * lax.rsqrt(jnp.mean(xc * xc, axis=-1, keepdims=True) + EPS) * g_ref[...] + lb_ref[...]
            o_ref[pl.ds(r0, 32), :] = (ln * _sigmoid(ln)).astype(BF16)
            return carry

        lax.fori_loop(0, tm // 32, norm, 0)

    return pl.pallas_call(
        body, name=name, grid=(t // tm,), in_specs=[main(0), halo(0), main(1), halo(1), wsp, row, row, row],
        out_specs=[main(0), main(0)], out_shape=[jax.ShapeDtypeStruct((t, d), BF16), jax.ShapeDtypeStruct((t, d), F32)],
        scratch_shapes=[pltpu.VMEM((tm + CONV_HALO, d), F32)], compiler_params=_params("parallel"),
    )(p, p, p, p, dw_w, dw_b, ln_g, ln_b)


def _conformer_mid_bwd(p, dc, ds, ln_g, ln_b, name):
    t = p.shape[0]
    tm, main, halo, row, wsp = _conformer_specs(t)
    d, nt = D_MODEL, t // tm
    rows, lanes = CONV_ROWS, CONV_LANES

    def body(pa_ref, pah_ref, pg_ref, pgh_ref, dc_ref, ds_ref, g_ref, lb_ref,
             ddc_ref, dw_ref, db_ref, dg_ref, dlb_ref, scr_ref, wacc_ref, racc_ref):
        i = pl.program_id(0)

        @pl.when(i == 0)
        def _():
            wacc_ref[...] = jnp.zeros_like(wacc_ref)
            racc_ref[...] = jnp.zeros_like(racc_ref)

        _glu_window(pa_ref, pah_ref, pg_ref, pgh_ref, scr_ref, i == 0)

        def norm_bwd(r, carry):
            r0 = pl.multiple_of(r * 32, 32)
            dcv = dc_ref[pl.ds(r0, 32), :]
            xc = dcv - jnp.mean(dcv, axis=-1, keepdims=True)
            rstd = lax.rsqrt(jnp.mean(xc * xc, axis=-1, keepdims=True) + EPS)
            xhat = xc * rstd
            ln = xhat * g_ref[...] + lb_ref[...]
            sg = _sigmoid(ln)
            dln = ds_ref[pl.ds(r0, 32), :].astype(F32) * (sg * (1.0 + ln * (1.0 - sg)))
            dxh = dln * g_ref[...]
            ddc = rstd * (dxh - jnp.mean(dxh, axis=-1, keepdims=True) - xhat * jnp.mean(dxh * xhat, axis=-1, keepdims=True))
            ddc_ref[pl.ds(r0, 32), :] = ddc
            racc_ref[0] += _fold8(dln * xhat)
            racc_ref[1] += _fold8(dln)
            racc_ref[2] += _fold8(ddc)
            return carry

        lax.fori_loop(0, tm // 32, norm_bwd, 0)

        for c in range(d // lanes):
            ls = slice(c * lanes, (c + 1) * lanes)

            def taps(r, carry, ls=ls):
                r0 = pl.multiple_of(r * rows, rows)
                ddc = ddc_ref[pl.ds(r0, rows), ls]
                for k, slab in _tap_slabs(scr_ref[pl.ds(r0, rows + CONV_HALO), ls], rows, False):
                    wacc_ref[k, :, ls] += _fold8(ddc * slab)
                return carry

            lax.fori_loop(0, tm // rows, taps, 0)

        @pl.when(i == nt - 1)
        def _():
            for k in range(CONV_WIDTH):
                dw_ref[k:k + 1, :] = _colsum(wacc_ref[k])
            dg_ref[...] = _colsum(racc_ref[0])
            dlb_ref[...] = _colsum(racc_ref[1])
            db_ref[...] = _colsum(racc_ref[2])

    return pl.pallas_call(
        body, name=name, grid=(nt,), in_specs=[main(0), halo(0), main(1), halo(1), main(0), main(0), row, row],
        out_specs=[main(0), wsp, row, row, row],
        out_shape=[jax.ShapeDtypeStruct((t, d), F32), jax.ShapeDtypeStruct((CONV_WIDTH, d), F32)]
        + [jax.ShapeDtypeStruct((1, d), F32)] * 3,
        scratch_shapes=[pltpu.VMEM((tm + CONV_HALO, d), F32), pltpu.VMEM((CONV_WIDTH, 8, d), F32), pltpu.VMEM((3, 8, d), F32)],
        compiler_params=_params("arbitrary"),
    )(p, p, p, p, dc, ds, ln_g, ln_b)


def _conformer_glu_bwd(p, ddc, dw_w, name):
    t = p.shape[0]
    d = D_MODEL
    tm = _tile(t, (CONV_TM, 128))
    hb = tm // CONV_HALO
    nt = t // tm
    last_halo = t // CONV_HALO - 1
    rows, lanes = CONV_ROWS, CONV_LANES
    col = lambda c: pl.BlockSpec((tm, d), lambda i: (i, c))
    nxt = pl.BlockSpec((CONV_HALO, d), lambda i: (jnp.minimum((i + 1) * hb, last_halo), 0))

    def body(pa_ref, pg_ref, ddc_ref, ddcn_ref, w_ref, dp_ref, db_ref, scr_ref, acc_ref):
        i = pl.program_id(0)

        @pl.when(i == 0)
        def _():
            acc_ref[...] = jnp.zeros_like(acc_ref)

        scr_ref[0:tm, :] = ddc_ref[...]
        scr_ref[tm:, :] = _zero_if(i == nt - 1, ddcn_ref[...])
        for c in range(d // lanes):
            ls = slice(c * lanes, (c + 1) * lanes)
            gs = slice(d + c * lanes, d + (c + 1) * lanes)

            def taps(r, carry, ls=ls, gs=gs):
                r0 = pl.multiple_of(r * rows, rows)
                dglu = None
                for k, slab in _tap_slabs(scr_ref[pl.ds(r0, rows + CONV_HALO), ls], rows, True):
                    term = w_ref[k:k + 1, ls] * slab
                    dglu = term if dglu is None else dglu + term
                a = pa_ref[pl.ds(r0, rows), ls].astype(F32)
                sg = _sigmoid(pg_ref[pl.ds(r0, rows), ls].astype(F32))
                da = (dglu * sg).astype(BF16)
                dg = (dglu * a * sg * (1.0 - sg)).astype(BF16)
                dp_ref[pl.ds(r0, rows), ls] = da
                dp_ref[pl.ds(r0, rows), gs] = dg
                acc_ref[:, ls] += _fold8(da.astype(F32))
                acc_ref[:, gs] += _fold8(dg.astype(F32))
                return carry

            lax.fori_loop(0, tm // rows, taps, 0)

        @pl.when(i == nt - 1)
        def _():
            db_ref[...] = _colsum(acc_ref[...])

    return pl.pallas_call(
        body, name=name, grid=(nt,),
        in_specs=[col(0), col(1), col(0), nxt, pl.BlockSpec((CONV_WIDTH, d), lambda i: (0, 0))],
        out_specs=[pl.BlockSpec((tm, 2 * d), lambda i: (i, 0)), pl.BlockSpec((1, 2 * d), lambda i: (0, 0))],
        out_shape=[jax.ShapeDtypeStruct((t, 2 * d), BF16), jax.ShapeDtypeStruct((1, 2 * d), F32)],
        scratch_shapes=[pltpu.VMEM((tm + CONV_HALO, d), F32), pltpu.VMEM((8, 2 * d), F32)],
        compiler_params=_params("arbitrary"),
    )(p, p, ddc, ddc, dw_w)


def _colsum_call(a, name):
    t, n = a.shape
    tm = _tile(t)

    def body(a_ref, o_ref):
        @pl.when(pl.program_id(0) == 0)
        def _():
            o_ref[...] = jnp.zeros_like(o_ref)

        o_ref[...] += _colsum(a_ref[...].astype(F32))

    return pl.pallas_call(
        body, name=name, grid=(t // tm,), in_specs=[pl.BlockSpec((tm, n), lambda i: (i, 0))],
        out_specs=pl.BlockSpec((1, n), lambda i: (0, 0)), out_shape=jax.ShapeDtypeStruct((1, n), F32),
        compiler_params=_params("arbitrary"),
    )(a)


def _ada_fwd(c_all, w, name):
    rows, d = c_all.shape
    n = w.shape[1]
    tn = _tile(n, (256, 128))

    def body(c_ref, w_ref, o_ref):
        c = c_ref[...]
        o_ref[...] = _dot((c * _sigmoid(c)).astype(BF16), w_ref[...].astype(BF16), _NN)

    return pl.pallas_call(
        body, name=name, grid=(n // tn,),
        in_specs=[pl.BlockSpec((rows, d), lambda j: (0, 0)), pl.BlockSpec((d, tn), lambda j: (0, j))],
        out_specs=pl.BlockSpec((rows, tn), lambda j: (0, j)), out_shape=jax.ShapeDtypeStruct((rows, n), F32),
        compiler_params=_params("parallel"),
    )(c_all, w)


def _ada_bwd(c_all, dmod, name):
    rows, d = c_all.shape
    n = dmod.shape[1]
    tn = _tile(n, (256, 128))

    def body(c_ref, g_ref, o_ref):
        c = c_ref[...]
        o_ref[...] = _dot((c * _sigmoid(c)).astype(BF16), g_ref[...].astype(BF16), _TN)

    return pl.pallas_call(
        body, name=name, grid=(n // tn,),
        in_specs=[pl.BlockSpec((rows, d), lambda j: (0, 0)), pl.BlockSpec((rows, tn), lambda j: (0, j))],
        out_specs=pl.BlockSpec((d, tn), lambda j: (0, j)), out_shape=jax.ShapeDtypeStruct((d, n), F32),
        compiler_params=_params("parallel"),
    )(c_all, dmod)


def _sum_slots(a, name):
    s, r, c = a.shape
    tr = _row_tile(r, 256)

    def body(a_ref, o_ref):
        acc = a_ref[0].astype(F32)
        for k in range(1, s):
            acc = acc + a_ref[k].astype(F32)
        o_ref[...] = acc

    return pl.pallas_call(
        body, name=name, grid=(r // tr,), in_specs=[pl.BlockSpec((s, tr, c), lambda i: (0, i, 0))],
        out_specs=pl.BlockSpec((tr, c), lambda i: (i, 0)), out_shape=jax.ShapeDtypeStruct((r, c), F32),
        compiler_params=_params("parallel"),
    )(a)


def _sum_with_own(blocks, land, me, name):
    s, r, c = land.shape
    tr = _row_tile(r, 256)
    slot = lambda k: pl.BlockSpec((None, tr, c), lambda i, me_ref: ((me_ref[0] + k) % s, i, 0))

    def body(me_ref, own_ref, *refs):
        o_ref = refs[-1]
        acc = own_ref[...].astype(F32)
        for ref in refs[:-1]:
            acc = acc + ref[...].astype(F32)
        o_ref[...] = acc

    return pl.pallas_call(
        body, name=name, out_shape=jax.ShapeDtypeStruct((r, c), F32),
        grid_spec=pltpu.PrefetchScalarGridSpec(
            num_scalar_prefetch=1, grid=(r // tr,), in_specs=[slot(0)] + [slot(k) for k in range(1, s)],
            out_specs=pl.BlockSpec((tr, c), lambda i, me_ref: (i, 0))),
        compiler_params=_params("parallel"),
    )(me, blocks, *[land] * (s - 1))


def _adamw_update(w, g, m, v):
    nm = ADAM_B1 * m + (1.0 - ADAM_B1) * g
    nv = ADAM_B2 * v + (1.0 - ADAM_B2) * (g * g)
    m_hat = nm * (1.0 / (1.0 - ADAM_B1 ** ADAM_STEP))
    v_hat = nv * (1.0 / (1.0 - ADAM_B2 ** ADAM_STEP))
    return -ADAM_LR * (m_hat / (jnp.sqrt(v_hat) + ADAM_EPS) + ADAM_WD * w), nm, nv


def _adamw(w, g, m, v, name):
    l, r, c = w.shape
    tr = _row_tile(r, 256)
    blk = pl.BlockSpec((None, tr, c), lambda k, i: (k, i, 0))

    def body(w_ref, g_ref, m_ref, v_ref, d_ref, nm_ref, nv_ref):
        d_ref[...], nm_ref[...], nv_ref[...] = _adamw_update(w_ref[...], g_ref[...], m_ref[...], v_ref[...])

    return pl.pallas_call(
        body, name=name, grid=(l, r // tr), in_specs=[blk] * 4, out_specs=[blk] * 3,
        out_shape=[jax.ShapeDtypeStruct(w.shape, F32)] * 3, compiler_params=_params("parallel", "parallel"),
    )(w, g, m, v)


def _adamw_small(ws, gs, ms, vs, name):
    n = len(ws)
    two_d = lambda a: a.reshape(-1, a.shape[-1])

    def body(*refs):
        ins, outs = refs[:4 * n], refs[4 * n:]
        for a in range(n):
            outs[a][...], outs[n + a][...], outs[2 * n + a][...] = _adamw_update(*[ins[k * n + a][...] for k in range(4)])

    res = pl.pallas_call(
        body, name=name, out_shape=[jax.ShapeDtypeStruct(two_d(w).shape, F32) for w in ws] * 3,
    )(*[two_d(a) for a in (*ws, *gs, *ms, *vs)])
    return [[res[k * n + a].reshape(ws[a].shape) for a in range(n)] for k in range(3)]


def _mesh_pos():
    return lax.axis_index("x"), lax.axis_index("y"), lax.axis_index("c")


def _all_gather_vmem(x_shard, name):
    m_per, n = x_shard.shape

    def body(x_ref, out_ref, send_sems, recv_sems, local_sem):
        x, y, c = _mesh_pos()
        me, sibling = (x, y, c), (x, y, 1 - c)
        chips = [(1 - x, y), (x, 1 - y), (1 - x, 1 - y)]

        def rows(px, py, pc):
            return out_ref.at[pl.ds((4 * px + 2 * py + pc) * m_per, m_per), :]

        def copy(k, block, to, src=None):
            return pltpu.make_async_remote_copy(
                src_ref=rows(*block) if src is None else src, dst_ref=rows(*block),
                send_sem=send_sems.at[k], recv_sem=recv_sems.at[k], device_id=to, device_id_type=MESH)

        mine = pltpu.make_async_copy(x_ref, rows(*me), local_sem)
        mine.start()
        first = [copy(0, me, sibling, src=x_ref)]
        first += [copy(1 + j, me, (*chip, c), src=x_ref) for j, chip in enumerate(chips)]
        for cp in first:
            cp.start()
        passed = [copy(4 + j, (*chip, c), sibling) for j, chip in enumerate(chips)]
        for j, chip in enumerate(chips):
            copy(1 + j, (*chip, c), me).wait_recv()
            passed[j].start()
        copy(0, sibling, me).wait_recv()
        for j, chip in enumerate(chips):
            copy(4 + j, (*chip, 1 - c), me).wait_recv()
        for cp in first + passed:
            cp.wait_send()
        mine.wait()

    return pl.pallas_call(
        body, name=name, out_shape=jax.ShapeDtypeStruct((N_DEV * m_per, n), x_shard.dtype),
        in_specs=[pl.BlockSpec(memory_space=pltpu.VMEM)], out_specs=pl.BlockSpec(memory_space=pltpu.VMEM),
        scratch_shapes=[pltpu.SemaphoreType.DMA((7,)), pltpu.SemaphoreType.DMA((7,)), pltpu.SemaphoreType.DMA],
    )(x_shard)


def _all_gather_hbm(shards, name):
    n = len(shards)
    out_shape = [jax.ShapeDtypeStruct((N_DEV,) + s.shape, s.dtype) for s in shards]

    def body(*refs):
        x_refs, out_refs = refs[:n], refs[n:2 * n]
        send_sems, recv_sems, local_sems = refs[2 * n:]
        x, y, c = _mesh_pos()
        me, sibling = (x, y, c), (x, y, 1 - c)
        chips = [(1 - x, y), (x, 1 - y), (1 - x, 1 - y)]

        def blk(a, p):
            return out_refs[a].at[4 * p[0] + 2 * p[1] + p[2]]

        def copy(a, k, block, to, src=None):
            return pltpu.make_async_remote_copy(
                src_ref=blk(a, block) if src is None else src, dst_ref=blk(a, block),
                send_sem=send_sems.at[7 * a + k], recv_sem=recv_sems.at[7 * a + k], device_id=to, device_id_type=MESH)

        mine = [pltpu.make_async_copy(x_refs[a], blk(a, me), local_sems.at[a]) for a in range(n)]
        for cp in mine:
            cp.start()
        first = []
        for a in range(n):
            first.append(copy(a, 0, me, sibling, src=x_refs[a]))
            first += [copy(a, 1 + j, me, (*chip, c), src=x_refs[a]) for j, chip in enumerate(chips)]
        for cp in first:
            cp.start()
        passed = []
        for j, chip in enumerate(chips):
            for a in range(n):
                copy(a, 1 + j, (*chip, c), me).wait_recv()
                fwd = copy(a, 4 + j, (*chip, c), sibling)
                fwd.start()
                passed.append(fwd)
        for a in range(n):
            copy(a, 0, sibling, me).wait_recv()
            for j, chip in enumerate(chips):
                copy(a, 4 + j, (*chip, 1 - c), me).wait_recv()
        for cp in first + passed:
            cp.wait_send()
        for cp in mine:
            cp.wait()

    return pl.pallas_call(
        body, name=name, out_shape=out_shape, in_specs=[pl.BlockSpec(memory_space=pltpu.VMEM)] * n,
        out_specs=[pl.BlockSpec(memory_space=pl.ANY)] * n,
        scratch_shapes=[pltpu.SemaphoreType.DMA((7 * n,)), pltpu.SemaphoreType.DMA((7 * n,)), pltpu.SemaphoreType.DMA((n,))],
    )(*shards)


def _peers(x, y, c):
    flip = lambda v, f: 1 - v if f else v
    return [(flip(x, m & 4), flip(y, m & 2), flip(c, m & 1)) for m in range(1, N_DEV)]


def _dev_index(p):
    return 4 * p[0] + 2 * p[1] + p[2]


def _push_copies(src_refs, land_refs, send_sems, recv_sems, scatter, receive):
    x, y, c = _mesh_pos()
    me = _dev_index((x, y, c))
    copies = []
    for a, (src, land) in enumerate(zip(src_refs, land_refs)):
        for k, p in enumerate(_peers(x, y, c)):
            copies.append(pltpu.make_async_remote_copy(
                src_ref=src.at[_dev_index(p)] if scatter else src, dst_ref=land.at[_dev_index(p) if receive else me],
                send_sem=send_sems.at[7 * a + k], recv_sem=recv_sems.at[7 * a + k], device_id=p, device_id_type=MESH))
    return copies


_HBM = pl.BlockSpec(memory_space=pltpu.HBM)
_SEM = pl.BlockSpec(memory_space=pltpu.SEMAPHORE)
_EFFECT = pltpu.SideEffectType.DATAFLOW_SIDE_EFFECTING


def _pushes_start(srcs, lands, scatter, name):
    n = len(srcs)

    def body(*refs):
        src_refs, land_refs = refs[:n], refs[n:2 * n]
        send_sems, recv_sems = refs[2 * n], refs[2 * n + 1]
        token = refs[-1]
        for cp in _push_copies(src_refs, land_refs, send_sems, recv_sems, scatter, receive=False):
            cp.start()
        token[...] = jnp.zeros_like(token)

    hbm = lambda a: pltpu.HBM(a.shape, a.dtype)
    sems = pltpu.SemaphoreType.DMA((7 * n,))
    outs = pl.pallas_call(
        body, name=name,
        out_shape=(sems, sems, *[hbm(a) for a in srcs], *[hbm(a) for a in lands], jax.ShapeDtypeStruct((8, 128), F32)),
        in_specs=[_HBM] * (2 * n), out_specs=(_SEM, _SEM, *[_HBM] * (2 * n), pl.BlockSpec(memory_space=pltpu.VMEM)),
        input_output_aliases={i: 2 + i for i in range(2 * n)},
        compiler_params=pltpu.CompilerParams(has_side_effects=_EFFECT),
    )(*[pltpu.with_memory_space_constraint(a, pltpu.HBM) for a in (*srcs, *lands)])
    return (outs[0], outs[1], outs[2:2 + n], outs[2 + n:2 + 2 * n], scatter), outs[-1]


def _pushes_wait(handle, after, name):
    send_sems, recv_sems, srcs, lands, scatter = handle
    n = len(srcs)

    def body(*refs):
        src_refs, land_refs = refs[:n], refs[n:2 * n]
        for cp in _push_copies(src_refs, land_refs, refs[2 * n], refs[2 * n + 1], scatter, receive=True):
            cp.wait_send()
            cp.wait_recv()

    hbm = lambda a: pltpu.HBM(a.shape, a.dtype)
    outs = pl.pallas_call(
        body, name=name, out_shape=tuple(hbm(a) for a in (*srcs, *lands)),
        in_specs=[_HBM] * (2 * n) + [_SEM, _SEM, pl.BlockSpec(memory_space=pl.ANY)], out_specs=tuple([_HBM] * (2 * n)),
        input_output_aliases={i: i for i in range(2 * n)},
        compiler_params=pltpu.CompilerParams(has_side_effects=_EFFECT),
    )(*srcs, *lands, send_sems, recv_sems, after)
    return outs[:n], outs[n:]


def _landing_zones(srcs, name):
    n = len(srcs)

    def body(*refs):
        src_refs, land_refs, bufs, sems = refs[:n], refs[n:2 * n], refs[2 * n:3 * n], refs[3 * n]
        me = _dev_index(_mesh_pos())
        load = [pltpu.make_async_copy(src, buf, sems.at[a]) for a, (src, buf) in enumerate(zip(src_refs, bufs))]
        store = [pltpu.make_async_copy(buf, land.at[me], sems.at[a]) for a, (buf, land) in enumerate(zip(bufs, land_refs))]
        for cp in load:
            cp.start()
        for ld, st in zip(load, store):
            ld.wait()
            st.start()
        for cp in store:
            cp.wait()

    any_spec = pl.BlockSpec(memory_space=pl.ANY)
    return pl.pallas_call(
        body, name=name, out_shape=[jax.ShapeDtypeStruct((N_DEV,) + s.shape, s.dtype) for s in srcs],
        in_specs=[any_spec] * n, out_specs=[any_spec] * n,
        scratch_shapes=[pltpu.VMEM(s.shape, s.dtype) for s in srcs] + [pltpu.SemaphoreType.DMA((n,))],
        compiler_params=pltpu.CompilerParams(vmem_limit_bytes=V7X_VMEM_LIMIT),
    )(*srcs)


def _ffn_forward(x, mod, norm_g, w, tag):
    sh, sc, gate = mod
    h = _modnorm(x, norm_g, sc, sh, f"{tag}_norm")
    u = _ffn_up(h, w["up_t"], f"{tag}_up")
    act, z = _ffn_act(u, w["dw_w"], w["dw_b"], f"{tag}_act")
    y, x_new = _matmul(act, w["down"], "nn", F32, f"{tag}_down", resid=(x, gate))
    return x_new, (x, h, u, z, act, y)


def _behind(row, token):
    return row if token is None else row + token[0:1, 0:1]


def _ffn_backward(dx_new, saved, mod, norm_g, w, tag, emit):
    x, h, u, z, act, y = saved
    _, sc, gate = mod
    dy, d_gate = _gate_bwd(dx_new, y, gate, f"{tag}_gate_bwd")
    d_down = _matmul_tn_acc(act, dy, f"{tag}_down_dw")
    dact = _matmul(dy, w["down"], "nt", BF16, f"{tag}_down_dx")
    du, d_dw_w, d_dw_b = _ffn_act_bwd(u, z, dact, w["dw_w"], f"{tag}_act_bwd")
    d_up_t = _matmul_tn_acc(du, h, f"{tag}_up_dw").reshape(2 * FFN_DIM, -1)
    token = emit([d_up_t, d_down])
    dh = _ffn_up_dx(du, w["up_t"], f"{tag}_up_dx")
    dx, d_w, d_sh = _modnorm_bwd(x, dh, norm_g, _behind(sc, token), dx_new, f"{tag}_norm_bwd")
    return dx, dict(dw_w=d_dw_w.transpose(1, 0, 2).reshape(FFN_CONV_WIDTH, 2 * FFN_DIM),
                    dw_b=d_dw_b.reshape(1, 2 * FFN_DIM), norm_g=d_w * (1.0 + sc), sh=d_sh, sc=d_w * norm_g, gate=d_gate)


def _mixer_forward(x, mod, norm_g, w, rope, tag):
    sh, sc, gate = mod
    h = _modnorm(x, norm_g, sc, sh, f"{tag}_norm")
    z = _matmul(h, w["w_in_t"], "nt", BF16, f"{tag}_in")
    ya = _gmlp_fwd(z, w["gain"], w["wtril"], w["bias_exp"], f"{tag}_gmlp")
    q, k, v = _qk_prep(z, rope[0], rope[1], w["gq"], w["gk"], w["seg"], f"{tag}_qk")
    outs, lses = zip(*[_attn_fwd(q[b], k[b], v[b], dil, f"{tag}_attn_d{dil}") for b, dil in enumerate(DILATIONS)])
    yb, lse, cat = _attn_merge(outs, lses, ya, f"{tag}_merge")
    y, x_new = _matmul(cat, w["w_out"], "nn", F32, f"{tag}_out", resid=(x, gate))
    return x_new, (x, h, z, q, k, v, yb, lse, cat, y)


def _mixer_backward(dx_new, saved, mod, norm_g, w, rope, tag, emit):
    x, h, z, q, k, v, yb, lse, cat, y = saved
    _, sc, gate = mod
    dy, d_gate = _gate_bwd(dx_new, y, gate, f"{tag}_gate_bwd")
    d_w_out = _matmul_tn_acc(cat, dy, f"{tag}_out_dw")
    dcat = _matmul(dy, w["w_out"], "nt", BF16, f"{tag}_out_dx")
    dz_a, d_sp_w, d_gain, d_bias_exp = _gmlp_bwd(z, dcat, w["gain"], w["wtril"], w["wtril_t"], w["bias_exp"], f"{tag}_gmlp_bwd")
    dyb = _subseq_views(dcat, A_WIDTH // B_WIDTH, f"{tag}_dyb_views")
    dqs, dks, dvs = zip(*[_attn_bwd(q[b], k[b], v[b], dyb[b], yb[b], lse[b], dil, f"{tag}_attn_bwd_d{dil}")
                          for b, dil in enumerate(DILATIONS)])
    dz_qkv, d_gq, d_gk = _qk_prep_bwd(z, dqs, dks, dvs, rope[0], rope[1], w["gq"], w["gk"], w["seg"], f"{tag}_qk_bwd")
    dz = jnp.concatenate([dz_a, dz_qkv], axis=1)
    d_w_in_t = _matmul_tn_acc(dz, h, f"{tag}_in_dw")
    token = emit([d_w_in_t, d_w_out])
    dh = _matmul(dz, w["w_in_t"], "nn", F32, f"{tag}_in_dx")
    dx, d_w, d_sh = _modnorm_bwd(x, dh, norm_g, _behind(sc, token), dx_new, f"{tag}_norm_bwd")
    return dx, dict(
        vnorm_g=d_gain.reshape(A_GROUPS, GROUP_DIM), spatial_w=d_sp_w,
        spatial_b=d_bias_exp.reshape(CHUNK, A_GROUPS, GROUP_DIM).sum(-1).T,
        q_norm_g=d_gq.reshape(HEADS, HEAD_DIM).sum(0), k_norm_g=d_gk.reshape(HEADS, HEAD_DIM).sum(0),
        norm_g=d_w * (1.0 + sc), sh=d_sh, sc=d_w * norm_g, gate=d_gate)


def _conformer_forward(x, mod, norm_g, w, tag):
    sh, sc, gate = mod
    h = _modnorm(x, norm_g, sc, sh, f"{tag}_norm")
    p = _matmul(h, w["pw1_t"], "nt", BF16, f"{tag}_pw1", bias=w["pw1_b"])
    s, dc = _conformer_mid(p, w["dw_w"], w["dw_b"], w["ln_g"], w["ln_b"], f"{tag}_mid")
    y, x_new = _matmul(s, w["pw2"], "nn", F32, f"{tag}_pw2", bias=w["pw2_b"], resid=(x, gate))
    return x_new, (x, h, p, dc, s, y)


def _conformer_backward(dx_new, saved, mod, norm_g, w, tag, emit):
    x, h, p, dc, s, y = saved
    _, sc, gate = mod
    dy, d_gate = _gate_bwd(dx_new, y, gate, f"{tag}_gate_bwd")
    d_pw2 = _matmul_tn_acc(s, dy, f"{tag}_pw2_dw")
    d_pw2_b = _colsum_call(dy, f"{tag}_pw2_db")
    ds = _matmul(dy, w["pw2"], "nt", BF16, f"{tag}_pw2_dx")
    ddc, d_dw_w, d_dw_b, d_ln_g, d_ln_b = _conformer_mid_bwd(p, dc, ds, w["ln_g"], w["ln_b"], f"{tag}_mid_bwd")
    dp, d_pw1_b = _conformer_glu_bwd(p, ddc, w["dw_w"], f"{tag}_glu_bwd")
    d_pw1_t = _matmul_tn_acc(dp, h, f"{tag}_pw1_dw")
    token = emit([d_pw1_t, d_pw2])
    dh = _matmul(dp, w["pw1_t"], "nn", F32, f"{tag}_pw1_dx")
    dx, d_w, d_sh = _modnorm_bwd(x, dh, norm_g, _behind(sc, token), dx_new, f"{tag}_norm_bwd")
    return dx, dict(pw1_b=d_pw1_b, dw_w=d_dw_w, dw_b=d_dw_b, ln_g=d_ln_g, ln_b=d_ln_b, pw2_b=d_pw2_b, norm_g=d_w * (1.0 + sc), sh=d_sh, sc=d_w * norm_g, gate=d_gate)


def _local_step(x, target, pos, mod, norm_mix_g, norm_ffn_g, mixer_w, conv_w, ffn_w, fetch, emit):
    d = D_MODEL
    inv_freq = 1.0 / (ROPE_THETA ** (jnp.arange(0, HEAD_DIM, 2, dtype=F32) / HEAD_DIM))
    inv_freq = jnp.tile(inv_freq, 2 * HEADS)[None, :]
    sign = jnp.tile(jnp.concatenate([-jnp.ones(HEAD_DIM // 2, F32), jnp.ones(HEAD_DIM // 2, F32)]), HEADS)[None, :]
    rope = _rope_tables(pos, inv_freq, sign, "rope_tables")
    mods = [[mod[l:l + 1, i * d:(i + 1) * d] for i in range(6)] for l in range(2)]
    mix = [(m[0], m[1], m[2]) for m in mods]
    ffn = [(m[3], m[4], m[5]) for m in mods]
    gm = [norm_mix_g[l:l + 1] for l in range(2)]
    gf = [norm_ffn_g[l:l + 1] for l in range(2)]

    mixer_w = {**mixer_w, **fetch("l0_mix", x)}
    x1, s_mix = _mixer_forward(x, mix[0], gm[0], mixer_w, rope, "l0_mix")
    ffn_w0 = {**ffn_w[0], **fetch("l0_ffn", x1)}
    x2, s_ffn0 = _ffn_forward(x1, ffn[0], gf[0], ffn_w0, "l0_ffn")
    conv_w = {**conv_w, **fetch("l1_conv", x2)}
    x3, s_conv = _conformer_forward(x2, mix[1], gm[1], conv_w, "l1_conv")
    ffn_w1 = {**ffn_w[1], **fetch("l1_ffn", x3)}
    x4, s_ffn1 = _ffn_forward(x3, ffn[1], gf[1], ffn_w1, "l1_ffn")
    dx, loss = _loss_head(x4, target, "loss_head")
    dx, g_ffn1 = _ffn_backward(dx, s_ffn1, ffn[1], gf[1], ffn_w1, "l1_ffn", functools.partial(emit, "l1_ffn"))
    dx, g_conv = _conformer_backward(dx, s_conv, mix[1], gm[1], conv_w, "l1_conv", functools.partial(emit, "l1_conv"))
    dx, g_ffn0 = _ffn_backward(dx, s_ffn0, ffn[0], gf[0], ffn_w0, "l0_ffn", functools.partial(emit, "l0_ffn"))
    dx, g_mix = _mixer_backward(dx, s_mix, mix[0], gm[0], mixer_w, rope, "l0_mix", functools.partial(emit, "l0_mix"))
    blocks = [g_mix, g_ffn0, g_conv, g_ffn1]
    dmod = jnp.stack([jnp.concatenate([a["sh"], a["sc"], a["gate"], b["sh"], b["sc"], b["gate"]], axis=1)[0]
                      for a, b in ((g_mix, g_ffn0), (g_conv, g_ffn1))])
    return loss, dx, dmod, blocks


def _pack(arrs, rows=8):
    flat = jnp.concatenate([a.reshape(-1).astype(F32) for a in arrs])
    n = flat.shape[0]
    cols = -(-n // (rows * 128)) * 128
    return jnp.pad(flat, (0, rows * cols - n)).reshape(rows, cols)


def _unpack(flat, shapes):
    out, off = [], 0
    for shp in shapes:
        n = math.prod(shp)
        out.append(flat[..., off:off + n].reshape(flat.shape[:-1] + tuple(shp)))
        off += n
    return out


def _take_block(a, idx, size, axis):
    return lax.dynamic_slice_in_dim(a, idx * size, size, axis)


def kernel(x, c, positions, ada_w, ada_b, norm_mix_g, norm_ffn_g, ab_w_in, a_vnorm_g, a_spatial_w, a_spatial_b, b_q_norm_g, b_k_norm_g, ab_w_out, conv_pw1_w, conv_pw1_b, conv_dw_w, conv_dw_b, conv_ln_g, conv_ln_b, conv_pw2_w, conv_pw2_b, ffn_up_w, ffn_dw_w, ffn_dw_b, ffn_down_w, loss_target, m_ada_w, m_ada_b, m_norm_mix_g, m_norm_ffn_g, m_ab_w_in, m_a_vnorm_g, m_a_spatial_w, m_a_spatial_b, m_b_q_norm_g, m_b_k_norm_g, m_ab_w_out, m_conv_pw1_w, m_conv_pw1_b, m_conv_dw_w, m_conv_dw_b, m_conv_ln_g, m_conv_ln_b, m_conv_pw2_w, m_conv_pw2_b, m_ffn_up_w, m_ffn_dw_w, m_ffn_dw_b, m_ffn_down_w, v_ada_w, v_ada_b, v_norm_mix_g, v_norm_ffn_g, v_ab_w_in, v_a_vnorm_g, v_a_spatial_w, v_a_spatial_b, v_b_q_norm_g, v_b_k_norm_g, v_ab_w_out, v_conv_pw1_w, v_conv_pw1_b, v_conv_dw_w, v_conv_dw_b, v_conv_ln_g, v_conv_ln_b, v_conv_pw2_w, v_conv_pw2_b, v_ffn_up_w, v_ffn_dw_w, v_ffn_dw_b, v_ffn_down_w):
    weights = dict(ada_w=ada_w, ada_b=ada_b, norm_mix_g=norm_mix_g, norm_ffn_g=norm_ffn_g, ab_w_in=ab_w_in, a_vnorm_g=a_vnorm_g, a_spatial_w=a_spatial_w, a_spatial_b=a_spatial_b, b_q_norm_g=b_q_norm_g, b_k_norm_g=b_k_norm_g, ab_w_out=ab_w_out, conv_pw1_w=conv_pw1_w, conv_pw1_b=conv_pw1_b, conv_dw_w=conv_dw_w, conv_dw_b=conv_dw_b, conv_ln_g=conv_ln_g, conv_ln_b=conv_ln_b, conv_pw2_w=conv_pw2_w, conv_pw2_b=conv_pw2_b, ffn_up_w=ffn_up_w, ffn_dw_w=ffn_dw_w, ffn_dw_b=ffn_dw_b, ffn_down_w=ffn_down_w)
    mom1 = dict(ada_w=m_ada_w, ada_b=m_ada_b, norm_mix_g=m_norm_mix_g, norm_ffn_g=m_norm_ffn_g, ab_w_in=m_ab_w_in, a_vnorm_g=m_a_vnorm_g, a_spatial_w=m_a_spatial_w, a_spatial_b=m_a_spatial_b, b_q_norm_g=m_b_q_norm_g, b_k_norm_g=m_b_k_norm_g, ab_w_out=m_ab_w_out, conv_pw1_w=m_conv_pw1_w, conv_pw1_b=m_conv_pw1_b, conv_dw_w=m_conv_dw_w, conv_dw_b=m_conv_dw_b, conv_ln_g=m_conv_ln_g, conv_ln_b=m_conv_ln_b, conv_pw2_w=m_conv_pw2_w, conv_pw2_b=m_conv_pw2_b, ffn_up_w=m_ffn_up_w, ffn_dw_w=m_ffn_dw_w, ffn_dw_b=m_ffn_dw_b, ffn_down_w=m_ffn_down_w)
    mom2 = dict(ada_w=v_ada_w, ada_b=v_ada_b, norm_mix_g=v_norm_mix_g, norm_ffn_g=v_norm_ffn_g, ab_w_in=v_ab_w_in, a_vnorm_g=v_a_vnorm_g, a_spatial_w=v_a_spatial_w, a_spatial_b=v_a_spatial_b, b_q_norm_g=v_b_q_norm_g, b_k_norm_g=v_b_k_norm_g, ab_w_out=v_ab_w_out, conv_pw1_w=v_conv_pw1_w, conv_pw1_b=v_conv_pw1_b, conv_dw_w=v_conv_dw_w, conv_dw_b=v_conv_dw_b, conv_ln_g=v_conv_ln_g, conv_ln_b=v_conv_ln_b, conv_pw2_w=v_conv_pw2_w, conv_pw2_b=v_conv_pw2_b, ffn_up_w=v_ffn_up_w, ffn_dw_w=v_ffn_dw_w, ffn_dw_b=v_ffn_dw_b, ffn_down_w=v_ffn_down_w)
    order = list(weights)
    d, f2 = D_MODEL, 2 * FFN_DIM
    t = x.shape[1]
    me = 4 * lax.axis_index("x") + 2 * lax.axis_index("y") + lax.axis_index("c")
    for window, dil in PATTERNS:
        assert window // dil == Q_BLOCK and t % (dil * Q_BLOCK) == 0

    small_in = [c[0], conv_pw1_b[0], conv_dw_w[0], conv_dw_b[0], conv_ln_g[0], conv_ln_b[0], conv_pw2_b[0], ffn_dw_w]
    g1 = _all_gather_vmem(_pack(small_in, rows=8), "gather_small").reshape(N_DEV, -1)
    c_all, pw1_b, dw_w, dw_b, ln_g, ln_b, pw2_b, fdw_w = _unpack(g1, [a.shape for a in small_in])
    pw1_b, dw_b, ln_g, ln_b, pw2_b = [a.reshape(1, -1) for a in (pw1_b, dw_b, ln_g, ln_b, pw2_b)]
    dw_w = dw_w.transpose(1, 0, 2).reshape(CONV_WIDTH, d)
    fdw_w = fdw_w.transpose(1, 2, 0, 3).reshape(2, FFN_CONV_WIDTH, f2)

    c16 = jnp.pad(c_all, ((0, 2 * N_DEV - c_all.shape[0]), (0, 0)))
    part = jnp.concatenate([_ada_fwd(c16, ada_w[l], f"ada_fwd{l}")[:N_DEV] for l in range(2)], axis=1)
    g2 = _all_gather_vmem(part, "gather_mod").reshape(N_DEV, N_DEV, 2, -1)
    mod = lax.dynamic_index_in_dim(g2, me, axis=1, keepdims=False).transpose(1, 0, 2).reshape(2, 6 * d) + ada_b

    stages = dict(l0_mix=[ab_w_in[0].T, ab_w_out[0]], l0_ffn=[ffn_up_w[0].T, ffn_down_w[0]],
                  l1_conv=[conv_pw1_w[0].T, conv_pw2_w[0]], l1_ffn=[ffn_up_w[1].T, ffn_down_w[1]])
    stages = {k: [s.astype(BF16) for s in v] for k, v in stages.items()}
    names = dict(l0_mix=("w_in_t", "w_out"), l0_ffn=("up_t", "down"), l1_conv=("pw1_t", "pw2"), l1_ffn=("up_t", "down"))
    ready = {"l0_mix": [a.reshape(-1, d) for a in _all_gather_hbm(stages["l0_mix"], "gather_mixer_weights")]}
    behind = (ready, mod)
    arriving = {}
    for stage, group in (("l0_ffn", ("l0_ffn",)), ("l1_conv", ("l1_conv", "l1_ffn"))):
        srcs, _ = lax.optimization_barrier(([s for g in group for s in stages[g]], behind))
        arriving[stage], behind = _pushes_start(
            srcs, _landing_zones(srcs, f"gather_{stage}_zones"), False, f"gather_{stage}_start")
        mod = mod + behind[0:1, 0:1]

    def fetch(stage, after):
        if stage in arriving:
            full = [a.reshape(-1, d) for a in _pushes_wait(arriving[stage], after, f"gather_{stage}_wait")[1]]
            ready[stage] = full[:2]
            if stage == "l1_conv":
                ready["l1_ffn"] = full[2:]
        return dict(zip(names[stage], ready[stage]))

    causal = jnp.tril(jnp.ones((CHUNK, CHUNK), bool))
    wtril = jnp.where(causal[None], a_spatial_w[0], 0.0)
    mixer_w = dict(
        gain=a_vnorm_g[0].reshape(1, A_WIDTH), wtril=wtril.astype(BF16),
        wtril_t=wtril.transpose(0, 2, 1).astype(BF16),
        bias_exp=jnp.repeat(a_spatial_b[0].T, GROUP_DIM, axis=1),
        gq=jnp.tile(b_q_norm_g[0], HEADS)[None, :], gk=jnp.tile(b_k_norm_g[0], HEADS)[None, :],
        seg=jnp.kron(jnp.eye(HEADS, dtype=BF16), jnp.ones((HEAD_DIM, HEAD_DIM), BF16)))
    conv_w = dict(pw1_b=pw1_b, dw_w=dw_w, dw_b=dw_b, ln_g=ln_g, ln_b=ln_b, pw2_b=pw2_b)
    ffn_w = [dict(dw_w=fdw_w[l].reshape(FFN_CONV_WIDTH, 2, FFN_DIM).transpose(1, 0, 2), dw_b=ffn_dw_b[l].reshape(2, 1, FFN_DIM))
             for l in range(2)]

    leaving = {}

    def emit(stage, grads):
        blocks = [g.reshape(N_DEV, g.shape[0] // N_DEV, d) for g in grads]
        leaving[stage], token = _pushes_start(
            blocks, [lax.empty(b.shape, b.dtype) for b in blocks], True, f"reduce_{stage}_start")
        return token

    loss, dx, dmod, (g_mix, g_ffn0, g_conv, g_ffn1) = _local_step(
        x[0], loss_target[0], positions[0].astype(F32)[:, None], mod, norm_mix_g, norm_ffn_g, mixer_w, conv_w, ffn_w,
        fetch, emit)

    me_op = me.astype(jnp.int32).reshape(1)

    def reduced(stage, after):
        blocks, lands = _pushes_wait(leaving[stage], after, f"reduce_{stage}_wait")
        return [_sum_with_own(b, a, me_op, f"reduce_{stage}_sum{i}") for i, (b, a) in enumerate(zip(blocks, lands))]

    (r_up_t1, r_down1), (r_pw1_t, r_pw2), (r_up_t0, r_down0) = [reduced(s, dx) for s in ("l1_ffn", "l1_conv", "l0_ffn")]

    small_g = [
        dmod, jnp.concatenate([g_mix["norm_g"], g_conv["norm_g"]]), jnp.concatenate([g_ffn0["norm_g"], g_ffn1["norm_g"]]),
        g_mix["vnorm_g"], g_mix["spatial_w"], g_mix["spatial_b"], g_mix["q_norm_g"], g_mix["k_norm_g"],
        g_conv["pw1_b"], g_conv["dw_w"], g_conv["dw_b"], g_conv["ln_g"], g_conv["ln_b"], g_conv["pw2_b"],
        jnp.stack([g_ffn0["dw_w"], g_ffn1["dw_w"]]), jnp.concatenate([g_ffn0["dw_b"], g_ffn1["dw_b"]])]
    packed = _pack(small_g, rows=8)
    g3 = _all_gather_vmem(packed, "gather_small_grads").reshape(N_DEV, 8, -1)
    total = _unpack(_sum_slots(g3, "sum_small_grads").reshape(-1), [a.shape for a in small_g])
    (s_dmod, s_mix_g, s_ffn_g, s_vnorm, s_sp_w, s_sp_b, s_gq, s_gk, s_pw1_b, s_dw_w, s_dw_b, s_ln_g, s_ln_b,
     s_pw2_b, s_fdw_w, s_fdw_b) = total
    dmod_all = g3.reshape(N_DEV, -1)[:, :2 * 6 * d].reshape(N_DEV, 2, 6 * d)
    n_ada = ada_w.shape[2]
    dmod16 = jnp.pad(_take_block(dmod_all, me, n_ada, 2), ((0, N_DEV), (0, 0), (0, 0)))
    g_ada_w = jnp.stack([_ada_bwd(c16, dmod16[:, l], f"ada_bwd{l}") for l in range(2)])

    grads = dict(
        ada_w=g_ada_w, ada_b=s_dmod, norm_mix_g=s_mix_g, norm_ffn_g=s_ffn_g,
        a_vnorm_g=s_vnorm[None], a_spatial_w=s_sp_w[None], a_spatial_b=s_sp_b[None], b_q_norm_g=s_gq[None],
        b_k_norm_g=s_gk[None],
        conv_pw1_b=_take_block(s_pw1_b, me, conv_pw1_b.shape[1], 1),
        conv_dw_w=_take_block(s_dw_w, me, conv_dw_w.shape[2], 1)[None],
        conv_dw_b=_take_block(s_dw_b, me, conv_dw_b.shape[1], 1), conv_ln_g=_take_block(s_ln_g, me, conv_ln_g.shape[1], 1),
        conv_ln_b=_take_block(s_ln_b, me, conv_ln_b.shape[1], 1), conv_pw2_w=r_pw2[None],
        conv_pw2_b=_take_block(s_pw2_b, me, conv_pw2_b.shape[1], 1),
        ffn_dw_w=_take_block(s_fdw_w, me, ffn_dw_w.shape[2], 2), ffn_dw_b=s_fdw_b, ffn_down_w=jnp.stack([r_down0, r_down1]))
    grads_t = dict(conv_pw1_w=r_pw1_t[None], ffn_up_w=jnp.stack([r_up_t0, r_up_t1]))

    large = ("ada_w", "conv_pw1_w", "conv_pw2_w", "ffn_up_w", "ffn_down_w", "ab_w_in", "ab_w_out")
    flip = lambda a: jnp.swapaxes(a, 1, 2)
    delta, new_m, new_v = {}, {}, {}
    for name in large:
        if name == "ab_w_in":
            r_in_t, r_out = reduced("l0_mix", new_v["ffn_down_w"])
            grads_t["ab_w_in"] = r_in_t[None]
            grads["ab_w_out"] = r_out[None]
        if name in grads_t:
            grads[name] = flip(grads_t[name])
            res = _adamw(flip(weights[name]), grads_t[name], flip(mom1[name]), flip(mom2[name]), f"adamw_{name}")
            delta[name], new_m[name], new_v[name] = [flip(r) for r in res]
        else:
            delta[name], new_m[name], new_v[name] = _adamw(weights[name], grads[name], mom1[name], mom2[name], f"adamw_{name}")
    small = [n for n in order if n not in large]
    res = _adamw_small(*[[src[n] for n in small] for src in (weights, grads, mom1, mom2)], "adamw_small")
    for dst, arrs in zip((delta, new_m, new_v), res):
        dst.update(zip(small, arrs))

    loss = lax.psum(loss[0, 0], ("x", "y", "c"))
    return (loss, dx[None], *[grads[n] for n in order], *[delta[n] for n in order],
            *[new_m[n] for n in order], *[new_v[n] for n in order])
```

```python
import functools
import math

import jax
import jax.numpy as jnp
from jax import lax
from jax.experimental import pallas as pl
from jax.experimental.pallas import tpu as pltpu

F32 = jnp.float32
BF16 = jnp.bfloat16
MESH = pl.DeviceIdType.MESH

D_MODEL = 1024
A_WIDTH = 512
A_GROUPS = 4
GROUP_DIM = 128
CHUNK = 128
B_WIDTH = 512
HEADS = 8
HEAD_DIM = 64
PATTERNS = ((128, 1), (512, 4), (2048, 16))
Q_BLOCK = 128
ROPE_THETA = 10000.0
AB_IN = 2560
CONV_WIDTH = 31
FFN_DIM = 2816
FFN_CONV_WIDTH = 3
EPS = 1e-6
NEG = -1e30
N_DEV = 8
ADAM_LR, ADAM_B1, ADAM_B2, ADAM_EPS, ADAM_WD, ADAM_STEP = 0.001, 0.9, 0.999, 1e-08, 0.01, 10

V7X_VMEM_LIMIT = 56 * 2**20
BF16_ROWS = 16
FFN_HALO = 16
CONV_HALO = 32

_NN = (((1,), (0,)), ((), ()))
_NT = (((1,), (1,)), ((), ()))
_TN = (((0,), (0,)), ((), ()))


def _tile(n, prefs=(512, 256, 128)):
    for t in prefs:
        if n % t == 0:
            return t
    return n


def _row_tile(n, cap=512):
    best = n
    for t in range(8, min(n, cap) + 1, 8):
        if n % t == 0:
            best = t
    return best if best <= cap else n


def _params(*sem):
    return pltpu.CompilerParams(dimension_semantics=sem, vmem_limit_bytes=V7X_VMEM_LIMIT)


def _dot(a, b, dims):
    return lax.dot_general(a, b, dims, preferred_element_type=F32)


def _sigmoid(x):
    return 1.0 / (1.0 + jnp.exp(-x))


def _gelu(x):
    return 0.5 * x * (1.0 + lax.erf(x * (2.0 ** -0.5)))


def _gelu_grad(x):
    return 0.5 * (1.0 + lax.erf(x * (2.0 ** -0.5))) + x * jnp.exp(-0.5 * x * x) * (1.0 / math.sqrt(2.0 * math.pi))


def _colsum(v):
    return jnp.sum(v, axis=0, keepdims=True)


MATMUL_VMEM_BUDGET = 40 * 2**20


def _matmul_tiles(m, n, k, out_bytes, with_resid):
    def options(dim):
        opts = [t for t in (1024, 512, 256, 128) if dim % t == 0]
        return opts + [dim] if dim <= 4096 and dim not in opts else opts

    best = None
    for tm in options(m):
        for tn in options(n):
            need = 4 * (tm * k + k * tn) + tm * tn * (4 + 2 * out_bytes) + (24 * tm * tn if with_resid else 0)
            if need <= MATMUL_VMEM_BUDGET and (best is None or tm * tn / (tm + tn) > best[0]):
                best = (tm * tn / (tm + tn), tm, tn)
    return best[1], best[2]


def _matmul_tn_acc(a, b, name, tk=1024):
    squeeze = a.ndim == 2
    a3 = a[None] if squeeze else a
    p_, t, m = a3.shape
    n = b.shape[1]
    nk = t // tk

    def body(a_ref, b_ref, o_ref, acc_ref):
        kt = pl.program_id(1)

        @pl.when(kt == 0)
        def _():
            acc_ref[...] = jnp.zeros_like(acc_ref)

        acc_ref[...] += _dot(a_ref[...], b_ref[...], _TN)

        @pl.when(kt == nk - 1)
        def _():
            o_ref[...] = acc_ref[...].astype(BF16)

    out = pl.pallas_call(
        body, name=name, grid=(p_, nk),
        in_specs=[pl.BlockSpec((None, tk, m), lambda p, kt: (p, kt, 0)), pl.BlockSpec((tk, n), lambda p, kt: (kt, 0))],
        out_specs=pl.BlockSpec((None, m, n), lambda p, kt: (p, 0, 0)), out_shape=jax.ShapeDtypeStruct((p_, m, n), BF16),
        scratch_shapes=[pltpu.VMEM((m, n), F32)], compiler_params=_params("parallel", "arbitrary"),
    )(a3, b)
    return out[0] if squeeze else out


def _matmul(a, b, mode, out_dtype, name, bias=None, resid=None):
    if mode == "nn":
        (m, k), (_, n) = a.shape, b.shape
    elif mode == "nt":
        (m, k), (n, _) = a.shape, b.shape
    else:
        (k, m), (_, n) = a.shape, b.shape
    tm, tn = _matmul_tiles(m, n, k, jnp.dtype(out_dtype).itemsize, resid is not None)
    dims = {"nn": _NN, "nt": _NT, "tn": _TN}[mode]
    a_spec = pl.BlockSpec((k, tm), lambda i, j: (0, i)) if mode == "tn" else pl.BlockSpec((tm, k), lambda i, j: (i, 0))
    b_spec = pl.BlockSpec((tn, k), lambda i, j: (j, 0)) if mode == "nt" else pl.BlockSpec((k, tn), lambda i, j: (0, j))
    in_specs, args = [a_spec, b_spec], [a, b]
    row_spec = pl.BlockSpec((1, tn), lambda i, j: (0, j))
    tile_spec = pl.BlockSpec((tm, tn), lambda i, j: (i, j))
    if bias is not None:
        in_specs.append(row_spec)
        args.append(bias)
    if resid is not None:
        in_specs += [tile_spec, row_spec]
        args += list(resid)
    out_shape = [jax.ShapeDtypeStruct((m, n), out_dtype)]
    out_specs = [tile_spec]
    if resid is not None:
        out_shape.append(jax.ShapeDtypeStruct((m, n), F32))
        out_specs.append(tile_spec)

    def body(*refs):
        a_ref, b_ref = refs[0], refs[1]
        pos = 2
        acc = _dot(a_ref[...], b_ref[...], dims)
        if bias is not None:
            acc = acc + refs[pos][...]
            pos += 1
        if resid is not None:
            x_ref, g_ref = refs[pos], refs[pos + 1]
            pos += 2
        refs[pos][...] = acc.astype(out_dtype)
        if resid is not None:
            refs[pos + 1][...] = x_ref[...] + g_ref[...] * acc

    outs = pl.pallas_call(
        body, name=name, grid=(m // tm, n // tn), in_specs=in_specs, out_specs=out_specs, out_shape=out_shape,
        compiler_params=_params("parallel", "parallel"),
    )(*args)
    return outs if resid is not None else outs[0]


def _modnorm(x, g, sc, sh, name):
    t, d = x.shape
    tm = _tile(t)
    row = pl.BlockSpec((1, d), lambda i: (0, 0))
    blk = pl.BlockSpec((tm, d), lambda i: (i, 0))

    def body(x_ref, g_ref, sc_ref, sh_ref, o_ref):
        x = x_ref[...]
        r = lax.rsqrt(jnp.mean(x * x, axis=-1, keepdims=True) + EPS)
        o_ref[...] = ((x * r) * g_ref[...] * (1.0 + sc_ref[...]) + sh_ref[...]).astype(BF16)

    return pl.pallas_call(
        body, name=name, grid=(t // tm,), in_specs=[blk, row, row, row], out_specs=blk,
        out_shape=jax.ShapeDtypeStruct((t, d), BF16), compiler_params=_params("parallel"),
    )(x, g, sc, sh)


def _modnorm_bwd(x, dh, g, sc, dres, name):
    t, d = x.shape
    tm = _tile(t)
    row = pl.BlockSpec((1, d), lambda i: (0, 0))
    blk = pl.BlockSpec((tm, d), lambda i: (i, 0))

    def body(x_ref, dh_ref, g_ref, sc_ref, dres_ref, dx_ref, dw_ref, dsh_ref):
        @pl.when(pl.program_id(0) == 0)
        def _():
            dw_ref[...] = jnp.zeros_like(dw_ref)
            dsh_ref[...] = jnp.zeros_like(dsh_ref)

        x = x_ref[...]
        dh = dh_ref[...].astype(F32)
        r = lax.rsqrt(jnp.mean(x * x, axis=-1, keepdims=True) + EPS)
        xn = x * r
        dxn = dh * (g_ref[...] * (1.0 + sc_ref[...]))
        dx_ref[...] = dres_ref[...] + r * (dxn - xn * jnp.mean(dxn * xn, axis=-1, keepdims=True))
        dw_ref[...] += _colsum(dh * xn)
        dsh_ref[...] += _colsum(dh)

    return pl.pallas_call(
        body, name=name, grid=(t // tm,), in_specs=[blk, blk, row, row, blk], out_specs=[blk, row, row],
        out_shape=[jax.ShapeDtypeStruct((t, d), F32), jax.ShapeDtypeStruct((1, d), F32), jax.ShapeDtypeStruct((1, d), F32)],
        compiler_params=_params("arbitrary"),
    )(x, dh, g, sc, dres)


def _gate_bwd(dxn, y, gate, name):
    t, d = dxn.shape
    tm = _tile(t)
    row = pl.BlockSpec((1, d), lambda i: (0, 0))
    blk = pl.BlockSpec((tm, d), lambda i: (i, 0))

    def body(dxn_ref, y_ref, g_ref, dy_ref, dg_ref):
        @pl.when(pl.program_id(0) == 0)
        def _():
            dg_ref[...] = jnp.zeros_like(dg_ref)

        dxn = dxn_ref[...]
        dy_ref[...] = (dxn * g_ref[...]).astype(BF16)
        dg_ref[...] += _colsum(dxn * y_ref[...])

    return pl.pallas_call(
        body, name=name, grid=(t // tm,), in_specs=[blk, blk, row], out_specs=[blk, row],
        out_shape=[jax.ShapeDtypeStruct((t, d), BF16), jax.ShapeDtypeStruct((1, d), F32)],
        compiler_params=_params("arbitrary"),
    )(dxn, y, gate)


def _loss_head(y, target, name):
    t, d = y.shape
    tm = _tile(t)
    blk = pl.BlockSpec((tm, d), lambda i: (i, 0))
    one = pl.BlockSpec((1, 1), lambda i: (0, 0))

    def body(y_ref, t_ref, dy_ref, loss_ref, acc_ref):
        @pl.when(pl.program_id(0) == 0)
        def _():
            acc_ref[...] = jnp.zeros_like(acc_ref)

        e = y_ref[...] - t_ref[...]
        dy_ref[...] = e * (1.0 / d)
        acc_ref[...] += _colsum(e * e)

        @pl.when(pl.program_id(0) == pl.num_programs(0) - 1)
        def _():
            loss_ref[...] = jnp.sum(acc_ref[...], axis=1, keepdims=True) * (0.5 / d)

    return pl.pallas_call(
        body, name=name, grid=(t // tm,), in_specs=[blk, blk], out_specs=[blk, one],
        out_shape=[jax.ShapeDtypeStruct((t, d), F32), jax.ShapeDtypeStruct((1, 1), F32)],
        scratch_shapes=[pltpu.VMEM((1, d), F32)], compiler_params=_params("arbitrary"),
    )(y, target)


def _group_norm(vg, gain):
    mu = jnp.mean(vg, axis=-1, keepdims=True)
    xc = vg - mu
    rstd = lax.rsqrt(jnp.mean(xc * xc, axis=-1, keepdims=True) + EPS)
    xhat = xc * rstd
    return xhat, rstd, xhat * gain


def _gmlp_fwd(z, gain, wtril, bias_exp, name):
    t = z.shape[0]
    zu = pl.BlockSpec((CHUNK, A_WIDTH), lambda i: (i, 0))
    zv = pl.BlockSpec((CHUNK, A_WIDTH), lambda i: (i, 1))
    full2 = lambda shp: pl.BlockSpec(shp, lambda i: (0, 0))
    w_spec = pl.BlockSpec((A_GROUPS, CHUNK, CHUNK), lambda i: (0, 0, 0))

    def body(zu_ref, zv_ref, gain_ref, w_ref, b_ref, ya_ref):
        ua = _gelu(zu_ref[...].astype(F32))
        vg = _gelu(zv_ref[...].astype(F32))
        for g in range(A_GROUPS):
            sl = slice(g * GROUP_DIM, (g + 1) * GROUP_DIM)
            _, _, vn = _group_norm(vg[:, sl], gain_ref[:, sl])
            f = _dot(w_ref[g], vn.astype(BF16), _NN) + b_ref[:, sl]
            ya_ref[:, sl] = (ua[:, sl] * f).astype(BF16)

    return pl.pallas_call(
        body, name=name, grid=(t // CHUNK,),
        in_specs=[zu, zv, full2((1, A_WIDTH)), w_spec, full2((CHUNK, A_WIDTH))], out_specs=zu,
        out_shape=jax.ShapeDtypeStruct((t, A_WIDTH + B_WIDTH), BF16), compiler_params=_params("parallel"),
    )(z, z, gain, wtril, bias_exp)


def _gmlp_bwd(z, dcat, gain, wtril, wtril_t, bias_exp, name):
    t = z.shape[0]
    zu = pl.BlockSpec((CHUNK, A_WIDTH), lambda i: (i, 0))
    zv = pl.BlockSpec((CHUNK, A_WIDTH), lambda i: (i, 1))
    full2 = lambda shp: pl.BlockSpec(shp, lambda i: (0, 0))
    w_spec = pl.BlockSpec((A_GROUPS, CHUNK, CHUNK), lambda i: (0, 0, 0))
    dz_spec = pl.BlockSpec((CHUNK, 2 * A_WIDTH), lambda i: (i, 0))

    def body(zu_ref, zv_ref, dya_ref, gain_ref, w_ref, wt_ref, b_ref, dz_ref, dw_ref, dgain_ref, dbias_ref):
        @pl.when(pl.program_id(0) == 0)
        def _():
            dw_ref[...] = jnp.zeros_like(dw_ref)
            dgain_ref[...] = jnp.zeros_like(dgain_ref)
            dbias_ref[...] = jnp.zeros_like(dbias_ref)

        zu_v = zu_ref[...].astype(F32)
        zv_v = zv_ref[...].astype(F32)
        dya = dya_ref[...].astype(F32)
        ua = _gelu(zu_v)
        vg = _gelu(zv_v)
        row = lax.broadcasted_iota(jnp.int32, (CHUNK, CHUNK), 0)
        col = lax.broadcasted_iota(jnp.int32, (CHUNK, CHUNK), 1)
        for g in range(A_GROUPS):
            sl = slice(g * GROUP_DIM, (g + 1) * GROUP_DIM)
            gain_g = gain_ref[:, sl]
            xhat, rstd, vn = _group_norm(vg[:, sl], gain_g)
            vn16 = vn.astype(BF16)
            f = _dot(w_ref[g], vn16, _NN) + b_ref[:, sl]
            df = dya[:, sl] * ua[:, sl]
            df16 = df.astype(BF16)
            dz_ref[:, sl] = (dya[:, sl] * f * _gelu_grad(zu_v[:, sl])).astype(BF16)
            dw_ref[g] += jnp.where(row >= col, _dot(df16, vn16, _NT), 0.0)
            dvn = _dot(wt_ref[g], df16, _NN)
            dgain_ref[:, sl] += _colsum(dvn * xhat)
            dxh = dvn * gain_g
            dvg = rstd * (dxh - jnp.mean(dxh, axis=-1, keepdims=True) - xhat * jnp.mean(dxh * xhat, axis=-1, keepdims=True))
            dz_ref[:, A_WIDTH + g * GROUP_DIM:A_WIDTH + (g + 1) * GROUP_DIM] = (dvg * _gelu_grad(zv_v[:, sl])).astype(BF16)
            dbias_ref[:, sl] += df

    return pl.pallas_call(
        body, name=name, grid=(t // CHUNK,),
        in_specs=[zu, zv, zu, full2((1, A_WIDTH)), w_spec, w_spec, full2((CHUNK, A_WIDTH))],
        out_specs=[dz_spec, w_spec, full2((1, A_WIDTH)), full2((CHUNK, A_WIDTH))],
        out_shape=[jax.ShapeDtypeStruct((t, 2 * A_WIDTH), BF16), jax.ShapeDtypeStruct((A_GROUPS, CHUNK, CHUNK), F32),
                   jax.ShapeDtypeStruct((1, A_WIDTH), F32), jax.ShapeDtypeStruct((CHUNK, A_WIDTH), F32)],
        compiler_params=_params("arbitrary"),
    )(z, z, dcat, gain, wtril, wtril_t, bias_exp)


def _rope_tables(pos, inv_freq, sign, name):
    t = pos.shape[0]
    tm = _tile(t)
    row = pl.BlockSpec((1, B_WIDTH), lambda i: (0, 0))
    blk = pl.BlockSpec((tm, B_WIDTH), lambda i: (i, 0))

    def body(pos_ref, f_ref, s_ref, cos_ref, sin_ref):
        ang = pos_ref[...] * f_ref[:, 0:LANES]
        cos_ref[...] = jnp.tile(jnp.cos(ang), (1, B_WIDTH // LANES))
        sin_ref[...] = jnp.tile(jnp.sin(ang) * s_ref[:, 0:LANES], (1, B_WIDTH // LANES))

    return pl.pallas_call(
        body, name=name, grid=(t // tm,), in_specs=[pl.BlockSpec((tm, 1), lambda i: (i, 0)), row, row],
        out_specs=[blk, blk], out_shape=[jax.ShapeDtypeStruct((t, B_WIDTH), F32)] * 2,
        compiler_params=_params("parallel"),
    )(pos, inv_freq, sign)


def _head_sum(v, seg):
    hi = v.astype(BF16)
    lo = (v - hi.astype(F32)).astype(BF16)
    return _dot(hi, seg, _NN) + _dot(lo, seg, _NN)


def _swap_halves(v):
    lane = lax.broadcasted_iota(jnp.int32, v.shape, 1)
    return jnp.where((lane & (HEAD_DIM - 1)) < HEAD_DIM // 2,pltpu.roll(v, B_WIDTH - HEAD_DIM // 2, 1), pltpu.roll(v, HEAD_DIM // 2, 1))


DILATIONS = tuple(dil for _, dil in PATTERNS)
SUBSEQ_TM = 256
LANES = 128


def _subseq_shape(t, dil):
    return (t // dil, dil * B_WIDTH)


def _subseq_spec(tm, dil):
    return pl.BlockSpec((tm // dil, dil * B_WIDTH), lambda i: (i, 0))


def _to_subseq(x, scr_ref, dil):
    if dil == 1:
        return x
    tm, w = x.shape
    for c in range(w // LANES):
        scr_ref[c * tm:(c + 1) * tm, :] = x[:, c * LANES:(c + 1) * LANES]
    return jnp.concatenate([scr_ref[pl.ds(c * tm + r, tm // dil, stride=dil), :]
                            for r in range(dil) for c in range(w // LANES)], axis=1)


def _from_subseq(y, scr_ref, dil):
    if dil == 1:
        return y
    n, w = y.shape[0], y.shape[1] // dil
    tm = n * dil
    for r in range(dil):
        for c in range(w // LANES):
            scr_ref[pl.ds(c * tm + r, n, stride=dil), :] = y[:, r * w + c * LANES:r * w + (c + 1) * LANES]
    return jnp.concatenate([scr_ref[c * tm:(c + 1) * tm, :] for c in range(w // LANES)], axis=1)


def _subseq_scratch(tm):
    return pltpu.VMEM((B_WIDTH // LANES * tm, LANES), F32)


def _qk_prep(z, cos_t, sin_t, gq, gk, seg, name):
    t = z.shape[0]
    tm = _tile(t, (SUBSEQ_TM,))
    col = lambda c: pl.BlockSpec((tm, B_WIDTH), lambda i: (i, c))
    row = pl.BlockSpec((1, B_WIDTH), lambda i: (0, 0))
    blk = col(0)
    nd = len(DILATIONS)

    def body(q_ref, k_ref, v_ref, cos_ref, sin_ref, gq_ref, gk_ref, seg_ref, *rest):
        out_refs, scr_ref = rest[:-1], rest[-1]

        def norm_rot(x, g):
            r = lax.rsqrt(_head_sum(x * x, seg_ref[...]) * (1.0 / HEAD_DIM) + EPS)
            xn = x * r * g
            return xn * cos_ref[...] + _swap_halves(xn) * sin_ref[...]

        vals = (norm_rot(q_ref[...].astype(F32), gq_ref[...]), norm_rot(k_ref[...].astype(F32), gk_ref[...]),
                v_ref[...].astype(F32))
        for a, val in enumerate(vals):
            for b, dil in enumerate(DILATIONS):
                out_refs[a * nd + b][...] = _to_subseq(val, scr_ref, dil).astype(BF16)

    outs = pl.pallas_call(
        body, name=name, grid=(t // tm,),
        in_specs=[col(2), col(3), col(4), blk, blk, row, row, pl.BlockSpec((B_WIDTH, B_WIDTH), lambda i: (0, 0))],
        out_specs=[_subseq_spec(tm, dil) for _ in range(3) for dil in DILATIONS],
        out_shape=[jax.ShapeDtypeStruct(_subseq_shape(t, dil), BF16) for _ in range(3) for dil in DILATIONS],
        scratch_shapes=[_subseq_scratch(tm)], compiler_params=_params("parallel"),
    )(z, z, z, cos_t, sin_t, gq, gk, seg)
    return outs[:nd], outs[nd:2 * nd], outs[2 * nd:]


def _qk_prep_bwd(z, dqs, dks, dvs, cos_t, sin_t, gq, gk, seg, name):
    t = z.shape[0]
    tm = _tile(t, (SUBSEQ_TM,))
    col = lambda c: pl.BlockSpec((tm, B_WIDTH), lambda i: (i, c))
    row = pl.BlockSpec((1, B_WIDTH), lambda i: (0, 0))
    blk = col(0)
    nb = len(DILATIONS)
    subs = [_subseq_spec(tm, dil) for dil in DILATIONS]

    def body(*refs):
        q_ref, k_ref = refs[0], refs[1]
        dq_refs, dk_refs, dv_refs = refs[2:2 + nb], refs[2 + nb:2 + 2 * nb], refs[2 + 2 * nb:2 + 3 * nb]
        cos_ref, sin_ref, gq_ref, gk_ref, seg_ref, dz_ref, dgq_ref, dgk_ref, scr_ref = refs[2 + 3 * nb:]

        @pl.when(pl.program_id(0) == 0)
        def _():
            dgq_ref[...] = jnp.zeros_like(dgq_ref)
            dgk_ref[...] = jnp.zeros_like(dgk_ref)

        def total(d_refs):
            return sum(_from_subseq(r_[...], scr_ref, dil) for r_, dil in zip(d_refs, DILATIONS))

        def back(x, d_refs, g, dg_ref):
            dout = total(d_refs)
            dy = dout * cos_ref[...] + _swap_halves(dout * sin_ref[...])
            r = lax.rsqrt(_head_sum(x * x, seg_ref[...]) * (1.0 / HEAD_DIM) + EPS)
            xn = x * r
            dg_ref[...] += _colsum(dy * xn)
            dxn = dy * g
            return r * (dxn - xn * (_head_sum(dxn * xn, seg_ref[...]) * (1.0 / HEAD_DIM)))

        dz_ref[:, 0:B_WIDTH] = back(q_ref[...].astype(F32), dq_refs, gq_ref[...], dgq_ref).astype(BF16)
        dz_ref[:, B_WIDTH:2 * B_WIDTH] = back(k_ref[...].astype(F32), dk_refs, gk_ref[...], dgk_ref).astype(BF16)
        dz_ref[:, 2 * B_WIDTH:3 * B_WIDTH] = total(dv_refs).astype(BF16)

    return pl.pallas_call(
        body, name=name, grid=(t // tm,),
        in_specs=[col(2), col(3)] + subs * 3 + [blk, blk, row, row, pl.BlockSpec((B_WIDTH, B_WIDTH), lambda i: (0, 0))],
        out_specs=[pl.BlockSpec((tm, 3 * B_WIDTH), lambda i: (i, 0)), row, row],
        out_shape=[jax.ShapeDtypeStruct((t, 3 * B_WIDTH), BF16), jax.ShapeDtypeStruct((1, B_WIDTH), F32),
                   jax.ShapeDtypeStruct((1, B_WIDTH), F32)],
        scratch_shapes=[_subseq_scratch(tm)], compiler_params=_params("arbitrary"),
    )(z, z, *dqs, *dks, *dvs, cos_t, sin_t, gq, gk, seg)


def _subseq_views(x, col, name):
    t = x.shape[0]
    tm = _tile(t, (SUBSEQ_TM,))

    def body(x_ref, *rest):
        out_refs, scr_ref = rest[:-1], rest[-1]
        val = x_ref[...].astype(F32)
        for o_ref, dil in zip(out_refs, DILATIONS):
            o_ref[...] = _to_subseq(val, scr_ref, dil).astype(o_ref.dtype)

    return pl.pallas_call(
        body, name=name, grid=(t // tm,), in_specs=[pl.BlockSpec((tm, B_WIDTH), lambda i: (i, col))],
        out_specs=[_subseq_spec(tm, dil) for dil in DILATIONS],
        out_shape=[jax.ShapeDtypeStruct(_subseq_shape(t, dil), x.dtype) for dil in DILATIONS],
        scratch_shapes=[_subseq_scratch(tm)], compiler_params=_params("parallel"),
    )(x)


def _attn_fwd(q, k, v, dil, name):
    t = q.shape[0] * dil
    nb = t // dil // Q_BLOCK
    cur = pl.BlockSpec((Q_BLOCK, B_WIDTH), lambda r, i: (i, r))
    prev = pl.BlockSpec((Q_BLOCK, B_WIDTH), lambda r, i: (jnp.maximum(i - 1, 0), r))

    def body(q_ref, kp_ref, kc_ref, vp_ref, vc_ref, o_ref, lse_ref):
        i = pl.program_id(1)
        q = q_ref[...]
        kk = jnp.concatenate([kp_ref[...], kc_ref[...]], axis=0)
        vv = jnp.concatenate([vp_ref[...], vc_ref[...]], axis=0)
        a = lax.broadcasted_iota(jnp.int32, (Q_BLOCK, 2 * Q_BLOCK), 0)
        j = lax.broadcasted_iota(jnp.int32, (Q_BLOCK, 2 * Q_BLOCK), 1)
        dist = a + Q_BLOCK - j
        mask = (dist >= 0) & (dist <= Q_BLOCK) & ((j >= Q_BLOCK) | (i > 0))
        sls = [slice(h * HEAD_DIM, (h + 1) * HEAD_DIM) for h in range(HEADS)]
        scores = [_dot(q[:, sl], kk[:, sl], _NT) for sl in sls]
        ps, dens = [], []
        for sl, s in zip(sls, scores):
            s = jnp.where(mask, s * (HEAD_DIM ** -0.5), NEG)
            m = jnp.max(s, axis=-1, keepdims=True)
            p = jnp.exp(s - m)
            den = jnp.sum(p, axis=-1, keepdims=True)
            ps.append(p.astype(BF16))
            dens.append(den)
            lse_ref[:, sl] = jnp.broadcast_to(m + jnp.log(den), (Q_BLOCK, HEAD_DIM))
        for sl, p, den in zip(sls, ps, dens):
            o_ref[:, sl] = _dot(p, vv[:, sl], _NN) / den

    return pl.pallas_call(
        body, name=name, grid=(dil, nb), in_specs=[cur, prev, cur, prev, cur], out_specs=[cur, cur],
        out_shape=[jax.ShapeDtypeStruct(_subseq_shape(t, dil), F32)] * 2,
        compiler_params=_params("parallel", "parallel"),
    )(q, k, k, v, v)


def _attn_merge(outs, lses, cat, name):
    nb = len(DILATIONS)
    t = cat.shape[0]
    tm = _tile(t, (SUBSEQ_TM,))
    subs = [_subseq_spec(tm, dil) for dil in DILATIONS]

    def body(*refs):
        o_refs, l_refs = refs[:nb], refs[nb:2 * nb]
        yb_refs, lse_refs, cat_ref, scr_ref = refs[2 * nb + 1:3 * nb + 1], refs[3 * nb + 1:4 * nb + 1], refs[4 * nb + 1], refs[4 * nb + 2]
        ls = [_from_subseq(r[...], scr_ref, dil) for r, dil in zip(l_refs, DILATIONS)]
        m = functools.reduce(jnp.maximum, ls)
        tot = m + jnp.log(sum(jnp.exp(l - m) for l in ls))
        yb = sum(jnp.exp(l - tot) * _from_subseq(o[...], scr_ref, dil) for l, o, dil in zip(ls, o_refs, DILATIONS))
        cat_ref[...] = yb.astype(BF16)
        yb = yb.astype(BF16).astype(F32)
        for yb_ref, lse_ref, dil in zip(yb_refs, lse_refs, DILATIONS):
            yb_ref[...] = _to_subseq(yb, scr_ref, dil).astype(BF16)
            lse_ref[...] = _to_subseq(tot, scr_ref, dil)

    outs_ = pl.pallas_call(
        body, name=name, grid=(t // tm,), in_specs=subs * 2 + [pl.BlockSpec(memory_space=pl.ANY)],
        out_specs=subs * 2 + [pl.BlockSpec((tm, B_WIDTH), lambda i: (i, A_WIDTH // B_WIDTH))],
        out_shape=[jax.ShapeDtypeStruct(_subseq_shape(t, dil), BF16) for dil in DILATIONS]
        + [jax.ShapeDtypeStruct(_subseq_shape(t, dil), F32) for dil in DILATIONS] + [jax.ShapeDtypeStruct(cat.shape, BF16)],
        input_output_aliases={2 * nb: 2 * nb}, scratch_shapes=[_subseq_scratch(tm)], compiler_params=_params("parallel"),
    )(*outs, *lses, cat)
    return outs_[:nb], outs_[nb:2 * nb], outs_[2 * nb]


def _attn_bwd(q, k, v, do, o, lse, dil, name):
    t = q.shape[0] * dil
    nb = t // dil // Q_BLOCK
    blk = lambda f: pl.BlockSpec((Q_BLOCK, B_WIDTH), lambda r, i: (f(i), r))
    cur = blk(lambda i: jnp.minimum(i, nb - 1))
    prev = blk(lambda i: jnp.clip(i - 1, 0, nb - 1))
    scale = HEAD_DIM ** -0.5

    def body(q_ref, kp_ref, kc_ref, vp_ref, vc_ref, do_ref, o_ref, lse_ref, dq_ref, dk_ref, dv_ref,
             ck_ref, cv_ref, tk_ref, tv_ref):
        i = pl.program_id(1)

        @pl.when(i == 0)
        def _():
            ck_ref[...] = jnp.zeros_like(ck_ref)
            cv_ref[...] = jnp.zeros_like(cv_ref)

        @pl.when(i < nb)
        def _():
            q = q_ref[...]
            kk = jnp.concatenate([kp_ref[...], kc_ref[...]], axis=0)
            vv = jnp.concatenate([vp_ref[...], vc_ref[...]], axis=0)
            do = do_ref[...]
            dof = do.astype(F32)
            of = o_ref[...].astype(F32)
            a = lax.broadcasted_iota(jnp.int32, (Q_BLOCK, 2 * Q_BLOCK), 0)
            j = lax.broadcasted_iota(jnp.int32, (Q_BLOCK, 2 * Q_BLOCK), 1)
            dist = a + Q_BLOCK - j
            mask = (dist >= 0) & (dist <= Q_BLOCK) & ((j >= Q_BLOCK) | (i > 0))
            sls = [slice(h * HEAD_DIM, (h + 1) * HEAD_DIM) for h in range(HEADS)]
            scores = [_dot(q[:, sl], kk[:, sl], _NT) for sl in sls]
            dps = [_dot(do[:, sl], vv[:, sl], _NT) for sl in sls]
            ps, dss = [], []
            for sl, s, dp in zip(sls, scores, dps):
                p = jnp.exp(jnp.where(mask, s * scale, NEG) - lse_ref[:, sl.start:sl.start + 1])
                delta = jnp.sum(dof[:, sl] * of[:, sl], axis=-1, keepdims=True)
                dss.append((p * (dp - delta) * scale).astype(BF16))
                ps.append(p.astype(BF16))
            for sl, p, ds in zip(sls, ps, dss):
                dq_ref[:, sl] = _dot(ds, kk[:, sl], _NN)
                dv_t = _dot(do[:, sl], p, _TN)
                dk_t = _dot(q[:, sl], ds, _TN)
                tk_ref[sl, :] = ck_ref[sl, :] + dk_t[:, :Q_BLOCK]
                tv_ref[sl, :] = cv_ref[sl, :] + dv_t[:, :Q_BLOCK]
                ck_ref[sl, :] = dk_t[:, Q_BLOCK:]
                cv_ref[sl, :] = dv_t[:, Q_BLOCK:]

        @pl.when(i == nb)
        def _():
            tk_ref[...] = ck_ref[...]
            tv_ref[...] = cv_ref[...]

        @pl.when(i >= 1)
        def _():
            dk_ref[...] = tk_ref[...].T
            dv_ref[...] = tv_ref[...].T

    return pl.pallas_call(
        body, name=name, grid=(dil, nb + 1), in_specs=[cur, prev, cur, prev, cur, cur, cur, cur],
        out_specs=[cur, prev, prev], out_shape=[jax.ShapeDtypeStruct(_subseq_shape(t, dil), F32)] * 3,
        scratch_shapes=[pltpu.VMEM((B_WIDTH, Q_BLOCK), F32)] * 4,
        compiler_params=_params("parallel", "arbitrary"),
    )(q, k, k, v, v, do, o, lse)


FFN_TN = 256
FFN_FWD_CHUNK = 256
FFN_BWD_CHUNK = 128


def _ffn_up(h, up_t, name):
    t, k = h.shape
    tm = _tile(t)

    def body(h_ref, w_ref, o_ref):
        o_ref[...] = _dot(h_ref[...], w_ref[...], _NT).astype(BF16)

    return pl.pallas_call(
        body, name=name, grid=(2, t // tm),
        in_specs=[pl.BlockSpec((tm, k), lambda p, i: (i, 0)), pl.BlockSpec((None, FFN_DIM, k), lambda p, i: (p, 0, 0))],
        out_specs=pl.BlockSpec((None, tm, FFN_DIM), lambda p, i: (p, i, 0)),
        out_shape=jax.ShapeDtypeStruct((2, t, FFN_DIM), BF16), compiler_params=_params("parallel", "parallel"),
    )(h, up_t.reshape(2, FFN_DIM, k))


def _ffn_up_dx(du, up_t, name):
    t = du.shape[1]
    k = up_t.shape[1]
    tm = _tile(t)

    def body(a_ref, b_ref, o_ref):
        o_ref[...] = _dot(a_ref[0], b_ref[0], _NN) + _dot(a_ref[1], b_ref[1], _NN)

    return pl.pallas_call(
        body, name=name, grid=(t // tm,),
        in_specs=[pl.BlockSpec((2, tm, FFN_DIM), lambda i: (0, i, 0)), pl.BlockSpec((2, FFN_DIM, k), lambda i: (0, 0, 0))],
        out_specs=pl.BlockSpec((tm, k), lambda i: (i, 0)), out_shape=jax.ShapeDtypeStruct((t, k), F32),
        compiler_params=_params("parallel"),
    )(du, up_t.reshape(2, FFN_DIM, k))


def _ffn_conv(win, w_ref, b_ref, p):
    x = win.astype(F32)
    x0, x1, x2 = x[FFN_HALO:], pltpu.roll(x, 1, 0)[FFN_HALO:], pltpu.roll(x, 2, 0)[FFN_HALO:]
    return b_ref[p] + w_ref[p, 2:3, :] * x0 + w_ref[p, 1:2, :] * x1 + w_ref[p, 0:1, :] * x2


def _zero_if(cond, v):
    return jnp.where(cond, 0, v).astype(v.dtype)


def _ffn_act(u, dw_w, dw_b, name):
    t = u.shape[1]
    tm = _tile(t)
    chunk = min(FFN_FWD_CHUNK, tm)
    hb = tm // FFN_HALO
    main = pl.BlockSpec((2, tm, FFN_TN), lambda i, j: (0, i, j))
    halo = pl.BlockSpec((2, FFN_HALO, FFN_TN), lambda i, j: (0, jnp.maximum(i * hb - 1, 0), j))
    wsp = pl.BlockSpec((2, FFN_CONV_WIDTH, FFN_TN), lambda i, j: (0, 0, j))
    bsp = pl.BlockSpec((2, 1, FFN_TN), lambda i, j: (0, 0, j))

    def body(u_ref, uh_ref, w_ref, b_ref, o_ref, z_ref):
        first = pl.program_id(0) == 0

        def emit(rows, wins):
            za, zb = _ffn_conv(wins[0], w_ref, b_ref, 0), _ffn_conv(wins[1], w_ref, b_ref, 1)
            o_ref[rows, :] = (za * _sigmoid(za) * zb).astype(BF16)
            z_ref[0, rows, :] = za.astype(BF16)
            z_ref[1, rows, :] = zb.astype(BF16)

        emit(pl.ds(0, chunk), [jnp.concatenate([_zero_if(first, uh_ref[p]), u_ref[p, 0:chunk, :]], axis=0) for p in range(2)])

        def step(c, carry):
            s = pl.multiple_of(c * chunk, chunk)
            emit(pl.ds(s, chunk), [u_ref[p, pl.ds(s - FFN_HALO, chunk + FFN_HALO), :] for p in range(2)])
            return carry

        lax.fori_loop(1, tm // chunk, step, 0)

    return pl.pallas_call(
        body, name=name, grid=(t // tm, FFN_DIM // FFN_TN), in_specs=[main, halo, wsp, bsp],
        out_specs=[pl.BlockSpec((tm, FFN_TN), lambda i, j: (i, j)), main],
        out_shape=[jax.ShapeDtypeStruct((t, FFN_DIM), BF16), jax.ShapeDtypeStruct((2, t, FFN_DIM), BF16)],
        compiler_params=_params("parallel", "parallel"),
    )(u, u, dw_w, dw_b)


def _fold8(v):
    return jnp.sum(v.reshape(v.shape[0] // 8, 8, v.shape[1]), axis=0)


def _ffn_act_bwd(u, z, dact, dw_w, name):
    t = u.shape[1]
    tm = _tile(t)
    chunk = min(FFN_BWD_CHUNK, tm // 2)
    halo = FFN_HALO
    hb = tm // halo
    nt = t // tm
    last_halo = t // halo - 1
    next_i = lambda i: jnp.minimum((i + 1) * hb, last_halo)
    main = pl.BlockSpec((2, tm, FFN_TN), lambda j, i: (0, i, j))
    nxt = pl.BlockSpec((2, halo, FFN_TN), lambda j, i: (0, next_i(i), j))
    wsp = pl.BlockSpec((2, FFN_CONV_WIDTH, FFN_TN), lambda j, i: (0, 0, j))
    bsp = pl.BlockSpec((2, 1, FFN_TN), lambda j, i: (0, 0, j))

    def body(u_ref, z_ref, zn_ref, da_ref, dan_ref, w_ref, du_ref, dw_ref, db_ref, acc_ref):
        i = pl.program_id(1)
        last = i == nt - 1
        acc_ref[...] = jnp.zeros_like(acc_ref)

        def emit(rows, zs, dact):
            n = chunk + halo
            za, zb, dact = zs[0].astype(F32), zs[1].astype(F32), dact.astype(F32)
            sg = _sigmoid(za)
            dzs = (dact * zb * (sg * (1.0 + za * (1.0 - sg))), dact * (za * sg))
            for p, dz in enumerate(dzs):
                ahead = (dz[:chunk], pltpu.roll(dz, n - 1, 0)[:chunk], pltpu.roll(dz, n - 2, 0)[:chunk])
                um = u_ref[p, rows, :].astype(F32)
                acc_ref[p, FFN_CONV_WIDTH] += _fold8(ahead[0])
                du = None
                for j, dzj in enumerate(ahead):
                    k = FFN_CONV_WIDTH - 1 - j
                    acc_ref[p, k] += _fold8(dzj * um)
                    term = w_ref[p, k:k + 1, :] * dzj
                    du = term if du is None else du + term
                du_ref[p, rows, :] = du.astype(BF16)

        def step(c, carry):
            s = pl.multiple_of(c * chunk, chunk)
            emit(pl.ds(s, chunk), [z_ref[p, pl.ds(s, chunk + halo), :] for p in range(2)], da_ref[pl.ds(s, chunk + halo), :])
            return carry

        lax.fori_loop(0, tm // chunk - 1, step, 0)
        s = tm - chunk
        emit(pl.ds(s, chunk),
             [jnp.concatenate([z_ref[p, s:tm, :], zn_ref[p]], axis=0) for p in range(2)],
             jnp.concatenate([da_ref[s:tm, :], _zero_if(last, dan_ref[...])], axis=0))

        @pl.when(i == 0)
        def _():
            dw_ref[...] = jnp.zeros_like(dw_ref)
            db_ref[...] = jnp.zeros_like(db_ref)

        for p in range(2):
            for k in range(FFN_CONV_WIDTH):
                dw_ref[p, k:k + 1, :] += _colsum(acc_ref[p, k])
            db_ref[p] += _colsum(acc_ref[p, FFN_CONV_WIDTH])

    return pl.pallas_call(
        body, name=name, grid=(FFN_DIM // FFN_TN, nt),
        in_specs=[main, main, nxt, pl.BlockSpec((tm, FFN_TN), lambda j, i: (i, j)),
                  pl.BlockSpec((halo, FFN_TN), lambda j, i: (next_i(i), j)), wsp],
        out_specs=[main, wsp, bsp],
        out_shape=[jax.ShapeDtypeStruct((2, t, FFN_DIM), BF16), jax.ShapeDtypeStruct((2, FFN_CONV_WIDTH, FFN_DIM), F32),
                   jax.ShapeDtypeStruct((2, 1, FFN_DIM), F32)],
        scratch_shapes=[pltpu.VMEM((2, FFN_CONV_WIDTH + 1, 8, FFN_TN), F32)],
        compiler_params=_params("parallel", "arbitrary"),
    )(u, z, z, dact, dact, dw_w)


CONV_TM = 256
CONV_ROWS = 128
CONV_LANES = 128


def _glu_window(pa_ref, pah_ref, pg_ref, pgh_ref, scr_ref, first):
    ah, gh = pah_ref[...].astype(F32), pgh_ref[...].astype(F32)
    scr_ref[0:CONV_HALO, :] = jnp.where(first, 0.0, ah * _sigmoid(gh))
    scr_ref[CONV_HALO:, :] = pa_ref[...].astype(F32) * _sigmoid(pg_ref[...].astype(F32))


def _tap_slabs(win, rows, ahead):
    n = win.shape[0]
    for s in range(8):
        ws = win if s == 0 else pltpu.roll(win, n - s if ahead else s, 0)
        for q in range(CONV_HALO // 8):
            o = 8 * q + s
            if o < CONV_WIDTH:
                start = 8 * q if ahead else CONV_HALO - 8 * q
                yield CONV_WIDTH - 1 - o, ws[start:start + rows]


def _conformer_specs(t):
    tm = _tile(t, (CONV_TM, 128))
    hb = tm // CONV_HALO
    d = D_MODEL
    main = lambda c: pl.BlockSpec((tm, d), lambda i: (i, c))
    halo = lambda c: pl.BlockSpec((CONV_HALO, d), lambda i: (jnp.maximum(i * hb - 1, 0), c))
    row = pl.BlockSpec((1, d), lambda i: (0, 0))
    wsp = pl.BlockSpec((CONV_WIDTH, d), lambda i: (0, 0))
    return tm, main, halo, row, wsp


def _conformer_mid(p, dw_w, dw_b, ln_g, ln_b, name):
    t = p.shape[0]
    tm, main, halo, row, wsp = _conformer_specs(t)
    d, lanes = D_MODEL, CONV_LANES

    def body(pa_ref, pah_ref, pg_ref, pgh_ref, w_ref, b_ref, g_ref, lb_ref, o_ref, dc_ref, scr_ref):
        _glu_window(pa_ref, pah_ref, pg_ref, pgh_ref, scr_ref, pl.program_id(0) == 0)
        for c in range(d // lanes):
            ls = slice(c * lanes, (c + 1) * lanes)
            acc = jnp.broadcast_to(b_ref[:, ls], (tm, lanes))
            for k, slab in _tap_slabs(scr_ref[:, ls], tm, False):
                acc = acc + w_ref[k:k + 1, ls] * slab
            dc_ref[:, ls] = acc

        def norm(r, carry):
            r0 = pl.multiple_of(r * 32, 32)
            dc = dc_ref[pl.ds(r0, 32), :]
            xc = dc - jnp.mean(dc, axis=-1, keepdims=True)
            ln = xc * lax.rsqrt(jnp.mean(xc * xc, axis=-1, keepdims=True) + EPS) * g_ref[...] + lb_ref[...]
            o_ref[pl.ds(r0, 32), :] = (ln * _sigmoid(ln)).astype(BF16)
            return carry

        lax.fori_loop(0, tm // 32, norm, 0)

    return pl.pallas_call(
        body, name=name, grid=(t // tm,), in_specs=[main(0), halo(0), main(1), halo(1), wsp, row, row, row],
        out_specs=[main(0), main(0)], out_shape=[jax.ShapeDtypeStruct((t, d), BF16), jax.ShapeDtypeStruct((t, d), F32)],
        scratch_shapes=[pltpu.VMEM((tm + CONV_HALO, d), F32)], compiler_params=_params("parallel"),
    )(p, p, p, p, dw_w, dw_b, ln_g, ln_b)


def _conformer_mid_bwd(p, dc, ds, ln_g, ln_b, name):
    t = p.shape[0]
    tm, main, halo, row, wsp = _conformer_specs(t)
    d, nt = D_MODEL, t // tm
    rows, lanes = CONV_ROWS, CONV_LANES

    def body(pa_ref, pah_ref, pg_ref, pgh_ref, dc_ref, ds_ref, g_ref, lb_ref,
             ddc_ref, dw_ref, db_ref, dg_ref, dlb_ref, scr_ref, wacc_ref, racc_ref):
        i = pl.program_id(0)

        @pl.when(i == 0)
        def _():
            wacc_ref[...] = jnp.zeros_like(wacc_ref)
            racc_ref[...] = jnp.zeros_like(racc_ref)

        _glu_window(pa_ref, pah_ref, pg_ref, pgh_ref, scr_ref, i == 0)

        def norm_bwd(r, carry):
            r0 = pl.multiple_of(r * 32, 32)
            dcv = dc_ref[pl.ds(r0, 32), :]
            xc = dcv - jnp.mean(dcv, axis=-1, keepdims=True)
            rstd = lax.rsqrt(jnp.mean(xc * xc, axis=-1, keepdims=True) + EPS)
            xhat = xc * rstd
            ln = xhat * g_ref[...] + lb_ref[...]
            sg = _sigmoid(ln)
            dln = ds_ref[pl.ds(r0, 32), :].astype(F32) * (sg * (1.0 + ln * (1.0 - sg)))
            dxh = dln * g_ref[...]
            ddc = rstd * (dxh - jnp.mean(dxh, axis=-1, keepdims=True) - xhat * jnp.mean(dxh * xhat, axis=-1, keepdims=True))
            ddc_ref[pl.ds(r0, 32), :] = ddc
            racc_ref[0] += _fold8(dln * xhat)
            racc_ref[1] += _fold8(dln)
            racc_ref[2] += _fold8(ddc)
            return carry

        lax.fori_loop(0, tm // 32, norm_bwd, 0)

        for c in range(d // lanes):
            ls = slice(c * lanes, (c + 1) * lanes)

            def taps(r, carry, ls=ls):
                r0 = pl.multiple_of(r * rows, rows)
                ddc = ddc_ref[pl.ds(r0, rows), ls]
                for k, slab in _tap_slabs(scr_ref[pl.ds(r0, rows + CONV_HALO), ls], rows, False):
                    wacc_ref[k, :, ls] += _fold8(ddc * slab)
                return carry

            lax.fori_loop(0, tm // rows, taps, 0)

        @pl.when(i == nt - 1)
        def _():
            for k in range(CONV_WIDTH):
                dw_ref[k:k + 1, :] = _colsum(wacc_ref[k])
            dg_ref[...] = _colsum(racc_ref[0])
            dlb_ref[...] = _colsum(racc_ref[1])
            db_ref[...] = _colsum(racc_ref[2])

    return pl.pallas_call(
        body, name=name, grid=(nt,), in_specs=[main(0), halo(0), main(1), halo(1), main(0), main(0), row, row],
        out_specs=[main(0), wsp, row, row, row],
        out_shape=[jax.ShapeDtypeStruct((t, d), F32), jax.ShapeDtypeStruct((CONV_WIDTH, d), F32)]
        + [jax.ShapeDtypeStruct((1, d), F32)] * 3,
        scratch_shapes=[pltpu.VMEM((tm + CONV_HALO, d), F32), pltpu.VMEM((CONV_WIDTH, 8, d), F32), pltpu.VMEM((3, 8, d), F32)],
        compiler_params=_params("arbitrary"),
    )(p, p, p, p, dc, ds, ln_g, ln_b)


def _conformer_glu_bwd(p, ddc, dw_w, name):
    t = p.shape[0]
    d = D_MODEL
    tm = _tile(t, (CONV_TM, 128))
    hb = tm // CONV_HALO
    nt = t // tm
    last_halo = t // CONV_HALO - 1
    rows, lanes = CONV_ROWS, CONV_LANES
    col = lambda c: pl.BlockSpec((tm, d), lambda i: (i, c))
    nxt = pl.BlockSpec((CONV_HALO, d), lambda i: (jnp.minimum((i + 1) * hb, last_halo), 0))

    def body(pa_ref, pg_ref, ddc_ref, ddcn_ref, w_ref, dp_ref, db_ref, scr_ref, acc_ref):
        i = pl.program_id(0)

        @pl.when(i == 0)
        def _():
            acc_ref[...] = jnp.zeros_like(acc_ref)

        scr_ref[0:tm, :] = ddc_ref[...]
        scr_ref[tm:, :] = _zero_if(i == nt - 1, ddcn_ref[...])
        for c in range(d // lanes):
            ls = slice(c * lanes, (c + 1) * lanes)
            gs = slice(d + c * lanes, d + (c + 1) * lanes)

            def taps(r, carry, ls=ls, gs=gs):
                r0 = pl.multiple_of(r * rows, rows)
                dglu = None
                for k, slab in _tap_slabs(scr_ref[pl.ds(r0, rows + CONV_HALO), ls], rows, True):
                    term = w_ref[k:k + 1, ls] * slab
                    dglu = term if dglu is None else dglu + term
                a = pa_ref[pl.ds(r0, rows), ls].astype(F32)
                sg = _sigmoid(pg_ref[pl.ds(r0, rows), ls].astype(F32))
                da = (dglu * sg).astype(BF16)
                dg = (dglu * a * sg * (1.0 - sg)).astype(BF16)
                dp_ref[pl.ds(r0, rows), ls] = da
                dp_ref[pl.ds(r0, rows), gs] = dg
                acc_ref[:, ls] += _fold8(da.astype(F32))
                acc_ref[:, gs] += _fold8(dg.astype(F32))
                return carry

            lax.fori_loop(0, tm // rows, taps, 0)

        @pl.when(i == nt - 1)
        def _():
            db_ref[...] = _colsum(acc_ref[...])

    return pl.pallas_call(
        body, name=name, grid=(nt,),
        in_specs=[col(0), col(1), col(0), nxt, pl.BlockSpec((CONV_WIDTH, d), lambda i: (0, 0))],
        out_specs=[pl.BlockSpec((tm, 2 * d), lambda i: (i, 0)), pl.BlockSpec((1, 2 * d), lambda i: (0, 0))],
        out_shape=[jax.ShapeDtypeStruct((t, 2 * d), BF16), jax.ShapeDtypeStruct((1, 2 * d), F32)],
        scratch_shapes=[pltpu.VMEM((tm + CONV_HALO, d), F32), pltpu.VMEM((8, 2 * d), F32)],
        compiler_params=_params("arbitrary"),
    )(p, p, ddc, ddc, dw_w)


def _colsum_call(a, name):
    t, n = a.shape
    tm = _tile(t)

    def body(a_ref, o_ref):
        @pl.when(pl.program_id(0) == 0)
        def _():
            o_ref[...] = jnp.zeros_like(o_ref)

        o_ref[...] += _colsum(a_ref[...].astype(F32))

    return pl.pallas_call(
        body, name=name, grid=(t // tm,), in_specs=[pl.BlockSpec((tm, n), lambda i: (i, 0))],
        out_specs=pl.BlockSpec((1, n), lambda i: (0, 0)), out_shape=jax.ShapeDtypeStruct((1, n), F32),
        compiler_params=_params("arbitrary"),
    )(a)


def _ada_fwd(c_all, w, name):
    rows, d = c_all.shape
    n = w.shape[1]
    tn = _tile(n, (256, 128))

    def body(c_ref, w_ref, o_ref):
        c = c_ref[...]
        o_ref[...] = _dot((c * _sigmoid(c)).astype(BF16), w_ref[...].astype(BF16), _NN)

    return pl.pallas_call(
        body, name=name, grid=(n // tn,),
        in_specs=[pl.BlockSpec((rows, d), lambda j: (0, 0)), pl.BlockSpec((d, tn), lambda j: (0, j))],
        out_specs=pl.BlockSpec((rows, tn), lambda j: (0, j)), out_shape=jax.ShapeDtypeStruct((rows, n), F32),
        compiler_params=_params("parallel"),
    )(c_all, w)


def _ada_bwd(c_all, dmod, name):
    rows, d = c_all.shape
    n = dmod.shape[1]
    tn = _tile(n, (256, 128))

    def body(c_ref, g_ref, o_ref):
        c = c_ref[...]
        o_ref[...] = _dot((c * _sigmoid(c)).astype(BF16), g_ref[...].astype(BF16), _TN)

    return pl.pallas_call(
        body, name=name, grid=(n // tn,),
        in_specs=[pl.BlockSpec((rows, d), lambda j: (0, 0)), pl.BlockSpec((rows, tn), lambda j: (0, j))],
        out_specs=pl.BlockSpec((d, tn), lambda j: (0, j)), out_shape=jax.ShapeDtypeStruct((d, n), F32),
        compiler_params=_params("parallel"),
    )(c_all, dmod)


def _sum_slots(a, name):
    s, r, c = a.shape
    tr = _row_tile(r, 256)

    def body(a_ref, o_ref):
        acc = a_ref[0].astype(F32)
        for k in range(1, s):
            acc = acc + a_ref[k].astype(F32)
        o_ref[...] = acc

    return pl.pallas_call(
        body, name=name, grid=(r // tr,), in_specs=[pl.BlockSpec((s, tr, c), lambda i: (0, i, 0))],
        out_specs=pl.BlockSpec((tr, c), lambda i: (i, 0)), out_shape=jax.ShapeDtypeStruct((r, c), F32),
        compiler_params=_params("parallel"),
    )(a)


def _sum_with_own(blocks, land, me, name):
    s, r, c = land.shape
    tr = _row_tile(r, 256)
    slot = lambda k: pl.BlockSpec((None, tr, c), lambda i, me_ref: ((me_ref[0] + k) % s, i, 0))

    def body(me_ref, own_ref, *refs):
        o_ref = refs[-1]
        acc = own_ref[...].astype(F32)
        for ref in refs[:-1]:
            acc = acc + ref[...].astype(F32)
        o_ref[...] = acc

    return pl.pallas_call(
        body, name=name, out_shape=jax.ShapeDtypeStruct((r, c), F32),
        grid_spec=pltpu.PrefetchScalarGridSpec(
            num_scalar_prefetch=1, grid=(r // tr,), in_specs=[slot(0)] + [slot(k) for k in range(1, s)],
            out_specs=pl.BlockSpec((tr, c), lambda i, me_ref: (i, 0))),
        compiler_params=_params("parallel"),
    )(me, blocks, *[land] * (s - 1))


def _adamw_update(w, g, m, v):
    nm = ADAM_B1 * m + (1.0 - ADAM_B1) * g
    nv = ADAM_B2 * v + (1.0 - ADAM_B2) * (g * g)
    m_hat = nm * (1.0 / (1.0 - ADAM_B1 ** ADAM_STEP))
    v_hat = nv * (1.0 / (1.0 - ADAM_B2 ** ADAM_STEP))
    return -ADAM_LR * (m_hat / (jnp.sqrt(v_hat) + ADAM_EPS) + ADAM_WD * w), nm, nv


def _adamw(w, g, m, v, name):
    l, r, c = w.shape
    tr = _row_tile(r, 256)
    blk = pl.BlockSpec((None, tr, c), lambda k, i: (k, i, 0))

    def body(w_ref, g_ref, m_ref, v_ref, d_ref, nm_ref, nv_ref):
        d_ref[...], nm_ref[...], nv_ref[...] = _adamw_update(w_ref[...], g_ref[...], m_ref[...], v_ref[...])

    return pl.pallas_call(
        body, name=name, grid=(l, r // tr), in_specs=[blk] * 4, out_specs=[blk] * 3,
        out_shape=[jax.ShapeDtypeStruct(w.shape, F32)] * 3, compiler_params=_params("parallel", "parallel"),
    )(w, g, m, v)


def _adamw_small(ws, gs, ms, vs, name):
    n = len(ws)
    two_d = lambda a: a.reshape(-1, a.shape[-1])

    def body(*refs):
        ins, outs = refs[:4 * n], refs[4 * n:]
        for a in range(n):
            outs[a][...], outs[n + a][...], outs[2 * n + a][...] = _adamw_update(*[ins[k * n + a][...] for k in range(4)])

    res = pl.pallas_call(
        body, name=name, out_shape=[jax.ShapeDtypeStruct(two_d(w).shape, F32) for w in ws] * 3,
    )(*[two_d(a) for a in (*ws, *gs, *ms, *vs)])
    return [[res[k * n + a].reshape(ws[a].shape) for a in range(n)] for k in range(3)]


def _mesh_pos():
    return lax.axis_index("x"), lax.axis_index("y"), lax.axis_index("c")


def _all_gather_vmem(x_shard, name):
    m_per, n = x_shard.shape

    def body(x_ref, out_ref, send_sems, recv_sems, local_sem):
        x, y, c = _mesh_pos()
        me, sibling = (x, y, c), (x, y, 1 - c)
        chips = [(1 - x, y), (x, 1 - y), (1 - x, 1 - y)]

        def rows(px, py, pc):
            return out_ref.at[pl.ds((4 * px + 2 * py + pc) * m_per, m_per), :]

        def copy(k, block, to, src=None):
            return pltpu.make_async_remote_copy(
                src_ref=rows(*block) if src is None else src, dst_ref=rows(*block),
                send_sem=send_sems.at[k], recv_sem=recv_sems.at[k], device_id=to, device_id_type=MESH)

        mine = pltpu.make_async_copy(x_ref, rows(*me), local_sem)
        mine.start()
        first = [copy(0, me, sibling, src=x_ref)]
        first += [copy(1 + j, me, (*chip, c), src=x_ref) for j, chip in enumerate(chips)]
        for cp in first:
            cp.start()
        passed = [copy(4 + j, (*chip, c), sibling) for j, chip in enumerate(chips)]
        for j, chip in enumerate(chips):
            copy(1 + j, (*chip, c), me).wait_recv()
            passed[j].start()
        copy(0, sibling, me).wait_recv()
        for j, chip in enumerate(chips):
            copy(4 + j, (*chip, 1 - c), me).wait_recv()
        for cp in first + passed:
            cp.wait_send()
        mine.wait()

    return pl.pallas_call(
        body, name=name, out_shape=jax.ShapeDtypeStruct((N_DEV * m_per, n), x_shard.dtype),
        in_specs=[pl.BlockSpec(memory_space=pltpu.VMEM)], out_specs=pl.BlockSpec(memory_space=pltpu.VMEM),
        scratch_shapes=[pltpu.SemaphoreType.DMA((7,)), pltpu.SemaphoreType.DMA((7,)), pltpu.SemaphoreType.DMA],
    )(x_shard)


def _all_gather_hbm(shards, name):
    n = len(shards)
    out_shape = [jax.ShapeDtypeStruct((N_DEV,) + s.shape, s.dtype) for s in shards]

    def body(*refs):
        x_refs, out_refs = refs[:n], refs[n:2 * n]
        send_sems, recv_sems, local_sems = refs[2 * n:]
        x, y, c = _mesh_pos()
        me, sibling = (x, y, c), (x, y, 1 - c)
        chips = [(1 - x, y), (x, 1 - y), (1 - x, 1 - y)]

        def blk(a, p):
            return out_refs[a].at[4 * p[0] + 2 * p[1] + p[2]]

        def copy(a, k, block, to, src=None):
            return pltpu.make_async_remote_copy(
                src_ref=blk(a, block) if src is None else src, dst_ref=blk(a, block),
                send_sem=send_sems.at[7 * a + k], recv_sem=recv_sems.at[7 * a + k], device_id=to, device_id_type=MESH)

        mine = [pltpu.make_async_copy(x_refs[a], blk(a, me), local_sems.at[a]) for a in range(n)]
        for cp in mine:
            cp.start()
        first = []
        for a in range(n):
            first.append(copy(a, 0, me, sibling, src=x_refs[a]))
            first += [copy(a, 1 + j, me, (*chip, c), src=x_refs[a]) for j, chip in enumerate(chips)]
        for cp in first:
            cp.start()
        passed = []
        for j, chip in enumerate(chips):
            for a in range(n):
                copy(a, 1 + j, (*chip, c), me).wait_recv()
                fwd = copy(a, 4 + j, (*chip, c), sibling)
                fwd.start()
                passed.append(fwd)
        for a in range(n):
            copy(a, 0, sibling, me).wait_recv()
            for j, chip in enumerate(chips):
                copy(a, 4 + j, (*chip, 1 - c), me).wait_recv()
        for cp in first + passed:
            cp.wait_send()
        for cp in mine:
            cp.wait()

    return pl.pallas_call(
        body, name=name, out_shape=out_shape, in_specs=[pl.BlockSpec(memory_space=pltpu.VMEM)] * n,
        out_specs=[pl.BlockSpec(memory_space=pl.ANY)] * n,
        scratch_shapes=[pltpu.SemaphoreType.DMA((7 * n,)), pltpu.SemaphoreType.DMA((7 * n,)), pltpu.SemaphoreType.DMA((n,))],
    )(*shards)


def _peers(x, y, c):
    flip = lambda v, f: 1 - v if f else v
    return [(flip(x, m & 4), flip(y, m & 2), flip(c, m & 1)) for m in range(1, N_DEV)]


def _dev_index(p):
    return 4 * p[0] + 2 * p[1] + p[2]


def _push_copies(src_refs, land_refs, send_sems, recv_sems, scatter, receive):
    x, y, c = _mesh_pos()
    me = _dev_index((x, y, c))
    copies = []
    for a, (src, land) in enumerate(zip(src_refs, land_refs)):
        for k, p in enumerate(_peers(x, y, c)):
            copies.append(pltpu.make_async_remote_copy(
                src_ref=src.at[_dev_index(p)] if scatter else src, dst_ref=land.at[_dev_index(p) if receive else me],
                send_sem=send_sems.at[7 * a + k], recv_sem=recv_sems.at[7 * a + k], device_id=p, device_id_type=MESH))
    return copies


_HBM = pl.BlockSpec(memory_space=pltpu.HBM)
_SEM = pl.BlockSpec(memory_space=pltpu.SEMAPHORE)
_EFFECT = pltpu.SideEffectType.DATAFLOW_SIDE_EFFECTING


def _pushes_start(srcs, lands, scatter, name):
    n = len(srcs)

    def body(*refs):
        src_refs, land_refs = refs[:n], refs[n:2 * n]
        send_sems, recv_sems = refs[2 * n], refs[2 * n + 1]
        token = refs[-1]
        for cp in _push_copies(src_refs, land_refs, send_sems, recv_sems, scatter, receive=False):
            cp.start()
        token[...] = jnp.zeros_like(token)

    hbm = lambda a: pltpu.HBM(a.shape, a.dtype)
    sems = pltpu.SemaphoreType.DMA((7 * n,))
    outs = pl.pallas_call(
        body, name=name,
        out_shape=(sems, sems, *[hbm(a) for a in srcs], *[hbm(a) for a in lands], jax.ShapeDtypeStruct((8, 128), F32)),
        in_specs=[_HBM] * (2 * n), out_specs=(_SEM, _SEM, *[_HBM] * (2 * n), pl.BlockSpec(memory_space=pltpu.VMEM)),
        input_output_aliases={i: 2 + i for i in range(2 * n)},
        compiler_params=pltpu.CompilerParams(has_side_effects=_EFFECT),
    )(*[pltpu.with_memory_space_constraint(a, pltpu.HBM) for a in (*srcs, *lands)])
    return (outs[0], outs[1], outs[2:2 + n], outs[2 + n:2 + 2 * n], scatter), outs[-1]


def _pushes_wait(handle, after, name):
    send_sems, recv_sems, srcs, lands, scatter = handle
    n = len(srcs)

    def body(*refs):
        src_refs, land_refs = refs[:n], refs[n:2 * n]
        for cp in _push_copies(src_refs, land_refs, refs[2 * n], refs[2 * n + 1], scatter, receive=True):
            cp.wait_send()
            cp.wait_recv()

    hbm = lambda a: pltpu.HBM(a.shape, a.dtype)
    outs = pl.pallas_call(
        body, name=name, out_shape=tuple(hbm(a) for a in (*srcs, *lands)),
        in_specs=[_HBM] * (2 * n) + [_SEM, _SEM, pl.BlockSpec(memory_space=pl.ANY)], out_specs=tuple([_HBM] * (2 * n)),
        input_output_aliases={i: i for i in range(2 * n)},
        compiler_params=pltpu.CompilerParams(has_side_effects=_EFFECT),
    )(*srcs, *lands, send_sems, recv_sems, after)
    return outs[:n], outs[n:]


def _landing_zones(srcs, name):
    n = len(srcs)

    def body(*refs):
        src_refs, land_refs, bufs, sems = refs[:n], refs[n:2 * n], refs[2 * n:3 * n], refs[3 * n]
        me = _dev_index(_mesh_pos())
        load = [pltpu.make_async_copy(src, buf, sems.at[a]) for a, (src, buf) in enumerate(zip(src_refs, bufs))]
        store = [pltpu.make_async_copy(buf, land.at[me], sems.at[a]) for a, (buf, land) in enumerate(zip(bufs, land_refs))]
        for cp in load:
            cp.start()
        for ld, st in zip(load, store):
            ld.wait()
            st.start()
        for cp in store:
            cp.wait()

    any_spec = pl.BlockSpec(memory_space=pl.ANY)
    return pl.pallas_call(
        body, name=name, out_shape=[jax.ShapeDtypeStruct((N_DEV,) + s.shape, s.dtype) for s in srcs],
        in_specs=[any_spec] * n, out_specs=[any_spec] * n,
        scratch_shapes=[pltpu.VMEM(s.shape, s.dtype) for s in srcs] + [pltpu.SemaphoreType.DMA((n,))],
        compiler_params=pltpu.CompilerParams(vmem_limit_bytes=V7X_VMEM_LIMIT),
    )(*srcs)


def _ffn_forward(x, mod, norm_g, w, tag):
    sh, sc, gate = mod
    h = _modnorm(x, norm_g, sc, sh, f"{tag}_norm")
    u = _ffn_up(h, w["up_t"], f"{tag}_up")
    act, z = _ffn_act(u, w["dw_w"], w["dw_b"], f"{tag}_act")
    y, x_new = _matmul(act, w["down"], "nn", F32, f"{tag}_down", resid=(x, gate))
    return x_new, (x, h, u, z, act, y)


def _behind(row, token):
    return row if token is None else row + token[0:1, 0:1]


def _ffn_backward(dx_new, saved, mod, norm_g, w, tag, emit):
    x, h, u, z, act, y = saved
    _, sc, gate = mod
    dy, d_gate = _gate_bwd(dx_new, y, gate, f"{tag}_gate_bwd")
    d_down = _matmul_tn_acc(act, dy, f"{tag}_down_dw")
    dact = _matmul(dy, w["down"], "nt", BF16, f"{tag}_down_dx")
    du, d_dw_w, d_dw_b = _ffn_act_bwd(u, z, dact, w["dw_w"], f"{tag}_act_bwd")
    d_up_t = _matmul_tn_acc(du, h, f"{tag}_up_dw").reshape(2 * FFN_DIM, -1)
    token = emit([d_up_t, d_down])
    dh = _ffn_up_dx(du, w["up_t"], f"{tag}_up_dx")
    dx, d_w, d_sh = _modnorm_bwd(x, dh, norm_g, _behind(sc, token), dx_new, f"{tag}_norm_bwd")
    return dx, dict(dw_w=d_dw_w.transpose(1, 0, 2).reshape(FFN_CONV_WIDTH, 2 * FFN_DIM),
                    dw_b=d_dw_b.reshape(1, 2 * FFN_DIM), norm_g=d_w * (1.0 + sc), sh=d_sh, sc=d_w * norm_g, gate=d_gate)


def _mixer_forward(x, mod, norm_g, w, rope, tag):
    sh, sc, gate = mod
    h = _modnorm(x, norm_g, sc, sh, f"{tag}_norm")
    z = _matmul(h, w["w_in_t"], "nt", BF16, f"{tag}_in")
    ya = _gmlp_fwd(z, w["gain"], w["wtril"], w["bias_exp"], f"{tag}_gmlp")
    q, k, v = _qk_prep(z, rope[0], rope[1], w["gq"], w["gk"], w["seg"], f"{tag}_qk")
    outs, lses = zip(*[_attn_fwd(q[b], k[b], v[b], dil, f"{tag}_attn_d{dil}") for b, dil in enumerate(DILATIONS)])
    yb, lse, cat = _attn_merge(outs, lses, ya, f"{tag}_merge")
    y, x_new = _matmul(cat, w["w_out"], "nn", F32, f"{tag}_out", resid=(x, gate))
    return x_new, (x, h, z, q, k, v, yb, lse, cat, y)


def _mixer_backward(dx_new, saved, mod, norm_g, w, rope, tag, emit):
    x, h, z, q, k, v, yb, lse, cat, y = saved
    _, sc, gate = mod
    dy, d_gate = _gate_bwd(dx_new, y, gate, f"{tag}_gate_bwd")
    d_w_out = _matmul_tn_acc(cat, dy, f"{tag}_out_dw")
    dcat = _matmul(dy, w["w_out"], "nt", BF16, f"{tag}_out_dx")
    dz_a, d_sp_w, d_gain, d_bias_exp = _gmlp_bwd(z, dcat, w["gain"], w["wtril"], w["wtril_t"], w["bias_exp"], f"{tag}_gmlp_bwd")
    dyb = _subseq_views(dcat, A_WIDTH // B_WIDTH, f"{tag}_dyb_views")
    dqs, dks, dvs = zip(*[_attn_bwd(q[b], k[b], v[b], dyb[b], yb[b], lse[b], dil, f"{tag}_attn_bwd_d{dil}")
                          for b, dil in enumerate(DILATIONS)])
    dz_qkv, d_gq, d_gk = _qk_prep_bwd(z, dqs, dks, dvs, rope[0], rope[1], w["gq"], w["gk"], w["seg"], f"{tag}_qk_bwd")
    dz = jnp.concatenate([dz_a, dz_qkv], axis=1)
    d_w_in_t = _matmul_tn_acc(dz, h, f"{tag}_in_dw")
    token = emit([d_w_in_t, d_w_out])
    dh = _matmul(dz, w["w_in_t"], "nn", F32, f"{tag}_in_dx")
    dx, d_w, d_sh = _modnorm_bwd(x, dh, norm_g, _behind(sc, token), dx_new, f"{tag}_norm_bwd")
    return dx, dict(
        vnorm_g=d_gain.reshape(A_GROUPS, GROUP_DIM), spatial_w=d_sp_w,
        spatial_b=d_bias_exp.reshape(CHUNK, A_GROUPS, GROUP_DIM).sum(-1).T,
        q_norm_g=d_gq.reshape(HEADS, HEAD_DIM).sum(0), k_norm_g=d_gk.reshape(HEADS, HEAD_DIM).sum(0),
        norm_g=d_w * (1.0 + sc), sh=d_sh, sc=d_w * norm_g, gate=d_gate)


def _conformer_forward(x, mod, norm_g, w, tag):
    sh, sc, gate = mod
    h = _modnorm(x, norm_g, sc, sh, f"{tag}_norm")
    p = _matmul(h, w["pw1_t"], "nt", BF16, f"{tag}_pw1", bias=w["pw1_b"])
    s, dc = _conformer_mid(p, w["dw_w"], w["dw_b"], w["ln_g"], w["ln_b"], f"{tag}_mid")
    y, x_new = _matmul(s, w["pw2"], "nn", F32, f"{tag}_pw2", bias=w["pw2_b"], resid=(x, gate))
    return x_new, (x, h, p, dc, s, y)


def _conformer_backward(dx_new, saved, mod, norm_g, w, tag, emit):
    x, h, p, dc, s, y = saved
    _, sc, gate = mod
    dy, d_gate = _gate_bwd(dx_new, y, gate, f"{tag}_gate_bwd")
    d_pw2 = _matmul_tn_acc(s, dy, f"{tag}_pw2_dw")
    d_pw2_b = _colsum_call(dy, f"{tag}_pw2_db")
    ds = _matmul(dy, w["pw2"], "nt", BF16, f"{tag}_pw2_dx")
    ddc, d_dw_w, d_dw_b, d_ln_g, d_ln_b = _conformer_mid_bwd(p, dc, ds, w["ln_g"], w["ln_b"], f"{tag}_mid_bwd")
    dp, d_pw1_b = _conformer_glu_bwd(p, ddc, w["dw_w"], f"{tag}_glu_bwd")
    d_pw1_t = _matmul_tn_acc(dp, h, f"{tag}_pw1_dw")
    token = emit([d_pw1_t, d_pw2])
    dh = _matmul(dp, w["pw1_t"], "nn", F32, f"{tag}_pw1_dx")
    dx, d_w, d_sh = _modnorm_bwd(x, dh, norm_g, _behind(sc, token), dx_new, f"{tag}_norm_bwd")
    return dx, dict(pw1_b=d_pw1_b, dw_w=d_dw_w, dw_b=d_dw_b, ln_g=d_ln_g, ln_b=d_ln_b, pw2_b=d_pw2_b, norm_g=d_w * (1.0 + sc), sh=d_sh, sc=d_w * norm_g, gate=d_gate)


def _local_step(x, target, pos, mod, norm_mix_g, norm_ffn_g, mixer_w, conv_w, ffn_w, fetch, emit):
    d = D_MODEL
    inv_freq = 1.0 / (ROPE_THETA ** (jnp.arange(0, HEAD_DIM, 2, dtype=F32) / HEAD_DIM))
    inv_freq = jnp.tile(inv_freq, 2 * HEADS)[None, :]
    sign = jnp.tile(jnp.concatenate([-jnp.ones(HEAD_DIM // 2, F32), jnp.ones(HEAD_DIM // 2, F32)]), HEADS)[None, :]
    rope = _rope_tables(pos, inv_freq, sign, "rope_tables")
    mods = [[mod[l:l + 1, i * d:(i + 1) * d] for i in range(6)] for l in range(2)]
    mix = [(m[0], m[1], m[2]) for m in mods]
    ffn = [(m[3], m[4], m[5]) for m in mods]
    gm = [norm_mix_g[l:l + 1] for l in range(2)]
    gf = [norm_ffn_g[l:l + 1] for l in range(2)]

    mixer_w = {**mixer_w, **fetch("l0_mix", x)}
    x1, s_mix = _mixer_forward(x, mix[0], gm[0], mixer_w, rope, "l0_mix")
    ffn_w0 = {**ffn_w[0], **fetch("l0_ffn", x1)}
    x2, s_ffn0 = _ffn_forward(x1, ffn[0], gf[0], ffn_w0, "l0_ffn")
    conv_w = {**conv_w, **fetch("l1_conv", x2)}
    x3, s_conv = _conformer_forward(x2, mix[1], gm[1], conv_w, "l1_conv")
    ffn_w1 = {**ffn_w[1], **fetch("l1_ffn", x3)}
    x4, s_ffn1 = _ffn_forward(x3, ffn[1], gf[1], ffn_w1, "l1_ffn")
    dx, loss = _loss_head(x4, target, "loss_head")
    dx, g_ffn1 = _ffn_backward(dx, s_ffn1, ffn[1], gf[1], ffn_w1, "l1_ffn", functools.partial(emit, "l1_ffn"))
    dx, g_conv = _conformer_backward(dx, s_conv, mix[1], gm[1], conv_w, "l1_conv", functools.partial(emit, "l1_conv"))
    dx, g_ffn0 = _ffn_backward(dx, s_ffn0, ffn[0], gf[0], ffn_w0, "l0_ffn", functools.partial(emit, "l0_ffn"))
    dx, g_mix = _mixer_backward(dx, s_mix, mix[0], gm[0], mixer_w, rope, "l0_mix", functools.partial(emit, "l0_mix"))
    blocks = [g_mix, g_ffn0, g_conv, g_ffn1]
    dmod = jnp.stack([jnp.concatenate([a["sh"], a["sc"], a["gate"], b["sh"], b["sc"], b["gate"]], axis=1)[0]
                      for a, b in ((g_mix, g_ffn0), (g_conv, g_ffn1))])
    return loss, dx, dmod, blocks


def _pack(arrs, rows=8):
    flat = jnp.concatenate([a.reshape(-1).astype(F32) for a in arrs])
    n = flat.shape[0]
    cols = -(-n // (rows * 128)) * 128
    return jnp.pad(flat, (0, rows * cols - n)).reshape(rows, cols)


def _unpack(flat, shapes):
    out, off = [], 0
    for shp in shapes:
        n = math.prod(shp)
        out.append(flat[..., off:off + n].reshape(flat.shape[:-1] + tuple(shp)))
        off += n
    return out


def _take_block(a, idx, size, axis):
    return lax.dynamic_slice_in_dim(a, idx * size, size, axis)


def kernel(x, c, positions, ada_w, ada_b, norm_mix_g, norm_ffn_g, ab_w_in, a_vnorm_g, a_spatial_w, a_spatial_b, b_q_norm_g, b_k_norm_g, ab_w_out, conv_pw1_w, conv_pw1_b, conv_dw_w, conv_dw_b, conv_ln_g, conv_ln_b, conv_pw2_w, conv_pw2_b, ffn_up_w, ffn_dw_w, ffn_dw_b, ffn_down_w, loss_target, m_ada_w, m_ada_b, m_norm_mix_g, m_norm_ffn_g, m_ab_w_in, m_a_vnorm_g, m_a_spatial_w, m_a_spatial_b, m_b_q_norm_g, m_b_k_norm_g, m_ab_w_out, m_conv_pw1_w, m_conv_pw1_b, m_conv_dw_w, m_conv_dw_b, m_conv_ln_g, m_conv_ln_b, m_conv_pw2_w, m_conv_pw2_b, m_ffn_up_w, m_ffn_dw_w, m_ffn_dw_b, m_ffn_down_w, v_ada_w, v_ada_b, v_norm_mix_g, v_norm_ffn_g, v_ab_w_in, v_a_vnorm_g, v_a_spatial_w, v_a_spatial_b, v_b_q_norm_g, v_b_k_norm_g, v_ab_w_out, v_conv_pw1_w, v_conv_pw1_b, v_conv_dw_w, v_conv_dw_b, v_conv_ln_g, v_conv_ln_b, v_conv_pw2_w, v_conv_pw2_b, v_ffn_up_w, v_ffn_dw_w, v_ffn_dw_b, v_ffn_down_w):
    weights = dict(ada_w=ada_w, ada_b=ada_b, norm_mix_g=norm_mix_g, norm_ffn_g=norm_ffn_g, ab_w_in=ab_w_in, a_vnorm_g=a_vnorm_g, a_spatial_w=a_spatial_w, a_spatial_b=a_spatial_b, b_q_norm_g=b_q_norm_g, b_k_norm_g=b_k_norm_g, ab_w_out=ab_w_out, conv_pw1_w=conv_pw1_w, conv_pw1_b=conv_pw1_b, conv_dw_w=conv_dw_w, conv_dw_b=conv_dw_b, conv_ln_g=conv_ln_g, conv_ln_b=conv_ln_b, conv_pw2_w=conv_pw2_w, conv_pw2_b=conv_pw2_b, ffn_up_w=ffn_up_w, ffn_dw_w=ffn_dw_w, ffn_dw_b=ffn_dw_b, ffn_down_w=ffn_down_w)
    mom1 = dict(ada_w=m_ada_w, ada_b=m_ada_b, norm_mix_g=m_norm_mix_g, norm_ffn_g=m_norm_ffn_g, ab_w_in=m_ab_w_in, a_vnorm_g=m_a_vnorm_g, a_spatial_w=m_a_spatial_w, a_spatial_b=m_a_spatial_b, b_q_norm_g=m_b_q_norm_g, b_k_norm_g=m_b_k_norm_g, ab_w_out=m_ab_w_out, conv_pw1_w=m_conv_pw1_w, conv_pw1_b=m_conv_pw1_b, conv_dw_w=m_conv_dw_w, conv_dw_b=m_conv_dw_b, conv_ln_g=m_conv_ln_g, conv_ln_b=m_conv_ln_b, conv_pw2_w=m_conv_pw2_w, conv_pw2_b=m_conv_pw2_b, ffn_up_w=m_ffn_up_w, ffn_dw_w=m_ffn_dw_w, ffn_dw_b=m_ffn_dw_b, ffn_down_w=m_ffn_down_w)
    mom2 = dict(ada_w=v_ada_w, ada_b=v_ada_b, norm_mix_g=v_norm_mix_g, norm_ffn_g=v_norm_ffn_g, ab_w_in=v_ab_w_in, a_vnorm_g=v_a_vnorm_g, a_spatial_w=v_a_spatial_w, a_spatial_b=v_a_spatial_b, b_q_norm_g=v_b_q_norm_g, b_k_norm_g=v_b_k_norm_g, ab_w_out=v_ab_w_out, conv_pw1_w=v_conv_pw1_w, conv_pw1_b=v_conv_pw1_b, conv_dw_w=v_conv_dw_w, conv_dw_b=v_conv_dw_b, conv_ln_g=v_conv_ln_g, conv_ln_b=v_conv_ln_b, conv_pw2_w=v_conv_pw2_w, conv_pw2_b=v_conv_pw2_b, ffn_up_w=v_ffn_up_w, ffn_dw_w=v_ffn_dw_w, ffn_dw_b=v_ffn_dw_b, ffn_down_w=v_ffn_down_w)
    order = list(weights)
    d, f2 = D_MODEL, 2 * FFN_DIM
    t = x.shape[1]
    me = 4 * lax.axis_index("x") + 2 * lax.axis_index("y") + lax.axis_index("c")
    for window, dil in PATTERNS:
        assert window // dil == Q_BLOCK and t % (dil * Q_BLOCK) == 0

    small_in = [c[0], conv_pw1_b[0], conv_dw_w[0], conv_dw_b[0], conv_ln_g[0], conv_ln_b[0], conv_pw2_b[0], ffn_dw_w]
    g1 = _all_gather_vmem(_pack(small_in, rows=8), "gather_small").reshape(N_DEV, -1)
    c_all, pw1_b, dw_w, dw_b, ln_g, ln_b, pw2_b, fdw_w = _unpack(g1, [a.shape for a in small_in])
    pw1_b, dw_b, ln_g, ln_b, pw2_b = [a.reshape(1, -1) for a in (pw1_b, dw_b, ln_g, ln_b, pw2_b)]
    dw_w = dw_w.transpose(1, 0, 2).reshape(CONV_WIDTH, d)
    fdw_w = fdw_w.transpose(1, 2, 0, 3).reshape(2, FFN_CONV_WIDTH, f2)

    c16 = jnp.pad(c_all, ((0, 2 * N_DEV - c_all.shape[0]), (0, 0)))
    part = jnp.concatenate([_ada_fwd(c16, ada_w[l], f"ada_fwd{l}")[:N_DEV] for l in range(2)], axis=1)
    g2 = _all_gather_vmem(part, "gather_mod").reshape(N_DEV, N_DEV, 2, -1)
    mod = lax.dynamic_index_in_dim(g2, me, axis=1, keepdims=False).transpose(1, 0, 2).reshape(2, 6 * d) + ada_b

    stages = dict(l0_mix=[ab_w_in[0].T, ab_w_out[0]], l0_ffn=[ffn_up_w[0].T, ffn_down_w[0]],
                  l1_conv=[conv_pw1_w[0].T, conv_pw2_w[0]], l1_ffn=[ffn_up_w[1].T, ffn_down_w[1]])
    stages = {k: [s.astype(BF16) for s in v] for k, v in stages.items()}
    names = dict(l0_mix=("w_in_t", "w_out"), l0_ffn=("up_t", "down"), l1_conv=("pw1_t", "pw2"), l1_ffn=("up_t", "down"))
    ready = {"l0_mix": [a.reshape(-1, d) for a in _all_gather_hbm(stages["l0_mix"], "gather_mixer_weights")]}
    behind = (ready, mod)
    arriving = {}
    for stage, group in (("l0_ffn", ("l0_ffn",)), ("l1_conv", ("l1_conv", "l1_ffn"))):
        srcs, _ = lax.optimization_barrier(([s for g in group for s in stages[g]], behind))
        arriving[stage], behind = _pushes_start(
            srcs, _landing_zones(srcs, f"gather_{stage}_zones"), False, f"gather_{stage}_start")
        mod = mod + behind[0:1, 0:1]

    def fetch(stage, after):
        if stage in arriving:
            full = [a.reshape(-1, d) for a in _pushes_wait(arriving[stage], after, f"gather_{stage}_wait")[1]]
            ready[stage] = full[:2]
            if stage == "l1_conv":
                ready["l1_ffn"] = full[2:]
        return dict(zip(names[stage], ready[stage]))

    causal = jnp.tril(jnp.ones((CHUNK, CHUNK), bool))
    wtril = jnp.where(causal[None], a_spatial_w[0], 0.0)
    mixer_w = dict(
        gain=a_vnorm_g[0].reshape(1, A_WIDTH), wtril=wtril.astype(BF16),
        wtril_t=wtril.transpose(0, 2, 1).astype(BF16),
        bias_exp=jnp.repeat(a_spatial_b[0].T, GROUP_DIM, axis=1),
        gq=jnp.tile(b_q_norm_g[0], HEADS)[None, :], gk=jnp.tile(b_k_norm_g[0], HEADS)[None, :],
        seg=jnp.kron(jnp.eye(HEADS, dtype=BF16), jnp.ones((HEAD_DIM, HEAD_DIM), BF16)))
    conv_w = dict(pw1_b=pw1_b, dw_w=dw_w, dw_b=dw_b, ln_g=ln_g, ln_b=ln_b, pw2_b=pw2_b)
    ffn_w = [dict(dw_w=fdw_w[l].reshape(FFN_CONV_WIDTH, 2, FFN_DIM).transpose(1, 0, 2), dw_b=ffn_dw_b[l].reshape(2, 1, FFN_DIM))
             for l in range(2)]

    leaving = {}

    def emit(stage, grads):
        blocks = [g.reshape(N_DEV, g.shape[0] // N_DEV, d) for g in grads]
        leaving[stage], token = _pushes_start(
            blocks, [lax.empty(b.shape, b.dtype) for b in blocks], True, f"reduce_{stage}_start")
        return token

    loss, dx, dmod, (g_mix, g_ffn0, g_conv, g_ffn1) = _local_step(
        x[0], loss_target[0], positions[0].astype(F32)[:, None], mod, norm_mix_g, norm_ffn_g, mixer_w, conv_w, ffn_w,
        fetch, emit)

    me_op = me.astype(jnp.int32).reshape(1)

    def reduced(stage, after):
        blocks, lands = _pushes_wait(leaving[stage], after, f"reduce_{stage}_wait")
        return [_sum_with_own(b, a, me_op, f"reduce_{stage}_sum{i}") for i, (b, a) in enumerate(zip(blocks, lands))]

    (r_up_t1, r_down1), (r_pw1_t, r_pw2), (r_up_t0, r_down0) = [reduced(s, dx) for s in ("l1_ffn", "l1_conv", "l0_ffn")]

    small_g = [
        dmod, jnp.concatenate([g_mix["norm_g"], g_conv["norm_g"]]), jnp.concatenate([g_ffn0["norm_g"], g_ffn1["norm_g"]]),
        g_mix["vnorm_g"], g_mix["spatial_w"], g_mix["spatial_b"], g_mix["q_norm_g"], g_mix["k_norm_g"],
        g_conv["pw1_b"], g_conv["dw_w"], g_conv["dw_b"], g_conv["ln_g"], g_conv["ln_b"], g_conv["pw2_b"],
        jnp.stack([g_ffn0["dw_w"], g_ffn1["dw_w"]]), jnp.concatenate([g_ffn0["dw_b"], g_ffn1["dw_b"]])]
    packed = _pack(small_g, rows=8)
    g3 = _all_gather_vmem(packed, "gather_small_grads").reshape(N_DEV, 8, -1)
    total = _unpack(_sum_slots(g3, "sum_small_grads").reshape(-1), [a.shape for a in small_g])
    (s_dmod, s_mix_g, s_ffn_g, s_vnorm, s_sp_w, s_sp_b, s_gq, s_gk, s_pw1_b, s_dw_w, s_dw_b, s_ln_g, s_ln_b,
     s_pw2_b, s_fdw_w, s_fdw_b) = total
    dmod_all = g3.reshape(N_DEV, -1)[:, :2 * 6 * d].reshape(N_DEV, 2, 6 * d)
    n_ada = ada_w.shape[2]
    dmod16 = jnp.pad(_take_block(dmod_all, me, n_ada, 2), ((0, N_DEV), (0, 0), (0, 0)))
    g_ada_w = jnp.stack([_ada_bwd(c16, dmod16[:, l], f"ada_bwd{l}") for l in range(2)])

    grads = dict(
        ada_w=g_ada_w, ada_b=s_dmod, norm_mix_g=s_mix_g, norm_ffn_g=s_ffn_g,
        a_vnorm_g=s_vnorm[None], a_spatial_w=s_sp_w[None], a_spatial_b=s_sp_b[None], b_q_norm_g=s_gq[None],
        b_k_norm_g=s_gk[None],
        conv_pw1_b=_take_block(s_pw1_b, me, conv_pw1_b.shape[1], 1),
        conv_dw_w=_take_block(s_dw_w, me, conv_dw_w.shape[2], 1)[None],
        conv_dw_b=_take_block(s_dw_b, me, conv_dw_b.shape[1], 1), conv_ln_g=_take_block(s_ln_g, me, conv_ln_g.shape[1], 1),
        conv_ln_b=_take_block(s_ln_b, me, conv_ln_b.shape[1], 1), conv_pw2_w=r_pw2[None],
        conv_pw2_b=_take_block(s_pw2_b, me, conv_pw2_b.shape[1], 1),
        ffn_dw_w=_take_block(s_fdw_w, me, ffn_dw_w.shape[2], 2), ffn_dw_b=s_fdw_b, ffn_down_w=jnp.stack([r_down0, r_down1]))
    grads_t = dict(conv_pw1_w=r_pw1_t[None], ffn_up_w=jnp.stack([r_up_t0, r_up_t1]))

    large = ("ada_w", "conv_pw1_w", "conv_pw2_w", "ffn_up_w", "ffn_down_w", "ab_w_in", "ab_w_out")
    flip = lambda a: jnp.swapaxes(a, 1, 2)
    delta, new_m, new_v = {}, {}, {}
    for name in large:
        if name == "ab_w_in":
            r_in_t, r_out = reduced("l0_mix", new_v["ffn_down_w"])
            grads_t["ab_w_in"] = r_in_t[None]
            grads["ab_w_out"] = r_out[None]
        if name in grads_t:
            grads[name] = flip(grads_t[name])
            res = _adamw(flip(weights[name]), grads_t[name], flip(mom1[name]), flip(mom2[name]), f"adamw_{name}")
            delta[name], new_m[name], new_v[name] = [flip(r) for r in res]
        else:
            delta[name], new_m[name], new_v[name] = _adamw(weights[name], grads[name], mom1[name], mom2[name], f"adamw_{name}")
    small = [n for n in order if n not in large]
    res = _adamw_small(*[[src[n] for n in small] for src in (weights, grads, mom1, mom2)], "adamw_small")
    for dst, arrs in zip((delta, new_m, new_v), res):
        dst.update(zip(small, arrs))

    loss = lax.psum(loss[0, 0], ("x", "y", "c"))
    return (loss, dx[None], *[grads[n] for n in order], *[delta[n] for n in order],
            *[new_m[n] for n in order], *[new_v[n] for n in order])
```

```python
import functools
import math

import jax
import jax.numpy as jnp
from jax import lax
from jax.experimental import pallas as pl
from jax.experimental.pallas import tpu as pltpu

F32 = jnp.float32
BF16 = jnp.bfloat16
MESH = pl.DeviceIdType.MESH

D_MODEL = 1024
A_WIDTH = 512
A_GROUPS = 4
GROUP_DIM = 128
CHUNK = 128
B_WIDTH = 512
HEADS = 8
HEAD_DIM = 64
PATTERNS = ((128, 1), (512, 4), (2048, 16))
Q_BLOCK = 128
ROPE_THETA = 10000.0
AB_IN = 2560
CONV_WIDTH = 31
FFN_DIM = 2816
FFN_CONV_WIDTH = 3
EPS = 1e-6
NEG = -1e30
N_DEV = 8
ADAM_LR, ADAM_B1, ADAM_B2, ADAM_EPS, ADAM_WD, ADAM_STEP = 0.001, 0.9, 0.999, 1e-08, 0.01, 10

V7X_VMEM_LIMIT = 56 * 2**20
BF16_ROWS = 16
FFN_HALO = 16
CONV_HALO = 32

_NN = (((1,), (0,)), ((), ()))
_NT = (((1,), (1,)), ((), ()))
_TN = (((0,), (0,)), ((), ()))


def _tile(n, prefs=(512, 256, 128)):
    for t in prefs:
        if n % t == 0:
            return t
    return n


def _row_tile(n, cap=512):
    best = n
    for t in range(8, min(n, cap) + 1, 8):
        if n % t == 0:
            best = t
    return best if best <= cap else n


def _params(*sem):
    return pltpu.CompilerParams(dimension_semantics=sem, vmem_limit_bytes=V7X_VMEM_LIMIT)


def _dot(a, b, dims):
    return lax.dot_general(a, b, dims, preferred_element_type=F32)


def _sigmoid(x):
    return 1.0 / (1.0 + jnp.exp(-x))


def _gelu(x):
    return 0.5 * x * (1.0 + lax.erf(x * (2.0 ** -0.5)))


def _gelu_grad(x):
    return 0.5 * (1.0 + lax.erf(x * (2.0 ** -0.5))) + x * jnp.exp(-0.5 * x * x) * (1.0 / math.sqrt(2.0 * math.pi))


def _colsum(v):
    return jnp.sum(v, axis=0, keepdims=True)


MATMUL_VMEM_BUDGET = 40 * 2**20


def _matmul_tiles(m, n, k, out_bytes, with_resid):
    def options(dim):
        opts = [t for t in (1024, 512, 256, 128) if dim % t == 0]
        return opts + [dim] if dim <= 4096 and dim not in opts else opts

    best = None
    for tm in options(m):
        for tn in options(n):
            need = 4 * (tm * k + k * tn) + tm * tn * (4 + 2 * out_bytes) + (24 * tm * tn if with_resid else 0)
            if need <= MATMUL_VMEM_BUDGET and (best is None or tm * tn / (tm + tn) > best[0]):
                best = (tm * tn / (tm + tn), tm, tn)
    return best[1], best[2]


def _matmul_tn_acc(a, b, name, tk=1024):
    squeeze = a.ndim == 2
    a3 = a[None] if squeeze else a
    p_, t, m = a3.shape
    n = b.shape[1]
    nk = t // tk

    def body(a_ref, b_ref, o_ref, acc_ref):
        kt = pl.program_id(1)

        @pl.when(kt == 0)
        def _():
            acc_ref[...] = jnp.zeros_like(acc_ref)

        acc_ref[...] += _dot(a_ref[...], b_ref[...], _TN)

        @pl.when(kt == nk - 1)
        def _():
            o_ref[...] = acc_ref[...].astype(BF16)

    out = pl.pallas_call(
        body, name=name, grid=(p_, nk),
        in_specs=[pl.BlockSpec((None, tk, m), lambda p, kt: (p, kt, 0)), pl.BlockSpec((tk, n), lambda p, kt: (kt, 0))],
        out_specs=pl.BlockSpec((None, m, n), lambda p, kt: (p, 0, 0)), out_shape=jax.ShapeDtypeStruct((p_, m, n), BF16),
        scratch_shapes=[pltpu.VMEM((m, n), F32)], compiler_params=_params("parallel", "arbitrary"),
    )(a3, b)
    return out[0] if squeeze else out


def _matmul(a, b, mode, out_dtype, name, bias=None, resid=None):
    if mode == "nn":
        (m, k), (_, n) = a.shape, b.shape
    elif mode == "nt":
        (m, k), (n, _) = a.shape, b.shape
    else:
        (k, m), (_, n) = a.shape, b.shape
    tm, tn = _matmul_tiles(m, n, k, jnp.dtype(out_dtype).itemsize, resid is not None)
    dims = {"nn": _NN, "nt": _NT, "tn": _TN}[mode]
    a_spec = pl.BlockSpec((k, tm), lambda i, j: (0, i)) if mode == "tn" else pl.BlockSpec((tm, k), lambda i, j: (i, 0))
    b_spec = pl.BlockSpec((tn, k), lambda i, j: (j, 0)) if mode == "nt" else pl.BlockSpec((k, tn), lambda i, j: (0, j))
    in_specs, args = [a_spec, b_spec], [a, b]
    row_spec = pl.BlockSpec((1, tn), lambda i, j: (0, j))
    tile_spec = pl.BlockSpec((tm, tn), lambda i, j: (i, j))
    if bias is not None:
        in_specs.append(row_spec)
        args.append(bias)
    if resid is not None:
        in_specs += [tile_spec, row_spec]
        args += list(resid)
    out_shape = [jax.ShapeDtypeStruct((m, n), out_dtype)]
    out_specs = [tile_spec]
    if resid is not None:
        out_shape.append(jax.ShapeDtypeStruct((m, n), F32))
        out_specs.append(tile_spec)

    def body(*refs):
        a_ref, b_ref = refs[0], refs[1]
        pos = 2
        acc = _dot(a_ref[...], b_ref[...], dims)
        if bias is not None:
            acc = acc + refs[pos][...]
            pos += 1
        if resid is not None:
            x_ref, g_ref = refs[pos], refs[pos + 1]
            pos += 2
        refs[pos][...] = acc.astype(out_dtype)
        if resid is not None:
            refs[pos + 1][...] = x_ref[...] + g_ref[...] * acc

    outs = pl.pallas_call(
        body, name=name, grid=(m // tm, n // tn), in_specs=in_specs, out_specs=out_specs, out_shape=out_shape,
        compiler_params=_params("parallel", "parallel"),
    )(*args)
    return outs if resid is not None else outs[0]


def _modnorm(x, g, sc, sh, name):
    t, d = x.shape
    tm = _tile(t)
    row = pl.BlockSpec((1, d), lambda i: (0, 0))
    blk = pl.BlockSpec((tm, d), lambda i: (i, 0))

    def body(x_ref, g_ref, sc_ref, sh_ref, o_ref):
        x = x_ref[...]
        r = lax.rsqrt(jnp.mean(x * x, axis=-1, keepdims=True) + EPS)
        o_ref[...] = ((x * r) * g_ref[...] * (1.0 + sc_ref[...]) + sh_ref[...]).astype(BF16)

    return pl.pallas_call(
        body, name=name, grid=(t // tm,), in_specs=[blk, row, row, row], out_specs=blk,
        out_shape=jax.ShapeDtypeStruct((t, d), BF16), compiler_params=_params("parallel"),
    )(x, g, sc, sh)


def _gate_bwd_tile(dx, y_ref, gate_ref, dy_ref, dgate_ref, first):
    @pl.when(first)
    def _():
        dgate_ref[...] = jnp.zeros_like(dgate_ref)

    dy_ref[...] = (dx * gate_ref[...]).astype(BF16)
    dgate_ref[...] += _colsum(dx * y_ref[...].astype(F32))


def _modnorm_bwd(x, dh, g, sc, dres, below, name):
    t, d = x.shape
    tm = _tile(t)
    row = pl.BlockSpec((1, d), lambda i: (0, 0))
    blk = pl.BlockSpec((tm, d), lambda i: (i, 0))

    def body(x_ref, dh_ref, g_ref, sc_ref, dres_ref, *rest):
        dx_ref, dw_ref, dsh_ref = rest[-5:-2] if below else rest
        first = pl.program_id(0) == 0

        @pl.when(first)
        def _():
            dw_ref[...] = jnp.zeros_like(dw_ref)
            dsh_ref[...] = jnp.zeros_like(dsh_ref)

        x = x_ref[...]
        dh = dh_ref[...].astype(F32)
        r = lax.rsqrt(jnp.mean(x * x, axis=-1, keepdims=True) + EPS)
        xn = x * r
        dxn = dh * (g_ref[...] * (1.0 + sc_ref[...]))
        dx = dres_ref[...] + r * (dxn - xn * jnp.mean(dxn * xn, axis=-1, keepdims=True))
        dx_ref[...] = dx
        dw_ref[...] += _colsum(dh * xn)
        dsh_ref[...] += _colsum(dh)
        if below:
            _gate_bwd_tile(dx, rest[0], rest[1], rest[-2], rest[-1], first)

    row_out = jax.ShapeDtypeStruct((1, d), F32)
    return pl.pallas_call(
        body, name=name, grid=(t // tm,), in_specs=[blk, blk, row, row, blk] + ([blk, row] if below else []),
        out_specs=[blk, row, row] + ([blk, row] if below else []),
        out_shape=[jax.ShapeDtypeStruct((t, d), F32), row_out, row_out]
        + ([jax.ShapeDtypeStruct((t, d), BF16), row_out] if below else []),
        compiler_params=_params("arbitrary"),
    )(x, dh, g, sc, dres, *(below or ()))


def _loss_head(y, target, below, name):
    t, d = y.shape
    tm = _tile(t)
    blk = pl.BlockSpec((tm, d), lambda i: (i, 0))
    row = pl.BlockSpec((1, d), lambda i: (0, 0))
    one = pl.BlockSpec((1, 1), lambda i: (0, 0))
    steps = t // tm

    def body(y_ref, t_ref, yb_ref, gate_ref, dx_ref, loss_ref, dy_ref, dgate_ref, acc_ref):
        first = pl.program_id(0) == 0

        @pl.when(first)
        def _():
            acc_ref[...] = jnp.zeros_like(acc_ref)

        e = y_ref[...] - t_ref[...]
        dx = e * (1.0 / d)
        dx_ref[...] = dx
        acc_ref[...] += _colsum(e * e)
        _gate_bwd_tile(dx, yb_ref, gate_ref, dy_ref, dgate_ref, first)

        @pl.when(pl.program_id(0) == steps - 1)
        def _():
            loss_ref[...] = jnp.sum(acc_ref[...], axis=1, keepdims=True) * (0.5 / d)

    return pl.pallas_call(
        body, name=name, grid=(steps,), in_specs=[blk, blk, blk, row], out_specs=[blk, one, blk, row],
        out_shape=[jax.ShapeDtypeStruct((t, d), F32), jax.ShapeDtypeStruct((1, 1), F32),
                   jax.ShapeDtypeStruct((t, d), BF16), jax.ShapeDtypeStruct((1, d), F32)],
        scratch_shapes=[pltpu.VMEM((1, d), F32)], compiler_params=_params("arbitrary"),
    )(y, target, *below)


def _group_norm(vg, gain):
    mu = jnp.mean(vg, axis=-1, keepdims=True)
    xc = vg - mu
    rstd = lax.rsqrt(jnp.mean(xc * xc, axis=-1, keepdims=True) + EPS)
    xhat = xc * rstd
    return xhat, rstd, xhat * gain


def _gmlp_fwd(z, gain, wtril, bias_exp, name):
    t = z.shape[0]
    zu = pl.BlockSpec((CHUNK, A_WIDTH), lambda i: (i, 0))
    zv = pl.BlockSpec((CHUNK, A_WIDTH), lambda i: (i, 1))
    full2 = lambda shp: pl.BlockSpec(shp, lambda i: (0, 0))
    w_spec = pl.BlockSpec((A_GROUPS, CHUNK, CHUNK), lambda i: (0, 0, 0))

    def body(zu_ref, zv_ref, gain_ref, w_ref, b_ref, ya_ref):
        ua = _gelu(zu_ref[...].astype(F32))
        vg = _gelu(zv_ref[...].astype(F32))
        for g in range(A_GROUPS):
            sl = slice(g * GROUP_DIM, (g + 1) * GROUP_DIM)
            _, _, vn = _group_norm(vg[:, sl], gain_ref[:, sl])
            f = _dot(w_ref[g], vn.astype(BF16), _NN) + b_ref[:, sl]
            ya_ref[:, sl] = (ua[:, sl] * f).astype(BF16)

    return pl.pallas_call(
        body, name=name, grid=(t // CHUNK,),
        in_specs=[zu, zv, full2((1, A_WIDTH)), w_spec, full2((CHUNK, A_WIDTH))], out_specs=zu,
        out_shape=jax.ShapeDtypeStruct((t, A_WIDTH + B_WIDTH), BF16), compiler_params=_params("parallel"),
    )(z, z, gain, wtril, bias_exp)


def _gmlp_bwd(z, dcat, gain, wtril, wtril_t, bias_exp, name):
    t = z.shape[0]
    zu = pl.BlockSpec((CHUNK, A_WIDTH), lambda i: (i, 0))
    zv = pl.BlockSpec((CHUNK, A_WIDTH), lambda i: (i, 1))
    full2 = lambda shp: pl.BlockSpec(shp, lambda i: (0, 0))
    w_spec = pl.BlockSpec((A_GROUPS, CHUNK, CHUNK), lambda i: (0, 0, 0))
    dz_spec = pl.BlockSpec((CHUNK, 2 * A_WIDTH), lambda i: (i, 0))

    def body(zu_ref, zv_ref, dya_ref, gain_ref, w_ref, wt_ref, b_ref, dz_ref, dw_ref, dgain_ref, dbias_ref):
        @pl.when(pl.program_id(0) == 0)
        def _():
            dw_ref[...] = jnp.zeros_like(dw_ref)
            dgain_ref[...] = jnp.zeros_like(dgain_ref)
            dbias_ref[...] = jnp.zeros_like(dbias_ref)

        zu_v = zu_ref[...].astype(F32)
        zv_v = zv_ref[...].astype(F32)
        dya = dya_ref[...].astype(F32)
        ua = _gelu(zu_v)
        vg = _gelu(zv_v)
        row = lax.broadcasted_iota(jnp.int32, (CHUNK, CHUNK), 0)
        col = lax.broadcasted_iota(jnp.int32, (CHUNK, CHUNK), 1)
        for g in range(A_GROUPS):
            sl = slice(g * GROUP_DIM, (g + 1) * GROUP_DIM)
            gain_g = gain_ref[:, sl]
            xhat, rstd, vn = _group_norm(vg[:, sl], gain_g)
            vn16 = vn.astype(BF16)
            f = _dot(w_ref[g], vn16, _NN) + b_ref[:, sl]
            df = dya[:, sl] * ua[:, sl]
            df16 = df.astype(BF16)
            dz_ref[:, sl] = (dya[:, sl] * f * _gelu_grad(zu_v[:, sl])).astype(BF16)
            dw_ref[g] += jnp.where(row >= col, _dot(df16, vn16, _NT), 0.0)
            dvn = _dot(wt_ref[g], df16, _NN)
            dgain_ref[:, sl] += _colsum(dvn * xhat)
            dxh = dvn * gain_g
            dvg = rstd * (dxh - jnp.mean(dxh, axis=-1, keepdims=True) - xhat * jnp.mean(dxh * xhat, axis=-1, keepdims=True))
            dz_ref[:, A_WIDTH + g * GROUP_DIM:A_WIDTH + (g + 1) * GROUP_DIM] = (dvg * _gelu_grad(zv_v[:, sl])).astype(BF16)
            dbias_ref[:, sl] += df

    return pl.pallas_call(
        body, name=name, grid=(t // CHUNK,),
        in_specs=[zu, zv, zu, full2((1, A_WIDTH)), w_spec, w_spec, full2((CHUNK, A_WIDTH))],
        out_specs=[dz_spec, w_spec, full2((1, A_WIDTH)), full2((CHUNK, A_WIDTH))],
        out_shape=[jax.ShapeDtypeStruct((t, 2 * A_WIDTH), BF16), jax.ShapeDtypeStruct((A_GROUPS, CHUNK, CHUNK), F32),
                   jax.ShapeDtypeStruct((1, A_WIDTH), F32), jax.ShapeDtypeStruct((CHUNK, A_WIDTH), F32)],
        compiler_params=_params("arbitrary"),
    )(z, z, dcat, gain, wtril, wtril_t, bias_exp)


def _rope_tables(pos, inv_freq, sign, name):
    t = pos.shape[0]
    tm = _tile(t)
    row = pl.BlockSpec((1, B_WIDTH), lambda i: (0, 0))
    blk = pl.BlockSpec((tm, B_WIDTH), lambda i: (i, 0))

    def body(pos_ref, f_ref, s_ref, cos_ref, sin_ref):
        ang = pos_ref[...] * f_ref[:, 0:LANES]
        cos_ref[...] = jnp.tile(jnp.cos(ang), (1, B_WIDTH // LANES))
        sin_ref[...] = jnp.tile(jnp.sin(ang) * s_ref[:, 0:LANES], (1, B_WIDTH // LANES))

    return pl.pallas_call(
        body, name=name, grid=(t // tm,), in_specs=[pl.BlockSpec((tm, 1), lambda i: (i, 0)), row, row],
        out_specs=[blk, blk], out_shape=[jax.ShapeDtypeStruct((t, B_WIDTH), F32)] * 2,
        compiler_params=_params("parallel"),
    )(pos, inv_freq, sign)


def _head_sum(v, seg):
    hi = v.astype(BF16)
    lo = (v - hi.astype(F32)).astype(BF16)
    return _dot(hi, seg, _NN) + _dot(lo, seg, _NN)


def _swap_halves(v):
    lane = lax.broadcasted_iota(jnp.int32, v.shape, 1)
    return jnp.where((lane & (HEAD_DIM - 1)) < HEAD_DIM // 2,pltpu.roll(v, B_WIDTH - HEAD_DIM // 2, 1), pltpu.roll(v, HEAD_DIM // 2, 1))


DILATIONS = tuple(dil for _, dil in PATTERNS)
SUBSEQ_TM = 256
LANES = 128


def _subseq_shape(t, dil):
    return (t // dil, dil * B_WIDTH)


def _subseq_spec(tm, dil):
    return pl.BlockSpec((tm // dil, dil * B_WIDTH), lambda i: (i, 0))


def _to_subseq(x, scr_ref, dil):
    if dil == 1:
        return x
    tm, w = x.shape
    for c in range(w // LANES):
        scr_ref[c * tm:(c + 1) * tm, :] = x[:, c * LANES:(c + 1) * LANES]
    return jnp.concatenate([scr_ref[pl.ds(c * tm + r, tm // dil, stride=dil), :]
                            for r in range(dil) for c in range(w // LANES)], axis=1)


def _from_subseq(y, scr_ref, dil):
    if dil == 1:
        return y
    n, w = y.shape[0], y.shape[1] // dil
    tm = n * dil
    for r in range(dil):
        for c in range(w // LANES):
            scr_ref[pl.ds(c * tm + r, n, stride=dil), :] = y[:, r * w + c * LANES:r * w + (c + 1) * LANES]
    return jnp.concatenate([scr_ref[c * tm:(c + 1) * tm, :] for c in range(w // LANES)], axis=1)


def _subseq_scratch(tm):
    return pltpu.VMEM((B_WIDTH // LANES * tm, LANES), F32)


def _qk_prep(z, cos_t, sin_t, gq, gk, seg, name):
    t = z.shape[0]
    tm = _tile(t, (SUBSEQ_TM,))
    col = lambda c: pl.BlockSpec((tm, B_WIDTH), lambda i: (i, c))
    row = pl.BlockSpec((1, B_WIDTH), lambda i: (0, 0))
    blk = col(0)
    nd = len(DILATIONS)

    def body(q_ref, k_ref, v_ref, cos_ref, sin_ref, gq_ref, gk_ref, seg_ref, *rest):
        out_refs, scr_ref = rest[:-1], rest[-1]

        def norm_rot(x, g):
            r = lax.rsqrt(_head_sum(x * x, seg_ref[...]) * (1.0 / HEAD_DIM) + EPS)
            xn = x * r * g
            return xn * cos_ref[...] + _swap_halves(xn) * sin_ref[...]

        vals = (norm_rot(q_ref[...].astype(F32), gq_ref[...]), norm_rot(k_ref[...].astype(F32), gk_ref[...]),
                v_ref[...].astype(F32))
        for a, val in enumerate(vals):
            for b, dil in enumerate(DILATIONS):
                out_refs[a * nd + b][...] = _to_subseq(val, scr_ref, dil).astype(BF16)

    outs = pl.pallas_call(
        body, name=name, grid=(t // tm,),
        in_specs=[col(2), col(3), col(4), blk, blk, row, row, pl.BlockSpec((B_WIDTH, B_WIDTH), lambda i: (0, 0))],
        out_specs=[_subseq_spec(tm, dil) for _ in range(3) for dil in DILATIONS],
        out_shape=[jax.ShapeDtypeStruct(_subseq_shape(t, dil), BF16) for _ in range(3) for dil in DILATIONS],
        scratch_shapes=[_subseq_scratch(tm)], compiler_params=_params("parallel"),
    )(z, z, z, cos_t, sin_t, gq, gk, seg)
    return outs[:nd], outs[nd:2 * nd], outs[2 * nd:]


def _qk_prep_bwd(z, dqs, dks, dvs, cos_t, sin_t, gq, gk, seg, name):
    t = z.shape[0]
    tm = _tile(t, (SUBSEQ_TM,))
    col = lambda c: pl.BlockSpec((tm, B_WIDTH), lambda i: (i, c))
    row = pl.BlockSpec((1, B_WIDTH), lambda i: (0, 0))
    blk = col(0)
    nb = len(DILATIONS)
    subs = [_subseq_spec(tm, dil) for dil in DILATIONS]

    def body(*refs):
        q_ref, k_ref = refs[0], refs[1]
        dq_refs, dk_refs, dv_refs = refs[2:2 + nb], refs[2 + nb:2 + 2 * nb], refs[2 + 2 * nb:2 + 3 * nb]
        cos_ref, sin_ref, gq_ref, gk_ref, seg_ref, dz_ref, dgq_ref, dgk_ref, scr_ref = refs[2 + 3 * nb:]

        @pl.when(pl.program_id(0) == 0)
        def _():
            dgq_ref[...] = jnp.zeros_like(dgq_ref)
            dgk_ref[...] = jnp.zeros_like(dgk_ref)

        def total(d_refs):
            return sum(_from_subseq(r_[...], scr_ref, dil) for r_, dil in zip(d_refs, DILATIONS))

        def back(x, d_refs, g, dg_ref):
            dout = total(d_refs)
            dy = dout * cos_ref[...] + _swap_halves(dout * sin_ref[...])
            r = lax.rsqrt(_head_sum(x * x, seg_ref[...]) * (1.0 / HEAD_DIM) + EPS)
            xn = x * r
            dg_ref[...] += _colsum(dy * xn)
            dxn = dy * g
            return r * (dxn - xn * (_head_sum(dxn * xn, seg_ref[...]) * (1.0 / HEAD_DIM)))

        dz_ref[:, 0:B_WIDTH] = back(q_ref[...].astype(F32), dq_refs, gq_ref[...], dgq_ref).astype(BF16)
        dz_ref[:, B_WIDTH:2 * B_WIDTH] = back(k_ref[...].astype(F32), dk_refs, gk_ref[...], dgk_ref).astype(BF16)
        dz_ref[:, 2 * B_WIDTH:3 * B_WIDTH] = total(dv_refs).astype(BF16)

    return pl.pallas_call(
        body, name=name, grid=(t // tm,),
        in_specs=[col(2), col(3)] + subs * 3 + [blk, blk, row, row, pl.BlockSpec((B_WIDTH, B_WIDTH), lambda i: (0, 0))],
        out_specs=[pl.BlockSpec((tm, 3 * B_WIDTH), lambda i: (i, 0)), row, row],
        out_shape=[jax.ShapeDtypeStruct((t, 3 * B_WIDTH), BF16), jax.ShapeDtypeStruct((1, B_WIDTH), F32),
                   jax.ShapeDtypeStruct((1, B_WIDTH), F32)],
        scratch_shapes=[_subseq_scratch(tm)], compiler_params=_params("arbitrary"),
    )(z, z, *dqs, *dks, *dvs, cos_t, sin_t, gq, gk, seg)


def _subseq_views(x, col, name):
    t = x.shape[0]
    tm = _tile(t, (SUBSEQ_TM,))

    def body(x_ref, *rest):
        out_refs, scr_ref = rest[:-1], rest[-1]
        val = x_ref[...].astype(F32)
        for o_ref, dil in zip(out_refs, DILATIONS):
            o_ref[...] = _to_subseq(val, scr_ref, dil).astype(o_ref.dtype)

    return pl.pallas_call(
        body, name=name, grid=(t // tm,), in_specs=[pl.BlockSpec((tm, B_WIDTH), lambda i: (i, col))],
        out_specs=[_subseq_spec(tm, dil) for dil in DILATIONS],
        out_shape=[jax.ShapeDtypeStruct(_subseq_shape(t, dil), x.dtype) for dil in DILATIONS],
        scratch_shapes=[_subseq_scratch(tm)], compiler_params=_params("parallel"),
    )(x)


def _attn_fwd(q, k, v, dil, name):
    t = q.shape[0] * dil
    nb = t // dil // Q_BLOCK
    cur = pl.BlockSpec((Q_BLOCK, B_WIDTH), lambda r, i: (i, r))
    prev = pl.BlockSpec((Q_BLOCK, B_WIDTH), lambda r, i: (jnp.maximum(i - 1, 0), r))

    def body(q_ref, kp_ref, kc_ref, vp_ref, vc_ref, o_ref, lse_ref):
        i = pl.program_id(1)
        q = q_ref[...]
        kk = jnp.concatenate([kp_ref[...], kc_ref[...]], axis=0)
        vv = jnp.concatenate([vp_ref[...], vc_ref[...]], axis=0)
        a = lax.broadcasted_iota(jnp.int32, (Q_BLOCK, 2 * Q_BLOCK), 0)
        j = lax.broadcasted_iota(jnp.int32, (Q_BLOCK, 2 * Q_BLOCK), 1)
        dist = a + Q_BLOCK - j
        mask = (dist >= 0) & (dist <= Q_BLOCK) & ((j >= Q_BLOCK) | (i > 0))
        sls = [slice(h * HEAD_DIM, (h + 1) * HEAD_DIM) for h in range(HEADS)]
        scores = [_dot(q[:, sl], kk[:, sl], _NT) for sl in sls]
        ps, dens = [], []
        for sl, s in zip(sls, scores):
            s = jnp.where(mask, s * (HEAD_DIM ** -0.5), NEG)
            m = jnp.max(s, axis=-1, keepdims=True)
            p = jnp.exp(s - m)
            den = jnp.sum(p, axis=-1, keepdims=True)
            ps.append(p.astype(BF16))
            dens.append(den)
            lse_ref[:, sl] = jnp.broadcast_to(m + jnp.log(den), (Q_BLOCK, HEAD_DIM))
        for sl, p, den in zip(sls, ps, dens):
            o_ref[:, sl] = _dot(p, vv[:, sl], _NN) / den

    return pl.pallas_call(
        body, name=name, grid=(dil, nb), in_specs=[cur, prev, cur, prev, cur], out_specs=[cur, cur],
        out_shape=[jax.ShapeDtypeStruct(_subseq_shape(t, dil), F32)] * 2,
        compiler_params=_params("parallel", "parallel"),
    )(q, k, k, v, v)


def _attn_merge(outs, lses, cat, name):
    nb = len(DILATIONS)
    t = cat.shape[0]
    tm = _tile(t, (SUBSEQ_TM,))
    subs = [_subseq_spec(tm, dil) for dil in DILATIONS]

    def body(*refs):
        o_refs, l_refs = refs[:nb], refs[nb:2 * nb]
        yb_refs, lse_refs, cat_ref, scr_ref = refs[2 * nb + 1:3 * nb + 1], refs[3 * nb + 1:4 * nb + 1], refs[4 * nb + 1], refs[4 * nb + 2]
        ls = [_from_subseq(r[...], scr_ref, dil) for r, dil in zip(l_refs, DILATIONS)]
        m = functools.reduce(jnp.maximum, ls)
        tot = m + jnp.log(sum(jnp.exp(l - m) for l in ls))
        yb = sum(jnp.exp(l - tot) * _from_subseq(o[...], scr_ref, dil) for l, o, dil in zip(ls, o_refs, DILATIONS))
        cat_ref[...] = yb.astype(BF16)
        yb = yb.astype(BF16).astype(F32)
        for yb_ref, lse_ref, dil in zip(yb_refs, lse_refs, DILATIONS):
            yb_ref[...] = _to_subseq(yb, scr_ref, dil).astype(BF16)
            lse_ref[...] = _to_subseq(tot, scr_ref, dil)

    outs_ = pl.pallas_call(
        body, name=name, grid=(t // tm,), in_specs=subs * 2 + [pl.BlockSpec(memory_space=pl.ANY)],
        out_specs=subs * 2 + [pl.BlockSpec((tm, B_WIDTH), lambda i: (i, A_WIDTH // B_WIDTH))],
        out_shape=[jax.ShapeDtypeStruct(_subseq_shape(t, dil), BF16) for dil in DILATIONS]
        + [jax.ShapeDtypeStruct(_subseq_shape(t, dil), F32) for dil in DILATIONS] + [jax.ShapeDtypeStruct(cat.shape, BF16)],
        input_output_aliases={2 * nb: 2 * nb}, scratch_shapes=[_subseq_scratch(tm)], compiler_params=_params("parallel"),
    )(*outs, *lses, cat)
    return outs_[:nb], outs_[nb:2 * nb], outs_[2 * nb]


def _attn_bwd(q, k, v, do, o, lse, dil, name):
    t = q.shape[0] * dil
    nb = t // dil // Q_BLOCK
    cur = pl.BlockSpec((Q_BLOCK, B_WIDTH), lambda r, i: (i, r))
    prev = pl.BlockSpec((Q_BLOCK, B_WIDTH), lambda r, i: (jnp.maximum(i - 1, 0), r))
    scale = HEAD_DIM ** -0.5

    def body(q_ref, kp_ref, kc_ref, vp_ref, vc_ref, do_ref, o_ref, lse_ref, dq_ref, dk_ref, dv_ref,
             ck_ref, cv_ref, tk_ref, tv_ref):
        i = pl.program_id(1)

        @pl.when(i == 0)
        def _():
            ck_ref[...] = jnp.zeros_like(ck_ref)
            cv_ref[...] = jnp.zeros_like(cv_ref)

        q = q_ref[...]
        kk = jnp.concatenate([kp_ref[...], kc_ref[...]], axis=0)
        vv = jnp.concatenate([vp_ref[...], vc_ref[...]], axis=0)
        do = do_ref[...]
        dof = do.astype(F32)
        of = o_ref[...].astype(F32)
        a = lax.broadcasted_iota(jnp.int32, (Q_BLOCK, 2 * Q_BLOCK), 0)
        j = lax.broadcasted_iota(jnp.int32, (Q_BLOCK, 2 * Q_BLOCK), 1)
        dist = a + Q_BLOCK - j
        mask = (dist >= 0) & (dist <= Q_BLOCK) & ((j >= Q_BLOCK) | (i > 0))
        sls = [slice(h * HEAD_DIM, (h + 1) * HEAD_DIM) for h in range(HEADS)]
        scores = [_dot(q[:, sl], kk[:, sl], _NT) for sl in sls]
        dps = [_dot(do[:, sl], vv[:, sl], _NT) for sl in sls]
        ps, dss = [], []
        for sl, s, dp in zip(sls, scores, dps):
            p = jnp.exp(jnp.where(mask, s * scale, NEG) - lse_ref[:, sl.start:sl.start + 1])
            delta = jnp.sum(dof[:, sl] * of[:, sl], axis=-1, keepdims=True)
            dss.append((p * (dp - delta) * scale).astype(BF16))
            ps.append(p.astype(BF16))
        for sl, p, ds in zip(sls, ps, dss):
            dq_ref[:, sl] = _dot(ds, kk[:, sl], _NN)
            dv_t = _dot(do[:, sl], p, _TN)
            dk_t = _dot(q[:, sl], ds, _TN)
            tk_ref[sl, :] = ck_ref[sl, :] + dk_t[:, :Q_BLOCK]
            tv_ref[sl, :] = cv_ref[sl, :] + dv_t[:, :Q_BLOCK]
            ck_ref[sl, :] = dk_t[:, Q_BLOCK:]
            cv_ref[sl, :] = dv_t[:, Q_BLOCK:]

        @pl.when(i >= 1)
        def _():
            rows = pl.ds(pl.multiple_of((i - 1) * Q_BLOCK, Q_BLOCK), Q_BLOCK)
            dk_ref[rows, :] = tk_ref[...].T
            dv_ref[rows, :] = tv_ref[...].T

        @pl.when(i == nb - 1)
        def _():
            rows = pl.ds((nb - 1) * Q_BLOCK, Q_BLOCK)
            dk_ref[rows, :] = ck_ref[...].T
            dv_ref[rows, :] = cv_ref[...].T

    whole = pl.BlockSpec((t // dil, B_WIDTH), lambda r, i: (0, r))
    return pl.pallas_call(
        body, name=name, grid=(dil, nb), in_specs=[cur, prev, cur, prev, cur, cur, cur, cur],
        out_specs=[cur, whole, whole], out_shape=[jax.ShapeDtypeStruct(_subseq_shape(t, dil), F32)] * 3,
        scratch_shapes=[pltpu.VMEM((B_WIDTH, Q_BLOCK), F32)] * 4,
        compiler_params=_params("parallel", "arbitrary"),
    )(q, k, k, v, v, do, o, lse)


FFN_TN = 256
FFN_FWD_CHUNK = 256
FFN_BWD_CHUNK = 128


def _ffn_up(h, up_t, name):
    t, k = h.shape
    tm = _tile(t)

    def body(h_ref, w_ref, o_ref):
        o_ref[...] = _dot(h_ref[...], w_ref[...], _NT).astype(BF16)

    return pl.pallas_call(
        body, name=name, grid=(2, t // tm),
        in_specs=[pl.BlockSpec((tm, k), lambda p, i: (i, 0)), pl.BlockSpec((None, FFN_DIM, k), lambda p, i: (p, 0, 0))],
        out_specs=pl.BlockSpec((None, tm, FFN_DIM), lambda p, i: (p, i, 0)),
        out_shape=jax.ShapeDtypeStruct((2, t, FFN_DIM), BF16), compiler_params=_params("parallel", "parallel"),
    )(h, up_t.reshape(2, FFN_DIM, k))


def _ffn_up_dx(du, up_t, name):
    t = du.shape[1]
    k = up_t.shape[1]
    tm = _tile(t)

    def body(a_ref, b_ref, o_ref):
        o_ref[...] = _dot(a_ref[0], b_ref[0], _NN) + _dot(a_ref[1], b_ref[1], _NN)

    return pl.pallas_call(
        body, name=name, grid=(t // tm,),
        in_specs=[pl.BlockSpec((2, tm, FFN_DIM), lambda i: (0, i, 0)), pl.BlockSpec((2, FFN_DIM, k), lambda i: (0, 0, 0))],
        out_specs=pl.BlockSpec((tm, k), lambda i: (i, 0)), out_shape=jax.ShapeDtypeStruct((t, k), F32),
        compiler_params=_params("parallel"),
    )(du, up_t.reshape(2, FFN_DIM, k))


def _ffn_conv(win, w_ref, b_ref, p):
    x = win.astype(F32)
    x0, x1, x2 = x[FFN_HALO:], pltpu.roll(x, 1, 0)[FFN_HALO:], pltpu.roll(x, 2, 0)[FFN_HALO:]
    return b_ref[p] + w_ref[p, 2:3, :] * x0 + w_ref[p, 1:2, :] * x1 + w_ref[p, 0:1, :] * x2


def _zero_if(cond, v):
    return jnp.where(cond, 0, v).astype(v.dtype)


def _ffn_act(u, dw_w, dw_b, name):
    t = u.shape[1]
    tm = _tile(t)
    chunk = min(FFN_FWD_CHUNK, tm)
    hb = tm // FFN_HALO
    main = pl.BlockSpec((2, tm, FFN_TN), lambda i, j: (0, i, j))
    halo = pl.BlockSpec((2, FFN_HALO, FFN_TN), lambda i, j: (0, jnp.maximum(i * hb - 1, 0), j))
    wsp = pl.BlockSpec((2, FFN_CONV_WIDTH, FFN_TN), lambda i, j: (0, 0, j))
    bsp = pl.BlockSpec((2, 1, FFN_TN), lambda i, j: (0, 0, j))

    def body(u_ref, uh_ref, w_ref, b_ref, o_ref, z_ref):
        first = pl.program_id(0) == 0

        def emit(rows, wins):
            za, zb = _ffn_conv(wins[0], w_ref, b_ref, 0), _ffn_conv(wins[1], w_ref, b_ref, 1)
            o_ref[rows, :] = (za * _sigmoid(za) * zb).astype(BF16)
            z_ref[0, rows, :] = za.astype(BF16)
            z_ref[1, rows, :] = zb.astype(BF16)

        emit(pl.ds(0, chunk), [jnp.concatenate([_zero_if(first, uh_ref[p]), u_ref[p, 0:chunk, :]], axis=0) for p in range(2)])

        def step(c, carry):
            s = pl.multiple_of(c * chunk, chunk)
            emit(pl.ds(s, chunk), [u_ref[p, pl.ds(s - FFN_HALO, chunk + FFN_HALO), :] for p in range(2)])
            return carry

        lax.fori_loop(1, tm // chunk, step, 0)

    return pl.pallas_call(
        body, name=name, grid=(t // tm, FFN_DIM // FFN_TN), in_specs=[main, halo, wsp, bsp],
        out_specs=[pl.BlockSpec((tm, FFN_TN), lambda i, j: (i, j)), main],
        out_shape=[jax.ShapeDtypeStruct((t, FFN_DIM), BF16), jax.ShapeDtypeStruct((2, t, FFN_DIM), BF16)],
        compiler_params=_params("parallel", "parallel"),
    )(u, u, dw_w, dw_b)


def _fold8(v):
    return jnp.sum(v.reshape(v.shape[0] // 8, 8, v.shape[1]), axis=0)


def _ffn_act_bwd(u, z, dact, dw_w, name):
    t = u.shape[1]
    tm = _tile(t)
    chunk = min(FFN_BWD_CHUNK, tm // 2)
    halo = FFN_HALO
    hb = tm // halo
    nt = t // tm
    last_halo = t // halo - 1
    next_i = lambda i: jnp.minimum((i + 1) * hb, last_halo)
    main = pl.BlockSpec((2, tm, FFN_TN), lambda j, i: (0, i, j))
    nxt = pl.BlockSpec((2, halo, FFN_TN), lambda j, i: (0, next_i(i), j))
    wsp = pl.BlockSpec((2, FFN_CONV_WIDTH, FFN_TN), lambda j, i: (0, 0, j))
    bsp = pl.BlockSpec((2, 1, FFN_TN), lambda j, i: (0, 0, j))

    def body(u_ref, z_ref, zn_ref, da_ref, dan_ref, w_ref, du_ref, dw_ref, db_ref, acc_ref):
        i = pl.program_id(1)
        last = i == nt - 1
        acc_ref[...] = jnp.zeros_like(acc_ref)

        def emit(rows, zs, dact):
            n = chunk + halo
            za, zb, dact = zs[0].astype(F32), zs[1].astype(F32), dact.astype(F32)
            sg = _sigmoid(za)
            dzs = (dact * zb * (sg * (1.0 + za * (1.0 - sg))), dact * (za * sg))
            for p, dz in enumerate(dzs):
                ahead = (dz[:chunk], pltpu.roll(dz, n - 1, 0)[:chunk], pltpu.roll(dz, n - 2, 0)[:chunk])
                um = u_ref[p, rows, :].astype(F32)
                acc_ref[p, FFN_CONV_WIDTH] += _fold8(ahead[0])
                du = None
                for j, dzj in enumerate(ahead):
                    k = FFN_CONV_WIDTH - 1 - j
                    acc_ref[p, k] += _fold8(dzj * um)
                    term = w_ref[p, k:k + 1, :] * dzj
                    du = term if du is None else du + term
                du_ref[p, rows, :] = du.astype(BF16)

        def step(c, carry):
            s = pl.multiple_of(c * chunk, chunk)
            emit(pl.ds(s, chunk), [z_ref[p, pl.ds(s, chunk + halo), :] for p in range(2)], da_ref[pl.ds(s, chunk + halo), :])
            return carry

        lax.fori_loop(0, tm // chunk - 1, step, 0)
        s = tm - chunk
        emit(pl.ds(s, chunk),
             [jnp.concatenate([z_ref[p, s:tm, :], zn_ref[p]], axis=0) for p in range(2)],
             jnp.concatenate([da_ref[s:tm, :], _zero_if(last, dan_ref[...])], axis=0))

        @pl.when(i == 0)
        def _():
            dw_ref[...] = jnp.zeros_like(dw_ref)
            db_ref[...] = jnp.zeros_like(db_ref)

        for p in range(2):
            for k in range(FFN_CONV_WIDTH):
                dw_ref[p, k:k + 1, :] += _colsum(acc_ref[p, k])
            db_ref[p] += _colsum(acc_ref[p, FFN_CONV_WIDTH])

    return pl.pallas_call(
        body, name=name, grid=(FFN_DIM // FFN_TN, nt),
        in_specs=[main, main, nxt, pl.BlockSpec((tm, FFN_TN), lambda j, i: (i, j)),
                  pl.BlockSpec((halo, FFN_TN), lambda j, i: (next_i(i), j)), wsp],
        out_specs=[main, wsp, bsp],
        out_shape=[jax.ShapeDtypeStruct((2, t, FFN_DIM), BF16), jax.ShapeDtypeStruct((2, FFN_CONV_WIDTH, FFN_DIM), F32),
                   jax.ShapeDtypeStruct((2, 1, FFN_DIM), F32)],
        scratch_shapes=[pltpu.VMEM((2, FFN_CONV_WIDTH + 1, 8, FFN_TN), F32)],
        compiler_params=_params("parallel", "arbitrary"),
    )(u, z, z, dact, dact, dw_w)


CONV_TM = 256
CONV_ROWS = 128
CONV_LANES = 128


def _glu_window(pa_ref, pah_ref, pg_ref, pgh_ref, scr_ref, first):
    ah, gh = pah_ref[...].astype(F32), pgh_ref[...].astype(F32)
    scr_ref[0:CONV_HALO, :] = jnp.where(first, 0.0, ah * _sigmoid(gh))
    scr_ref[CONV_HALO:, :] = pa_ref[...].astype(F32) * _sigmoid(pg_ref[...].astype(F32))


def _tap_slabs(win, rows, ahead):
    n = win.shape[0]
    for s in range(8):
        ws = win if s == 0 else pltpu.roll(win, n - s if ahead else s, 0)
        for q in range(CONV_HALO // 8):
            o = 8 * q + s
            if o < CONV_WIDTH:
                start = 8 * q if ahead else CONV_HALO - 8 * q
                yield CONV_WIDTH - 1 - o, ws[start:start + rows]


def _conformer_specs(t):
    tm = _tile(t, (CONV_TM, 128))
    hb = tm // CONV_HALO
    d = D_MODEL
    main = lambda c: pl.BlockSpec((tm, d), lambda i: (i, c))
    halo = lambda c: pl.BlockSpec((CONV_HALO, d), lambda i: (jnp.maximum(i * hb - 1, 0), c))
    row = pl.BlockSpec((1, d), lambda i: (0, 0))
    wsp = pl.BlockSpec((CONV_WIDTH, d), lambda i: (0, 0))
    return tm, main, halo, row, wsp


def _conformer_mid(p, dw_w, dw_b, ln_g, ln_b, name):
    t = p.shape[0]
    tm, main, halo, row, wsp = _conformer_specs(t)
    d, lanes = D_MODEL, CONV_LANES

    def body(pa_ref, pah_ref, pg_ref, pgh_ref, w_ref, b_ref, g_ref, lb_ref, o_ref, dc_ref, scr_ref):
        _glu_window(pa_ref, pah_ref, pg_ref, pgh_ref, scr_ref, pl.program_id(0) == 0)
        for c in range(d // lanes):
            ls = slice(c * lanes, (c + 1) * lanes)
            acc = jnp.broadcast_to(b_ref[:, ls], (tm, lanes))
            for k, slab in _tap_slabs(scr_ref[:, ls], tm, False):
                acc = acc + w_ref[k:k + 1, ls] * slab
            dc_ref[:, ls] = acc

        def norm(r, carry):
            r0 = pl.multiple_of(r * 32, 32)
            dc = dc_ref[pl.ds(r0, 32), :]
            xc = dc - jnp.mean(dc, axis=-1, keepdims=True)
            ln = xc * lax.rsqrt(jnp.mean(xc * xc, axis=-1, keepdims=True) + EPS) * g_ref[...] + lb_ref[...]
            o_ref[pl.ds(r0, 32), :] = (ln * _sigmoid(ln)).astype(BF16)
            return carry

        lax.fori_loop(0, tm // 32, norm, 0)

    return pl.pallas_call(
        body, name=name, grid=(t // tm,), in_specs=[main(0), halo(0), main(1), halo(1), wsp, row, row, row],
        out_specs=[main(0), main(0)], out_shape=[jax.ShapeDtypeStruct((t, d), BF16), jax.ShapeDtypeStruct((t, d), F32)],
        scratch_shapes=[pltpu.VMEM((tm + CONV_HALO, d), F32)], compiler_params=_params("parallel"),
    )(p, p, p, p, dw_w, dw_b, ln_g, ln_b)


def _conformer_mid_bwd(p, dc, ds, ln_g, ln_b, name):
    t = p.shape[0]
    tm, main, halo, row, wsp = _conformer_specs(t)
    d, nt = D_MODEL, t // tm
    rows, lanes = CONV_ROWS, CONV_LANES

    def body(pa_ref, pah_ref, pg_ref, pgh_ref, dc_ref, ds_ref, g_ref, lb_ref,
             ddc_ref, dw_ref, db_ref, dg_ref, dlb_ref, scr_ref, wacc_ref, racc_ref):
        i = pl.program_id(0)

        @pl.when(i == 0)
        def _():
            wacc_ref[...] = jnp.zeros_like(wacc_ref)
            racc_ref[...] = jnp.zeros_like(racc_ref)

        _glu_window(pa_ref, pah_ref, pg_ref, pgh_ref, scr_ref, i == 0)

        def norm_bwd(r, carry):
            r0 = pl.multiple_of(r * 32, 32)
            dcv = dc_ref[pl.ds(r0, 32), :]
            xc = dcv - jnp.mean(dcv, axis=-1, keepdims=True)
            rstd = lax.rsqrt(jnp.mean(xc * xc, axis=-1, keepdims=True) + EPS)
            xhat = xc * rstd
            ln = xhat * g_ref[...] + lb_ref[...]
            sg = _sigmoid(ln)
            dln = ds_ref[pl.ds(r0, 32), :].astype(F32) * (sg * (1.0 + ln * (1.0 - sg)))
            dxh = dln * g_ref[...]
            ddc = rstd * (dxh - jnp.mean(dxh, axis=-1, keepdims=True) - xhat * jnp.mean(dxh * xhat, axis=-1, keepdims=True))
            ddc_ref[pl.ds(r0, 32), :] = ddc
            racc_ref[0] += _fold8(dln * xhat)
            racc_ref[1] += _fold8(dln)
            racc_ref[2] += _fold8(ddc)
            return carry

        lax.fori_loop(0, tm // 32, norm_bwd, 0)

        for c in range(d // lanes):
            ls = slice(c * lanes, (c + 1) * lanes)

            def taps(r, carry, ls=ls):
                r0 = pl.multiple_of(r * rows, rows)
                ddc = ddc_ref[pl.ds(r0, rows), ls]
                for k, slab in _tap_slabs(scr_ref[pl.ds(r0, rows + CONV_HALO), ls], rows, False):
                    wacc_ref[k, :, ls] += _fold8(ddc * slab)
                return carry

            lax.fori_loop(0, tm // rows, taps, 0)

        @pl.when(i == nt - 1)
        def _():
            for k in range(CONV_WIDTH):
                dw_ref[k:k + 1, :] = _colsum(wacc_ref[k])
            dg_ref[...] = _colsum(racc_ref[0])
            dlb_ref[...] = _colsum(racc_ref[1])
            db_ref[...] = _colsum(racc_ref[2])

    return pl.pallas_call(
        body, name=name, grid=(nt,), in_specs=[main(0), halo(0), main(1), halo(1), main(0), main(0), row, row],
        out_specs=[main(0), wsp, row, row, row],
        out_shape=[jax.ShapeDtypeStruct((t, d), F32), jax.ShapeDtypeStruct((CONV_WIDTH, d), F32)]
        + [jax.ShapeDtypeStruct((1, d), F32)] * 3,
        scratch_shapes=[pltpu.VMEM((tm + CONV_HALO, d), F32), pltpu.VMEM((CONV_WIDTH, 8, d), F32), pltpu.VMEM((3, 8, d), F32)],
        compiler_params=_params("arbitrary"),
    )(p, p, p, p, dc, ds, ln_g, ln_b)


def _conformer_glu_bwd(p, ddc, dw_w, name):
    t = p.shape[0]
    d = D_MODEL
    tm = _tile(t, (CONV_TM, 128))
    hb = tm // CONV_HALO
    nt = t // tm
    last_halo = t // CONV_HALO - 1
    rows, lanes = CONV_ROWS, CONV_LANES
    col = lambda c: pl.BlockSpec((tm, d), lambda i: (i, c))
    nxt = pl.BlockSpec((CONV_HALO, d), lambda i: (jnp.minimum((i + 1) * hb, last_halo), 0))

    def body(pa_ref, pg_ref, ddc_ref, ddcn_ref, w_ref, dp_ref, db_ref, scr_ref, acc_ref):
        i = pl.program_id(0)

        @pl.when(i == 0)
        def _():
            acc_ref[...] = jnp.zeros_like(acc_ref)

        scr_ref[0:tm, :] = ddc_ref[...]
        scr_ref[tm:, :] = _zero_if(i == nt - 1, ddcn_ref[...])
        for c in range(d // lanes):
            ls = slice(c * lanes, (c + 1) * lanes)
            gs = slice(d + c * lanes, d + (c + 1) * lanes)

            def taps(r, carry, ls=ls, gs=gs):
                r0 = pl.multiple_of(r * rows, rows)
                dglu = None
                for k, slab in _tap_slabs(scr_ref[pl.ds(r0, rows + CONV_HALO), ls], rows, True):
                    term = w_ref[k:k + 1, ls] * slab
                    dglu = term if dglu is None else dglu + term
                a = pa_ref[pl.ds(r0, rows), ls].astype(F32)
                sg = _sigmoid(pg_ref[pl.ds(r0, rows), ls].astype(F32))
                da = (dglu * sg).astype(BF16)
                dg = (dglu * a * sg * (1.0 - sg)).astype(BF16)
                dp_ref[pl.ds(r0, rows), ls] = da
                dp_ref[pl.ds(r0, rows), gs] = dg
                acc_ref[:, ls] += _fold8(da.astype(F32))
                acc_ref[:, gs] += _fold8(dg.astype(F32))
                return carry

            lax.fori_loop(0, tm // rows, taps, 0)

        @pl.when(i == nt - 1)
        def _():
            db_ref[...] = _colsum(acc_ref[...])

    return pl.pallas_call(
        body, name=name, grid=(nt,),
        in_specs=[col(0), col(1), col(0), nxt, pl.BlockSpec((CONV_WIDTH, d), lambda i: (0, 0))],
        out_specs=[pl.BlockSpec((tm, 2 * d), lambda i: (i, 0)), pl.BlockSpec((1, 2 * d), lambda i: (0, 0))],
        out_shape=[jax.ShapeDtypeStruct((t, 2 * d), BF16), jax.ShapeDtypeStruct((1, 2 * d), F32)],
        scratch_shapes=[pltpu.VMEM((tm + CONV_HALO, d), F32), pltpu.VMEM((8, 2 * d), F32)],
        compiler_params=_params("arbitrary"),
    )(p, p, ddc, ddc, dw_w)


def _colsum_call(a, name):
    t, n = a.shape
    tm = _tile(t)

    def body(a_ref, o_ref):
        @pl.when(pl.program_id(0) == 0)
        def _():
            o_ref[...] = jnp.zeros_like(o_ref)

        o_ref[...] += _colsum(a_ref[...].astype(F32))

    return pl.pallas_call(
        body, name=name, grid=(t // tm,), in_specs=[pl.BlockSpec((tm, n), lambda i: (i, 0))],
        out_specs=pl.BlockSpec((1, n), lambda i: (0, 0)), out_shape=jax.ShapeDtypeStruct((1, n), F32),
        compiler_params=_params("arbitrary"),
    )(a)


def _ada_fwd(c_all, w, name):
    rows, d = c_all.shape
    n = w.shape[1]
    tn = _tile(n, (256, 128))

    def body(c_ref, w_ref, o_ref):
        c = c_ref[...]
        o_ref[...] = _dot((c * _sigmoid(c)).astype(BF16), w_ref[...].astype(BF16), _NN)

    return pl.pallas_call(
        body, name=name, grid=(n // tn,),
        in_specs=[pl.BlockSpec((rows, d), lambda j: (0, 0)), pl.BlockSpec((d, tn), lambda j: (0, j))],
        out_specs=pl.BlockSpec((rows, tn), lambda j: (0, j)), out_shape=jax.ShapeDtypeStruct((rows, n), F32),
        compiler_params=_params("parallel"),
    )(c_all, w)


def _ada_bwd(c_all, dmod, name):
    rows, d = c_all.shape
    n = dmod.shape[1]
    tn = _tile(n, (256, 128))

    def body(c_ref, g_ref, o_ref):
        c = c_ref[...]
        o_ref[...] = _dot((c * _sigmoid(c)).astype(BF16), g_ref[...].astype(BF16), _TN)

    return pl.pallas_call(
        body, name=name, grid=(n // tn,),
        in_specs=[pl.BlockSpec((rows, d), lambda j: (0, 0)), pl.BlockSpec((rows, tn), lambda j: (0, j))],
        out_specs=pl.BlockSpec((d, tn), lambda j: (0, j)), out_shape=jax.ShapeDtypeStruct((d, n), F32),
        compiler_params=_params("parallel"),
    )(c_all, dmod)


def _sum_slots(a, name):
    s, r, c = a.shape
    tr = _row_tile(r, 256)

    def body(a_ref, o_ref):
        acc = a_ref[0].astype(F32)
        for k in range(1, s):
            acc = acc + a_ref[k].astype(F32)
        o_ref[...] = acc

    return pl.pallas_call(
        body, name=name, grid=(r // tr,), in_specs=[pl.BlockSpec((s, tr, c), lambda i: (0, i, 0))],
        out_specs=pl.BlockSpec((tr, c), lambda i: (i, 0)), out_shape=jax.ShapeDtypeStruct((r, c), F32),
        compiler_params=_params("parallel"),
    )(a)


def _sum_with_own(blocks, land, me, name):
    s, r, c = land.shape
    tr = _row_tile(r, 256)
    slot = lambda k: pl.BlockSpec((None, tr, c), lambda i, me_ref: ((me_ref[0] + k) % s, i, 0))

    def body(me_ref, own_ref, *refs):
        o_ref = refs[-1]
        acc = own_ref[...].astype(F32)
        for ref in refs[:-1]:
            acc = acc + ref[...].astype(F32)
        o_ref[...] = acc

    return pl.pallas_call(
        body, name=name, out_shape=jax.ShapeDtypeStruct((r, c), F32),
        grid_spec=pltpu.PrefetchScalarGridSpec(
            num_scalar_prefetch=1, grid=(r // tr,), in_specs=[slot(0)] + [slot(k) for k in range(1, s)],
            out_specs=pl.BlockSpec((tr, c), lambda i, me_ref: (i, 0))),
        compiler_params=_params("parallel"),
    )(me, blocks, *[land] * (s - 1))


def _adamw_update(w, g, m, v):
    nm = ADAM_B1 * m + (1.0 - ADAM_B1) * g
    nv = ADAM_B2 * v + (1.0 - ADAM_B2) * (g * g)
    m_hat = nm * (1.0 / (1.0 - ADAM_B1 ** ADAM_STEP))
    v_hat = nv * (1.0 / (1.0 - ADAM_B2 ** ADAM_STEP))
    return -ADAM_LR * (m_hat / (jnp.sqrt(v_hat) + ADAM_EPS) + ADAM_WD * w), nm, nv


def _adamw(w, g, m, v, name):
    l, r, c = w.shape
    tr = _row_tile(r, 256)
    blk = pl.BlockSpec((None, tr, c), lambda k, i: (k, i, 0))

    def body(w_ref, g_ref, m_ref, v_ref, d_ref, nm_ref, nv_ref):
        d_ref[...], nm_ref[...], nv_ref[...] = _adamw_update(w_ref[...], g_ref[...], m_ref[...], v_ref[...])

    return pl.pallas_call(
        body, name=name, grid=(l, r // tr), in_specs=[blk] * 4, out_specs=[blk] * 3,
        out_shape=[jax.ShapeDtypeStruct(w.shape, F32)] * 3, compiler_params=_params("parallel", "parallel"),
    )(w, g, m, v)


def _adamw_small(ws, gs, ms, vs, name):
    n = len(ws)
    two_d = lambda a: a.reshape(-1, a.shape[-1])

    def body(*refs):
        ins, outs = refs[:4 * n], refs[4 * n:]
        for a in range(n):
            outs[a][...], outs[n + a][...], outs[2 * n + a][...] = _adamw_update(*[ins[k * n + a][...] for k in range(4)])

    res = pl.pallas_call(
        body, name=name, out_shape=[jax.ShapeDtypeStruct(two_d(w).shape, F32) for w in ws] * 3,
    )(*[two_d(a) for a in (*ws, *gs, *ms, *vs)])
    return [[res[k * n + a].reshape(ws[a].shape) for a in range(n)] for k in range(3)]


def _mesh_pos():
    return lax.axis_index("x"), lax.axis_index("y"), lax.axis_index("c")


def _all_gather_vmem(x_shard, name):
    m_per, n = x_shard.shape

    def body(x_ref, out_ref, send_sems, recv_sems, local_sem):
        x, y, c = _mesh_pos()
        me, sibling = (x, y, c), (x, y, 1 - c)
        chips = [(1 - x, y), (x, 1 - y), (1 - x, 1 - y)]

        def rows(px, py, pc):
            return out_ref.at[pl.ds((4 * px + 2 * py + pc) * m_per, m_per), :]

        def copy(k, block, to, src=None):
            return pltpu.make_async_remote_copy(
                src_ref=rows(*block) if src is None else src, dst_ref=rows(*block),
                send_sem=send_sems.at[k], recv_sem=recv_sems.at[k], device_id=to, device_id_type=MESH)

        mine = pltpu.make_async_copy(x_ref, rows(*me), local_sem)
        mine.start()
        first = [copy(0, me, sibling, src=x_ref)]
        first += [copy(1 + j, me, (*chip, c), src=x_ref) for j, chip in enumerate(chips)]
        for cp in first:
            cp.start()
        passed = [copy(4 + j, (*chip, c), sibling) for j, chip in enumerate(chips)]
        for j, chip in enumerate(chips):
            copy(1 + j, (*chip, c), me).wait_recv()
            passed[j].start()
        copy(0, sibling, me).wait_recv()
        for j, chip in enumerate(chips):
            copy(4 + j, (*chip, 1 - c), me).wait_recv()
        for cp in first + passed:
            cp.wait_send()
        mine.wait()

    return pl.pallas_call(
        body, name=name, out_shape=jax.ShapeDtypeStruct((N_DEV * m_per, n), x_shard.dtype),
        in_specs=[pl.BlockSpec(memory_space=pltpu.VMEM)], out_specs=pl.BlockSpec(memory_space=pltpu.VMEM),
        scratch_shapes=[pltpu.SemaphoreType.DMA((7,)), pltpu.SemaphoreType.DMA((7,)), pltpu.SemaphoreType.DMA],
    )(x_shard)


def _all_gather_hbm(shards, name):
    n = len(shards)
    out_shape = [jax.ShapeDtypeStruct((N_DEV,) + s.shape, s.dtype) for s in shards]

    def body(*refs):
        x_refs, out_refs = refs[:n], refs[n:2 * n]
        send_sems, recv_sems, local_sems = refs[2 * n:]
        x, y, c = _mesh_pos()
        me, sibling = (x, y, c), (x, y, 1 - c)
        chips = [(1 - x, y), (x, 1 - y), (1 - x, 1 - y)]

        def blk(a, p):
            return out_refs[a].at[4 * p[0] + 2 * p[1] + p[2]]

        def copy(a, k, block, to, src=None):
            return pltpu.make_async_remote_copy(
                src_ref=blk(a, block) if src is None else src, dst_ref=blk(a, block),
                send_sem=send_sems.at[7 * a + k], recv_sem=recv_sems.at[7 * a + k], device_id=to, device_id_type=MESH)

        mine = [pltpu.make_async_copy(x_refs[a], blk(a, me), local_sems.at[a]) for a in range(n)]
        for cp in mine:
            cp.start()
        first = []
        for a in range(n):
            first.append(copy(a, 0, me, sibling, src=x_refs[a]))
            first += [copy(a, 1 + j, me, (*chip, c), src=x_refs[a]) for j, chip in enumerate(chips)]
        for cp in first:
            cp.start()
        passed = []
        for j, chip in enumerate(chips):
            for a in range(n):
                copy(a, 1 + j, (*chip, c), me).wait_recv()
                fwd = copy(a, 4 + j, (*chip, c), sibling)
                fwd.start()
                passed.append(fwd)
        for a in range(n):
            copy(a, 0, sibling, me).wait_recv()
            for j, chip in enumerate(chips):
                copy(a, 4 + j, (*chip, 1 - c), me).wait_recv()
        for cp in first + passed:
            cp.wait_send()
        for cp in mine:
            cp.wait()

    return pl.pallas_call(
        body, name=name, out_shape=out_shape, in_specs=[pl.BlockSpec(memory_space=pltpu.VMEM)] * n,
        out_specs=[pl.BlockSpec(memory_space=pl.ANY)] * n,
        scratch_shapes=[pltpu.SemaphoreType.DMA((7 * n,)), pltpu.SemaphoreType.DMA((7 * n,)), pltpu.SemaphoreType.DMA((n,))],
    )(*shards)


def _peers(x, y, c):
    flip = lambda v, f: 1 - v if f else v
    return [(flip(x, m & 4), flip(y, m & 2), flip(c, m & 1)) for m in range(1, N_DEV)]


def _dev_index(p):
    return 4 * p[0] + 2 * p[1] + p[2]


def _push_copies(src_refs, land_refs, send_sems, recv_sems, scatter, receive):
    x, y, c = _mesh_pos()
    me = _dev_index((x, y, c))
    copies = []
    for a, (src, land) in enumerate(zip(src_refs, land_refs)):
        for k, p in enumerate(_peers(x, y, c)):
            copies.append(pltpu.make_async_remote_copy(
                src_ref=src.at[_dev_index(p)] if scatter else src, dst_ref=land.at[_dev_index(p) if receive else me],
                send_sem=send_sems.at[7 * a + k], recv_sem=recv_sems.at[7 * a + k], device_id=p, device_id_type=MESH))
    return copies


_HBM = pl.BlockSpec(memory_space=pltpu.HBM)
_SEM = pl.BlockSpec(memory_space=pltpu.SEMAPHORE)
_EFFECT = pltpu.SideEffectType.DATAFLOW_SIDE_EFFECTING


def _pushes_start(srcs, lands, scatter, name):
    n = len(srcs)

    def body(*refs):
        src_refs, land_refs = refs[:n], refs[n:2 * n]
        send_sems, recv_sems = refs[2 * n], refs[2 * n + 1]
        token = refs[-1]
        for cp in _push_copies(src_refs, land_refs, send_sems, recv_sems, scatter, receive=False):
            cp.start()
        token[...] = jnp.zeros_like(token)

    hbm = lambda a: pltpu.HBM(a.shape, a.dtype)
    sems = pltpu.SemaphoreType.DMA((7 * n,))
    outs = pl.pallas_call(
        body, name=name,
        out_shape=(sems, sems, *[hbm(a) for a in srcs], *[hbm(a) for a in lands], jax.ShapeDtypeStruct((8, 128), F32)),
        in_specs=[_HBM] * (2 * n), out_specs=(_SEM, _SEM, *[_HBM] * (2 * n), pl.BlockSpec(memory_space=pltpu.VMEM)),
        input_output_aliases={i: 2 + i for i in range(2 * n)},
        compiler_params=pltpu.CompilerParams(has_side_effects=_EFFECT),
    )(*[pltpu.with_memory_space_constraint(a, pltpu.HBM) for a in (*srcs, *lands)])
    return (outs[0], outs[1], outs[2:2 + n], outs[2 + n:2 + 2 * n], scatter), outs[-1]


def _pushes_wait(handle, after, name):
    send_sems, recv_sems, srcs, lands, scatter = handle
    n = len(srcs)

    def body(*refs):
        src_refs, land_refs = refs[:n], refs[n:2 * n]
        for cp in _push_copies(src_refs, land_refs, refs[2 * n], refs[2 * n + 1], scatter, receive=True):
            cp.wait_send()
            cp.wait_recv()

    hbm = lambda a: pltpu.HBM(a.shape, a.dtype)
    outs = pl.pallas_call(
        body, name=name, out_shape=tuple(hbm(a) for a in (*srcs, *lands)),
        in_specs=[_HBM] * (2 * n) + [_SEM, _SEM, pl.BlockSpec(memory_space=pl.ANY)], out_specs=tuple([_HBM] * (2 * n)),
        input_output_aliases={i: i for i in range(2 * n)},
        compiler_params=pltpu.CompilerParams(has_side_effects=_EFFECT),
    )(*srcs, *lands, send_sems, recv_sems, after)
    return outs[:n], outs[n:]


def _landing_zones(srcs, name):
    n = len(srcs)

    def body(*refs):
        src_refs, land_refs, bufs, sems = refs[:n], refs[n:2 * n], refs[2 * n:3 * n], refs[3 * n]
        me = _dev_index(_mesh_pos())
        load = [pltpu.make_async_copy(src, buf, sems.at[a]) for a, (src, buf) in enumerate(zip(src_refs, bufs))]
        store = [pltpu.make_async_copy(buf, land.at[me], sems.at[a]) for a, (buf, land) in enumerate(zip(bufs, land_refs))]
        for cp in load:
            cp.start()
        for ld, st in zip(load, store):
            ld.wait()
            st.start()
        for cp in store:
            cp.wait()

    any_spec = pl.BlockSpec(memory_space=pl.ANY)
    return pl.pallas_call(
        body, name=name, out_shape=[jax.ShapeDtypeStruct((N_DEV,) + s.shape, s.dtype) for s in srcs],
        in_specs=[any_spec] * n, out_specs=[any_spec] * n,
        scratch_shapes=[pltpu.VMEM(s.shape, s.dtype) for s in srcs] + [pltpu.SemaphoreType.DMA((n,))],
        compiler_params=pltpu.CompilerParams(vmem_limit_bytes=V7X_VMEM_LIMIT),
    )(*srcs)


def _ffn_forward(x, mod, norm_g, w, tag):
    sh, sc, gate = mod
    h = _modnorm(x, norm_g, sc, sh, f"{tag}_norm")
    u = _ffn_up(h, w["up_t"], f"{tag}_up")
    act, z = _ffn_act(u, w["dw_w"], w["dw_b"], f"{tag}_act")
    y, x_new = _matmul(act, w["down"], "nn", BF16, f"{tag}_down", resid=(x, gate))
    return x_new, (x, h, u, z, act, y)


def _behind(row, token):
    return row if token is None else row + token[0:1, 0:1]


def _ffn_backward(dx_new, dy, d_gate, saved, mod, norm_g, w, tag, emit, below):
    x, h, u, z, act, _ = saved
    _, sc, _ = mod
    d_down = _matmul_tn_acc(act, dy, f"{tag}_down_dw")
    dact = _matmul(dy, w["down"], "nt", BF16, f"{tag}_down_dx")
    du, d_dw_w, d_dw_b = _ffn_act_bwd(u, z, dact, w["dw_w"], f"{tag}_act_bwd")
    d_up_t = _matmul_tn_acc(du, h, f"{tag}_up_dw").reshape(2 * FFN_DIM, -1)
    token = emit([d_up_t, d_down])
    dh = _ffn_up_dx(du, w["up_t"], f"{tag}_up_dx")
    dx, d_w, d_sh, *dy_below = _modnorm_bwd(x, dh, norm_g, _behind(sc, token), dx_new, below, f"{tag}_norm_bwd")
    return (dx, *dy_below), dict(dw_w=d_dw_w.transpose(1, 0, 2).reshape(FFN_CONV_WIDTH, 2 * FFN_DIM),
                    dw_b=d_dw_b.reshape(1, 2 * FFN_DIM), norm_g=d_w * (1.0 + sc), sh=d_sh, sc=d_w * norm_g, gate=d_gate)


def _mixer_forward(x, mod, norm_g, w, rope, tag):
    sh, sc, gate = mod
    h = _modnorm(x, norm_g, sc, sh, f"{tag}_norm")
    z = _matmul(h, w["w_in_t"], "nt", BF16, f"{tag}_in")
    ya = _gmlp_fwd(z, w["gain"], w["wtril"], w["bias_exp"], f"{tag}_gmlp")
    q, k, v = _qk_prep(z, rope[0], rope[1], w["gq"], w["gk"], w["seg"], f"{tag}_qk")
    outs, lses = zip(*[_attn_fwd(q[b], k[b], v[b], dil, f"{tag}_attn_d{dil}") for b, dil in enumerate(DILATIONS)])
    yb, lse, cat = _attn_merge(outs, lses, ya, f"{tag}_merge")
    y, x_new = _matmul(cat, w["w_out"], "nn", BF16, f"{tag}_out", resid=(x, gate))
    return x_new, (x, h, z, q, k, v, yb, lse, cat, y)


def _mixer_backward(dx_new, dy, d_gate, saved, mod, norm_g, w, rope, tag, emit, below):
    x, h, z, q, k, v, yb, lse, cat, _ = saved
    _, sc, _ = mod
    d_w_out = _matmul_tn_acc(cat, dy, f"{tag}_out_dw")
    dcat = _matmul(dy, w["w_out"], "nt", BF16, f"{tag}_out_dx")
    dz_a, d_sp_w, d_gain, d_bias_exp = _gmlp_bwd(z, dcat, w["gain"], w["wtril"], w["wtril_t"], w["bias_exp"], f"{tag}_gmlp_bwd")
    dyb = _subseq_views(dcat, A_WIDTH // B_WIDTH, f"{tag}_dyb_views")
    dqs, dks, dvs = zip(*[_attn_bwd(q[b], k[b], v[b], dyb[b], yb[b], lse[b], dil, f"{tag}_attn_bwd_d{dil}")
                          for b, dil in enumerate(DILATIONS)])
    dz_qkv, d_gq, d_gk = _qk_prep_bwd(z, dqs, dks, dvs, rope[0], rope[1], w["gq"], w["gk"], w["seg"], f"{tag}_qk_bwd")
    dz = jnp.concatenate([dz_a, dz_qkv], axis=1)
    d_w_in_t = _matmul_tn_acc(dz, h, f"{tag}_in_dw")
    token = emit([d_w_in_t, d_w_out])
    dh = _matmul(dz, w["w_in_t"], "nn", F32, f"{tag}_in_dx")
    dx, d_w, d_sh, *dy_below = _modnorm_bwd(x, dh, norm_g, _behind(sc, token), dx_new, below, f"{tag}_norm_bwd")
    return (dx, *dy_below), dict(
        vnorm_g=d_gain.reshape(A_GROUPS, GROUP_DIM), spatial_w=d_sp_w,
        spatial_b=d_bias_exp.reshape(CHUNK, A_GROUPS, GROUP_DIM).sum(-1).T,
        q_norm_g=d_gq.reshape(HEADS, HEAD_DIM).sum(0), k_norm_g=d_gk.reshape(HEADS, HEAD_DIM).sum(0),
        norm_g=d_w * (1.0 + sc), sh=d_sh, sc=d_w * norm_g, gate=d_gate)


def _conformer_forward(x, mod, norm_g, w, tag):
    sh, sc, gate = mod
    h = _modnorm(x, norm_g, sc, sh, f"{tag}_norm")
    p = _matmul(h, w["pw1_t"], "nt", BF16, f"{tag}_pw1", bias=w["pw1_b"])
    s, dc = _conformer_mid(p, w["dw_w"], w["dw_b"], w["ln_g"], w["ln_b"], f"{tag}_mid")
    y, x_new = _matmul(s, w["pw2"], "nn", BF16, f"{tag}_pw2", bias=w["pw2_b"], resid=(x, gate))
    return x_new, (x, h, p, dc, s, y)


def _conformer_backward(dx_new, dy, d_gate, saved, mod, norm_g, w, tag, emit, below):
    x, h, p, dc, s, _ = saved
    _, sc, _ = mod
    d_pw2 = _matmul_tn_acc(s, dy, f"{tag}_pw2_dw")
    d_pw2_b = _colsum_call(dy, f"{tag}_pw2_db")
    ds = _matmul(dy, w["pw2"], "nt", BF16, f"{tag}_pw2_dx")
    ddc, d_dw_w, d_dw_b, d_ln_g, d_ln_b = _conformer_mid_bwd(p, dc, ds, w["ln_g"], w["ln_b"], f"{tag}_mid_bwd")
    dp, d_pw1_b = _conformer_glu_bwd(p, ddc, w["dw_w"], f"{tag}_glu_bwd")
    d_pw1_t = _matmul_tn_acc(dp, h, f"{tag}_pw1_dw")
    token = emit([d_pw1_t, d_pw2])
    dh = _matmul(dp, w["pw1_t"], "nn", F32, f"{tag}_pw1_dx")
    dx, d_w, d_sh, *dy_below = _modnorm_bwd(x, dh, norm_g, _behind(sc, token), dx_new, below, f"{tag}_norm_bwd")
    return (dx, *dy_below), dict(pw1_b=d_pw1_b, dw_w=d_dw_w, dw_b=d_dw_b, ln_g=d_ln_g, ln_b=d_ln_b, pw2_b=d_pw2_b, norm_g=d_w * (1.0 + sc), sh=d_sh, sc=d_w * norm_g, gate=d_gate)


def _local_step(x, target, pos, mod, norm_mix_g, norm_ffn_g, mixer_w, conv_w, ffn_w, fetch, emit):
    d = D_MODEL
    inv_freq = 1.0 / (ROPE_THETA ** (jnp.arange(0, HEAD_DIM, 2, dtype=F32) / HEAD_DIM))
    inv_freq = jnp.tile(inv_freq, 2 * HEADS)[None, :]
    sign = jnp.tile(jnp.concatenate([-jnp.ones(HEAD_DIM // 2, F32), jnp.ones(HEAD_DIM // 2, F32)]), HEADS)[None, :]
    rope = _rope_tables(pos, inv_freq, sign, "rope_tables")
    mods = [[mod[l:l + 1, i * d:(i + 1) * d] for i in range(6)] for l in range(2)]
    mix = [(m[0], m[1], m[2]) for m in mods]
    ffn = [(m[3], m[4], m[5]) for m in mods]
    gm = [norm_mix_g[l:l + 1] for l in range(2)]
    gf = [norm_ffn_g[l:l + 1] for l in range(2)]

    mixer_w = {**mixer_w, **fetch("l0_mix", x)}
    x1, s_mix = _mixer_forward(x, mix[0], gm[0], mixer_w, rope, "l0_mix")
    ffn_w0 = {**ffn_w[0], **fetch("l0_ffn", x1)}
    x2, s_ffn0 = _ffn_forward(x1, ffn[0], gf[0], ffn_w0, "l0_ffn")
    conv_w = {**conv_w, **fetch("l1_conv", x2)}
    x3, s_conv = _conformer_forward(x2, mix[1], gm[1], conv_w, "l1_conv")
    ffn_w1 = {**ffn_w[1], **fetch("l1_ffn", x3)}
    x4, s_ffn1 = _ffn_forward(x3, ffn[1], gf[1], ffn_w1, "l1_ffn")
    below = lambda saved, m: (saved[-1], m[2])
    dx, loss, dy, dg = _loss_head(x4, target, below(s_ffn1, ffn[1]), "loss_head")
    (dx, dy, dg), g_ffn1 = _ffn_backward(dx, dy, dg, s_ffn1, ffn[1], gf[1], ffn_w1, "l1_ffn",
                                         functools.partial(emit, "l1_ffn"), below(s_conv, mix[1]))
    (dx, dy, dg), g_conv = _conformer_backward(dx, dy, dg, s_conv, mix[1], gm[1], conv_w, "l1_conv",
                                               functools.partial(emit, "l1_conv"), below(s_ffn0, ffn[0]))
    (dx, dy, dg), g_ffn0 = _ffn_backward(dx, dy, dg, s_ffn0, ffn[0], gf[0], ffn_w0, "l0_ffn",
                                         functools.partial(emit, "l0_ffn"), below(s_mix, mix[0]))
    (dx,), g_mix = _mixer_backward(dx, dy, dg, s_mix, mix[0], gm[0], mixer_w, rope, "l0_mix",
                                   functools.partial(emit, "l0_mix"), None)
    blocks = [g_mix, g_ffn0, g_conv, g_ffn1]
    dmod = jnp.stack([jnp.concatenate([a["sh"], a["sc"], a["gate"], b["sh"], b["sc"], b["gate"]], axis=1)[0]
                      for a, b in ((g_mix, g_ffn0), (g_conv, g_ffn1))])
    return loss, dx, dmod, blocks


def _pack(arrs, rows=8):
    flat = jnp.concatenate([a.reshape(-1).astype(F32) for a in arrs])
    n = flat.shape[0]
    cols = -(-n // (rows * 128)) * 128
    return jnp.pad(flat, (0, rows * cols - n)).reshape(rows, cols)


def _unpack(flat, shapes):
    out, off = [], 0
    for shp in shapes:
        n = math.prod(shp)
        out.append(flat[..., off:off + n].reshape(flat.shape[:-1] + tuple(shp)))
        off += n
    return out


def _take_block(a, idx, size, axis):
    return lax.dynamic_slice_in_dim(a, idx * size, size, axis)


def kernel(x, c, positions, ada_w, ada_b, norm_mix_g, norm_ffn_g, ab_w_in, a_vnorm_g, a_spatial_w, a_spatial_b, b_q_norm_g, b_k_norm_g, ab_w_out, conv_pw1_w, conv_pw1_b, conv_dw_w, conv_dw_b, conv_ln_g, conv_ln_b, conv_pw2_w, conv_pw2_b, ffn_up_w, ffn_dw_w, ffn_dw_b, ffn_down_w, loss_target, m_ada_w, m_ada_b, m_norm_mix_g, m_norm_ffn_g, m_ab_w_in, m_a_vnorm_g, m_a_spatial_w, m_a_spatial_b, m_b_q_norm_g, m_b_k_norm_g, m_ab_w_out, m_conv_pw1_w, m_conv_pw1_b, m_conv_dw_w, m_conv_dw_b, m_conv_ln_g, m_conv_ln_b, m_conv_pw2_w, m_conv_pw2_b, m_ffn_up_w, m_ffn_dw_w, m_ffn_dw_b, m_ffn_down_w, v_ada_w, v_ada_b, v_norm_mix_g, v_norm_ffn_g, v_ab_w_in, v_a_vnorm_g, v_a_spatial_w, v_a_spatial_b, v_b_q_norm_g, v_b_k_norm_g, v_ab_w_out, v_conv_pw1_w, v_conv_pw1_b, v_conv_dw_w, v_conv_dw_b, v_conv_ln_g, v_conv_ln_b, v_conv_pw2_w, v_conv_pw2_b, v_ffn_up_w, v_ffn_dw_w, v_ffn_dw_b, v_ffn_down_w):
    weights = dict(ada_w=ada_w, ada_b=ada_b, norm_mix_g=norm_mix_g, norm_ffn_g=norm_ffn_g, ab_w_in=ab_w_in, a_vnorm_g=a_vnorm_g, a_spatial_w=a_spatial_w, a_spatial_b=a_spatial_b, b_q_norm_g=b_q_norm_g, b_k_norm_g=b_k_norm_g, ab_w_out=ab_w_out, conv_pw1_w=conv_pw1_w, conv_pw1_b=conv_pw1_b, conv_dw_w=conv_dw_w, conv_dw_b=conv_dw_b, conv_ln_g=conv_ln_g, conv_ln_b=conv_ln_b, conv_pw2_w=conv_pw2_w, conv_pw2_b=conv_pw2_b, ffn_up_w=ffn_up_w, ffn_dw_w=ffn_dw_w, ffn_dw_b=ffn_dw_b, ffn_down_w=ffn_down_w)
    mom1 = dict(ada_w=m_ada_w, ada_b=m_ada_b, norm_mix_g=m_norm_mix_g, norm_ffn_g=m_norm_ffn_g, ab_w_in=m_ab_w_in, a_vnorm_g=m_a_vnorm_g, a_spatial_w=m_a_spatial_w, a_spatial_b=m_a_spatial_b, b_q_norm_g=m_b_q_norm_g, b_k_norm_g=m_b_k_norm_g, ab_w_out=m_ab_w_out, conv_pw1_w=m_conv_pw1_w, conv_pw1_b=m_conv_pw1_b, conv_dw_w=m_conv_dw_w, conv_dw_b=m_conv_dw_b, conv_ln_g=m_conv_ln_g, conv_ln_b=m_conv_ln_b, conv_pw2_w=m_conv_pw2_w, conv_pw2_b=m_conv_pw2_b, ffn_up_w=m_ffn_up_w, ffn_dw_w=m_ffn_dw_w, ffn_dw_b=m_ffn_dw_b, ffn_down_w=m_ffn_down_w)
    mom2 = dict(ada_w=v_ada_w, ada_b=v_ada_b, norm_mix_g=v_norm_mix_g, norm_ffn_g=v_norm_ffn_g, ab_w_in=v_ab_w_in, a_vnorm_g=v_a_vnorm_g, a_spatial_w=v_a_spatial_w, a_spatial_b=v_a_spatial_b, b_q_norm_g=v_b_q_norm_g, b_k_norm_g=v_b_k_norm_g, ab_w_out=v_ab_w_out, conv_pw1_w=v_conv_pw1_w, conv_pw1_b=v_conv_pw1_b, conv_dw_w=v_conv_dw_w, conv_dw_b=v_conv_dw_b, conv_ln_g=v_conv_ln_g, conv_ln_b=v_conv_ln_b, conv_pw2_w=v_conv_pw2_w, conv_pw2_b=v_conv_pw2_b, ffn_up_w=v_ffn_up_w, ffn_dw_w=v_ffn_dw_w, ffn_dw_b=v_ffn_dw_b, ffn_down_w=v_ffn_down_w)
    order = list(weights)
    d, f2 = D_MODEL, 2 * FFN_DIM
    t = x.shape[1]
    me = 4 * lax.axis_index("x") + 2 * lax.axis_index("y") + lax.axis_index("c")
    for window, dil in PATTERNS:
        assert window // dil == Q_BLOCK and t % (dil * Q_BLOCK) == 0

    small_in = [c[0], conv_pw1_b[0], conv_dw_w[0], conv_dw_b[0], conv_ln_g[0], conv_ln_b[0], conv_pw2_b[0], ffn_dw_w]
    g1 = _all_gather_vmem(_pack(small_in, rows=8), "gather_small").reshape(N_DEV, -1)
    c_all, pw1_b, dw_w, dw_b, ln_g, ln_b, pw2_b, fdw_w = _unpack(g1, [a.shape for a in small_in])
    pw1_b, dw_b, ln_g, ln_b, pw2_b = [a.reshape(1, -1) for a in (pw1_b, dw_b, ln_g, ln_b, pw2_b)]
    dw_w = dw_w.transpose(1, 0, 2).reshape(CONV_WIDTH, d)
    fdw_w = fdw_w.transpose(1, 2, 0, 3).reshape(2, FFN_CONV_WIDTH, f2)

    c16 = jnp.pad(c_all, ((0, 2 * N_DEV - c_all.shape[0]), (0, 0)))
    part = jnp.concatenate([_ada_fwd(c16, ada_w[l], f"ada_fwd{l}")[:N_DEV] for l in range(2)], axis=1)
    g2 = _all_gather_vmem(part, "gather_mod").reshape(N_DEV, N_DEV, 2, -1)
    mod = lax.dynamic_index_in_dim(g2, me, axis=1, keepdims=False).transpose(1, 0, 2).reshape(2, 6 * d) + ada_b

    stages = dict(l0_mix=[ab_w_in[0].T, ab_w_out[0]], l0_ffn=[ffn_up_w[0].T, ffn_down_w[0]],
                  l1_conv=[conv_pw1_w[0].T, conv_pw2_w[0]], l1_ffn=[ffn_up_w[1].T, ffn_down_w[1]])
    stages = {k: [s.astype(BF16) for s in v] for k, v in stages.items()}
    names = dict(l0_mix=("w_in_t", "w_out"), l0_ffn=("up_t", "down"), l1_conv=("pw1_t", "pw2"), l1_ffn=("up_t", "down"))
    ready = {"l0_mix": [a.reshape(-1, d) for a in _all_gather_hbm(stages["l0_mix"], "gather_mixer_weights")]}
    behind = (ready, mod)
    arriving = {}
    for stage, group in (("l0_ffn", ("l0_ffn",)), ("l1_conv", ("l1_conv", "l1_ffn"))):
        srcs, _ = lax.optimization_barrier(([s for g in group for s in stages[g]], behind))
        arriving[stage], behind = _pushes_start(
            srcs, _landing_zones(srcs, f"gather_{stage}_zones"), False, f"gather_{stage}_start")
        mod = mod + behind[0:1, 0:1]

    def fetch(stage, after):
        if stage in arriving:
            full = [a.reshape(-1, d) for a in _pushes_wait(arriving[stage], after, f"gather_{stage}_wait")[1]]
            ready[stage] = full[:2]
            if stage == "l1_conv":
                ready["l1_ffn"] = full[2:]
        return dict(zip(names[stage], ready[stage]))

    causal = jnp.tril(jnp.ones((CHUNK, CHUNK), bool))
    wtril = jnp.where(causal[None], a_spatial_w[0], 0.0)
    mixer_w = dict(
        gain=a_vnorm_g[0].reshape(1, A_WIDTH), wtril=wtril.astype(BF16),
        wtril_t=wtril.transpose(0, 2, 1).astype(BF16),
        bias_exp=jnp.repeat(a_spatial_b[0].T, GROUP_DIM, axis=1),
        gq=jnp.tile(b_q_norm_g[0], HEADS)[None, :], gk=jnp.tile(b_k_norm_g[0], HEADS)[None, :],
        seg=jnp.kron(jnp.eye(HEADS, dtype=BF16), jnp.ones((HEAD_DIM, HEAD_DIM), BF16)))
    conv_w = dict(pw1_b=pw1_b, dw_w=dw_w, dw_b=dw_b, ln_g=ln_g, ln_b=ln_b, pw2_b=pw2_b)
    ffn_w = [dict(dw_w=fdw_w[l].reshape(FFN_CONV_WIDTH, 2, FFN_DIM).transpose(1, 0, 2), dw_b=ffn_dw_b[l].reshape(2, 1, FFN_DIM))
             for l in range(2)]

    leaving = {}

    def emit(stage, grads):
        blocks = [g.reshape(N_DEV, g.shape[0] // N_DEV, d) for g in grads]
        leaving[stage], token = _pushes_start(
            blocks, [lax.empty(b.shape, b.dtype) for b in blocks], True, f"reduce_{stage}_start")
        return token

    loss, dx, dmod, (g_mix, g_ffn0, g_conv, g_ffn1) = _local_step(
        x[0], loss_target[0], positions[0].astype(F32)[:, None], mod, norm_mix_g, norm_ffn_g, mixer_w, conv_w, ffn_w,
        fetch, emit)

    me_op = me.astype(jnp.int32).reshape(1)

    def reduced(stage, after):
        blocks, lands = _pushes_wait(leaving[stage], after, f"reduce_{stage}_wait")
        return [_sum_with_own(b, a, me_op, f"reduce_{stage}_sum{i}") for i, (b, a) in enumerate(zip(blocks, lands))]

    (r_up_t1, r_down1), (r_pw1_t, r_pw2), (r_up_t0, r_down0) = [reduced(s, dx) for s in ("l1_ffn", "l1_conv", "l0_ffn")]

    small_g = [
        dmod, jnp.concatenate([g_mix["norm_g"], g_conv["norm_g"]]), jnp.concatenate([g_ffn0["norm_g"], g_ffn1["norm_g"]]),
        g_mix["vnorm_g"], g_mix["spatial_w"], g_mix["spatial_b"], g_mix["q_norm_g"], g_mix["k_norm_g"],
        g_conv["pw1_b"], g_conv["dw_w"], g_conv["dw_b"], g_conv["ln_g"], g_conv["ln_b"], g_conv["pw2_b"],
        jnp.stack([g_ffn0["dw_w"], g_ffn1["dw_w"]]), jnp.concatenate([g_ffn0["dw_b"], g_ffn1["dw_b"]])]
    packed = _pack(small_g, rows=8)
    g3 = _all_gather_vmem(packed, "gather_small_grads").reshape(N_DEV, 8, -1)
    total = _unpack(_sum_slots(g3, "sum_small_grads").reshape(-1), [a.shape for a in small_g])
    (s_dmod, s_mix_g, s_ffn_g, s_vnorm, s_sp_w, s_sp_b, s_gq, s_gk, s_pw1_b, s_dw_w, s_dw_b, s_ln_g, s_ln_b,
     s_pw2_b, s_fdw_w, s_fdw_b) = total
    dmod_all = g3.reshape(N_DEV, -1)[:, :2 * 6 * d].reshape(N_DEV, 2, 6 * d)
    n_ada = ada_w.shape[2]
    dmod16 = jnp.pad(_take_block(dmod_all, me, n_ada, 2), ((0, N_DEV), (0, 0), (0, 0)))
    g_ada_w = jnp.stack([_ada_bwd(c16, dmod16[:, l], f"ada_bwd{l}") for l in range(2)])

    grads = dict(
        ada_w=g_ada_w, ada_b=s_dmod, norm_mix_g=s_mix_g, norm_ffn_g=s_ffn_g,
        a_vnorm_g=s_vnorm[None], a_spatial_w=s_sp_w[None], a_spatial_b=s_sp_b[None], b_q_norm_g=s_gq[None],
        b_k_norm_g=s_gk[None],
        conv_pw1_b=_take_block(s_pw1_b, me, conv_pw1_b.shape[1], 1),
        conv_dw_w=_take_block(s_dw_w, me, conv_dw_w.shape[2], 1)[None],
        conv_dw_b=_take_block(s_dw_b, me, conv_dw_b.shape[1], 1), conv_ln_g=_take_block(s_ln_g, me, conv_ln_g.shape[1], 1),
        conv_ln_b=_take_block(s_ln_b, me, conv_ln_b.shape[1], 1), conv_pw2_w=r_pw2[None],
        conv_pw2_b=_take_block(s_pw2_b, me, conv_pw2_b.shape[1], 1),
        ffn_dw_w=_take_block(s_fdw_w, me, ffn_dw_w.shape[2], 2), ffn_dw_b=s_fdw_b, ffn_down_w=jnp.stack([r_down0, r_down1]))
    grads_t = dict(conv_pw1_w=r_pw1_t[None], ffn_up_w=jnp.stack([r_up_t0, r_up_t1]))

    large = ("ada_w", "conv_pw1_w", "conv_pw2_w", "ffn_up_w", "ffn_down_w", "ab_w_in", "ab_w_out")
    flip = lambda a: jnp.swapaxes(a, 1, 2)
    delta, new_m, new_v = {}, {}, {}
    for name in large:
        if name == "ab_w_in":
            r_in_t, r_out = reduced("l0_mix", new_v["ffn_down_w"])
            grads_t["ab_w_in"] = r_in_t[None]
            grads["ab_w_out"] = r_out[None]
        if name in grads_t:
            grads[name] = flip(grads_t[name])
            res = _adamw(flip(weights[name]), grads_t[name], flip(mom1[name]), flip(mom2[name]), f"adamw_{name}")
            delta[name], new_m[name], new_v[name] = [flip(r) for r in res]
        else:
            delta[name], new_m[name], new_v[name] = _adamw(weights[name], grads[name], mom1[name], mom2[name], f"adamw_{name}")
    small = [n for n in order if n not in large]
    res = _adamw_small(*[[src[n] for n in small] for src in (weights, grads, mom1, mom2)], "adamw_small")
    for dst, arrs in zip((delta, new_m, new_v), res):
        dst.update(zip(small, arrs))

    loss = lax.psum(loss[0, 0], ("x", "y", "c"))
    return (loss, dx[None], *[grads[n] for n in order], *[delta[n] for n in order],
            *[new_m[n] for n in order], *[new_v[n] for n in order])
```

```python
import functools
import math

import jax
import jax.numpy as jnp
from jax import lax
from jax.experimental import pallas as pl
from jax.experimental.pallas import tpu as pltpu

F32 = jnp.float32
BF16 = jnp.bfloat16
MESH = pl.DeviceIdType.MESH

D_MODEL = 1024
A_WIDTH = 512
A_GROUPS = 4
GROUP_DIM = 128
CHUNK = 128
B_WIDTH = 512
HEADS = 8
HEAD_DIM = 64
PATTERNS = ((128, 1), (512, 4), (2048, 16))
Q_BLOCK = 128
ROPE_THETA = 10000.0
AB_IN = 2560
CONV_WIDTH = 31
FFN_DIM = 2816
FFN_CONV_WIDTH = 3
EPS = 1e-6
NEG = -1e30
N_DEV = 8
ADAM_LR, ADAM_B1, ADAM_B2, ADAM_EPS, ADAM_WD, ADAM_STEP = 0.001, 0.9, 0.999, 1e-08, 0.01, 10

V7X_VMEM_LIMIT = 56 * 2**20
FFN_HALO = 16
CONV_HALO = 32

_NN = (((1,), (0,)), ((), ()))
_NT = (((1,), (1,)), ((), ()))
_TN = (((0,), (0,)), ((), ()))


def _tile(n, prefs=(512, 256, 128)):
    for t in prefs:
        if n % t == 0:
            return t
    return n


def _row_tile(n, cap=512):
    best = n
    for t in range(8, min(n, cap) + 1, 8):
        if n % t == 0:
            best = t
    return best if best <= cap else n


def _params(*sem):
    return pltpu.CompilerParams(dimension_semantics=sem, vmem_limit_bytes=V7X_VMEM_LIMIT)


def _dot(a, b, dims):
    return lax.dot_general(a, b, dims, preferred_element_type=F32)


def _sigmoid(x):
    return 1.0 / (1.0 + jnp.exp(-x))


def _gelu(x):
    return 0.5 * x * (1.0 + lax.erf(x * (2.0 ** -0.5)))


def _gelu_grad(x):
    return 0.5 * (1.0 + lax.erf(x * (2.0 ** -0.5))) + x * jnp.exp(-0.5 * x * x) * (1.0 / math.sqrt(2.0 * math.pi))


def _colsum(v):
    return jnp.sum(v, axis=0, keepdims=True)


MATMUL_VMEM_BUDGET = 40 * 2**20


def _matmul_tiles(m, n, k, out_bytes, with_resid):
    def options(dim):
        opts = [t for t in (1024, 512, 256, 128) if dim % t == 0]
        return opts + [dim] if dim <= 4096 and dim not in opts else opts

    best = None
    for tm in options(m):
        for tn in options(n):
            need = 4 * (tm * k + k * tn) + tm * tn * (4 + 2 * out_bytes) + (24 * tm * tn if with_resid else 0)
            if need <= MATMUL_VMEM_BUDGET and (best is None or tm * tn / (tm + tn) > best[0]):
                best = (tm * tn / (tm + tn), tm, tn)
    return best[1], best[2]


def _matmul_tn_acc(a, b, name, tk=1024):
    squeeze = a.ndim == 2
    a3 = a[None] if squeeze else a
    p_, t, m = a3.shape
    n = b.shape[1]
    nk = t // tk

    def body(a_ref, b_ref, o_ref, acc_ref):
        kt = pl.program_id(1)

        @pl.when(kt == 0)
        def _():
            acc_ref[...] = jnp.zeros_like(acc_ref)

        acc_ref[...] += _dot(a_ref[...], b_ref[...], _TN)

        @pl.when(kt == nk - 1)
        def _():
            o_ref[...] = acc_ref[...].astype(BF16)

    out = pl.pallas_call(
        body, name=name, grid=(p_, nk),
        in_specs=[pl.BlockSpec((None, tk, m), lambda p, kt: (p, kt, 0)), pl.BlockSpec((tk, n), lambda p, kt: (kt, 0))],
        out_specs=pl.BlockSpec((None, m, n), lambda p, kt: (p, 0, 0)), out_shape=jax.ShapeDtypeStruct((p_, m, n), BF16),
        scratch_shapes=[pltpu.VMEM((m, n), F32)], compiler_params=_params("parallel", "arbitrary"),
    )(a3, b)
    return out[0] if squeeze else out


def _matmul(a, b, mode, out_dtype, name, bias=None, resid=None):
    if mode == "nn":
        (m, k), (_, n) = a.shape, b.shape
    elif mode == "nt":
        (m, k), (n, _) = a.shape, b.shape
    else:
        (k, m), (_, n) = a.shape, b.shape
    tm, tn = _matmul_tiles(m, n, k, jnp.dtype(out_dtype).itemsize, resid is not None)
    dims = {"nn": _NN, "nt": _NT, "tn": _TN}[mode]
    a_spec = pl.BlockSpec((k, tm), lambda i, j: (0, i)) if mode == "tn" else pl.BlockSpec((tm, k), lambda i, j: (i, 0))
    b_spec = pl.BlockSpec((tn, k), lambda i, j: (j, 0)) if mode == "nt" else pl.BlockSpec((k, tn), lambda i, j: (0, j))
    in_specs, args = [a_spec, b_spec], [a, b]
    row_spec = pl.BlockSpec((1, tn), lambda i, j: (0, j))
    tile_spec = pl.BlockSpec((tm, tn), lambda i, j: (i, j))
    if bias is not None:
        in_specs.append(row_spec)
        args.append(bias)
    if resid is not None:
        in_specs += [tile_spec, row_spec]
        args += list(resid)
    out_shape = [jax.ShapeDtypeStruct((m, n), out_dtype)]
    out_specs = [tile_spec]
    if resid is not None:
        out_shape.append(jax.ShapeDtypeStruct((m, n), F32))
        out_specs.append(tile_spec)

    def body(*refs):
        a_ref, b_ref = refs[0], refs[1]
        pos = 2
        acc = _dot(a_ref[...], b_ref[...], dims)
        if bias is not None:
            acc = acc + refs[pos][...]
            pos += 1
        if resid is not None:
            x_ref, g_ref = refs[pos], refs[pos + 1]
            pos += 2
        refs[pos][...] = acc.astype(out_dtype)
        if resid is not None:
            refs[pos + 1][...] = x_ref[...] + g_ref[...] * acc

    outs = pl.pallas_call(
        body, name=name, grid=(m // tm, n // tn), in_specs=in_specs, out_specs=out_specs, out_shape=out_shape,
        compiler_params=_params("parallel", "parallel"),
    )(*args)
    return outs if resid is not None else outs[0]


def _modnorm(x, g, sc, sh, name):
    t, d = x.shape
    tm = _tile(t)
    row = pl.BlockSpec((1, d), lambda i: (0, 0))
    blk = pl.BlockSpec((tm, d), lambda i: (i, 0))

    def body(x_ref, g_ref, sc_ref, sh_ref, o_ref):
        x = x_ref[...]
        r = lax.rsqrt(jnp.mean(x * x, axis=-1, keepdims=True) + EPS)
        o_ref[...] = ((x * r) * g_ref[...] * (1.0 + sc_ref[...]) + sh_ref[...]).astype(BF16)

    return pl.pallas_call(
        body, name=name, grid=(t // tm,), in_specs=[blk, row, row, row], out_specs=blk,
        out_shape=jax.ShapeDtypeStruct((t, d), BF16), compiler_params=_params("parallel"),
    )(x, g, sc, sh)


def _gate_bwd_tile(dx, y_ref, gate_ref, dy_ref, dgate_ref, first):
    @pl.when(first)
    def _():
        dgate_ref[...] = jnp.zeros_like(dgate_ref)

    dy_ref[...] = (dx * gate_ref[...]).astype(BF16)
    dgate_ref[...] += _colsum(dx * y_ref[...].astype(F32))


def _modnorm_bwd(x, dh, g, sc, dres, below, name):
    t, d = x.shape
    tm = _tile(t)
    row = pl.BlockSpec((1, d), lambda i: (0, 0))
    blk = pl.BlockSpec((tm, d), lambda i: (i, 0))

    def body(x_ref, dh_ref, g_ref, sc_ref, dres_ref, *rest):
        dx_ref, dw_ref, dsh_ref = rest[-5:-2] if below else rest
        first = pl.program_id(0) == 0

        @pl.when(first)
        def _():
            dw_ref[...] = jnp.zeros_like(dw_ref)
            dsh_ref[...] = jnp.zeros_like(dsh_ref)

        x = x_ref[...]
        dh = dh_ref[...].astype(F32)
        r = lax.rsqrt(jnp.mean(x * x, axis=-1, keepdims=True) + EPS)
        xn = x * r
        dxn = dh * (g_ref[...] * (1.0 + sc_ref[...]))
        dx = dres_ref[...] + r * (dxn - xn * jnp.mean(dxn * xn, axis=-1, keepdims=True))
        dx_ref[...] = dx
        dw_ref[...] += _colsum(dh * xn)
        dsh_ref[...] += _colsum(dh)
        if below:
            _gate_bwd_tile(dx, rest[0], rest[1], rest[-2], rest[-1], first)

    row_out = jax.ShapeDtypeStruct((1, d), F32)
    return pl.pallas_call(
        body, name=name, grid=(t // tm,), in_specs=[blk, blk, row, row, blk] + ([blk, row] if below else []),
        out_specs=[blk, row, row] + ([blk, row] if below else []),
        out_shape=[jax.ShapeDtypeStruct((t, d), F32), row_out, row_out]
        + ([jax.ShapeDtypeStruct((t, d), BF16), row_out] if below else []),
        compiler_params=_params("arbitrary"),
    )(x, dh, g, sc, dres, *(below or ()))


def _loss_head(y, target, below, name):
    t, d = y.shape
    tm = _tile(t)
    blk = pl.BlockSpec((tm, d), lambda i: (i, 0))
    row = pl.BlockSpec((1, d), lambda i: (0, 0))
    one = pl.BlockSpec((1, 1), lambda i: (0, 0))
    steps = t // tm

    def body(y_ref, t_ref, yb_ref, gate_ref, dx_ref, loss_ref, dy_ref, dgate_ref, acc_ref):
        first = pl.program_id(0) == 0

        @pl.when(first)
        def _():
            acc_ref[...] = jnp.zeros_like(acc_ref)

        e = y_ref[...] - t_ref[...]
        dx = e * (1.0 / d)
        dx_ref[...] = dx
        acc_ref[...] += _colsum(e * e)
        _gate_bwd_tile(dx, yb_ref, gate_ref, dy_ref, dgate_ref, first)

        @pl.when(pl.program_id(0) == steps - 1)
        def _():
            loss_ref[...] = jnp.sum(acc_ref[...], axis=1, keepdims=True) * (0.5 / d)

    return pl.pallas_call(
        body, name=name, grid=(steps,), in_specs=[blk, blk, blk, row], out_specs=[blk, one, blk, row],
        out_shape=[jax.ShapeDtypeStruct((t, d), F32), jax.ShapeDtypeStruct((1, 1), F32),
                   jax.ShapeDtypeStruct((t, d), BF16), jax.ShapeDtypeStruct((1, d), F32)],
        scratch_shapes=[pltpu.VMEM((1, d), F32)], compiler_params=_params("arbitrary"),
    )(y, target, *below)


def _group_norm(vg, gain):
    mu = jnp.mean(vg, axis=-1, keepdims=True)
    xc = vg - mu
    rstd = lax.rsqrt(jnp.mean(xc * xc, axis=-1, keepdims=True) + EPS)
    xhat = xc * rstd
    return xhat, rstd, xhat * gain


def _gmlp_fwd(z, gain, wtril, bias_exp, name):
    t = z.shape[0]
    zu = pl.BlockSpec((CHUNK, A_WIDTH), lambda i: (i, 0))
    zv = pl.BlockSpec((CHUNK, A_WIDTH), lambda i: (i, 1))
    full2 = lambda shp: pl.BlockSpec(shp, lambda i: (0, 0))
    w_spec = pl.BlockSpec((A_GROUPS, CHUNK, CHUNK), lambda i: (0, 0, 0))

    def body(zu_ref, zv_ref, gain_ref, w_ref, b_ref, ya_ref):
        ua = _gelu(zu_ref[...].astype(F32))
        vg = _gelu(zv_ref[...].astype(F32))
        for g in range(A_GROUPS):
            sl = slice(g * GROUP_DIM, (g + 1) * GROUP_DIM)
            _, _, vn = _group_norm(vg[:, sl], gain_ref[:, sl])
            f = _dot(w_ref[g], vn.astype(BF16), _NN) + b_ref[:, sl]
            ya_ref[:, sl] = (ua[:, sl] * f).astype(BF16)

    return pl.pallas_call(
        body, name=name, grid=(t // CHUNK,),
        in_specs=[zu, zv, full2((1, A_WIDTH)), w_spec, full2((CHUNK, A_WIDTH))], out_specs=zu,
        out_shape=jax.ShapeDtypeStruct((t, A_WIDTH + B_WIDTH), BF16), compiler_params=_params("parallel"),
    )(z, z, gain, wtril, bias_exp)


def _gmlp_bwd(z, dcat, gain, wtril, wtril_t, bias_exp, name):
    t = z.shape[0]
    zu = pl.BlockSpec((CHUNK, A_WIDTH), lambda i: (i, 0))
    zv = pl.BlockSpec((CHUNK, A_WIDTH), lambda i: (i, 1))
    full2 = lambda shp: pl.BlockSpec(shp, lambda i: (0, 0))
    w_spec = pl.BlockSpec((A_GROUPS, CHUNK, CHUNK), lambda i: (0, 0, 0))
    dz_spec = pl.BlockSpec((CHUNK, 2 * A_WIDTH), lambda i: (i, 0))

    def body(zu_ref, zv_ref, dya_ref, gain_ref, w_ref, wt_ref, b_ref, dz_ref, dw_ref, dgain_ref, dbias_ref):
        @pl.when(pl.program_id(0) == 0)
        def _():
            dw_ref[...] = jnp.zeros_like(dw_ref)
            dgain_ref[...] = jnp.zeros_like(dgain_ref)
            dbias_ref[...] = jnp.zeros_like(dbias_ref)

        zu_v = zu_ref[...].astype(F32)
        zv_v = zv_ref[...].astype(F32)
        dya = dya_ref[...].astype(F32)
        ua = _gelu(zu_v)
        vg = _gelu(zv_v)
        row = lax.broadcasted_iota(jnp.int32, (CHUNK, CHUNK), 0)
        col = lax.broadcasted_iota(jnp.int32, (CHUNK, CHUNK), 1)
        for g in range(A_GROUPS):
            sl = slice(g * GROUP_DIM, (g + 1) * GROUP_DIM)
            gain_g = gain_ref[:, sl]
            xhat, rstd, vn = _group_norm(vg[:, sl], gain_g)
            vn16 = vn.astype(BF16)
            f = _dot(w_ref[g], vn16, _NN) + b_ref[:, sl]
            df = dya[:, sl] * ua[:, sl]
            df16 = df.astype(BF16)
            dz_ref[:, sl] = (dya[:, sl] * f * _gelu_grad(zu_v[:, sl])).astype(BF16)
            dw_ref[g] += jnp.where(row >= col, _dot(df16, vn16, _NT), 0.0)
            dvn = _dot(wt_ref[g], df16, _NN)
            dgain_ref[:, sl] += _colsum(dvn * xhat)
            dxh = dvn * gain_g
            dvg = rstd * (dxh - jnp.mean(dxh, axis=-1, keepdims=True) - xhat * jnp.mean(dxh * xhat, axis=-1, keepdims=True))
            dz_ref[:, A_WIDTH + g * GROUP_DIM:A_WIDTH + (g + 1) * GROUP_DIM] = (dvg * _gelu_grad(zv_v[:, sl])).astype(BF16)
            dbias_ref[:, sl] += df

    return pl.pallas_call(
        body, name=name, grid=(t // CHUNK,),
        in_specs=[zu, zv, zu, full2((1, A_WIDTH)), w_spec, w_spec, full2((CHUNK, A_WIDTH))],
        out_specs=[dz_spec, w_spec, full2((1, A_WIDTH)), full2((CHUNK, A_WIDTH))],
        out_shape=[jax.ShapeDtypeStruct((t, 2 * A_WIDTH), BF16), jax.ShapeDtypeStruct((A_GROUPS, CHUNK, CHUNK), F32),
                   jax.ShapeDtypeStruct((1, A_WIDTH), F32), jax.ShapeDtypeStruct((CHUNK, A_WIDTH), F32)],
        compiler_params=_params("arbitrary"),
    )(z, z, dcat, gain, wtril, wtril_t, bias_exp)


def _rope_tables(pos, inv_freq, sign, name):
    t = pos.shape[0]
    tm = _tile(t)
    row = pl.BlockSpec((1, B_WIDTH), lambda i: (0, 0))
    blk = pl.BlockSpec((tm, B_WIDTH), lambda i: (i, 0))

    def body(pos_ref, f_ref, s_ref, cos_ref, sin_ref):
        ang = pos_ref[...] * f_ref[:, 0:LANES]
        cos_ref[...] = jnp.tile(jnp.cos(ang), (1, B_WIDTH // LANES))
        sin_ref[...] = jnp.tile(jnp.sin(ang) * s_ref[:, 0:LANES], (1, B_WIDTH // LANES))

    return pl.pallas_call(
        body, name=name, grid=(t // tm,), in_specs=[pl.BlockSpec((tm, 1), lambda i: (i, 0)), row, row],
        out_specs=[blk, blk], out_shape=[jax.ShapeDtypeStruct((t, B_WIDTH), F32)] * 2,
        compiler_params=_params("parallel"),
    )(pos, inv_freq, sign)


def _head_sum(v, seg):
    hi = v.astype(BF16)
    lo = (v - hi.astype(F32)).astype(BF16)
    return _dot(hi, seg, _NN) + _dot(lo, seg, _NN)


def _swap_halves(v):
    lane = lax.broadcasted_iota(jnp.int32, v.shape, 1)
    return jnp.where((lane & (HEAD_DIM - 1)) < HEAD_DIM // 2,pltpu.roll(v, B_WIDTH - HEAD_DIM // 2, 1), pltpu.roll(v, HEAD_DIM // 2, 1))


DILATIONS = tuple(dil for _, dil in PATTERNS)
SUBSEQ_TM = 256
LANES = 128


def _subseq_shape(t, dil):
    return (t // dil, dil * B_WIDTH)


def _subseq_spec(tm, dil):
    return pl.BlockSpec((tm // dil, dil * B_WIDTH), lambda i: (i, 0))


def _to_subseq(x, scr_ref, dil):
    if dil == 1:
        return x
    tm, w = x.shape
    for c in range(w // LANES):
        scr_ref[c * tm:(c + 1) * tm, :] = x[:, c * LANES:(c + 1) * LANES]
    return jnp.concatenate([scr_ref[pl.ds(c * tm + r, tm // dil, stride=dil), :]
                            for r in range(dil) for c in range(w // LANES)], axis=1)


def _from_subseq(y, scr_ref, dil):
    if dil == 1:
        return y
    n, w = y.shape[0], y.shape[1] // dil
    tm = n * dil
    for r in range(dil):
        for c in range(w // LANES):
            scr_ref[pl.ds(c * tm + r, n, stride=dil), :] = y[:, r * w + c * LANES:r * w + (c + 1) * LANES]
    return jnp.concatenate([scr_ref[c * tm:(c + 1) * tm, :] for c in range(w // LANES)], axis=1)


def _subseq_scratch(tm):
    return pltpu.VMEM((B_WIDTH // LANES * tm, LANES), F32)


def _qk_prep(z, cos_t, sin_t, gq, gk, seg, name):
    t = z.shape[0]
    tm = _tile(t, (SUBSEQ_TM,))
    col = lambda c: pl.BlockSpec((tm, B_WIDTH), lambda i: (i, c))
    row = pl.BlockSpec((1, B_WIDTH), lambda i: (0, 0))
    blk = col(0)
    nd = len(DILATIONS)

    def body(q_ref, k_ref, v_ref, cos_ref, sin_ref, gq_ref, gk_ref, seg_ref, *rest):
        out_refs, scr_ref = rest[:-1], rest[-1]

        def norm_rot(x, g):
            r = lax.rsqrt(_head_sum(x * x, seg_ref[...]) * (1.0 / HEAD_DIM) + EPS)
            xn = x * r * g
            return xn * cos_ref[...] + _swap_halves(xn) * sin_ref[...]

        vals = (norm_rot(q_ref[...].astype(F32), gq_ref[...]), norm_rot(k_ref[...].astype(F32), gk_ref[...]),
                v_ref[...].astype(F32))
        for a, val in enumerate(vals):
            for b, dil in enumerate(DILATIONS):
                out_refs[a * nd + b][...] = _to_subseq(val, scr_ref, dil).astype(BF16)

    outs = pl.pallas_call(
        body, name=name, grid=(t // tm,),
        in_specs=[col(2), col(3), col(4), blk, blk, row, row, pl.BlockSpec((B_WIDTH, B_WIDTH), lambda i: (0, 0))],
        out_specs=[_subseq_spec(tm, dil) for _ in range(3) for dil in DILATIONS],
        out_shape=[jax.ShapeDtypeStruct(_subseq_shape(t, dil), BF16) for _ in range(3) for dil in DILATIONS],
        scratch_shapes=[_subseq_scratch(tm)], compiler_params=_params("parallel"),
    )(z, z, z, cos_t, sin_t, gq, gk, seg)
    return outs[:nd], outs[nd:2 * nd], outs[2 * nd:]


def _qk_prep_bwd(z, dqs, dks, dvs, cos_t, sin_t, gq, gk, seg, name):
    t = z.shape[0]
    tm = _tile(t, (SUBSEQ_TM,))
    col = lambda c: pl.BlockSpec((tm, B_WIDTH), lambda i: (i, c))
    row = pl.BlockSpec((1, B_WIDTH), lambda i: (0, 0))
    blk = col(0)
    nb = len(DILATIONS)
    subs = [_subseq_spec(tm, dil) for dil in DILATIONS]

    def body(*refs):
        q_ref, k_ref = refs[0], refs[1]
        dq_refs, dk_refs, dv_refs = refs[2:2 + nb], refs[2 + nb:2 + 2 * nb], refs[2 + 2 * nb:2 + 3 * nb]
        cos_ref, sin_ref, gq_ref, gk_ref, seg_ref, dz_ref, dgq_ref, dgk_ref, scr_ref = refs[2 + 3 * nb:]

        @pl.when(pl.program_id(0) == 0)
        def _():
            dgq_ref[...] = jnp.zeros_like(dgq_ref)
            dgk_ref[...] = jnp.zeros_like(dgk_ref)

        def total(d_refs):
            return sum(_from_subseq(r_[...], scr_ref, dil) for r_, dil in zip(d_refs, DILATIONS))

        def back(x, d_refs, g, dg_ref):
            dout = total(d_refs)
            dy = dout * cos_ref[...] + _swap_halves(dout * sin_ref[...])
            r = lax.rsqrt(_head_sum(x * x, seg_ref[...]) * (1.0 / HEAD_DIM) + EPS)
            xn = x * r
            dg_ref[...] += _colsum(dy * xn)
            dxn = dy * g
            return r * (dxn - xn * (_head_sum(dxn * xn, seg_ref[...]) * (1.0 / HEAD_DIM)))

        dz_ref[:, 0:B_WIDTH] = back(q_ref[...].astype(F32), dq_refs, gq_ref[...], dgq_ref).astype(BF16)
        dz_ref[:, B_WIDTH:2 * B_WIDTH] = back(k_ref[...].astype(F32), dk_refs, gk_ref[...], dgk_ref).astype(BF16)
        dz_ref[:, 2 * B_WIDTH:3 * B_WIDTH] = total(dv_refs).astype(BF16)

    return pl.pallas_call(
        body, name=name, grid=(t // tm,),
        in_specs=[col(2), col(3)] + subs * 3 + [blk, blk, row, row, pl.BlockSpec((B_WIDTH, B_WIDTH), lambda i: (0, 0))],
        out_specs=[pl.BlockSpec((tm, 3 * B_WIDTH), lambda i: (i, 0)), row, row],
        out_shape=[jax.ShapeDtypeStruct((t, 3 * B_WIDTH), BF16), jax.ShapeDtypeStruct((1, B_WIDTH), F32),
                   jax.ShapeDtypeStruct((1, B_WIDTH), F32)],
        scratch_shapes=[_subseq_scratch(tm)], compiler_params=_params("arbitrary"),
    )(z, z, *dqs, *dks, *dvs, cos_t, sin_t, gq, gk, seg)


def _subseq_views(x, col, name):
    t = x.shape[0]
    tm = _tile(t, (SUBSEQ_TM,))

    def body(x_ref, *rest):
        out_refs, scr_ref = rest[:-1], rest[-1]
        val = x_ref[...].astype(F32)
        for o_ref, dil in zip(out_refs, DILATIONS):
            o_ref[...] = _to_subseq(val, scr_ref, dil).astype(o_ref.dtype)

    return pl.pallas_call(
        body, name=name, grid=(t // tm,), in_specs=[pl.BlockSpec((tm, B_WIDTH), lambda i: (i, col))],
        out_specs=[_subseq_spec(tm, dil) for dil in DILATIONS],
        out_shape=[jax.ShapeDtypeStruct(_subseq_shape(t, dil), x.dtype) for dil in DILATIONS],
        scratch_shapes=[_subseq_scratch(tm)], compiler_params=_params("parallel"),
    )(x)


def _attn_fwd(q, k, v, dil, name):
    t = q.shape[0] * dil
    nb = t // dil // Q_BLOCK
    cur = pl.BlockSpec((Q_BLOCK, B_WIDTH), lambda r, i: (i, r))
    prev = pl.BlockSpec((Q_BLOCK, B_WIDTH), lambda r, i: (jnp.maximum(i - 1, 0), r))

    def body(q_ref, kp_ref, kc_ref, vp_ref, vc_ref, o_ref, lse_ref):
        i = pl.program_id(1)
        q = q_ref[...]
        kk = jnp.concatenate([kp_ref[...], kc_ref[...]], axis=0)
        vv = jnp.concatenate([vp_ref[...], vc_ref[...]], axis=0)
        a = lax.broadcasted_iota(jnp.int32, (Q_BLOCK, 2 * Q_BLOCK), 0)
        j = lax.broadcasted_iota(jnp.int32, (Q_BLOCK, 2 * Q_BLOCK), 1)
        dist = a + Q_BLOCK - j
        mask = (dist >= 0) & (dist <= Q_BLOCK) & ((j >= Q_BLOCK) | (i > 0))
        sls = [slice(h * HEAD_DIM, (h + 1) * HEAD_DIM) for h in range(HEADS)]
        scores = [_dot(q[:, sl], kk[:, sl], _NT) for sl in sls]
        ps, dens = [], []
        for sl, s in zip(sls, scores):
            s = jnp.where(mask, s * (HEAD_DIM ** -0.5), NEG)
            m = jnp.max(s, axis=-1, keepdims=True)
            p = jnp.exp(s - m)
            den = jnp.sum(p, axis=-1, keepdims=True)
            ps.append(p.astype(BF16))
            dens.append(den)
            lse_ref[:, sl] = jnp.broadcast_to(m + jnp.log(den), (Q_BLOCK, HEAD_DIM))
        for sl, p, den in zip(sls, ps, dens):
            o_ref[:, sl] = _dot(p, vv[:, sl], _NN) / den

    return pl.pallas_call(
        body, name=name, grid=(dil, nb), in_specs=[cur, prev, cur, prev, cur], out_specs=[cur, cur],
        out_shape=[jax.ShapeDtypeStruct(_subseq_shape(t, dil), F32)] * 2,
        compiler_params=_params("parallel", "parallel"),
    )(q, k, k, v, v)


def _attn_merge(outs, lses, cat, name):
    nb = len(DILATIONS)
    t = cat.shape[0]
    tm = _tile(t, (SUBSEQ_TM,))
    subs = [_subseq_spec(tm, dil) for dil in DILATIONS]

    def body(*refs):
        o_refs, l_refs = refs[:nb], refs[nb:2 * nb]
        yb_refs, lse_refs, cat_ref, scr_ref = refs[2 * nb + 1:3 * nb + 1], refs[3 * nb + 1:4 * nb + 1], refs[4 * nb + 1], refs[4 * nb + 2]
        ls = [_from_subseq(r[...], scr_ref, dil) for r, dil in zip(l_refs, DILATIONS)]
        m = functools.reduce(jnp.maximum, ls)
        tot = m + jnp.log(sum(jnp.exp(l - m) for l in ls))
        yb = sum(jnp.exp(l - tot) * _from_subseq(o[...], scr_ref, dil) for l, o, dil in zip(ls, o_refs, DILATIONS))
        cat_ref[...] = yb.astype(BF16)
        yb = yb.astype(BF16).astype(F32)
        for yb_ref, lse_ref, dil in zip(yb_refs, lse_refs, DILATIONS):
            yb_ref[...] = _to_subseq(yb, scr_ref, dil).astype(BF16)
            lse_ref[...] = _to_subseq(tot, scr_ref, dil)

    outs_ = pl.pallas_call(
        body, name=name, grid=(t // tm,), in_specs=subs * 2 + [pl.BlockSpec(memory_space=pl.ANY)],
        out_specs=subs * 2 + [pl.BlockSpec((tm, B_WIDTH), lambda i: (i, A_WIDTH // B_WIDTH))],
        out_shape=[jax.ShapeDtypeStruct(_subseq_shape(t, dil), BF16) for dil in DILATIONS]
        + [jax.ShapeDtypeStruct(_subseq_shape(t, dil), F32) for dil in DILATIONS] + [jax.ShapeDtypeStruct(cat.shape, BF16)],
        input_output_aliases={2 * nb: 2 * nb}, scratch_shapes=[_subseq_scratch(tm)], compiler_params=_params("parallel"),
    )(*outs, *lses, cat)
    return outs_[:nb], outs_[nb:2 * nb], outs_[2 * nb]


def _attn_bwd(q, k, v, do, o, lse, dil, name):
    t = q.shape[0] * dil
    nb = t // dil // Q_BLOCK
    cur = pl.BlockSpec((Q_BLOCK, B_WIDTH), lambda r, i: (i, r))
    prev = pl.BlockSpec((Q_BLOCK, B_WIDTH), lambda r, i: (jnp.maximum(i - 1, 0), r))
    scale = HEAD_DIM ** -0.5

    def body(q_ref, kp_ref, kc_ref, vp_ref, vc_ref, do_ref, o_ref, lse_ref, dq_ref, dk_ref, dv_ref,
             ck_ref, cv_ref, tk_ref, tv_ref):
        i = pl.program_id(1)

        @pl.when(i == 0)
        def _():
            ck_ref[...] = jnp.zeros_like(ck_ref)
            cv_ref[...] = jnp.zeros_like(cv_ref)

        q = q_ref[...]
        kk = jnp.concatenate([kp_ref[...], kc_ref[...]], axis=0)
        vv = jnp.concatenate([vp_ref[...], vc_ref[...]], axis=0)
        do = do_ref[...]
        dof = do.astype(F32)
        of = o_ref[...].astype(F32)
        a = lax.broadcasted_iota(jnp.int32, (Q_BLOCK, 2 * Q_BLOCK), 0)
        j = lax.broadcasted_iota(jnp.int32, (Q_BLOCK, 2 * Q_BLOCK), 1)
        dist = a + Q_BLOCK - j
        mask = (dist >= 0) & (dist <= Q_BLOCK) & ((j >= Q_BLOCK) | (i > 0))
        sls = [slice(h * HEAD_DIM, (h + 1) * HEAD_DIM) for h in range(HEADS)]
        scores = [_dot(q[:, sl], kk[:, sl], _NT) for sl in sls]
        dps = [_dot(do[:, sl], vv[:, sl], _NT) for sl in sls]
        ps, dss = [], []
        for sl, s, dp in zip(sls, scores, dps):
            p = jnp.exp(jnp.where(mask, s * scale, NEG) - lse_ref[:, sl.start:sl.start + 1])
            delta = jnp.sum(dof[:, sl] * of[:, sl], axis=-1, keepdims=True)
            dss.append((p * (dp - delta) * scale).astype(BF16))
            ps.append(p.astype(BF16))
        for sl, p, ds in zip(sls, ps, dss):
            dq_ref[:, sl] = _dot(ds, kk[:, sl], _NN)
            dv_t = _dot(do[:, sl], p, _TN)
            dk_t = _dot(q[:, sl], ds, _TN)
            tk_ref[sl, :] = ck_ref[sl, :] + dk_t[:, :Q_BLOCK]
            tv_ref[sl, :] = cv_ref[sl, :] + dv_t[:, :Q_BLOCK]
            ck_ref[sl, :] = dk_t[:, Q_BLOCK:]
            cv_ref[sl, :] = dv_t[:, Q_BLOCK:]

        @pl.when(i >= 1)
        def _():
            rows = pl.ds(pl.multiple_of((i - 1) * Q_BLOCK, Q_BLOCK), Q_BLOCK)
            dk_ref[rows, :] = tk_ref[...].T
            dv_ref[rows, :] = tv_ref[...].T

        @pl.when(i == nb - 1)
        def _():
            rows = pl.ds((nb - 1) * Q_BLOCK, Q_BLOCK)
            dk_ref[rows, :] = ck_ref[...].T
            dv_ref[rows, :] = cv_ref[...].T

    whole = pl.BlockSpec((t // dil, B_WIDTH), lambda r, i: (0, r))
    return pl.pallas_call(
        body, name=name, grid=(dil, nb), in_specs=[cur, prev, cur, prev, cur, cur, cur, cur],
        out_specs=[cur, whole, whole], out_shape=[jax.ShapeDtypeStruct(_subseq_shape(t, dil), F32)] * 3,
        scratch_shapes=[pltpu.VMEM((B_WIDTH, Q_BLOCK), F32)] * 4,
        compiler_params=_params("parallel", "arbitrary"),
    )(q, k, k, v, v, do, o, lse)


FFN_TN = 256
FFN_FWD_CHUNK = 256
FFN_BWD_CHUNK = 128


def _ffn_up(h, up_t, name):
    t, k = h.shape
    tm = _tile(t)

    def body(h_ref, w_ref, o_ref):
        o_ref[...] = _dot(h_ref[...], w_ref[...], _NT).astype(BF16)

    return pl.pallas_call(
        body, name=name, grid=(2, t // tm),
        in_specs=[pl.BlockSpec((tm, k), lambda p, i: (i, 0)), pl.BlockSpec((None, FFN_DIM, k), lambda p, i: (p, 0, 0))],
        out_specs=pl.BlockSpec((None, tm, FFN_DIM), lambda p, i: (p, i, 0)),
        out_shape=jax.ShapeDtypeStruct((2, t, FFN_DIM), BF16), compiler_params=_params("parallel", "parallel"),
    )(h, up_t.reshape(2, FFN_DIM, k))


def _ffn_up_dx(du, up_t, name):
    t = du.shape[1]
    k = up_t.shape[1]
    tm = _tile(t)

    def body(a_ref, b_ref, o_ref):
        o_ref[...] = _dot(a_ref[0], b_ref[0], _NN) + _dot(a_ref[1], b_ref[1], _NN)

    return pl.pallas_call(
        body, name=name, grid=(t // tm,),
        in_specs=[pl.BlockSpec((2, tm, FFN_DIM), lambda i: (0, i, 0)), pl.BlockSpec((2, FFN_DIM, k), lambda i: (0, 0, 0))],
        out_specs=pl.BlockSpec((tm, k), lambda i: (i, 0)), out_shape=jax.ShapeDtypeStruct((t, k), F32),
        compiler_params=_params("parallel"),
    )(du, up_t.reshape(2, FFN_DIM, k))


def _ffn_conv(win, w_ref, b_ref, p):
    x = win.astype(F32)
    x0, x1, x2 = x[FFN_HALO:], pltpu.roll(x, 1, 0)[FFN_HALO:], pltpu.roll(x, 2, 0)[FFN_HALO:]
    return b_ref[p] + w_ref[p, 2:3, :] * x0 + w_ref[p, 1:2, :] * x1 + w_ref[p, 0:1, :] * x2


def _zero_if(cond, v):
    return jnp.where(cond, 0, v).astype(v.dtype)


def _ffn_act(u, dw_w, dw_b, name):
    t = u.shape[1]
    tm = _tile(t)
    chunk = min(FFN_FWD_CHUNK, tm)
    hb = tm // FFN_HALO
    main = pl.BlockSpec((2, tm, FFN_TN), lambda i, j: (0, i, j))
    halo = pl.BlockSpec((2, FFN_HALO, FFN_TN), lambda i, j: (0, jnp.maximum(i * hb - 1, 0), j))
    wsp = pl.BlockSpec((2, FFN_CONV_WIDTH, FFN_TN), lambda i, j: (0, 0, j))
    bsp = pl.BlockSpec((2, 1, FFN_TN), lambda i, j: (0, 0, j))

    def body(u_ref, uh_ref, w_ref, b_ref, o_ref, z_ref):
        first = pl.program_id(0) == 0

        def emit(rows, wins):
            za, zb = _ffn_conv(wins[0], w_ref, b_ref, 0), _ffn_conv(wins[1], w_ref, b_ref, 1)
            o_ref[rows, :] = (za * _sigmoid(za) * zb).astype(BF16)
            z_ref[0, rows, :] = za.astype(BF16)
            z_ref[1, rows, :] = zb.astype(BF16)

        emit(pl.ds(0, chunk), [jnp.concatenate([_zero_if(first, uh_ref[p]), u_ref[p, 0:chunk, :]], axis=0) for p in range(2)])

        def step(c, carry):
            s = pl.multiple_of(c * chunk, chunk)
            emit(pl.ds(s, chunk), [u_ref[p, pl.ds(s - FFN_HALO, chunk + FFN_HALO), :] for p in range(2)])
            return carry

        lax.fori_loop(1, tm // chunk, step, 0)

    return pl.pallas_call(
        body, name=name, grid=(t // tm, FFN_DIM // FFN_TN), in_specs=[main, halo, wsp, bsp],
        out_specs=[pl.BlockSpec((tm, FFN_TN), lambda i, j: (i, j)), main],
        out_shape=[jax.ShapeDtypeStruct((t, FFN_DIM), BF16), jax.ShapeDtypeStruct((2, t, FFN_DIM), BF16)],
        compiler_params=_params("parallel", "parallel"),
    )(u, u, dw_w, dw_b)


def _fold8(v):
    return jnp.sum(v.reshape(v.shape[0] // 8, 8, v.shape[1]), axis=0)


def _ffn_act_bwd(u, z, dact, dw_w, name):
    t = u.shape[1]
    tm = _tile(t)
    chunk = min(FFN_BWD_CHUNK, tm // 2)
    halo = FFN_HALO
    hb = tm // halo
    nt = t // tm
    last_halo = t // halo - 1
    next_i = lambda i: jnp.minimum((i + 1) * hb, last_halo)
    main = pl.BlockSpec((2, tm, FFN_TN), lambda j, i: (0, i, j))
    nxt = pl.BlockSpec((2, halo, FFN_TN), lambda j, i: (0, next_i(i), j))
    wsp = pl.BlockSpec((2, FFN_CONV_WIDTH, FFN_TN), lambda j, i: (0, 0, j))
    bsp = pl.BlockSpec((2, 1, FFN_TN), lambda j, i: (0, 0, j))

    def body(u_ref, z_ref, zn_ref, da_ref, dan_ref, w_ref, du_ref, dw_ref, db_ref, acc_ref):
        i = pl.program_id(1)
        last = i == nt - 1
        acc_ref[...] = jnp.zeros_like(acc_ref)

        def emit(rows, zs, dact):
            n = chunk + halo
            za, zb, dact = zs[0].astype(F32), zs[1].astype(F32), dact.astype(F32)
            sg = _sigmoid(za)
            dzs = (dact * zb * (sg * (1.0 + za * (1.0 - sg))), dact * (za * sg))
            for p, dz in enumerate(dzs):
                ahead = (dz[:chunk], pltpu.roll(dz, n - 1, 0)[:chunk], pltpu.roll(dz, n - 2, 0)[:chunk])
                um = u_ref[p, rows, :].astype(F32)
                acc_ref[p, FFN_CONV_WIDTH] += _fold8(ahead[0])
                du = None
                for j, dzj in enumerate(ahead):
                    k = FFN_CONV_WIDTH - 1 - j
                    acc_ref[p, k] += _fold8(dzj * um)
                    term = w_ref[p, k:k + 1, :] * dzj
                    du = term if du is None else du + term
                du_ref[p, rows, :] = du.astype(BF16)

        def step(c, carry):
            s = pl.multiple_of(c * chunk, chunk)
            emit(pl.ds(s, chunk), [z_ref[p, pl.ds(s, chunk + halo), :] for p in range(2)], da_ref[pl.ds(s, chunk + halo), :])
            return carry

        lax.fori_loop(0, tm // chunk - 1, step, 0)
        s = tm - chunk
        emit(pl.ds(s, chunk),
             [jnp.concatenate([z_ref[p, s:tm, :], zn_ref[p]], axis=0) for p in range(2)],
             jnp.concatenate([da_ref[s:tm, :], _zero_if(last, dan_ref[...])], axis=0))

        @pl.when(i == 0)
        def _():
            dw_ref[...] = jnp.zeros_like(dw_ref)
            db_ref[...] = jnp.zeros_like(db_ref)

        for p in range(2):
            for k in range(FFN_CONV_WIDTH):
                dw_ref[p, k:k + 1, :] += _colsum(acc_ref[p, k])
            db_ref[p] += _colsum(acc_ref[p, FFN_CONV_WIDTH])

    return pl.pallas_call(
        body, name=name, grid=(FFN_DIM // FFN_TN, nt),
        in_specs=[main, main, nxt, pl.BlockSpec((tm, FFN_TN), lambda j, i: (i, j)),
                  pl.BlockSpec((halo, FFN_TN), lambda j, i: (next_i(i), j)), wsp],
        out_specs=[main, wsp, bsp],
        out_shape=[jax.ShapeDtypeStruct((2, t, FFN_DIM), BF16), jax.ShapeDtypeStruct((2, FFN_CONV_WIDTH, FFN_DIM), F32),
                   jax.ShapeDtypeStruct((2, 1, FFN_DIM), F32)],
        scratch_shapes=[pltpu.VMEM((2, FFN_CONV_WIDTH + 1, 8, FFN_TN), F32)],
        compiler_params=_params("parallel", "arbitrary"),
    )(u, z, z, dact, dact, dw_w)


CONV_TM = 256
CONV_ROWS = 128
CONV_LANES = 128


def _glu_window(pa_ref, pah_ref, pg_ref, pgh_ref, scr_ref, first):
    ah, gh = pah_ref[...].astype(F32), pgh_ref[...].astype(F32)
    scr_ref[0:CONV_HALO, :] = jnp.where(first, 0.0, ah * _sigmoid(gh))
    scr_ref[CONV_HALO:, :] = pa_ref[...].astype(F32) * _sigmoid(pg_ref[...].astype(F32))


def _tap_slabs(win, rows, ahead):
    n = win.shape[0]
    for s in range(8):
        ws = win if s == 0 else pltpu.roll(win, n - s if ahead else s, 0)
        for q in range(CONV_HALO // 8):
            o = 8 * q + s
            if o < CONV_WIDTH:
                start = 8 * q if ahead else CONV_HALO - 8 * q
                yield CONV_WIDTH - 1 - o, ws[start:start + rows]


def _conformer_specs(t):
    tm = _tile(t, (CONV_TM, 128))
    hb = tm // CONV_HALO
    d = D_MODEL
    main = lambda c: pl.BlockSpec((tm, d), lambda i: (i, c))
    halo = lambda c: pl.BlockSpec((CONV_HALO, d), lambda i: (jnp.maximum(i * hb - 1, 0), c))
    row = pl.BlockSpec((1, d), lambda i: (0, 0))
    wsp = pl.BlockSpec((CONV_WIDTH, d), lambda i: (0, 0))
    return tm, main, halo, row, wsp


def _conformer_mid(p, dw_w, dw_b, ln_g, ln_b, name):
    t = p.shape[0]
    tm, main, halo, row, wsp = _conformer_specs(t)
    d, lanes = D_MODEL, CONV_LANES

    def body(pa_ref, pah_ref, pg_ref, pgh_ref, w_ref, b_ref, g_ref, lb_ref, o_ref, dc_ref, scr_ref):
        _glu_window(pa_ref, pah_ref, pg_ref, pgh_ref, scr_ref, pl.program_id(0) == 0)
        for c in range(d // lanes):
            ls = slice(c * lanes, (c + 1) * lanes)
            acc = jnp.broadcast_to(b_ref[:, ls], (tm, lanes))
            for k, slab in _tap_slabs(scr_ref[:, ls], tm, False):
                acc = acc + w_ref[k:k + 1, ls] * slab
            dc_ref[:, ls] = acc

        def norm(r, carry):
            r0 = pl.multiple_of(r * 32, 32)
            dc = dc_ref[pl.ds(r0, 32), :]
            xc = dc - jnp.mean(dc, axis=-1, keepdims=True)
            ln = xc * lax.rsqrt(jnp.mean(xc * xc, axis=-1, keepdims=True) + EPS) * g_ref[...] + lb_ref[...]
            o_ref[pl.ds(r0, 32), :] = (ln * _sigmoid(ln)).astype(BF16)
            return carry

        lax.fori_loop(0, tm // 32, norm, 0)

    return pl.pallas_call(
        body, name=name, grid=(t // tm,), in_specs=[main(0), halo(0), main(1), halo(1), wsp, row, row, row],
        out_specs=[main(0), main(0)], out_shape=[jax.ShapeDtypeStruct((t, d), BF16), jax.ShapeDtypeStruct((t, d), F32)],
        scratch_shapes=[pltpu.VMEM((tm + CONV_HALO, d), F32)], compiler_params=_params("parallel"),
    )(p, p, p, p, dw_w, dw_b, ln_g, ln_b)


def _conformer_mid_bwd(p, dc, ds, ln_g, ln_b, name):
    t = p.shape[0]
    tm, main, halo, row, wsp = _conformer_specs(t)
    d, nt = D_MODEL, t // tm
    rows, lanes = CONV_ROWS, CONV_LANES

    def body(pa_ref, pah_ref, pg_ref, pgh_ref, dc_ref, ds_ref, g_ref, lb_ref,
             ddc_ref, dw_ref, db_ref, dg_ref, dlb_ref, scr_ref, wacc_ref, racc_ref):
        i = pl.program_id(0)

        @pl.when(i == 0)
        def _():
            wacc_ref[...] = jnp.zeros_like(wacc_ref)
            racc_ref[...] = jnp.zeros_like(racc_ref)

        _glu_window(pa_ref, pah_ref, pg_ref, pgh_ref, scr_ref, i == 0)

        def norm_bwd(r, carry):
            r0 = pl.multiple_of(r * 32, 32)
            dcv = dc_ref[pl.ds(r0, 32), :]
            xc = dcv - jnp.mean(dcv, axis=-1, keepdims=True)
            rstd = lax.rsqrt(jnp.mean(xc * xc, axis=-1, keepdims=True) + EPS)
            xhat = xc * rstd
            ln = xhat * g_ref[...] + lb_ref[...]
            sg = _sigmoid(ln)
            dln = ds_ref[pl.ds(r0, 32), :].astype(F32) * (sg * (1.0 + ln * (1.0 - sg)))
            dxh = dln * g_ref[...]
            ddc = rstd * (dxh - jnp.mean(dxh, axis=-1, keepdims=True) - xhat * jnp.mean(dxh * xhat, axis=-1, keepdims=True))
            ddc_ref[pl.ds(r0, 32), :] = ddc
            racc_ref[0] += _fold8(dln * xhat)
            racc_ref[1] += _fold8(dln)
            racc_ref[2] += _fold8(ddc)
            return carry

        lax.fori_loop(0, tm // 32, norm_bwd, 0)

        for c in range(d // lanes):
            ls = slice(c * lanes, (c + 1) * lanes)

            def taps(r, carry, ls=ls):
                r0 = pl.multiple_of(r * rows, rows)
                ddc = ddc_ref[pl.ds(r0, rows), ls]
                for k, slab in _tap_slabs(scr_ref[pl.ds(r0, rows + CONV_HALO), ls], rows, False):
                    wacc_ref[k, :, ls] += _fold8(ddc * slab)
                return carry

            lax.fori_loop(0, tm // rows, taps, 0)

        @pl.when(i == nt - 1)
        def _():
            for k in range(CONV_WIDTH):
                dw_ref[k:k + 1, :] = _colsum(wacc_ref[k])
            dg_ref[...] = _colsum(racc_ref[0])
            dlb_ref[...] = _colsum(racc_ref[1])
            db_ref[...] = _colsum(racc_ref[2])

    return pl.pallas_call(
        body, name=name, grid=(nt,), in_specs=[main(0), halo(0), main(1), halo(1), main(0), main(0), row, row],
        out_specs=[main(0), wsp, row, row, row],
        out_shape=[jax.ShapeDtypeStruct((t, d), F32), jax.ShapeDtypeStruct((CONV_WIDTH, d), F32)]
        + [jax.ShapeDtypeStruct((1, d), F32)] * 3,
        scratch_shapes=[pltpu.VMEM((tm + CONV_HALO, d), F32), pltpu.VMEM((CONV_WIDTH, 8, d), F32), pltpu.VMEM((3, 8, d), F32)],
        compiler_params=_params("arbitrary"),
    )(p, p, p, p, dc, ds, ln_g, ln_b)


def _conformer_glu_bwd(p, ddc, dw_w, name):
    t = p.shape[0]
    d = D_MODEL
    tm = _tile(t, (CONV_TM, 128))
    hb = tm // CONV_HALO
    nt = t // tm
    last_halo = t // CONV_HALO - 1
    rows, lanes = CONV_ROWS, CONV_LANES
    col = lambda c: pl.BlockSpec((tm, d), lambda i: (i, c))
    nxt = pl.BlockSpec((CONV_HALO, d), lambda i: (jnp.minimum((i + 1) * hb, last_halo), 0))

    def body(pa_ref, pg_ref, ddc_ref, ddcn_ref, w_ref, dp_ref, db_ref, scr_ref, acc_ref):
        i = pl.program_id(0)

        @pl.when(i == 0)
        def _():
            acc_ref[...] = jnp.zeros_like(acc_ref)

        scr_ref[0:tm, :] = ddc_ref[...]
        scr_ref[tm:, :] = _zero_if(i == nt - 1, ddcn_ref[...])
        for c in range(d // lanes):
            ls = slice(c * lanes, (c + 1) * lanes)
            gs = slice(d + c * lanes, d + (c + 1) * lanes)

            def taps(r, carry, ls=ls, gs=gs):
                r0 = pl.multiple_of(r * rows, rows)
                dglu = None
                for k, slab in _tap_slabs(scr_ref[pl.ds(r0, rows + CONV_HALO), ls], rows, True):
                    term = w_ref[k:k + 1, ls] * slab
                    dglu = term if dglu is None else dglu + term
                a = pa_ref[pl.ds(r0, rows), ls].astype(F32)
                sg = _sigmoid(pg_ref[pl.ds(r0, rows), ls].astype(F32))
                da = (dglu * sg).astype(BF16)
                dg = (dglu * a * sg * (1.0 - sg)).astype(BF16)
                dp_ref[pl.ds(r0, rows), ls] = da
                dp_ref[pl.ds(r0, rows), gs] = dg
                acc_ref[:, ls] += _fold8(da.astype(F32))
                acc_ref[:, gs] += _fold8(dg.astype(F32))
                return carry

            lax.fori_loop(0, tm // rows, taps, 0)

        @pl.when(i == nt - 1)
        def _():
            db_ref[...] = _colsum(acc_ref[...])

    return pl.pallas_call(
        body, name=name, grid=(nt,),
        in_specs=[col(0), col(1), col(0), nxt, pl.BlockSpec((CONV_WIDTH, d), lambda i: (0, 0))],
        out_specs=[pl.BlockSpec((tm, 2 * d), lambda i: (i, 0)), pl.BlockSpec((1, 2 * d), lambda i: (0, 0))],
        out_shape=[jax.ShapeDtypeStruct((t, 2 * d), BF16), jax.ShapeDtypeStruct((1, 2 * d), F32)],
        scratch_shapes=[pltpu.VMEM((tm + CONV_HALO, d), F32), pltpu.VMEM((8, 2 * d), F32)],
        compiler_params=_params("arbitrary"),
    )(p, p, ddc, ddc, dw_w)


def _colsum_call(a, name):
    t, n = a.shape
    tm = _tile(t)

    def body(a_ref, o_ref):
        @pl.when(pl.program_id(0) == 0)
        def _():
            o_ref[...] = jnp.zeros_like(o_ref)

        o_ref[...] += _colsum(a_ref[...].astype(F32))

    return pl.pallas_call(
        body, name=name, grid=(t // tm,), in_specs=[pl.BlockSpec((tm, n), lambda i: (i, 0))],
        out_specs=pl.BlockSpec((1, n), lambda i: (0, 0)), out_shape=jax.ShapeDtypeStruct((1, n), F32),
        compiler_params=_params("arbitrary"),
    )(a)


def _ada_fwd(c_all, w, name):
    rows, d = c_all.shape
    n = w.shape[1]
    tn = _tile(n, (256, 128))

    def body(c_ref, w_ref, o_ref):
        c = c_ref[...]
        o_ref[...] = _dot((c * _sigmoid(c)).astype(BF16), w_ref[...].astype(BF16), _NN)

    return pl.pallas_call(
        body, name=name, grid=(n // tn,),
        in_specs=[pl.BlockSpec((rows, d), lambda j: (0, 0)), pl.BlockSpec((d, tn), lambda j: (0, j))],
        out_specs=pl.BlockSpec((rows, tn), lambda j: (0, j)), out_shape=jax.ShapeDtypeStruct((rows, n), F32),
        compiler_params=_params("parallel"),
    )(c_all, w)


def _ada_bwd(c_all, dmod, name):
    rows, d = c_all.shape
    n = dmod.shape[1]
    tn = _tile(n, (256, 128))

    def body(c_ref, g_ref, o_ref):
        c = c_ref[...]
        o_ref[...] = _dot((c * _sigmoid(c)).astype(BF16), g_ref[...].astype(BF16), _TN)

    return pl.pallas_call(
        body, name=name, grid=(n // tn,),
        in_specs=[pl.BlockSpec((rows, d), lambda j: (0, 0)), pl.BlockSpec((rows, tn), lambda j: (0, j))],
        out_specs=pl.BlockSpec((d, tn), lambda j: (0, j)), out_shape=jax.ShapeDtypeStruct((d, n), F32),
        compiler_params=_params("parallel"),
    )(c_all, dmod)


def _sum_in_device_order(own, land, me, name):
    s, r, c = land.shape
    tr = _row_tile(r, 256)
    slot = lambda k: pl.BlockSpec((None, tr, c), lambda i, me_ref: (jnp.where(me_ref[0] == k, (k + 1) % s, k), i, 0))
    own_spec = pl.BlockSpec((tr, c), lambda i, me_ref: (i, 0))

    def body(me_ref, own_ref, *refs):
        o_ref = refs[-1]
        acc = None
        for k, ref in enumerate(refs[:-1]):
            term = jnp.where(me_ref[0] == k, own_ref[...], ref[...]).astype(F32)
            acc = term if acc is None else acc + term
        o_ref[...] = acc

    return pl.pallas_call(
        body, name=name, out_shape=jax.ShapeDtypeStruct((r, c), F32),
        grid_spec=pltpu.PrefetchScalarGridSpec(
            num_scalar_prefetch=1, grid=(r // tr,), in_specs=[own_spec] + [slot(k) for k in range(s)], out_specs=own_spec),
        compiler_params=_params("parallel"),
    )(me, own, *[land] * s)


def _sum_with_own(blocks, land, me, name):
    s, r, c = land.shape
    tr = _row_tile(r, 256)
    slot = lambda k: pl.BlockSpec((None, tr, c), lambda i, me_ref: ((me_ref[0] + k) % s, i, 0))

    def body(me_ref, own_ref, *refs):
        o_ref = refs[-1]
        acc = own_ref[...].astype(F32)
        for ref in refs[:-1]:
            acc = acc + ref[...].astype(F32)
        o_ref[...] = acc

    return pl.pallas_call(
        body, name=name, out_shape=jax.ShapeDtypeStruct((r, c), F32),
        grid_spec=pltpu.PrefetchScalarGridSpec(
            num_scalar_prefetch=1, grid=(r // tr,), in_specs=[slot(0)] + [slot(k) for k in range(1, s)],
            out_specs=pl.BlockSpec((tr, c), lambda i, me_ref: (i, 0))),
        compiler_params=_params("parallel"),
    )(me, blocks, *[land] * (s - 1))


def _adamw_update(w, g, m, v):
    nm = ADAM_B1 * m + (1.0 - ADAM_B1) * g
    nv = ADAM_B2 * v + (1.0 - ADAM_B2) * (g * g)
    m_hat = nm * (1.0 / (1.0 - ADAM_B1 ** ADAM_STEP))
    v_hat = nv * (1.0 / (1.0 - ADAM_B2 ** ADAM_STEP))
    return -ADAM_LR * (m_hat / (jnp.sqrt(v_hat) + ADAM_EPS) + ADAM_WD * w), nm, nv


def _adamw(w, g, m, v, name):
    l, r, c = w.shape
    tr = _row_tile(r, 256)
    blk = pl.BlockSpec((None, tr, c), lambda k, i: (k, i, 0))

    def body(w_ref, g_ref, m_ref, v_ref, d_ref, nm_ref, nv_ref):
        d_ref[...], nm_ref[...], nv_ref[...] = _adamw_update(w_ref[...], g_ref[...], m_ref[...], v_ref[...])

    return pl.pallas_call(
        body, name=name, grid=(l, r // tr), in_specs=[blk] * 4, out_specs=[blk] * 3,
        out_shape=[jax.ShapeDtypeStruct(w.shape, F32)] * 3, compiler_params=_params("parallel", "parallel"),
    )(w, g, m, v)


def _adamw_small(ws, gs, ms, vs, name):
    n = len(ws)
    two_d = lambda a: a.reshape(-1, a.shape[-1])

    def body(*refs):
        ins, outs = refs[:4 * n], refs[4 * n:]
        for a in range(n):
            outs[a][...], outs[n + a][...], outs[2 * n + a][...] = _adamw_update(*[ins[k * n + a][...] for k in range(4)])

    res = pl.pallas_call(
        body, name=name, out_shape=[jax.ShapeDtypeStruct(two_d(w).shape, F32) for w in ws] * 3,
    )(*[two_d(a) for a in (*ws, *gs, *ms, *vs)])
    return [[res[k * n + a].reshape(ws[a].shape) for a in range(n)] for k in range(3)]


def _mesh_pos():
    return lax.axis_index("x"), lax.axis_index("y"), lax.axis_index("c")


def _all_gather_vmem(x_shard, name):
    m_per, n = x_shard.shape

    def body(x_ref, out_ref, send_sems, recv_sems, local_sem):
        x, y, c = _mesh_pos()
        me, sibling = (x, y, c), (x, y, 1 - c)
        chips = [(1 - x, y), (x, 1 - y), (1 - x, 1 - y)]

        def rows(px, py, pc):
            return out_ref.at[pl.ds((4 * px + 2 * py + pc) * m_per, m_per), :]

        def copy(k, block, to, src=None):
            return pltpu.make_async_remote_copy(
                src_ref=rows(*block) if src is None else src, dst_ref=rows(*block),
                send_sem=send_sems.at[k], recv_sem=recv_sems.at[k], device_id=to, device_id_type=MESH)

        mine = pltpu.make_async_copy(x_ref, rows(*me), local_sem)
        mine.start()
        first = [copy(0, me, sibling, src=x_ref)]
        first += [copy(1 + j, me, (*chip, c), src=x_ref) for j, chip in enumerate(chips)]
        for cp in first:
            cp.start()
        passed = [copy(4 + j, (*chip, c), sibling) for j, chip in enumerate(chips)]
        for j, chip in enumerate(chips):
            copy(1 + j, (*chip, c), me).wait_recv()
            passed[j].start()
        copy(0, sibling, me).wait_recv()
        for j, chip in enumerate(chips):
            copy(4 + j, (*chip, 1 - c), me).wait_recv()
        for cp in first + passed:
            cp.wait_send()
        mine.wait()

    return pl.pallas_call(
        body, name=name, out_shape=jax.ShapeDtypeStruct((N_DEV * m_per, n), x_shard.dtype),
        in_specs=[pl.BlockSpec(memory_space=pltpu.VMEM)], out_specs=pl.BlockSpec(memory_space=pltpu.VMEM),
        scratch_shapes=[pltpu.SemaphoreType.DMA((7,)), pltpu.SemaphoreType.DMA((7,)), pltpu.SemaphoreType.DMA],
    )(x_shard)


def _all_gather_hbm(shards, name):
    n = len(shards)
    out_shape = [jax.ShapeDtypeStruct((N_DEV,) + s.shape, s.dtype) for s in shards]

    def body(*refs):
        x_refs, out_refs = refs[:n], refs[n:2 * n]
        send_sems, recv_sems, local_sems = refs[2 * n:]
        x, y, c = _mesh_pos()
        me, sibling = (x, y, c), (x, y, 1 - c)
        chips = [(1 - x, y), (x, 1 - y), (1 - x, 1 - y)]

        def blk(a, p):
            return out_refs[a].at[4 * p[0] + 2 * p[1] + p[2]]

        def copy(a, k, block, to, src=None):
            return pltpu.make_async_remote_copy(
                src_ref=blk(a, block) if src is None else src, dst_ref=blk(a, block),
                send_sem=send_sems.at[7 * a + k], recv_sem=recv_sems.at[7 * a + k], device_id=to, device_id_type=MESH)

        mine = [pltpu.make_async_copy(x_refs[a], blk(a, me), local_sems.at[a]) for a in range(n)]
        for cp in mine:
            cp.start()
        first = []
        for a in range(n):
            first.append(copy(a, 0, me, sibling, src=x_refs[a]))
            first += [copy(a, 1 + j, me, (*chip, c), src=x_refs[a]) for j, chip in enumerate(chips)]
        for cp in first:
            cp.start()
        passed = []
        for j, chip in enumerate(chips):
            for a in range(n):
                copy(a, 1 + j, (*chip, c), me).wait_recv()
                fwd = copy(a, 4 + j, (*chip, c), sibling)
                fwd.start()
                passed.append(fwd)
        for a in range(n):
            copy(a, 0, sibling, me).wait_recv()
            for j, chip in enumerate(chips):
                copy(a, 4 + j, (*chip, 1 - c), me).wait_recv()
        for cp in first + passed:
            cp.wait_send()
        for cp in mine:
            cp.wait()

    return pl.pallas_call(
        body, name=name, out_shape=out_shape, in_specs=[pl.BlockSpec(memory_space=pltpu.VMEM)] * n,
        out_specs=[pl.BlockSpec(memory_space=pl.ANY)] * n,
        scratch_shapes=[pltpu.SemaphoreType.DMA((7 * n,)), pltpu.SemaphoreType.DMA((7 * n,)), pltpu.SemaphoreType.DMA((n,))],
    )(*shards)


def _peers(x, y, c):
    flip = lambda v, f: 1 - v if f else v
    return [(flip(x, m & 4), flip(y, m & 2), flip(c, m & 1)) for m in range(1, N_DEV)]


def _dev_index(p):
    return 4 * p[0] + 2 * p[1] + p[2]


def _push_copies(src_refs, land_refs, send_sems, recv_sems, scatter, receive):
    x, y, c = _mesh_pos()
    me = _dev_index((x, y, c))
    copies = []
    for a, (src, land) in enumerate(zip(src_refs, land_refs)):
        for k, p in enumerate(_peers(x, y, c)):
            copies.append(pltpu.make_async_remote_copy(
                src_ref=src.at[_dev_index(p)] if scatter else src, dst_ref=land.at[_dev_index(p) if receive else me],
                send_sem=send_sems.at[7 * a + k], recv_sem=recv_sems.at[7 * a + k], device_id=p, device_id_type=MESH))
    return copies


_HBM = pl.BlockSpec(memory_space=pltpu.HBM)
_SEM = pl.BlockSpec(memory_space=pltpu.SEMAPHORE)
_EFFECT = pltpu.SideEffectType.DATAFLOW_SIDE_EFFECTING


def _pushes_start(srcs, lands, scatter, name):
    n = len(srcs)

    def body(*refs):
        src_refs, land_refs = refs[:n], refs[n:2 * n]
        send_sems, recv_sems = refs[2 * n], refs[2 * n + 1]
        token = refs[-1]
        for cp in _push_copies(src_refs, land_refs, send_sems, recv_sems, scatter, receive=False):
            cp.start()
        token[...] = jnp.zeros_like(token)

    hbm = lambda a: pltpu.HBM(a.shape, a.dtype)
    sems = pltpu.SemaphoreType.DMA((7 * n,))
    outs = pl.pallas_call(
        body, name=name,
        out_shape=(sems, sems, *[hbm(a) for a in srcs], *[hbm(a) for a in lands], jax.ShapeDtypeStruct((8, 128), F32)),
        in_specs=[_HBM] * (2 * n), out_specs=(_SEM, _SEM, *[_HBM] * (2 * n), pl.BlockSpec(memory_space=pltpu.VMEM)),
        input_output_aliases={i: 2 + i for i in range(2 * n)},
        compiler_params=pltpu.CompilerParams(has_side_effects=_EFFECT),
    )(*[pltpu.with_memory_space_constraint(a, pltpu.HBM) for a in (*srcs, *lands)])
    return (outs[0], outs[1], outs[2:2 + n], outs[2 + n:2 + 2 * n], scatter), outs[-1]


def _pushes_wait(handle, after, name):
    send_sems, recv_sems, srcs, lands, scatter = handle
    n = len(srcs)

    def body(*refs):
        src_refs, land_refs = refs[:n], refs[n:2 * n]
        for cp in _push_copies(src_refs, land_refs, refs[2 * n], refs[2 * n + 1], scatter, receive=True):
            cp.wait_send()
            cp.wait_recv()

    hbm = lambda a: pltpu.HBM(a.shape, a.dtype)
    outs = pl.pallas_call(
        body, name=name, out_shape=tuple(hbm(a) for a in (*srcs, *lands)),
        in_specs=[_HBM] * (2 * n) + [_SEM, _SEM, pl.BlockSpec(memory_space=pl.ANY)], out_specs=tuple([_HBM] * (2 * n)),
        input_output_aliases={i: i for i in range(2 * n)},
        compiler_params=pltpu.CompilerParams(has_side_effects=_EFFECT),
    )(*srcs, *lands, send_sems, recv_sems, after)
    return outs[:n], outs[n:]


def _landing_zones(srcs, name):
    n = len(srcs)

    def body(*refs):
        src_refs, land_refs, bufs, sems = refs[:n], refs[n:2 * n], refs[2 * n:3 * n], refs[3 * n]
        me = _dev_index(_mesh_pos())
        load = [pltpu.make_async_copy(src, buf, sems.at[a]) for a, (src, buf) in enumerate(zip(src_refs, bufs))]
        store = [pltpu.make_async_copy(buf, land.at[me], sems.at[a]) for a, (buf, land) in enumerate(zip(bufs, land_refs))]
        for cp in load:
            cp.start()
        for ld, st in zip(load, store):
            ld.wait()
            st.start()
        for cp in store:
            cp.wait()

    any_spec = pl.BlockSpec(memory_space=pl.ANY)
    return pl.pallas_call(
        body, name=name, out_shape=[jax.ShapeDtypeStruct((N_DEV,) + s.shape, s.dtype) for s in srcs],
        in_specs=[any_spec] * n, out_specs=[any_spec] * n,
        scratch_shapes=[pltpu.VMEM(s.shape, s.dtype) for s in srcs] + [pltpu.SemaphoreType.DMA((n,))],
        compiler_params=pltpu.CompilerParams(vmem_limit_bytes=V7X_VMEM_LIMIT),
    )(*srcs)


def _ffn_forward(x, mod, norm_g, w, tag):
    sh, sc, gate = mod
    h = _modnorm(x, norm_g, sc, sh, f"{tag}_norm")
    u = _ffn_up(h, w["up_t"], f"{tag}_up")
    act, z = _ffn_act(u, w["dw_w"], w["dw_b"], f"{tag}_act")
    y, x_new = _matmul(act, w["down"], "nn", BF16, f"{tag}_down", resid=(x, gate))
    return x_new, (x, h, u, z, act, y)


def _behind(row, token):
    return row if token is None else row + token[0:1, 0:1]


def _ffn_backward(dx_new, dy, d_gate, saved, mod, norm_g, w, tag, emit, below):
    x, h, u, z, act, _ = saved
    _, sc, _ = mod
    d_down = _matmul_tn_acc(act, dy, f"{tag}_down_dw")
    dact = _matmul(dy, w["down"], "nt", BF16, f"{tag}_down_dx")
    du, d_dw_w, d_dw_b = _ffn_act_bwd(u, z, dact, w["dw_w"], f"{tag}_act_bwd")
    d_up_t = _matmul_tn_acc(du, h, f"{tag}_up_dw").reshape(2 * FFN_DIM, -1)
    token = emit([d_up_t, d_down])
    dh = _ffn_up_dx(du, w["up_t"], f"{tag}_up_dx")
    dx, d_w, d_sh, *dy_below = _modnorm_bwd(x, dh, norm_g, _behind(sc, token), dx_new, below, f"{tag}_norm_bwd")
    return (dx, *dy_below), dict(dw_w=d_dw_w.transpose(1, 0, 2).reshape(FFN_CONV_WIDTH, 2 * FFN_DIM),
                    dw_b=d_dw_b.reshape(1, 2 * FFN_DIM), norm_g=d_w * (1.0 + sc), sh=d_sh, sc=d_w * norm_g, gate=d_gate)


def _mixer_forward(x, mod, norm_g, w, rope, tag):
    sh, sc, gate = mod
    h = _modnorm(x, norm_g, sc, sh, f"{tag}_norm")
    z = _matmul(h, w["w_in_t"], "nt", BF16, f"{tag}_in")
    ya = _gmlp_fwd(z, w["gain"], w["wtril"], w["bias_exp"], f"{tag}_gmlp")
    q, k, v = _qk_prep(z, rope[0], rope[1], w["gq"], w["gk"], w["seg"], f"{tag}_qk")
    outs, lses = zip(*[_attn_fwd(q[b], k[b], v[b], dil, f"{tag}_attn_d{dil}") for b, dil in enumerate(DILATIONS)])
    yb, lse, cat = _attn_merge(outs, lses, ya, f"{tag}_merge")
    y, x_new = _matmul(cat, w["w_out"], "nn", BF16, f"{tag}_out", resid=(x, gate))
    return x_new, (x, h, z, q, k, v, yb, lse, cat, y)


def _mixer_backward(dx_new, dy, d_gate, saved, mod, norm_g, w, rope, tag, emit, below):
    x, h, z, q, k, v, yb, lse, cat, _ = saved
    _, sc, _ = mod
    d_w_out = _matmul_tn_acc(cat, dy, f"{tag}_out_dw")
    dcat = _matmul(dy, w["w_out"], "nt", BF16, f"{tag}_out_dx")
    dz_a, d_sp_w, d_gain, d_bias_exp = _gmlp_bwd(z, dcat, w["gain"], w["wtril"], w["wtril_t"], w["bias_exp"], f"{tag}_gmlp_bwd")
    dyb = _subseq_views(dcat, A_WIDTH // B_WIDTH, f"{tag}_dyb_views")
    dqs, dks, dvs = zip(*[_attn_bwd(q[b], k[b], v[b], dyb[b], yb[b], lse[b], dil, f"{tag}_attn_bwd_d{dil}")
                          for b, dil in enumerate(DILATIONS)])
    dz_qkv, d_gq, d_gk = _qk_prep_bwd(z, dqs, dks, dvs, rope[0], rope[1], w["gq"], w["gk"], w["seg"], f"{tag}_qk_bwd")
    dz = jnp.concatenate([dz_a, dz_qkv], axis=1)
    d_w_in_t = _matmul_tn_acc(dz, h, f"{tag}_in_dw")
    token = emit([d_w_in_t, d_w_out])
    dh = _matmul(dz, w["w_in_t"], "nn", F32, f"{tag}_in_dx")
    dx, d_w, d_sh, *dy_below = _modnorm_bwd(x, dh, norm_g, _behind(sc, token), dx_new, below, f"{tag}_norm_bwd")
    return (dx, *dy_below), dict(
        vnorm_g=d_gain.reshape(A_GROUPS, GROUP_DIM), spatial_w=d_sp_w,
        spatial_b=d_bias_exp.reshape(CHUNK, A_GROUPS, GROUP_DIM).sum(-1).T,
        q_norm_g=d_gq.reshape(HEADS, HEAD_DIM).sum(0), k_norm_g=d_gk.reshape(HEADS, HEAD_DIM).sum(0),
        norm_g=d_w * (1.0 + sc), sh=d_sh, sc=d_w * norm_g, gate=d_gate)


def _conformer_forward(x, mod, norm_g, w, tag):
    sh, sc, gate = mod
    h = _modnorm(x, norm_g, sc, sh, f"{tag}_norm")
    p = _matmul(h, w["pw1_t"], "nt", BF16, f"{tag}_pw1", bias=w["pw1_b"])
    s, dc = _conformer_mid(p, w["dw_w"], w["dw_b"], w["ln_g"], w["ln_b"], f"{tag}_mid")
    y, x_new = _matmul(s, w["pw2"], "nn", BF16, f"{tag}_pw2", bias=w["pw2_b"], resid=(x, gate))
    return x_new, (x, h, p, dc, s, y)


def _conformer_backward(dx_new, dy, d_gate, saved, mod, norm_g, w, tag, emit, below):
    x, h, p, dc, s, _ = saved
    _, sc, _ = mod
    d_pw2 = _matmul_tn_acc(s, dy, f"{tag}_pw2_dw")
    d_pw2_b = _colsum_call(dy, f"{tag}_pw2_db")
    ds = _matmul(dy, w["pw2"], "nt", BF16, f"{tag}_pw2_dx")
    ddc, d_dw_w, d_dw_b, d_ln_g, d_ln_b = _conformer_mid_bwd(p, dc, ds, w["ln_g"], w["ln_b"], f"{tag}_mid_bwd")
    dp, d_pw1_b = _conformer_glu_bwd(p, ddc, w["dw_w"], f"{tag}_glu_bwd")
    d_pw1_t = _matmul_tn_acc(dp, h, f"{tag}_pw1_dw")
    token = emit([d_pw1_t, d_pw2])
    dh = _matmul(dp, w["pw1_t"], "nn", F32, f"{tag}_pw1_dx")
    dx, d_w, d_sh, *dy_below = _modnorm_bwd(x, dh, norm_g, _behind(sc, token), dx_new, below, f"{tag}_norm_bwd")
    return (dx, *dy_below), dict(pw1_b=d_pw1_b, dw_w=d_dw_w, dw_b=d_dw_b, ln_g=d_ln_g, ln_b=d_ln_b, pw2_b=d_pw2_b, norm_g=d_w * (1.0 + sc), sh=d_sh, sc=d_w * norm_g, gate=d_gate)


def _local_step(x, target, pos, mod, norm_mix_g, norm_ffn_g, mixer_w, conv_w, ffn_w, fetch, emit):
    d = D_MODEL
    inv_freq = 1.0 / (ROPE_THETA ** (jnp.arange(0, HEAD_DIM, 2, dtype=F32) / HEAD_DIM))
    inv_freq = jnp.tile(inv_freq, 2 * HEADS)[None, :]
    sign = jnp.tile(jnp.concatenate([-jnp.ones(HEAD_DIM // 2, F32), jnp.ones(HEAD_DIM // 2, F32)]), HEADS)[None, :]
    rope = _rope_tables(pos, inv_freq, sign, "rope_tables")
    mods = [[mod[l:l + 1, i * d:(i + 1) * d] for i in range(6)] for l in range(2)]
    mix = [(m[0], m[1], m[2]) for m in mods]
    ffn = [(m[3], m[4], m[5]) for m in mods]
    gm = [norm_mix_g[l:l + 1] for l in range(2)]
    gf = [norm_ffn_g[l:l + 1] for l in range(2)]

    mixer_w = {**mixer_w, **fetch("l0_mix", x)}
    x1, s_mix = _mixer_forward(x, mix[0], gm[0], mixer_w, rope, "l0_mix")
    ffn_w0 = {**ffn_w[0], **fetch("l0_ffn", x1)}
    x2, s_ffn0 = _ffn_forward(x1, ffn[0], gf[0], ffn_w0, "l0_ffn")
    conv_w = {**conv_w, **fetch("l1_conv", x2)}
    x3, s_conv = _conformer_forward(x2, mix[1], gm[1], conv_w, "l1_conv")
    ffn_w1 = {**ffn_w[1], **fetch("l1_ffn", x3)}
    x4, s_ffn1 = _ffn_forward(x3, ffn[1], gf[1], ffn_w1, "l1_ffn")
    below = lambda saved, m: (saved[-1], m[2])
    dx, loss, dy, dg = _loss_head(x4, target, below(s_ffn1, ffn[1]), "loss_head")
    (dx, dy, dg), g_ffn1 = _ffn_backward(dx, dy, dg, s_ffn1, ffn[1], gf[1], ffn_w1, "l1_ffn",
                                         functools.partial(emit, "l1_ffn"), below(s_conv, mix[1]))
    (dx, dy, dg), g_conv = _conformer_backward(dx, dy, dg, s_conv, mix[1], gm[1], conv_w, "l1_conv",
                                               functools.partial(emit, "l1_conv"), below(s_ffn0, ffn[0]))
    (dx, dy, dg), g_ffn0 = _ffn_backward(dx, dy, dg, s_ffn0, ffn[0], gf[0], ffn_w0, "l0_ffn",
                                         functools.partial(emit, "l0_ffn"), below(s_mix, mix[0]))
    (dx,), g_mix = _mixer_backward(dx, dy, dg, s_mix, mix[0], gm[0], mixer_w, rope, "l0_mix",
                                   functools.partial(emit, "l0_mix"), None)
    blocks = [g_mix, g_ffn0, g_conv, g_ffn1]
    dmod = jnp.stack([jnp.concatenate([a["sh"], a["sc"], a["gate"], b["sh"], b["sc"], b["gate"]], axis=1)[0]
                      for a, b in ((g_mix, g_ffn0), (g_conv, g_ffn1))])
    return loss, dx, dmod, blocks


def _pack(arrs, rows=8):
    flat = jnp.concatenate([a.reshape(-1).astype(F32) for a in arrs])
    n = flat.shape[0]
    cols = -(-n // (rows * 128)) * 128
    return jnp.pad(flat, (0, rows * cols - n)).reshape(rows, cols)


def _unpack(flat, shapes):
    out, off = [], 0
    for shp in shapes:
        n = math.prod(shp)
        out.append(flat[..., off:off + n].reshape(flat.shape[:-1] + tuple(shp)))
        off += n
    return out


def _take_block(a, idx, size, axis):
    return lax.dynamic_slice_in_dim(a, idx * size, size, axis)


def kernel(x, c, positions, ada_w, ada_b, norm_mix_g, norm_ffn_g, ab_w_in, a_vnorm_g, a_spatial_w, a_spatial_b, b_q_norm_g, b_k_norm_g, ab_w_out, conv_pw1_w, conv_pw1_b, conv_dw_w, conv_dw_b, conv_ln_g, conv_ln_b, conv_pw2_w, conv_pw2_b, ffn_up_w, ffn_dw_w, ffn_dw_b, ffn_down_w, loss_target, m_ada_w, m_ada_b, m_norm_mix_g, m_norm_ffn_g, m_ab_w_in, m_a_vnorm_g, m_a_spatial_w, m_a_spatial_b, m_b_q_norm_g, m_b_k_norm_g, m_ab_w_out, m_conv_pw1_w, m_conv_pw1_b, m_conv_dw_w, m_conv_dw_b, m_conv_ln_g, m_conv_ln_b, m_conv_pw2_w, m_conv_pw2_b, m_ffn_up_w, m_ffn_dw_w, m_ffn_dw_b, m_ffn_down_w, v_ada_w, v_ada_b, v_norm_mix_g, v_norm_ffn_g, v_ab_w_in, v_a_vnorm_g, v_a_spatial_w, v_a_spatial_b, v_b_q_norm_g, v_b_k_norm_g, v_ab_w_out, v_conv_pw1_w, v_conv_pw1_b, v_conv_dw_w, v_conv_dw_b, v_conv_ln_g, v_conv_ln_b, v_conv_pw2_w, v_conv_pw2_b, v_ffn_up_w, v_ffn_dw_w, v_ffn_dw_b, v_ffn_down_w):
    weights = dict(ada_w=ada_w, ada_b=ada_b, norm_mix_g=norm_mix_g, norm_ffn_g=norm_ffn_g, ab_w_in=ab_w_in, a_vnorm_g=a_vnorm_g, a_spatial_w=a_spatial_w, a_spatial_b=a_spatial_b, b_q_norm_g=b_q_norm_g, b_k_norm_g=b_k_norm_g, ab_w_out=ab_w_out, conv_pw1_w=conv_pw1_w, conv_pw1_b=conv_pw1_b, conv_dw_w=conv_dw_w, conv_dw_b=conv_dw_b, conv_ln_g=conv_ln_g, conv_ln_b=conv_ln_b, conv_pw2_w=conv_pw2_w, conv_pw2_b=conv_pw2_b, ffn_up_w=ffn_up_w, ffn_dw_w=ffn_dw_w, ffn_dw_b=ffn_dw_b, ffn_down_w=ffn_down_w)
    mom1 = dict(ada_w=m_ada_w, ada_b=m_ada_b, norm_mix_g=m_norm_mix_g, norm_ffn_g=m_norm_ffn_g, ab_w_in=m_ab_w_in, a_vnorm_g=m_a_vnorm_g, a_spatial_w=m_a_spatial_w, a_spatial_b=m_a_spatial_b, b_q_norm_g=m_b_q_norm_g, b_k_norm_g=m_b_k_norm_g, ab_w_out=m_ab_w_out, conv_pw1_w=m_conv_pw1_w, conv_pw1_b=m_conv_pw1_b, conv_dw_w=m_conv_dw_w, conv_dw_b=m_conv_dw_b, conv_ln_g=m_conv_ln_g, conv_ln_b=m_conv_ln_b, conv_pw2_w=m_conv_pw2_w, conv_pw2_b=m_conv_pw2_b, ffn_up_w=m_ffn_up_w, ffn_dw_w=m_ffn_dw_w, ffn_dw_b=m_ffn_dw_b, ffn_down_w=m_ffn_down_w)
    mom2 = dict(ada_w=v_ada_w, ada_b=v_ada_b, norm_mix_g=v_norm_mix_g, norm_ffn_g=v_norm_ffn_g, ab_w_in=v_ab_w_in, a_vnorm_g=v_a_vnorm_g, a_spatial_w=v_a_spatial_w, a_spatial_b=v_a_spatial_b, b_q_norm_g=v_b_q_norm_g, b_k_norm_g=v_b_k_norm_g, ab_w_out=v_ab_w_out, conv_pw1_w=v_conv_pw1_w, conv_pw1_b=v_conv_pw1_b, conv_dw_w=v_conv_dw_w, conv_dw_b=v_conv_dw_b, conv_ln_g=v_conv_ln_g, conv_ln_b=v_conv_ln_b, conv_pw2_w=v_conv_pw2_w, conv_pw2_b=v_conv_pw2_b, ffn_up_w=v_ffn_up_w, ffn_dw_w=v_ffn_dw_w, ffn_dw_b=v_ffn_dw_b, ffn_down_w=v_ffn_down_w)
    order = list(weights)
    d, f2 = D_MODEL, 2 * FFN_DIM
    t = x.shape[1]
    me = 4 * lax.axis_index("x") + 2 * lax.axis_index("y") + lax.axis_index("c")
    for window, dil in PATTERNS:
        assert window // dil == Q_BLOCK and t % (dil * Q_BLOCK) == 0

    small_in = [c[0], conv_pw1_b[0], conv_dw_w[0], conv_dw_b[0], conv_ln_g[0], conv_ln_b[0], conv_pw2_b[0], ffn_dw_w]
    g1 = _all_gather_vmem(_pack(small_in, rows=8), "gather_small").reshape(N_DEV, -1)
    c_all, pw1_b, dw_w, dw_b, ln_g, ln_b, pw2_b, fdw_w = _unpack(g1, [a.shape for a in small_in])
    pw1_b, dw_b, ln_g, ln_b, pw2_b = [a.reshape(1, -1) for a in (pw1_b, dw_b, ln_g, ln_b, pw2_b)]
    dw_w = dw_w.transpose(1, 0, 2).reshape(CONV_WIDTH, d)
    fdw_w = fdw_w.transpose(1, 2, 0, 3).reshape(2, FFN_CONV_WIDTH, f2)

    c16 = jnp.pad(c_all, ((0, 2 * N_DEV - c_all.shape[0]), (0, 0)))
    part = jnp.concatenate([_ada_fwd(c16, ada_w[l], f"ada_fwd{l}")[:N_DEV] for l in range(2)], axis=1)
    g2 = _all_gather_vmem(part, "gather_mod").reshape(N_DEV, N_DEV, 2, -1)
    mod = lax.dynamic_index_in_dim(g2, me, axis=1, keepdims=False).transpose(1, 0, 2).reshape(2, 6 * d) + ada_b

    stages = dict(l0_mix=[ab_w_in[0].T, ab_w_out[0]], l0_ffn=[ffn_up_w[0].T, ffn_down_w[0]],
                  l1_conv=[conv_pw1_w[0].T, conv_pw2_w[0]], l1_ffn=[ffn_up_w[1].T, ffn_down_w[1]])
    stages = {k: [s.astype(BF16) for s in v] for k, v in stages.items()}
    names = dict(l0_mix=("w_in_t", "w_out"), l0_ffn=("up_t", "down"), l1_conv=("pw1_t", "pw2"), l1_ffn=("up_t", "down"))
    ready = {"l0_mix": [a.reshape(-1, d) for a in _all_gather_hbm(stages["l0_mix"], "gather_mixer_weights")]}
    behind = (ready, mod)
    arriving = {}
    for stage, group in (("l0_ffn", ("l0_ffn",)), ("l1_conv", ("l1_conv", "l1_ffn"))):
        srcs, _ = lax.optimization_barrier(([s for g in group for s in stages[g]], behind))
        arriving[stage], behind = _pushes_start(
            srcs, _landing_zones(srcs, f"gather_{stage}_zones"), False, f"gather_{stage}_start")
        mod = mod + behind[0:1, 0:1]

    def fetch(stage, after):
        if stage in arriving:
            full = [a.reshape(-1, d) for a in _pushes_wait(arriving[stage], after, f"gather_{stage}_wait")[1]]
            ready[stage] = full[:2]
            if stage == "l1_conv":
                ready["l1_ffn"] = full[2:]
        return dict(zip(names[stage], ready[stage]))

    causal = jnp.tril(jnp.ones((CHUNK, CHUNK), bool))
    wtril = jnp.where(causal[None], a_spatial_w[0], 0.0)
    mixer_w = dict(
        gain=a_vnorm_g[0].reshape(1, A_WIDTH), wtril=wtril.astype(BF16),
        wtril_t=wtril.transpose(0, 2, 1).astype(BF16),
        bias_exp=jnp.repeat(a_spatial_b[0].T, GROUP_DIM, axis=1),
        gq=jnp.tile(b_q_norm_g[0], HEADS)[None, :], gk=jnp.tile(b_k_norm_g[0], HEADS)[None, :],
        seg=jnp.kron(jnp.eye(HEADS, dtype=BF16), jnp.ones((HEAD_DIM, HEAD_DIM), BF16)))
    conv_w = dict(pw1_b=pw1_b, dw_w=dw_w, dw_b=dw_b, ln_g=ln_g, ln_b=ln_b, pw2_b=pw2_b)
    ffn_w = [dict(dw_w=fdw_w[l].reshape(FFN_CONV_WIDTH, 2, FFN_DIM).transpose(1, 0, 2), dw_b=ffn_dw_b[l].reshape(2, 1, FFN_DIM))
             for l in range(2)]

    leaving = {}

    def emit(stage, grads):
        blocks = [g.reshape(N_DEV, g.shape[0] // N_DEV, d) for g in grads]
        leaving[stage], token = _pushes_start(
            blocks, [lax.empty(b.shape, b.dtype) for b in blocks], True, f"reduce_{stage}_start")
        return token

    loss, dx, dmod, (g_mix, g_ffn0, g_conv, g_ffn1) = _local_step(
        x[0], loss_target[0], positions[0].astype(F32)[:, None], mod, norm_mix_g, norm_ffn_g, mixer_w, conv_w, ffn_w,
        fetch, emit)

    me_op = me.astype(jnp.int32).reshape(1)

    def reduced(stage, after):
        blocks, lands = _pushes_wait(leaving[stage], after, f"reduce_{stage}_wait")
        return [_sum_with_own(b, a, me_op, f"reduce_{stage}_sum{i}") for i, (b, a) in enumerate(zip(blocks, lands))]

    (r_up_t1, r_down1), (r_pw1_t, r_pw2), (r_up_t0, r_down0) = [reduced(s, dx) for s in ("l1_ffn", "l1_conv", "l0_ffn")]

    small_g = [
        dmod, jnp.concatenate([g_mix["norm_g"], g_conv["norm_g"]]), jnp.concatenate([g_ffn0["norm_g"], g_ffn1["norm_g"]]),
        g_mix["vnorm_g"], g_mix["spatial_w"], g_mix["spatial_b"], g_mix["q_norm_g"], g_mix["k_norm_g"],
        g_conv["pw1_b"], g_conv["dw_w"], g_conv["dw_b"], g_conv["ln_g"], g_conv["ln_b"], g_conv["pw2_b"],
        jnp.stack([g_ffn0["dw_w"], g_ffn1["dw_w"]]), jnp.concatenate([g_ffn0["dw_b"], g_ffn1["dw_b"]])]
    packed = _pack(small_g, rows=8)
    small_leaving, _ = _pushes_start([packed], [lax.empty((N_DEV,) + packed.shape, F32)], False, "gather_small_grads_start")

    grads = dict(conv_pw2_w=r_pw2[None], ffn_down_w=jnp.stack([r_down0, r_down1]))
    grads_t = dict(conv_pw1_w=r_pw1_t[None], ffn_up_w=jnp.stack([r_up_t0, r_up_t1]))
    flip = lambda a: jnp.swapaxes(a, 1, 2)
    delta, new_m, new_v = {}, {}, {}

    def update(name):
        if name in grads_t:
            grads[name] = flip(grads_t[name])
            res = _adamw(flip(weights[name]), grads_t[name], flip(mom1[name]), flip(mom2[name]), f"adamw_{name}")
            delta[name], new_m[name], new_v[name] = [flip(r) for r in res]
        else:
            delta[name], new_m[name], new_v[name] = _adamw(weights[name], grads[name], mom1[name], mom2[name], f"adamw_{name}")

    for name in ("conv_pw1_w", "conv_pw2_w", "ffn_up_w", "ffn_down_w"):
        update(name)
    r_in_t, r_out = reduced("l0_mix", new_v["ffn_down_w"])
    grads_t["ab_w_in"], grads["ab_w_out"] = r_in_t[None], r_out[None]
    update("ab_w_in")
    update("ab_w_out")

    (packed,), (landed,) = _pushes_wait(small_leaving, new_v["ab_w_out"], "gather_small_grads_wait")
    total = _sum_in_device_order(packed, landed, me_op, "sum_small_grads")
    (s_dmod, s_mix_g, s_ffn_g, s_vnorm, s_sp_w, s_sp_b, s_gq, s_gk, s_pw1_b, s_dw_w, s_dw_b, s_ln_g, s_ln_b,
     s_pw2_b, s_fdw_w, s_fdw_b) = _unpack(total.reshape(-1), [a.shape for a in small_g])
    dmod_all = lax.dynamic_update_slice(
        landed.reshape(N_DEV, -1)[:, :dmod.size].reshape((N_DEV,) + dmod.shape), dmod[None], (me, 0, 0))
    n_ada = ada_w.shape[2]
    dmod16 = jnp.pad(_take_block(dmod_all, me, n_ada, 2), ((0, N_DEV), (0, 0), (0, 0)))
    grads.update(
        ada_w=jnp.stack([_ada_bwd(c16, dmod16[:, l], f"ada_bwd{l}") for l in range(2)]),
        ada_b=s_dmod, norm_mix_g=s_mix_g, norm_ffn_g=s_ffn_g,
        a_vnorm_g=s_vnorm[None], a_spatial_w=s_sp_w[None], a_spatial_b=s_sp_b[None], b_q_norm_g=s_gq[None],
        b_k_norm_g=s_gk[None],
        conv_pw1_b=_take_block(s_pw1_b, me, conv_pw1_b.shape[1], 1),
        conv_dw_w=_take_block(s_dw_w, me, conv_dw_w.shape[2], 1)[None],
        conv_dw_b=_take_block(s_dw_b, me, conv_dw_b.shape[1], 1), conv_ln_g=_take_block(s_ln_g, me, conv_ln_g.shape[1], 1),
        conv_ln_b=_take_block(s_ln_b, me, conv_ln_b.shape[1], 1),
        conv_pw2_b=_take_block(s_pw2_b, me, conv_pw2_b.shape[1], 1),
        ffn_dw_w=_take_block(s_fdw_w, me, ffn_dw_w.shape[2], 2), ffn_dw_b=s_fdw_b)
    update("ada_w")
    large = ("ada_w", "conv_pw1_w", "conv_pw2_w", "ffn_up_w", "ffn_down_w", "ab_w_in", "ab_w_out")
    small = [n for n in order if n not in large]
    res = _adamw_small(*[[src[n] for n in small] for src in (weights, grads, mom1, mom2)], "adamw_small")
    for dst, arrs in zip((delta, new_m, new_v), res):
        dst.update(zip(small, arrs))

    loss = lax.psum(loss[0, 0], ("x", "y", "c"))
    return (loss, dx[None], *[grads[n] for n in order], *[delta[n] for n in order],
            *[new_m[n] for n in order], *[new_v[n] for n in order])
```

```python
import functools
import math

import jax
import jax.numpy as jnp
from jax import lax
from jax.experimental import pallas as pl
from jax.experimental.pallas import tpu as pltpu

F32 = jnp.float32
BF16 = jnp.bfloat16
MESH = pl.DeviceIdType.MESH

D_MODEL = 1024
A_WIDTH = 512
A_GROUPS = 4
GROUP_DIM = 128
CHUNK = 128
B_WIDTH = 512
HEADS = 8
HEAD_DIM = 64
PATTERNS = ((128, 1), (512, 4), (2048, 16))
Q_BLOCK = 128
ROPE_THETA = 10000.0
AB_IN = 2560
CONV_WIDTH = 31
FFN_DIM = 2816
FFN_CONV_WIDTH = 3
EPS = 1e-6
NEG = -1e30
N_DEV = 8
ADAM_LR, ADAM_B1, ADAM_B2, ADAM_EPS, ADAM_WD, ADAM_STEP = 0.001, 0.9, 0.999, 1e-08, 0.01, 10

V7X_VMEM_LIMIT = 56 * 2**20
FFN_HALO = 16
CONV_HALO = 32

_NN = (((1,), (0,)), ((), ()))
_NT = (((1,), (1,)), ((), ()))
_TN = (((0,), (0,)), ((), ()))


def _tile(n, prefs=(512, 256, 128)):
    for t in prefs:
        if n % t == 0:
            return t
    return n


def _row_tile(n, cap=512):
    best = n
    for t in range(8, min(n, cap) + 1, 8):
        if n % t == 0:
            best = t
    return best if best <= cap else n


def _params(*sem):
    return pltpu.CompilerParams(dimension_semantics=sem, vmem_limit_bytes=V7X_VMEM_LIMIT)


def _dot(a, b, dims):
    return lax.dot_general(a, b, dims, preferred_element_type=F32)


def _sigmoid(x):
    return 1.0 / (1.0 + jnp.exp(-x))


def _gelu(x):
    return 0.5 * x * (1.0 + lax.erf(x * (2.0 ** -0.5)))


def _gelu_grad(x):
    return 0.5 * (1.0 + lax.erf(x * (2.0 ** -0.5))) + x * jnp.exp(-0.5 * x * x) * (1.0 / math.sqrt(2.0 * math.pi))


def _colsum(v):
    return jnp.sum(v, axis=0, keepdims=True)


MATMUL_VMEM_BUDGET = 40 * 2**20


def _matmul_tiles(m, n, k, out_bytes, with_resid):
    def options(dim):
        opts = [t for t in (1024, 512, 256, 128) if dim % t == 0]
        return opts + [dim] if dim <= 4096 and dim not in opts else opts

    best = None
    for tm in options(m):
        for tn in options(n):
            need = 4 * (tm * k + k * tn) + tm * tn * (4 + 2 * out_bytes) + (24 * tm * tn if with_resid else 0)
            if need <= MATMUL_VMEM_BUDGET and (best is None or tm * tn / (tm + tn) > best[0]):
                best = (tm * tn / (tm + tn), tm, tn)
    return best[1], best[2]


def _matmul_tn_acc(a, b, name, tk=1024):
    squeeze = a.ndim == 2
    a3 = a[None] if squeeze else a
    p_, t, m = a3.shape
    n = b.shape[1]
    nk = t // tk

    def body(a_ref, b_ref, o_ref, acc_ref):
        kt = pl.program_id(1)

        @pl.when(kt == 0)
        def _():
            acc_ref[...] = jnp.zeros_like(acc_ref)

        acc_ref[...] += _dot(a_ref[...], b_ref[...], _TN)

        @pl.when(kt == nk - 1)
        def _():
            o_ref[...] = acc_ref[...].astype(BF16)

    out = pl.pallas_call(
        body, name=name, grid=(p_, nk),
        in_specs=[pl.BlockSpec((None, tk, m), lambda p, kt: (p, kt, 0)), pl.BlockSpec((tk, n), lambda p, kt: (kt, 0))],
        out_specs=pl.BlockSpec((None, m, n), lambda p, kt: (p, 0, 0)), out_shape=jax.ShapeDtypeStruct((p_, m, n), BF16),
        scratch_shapes=[pltpu.VMEM((m, n), F32)], compiler_params=_params("parallel", "arbitrary"),
    )(a3, b)
    return out[0] if squeeze else out


def _matmul(a, b, mode, out_dtype, name, bias=None, resid=None):
    if mode == "nn":
        (m, k), (_, n) = a.shape, b.shape
    elif mode == "nt":
        (m, k), (n, _) = a.shape, b.shape
    else:
        (k, m), (_, n) = a.shape, b.shape
    tm, tn = _matmul_tiles(m, n, k, jnp.dtype(out_dtype).itemsize, resid is not None)
    dims = {"nn": _NN, "nt": _NT, "tn": _TN}[mode]
    a_spec = pl.BlockSpec((k, tm), lambda i, j: (0, i)) if mode == "tn" else pl.BlockSpec((tm, k), lambda i, j: (i, 0))
    b_spec = pl.BlockSpec((tn, k), lambda i, j: (j, 0)) if mode == "nt" else pl.BlockSpec((k, tn), lambda i, j: (0, j))
    in_specs, args = [a_spec, b_spec], [a, b]
    row_spec = pl.BlockSpec((1, tn), lambda i, j: (0, j))
    tile_spec = pl.BlockSpec((tm, tn), lambda i, j: (i, j))
    if bias is not None:
        in_specs.append(row_spec)
        args.append(bias)
    if resid is not None:
        in_specs += [tile_spec, row_spec]
        args += list(resid)
    out_shape = [jax.ShapeDtypeStruct((m, n), out_dtype)]
    out_specs = [tile_spec]
    if resid is not None:
        out_shape.append(jax.ShapeDtypeStruct((m, n), F32))
        out_specs.append(tile_spec)

    def body(*refs):
        a_ref, b_ref = refs[0], refs[1]
        pos = 2
        acc = _dot(a_ref[...], b_ref[...], dims)
        if bias is not None:
            acc = acc + refs[pos][...]
            pos += 1
        if resid is not None:
            x_ref, g_ref = refs[pos], refs[pos + 1]
            pos += 2
        refs[pos][...] = acc.astype(out_dtype)
        if resid is not None:
            refs[pos + 1][...] = x_ref[...] + g_ref[...] * acc

    outs = pl.pallas_call(
        body, name=name, grid=(m // tm, n // tn), in_specs=in_specs, out_specs=out_specs, out_shape=out_shape,
        compiler_params=_params("parallel", "parallel"),
    )(*args)
    return outs if resid is not None else outs[0]


def _modnorm(x, g, sc, sh, name):
    t, d = x.shape
    tm = _tile(t)
    row = pl.BlockSpec((1, d), lambda i: (0, 0))
    blk = pl.BlockSpec((tm, d), lambda i: (i, 0))

    def body(x_ref, g_ref, sc_ref, sh_ref, o_ref):
        x = x_ref[...]
        r = lax.rsqrt(jnp.mean(x * x, axis=-1, keepdims=True) + EPS)
        o_ref[...] = ((x * r) * g_ref[...] * (1.0 + sc_ref[...]) + sh_ref[...]).astype(BF16)

    return pl.pallas_call(
        body, name=name, grid=(t // tm,), in_specs=[blk, row, row, row], out_specs=blk,
        out_shape=jax.ShapeDtypeStruct((t, d), BF16), compiler_params=_params("parallel"),
    )(x, g, sc, sh)


def _gate_bwd_tile(dx, y_ref, gate_ref, dy_ref, dgate_ref, first):
    @pl.when(first)
    def _():
        dgate_ref[...] = jnp.zeros_like(dgate_ref)

    dy_ref[...] = (dx * gate_ref[...]).astype(BF16)
    dgate_ref[...] += _colsum(dx * y_ref[...].astype(F32))


def _modnorm_bwd(x, dh, g, sc, dres, below, name):
    t, d = x.shape
    tm = _tile(t)
    row = pl.BlockSpec((1, d), lambda i: (0, 0))
    blk = pl.BlockSpec((tm, d), lambda i: (i, 0))

    def body(x_ref, dh_ref, g_ref, sc_ref, dres_ref, *rest):
        dx_ref, dw_ref, dsh_ref = rest[-5:-2] if below else rest
        first = pl.program_id(0) == 0

        @pl.when(first)
        def _():
            dw_ref[...] = jnp.zeros_like(dw_ref)
            dsh_ref[...] = jnp.zeros_like(dsh_ref)

        x = x_ref[...]
        dh = dh_ref[...].astype(F32)
        r = lax.rsqrt(jnp.mean(x * x, axis=-1, keepdims=True) + EPS)
        xn = x * r
        dxn = dh * (g_ref[...] * (1.0 + sc_ref[...]))
        dx = dres_ref[...] + r * (dxn - xn * jnp.mean(dxn * xn, axis=-1, keepdims=True))
        dx_ref[...] = dx
        dw_ref[...] += _colsum(dh * xn)
        dsh_ref[...] += _colsum(dh)
        if below:
            _gate_bwd_tile(dx, rest[0], rest[1], rest[-2], rest[-1], first)

    row_out = jax.ShapeDtypeStruct((1, d), F32)
    return pl.pallas_call(
        body, name=name, grid=(t // tm,), in_specs=[blk, blk, row, row, blk] + ([blk, row] if below else []),
        out_specs=[blk, row, row] + ([blk, row] if below else []),
        out_shape=[jax.ShapeDtypeStruct((t, d), F32), row_out, row_out]
        + ([jax.ShapeDtypeStruct((t, d), BF16), row_out] if below else []),
        compiler_params=_params("arbitrary"),
    )(x, dh, g, sc, dres, *(below or ()))


def _loss_head(y, target, below, name):
    t, d = y.shape
    tm = _tile(t)
    blk = pl.BlockSpec((tm, d), lambda i: (i, 0))
    row = pl.BlockSpec((1, d), lambda i: (0, 0))
    one = pl.BlockSpec((1, 1), lambda i: (0, 0))
    steps = t // tm

    def body(y_ref, t_ref, yb_ref, gate_ref, dx_ref, loss_ref, dy_ref, dgate_ref, acc_ref):
        first = pl.program_id(0) == 0

        @pl.when(first)
        def _():
            acc_ref[...] = jnp.zeros_like(acc_ref)

        e = y_ref[...] - t_ref[...]
        dx = e * (1.0 / d)
        dx_ref[...] = dx
        acc_ref[...] += _colsum(e * e)
        _gate_bwd_tile(dx, yb_ref, gate_ref, dy_ref, dgate_ref, first)

        @pl.when(pl.program_id(0) == steps - 1)
        def _():
            loss_ref[...] = jnp.sum(acc_ref[...], axis=1, keepdims=True) * (0.5 / d)

    return pl.pallas_call(
        body, name=name, grid=(steps,), in_specs=[blk, blk, blk, row], out_specs=[blk, one, blk, row],
        out_shape=[jax.ShapeDtypeStruct((t, d), F32), jax.ShapeDtypeStruct((1, 1), F32),
                   jax.ShapeDtypeStruct((t, d), BF16), jax.ShapeDtypeStruct((1, d), F32)],
        scratch_shapes=[pltpu.VMEM((1, d), F32)], compiler_params=_params("arbitrary"),
    )(y, target, *below)


def _group_norm(vg, gain):
    mu = jnp.mean(vg, axis=-1, keepdims=True)
    xc = vg - mu
    rstd = lax.rsqrt(jnp.mean(xc * xc, axis=-1, keepdims=True) + EPS)
    xhat = xc * rstd
    return xhat, rstd, xhat * gain


def _gmlp_fwd(z, gain, wtril, bias_exp, name):
    t = z.shape[0]
    zu = pl.BlockSpec((CHUNK, A_WIDTH), lambda i: (i, 0))
    zv = pl.BlockSpec((CHUNK, A_WIDTH), lambda i: (i, 1))
    full2 = lambda shp: pl.BlockSpec(shp, lambda i: (0, 0))
    w_spec = pl.BlockSpec((A_GROUPS, CHUNK, CHUNK), lambda i: (0, 0, 0))

    def body(zu_ref, zv_ref, gain_ref, w_ref, b_ref, ya_ref):
        ua = _gelu(zu_ref[...].astype(F32))
        vg = _gelu(zv_ref[...].astype(F32))
        for g in range(A_GROUPS):
            sl = slice(g * GROUP_DIM, (g + 1) * GROUP_DIM)
            _, _, vn = _group_norm(vg[:, sl], gain_ref[:, sl])
            f = _dot(w_ref[g], vn.astype(BF16), _NN) + b_ref[:, sl]
            ya_ref[:, sl] = (ua[:, sl] * f).astype(BF16)

    return pl.pallas_call(
        body, name=name, grid=(t // CHUNK,),
        in_specs=[zu, zv, full2((1, A_WIDTH)), w_spec, full2((CHUNK, A_WIDTH))], out_specs=zu,
        out_shape=jax.ShapeDtypeStruct((t, A_WIDTH + B_WIDTH), BF16), compiler_params=_params("parallel"),
    )(z, z, gain, wtril, bias_exp)


def _gmlp_bwd(z, dcat, gain, wtril, wtril_t, bias_exp, name):
    t = z.shape[0]
    zu = pl.BlockSpec((CHUNK, A_WIDTH), lambda i: (i, 0))
    zv = pl.BlockSpec((CHUNK, A_WIDTH), lambda i: (i, 1))
    full2 = lambda shp: pl.BlockSpec(shp, lambda i: (0, 0))
    w_spec = pl.BlockSpec((A_GROUPS, CHUNK, CHUNK), lambda i: (0, 0, 0))
    dz_spec = pl.BlockSpec((CHUNK, 2 * A_WIDTH), lambda i: (i, 0))

    def body(zu_ref, zv_ref, dya_ref, gain_ref, w_ref, wt_ref, b_ref, dz_ref, dw_ref, dgain_ref, dbias_ref):
        @pl.when(pl.program_id(0) == 0)
        def _():
            dw_ref[...] = jnp.zeros_like(dw_ref)
            dgain_ref[...] = jnp.zeros_like(dgain_ref)
            dbias_ref[...] = jnp.zeros_like(dbias_ref)

        zu_v = zu_ref[...].astype(F32)
        zv_v = zv_ref[...].astype(F32)
        dya = dya_ref[...].astype(F32)
        ua = _gelu(zu_v)
        vg = _gelu(zv_v)
        row = lax.broadcasted_iota(jnp.int32, (CHUNK, CHUNK), 0)
        col = lax.broadcasted_iota(jnp.int32, (CHUNK, CHUNK), 1)
        for g in range(A_GROUPS):
            sl = slice(g * GROUP_DIM, (g + 1) * GROUP_DIM)
            gain_g = gain_ref[:, sl]
            xhat, rstd, vn = _group_norm(vg[:, sl], gain_g)
            vn16 = vn.astype(BF16)
            f = _dot(w_ref[g], vn16, _NN) + b_ref[:, sl]
            df = dya[:, sl] * ua[:, sl]
            df16 = df.astype(BF16)
            dz_ref[:, sl] = (dya[:, sl] * f * _gelu_grad(zu_v[:, sl])).astype(BF16)
            dw_ref[g] += jnp.where(row >= col, _dot(df16, vn16, _NT), 0.0)
            dvn = _dot(wt_ref[g], df16, _NN)
            dgain_ref[:, sl] += _colsum(dvn * xhat)
            dxh = dvn * gain_g
            dvg = rstd * (dxh - jnp.mean(dxh, axis=-1, keepdims=True) - xhat * jnp.mean(dxh * xhat, axis=-1, keepdims=True))
            dz_ref[:, A_WIDTH + g * GROUP_DIM:A_WIDTH + (g + 1) * GROUP_DIM] = (dvg * _gelu_grad(zv_v[:, sl])).astype(BF16)
            dbias_ref[:, sl] += df

    return pl.pallas_call(
        body, name=name, grid=(t // CHUNK,),
        in_specs=[zu, zv, zu, full2((1, A_WIDTH)), w_spec, w_spec, full2((CHUNK, A_WIDTH))],
        out_specs=[dz_spec, w_spec, full2((1, A_WIDTH)), full2((CHUNK, A_WIDTH))],
        out_shape=[jax.ShapeDtypeStruct((t, 2 * A_WIDTH), BF16), jax.ShapeDtypeStruct((A_GROUPS, CHUNK, CHUNK), F32),
                   jax.ShapeDtypeStruct((1, A_WIDTH), F32), jax.ShapeDtypeStruct((CHUNK, A_WIDTH), F32)],
        compiler_params=_params("arbitrary"),
    )(z, z, dcat, gain, wtril, wtril_t, bias_exp)


def _rope_tables(pos, inv_freq, sign, name):
    t = pos.shape[0]
    tm = _tile(t)
    row = pl.BlockSpec((1, B_WIDTH), lambda i: (0, 0))
    blk = pl.BlockSpec((tm, B_WIDTH), lambda i: (i, 0))

    def body(pos_ref, f_ref, s_ref, cos_ref, sin_ref):
        ang = pos_ref[...] * f_ref[:, 0:LANES]
        cos_ref[...] = jnp.tile(jnp.cos(ang), (1, B_WIDTH // LANES))
        sin_ref[...] = jnp.tile(jnp.sin(ang) * s_ref[:, 0:LANES], (1, B_WIDTH // LANES))

    return pl.pallas_call(
        body, name=name, grid=(t // tm,), in_specs=[pl.BlockSpec((tm, 1), lambda i: (i, 0)), row, row],
        out_specs=[blk, blk], out_shape=[jax.ShapeDtypeStruct((t, B_WIDTH), F32)] * 2,
        compiler_params=_params("parallel"),
    )(pos, inv_freq, sign)


def _head_sum(v, seg):
    hi = v.astype(BF16)
    lo = (v - hi.astype(F32)).astype(BF16)
    return _dot(hi, seg, _NN) + _dot(lo, seg, _NN)


def _swap_halves(v):
    lane = lax.broadcasted_iota(jnp.int32, v.shape, 1)
    return jnp.where((lane & (HEAD_DIM - 1)) < HEAD_DIM // 2,pltpu.roll(v, B_WIDTH - HEAD_DIM // 2, 1), pltpu.roll(v, HEAD_DIM // 2, 1))


DILATIONS = tuple(dil for _, dil in PATTERNS)
SUBSEQ_TM = 256
LANES = 128


def _subseq_shape(t, dil):
    return (t // dil, dil * B_WIDTH)


def _subseq_spec(tm, dil):
    return pl.BlockSpec((tm // dil, dil * B_WIDTH), lambda i: (i, 0))


def _to_subseq(x, scr_ref, dil):
    if dil == 1:
        return x
    tm, w = x.shape
    for c in range(w // LANES):
        scr_ref[c * tm:(c + 1) * tm, :] = x[:, c * LANES:(c + 1) * LANES]
    return jnp.concatenate([scr_ref[pl.ds(c * tm + r, tm // dil, stride=dil), :]
                            for r in range(dil) for c in range(w // LANES)], axis=1)


def _from_subseq(y, scr_ref, dil):
    if dil == 1:
        return y
    n, w = y.shape[0], y.shape[1] // dil
    tm = n * dil
    for r in range(dil):
        for c in range(w // LANES):
            scr_ref[pl.ds(c * tm + r, n, stride=dil), :] = y[:, r * w + c * LANES:r * w + (c + 1) * LANES]
    return jnp.concatenate([scr_ref[c * tm:(c + 1) * tm, :] for c in range(w // LANES)], axis=1)


def _subseq_scratch(tm):
    return pltpu.VMEM((B_WIDTH // LANES * tm, LANES), F32)


def _qk_prep(z, cos_t, sin_t, gq, gk, seg, name):
    t = z.shape[0]
    tm = _tile(t, (SUBSEQ_TM,))
    col = lambda c: pl.BlockSpec((tm, B_WIDTH), lambda i: (i, c))
    row = pl.BlockSpec((1, B_WIDTH), lambda i: (0, 0))
    blk = col(0)
    nd = len(DILATIONS)

    def body(q_ref, k_ref, v_ref, cos_ref, sin_ref, gq_ref, gk_ref, seg_ref, *rest):
        out_refs, scr_ref = rest[:-1], rest[-1]

        def norm_rot(x, g):
            r = lax.rsqrt(_head_sum(x * x, seg_ref[...]) * (1.0 / HEAD_DIM) + EPS)
            xn = x * r * g
            return xn * cos_ref[...] + _swap_halves(xn) * sin_ref[...]

        vals = (norm_rot(q_ref[...].astype(F32), gq_ref[...]), norm_rot(k_ref[...].astype(F32), gk_ref[...]),
                v_ref[...].astype(F32))
        for a, val in enumerate(vals):
            for b, dil in enumerate(DILATIONS):
                out_refs[a * nd + b][...] = _to_subseq(val, scr_ref, dil).astype(BF16)

    outs = pl.pallas_call(
        body, name=name, grid=(t // tm,),
        in_specs=[col(2), col(3), col(4), blk, blk, row, row, pl.BlockSpec((B_WIDTH, B_WIDTH), lambda i: (0, 0))],
        out_specs=[_subseq_spec(tm, dil) for _ in range(3) for dil in DILATIONS],
        out_shape=[jax.ShapeDtypeStruct(_subseq_shape(t, dil), BF16) for _ in range(3) for dil in DILATIONS],
        scratch_shapes=[_subseq_scratch(tm)], compiler_params=_params("parallel"),
    )(z, z, z, cos_t, sin_t, gq, gk, seg)
    return outs[:nd], outs[nd:2 * nd], outs[2 * nd:]


def _qk_prep_bwd(z, dqs, dks, dvs, cos_t, sin_t, gq, gk, seg, name):
    t = z.shape[0]
    tm = _tile(t, (SUBSEQ_TM,))
    col = lambda c: pl.BlockSpec((tm, B_WIDTH), lambda i: (i, c))
    row = pl.BlockSpec((1, B_WIDTH), lambda i: (0, 0))
    blk = col(0)
    nb = len(DILATIONS)
    subs = [_subseq_spec(tm, dil) for dil in DILATIONS]

    def body(*refs):
        q_ref, k_ref = refs[0], refs[1]
        dq_refs, dk_refs, dv_refs = refs[2:2 + nb], refs[2 + nb:2 + 2 * nb], refs[2 + 2 * nb:2 + 3 * nb]
        cos_ref, sin_ref, gq_ref, gk_ref, seg_ref, dz_ref, dgq_ref, dgk_ref, scr_ref = refs[2 + 3 * nb:]

        @pl.when(pl.program_id(0) == 0)
        def _():
            dgq_ref[...] = jnp.zeros_like(dgq_ref)
            dgk_ref[...] = jnp.zeros_like(dgk_ref)

        def total(d_refs):
            return sum(_from_subseq(r_[...], scr_ref, dil) for r_, dil in zip(d_refs, DILATIONS))

        def back(x, d_refs, g, dg_ref):
            dout = total(d_refs)
            dy = dout * cos_ref[...] + _swap_halves(dout * sin_ref[...])
            r = lax.rsqrt(_head_sum(x * x, seg_ref[...]) * (1.0 / HEAD_DIM) + EPS)
            xn = x * r
            dg_ref[...] += _colsum(dy * xn)
            dxn = dy * g
            return r * (dxn - xn * (_head_sum(dxn * xn, seg_ref[...]) * (1.0 / HEAD_DIM)))

        dz_ref[:, 0:B_WIDTH] = back(q_ref[...].astype(F32), dq_refs, gq_ref[...], dgq_ref).astype(BF16)
        dz_ref[:, B_WIDTH:2 * B_WIDTH] = back(k_ref[...].astype(F32), dk_refs, gk_ref[...], dgk_ref).astype(BF16)
        dz_ref[:, 2 * B_WIDTH:3 * B_WIDTH] = total(dv_refs).astype(BF16)

    return pl.pallas_call(
        body, name=name, grid=(t // tm,),
        in_specs=[col(2), col(3)] + subs * 3 + [blk, blk, row, row, pl.BlockSpec((B_WIDTH, B_WIDTH), lambda i: (0, 0))],
        out_specs=[pl.BlockSpec((tm, 3 * B_WIDTH), lambda i: (i, 0)), row, row],
        out_shape=[jax.ShapeDtypeStruct((t, 3 * B_WIDTH), BF16), jax.ShapeDtypeStruct((1, B_WIDTH), F32),
                   jax.ShapeDtypeStruct((1, B_WIDTH), F32)],
        scratch_shapes=[_subseq_scratch(tm)], compiler_params=_params("arbitrary"),
    )(z, z, *dqs, *dks, *dvs, cos_t, sin_t, gq, gk, seg)


def _subseq_views(x, col, name):
    t = x.shape[0]
    tm = _tile(t, (SUBSEQ_TM,))

    def body(x_ref, *rest):
        out_refs, scr_ref = rest[:-1], rest[-1]
        val = x_ref[...].astype(F32)
        for o_ref, dil in zip(out_refs, DILATIONS):
            o_ref[...] = _to_subseq(val, scr_ref, dil).astype(o_ref.dtype)

    return pl.pallas_call(
        body, name=name, grid=(t // tm,), in_specs=[pl.BlockSpec((tm, B_WIDTH), lambda i: (i, col))],
        out_specs=[_subseq_spec(tm, dil) for dil in DILATIONS],
        out_shape=[jax.ShapeDtypeStruct(_subseq_shape(t, dil), x.dtype) for dil in DILATIONS],
        scratch_shapes=[_subseq_scratch(tm)], compiler_params=_params("parallel"),
    )(x)


def _attn_fwd(q, k, v, dil, name):
    t = q.shape[0] * dil
    nb = t // dil // Q_BLOCK
    cur = pl.BlockSpec((Q_BLOCK, B_WIDTH), lambda r, i: (i, r))
    prev = pl.BlockSpec((Q_BLOCK, B_WIDTH), lambda r, i: (jnp.maximum(i - 1, 0), r))

    def body(q_ref, kp_ref, kc_ref, vp_ref, vc_ref, o_ref, lse_ref):
        i = pl.program_id(1)
        q = q_ref[...]
        kk = jnp.concatenate([kp_ref[...], kc_ref[...]], axis=0)
        vv = jnp.concatenate([vp_ref[...], vc_ref[...]], axis=0)
        a = lax.broadcasted_iota(jnp.int32, (Q_BLOCK, 2 * Q_BLOCK), 0)
        j = lax.broadcasted_iota(jnp.int32, (Q_BLOCK, 2 * Q_BLOCK), 1)
        dist = a + Q_BLOCK - j
        mask = (dist >= 0) & (dist <= Q_BLOCK) & ((j >= Q_BLOCK) | (i > 0))
        sls = [slice(h * HEAD_DIM, (h + 1) * HEAD_DIM) for h in range(HEADS)]
        scores = [_dot(q[:, sl], kk[:, sl], _NT) for sl in sls]
        ps, dens = [], []
        for sl, s in zip(sls, scores):
            s = jnp.where(mask, s * (HEAD_DIM ** -0.5), NEG)
            m = jnp.max(s, axis=-1, keepdims=True)
            p = jnp.exp(s - m)
            den = jnp.sum(p, axis=-1, keepdims=True)
            ps.append(p.astype(BF16))
            dens.append(den)
            lse_ref[:, sl] = jnp.broadcast_to(m + jnp.log(den), (Q_BLOCK, HEAD_DIM))
        for sl, p, den in zip(sls, ps, dens):
            o_ref[:, sl] = _dot(p, vv[:, sl], _NN) / den

    return pl.pallas_call(
        body, name=name, grid=(dil, nb), in_specs=[cur, prev, cur, prev, cur], out_specs=[cur, cur],
        out_shape=[jax.ShapeDtypeStruct(_subseq_shape(t, dil), F32)] * 2,
        compiler_params=_params("parallel", "parallel"),
    )(q, k, k, v, v)


def _attn_merge(outs, lses, cat, name):
    nb = len(DILATIONS)
    t = cat.shape[0]
    tm = _tile(t, (SUBSEQ_TM,))
    subs = [_subseq_spec(tm, dil) for dil in DILATIONS]

    def body(*refs):
        o_refs, l_refs = refs[:nb], refs[nb:2 * nb]
        yb_refs, lse_refs, cat_ref, scr_ref = refs[2 * nb + 1:3 * nb + 1], refs[3 * nb + 1:4 * nb + 1], refs[4 * nb + 1], refs[4 * nb + 2]
        ls = [_from_subseq(r[...], scr_ref, dil) for r, dil in zip(l_refs, DILATIONS)]
        m = functools.reduce(jnp.maximum, ls)
        tot = m + jnp.log(sum(jnp.exp(l - m) for l in ls))
        yb = sum(jnp.exp(l - tot) * _from_subseq(o[...], scr_ref, dil) for l, o, dil in zip(ls, o_refs, DILATIONS))
        cat_ref[...] = yb.astype(BF16)
        yb = yb.astype(BF16).astype(F32)
        for yb_ref, lse_ref, dil in zip(yb_refs, lse_refs, DILATIONS):
            yb_ref[...] = _to_subseq(yb, scr_ref, dil).astype(BF16)
            lse_ref[...] = _to_subseq(tot, scr_ref, dil)

    outs_ = pl.pallas_call(
        body, name=name, grid=(t // tm,), in_specs=subs * 2 + [pl.BlockSpec(memory_space=pl.ANY)],
        out_specs=subs * 2 + [pl.BlockSpec((tm, B_WIDTH), lambda i: (i, A_WIDTH // B_WIDTH))],
        out_shape=[jax.ShapeDtypeStruct(_subseq_shape(t, dil), BF16) for dil in DILATIONS]
        + [jax.ShapeDtypeStruct(_subseq_shape(t, dil), F32) for dil in DILATIONS] + [jax.ShapeDtypeStruct(cat.shape, BF16)],
        input_output_aliases={2 * nb: 2 * nb}, scratch_shapes=[_subseq_scratch(tm)], compiler_params=_params("parallel"),
    )(*outs, *lses, cat)
    return outs_[:nb], outs_[nb:2 * nb], outs_[2 * nb]


def _attn_bwd(q, k, v, do, o, lse, dil, name):
    t = q.shape[0] * dil
    nb = t // dil // Q_BLOCK
    cur = pl.BlockSpec((Q_BLOCK, B_WIDTH), lambda r, i: (i, r))
    prev = pl.BlockSpec((Q_BLOCK, B_WIDTH), lambda r, i: (jnp.maximum(i - 1, 0), r))
    scale = HEAD_DIM ** -0.5

    def body(q_ref, kp_ref, kc_ref, vp_ref, vc_ref, do_ref, o_ref, lse_ref, dq_ref, dk_ref, dv_ref,
             ck_ref, cv_ref, tk_ref, tv_ref):
        i = pl.program_id(1)

        @pl.when(i == 0)
        def _():
            ck_ref[...] = jnp.zeros_like(ck_ref)
            cv_ref[...] = jnp.zeros_like(cv_ref)

        q = q_ref[...]
        kk = jnp.concatenate([kp_ref[...], kc_ref[...]], axis=0)
        vv = jnp.concatenate([vp_ref[...], vc_ref[...]], axis=0)
        do = do_ref[...]
        dof = do.astype(F32)
        of = o_ref[...].astype(F32)
        a = lax.broadcasted_iota(jnp.int32, (Q_BLOCK, 2 * Q_BLOCK), 0)
        j = lax.broadcasted_iota(jnp.int32, (Q_BLOCK, 2 * Q_BLOCK), 1)
        dist = a + Q_BLOCK - j
        mask = (dist >= 0) & (dist <= Q_BLOCK) & ((j >= Q_BLOCK) | (i > 0))
        sls = [slice(h * HEAD_DIM, (h + 1) * HEAD_DIM) for h in range(HEADS)]
        scores = [_dot(q[:, sl], kk[:, sl], _NT) for sl in sls]
        dps = [_dot(do[:, sl], vv[:, sl], _NT) for sl in sls]
        ps, dss = [], []
        for sl, s, dp in zip(sls, scores, dps):
            p = jnp.exp(jnp.where(mask, s * scale, NEG) - lse_ref[:, sl.start:sl.start + 1])
            delta = jnp.sum(dof[:, sl] * of[:, sl], axis=-1, keepdims=True)
            dss.append((p * (dp - delta) * scale).astype(BF16))
            ps.append(p.astype(BF16))
        for sl, p, ds in zip(sls, ps, dss):
            dq_ref[:, sl] = _dot(ds, kk[:, sl], _NN)
            dv_t = _dot(do[:, sl], p, _TN)
            dk_t = _dot(q[:, sl], ds, _TN)
            tk_ref[sl, :] = ck_ref[sl, :] + dk_t[:, :Q_BLOCK]
            tv_ref[sl, :] = cv_ref[sl, :] + dv_t[:, :Q_BLOCK]
            ck_ref[sl, :] = dk_t[:, Q_BLOCK:]
            cv_ref[sl, :] = dv_t[:, Q_BLOCK:]

        @pl.when(i >= 1)
        def _():
            rows = pl.ds(pl.multiple_of((i - 1) * Q_BLOCK, Q_BLOCK), Q_BLOCK)
            dk_ref[rows, :] = tk_ref[...].T
            dv_ref[rows, :] = tv_ref[...].T

        @pl.when(i == nb - 1)
        def _():
            rows = pl.ds((nb - 1) * Q_BLOCK, Q_BLOCK)
            dk_ref[rows, :] = ck_ref[...].T
            dv_ref[rows, :] = cv_ref[...].T

    whole = pl.BlockSpec((t // dil, B_WIDTH), lambda r, i: (0, r))
    return pl.pallas_call(
        body, name=name, grid=(dil, nb), in_specs=[cur, prev, cur, prev, cur, cur, cur, cur],
        out_specs=[cur, whole, whole], out_shape=[jax.ShapeDtypeStruct(_subseq_shape(t, dil), F32)] * 3,
        scratch_shapes=[pltpu.VMEM((B_WIDTH, Q_BLOCK), F32)] * 4,
        compiler_params=_params("parallel", "arbitrary"),
    )(q, k, k, v, v, do, o, lse)


FFN_TN = 256
FFN_FWD_CHUNK = 256
FFN_BWD_CHUNK = 128


def _ffn_up(h, up_t, name):
    t, k = h.shape
    tm = _tile(t)

    def body(h_ref, w_ref, o_ref):
        o_ref[...] = _dot(h_ref[...], w_ref[...], _NT).astype(BF16)

    return pl.pallas_call(
        body, name=name, grid=(2, t // tm),
        in_specs=[pl.BlockSpec((tm, k), lambda p, i: (i, 0)), pl.BlockSpec((None, FFN_DIM, k), lambda p, i: (p, 0, 0))],
        out_specs=pl.BlockSpec((None, tm, FFN_DIM), lambda p, i: (p, i, 0)),
        out_shape=jax.ShapeDtypeStruct((2, t, FFN_DIM), BF16), compiler_params=_params("parallel", "parallel"),
    )(h, up_t.reshape(2, FFN_DIM, k))


def _ffn_up_dx(du, up_t, name):
    t = du.shape[1]
    k = up_t.shape[1]
    tm = _tile(t)

    def body(a_ref, b_ref, o_ref):
        o_ref[...] = _dot(a_ref[0], b_ref[0], _NN) + _dot(a_ref[1], b_ref[1], _NN)

    return pl.pallas_call(
        body, name=name, grid=(t // tm,),
        in_specs=[pl.BlockSpec((2, tm, FFN_DIM), lambda i: (0, i, 0)), pl.BlockSpec((2, FFN_DIM, k), lambda i: (0, 0, 0))],
        out_specs=pl.BlockSpec((tm, k), lambda i: (i, 0)), out_shape=jax.ShapeDtypeStruct((t, k), F32),
        compiler_params=_params("parallel"),
    )(du, up_t.reshape(2, FFN_DIM, k))


def _ffn_conv(win, w_ref, b_ref, p):
    x = win.astype(F32)
    x0, x1, x2 = x[FFN_HALO:], pltpu.roll(x, 1, 0)[FFN_HALO:], pltpu.roll(x, 2, 0)[FFN_HALO:]
    return b_ref[p] + w_ref[p, 2:3, :] * x0 + w_ref[p, 1:2, :] * x1 + w_ref[p, 0:1, :] * x2


def _zero_if(cond, v):
    return jnp.where(cond, 0, v).astype(v.dtype)


def _ffn_act(u, dw_w, dw_b, name):
    t = u.shape[1]
    tm = _tile(t)
    chunk = min(FFN_FWD_CHUNK, tm)
    hb = tm // FFN_HALO
    main = pl.BlockSpec((2, tm, FFN_TN), lambda i, j: (0, i, j))
    halo = pl.BlockSpec((2, FFN_HALO, FFN_TN), lambda i, j: (0, jnp.maximum(i * hb - 1, 0), j))
    wsp = pl.BlockSpec((2, FFN_CONV_WIDTH, FFN_TN), lambda i, j: (0, 0, j))
    bsp = pl.BlockSpec((2, 1, FFN_TN), lambda i, j: (0, 0, j))

    def body(u_ref, uh_ref, w_ref, b_ref, o_ref, z_ref):
        first = pl.program_id(0) == 0

        def emit(rows, wins):
            za, zb = _ffn_conv(wins[0], w_ref, b_ref, 0), _ffn_conv(wins[1], w_ref, b_ref, 1)
            o_ref[rows, :] = (za * _sigmoid(za) * zb).astype(BF16)
            z_ref[0, rows, :] = za.astype(BF16)
            z_ref[1, rows, :] = zb.astype(BF16)

        emit(pl.ds(0, chunk), [jnp.concatenate([_zero_if(first, uh_ref[p]), u_ref[p, 0:chunk, :]], axis=0) for p in range(2)])

        def step(c, carry):
            s = pl.multiple_of(c * chunk, chunk)
            emit(pl.ds(s, chunk), [u_ref[p, pl.ds(s - FFN_HALO, chunk + FFN_HALO), :] for p in range(2)])
            return carry

        lax.fori_loop(1, tm // chunk, step, 0)

    return pl.pallas_call(
        body, name=name, grid=(t // tm, FFN_DIM // FFN_TN), in_specs=[main, halo, wsp, bsp],
        out_specs=[pl.BlockSpec((tm, FFN_TN), lambda i, j: (i, j)), main],
        out_shape=[jax.ShapeDtypeStruct((t, FFN_DIM), BF16), jax.ShapeDtypeStruct((2, t, FFN_DIM), BF16)],
        compiler_params=_params("parallel", "parallel"),
    )(u, u, dw_w, dw_b)


def _fold8(v):
    return jnp.sum(v.reshape(v.shape[0] // 8, 8, v.shape[1]), axis=0)


def _ffn_act_bwd(u, z, dact, dw_w, name):
    t = u.shape[1]
    tm = _tile(t)
    chunk = min(FFN_BWD_CHUNK, tm // 2)
    halo = FFN_HALO
    hb = tm // halo
    nt = t // tm
    last_halo = t // halo - 1
    next_i = lambda i: jnp.minimum((i + 1) * hb, last_halo)
    main = pl.BlockSpec((2, tm, FFN_TN), lambda j, i: (0, i, j))
    nxt = pl.BlockSpec((2, halo, FFN_TN), lambda j, i: (0, next_i(i), j))
    wsp = pl.BlockSpec((2, FFN_CONV_WIDTH, FFN_TN), lambda j, i: (0, 0, j))
    bsp = pl.BlockSpec((2, 1, FFN_TN), lambda j, i: (0, 0, j))

    def body(u_ref, z_ref, zn_ref, da_ref, dan_ref, w_ref, du_ref, dw_ref, db_ref, acc_ref):
        i = pl.program_id(1)
        last = i == nt - 1
        acc_ref[...] = jnp.zeros_like(acc_ref)

        def emit(rows, zs, dact):
            n = chunk + halo
            za, zb, dact = zs[0].astype(F32), zs[1].astype(F32), dact.astype(F32)
            sg = _sigmoid(za)
            dzs = (dact * zb * (sg * (1.0 + za * (1.0 - sg))), dact * (za * sg))
            for p, dz in enumerate(dzs):
                ahead = (dz[:chunk], pltpu.roll(dz, n - 1, 0)[:chunk], pltpu.roll(dz, n - 2, 0)[:chunk])
                um = u_ref[p, rows, :].astype(F32)
                acc_ref[p, FFN_CONV_WIDTH] += _fold8(ahead[0])
                du = None
                for j, dzj in enumerate(ahead):
                    k = FFN_CONV_WIDTH - 1 - j
                    acc_ref[p, k] += _fold8(dzj * um)
                    term = w_ref[p, k:k + 1, :] * dzj
                    du = term if du is None else du + term
                du_ref[p, rows, :] = du.astype(BF16)

        def step(c, carry):
            s = pl.multiple_of(c * chunk, chunk)
            emit(pl.ds(s, chunk), [z_ref[p, pl.ds(s, chunk + halo), :] for p in range(2)], da_ref[pl.ds(s, chunk + halo), :])
            return carry

        lax.fori_loop(0, tm // chunk - 1, step, 0)
        s = tm - chunk
        emit(pl.ds(s, chunk),
             [jnp.concatenate([z_ref[p, s:tm, :], zn_ref[p]], axis=0) for p in range(2)],
             jnp.concatenate([da_ref[s:tm, :], _zero_if(last, dan_ref[...])], axis=0))

        @pl.when(i == 0)
        def _():
            dw_ref[...] = jnp.zeros_like(dw_ref)
            db_ref[...] = jnp.zeros_like(db_ref)

        for p in range(2):
            for k in range(FFN_CONV_WIDTH):
                dw_ref[p, k:k + 1, :] += _colsum(acc_ref[p, k])
            db_ref[p] += _colsum(acc_ref[p, FFN_CONV_WIDTH])

    return pl.pallas_call(
        body, name=name, grid=(FFN_DIM // FFN_TN, nt),
        in_specs=[main, main, nxt, pl.BlockSpec((tm, FFN_TN), lambda j, i: (i, j)),
                  pl.BlockSpec((halo, FFN_TN), lambda j, i: (next_i(i), j)), wsp],
        out_specs=[main, wsp, bsp],
        out_shape=[jax.ShapeDtypeStruct((2, t, FFN_DIM), BF16), jax.ShapeDtypeStruct((2, FFN_CONV_WIDTH, FFN_DIM), F32),
                   jax.ShapeDtypeStruct((2, 1, FFN_DIM), F32)],
        scratch_shapes=[pltpu.VMEM((2, FFN_CONV_WIDTH + 1, 8, FFN_TN), F32)],
        compiler_params=_params("parallel", "arbitrary"),
    )(u, z, z, dact, dact, dw_w)


CONV_TM = 256
CONV_ROWS = 128
CONV_LANES = 128


def _glu_window(pa_ref, pah_ref, pg_ref, pgh_ref, scr_ref, first):
    ah, gh = pah_ref[...].astype(F32), pgh_ref[...].astype(F32)
    scr_ref[0:CONV_HALO, :] = jnp.where(first, 0.0, ah * _sigmoid(gh))
    scr_ref[CONV_HALO:, :] = pa_ref[...].astype(F32) * _sigmoid(pg_ref[...].astype(F32))


def _tap_slabs(win, rows, ahead):
    n = win.shape[0]
    for s in range(8):
        ws = win if s == 0 else pltpu.roll(win, n - s if ahead else s, 0)
        for q in range(CONV_HALO // 8):
            o = 8 * q + s
            if o < CONV_WIDTH:
                start = 8 * q if ahead else CONV_HALO - 8 * q
                yield CONV_WIDTH - 1 - o, ws[start:start + rows]


def _conformer_specs(t):
    tm = _tile(t, (CONV_TM, 128))
    hb = tm // CONV_HALO
    d = D_MODEL
    main = lambda c: pl.BlockSpec((tm, d), lambda i: (i, c))
    halo = lambda c: pl.BlockSpec((CONV_HALO, d), lambda i: (jnp.maximum(i * hb - 1, 0), c))
    row = pl.BlockSpec((1, d), lambda i: (0, 0))
    wsp = pl.BlockSpec((CONV_WIDTH, d), lambda i: (0, 0))
    return tm, main, halo, row, wsp


def _conformer_mid(p, dw_w, dw_b, ln_g, ln_b, name):
    t = p.shape[0]
    tm, main, halo, row, wsp = _conformer_specs(t)
    d, lanes = D_MODEL, CONV_LANES

    def body(pa_ref, pah_ref, pg_ref, pgh_ref, w_ref, b_ref, g_ref, lb_ref, o_ref, dc_ref, scr_ref):
        _glu_window(pa_ref, pah_ref, pg_ref, pgh_ref, scr_ref, pl.program_id(0) == 0)
        for c in range(d // lanes):
            ls = slice(c * lanes, (c + 1) * lanes)
            acc = jnp.broadcast_to(b_ref[:, ls], (tm, lanes))
            for k, slab in _tap_slabs(scr_ref[:, ls], tm, False):
                acc = acc + w_ref[k:k + 1, ls] * slab
            dc_ref[:, ls] = acc

        def norm(r, carry):
            r0 = pl.multiple_of(r * 32, 32)
            dc = dc_ref[pl.ds(r0, 32), :]
            xc = dc - jnp.mean(dc, axis=-1, keepdims=True)
            ln = xc * lax.rsqrt(jnp.mean(xc * xc, axis=-1, keepdims=True) + EPS) * g_ref[...] + lb_ref[...]
            o_ref[pl.ds(r0, 32), :] = (ln * _sigmoid(ln)).astype(BF16)
            return carry

        lax.fori_loop(0, tm // 32, norm, 0)

    return pl.pallas_call(
        body, name=name, grid=(t // tm,), in_specs=[main(0), halo(0), main(1), halo(1), wsp, row, row, row],
        out_specs=[main(0), main(0)], out_shape=[jax.ShapeDtypeStruct((t, d), BF16), jax.ShapeDtypeStruct((t, d), F32)],
        scratch_shapes=[pltpu.VMEM((tm + CONV_HALO, d), F32)], compiler_params=_params("parallel"),
    )(p, p, p, p, dw_w, dw_b, ln_g, ln_b)


def _conformer_mid_bwd(p, dc, ds, ln_g, ln_b, name):
    t = p.shape[0]
    tm, main, halo, row, wsp = _conformer_specs(t)
    d, nt = D_MODEL, t // tm
    rows, lanes = CONV_ROWS, CONV_LANES

    def body(pa_ref, pah_ref, pg_ref, pgh_ref, dc_ref, ds_ref, g_ref, lb_ref,
             ddc_ref, dw_ref, db_ref, dg_ref, dlb_ref, scr_ref, wacc_ref, racc_ref):
        i = pl.program_id(0)

        @pl.when(i == 0)
        def _():
            wacc_ref[...] = jnp.zeros_like(wacc_ref)
            racc_ref[...] = jnp.zeros_like(racc_ref)

        _glu_window(pa_ref, pah_ref, pg_ref, pgh_ref, scr_ref, i == 0)

        def norm_bwd(r, carry):
            r0 = pl.multiple_of(r * 32, 32)
            dcv = dc_ref[pl.ds(r0, 32), :]
            xc = dcv - jnp.mean(dcv, axis=-1, keepdims=True)
            rstd = lax.rsqrt(jnp.mean(xc * xc, axis=-1, keepdims=True) + EPS)
            xhat = xc * rstd
            ln = xhat * g_ref[...] + lb_ref[...]
            sg = _sigmoid(ln)
            dln = ds_ref[pl.ds(r0, 32), :].astype(F32) * (sg * (1.0 + ln * (1.0 - sg)))
            dxh = dln * g_ref[...]
            ddc = rstd * (dxh - jnp.mean(dxh, axis=-1, keepdims=True) - xhat * jnp.mean(dxh * xhat, axis=-1, keepdims=True))
            ddc_ref[pl.ds(r0, 32), :] = ddc
            racc_ref[0] += _fold8(dln * xhat)
            racc_ref[1] += _fold8(dln)
            racc_ref[2] += _fold8(ddc)
            return carry

        lax.fori_loop(0, tm // 32, norm_bwd, 0)

        for c in range(d // lanes):
            ls = slice(c * lanes, (c + 1) * lanes)

            def taps(r, carry, ls=ls):
                r0 = pl.multiple_of(r * rows, rows)
                ddc = ddc_ref[pl.ds(r0, rows), ls]
                for k, slab in _tap_slabs(scr_ref[pl.ds(r0, rows + CONV_HALO), ls], rows, False):
                    wacc_ref[k, :, ls] += _fold8(ddc * slab)
                return carry

            lax.fori_loop(0, tm // rows, taps, 0)

        @pl.when(i == nt - 1)
        def _():
            for k in range(CONV_WIDTH):
                dw_ref[k:k + 1, :] = _colsum(wacc_ref[k])
            dg_ref[...] = _colsum(racc_ref[0])
            dlb_ref[...] = _colsum(racc_ref[1])
            db_ref[...] = _colsum(racc_ref[2])

    return pl.pallas_call(
        body, name=name, grid=(nt,), in_specs=[main(0), halo(0), main(1), halo(1), main(0), main(0), row, row],
        out_specs=[main(0), wsp, row, row, row],
        out_shape=[jax.ShapeDtypeStruct((t, d), F32), jax.ShapeDtypeStruct((CONV_WIDTH, d), F32)]
        + [jax.ShapeDtypeStruct((1, d), F32)] * 3,
        scratch_shapes=[pltpu.VMEM((tm + CONV_HALO, d), F32), pltpu.VMEM((CONV_WIDTH, 8, d), F32), pltpu.VMEM((3, 8, d), F32)],
        compiler_params=_params("arbitrary"),
    )(p, p, p, p, dc, ds, ln_g, ln_b)


def _conformer_glu_bwd(p, ddc, dw_w, name):
    t = p.shape[0]
    d = D_MODEL
    tm = _tile(t, (CONV_TM, 128))
    hb = tm // CONV_HALO
    nt = t // tm
    last_halo = t // CONV_HALO - 1
    rows, lanes = CONV_ROWS, CONV_LANES
    col = lambda c: pl.BlockSpec((tm, d), lambda i: (i, c))
    nxt = pl.BlockSpec((CONV_HALO, d), lambda i: (jnp.minimum((i + 1) * hb, last_halo), 0))

    def body(pa_ref, pg_ref, ddc_ref, ddcn_ref, w_ref, dp_ref, db_ref, scr_ref, acc_ref):
        i = pl.program_id(0)

        @pl.when(i == 0)
        def _():
            acc_ref[...] = jnp.zeros_like(acc_ref)

        scr_ref[0:tm, :] = ddc_ref[...]
        scr_ref[tm:, :] = _zero_if(i == nt - 1, ddcn_ref[...])
        for c in range(d // lanes):
            ls = slice(c * lanes, (c + 1) * lanes)
            gs = slice(d + c * lanes, d + (c + 1) * lanes)

            def taps(r, carry, ls=ls, gs=gs):
                r0 = pl.multiple_of(r * rows, rows)
                dglu = None
                for k, slab in _tap_slabs(scr_ref[pl.ds(r0, rows + CONV_HALO), ls], rows, True):
                    term = w_ref[k:k + 1, ls] * slab
                    dglu = term if dglu is None else dglu + term
                a = pa_ref[pl.ds(r0, rows), ls].astype(F32)
                sg = _sigmoid(pg_ref[pl.ds(r0, rows), ls].astype(F32))
                da = (dglu * sg).astype(BF16)
                dg = (dglu * a * sg * (1.0 - sg)).astype(BF16)
                dp_ref[pl.ds(r0, rows), ls] = da
                dp_ref[pl.ds(r0, rows), gs] = dg
                acc_ref[:, ls] += _fold8(da.astype(F32))
                acc_ref[:, gs] += _fold8(dg.astype(F32))
                return carry

            lax.fori_loop(0, tm // rows, taps, 0)

        @pl.when(i == nt - 1)
        def _():
            db_ref[...] = _colsum(acc_ref[...])

    return pl.pallas_call(
        body, name=name, grid=(nt,),
        in_specs=[col(0), col(1), col(0), nxt, pl.BlockSpec((CONV_WIDTH, d), lambda i: (0, 0))],
        out_specs=[pl.BlockSpec((tm, 2 * d), lambda i: (i, 0)), pl.BlockSpec((1, 2 * d), lambda i: (0, 0))],
        out_shape=[jax.ShapeDtypeStruct((t, 2 * d), BF16), jax.ShapeDtypeStruct((1, 2 * d), F32)],
        scratch_shapes=[pltpu.VMEM((tm + CONV_HALO, d), F32), pltpu.VMEM((8, 2 * d), F32)],
        compiler_params=_params("arbitrary"),
    )(p, p, ddc, ddc, dw_w)


def _colsum_call(a, name):
    t, n = a.shape
    tm = _tile(t)

    def body(a_ref, o_ref):
        @pl.when(pl.program_id(0) == 0)
        def _():
            o_ref[...] = jnp.zeros_like(o_ref)

        o_ref[...] += _colsum(a_ref[...].astype(F32))

    return pl.pallas_call(
        body, name=name, grid=(t // tm,), in_specs=[pl.BlockSpec((tm, n), lambda i: (i, 0))],
        out_specs=pl.BlockSpec((1, n), lambda i: (0, 0)), out_shape=jax.ShapeDtypeStruct((1, n), F32),
        compiler_params=_params("arbitrary"),
    )(a)


def _ada_fwd(c_all, w, name):
    rows, d = c_all.shape
    n = w.shape[1]
    tn = _tile(n, (256, 128))

    def body(c_ref, w_ref, o_ref):
        c = c_ref[...]
        o_ref[...] = _dot((c * _sigmoid(c)).astype(BF16), w_ref[...].astype(BF16), _NN)

    return pl.pallas_call(
        body, name=name, grid=(n // tn,),
        in_specs=[pl.BlockSpec((rows, d), lambda j: (0, 0)), pl.BlockSpec((d, tn), lambda j: (0, j))],
        out_specs=pl.BlockSpec((rows, tn), lambda j: (0, j)), out_shape=jax.ShapeDtypeStruct((rows, n), F32),
        compiler_params=_params("parallel"),
    )(c_all, w)


def _ada_bwd(c_all, dmod, name):
    rows, d = c_all.shape
    n = dmod.shape[1]
    tn = _tile(n, (256, 128))

    def body(c_ref, g_ref, o_ref):
        c = c_ref[...]
        o_ref[...] = _dot((c * _sigmoid(c)).astype(BF16), g_ref[...].astype(BF16), _TN)

    return pl.pallas_call(
        body, name=name, grid=(n // tn,),
        in_specs=[pl.BlockSpec((rows, d), lambda j: (0, 0)), pl.BlockSpec((rows, tn), lambda j: (0, j))],
        out_specs=pl.BlockSpec((d, tn), lambda j: (0, j)), out_shape=jax.ShapeDtypeStruct((d, n), F32),
        compiler_params=_params("parallel"),
    )(c_all, dmod)


def _sum_in_device_order(own, land, me, name):
    s, r, c = land.shape
    tr = _row_tile(r, 256)
    slot = lambda k: pl.BlockSpec((None, tr, c), lambda i, me_ref: (jnp.where(me_ref[0] == k, (k + 1) % s, k), i, 0))
    own_spec = pl.BlockSpec((tr, c), lambda i, me_ref: (i, 0))

    def body(me_ref, own_ref, *refs):
        o_ref = refs[-1]
        acc = None
        for k, ref in enumerate(refs[:-1]):
            term = jnp.where(me_ref[0] == k, own_ref[...], ref[...]).astype(F32)
            acc = term if acc is None else acc + term
        o_ref[...] = acc

    return pl.pallas_call(
        body, name=name, out_shape=jax.ShapeDtypeStruct((r, c), F32),
        grid_spec=pltpu.PrefetchScalarGridSpec(
            num_scalar_prefetch=1, grid=(r // tr,), in_specs=[own_spec] + [slot(k) for k in range(s)], out_specs=own_spec),
        compiler_params=_params("parallel"),
    )(me, own, *[land] * s)


def _sum_with_own(blocks, land, me, name):
    s, r, c = land.shape
    tr = _row_tile(r, 256)
    slot = lambda k: pl.BlockSpec((None, tr, c), lambda i, me_ref: ((me_ref[0] + k) % s, i, 0))

    def body(me_ref, own_ref, *refs):
        o_ref = refs[-1]
        acc = own_ref[...].astype(F32)
        for ref in refs[:-1]:
            acc = acc + ref[...].astype(F32)
        o_ref[...] = acc

    return pl.pallas_call(
        body, name=name, out_shape=jax.ShapeDtypeStruct((r, c), F32),
        grid_spec=pltpu.PrefetchScalarGridSpec(
            num_scalar_prefetch=1, grid=(r // tr,), in_specs=[slot(0)] + [slot(k) for k in range(1, s)],
            out_specs=pl.BlockSpec((tr, c), lambda i, me_ref: (i, 0))),
        compiler_params=_params("parallel"),
    )(me, blocks, *[land] * (s - 1))


def _adamw_update(w, g, m, v):
    nm = ADAM_B1 * m + (1.0 - ADAM_B1) * g
    nv = ADAM_B2 * v + (1.0 - ADAM_B2) * (g * g)
    m_hat = nm * (1.0 / (1.0 - ADAM_B1 ** ADAM_STEP))
    v_hat = nv * (1.0 / (1.0 - ADAM_B2 ** ADAM_STEP))
    return -ADAM_LR * (m_hat / (jnp.sqrt(v_hat) + ADAM_EPS) + ADAM_WD * w), nm, nv


def _adamw(w, g, m, v, name):
    l, r, c = w.shape
    tr = _row_tile(r, 256)
    blk = pl.BlockSpec((None, tr, c), lambda k, i: (k, i, 0))

    def body(w_ref, g_ref, m_ref, v_ref, d_ref, nm_ref, nv_ref):
        d_ref[...], nm_ref[...], nv_ref[...] = _adamw_update(w_ref[...], g_ref[...], m_ref[...], v_ref[...])

    return pl.pallas_call(
        body, name=name, grid=(l, r // tr), in_specs=[blk] * 4, out_specs=[blk] * 3,
        out_shape=[jax.ShapeDtypeStruct(w.shape, F32)] * 3, compiler_params=_params("parallel", "parallel"),
    )(w, g, m, v)


def _adamw_small(ws, gs, ms, vs, name):
    n = len(ws)
    two_d = lambda a: a.reshape(-1, a.shape[-1])

    def body(*refs):
        ins, outs = refs[:4 * n], refs[4 * n:]
        for a in range(n):
            outs[a][...], outs[n + a][...], outs[2 * n + a][...] = _adamw_update(*[ins[k * n + a][...] for k in range(4)])

    res = pl.pallas_call(
        body, name=name, out_shape=[jax.ShapeDtypeStruct(two_d(w).shape, F32) for w in ws] * 3,
    )(*[two_d(a) for a in (*ws, *gs, *ms, *vs)])
    return [[res[k * n + a].reshape(ws[a].shape) for a in range(n)] for k in range(3)]


def _mesh_pos():
    return lax.axis_index("x"), lax.axis_index("y"), lax.axis_index("c")


def _all_gather_vmem(x_shard, name):
    m_per, n = x_shard.shape

    def body(x_ref, out_ref, send_sems, recv_sems, local_sem):
        x, y, c = _mesh_pos()
        me, sibling = (x, y, c), (x, y, 1 - c)
        chips = [(1 - x, y), (x, 1 - y), (1 - x, 1 - y)]

        def rows(px, py, pc):
            return out_ref.at[pl.ds((4 * px + 2 * py + pc) * m_per, m_per), :]

        def copy(k, block, to, src=None):
            return pltpu.make_async_remote_copy(
                src_ref=rows(*block) if src is None else src, dst_ref=rows(*block),
                send_sem=send_sems.at[k], recv_sem=recv_sems.at[k], device_id=to, device_id_type=MESH)

        mine = pltpu.make_async_copy(x_ref, rows(*me), local_sem)
        mine.start()
        first = [copy(0, me, sibling, src=x_ref)]
        first += [copy(1 + j, me, (*chip, c), src=x_ref) for j, chip in enumerate(chips)]
        for cp in first:
            cp.start()
        passed = [copy(4 + j, (*chip, c), sibling) for j, chip in enumerate(chips)]
        for j, chip in enumerate(chips):
            copy(1 + j, (*chip, c), me).wait_recv()
            passed[j].start()
        copy(0, sibling, me).wait_recv()
        for j, chip in enumerate(chips):
            copy(4 + j, (*chip, 1 - c), me).wait_recv()
        for cp in first + passed:
            cp.wait_send()
        mine.wait()

    return pl.pallas_call(
        body, name=name, out_shape=jax.ShapeDtypeStruct((N_DEV * m_per, n), x_shard.dtype),
        in_specs=[pl.BlockSpec(memory_space=pltpu.VMEM)], out_specs=pl.BlockSpec(memory_space=pltpu.VMEM),
        scratch_shapes=[pltpu.SemaphoreType.DMA((7,)), pltpu.SemaphoreType.DMA((7,)), pltpu.SemaphoreType.DMA],
    )(x_shard)


def _all_gather_hbm(shards, name):
    n = len(shards)
    out_shape = [jax.ShapeDtypeStruct((N_DEV,) + s.shape, s.dtype) for s in shards]

    def body(*refs):
        x_refs, out_refs = refs[:n], refs[n:2 * n]
        send_sems, recv_sems, local_sems = refs[2 * n:]
        x, y, c = _mesh_pos()
        me, sibling = (x, y, c), (x, y, 1 - c)
        chips = [(1 - x, y), (x, 1 - y), (1 - x, 1 - y)]

        def blk(a, p):
            return out_refs[a].at[4 * p[0] + 2 * p[1] + p[2]]

        def copy(a, k, block, to, src=None):
            return pltpu.make_async_remote_copy(
                src_ref=blk(a, block) if src is None else src, dst_ref=blk(a, block),
                send_sem=send_sems.at[7 * a + k], recv_sem=recv_sems.at[7 * a + k], device_id=to, device_id_type=MESH)

        mine = [pltpu.make_async_copy(x_refs[a], blk(a, me), local_sems.at[a]) for a in range(n)]
        for cp in mine:
            cp.start()
        first = []
        for a in range(n):
            first.append(copy(a, 0, me, sibling, src=x_refs[a]))
            first += [copy(a, 1 + j, me, (*chip, c), src=x_refs[a]) for j, chip in enumerate(chips)]
        for cp in first:
            cp.start()
        passed = []
        for j, chip in enumerate(chips):
            for a in range(n):
                copy(a, 1 + j, (*chip, c), me).wait_recv()
                fwd = copy(a, 4 + j, (*chip, c), sibling)
                fwd.start()
                passed.append(fwd)
        for a in range(n):
            copy(a, 0, sibling, me).wait_recv()
            for j, chip in enumerate(chips):
                copy(a, 4 + j, (*chip, 1 - c), me).wait_recv()
        for cp in first + passed:
            cp.wait_send()
        for cp in mine:
            cp.wait()

    return pl.pallas_call(
        body, name=name, out_shape=out_shape, in_specs=[pl.BlockSpec(memory_space=pltpu.VMEM)] * n,
        out_specs=[pl.BlockSpec(memory_space=pl.ANY)] * n,
        scratch_shapes=[pltpu.SemaphoreType.DMA((7 * n,)), pltpu.SemaphoreType.DMA((7 * n,)), pltpu.SemaphoreType.DMA((n,))],
    )(*shards)


def _peers(x, y, c):
    flip = lambda v, f: 1 - v if f else v
    return [(flip(x, m & 4), flip(y, m & 2), flip(c, m & 1)) for m in range(1, N_DEV)]


def _dev_index(p):
    return 4 * p[0] + 2 * p[1] + p[2]


def _push_copies(src_refs, land_refs, send_sems, recv_sems, scatter, receive):
    x, y, c = _mesh_pos()
    me = _dev_index((x, y, c))
    copies = []
    for a, (src, land) in enumerate(zip(src_refs, land_refs)):
        for k, p in enumerate(_peers(x, y, c)):
            copies.append(pltpu.make_async_remote_copy(
                src_ref=src.at[_dev_index(p)] if scatter else src, dst_ref=land.at[_dev_index(p) if receive else me],
                send_sem=send_sems.at[7 * a + k], recv_sem=recv_sems.at[7 * a + k], device_id=p, device_id_type=MESH))
    return copies


_HBM = pl.BlockSpec(memory_space=pltpu.HBM)
_SEM = pl.BlockSpec(memory_space=pltpu.SEMAPHORE)
_EFFECT = pltpu.SideEffectType.DATAFLOW_SIDE_EFFECTING


def _pushes_start(srcs, lands, scatter, name):
    n = len(srcs)

    def body(*refs):
        src_refs, land_refs = refs[:n], refs[n:2 * n]
        send_sems, recv_sems = refs[2 * n], refs[2 * n + 1]
        token = refs[-1]
        for cp in _push_copies(src_refs, land_refs, send_sems, recv_sems, scatter, receive=False):
            cp.start()
        token[...] = jnp.zeros_like(token)

    hbm = lambda a: pltpu.HBM(a.shape, a.dtype)
    sems = pltpu.SemaphoreType.DMA((7 * n,))
    outs = pl.pallas_call(
        body, name=name,
        out_shape=(sems, sems, *[hbm(a) for a in srcs], *[hbm(a) for a in lands], jax.ShapeDtypeStruct((8, 128), F32)),
        in_specs=[_HBM] * (2 * n), out_specs=(_SEM, _SEM, *[_HBM] * (2 * n), pl.BlockSpec(memory_space=pltpu.VMEM)),
        input_output_aliases={i: 2 + i for i in range(2 * n)},
        compiler_params=pltpu.CompilerParams(has_side_effects=_EFFECT),
    )(*[pltpu.with_memory_space_constraint(a, pltpu.HBM) for a in (*srcs, *lands)])
    return (outs[0], outs[1], outs[2:2 + n], outs[2 + n:2 + 2 * n], scatter), outs[-1]


def _pushes_wait(handle, after, name):
    send_sems, recv_sems, srcs, lands, scatter = handle
    n = len(srcs)
    after = after if isinstance(after, (tuple, list)) else (after,)

    def body(*refs):
        src_refs, land_refs = refs[:n], refs[n:2 * n]
        for cp in _push_copies(src_refs, land_refs, refs[2 * n], refs[2 * n + 1], scatter, receive=True):
            cp.wait_send()
            cp.wait_recv()

    hbm = lambda a: pltpu.HBM(a.shape, a.dtype)
    outs = pl.pallas_call(
        body, name=name, out_shape=tuple(hbm(a) for a in (*srcs, *lands)),
        in_specs=[_HBM] * (2 * n) + [_SEM, _SEM] + [pl.BlockSpec(memory_space=pl.ANY)] * len(after),
        out_specs=tuple([_HBM] * (2 * n)), input_output_aliases={i: i for i in range(2 * n)},
        compiler_params=pltpu.CompilerParams(has_side_effects=_EFFECT),
    )(*srcs, *lands, send_sems, recv_sems, *after)
    return outs[:n], outs[n:]


def _landing_zones(srcs, name):
    n = len(srcs)

    def body(*refs):
        src_refs, land_refs, bufs, sems = refs[:n], refs[n:2 * n], refs[2 * n:3 * n], refs[3 * n]
        me = _dev_index(_mesh_pos())
        load = [pltpu.make_async_copy(src, buf, sems.at[a]) for a, (src, buf) in enumerate(zip(src_refs, bufs))]
        store = [pltpu.make_async_copy(buf, land.at[me], sems.at[a]) for a, (buf, land) in enumerate(zip(bufs, land_refs))]
        for cp in load:
            cp.start()
        for ld, st in zip(load, store):
            ld.wait()
            st.start()
        for cp in store:
            cp.wait()

    any_spec = pl.BlockSpec(memory_space=pl.ANY)
    return pl.pallas_call(
        body, name=name, out_shape=[jax.ShapeDtypeStruct((N_DEV,) + s.shape, s.dtype) for s in srcs],
        in_specs=[any_spec] * n, out_specs=[any_spec] * n,
        scratch_shapes=[pltpu.VMEM(s.shape, s.dtype) for s in srcs] + [pltpu.SemaphoreType.DMA((n,))],
        compiler_params=pltpu.CompilerParams(vmem_limit_bytes=V7X_VMEM_LIMIT),
    )(*srcs)


def _ffn_forward(x, mod, norm_g, w, tag):
    sh, sc, gate = mod
    h = _modnorm(x, norm_g, sc, sh, f"{tag}_norm")
    u = _ffn_up(h, w["up_t"], f"{tag}_up")
    act, z = _ffn_act(u, w["dw_w"], w["dw_b"], f"{tag}_act")
    y, x_new = _matmul(act, w["down"], "nn", BF16, f"{tag}_down", resid=(x, gate))
    return x_new, (x, h, u, z, act, y)


def _behind(row, token):
    return row if token is None else row + token[0:1, 0:1]


def _ffn_backward(dx_new, dy, d_gate, saved, mod, norm_g, w, tag, emit, below):
    x, h, u, z, act, _ = saved
    _, sc, _ = mod
    d_down = _matmul_tn_acc(act, dy, f"{tag}_down_dw")
    dact = _matmul(dy, w["down"], "nt", BF16, f"{tag}_down_dx")
    du, d_dw_w, d_dw_b = _ffn_act_bwd(u, z, dact, w["dw_w"], f"{tag}_act_bwd")
    d_up_t = _matmul_tn_acc(du, h, f"{tag}_up_dw").reshape(2 * FFN_DIM, -1)
    token = emit([d_up_t, d_down])
    dh = _ffn_up_dx(du, w["up_t"], f"{tag}_up_dx")
    dx, d_w, d_sh, *dy_below = _modnorm_bwd(x, dh, norm_g, _behind(sc, token), dx_new, below, f"{tag}_norm_bwd")
    return (dx, *dy_below), dict(dw_w=d_dw_w.transpose(1, 0, 2).reshape(FFN_CONV_WIDTH, 2 * FFN_DIM),
                    dw_b=d_dw_b.reshape(1, 2 * FFN_DIM), norm_g=d_w * (1.0 + sc), sh=d_sh, sc=d_w * norm_g, gate=d_gate)


def _mixer_forward(x, mod, norm_g, w, rope, tag):
    sh, sc, gate = mod
    h = _modnorm(x, norm_g, sc, sh, f"{tag}_norm")
    z = _matmul(h, w["w_in_t"], "nt", BF16, f"{tag}_in")
    ya = _gmlp_fwd(z, w["gain"], w["wtril"], w["bias_exp"], f"{tag}_gmlp")
    q, k, v = _qk_prep(z, rope[0], rope[1], w["gq"], w["gk"], w["seg"], f"{tag}_qk")
    outs, lses = zip(*[_attn_fwd(q[b], k[b], v[b], dil, f"{tag}_attn_d{dil}") for b, dil in enumerate(DILATIONS)])
    yb, lse, cat = _attn_merge(outs, lses, ya, f"{tag}_merge")
    y, x_new = _matmul(cat, w["w_out"], "nn", BF16, f"{tag}_out", resid=(x, gate))
    return x_new, (x, h, z, q, k, v, yb, lse, cat, y)


def _mixer_backward(dx_new, dy, d_gate, saved, mod, norm_g, w, rope, tag, emit, below):
    x, h, z, q, k, v, yb, lse, cat, _ = saved
    _, sc, _ = mod
    d_w_out = _matmul_tn_acc(cat, dy, f"{tag}_out_dw")
    dcat = _matmul(dy, w["w_out"], "nt", BF16, f"{tag}_out_dx")
    dz_a, d_sp_w, d_gain, d_bias_exp = _gmlp_bwd(z, dcat, w["gain"], w["wtril"], w["wtril_t"], w["bias_exp"], f"{tag}_gmlp_bwd")
    dyb = _subseq_views(dcat, A_WIDTH // B_WIDTH, f"{tag}_dyb_views")
    dqs, dks, dvs = zip(*[_attn_bwd(q[b], k[b], v[b], dyb[b], yb[b], lse[b], dil, f"{tag}_attn_bwd_d{dil}")
                          for b, dil in enumerate(DILATIONS)])
    dz_qkv, d_gq, d_gk = _qk_prep_bwd(z, dqs, dks, dvs, rope[0], rope[1], w["gq"], w["gk"], w["seg"], f"{tag}_qk_bwd")
    dz = jnp.concatenate([dz_a, dz_qkv], axis=1)
    d_w_in_t = _matmul_tn_acc(dz, h, f"{tag}_in_dw")
    token = emit([d_w_in_t, d_w_out])
    dh = _matmul(dz, w["w_in_t"], "nn", F32, f"{tag}_in_dx")
    dx, d_w, d_sh, *dy_below = _modnorm_bwd(x, dh, norm_g, _behind(sc, token), dx_new, below, f"{tag}_norm_bwd")
    return (dx, *dy_below), dict(
        vnorm_g=d_gain.reshape(A_GROUPS, GROUP_DIM), spatial_w=d_sp_w,
        spatial_b=d_bias_exp.reshape(CHUNK, A_GROUPS, GROUP_DIM).sum(-1).T,
        q_norm_g=d_gq.reshape(HEADS, HEAD_DIM).sum(0), k_norm_g=d_gk.reshape(HEADS, HEAD_DIM).sum(0),
        norm_g=d_w * (1.0 + sc), sh=d_sh, sc=d_w * norm_g, gate=d_gate)


def _conformer_forward(x, mod, norm_g, w, tag):
    sh, sc, gate = mod
    h = _modnorm(x, norm_g, sc, sh, f"{tag}_norm")
    p = _matmul(h, w["pw1_t"], "nt", BF16, f"{tag}_pw1", bias=w["pw1_b"])
    s, dc = _conformer_mid(p, w["dw_w"], w["dw_b"], w["ln_g"], w["ln_b"], f"{tag}_mid")
    y, x_new = _matmul(s, w["pw2"], "nn", BF16, f"{tag}_pw2", bias=w["pw2_b"], resid=(x, gate))
    return x_new, (x, h, p, dc, s, y)


def _conformer_backward(dx_new, dy, d_gate, saved, mod, norm_g, w, tag, emit, below):
    x, h, p, dc, s, _ = saved
    _, sc, _ = mod
    d_pw2 = _matmul_tn_acc(s, dy, f"{tag}_pw2_dw")
    d_pw2_b = _colsum_call(dy, f"{tag}_pw2_db")
    ds = _matmul(dy, w["pw2"], "nt", BF16, f"{tag}_pw2_dx")
    ddc, d_dw_w, d_dw_b, d_ln_g, d_ln_b = _conformer_mid_bwd(p, dc, ds, w["ln_g"], w["ln_b"], f"{tag}_mid_bwd")
    dp, d_pw1_b = _conformer_glu_bwd(p, ddc, w["dw_w"], f"{tag}_glu_bwd")
    d_pw1_t = _matmul_tn_acc(dp, h, f"{tag}_pw1_dw")
    token = emit([d_pw1_t, d_pw2])
    dh = _matmul(dp, w["pw1_t"], "nn", F32, f"{tag}_pw1_dx")
    dx, d_w, d_sh, *dy_below = _modnorm_bwd(x, dh, norm_g, _behind(sc, token), dx_new, below, f"{tag}_norm_bwd")
    return (dx, *dy_below), dict(pw1_b=d_pw1_b, dw_w=d_dw_w, dw_b=d_dw_b, ln_g=d_ln_g, ln_b=d_ln_b, pw2_b=d_pw2_b, norm_g=d_w * (1.0 + sc), sh=d_sh, sc=d_w * norm_g, gate=d_gate)


def _local_step(x, target, pos, mod, norm_mix_g, norm_ffn_g, mixer_w, conv_w, ffn_w, fetch, emit):
    d = D_MODEL
    inv_freq = 1.0 / (ROPE_THETA ** (jnp.arange(0, HEAD_DIM, 2, dtype=F32) / HEAD_DIM))
    inv_freq = jnp.tile(inv_freq, 2 * HEADS)[None, :]
    sign = jnp.tile(jnp.concatenate([-jnp.ones(HEAD_DIM // 2, F32), jnp.ones(HEAD_DIM // 2, F32)]), HEADS)[None, :]
    rope = _rope_tables(pos, inv_freq, sign, "rope_tables")
    mods = [[mod[l:l + 1, i * d:(i + 1) * d] for i in range(6)] for l in range(2)]
    mix = [(m[0], m[1], m[2]) for m in mods]
    ffn = [(m[3], m[4], m[5]) for m in mods]
    gm = [norm_mix_g[l:l + 1] for l in range(2)]
    gf = [norm_ffn_g[l:l + 1] for l in range(2)]

    mixer_w = {**mixer_w, **fetch("l0_mix", x)}
    x1, s_mix = _mixer_forward(x, mix[0], gm[0], mixer_w, rope, "l0_mix")
    ffn_w0 = {**ffn_w[0], **fetch("l0_ffn", x1)}
    x2, s_ffn0 = _ffn_forward(x1, ffn[0], gf[0], ffn_w0, "l0_ffn")
    conv_w = {**conv_w, **fetch("l1_conv", x2)}
    x3, s_conv = _conformer_forward(x2, mix[1], gm[1], conv_w, "l1_conv")
    ffn_w1 = {**ffn_w[1], **fetch("l1_ffn", x3)}
    x4, s_ffn1 = _ffn_forward(x3, ffn[1], gf[1], ffn_w1, "l1_ffn")
    below = lambda saved, m: (saved[-1], m[2])
    dx, loss, dy, dg = _loss_head(x4, target, below(s_ffn1, ffn[1]), "loss_head")
    (dx, dy, dg), g_ffn1 = _ffn_backward(dx, dy, dg, s_ffn1, ffn[1], gf[1], ffn_w1, "l1_ffn",
                                         functools.partial(emit, "l1_ffn"), below(s_conv, mix[1]))
    (dx, dy, dg), g_conv = _conformer_backward(dx, dy, dg, s_conv, mix[1], gm[1], conv_w, "l1_conv",
                                               functools.partial(emit, "l1_conv"), below(s_ffn0, ffn[0]))
    (dx, dy, dg), g_ffn0 = _ffn_backward(dx, dy, dg, s_ffn0, ffn[0], gf[0], ffn_w0, "l0_ffn",
                                         functools.partial(emit, "l0_ffn"), below(s_mix, mix[0]))
    (dx,), g_mix = _mixer_backward(dx, dy, dg, s_mix, mix[0], gm[0], mixer_w, rope, "l0_mix",
                                   functools.partial(emit, "l0_mix"), None)
    blocks = [g_mix, g_ffn0, g_conv, g_ffn1]
    dmod = jnp.stack([jnp.concatenate([a["sh"], a["sc"], a["gate"], b["sh"], b["sc"], b["gate"]], axis=1)[0]
                      for a, b in ((g_mix, g_ffn0), (g_conv, g_ffn1))])
    return loss, dx, dmod, blocks


def _pack(arrs, rows=8):
    flat = jnp.concatenate([a.reshape(-1).astype(F32) for a in arrs])
    n = flat.shape[0]
    cols = -(-n // (rows * 128)) * 128
    return jnp.pad(flat, (0, rows * cols - n)).reshape(rows, cols)


def _unpack(flat, shapes):
    out, off = [], 0
    for shp in shapes:
        n = math.prod(shp)
        out.append(flat[..., off:off + n].reshape(flat.shape[:-1] + tuple(shp)))
        off += n
    return out


def _take_block(a, idx, size, axis):
    return lax.dynamic_slice_in_dim(a, idx * size, size, axis)


def kernel(x, c, positions, ada_w, ada_b, norm_mix_g, norm_ffn_g, ab_w_in, a_vnorm_g, a_spatial_w, a_spatial_b, b_q_norm_g, b_k_norm_g, ab_w_out, conv_pw1_w, conv_pw1_b, conv_dw_w, conv_dw_b, conv_ln_g, conv_ln_b, conv_pw2_w, conv_pw2_b, ffn_up_w, ffn_dw_w, ffn_dw_b, ffn_down_w, loss_target, m_ada_w, m_ada_b, m_norm_mix_g, m_norm_ffn_g, m_ab_w_in, m_a_vnorm_g, m_a_spatial_w, m_a_spatial_b, m_b_q_norm_g, m_b_k_norm_g, m_ab_w_out, m_conv_pw1_w, m_conv_pw1_b, m_conv_dw_w, m_conv_dw_b, m_conv_ln_g, m_conv_ln_b, m_conv_pw2_w, m_conv_pw2_b, m_ffn_up_w, m_ffn_dw_w, m_ffn_dw_b, m_ffn_down_w, v_ada_w, v_ada_b, v_norm_mix_g, v_norm_ffn_g, v_ab_w_in, v_a_vnorm_g, v_a_spatial_w, v_a_spatial_b, v_b_q_norm_g, v_b_k_norm_g, v_ab_w_out, v_conv_pw1_w, v_conv_pw1_b, v_conv_dw_w, v_conv_dw_b, v_conv_ln_g, v_conv_ln_b, v_conv_pw2_w, v_conv_pw2_b, v_ffn_up_w, v_ffn_dw_w, v_ffn_dw_b, v_ffn_down_w):
    weights = dict(ada_w=ada_w, ada_b=ada_b, norm_mix_g=norm_mix_g, norm_ffn_g=norm_ffn_g, ab_w_in=ab_w_in, a_vnorm_g=a_vnorm_g, a_spatial_w=a_spatial_w, a_spatial_b=a_spatial_b, b_q_norm_g=b_q_norm_g, b_k_norm_g=b_k_norm_g, ab_w_out=ab_w_out, conv_pw1_w=conv_pw1_w, conv_pw1_b=conv_pw1_b, conv_dw_w=conv_dw_w, conv_dw_b=conv_dw_b, conv_ln_g=conv_ln_g, conv_ln_b=conv_ln_b, conv_pw2_w=conv_pw2_w, conv_pw2_b=conv_pw2_b, ffn_up_w=ffn_up_w, ffn_dw_w=ffn_dw_w, ffn_dw_b=ffn_dw_b, ffn_down_w=ffn_down_w)
    mom1 = dict(ada_w=m_ada_w, ada_b=m_ada_b, norm_mix_g=m_norm_mix_g, norm_ffn_g=m_norm_ffn_g, ab_w_in=m_ab_w_in, a_vnorm_g=m_a_vnorm_g, a_spatial_w=m_a_spatial_w, a_spatial_b=m_a_spatial_b, b_q_norm_g=m_b_q_norm_g, b_k_norm_g=m_b_k_norm_g, ab_w_out=m_ab_w_out, conv_pw1_w=m_conv_pw1_w, conv_pw1_b=m_conv_pw1_b, conv_dw_w=m_conv_dw_w, conv_dw_b=m_conv_dw_b, conv_ln_g=m_conv_ln_g, conv_ln_b=m_conv_ln_b, conv_pw2_w=m_conv_pw2_w, conv_pw2_b=m_conv_pw2_b, ffn_up_w=m_ffn_up_w, ffn_dw_w=m_ffn_dw_w, ffn_dw_b=m_ffn_dw_b, ffn_down_w=m_ffn_down_w)
    mom2 = dict(ada_w=v_ada_w, ada_b=v_ada_b, norm_mix_g=v_norm_mix_g, norm_ffn_g=v_norm_ffn_g, ab_w_in=v_ab_w_in, a_vnorm_g=v_a_vnorm_g, a_spatial_w=v_a_spatial_w, a_spatial_b=v_a_spatial_b, b_q_norm_g=v_b_q_norm_g, b_k_norm_g=v_b_k_norm_g, ab_w_out=v_ab_w_out, conv_pw1_w=v_conv_pw1_w, conv_pw1_b=v_conv_pw1_b, conv_dw_w=v_conv_dw_w, conv_dw_b=v_conv_dw_b, conv_ln_g=v_conv_ln_g, conv_ln_b=v_conv_ln_b, conv_pw2_w=v_conv_pw2_w, conv_pw2_b=v_conv_pw2_b, ffn_up_w=v_ffn_up_w, ffn_dw_w=v_ffn_dw_w, ffn_dw_b=v_ffn_dw_b, ffn_down_w=v_ffn_down_w)
    order = list(weights)
    d, f2 = D_MODEL, 2 * FFN_DIM
    t = x.shape[1]
    me = 4 * lax.axis_index("x") + 2 * lax.axis_index("y") + lax.axis_index("c")
    for window, dil in PATTERNS:
        assert window // dil == Q_BLOCK and t % (dil * Q_BLOCK) == 0

    small_in = [c[0], conv_pw1_b[0], conv_dw_w[0], conv_dw_b[0], conv_ln_g[0], conv_ln_b[0], conv_pw2_b[0], ffn_dw_w]
    g1 = _all_gather_vmem(_pack(small_in, rows=8), "gather_small").reshape(N_DEV, -1)
    c_all, pw1_b, dw_w, dw_b, ln_g, ln_b, pw2_b, fdw_w = _unpack(g1, [a.shape for a in small_in])
    pw1_b, dw_b, ln_g, ln_b, pw2_b = [a.reshape(1, -1) for a in (pw1_b, dw_b, ln_g, ln_b, pw2_b)]
    dw_w = dw_w.transpose(1, 0, 2).reshape(CONV_WIDTH, d)
    fdw_w = fdw_w.transpose(1, 2, 0, 3).reshape(2, FFN_CONV_WIDTH, f2)

    c16 = jnp.pad(c_all, ((0, 2 * N_DEV - c_all.shape[0]), (0, 0)))
    part = jnp.concatenate([_ada_fwd(c16, ada_w[l], f"ada_fwd{l}")[:N_DEV] for l in range(2)], axis=1)
    g2 = _all_gather_vmem(part, "gather_mod").reshape(N_DEV, N_DEV, 2, -1)
    mod = lax.dynamic_index_in_dim(g2, me, axis=1, keepdims=False).transpose(1, 0, 2).reshape(2, 6 * d) + ada_b

    stages = dict(l0_mix=[ab_w_in[0].T, ab_w_out[0]], l0_ffn=[ffn_up_w[0].T, ffn_down_w[0]],
                  l1_conv=[conv_pw1_w[0].T, conv_pw2_w[0]], l1_ffn=[ffn_up_w[1].T, ffn_down_w[1]])
    stages = {k: [s.astype(BF16) for s in v] for k, v in stages.items()}
    names = dict(l0_mix=("w_in_t", "w_out"), l0_ffn=("up_t", "down"), l1_conv=("pw1_t", "pw2"), l1_ffn=("up_t", "down"))
    ready = {"l0_mix": [a.reshape(-1, d) for a in _all_gather_hbm(stages["l0_mix"], "gather_mixer_weights")]}
    behind = (ready, mod)
    arriving = {}
    for stage, group in (("l0_ffn", ("l0_ffn",)), ("l1_conv", ("l1_conv", "l1_ffn"))):
        srcs, _ = lax.optimization_barrier(([s for g in group for s in stages[g]], behind))
        arriving[stage], behind = _pushes_start(
            srcs, _landing_zones(srcs, f"gather_{stage}_zones"), False, f"gather_{stage}_start")
        mod = mod + behind[0:1, 0:1]

    def fetch(stage, after):
        if stage in arriving:
            full = [a.reshape(-1, d) for a in _pushes_wait(arriving[stage], after, f"gather_{stage}_wait")[1]]
            ready[stage] = full[:2]
            if stage == "l1_conv":
                ready["l1_ffn"] = full[2:]
        return dict(zip(names[stage], ready[stage]))

    causal = jnp.tril(jnp.ones((CHUNK, CHUNK), bool))
    wtril = jnp.where(causal[None], a_spatial_w[0], 0.0)
    mixer_w = dict(
        gain=a_vnorm_g[0].reshape(1, A_WIDTH), wtril=wtril.astype(BF16),
        wtril_t=wtril.transpose(0, 2, 1).astype(BF16),
        bias_exp=jnp.repeat(a_spatial_b[0].T, GROUP_DIM, axis=1),
        gq=jnp.tile(b_q_norm_g[0], HEADS)[None, :], gk=jnp.tile(b_k_norm_g[0], HEADS)[None, :],
        seg=jnp.kron(jnp.eye(HEADS, dtype=BF16), jnp.ones((HEAD_DIM, HEAD_DIM), BF16)))
    conv_w = dict(pw1_b=pw1_b, dw_w=dw_w, dw_b=dw_b, ln_g=ln_g, ln_b=ln_b, pw2_b=pw2_b)
    ffn_w = [dict(dw_w=fdw_w[l].reshape(FFN_CONV_WIDTH, 2, FFN_DIM).transpose(1, 0, 2), dw_b=ffn_dw_b[l].reshape(2, 1, FFN_DIM))
             for l in range(2)]

    leaving = {}

    def emit(stage, grads):
        blocks = [g.reshape(N_DEV, g.shape[0] // N_DEV, d) for g in grads]
        leaving[stage], token = _pushes_start(
            blocks, [lax.empty(b.shape, b.dtype) for b in blocks], True, f"reduce_{stage}_start")
        return token

    loss, dx, dmod, (g_mix, g_ffn0, g_conv, g_ffn1) = _local_step(
        x[0], loss_target[0], positions[0].astype(F32)[:, None], mod, norm_mix_g, norm_ffn_g, mixer_w, conv_w, ffn_w,
        fetch, emit)

    me_op = me.astype(jnp.int32).reshape(1)

    def reduced(stage, after):
        blocks, lands = _pushes_wait(leaving[stage], after, f"reduce_{stage}_wait")
        return [_sum_with_own(b, a, me_op, f"reduce_{stage}_sum{i}") for i, (b, a) in enumerate(zip(blocks, lands))]

    (r_up_t1, r_down1), (r_pw1_t, r_pw2), (r_up_t0, r_down0) = [reduced(s, dx) for s in ("l1_ffn", "l1_conv", "l0_ffn")]

    small_g = [
        dmod, jnp.concatenate([g_mix["norm_g"], g_conv["norm_g"]]), jnp.concatenate([g_ffn0["norm_g"], g_ffn1["norm_g"]]),
        g_mix["vnorm_g"], g_mix["spatial_w"], g_mix["spatial_b"], g_mix["q_norm_g"], g_mix["k_norm_g"],
        g_conv["pw1_b"], g_conv["dw_w"], g_conv["dw_b"], g_conv["ln_g"], g_conv["ln_b"], g_conv["pw2_b"],
        jnp.stack([g_ffn0["dw_w"], g_ffn1["dw_w"]]), jnp.concatenate([g_ffn0["dw_b"], g_ffn1["dw_b"]])]
    packed = _pack(small_g, rows=8)
    small_leaving, _ = _pushes_start([packed], [lax.empty((N_DEV,) + packed.shape, F32)], False, "gather_small_grads_start")

    grads = dict(conv_pw2_w=r_pw2[None], ffn_down_w=jnp.stack([r_down0, r_down1]))
    grads_t = dict(conv_pw1_w=r_pw1_t[None], ffn_up_w=jnp.stack([r_up_t0, r_up_t1]))
    flip = lambda a: jnp.swapaxes(a, 1, 2)
    delta, new_m, new_v = {}, {}, {}

    def update(name):
        if name in grads_t:
            grads[name] = flip(grads_t[name])
            res = _adamw(flip(weights[name]), grads_t[name], flip(mom1[name]), flip(mom2[name]), f"adamw_{name}")
            delta[name], new_m[name], new_v[name] = [flip(r) for r in res]
        else:
            delta[name], new_m[name], new_v[name] = _adamw(weights[name], grads[name], mom1[name], mom2[name], f"adamw_{name}")

    for name in ("conv_pw1_w", "conv_pw2_w", "ffn_up_w", "ffn_down_w"):
        update(name)
    r_in_t, r_out = reduced("l0_mix", new_v["ffn_down_w"])
    grads_t["ab_w_in"], grads["ab_w_out"] = r_in_t[None], r_out[None]
    update("ab_w_in")
    update("ab_w_out")

    (packed,), (landed,) = _pushes_wait(small_leaving, tuple(new_v.values()), "gather_small_grads_wait")
    total = _sum_in_device_order(packed, landed, me_op, "sum_small_grads")
    (s_dmod, s_mix_g, s_ffn_g, s_vnorm, s_sp_w, s_sp_b, s_gq, s_gk, s_pw1_b, s_dw_w, s_dw_b, s_ln_g, s_ln_b,
     s_pw2_b, s_fdw_w, s_fdw_b) = _unpack(total.reshape(-1), [a.shape for a in small_g])
    dmod_all = lax.dynamic_update_slice(
        landed.reshape(N_DEV, -1)[:, :dmod.size].reshape((N_DEV,) + dmod.shape), dmod[None], (me, 0, 0))
    n_ada = ada_w.shape[2]
    dmod16 = jnp.pad(_take_block(dmod_all, me, n_ada, 2), ((0, N_DEV), (0, 0), (0, 0)))
    grads.update(
        ada_w=jnp.stack([_ada_bwd(c16, dmod16[:, l], f"ada_bwd{l}") for l in range(2)]),
        ada_b=s_dmod, norm_mix_g=s_mix_g, norm_ffn_g=s_ffn_g,
        a_vnorm_g=s_vnorm[None], a_spatial_w=s_sp_w[None], a_spatial_b=s_sp_b[None], b_q_norm_g=s_gq[None],
        b_k_norm_g=s_gk[None],
        conv_pw1_b=_take_block(s_pw1_b, me, conv_pw1_b.shape[1], 1),
        conv_dw_w=_take_block(s_dw_w, me, conv_dw_w.shape[2], 1)[None],
        conv_dw_b=_take_block(s_dw_b, me, conv_dw_b.shape[1], 1), conv_ln_g=_take_block(s_ln_g, me, conv_ln_g.shape[1], 1),
        conv_ln_b=_take_block(s_ln_b, me, conv_ln_b.shape[1], 1),
        conv_pw2_b=_take_block(s_pw2_b, me, conv_pw2_b.shape[1], 1),
        ffn_dw_w=_take_block(s_fdw_w, me, ffn_dw_w.shape[2], 2), ffn_dw_b=s_fdw_b)
    update("ada_w")
    large = ("ada_w", "conv_pw1_w", "conv_pw2_w", "ffn_up_w", "ffn_down_w", "ab_w_in", "ab_w_out")
    small = [n for n in order if n not in large]
    res = _adamw_small(*[[src[n] for n in small] for src in (weights, grads, mom1, mom2)], "adamw_small")
    for dst, arrs in zip((delta, new_m, new_v), res):
        dst.update(zip(small, arrs))

    loss = lax.psum(loss[0, 0], ("x", "y", "c"))
    return (loss, dx[None], *[grads[n] for n in order], *[delta[n] for n in order],
            *[new_m[n] for n in order], *[new_v[n] for n in order])
```

```python
import functools
import math

import jax
import jax.numpy as jnp
from jax import lax
from jax.experimental import pallas as pl
from jax.experimental.pallas import tpu as pltpu

F32 = jnp.float32
BF16 = jnp.bfloat16
MESH = pl.DeviceIdType.MESH

D_MODEL = 1024
A_WIDTH = 512
A_GROUPS = 4
GROUP_DIM = 128
CHUNK = 128
B_WIDTH = 512
HEADS = 8
HEAD_DIM = 64
PATTERNS = ((128, 1), (512, 4), (2048, 16))
Q_BLOCK = 128
ROPE_THETA = 10000.0
AB_IN = 2560
CONV_WIDTH = 31
FFN_DIM = 2816
FFN_CONV_WIDTH = 3
EPS = 1e-6
NEG = -1e30
N_DEV = 8
ADAM_LR, ADAM_B1, ADAM_B2, ADAM_EPS, ADAM_WD, ADAM_STEP = 0.001, 0.9, 0.999, 1e-08, 0.01, 10

V7X_VMEM_LIMIT = 56 * 2**20
FFN_HALO = 16
CONV_HALO = 32

_NN = (((1,), (0,)), ((), ()))
_NT = (((1,), (1,)), ((), ()))
_TN = (((0,), (0,)), ((), ()))


def _tile(n, prefs=(512, 256, 128)):
    for t in prefs:
        if n % t == 0:
            return t
    return n


def _row_tile(n, cap=512):
    best = n
    for t in range(8, min(n, cap) + 1, 8):
        if n % t == 0:
            best = t
    return best if best <= cap else n


def _params(*sem):
    return pltpu.CompilerParams(dimension_semantics=sem, vmem_limit_bytes=V7X_VMEM_LIMIT)


def _dot(a, b, dims):
    return lax.dot_general(a, b, dims, preferred_element_type=F32)


def _sigmoid(x):
    return 1.0 / (1.0 + jnp.exp(-x))


def _gelu(x):
    return 0.5 * x * (1.0 + lax.erf(x * (2.0 ** -0.5)))


def _gelu_grad(x):
    return 0.5 * (1.0 + lax.erf(x * (2.0 ** -0.5))) + x * jnp.exp(-0.5 * x * x) * (1.0 / math.sqrt(2.0 * math.pi))


def _colsum(v):
    return jnp.sum(v, axis=0, keepdims=True)


MATMUL_VMEM_BUDGET = 40 * 2**20


def _matmul_tiles(m, n, k, out_bytes, with_resid):
    def options(dim):
        opts = [t for t in (1024, 512, 256, 128) if dim % t == 0]
        return opts + [dim] if dim <= 4096 and dim not in opts else opts

    best = None
    for tm in options(m):
        for tn in options(n):
            need = 4 * (tm * k + k * tn) + tm * tn * (4 + 2 * out_bytes) + (24 * tm * tn if with_resid else 0)
            if need <= MATMUL_VMEM_BUDGET and (best is None or tm * tn / (tm + tn) > best[0]):
                best = (tm * tn / (tm + tn), tm, tn)
    return best[1], best[2]


def _matmul_tn_acc(a, b, name, tk=1024):
    squeeze = a.ndim == 2
    a3 = a[None] if squeeze else a
    p_, t, m = a3.shape
    n = b.shape[1]
    nk = t // tk

    def body(a_ref, b_ref, o_ref, acc_ref):
        kt = pl.program_id(1)

        @pl.when(kt == 0)
        def _():
            acc_ref[...] = jnp.zeros_like(acc_ref)

        acc_ref[...] += _dot(a_ref[...], b_ref[...], _TN)

        @pl.when(kt == nk - 1)
        def _():
            o_ref[...] = acc_ref[...].astype(BF16)

    out = pl.pallas_call(
        body, name=name, grid=(p_, nk),
        in_specs=[pl.BlockSpec((None, tk, m), lambda p, kt: (p, kt, 0)), pl.BlockSpec((tk, n), lambda p, kt: (kt, 0))],
        out_specs=pl.BlockSpec((None, m, n), lambda p, kt: (p, 0, 0)), out_shape=jax.ShapeDtypeStruct((p_, m, n), BF16),
        scratch_shapes=[pltpu.VMEM((m, n), F32)], compiler_params=_params("parallel", "arbitrary"),
    )(a3, b)
    return out[0] if squeeze else out


def _matmul(a, b, mode, out_dtype, name, bias=None, resid=None):
    if mode == "nn":
        (m, k), (_, n) = a.shape, b.shape
    elif mode == "nt":
        (m, k), (n, _) = a.shape, b.shape
    else:
        (k, m), (_, n) = a.shape, b.shape
    tm, tn = _matmul_tiles(m, n, k, jnp.dtype(out_dtype).itemsize, resid is not None)
    dims = {"nn": _NN, "nt": _NT, "tn": _TN}[mode]
    a_spec = pl.BlockSpec((k, tm), lambda i, j: (0, i)) if mode == "tn" else pl.BlockSpec((tm, k), lambda i, j: (i, 0))
    b_spec = pl.BlockSpec((tn, k), lambda i, j: (j, 0)) if mode == "nt" else pl.BlockSpec((k, tn), lambda i, j: (0, j))
    in_specs, args = [a_spec, b_spec], [a, b]
    row_spec = pl.BlockSpec((1, tn), lambda i, j: (0, j))
    tile_spec = pl.BlockSpec((tm, tn), lambda i, j: (i, j))
    if bias is not None:
        in_specs.append(row_spec)
        args.append(bias)
    if resid is not None:
        in_specs += [tile_spec, row_spec]
        args += list(resid)
    out_shape = [jax.ShapeDtypeStruct((m, n), out_dtype)]
    out_specs = [tile_spec]
    if resid is not None:
        out_shape.append(jax.ShapeDtypeStruct((m, n), F32))
        out_specs.append(tile_spec)

    def body(*refs):
        a_ref, b_ref = refs[0], refs[1]
        pos = 2
        acc = _dot(a_ref[...], b_ref[...], dims)
        if bias is not None:
            acc = acc + refs[pos][...]
            pos += 1
        if resid is not None:
            x_ref, g_ref = refs[pos], refs[pos + 1]
            pos += 2
        refs[pos][...] = acc.astype(out_dtype)
        if resid is not None:
            refs[pos + 1][...] = x_ref[...] + g_ref[...] * acc

    outs = pl.pallas_call(
        body, name=name, grid=(m // tm, n // tn), in_specs=in_specs, out_specs=out_specs, out_shape=out_shape,
        compiler_params=_params("parallel", "parallel"),
    )(*args)
    return outs if resid is not None else outs[0]


def _modnorm(x, g, sc, sh, name):
    t, d = x.shape
    tm = _tile(t)
    row = pl.BlockSpec((1, d), lambda i: (0, 0))
    blk = pl.BlockSpec((tm, d), lambda i: (i, 0))

    def body(x_ref, g_ref, sc_ref, sh_ref, o_ref):
        x = x_ref[...]
        r = lax.rsqrt(jnp.mean(x * x, axis=-1, keepdims=True) + EPS)
        o_ref[...] = ((x * r) * g_ref[...] * (1.0 + sc_ref[...]) + sh_ref[...]).astype(BF16)

    return pl.pallas_call(
        body, name=name, grid=(t // tm,), in_specs=[blk, row, row, row], out_specs=blk,
        out_shape=jax.ShapeDtypeStruct((t, d), BF16), compiler_params=_params("parallel"),
    )(x, g, sc, sh)


def _gate_bwd_tile(dx, y_ref, gate_ref, dy_ref, dgate_ref, first):
    @pl.when(first)
    def _():
        dgate_ref[...] = jnp.zeros_like(dgate_ref)

    dy_ref[...] = (dx * gate_ref[...]).astype(BF16)
    dgate_ref[...] += _colsum(dx * y_ref[...].astype(F32))


def _modnorm_bwd(x, dh, g, sc, dres, below, name):
    t, d = x.shape
    tm = _tile(t)
    row = pl.BlockSpec((1, d), lambda i: (0, 0))
    blk = pl.BlockSpec((tm, d), lambda i: (i, 0))

    def body(x_ref, dh_ref, g_ref, sc_ref, dres_ref, *rest):
        dx_ref, dw_ref, dsh_ref = rest[-5:-2] if below else rest
        first = pl.program_id(0) == 0

        @pl.when(first)
        def _():
            dw_ref[...] = jnp.zeros_like(dw_ref)
            dsh_ref[...] = jnp.zeros_like(dsh_ref)

        x = x_ref[...]
        dh = dh_ref[...].astype(F32)
        r = lax.rsqrt(jnp.mean(x * x, axis=-1, keepdims=True) + EPS)
        xn = x * r
        dxn = dh * (g_ref[...] * (1.0 + sc_ref[...]))
        dx = dres_ref[...] + r * (dxn - xn * jnp.mean(dxn * xn, axis=-1, keepdims=True))
        dx_ref[...] = dx
        dw_ref[...] += _colsum(dh * xn)
        dsh_ref[...] += _colsum(dh)
        if below:
            _gate_bwd_tile(dx, rest[0], rest[1], rest[-2], rest[-1], first)

    row_out = jax.ShapeDtypeStruct((1, d), F32)
    return pl.pallas_call(
        body, name=name, grid=(t // tm,), in_specs=[blk, blk, row, row, blk] + ([blk, row] if below else []),
        out_specs=[blk, row, row] + ([blk, row] if below else []),
        out_shape=[jax.ShapeDtypeStruct((t, d), F32), row_out, row_out]
        + ([jax.ShapeDtypeStruct((t, d), BF16), row_out] if below else []),
        compiler_params=_params("arbitrary"),
    )(x, dh, g, sc, dres, *(below or ()))


def _loss_head(y, target, below, name):
    t, d = y.shape
    tm = _tile(t)
    blk = pl.BlockSpec((tm, d), lambda i: (i, 0))
    row = pl.BlockSpec((1, d), lambda i: (0, 0))
    one = pl.BlockSpec((1, 1), lambda i: (0, 0))
    steps = t // tm

    def body(y_ref, t_ref, yb_ref, gate_ref, dx_ref, loss_ref, dy_ref, dgate_ref, acc_ref):
        first = pl.program_id(0) == 0

        @pl.when(first)
        def _():
            acc_ref[...] = jnp.zeros_like(acc_ref)

        e = y_ref[...] - t_ref[...]
        dx = e * (1.0 / d)
        dx_ref[...] = dx
        acc_ref[...] += _colsum(e * e)
        _gate_bwd_tile(dx, yb_ref, gate_ref, dy_ref, dgate_ref, first)

        @pl.when(pl.program_id(0) == steps - 1)
        def _():
            loss_ref[...] = jnp.sum(acc_ref[...], axis=1, keepdims=True) * (0.5 / d)

    return pl.pallas_call(
        body, name=name, grid=(steps,), in_specs=[blk, blk, blk, row], out_specs=[blk, one, blk, row],
        out_shape=[jax.ShapeDtypeStruct((t, d), F32), jax.ShapeDtypeStruct((1, 1), F32),
                   jax.ShapeDtypeStruct((t, d), BF16), jax.ShapeDtypeStruct((1, d), F32)],
        scratch_shapes=[pltpu.VMEM((1, d), F32)], compiler_params=_params("arbitrary"),
    )(y, target, *below)


def _group_norm(vg, gain):
    mu = jnp.mean(vg, axis=-1, keepdims=True)
    xc = vg - mu
    rstd = lax.rsqrt(jnp.mean(xc * xc, axis=-1, keepdims=True) + EPS)
    xhat = xc * rstd
    return xhat, rstd, xhat * gain


def _gmlp_fwd(z, gain, wtril, bias_exp, name):
    t = z.shape[0]
    zu = pl.BlockSpec((CHUNK, A_WIDTH), lambda i: (i, 0))
    zv = pl.BlockSpec((CHUNK, A_WIDTH), lambda i: (i, 1))
    full2 = lambda shp: pl.BlockSpec(shp, lambda i: (0, 0))
    w_spec = pl.BlockSpec((A_GROUPS, CHUNK, CHUNK), lambda i: (0, 0, 0))

    def body(zu_ref, zv_ref, gain_ref, w_ref, b_ref, ya_ref):
        ua = _gelu(zu_ref[...].astype(F32))
        vg = _gelu(zv_ref[...].astype(F32))
        for g in range(A_GROUPS):
            sl = slice(g * GROUP_DIM, (g + 1) * GROUP_DIM)
            _, _, vn = _group_norm(vg[:, sl], gain_ref[:, sl])
            f = _dot(w_ref[g], vn.astype(BF16), _NN) + b_ref[:, sl]
            ya_ref[:, sl] = (ua[:, sl] * f).astype(BF16)

    return pl.pallas_call(
        body, name=name, grid=(t // CHUNK,),
        in_specs=[zu, zv, full2((1, A_WIDTH)), w_spec, full2((CHUNK, A_WIDTH))], out_specs=zu,
        out_shape=jax.ShapeDtypeStruct((t, A_WIDTH + B_WIDTH), BF16), compiler_params=_params("parallel"),
    )(z, z, gain, wtril, bias_exp)


def _gmlp_bwd(z, dcat, gain, wtril, wtril_t, bias_exp, name):
    t = z.shape[0]
    zu = pl.BlockSpec((CHUNK, A_WIDTH), lambda i: (i, 0))
    zv = pl.BlockSpec((CHUNK, A_WIDTH), lambda i: (i, 1))
    full2 = lambda shp: pl.BlockSpec(shp, lambda i: (0, 0))
    w_spec = pl.BlockSpec((A_GROUPS, CHUNK, CHUNK), lambda i: (0, 0, 0))
    dz_spec = pl.BlockSpec((CHUNK, 2 * A_WIDTH), lambda i: (i, 0))

    def body(zu_ref, zv_ref, dya_ref, gain_ref, w_ref, wt_ref, b_ref, dz_ref, dw_ref, dgain_ref, dbias_ref):
        @pl.when(pl.program_id(0) == 0)
        def _():
            dw_ref[...] = jnp.zeros_like(dw_ref)
            dgain_ref[...] = jnp.zeros_like(dgain_ref)
            dbias_ref[...] = jnp.zeros_like(dbias_ref)

        zu_v = zu_ref[...].astype(F32)
        zv_v = zv_ref[...].astype(F32)
        dya = dya_ref[...].astype(F32)
        ua = _gelu(zu_v)
        vg = _gelu(zv_v)
        row = lax.broadcasted_iota(jnp.int32, (CHUNK, CHUNK), 0)
        col = lax.broadcasted_iota(jnp.int32, (CHUNK, CHUNK), 1)
        for g in range(A_GROUPS):
            sl = slice(g * GROUP_DIM, (g + 1) * GROUP_DIM)
            gain_g = gain_ref[:, sl]
            xhat, rstd, vn = _group_norm(vg[:, sl], gain_g)
            vn16 = vn.astype(BF16)
            f = _dot(w_ref[g], vn16, _NN) + b_ref[:, sl]
            df = dya[:, sl] * ua[:, sl]
            df16 = df.astype(BF16)
            dz_ref[:, sl] = (dya[:, sl] * f * _gelu_grad(zu_v[:, sl])).astype(BF16)
            dw_ref[g] += jnp.where(row >= col, _dot(df16, vn16, _NT), 0.0)
            dvn = _dot(wt_ref[g], df16, _NN)
            dgain_ref[:, sl] += _colsum(dvn * xhat)
            dxh = dvn * gain_g
            dvg = rstd * (dxh - jnp.mean(dxh, axis=-1, keepdims=True) - xhat * jnp.mean(dxh * xhat, axis=-1, keepdims=True))
            dz_ref[:, A_WIDTH + g * GROUP_DIM:A_WIDTH + (g + 1) * GROUP_DIM] = (dvg * _gelu_grad(zv_v[:, sl])).astype(BF16)
            dbias_ref[:, sl] += df

    return pl.pallas_call(
        body, name=name, grid=(t // CHUNK,),
        in_specs=[zu, zv, zu, full2((1, A_WIDTH)), w_spec, w_spec, full2((CHUNK, A_WIDTH))],
        out_specs=[dz_spec, w_spec, full2((1, A_WIDTH)), full2((CHUNK, A_WIDTH))],
        out_shape=[jax.ShapeDtypeStruct((t, 2 * A_WIDTH), BF16), jax.ShapeDtypeStruct((A_GROUPS, CHUNK, CHUNK), F32),
                   jax.ShapeDtypeStruct((1, A_WIDTH), F32), jax.ShapeDtypeStruct((CHUNK, A_WIDTH), F32)],
        compiler_params=_params("arbitrary"),
    )(z, z, dcat, gain, wtril, wtril_t, bias_exp)


def _rope_tables(pos, inv_freq, sign, name):
    t = pos.shape[0]
    tm = _tile(t)
    row = pl.BlockSpec((1, B_WIDTH), lambda i: (0, 0))
    blk = pl.BlockSpec((tm, B_WIDTH), lambda i: (i, 0))

    def body(pos_ref, f_ref, s_ref, cos_ref, sin_ref):
        ang = pos_ref[...] * f_ref[:, 0:LANES]
        cos_ref[...] = jnp.tile(jnp.cos(ang), (1, B_WIDTH // LANES))
        sin_ref[...] = jnp.tile(jnp.sin(ang) * s_ref[:, 0:LANES], (1, B_WIDTH // LANES))

    return pl.pallas_call(
        body, name=name, grid=(t // tm,), in_specs=[pl.BlockSpec((tm, 1), lambda i: (i, 0)), row, row],
        out_specs=[blk, blk], out_shape=[jax.ShapeDtypeStruct((t, B_WIDTH), F32)] * 2,
        compiler_params=_params("parallel"),
    )(pos, inv_freq, sign)


def _head_sum(v, seg):
    hi = v.astype(BF16)
    lo = (v - hi.astype(F32)).astype(BF16)
    return _dot(hi, seg, _NN) + _dot(lo, seg, _NN)


def _swap_halves(v):
    lane = lax.broadcasted_iota(jnp.int32, v.shape, 1)
    return jnp.where((lane & (HEAD_DIM - 1)) < HEAD_DIM // 2,pltpu.roll(v, B_WIDTH - HEAD_DIM // 2, 1), pltpu.roll(v, HEAD_DIM // 2, 1))


DILATIONS = tuple(dil for _, dil in PATTERNS)
SUBSEQ_TM = 256
LANES = 128


def _subseq_shape(t, dil):
    return (t // dil, dil * B_WIDTH)


def _subseq_spec(tm, dil):
    return pl.BlockSpec((tm // dil, dil * B_WIDTH), lambda i: (i, 0))


def _to_subseq(x, scr_ref, dil):
    if dil == 1:
        return x
    tm, w = x.shape
    for c in range(w // LANES):
        scr_ref[c * tm:(c + 1) * tm, :] = x[:, c * LANES:(c + 1) * LANES]
    return jnp.concatenate([scr_ref[pl.ds(c * tm + r, tm // dil, stride=dil), :]
                            for r in range(dil) for c in range(w // LANES)], axis=1)


def _from_subseq(y, scr_ref, dil):
    if dil == 1:
        return y
    n, w = y.shape[0], y.shape[1] // dil
    tm = n * dil
    for r in range(dil):
        for c in range(w // LANES):
            scr_ref[pl.ds(c * tm + r, n, stride=dil), :] = y[:, r * w + c * LANES:r * w + (c + 1) * LANES]
    return jnp.concatenate([scr_ref[c * tm:(c + 1) * tm, :] for c in range(w // LANES)], axis=1)


def _subseq_scratch(tm):
    return pltpu.VMEM((B_WIDTH // LANES * tm, LANES), F32)


def _qk_prep(z, cos_t, sin_t, gq, gk, seg, name):
    t = z.shape[0]
    tm = _tile(t, (SUBSEQ_TM,))
    col = lambda c: pl.BlockSpec((tm, B_WIDTH), lambda i: (i, c))
    row = pl.BlockSpec((1, B_WIDTH), lambda i: (0, 0))
    blk = col(0)
    nd = len(DILATIONS)

    def body(q_ref, k_ref, v_ref, cos_ref, sin_ref, gq_ref, gk_ref, seg_ref, *rest):
        out_refs, scr_ref = rest[:-1], rest[-1]

        def norm_rot(x, g):
            r = lax.rsqrt(_head_sum(x * x, seg_ref[...]) * (1.0 / HEAD_DIM) + EPS)
            xn = x * r * g
            return xn * cos_ref[...] + _swap_halves(xn) * sin_ref[...]

        vals = (norm_rot(q_ref[...].astype(F32), gq_ref[...]), norm_rot(k_ref[...].astype(F32), gk_ref[...]),
                v_ref[...].astype(F32))
        for a, val in enumerate(vals):
            for b, dil in enumerate(DILATIONS):
                out_refs[a * nd + b][...] = _to_subseq(val, scr_ref, dil).astype(BF16)

    outs = pl.pallas_call(
        body, name=name, grid=(t // tm,),
        in_specs=[col(2), col(3), col(4), blk, blk, row, row, pl.BlockSpec((B_WIDTH, B_WIDTH), lambda i: (0, 0))],
        out_specs=[_subseq_spec(tm, dil) for _ in range(3) for dil in DILATIONS],
        out_shape=[jax.ShapeDtypeStruct(_subseq_shape(t, dil), BF16) for _ in range(3) for dil in DILATIONS],
        scratch_shapes=[_subseq_scratch(tm)], compiler_params=_params("parallel"),
    )(z, z, z, cos_t, sin_t, gq, gk, seg)
    return outs[:nd], outs[nd:2 * nd], outs[2 * nd:]


def _qk_prep_bwd(z, dqs, dks, dvs, cos_t, sin_t, gq, gk, seg, name):
    t = z.shape[0]
    tm = _tile(t, (SUBSEQ_TM,))
    col = lambda c: pl.BlockSpec((tm, B_WIDTH), lambda i: (i, c))
    row = pl.BlockSpec((1, B_WIDTH), lambda i: (0, 0))
    blk = col(0)
    nb = len(DILATIONS)
    subs = [_subseq_spec(tm, dil) for dil in DILATIONS]

    def body(*refs):
        q_ref, k_ref = refs[0], refs[1]
        dq_refs, dk_refs, dv_refs = refs[2:2 + nb], refs[2 + nb:2 + 2 * nb], refs[2 + 2 * nb:2 + 3 * nb]
        cos_ref, sin_ref, gq_ref, gk_ref, seg_ref, dz_ref, dgq_ref, dgk_ref, scr_ref = refs[2 + 3 * nb:]

        @pl.when(pl.program_id(0) == 0)
        def _():
            dgq_ref[...] = jnp.zeros_like(dgq_ref)
            dgk_ref[...] = jnp.zeros_like(dgk_ref)

        def total(d_refs):
            return sum(_from_subseq(r_[...], scr_ref, dil) for r_, dil in zip(d_refs, DILATIONS))

        def back(x, d_refs, g, dg_ref):
            dout = total(d_refs)
            dy = dout * cos_ref[...] + _swap_halves(dout * sin_ref[...])
            r = lax.rsqrt(_head_sum(x * x, seg_ref[...]) * (1.0 / HEAD_DIM) + EPS)
            xn = x * r
            dg_ref[...] += _colsum(dy * xn)
            dxn = dy * g
            return r * (dxn - xn * (_head_sum(dxn * xn, seg_ref[...]) * (1.0 / HEAD_DIM)))

        dz_ref[:, 0:B_WIDTH] = back(q_ref[...].astype(F32), dq_refs, gq_ref[...], dgq_ref).astype(BF16)
        dz_ref[:, B_WIDTH:2 * B_WIDTH] = back(k_ref[...].astype(F32), dk_refs, gk_ref[...], dgk_ref).astype(BF16)
        dz_ref[:, 2 * B_WIDTH:3 * B_WIDTH] = total(dv_refs).astype(BF16)

    return pl.pallas_call(
        body, name=name, grid=(t // tm,),
        in_specs=[col(2), col(3)] + subs * 3 + [blk, blk, row, row, pl.BlockSpec((B_WIDTH, B_WIDTH), lambda i: (0, 0))],
        out_specs=[pl.BlockSpec((tm, 3 * B_WIDTH), lambda i: (i, 0)), row, row],
        out_shape=[jax.ShapeDtypeStruct((t, 3 * B_WIDTH), BF16), jax.ShapeDtypeStruct((1, B_WIDTH), F32),
                   jax.ShapeDtypeStruct((1, B_WIDTH), F32)],
        scratch_shapes=[_subseq_scratch(tm)], compiler_params=_params("arbitrary"),
    )(z, z, *dqs, *dks, *dvs, cos_t, sin_t, gq, gk, seg)


def _subseq_views(x, col, name):
    t = x.shape[0]
    tm = _tile(t, (SUBSEQ_TM,))

    def body(x_ref, *rest):
        out_refs, scr_ref = rest[:-1], rest[-1]
        val = x_ref[...].astype(F32)
        for o_ref, dil in zip(out_refs, DILATIONS):
            o_ref[...] = _to_subseq(val, scr_ref, dil).astype(o_ref.dtype)

    return pl.pallas_call(
        body, name=name, grid=(t // tm,), in_specs=[pl.BlockSpec((tm, B_WIDTH), lambda i: (i, col))],
        out_specs=[_subseq_spec(tm, dil) for dil in DILATIONS],
        out_shape=[jax.ShapeDtypeStruct(_subseq_shape(t, dil), x.dtype) for dil in DILATIONS],
        scratch_shapes=[_subseq_scratch(tm)], compiler_params=_params("parallel"),
    )(x)


def _attn_fwd(q, k, v, dil, name):
    t = q.shape[0] * dil
    nb = t // dil // Q_BLOCK
    cur = pl.BlockSpec((Q_BLOCK, B_WIDTH), lambda r, i: (i, r))
    prev = pl.BlockSpec((Q_BLOCK, B_WIDTH), lambda r, i: (jnp.maximum(i - 1, 0), r))

    def body(q_ref, kp_ref, kc_ref, vp_ref, vc_ref, o_ref, lse_ref):
        i = pl.program_id(1)
        q = q_ref[...]
        kk = jnp.concatenate([kp_ref[...], kc_ref[...]], axis=0)
        vv = jnp.concatenate([vp_ref[...], vc_ref[...]], axis=0)
        a = lax.broadcasted_iota(jnp.int32, (Q_BLOCK, 2 * Q_BLOCK), 0)
        j = lax.broadcasted_iota(jnp.int32, (Q_BLOCK, 2 * Q_BLOCK), 1)
        dist = a + Q_BLOCK - j
        mask = (dist >= 0) & (dist <= Q_BLOCK) & ((j >= Q_BLOCK) | (i > 0))
        sls = [slice(h * HEAD_DIM, (h + 1) * HEAD_DIM) for h in range(HEADS)]
        scores = [_dot(q[:, sl], kk[:, sl], _NT) for sl in sls]
        ps, dens = [], []
        for sl, s in zip(sls, scores):
            s = jnp.where(mask, s * (HEAD_DIM ** -0.5), NEG)
            m = jnp.max(s, axis=-1, keepdims=True)
            p = jnp.exp(s - m)
            den = jnp.sum(p, axis=-1, keepdims=True)
            ps.append(p.astype(BF16))
            dens.append(den)
            lse_ref[:, sl] = jnp.broadcast_to(m + jnp.log(den), (Q_BLOCK, HEAD_DIM))
        for sl, p, den in zip(sls, ps, dens):
            o_ref[:, sl] = _dot(p, vv[:, sl], _NN) / den

    return pl.pallas_call(
        body, name=name, grid=(dil, nb), in_specs=[cur, prev, cur, prev, cur], out_specs=[cur, cur],
        out_shape=[jax.ShapeDtypeStruct(_subseq_shape(t, dil), F32)] * 2,
        compiler_params=_params("parallel", "parallel"),
    )(q, k, k, v, v)


def _attn_merge(outs, lses, cat, name):
    nb = len(DILATIONS)
    t = cat.shape[0]
    tm = _tile(t, (SUBSEQ_TM,))
    subs = [_subseq_spec(tm, dil) for dil in DILATIONS]

    def body(*refs):
        o_refs, l_refs = refs[:nb], refs[nb:2 * nb]
        yb_refs, lse_refs, cat_ref, scr_ref = refs[2 * nb + 1:3 * nb + 1], refs[3 * nb + 1:4 * nb + 1], refs[4 * nb + 1], refs[4 * nb + 2]
        ls = [_from_subseq(r[...], scr_ref, dil) for r, dil in zip(l_refs, DILATIONS)]
        m = functools.reduce(jnp.maximum, ls)
        tot = m + jnp.log(sum(jnp.exp(l - m) for l in ls))
        yb = sum(jnp.exp(l - tot) * _from_subseq(o[...], scr_ref, dil) for l, o, dil in zip(ls, o_refs, DILATIONS))
        cat_ref[...] = yb.astype(BF16)
        yb = yb.astype(BF16).astype(F32)
        for yb_ref, lse_ref, dil in zip(yb_refs, lse_refs, DILATIONS):
            yb_ref[...] = _to_subseq(yb, scr_ref, dil).astype(BF16)
            lse_ref[...] = _to_subseq(tot, scr_ref, dil)

    outs_ = pl.pallas_call(
        body, name=name, grid=(t // tm,), in_specs=subs * 2 + [pl.BlockSpec(memory_space=pl.ANY)],
        out_specs=subs * 2 + [pl.BlockSpec((tm, B_WIDTH), lambda i: (i, A_WIDTH // B_WIDTH))],
        out_shape=[jax.ShapeDtypeStruct(_subseq_shape(t, dil), BF16) for dil in DILATIONS]
        + [jax.ShapeDtypeStruct(_subseq_shape(t, dil), F32) for dil in DILATIONS] + [jax.ShapeDtypeStruct(cat.shape, BF16)],
        input_output_aliases={2 * nb: 2 * nb}, scratch_shapes=[_subseq_scratch(tm)], compiler_params=_params("parallel"),
    )(*outs, *lses, cat)
    return outs_[:nb], outs_[nb:2 * nb], outs_[2 * nb]


def _attn_bwd(q, k, v, do, o, lse, dil, name):
    t = q.shape[0] * dil
    nb = t // dil // Q_BLOCK
    cur = pl.BlockSpec((Q_BLOCK, B_WIDTH), lambda r, i: (i, r))
    prev = pl.BlockSpec((Q_BLOCK, B_WIDTH), lambda r, i: (jnp.maximum(i - 1, 0), r))
    scale = HEAD_DIM ** -0.5

    def body(q_ref, kp_ref, kc_ref, vp_ref, vc_ref, do_ref, o_ref, lse_ref, dq_ref, dk_ref, dv_ref,
             ck_ref, cv_ref, tk_ref, tv_ref):
        i = pl.program_id(1)

        @pl.when(i == 0)
        def _():
            ck_ref[...] = jnp.zeros_like(ck_ref)
            cv_ref[...] = jnp.zeros_like(cv_ref)

        q = q_ref[...]
        kk = jnp.concatenate([kp_ref[...], kc_ref[...]], axis=0)
        vv = jnp.concatenate([vp_ref[...], vc_ref[...]], axis=0)
        do = do_ref[...]
        dof = do.astype(F32)
        of = o_ref[...].astype(F32)
        a = lax.broadcasted_iota(jnp.int32, (Q_BLOCK, 2 * Q_BLOCK), 0)
        j = lax.broadcasted_iota(jnp.int32, (Q_BLOCK, 2 * Q_BLOCK), 1)
        dist = a + Q_BLOCK - j
        mask = (dist >= 0) & (dist <= Q_BLOCK) & ((j >= Q_BLOCK) | (i > 0))
        sls = [slice(h * HEAD_DIM, (h + 1) * HEAD_DIM) for h in range(HEADS)]
        scores = [_dot(q[:, sl], kk[:, sl], _NT) for sl in sls]
        dps = [_dot(do[:, sl], vv[:, sl], _NT) for sl in sls]
        ps, dss = [], []
        for sl, s, dp in zip(sls, scores, dps):
            p = jnp.exp(jnp.where(mask, s * scale, NEG) - lse_ref[:, sl.start:sl.start + 1])
            delta = jnp.sum(dof[:, sl] * of[:, sl], axis=-1, keepdims=True)
            dss.append((p * (dp - delta) * scale).astype(BF16))
            ps.append(p.astype(BF16))
        for sl, p, ds in zip(sls, ps, dss):
            dq_ref[:, sl] = _dot(ds, kk[:, sl], _NN)
            dv_t = _dot(do[:, sl], p, _TN)
            dk_t = _dot(q[:, sl], ds, _TN)
            tk_ref[sl, :] = ck_ref[sl, :] + dk_t[:, :Q_BLOCK]
            tv_ref[sl, :] = cv_ref[sl, :] + dv_t[:, :Q_BLOCK]
            ck_ref[sl, :] = dk_t[:, Q_BLOCK:]
            cv_ref[sl, :] = dv_t[:, Q_BLOCK:]

        @pl.when(i >= 1)
        def _():
            rows = pl.ds(pl.multiple_of((i - 1) * Q_BLOCK, Q_BLOCK), Q_BLOCK)
            dk_ref[rows, :] = tk_ref[...].T
            dv_ref[rows, :] = tv_ref[...].T

        @pl.when(i == nb - 1)
        def _():
            rows = pl.ds((nb - 1) * Q_BLOCK, Q_BLOCK)
            dk_ref[rows, :] = ck_ref[...].T
            dv_ref[rows, :] = cv_ref[...].T

    whole = pl.BlockSpec((t // dil, B_WIDTH), lambda r, i: (0, r))
    return pl.pallas_call(
        body, name=name, grid=(dil, nb), in_specs=[cur, prev, cur, prev, cur, cur, cur, cur],
        out_specs=[cur, whole, whole], out_shape=[jax.ShapeDtypeStruct(_subseq_shape(t, dil), F32)] * 3,
        scratch_shapes=[pltpu.VMEM((B_WIDTH, Q_BLOCK), F32)] * 4,
        compiler_params=_params("parallel", "arbitrary"),
    )(q, k, k, v, v, do, o, lse)


FFN_TN = 256
FFN_FWD_CHUNK = 256
FFN_BWD_CHUNK = 128


def _ffn_up(h, up_t, name):
    t, k = h.shape
    tm = _tile(t)

    def body(h_ref, w_ref, o_ref):
        o_ref[...] = _dot(h_ref[...], w_ref[...], _NT).astype(BF16)

    return pl.pallas_call(
        body, name=name, grid=(2, t // tm),
        in_specs=[pl.BlockSpec((tm, k), lambda p, i: (i, 0)), pl.BlockSpec((None, FFN_DIM, k), lambda p, i: (p, 0, 0))],
        out_specs=pl.BlockSpec((None, tm, FFN_DIM), lambda p, i: (p, i, 0)),
        out_shape=jax.ShapeDtypeStruct((2, t, FFN_DIM), BF16), compiler_params=_params("parallel", "parallel"),
    )(h, up_t.reshape(2, FFN_DIM, k))


def _ffn_up_dx(du, up_t, name):
    t = du.shape[1]
    k = up_t.shape[1]
    tm = _tile(t)

    def body(a_ref, b_ref, o_ref):
        o_ref[...] = _dot(a_ref[0], b_ref[0], _NN) + _dot(a_ref[1], b_ref[1], _NN)

    return pl.pallas_call(
        body, name=name, grid=(t // tm,),
        in_specs=[pl.BlockSpec((2, tm, FFN_DIM), lambda i: (0, i, 0)), pl.BlockSpec((2, FFN_DIM, k), lambda i: (0, 0, 0))],
        out_specs=pl.BlockSpec((tm, k), lambda i: (i, 0)), out_shape=jax.ShapeDtypeStruct((t, k), F32),
        compiler_params=_params("parallel"),
    )(du, up_t.reshape(2, FFN_DIM, k))


def _ffn_conv(win, w_ref, b_ref, p):
    x = win.astype(F32)
    x0, x1, x2 = x[FFN_HALO:], pltpu.roll(x, 1, 0)[FFN_HALO:], pltpu.roll(x, 2, 0)[FFN_HALO:]
    return b_ref[p] + w_ref[p, 2:3, :] * x0 + w_ref[p, 1:2, :] * x1 + w_ref[p, 0:1, :] * x2


def _zero_if(cond, v):
    return jnp.where(cond, 0, v).astype(v.dtype)


def _ffn_act(u, dw_w, dw_b, name):
    t = u.shape[1]
    tm = _tile(t)
    chunk = min(FFN_FWD_CHUNK, tm)
    hb = tm // FFN_HALO
    main = pl.BlockSpec((2, tm, FFN_TN), lambda i, j: (0, i, j))
    halo = pl.BlockSpec((2, FFN_HALO, FFN_TN), lambda i, j: (0, jnp.maximum(i * hb - 1, 0), j))
    wsp = pl.BlockSpec((2, FFN_CONV_WIDTH, FFN_TN), lambda i, j: (0, 0, j))
    bsp = pl.BlockSpec((2, 1, FFN_TN), lambda i, j: (0, 0, j))

    def body(u_ref, uh_ref, w_ref, b_ref, o_ref, z_ref):
        first = pl.program_id(0) == 0

        def emit(rows, wins):
            za, zb = _ffn_conv(wins[0], w_ref, b_ref, 0), _ffn_conv(wins[1], w_ref, b_ref, 1)
            o_ref[rows, :] = (za * _sigmoid(za) * zb).astype(BF16)
            z_ref[0, rows, :] = za.astype(BF16)
            z_ref[1, rows, :] = zb.astype(BF16)

        emit(pl.ds(0, chunk), [jnp.concatenate([_zero_if(first, uh_ref[p]), u_ref[p, 0:chunk, :]], axis=0) for p in range(2)])

        def step(c, carry):
            s = pl.multiple_of(c * chunk, chunk)
            emit(pl.ds(s, chunk), [u_ref[p, pl.ds(s - FFN_HALO, chunk + FFN_HALO), :] for p in range(2)])
            return carry

        lax.fori_loop(1, tm // chunk, step, 0)

    return pl.pallas_call(
        body, name=name, grid=(t // tm, FFN_DIM // FFN_TN), in_specs=[main, halo, wsp, bsp],
        out_specs=[pl.BlockSpec((tm, FFN_TN), lambda i, j: (i, j)), main],
        out_shape=[jax.ShapeDtypeStruct((t, FFN_DIM), BF16), jax.ShapeDtypeStruct((2, t, FFN_DIM), BF16)],
        compiler_params=_params("parallel", "parallel"),
    )(u, u, dw_w, dw_b)


def _fold8(v):
    return jnp.sum(v.reshape(v.shape[0] // 8, 8, v.shape[1]), axis=0)


def _ffn_act_bwd(u, z, dact, dw_w, name):
    t = u.shape[1]
    tm = _tile(t)
    chunk = min(FFN_BWD_CHUNK, tm // 2)
    halo = FFN_HALO
    hb = tm // halo
    nt = t // tm
    last_halo = t // halo - 1
    next_i = lambda i: jnp.minimum((i + 1) * hb, last_halo)
    main = pl.BlockSpec((2, tm, FFN_TN), lambda j, i: (0, i, j))
    nxt = pl.BlockSpec((2, halo, FFN_TN), lambda j, i: (0, next_i(i), j))
    wsp = pl.BlockSpec((2, FFN_CONV_WIDTH, FFN_TN), lambda j, i: (0, 0, j))
    bsp = pl.BlockSpec((2, 1, FFN_TN), lambda j, i: (0, 0, j))

    def body(u_ref, z_ref, zn_ref, da_ref, dan_ref, w_ref, du_ref, dw_ref, db_ref, acc_ref):
        i = pl.program_id(1)
        last = i == nt - 1
        acc_ref[...] = jnp.zeros_like(acc_ref)

        def emit(rows, zs, dact):
            n = chunk + halo
            za, zb, dact = zs[0].astype(F32), zs[1].astype(F32), dact.astype(F32)
            sg = _sigmoid(za)
            dzs = (dact * zb * (sg * (1.0 + za * (1.0 - sg))), dact * (za * sg))
            for p, dz in enumerate(dzs):
                ahead = (dz[:chunk], pltpu.roll(dz, n - 1, 0)[:chunk], pltpu.roll(dz, n - 2, 0)[:chunk])
                um = u_ref[p, rows, :].astype(F32)
                acc_ref[p, FFN_CONV_WIDTH] += _fold8(ahead[0])
                du = None
                for j, dzj in enumerate(ahead):
                    k = FFN_CONV_WIDTH - 1 - j
                    acc_ref[p, k] += _fold8(dzj * um)
                    term = w_ref[p, k:k + 1, :] * dzj
                    du = term if du is None else du + term
                du_ref[p, rows, :] = du.astype(BF16)

        def step(c, carry):
            s = pl.multiple_of(c * chunk, chunk)
            emit(pl.ds(s, chunk), [z_ref[p, pl.ds(s, chunk + halo), :] for p in range(2)], da_ref[pl.ds(s, chunk + halo), :])
            return carry

        lax.fori_loop(0, tm // chunk - 1, step, 0)
        s = tm - chunk
        emit(pl.ds(s, chunk),
             [jnp.concatenate([z_ref[p, s:tm, :], zn_ref[p]], axis=0) for p in range(2)],
             jnp.concatenate([da_ref[s:tm, :], _zero_if(last, dan_ref[...])], axis=0))

        @pl.when(i == 0)
        def _():
            dw_ref[...] = jnp.zeros_like(dw_ref)
            db_ref[...] = jnp.zeros_like(db_ref)

        for p in range(2):
            for k in range(FFN_CONV_WIDTH):
                dw_ref[p, k:k + 1, :] += _colsum(acc_ref[p, k])
            db_ref[p] += _colsum(acc_ref[p, FFN_CONV_WIDTH])

    return pl.pallas_call(
        body, name=name, grid=(FFN_DIM // FFN_TN, nt),
        in_specs=[main, main, nxt, pl.BlockSpec((tm, FFN_TN), lambda j, i: (i, j)),
                  pl.BlockSpec((halo, FFN_TN), lambda j, i: (next_i(i), j)), wsp],
        out_specs=[main, wsp, bsp],
        out_shape=[jax.ShapeDtypeStruct((2, t, FFN_DIM), BF16), jax.ShapeDtypeStruct((2, FFN_CONV_WIDTH, FFN_DIM), F32),
                   jax.ShapeDtypeStruct((2, 1, FFN_DIM), F32)],
        scratch_shapes=[pltpu.VMEM((2, FFN_CONV_WIDTH + 1, 8, FFN_TN), F32)],
        compiler_params=_params("parallel", "arbitrary"),
    )(u, z, z, dact, dact, dw_w)


CONV_TM = 256
CONV_ROWS = 128
CONV_LANES = 128


def _glu_window(pa_ref, pah_ref, pg_ref, pgh_ref, scr_ref, first):
    ah, gh = pah_ref[...].astype(F32), pgh_ref[...].astype(F32)
    scr_ref[0:CONV_HALO, :] = jnp.where(first, 0.0, ah * _sigmoid(gh))
    scr_ref[CONV_HALO:, :] = pa_ref[...].astype(F32) * _sigmoid(pg_ref[...].astype(F32))


def _tap_slabs(win, rows, ahead):
    n = win.shape[0]
    for s in range(8):
        ws = win if s == 0 else pltpu.roll(win, n - s if ahead else s, 0)
        for q in range(CONV_HALO // 8):
            o = 8 * q + s
            if o < CONV_WIDTH:
                start = 8 * q if ahead else CONV_HALO - 8 * q
                yield CONV_WIDTH - 1 - o, ws[start:start + rows]


def _conformer_specs(t):
    tm = _tile(t, (CONV_TM, 128))
    hb = tm // CONV_HALO
    d = D_MODEL
    main = lambda c: pl.BlockSpec((tm, d), lambda i: (i, c))
    halo = lambda c: pl.BlockSpec((CONV_HALO, d), lambda i: (jnp.maximum(i * hb - 1, 0), c))
    row = pl.BlockSpec((1, d), lambda i: (0, 0))
    wsp = pl.BlockSpec((CONV_WIDTH, d), lambda i: (0, 0))
    return tm, main, halo, row, wsp


def _conformer_mid(p, dw_w, dw_b, ln_g, ln_b, name):
    t = p.shape[0]
    tm, main, halo, row, wsp = _conformer_specs(t)
    d, lanes = D_MODEL, CONV_LANES

    def body(pa_ref, pah_ref, pg_ref, pgh_ref, w_ref, b_ref, g_ref, lb_ref, o_ref, dc_ref, scr_ref):
        _glu_window(pa_ref, pah_ref, pg_ref, pgh_ref, scr_ref, pl.program_id(0) == 0)
        for c in range(d // lanes):
            ls = slice(c * lanes, (c + 1) * lanes)
            acc = jnp.broadcast_to(b_ref[:, ls], (tm, lanes))
            for k, slab in _tap_slabs(scr_ref[:, ls], tm, False):
                acc = acc + w_ref[k:k + 1, ls] * slab
            dc_ref[:, ls] = acc

        def norm(r, carry):
            r0 = pl.multiple_of(r * 32, 32)
            dc = dc_ref[pl.ds(r0, 32), :]
            xc = dc - jnp.mean(dc, axis=-1, keepdims=True)
            ln = xc * lax.rsqrt(jnp.mean(xc * xc, axis=-1, keepdims=True) + EPS) * g_ref[...] + lb_ref[...]
            o_ref[pl.ds(r0, 32), :] = (ln * _sigmoid(ln)).astype(BF16)
            return carry

        lax.fori_loop(0, tm // 32, norm, 0)

    return pl.pallas_call(
        body, name=name, grid=(t // tm,), in_specs=[main(0), halo(0), main(1), halo(1), wsp, row, row, row],
        out_specs=[main(0), main(0)], out_shape=[jax.ShapeDtypeStruct((t, d), BF16), jax.ShapeDtypeStruct((t, d), F32)],
        scratch_shapes=[pltpu.VMEM((tm + CONV_HALO, d), F32)], compiler_params=_params("parallel"),
    )(p, p, p, p, dw_w, dw_b, ln_g, ln_b)


def _conformer_mid_bwd(p, dc, ds, ln_g, ln_b, name):
    t = p.shape[0]
    tm, main, halo, row, wsp = _conformer_specs(t)
    d, nt = D_MODEL, t // tm
    rows, lanes = CONV_ROWS, CONV_LANES

    def body(pa_ref, pah_ref, pg_ref, pgh_ref, dc_ref, ds_ref, g_ref, lb_ref,
             ddc_ref, dw_ref, db_ref, dg_ref, dlb_ref, scr_ref, wacc_ref, racc_ref):
        i = pl.program_id(0)

        @pl.when(i == 0)
        def _():
            wacc_ref[...] = jnp.zeros_like(wacc_ref)
            racc_ref[...] = jnp.zeros_like(racc_ref)

        _glu_window(pa_ref, pah_ref, pg_ref, pgh_ref, scr_ref, i == 0)

        def norm_bwd(r, carry):
            r0 = pl.multiple_of(r * 32, 32)
            dcv = dc_ref[pl.ds(r0, 32), :]
            xc = dcv - jnp.mean(dcv, axis=-1, keepdims=True)
            rstd = lax.rsqrt(jnp.mean(xc * xc, axis=-1, keepdims=True) + EPS)
            xhat = xc * rstd
            ln = xhat * g_ref[...] + lb_ref[...]
            sg = _sigmoid(ln)
            dln = ds_ref[pl.ds(r0, 32), :].astype(F32) * (sg * (1.0 + ln * (1.0 - sg)))
            dxh = dln * g_ref[...]
            ddc = rstd * (dxh - jnp.mean(dxh, axis=-1, keepdims=True) - xhat * jnp.mean(dxh * xhat, axis=-1, keepdims=True))
            ddc_ref[pl.ds(r0, 32), :] = ddc
            racc_ref[0] += _fold8(dln * xhat)
            racc_ref[1] += _fold8(dln)
            racc_ref[2] += _fold8(ddc)
            return carry

        lax.fori_loop(0, tm // 32, norm_bwd, 0)

        for c in range(d // lanes):
            ls = slice(c * lanes, (c + 1) * lanes)

            def taps(r, carry, ls=ls):
                r0 = pl.multiple_of(r * rows, rows)
                ddc = ddc_ref[pl.ds(r0, rows), ls]
                for k, slab in _tap_slabs(scr_ref[pl.ds(r0, rows + CONV_HALO), ls], rows, False):
                    wacc_ref[k, :, ls] += _fold8(ddc * slab)
                return carry

            lax.fori_loop(0, tm // rows, taps, 0)

        @pl.when(i == nt - 1)
        def _():
            for k in range(CONV_WIDTH):
                dw_ref[k:k + 1, :] = _colsum(wacc_ref[k])
            dg_ref[...] = _colsum(racc_ref[0])
            dlb_ref[...] = _colsum(racc_ref[1])
            db_ref[...] = _colsum(racc_ref[2])

    return pl.pallas_call(
        body, name=name, grid=(nt,), in_specs=[main(0), halo(0), main(1), halo(1), main(0), main(0), row, row],
        out_specs=[main(0), wsp, row, row, row],
        out_shape=[jax.ShapeDtypeStruct((t, d), F32), jax.ShapeDtypeStruct((CONV_WIDTH, d), F32)]
        + [jax.ShapeDtypeStruct((1, d), F32)] * 3,
        scratch_shapes=[pltpu.VMEM((tm + CONV_HALO, d), F32), pltpu.VMEM((CONV_WIDTH, 8, d), F32), pltpu.VMEM((3, 8, d), F32)],
        compiler_params=_params("arbitrary"),
    )(p, p, p, p, dc, ds, ln_g, ln_b)


def _conformer_glu_bwd(p, ddc, dw_w, name):
    t = p.shape[0]
    d = D_MODEL
    tm = _tile(t, (CONV_TM, 128))
    hb = tm // CONV_HALO
    nt = t // tm
    last_halo = t // CONV_HALO - 1
    rows, lanes = CONV_ROWS, CONV_LANES
    col = lambda c: pl.BlockSpec((tm, d), lambda i: (i, c))
    nxt = pl.BlockSpec((CONV_HALO, d), lambda i: (jnp.minimum((i + 1) * hb, last_halo), 0))

    def body(pa_ref, pg_ref, ddc_ref, ddcn_ref, w_ref, dp_ref, db_ref, scr_ref, acc_ref):
        i = pl.program_id(0)

        @pl.when(i == 0)
        def _():
            acc_ref[...] = jnp.zeros_like(acc_ref)

        scr_ref[0:tm, :] = ddc_ref[...]
        scr_ref[tm:, :] = _zero_if(i == nt - 1, ddcn_ref[...])
        for c in range(d // lanes):
            ls = slice(c * lanes, (c + 1) * lanes)
            gs = slice(d + c * lanes, d + (c + 1) * lanes)

            def taps(r, carry, ls=ls, gs=gs):
                r0 = pl.multiple_of(r * rows, rows)
                dglu = None
                for k, slab in _tap_slabs(scr_ref[pl.ds(r0, rows + CONV_HALO), ls], rows, True):
                    term = w_ref[k:k + 1, ls] * slab
                    dglu = term if dglu is None else dglu + term
                a = pa_ref[pl.ds(r0, rows), ls].astype(F32)
                sg = _sigmoid(pg_ref[pl.ds(r0, rows), ls].astype(F32))
                da = (dglu * sg).astype(BF16)
                dg = (dglu * a * sg * (1.0 - sg)).astype(BF16)
                dp_ref[pl.ds(r0, rows), ls] = da
                dp_ref[pl.ds(r0, rows), gs] = dg
                acc_ref[:, ls] += _fold8(da.astype(F32))
                acc_ref[:, gs] += _fold8(dg.astype(F32))
                return carry

            lax.fori_loop(0, tm // rows, taps, 0)

        @pl.when(i == nt - 1)
        def _():
            db_ref[...] = _colsum(acc_ref[...])

    return pl.pallas_call(
        body, name=name, grid=(nt,),
        in_specs=[col(0), col(1), col(0), nxt, pl.BlockSpec((CONV_WIDTH, d), lambda i: (0, 0))],
        out_specs=[pl.BlockSpec((tm, 2 * d), lambda i: (i, 0)), pl.BlockSpec((1, 2 * d), lambda i: (0, 0))],
        out_shape=[jax.ShapeDtypeStruct((t, 2 * d), BF16), jax.ShapeDtypeStruct((1, 2 * d), F32)],
        scratch_shapes=[pltpu.VMEM((tm + CONV_HALO, d), F32), pltpu.VMEM((8, 2 * d), F32)],
        compiler_params=_params("arbitrary"),
    )(p, p, ddc, ddc, dw_w)


def _colsum_call(a, name):
    t, n = a.shape
    tm = _tile(t)

    def body(a_ref, o_ref):
        @pl.when(pl.program_id(0) == 0)
        def _():
            o_ref[...] = jnp.zeros_like(o_ref)

        o_ref[...] += _colsum(a_ref[...].astype(F32))

    return pl.pallas_call(
        body, name=name, grid=(t // tm,), in_specs=[pl.BlockSpec((tm, n), lambda i: (i, 0))],
        out_specs=pl.BlockSpec((1, n), lambda i: (0, 0)), out_shape=jax.ShapeDtypeStruct((1, n), F32),
        compiler_params=_params("arbitrary"),
    )(a)


def _ada_fwd(c_all, w, name):
    rows, d = c_all.shape
    n = w.shape[1]
    tn = _tile(n, (256, 128))

    def body(c_ref, w_ref, o_ref):
        c = c_ref[...]
        o_ref[...] = _dot((c * _sigmoid(c)).astype(BF16), w_ref[...].astype(BF16), _NN)

    return pl.pallas_call(
        body, name=name, grid=(n // tn,),
        in_specs=[pl.BlockSpec((rows, d), lambda j: (0, 0)), pl.BlockSpec((d, tn), lambda j: (0, j))],
        out_specs=pl.BlockSpec((rows, tn), lambda j: (0, j)), out_shape=jax.ShapeDtypeStruct((rows, n), F32),
        compiler_params=_params("parallel"),
    )(c_all, w)


def _ada_bwd(c_all, dmod, name):
    rows, d = c_all.shape
    n = dmod.shape[1]
    tn = _tile(n, (256, 128))

    def body(c_ref, g_ref, o_ref):
        c = c_ref[...]
        o_ref[...] = _dot((c * _sigmoid(c)).astype(BF16), g_ref[...].astype(BF16), _TN)

    return pl.pallas_call(
        body, name=name, grid=(n // tn,),
        in_specs=[pl.BlockSpec((rows, d), lambda j: (0, 0)), pl.BlockSpec((rows, tn), lambda j: (0, j))],
        out_specs=pl.BlockSpec((d, tn), lambda j: (0, j)), out_shape=jax.ShapeDtypeStruct((d, n), F32),
        compiler_params=_params("parallel"),
    )(c_all, dmod)


def _sum_in_device_order(own, land, me, name):
    s, r, c = land.shape
    tr = _row_tile(r, 256)
    slot = lambda k: pl.BlockSpec((None, tr, c), lambda i, me_ref: (jnp.where(me_ref[0] == k, (k + 1) % s, k), i, 0))
    own_spec = pl.BlockSpec((tr, c), lambda i, me_ref: (i, 0))

    def body(me_ref, own_ref, *refs):
        o_ref = refs[-1]
        acc = None
        for k, ref in enumerate(refs[:-1]):
            term = jnp.where(me_ref[0] == k, own_ref[...], ref[...]).astype(F32)
            acc = term if acc is None else acc + term
        o_ref[...] = acc

    return pl.pallas_call(
        body, name=name, out_shape=jax.ShapeDtypeStruct((r, c), F32),
        grid_spec=pltpu.PrefetchScalarGridSpec(
            num_scalar_prefetch=1, grid=(r // tr,), in_specs=[own_spec] + [slot(k) for k in range(s)], out_specs=own_spec),
        compiler_params=_params("parallel"),
    )(me, own, *[land] * s)


def _sum_with_own(blocks, land, me, name):
    s, r, c = land.shape
    tr = _row_tile(r, 256)
    slot = lambda k: pl.BlockSpec((None, tr, c), lambda i, me_ref: ((me_ref[0] + k) % s, i, 0))

    def body(me_ref, own_ref, *refs):
        o_ref = refs[-1]
        acc = own_ref[...].astype(F32)
        for ref in refs[:-1]:
            acc = acc + ref[...].astype(F32)
        o_ref[...] = acc

    return pl.pallas_call(
        body, name=name, out_shape=jax.ShapeDtypeStruct((r, c), F32),
        grid_spec=pltpu.PrefetchScalarGridSpec(
            num_scalar_prefetch=1, grid=(r // tr,), in_specs=[slot(0)] + [slot(k) for k in range(1, s)],
            out_specs=pl.BlockSpec((tr, c), lambda i, me_ref: (i, 0))),
        compiler_params=_params("parallel"),
    )(me, blocks, *[land] * (s - 1))


def _adamw_update(w, g, m, v):
    nm = ADAM_B1 * m + (1.0 - ADAM_B1) * g
    nv = ADAM_B2 * v + (1.0 - ADAM_B2) * (g * g)
    m_hat = nm * (1.0 / (1.0 - ADAM_B1 ** ADAM_STEP))
    v_hat = nv * (1.0 / (1.0 - ADAM_B2 ** ADAM_STEP))
    return -ADAM_LR * (m_hat / (jnp.sqrt(v_hat) + ADAM_EPS) + ADAM_WD * w), nm, nv


def _adamw(w, g, m, v, name):
    l, r, c = w.shape
    tr = _row_tile(r, 256)
    blk = pl.BlockSpec((None, tr, c), lambda k, i: (k, i, 0))

    def body(w_ref, g_ref, m_ref, v_ref, d_ref, nm_ref, nv_ref):
        d_ref[...], nm_ref[...], nv_ref[...] = _adamw_update(w_ref[...], g_ref[...], m_ref[...], v_ref[...])

    return pl.pallas_call(
        body, name=name, grid=(l, r // tr), in_specs=[blk] * 4, out_specs=[blk] * 3,
        out_shape=[jax.ShapeDtypeStruct(w.shape, F32)] * 3, compiler_params=_params("parallel", "parallel"),
    )(w, g, m, v)


def _adamw_small(ws, gs, ms, vs, name):
    n = len(ws)
    two_d = lambda a: a.reshape(-1, a.shape[-1])

    def body(*refs):
        ins, outs = refs[:4 * n], refs[4 * n:]
        for a in range(n):
            outs[a][...], outs[n + a][...], outs[2 * n + a][...] = _adamw_update(*[ins[k * n + a][...] for k in range(4)])

    res = pl.pallas_call(
        body, name=name, out_shape=[jax.ShapeDtypeStruct(two_d(w).shape, F32) for w in ws] * 3,
    )(*[two_d(a) for a in (*ws, *gs, *ms, *vs)])
    return [[res[k * n + a].reshape(ws[a].shape) for a in range(n)] for k in range(3)]


def _mesh_pos():
    return lax.axis_index("x"), lax.axis_index("y"), lax.axis_index("c")


def _all_gather_vmem(x_shard, name):
    m_per, n = x_shard.shape

    def body(x_ref, out_ref, send_sems, recv_sems, local_sem):
        x, y, c = _mesh_pos()
        me, sibling = (x, y, c), (x, y, 1 - c)
        chips = [(1 - x, y), (x, 1 - y), (1 - x, 1 - y)]

        def rows(px, py, pc):
            return out_ref.at[pl.ds((4 * px + 2 * py + pc) * m_per, m_per), :]

        def copy(k, block, to, src=None):
            return pltpu.make_async_remote_copy(
                src_ref=rows(*block) if src is None else src, dst_ref=rows(*block),
                send_sem=send_sems.at[k], recv_sem=recv_sems.at[k], device_id=to, device_id_type=MESH)

        mine = pltpu.make_async_copy(x_ref, rows(*me), local_sem)
        mine.start()
        first = [copy(0, me, sibling, src=x_ref)]
        first += [copy(1 + j, me, (*chip, c), src=x_ref) for j, chip in enumerate(chips)]
        for cp in first:
            cp.start()
        passed = [copy(4 + j, (*chip, c), sibling) for j, chip in enumerate(chips)]
        for j, chip in enumerate(chips):
            copy(1 + j, (*chip, c), me).wait_recv()
            passed[j].start()
        copy(0, sibling, me).wait_recv()
        for j, chip in enumerate(chips):
            copy(4 + j, (*chip, 1 - c), me).wait_recv()
        for cp in first + passed:
            cp.wait_send()
        mine.wait()

    return pl.pallas_call(
        body, name=name, out_shape=jax.ShapeDtypeStruct((N_DEV * m_per, n), x_shard.dtype),
        in_specs=[pl.BlockSpec(memory_space=pltpu.VMEM)], out_specs=pl.BlockSpec(memory_space=pltpu.VMEM),
        scratch_shapes=[pltpu.SemaphoreType.DMA((7,)), pltpu.SemaphoreType.DMA((7,)), pltpu.SemaphoreType.DMA],
    )(x_shard)


def _all_gather_hbm(shards, name):
    n = len(shards)
    out_shape = [jax.ShapeDtypeStruct((N_DEV,) + s.shape, s.dtype) for s in shards]

    def body(*refs):
        x_refs, out_refs = refs[:n], refs[n:2 * n]
        send_sems, recv_sems, local_sems = refs[2 * n:]
        x, y, c = _mesh_pos()
        me, sibling = (x, y, c), (x, y, 1 - c)
        chips = [(1 - x, y), (x, 1 - y), (1 - x, 1 - y)]

        def blk(a, p):
            return out_refs[a].at[4 * p[0] + 2 * p[1] + p[2]]

        def copy(a, k, block, to, src=None):
            return pltpu.make_async_remote_copy(
                src_ref=blk(a, block) if src is None else src, dst_ref=blk(a, block),
                send_sem=send_sems.at[7 * a + k], recv_sem=recv_sems.at[7 * a + k], device_id=to, device_id_type=MESH)

        mine = [pltpu.make_async_copy(x_refs[a], blk(a, me), local_sems.at[a]) for a in range(n)]
        for cp in mine:
            cp.start()
        first = []
        for a in range(n):
            first.append(copy(a, 0, me, sibling, src=x_refs[a]))
            first += [copy(a, 1 + j, me, (*chip, c), src=x_refs[a]) for j, chip in enumerate(chips)]
        for cp in first:
            cp.start()
        passed = []
        for j, chip in enumerate(chips):
            for a in range(n):
                copy(a, 1 + j, (*chip, c), me).wait_recv()
                fwd = copy(a, 4 + j, (*chip, c), sibling)
                fwd.start()
                passed.append(fwd)
        for a in range(n):
            copy(a, 0, sibling, me).wait_recv()
            for j, chip in enumerate(chips):
                copy(a, 4 + j, (*chip, 1 - c), me).wait_recv()
        for cp in first + passed:
            cp.wait_send()
        for cp in mine:
            cp.wait()

    return pl.pallas_call(
        body, name=name, out_shape=out_shape, in_specs=[pl.BlockSpec(memory_space=pltpu.VMEM)] * n,
        out_specs=[pl.BlockSpec(memory_space=pl.ANY)] * n,
        scratch_shapes=[pltpu.SemaphoreType.DMA((7 * n,)), pltpu.SemaphoreType.DMA((7 * n,)), pltpu.SemaphoreType.DMA((n,))],
    )(*shards)


def _peers(x, y, c):
    flip = lambda v, f: 1 - v if f else v
    return [(flip(x, m & 4), flip(y, m & 2), flip(c, m & 1)) for m in range(1, N_DEV)]


def _dev_index(p):
    return 4 * p[0] + 2 * p[1] + p[2]


def _push_copies(src_refs, land_refs, send_sems, recv_sems, scatter, receive):
    x, y, c = _mesh_pos()
    me = _dev_index((x, y, c))
    copies = []
    for a, (src, land) in enumerate(zip(src_refs, land_refs)):
        for k, p in enumerate(_peers(x, y, c)):
            copies.append(pltpu.make_async_remote_copy(
                src_ref=src.at[_dev_index(p)] if scatter else src, dst_ref=land.at[_dev_index(p) if receive else me],
                send_sem=send_sems.at[7 * a + k], recv_sem=recv_sems.at[7 * a + k], device_id=p, device_id_type=MESH))
    return copies


_HBM = pl.BlockSpec(memory_space=pltpu.HBM)
_SEM = pl.BlockSpec(memory_space=pltpu.SEMAPHORE)
_EFFECT = pltpu.SideEffectType.DATAFLOW_SIDE_EFFECTING


def _pushes_start(srcs, lands, scatter, name):
    n = len(srcs)

    def body(*refs):
        src_refs, land_refs = refs[:n], refs[n:2 * n]
        send_sems, recv_sems = refs[2 * n], refs[2 * n + 1]
        token = refs[-1]
        for cp in _push_copies(src_refs, land_refs, send_sems, recv_sems, scatter, receive=False):
            cp.start()
        token[...] = jnp.zeros_like(token)

    hbm = lambda a: pltpu.HBM(a.shape, a.dtype)
    sems = pltpu.SemaphoreType.DMA((7 * n,))
    outs = pl.pallas_call(
        body, name=name,
        out_shape=(sems, sems, *[hbm(a) for a in srcs], *[hbm(a) for a in lands], jax.ShapeDtypeStruct((8, 128), F32)),
        in_specs=[_HBM] * (2 * n), out_specs=(_SEM, _SEM, *[_HBM] * (2 * n), pl.BlockSpec(memory_space=pltpu.VMEM)),
        input_output_aliases={i: 2 + i for i in range(2 * n)},
        compiler_params=pltpu.CompilerParams(has_side_effects=_EFFECT),
    )(*[pltpu.with_memory_space_constraint(a, pltpu.HBM) for a in (*srcs, *lands)])
    return (outs[0], outs[1], outs[2:2 + n], outs[2 + n:2 + 2 * n], scatter), outs[-1]


def _pushes_wait(handle, after, name):
    send_sems, recv_sems, srcs, lands, scatter = handle
    n = len(srcs)
    after = after if isinstance(after, (tuple, list)) else (after,)

    def body(*refs):
        src_refs, land_refs = refs[:n], refs[n:2 * n]
        for cp in _push_copies(src_refs, land_refs, refs[2 * n], refs[2 * n + 1], scatter, receive=True):
            cp.wait_send()
            cp.wait_recv()

    hbm = lambda a: pltpu.HBM(a.shape, a.dtype)
    outs = pl.pallas_call(
        body, name=name, out_shape=tuple(hbm(a) for a in (*srcs, *lands)),
        in_specs=[_HBM] * (2 * n) + [_SEM, _SEM] + [pl.BlockSpec(memory_space=pl.ANY)] * len(after),
        out_specs=tuple([_HBM] * (2 * n)), input_output_aliases={i: i for i in range(2 * n)},
        compiler_params=pltpu.CompilerParams(has_side_effects=_EFFECT),
    )(*srcs, *lands, send_sems, recv_sems, *after)
    return outs[:n], outs[n:]


def _landing_zones(srcs, name):
    n = len(srcs)

    def body(*refs):
        src_refs, land_refs, bufs, sems = refs[:n], refs[n:2 * n], refs[2 * n:3 * n], refs[3 * n]
        me = _dev_index(_mesh_pos())
        load = [pltpu.make_async_copy(src, buf, sems.at[a]) for a, (src, buf) in enumerate(zip(src_refs, bufs))]
        store = [pltpu.make_async_copy(buf, land.at[me], sems.at[a]) for a, (buf, land) in enumerate(zip(bufs, land_refs))]
        for cp in load:
            cp.start()
        for ld, st in zip(load, store):
            ld.wait()
            st.start()
        for cp in store:
            cp.wait()

    any_spec = pl.BlockSpec(memory_space=pl.ANY)
    return pl.pallas_call(
        body, name=name, out_shape=[jax.ShapeDtypeStruct((N_DEV,) + s.shape, s.dtype) for s in srcs],
        in_specs=[any_spec] * n, out_specs=[any_spec] * n,
        scratch_shapes=[pltpu.VMEM(s.shape, s.dtype) for s in srcs] + [pltpu.SemaphoreType.DMA((n,))],
        compiler_params=pltpu.CompilerParams(vmem_limit_bytes=V7X_VMEM_LIMIT),
    )(*srcs)


def _ffn_forward(x, mod, norm_g, w, tag):
    sh, sc, gate = mod
    h = _modnorm(x, norm_g, sc, sh, f"{tag}_norm")
    u = _ffn_up(h, w["up_t"], f"{tag}_up")
    act, z = _ffn_act(u, w["dw_w"], w["dw_b"], f"{tag}_act")
    y, x_new = _matmul(act, w["down"], "nn", BF16, f"{tag}_down", resid=(x, gate))
    return x_new, (x, h, u, z, act, y)


def _behind(row, token):
    return row if token is None else row + token[0:1, 0:1]


def _ffn_backward(dx_new, dy, d_gate, saved, mod, norm_g, w, tag, emit, below):
    x, h, u, z, act, _ = saved
    _, sc, _ = mod
    d_down = _matmul_tn_acc(act, dy, f"{tag}_down_dw")
    dact = _matmul(dy, w["down"], "nt", BF16, f"{tag}_down_dx")
    du, d_dw_w, d_dw_b = _ffn_act_bwd(u, z, dact, w["dw_w"], f"{tag}_act_bwd")
    d_up_t = _matmul_tn_acc(du, h, f"{tag}_up_dw").reshape(2 * FFN_DIM, -1)
    token = emit([d_up_t, d_down])
    dh = _ffn_up_dx(du, w["up_t"], f"{tag}_up_dx")
    dx, d_w, d_sh, *dy_below = _modnorm_bwd(x, dh, norm_g, _behind(sc, token), dx_new, below, f"{tag}_norm_bwd")
    return (dx, *dy_below), dict(dw_w=d_dw_w.transpose(1, 0, 2).reshape(FFN_CONV_WIDTH, 2 * FFN_DIM),
                    dw_b=d_dw_b.reshape(1, 2 * FFN_DIM), norm_g=d_w * (1.0 + sc), sh=d_sh, sc=d_w * norm_g, gate=d_gate)


def _mixer_forward(x, mod, norm_g, w, rope, tag):
    sh, sc, gate = mod
    h = _modnorm(x, norm_g, sc, sh, f"{tag}_norm")
    z = _matmul(h, w["w_in_t"], "nt", BF16, f"{tag}_in")
    ya = _gmlp_fwd(z, w["gain"], w["wtril"], w["bias_exp"], f"{tag}_gmlp")
    q, k, v = _qk_prep(z, rope[0], rope[1], w["gq"], w["gk"], w["seg"], f"{tag}_qk")
    outs, lses = zip(*[_attn_fwd(q[b], k[b], v[b], dil, f"{tag}_attn_d{dil}") for b, dil in enumerate(DILATIONS)])
    yb, lse, cat = _attn_merge(outs, lses, ya, f"{tag}_merge")
    y, x_new = _matmul(cat, w["w_out"], "nn", BF16, f"{tag}_out", resid=(x, gate))
    return x_new, (x, h, z, q, k, v, yb, lse, cat, y)


def _mixer_backward(dx_new, dy, d_gate, saved, mod, norm_g, w, rope, tag, emit, below):
    x, h, z, q, k, v, yb, lse, cat, _ = saved
    _, sc, _ = mod
    d_w_out = _matmul_tn_acc(cat, dy, f"{tag}_out_dw")
    dcat = _matmul(dy, w["w_out"], "nt", BF16, f"{tag}_out_dx")
    dz_a, d_sp_w, d_gain, d_bias_exp = _gmlp_bwd(z, dcat, w["gain"], w["wtril"], w["wtril_t"], w["bias_exp"], f"{tag}_gmlp_bwd")
    dyb = _subseq_views(dcat, A_WIDTH // B_WIDTH, f"{tag}_dyb_views")
    dqs, dks, dvs = zip(*[_attn_bwd(q[b], k[b], v[b], dyb[b], yb[b], lse[b], dil, f"{tag}_attn_bwd_d{dil}")
                          for b, dil in enumerate(DILATIONS)])
    dz_qkv, d_gq, d_gk = _qk_prep_bwd(z, dqs, dks, dvs, rope[0], rope[1], w["gq"], w["gk"], w["seg"], f"{tag}_qk_bwd")
    dz = jnp.concatenate([dz_a, dz_qkv], axis=1)
    d_w_in_t = _matmul_tn_acc(dz, h, f"{tag}_in_dw")
    token = emit([d_w_in_t, d_w_out])
    dh = _matmul(dz, w["w_in_t"], "nn", F32, f"{tag}_in_dx")
    dx, d_w, d_sh, *dy_below = _modnorm_bwd(x, dh, norm_g, _behind(sc, token), dx_new, below, f"{tag}_norm_bwd")
    return (dx, *dy_below), dict(
        vnorm_g=d_gain.reshape(A_GROUPS, GROUP_DIM), spatial_w=d_sp_w,
        spatial_b=d_bias_exp.reshape(CHUNK, A_GROUPS, GROUP_DIM).sum(-1).T,
        q_norm_g=d_gq.reshape(HEADS, HEAD_DIM).sum(0), k_norm_g=d_gk.reshape(HEADS, HEAD_DIM).sum(0),
        norm_g=d_w * (1.0 + sc), sh=d_sh, sc=d_w * norm_g, gate=d_gate)


def _conformer_forward(x, mod, norm_g, w, tag):
    sh, sc, gate = mod
    h = _modnorm(x, norm_g, sc, sh, f"{tag}_norm")
    p = _matmul(h, w["pw1_t"], "nt", BF16, f"{tag}_pw1", bias=w["pw1_b"])
    s, dc = _conformer_mid(p, w["dw_w"], w["dw_b"], w["ln_g"], w["ln_b"], f"{tag}_mid")
    y, x_new = _matmul(s, w["pw2"], "nn", BF16, f"{tag}_pw2", bias=w["pw2_b"], resid=(x, gate))
    return x_new, (x, h, p, dc, s, y)


def _conformer_backward(dx_new, dy, d_gate, saved, mod, norm_g, w, tag, emit, below):
    x, h, p, dc, s, _ = saved
    _, sc, _ = mod
    d_pw2 = _matmul_tn_acc(s, dy, f"{tag}_pw2_dw")
    d_pw2_b = _colsum_call(dy, f"{tag}_pw2_db")
    ds = _matmul(dy, w["pw2"], "nt", BF16, f"{tag}_pw2_dx")
    ddc, d_dw_w, d_dw_b, d_ln_g, d_ln_b = _conformer_mid_bwd(p, dc, ds, w["ln_g"], w["ln_b"], f"{tag}_mid_bwd")
    dp, d_pw1_b = _conformer_glu_bwd(p, ddc, w["dw_w"], f"{tag}_glu_bwd")
    d_pw1_t = _matmul_tn_acc(dp, h, f"{tag}_pw1_dw")
    token = emit([d_pw1_t, d_pw2])
    dh = _matmul(dp, w["pw1_t"], "nn", F32, f"{tag}_pw1_dx")
    dx, d_w, d_sh, *dy_below = _modnorm_bwd(x, dh, norm_g, _behind(sc, token), dx_new, below, f"{tag}_norm_bwd")
    return (dx, *dy_below), dict(pw1_b=d_pw1_b, dw_w=d_dw_w, dw_b=d_dw_b, ln_g=d_ln_g, ln_b=d_ln_b, pw2_b=d_pw2_b, norm_g=d_w * (1.0 + sc), sh=d_sh, sc=d_w * norm_g, gate=d_gate)


def _local_step(x, target, pos, mod, norm_mix_g, norm_ffn_g, mixer_w, conv_w, ffn_w, fetch, emit):
    d = D_MODEL
    inv_freq = 1.0 / (ROPE_THETA ** (jnp.arange(0, HEAD_DIM, 2, dtype=F32) / HEAD_DIM))
    inv_freq = jnp.tile(inv_freq, 2 * HEADS)[None, :]
    sign = jnp.tile(jnp.concatenate([-jnp.ones(HEAD_DIM // 2, F32), jnp.ones(HEAD_DIM // 2, F32)]), HEADS)[None, :]
    rope = _rope_tables(pos, inv_freq, sign, "rope_tables")
    mods = [[mod[l:l + 1, i * d:(i + 1) * d] for i in range(6)] for l in range(2)]
    mix = [(m[0], m[1], m[2]) for m in mods]
    ffn = [(m[3], m[4], m[5]) for m in mods]
    gm = [norm_mix_g[l:l + 1] for l in range(2)]
    gf = [norm_ffn_g[l:l + 1] for l in range(2)]

    mixer_w = {**mixer_w, **fetch("l0_mix", x)}
    x1, s_mix = _mixer_forward(x, mix[0], gm[0], mixer_w, rope, "l0_mix")
    ffn_w0 = {**ffn_w[0], **fetch("l0_ffn", x1)}
    x2, s_ffn0 = _ffn_forward(x1, ffn[0], gf[0], ffn_w0, "l0_ffn")
    conv_w = {**conv_w, **fetch("l1_conv", x2)}
    x3, s_conv = _conformer_forward(x2, mix[1], gm[1], conv_w, "l1_conv")
    ffn_w1 = {**ffn_w[1], **fetch("l1_ffn", x3)}
    x4, s_ffn1 = _ffn_forward(x3, ffn[1], gf[1], ffn_w1, "l1_ffn")
    below = lambda saved, m: (saved[-1], m[2])
    dx, loss, dy, dg = _loss_head(x4, target, below(s_ffn1, ffn[1]), "loss_head")
    (dx, dy, dg), g_ffn1 = _ffn_backward(dx, dy, dg, s_ffn1, ffn[1], gf[1], ffn_w1, "l1_ffn",
                                         functools.partial(emit, "l1_ffn"), below(s_conv, mix[1]))
    (dx, dy, dg), g_conv = _conformer_backward(dx, dy, dg, s_conv, mix[1], gm[1], conv_w, "l1_conv",
                                               functools.partial(emit, "l1_conv"), below(s_ffn0, ffn[0]))
    (dx, dy, dg), g_ffn0 = _ffn_backward(dx, dy, dg, s_ffn0, ffn[0], gf[0], ffn_w0, "l0_ffn",
                                         functools.partial(emit, "l0_ffn"), below(s_mix, mix[0]))
    (dx,), g_mix = _mixer_backward(dx, dy, dg, s_mix, mix[0], gm[0], mixer_w, rope, "l0_mix",
                                   functools.partial(emit, "l0_mix"), None)
    blocks = [g_mix, g_ffn0, g_conv, g_ffn1]
    dmod = jnp.stack([jnp.concatenate([a["sh"], a["sc"], a["gate"], b["sh"], b["sc"], b["gate"]], axis=1)[0]
                      for a, b in ((g_mix, g_ffn0), (g_conv, g_ffn1))])
    return loss, dx, dmod, blocks


def _pack(arrs, rows=8):
    flat = jnp.concatenate([a.reshape(-1).astype(F32) for a in arrs])
    n = flat.shape[0]
    cols = -(-n // (rows * 128)) * 128
    return jnp.pad(flat, (0, rows * cols - n)).reshape(rows, cols)


def _unpack(flat, shapes):
    out, off = [], 0
    for shp in shapes:
        n = math.prod(shp)
        out.append(flat[..., off:off + n].reshape(flat.shape[:-1] + tuple(shp)))
        off += n
    return out


def _take_block(a, idx, size, axis):
    return lax.dynamic_slice_in_dim(a, idx * size, size, axis)


def kernel(x, c, positions, ada_w, ada_b, norm_mix_g, norm_ffn_g, ab_w_in, a_vnorm_g, a_spatial_w, a_spatial_b, b_q_norm_g, b_k_norm_g, ab_w_out, conv_pw1_w, conv_pw1_b, conv_dw_w, conv_dw_b, conv_ln_g, conv_ln_b, conv_pw2_w, conv_pw2_b, ffn_up_w, ffn_dw_w, ffn_dw_b, ffn_down_w, loss_target, m_ada_w, m_ada_b, m_norm_mix_g, m_norm_ffn_g, m_ab_w_in, m_a_vnorm_g, m_a_spatial_w, m_a_spatial_b, m_b_q_norm_g, m_b_k_norm_g, m_ab_w_out, m_conv_pw1_w, m_conv_pw1_b, m_conv_dw_w, m_conv_dw_b, m_conv_ln_g, m_conv_ln_b, m_conv_pw2_w, m_conv_pw2_b, m_ffn_up_w, m_ffn_dw_w, m_ffn_dw_b, m_ffn_down_w, v_ada_w, v_ada_b, v_norm_mix_g, v_norm_ffn_g, v_ab_w_in, v_a_vnorm_g, v_a_spatial_w, v_a_spatial_b, v_b_q_norm_g, v_b_k_norm_g, v_ab_w_out, v_conv_pw1_w, v_conv_pw1_b, v_conv_dw_w, v_conv_dw_b, v_conv_ln_g, v_conv_ln_b, v_conv_pw2_w, v_conv_pw2_b, v_ffn_up_w, v_ffn_dw_w, v_ffn_dw_b, v_ffn_down_w):
    weights = dict(ada_w=ada_w, ada_b=ada_b, norm_mix_g=norm_mix_g, norm_ffn_g=norm_ffn_g, ab_w_in=ab_w_in, a_vnorm_g=a_vnorm_g, a_spatial_w=a_spatial_w, a_spatial_b=a_spatial_b, b_q_norm_g=b_q_norm_g, b_k_norm_g=b_k_norm_g, ab_w_out=ab_w_out, conv_pw1_w=conv_pw1_w, conv_pw1_b=conv_pw1_b, conv_dw_w=conv_dw_w, conv_dw_b=conv_dw_b, conv_ln_g=conv_ln_g, conv_ln_b=conv_ln_b, conv_pw2_w=conv_pw2_w, conv_pw2_b=conv_pw2_b, ffn_up_w=ffn_up_w, ffn_dw_w=ffn_dw_w, ffn_dw_b=ffn_dw_b, ffn_down_w=ffn_down_w)
    mom1 = dict(ada_w=m_ada_w, ada_b=m_ada_b, norm_mix_g=m_norm_mix_g, norm_ffn_g=m_norm_ffn_g, ab_w_in=m_ab_w_in, a_vnorm_g=m_a_vnorm_g, a_spatial_w=m_a_spatial_w, a_spatial_b=m_a_spatial_b, b_q_norm_g=m_b_q_norm_g, b_k_norm_g=m_b_k_norm_g, ab_w_out=m_ab_w_out, conv_pw1_w=m_conv_pw1_w, conv_pw1_b=m_conv_pw1_b, conv_dw_w=m_conv_dw_w, conv_dw_b=m_conv_dw_b, conv_ln_g=m_conv_ln_g, conv_ln_b=m_conv_ln_b, conv_pw2_w=m_conv_pw2_w, conv_pw2_b=m_conv_pw2_b, ffn_up_w=m_ffn_up_w, ffn_dw_w=m_ffn_dw_w, ffn_dw_b=m_ffn_dw_b, ffn_down_w=m_ffn_down_w)
    mom2 = dict(ada_w=v_ada_w, ada_b=v_ada_b, norm_mix_g=v_norm_mix_g, norm_ffn_g=v_norm_ffn_g, ab_w_in=v_ab_w_in, a_vnorm_g=v_a_vnorm_g, a_spatial_w=v_a_spatial_w, a_spatial_b=v_a_spatial_b, b_q_norm_g=v_b_q_norm_g, b_k_norm_g=v_b_k_norm_g, ab_w_out=v_ab_w_out, conv_pw1_w=v_conv_pw1_w, conv_pw1_b=v_conv_pw1_b, conv_dw_w=v_conv_dw_w, conv_dw_b=v_conv_dw_b, conv_ln_g=v_conv_ln_g, conv_ln_b=v_conv_ln_b, conv_pw2_w=v_conv_pw2_w, conv_pw2_b=v_conv_pw2_b, ffn_up_w=v_ffn_up_w, ffn_dw_w=v_ffn_dw_w, ffn_dw_b=v_ffn_dw_b, ffn_down_w=v_ffn_down_w)
    order = list(weights)
    d, f2 = D_MODEL, 2 * FFN_DIM
    t = x.shape[1]
    me = 4 * lax.axis_index("x") + 2 * lax.axis_index("y") + lax.axis_index("c")
    for window, dil in PATTERNS:
        assert window // dil == Q_BLOCK and t % (dil * Q_BLOCK) == 0

    small_in = [c[0], conv_pw1_b[0], conv_dw_w[0], conv_dw_b[0], conv_ln_g[0], conv_ln_b[0], conv_pw2_b[0], ffn_dw_w]
    g1 = _all_gather_vmem(_pack(small_in, rows=8), "gather_small").reshape(N_DEV, -1)
    c_all, pw1_b, dw_w, dw_b, ln_g, ln_b, pw2_b, fdw_w = _unpack(g1, [a.shape for a in small_in])
    pw1_b, dw_b, ln_g, ln_b, pw2_b = [a.reshape(1, -1) for a in (pw1_b, dw_b, ln_g, ln_b, pw2_b)]
    dw_w = dw_w.transpose(1, 0, 2).reshape(CONV_WIDTH, d)
    fdw_w = fdw_w.transpose(1, 2, 0, 3).reshape(2, FFN_CONV_WIDTH, f2)

    c16 = jnp.pad(c_all, ((0, 2 * N_DEV - c_all.shape[0]), (0, 0)))
    part = jnp.concatenate([_ada_fwd(c16, ada_w[l], f"ada_fwd{l}")[:N_DEV] for l in range(2)], axis=1)
    g2 = _all_gather_vmem(part, "gather_mod").reshape(N_DEV, N_DEV, 2, -1)
    mod = lax.dynamic_index_in_dim(g2, me, axis=1, keepdims=False).transpose(1, 0, 2).reshape(2, 6 * d) + ada_b

    stages = dict(l0_mix=[ab_w_in[0].T, ab_w_out[0]], l0_ffn=[ffn_up_w[0].T, ffn_down_w[0]],
                  l1_conv=[conv_pw1_w[0].T, conv_pw2_w[0]], l1_ffn=[ffn_up_w[1].T, ffn_down_w[1]])
    stages = {k: [s.astype(BF16) for s in v] for k, v in stages.items()}
    names = dict(l0_mix=("w_in_t", "w_out"), l0_ffn=("up_t", "down"), l1_conv=("pw1_t", "pw2"), l1_ffn=("up_t", "down"))
    ready = {"l0_mix": [a.reshape(-1, d) for a in _all_gather_hbm(stages["l0_mix"], "gather_mixer_weights")]}
    behind = (ready, mod)
    arriving = {}
    for stage, group in (("l0_ffn", ("l0_ffn",)), ("l1_conv", ("l1_conv", "l1_ffn"))):
        srcs, _ = lax.optimization_barrier(([s for g in group for s in stages[g]], behind))
        arriving[stage], behind = _pushes_start(
            srcs, _landing_zones(srcs, f"gather_{stage}_zones"), False, f"gather_{stage}_start")
        mod = mod + behind[0:1, 0:1]

    def fetch(stage, after):
        if stage in arriving:
            full = [a.reshape(-1, d) for a in _pushes_wait(arriving[stage], after, f"gather_{stage}_wait")[1]]
            ready[stage] = full[:2]
            if stage == "l1_conv":
                ready["l1_ffn"] = full[2:]
        return dict(zip(names[stage], ready[stage]))

    causal = jnp.tril(jnp.ones((CHUNK, CHUNK), bool))
    wtril = jnp.where(causal[None], a_spatial_w[0], 0.0)
    mixer_w = dict(
        gain=a_vnorm_g[0].reshape(1, A_WIDTH), wtril=wtril.astype(BF16),
        wtril_t=wtril.transpose(0, 2, 1).astype(BF16),
        bias_exp=jnp.repeat(a_spatial_b[0].T, GROUP_DIM, axis=1),
        gq=jnp.tile(b_q_norm_g[0], HEADS)[None, :], gk=jnp.tile(b_k_norm_g[0], HEADS)[None, :],
        seg=jnp.kron(jnp.eye(HEADS, dtype=BF16), jnp.ones((HEAD_DIM, HEAD_DIM), BF16)))
    conv_w = dict(pw1_b=pw1_b, dw_w=dw_w, dw_b=dw_b, ln_g=ln_g, ln_b=ln_b, pw2_b=pw2_b)
    ffn_w = [dict(dw_w=fdw_w[l].reshape(FFN_CONV_WIDTH, 2, FFN_DIM).transpose(1, 0, 2), dw_b=ffn_dw_b[l].reshape(2, 1, FFN_DIM))
             for l in range(2)]

    leaving = {}

    def emit(stage, grads):
        blocks = [g.reshape(N_DEV, g.shape[0] // N_DEV, d) for g in grads]
        leaving[stage], token = _pushes_start(
            blocks, [lax.empty(b.shape, b.dtype) for b in blocks], True, f"reduce_{stage}_start")
        return token

    loss, dx, dmod, (g_mix, g_ffn0, g_conv, g_ffn1) = _local_step(
        x[0], loss_target[0], positions[0].astype(F32)[:, None], mod, norm_mix_g, norm_ffn_g, mixer_w, conv_w, ffn_w,
        fetch, emit)

    me_op = me.astype(jnp.int32).reshape(1)

    def reduced(stage, after):
        blocks, lands = _pushes_wait(leaving[stage], after, f"reduce_{stage}_wait")
        return [_sum_with_own(b, a, me_op, f"reduce_{stage}_sum{i}") for i, (b, a) in enumerate(zip(blocks, lands))]

    (r_up_t1, r_down1), (r_pw1_t, r_pw2), (r_up_t0, r_down0) = [reduced(s, dx) for s in ("l1_ffn", "l1_conv", "l0_ffn")]

    small_g = [
        dmod, jnp.concatenate([g_mix["norm_g"], g_conv["norm_g"]]), jnp.concatenate([g_ffn0["norm_g"], g_ffn1["norm_g"]]),
        g_mix["vnorm_g"], g_mix["spatial_w"], g_mix["spatial_b"], g_mix["q_norm_g"], g_mix["k_norm_g"],
        g_conv["pw1_b"], g_conv["dw_w"], g_conv["dw_b"], g_conv["ln_g"], g_conv["ln_b"], g_conv["pw2_b"],
        jnp.stack([g_ffn0["dw_w"], g_ffn1["dw_w"]]), jnp.concatenate([g_ffn0["dw_b"], g_ffn1["dw_b"]])]
    packed = _pack(small_g, rows=8)
    small_leaving, token = _pushes_start([packed], [lax.empty((N_DEV,) + packed.shape, F32)], False, "gather_small_grads_start")
    (r_pw1_t, r_pw2, r_up_t0, r_up_t1, r_down0, r_down1), _ = lax.optimization_barrier(
        ((r_pw1_t, r_pw2, r_up_t0, r_up_t1, r_down0, r_down1), token))

    grads = dict(conv_pw2_w=r_pw2[None], ffn_down_w=jnp.stack([r_down0, r_down1]))
    grads_t = dict(conv_pw1_w=r_pw1_t[None], ffn_up_w=jnp.stack([r_up_t0, r_up_t1]))
    flip = lambda a: jnp.swapaxes(a, 1, 2)
    delta, new_m, new_v = {}, {}, {}

    def update(name):
        if name in grads_t:
            grads[name] = flip(grads_t[name])
            res = _adamw(flip(weights[name]), grads_t[name], flip(mom1[name]), flip(mom2[name]), f"adamw_{name}")
            delta[name], new_m[name], new_v[name] = [flip(r) for r in res]
        else:
            delta[name], new_m[name], new_v[name] = _adamw(weights[name], grads[name], mom1[name], mom2[name], f"adamw_{name}")

    for name in ("conv_pw1_w", "conv_pw2_w", "ffn_up_w", "ffn_down_w"):
        update(name)
    r_in_t, r_out = reduced("l0_mix", new_v["ffn_down_w"])
    grads_t["ab_w_in"], grads["ab_w_out"] = r_in_t[None], r_out[None]
    update("ab_w_in")
    update("ab_w_out")

    (packed,), (landed,) = _pushes_wait(small_leaving, tuple(new_v.values()), "gather_small_grads_wait")
    total = _sum_in_device_order(packed, landed, me_op, "sum_small_grads")
    (s_dmod, s_mix_g, s_ffn_g, s_vnorm, s_sp_w, s_sp_b, s_gq, s_gk, s_pw1_b, s_dw_w, s_dw_b, s_ln_g, s_ln_b,
     s_pw2_b, s_fdw_w, s_fdw_b) = _unpack(total.reshape(-1), [a.shape for a in small_g])
    dmod_all = lax.dynamic_update_slice(
        landed.reshape(N_DEV, -1)[:, :dmod.size].reshape((N_DEV,) + dmod.shape), dmod[None], (me, 0, 0))
    n_ada = ada_w.shape[2]
    dmod16 = jnp.pad(_take_block(dmod_all, me, n_ada, 2), ((0, N_DEV), (0, 0), (0, 0)))
    grads.update(
        ada_w=jnp.stack([_ada_bwd(c16, dmod16[:, l], f"ada_bwd{l}") for l in range(2)]),
        ada_b=s_dmod, norm_mix_g=s_mix_g, norm_ffn_g=s_ffn_g,
        a_vnorm_g=s_vnorm[None], a_spatial_w=s_sp_w[None], a_spatial_b=s_sp_b[None], b_q_norm_g=s_gq[None],
        b_k_norm_g=s_gk[None],
        conv_pw1_b=_take_block(s_pw1_b, me, conv_pw1_b.shape[1], 1),
        conv_dw_w=_take_block(s_dw_w, me, conv_dw_w.shape[2], 1)[None],
        conv_dw_b=_take_block(s_dw_b, me, conv_dw_b.shape[1], 1), conv_ln_g=_take_block(s_ln_g, me, conv_ln_g.shape[1], 1),
        conv_ln_b=_take_block(s_ln_b, me, conv_ln_b.shape[1], 1),
        conv_pw2_b=_take_block(s_pw2_b, me, conv_pw2_b.shape[1], 1),
        ffn_dw_w=_take_block(s_fdw_w, me, ffn_dw_w.shape[2], 2), ffn_dw_b=s_fdw_b)
    update("ada_w")
    large = ("ada_w", "conv_pw1_w", "conv_pw2_w", "ffn_up_w", "ffn_down_w", "ab_w_in", "ab_w_out")
    small = [n for n in order if n not in large]
    res = _adamw_small(*[[src[n] for n in small] for src in (weights, grads, mom1, mom2)], "adamw_small")
    for dst, arrs in zip((delta, new_m, new_v), res):
        dst.update(zip(small, arrs))

    loss = lax.psum(loss[0, 0], ("x", "y", "c"))
    return (loss, dx[None], *[grads[n] for n in order], *[delta[n] for n in order],
            *[new_m[n] for n in order], *[new_v[n] for n in order])
```

```python
import functools
import math

import jax
import jax.numpy as jnp
from jax import lax
from jax.experimental import pallas as pl
from jax.experimental.pallas import tpu as pltpu

F32 = jnp.float32
BF16 = jnp.bfloat16
MESH = pl.DeviceIdType.MESH

D_MODEL = 1024
A_WIDTH = 512
A_GROUPS = 4
GROUP_DIM = 128
CHUNK = 128
B_WIDTH = 512
HEADS = 8
HEAD_DIM = 64
PATTERNS = ((128, 1), (512, 4), (2048, 16))
Q_BLOCK = 128
ROPE_THETA = 10000.0
AB_IN = 2560
CONV_WIDTH = 31
FFN_DIM = 2816
FFN_CONV_WIDTH = 3
EPS = 1e-6
NEG = -1e30
N_DEV = 8
ADAM_LR, ADAM_B1, ADAM_B2, ADAM_EPS, ADAM_WD, ADAM_STEP = 0.001, 0.9, 0.999, 1e-08, 0.01, 10

V7X_VMEM_LIMIT = 56 * 2**20
FFN_HALO = 16
CONV_HALO = 32

_NN = (((1,), (0,)), ((), ()))
_NT = (((1,), (1,)), ((), ()))
_TN = (((0,), (0,)), ((), ()))


def _tile(n, prefs=(512, 256, 128)):
    for t in prefs:
        if n % t == 0:
            return t
    return n


def _row_tile(n, cap=512):
    best = n
    for t in range(8, min(n, cap) + 1, 8):
        if n % t == 0:
            best = t
    return best if best <= cap else n


def _params(*sem):
    return pltpu.CompilerParams(dimension_semantics=sem, vmem_limit_bytes=V7X_VMEM_LIMIT)


def _dot(a, b, dims):
    return lax.dot_general(a, b, dims, preferred_element_type=F32)


def _sigmoid(x):
    return 1.0 / (1.0 + jnp.exp(-x))


def _gelu(x):
    return 0.5 * x * (1.0 + lax.erf(x * (2.0 ** -0.5)))


def _gelu_grad(x):
    return 0.5 * (1.0 + lax.erf(x * (2.0 ** -0.5))) + x * jnp.exp(-0.5 * x * x) * (1.0 / math.sqrt(2.0 * math.pi))


def _colsum(v):
    return jnp.sum(v, axis=0, keepdims=True)


MATMUL_VMEM_BUDGET = 40 * 2**20


def _matmul_tiles(m, n, k, out_bytes, with_resid):
    def options(dim):
        opts = [t for t in (1024, 512, 256, 128) if dim % t == 0]
        return opts + [dim] if dim <= 4096 and dim not in opts else opts

    best = None
    for tm in options(m):
        for tn in options(n):
            need = 4 * (tm * k + k * tn) + tm * tn * (4 + 2 * out_bytes) + (24 * tm * tn if with_resid else 0)
            if need <= MATMUL_VMEM_BUDGET and (best is None or tm * tn / (tm + tn) > best[0]):
                best = (tm * tn / (tm + tn), tm, tn)
    return best[1], best[2]


def _matmul_tn_acc(a, b, name, tk=1024):
    squeeze = a.ndim == 2
    a3 = a[None] if squeeze else a
    p_, t, m = a3.shape
    n = b.shape[1]
    nk = t // tk

    def body(a_ref, b_ref, o_ref, acc_ref):
        kt = pl.program_id(1)

        @pl.when(kt == 0)
        def _():
            acc_ref[...] = jnp.zeros_like(acc_ref)

        acc_ref[...] += _dot(a_ref[...], b_ref[...], _TN)

        @pl.when(kt == nk - 1)
        def _():
            o_ref[...] = acc_ref[...].astype(BF16)

    out = pl.pallas_call(
        body, name=name, grid=(p_, nk),
        in_specs=[pl.BlockSpec((None, tk, m), lambda p, kt: (p, kt, 0)), pl.BlockSpec((tk, n), lambda p, kt: (kt, 0))],
        out_specs=pl.BlockSpec((None, m, n), lambda p, kt: (p, 0, 0)), out_shape=jax.ShapeDtypeStruct((p_, m, n), BF16),
        scratch_shapes=[pltpu.VMEM((m, n), F32)], compiler_params=_params("parallel", "arbitrary"),
    )(a3, b)
    return out[0] if squeeze else out


def _matmul(a, b, mode, out_dtype, name, bias=None, resid=None):
    if mode == "nn":
        (m, k), (_, n) = a.shape, b.shape
    elif mode == "nt":
        (m, k), (n, _) = a.shape, b.shape
    else:
        (k, m), (_, n) = a.shape, b.shape
    tm, tn = _matmul_tiles(m, n, k, jnp.dtype(out_dtype).itemsize, resid is not None)
    dims = {"nn": _NN, "nt": _NT, "tn": _TN}[mode]
    a_spec = pl.BlockSpec((k, tm), lambda i, j: (0, i)) if mode == "tn" else pl.BlockSpec((tm, k), lambda i, j: (i, 0))
    b_spec = pl.BlockSpec((tn, k), lambda i, j: (j, 0)) if mode == "nt" else pl.BlockSpec((k, tn), lambda i, j: (0, j))
    in_specs, args = [a_spec, b_spec], [a, b]
    row_spec = pl.BlockSpec((1, tn), lambda i, j: (0, j))
    tile_spec = pl.BlockSpec((tm, tn), lambda i, j: (i, j))
    if bias is not None:
        in_specs.append(row_spec)
        args.append(bias)
    if resid is not None:
        in_specs += [tile_spec, row_spec]
        args += list(resid)
    out_shape = [jax.ShapeDtypeStruct((m, n), out_dtype)]
    out_specs = [tile_spec]
    if resid is not None:
        out_shape.append(jax.ShapeDtypeStruct((m, n), F32))
        out_specs.append(tile_spec)

    def body(*refs):
        a_ref, b_ref = refs[0], refs[1]
        pos = 2
        acc = _dot(a_ref[...], b_ref[...], dims)
        if bias is not None:
            acc = acc + refs[pos][...]
            pos += 1
        if resid is not None:
            x_ref, g_ref = refs[pos], refs[pos + 1]
            pos += 2
        refs[pos][...] = acc.astype(out_dtype)
        if resid is not None:
            refs[pos + 1][...] = x_ref[...] + g_ref[...] * acc

    outs = pl.pallas_call(
        body, name=name, grid=(m // tm, n // tn), in_specs=in_specs, out_specs=out_specs, out_shape=out_shape,
        compiler_params=_params("parallel", "parallel"),
    )(*args)
    return outs if resid is not None else outs[0]


def _modnorm(x, g, sc, sh, name):
    t, d = x.shape
    tm = _tile(t)
    row = pl.BlockSpec((1, d), lambda i: (0, 0))
    blk = pl.BlockSpec((tm, d), lambda i: (i, 0))

    def body(x_ref, g_ref, sc_ref, sh_ref, o_ref):
        x = x_ref[...]
        r = lax.rsqrt(jnp.mean(x * x, axis=-1, keepdims=True) + EPS)
        o_ref[...] = ((x * r) * g_ref[...] * (1.0 + sc_ref[...]) + sh_ref[...]).astype(BF16)

    return pl.pallas_call(
        body, name=name, grid=(t // tm,), in_specs=[blk, row, row, row], out_specs=blk,
        out_shape=jax.ShapeDtypeStruct((t, d), BF16), compiler_params=_params("parallel"),
    )(x, g, sc, sh)


def _gate_bwd_tile(dx, y_ref, gate_ref, dy_ref, dgate_ref, first):
    @pl.when(first)
    def _():
        dgate_ref[...] = jnp.zeros_like(dgate_ref)

    dy_ref[...] = (dx * gate_ref[...]).astype(BF16)
    dgate_ref[...] += _colsum(dx * y_ref[...].astype(F32))


def _modnorm_bwd(x, dh, g, sc, dres, below, name):
    t, d = x.shape
    tm = _tile(t)
    row = pl.BlockSpec((1, d), lambda i: (0, 0))
    blk = pl.BlockSpec((tm, d), lambda i: (i, 0))

    def body(x_ref, dh_ref, g_ref, sc_ref, dres_ref, *rest):
        dx_ref, dw_ref, dsh_ref = rest[-5:-2] if below else rest
        first = pl.program_id(0) == 0

        @pl.when(first)
        def _():
            dw_ref[...] = jnp.zeros_like(dw_ref)
            dsh_ref[...] = jnp.zeros_like(dsh_ref)

        x = x_ref[...]
        dh = dh_ref[...].astype(F32)
        r = lax.rsqrt(jnp.mean(x * x, axis=-1, keepdims=True) + EPS)
        xn = x * r
        dxn = dh * (g_ref[...] * (1.0 + sc_ref[...]))
        dx = dres_ref[...] + r * (dxn - xn * jnp.mean(dxn * xn, axis=-1, keepdims=True))
        dx_ref[...] = dx
        dw_ref[...] += _colsum(dh * xn)
        dsh_ref[...] += _colsum(dh)
        if below:
            _gate_bwd_tile(dx, rest[0], rest[1], rest[-2], rest[-1], first)

    row_out = jax.ShapeDtypeStruct((1, d), F32)
    return pl.pallas_call(
        body, name=name, grid=(t // tm,), in_specs=[blk, blk, row, row, blk] + ([blk, row] if below else []),
        out_specs=[blk, row, row] + ([blk, row] if below else []),
        out_shape=[jax.ShapeDtypeStruct((t, d), F32), row_out, row_out]
        + ([jax.ShapeDtypeStruct((t, d), BF16), row_out] if below else []),
        compiler_params=_params("arbitrary"),
    )(x, dh, g, sc, dres, *(below or ()))


def _loss_head(y, target, below, name):
    t, d = y.shape
    tm = _tile(t)
    blk = pl.BlockSpec((tm, d), lambda i: (i, 0))
    row = pl.BlockSpec((1, d), lambda i: (0, 0))
    one = pl.BlockSpec((1, 1), lambda i: (0, 0))
    steps = t // tm

    def body(y_ref, t_ref, yb_ref, gate_ref, dx_ref, loss_ref, dy_ref, dgate_ref, acc_ref):
        first = pl.program_id(0) == 0

        @pl.when(first)
        def _():
            acc_ref[...] = jnp.zeros_like(acc_ref)

        e = y_ref[...] - t_ref[...]
        dx = e * (1.0 / d)
        dx_ref[...] = dx
        acc_ref[...] += _colsum(e * e)
        _gate_bwd_tile(dx, yb_ref, gate_ref, dy_ref, dgate_ref, first)

        @pl.when(pl.program_id(0) == steps - 1)
        def _():
            loss_ref[...] = jnp.sum(acc_ref[...], axis=1, keepdims=True) * (0.5 / d)

    return pl.pallas_call(
        body, name=name, grid=(steps,), in_specs=[blk, blk, blk, row], out_specs=[blk, one, blk, row],
        out_shape=[jax.ShapeDtypeStruct((t, d), F32), jax.ShapeDtypeStruct((1, 1), F32),
                   jax.ShapeDtypeStruct((t, d), BF16), jax.ShapeDtypeStruct((1, d), F32)],
        scratch_shapes=[pltpu.VMEM((1, d), F32)], compiler_params=_params("arbitrary"),
    )(y, target, *below)


def _group_norm(vg, gain):
    mu = jnp.mean(vg, axis=-1, keepdims=True)
    xc = vg - mu
    rstd = lax.rsqrt(jnp.mean(xc * xc, axis=-1, keepdims=True) + EPS)
    xhat = xc * rstd
    return xhat, rstd, xhat * gain


def _gmlp_fwd(z, gain, wtril, bias_exp, name):
    t = z.shape[0]
    zu = pl.BlockSpec((CHUNK, A_WIDTH), lambda i: (i, 0))
    zv = pl.BlockSpec((CHUNK, A_WIDTH), lambda i: (i, 1))
    full2 = lambda shp: pl.BlockSpec(shp, lambda i: (0, 0))
    w_spec = pl.BlockSpec((A_GROUPS, CHUNK, CHUNK), lambda i: (0, 0, 0))

    def body(zu_ref, zv_ref, gain_ref, w_ref, b_ref, ya_ref):
        ua = _gelu(zu_ref[...].astype(F32))
        vg = _gelu(zv_ref[...].astype(F32))
        for g in range(A_GROUPS):
            sl = slice(g * GROUP_DIM, (g + 1) * GROUP_DIM)
            _, _, vn = _group_norm(vg[:, sl], gain_ref[:, sl])
            f = _dot(w_ref[g], vn.astype(BF16), _NN) + b_ref[:, sl]
            ya_ref[:, sl] = (ua[:, sl] * f).astype(BF16)

    return pl.pallas_call(
        body, name=name, grid=(t // CHUNK,),
        in_specs=[zu, zv, full2((1, A_WIDTH)), w_spec, full2((CHUNK, A_WIDTH))], out_specs=zu,
        out_shape=jax.ShapeDtypeStruct((t, A_WIDTH + B_WIDTH), BF16), compiler_params=_params("parallel"),
    )(z, z, gain, wtril, bias_exp)


def _gmlp_bwd(z, dcat, gain, wtril, wtril_t, bias_exp, name):
    t = z.shape[0]
    zu = pl.BlockSpec((CHUNK, A_WIDTH), lambda i: (i, 0))
    zv = pl.BlockSpec((CHUNK, A_WIDTH), lambda i: (i, 1))
    full2 = lambda shp: pl.BlockSpec(shp, lambda i: (0, 0))
    w_spec = pl.BlockSpec((A_GROUPS, CHUNK, CHUNK), lambda i: (0, 0, 0))
    dz_spec = pl.BlockSpec((CHUNK, 2 * A_WIDTH), lambda i: (i, 0))

    def body(zu_ref, zv_ref, dya_ref, gain_ref, w_ref, wt_ref, b_ref, dz_ref, dw_ref, dgain_ref, dbias_ref):
        @pl.when(pl.program_id(0) == 0)
        def _():
            dw_ref[...] = jnp.zeros_like(dw_ref)
            dgain_ref[...] = jnp.zeros_like(dgain_ref)
            dbias_ref[...] = jnp.zeros_like(dbias_ref)

        zu_v = zu_ref[...].astype(F32)
        zv_v = zv_ref[...].astype(F32)
        dya = dya_ref[...].astype(F32)
        ua = _gelu(zu_v)
        vg = _gelu(zv_v)
        row = lax.broadcasted_iota(jnp.int32, (CHUNK, CHUNK), 0)
        col = lax.broadcasted_iota(jnp.int32, (CHUNK, CHUNK), 1)
        for g in range(A_GROUPS):
            sl = slice(g * GROUP_DIM, (g + 1) * GROUP_DIM)
            gain_g = gain_ref[:, sl]
            xhat, rstd, vn = _group_norm(vg[:, sl], gain_g)
            vn16 = vn.astype(BF16)
            f = _dot(w_ref[g], vn16, _NN) + b_ref[:, sl]
            df = dya[:, sl] * ua[:, sl]
            df16 = df.astype(BF16)
            dz_ref[:, sl] = (dya[:, sl] * f * _gelu_grad(zu_v[:, sl])).astype(BF16)
            dw_ref[g] += jnp.where(row >= col, _dot(df16, vn16, _NT), 0.0)
            dvn = _dot(wt_ref[g], df16, _NN)
            dgain_ref[:, sl] += _colsum(dvn * xhat)
            dxh = dvn * gain_g
            dvg = rstd * (dxh - jnp.mean(dxh, axis=-1, keepdims=True) - xhat * jnp.mean(dxh * xhat, axis=-1, keepdims=True))
            dz_ref[:, A_WIDTH + g * GROUP_DIM:A_WIDTH + (g + 1) * GROUP_DIM] = (dvg * _gelu_grad(zv_v[:, sl])).astype(BF16)
            dbias_ref[:, sl] += df

    return pl.pallas_call(
        body, name=name, grid=(t // CHUNK,),
        in_specs=[zu, zv, zu, full2((1, A_WIDTH)), w_spec, w_spec, full2((CHUNK, A_WIDTH))],
        out_specs=[dz_spec, w_spec, full2((1, A_WIDTH)), full2((CHUNK, A_WIDTH))],
        out_shape=[jax.ShapeDtypeStruct((t, 2 * A_WIDTH), BF16), jax.ShapeDtypeStruct((A_GROUPS, CHUNK, CHUNK), F32),
                   jax.ShapeDtypeStruct((1, A_WIDTH), F32), jax.ShapeDtypeStruct((CHUNK, A_WIDTH), F32)],
        compiler_params=_params("arbitrary"),
    )(z, z, dcat, gain, wtril, wtril_t, bias_exp)


def _rope_tables(pos, inv_freq, sign, name):
    t = pos.shape[0]
    tm = _tile(t)
    row = pl.BlockSpec((1, B_WIDTH), lambda i: (0, 0))
    blk = pl.BlockSpec((tm, B_WIDTH), lambda i: (i, 0))

    def body(pos_ref, f_ref, s_ref, cos_ref, sin_ref):
        ang = pos_ref[...] * f_ref[:, 0:LANES]
        cos_ref[...] = jnp.tile(jnp.cos(ang), (1, B_WIDTH // LANES))
        sin_ref[...] = jnp.tile(jnp.sin(ang) * s_ref[:, 0:LANES], (1, B_WIDTH // LANES))

    return pl.pallas_call(
        body, name=name, grid=(t // tm,), in_specs=[pl.BlockSpec((tm, 1), lambda i: (i, 0)), row, row],
        out_specs=[blk, blk], out_shape=[jax.ShapeDtypeStruct((t, B_WIDTH), F32)] * 2,
        compiler_params=_params("parallel"),
    )(pos, inv_freq, sign)


def _head_sum(v, seg):
    hi = v.astype(BF16)
    lo = (v - hi.astype(F32)).astype(BF16)
    return _dot(hi, seg, _NN) + _dot(lo, seg, _NN)


def _swap_halves(v):
    lane = lax.broadcasted_iota(jnp.int32, v.shape, 1)
    return jnp.where((lane & (HEAD_DIM - 1)) < HEAD_DIM // 2,pltpu.roll(v, B_WIDTH - HEAD_DIM // 2, 1), pltpu.roll(v, HEAD_DIM // 2, 1))


DILATIONS = tuple(dil for _, dil in PATTERNS)
SUBSEQ_TM = 256
LANES = 128


def _subseq_shape(t, dil):
    return (t // dil, dil * B_WIDTH)


def _subseq_spec(tm, dil):
    return pl.BlockSpec((tm // dil, dil * B_WIDTH), lambda i: (i, 0))


def _to_subseq(x, scr_ref, dil):
    if dil == 1:
        return x
    tm, w = x.shape
    for c in range(w // LANES):
        scr_ref[c * tm:(c + 1) * tm, :] = x[:, c * LANES:(c + 1) * LANES]
    return jnp.concatenate([scr_ref[pl.ds(c * tm + r, tm // dil, stride=dil), :]
                            for r in range(dil) for c in range(w // LANES)], axis=1)


def _from_subseq(y, scr_ref, dil):
    if dil == 1:
        return y
    n, w = y.shape[0], y.shape[1] // dil
    tm = n * dil
    for r in range(dil):
        for c in range(w // LANES):
            scr_ref[pl.ds(c * tm + r, n, stride=dil), :] = y[:, r * w + c * LANES:r * w + (c + 1) * LANES]
    return jnp.concatenate([scr_ref[c * tm:(c + 1) * tm, :] for c in range(w // LANES)], axis=1)


def _subseq_scratch(tm):
    return pltpu.VMEM((B_WIDTH // LANES * tm, LANES), F32)


def _qk_prep(z, cos_t, sin_t, gq, gk, seg, name):
    t = z.shape[0]
    tm = _tile(t, (SUBSEQ_TM,))
    col = lambda c: pl.BlockSpec((tm, B_WIDTH), lambda i: (i, c))
    row = pl.BlockSpec((1, B_WIDTH), lambda i: (0, 0))
    blk = col(0)
    nd = len(DILATIONS)

    def body(q_ref, k_ref, v_ref, cos_ref, sin_ref, gq_ref, gk_ref, seg_ref, *rest):
        out_refs, scr_ref = rest[:-1], rest[-1]

        def norm_rot(x, g):
            r = lax.rsqrt(_head_sum(x * x, seg_ref[...]) * (1.0 / HEAD_DIM) + EPS)
            xn = x * r * g
            return xn * cos_ref[...] + _swap_halves(xn) * sin_ref[...]

        vals = (norm_rot(q_ref[...].astype(F32), gq_ref[...]), norm_rot(k_ref[...].astype(F32), gk_ref[...]),
                v_ref[...].astype(F32))
        for a, val in enumerate(vals):
            for b, dil in enumerate(DILATIONS):
                out_refs[a * nd + b][...] = _to_subseq(val, scr_ref, dil).astype(BF16)

    outs = pl.pallas_call(
        body, name=name, grid=(t // tm,),
        in_specs=[col(2), col(3), col(4), blk, blk, row, row, pl.BlockSpec((B_WIDTH, B_WIDTH), lambda i: (0, 0))],
        out_specs=[_subseq_spec(tm, dil) for _ in range(3) for dil in DILATIONS],
        out_shape=[jax.ShapeDtypeStruct(_subseq_shape(t, dil), BF16) for _ in range(3) for dil in DILATIONS],
        scratch_shapes=[_subseq_scratch(tm)], compiler_params=_params("parallel"),
    )(z, z, z, cos_t, sin_t, gq, gk, seg)
    return outs[:nd], outs[nd:2 * nd], outs[2 * nd:]


def _qk_prep_bwd(z, dqs, dks, dvs, cos_t, sin_t, gq, gk, seg, name):
    t = z.shape[0]
    tm = _tile(t, (SUBSEQ_TM,))
    col = lambda c: pl.BlockSpec((tm, B_WIDTH), lambda i: (i, c))
    row = pl.BlockSpec((1, B_WIDTH), lambda i: (0, 0))
    blk = col(0)
    nb = len(DILATIONS)
    subs = [_subseq_spec(tm, dil) for dil in DILATIONS]

    def body(*refs):
        q_ref, k_ref = refs[0], refs[1]
        dq_refs, dk_refs, dv_refs = refs[2:2 + nb], refs[2 + nb:2 + 2 * nb], refs[2 + 2 * nb:2 + 3 * nb]
        cos_ref, sin_ref, gq_ref, gk_ref, seg_ref, dz_ref, dgq_ref, dgk_ref, scr_ref = refs[2 + 3 * nb:]

        @pl.when(pl.program_id(0) == 0)
        def _():
            dgq_ref[...] = jnp.zeros_like(dgq_ref)
            dgk_ref[...] = jnp.zeros_like(dgk_ref)

        def total(d_refs):
            return sum(_from_subseq(r_[...], scr_ref, dil) for r_, dil in zip(d_refs, DILATIONS))

        def back(x, d_refs, g, dg_ref):
            dout = total(d_refs)
            dy = dout * cos_ref[...] + _swap_halves(dout * sin_ref[...])
            r = lax.rsqrt(_head_sum(x * x, seg_ref[...]) * (1.0 / HEAD_DIM) + EPS)
            xn = x * r
            dg_ref[...] += _colsum(dy * xn)
            dxn = dy * g
            return r * (dxn - xn * (_head_sum(dxn * xn, seg_ref[...]) * (1.0 / HEAD_DIM)))

        dz_ref[:, 0:B_WIDTH] = back(q_ref[...].astype(F32), dq_refs, gq_ref[...], dgq_ref).astype(BF16)
        dz_ref[:, B_WIDTH:2 * B_WIDTH] = back(k_ref[...].astype(F32), dk_refs, gk_ref[...], dgk_ref).astype(BF16)
        dz_ref[:, 2 * B_WIDTH:3 * B_WIDTH] = total(dv_refs).astype(BF16)

    return pl.pallas_call(
        body, name=name, grid=(t // tm,),
        in_specs=[col(2), col(3)] + subs * 3 + [blk, blk, row, row, pl.BlockSpec((B_WIDTH, B_WIDTH), lambda i: (0, 0))],
        out_specs=[pl.BlockSpec((tm, 3 * B_WIDTH), lambda i: (i, 0)), row, row],
        out_shape=[jax.ShapeDtypeStruct((t, 3 * B_WIDTH), BF16), jax.ShapeDtypeStruct((1, B_WIDTH), F32),
                   jax.ShapeDtypeStruct((1, B_WIDTH), F32)],
        scratch_shapes=[_subseq_scratch(tm)], compiler_params=_params("arbitrary"),
    )(z, z, *dqs, *dks, *dvs, cos_t, sin_t, gq, gk, seg)


def _subseq_views(x, col, name):
    t = x.shape[0]
    tm = _tile(t, (SUBSEQ_TM,))

    def body(x_ref, *rest):
        out_refs, scr_ref = rest[:-1], rest[-1]
        val = x_ref[...].astype(F32)
        for o_ref, dil in zip(out_refs, DILATIONS):
            o_ref[...] = _to_subseq(val, scr_ref, dil).astype(o_ref.dtype)

    return pl.pallas_call(
        body, name=name, grid=(t // tm,), in_specs=[pl.BlockSpec((tm, B_WIDTH), lambda i: (i, col))],
        out_specs=[_subseq_spec(tm, dil) for dil in DILATIONS],
        out_shape=[jax.ShapeDtypeStruct(_subseq_shape(t, dil), x.dtype) for dil in DILATIONS],
        scratch_shapes=[_subseq_scratch(tm)], compiler_params=_params("parallel"),
    )(x)


def _attn_fwd(q, k, v, dil, name):
    t = q.shape[0] * dil
    nb = t // dil // Q_BLOCK
    cur = pl.BlockSpec((Q_BLOCK, B_WIDTH), lambda r, i: (i, r))
    prev = pl.BlockSpec((Q_BLOCK, B_WIDTH), lambda r, i: (jnp.maximum(i - 1, 0), r))

    def body(q_ref, kp_ref, kc_ref, vp_ref, vc_ref, o_ref, lse_ref):
        i = pl.program_id(1)
        q = q_ref[...]
        kk = jnp.concatenate([kp_ref[...], kc_ref[...]], axis=0)
        vv = jnp.concatenate([vp_ref[...], vc_ref[...]], axis=0)
        a = lax.broadcasted_iota(jnp.int32, (Q_BLOCK, 2 * Q_BLOCK), 0)
        j = lax.broadcasted_iota(jnp.int32, (Q_BLOCK, 2 * Q_BLOCK), 1)
        dist = a + Q_BLOCK - j
        mask = (dist >= 0) & (dist <= Q_BLOCK) & ((j >= Q_BLOCK) | (i > 0))
        sls = [slice(h * HEAD_DIM, (h + 1) * HEAD_DIM) for h in range(HEADS)]
        scores = [_dot(q[:, sl], kk[:, sl], _NT) for sl in sls]
        ps, dens = [], []
        for sl, s in zip(sls, scores):
            s = jnp.where(mask, s * (HEAD_DIM ** -0.5), NEG)
            m = jnp.max(s, axis=-1, keepdims=True)
            p = jnp.exp(s - m)
            den = jnp.sum(p, axis=-1, keepdims=True)
            ps.append(p.astype(BF16))
            dens.append(den)
            lse_ref[:, sl] = jnp.broadcast_to(m + jnp.log(den), (Q_BLOCK, HEAD_DIM))
        for sl, p, den in zip(sls, ps, dens):
            o_ref[:, sl] = _dot(p, vv[:, sl], _NN) / den

    return pl.pallas_call(
        body, name=name, grid=(dil, nb), in_specs=[cur, prev, cur, prev, cur], out_specs=[cur, cur],
        out_shape=[jax.ShapeDtypeStruct(_subseq_shape(t, dil), F32)] * 2,
        compiler_params=_params("parallel", "parallel"),
    )(q, k, k, v, v)


def _attn_merge(outs, lses, cat, name):
    nb = len(DILATIONS)
    t = cat.shape[0]
    tm = _tile(t, (SUBSEQ_TM,))
    subs = [_subseq_spec(tm, dil) for dil in DILATIONS]

    def body(*refs):
        o_refs, l_refs = refs[:nb], refs[nb:2 * nb]
        yb_refs, lse_refs, cat_ref, scr_ref = refs[2 * nb + 1:3 * nb + 1], refs[3 * nb + 1:4 * nb + 1], refs[4 * nb + 1], refs[4 * nb + 2]
        ls = [_from_subseq(r[...], scr_ref, dil) for r, dil in zip(l_refs, DILATIONS)]
        m = functools.reduce(jnp.maximum, ls)
        tot = m + jnp.log(sum(jnp.exp(l - m) for l in ls))
        yb = sum(jnp.exp(l - tot) * _from_subseq(o[...], scr_ref, dil) for l, o, dil in zip(ls, o_refs, DILATIONS))
        cat_ref[...] = yb.astype(BF16)
        yb = yb.astype(BF16).astype(F32)
        for yb_ref, lse_ref, dil in zip(yb_refs, lse_refs, DILATIONS):
            yb_ref[...] = _to_subseq(yb, scr_ref, dil).astype(BF16)
            lse_ref[...] = _to_subseq(tot, scr_ref, dil)

    outs_ = pl.pallas_call(
        body, name=name, grid=(t // tm,), in_specs=subs * 2 + [pl.BlockSpec(memory_space=pl.ANY)],
        out_specs=subs * 2 + [pl.BlockSpec((tm, B_WIDTH), lambda i: (i, A_WIDTH // B_WIDTH))],
        out_shape=[jax.ShapeDtypeStruct(_subseq_shape(t, dil), BF16) for dil in DILATIONS]
        + [jax.ShapeDtypeStruct(_subseq_shape(t, dil), F32) for dil in DILATIONS] + [jax.ShapeDtypeStruct(cat.shape, BF16)],
        input_output_aliases={2 * nb: 2 * nb}, scratch_shapes=[_subseq_scratch(tm)], compiler_params=_params("parallel"),
    )(*outs, *lses, cat)
    return outs_[:nb], outs_[nb:2 * nb], outs_[2 * nb]


def _attn_bwd(q, k, v, do, o, lse, dil, name):
    t = q.shape[0] * dil
    nb = t // dil // Q_BLOCK
    cur = pl.BlockSpec((Q_BLOCK, B_WIDTH), lambda r, i: (i, r))
    prev = pl.BlockSpec((Q_BLOCK, B_WIDTH), lambda r, i: (jnp.maximum(i - 1, 0), r))
    scale = HEAD_DIM ** -0.5

    def body(q_ref, kp_ref, kc_ref, vp_ref, vc_ref, do_ref, o_ref, lse_ref, dq_ref, dk_ref, dv_ref,
             ck_ref, cv_ref, tk_ref, tv_ref):
        i = pl.program_id(1)

        @pl.when(i == 0)
        def _():
            ck_ref[...] = jnp.zeros_like(ck_ref)
            cv_ref[...] = jnp.zeros_like(cv_ref)

        q = q_ref[...]
        kk = jnp.concatenate([kp_ref[...], kc_ref[...]], axis=0)
        vv = jnp.concatenate([vp_ref[...], vc_ref[...]], axis=0)
        do = do_ref[...]
        dof = do.astype(F32)
        of = o_ref[...].astype(F32)
        a = lax.broadcasted_iota(jnp.int32, (Q_BLOCK, 2 * Q_BLOCK), 0)
        j = lax.broadcasted_iota(jnp.int32, (Q_BLOCK, 2 * Q_BLOCK), 1)
        dist = a + Q_BLOCK - j
        mask = (dist >= 0) & (dist <= Q_BLOCK) & ((j >= Q_BLOCK) | (i > 0))
        sls = [slice(h * HEAD_DIM, (h + 1) * HEAD_DIM) for h in range(HEADS)]
        scores = [_dot(q[:, sl], kk[:, sl], _NT) for sl in sls]
        dps = [_dot(do[:, sl], vv[:, sl], _NT) for sl in sls]
        ps, dss = [], []
        for sl, s, dp in zip(sls, scores, dps):
            p = jnp.exp(jnp.where(mask, s * scale, NEG) - lse_ref[:, sl.start:sl.start + 1])
            delta = jnp.sum(dof[:, sl] * of[:, sl], axis=-1, keepdims=True)
            dss.append((p * (dp - delta) * scale).astype(BF16))
            ps.append(p.astype(BF16))
        for sl, p, ds in zip(sls, ps, dss):
            dq_ref[:, sl] = _dot(ds, kk[:, sl], _NN)
            dv_t = _dot(do[:, sl], p, _TN)
            dk_t = _dot(q[:, sl], ds, _TN)
            tk_ref[sl, :] = ck_ref[sl, :] + dk_t[:, :Q_BLOCK]
            tv_ref[sl, :] = cv_ref[sl, :] + dv_t[:, :Q_BLOCK]
            ck_ref[sl, :] = dk_t[:, Q_BLOCK:]
            cv_ref[sl, :] = dv_t[:, Q_BLOCK:]

        @pl.when(i >= 1)
        def _():
            rows = pl.ds(pl.multiple_of((i - 1) * Q_BLOCK, Q_BLOCK), Q_BLOCK)
            dk_ref[rows, :] = tk_ref[...].T
            dv_ref[rows, :] = tv_ref[...].T

        @pl.when(i == nb - 1)
        def _():
            rows = pl.ds((nb - 1) * Q_BLOCK, Q_BLOCK)
            dk_ref[rows, :] = ck_ref[...].T
            dv_ref[rows, :] = cv_ref[...].T

    whole = pl.BlockSpec((t // dil, B_WIDTH), lambda r, i: (0, r))
    return pl.pallas_call(
        body, name=name, grid=(dil, nb), in_specs=[cur, prev, cur, prev, cur, cur, cur, cur],
        out_specs=[cur, whole, whole], out_shape=[jax.ShapeDtypeStruct(_subseq_shape(t, dil), F32)] * 3,
        scratch_shapes=[pltpu.VMEM((B_WIDTH, Q_BLOCK), F32)] * 4,
        compiler_params=_params("parallel", "arbitrary"),
    )(q, k, k, v, v, do, o, lse)


FFN_TN = 256
FFN_FWD_CHUNK = 256
FFN_BWD_CHUNK = 128


def _ffn_up(h, up_t, name):
    t, k = h.shape
    tm = _tile(t)

    def body(h_ref, w_ref, o_ref):
        o_ref[...] = _dot(h_ref[...], w_ref[...], _NT).astype(BF16)

    return pl.pallas_call(
        body, name=name, grid=(2, t // tm),
        in_specs=[pl.BlockSpec((tm, k), lambda p, i: (i, 0)), pl.BlockSpec((None, FFN_DIM, k), lambda p, i: (p, 0, 0))],
        out_specs=pl.BlockSpec((None, tm, FFN_DIM), lambda p, i: (p, i, 0)),
        out_shape=jax.ShapeDtypeStruct((2, t, FFN_DIM), BF16), compiler_params=_params("parallel", "parallel"),
    )(h, up_t.reshape(2, FFN_DIM, k))


def _ffn_up_dx(du, up_t, name):
    t = du.shape[1]
    k = up_t.shape[1]
    tm = _tile(t)

    def body(a_ref, b_ref, o_ref):
        o_ref[...] = _dot(a_ref[0], b_ref[0], _NN) + _dot(a_ref[1], b_ref[1], _NN)

    return pl.pallas_call(
        body, name=name, grid=(t // tm,),
        in_specs=[pl.BlockSpec((2, tm, FFN_DIM), lambda i: (0, i, 0)), pl.BlockSpec((2, FFN_DIM, k), lambda i: (0, 0, 0))],
        out_specs=pl.BlockSpec((tm, k), lambda i: (i, 0)), out_shape=jax.ShapeDtypeStruct((t, k), F32),
        compiler_params=_params("parallel"),
    )(du, up_t.reshape(2, FFN_DIM, k))


def _ffn_conv(win, w_ref, b_ref, p):
    x = win.astype(F32)
    x0, x1, x2 = x[FFN_HALO:], pltpu.roll(x, 1, 0)[FFN_HALO:], pltpu.roll(x, 2, 0)[FFN_HALO:]
    return b_ref[p] + w_ref[p, 2:3, :] * x0 + w_ref[p, 1:2, :] * x1 + w_ref[p, 0:1, :] * x2


def _zero_if(cond, v):
    return jnp.where(cond, 0, v).astype(v.dtype)


def _ffn_act(u, dw_w, dw_b, name):
    t = u.shape[1]
    tm = _tile(t)
    chunk = min(FFN_FWD_CHUNK, tm)
    hb = tm // FFN_HALO
    main = pl.BlockSpec((2, tm, FFN_TN), lambda i, j: (0, i, j))
    halo = pl.BlockSpec((2, FFN_HALO, FFN_TN), lambda i, j: (0, jnp.maximum(i * hb - 1, 0), j))
    wsp = pl.BlockSpec((2, FFN_CONV_WIDTH, FFN_TN), lambda i, j: (0, 0, j))
    bsp = pl.BlockSpec((2, 1, FFN_TN), lambda i, j: (0, 0, j))

    def body(u_ref, uh_ref, w_ref, b_ref, o_ref, z_ref):
        first = pl.program_id(0) == 0

        def emit(rows, wins):
            za, zb = _ffn_conv(wins[0], w_ref, b_ref, 0), _ffn_conv(wins[1], w_ref, b_ref, 1)
            o_ref[rows, :] = (za * _sigmoid(za) * zb).astype(BF16)
            z_ref[0, rows, :] = za.astype(BF16)
            z_ref[1, rows, :] = zb.astype(BF16)

        emit(pl.ds(0, chunk), [jnp.concatenate([_zero_if(first, uh_ref[p]), u_ref[p, 0:chunk, :]], axis=0) for p in range(2)])

        def step(c, carry):
            s = pl.multiple_of(c * chunk, chunk)
            emit(pl.ds(s, chunk), [u_ref[p, pl.ds(s - FFN_HALO, chunk + FFN_HALO), :] for p in range(2)])
            return carry

        lax.fori_loop(1, tm // chunk, step, 0)

    return pl.pallas_call(
        body, name=name, grid=(t // tm, FFN_DIM // FFN_TN), in_specs=[main, halo, wsp, bsp],
        out_specs=[pl.BlockSpec((tm, FFN_TN), lambda i, j: (i, j)), main],
        out_shape=[jax.ShapeDtypeStruct((t, FFN_DIM), BF16), jax.ShapeDtypeStruct((2, t, FFN_DIM), BF16)],
        compiler_params=_params("parallel", "parallel"),
    )(u, u, dw_w, dw_b)


def _fold8(v):
    return jnp.sum(v.reshape(v.shape[0] // 8, 8, v.shape[1]), axis=0)


def _ffn_act_bwd(u, z, dact, dw_w, name):
    t = u.shape[1]
    tm = _tile(t)
    chunk = min(FFN_BWD_CHUNK, tm // 2)
    halo = FFN_HALO
    hb = tm // halo
    nt = t // tm
    last_halo = t // halo - 1
    next_i = lambda i: jnp.minimum((i + 1) * hb, last_halo)
    main = pl.BlockSpec((2, tm, FFN_TN), lambda j, i: (0, i, j))
    nxt = pl.BlockSpec((2, halo, FFN_TN), lambda j, i: (0, next_i(i), j))
    wsp = pl.BlockSpec((2, FFN_CONV_WIDTH, FFN_TN), lambda j, i: (0, 0, j))
    bsp = pl.BlockSpec((2, 1, FFN_TN), lambda j, i: (0, 0, j))

    def body(u_ref, z_ref, zn_ref, da_ref, dan_ref, w_ref, du_ref, dw_ref, db_ref, acc_ref):
        i = pl.program_id(1)
        last = i == nt - 1
        acc_ref[...] = jnp.zeros_like(acc_ref)

        def emit(rows, zs, dact):
            n = chunk + halo
            za, zb, dact = zs[0].astype(F32), zs[1].astype(F32), dact.astype(F32)
            sg = _sigmoid(za)
            dzs = (dact * zb * (sg * (1.0 + za * (1.0 - sg))), dact * (za * sg))
            for p, dz in enumerate(dzs):
                ahead = (dz[:chunk], pltpu.roll(dz, n - 1, 0)[:chunk], pltpu.roll(dz, n - 2, 0)[:chunk])
                um = u_ref[p, rows, :].astype(F32)
                acc_ref[p, FFN_CONV_WIDTH] += _fold8(ahead[0])
                du = None
                for j, dzj in enumerate(ahead):
                    k = FFN_CONV_WIDTH - 1 - j
                    acc_ref[p, k] += _fold8(dzj * um)
                    term = w_ref[p, k:k + 1, :] * dzj
                    du = term if du is None else du + term
                du_ref[p, rows, :] = du.astype(BF16)

        def step(c, carry):
            s = pl.multiple_of(c * chunk, chunk)
            emit(pl.ds(s, chunk), [z_ref[p, pl.ds(s, chunk + halo), :] for p in range(2)], da_ref[pl.ds(s, chunk + halo), :])
            return carry

        lax.fori_loop(0, tm // chunk - 1, step, 0)
        s = tm - chunk
        emit(pl.ds(s, chunk),
             [jnp.concatenate([z_ref[p, s:tm, :], zn_ref[p]], axis=0) for p in range(2)],
             jnp.concatenate([da_ref[s:tm, :], _zero_if(last, dan_ref[...])], axis=0))

        @pl.when(i == 0)
        def _():
            dw_ref[...] = jnp.zeros_like(dw_ref)
            db_ref[...] = jnp.zeros_like(db_ref)

        for p in range(2):
            for k in range(FFN_CONV_WIDTH):
                dw_ref[p, k:k + 1, :] += _colsum(acc_ref[p, k])
            db_ref[p] += _colsum(acc_ref[p, FFN_CONV_WIDTH])

    return pl.pallas_call(
        body, name=name, grid=(FFN_DIM // FFN_TN, nt),
        in_specs=[main, main, nxt, pl.BlockSpec((tm, FFN_TN), lambda j, i: (i, j)),
                  pl.BlockSpec((halo, FFN_TN), lambda j, i: (next_i(i), j)), wsp],
        out_specs=[main, wsp, bsp],
        out_shape=[jax.ShapeDtypeStruct((2, t, FFN_DIM), BF16), jax.ShapeDtypeStruct((2, FFN_CONV_WIDTH, FFN_DIM), F32),
                   jax.ShapeDtypeStruct((2, 1, FFN_DIM), F32)],
        scratch_shapes=[pltpu.VMEM((2, FFN_CONV_WIDTH + 1, 8, FFN_TN), F32)],
        compiler_params=_params("parallel", "arbitrary"),
    )(u, z, z, dact, dact, dw_w)


CONV_TM = 256
CONV_ROWS = 128
CONV_LANES = 128


def _glu_window(pa_ref, pah_ref, pg_ref, pgh_ref, scr_ref, first):
    ah, gh = pah_ref[...].astype(F32), pgh_ref[...].astype(F32)
    scr_ref[0:CONV_HALO, :] = jnp.where(first, 0.0, ah * _sigmoid(gh))
    scr_ref[CONV_HALO:, :] = pa_ref[...].astype(F32) * _sigmoid(pg_ref[...].astype(F32))


def _tap_slabs(win, rows, ahead):
    n = win.shape[0]
    for s in range(8):
        ws = win if s == 0 else pltpu.roll(win, n - s if ahead else s, 0)
        for q in range(CONV_HALO // 8):
            o = 8 * q + s
            if o < CONV_WIDTH:
                start = 8 * q if ahead else CONV_HALO - 8 * q
                yield CONV_WIDTH - 1 - o, ws[start:start + rows]


def _conformer_specs(t):
    tm = _tile(t, (CONV_TM, 128))
    hb = tm // CONV_HALO
    d = D_MODEL
    main = lambda c: pl.BlockSpec((tm, d), lambda i: (i, c))
    halo = lambda c: pl.BlockSpec((CONV_HALO, d), lambda i: (jnp.maximum(i * hb - 1, 0), c))
    row = pl.BlockSpec((1, d), lambda i: (0, 0))
    wsp = pl.BlockSpec((CONV_WIDTH, d), lambda i: (0, 0))
    return tm, main, halo, row, wsp


def _conformer_mid(p, dw_w, dw_b, ln_g, ln_b, name):
    t = p.shape[0]
    tm, main, halo, row, wsp = _conformer_specs(t)
    d, lanes = D_MODEL, CONV_LANES

    def body(pa_ref, pah_ref, pg_ref, pgh_ref, w_ref, b_ref, g_ref, lb_ref, o_ref, dc_ref, scr_ref):
        _glu_window(pa_ref, pah_ref, pg_ref, pgh_ref, scr_ref, pl.program_id(0) == 0)
        for c in range(d // lanes):
            ls = slice(c * lanes, (c + 1) * lanes)
            acc = jnp.broadcast_to(b_ref[:, ls], (tm, lanes))
            for k, slab in _tap_slabs(scr_ref[:, ls], tm, False):
                acc = acc + w_ref[k:k + 1, ls] * slab
            dc_ref[:, ls] = acc

        def norm(r, carry):
            r0 = pl.multiple_of(r * 32, 32)
            dc = dc_ref[pl.ds(r0, 32), :]
            xc = dc - jnp.mean(dc, axis=-1, keepdims=True)
            ln = xc * lax.rsqrt(jnp.mean(xc * xc, axis=-1, keepdims=True) + EPS) * g_ref[...] + lb_ref[...]
            o_ref[pl.ds(r0, 32), :] = (ln * _sigmoid(ln)).astype(BF16)
            return carry

        lax.fori_loop(0, tm // 32, norm, 0)

    return pl.pallas_call(
        body, name=name, grid=(t // tm,), in_specs=[main(0), halo(0), main(1), halo(1), wsp, row, row, row],
        out_specs=[main(0), main(0)], out_shape=[jax.ShapeDtypeStruct((t, d), BF16), jax.ShapeDtypeStruct((t, d), F32)],
        scratch_shapes=[pltpu.VMEM((tm + CONV_HALO, d), F32)], compiler_params=_params("parallel"),
    )(p, p, p, p, dw_w, dw_b, ln_g, ln_b)


def _conformer_mid_bwd(p, dc, ds, ln_g, ln_b, name):
    t = p.shape[0]
    tm, main, halo, row, wsp = _conformer_specs(t)
    d, nt = D_MODEL, t // tm
    rows, lanes = CONV_ROWS, CONV_LANES

    def body(pa_ref, pah_ref, pg_ref, pgh_ref, dc_ref, ds_ref, g_ref, lb_ref,
             ddc_ref, dw_ref, db_ref, dg_ref, dlb_ref, scr_ref, wacc_ref, racc_ref):
        i = pl.program_id(0)

        @pl.when(i == 0)
        def _():
            wacc_ref[...] = jnp.zeros_like(wacc_ref)
            racc_ref[...] = jnp.zeros_like(racc_ref)

        _glu_window(pa_ref, pah_ref, pg_ref, pgh_ref, scr_ref, i == 0)

        def norm_bwd(r, carry):
            r0 = pl.multiple_of(r * 32, 32)
            dcv = dc_ref[pl.ds(r0, 32), :]
            xc = dcv - jnp.mean(dcv, axis=-1, keepdims=True)
            rstd = lax.rsqrt(jnp.mean(xc * xc, axis=-1, keepdims=True) + EPS)
            xhat = xc * rstd
            ln = xhat * g_ref[...] + lb_ref[...]
            sg = _sigmoid(ln)
            dln = ds_ref[pl.ds(r0, 32), :].astype(F32) * (sg * (1.0 + ln * (1.0 - sg)))
            dxh = dln * g_ref[...]
            ddc = rstd * (dxh - jnp.mean(dxh, axis=-1, keepdims=True) - xhat * jnp.mean(dxh * xhat, axis=-1, keepdims=True))
            ddc_ref[pl.ds(r0, 32), :] = ddc
            racc_ref[0] += _fold8(dln * xhat)
            racc_ref[1] += _fold8(dln)
            racc_ref[2] += _fold8(ddc)
            return carry

        lax.fori_loop(0, tm // 32, norm_bwd, 0)

        for c in range(d // lanes):
            ls = slice(c * lanes, (c + 1) * lanes)

            def taps(r, carry, ls=ls):
                r0 = pl.multiple_of(r * rows, rows)
                ddc = ddc_ref[pl.ds(r0, rows), ls]
                for k, slab in _tap_slabs(scr_ref[pl.ds(r0, rows + CONV_HALO), ls], rows, False):
                    wacc_ref[k, :, ls] += _fold8(ddc * slab)
                return carry

            lax.fori_loop(0, tm // rows, taps, 0)

        @pl.when(i == nt - 1)
        def _():
            for k in range(CONV_WIDTH):
                dw_ref[k:k + 1, :] = _colsum(wacc_ref[k])
            dg_ref[...] = _colsum(racc_ref[0])
            dlb_ref[...] = _colsum(racc_ref[1])
            db_ref[...] = _colsum(racc_ref[2])

    return pl.pallas_call(
        body, name=name, grid=(nt,), in_specs=[main(0), halo(0), main(1), halo(1), main(0), main(0), row, row],
        out_specs=[main(0), wsp, row, row, row],
        out_shape=[jax.ShapeDtypeStruct((t, d), F32), jax.ShapeDtypeStruct((CONV_WIDTH, d), F32)]
        + [jax.ShapeDtypeStruct((1, d), F32)] * 3,
        scratch_shapes=[pltpu.VMEM((tm + CONV_HALO, d), F32), pltpu.VMEM((CONV_WIDTH, 8, d), F32), pltpu.VMEM((3, 8, d), F32)],
        compiler_params=_params("arbitrary"),
    )(p, p, p, p, dc, ds, ln_g, ln_b)


def _conformer_glu_bwd(p, ddc, dw_w, name):
    t = p.shape[0]
    d = D_MODEL
    tm = _tile(t, (CONV_TM, 128))
    hb = tm // CONV_HALO
    nt = t // tm
    last_halo = t // CONV_HALO - 1
    rows, lanes = CONV_ROWS, CONV_LANES
    col = lambda c: pl.BlockSpec((tm, d), lambda i: (i, c))
    nxt = pl.BlockSpec((CONV_HALO, d), lambda i: (jnp.minimum((i + 1) * hb, last_halo), 0))

    def body(pa_ref, pg_ref, ddc_ref, ddcn_ref, w_ref, dp_ref, db_ref, scr_ref, acc_ref):
        i = pl.program_id(0)

        @pl.when(i == 0)
        def _():
            acc_ref[...] = jnp.zeros_like(acc_ref)

        scr_ref[0:tm, :] = ddc_ref[...]
        scr_ref[tm:, :] = _zero_if(i == nt - 1, ddcn_ref[...])
        for c in range(d // lanes):
            ls = slice(c * lanes, (c + 1) * lanes)
            gs = slice(d + c * lanes, d + (c + 1) * lanes)

            def taps(r, carry, ls=ls, gs=gs):
                r0 = pl.multiple_of(r * rows, rows)
                dglu = None
                for k, slab in _tap_slabs(scr_ref[pl.ds(r0, rows + CONV_HALO), ls], rows, True):
                    term = w_ref[k:k + 1, ls] * slab
                    dglu = term if dglu is None else dglu + term
                a = pa_ref[pl.ds(r0, rows), ls].astype(F32)
                sg = _sigmoid(pg_ref[pl.ds(r0, rows), ls].astype(F32))
                da = (dglu * sg).astype(BF16)
                dg = (dglu * a * sg * (1.0 - sg)).astype(BF16)
                dp_ref[pl.ds(r0, rows), ls] = da
                dp_ref[pl.ds(r0, rows), gs] = dg
                acc_ref[:, ls] += _fold8(da.astype(F32))
                acc_ref[:, gs] += _fold8(dg.astype(F32))
                return carry

            lax.fori_loop(0, tm // rows, taps, 0)

        @pl.when(i == nt - 1)
        def _():
            db_ref[...] = _colsum(acc_ref[...])

    return pl.pallas_call(
        body, name=name, grid=(nt,),
        in_specs=[col(0), col(1), col(0), nxt, pl.BlockSpec((CONV_WIDTH, d), lambda i: (0, 0))],
        out_specs=[pl.BlockSpec((tm, 2 * d), lambda i: (i, 0)), pl.BlockSpec((1, 2 * d), lambda i: (0, 0))],
        out_shape=[jax.ShapeDtypeStruct((t, 2 * d), BF16), jax.ShapeDtypeStruct((1, 2 * d), F32)],
        scratch_shapes=[pltpu.VMEM((tm + CONV_HALO, d), F32), pltpu.VMEM((8, 2 * d), F32)],
        compiler_params=_params("arbitrary"),
    )(p, p, ddc, ddc, dw_w)


def _colsum_call(a, name):
    t, n = a.shape
    tm = _tile(t)

    def body(a_ref, o_ref):
        @pl.when(pl.program_id(0) == 0)
        def _():
            o_ref[...] = jnp.zeros_like(o_ref)

        o_ref[...] += _colsum(a_ref[...].astype(F32))

    return pl.pallas_call(
        body, name=name, grid=(t // tm,), in_specs=[pl.BlockSpec((tm, n), lambda i: (i, 0))],
        out_specs=pl.BlockSpec((1, n), lambda i: (0, 0)), out_shape=jax.ShapeDtypeStruct((1, n), F32),
        compiler_params=_params("arbitrary"),
    )(a)


def _ada_fwd(c_all, w, name):
    rows, d = c_all.shape
    n = w.shape[1]
    tn = _tile(n, (256, 128))

    def body(c_ref, w_ref, o_ref):
        c = c_ref[...]
        o_ref[...] = _dot((c * _sigmoid(c)).astype(BF16), w_ref[...].astype(BF16), _NN)

    return pl.pallas_call(
        body, name=name, grid=(n // tn,),
        in_specs=[pl.BlockSpec((rows, d), lambda j: (0, 0)), pl.BlockSpec((d, tn), lambda j: (0, j))],
        out_specs=pl.BlockSpec((rows, tn), lambda j: (0, j)), out_shape=jax.ShapeDtypeStruct((rows, n), F32),
        compiler_params=_params("parallel"),
    )(c_all, w)


def _ada_bwd(c_all, dmod, name):
    rows, d = c_all.shape
    n = dmod.shape[1]
    tn = _tile(n, (256, 128))

    def body(c_ref, g_ref, o_ref):
        c = c_ref[...]
        o_ref[...] = _dot((c * _sigmoid(c)).astype(BF16), g_ref[...].astype(BF16), _TN)

    return pl.pallas_call(
        body, name=name, grid=(n // tn,),
        in_specs=[pl.BlockSpec((rows, d), lambda j: (0, 0)), pl.BlockSpec((rows, tn), lambda j: (0, j))],
        out_specs=pl.BlockSpec((d, tn), lambda j: (0, j)), out_shape=jax.ShapeDtypeStruct((d, n), F32),
        compiler_params=_params("parallel"),
    )(c_all, dmod)


def _sum_in_device_order(own, land, me, name):
    s, r, c = land.shape
    tr = _row_tile(r, 256)
    slot = lambda k: pl.BlockSpec((None, tr, c), lambda i, me_ref: (jnp.where(me_ref[0] == k, (k + 1) % s, k), i, 0))
    own_spec = pl.BlockSpec((tr, c), lambda i, me_ref: (i, 0))

    def body(me_ref, own_ref, *refs):
        o_ref = refs[-1]
        acc = None
        for k, ref in enumerate(refs[:-1]):
            term = jnp.where(me_ref[0] == k, own_ref[...], ref[...]).astype(F32)
            acc = term if acc is None else acc + term
        o_ref[...] = acc

    return pl.pallas_call(
        body, name=name, out_shape=jax.ShapeDtypeStruct((r, c), F32),
        grid_spec=pltpu.PrefetchScalarGridSpec(
            num_scalar_prefetch=1, grid=(r // tr,), in_specs=[own_spec] + [slot(k) for k in range(s)], out_specs=own_spec),
        compiler_params=_params("parallel"),
    )(me, own, *[land] * s)


def _sum_with_own(blocks, land, me, name):
    s, r, c = land.shape
    tr = _row_tile(r, 256)
    slot = lambda k: pl.BlockSpec((None, tr, c), lambda i, me_ref: ((me_ref[0] + k) % s, i, 0))

    def body(me_ref, own_ref, *refs):
        o_ref = refs[-1]
        acc = own_ref[...].astype(F32)
        for ref in refs[:-1]:
            acc = acc + ref[...].astype(F32)
        o_ref[...] = acc

    return pl.pallas_call(
        body, name=name, out_shape=jax.ShapeDtypeStruct((r, c), F32),
        grid_spec=pltpu.PrefetchScalarGridSpec(
            num_scalar_prefetch=1, grid=(r // tr,), in_specs=[slot(0)] + [slot(k) for k in range(1, s)],
            out_specs=pl.BlockSpec((tr, c), lambda i, me_ref: (i, 0))),
        compiler_params=_params("parallel"),
    )(me, blocks, *[land] * (s - 1))


def _adamw_update(w, g, m, v):
    nm = ADAM_B1 * m + (1.0 - ADAM_B1) * g
    nv = ADAM_B2 * v + (1.0 - ADAM_B2) * (g * g)
    m_hat = nm * (1.0 / (1.0 - ADAM_B1 ** ADAM_STEP))
    v_hat = nv * (1.0 / (1.0 - ADAM_B2 ** ADAM_STEP))
    return -ADAM_LR * (m_hat / (jnp.sqrt(v_hat) + ADAM_EPS) + ADAM_WD * w), nm, nv


def _adamw(w, g, m, v, behind, name):
    l, r, c = w.shape
    tr = _row_tile(r, 256)
    blk = pl.BlockSpec((None, tr, c), lambda k, i: (k, i, 0))
    order = [] if behind is None else [behind]

    def body(w_ref, g_ref, m_ref, v_ref, *rest):
        d_ref, nm_ref, nv_ref = rest[-3:]
        d_ref[...], nm_ref[...], nv_ref[...] = _adamw_update(w_ref[...], g_ref[...], m_ref[...], v_ref[...])

    return pl.pallas_call(
        body, name=name, grid=(l, r // tr), in_specs=[blk] * 4 + [pl.BlockSpec(memory_space=pl.ANY)] * len(order),
        out_specs=[blk] * 3, out_shape=[jax.ShapeDtypeStruct(w.shape, F32)] * 3,
        compiler_params=_params("parallel", "parallel"),
    )(w, g, m, v, *order)


def _adamw_small(ws, gs, ms, vs, name):
    n = len(ws)
    two_d = lambda a: a.reshape(-1, a.shape[-1])

    def body(*refs):
        ins, outs = refs[:4 * n], refs[4 * n:]
        for a in range(n):
            outs[a][...], outs[n + a][...], outs[2 * n + a][...] = _adamw_update(*[ins[k * n + a][...] for k in range(4)])

    res = pl.pallas_call(
        body, name=name, out_shape=[jax.ShapeDtypeStruct(two_d(w).shape, F32) for w in ws] * 3,
    )(*[two_d(a) for a in (*ws, *gs, *ms, *vs)])
    return [[res[k * n + a].reshape(ws[a].shape) for a in range(n)] for k in range(3)]


def _mesh_pos():
    return lax.axis_index("x"), lax.axis_index("y"), lax.axis_index("c")


def _all_gather_vmem(x_shard, name):
    m_per, n = x_shard.shape

    def body(x_ref, out_ref, send_sems, recv_sems, local_sem):
        x, y, c = _mesh_pos()
        me, sibling = (x, y, c), (x, y, 1 - c)
        chips = [(1 - x, y), (x, 1 - y), (1 - x, 1 - y)]

        def rows(px, py, pc):
            return out_ref.at[pl.ds((4 * px + 2 * py + pc) * m_per, m_per), :]

        def copy(k, block, to, src=None):
            return pltpu.make_async_remote_copy(
                src_ref=rows(*block) if src is None else src, dst_ref=rows(*block),
                send_sem=send_sems.at[k], recv_sem=recv_sems.at[k], device_id=to, device_id_type=MESH)

        mine = pltpu.make_async_copy(x_ref, rows(*me), local_sem)
        mine.start()
        first = [copy(0, me, sibling, src=x_ref)]
        first += [copy(1 + j, me, (*chip, c), src=x_ref) for j, chip in enumerate(chips)]
        for cp in first:
            cp.start()
        passed = [copy(4 + j, (*chip, c), sibling) for j, chip in enumerate(chips)]
        for j, chip in enumerate(chips):
            copy(1 + j, (*chip, c), me).wait_recv()
            passed[j].start()
        copy(0, sibling, me).wait_recv()
        for j, chip in enumerate(chips):
            copy(4 + j, (*chip, 1 - c), me).wait_recv()
        for cp in first + passed:
            cp.wait_send()
        mine.wait()

    return pl.pallas_call(
        body, name=name, out_shape=jax.ShapeDtypeStruct((N_DEV * m_per, n), x_shard.dtype),
        in_specs=[pl.BlockSpec(memory_space=pltpu.VMEM)], out_specs=pl.BlockSpec(memory_space=pltpu.VMEM),
        scratch_shapes=[pltpu.SemaphoreType.DMA((7,)), pltpu.SemaphoreType.DMA((7,)), pltpu.SemaphoreType.DMA],
    )(x_shard)


def _all_gather_hbm(shards, name):
    n = len(shards)
    out_shape = [jax.ShapeDtypeStruct((N_DEV,) + s.shape, s.dtype) for s in shards]

    def body(*refs):
        x_refs, out_refs = refs[:n], refs[n:2 * n]
        send_sems, recv_sems, local_sems = refs[2 * n:]
        x, y, c = _mesh_pos()
        me, sibling = (x, y, c), (x, y, 1 - c)
        chips = [(1 - x, y), (x, 1 - y), (1 - x, 1 - y)]

        def blk(a, p):
            return out_refs[a].at[4 * p[0] + 2 * p[1] + p[2]]

        def copy(a, k, block, to, src=None):
            return pltpu.make_async_remote_copy(
                src_ref=blk(a, block) if src is None else src, dst_ref=blk(a, block),
                send_sem=send_sems.at[7 * a + k], recv_sem=recv_sems.at[7 * a + k], device_id=to, device_id_type=MESH)

        mine = [pltpu.make_async_copy(x_refs[a], blk(a, me), local_sems.at[a]) for a in range(n)]
        for cp in mine:
            cp.start()
        first = []
        for a in range(n):
            first.append(copy(a, 0, me, sibling, src=x_refs[a]))
            first += [copy(a, 1 + j, me, (*chip, c), src=x_refs[a]) for j, chip in enumerate(chips)]
        for cp in first:
            cp.start()
        passed = []
        for j, chip in enumerate(chips):
            for a in range(n):
                copy(a, 1 + j, (*chip, c), me).wait_recv()
                fwd = copy(a, 4 + j, (*chip, c), sibling)
                fwd.start()
                passed.append(fwd)
        for a in range(n):
            copy(a, 0, sibling, me).wait_recv()
            for j, chip in enumerate(chips):
                copy(a, 4 + j, (*chip, 1 - c), me).wait_recv()
        for cp in first + passed:
            cp.wait_send()
        for cp in mine:
            cp.wait()

    return pl.pallas_call(
        body, name=name, out_shape=out_shape, in_specs=[pl.BlockSpec(memory_space=pltpu.VMEM)] * n,
        out_specs=[pl.BlockSpec(memory_space=pl.ANY)] * n,
        scratch_shapes=[pltpu.SemaphoreType.DMA((7 * n,)), pltpu.SemaphoreType.DMA((7 * n,)), pltpu.SemaphoreType.DMA((n,))],
    )(*shards)


def _peers(x, y, c):
    flip = lambda v, f: 1 - v if f else v
    return [(flip(x, m & 4), flip(y, m & 2), flip(c, m & 1)) for m in range(1, N_DEV)]


def _dev_index(p):
    return 4 * p[0] + 2 * p[1] + p[2]


def _push_copies(src_refs, land_refs, send_sems, recv_sems, scatter, receive):
    x, y, c = _mesh_pos()
    me = _dev_index((x, y, c))
    copies = []
    for a, (src, land) in enumerate(zip(src_refs, land_refs)):
        for k, p in enumerate(_peers(x, y, c)):
            copies.append(pltpu.make_async_remote_copy(
                src_ref=src.at[_dev_index(p)] if scatter else src, dst_ref=land.at[_dev_index(p) if receive else me],
                send_sem=send_sems.at[7 * a + k], recv_sem=recv_sems.at[7 * a + k], device_id=p, device_id_type=MESH))
    return copies


_HBM = pl.BlockSpec(memory_space=pltpu.HBM)
_SEM = pl.BlockSpec(memory_space=pltpu.SEMAPHORE)
_EFFECT = pltpu.SideEffectType.DATAFLOW_SIDE_EFFECTING


def _pushes_start(srcs, lands, scatter, name):
    n = len(srcs)

    def body(*refs):
        src_refs, land_refs = refs[:n], refs[n:2 * n]
        send_sems, recv_sems = refs[2 * n], refs[2 * n + 1]
        token = refs[-1]
        for cp in _push_copies(src_refs, land_refs, send_sems, recv_sems, scatter, receive=False):
            cp.start()
        token[...] = jnp.zeros_like(token)

    hbm = lambda a: pltpu.HBM(a.shape, a.dtype)
    sems = pltpu.SemaphoreType.DMA((7 * n,))
    outs = pl.pallas_call(
        body, name=name,
        out_shape=(sems, sems, *[hbm(a) for a in srcs], *[hbm(a) for a in lands], jax.ShapeDtypeStruct((8, 128), F32)),
        in_specs=[_HBM] * (2 * n), out_specs=(_SEM, _SEM, *[_HBM] * (2 * n), pl.BlockSpec(memory_space=pltpu.VMEM)),
        input_output_aliases={i: 2 + i for i in range(2 * n)},
        compiler_params=pltpu.CompilerParams(has_side_effects=_EFFECT),
    )(*[pltpu.with_memory_space_constraint(a, pltpu.HBM) for a in (*srcs, *lands)])
    return (outs[0], outs[1], outs[2:2 + n], outs[2 + n:2 + 2 * n], scatter), outs[-1]


def _pushes_wait(handle, after, name):
    send_sems, recv_sems, srcs, lands, scatter = handle
    n = len(srcs)
    after = after if isinstance(after, (tuple, list)) else (after,)

    def body(*refs):
        src_refs, land_refs = refs[:n], refs[n:2 * n]
        for cp in _push_copies(src_refs, land_refs, refs[2 * n], refs[2 * n + 1], scatter, receive=True):
            cp.wait_send()
            cp.wait_recv()

    hbm = lambda a: pltpu.HBM(a.shape, a.dtype)
    outs = pl.pallas_call(
        body, name=name, out_shape=tuple(hbm(a) for a in (*srcs, *lands)),
        in_specs=[_HBM] * (2 * n) + [_SEM, _SEM] + [pl.BlockSpec(memory_space=pl.ANY)] * len(after),
        out_specs=tuple([_HBM] * (2 * n)), input_output_aliases={i: i for i in range(2 * n)},
        compiler_params=pltpu.CompilerParams(has_side_effects=_EFFECT),
    )(*srcs, *lands, send_sems, recv_sems, *after)
    return outs[:n], outs[n:]


def _landing_zones(srcs, behind, name):
    n, nb = len(srcs), len(behind)

    def body(*refs):
        src_refs, land_refs, bufs, sems = refs[:n], refs[n + nb:2 * n + nb], refs[2 * n + nb:3 * n + nb], refs[3 * n + nb]
        me = _dev_index(_mesh_pos())
        load = [pltpu.make_async_copy(src, buf, sems.at[a]) for a, (src, buf) in enumerate(zip(src_refs, bufs))]
        store = [pltpu.make_async_copy(buf, land.at[me], sems.at[a]) for a, (buf, land) in enumerate(zip(bufs, land_refs))]
        for cp in load:
            cp.start()
        for ld, st in zip(load, store):
            ld.wait()
            st.start()
        for cp in store:
            cp.wait()

    any_spec = pl.BlockSpec(memory_space=pl.ANY)
    return pl.pallas_call(
        body, name=name, out_shape=[jax.ShapeDtypeStruct((N_DEV,) + s.shape, s.dtype) for s in srcs],
        in_specs=[any_spec] * (n + nb), out_specs=[any_spec] * n,
        scratch_shapes=[pltpu.VMEM(s.shape, s.dtype) for s in srcs] + [pltpu.SemaphoreType.DMA((n,))],
        compiler_params=pltpu.CompilerParams(vmem_limit_bytes=V7X_VMEM_LIMIT),
    )(*srcs, *behind)


def _ffn_forward(x, mod, norm_g, w, tag):
    sh, sc, gate = mod
    h = _modnorm(x, norm_g, sc, sh, f"{tag}_norm")
    u = _ffn_up(h, w["up_t"], f"{tag}_up")
    act, z = _ffn_act(u, w["dw_w"], w["dw_b"], f"{tag}_act")
    y, x_new = _matmul(act, w["down"], "nn", BF16, f"{tag}_down", resid=(x, gate))
    return x_new, (x, h, u, z, act, y)


def _behind(row, token):
    return row if token is None else row + token[0:1, 0:1]


def _ffn_backward(dx_new, dy, d_gate, saved, mod, norm_g, w, tag, emit, below):
    x, h, u, z, act, _ = saved
    _, sc, _ = mod
    d_down = _matmul_tn_acc(act, dy, f"{tag}_down_dw")
    dact = _matmul(dy, w["down"], "nt", BF16, f"{tag}_down_dx")
    du, d_dw_w, d_dw_b = _ffn_act_bwd(u, z, dact, w["dw_w"], f"{tag}_act_bwd")
    d_up_t = _matmul_tn_acc(du, h, f"{tag}_up_dw").reshape(2 * FFN_DIM, -1)
    token = emit([d_up_t, d_down])
    dh = _ffn_up_dx(du, w["up_t"], f"{tag}_up_dx")
    dx, d_w, d_sh, *dy_below = _modnorm_bwd(x, dh, norm_g, _behind(sc, token), dx_new, below, f"{tag}_norm_bwd")
    return (dx, *dy_below), dict(dw_w=d_dw_w.transpose(1, 0, 2).reshape(FFN_CONV_WIDTH, 2 * FFN_DIM),
                    dw_b=d_dw_b.reshape(1, 2 * FFN_DIM), norm_g=d_w * (1.0 + sc), sh=d_sh, sc=d_w * norm_g, gate=d_gate)


def _mixer_forward(x, mod, norm_g, w, rope, tag):
    sh, sc, gate = mod
    h = _modnorm(x, norm_g, sc, sh, f"{tag}_norm")
    z = _matmul(h, w["w_in_t"], "nt", BF16, f"{tag}_in")
    ya = _gmlp_fwd(z, w["gain"], w["wtril"], w["bias_exp"], f"{tag}_gmlp")
    q, k, v = _qk_prep(z, rope[0], rope[1], w["gq"], w["gk"], w["seg"], f"{tag}_qk")
    outs, lses = zip(*[_attn_fwd(q[b], k[b], v[b], dil, f"{tag}_attn_d{dil}") for b, dil in enumerate(DILATIONS)])
    yb, lse, cat = _attn_merge(outs, lses, ya, f"{tag}_merge")
    y, x_new = _matmul(cat, w["w_out"], "nn", BF16, f"{tag}_out", resid=(x, gate))
    return x_new, (x, h, z, q, k, v, yb, lse, cat, y)


def _mixer_backward(dx_new, dy, d_gate, saved, mod, norm_g, w, rope, tag, emit, below):
    x, h, z, q, k, v, yb, lse, cat, _ = saved
    _, sc, _ = mod
    d_w_out = _matmul_tn_acc(cat, dy, f"{tag}_out_dw")
    dcat = _matmul(dy, w["w_out"], "nt", BF16, f"{tag}_out_dx")
    dz_a, d_sp_w, d_gain, d_bias_exp = _gmlp_bwd(z, dcat, w["gain"], w["wtril"], w["wtril_t"], w["bias_exp"], f"{tag}_gmlp_bwd")
    dyb = _subseq_views(dcat, A_WIDTH // B_WIDTH, f"{tag}_dyb_views")
    dqs, dks, dvs = zip(*[_attn_bwd(q[b], k[b], v[b], dyb[b], yb[b], lse[b], dil, f"{tag}_attn_bwd_d{dil}")
                          for b, dil in enumerate(DILATIONS)])
    dz_qkv, d_gq, d_gk = _qk_prep_bwd(z, dqs, dks, dvs, rope[0], rope[1], w["gq"], w["gk"], w["seg"], f"{tag}_qk_bwd")
    dz = jnp.concatenate([dz_a, dz_qkv], axis=1)
    d_w_in_t = _matmul_tn_acc(dz, h, f"{tag}_in_dw")
    token = emit([d_w_in_t, d_w_out])
    dh = _matmul(dz, w["w_in_t"], "nn", F32, f"{tag}_in_dx")
    dx, d_w, d_sh, *dy_below = _modnorm_bwd(x, dh, norm_g, _behind(sc, token), dx_new, below, f"{tag}_norm_bwd")
    return (dx, *dy_below), dict(
        vnorm_g=d_gain.reshape(A_GROUPS, GROUP_DIM), spatial_w=d_sp_w,
        spatial_b=d_bias_exp.reshape(CHUNK, A_GROUPS, GROUP_DIM).sum(-1).T,
        q_norm_g=d_gq.reshape(HEADS, HEAD_DIM).sum(0), k_norm_g=d_gk.reshape(HEADS, HEAD_DIM).sum(0),
        norm_g=d_w * (1.0 + sc), sh=d_sh, sc=d_w * norm_g, gate=d_gate)


def _conformer_forward(x, mod, norm_g, w, tag):
    sh, sc, gate = mod
    h = _modnorm(x, norm_g, sc, sh, f"{tag}_norm")
    p = _matmul(h, w["pw1_t"], "nt", BF16, f"{tag}_pw1", bias=w["pw1_b"])
    s, dc = _conformer_mid(p, w["dw_w"], w["dw_b"], w["ln_g"], w["ln_b"], f"{tag}_mid")
    y, x_new = _matmul(s, w["pw2"], "nn", BF16, f"{tag}_pw2", bias=w["pw2_b"], resid=(x, gate))
    return x_new, (x, h, p, dc, s, y)


def _conformer_backward(dx_new, dy, d_gate, saved, mod, norm_g, w, tag, emit, below):
    x, h, p, dc, s, _ = saved
    _, sc, _ = mod
    d_pw2 = _matmul_tn_acc(s, dy, f"{tag}_pw2_dw")
    d_pw2_b = _colsum_call(dy, f"{tag}_pw2_db")
    ds = _matmul(dy, w["pw2"], "nt", BF16, f"{tag}_pw2_dx")
    ddc, d_dw_w, d_dw_b, d_ln_g, d_ln_b = _conformer_mid_bwd(p, dc, ds, w["ln_g"], w["ln_b"], f"{tag}_mid_bwd")
    dp, d_pw1_b = _conformer_glu_bwd(p, ddc, w["dw_w"], f"{tag}_glu_bwd")
    d_pw1_t = _matmul_tn_acc(dp, h, f"{tag}_pw1_dw")
    token = emit([d_pw1_t, d_pw2])
    dh = _matmul(dp, w["pw1_t"], "nn", F32, f"{tag}_pw1_dx")
    dx, d_w, d_sh, *dy_below = _modnorm_bwd(x, dh, norm_g, _behind(sc, token), dx_new, below, f"{tag}_norm_bwd")
    return (dx, *dy_below), dict(pw1_b=d_pw1_b, dw_w=d_dw_w, dw_b=d_dw_b, ln_g=d_ln_g, ln_b=d_ln_b, pw2_b=d_pw2_b, norm_g=d_w * (1.0 + sc), sh=d_sh, sc=d_w * norm_g, gate=d_gate)


def _local_step(x, target, pos, mod, norm_mix_g, norm_ffn_g, mixer_w, conv_w, ffn_w, fetch, emit):
    d = D_MODEL
    inv_freq = 1.0 / (ROPE_THETA ** (jnp.arange(0, HEAD_DIM, 2, dtype=F32) / HEAD_DIM))
    inv_freq = jnp.tile(inv_freq, 2 * HEADS)[None, :]
    sign = jnp.tile(jnp.concatenate([-jnp.ones(HEAD_DIM // 2, F32), jnp.ones(HEAD_DIM // 2, F32)]), HEADS)[None, :]
    rope = _rope_tables(pos, inv_freq, sign, "rope_tables")
    mods = [[mod[l:l + 1, i * d:(i + 1) * d] for i in range(6)] for l in range(2)]
    mix = [(m[0], m[1], m[2]) for m in mods]
    ffn = [(m[3], m[4], m[5]) for m in mods]
    gm = [norm_mix_g[l:l + 1] for l in range(2)]
    gf = [norm_ffn_g[l:l + 1] for l in range(2)]

    mixer_w = {**mixer_w, **fetch("l0_mix", x)}
    x1, s_mix = _mixer_forward(x, mix[0], gm[0], mixer_w, rope, "l0_mix")
    ffn_w0 = {**ffn_w[0], **fetch("l0_ffn", x1)}
    x2, s_ffn0 = _ffn_forward(x1, ffn[0], gf[0], ffn_w0, "l0_ffn")
    conv_w = {**conv_w, **fetch("l1_conv", x2)}
    x3, s_conv = _conformer_forward(x2, mix[1], gm[1], conv_w, "l1_conv")
    ffn_w1 = {**ffn_w[1], **fetch("l1_ffn", x3)}
    x4, s_ffn1 = _ffn_forward(x3, ffn[1], gf[1], ffn_w1, "l1_ffn")
    below = lambda saved, m: (saved[-1], m[2])
    dx, loss, dy, dg = _loss_head(x4, target, below(s_ffn1, ffn[1]), "loss_head")
    (dx, dy, dg), g_ffn1 = _ffn_backward(dx, dy, dg, s_ffn1, ffn[1], gf[1], ffn_w1, "l1_ffn",
                                         functools.partial(emit, "l1_ffn"), below(s_conv, mix[1]))
    (dx, dy, dg), g_conv = _conformer_backward(dx, dy, dg, s_conv, mix[1], gm[1], conv_w, "l1_conv",
                                               functools.partial(emit, "l1_conv"), below(s_ffn0, ffn[0]))
    (dx, dy, dg), g_ffn0 = _ffn_backward(dx, dy, dg, s_ffn0, ffn[0], gf[0], ffn_w0, "l0_ffn",
                                         functools.partial(emit, "l0_ffn"), below(s_mix, mix[0]))
    (dx,), g_mix = _mixer_backward(dx, dy, dg, s_mix, mix[0], gm[0], mixer_w, rope, "l0_mix",
                                   functools.partial(emit, "l0_mix"), None)
    blocks = [g_mix, g_ffn0, g_conv, g_ffn1]
    dmod = jnp.stack([jnp.concatenate([a["sh"], a["sc"], a["gate"], b["sh"], b["sc"], b["gate"]], axis=1)[0]
                      for a, b in ((g_mix, g_ffn0), (g_conv, g_ffn1))])
    return loss, dx, dmod, blocks


def _pack(arrs, rows=8):
    flat = jnp.concatenate([a.reshape(-1).astype(F32) for a in arrs])
    n = flat.shape[0]
    cols = -(-n // (rows * 128)) * 128
    return jnp.pad(flat, (0, rows * cols - n)).reshape(rows, cols)


def _unpack(flat, shapes):
    out, off = [], 0
    for shp in shapes:
        n = math.prod(shp)
        out.append(flat[..., off:off + n].reshape(flat.shape[:-1] + tuple(shp)))
        off += n
    return out


def _take_block(a, idx, size, axis):
    return lax.dynamic_slice_in_dim(a, idx * size, size, axis)


def kernel(x, c, positions, ada_w, ada_b, norm_mix_g, norm_ffn_g, ab_w_in, a_vnorm_g, a_spatial_w, a_spatial_b, b_q_norm_g, b_k_norm_g, ab_w_out, conv_pw1_w, conv_pw1_b, conv_dw_w, conv_dw_b, conv_ln_g, conv_ln_b, conv_pw2_w, conv_pw2_b, ffn_up_w, ffn_dw_w, ffn_dw_b, ffn_down_w, loss_target, m_ada_w, m_ada_b, m_norm_mix_g, m_norm_ffn_g, m_ab_w_in, m_a_vnorm_g, m_a_spatial_w, m_a_spatial_b, m_b_q_norm_g, m_b_k_norm_g, m_ab_w_out, m_conv_pw1_w, m_conv_pw1_b, m_conv_dw_w, m_conv_dw_b, m_conv_ln_g, m_conv_ln_b, m_conv_pw2_w, m_conv_pw2_b, m_ffn_up_w, m_ffn_dw_w, m_ffn_dw_b, m_ffn_down_w, v_ada_w, v_ada_b, v_norm_mix_g, v_norm_ffn_g, v_ab_w_in, v_a_vnorm_g, v_a_spatial_w, v_a_spatial_b, v_b_q_norm_g, v_b_k_norm_g, v_ab_w_out, v_conv_pw1_w, v_conv_pw1_b, v_conv_dw_w, v_conv_dw_b, v_conv_ln_g, v_conv_ln_b, v_conv_pw2_w, v_conv_pw2_b, v_ffn_up_w, v_ffn_dw_w, v_ffn_dw_b, v_ffn_down_w):
    weights = dict(ada_w=ada_w, ada_b=ada_b, norm_mix_g=norm_mix_g, norm_ffn_g=norm_ffn_g, ab_w_in=ab_w_in, a_vnorm_g=a_vnorm_g, a_spatial_w=a_spatial_w, a_spatial_b=a_spatial_b, b_q_norm_g=b_q_norm_g, b_k_norm_g=b_k_norm_g, ab_w_out=ab_w_out, conv_pw1_w=conv_pw1_w, conv_pw1_b=conv_pw1_b, conv_dw_w=conv_dw_w, conv_dw_b=conv_dw_b, conv_ln_g=conv_ln_g, conv_ln_b=conv_ln_b, conv_pw2_w=conv_pw2_w, conv_pw2_b=conv_pw2_b, ffn_up_w=ffn_up_w, ffn_dw_w=ffn_dw_w, ffn_dw_b=ffn_dw_b, ffn_down_w=ffn_down_w)
    mom1 = dict(ada_w=m_ada_w, ada_b=m_ada_b, norm_mix_g=m_norm_mix_g, norm_ffn_g=m_norm_ffn_g, ab_w_in=m_ab_w_in, a_vnorm_g=m_a_vnorm_g, a_spatial_w=m_a_spatial_w, a_spatial_b=m_a_spatial_b, b_q_norm_g=m_b_q_norm_g, b_k_norm_g=m_b_k_norm_g, ab_w_out=m_ab_w_out, conv_pw1_w=m_conv_pw1_w, conv_pw1_b=m_conv_pw1_b, conv_dw_w=m_conv_dw_w, conv_dw_b=m_conv_dw_b, conv_ln_g=m_conv_ln_g, conv_ln_b=m_conv_ln_b, conv_pw2_w=m_conv_pw2_w, conv_pw2_b=m_conv_pw2_b, ffn_up_w=m_ffn_up_w, ffn_dw_w=m_ffn_dw_w, ffn_dw_b=m_ffn_dw_b, ffn_down_w=m_ffn_down_w)
    mom2 = dict(ada_w=v_ada_w, ada_b=v_ada_b, norm_mix_g=v_norm_mix_g, norm_ffn_g=v_norm_ffn_g, ab_w_in=v_ab_w_in, a_vnorm_g=v_a_vnorm_g, a_spatial_w=v_a_spatial_w, a_spatial_b=v_a_spatial_b, b_q_norm_g=v_b_q_norm_g, b_k_norm_g=v_b_k_norm_g, ab_w_out=v_ab_w_out, conv_pw1_w=v_conv_pw1_w, conv_pw1_b=v_conv_pw1_b, conv_dw_w=v_conv_dw_w, conv_dw_b=v_conv_dw_b, conv_ln_g=v_conv_ln_g, conv_ln_b=v_conv_ln_b, conv_pw2_w=v_conv_pw2_w, conv_pw2_b=v_conv_pw2_b, ffn_up_w=v_ffn_up_w, ffn_dw_w=v_ffn_dw_w, ffn_dw_b=v_ffn_dw_b, ffn_down_w=v_ffn_down_w)
    order = list(weights)
    d, f2 = D_MODEL, 2 * FFN_DIM
    t = x.shape[1]
    me = 4 * lax.axis_index("x") + 2 * lax.axis_index("y") + lax.axis_index("c")
    for window, dil in PATTERNS:
        assert window // dil == Q_BLOCK and t % (dil * Q_BLOCK) == 0

    small_in = [c[0], conv_pw1_b[0], conv_dw_w[0], conv_dw_b[0], conv_ln_g[0], conv_ln_b[0], conv_pw2_b[0], ffn_dw_w]
    g1 = _all_gather_vmem(_pack(small_in, rows=8), "gather_small").reshape(N_DEV, -1)
    c_all, pw1_b, dw_w, dw_b, ln_g, ln_b, pw2_b, fdw_w = _unpack(g1, [a.shape for a in small_in])
    pw1_b, dw_b, ln_g, ln_b, pw2_b = [a.reshape(1, -1) for a in (pw1_b, dw_b, ln_g, ln_b, pw2_b)]
    dw_w = dw_w.transpose(1, 0, 2).reshape(CONV_WIDTH, d)
    fdw_w = fdw_w.transpose(1, 2, 0, 3).reshape(2, FFN_CONV_WIDTH, f2)

    c16 = jnp.pad(c_all, ((0, 2 * N_DEV - c_all.shape[0]), (0, 0)))
    part = jnp.concatenate([_ada_fwd(c16, ada_w[l], f"ada_fwd{l}")[:N_DEV] for l in range(2)], axis=1)
    g2 = _all_gather_vmem(part, "gather_mod").reshape(N_DEV, N_DEV, 2, -1)
    mod = lax.dynamic_index_in_dim(g2, me, axis=1, keepdims=False).transpose(1, 0, 2).reshape(2, 6 * d) + ada_b

    stages = dict(l0_mix=[ab_w_in[0].T, ab_w_out[0]], l0_ffn=[ffn_up_w[0].T, ffn_down_w[0]],
                  l1_conv=[conv_pw1_w[0].T, conv_pw2_w[0]], l1_ffn=[ffn_up_w[1].T, ffn_down_w[1]])
    stages = {k: [s.astype(BF16) for s in v] for k, v in stages.items()}
    names = dict(l0_mix=("w_in_t", "w_out"), l0_ffn=("up_t", "down"), l1_conv=("pw1_t", "pw2"), l1_ffn=("up_t", "down"))
    ready = {"l0_mix": [a.reshape(-1, d) for a in _all_gather_hbm(stages["l0_mix"], "gather_mixer_weights")]}
    behind = (*ready["l0_mix"], mod)
    arriving = {}
    for stage, group in (("l0_ffn", ("l0_ffn",)), ("l1_conv", ("l1_conv", "l1_ffn"))):
        srcs = [s for g in group for s in stages[g]]
        arriving[stage], token = _pushes_start(
            srcs, _landing_zones(srcs, behind, f"gather_{stage}_zones"), False, f"gather_{stage}_start")
        behind = (token,)
        mod = mod + token[0:1, 0:1]

    def fetch(stage, after):
        if stage in arriving:
            full = [a.reshape(-1, d) for a in _pushes_wait(arriving[stage], after, f"gather_{stage}_wait")[1]]
            ready[stage] = full[:2]
            if stage == "l1_conv":
                ready["l1_ffn"] = full[2:]
        return dict(zip(names[stage], ready[stage]))

    causal = jnp.tril(jnp.ones((CHUNK, CHUNK), bool))
    wtril = jnp.where(causal[None], a_spatial_w[0], 0.0)
    mixer_w = dict(
        gain=a_vnorm_g[0].reshape(1, A_WIDTH), wtril=wtril.astype(BF16),
        wtril_t=wtril.transpose(0, 2, 1).astype(BF16),
        bias_exp=jnp.repeat(a_spatial_b[0].T, GROUP_DIM, axis=1),
        gq=jnp.tile(b_q_norm_g[0], HEADS)[None, :], gk=jnp.tile(b_k_norm_g[0], HEADS)[None, :],
        seg=jnp.kron(jnp.eye(HEADS, dtype=BF16), jnp.ones((HEAD_DIM, HEAD_DIM), BF16)))
    conv_w = dict(pw1_b=pw1_b, dw_w=dw_w, dw_b=dw_b, ln_g=ln_g, ln_b=ln_b, pw2_b=pw2_b)
    ffn_w = [dict(dw_w=fdw_w[l].reshape(FFN_CONV_WIDTH, 2, FFN_DIM).transpose(1, 0, 2), dw_b=ffn_dw_b[l].reshape(2, 1, FFN_DIM))
             for l in range(2)]

    leaving = {}

    def emit(stage, grads):
        blocks = [g.reshape(N_DEV, g.shape[0] // N_DEV, d) for g in grads]
        leaving[stage], token = _pushes_start(
            blocks, [lax.empty(b.shape, b.dtype) for b in blocks], True, f"reduce_{stage}_start")
        return token

    loss, dx, dmod, (g_mix, g_ffn0, g_conv, g_ffn1) = _local_step(
        x[0], loss_target[0], positions[0].astype(F32)[:, None], mod, norm_mix_g, norm_ffn_g, mixer_w, conv_w, ffn_w,
        fetch, emit)

    me_op = me.astype(jnp.int32).reshape(1)

    def reduced(stage, after):
        blocks, lands = _pushes_wait(leaving[stage], after, f"reduce_{stage}_wait")
        return [_sum_with_own(b, a, me_op, f"reduce_{stage}_sum{i}") for i, (b, a) in enumerate(zip(blocks, lands))]

    (r_up_t1, r_down1), (r_pw1_t, r_pw2), (r_up_t0, r_down0) = [reduced(s, dx) for s in ("l1_ffn", "l1_conv", "l0_ffn")]

    small_g = [
        dmod, jnp.concatenate([g_mix["norm_g"], g_conv["norm_g"]]), jnp.concatenate([g_ffn0["norm_g"], g_ffn1["norm_g"]]),
        g_mix["vnorm_g"], g_mix["spatial_w"], g_mix["spatial_b"], g_mix["q_norm_g"], g_mix["k_norm_g"],
        g_conv["pw1_b"], g_conv["dw_w"], g_conv["dw_b"], g_conv["ln_g"], g_conv["ln_b"], g_conv["pw2_b"],
        jnp.stack([g_ffn0["dw_w"], g_ffn1["dw_w"]]), jnp.concatenate([g_ffn0["dw_b"], g_ffn1["dw_b"]])]
    packed = _pack(small_g, rows=8)
    small_leaving, token = _pushes_start([packed], [lax.empty((N_DEV,) + packed.shape, F32)], False, "gather_small_grads_start")

    grads = dict(conv_pw2_w=r_pw2[None], ffn_down_w=jnp.stack([r_down0, r_down1]))
    grads_t = dict(conv_pw1_w=r_pw1_t[None], ffn_up_w=jnp.stack([r_up_t0, r_up_t1]))
    flip = lambda a: jnp.swapaxes(a, 1, 2)
    delta, new_m, new_v = {}, {}, {}

    def update(name, behind):
        if name in grads_t:
            grads[name] = flip(grads_t[name])
            res = _adamw(flip(weights[name]), grads_t[name], flip(mom1[name]), flip(mom2[name]), behind, f"adamw_{name}")
            delta[name], new_m[name], new_v[name] = [flip(r) for r in res]
        else:
            delta[name], new_m[name], new_v[name] = _adamw(
                weights[name], grads[name], mom1[name], mom2[name], behind, f"adamw_{name}")

    for name in ("conv_pw1_w", "conv_pw2_w", "ffn_up_w", "ffn_down_w"):
        update(name, token)
    r_in_t, r_out = reduced("l0_mix", new_v["ffn_down_w"])
    grads_t["ab_w_in"], grads["ab_w_out"] = r_in_t[None], r_out[None]
    update("ab_w_in", token)
    update("ab_w_out", token)

    (packed,), (landed,) = _pushes_wait(small_leaving, tuple(new_v.values()), "gather_small_grads_wait")
    total = _sum_in_device_order(packed, landed, me_op, "sum_small_grads")
    (s_dmod, s_mix_g, s_ffn_g, s_vnorm, s_sp_w, s_sp_b, s_gq, s_gk, s_pw1_b, s_dw_w, s_dw_b, s_ln_g, s_ln_b,
     s_pw2_b, s_fdw_w, s_fdw_b) = _unpack(total.reshape(-1), [a.shape for a in small_g])
    dmod_all = lax.dynamic_update_slice(
        landed.reshape(N_DEV, -1)[:, :dmod.size].reshape((N_DEV,) + dmod.shape), dmod[None], (me, 0, 0))
    n_ada = ada_w.shape[2]
    dmod16 = jnp.pad(_take_block(dmod_all, me, n_ada, 2), ((0, N_DEV), (0, 0), (0, 0)))
    grads.update(
        ada_w=jnp.stack([_ada_bwd(c16, dmod16[:, l], f"ada_bwd{l}") for l in range(2)]),
        ada_b=s_dmod, norm_mix_g=s_mix_g, norm_ffn_g=s_ffn_g,
        a_vnorm_g=s_vnorm[None], a_spatial_w=s_sp_w[None], a_spatial_b=s_sp_b[None], b_q_norm_g=s_gq[None],
        b_k_norm_g=s_gk[None],
        conv_pw1_b=_take_block(s_pw1_b, me, conv_pw1_b.shape[1], 1),
        conv_dw_w=_take_block(s_dw_w, me, conv_dw_w.shape[2], 1)[None],
        conv_dw_b=_take_block(s_dw_b, me, conv_dw_b.shape[1], 1), conv_ln_g=_take_block(s_ln_g, me, conv_ln_g.shape[1], 1),
        conv_ln_b=_take_block(s_ln_b, me, conv_ln_b.shape[1], 1),
        conv_pw2_b=_take_block(s_pw2_b, me, conv_pw2_b.shape[1], 1),
        ffn_dw_w=_take_block(s_fdw_w, me, ffn_dw_w.shape[2], 2), ffn_dw_b=s_fdw_b)
    update("ada_w", None)
    large = ("ada_w", "conv_pw1_w", "conv_pw2_w", "ffn_up_w", "ffn_down_w", "ab_w_in", "ab_w_out")
    small = [n for n in order if n not in large]
    res = _adamw_small(*[[src[n] for n in small] for src in (weights, grads, mom1, mom2)], "adamw_small")
    for dst, arrs in zip((delta, new_m, new_v), res):
        dst.update(zip(small, arrs))

    loss = lax.psum(loss[0, 0], ("x", "y", "c"))
    return (loss, dx[None], *[grads[n] for n in order], *[delta[n] for n in order],
            *[new_m[n] for n in order], *[new_v[n] for n in order])
```

```python
import functools
import math

import jax
import jax.numpy as jnp
from jax import lax
from jax.experimental import pallas as pl
from jax.experimental.pallas import tpu as pltpu

F32 = jnp.float32
BF16 = jnp.bfloat16
MESH = pl.DeviceIdType.MESH

D_MODEL = 1024
A_WIDTH = 512
A_GROUPS = 4
GROUP_DIM = 128
CHUNK = 128
B_WIDTH = 512
HEADS = 8
HEAD_DIM = 64
PATTERNS = ((128, 1), (512, 4), (2048, 16))
Q_BLOCK = 128
ROPE_THETA = 10000.0
AB_IN = 2560
CONV_WIDTH = 31
FFN_DIM = 2816
FFN_CONV_WIDTH = 3
EPS = 1e-6
NEG = -1e30
N_DEV = 8
ADAM_LR, ADAM_B1, ADAM_B2, ADAM_EPS, ADAM_WD, ADAM_STEP = 0.001, 0.9, 0.999, 1e-08, 0.01, 10

V7X_VMEM_LIMIT = 56 * 2**20
FFN_HALO = 16
CONV_HALO = 32

_NN = (((1,), (0,)), ((), ()))
_NT = (((1,), (1,)), ((), ()))
_TN = (((0,), (0,)), ((), ()))


def _tile(n, prefs=(512, 256, 128)):
    for t in prefs:
        if n % t == 0:
            return t
    return n


def _row_tile(n, cap=512):
    best = n
    for t in range(8, min(n, cap) + 1, 8):
        if n % t == 0:
            best = t
    return best if best <= cap else n


def _params(*sem):
    return pltpu.CompilerParams(dimension_semantics=sem, vmem_limit_bytes=V7X_VMEM_LIMIT)


def _dot(a, b, dims):
    return lax.dot_general(a, b, dims, preferred_element_type=F32)


def _sigmoid(x):
    return 1.0 / (1.0 + jnp.exp(-x))


def _gelu(x):
    return 0.5 * x * (1.0 + lax.erf(x * (2.0 ** -0.5)))


def _gelu_grad(x):
    return 0.5 * (1.0 + lax.erf(x * (2.0 ** -0.5))) + x * jnp.exp(-0.5 * x * x) * (1.0 / math.sqrt(2.0 * math.pi))


def _colsum(v):
    return jnp.sum(v, axis=0, keepdims=True)


MATMUL_VMEM_BUDGET = 40 * 2**20


def _matmul_tiles(m, n, k, out_bytes, with_resid):
    def options(dim):
        opts = [t for t in (1024, 512, 256, 128) if dim % t == 0]
        return opts + [dim] if dim <= 4096 and dim not in opts else opts

    best = None
    for tm in options(m):
        for tn in options(n):
            need = 4 * (tm * k + k * tn) + tm * tn * (4 + 2 * out_bytes) + (24 * tm * tn if with_resid else 0)
            if need <= MATMUL_VMEM_BUDGET and (best is None or tm * tn / (tm + tn) > best[0]):
                best = (tm * tn / (tm + tn), tm, tn)
    return best[1], best[2]


def _matmul_tn_acc(a, b, name, tk=1024):
    squeeze = a.ndim == 2
    a3 = a[None] if squeeze else a
    p_, t, m = a3.shape
    n = b.shape[1]
    nk = t // tk

    def body(a_ref, b_ref, o_ref, acc_ref):
        kt = pl.program_id(1)

        @pl.when(kt == 0)
        def _():
            acc_ref[...] = jnp.zeros_like(acc_ref)

        acc_ref[...] += _dot(a_ref[...], b_ref[...], _TN)

        @pl.when(kt == nk - 1)
        def _():
            o_ref[...] = acc_ref[...].astype(BF16)

    out = pl.pallas_call(
        body, name=name, grid=(p_, nk),
        in_specs=[pl.BlockSpec((None, tk, m), lambda p, kt: (p, kt, 0)), pl.BlockSpec((tk, n), lambda p, kt: (kt, 0))],
        out_specs=pl.BlockSpec((None, m, n), lambda p, kt: (p, 0, 0)), out_shape=jax.ShapeDtypeStruct((p_, m, n), BF16),
        scratch_shapes=[pltpu.VMEM((m, n), F32)], compiler_params=_params("parallel", "arbitrary"),
    )(a3, b)
    return out[0] if squeeze else out


def _matmul(a, b, mode, out_dtype, name, bias=None, resid=None):
    if mode == "nn":
        (m, k), (_, n) = a.shape, b.shape
    elif mode == "nt":
        (m, k), (n, _) = a.shape, b.shape
    else:
        (k, m), (_, n) = a.shape, b.shape
    tm, tn = _matmul_tiles(m, n, k, jnp.dtype(out_dtype).itemsize, resid is not None)
    dims = {"nn": _NN, "nt": _NT, "tn": _TN}[mode]
    a_spec = pl.BlockSpec((k, tm), lambda i, j: (0, i)) if mode == "tn" else pl.BlockSpec((tm, k), lambda i, j: (i, 0))
    b_spec = pl.BlockSpec((tn, k), lambda i, j: (j, 0)) if mode == "nt" else pl.BlockSpec((k, tn), lambda i, j: (0, j))
    in_specs, args = [a_spec, b_spec], [a, b]
    row_spec = pl.BlockSpec((1, tn), lambda i, j: (0, j))
    tile_spec = pl.BlockSpec((tm, tn), lambda i, j: (i, j))
    if bias is not None:
        in_specs.append(row_spec)
        args.append(bias)
    if resid is not None:
        in_specs += [tile_spec, row_spec]
        args += list(resid)
    out_shape = [jax.ShapeDtypeStruct((m, n), out_dtype)]
    out_specs = [tile_spec]
    if resid is not None:
        out_shape.append(jax.ShapeDtypeStruct((m, n), F32))
        out_specs.append(tile_spec)

    def body(*refs):
        a_ref, b_ref = refs[0], refs[1]
        pos = 2
        acc = _dot(a_ref[...], b_ref[...], dims)
        if bias is not None:
            acc = acc + refs[pos][...]
            pos += 1
        if resid is not None:
            x_ref, g_ref = refs[pos], refs[pos + 1]
            pos += 2
        refs[pos][...] = acc.astype(out_dtype)
        if resid is not None:
            refs[pos + 1][...] = x_ref[...] + g_ref[...] * acc

    outs = pl.pallas_call(
        body, name=name, grid=(m // tm, n // tn), in_specs=in_specs, out_specs=out_specs, out_shape=out_shape,
        compiler_params=_params("parallel", "parallel"),
    )(*args)
    return outs if resid is not None else outs[0]


def _modnorm(x, g, sc, sh, name):
    t, d = x.shape
    tm = _tile(t)
    row = pl.BlockSpec((1, d), lambda i: (0, 0))
    blk = pl.BlockSpec((tm, d), lambda i: (i, 0))

    def body(x_ref, g_ref, sc_ref, sh_ref, o_ref):
        x = x_ref[...]
        r = lax.rsqrt(jnp.mean(x * x, axis=-1, keepdims=True) + EPS)
        o_ref[...] = ((x * r) * g_ref[...] * (1.0 + sc_ref[...]) + sh_ref[...]).astype(BF16)

    return pl.pallas_call(
        body, name=name, grid=(t // tm,), in_specs=[blk, row, row, row], out_specs=blk,
        out_shape=jax.ShapeDtypeStruct((t, d), BF16), compiler_params=_params("parallel"),
    )(x, g, sc, sh)


def _gate_bwd_tile(dx, y_ref, gate_ref, dy_ref, dgate_ref, first):
    @pl.when(first)
    def _():
        dgate_ref[...] = jnp.zeros_like(dgate_ref)

    dy_ref[...] = (dx * gate_ref[...]).astype(BF16)
    dgate_ref[...] += _colsum(dx * y_ref[...].astype(F32))


def _modnorm_bwd(x, dh, g, sc, dres, below, name):
    t, d = x.shape
    tm = _tile(t)
    row = pl.BlockSpec((1, d), lambda i: (0, 0))
    blk = pl.BlockSpec((tm, d), lambda i: (i, 0))

    def body(x_ref, dh_ref, g_ref, sc_ref, dres_ref, *rest):
        dx_ref, dw_ref, dsh_ref = rest[-5:-2] if below else rest
        first = pl.program_id(0) == 0

        @pl.when(first)
        def _():
            dw_ref[...] = jnp.zeros_like(dw_ref)
            dsh_ref[...] = jnp.zeros_like(dsh_ref)

        x = x_ref[...]
        dh = dh_ref[...].astype(F32)
        r = lax.rsqrt(jnp.mean(x * x, axis=-1, keepdims=True) + EPS)
        xn = x * r
        dxn = dh * (g_ref[...] * (1.0 + sc_ref[...]))
        dx = dres_ref[...] + r * (dxn - xn * jnp.mean(dxn * xn, axis=-1, keepdims=True))
        dx_ref[...] = dx
        dw_ref[...] += _colsum(dh * xn)
        dsh_ref[...] += _colsum(dh)
        if below:
            _gate_bwd_tile(dx, rest[0], rest[1], rest[-2], rest[-1], first)

    row_out = jax.ShapeDtypeStruct((1, d), F32)
    return pl.pallas_call(
        body, name=name, grid=(t // tm,), in_specs=[blk, blk, row, row, blk] + ([blk, row] if below else []),
        out_specs=[blk, row, row] + ([blk, row] if below else []),
        out_shape=[jax.ShapeDtypeStruct((t, d), F32), row_out, row_out]
        + ([jax.ShapeDtypeStruct((t, d), BF16), row_out] if below else []),
        compiler_params=_params("arbitrary"),
    )(x, dh, g, sc, dres, *(below or ()))


def _loss_head(y, target, below, name):
    t, d = y.shape
    tm = _tile(t)
    blk = pl.BlockSpec((tm, d), lambda i: (i, 0))
    row = pl.BlockSpec((1, d), lambda i: (0, 0))
    one = pl.BlockSpec((1, 1), lambda i: (0, 0))
    steps = t // tm

    def body(y_ref, t_ref, yb_ref, gate_ref, dx_ref, loss_ref, dy_ref, dgate_ref, acc_ref):
        first = pl.program_id(0) == 0

        @pl.when(first)
        def _():
            acc_ref[...] = jnp.zeros_like(acc_ref)

        e = y_ref[...] - t_ref[...]
        dx = e * (1.0 / d)
        dx_ref[...] = dx
        acc_ref[...] += _colsum(e * e)
        _gate_bwd_tile(dx, yb_ref, gate_ref, dy_ref, dgate_ref, first)

        @pl.when(pl.program_id(0) == steps - 1)
        def _():
            loss_ref[...] = jnp.sum(acc_ref[...], axis=1, keepdims=True) * (0.5 / d)

    return pl.pallas_call(
        body, name=name, grid=(steps,), in_specs=[blk, blk, blk, row], out_specs=[blk, one, blk, row],
        out_shape=[jax.ShapeDtypeStruct((t, d), F32), jax.ShapeDtypeStruct((1, 1), F32),
                   jax.ShapeDtypeStruct((t, d), BF16), jax.ShapeDtypeStruct((1, d), F32)],
        scratch_shapes=[pltpu.VMEM((1, d), F32)], compiler_params=_params("arbitrary"),
    )(y, target, *below)


GMLP_TM = 512


def _group_norm(vg, gain):
    mu = jnp.mean(vg, axis=-1, keepdims=True)
    xc = vg - mu
    rstd = lax.rsqrt(jnp.mean(xc * xc, axis=-1, keepdims=True) + EPS)
    xhat = xc * rstd
    return xhat, rstd, xhat * gain


def _gmlp_fwd(z, gain, wtril, bias_exp, name):
    t = z.shape[0]
    tm = _tile(t, (GMLP_TM,))
    zu = pl.BlockSpec((tm, A_WIDTH), lambda i: (i, 0))
    zv = pl.BlockSpec((tm, A_WIDTH), lambda i: (i, 1))
    full2 = lambda shp: pl.BlockSpec(shp, lambda i: (0, 0))
    w_spec = pl.BlockSpec((A_GROUPS, CHUNK, CHUNK), lambda i: (0, 0, 0))

    def body(zu_ref, zv_ref, gain_ref, w_ref, b_ref, ya_ref):
        for c in range(tm // CHUNK):
            rows = slice(c * CHUNK, (c + 1) * CHUNK)
            ua = _gelu(zu_ref[rows, :].astype(F32))
            vg = _gelu(zv_ref[rows, :].astype(F32))
            for g in range(A_GROUPS):
                sl = slice(g * GROUP_DIM, (g + 1) * GROUP_DIM)
                _, _, vn = _group_norm(vg[:, sl], gain_ref[:, sl])
                f = _dot(w_ref[g], vn.astype(BF16), _NN) + b_ref[:, sl]
                ya_ref[rows, sl] = (ua[:, sl] * f).astype(BF16)

    return pl.pallas_call(
        body, name=name, grid=(t // tm,),
        in_specs=[zu, zv, full2((1, A_WIDTH)), w_spec, full2((CHUNK, A_WIDTH))], out_specs=zu,
        out_shape=jax.ShapeDtypeStruct((t, A_WIDTH + B_WIDTH), BF16), compiler_params=_params("parallel"),
    )(z, z, gain, wtril, bias_exp)


def _gmlp_bwd(z, dcat, gain, wtril, wtril_t, bias_exp, name):
    t = z.shape[0]
    tm = _tile(t, (GMLP_TM,))
    zu = pl.BlockSpec((tm, A_WIDTH), lambda i: (i, 0))
    zv = pl.BlockSpec((tm, A_WIDTH), lambda i: (i, 1))
    full2 = lambda shp: pl.BlockSpec(shp, lambda i: (0, 0))
    w_spec = pl.BlockSpec((A_GROUPS, CHUNK, CHUNK), lambda i: (0, 0, 0))
    dz_spec = pl.BlockSpec((tm, 2 * A_WIDTH), lambda i: (i, 0))

    def body(zu_ref, zv_ref, dya_ref, gain_ref, w_ref, wt_ref, b_ref, dz_ref, dw_ref, dgain_ref, dbias_ref):
        @pl.when(pl.program_id(0) == 0)
        def _():
            dw_ref[...] = jnp.zeros_like(dw_ref)
            dgain_ref[...] = jnp.zeros_like(dgain_ref)
            dbias_ref[...] = jnp.zeros_like(dbias_ref)

        row = lax.broadcasted_iota(jnp.int32, (CHUNK, CHUNK), 0)
        col = lax.broadcasted_iota(jnp.int32, (CHUNK, CHUNK), 1)
        for c in range(tm // CHUNK):
            rows = slice(c * CHUNK, (c + 1) * CHUNK)
            zu_v = zu_ref[rows, :].astype(F32)
            zv_v = zv_ref[rows, :].astype(F32)
            dya = dya_ref[rows, :].astype(F32)
            ua = _gelu(zu_v)
            vg = _gelu(zv_v)
            for g in range(A_GROUPS):
                sl = slice(g * GROUP_DIM, (g + 1) * GROUP_DIM)
                gain_g = gain_ref[:, sl]
                xhat, rstd, vn = _group_norm(vg[:, sl], gain_g)
                vn16 = vn.astype(BF16)
                f = _dot(w_ref[g], vn16, _NN) + b_ref[:, sl]
                df = dya[:, sl] * ua[:, sl]
                df16 = df.astype(BF16)
                dz_ref[rows, sl] = (dya[:, sl] * f * _gelu_grad(zu_v[:, sl])).astype(BF16)
                dw_ref[g] += jnp.where(row >= col, _dot(df16, vn16, _NT), 0.0)
                dvn = _dot(wt_ref[g], df16, _NN)
                dgain_ref[:, sl] += _colsum(dvn * xhat)
                dxh = dvn * gain_g
                dvg = rstd * (dxh - jnp.mean(dxh, axis=-1, keepdims=True) - xhat * jnp.mean(dxh * xhat, axis=-1, keepdims=True))
                dz_ref[rows, A_WIDTH + g * GROUP_DIM:A_WIDTH + (g + 1) * GROUP_DIM] = (dvg * _gelu_grad(zv_v[:, sl])).astype(BF16)
                dbias_ref[:, sl] += df

    return pl.pallas_call(
        body, name=name, grid=(t // tm,),
        in_specs=[zu, zv, zu, full2((1, A_WIDTH)), w_spec, w_spec, full2((CHUNK, A_WIDTH))],
        out_specs=[dz_spec, w_spec, full2((1, A_WIDTH)), full2((CHUNK, A_WIDTH))],
        out_shape=[jax.ShapeDtypeStruct((t, 2 * A_WIDTH), BF16), jax.ShapeDtypeStruct((A_GROUPS, CHUNK, CHUNK), F32),
                   jax.ShapeDtypeStruct((1, A_WIDTH), F32), jax.ShapeDtypeStruct((CHUNK, A_WIDTH), F32)],
        compiler_params=_params("arbitrary"),
    )(z, z, dcat, gain, wtril, wtril_t, bias_exp)


def _rope_tables(pos, inv_freq, sign, name):
    t = pos.shape[0]
    tm = _tile(t)
    row = pl.BlockSpec((1, B_WIDTH), lambda i: (0, 0))
    blk = pl.BlockSpec((tm, B_WIDTH), lambda i: (i, 0))

    def body(pos_ref, f_ref, s_ref, cos_ref, sin_ref):
        ang = pos_ref[...] * f_ref[:, 0:LANES]
        cos_ref[...] = jnp.tile(jnp.cos(ang), (1, B_WIDTH // LANES))
        sin_ref[...] = jnp.tile(jnp.sin(ang) * s_ref[:, 0:LANES], (1, B_WIDTH // LANES))

    return pl.pallas_call(
        body, name=name, grid=(t // tm,), in_specs=[pl.BlockSpec((tm, 1), lambda i: (i, 0)), row, row],
        out_specs=[blk, blk], out_shape=[jax.ShapeDtypeStruct((t, B_WIDTH), F32)] * 2,
        compiler_params=_params("parallel"),
    )(pos, inv_freq, sign)


def _head_sum(v, seg):
    hi = v.astype(BF16)
    lo = (v - hi.astype(F32)).astype(BF16)
    return _dot(hi, seg, _NN) + _dot(lo, seg, _NN)


def _swap_halves(v):
    lane = lax.broadcasted_iota(jnp.int32, v.shape, 1)
    return jnp.where((lane & (HEAD_DIM - 1)) < HEAD_DIM // 2,pltpu.roll(v, B_WIDTH - HEAD_DIM // 2, 1), pltpu.roll(v, HEAD_DIM // 2, 1))


DILATIONS = tuple(dil for _, dil in PATTERNS)
SUBSEQ_TM = 256
LANES = 128


def _subseq_shape(t, dil):
    return (t // dil, dil * B_WIDTH)


def _subseq_spec(tm, dil):
    return pl.BlockSpec((tm // dil, dil * B_WIDTH), lambda i: (i, 0))


def _to_subseq(x, scr_ref, dil):
    if dil == 1:
        return x
    tm, w = x.shape
    for c in range(w // LANES):
        scr_ref[c * tm:(c + 1) * tm, :] = x[:, c * LANES:(c + 1) * LANES]
    return jnp.concatenate([scr_ref[pl.ds(c * tm + r, tm // dil, stride=dil), :]
                            for r in range(dil) for c in range(w // LANES)], axis=1)


def _from_subseq(y, scr_ref, dil):
    if dil == 1:
        return y
    n, w = y.shape[0], y.shape[1] // dil
    tm = n * dil
    for r in range(dil):
        for c in range(w // LANES):
            scr_ref[pl.ds(c * tm + r, n, stride=dil), :] = y[:, r * w + c * LANES:r * w + (c + 1) * LANES]
    return jnp.concatenate([scr_ref[c * tm:(c + 1) * tm, :] for c in range(w // LANES)], axis=1)


def _subseq_scratch(tm):
    return pltpu.VMEM((B_WIDTH // LANES * tm, LANES), F32)


def _qk_prep(z, cos_t, sin_t, gq, gk, seg, name):
    t = z.shape[0]
    tm = _tile(t, (SUBSEQ_TM,))
    col = lambda c: pl.BlockSpec((tm, B_WIDTH), lambda i: (i, c))
    row = pl.BlockSpec((1, B_WIDTH), lambda i: (0, 0))
    blk = col(0)
    nd = len(DILATIONS)

    def body(q_ref, k_ref, v_ref, cos_ref, sin_ref, gq_ref, gk_ref, seg_ref, *rest):
        out_refs, scr_ref = rest[:-1], rest[-1]

        def norm_rot(x, g):
            r = lax.rsqrt(_head_sum(x * x, seg_ref[...]) * (1.0 / HEAD_DIM) + EPS)
            xn = x * r * g
            return xn * cos_ref[...] + _swap_halves(xn) * sin_ref[...]

        vals = (norm_rot(q_ref[...].astype(F32), gq_ref[...]), norm_rot(k_ref[...].astype(F32), gk_ref[...]),
                v_ref[...].astype(F32))
        for a, val in enumerate(vals):
            for b, dil in enumerate(DILATIONS):
                out_refs[a * nd + b][...] = _to_subseq(val, scr_ref, dil).astype(BF16)

    outs = pl.pallas_call(
        body, name=name, grid=(t // tm,),
        in_specs=[col(2), col(3), col(4), blk, blk, row, row, pl.BlockSpec((B_WIDTH, B_WIDTH), lambda i: (0, 0))],
        out_specs=[_subseq_spec(tm, dil) for _ in range(3) for dil in DILATIONS],
        out_shape=[jax.ShapeDtypeStruct(_subseq_shape(t, dil), BF16) for _ in range(3) for dil in DILATIONS],
        scratch_shapes=[_subseq_scratch(tm)], compiler_params=_params("parallel"),
    )(z, z, z, cos_t, sin_t, gq, gk, seg)
    return outs[:nd], outs[nd:2 * nd], outs[2 * nd:]


def _qk_prep_bwd(z, dqs, dks, dvs, cos_t, sin_t, gq, gk, seg, name):
    t = z.shape[0]
    tm = _tile(t, (SUBSEQ_TM,))
    col = lambda c: pl.BlockSpec((tm, B_WIDTH), lambda i: (i, c))
    row = pl.BlockSpec((1, B_WIDTH), lambda i: (0, 0))
    blk = col(0)
    nb = len(DILATIONS)
    subs = [_subseq_spec(tm, dil) for dil in DILATIONS]

    def body(*refs):
        q_ref, k_ref = refs[0], refs[1]
        dq_refs, dk_refs, dv_refs = refs[2:2 + nb], refs[2 + nb:2 + 2 * nb], refs[2 + 2 * nb:2 + 3 * nb]
        cos_ref, sin_ref, gq_ref, gk_ref, seg_ref, dz_ref, dgq_ref, dgk_ref, scr_ref = refs[2 + 3 * nb:]

        @pl.when(pl.program_id(0) == 0)
        def _():
            dgq_ref[...] = jnp.zeros_like(dgq_ref)
            dgk_ref[...] = jnp.zeros_like(dgk_ref)

        def total(d_refs):
            return sum(_from_subseq(r_[...], scr_ref, dil) for r_, dil in zip(d_refs, DILATIONS))

        def back(x, d_refs, g, dg_ref):
            dout = total(d_refs)
            dy = dout * cos_ref[...] + _swap_halves(dout * sin_ref[...])
            r = lax.rsqrt(_head_sum(x * x, seg_ref[...]) * (1.0 / HEAD_DIM) + EPS)
            xn = x * r
            dg_ref[...] += _colsum(dy * xn)
            dxn = dy * g
            return r * (dxn - xn * (_head_sum(dxn * xn, seg_ref[...]) * (1.0 / HEAD_DIM)))

        dz_ref[:, 0:B_WIDTH] = back(q_ref[...].astype(F32), dq_refs, gq_ref[...], dgq_ref).astype(BF16)
        dz_ref[:, B_WIDTH:2 * B_WIDTH] = back(k_ref[...].astype(F32), dk_refs, gk_ref[...], dgk_ref).astype(BF16)
        dz_ref[:, 2 * B_WIDTH:3 * B_WIDTH] = total(dv_refs).astype(BF16)

    return pl.pallas_call(
        body, name=name, grid=(t // tm,),
        in_specs=[col(2), col(3)] + subs * 3 + [blk, blk, row, row, pl.BlockSpec((B_WIDTH, B_WIDTH), lambda i: (0, 0))],
        out_specs=[pl.BlockSpec((tm, 3 * B_WIDTH), lambda i: (i, 0)), row, row],
        out_shape=[jax.ShapeDtypeStruct((t, 3 * B_WIDTH), BF16), jax.ShapeDtypeStruct((1, B_WIDTH), F32),
                   jax.ShapeDtypeStruct((1, B_WIDTH), F32)],
        scratch_shapes=[_subseq_scratch(tm)], compiler_params=_params("arbitrary"),
    )(z, z, *dqs, *dks, *dvs, cos_t, sin_t, gq, gk, seg)


def _subseq_views(x, col, name):
    t = x.shape[0]
    tm = _tile(t, (SUBSEQ_TM,))

    def body(x_ref, *rest):
        out_refs, scr_ref = rest[:-1], rest[-1]
        val = x_ref[...].astype(F32)
        for o_ref, dil in zip(out_refs, DILATIONS):
            o_ref[...] = _to_subseq(val, scr_ref, dil).astype(o_ref.dtype)

    return pl.pallas_call(
        body, name=name, grid=(t // tm,), in_specs=[pl.BlockSpec((tm, B_WIDTH), lambda i: (i, col))],
        out_specs=[_subseq_spec(tm, dil) for dil in DILATIONS],
        out_shape=[jax.ShapeDtypeStruct(_subseq_shape(t, dil), x.dtype) for dil in DILATIONS],
        scratch_shapes=[_subseq_scratch(tm)], compiler_params=_params("parallel"),
    )(x)


def _attn_fwd(q, k, v, dil, name):
    t = q.shape[0] * dil
    nb = t // dil // Q_BLOCK
    cur = pl.BlockSpec((Q_BLOCK, B_WIDTH), lambda r, i: (i, r))
    prev = pl.BlockSpec((Q_BLOCK, B_WIDTH), lambda r, i: (jnp.maximum(i - 1, 0), r))

    def body(q_ref, kp_ref, kc_ref, vp_ref, vc_ref, o_ref, lse_ref):
        i = pl.program_id(1)
        q = q_ref[...]
        kk = jnp.concatenate([kp_ref[...], kc_ref[...]], axis=0)
        vv = jnp.concatenate([vp_ref[...], vc_ref[...]], axis=0)
        a = lax.broadcasted_iota(jnp.int32, (Q_BLOCK, 2 * Q_BLOCK), 0)
        j = lax.broadcasted_iota(jnp.int32, (Q_BLOCK, 2 * Q_BLOCK), 1)
        dist = a + Q_BLOCK - j
        mask = (dist >= 0) & (dist <= Q_BLOCK) & ((j >= Q_BLOCK) | (i > 0))
        sls = [slice(h * HEAD_DIM, (h + 1) * HEAD_DIM) for h in range(HEADS)]
        scores = [_dot(q[:, sl], kk[:, sl], _NT) for sl in sls]
        ps, dens = [], []
        for sl, s in zip(sls, scores):
            s = jnp.where(mask, s * (HEAD_DIM ** -0.5), NEG)
            m = jnp.max(s, axis=-1, keepdims=True)
            p = jnp.exp(s - m)
            den = jnp.sum(p, axis=-1, keepdims=True)
            ps.append(p.astype(BF16))
            dens.append(den)
            lse_ref[:, sl] = jnp.broadcast_to(m + jnp.log(den), (Q_BLOCK, HEAD_DIM))
        for sl, p, den in zip(sls, ps, dens):
            o_ref[:, sl] = _dot(p, vv[:, sl], _NN) / den

    return pl.pallas_call(
        body, name=name, grid=(dil, nb), in_specs=[cur, prev, cur, prev, cur], out_specs=[cur, cur],
        out_shape=[jax.ShapeDtypeStruct(_subseq_shape(t, dil), F32)] * 2,
        compiler_params=_params("parallel", "parallel"),
    )(q, k, k, v, v)


def _attn_merge(outs, lses, cat, name):
    nb = len(DILATIONS)
    t = cat.shape[0]
    tm = _tile(t, (SUBSEQ_TM,))
    subs = [_subseq_spec(tm, dil) for dil in DILATIONS]

    def body(*refs):
        o_refs, l_refs = refs[:nb], refs[nb:2 * nb]
        yb_refs, lse_refs, cat_ref, scr_ref = refs[2 * nb + 1:3 * nb + 1], refs[3 * nb + 1:4 * nb + 1], refs[4 * nb + 1], refs[4 * nb + 2]
        ls = [_from_subseq(r[...], scr_ref, dil) for r, dil in zip(l_refs, DILATIONS)]
        m = functools.reduce(jnp.maximum, ls)
        tot = m + jnp.log(sum(jnp.exp(l - m) for l in ls))
        yb = sum(jnp.exp(l - tot) * _from_subseq(o[...], scr_ref, dil) for l, o, dil in zip(ls, o_refs, DILATIONS))
        cat_ref[...] = yb.astype(BF16)
        yb = yb.astype(BF16).astype(F32)
        for yb_ref, lse_ref, dil in zip(yb_refs, lse_refs, DILATIONS):
            yb_ref[...] = _to_subseq(yb, scr_ref, dil).astype(BF16)
            lse_ref[...] = _to_subseq(tot, scr_ref, dil)

    outs_ = pl.pallas_call(
        body, name=name, grid=(t // tm,), in_specs=subs * 2 + [pl.BlockSpec(memory_space=pl.ANY)],
        out_specs=subs * 2 + [pl.BlockSpec((tm, B_WIDTH), lambda i: (i, A_WIDTH // B_WIDTH))],
        out_shape=[jax.ShapeDtypeStruct(_subseq_shape(t, dil), BF16) for dil in DILATIONS]
        + [jax.ShapeDtypeStruct(_subseq_shape(t, dil), F32) for dil in DILATIONS] + [jax.ShapeDtypeStruct(cat.shape, BF16)],
        input_output_aliases={2 * nb: 2 * nb}, scratch_shapes=[_subseq_scratch(tm)], compiler_params=_params("parallel"),
    )(*outs, *lses, cat)
    return outs_[:nb], outs_[nb:2 * nb], outs_[2 * nb]


def _attn_bwd(q, k, v, do, o, lse, dil, name):
    t = q.shape[0] * dil
    nb = t // dil // Q_BLOCK
    cur = pl.BlockSpec((Q_BLOCK, B_WIDTH), lambda r, i: (i, r))
    prev = pl.BlockSpec((Q_BLOCK, B_WIDTH), lambda r, i: (jnp.maximum(i - 1, 0), r))
    scale = HEAD_DIM ** -0.5

    def body(q_ref, kp_ref, kc_ref, vp_ref, vc_ref, do_ref, o_ref, lse_ref, dq_ref, dk_ref, dv_ref,
             ck_ref, cv_ref, tk_ref, tv_ref):
        i = pl.program_id(1)

        @pl.when(i == 0)
        def _():
            ck_ref[...] = jnp.zeros_like(ck_ref)
            cv_ref[...] = jnp.zeros_like(cv_ref)

        q = q_ref[...]
        kk = jnp.concatenate([kp_ref[...], kc_ref[...]], axis=0)
        vv = jnp.concatenate([vp_ref[...], vc_ref[...]], axis=0)
        do = do_ref[...]
        dof = do.astype(F32)
        of = o_ref[...].astype(F32)
        a = lax.broadcasted_iota(jnp.int32, (Q_BLOCK, 2 * Q_BLOCK), 0)
        j = lax.broadcasted_iota(jnp.int32, (Q_BLOCK, 2 * Q_BLOCK), 1)
        dist = a + Q_BLOCK - j
        mask = (dist >= 0) & (dist <= Q_BLOCK) & ((j >= Q_BLOCK) | (i > 0))
        sls = [slice(h * HEAD_DIM, (h + 1) * HEAD_DIM) for h in range(HEADS)]
        scores = [_dot(q[:, sl], kk[:, sl], _NT) for sl in sls]
        dps = [_dot(do[:, sl], vv[:, sl], _NT) for sl in sls]
        ps, dss = [], []
        for sl, s, dp in zip(sls, scores, dps):
            p = jnp.exp(jnp.where(mask, s * scale, NEG) - lse_ref[:, sl.start:sl.start + 1])
            delta = jnp.sum(dof[:, sl] * of[:, sl], axis=-1, keepdims=True)
            dss.append((p * (dp - delta) * scale).astype(BF16))
            ps.append(p.astype(BF16))
        for sl, p, ds in zip(sls, ps, dss):
            dq_ref[:, sl] = _dot(ds, kk[:, sl], _NN)
            dv_t = _dot(do[:, sl], p, _TN)
            dk_t = _dot(q[:, sl], ds, _TN)
            tk_ref[sl, :] = ck_ref[sl, :] + dk_t[:, :Q_BLOCK]
            tv_ref[sl, :] = cv_ref[sl, :] + dv_t[:, :Q_BLOCK]
            ck_ref[sl, :] = dk_t[:, Q_BLOCK:]
            cv_ref[sl, :] = dv_t[:, Q_BLOCK:]

        @pl.when(i >= 1)
        def _():
            rows = pl.ds(pl.multiple_of((i - 1) * Q_BLOCK, Q_BLOCK), Q_BLOCK)
            dk_ref[rows, :] = tk_ref[...].T
            dv_ref[rows, :] = tv_ref[...].T

        @pl.when(i == nb - 1)
        def _():
            rows = pl.ds((nb - 1) * Q_BLOCK, Q_BLOCK)
            dk_ref[rows, :] = ck_ref[...].T
            dv_ref[rows, :] = cv_ref[...].T

    whole = pl.BlockSpec((t // dil, B_WIDTH), lambda r, i: (0, r))
    return pl.pallas_call(
        body, name=name, grid=(dil, nb), in_specs=[cur, prev, cur, prev, cur, cur, cur, cur],
        out_specs=[cur, whole, whole], out_shape=[jax.ShapeDtypeStruct(_subseq_shape(t, dil), F32)] * 3,
        scratch_shapes=[pltpu.VMEM((B_WIDTH, Q_BLOCK), F32)] * 4,
        compiler_params=_params("parallel", "arbitrary"),
    )(q, k, k, v, v, do, o, lse)


FFN_TN = 256
FFN_FWD_CHUNK = 256
FFN_BWD_CHUNK = 128


def _ffn_up(h, up_t, name):
    t, k = h.shape
    tm = _tile(t)

    def body(h_ref, w_ref, o_ref):
        o_ref[...] = _dot(h_ref[...], w_ref[...], _NT).astype(BF16)

    return pl.pallas_call(
        body, name=name, grid=(2, t // tm),
        in_specs=[pl.BlockSpec((tm, k), lambda p, i: (i, 0)), pl.BlockSpec((None, FFN_DIM, k), lambda p, i: (p, 0, 0))],
        out_specs=pl.BlockSpec((None, tm, FFN_DIM), lambda p, i: (p, i, 0)),
        out_shape=jax.ShapeDtypeStruct((2, t, FFN_DIM), BF16), compiler_params=_params("parallel", "parallel"),
    )(h, up_t.reshape(2, FFN_DIM, k))


def _ffn_up_dx(du, up_t, name):
    t = du.shape[1]
    k = up_t.shape[1]
    tm = _tile(t)

    def body(a_ref, b_ref, o_ref):
        o_ref[...] = _dot(a_ref[0], b_ref[0], _NN) + _dot(a_ref[1], b_ref[1], _NN)

    return pl.pallas_call(
        body, name=name, grid=(t // tm,),
        in_specs=[pl.BlockSpec((2, tm, FFN_DIM), lambda i: (0, i, 0)), pl.BlockSpec((2, FFN_DIM, k), lambda i: (0, 0, 0))],
        out_specs=pl.BlockSpec((tm, k), lambda i: (i, 0)), out_shape=jax.ShapeDtypeStruct((t, k), F32),
        compiler_params=_params("parallel"),
    )(du, up_t.reshape(2, FFN_DIM, k))


def _ffn_conv(win, w_ref, b_ref, p):
    x = win.astype(F32)
    x0, x1, x2 = x[FFN_HALO:], pltpu.roll(x, 1, 0)[FFN_HALO:], pltpu.roll(x, 2, 0)[FFN_HALO:]
    return b_ref[p] + w_ref[p, 2:3, :] * x0 + w_ref[p, 1:2, :] * x1 + w_ref[p, 0:1, :] * x2


def _zero_if(cond, v):
    return jnp.where(cond, 0, v).astype(v.dtype)


def _ffn_act(u, dw_w, dw_b, name):
    t = u.shape[1]
    tm = _tile(t)
    chunk = min(FFN_FWD_CHUNK, tm)
    hb = tm // FFN_HALO
    main = pl.BlockSpec((2, tm, FFN_TN), lambda i, j: (0, i, j))
    halo = pl.BlockSpec((2, FFN_HALO, FFN_TN), lambda i, j: (0, jnp.maximum(i * hb - 1, 0), j))
    wsp = pl.BlockSpec((2, FFN_CONV_WIDTH, FFN_TN), lambda i, j: (0, 0, j))
    bsp = pl.BlockSpec((2, 1, FFN_TN), lambda i, j: (0, 0, j))

    def body(u_ref, uh_ref, w_ref, b_ref, o_ref, z_ref):
        first = pl.program_id(0) == 0

        def emit(rows, wins):
            za, zb = _ffn_conv(wins[0], w_ref, b_ref, 0), _ffn_conv(wins[1], w_ref, b_ref, 1)
            o_ref[rows, :] = (za * _sigmoid(za) * zb).astype(BF16)
            z_ref[0, rows, :] = za.astype(BF16)
            z_ref[1, rows, :] = zb.astype(BF16)

        emit(pl.ds(0, chunk), [jnp.concatenate([_zero_if(first, uh_ref[p]), u_ref[p, 0:chunk, :]], axis=0) for p in range(2)])

        def step(c, carry):
            s = pl.multiple_of(c * chunk, chunk)
            emit(pl.ds(s, chunk), [u_ref[p, pl.ds(s - FFN_HALO, chunk + FFN_HALO), :] for p in range(2)])
            return carry

        lax.fori_loop(1, tm // chunk, step, 0)

    return pl.pallas_call(
        body, name=name, grid=(t // tm, FFN_DIM // FFN_TN), in_specs=[main, halo, wsp, bsp],
        out_specs=[pl.BlockSpec((tm, FFN_TN), lambda i, j: (i, j)), main],
        out_shape=[jax.ShapeDtypeStruct((t, FFN_DIM), BF16), jax.ShapeDtypeStruct((2, t, FFN_DIM), BF16)],
        compiler_params=_params("parallel", "parallel"),
    )(u, u, dw_w, dw_b)


def _fold8(v):
    return jnp.sum(v.reshape(v.shape[0] // 8, 8, v.shape[1]), axis=0)


def _ffn_act_bwd(u, z, dact, dw_w, name):
    t = u.shape[1]
    tm = _tile(t)
    chunk = min(FFN_BWD_CHUNK, tm // 2)
    halo = FFN_HALO
    hb = tm // halo
    nt = t // tm
    last_halo = t // halo - 1
    next_i = lambda i: jnp.minimum((i + 1) * hb, last_halo)
    main = pl.BlockSpec((2, tm, FFN_TN), lambda j, i: (0, i, j))
    nxt = pl.BlockSpec((2, halo, FFN_TN), lambda j, i: (0, next_i(i), j))
    wsp = pl.BlockSpec((2, FFN_CONV_WIDTH, FFN_TN), lambda j, i: (0, 0, j))
    bsp = pl.BlockSpec((2, 1, FFN_TN), lambda j, i: (0, 0, j))

    def body(u_ref, z_ref, zn_ref, da_ref, dan_ref, w_ref, du_ref, dw_ref, db_ref, acc_ref):
        i = pl.program_id(1)
        last = i == nt - 1
        acc_ref[...] = jnp.zeros_like(acc_ref)

        def emit(rows, zs, dact):
            n = chunk + halo
            za, zb, dact = zs[0].astype(F32), zs[1].astype(F32), dact.astype(F32)
            sg = _sigmoid(za)
            dzs = (dact * zb * (sg * (1.0 + za * (1.0 - sg))), dact * (za * sg))
            for p, dz in enumerate(dzs):
                ahead = (dz[:chunk], pltpu.roll(dz, n - 1, 0)[:chunk], pltpu.roll(dz, n - 2, 0)[:chunk])
                um = u_ref[p, rows, :].astype(F32)
                acc_ref[p, FFN_CONV_WIDTH] += _fold8(ahead[0])
                du = None
                for j, dzj in enumerate(ahead):
                    k = FFN_CONV_WIDTH - 1 - j
                    acc_ref[p, k] += _fold8(dzj * um)
                    term = w_ref[p, k:k + 1, :] * dzj
                    du = term if du is None else du + term
                du_ref[p, rows, :] = du.astype(BF16)

        def step(c, carry):
            s = pl.multiple_of(c * chunk, chunk)
            emit(pl.ds(s, chunk), [z_ref[p, pl.ds(s, chunk + halo), :] for p in range(2)], da_ref[pl.ds(s, chunk + halo), :])
            return carry

        lax.fori_loop(0, tm // chunk - 1, step, 0)
        s = tm - chunk
        emit(pl.ds(s, chunk),
             [jnp.concatenate([z_ref[p, s:tm, :], zn_ref[p]], axis=0) for p in range(2)],
             jnp.concatenate([da_ref[s:tm, :], _zero_if(last, dan_ref[...])], axis=0))

        @pl.when(i == 0)
        def _():
            dw_ref[...] = jnp.zeros_like(dw_ref)
            db_ref[...] = jnp.zeros_like(db_ref)

        for p in range(2):
            for k in range(FFN_CONV_WIDTH):
                dw_ref[p, k:k + 1, :] += _colsum(acc_ref[p, k])
            db_ref[p] += _colsum(acc_ref[p, FFN_CONV_WIDTH])

    return pl.pallas_call(
        body, name=name, grid=(FFN_DIM // FFN_TN, nt),
        in_specs=[main, main, nxt, pl.BlockSpec((tm, FFN_TN), lambda j, i: (i, j)),
                  pl.BlockSpec((halo, FFN_TN), lambda j, i: (next_i(i), j)), wsp],
        out_specs=[main, wsp, bsp],
        out_shape=[jax.ShapeDtypeStruct((2, t, FFN_DIM), BF16), jax.ShapeDtypeStruct((2, FFN_CONV_WIDTH, FFN_DIM), F32),
                   jax.ShapeDtypeStruct((2, 1, FFN_DIM), F32)],
        scratch_shapes=[pltpu.VMEM((2, FFN_CONV_WIDTH + 1, 8, FFN_TN), F32)],
        compiler_params=_params("parallel", "arbitrary"),
    )(u, z, z, dact, dact, dw_w)


CONV_TM = 256
CONV_ROWS = 128
CONV_LANES = 128
CONV_NORM_ROWS = 32


def _glu_window(pa_ref, pah_ref, pg_ref, pgh_ref, scr_ref, first):
    ah, gh = pah_ref[...].astype(F32), pgh_ref[...].astype(F32)
    scr_ref[0:CONV_HALO, :] = jnp.where(first, 0.0, ah * _sigmoid(gh))
    scr_ref[CONV_HALO:, :] = pa_ref[...].astype(F32) * _sigmoid(pg_ref[...].astype(F32))


def _tap_slabs(win, rows, ahead):
    n = win.shape[0]
    for s in range(8):
        ws = win if s == 0 else pltpu.roll(win, n - s if ahead else s, 0)
        for q in range(CONV_HALO // 8):
            o = 8 * q + s
            if o < CONV_WIDTH:
                start = 8 * q if ahead else CONV_HALO - 8 * q
                yield CONV_WIDTH - 1 - o, ws[start:start + rows]


def _conformer_specs(t):
    tm = _tile(t, (CONV_TM, 128))
    hb = tm // CONV_HALO
    d = D_MODEL
    main = lambda c: pl.BlockSpec((tm, d), lambda i: (i, c))
    halo = lambda c: pl.BlockSpec((CONV_HALO, d), lambda i: (jnp.maximum(i * hb - 1, 0), c))
    row = pl.BlockSpec((1, d), lambda i: (0, 0))
    wsp = pl.BlockSpec((CONV_WIDTH, d), lambda i: (0, 0))
    return tm, main, halo, row, wsp


def _conformer_mid(p, dw_w, dw_b, ln_g, ln_b, name):
    t = p.shape[0]
    tm, main, halo, row, wsp = _conformer_specs(t)
    d, lanes = D_MODEL, CONV_LANES

    def body(pa_ref, pah_ref, pg_ref, pgh_ref, w_ref, b_ref, g_ref, lb_ref, o_ref, dc_ref, scr_ref):
        _glu_window(pa_ref, pah_ref, pg_ref, pgh_ref, scr_ref, pl.program_id(0) == 0)
        for c in range(d // lanes):
            ls = slice(c * lanes, (c + 1) * lanes)
            acc = jnp.broadcast_to(b_ref[:, ls], (tm, lanes))
            for k, slab in _tap_slabs(scr_ref[:, ls], tm, False):
                acc = acc + w_ref[k:k + 1, ls] * slab
            dc_ref[:, ls] = acc

        def norm(r, carry):
            r0 = pl.multiple_of(r * CONV_NORM_ROWS, CONV_NORM_ROWS)
            dc = dc_ref[pl.ds(r0, CONV_NORM_ROWS), :]
            xc = dc - jnp.mean(dc, axis=-1, keepdims=True)
            ln = xc * lax.rsqrt(jnp.mean(xc * xc, axis=-1, keepdims=True) + EPS) * g_ref[...] + lb_ref[...]
            o_ref[pl.ds(r0, CONV_NORM_ROWS), :] = (ln * _sigmoid(ln)).astype(BF16)
            return carry

        lax.fori_loop(0, tm // CONV_NORM_ROWS,norm, 0)

    return pl.pallas_call(
        body, name=name, grid=(t // tm,), in_specs=[main(0), halo(0), main(1), halo(1), wsp, row, row, row],
        out_specs=[main(0), main(0)], out_shape=[jax.ShapeDtypeStruct((t, d), BF16), jax.ShapeDtypeStruct((t, d), F32)],
        scratch_shapes=[pltpu.VMEM((tm + CONV_HALO, d), F32)], compiler_params=_params("parallel"),
    )(p, p, p, p, dw_w, dw_b, ln_g, ln_b)


def _conformer_mid_bwd(p, dc, ds, ln_g, ln_b, name):
    t = p.shape[0]
    tm, main, halo, row, wsp = _conformer_specs(t)
    d, nt = D_MODEL, t // tm
    rows, lanes = CONV_ROWS, CONV_LANES

    def body(pa_ref, pah_ref, pg_ref, pgh_ref, dc_ref, ds_ref, g_ref, lb_ref,
             ddc_ref, dw_ref, db_ref, dg_ref, dlb_ref, scr_ref, wacc_ref, racc_ref):
        i = pl.program_id(0)

        @pl.when(i == 0)
        def _():
            wacc_ref[...] = jnp.zeros_like(wacc_ref)
            racc_ref[...] = jnp.zeros_like(racc_ref)

        _glu_window(pa_ref, pah_ref, pg_ref, pgh_ref, scr_ref, i == 0)

        def norm_bwd(r, carry):
            r0 = pl.multiple_of(r * CONV_NORM_ROWS, CONV_NORM_ROWS)
            dcv = dc_ref[pl.ds(r0, CONV_NORM_ROWS), :]
            xc = dcv - jnp.mean(dcv, axis=-1, keepdims=True)
            rstd = lax.rsqrt(jnp.mean(xc * xc, axis=-1, keepdims=True) + EPS)
            xhat = xc * rstd
            ln = xhat * g_ref[...] + lb_ref[...]
            sg = _sigmoid(ln)
            dln = ds_ref[pl.ds(r0, CONV_NORM_ROWS), :].astype(F32) * (sg * (1.0 + ln * (1.0 - sg)))
            dxh = dln * g_ref[...]
            ddc = rstd * (dxh - jnp.mean(dxh, axis=-1, keepdims=True) - xhat * jnp.mean(dxh * xhat, axis=-1, keepdims=True))
            ddc_ref[pl.ds(r0, CONV_NORM_ROWS), :] = ddc
            racc_ref[0] += _fold8(dln * xhat)
            racc_ref[1] += _fold8(dln)
            racc_ref[2] += _fold8(ddc)
            return carry

        lax.fori_loop(0, tm // CONV_NORM_ROWS,norm_bwd, 0)

        for c in range(d // lanes):
            ls = slice(c * lanes, (c + 1) * lanes)

            def taps(r, carry, ls=ls):
                r0 = pl.multiple_of(r * rows, rows)
                ddc = ddc_ref[pl.ds(r0, rows), ls]
                for k, slab in _tap_slabs(scr_ref[pl.ds(r0, rows + CONV_HALO), ls], rows, False):
                    wacc_ref[k, :, ls] += _fold8(ddc * slab)
                return carry

            lax.fori_loop(0, tm // rows, taps, 0)

        @pl.when(i == nt - 1)
        def _():
            for k in range(CONV_WIDTH):
                dw_ref[k:k + 1, :] = _colsum(wacc_ref[k])
            dg_ref[...] = _colsum(racc_ref[0])
            dlb_ref[...] = _colsum(racc_ref[1])
            db_ref[...] = _colsum(racc_ref[2])

    return pl.pallas_call(
        body, name=name, grid=(nt,), in_specs=[main(0), halo(0), main(1), halo(1), main(0), main(0), row, row],
        out_specs=[main(0), wsp, row, row, row],
        out_shape=[jax.ShapeDtypeStruct((t, d), F32), jax.ShapeDtypeStruct((CONV_WIDTH, d), F32)]
        + [jax.ShapeDtypeStruct((1, d), F32)] * 3,
        scratch_shapes=[pltpu.VMEM((tm + CONV_HALO, d), F32), pltpu.VMEM((CONV_WIDTH, 8, d), F32), pltpu.VMEM((3, 8, d), F32)],
        compiler_params=_params("arbitrary"),
    )(p, p, p, p, dc, ds, ln_g, ln_b)


def _conformer_glu_bwd(p, ddc, dw_w, name):
    t = p.shape[0]
    d = D_MODEL
    tm = _tile(t, (CONV_TM, 128))
    hb = tm // CONV_HALO
    nt = t // tm
    last_halo = t // CONV_HALO - 1
    rows, lanes = CONV_ROWS, CONV_LANES
    col = lambda c: pl.BlockSpec((tm, d), lambda i: (i, c))
    nxt = pl.BlockSpec((CONV_HALO, d), lambda i: (jnp.minimum((i + 1) * hb, last_halo), 0))

    def body(pa_ref, pg_ref, ddc_ref, ddcn_ref, w_ref, dp_ref, db_ref, scr_ref, acc_ref):
        i = pl.program_id(0)

        @pl.when(i == 0)
        def _():
            acc_ref[...] = jnp.zeros_like(acc_ref)

        scr_ref[0:tm, :] = ddc_ref[...]
        scr_ref[tm:, :] = _zero_if(i == nt - 1, ddcn_ref[...])
        for c in range(d // lanes):
            ls = slice(c * lanes, (c + 1) * lanes)
            gs = slice(d + c * lanes, d + (c + 1) * lanes)

            def taps(r, carry, ls=ls, gs=gs):
                r0 = pl.multiple_of(r * rows, rows)
                dglu = None
                for k, slab in _tap_slabs(scr_ref[pl.ds(r0, rows + CONV_HALO), ls], rows, True):
                    term = w_ref[k:k + 1, ls] * slab
                    dglu = term if dglu is None else dglu + term
                a = pa_ref[pl.ds(r0, rows), ls].astype(F32)
                sg = _sigmoid(pg_ref[pl.ds(r0, rows), ls].astype(F32))
                da = (dglu * sg).astype(BF16)
                dg = (dglu * a * sg * (1.0 - sg)).astype(BF16)
                dp_ref[pl.ds(r0, rows), ls] = da
                dp_ref[pl.ds(r0, rows), gs] = dg
                acc_ref[:, ls] += _fold8(da.astype(F32))
                acc_ref[:, gs] += _fold8(dg.astype(F32))
                return carry

            lax.fori_loop(0, tm // rows, taps, 0)

        @pl.when(i == nt - 1)
        def _():
            db_ref[...] = _colsum(acc_ref[...])

    return pl.pallas_call(
        body, name=name, grid=(nt,),
        in_specs=[col(0), col(1), col(0), nxt, pl.BlockSpec((CONV_WIDTH, d), lambda i: (0, 0))],
        out_specs=[pl.BlockSpec((tm, 2 * d), lambda i: (i, 0)), pl.BlockSpec((1, 2 * d), lambda i: (0, 0))],
        out_shape=[jax.ShapeDtypeStruct((t, 2 * d), BF16), jax.ShapeDtypeStruct((1, 2 * d), F32)],
        scratch_shapes=[pltpu.VMEM((tm + CONV_HALO, d), F32), pltpu.VMEM((8, 2 * d), F32)],
        compiler_params=_params("arbitrary"),
    )(p, p, ddc, ddc, dw_w)


def _colsum_call(a, name):
    t, n = a.shape
    tm = _tile(t)

    def body(a_ref, o_ref):
        @pl.when(pl.program_id(0) == 0)
        def _():
            o_ref[...] = jnp.zeros_like(o_ref)

        o_ref[...] += _colsum(a_ref[...].astype(F32))

    return pl.pallas_call(
        body, name=name, grid=(t // tm,), in_specs=[pl.BlockSpec((tm, n), lambda i: (i, 0))],
        out_specs=pl.BlockSpec((1, n), lambda i: (0, 0)), out_shape=jax.ShapeDtypeStruct((1, n), F32),
        compiler_params=_params("arbitrary"),
    )(a)


def _ada_fwd(c_all, w, name):
    rows, d = c_all.shape
    n = w.shape[1]
    tn = _tile(n, (256, 128))

    def body(c_ref, w_ref, o_ref):
        c = c_ref[...]
        o_ref[...] = _dot((c * _sigmoid(c)).astype(BF16), w_ref[...].astype(BF16), _NN)

    return pl.pallas_call(
        body, name=name, grid=(n // tn,),
        in_specs=[pl.BlockSpec((rows, d), lambda j: (0, 0)), pl.BlockSpec((d, tn), lambda j: (0, j))],
        out_specs=pl.BlockSpec((rows, tn), lambda j: (0, j)), out_shape=jax.ShapeDtypeStruct((rows, n), F32),
        compiler_params=_params("parallel"),
    )(c_all, w)


def _ada_bwd(c_all, dmod, name):
    rows, d = c_all.shape
    n = dmod.shape[1]
    tn = _tile(n, (256, 128))

    def body(c_ref, g_ref, o_ref):
        c = c_ref[...]
        o_ref[...] = _dot((c * _sigmoid(c)).astype(BF16), g_ref[...].astype(BF16), _TN)

    return pl.pallas_call(
        body, name=name, grid=(n // tn,),
        in_specs=[pl.BlockSpec((rows, d), lambda j: (0, 0)), pl.BlockSpec((rows, tn), lambda j: (0, j))],
        out_specs=pl.BlockSpec((d, tn), lambda j: (0, j)), out_shape=jax.ShapeDtypeStruct((d, n), F32),
        compiler_params=_params("parallel"),
    )(c_all, dmod)


def _sum_in_device_order(own, land, me, name):
    s, r, c = land.shape
    tr = _row_tile(r, 256)
    slot = lambda k: pl.BlockSpec((None, tr, c), lambda i, me_ref: (jnp.where(me_ref[0] == k, (k + 1) % s, k), i, 0))
    own_spec = pl.BlockSpec((tr, c), lambda i, me_ref: (i, 0))

    def body(me_ref, own_ref, *refs):
        o_ref = refs[-1]
        acc = None
        for k, ref in enumerate(refs[:-1]):
            term = jnp.where(me_ref[0] == k, own_ref[...], ref[...]).astype(F32)
            acc = term if acc is None else acc + term
        o_ref[...] = acc

    return pl.pallas_call(
        body, name=name, out_shape=jax.ShapeDtypeStruct((r, c), F32),
        grid_spec=pltpu.PrefetchScalarGridSpec(
            num_scalar_prefetch=1, grid=(r // tr,), in_specs=[own_spec] + [slot(k) for k in range(s)], out_specs=own_spec),
        compiler_params=_params("parallel"),
    )(me, own, *[land] * s)


def _sum_with_own(blocks, land, me, name):
    s, r, c = land.shape
    tr = _row_tile(r, 256)
    slot = lambda k: pl.BlockSpec((None, tr, c), lambda i, me_ref: ((me_ref[0] + k) % s, i, 0))

    def body(me_ref, own_ref, *refs):
        o_ref = refs[-1]
        acc = own_ref[...].astype(F32)
        for ref in refs[:-1]:
            acc = acc + ref[...].astype(F32)
        o_ref[...] = acc

    return pl.pallas_call(
        body, name=name, out_shape=jax.ShapeDtypeStruct((r, c), F32),
        grid_spec=pltpu.PrefetchScalarGridSpec(
            num_scalar_prefetch=1, grid=(r // tr,), in_specs=[slot(0)] + [slot(k) for k in range(1, s)],
            out_specs=pl.BlockSpec((tr, c), lambda i, me_ref: (i, 0))),
        compiler_params=_params("parallel"),
    )(me, blocks, *[land] * (s - 1))


def _adamw_update(w, g, m, v):
    nm = ADAM_B1 * m + (1.0 - ADAM_B1) * g
    nv = ADAM_B2 * v + (1.0 - ADAM_B2) * (g * g)
    m_hat = nm * (1.0 / (1.0 - ADAM_B1 ** ADAM_STEP))
    v_hat = nv * (1.0 / (1.0 - ADAM_B2 ** ADAM_STEP))
    return -ADAM_LR * (m_hat / (jnp.sqrt(v_hat) + ADAM_EPS) + ADAM_WD * w), nm, nv


def _adamw(w, g, m, v, behind, name):
    l, r, c = w.shape
    tr = _row_tile(r, 256)
    blk = pl.BlockSpec((None, tr, c), lambda k, i: (k, i, 0))
    order = [] if behind is None else [behind]

    def body(w_ref, g_ref, m_ref, v_ref, *rest):
        d_ref, nm_ref, nv_ref = rest[-3:]
        d_ref[...], nm_ref[...], nv_ref[...] = _adamw_update(w_ref[...], g_ref[...], m_ref[...], v_ref[...])

    return pl.pallas_call(
        body, name=name, grid=(l, r // tr), in_specs=[blk] * 4 + [pl.BlockSpec(memory_space=pl.ANY)] * len(order),
        out_specs=[blk] * 3, out_shape=[jax.ShapeDtypeStruct(w.shape, F32)] * 3,
        compiler_params=_params("parallel", "parallel"),
    )(w, g, m, v, *order)


def _adamw_small(ws, gs, ms, vs, name):
    n = len(ws)
    two_d = lambda a: a.reshape(-1, a.shape[-1])

    def body(*refs):
        ins, outs = refs[:4 * n], refs[4 * n:]
        for a in range(n):
            outs[a][...], outs[n + a][...], outs[2 * n + a][...] = _adamw_update(*[ins[k * n + a][...] for k in range(4)])

    res = pl.pallas_call(
        body, name=name, out_shape=[jax.ShapeDtypeStruct(two_d(w).shape, F32) for w in ws] * 3,
    )(*[two_d(a) for a in (*ws, *gs, *ms, *vs)])
    return [[res[k * n + a].reshape(ws[a].shape) for a in range(n)] for k in range(3)]


def _mesh_pos():
    return lax.axis_index("x"), lax.axis_index("y"), lax.axis_index("c")


def _all_gather_vmem(x_shard, name):
    m_per, n = x_shard.shape

    def body(x_ref, out_ref, send_sems, recv_sems, local_sem):
        x, y, c = _mesh_pos()
        me, sibling = (x, y, c), (x, y, 1 - c)
        chips = [(1 - x, y), (x, 1 - y), (1 - x, 1 - y)]

        def rows(px, py, pc):
            return out_ref.at[pl.ds((4 * px + 2 * py + pc) * m_per, m_per), :]

        def copy(k, block, to, src=None):
            return pltpu.make_async_remote_copy(
                src_ref=rows(*block) if src is None else src, dst_ref=rows(*block),
                send_sem=send_sems.at[k], recv_sem=recv_sems.at[k], device_id=to, device_id_type=MESH)

        mine = pltpu.make_async_copy(x_ref, rows(*me), local_sem)
        mine.start()
        first = [copy(0, me, sibling, src=x_ref)]
        first += [copy(1 + j, me, (*chip, c), src=x_ref) for j, chip in enumerate(chips)]
        for cp in first:
            cp.start()
        passed = [copy(4 + j, (*chip, c), sibling) for j, chip in enumerate(chips)]
        for j, chip in enumerate(chips):
            copy(1 + j, (*chip, c), me).wait_recv()
            passed[j].start()
        copy(0, sibling, me).wait_recv()
        for j, chip in enumerate(chips):
            copy(4 + j, (*chip, 1 - c), me).wait_recv()
        for cp in first + passed:
            cp.wait_send()
        mine.wait()

    return pl.pallas_call(
        body, name=name, out_shape=jax.ShapeDtypeStruct((N_DEV * m_per, n), x_shard.dtype),
        in_specs=[pl.BlockSpec(memory_space=pltpu.VMEM)], out_specs=pl.BlockSpec(memory_space=pltpu.VMEM),
        scratch_shapes=[pltpu.SemaphoreType.DMA((7,)), pltpu.SemaphoreType.DMA((7,)), pltpu.SemaphoreType.DMA],
    )(x_shard)


def _all_gather_hbm(shards, name):
    n = len(shards)
    out_shape = [jax.ShapeDtypeStruct((N_DEV,) + s.shape, s.dtype) for s in shards]

    def body(*refs):
        x_refs, out_refs = refs[:n], refs[n:2 * n]
        send_sems, recv_sems, local_sems = refs[2 * n:]
        x, y, c = _mesh_pos()
        me, sibling = (x, y, c), (x, y, 1 - c)
        chips = [(1 - x, y), (x, 1 - y), (1 - x, 1 - y)]

        def blk(a, p):
            return out_refs[a].at[4 * p[0] + 2 * p[1] + p[2]]

        def copy(a, k, block, to, src=None):
            return pltpu.make_async_remote_copy(
                src_ref=blk(a, block) if src is None else src, dst_ref=blk(a, block),
                send_sem=send_sems.at[7 * a + k], recv_sem=recv_sems.at[7 * a + k], device_id=to, device_id_type=MESH)

        mine = [pltpu.make_async_copy(x_refs[a], blk(a, me), local_sems.at[a]) for a in range(n)]
        for cp in mine:
            cp.start()
        first = []
        for a in range(n):
            first.append(copy(a, 0, me, sibling, src=x_refs[a]))
            first += [copy(a, 1 + j, me, (*chip, c), src=x_refs[a]) for j, chip in enumerate(chips)]
        for cp in first:
            cp.start()
        passed = []
        for j, chip in enumerate(chips):
            for a in range(n):
                copy(a, 1 + j, (*chip, c), me).wait_recv()
                fwd = copy(a, 4 + j, (*chip, c), sibling)
                fwd.start()
                passed.append(fwd)
        for a in range(n):
            copy(a, 0, sibling, me).wait_recv()
            for j, chip in enumerate(chips):
                copy(a, 4 + j, (*chip, 1 - c), me).wait_recv()
        for cp in first + passed:
            cp.wait_send()
        for cp in mine:
            cp.wait()

    return pl.pallas_call(
        body, name=name, out_shape=out_shape, in_specs=[pl.BlockSpec(memory_space=pltpu.VMEM)] * n,
        out_specs=[pl.BlockSpec(memory_space=pl.ANY)] * n,
        scratch_shapes=[pltpu.SemaphoreType.DMA((7 * n,)), pltpu.SemaphoreType.DMA((7 * n,)), pltpu.SemaphoreType.DMA((n,))],
    )(*shards)


def _peers(x, y, c):
    flip = lambda v, f: 1 - v if f else v
    return [(flip(x, m & 4), flip(y, m & 2), flip(c, m & 1)) for m in range(1, N_DEV)]


def _dev_index(p):
    return 4 * p[0] + 2 * p[1] + p[2]


def _push_copies(src_refs, land_refs, send_sems, recv_sems, scatter, receive):
    x, y, c = _mesh_pos()
    me = _dev_index((x, y, c))
    copies = []
    for a, (src, land) in enumerate(zip(src_refs, land_refs)):
        for k, p in enumerate(_peers(x, y, c)):
            copies.append(pltpu.make_async_remote_copy(
                src_ref=src.at[_dev_index(p)] if scatter else src, dst_ref=land.at[_dev_index(p) if receive else me],
                send_sem=send_sems.at[7 * a + k], recv_sem=recv_sems.at[7 * a + k], device_id=p, device_id_type=MESH))
    return copies


_HBM = pl.BlockSpec(memory_space=pltpu.HBM)
_SEM = pl.BlockSpec(memory_space=pltpu.SEMAPHORE)
_EFFECT = pltpu.SideEffectType.DATAFLOW_SIDE_EFFECTING


def _pushes_start(srcs, lands, scatter, name):
    n = len(srcs)

    def body(*refs):
        src_refs, land_refs = refs[:n], refs[n:2 * n]
        send_sems, recv_sems = refs[2 * n], refs[2 * n + 1]
        token = refs[-1]
        for cp in _push_copies(src_refs, land_refs, send_sems, recv_sems, scatter, receive=False):
            cp.start()
        token[...] = jnp.zeros_like(token)

    hbm = lambda a: pltpu.HBM(a.shape, a.dtype)
    sems = pltpu.SemaphoreType.DMA((7 * n,))
    outs = pl.pallas_call(
        body, name=name,
        out_shape=(sems, sems, *[hbm(a) for a in srcs], *[hbm(a) for a in lands], jax.ShapeDtypeStruct((8, 128), F32)),
        in_specs=[_HBM] * (2 * n), out_specs=(_SEM, _SEM, *[_HBM] * (2 * n), pl.BlockSpec(memory_space=pltpu.VMEM)),
        input_output_aliases={i: 2 + i for i in range(2 * n)},
        compiler_params=pltpu.CompilerParams(has_side_effects=_EFFECT),
    )(*[pltpu.with_memory_space_constraint(a, pltpu.HBM) for a in (*srcs, *lands)])
    return (outs[0], outs[1], outs[2:2 + n], outs[2 + n:2 + 2 * n], scatter), outs[-1]


def _pushes_wait(handle, after, name):
    send_sems, recv_sems, srcs, lands, scatter = handle
    n = len(srcs)
    after = after if isinstance(after, (tuple, list)) else (after,)

    def body(*refs):
        src_refs, land_refs = refs[:n], refs[n:2 * n]
        for cp in _push_copies(src_refs, land_refs, refs[2 * n], refs[2 * n + 1], scatter, receive=True):
            cp.wait_send()
            cp.wait_recv()

    hbm = lambda a: pltpu.HBM(a.shape, a.dtype)
    outs = pl.pallas_call(
        body, name=name, out_shape=tuple(hbm(a) for a in (*srcs, *lands)),
        in_specs=[_HBM] * (2 * n) + [_SEM, _SEM] + [pl.BlockSpec(memory_space=pl.ANY)] * len(after),
        out_specs=tuple([_HBM] * (2 * n)), input_output_aliases={i: i for i in range(2 * n)},
        compiler_params=pltpu.CompilerParams(has_side_effects=_EFFECT),
    )(*srcs, *lands, send_sems, recv_sems, *after)
    return outs[:n], outs[n:]


def _landing_zones(srcs, behind, name):
    n, nb = len(srcs), len(behind)

    def body(*refs):
        src_refs, land_refs, bufs, sems = refs[:n], refs[n + nb:2 * n + nb], refs[2 * n + nb:3 * n + nb], refs[3 * n + nb]
        me = _dev_index(_mesh_pos())
        load = [pltpu.make_async_copy(src, buf, sems.at[a]) for a, (src, buf) in enumerate(zip(src_refs, bufs))]
        store = [pltpu.make_async_copy(buf, land.at[me], sems.at[a]) for a, (buf, land) in enumerate(zip(bufs, land_refs))]
        for cp in load:
            cp.start()
        for ld, st in zip(load, store):
            ld.wait()
            st.start()
        for cp in store:
            cp.wait()

    any_spec = pl.BlockSpec(memory_space=pl.ANY)
    return pl.pallas_call(
        body, name=name, out_shape=[jax.ShapeDtypeStruct((N_DEV,) + s.shape, s.dtype) for s in srcs],
        in_specs=[any_spec] * (n + nb), out_specs=[any_spec] * n,
        scratch_shapes=[pltpu.VMEM(s.shape, s.dtype) for s in srcs] + [pltpu.SemaphoreType.DMA((n,))],
        compiler_params=pltpu.CompilerParams(vmem_limit_bytes=V7X_VMEM_LIMIT),
    )(*srcs, *behind)


def _ffn_forward(x, mod, norm_g, w, tag):
    sh, sc, gate = mod
    h = _modnorm(x, norm_g, sc, sh, f"{tag}_norm")
    u = _ffn_up(h, w["up_t"], f"{tag}_up")
    act, z = _ffn_act(u, w["dw_w"], w["dw_b"], f"{tag}_act")
    y, x_new = _matmul(act, w["down"], "nn", BF16, f"{tag}_down", resid=(x, gate))
    return x_new, (x, h, u, z, act, y)


def _behind(row, token):
    return row if token is None else row + token[0:1, 0:1]


def _ffn_backward(dx_new, dy, d_gate, saved, mod, norm_g, w, tag, emit, below):
    x, h, u, z, act, _ = saved
    _, sc, _ = mod
    d_down = _matmul_tn_acc(act, dy, f"{tag}_down_dw")
    dact = _matmul(dy, w["down"], "nt", BF16, f"{tag}_down_dx")
    du, d_dw_w, d_dw_b = _ffn_act_bwd(u, z, dact, w["dw_w"], f"{tag}_act_bwd")
    d_up_t = _matmul_tn_acc(du, h, f"{tag}_up_dw").reshape(2 * FFN_DIM, -1)
    token = emit([d_up_t, d_down])
    dh = _ffn_up_dx(du, w["up_t"], f"{tag}_up_dx")
    dx, d_w, d_sh, *dy_below = _modnorm_bwd(x, dh, norm_g, _behind(sc, token), dx_new, below, f"{tag}_norm_bwd")
    return (dx, *dy_below), dict(dw_w=d_dw_w.transpose(1, 0, 2).reshape(FFN_CONV_WIDTH, 2 * FFN_DIM),
                    dw_b=d_dw_b.reshape(1, 2 * FFN_DIM), norm_g=d_w * (1.0 + sc), sh=d_sh, sc=d_w * norm_g, gate=d_gate)


def _mixer_forward(x, mod, norm_g, w, rope, tag):
    sh, sc, gate = mod
    h = _modnorm(x, norm_g, sc, sh, f"{tag}_norm")
    z = _matmul(h, w["w_in_t"], "nt", BF16, f"{tag}_in")
    ya = _gmlp_fwd(z, w["gain"], w["wtril"], w["bias_exp"], f"{tag}_gmlp")
    q, k, v = _qk_prep(z, rope[0], rope[1], w["gq"], w["gk"], w["seg"], f"{tag}_qk")
    outs, lses = zip(*[_attn_fwd(q[b], k[b], v[b], dil, f"{tag}_attn_d{dil}") for b, dil in enumerate(DILATIONS)])
    yb, lse, cat = _attn_merge(outs, lses, ya, f"{tag}_merge")
    y, x_new = _matmul(cat, w["w_out"], "nn", BF16, f"{tag}_out", resid=(x, gate))
    return x_new, (x, h, z, q, k, v, yb, lse, cat, y)


def _mixer_backward(dx_new, dy, d_gate, saved, mod, norm_g, w, rope, tag, emit, below):
    x, h, z, q, k, v, yb, lse, cat, _ = saved
    _, sc, _ = mod
    d_w_out = _matmul_tn_acc(cat, dy, f"{tag}_out_dw")
    dcat = _matmul(dy, w["w_out"], "nt", BF16, f"{tag}_out_dx")
    dz_a, d_sp_w, d_gain, d_bias_exp = _gmlp_bwd(z, dcat, w["gain"], w["wtril"], w["wtril_t"], w["bias_exp"], f"{tag}_gmlp_bwd")
    dyb = _subseq_views(dcat, A_WIDTH // B_WIDTH, f"{tag}_dyb_views")
    dqs, dks, dvs = zip(*[_attn_bwd(q[b], k[b], v[b], dyb[b], yb[b], lse[b], dil, f"{tag}_attn_bwd_d{dil}")
                          for b, dil in enumerate(DILATIONS)])
    dz_qkv, d_gq, d_gk = _qk_prep_bwd(z, dqs, dks, dvs, rope[0], rope[1], w["gq"], w["gk"], w["seg"], f"{tag}_qk_bwd")
    dz = jnp.concatenate([dz_a, dz_qkv], axis=1)
    d_w_in_t = _matmul_tn_acc(dz, h, f"{tag}_in_dw")
    token = emit([d_w_in_t, d_w_out])
    dh = _matmul(dz, w["w_in_t"], "nn", F32, f"{tag}_in_dx")
    dx, d_w, d_sh, *dy_below = _modnorm_bwd(x, dh, norm_g, _behind(sc, token), dx_new, below, f"{tag}_norm_bwd")
    return (dx, *dy_below), dict(
        vnorm_g=d_gain.reshape(A_GROUPS, GROUP_DIM), spatial_w=d_sp_w,
        spatial_b=d_bias_exp.reshape(CHUNK, A_GROUPS, GROUP_DIM).sum(-1).T,
        q_norm_g=d_gq.reshape(HEADS, HEAD_DIM).sum(0), k_norm_g=d_gk.reshape(HEADS, HEAD_DIM).sum(0),
        norm_g=d_w * (1.0 + sc), sh=d_sh, sc=d_w * norm_g, gate=d_gate)


def _conformer_forward(x, mod, norm_g, w, tag):
    sh, sc, gate = mod
    h = _modnorm(x, norm_g, sc, sh, f"{tag}_norm")
    p = _matmul(h, w["pw1_t"], "nt", BF16, f"{tag}_pw1", bias=w["pw1_b"])
    s, dc = _conformer_mid(p, w["dw_w"], w["dw_b"], w["ln_g"], w["ln_b"], f"{tag}_mid")
    y, x_new = _matmul(s, w["pw2"], "nn", BF16, f"{tag}_pw2", bias=w["pw2_b"], resid=(x, gate))
    return x_new, (x, h, p, dc, s, y)


def _conformer_backward(dx_new, dy, d_gate, saved, mod, norm_g, w, tag, emit, below):
    x, h, p, dc, s, _ = saved
    _, sc, _ = mod
    d_pw2 = _matmul_tn_acc(s, dy, f"{tag}_pw2_dw")
    d_pw2_b = _colsum_call(dy, f"{tag}_pw2_db")
    ds = _matmul(dy, w["pw2"], "nt", BF16, f"{tag}_pw2_dx")
    ddc, d_dw_w, d_dw_b, d_ln_g, d_ln_b = _conformer_mid_bwd(p, dc, ds, w["ln_g"], w["ln_b"], f"{tag}_mid_bwd")
    dp, d_pw1_b = _conformer_glu_bwd(p, ddc, w["dw_w"], f"{tag}_glu_bwd")
    d_pw1_t = _matmul_tn_acc(dp, h, f"{tag}_pw1_dw")
    token = emit([d_pw1_t, d_pw2])
    dh = _matmul(dp, w["pw1_t"], "nn", F32, f"{tag}_pw1_dx")
    dx, d_w, d_sh, *dy_below = _modnorm_bwd(x, dh, norm_g, _behind(sc, token), dx_new, below, f"{tag}_norm_bwd")
    return (dx, *dy_below), dict(pw1_b=d_pw1_b, dw_w=d_dw_w, dw_b=d_dw_b, ln_g=d_ln_g, ln_b=d_ln_b, pw2_b=d_pw2_b, norm_g=d_w * (1.0 + sc), sh=d_sh, sc=d_w * norm_g, gate=d_gate)


def _local_step(x, target, pos, mod, norm_mix_g, norm_ffn_g, mixer_w, conv_w, ffn_w, fetch, emit):
    d = D_MODEL
    inv_freq = 1.0 / (ROPE_THETA ** (jnp.arange(0, HEAD_DIM, 2, dtype=F32) / HEAD_DIM))
    inv_freq = jnp.tile(inv_freq, 2 * HEADS)[None, :]
    sign = jnp.tile(jnp.concatenate([-jnp.ones(HEAD_DIM // 2, F32), jnp.ones(HEAD_DIM // 2, F32)]), HEADS)[None, :]
    rope = _rope_tables(pos, inv_freq, sign, "rope_tables")
    mods = [[mod[l:l + 1, i * d:(i + 1) * d] for i in range(6)] for l in range(2)]
    mix = [(m[0], m[1], m[2]) for m in mods]
    ffn = [(m[3], m[4], m[5]) for m in mods]
    gm = [norm_mix_g[l:l + 1] for l in range(2)]
    gf = [norm_ffn_g[l:l + 1] for l in range(2)]

    mixer_w = {**mixer_w, **fetch("l0_mix", x)}
    x1, s_mix = _mixer_forward(x, mix[0], gm[0], mixer_w, rope, "l0_mix")
    ffn_w0 = {**ffn_w[0], **fetch("l0_ffn", x1)}
    x2, s_ffn0 = _ffn_forward(x1, ffn[0], gf[0], ffn_w0, "l0_ffn")
    conv_w = {**conv_w, **fetch("l1_conv", x2)}
    x3, s_conv = _conformer_forward(x2, mix[1], gm[1], conv_w, "l1_conv")
    ffn_w1 = {**ffn_w[1], **fetch("l1_ffn", x3)}
    x4, s_ffn1 = _ffn_forward(x3, ffn[1], gf[1], ffn_w1, "l1_ffn")
    below = lambda saved, m: (saved[-1], m[2])
    dx, loss, dy, dg = _loss_head(x4, target, below(s_ffn1, ffn[1]), "loss_head")
    (dx, dy, dg), g_ffn1 = _ffn_backward(dx, dy, dg, s_ffn1, ffn[1], gf[1], ffn_w1, "l1_ffn",
                                         functools.partial(emit, "l1_ffn"), below(s_conv, mix[1]))
    (dx, dy, dg), g_conv = _conformer_backward(dx, dy, dg, s_conv, mix[1], gm[1], conv_w, "l1_conv",
                                               functools.partial(emit, "l1_conv"), below(s_ffn0, ffn[0]))
    (dx, dy, dg), g_ffn0 = _ffn_backward(dx, dy, dg, s_ffn0, ffn[0], gf[0], ffn_w0, "l0_ffn",
                                         functools.partial(emit, "l0_ffn"), below(s_mix, mix[0]))
    (dx,), g_mix = _mixer_backward(dx, dy, dg, s_mix, mix[0], gm[0], mixer_w, rope, "l0_mix",
                                   functools.partial(emit, "l0_mix"), None)
    blocks = [g_mix, g_ffn0, g_conv, g_ffn1]
    dmod = jnp.stack([jnp.concatenate([a["sh"], a["sc"], a["gate"], b["sh"], b["sc"], b["gate"]], axis=1)[0]
                      for a, b in ((g_mix, g_ffn0), (g_conv, g_ffn1))])
    return loss, dx, dmod, blocks


def _pack(arrs, rows=8):
    flat = jnp.concatenate([a.reshape(-1).astype(F32) for a in arrs])
    n = flat.shape[0]
    cols = -(-n // (rows * 128)) * 128
    return jnp.pad(flat, (0, rows * cols - n)).reshape(rows, cols)


def _unpack(flat, shapes):
    out, off = [], 0
    for shp in shapes:
        n = math.prod(shp)
        out.append(flat[..., off:off + n].reshape(flat.shape[:-1] + tuple(shp)))
        off += n
    return out


def _take_block(a, idx, size, axis):
    return lax.dynamic_slice_in_dim(a, idx * size, size, axis)


def kernel(x, c, positions, ada_w, ada_b, norm_mix_g, norm_ffn_g, ab_w_in, a_vnorm_g, a_spatial_w, a_spatial_b, b_q_norm_g, b_k_norm_g, ab_w_out, conv_pw1_w, conv_pw1_b, conv_dw_w, conv_dw_b, conv_ln_g, conv_ln_b, conv_pw2_w, conv_pw2_b, ffn_up_w, ffn_dw_w, ffn_dw_b, ffn_down_w, loss_target, m_ada_w, m_ada_b, m_norm_mix_g, m_norm_ffn_g, m_ab_w_in, m_a_vnorm_g, m_a_spatial_w, m_a_spatial_b, m_b_q_norm_g, m_b_k_norm_g, m_ab_w_out, m_conv_pw1_w, m_conv_pw1_b, m_conv_dw_w, m_conv_dw_b, m_conv_ln_g, m_conv_ln_b, m_conv_pw2_w, m_conv_pw2_b, m_ffn_up_w, m_ffn_dw_w, m_ffn_dw_b, m_ffn_down_w, v_ada_w, v_ada_b, v_norm_mix_g, v_norm_ffn_g, v_ab_w_in, v_a_vnorm_g, v_a_spatial_w, v_a_spatial_b, v_b_q_norm_g, v_b_k_norm_g, v_ab_w_out, v_conv_pw1_w, v_conv_pw1_b, v_conv_dw_w, v_conv_dw_b, v_conv_ln_g, v_conv_ln_b, v_conv_pw2_w, v_conv_pw2_b, v_ffn_up_w, v_ffn_dw_w, v_ffn_dw_b, v_ffn_down_w):
    weights = dict(ada_w=ada_w, ada_b=ada_b, norm_mix_g=norm_mix_g, norm_ffn_g=norm_ffn_g, ab_w_in=ab_w_in, a_vnorm_g=a_vnorm_g, a_spatial_w=a_spatial_w, a_spatial_b=a_spatial_b, b_q_norm_g=b_q_norm_g, b_k_norm_g=b_k_norm_g, ab_w_out=ab_w_out, conv_pw1_w=conv_pw1_w, conv_pw1_b=conv_pw1_b, conv_dw_w=conv_dw_w, conv_dw_b=conv_dw_b, conv_ln_g=conv_ln_g, conv_ln_b=conv_ln_b, conv_pw2_w=conv_pw2_w, conv_pw2_b=conv_pw2_b, ffn_up_w=ffn_up_w, ffn_dw_w=ffn_dw_w, ffn_dw_b=ffn_dw_b, ffn_down_w=ffn_down_w)
    mom1 = dict(ada_w=m_ada_w, ada_b=m_ada_b, norm_mix_g=m_norm_mix_g, norm_ffn_g=m_norm_ffn_g, ab_w_in=m_ab_w_in, a_vnorm_g=m_a_vnorm_g, a_spatial_w=m_a_spatial_w, a_spatial_b=m_a_spatial_b, b_q_norm_g=m_b_q_norm_g, b_k_norm_g=m_b_k_norm_g, ab_w_out=m_ab_w_out, conv_pw1_w=m_conv_pw1_w, conv_pw1_b=m_conv_pw1_b, conv_dw_w=m_conv_dw_w, conv_dw_b=m_conv_dw_b, conv_ln_g=m_conv_ln_g, conv_ln_b=m_conv_ln_b, conv_pw2_w=m_conv_pw2_w, conv_pw2_b=m_conv_pw2_b, ffn_up_w=m_ffn_up_w, ffn_dw_w=m_ffn_dw_w, ffn_dw_b=m_ffn_dw_b, ffn_down_w=m_ffn_down_w)
    mom2 = dict(ada_w=v_ada_w, ada_b=v_ada_b, norm_mix_g=v_norm_mix_g, norm_ffn_g=v_norm_ffn_g, ab_w_in=v_ab_w_in, a_vnorm_g=v_a_vnorm_g, a_spatial_w=v_a_spatial_w, a_spatial_b=v_a_spatial_b, b_q_norm_g=v_b_q_norm_g, b_k_norm_g=v_b_k_norm_g, ab_w_out=v_ab_w_out, conv_pw1_w=v_conv_pw1_w, conv_pw1_b=v_conv_pw1_b, conv_dw_w=v_conv_dw_w, conv_dw_b=v_conv_dw_b, conv_ln_g=v_conv_ln_g, conv_ln_b=v_conv_ln_b, conv_pw2_w=v_conv_pw2_w, conv_pw2_b=v_conv_pw2_b, ffn_up_w=v_ffn_up_w, ffn_dw_w=v_ffn_dw_w, ffn_dw_b=v_ffn_dw_b, ffn_down_w=v_ffn_down_w)
    order = list(weights)
    d, f2 = D_MODEL, 2 * FFN_DIM
    t = x.shape[1]
    me = 4 * lax.axis_index("x") + 2 * lax.axis_index("y") + lax.axis_index("c")
    for window, dil in PATTERNS:
        assert window // dil == Q_BLOCK and t % (dil * Q_BLOCK) == 0

    small_in = [c[0], conv_pw1_b[0], conv_dw_w[0], conv_dw_b[0], conv_ln_g[0], conv_ln_b[0], conv_pw2_b[0], ffn_dw_w]
    g1 = _all_gather_vmem(_pack(small_in, rows=8), "gather_small").reshape(N_DEV, -1)
    c_all, pw1_b, dw_w, dw_b, ln_g, ln_b, pw2_b, fdw_w = _unpack(g1, [a.shape for a in small_in])
    pw1_b, dw_b, ln_g, ln_b, pw2_b = [a.reshape(1, -1) for a in (pw1_b, dw_b, ln_g, ln_b, pw2_b)]
    dw_w = dw_w.transpose(1, 0, 2).reshape(CONV_WIDTH, d)
    fdw_w = fdw_w.transpose(1, 2, 0, 3).reshape(2, FFN_CONV_WIDTH, f2)

    c16 = jnp.pad(c_all, ((0, 2 * N_DEV - c_all.shape[0]), (0, 0)))
    part = jnp.concatenate([_ada_fwd(c16, ada_w[l], f"ada_fwd{l}")[:N_DEV] for l in range(2)], axis=1)
    g2 = _all_gather_vmem(part, "gather_mod").reshape(N_DEV, N_DEV, 2, -1)
    mod = lax.dynamic_index_in_dim(g2, me, axis=1, keepdims=False).transpose(1, 0, 2).reshape(2, 6 * d) + ada_b

    stages = dict(l0_mix=[ab_w_in[0].T, ab_w_out[0]], l0_ffn=[ffn_up_w[0].T, ffn_down_w[0]],
                  l1_conv=[conv_pw1_w[0].T, conv_pw2_w[0]], l1_ffn=[ffn_up_w[1].T, ffn_down_w[1]])
    stages = {k: [s.astype(BF16) for s in v] for k, v in stages.items()}
    names = dict(l0_mix=("w_in_t", "w_out"), l0_ffn=("up_t", "down"), l1_conv=("pw1_t", "pw2"), l1_ffn=("up_t", "down"))
    ready = {"l0_mix": [a.reshape(-1, d) for a in _all_gather_hbm(stages["l0_mix"], "gather_mixer_weights")]}
    behind = (*ready["l0_mix"], mod)
    arriving = {}
    for stage, group in (("l0_ffn", ("l0_ffn",)), ("l1_conv", ("l1_conv", "l1_ffn"))):
        srcs = [s for g in group for s in stages[g]]
        arriving[stage], token = _pushes_start(
            srcs, _landing_zones(srcs, behind, f"gather_{stage}_zones"), False, f"gather_{stage}_start")
        behind = (token,)
        mod = mod + token[0:1, 0:1]

    def fetch(stage, after):
        if stage in arriving:
            full = [a.reshape(-1, d) for a in _pushes_wait(arriving[stage], after, f"gather_{stage}_wait")[1]]
            ready[stage] = full[:2]
            if stage == "l1_conv":
                ready["l1_ffn"] = full[2:]
        return dict(zip(names[stage], ready[stage]))

    causal = jnp.tril(jnp.ones((CHUNK, CHUNK), bool))
    wtril = jnp.where(causal[None], a_spatial_w[0], 0.0)
    mixer_w = dict(
        gain=a_vnorm_g[0].reshape(1, A_WIDTH), wtril=wtril.astype(BF16),
        wtril_t=wtril.transpose(0, 2, 1).astype(BF16),
        bias_exp=jnp.repeat(a_spatial_b[0].T, GROUP_DIM, axis=1),
        gq=jnp.tile(b_q_norm_g[0], HEADS)[None, :], gk=jnp.tile(b_k_norm_g[0], HEADS)[None, :],
        seg=jnp.kron(jnp.eye(HEADS, dtype=BF16), jnp.ones((HEAD_DIM, HEAD_DIM), BF16)))
    conv_w = dict(pw1_b=pw1_b, dw_w=dw_w, dw_b=dw_b, ln_g=ln_g, ln_b=ln_b, pw2_b=pw2_b)
    ffn_w = [dict(dw_w=fdw_w[l].reshape(FFN_CONV_WIDTH, 2, FFN_DIM).transpose(1, 0, 2), dw_b=ffn_dw_b[l].reshape(2, 1, FFN_DIM))
             for l in range(2)]

    leaving = {}

    def emit(stage, grads):
        blocks = [g.reshape(N_DEV, g.shape[0] // N_DEV, d) for g in grads]
        leaving[stage], token = _pushes_start(
            blocks, [lax.empty(b.shape, b.dtype) for b in blocks], True, f"reduce_{stage}_start")
        return token

    loss, dx, dmod, (g_mix, g_ffn0, g_conv, g_ffn1) = _local_step(
        x[0], loss_target[0], positions[0].astype(F32)[:, None], mod, norm_mix_g, norm_ffn_g, mixer_w, conv_w, ffn_w,
        fetch, emit)

    me_op = me.astype(jnp.int32).reshape(1)

    def reduced(stage, after):
        blocks, lands = _pushes_wait(leaving[stage], after, f"reduce_{stage}_wait")
        return [_sum_with_own(b, a, me_op, f"reduce_{stage}_sum{i}") for i, (b, a) in enumerate(zip(blocks, lands))]

    (r_up_t1, r_down1), (r_pw1_t, r_pw2), (r_up_t0, r_down0) = [reduced(s, dx) for s in ("l1_ffn", "l1_conv", "l0_ffn")]

    small_g = [
        dmod, jnp.concatenate([g_mix["norm_g"], g_conv["norm_g"]]), jnp.concatenate([g_ffn0["norm_g"], g_ffn1["norm_g"]]),
        g_mix["vnorm_g"], g_mix["spatial_w"], g_mix["spatial_b"], g_mix["q_norm_g"], g_mix["k_norm_g"],
        g_conv["pw1_b"], g_conv["dw_w"], g_conv["dw_b"], g_conv["ln_g"], g_conv["ln_b"], g_conv["pw2_b"],
        jnp.stack([g_ffn0["dw_w"], g_ffn1["dw_w"]]), jnp.concatenate([g_ffn0["dw_b"], g_ffn1["dw_b"]])]
    packed = _pack(small_g, rows=8)
    small_leaving, token = _pushes_start([packed], [lax.empty((N_DEV,) + packed.shape, F32)], False, "gather_small_grads_start")

    grads = dict(conv_pw2_w=r_pw2[None], ffn_down_w=jnp.stack([r_down0, r_down1]))
    grads_t = dict(conv_pw1_w=r_pw1_t[None], ffn_up_w=jnp.stack([r_up_t0, r_up_t1]))
    flip = lambda a: jnp.swapaxes(a, 1, 2)
    delta, new_m, new_v = {}, {}, {}

    def update(name, behind):
        if name in grads_t:
            grads[name] = flip(grads_t[name])
            res = _adamw(flip(weights[name]), grads_t[name], flip(mom1[name]), flip(mom2[name]), behind, f"adamw_{name}")
            delta[name], new_m[name], new_v[name] = [flip(r) for r in res]
        else:
            delta[name], new_m[name], new_v[name] = _adamw(
                weights[name], grads[name], mom1[name], mom2[name], behind, f"adamw_{name}")

    for name in ("conv_pw1_w", "conv_pw2_w", "ffn_up_w", "ffn_down_w"):
        update(name, token)
    r_in_t, r_out = reduced("l0_mix", new_v["ffn_down_w"])
    grads_t["ab_w_in"], grads["ab_w_out"] = r_in_t[None], r_out[None]
    update("ab_w_in", token)
    update("ab_w_out", token)

    (packed,), (landed,) = _pushes_wait(small_leaving, tuple(new_v.values()), "gather_small_grads_wait")
    total = _sum_in_device_order(packed, landed, me_op, "sum_small_grads")
    (s_dmod, s_mix_g, s_ffn_g, s_vnorm, s_sp_w, s_sp_b, s_gq, s_gk, s_pw1_b, s_dw_w, s_dw_b, s_ln_g, s_ln_b,
     s_pw2_b, s_fdw_w, s_fdw_b) = _unpack(total.reshape(-1), [a.shape for a in small_g])
    dmod_all = lax.dynamic_update_slice(
        landed.reshape(N_DEV, -1)[:, :dmod.size].reshape((N_DEV,) + dmod.shape), dmod[None], (me, 0, 0))
    n_ada = ada_w.shape[2]
    dmod16 = jnp.pad(_take_block(dmod_all, me, n_ada, 2), ((0, N_DEV), (0, 0), (0, 0)))
    grads.update(
        ada_w=jnp.stack([_ada_bwd(c16, dmod16[:, l], f"ada_bwd{l}") for l in range(2)]),
        ada_b=s_dmod, norm_mix_g=s_mix_g, norm_ffn_g=s_ffn_g,
        a_vnorm_g=s_vnorm[None], a_spatial_w=s_sp_w[None], a_spatial_b=s_sp_b[None], b_q_norm_g=s_gq[None],
        b_k_norm_g=s_gk[None],
        conv_pw1_b=_take_block(s_pw1_b, me, conv_pw1_b.shape[1], 1),
        conv_dw_w=_take_block(s_dw_w, me, conv_dw_w.shape[2], 1)[None],
        conv_dw_b=_take_block(s_dw_b, me, conv_dw_b.shape[1], 1), conv_ln_g=_take_block(s_ln_g, me, conv_ln_g.shape[1], 1),
        conv_ln_b=_take_block(s_ln_b, me, conv_ln_b.shape[1], 1),
        conv_pw2_b=_take_block(s_pw2_b, me, conv_pw2_b.shape[1], 1),
        ffn_dw_w=_take_block(s_fdw_w, me, ffn_dw_w.shape[2], 2), ffn_dw_b=s_fdw_b)
    update("ada_w", None)
    large = ("ada_w", "conv_pw1_w", "conv_pw2_w", "ffn_up_w", "ffn_down_w", "ab_w_in", "ab_w_out")
    small = [n for n in order if n not in large]
    res = _adamw_small(*[[src[n] for n in small] for src in (weights, grads, mom1, mom2)], "adamw_small")
    for dst, arrs in zip((delta, new_m, new_v), res):
        dst.update(zip(small, arrs))

    loss = lax.psum(loss[0, 0], ("x", "y", "c"))
    return (loss, dx[None], *[grads[n] for n in order], *[delta[n] for n in order],
            *[new_m[n] for n in order], *[new_v[n] for n in order])
```

```python
import functools
import math

import jax
import jax.numpy as jnp
from jax import lax
from jax.experimental import pallas as pl
from jax.experimental.pallas import tpu as pltpu

F32 = jnp.float32
BF16 = jnp.bfloat16
MESH = pl.DeviceIdType.MESH

D_MODEL = 1024
A_WIDTH = 512
A_GROUPS = 4
GROUP_DIM = 128
CHUNK = 128
B_WIDTH = 512
HEADS = 8
HEAD_DIM = 64
PATTERNS = ((128, 1), (512, 4), (2048, 16))
Q_BLOCK = 128
ROPE_THETA = 10000.0
AB_IN = 2560
CONV_WIDTH = 31
FFN_DIM = 2816
FFN_CONV_WIDTH = 3
EPS = 1e-6
NEG = -1e30
N_DEV = 8
ADAM_LR, ADAM_B1, ADAM_B2, ADAM_EPS, ADAM_WD, ADAM_STEP = 0.001, 0.9, 0.999, 1e-08, 0.01, 10

V7X_VMEM_LIMIT = 56 * 2**20
FFN_HALO = 16
CONV_HALO = 32

_NN = (((1,), (0,)), ((), ()))
_NT = (((1,), (1,)), ((), ()))
_TN = (((0,), (0,)), ((), ()))


def _tile(n, prefs=(512, 256, 128)):
    for t in prefs:
        if n % t == 0:
            return t
    return n


def _row_tile(n, cap=512):
    best = n
    for t in range(8, min(n, cap) + 1, 8):
        if n % t == 0:
            best = t
    return best if best <= cap else n


def _params(*sem):
    return pltpu.CompilerParams(dimension_semantics=sem, vmem_limit_bytes=V7X_VMEM_LIMIT)


def _dot(a, b, dims):
    return lax.dot_general(a, b, dims, preferred_element_type=F32)


def _sigmoid(x):
    return 1.0 / (1.0 + jnp.exp(-x))


def _gelu(x):
    return 0.5 * x * (1.0 + lax.erf(x * (2.0 ** -0.5)))


def _gelu_grad(x):
    return 0.5 * (1.0 + lax.erf(x * (2.0 ** -0.5))) + x * jnp.exp(-0.5 * x * x) * (1.0 / math.sqrt(2.0 * math.pi))


def _colsum(v):
    return jnp.sum(v, axis=0, keepdims=True)


MATMUL_VMEM_BUDGET = 40 * 2**20


def _matmul_tiles(m, n, k, out_bytes, with_resid):
    def options(dim):
        opts = [t for t in (1024, 512, 256, 128) if dim % t == 0]
        return opts + [dim] if dim <= 4096 and dim not in opts else opts

    best = None
    for tm in options(m):
        for tn in options(n):
            need = 4 * (tm * k + k * tn) + tm * tn * (4 + 2 * out_bytes) + (24 * tm * tn if with_resid else 0)
            if need <= MATMUL_VMEM_BUDGET and (best is None or tm * tn / (tm + tn) > best[0]):
                best = (tm * tn / (tm + tn), tm, tn)
    return best[1], best[2]


def _matmul_tn_acc(a, b, name, tk=1024):
    squeeze = a.ndim == 2
    a3 = a[None] if squeeze else a
    p_, t, m = a3.shape
    n = b.shape[1]
    nk = t // tk

    def body(a_ref, b_ref, o_ref, acc_ref):
        kt = pl.program_id(1)

        @pl.when(kt == 0)
        def _():
            acc_ref[...] = jnp.zeros_like(acc_ref)

        acc_ref[...] += _dot(a_ref[...], b_ref[...], _TN)

        @pl.when(kt == nk - 1)
        def _():
            o_ref[...] = acc_ref[...].astype(BF16)

    out = pl.pallas_call(
        body, name=name, grid=(p_, nk),
        in_specs=[pl.BlockSpec((None, tk, m), lambda p, kt: (p, kt, 0)), pl.BlockSpec((tk, n), lambda p, kt: (kt, 0))],
        out_specs=pl.BlockSpec((None, m, n), lambda p, kt: (p, 0, 0)), out_shape=jax.ShapeDtypeStruct((p_, m, n), BF16),
        scratch_shapes=[pltpu.VMEM((m, n), F32)], compiler_params=_params("parallel", "arbitrary"),
    )(a3, b)
    return out[0] if squeeze else out


def _matmul(a, b, mode, out_dtype, name, bias=None, resid=None):
    if mode == "nn":
        (m, k), (_, n) = a.shape, b.shape
    elif mode == "nt":
        (m, k), (n, _) = a.shape, b.shape
    else:
        (k, m), (_, n) = a.shape, b.shape
    tm, tn = _matmul_tiles(m, n, k, jnp.dtype(out_dtype).itemsize, resid is not None)
    dims = {"nn": _NN, "nt": _NT, "tn": _TN}[mode]
    a_spec = pl.BlockSpec((k, tm), lambda i, j: (0, i)) if mode == "tn" else pl.BlockSpec((tm, k), lambda i, j: (i, 0))
    b_spec = pl.BlockSpec((tn, k), lambda i, j: (j, 0)) if mode == "nt" else pl.BlockSpec((k, tn), lambda i, j: (0, j))
    in_specs, args = [a_spec, b_spec], [a, b]
    row_spec = pl.BlockSpec((1, tn), lambda i, j: (0, j))
    tile_spec = pl.BlockSpec((tm, tn), lambda i, j: (i, j))
    if bias is not None:
        in_specs.append(row_spec)
        args.append(bias)
    if resid is not None:
        in_specs += [tile_spec, row_spec]
        args += list(resid)
    out_shape = [jax.ShapeDtypeStruct((m, n), out_dtype)]
    out_specs = [tile_spec]
    if resid is not None:
        out_shape.append(jax.ShapeDtypeStruct((m, n), F32))
        out_specs.append(tile_spec)

    def body(*refs):
        a_ref, b_ref = refs[0], refs[1]
        pos = 2
        acc = _dot(a_ref[...], b_ref[...], dims)
        if bias is not None:
            acc = acc + refs[pos][...]
            pos += 1
        if resid is not None:
            x_ref, g_ref = refs[pos], refs[pos + 1]
            pos += 2
        refs[pos][...] = acc.astype(out_dtype)
        if resid is not None:
            refs[pos + 1][...] = x_ref[...] + g_ref[...] * acc

    outs = pl.pallas_call(
        body, name=name, grid=(m // tm, n // tn), in_specs=in_specs, out_specs=out_specs, out_shape=out_shape,
        compiler_params=_params("parallel", "parallel"),
    )(*args)
    return outs if resid is not None else outs[0]


def _modnorm(x, g, sc, sh, name):
    t, d = x.shape
    tm = _tile(t)
    row = pl.BlockSpec((1, d), lambda i: (0, 0))
    blk = pl.BlockSpec((tm, d), lambda i: (i, 0))

    def body(x_ref, g_ref, sc_ref, sh_ref, o_ref):
        x = x_ref[...]
        r = lax.rsqrt(jnp.mean(x * x, axis=-1, keepdims=True) + EPS)
        o_ref[...] = ((x * r) * g_ref[...] * (1.0 + sc_ref[...]) + sh_ref[...]).astype(BF16)

    return pl.pallas_call(
        body, name=name, grid=(t // tm,), in_specs=[blk, row, row, row], out_specs=blk,
        out_shape=jax.ShapeDtypeStruct((t, d), BF16), compiler_params=_params("parallel"),
    )(x, g, sc, sh)


def _gate_bwd_tile(dx, y_ref, gate_ref, dy_ref, dgate_ref, first):
    @pl.when(first)
    def _():
        dgate_ref[...] = jnp.zeros_like(dgate_ref)

    dy_ref[...] = (dx * gate_ref[...]).astype(BF16)
    dgate_ref[...] += _colsum(dx * y_ref[...].astype(F32))


def _modnorm_bwd(x, dh, g, sc, dres, below, name):
    t, d = x.shape
    tm = _tile(t)
    row = pl.BlockSpec((1, d), lambda i: (0, 0))
    blk = pl.BlockSpec((tm, d), lambda i: (i, 0))

    def body(x_ref, dh_ref, g_ref, sc_ref, dres_ref, *rest):
        dx_ref, dw_ref, dsh_ref = rest[-5:-2] if below else rest
        first = pl.program_id(0) == 0

        @pl.when(first)
        def _():
            dw_ref[...] = jnp.zeros_like(dw_ref)
            dsh_ref[...] = jnp.zeros_like(dsh_ref)

        x = x_ref[...]
        dh = dh_ref[...].astype(F32)
        r = lax.rsqrt(jnp.mean(x * x, axis=-1, keepdims=True) + EPS)
        xn = x * r
        dxn = dh * (g_ref[...] * (1.0 + sc_ref[...]))
        dx = dres_ref[...] + r * (dxn - xn * jnp.mean(dxn * xn, axis=-1, keepdims=True))
        dx_ref[...] = dx
        dw_ref[...] += _colsum(dh * xn)
        dsh_ref[...] += _colsum(dh)
        if below:
            _gate_bwd_tile(dx, rest[0], rest[1], rest[-2], rest[-1], first)

    row_out = jax.ShapeDtypeStruct((1, d), F32)
    return pl.pallas_call(
        body, name=name, grid=(t // tm,), in_specs=[blk, blk, row, row, blk] + ([blk, row] if below else []),
        out_specs=[blk, row, row] + ([blk, row] if below else []),
        out_shape=[jax.ShapeDtypeStruct((t, d), F32), row_out, row_out]
        + ([jax.ShapeDtypeStruct((t, d), BF16), row_out] if below else []),
        compiler_params=_params("arbitrary"),
    )(x, dh, g, sc, dres, *(below or ()))


def _loss_head(y, target, below, name):
    t, d = y.shape
    tm = _tile(t)
    blk = pl.BlockSpec((tm, d), lambda i: (i, 0))
    row = pl.BlockSpec((1, d), lambda i: (0, 0))
    one = pl.BlockSpec((1, 1), lambda i: (0, 0))
    steps = t // tm

    def body(y_ref, t_ref, yb_ref, gate_ref, dx_ref, loss_ref, dy_ref, dgate_ref, acc_ref):
        first = pl.program_id(0) == 0

        @pl.when(first)
        def _():
            acc_ref[...] = jnp.zeros_like(acc_ref)

        e = y_ref[...] - t_ref[...]
        dx = e * (1.0 / d)
        dx_ref[...] = dx
        acc_ref[...] += _colsum(e * e)
        _gate_bwd_tile(dx, yb_ref, gate_ref, dy_ref, dgate_ref, first)

        @pl.when(pl.program_id(0) == steps - 1)
        def _():
            loss_ref[...] = jnp.sum(acc_ref[...], axis=1, keepdims=True) * (0.5 / d)

    return pl.pallas_call(
        body, name=name, grid=(steps,), in_specs=[blk, blk, blk, row], out_specs=[blk, one, blk, row],
        out_shape=[jax.ShapeDtypeStruct((t, d), F32), jax.ShapeDtypeStruct((1, 1), F32),
                   jax.ShapeDtypeStruct((t, d), BF16), jax.ShapeDtypeStruct((1, d), F32)],
        scratch_shapes=[pltpu.VMEM((1, d), F32)], compiler_params=_params("arbitrary"),
    )(y, target, *below)


GMLP_TM = 512


def _group_norm(vg, gain):
    mu = jnp.mean(vg, axis=-1, keepdims=True)
    xc = vg - mu
    rstd = lax.rsqrt(jnp.mean(xc * xc, axis=-1, keepdims=True) + EPS)
    xhat = xc * rstd
    return xhat, rstd, xhat * gain


def _gmlp_fwd(z, gain, wtril, bias_exp, name):
    t = z.shape[0]
    tm = _tile(t, (GMLP_TM,))
    zu = pl.BlockSpec((tm, A_WIDTH), lambda i: (i, 0))
    zv = pl.BlockSpec((tm, A_WIDTH), lambda i: (i, 1))
    full2 = lambda shp: pl.BlockSpec(shp, lambda i: (0, 0))
    w_spec = pl.BlockSpec((A_GROUPS, CHUNK, CHUNK), lambda i: (0, 0, 0))

    def body(zu_ref, zv_ref, gain_ref, w_ref, b_ref, ya_ref):
        for c in range(tm // CHUNK):
            rows = slice(c * CHUNK, (c + 1) * CHUNK)
            ua = _gelu(zu_ref[rows, :].astype(F32))
            vg = _gelu(zv_ref[rows, :].astype(F32))
            for g in range(A_GROUPS):
                sl = slice(g * GROUP_DIM, (g + 1) * GROUP_DIM)
                _, _, vn = _group_norm(vg[:, sl], gain_ref[:, sl])
                f = _dot(w_ref[g], vn.astype(BF16), _NN) + b_ref[:, sl]
                ya_ref[rows, sl] = (ua[:, sl] * f).astype(BF16)

    return pl.pallas_call(
        body, name=name, grid=(t // tm,),
        in_specs=[zu, zv, full2((1, A_WIDTH)), w_spec, full2((CHUNK, A_WIDTH))], out_specs=zu,
        out_shape=jax.ShapeDtypeStruct((t, A_WIDTH + B_WIDTH), BF16), compiler_params=_params("parallel"),
    )(z, z, gain, wtril, bias_exp)


def _gmlp_bwd(z, dcat, gain, wtril, wtril_t, bias_exp, name):
    t = z.shape[0]
    tm = _tile(t, (GMLP_TM,))
    zu = pl.BlockSpec((tm, A_WIDTH), lambda i: (i, 0))
    zv = pl.BlockSpec((tm, A_WIDTH), lambda i: (i, 1))
    full2 = lambda shp: pl.BlockSpec(shp, lambda i: (0, 0))
    w_spec = pl.BlockSpec((A_GROUPS, CHUNK, CHUNK), lambda i: (0, 0, 0))
    dz_spec = pl.BlockSpec((tm, 2 * A_WIDTH), lambda i: (i, 0))

    def body(zu_ref, zv_ref, dya_ref, gain_ref, w_ref, wt_ref, b_ref, dz_ref, dw_ref, dgain_ref, dbias_ref):
        @pl.when(pl.program_id(0) == 0)
        def _():
            dw_ref[...] = jnp.zeros_like(dw_ref)
            dgain_ref[...] = jnp.zeros_like(dgain_ref)
            dbias_ref[...] = jnp.zeros_like(dbias_ref)

        row = lax.broadcasted_iota(jnp.int32, (CHUNK, CHUNK), 0)
        col = lax.broadcasted_iota(jnp.int32, (CHUNK, CHUNK), 1)
        for c in range(tm // CHUNK):
            rows = slice(c * CHUNK, (c + 1) * CHUNK)
            zu_v = zu_ref[rows, :].astype(F32)
            zv_v = zv_ref[rows, :].astype(F32)
            dya = dya_ref[rows, :].astype(F32)
            ua = _gelu(zu_v)
            vg = _gelu(zv_v)
            for g in range(A_GROUPS):
                sl = slice(g * GROUP_DIM, (g + 1) * GROUP_DIM)
                gain_g = gain_ref[:, sl]
                xhat, rstd, vn = _group_norm(vg[:, sl], gain_g)
                vn16 = vn.astype(BF16)
                f = _dot(w_ref[g], vn16, _NN) + b_ref[:, sl]
                df = dya[:, sl] * ua[:, sl]
                df16 = df.astype(BF16)
                dz_ref[rows, sl] = (dya[:, sl] * f * _gelu_grad(zu_v[:, sl])).astype(BF16)
                dw_ref[g] += jnp.where(row >= col, _dot(df16, vn16, _NT), 0.0)
                dvn = _dot(wt_ref[g], df16, _NN)
                dgain_ref[:, sl] += _colsum(dvn * xhat)
                dxh = dvn * gain_g
                dvg = rstd * (dxh - jnp.mean(dxh, axis=-1, keepdims=True) - xhat * jnp.mean(dxh * xhat, axis=-1, keepdims=True))
                dz_ref[rows, A_WIDTH + g * GROUP_DIM:A_WIDTH + (g + 1) * GROUP_DIM] = (dvg * _gelu_grad(zv_v[:, sl])).astype(BF16)
                dbias_ref[:, sl] += df

    return pl.pallas_call(
        body, name=name, grid=(t // tm,),
        in_specs=[zu, zv, zu, full2((1, A_WIDTH)), w_spec, w_spec, full2((CHUNK, A_WIDTH))],
        out_specs=[dz_spec, w_spec, full2((1, A_WIDTH)), full2((CHUNK, A_WIDTH))],
        out_shape=[jax.ShapeDtypeStruct((t, 2 * A_WIDTH), BF16), jax.ShapeDtypeStruct((A_GROUPS, CHUNK, CHUNK), F32),
                   jax.ShapeDtypeStruct((1, A_WIDTH), F32), jax.ShapeDtypeStruct((CHUNK, A_WIDTH), F32)],
        compiler_params=_params("arbitrary"),
    )(z, z, dcat, gain, wtril, wtril_t, bias_exp)


def _rope_tables(pos, inv_freq, sign, name):
    t = pos.shape[0]
    tm = _tile(t)
    row = pl.BlockSpec((1, B_WIDTH), lambda i: (0, 0))
    blk = pl.BlockSpec((tm, B_WIDTH), lambda i: (i, 0))

    def body(pos_ref, f_ref, s_ref, cos_ref, sin_ref):
        ang = pos_ref[...] * f_ref[:, 0:LANES]
        cos_ref[...] = jnp.tile(jnp.cos(ang), (1, B_WIDTH // LANES))
        sin_ref[...] = jnp.tile(jnp.sin(ang) * s_ref[:, 0:LANES], (1, B_WIDTH // LANES))

    return pl.pallas_call(
        body, name=name, grid=(t // tm,), in_specs=[pl.BlockSpec((tm, 1), lambda i: (i, 0)), row, row],
        out_specs=[blk, blk], out_shape=[jax.ShapeDtypeStruct((t, B_WIDTH), F32)] * 2,
        compiler_params=_params("parallel"),
    )(pos, inv_freq, sign)


def _head_sum(v, seg):
    hi = v.astype(BF16)
    lo = (v - hi.astype(F32)).astype(BF16)
    return _dot(hi, seg, _NN) + _dot(lo, seg, _NN)


def _swap_halves(v):
    lane = lax.broadcasted_iota(jnp.int32, v.shape, 1)
    return jnp.where((lane & (HEAD_DIM - 1)) < HEAD_DIM // 2,pltpu.roll(v, B_WIDTH - HEAD_DIM // 2, 1), pltpu.roll(v, HEAD_DIM // 2, 1))


DILATIONS = tuple(dil for _, dil in PATTERNS)
SUBSEQ_TM = 256
LANES = 128


def _subseq_shape(t, dil):
    return (t // dil, dil * B_WIDTH)


def _subseq_spec(tm, dil):
    return pl.BlockSpec((tm // dil, dil * B_WIDTH), lambda i: (i, 0))


def _to_subseq(x, scr_ref, dil):
    if dil == 1:
        return x
    tm, w = x.shape
    for c in range(w // LANES):
        scr_ref[c * tm:(c + 1) * tm, :] = x[:, c * LANES:(c + 1) * LANES]
    return jnp.concatenate([scr_ref[pl.ds(c * tm + r, tm // dil, stride=dil), :]
                            for r in range(dil) for c in range(w // LANES)], axis=1)


def _from_subseq(y, scr_ref, dil):
    if dil == 1:
        return y
    n, w = y.shape[0], y.shape[1] // dil
    tm = n * dil
    for r in range(dil):
        for c in range(w // LANES):
            scr_ref[pl.ds(c * tm + r, n, stride=dil), :] = y[:, r * w + c * LANES:r * w + (c + 1) * LANES]
    return jnp.concatenate([scr_ref[c * tm:(c + 1) * tm, :] for c in range(w // LANES)], axis=1)


def _subseq_scratch(tm):
    return pltpu.VMEM((B_WIDTH // LANES * tm, LANES), F32)


def _qk_prep(z, cos_t, sin_t, gq, gk, seg, name):
    t = z.shape[0]
    tm = _tile(t, (SUBSEQ_TM,))
    col = lambda c: pl.BlockSpec((tm, B_WIDTH), lambda i: (i, c))
    row = pl.BlockSpec((1, B_WIDTH), lambda i: (0, 0))
    blk = col(0)
    nd = len(DILATIONS)

    def body(q_ref, k_ref, v_ref, cos_ref, sin_ref, gq_ref, gk_ref, seg_ref, *rest):
        out_refs, scr_ref = rest[:-1], rest[-1]

        def norm_rot(x, g):
            r = lax.rsqrt(_head_sum(x * x, seg_ref[...]) * (1.0 / HEAD_DIM) + EPS)
            xn = x * r * g
            return xn * cos_ref[...] + _swap_halves(xn) * sin_ref[...]

        vals = (norm_rot(q_ref[...].astype(F32), gq_ref[...]), norm_rot(k_ref[...].astype(F32), gk_ref[...]),
                v_ref[...].astype(F32))
        for a, val in enumerate(vals):
            for b, dil in enumerate(DILATIONS):
                out_refs[a * nd + b][...] = _to_subseq(val, scr_ref, dil).astype(BF16)

    outs = pl.pallas_call(
        body, name=name, grid=(t // tm,),
        in_specs=[col(2), col(3), col(4), blk, blk, row, row, pl.BlockSpec((B_WIDTH, B_WIDTH), lambda i: (0, 0))],
        out_specs=[_subseq_spec(tm, dil) for _ in range(3) for dil in DILATIONS],
        out_shape=[jax.ShapeDtypeStruct(_subseq_shape(t, dil), BF16) for _ in range(3) for dil in DILATIONS],
        scratch_shapes=[_subseq_scratch(tm)], compiler_params=_params("parallel"),
    )(z, z, z, cos_t, sin_t, gq, gk, seg)
    return outs[:nd], outs[nd:2 * nd], outs[2 * nd:]


def _qk_prep_bwd(z, dqs, dks, dvs, cos_t, sin_t, gq, gk, seg, name):
    t = z.shape[0]
    tm = _tile(t, (SUBSEQ_TM,))
    col = lambda c: pl.BlockSpec((tm, B_WIDTH), lambda i: (i, c))
    row = pl.BlockSpec((1, B_WIDTH), lambda i: (0, 0))
    blk = col(0)
    nb = len(DILATIONS)
    subs = [_subseq_spec(tm, dil) for dil in DILATIONS]

    def body(*refs):
        q_ref, k_ref = refs[0], refs[1]
        dq_refs, dk_refs, dv_refs = refs[2:2 + nb], refs[2 + nb:2 + 2 * nb], refs[2 + 2 * nb:2 + 3 * nb]
        cos_ref, sin_ref, gq_ref, gk_ref, seg_ref, dz_ref, dgq_ref, dgk_ref, scr_ref = refs[2 + 3 * nb:]

        @pl.when(pl.program_id(0) == 0)
        def _():
            dgq_ref[...] = jnp.zeros_like(dgq_ref)
            dgk_ref[...] = jnp.zeros_like(dgk_ref)

        def total(d_refs):
            return sum(_from_subseq(r_[...], scr_ref, dil) for r_, dil in zip(d_refs, DILATIONS))

        def back(x, d_refs, g, dg_ref):
            dout = total(d_refs)
            dy = dout * cos_ref[...] + _swap_halves(dout * sin_ref[...])
            r = lax.rsqrt(_head_sum(x * x, seg_ref[...]) * (1.0 / HEAD_DIM) + EPS)
            xn = x * r
            dg_ref[...] += _colsum(dy * xn)
            dxn = dy * g
            return r * (dxn - xn * (_head_sum(dxn * xn, seg_ref[...]) * (1.0 / HEAD_DIM)))

        dz_ref[:, 0:B_WIDTH] = back(q_ref[...].astype(F32), dq_refs, gq_ref[...], dgq_ref).astype(BF16)
        dz_ref[:, B_WIDTH:2 * B_WIDTH] = back(k_ref[...].astype(F32), dk_refs, gk_ref[...], dgk_ref).astype(BF16)
        dz_ref[:, 2 * B_WIDTH:3 * B_WIDTH] = total(dv_refs).astype(BF16)

    return pl.pallas_call(
        body, name=name, grid=(t // tm,),
        in_specs=[col(2), col(3)] + subs * 3 + [blk, blk, row, row, pl.BlockSpec((B_WIDTH, B_WIDTH), lambda i: (0, 0))],
        out_specs=[pl.BlockSpec((tm, 3 * B_WIDTH), lambda i: (i, 0)), row, row],
        out_shape=[jax.ShapeDtypeStruct((t, 3 * B_WIDTH), BF16), jax.ShapeDtypeStruct((1, B_WIDTH), F32),
                   jax.ShapeDtypeStruct((1, B_WIDTH), F32)],
        scratch_shapes=[_subseq_scratch(tm)], compiler_params=_params("arbitrary"),
    )(z, z, *dqs, *dks, *dvs, cos_t, sin_t, gq, gk, seg)


def _subseq_views(x, col, name):
    t = x.shape[0]
    tm = _tile(t, (SUBSEQ_TM,))

    def body(x_ref, *rest):
        out_refs, scr_ref = rest[:-1], rest[-1]
        val = x_ref[...].astype(F32)
        for o_ref, dil in zip(out_refs, DILATIONS):
            o_ref[...] = _to_subseq(val, scr_ref, dil).astype(o_ref.dtype)

    return pl.pallas_call(
        body, name=name, grid=(t // tm,), in_specs=[pl.BlockSpec((tm, B_WIDTH), lambda i: (i, col))],
        out_specs=[_subseq_spec(tm, dil) for dil in DILATIONS],
        out_shape=[jax.ShapeDtypeStruct(_subseq_shape(t, dil), x.dtype) for dil in DILATIONS],
        scratch_shapes=[_subseq_scratch(tm)], compiler_params=_params("parallel"),
    )(x)


def _attn_fwd(q, k, v, dil, name):
    t = q.shape[0] * dil
    nb = t // dil // Q_BLOCK
    cur = pl.BlockSpec((Q_BLOCK, B_WIDTH), lambda r, i: (i, r))
    prev = pl.BlockSpec((Q_BLOCK, B_WIDTH), lambda r, i: (jnp.maximum(i - 1, 0), r))

    def body(q_ref, kp_ref, kc_ref, vp_ref, vc_ref, o_ref, lse_ref):
        i = pl.program_id(1)
        q = q_ref[...]
        kk = jnp.concatenate([kp_ref[...], kc_ref[...]], axis=0)
        vv = jnp.concatenate([vp_ref[...], vc_ref[...]], axis=0)
        a = lax.broadcasted_iota(jnp.int32, (Q_BLOCK, 2 * Q_BLOCK), 0)
        j = lax.broadcasted_iota(jnp.int32, (Q_BLOCK, 2 * Q_BLOCK), 1)
        dist = a + Q_BLOCK - j
        mask = (dist >= 0) & (dist <= Q_BLOCK) & ((j >= Q_BLOCK) | (i > 0))
        sls = [slice(h * HEAD_DIM, (h + 1) * HEAD_DIM) for h in range(HEADS)]
        scores = [_dot(q[:, sl], kk[:, sl], _NT) for sl in sls]
        ps, dens = [], []
        for sl, s in zip(sls, scores):
            s = jnp.where(mask, s * (HEAD_DIM ** -0.5), NEG)
            m = jnp.max(s, axis=-1, keepdims=True)
            p = jnp.exp(s - m)
            den = jnp.sum(p, axis=-1, keepdims=True)
            ps.append(p.astype(BF16))
            dens.append(den)
            lse_ref[:, sl] = jnp.broadcast_to(m + jnp.log(den), (Q_BLOCK, HEAD_DIM))
        for sl, p, den in zip(sls, ps, dens):
            o_ref[:, sl] = _dot(p, vv[:, sl], _NN) / den

    return pl.pallas_call(
        body, name=name, grid=(dil, nb), in_specs=[cur, prev, cur, prev, cur], out_specs=[cur, cur],
        out_shape=[jax.ShapeDtypeStruct(_subseq_shape(t, dil), F32)] * 2,
        compiler_params=_params("parallel", "parallel"),
    )(q, k, k, v, v)


def _attn_merge(outs, lses, cat, name):
    nb = len(DILATIONS)
    t = cat.shape[0]
    tm = _tile(t, (SUBSEQ_TM,))
    subs = [_subseq_spec(tm, dil) for dil in DILATIONS]

    def body(*refs):
        o_refs, l_refs = refs[:nb], refs[nb:2 * nb]
        yb_refs, lse_refs, cat_ref, scr_ref = refs[2 * nb + 1:3 * nb + 1], refs[3 * nb + 1:4 * nb + 1], refs[4 * nb + 1], refs[4 * nb + 2]
        ls = [_from_subseq(r[...], scr_ref, dil) for r, dil in zip(l_refs, DILATIONS)]
        m = functools.reduce(jnp.maximum, ls)
        tot = m + jnp.log(sum(jnp.exp(l - m) for l in ls))
        yb = sum(jnp.exp(l - tot) * _from_subseq(o[...], scr_ref, dil) for l, o, dil in zip(ls, o_refs, DILATIONS))
        cat_ref[...] = yb.astype(BF16)
        yb = yb.astype(BF16).astype(F32)
        for yb_ref, lse_ref, dil in zip(yb_refs, lse_refs, DILATIONS):
            yb_ref[...] = _to_subseq(yb, scr_ref, dil).astype(BF16)
            lse_ref[...] = _to_subseq(tot, scr_ref, dil)

    outs_ = pl.pallas_call(
        body, name=name, grid=(t // tm,), in_specs=subs * 2 + [pl.BlockSpec(memory_space=pl.ANY)],
        out_specs=subs * 2 + [pl.BlockSpec((tm, B_WIDTH), lambda i: (i, A_WIDTH // B_WIDTH))],
        out_shape=[jax.ShapeDtypeStruct(_subseq_shape(t, dil), BF16) for dil in DILATIONS]
        + [jax.ShapeDtypeStruct(_subseq_shape(t, dil), F32) for dil in DILATIONS] + [jax.ShapeDtypeStruct(cat.shape, BF16)],
        input_output_aliases={2 * nb: 2 * nb}, scratch_shapes=[_subseq_scratch(tm)], compiler_params=_params("parallel"),
    )(*outs, *lses, cat)
    return outs_[:nb], outs_[nb:2 * nb], outs_[2 * nb]


def _attn_bwd(q, k, v, do, o, lse, dil, name):
    t = q.shape[0] * dil
    nb = t // dil // Q_BLOCK
    cur = pl.BlockSpec((Q_BLOCK, B_WIDTH), lambda r, i: (i, r))
    prev = pl.BlockSpec((Q_BLOCK, B_WIDTH), lambda r, i: (jnp.maximum(i - 1, 0), r))
    scale = HEAD_DIM ** -0.5

    def body(q_ref, kp_ref, kc_ref, vp_ref, vc_ref, do_ref, o_ref, lse_ref, dq_ref, dk_ref, dv_ref,
             ck_ref, cv_ref, tk_ref, tv_ref):
        i = pl.program_id(1)

        @pl.when(i == 0)
        def _():
            ck_ref[...] = jnp.zeros_like(ck_ref)
            cv_ref[...] = jnp.zeros_like(cv_ref)

        q = q_ref[...]
        kk = jnp.concatenate([kp_ref[...], kc_ref[...]], axis=0)
        vv = jnp.concatenate([vp_ref[...], vc_ref[...]], axis=0)
        do = do_ref[...]
        dof = do.astype(F32)
        of = o_ref[...].astype(F32)
        a = lax.broadcasted_iota(jnp.int32, (Q_BLOCK, 2 * Q_BLOCK), 0)
        j = lax.broadcasted_iota(jnp.int32, (Q_BLOCK, 2 * Q_BLOCK), 1)
        dist = a + Q_BLOCK - j
        mask = (dist >= 0) & (dist <= Q_BLOCK) & ((j >= Q_BLOCK) | (i > 0))
        sls = [slice(h * HEAD_DIM, (h + 1) * HEAD_DIM) for h in range(HEADS)]
        scores = [_dot(q[:, sl], kk[:, sl], _NT) for sl in sls]
        dps = [_dot(do[:, sl], vv[:, sl], _NT) for sl in sls]
        ps, dss = [], []
        for sl, s, dp in zip(sls, scores, dps):
            p = jnp.exp(jnp.where(mask, s * scale, NEG) - lse_ref[:, sl.start:sl.start + 1])
            delta = jnp.sum(dof[:, sl] * of[:, sl], axis=-1, keepdims=True)
            dss.append((p * (dp - delta) * scale).astype(BF16))
            ps.append(p.astype(BF16))
        for sl, p, ds in zip(sls, ps, dss):
            dq_ref[:, sl] = _dot(ds, kk[:, sl], _NN)
            dv_t = _dot(do[:, sl], p, _TN)
            dk_t = _dot(q[:, sl], ds, _TN)
            tk_ref[sl, :] = ck_ref[sl, :] + dk_t[:, :Q_BLOCK]
            tv_ref[sl, :] = cv_ref[sl, :] + dv_t[:, :Q_BLOCK]
            ck_ref[sl, :] = dk_t[:, Q_BLOCK:]
            cv_ref[sl, :] = dv_t[:, Q_BLOCK:]

        @pl.when(i >= 1)
        def _():
            rows = pl.ds(pl.multiple_of((i - 1) * Q_BLOCK, Q_BLOCK), Q_BLOCK)
            dk_ref[rows, :] = tk_ref[...].T
            dv_ref[rows, :] = tv_ref[...].T

        @pl.when(i == nb - 1)
        def _():
            rows = pl.ds((nb - 1) * Q_BLOCK, Q_BLOCK)
            dk_ref[rows, :] = ck_ref[...].T
            dv_ref[rows, :] = cv_ref[...].T

    whole = pl.BlockSpec((t // dil, B_WIDTH), lambda r, i: (0, r))
    return pl.pallas_call(
        body, name=name, grid=(dil, nb), in_specs=[cur, prev, cur, prev, cur, cur, cur, cur],
        out_specs=[cur, whole, whole], out_shape=[jax.ShapeDtypeStruct(_subseq_shape(t, dil), F32)] * 3,
        scratch_shapes=[pltpu.VMEM((B_WIDTH, Q_BLOCK), F32)] * 4,
        compiler_params=_params("parallel", "arbitrary"),
    )(q, k, k, v, v, do, o, lse)


FFN_TN = 256
FFN_ACT_TM = (2048, 1024, 512, 256, 128)
FFN_FWD_CHUNK = 256
FFN_BWD_CHUNK = 128


def _ffn_up(h, up_t, name):
    t, k = h.shape
    tm = _tile(t)

    def body(h_ref, w_ref, o_ref):
        o_ref[...] = _dot(h_ref[...], w_ref[...], _NT).astype(BF16)

    return pl.pallas_call(
        body, name=name, grid=(2, t // tm),
        in_specs=[pl.BlockSpec((tm, k), lambda p, i: (i, 0)), pl.BlockSpec((None, FFN_DIM, k), lambda p, i: (p, 0, 0))],
        out_specs=pl.BlockSpec((None, tm, FFN_DIM), lambda p, i: (p, i, 0)),
        out_shape=jax.ShapeDtypeStruct((2, t, FFN_DIM), BF16), compiler_params=_params("parallel", "parallel"),
    )(h, up_t.reshape(2, FFN_DIM, k))


def _ffn_up_dx(du, up_t, name):
    t = du.shape[1]
    k = up_t.shape[1]
    tm = _tile(t)

    def body(a_ref, b_ref, o_ref):
        o_ref[...] = _dot(a_ref[0], b_ref[0], _NN) + _dot(a_ref[1], b_ref[1], _NN)

    return pl.pallas_call(
        body, name=name, grid=(t // tm,),
        in_specs=[pl.BlockSpec((2, tm, FFN_DIM), lambda i: (0, i, 0)), pl.BlockSpec((2, FFN_DIM, k), lambda i: (0, 0, 0))],
        out_specs=pl.BlockSpec((tm, k), lambda i: (i, 0)), out_shape=jax.ShapeDtypeStruct((t, k), F32),
        compiler_params=_params("parallel"),
    )(du, up_t.reshape(2, FFN_DIM, k))


def _ffn_conv(win, w_ref, b_ref, p):
    x = win.astype(F32)
    x0, x1, x2 = x[FFN_HALO:], pltpu.roll(x, 1, 0)[FFN_HALO:], pltpu.roll(x, 2, 0)[FFN_HALO:]
    return b_ref[p] + w_ref[p, 2:3, :] * x0 + w_ref[p, 1:2, :] * x1 + w_ref[p, 0:1, :] * x2


def _zero_if(cond, v):
    return jnp.where(cond, 0, v).astype(v.dtype)


def _ffn_act(u, dw_w, dw_b, name):
    t = u.shape[1]
    tm = _tile(t, FFN_ACT_TM)
    chunk = min(FFN_FWD_CHUNK, tm)
    hb = tm // FFN_HALO
    main = pl.BlockSpec((2, tm, FFN_TN), lambda i, j: (0, i, j))
    halo = pl.BlockSpec((2, FFN_HALO, FFN_TN), lambda i, j: (0, jnp.maximum(i * hb - 1, 0), j))
    wsp = pl.BlockSpec((2, FFN_CONV_WIDTH, FFN_TN), lambda i, j: (0, 0, j))
    bsp = pl.BlockSpec((2, 1, FFN_TN), lambda i, j: (0, 0, j))

    def body(u_ref, uh_ref, w_ref, b_ref, o_ref, z_ref):
        first = pl.program_id(0) == 0

        def emit(rows, wins):
            za, zb = _ffn_conv(wins[0], w_ref, b_ref, 0), _ffn_conv(wins[1], w_ref, b_ref, 1)
            o_ref[rows, :] = (za * _sigmoid(za) * zb).astype(BF16)
            z_ref[0, rows, :] = za.astype(BF16)
            z_ref[1, rows, :] = zb.astype(BF16)

        emit(pl.ds(0, chunk), [jnp.concatenate([_zero_if(first, uh_ref[p]), u_ref[p, 0:chunk, :]], axis=0) for p in range(2)])

        def step(c, carry):
            s = pl.multiple_of(c * chunk, chunk)
            emit(pl.ds(s, chunk), [u_ref[p, pl.ds(s - FFN_HALO, chunk + FFN_HALO), :] for p in range(2)])
            return carry

        lax.fori_loop(1, tm // chunk, step, 0)

    return pl.pallas_call(
        body, name=name, grid=(t // tm, FFN_DIM // FFN_TN), in_specs=[main, halo, wsp, bsp],
        out_specs=[pl.BlockSpec((tm, FFN_TN), lambda i, j: (i, j)), main],
        out_shape=[jax.ShapeDtypeStruct((t, FFN_DIM), BF16), jax.ShapeDtypeStruct((2, t, FFN_DIM), BF16)],
        compiler_params=_params("parallel", "parallel"),
    )(u, u, dw_w, dw_b)


def _fold8(v):
    return jnp.sum(v.reshape(v.shape[0] // 8, 8, v.shape[1]), axis=0)


def _ffn_act_bwd(u, z, dact, dw_w, name):
    t = u.shape[1]
    tm = _tile(t, FFN_ACT_TM)
    chunk = min(FFN_BWD_CHUNK, tm // 2)
    halo = FFN_HALO
    hb = tm // halo
    nt = t // tm
    last_halo = t // halo - 1
    next_i = lambda i: jnp.minimum((i + 1) * hb, last_halo)
    main = pl.BlockSpec((2, tm, FFN_TN), lambda j, i: (0, i, j))
    nxt = pl.BlockSpec((2, halo, FFN_TN), lambda j, i: (0, next_i(i), j))
    wsp = pl.BlockSpec((2, FFN_CONV_WIDTH, FFN_TN), lambda j, i: (0, 0, j))
    bsp = pl.BlockSpec((2, 1, FFN_TN), lambda j, i: (0, 0, j))

    def body(u_ref, z_ref, zn_ref, da_ref, dan_ref, w_ref, du_ref, dw_ref, db_ref, acc_ref):
        i = pl.program_id(1)
        last = i == nt - 1
        acc_ref[...] = jnp.zeros_like(acc_ref)

        def emit(rows, zs, dact):
            n = chunk + halo
            za, zb, dact = zs[0].astype(F32), zs[1].astype(F32), dact.astype(F32)
            sg = _sigmoid(za)
            dzs = (dact * zb * (sg * (1.0 + za * (1.0 - sg))), dact * (za * sg))
            for p, dz in enumerate(dzs):
                ahead = (dz[:chunk], pltpu.roll(dz, n - 1, 0)[:chunk], pltpu.roll(dz, n - 2, 0)[:chunk])
                um = u_ref[p, rows, :].astype(F32)
                acc_ref[p, FFN_CONV_WIDTH] += _fold8(ahead[0])
                du = None
                for j, dzj in enumerate(ahead):
                    k = FFN_CONV_WIDTH - 1 - j
                    acc_ref[p, k] += _fold8(dzj * um)
                    term = w_ref[p, k:k + 1, :] * dzj
                    du = term if du is None else du + term
                du_ref[p, rows, :] = du.astype(BF16)

        def step(c, carry):
            s = pl.multiple_of(c * chunk, chunk)
            emit(pl.ds(s, chunk), [z_ref[p, pl.ds(s, chunk + halo), :] for p in range(2)], da_ref[pl.ds(s, chunk + halo), :])
            return carry

        lax.fori_loop(0, tm // chunk - 1, step, 0)
        s = tm - chunk
        emit(pl.ds(s, chunk),
             [jnp.concatenate([z_ref[p, s:tm, :], zn_ref[p]], axis=0) for p in range(2)],
             jnp.concatenate([da_ref[s:tm, :], _zero_if(last, dan_ref[...])], axis=0))

        @pl.when(i == 0)
        def _():
            dw_ref[...] = jnp.zeros_like(dw_ref)
            db_ref[...] = jnp.zeros_like(db_ref)

        for p in range(2):
            for k in range(FFN_CONV_WIDTH):
                dw_ref[p, k:k + 1, :] += _colsum(acc_ref[p, k])
            db_ref[p] += _colsum(acc_ref[p, FFN_CONV_WIDTH])

    return pl.pallas_call(
        body, name=name, grid=(FFN_DIM // FFN_TN, nt),
        in_specs=[main, main, nxt, pl.BlockSpec((tm, FFN_TN), lambda j, i: (i, j)),
                  pl.BlockSpec((halo, FFN_TN), lambda j, i: (next_i(i), j)), wsp],
        out_specs=[main, wsp, bsp],
        out_shape=[jax.ShapeDtypeStruct((2, t, FFN_DIM), BF16), jax.ShapeDtypeStruct((2, FFN_CONV_WIDTH, FFN_DIM), F32),
                   jax.ShapeDtypeStruct((2, 1, FFN_DIM), F32)],
        scratch_shapes=[pltpu.VMEM((2, FFN_CONV_WIDTH + 1, 8, FFN_TN), F32)],
        compiler_params=_params("parallel", "arbitrary"),
    )(u, z, z, dact, dact, dw_w)


CONV_TM = 256
CONV_ROWS = 128
CONV_LANES = 128
CONV_NORM_ROWS = 32


def _glu_window(pa_ref, pah_ref, pg_ref, pgh_ref, scr_ref, first):
    ah, gh = pah_ref[...].astype(F32), pgh_ref[...].astype(F32)
    scr_ref[0:CONV_HALO, :] = jnp.where(first, 0.0, ah * _sigmoid(gh))
    scr_ref[CONV_HALO:, :] = pa_ref[...].astype(F32) * _sigmoid(pg_ref[...].astype(F32))


def _tap_slabs(win, rows, ahead):
    n = win.shape[0]
    for s in range(8):
        ws = win if s == 0 else pltpu.roll(win, n - s if ahead else s, 0)
        for q in range(CONV_HALO // 8):
            o = 8 * q + s
            if o < CONV_WIDTH:
                start = 8 * q if ahead else CONV_HALO - 8 * q
                yield CONV_WIDTH - 1 - o, ws[start:start + rows]


def _conformer_specs(t):
    tm = _tile(t, (CONV_TM, 128))
    hb = tm // CONV_HALO
    d = D_MODEL
    main = lambda c: pl.BlockSpec((tm, d), lambda i: (i, c))
    halo = lambda c: pl.BlockSpec((CONV_HALO, d), lambda i: (jnp.maximum(i * hb - 1, 0), c))
    row = pl.BlockSpec((1, d), lambda i: (0, 0))
    wsp = pl.BlockSpec((CONV_WIDTH, d), lambda i: (0, 0))
    return tm, main, halo, row, wsp


def _conformer_mid(p, dw_w, dw_b, ln_g, ln_b, name):
    t = p.shape[0]
    tm, main, halo, row, wsp = _conformer_specs(t)
    d, lanes = D_MODEL, CONV_LANES

    def body(pa_ref, pah_ref, pg_ref, pgh_ref, w_ref, b_ref, g_ref, lb_ref, o_ref, dc_ref, scr_ref):
        _glu_window(pa_ref, pah_ref, pg_ref, pgh_ref, scr_ref, pl.program_id(0) == 0)
        for c in range(d // lanes):
            ls = slice(c * lanes, (c + 1) * lanes)
            acc = jnp.broadcast_to(b_ref[:, ls], (tm, lanes))
            for k, slab in _tap_slabs(scr_ref[:, ls], tm, False):
                acc = acc + w_ref[k:k + 1, ls] * slab
            dc_ref[:, ls] = acc

        def norm(r, carry):
            r0 = pl.multiple_of(r * CONV_NORM_ROWS, CONV_NORM_ROWS)
            dc = dc_ref[pl.ds(r0, CONV_NORM_ROWS), :]
            xc = dc - jnp.mean(dc, axis=-1, keepdims=True)
            ln = xc * lax.rsqrt(jnp.mean(xc * xc, axis=-1, keepdims=True) + EPS) * g_ref[...] + lb_ref[...]
            o_ref[pl.ds(r0, CONV_NORM_ROWS), :] = (ln * _sigmoid(ln)).astype(BF16)
            return carry

        lax.fori_loop(0, tm // CONV_NORM_ROWS,norm, 0)

    return pl.pallas_call(
        body, name=name, grid=(t // tm,), in_specs=[main(0), halo(0), main(1), halo(1), wsp, row, row, row],
        out_specs=[main(0), main(0)], out_shape=[jax.ShapeDtypeStruct((t, d), BF16), jax.ShapeDtypeStruct((t, d), F32)],
        scratch_shapes=[pltpu.VMEM((tm + CONV_HALO, d), F32)], compiler_params=_params("parallel"),
    )(p, p, p, p, dw_w, dw_b, ln_g, ln_b)


def _conformer_mid_bwd(p, dc, ds, ln_g, ln_b, name):
    t = p.shape[0]
    tm, main, halo, row, wsp = _conformer_specs(t)
    d, nt = D_MODEL, t // tm
    rows, lanes = CONV_ROWS, CONV_LANES

    def body(pa_ref, pah_ref, pg_ref, pgh_ref, dc_ref, ds_ref, g_ref, lb_ref,
             ddc_ref, dw_ref, db_ref, dg_ref, dlb_ref, scr_ref, wacc_ref, racc_ref):
        i = pl.program_id(0)

        @pl.when(i == 0)
        def _():
            wacc_ref[...] = jnp.zeros_like(wacc_ref)
            racc_ref[...] = jnp.zeros_like(racc_ref)

        _glu_window(pa_ref, pah_ref, pg_ref, pgh_ref, scr_ref, i == 0)

        def norm_bwd(r, carry):
            r0 = pl.multiple_of(r * CONV_NORM_ROWS, CONV_NORM_ROWS)
            dcv = dc_ref[pl.ds(r0, CONV_NORM_ROWS), :]
            xc = dcv - jnp.mean(dcv, axis=-1, keepdims=True)
            rstd = lax.rsqrt(jnp.mean(xc * xc, axis=-1, keepdims=True) + EPS)
            xhat = xc * rstd
            ln = xhat * g_ref[...] + lb_ref[...]
            sg = _sigmoid(ln)
            dln = ds_ref[pl.ds(r0, CONV_NORM_ROWS), :].astype(F32) * (sg * (1.0 + ln * (1.0 - sg)))
            dxh = dln * g_ref[...]
            ddc = rstd * (dxh - jnp.mean(dxh, axis=-1, keepdims=True) - xhat * jnp.mean(dxh * xhat, axis=-1, keepdims=True))
            ddc_ref[pl.ds(r0, CONV_NORM_ROWS), :] = ddc
            racc_ref[0] += _fold8(dln * xhat)
            racc_ref[1] += _fold8(dln)
            racc_ref[2] += _fold8(ddc)
            return carry

        lax.fori_loop(0, tm // CONV_NORM_ROWS,norm_bwd, 0)

        for c in range(d // lanes):
            ls = slice(c * lanes, (c + 1) * lanes)

            def taps(r, carry, ls=ls):
                r0 = pl.multiple_of(r * rows, rows)
                ddc = ddc_ref[pl.ds(r0, rows), ls]
                for k, slab in _tap_slabs(scr_ref[pl.ds(r0, rows + CONV_HALO), ls], rows, False):
                    wacc_ref[k, :, ls] += _fold8(ddc * slab)
                return carry

            lax.fori_loop(0, tm // rows, taps, 0)

        @pl.when(i == nt - 1)
        def _():
            for k in range(CONV_WIDTH):
                dw_ref[k:k + 1, :] = _colsum(wacc_ref[k])
            dg_ref[...] = _colsum(racc_ref[0])
            dlb_ref[...] = _colsum(racc_ref[1])
            db_ref[...] = _colsum(racc_ref[2])

    return pl.pallas_call(
        body, name=name, grid=(nt,), in_specs=[main(0), halo(0), main(1), halo(1), main(0), main(0), row, row],
        out_specs=[main(0), wsp, row, row, row],
        out_shape=[jax.ShapeDtypeStruct((t, d), F32), jax.ShapeDtypeStruct((CONV_WIDTH, d), F32)]
        + [jax.ShapeDtypeStruct((1, d), F32)] * 3,
        scratch_shapes=[pltpu.VMEM((tm + CONV_HALO, d), F32), pltpu.VMEM((CONV_WIDTH, 8, d), F32), pltpu.VMEM((3, 8, d), F32)],
        compiler_params=_params("arbitrary"),
    )(p, p, p, p, dc, ds, ln_g, ln_b)


def _conformer_glu_bwd(p, ddc, dw_w, name):
    t = p.shape[0]
    d = D_MODEL
    tm = _tile(t, (CONV_TM, 128))
    hb = tm // CONV_HALO
    nt = t // tm
    last_halo = t // CONV_HALO - 1
    rows, lanes = CONV_ROWS, CONV_LANES
    col = lambda c: pl.BlockSpec((tm, d), lambda i: (i, c))
    nxt = pl.BlockSpec((CONV_HALO, d), lambda i: (jnp.minimum((i + 1) * hb, last_halo), 0))

    def body(pa_ref, pg_ref, ddc_ref, ddcn_ref, w_ref, dp_ref, db_ref, scr_ref, acc_ref):
        i = pl.program_id(0)

        @pl.when(i == 0)
        def _():
            acc_ref[...] = jnp.zeros_like(acc_ref)

        scr_ref[0:tm, :] = ddc_ref[...]
        scr_ref[tm:, :] = _zero_if(i == nt - 1, ddcn_ref[...])
        for c in range(d // lanes):
            ls = slice(c * lanes, (c + 1) * lanes)
            gs = slice(d + c * lanes, d + (c + 1) * lanes)

            def taps(r, carry, ls=ls, gs=gs):
                r0 = pl.multiple_of(r * rows, rows)
                dglu = None
                for k, slab in _tap_slabs(scr_ref[pl.ds(r0, rows + CONV_HALO), ls], rows, True):
                    term = w_ref[k:k + 1, ls] * slab
                    dglu = term if dglu is None else dglu + term
                a = pa_ref[pl.ds(r0, rows), ls].astype(F32)
                sg = _sigmoid(pg_ref[pl.ds(r0, rows), ls].astype(F32))
                da = (dglu * sg).astype(BF16)
                dg = (dglu * a * sg * (1.0 - sg)).astype(BF16)
                dp_ref[pl.ds(r0, rows), ls] = da
                dp_ref[pl.ds(r0, rows), gs] = dg
                acc_ref[:, ls] += _fold8(da.astype(F32))
                acc_ref[:, gs] += _fold8(dg.astype(F32))
                return carry

            lax.fori_loop(0, tm // rows, taps, 0)

        @pl.when(i == nt - 1)
        def _():
            db_ref[...] = _colsum(acc_ref[...])

    return pl.pallas_call(
        body, name=name, grid=(nt,),
        in_specs=[col(0), col(1), col(0), nxt, pl.BlockSpec((CONV_WIDTH, d), lambda i: (0, 0))],
        out_specs=[pl.BlockSpec((tm, 2 * d), lambda i: (i, 0)), pl.BlockSpec((1, 2 * d), lambda i: (0, 0))],
        out_shape=[jax.ShapeDtypeStruct((t, 2 * d), BF16), jax.ShapeDtypeStruct((1, 2 * d), F32)],
        scratch_shapes=[pltpu.VMEM((tm + CONV_HALO, d), F32), pltpu.VMEM((8, 2 * d), F32)],
        compiler_params=_params("arbitrary"),
    )(p, p, ddc, ddc, dw_w)


def _colsum_call(a, name):
    t, n = a.shape
    tm = _tile(t)

    def body(a_ref, o_ref):
        @pl.when(pl.program_id(0) == 0)
        def _():
            o_ref[...] = jnp.zeros_like(o_ref)

        o_ref[...] += _colsum(a_ref[...].astype(F32))

    return pl.pallas_call(
        body, name=name, grid=(t // tm,), in_specs=[pl.BlockSpec((tm, n), lambda i: (i, 0))],
        out_specs=pl.BlockSpec((1, n), lambda i: (0, 0)), out_shape=jax.ShapeDtypeStruct((1, n), F32),
        compiler_params=_params("arbitrary"),
    )(a)


def _ada_fwd(c_all, w, name):
    rows, d = c_all.shape
    n = w.shape[1]
    tn = _tile(n, (256, 128))

    def body(c_ref, w_ref, o_ref):
        c = c_ref[...]
        o_ref[...] = _dot((c * _sigmoid(c)).astype(BF16), w_ref[...].astype(BF16), _NN)

    return pl.pallas_call(
        body, name=name, grid=(n // tn,),
        in_specs=[pl.BlockSpec((rows, d), lambda j: (0, 0)), pl.BlockSpec((d, tn), lambda j: (0, j))],
        out_specs=pl.BlockSpec((rows, tn), lambda j: (0, j)), out_shape=jax.ShapeDtypeStruct((rows, n), F32),
        compiler_params=_params("parallel"),
    )(c_all, w)


def _ada_bwd(c_all, dmod, name):
    rows, d = c_all.shape
    n = dmod.shape[1]
    tn = _tile(n, (256, 128))

    def body(c_ref, g_ref, o_ref):
        c = c_ref[...]
        o_ref[...] = _dot((c * _sigmoid(c)).astype(BF16), g_ref[...].astype(BF16), _TN)

    return pl.pallas_call(
        body, name=name, grid=(n // tn,),
        in_specs=[pl.BlockSpec((rows, d), lambda j: (0, 0)), pl.BlockSpec((rows, tn), lambda j: (0, j))],
        out_specs=pl.BlockSpec((d, tn), lambda j: (0, j)), out_shape=jax.ShapeDtypeStruct((d, n), F32),
        compiler_params=_params("parallel"),
    )(c_all, dmod)


def _sum_in_device_order(own, land, me, name):
    s, r, c = land.shape
    tr = _row_tile(r, 256)
    slot = lambda k: pl.BlockSpec((None, tr, c), lambda i, me_ref: (jnp.where(me_ref[0] == k, (k + 1) % s, k), i, 0))
    own_spec = pl.BlockSpec((tr, c), lambda i, me_ref: (i, 0))

    def body(me_ref, own_ref, *refs):
        o_ref = refs[-1]
        acc = None
        for k, ref in enumerate(refs[:-1]):
            term = jnp.where(me_ref[0] == k, own_ref[...], ref[...]).astype(F32)
            acc = term if acc is None else acc + term
        o_ref[...] = acc

    return pl.pallas_call(
        body, name=name, out_shape=jax.ShapeDtypeStruct((r, c), F32),
        grid_spec=pltpu.PrefetchScalarGridSpec(
            num_scalar_prefetch=1, grid=(r // tr,), in_specs=[own_spec] + [slot(k) for k in range(s)], out_specs=own_spec),
        compiler_params=_params("parallel"),
    )(me, own, *[land] * s)


def _sum_with_own(blocks, land, me, name):
    s, r, c = land.shape
    tr = _row_tile(r, 256)
    slot = lambda k: pl.BlockSpec((None, tr, c), lambda i, me_ref: ((me_ref[0] + k) % s, i, 0))

    def body(me_ref, own_ref, *refs):
        o_ref = refs[-1]
        acc = own_ref[...].astype(F32)
        for ref in refs[:-1]:
            acc = acc + ref[...].astype(F32)
        o_ref[...] = acc

    return pl.pallas_call(
        body, name=name, out_shape=jax.ShapeDtypeStruct((r, c), F32),
        grid_spec=pltpu.PrefetchScalarGridSpec(
            num_scalar_prefetch=1, grid=(r // tr,), in_specs=[slot(0)] + [slot(k) for k in range(1, s)],
            out_specs=pl.BlockSpec((tr, c), lambda i, me_ref: (i, 0))),
        compiler_params=_params("parallel"),
    )(me, blocks, *[land] * (s - 1))


def _adamw_update(w, g, m, v):
    nm = ADAM_B1 * m + (1.0 - ADAM_B1) * g
    nv = ADAM_B2 * v + (1.0 - ADAM_B2) * (g * g)
    m_hat = nm * (1.0 / (1.0 - ADAM_B1 ** ADAM_STEP))
    v_hat = nv * (1.0 / (1.0 - ADAM_B2 ** ADAM_STEP))
    return -ADAM_LR * (m_hat / (jnp.sqrt(v_hat) + ADAM_EPS) + ADAM_WD * w), nm, nv


def _adamw(w, g, m, v, behind, name):
    l, r, c = w.shape
    tr = _row_tile(r, 256)
    blk = pl.BlockSpec((None, tr, c), lambda k, i: (k, i, 0))
    order = [] if behind is None else [behind]

    def body(w_ref, g_ref, m_ref, v_ref, *rest):
        d_ref, nm_ref, nv_ref = rest[-3:]
        d_ref[...], nm_ref[...], nv_ref[...] = _adamw_update(w_ref[...], g_ref[...], m_ref[...], v_ref[...])

    return pl.pallas_call(
        body, name=name, grid=(l, r // tr), in_specs=[blk] * 4 + [pl.BlockSpec(memory_space=pl.ANY)] * len(order),
        out_specs=[blk] * 3, out_shape=[jax.ShapeDtypeStruct(w.shape, F32)] * 3,
        compiler_params=_params("parallel", "parallel"),
    )(w, g, m, v, *order)


def _adamw_small(ws, gs, ms, vs, name):
    n = len(ws)
    two_d = lambda a: a.reshape(-1, a.shape[-1])

    def body(*refs):
        ins, outs = refs[:4 * n], refs[4 * n:]
        for a in range(n):
            outs[a][...], outs[n + a][...], outs[2 * n + a][...] = _adamw_update(*[ins[k * n + a][...] for k in range(4)])

    res = pl.pallas_call(
        body, name=name, out_shape=[jax.ShapeDtypeStruct(two_d(w).shape, F32) for w in ws] * 3,
    )(*[two_d(a) for a in (*ws, *gs, *ms, *vs)])
    return [[res[k * n + a].reshape(ws[a].shape) for a in range(n)] for k in range(3)]


def _mesh_pos():
    return lax.axis_index("x"), lax.axis_index("y"), lax.axis_index("c")


def _all_gather_vmem(x_shard, name):
    m_per, n = x_shard.shape

    def body(x_ref, out_ref, send_sems, recv_sems, local_sem):
        x, y, c = _mesh_pos()
        me, sibling = (x, y, c), (x, y, 1 - c)
        chips = [(1 - x, y), (x, 1 - y), (1 - x, 1 - y)]

        def rows(px, py, pc):
            return out_ref.at[pl.ds((4 * px + 2 * py + pc) * m_per, m_per), :]

        def copy(k, block, to, src=None):
            return pltpu.make_async_remote_copy(
                src_ref=rows(*block) if src is None else src, dst_ref=rows(*block),
                send_sem=send_sems.at[k], recv_sem=recv_sems.at[k], device_id=to, device_id_type=MESH)

        mine = pltpu.make_async_copy(x_ref, rows(*me), local_sem)
        mine.start()
        first = [copy(0, me, sibling, src=x_ref)]
        first += [copy(1 + j, me, (*chip, c), src=x_ref) for j, chip in enumerate(chips)]
        for cp in first:
            cp.start()
        passed = [copy(4 + j, (*chip, c), sibling) for j, chip in enumerate(chips)]
        for j, chip in enumerate(chips):
            copy(1 + j, (*chip, c), me).wait_recv()
            passed[j].start()
        copy(0, sibling, me).wait_recv()
        for j, chip in enumerate(chips):
            copy(4 + j, (*chip, 1 - c), me).wait_recv()
        for cp in first + passed:
            cp.wait_send()
        mine.wait()

    return pl.pallas_call(
        body, name=name, out_shape=jax.ShapeDtypeStruct((N_DEV * m_per, n), x_shard.dtype),
        in_specs=[pl.BlockSpec(memory_space=pltpu.VMEM)], out_specs=pl.BlockSpec(memory_space=pltpu.VMEM),
        scratch_shapes=[pltpu.SemaphoreType.DMA((7,)), pltpu.SemaphoreType.DMA((7,)), pltpu.SemaphoreType.DMA],
    )(x_shard)


def _all_gather_hbm(shards, name):
    n = len(shards)
    out_shape = [jax.ShapeDtypeStruct((N_DEV,) + s.shape, s.dtype) for s in shards]

    def body(*refs):
        x_refs, out_refs = refs[:n], refs[n:2 * n]
        send_sems, recv_sems, local_sems = refs[2 * n:]
        x, y, c = _mesh_pos()
        me, sibling = (x, y, c), (x, y, 1 - c)
        chips = [(1 - x, y), (x, 1 - y), (1 - x, 1 - y)]

        def blk(a, p):
            return out_refs[a].at[4 * p[0] + 2 * p[1] + p[2]]

        def copy(a, k, block, to, src=None):
            return pltpu.make_async_remote_copy(
                src_ref=blk(a, block) if src is None else src, dst_ref=blk(a, block),
                send_sem=send_sems.at[7 * a + k], recv_sem=recv_sems.at[7 * a + k], device_id=to, device_id_type=MESH)

        mine = [pltpu.make_async_copy(x_refs[a], blk(a, me), local_sems.at[a]) for a in range(n)]
        for cp in mine:
            cp.start()
        first = []
        for a in range(n):
            first.append(copy(a, 0, me, sibling, src=x_refs[a]))
            first += [copy(a, 1 + j, me, (*chip, c), src=x_refs[a]) for j, chip in enumerate(chips)]
        for cp in first:
            cp.start()
        passed = []
        for j, chip in enumerate(chips):
            for a in range(n):
                copy(a, 1 + j, (*chip, c), me).wait_recv()
                fwd = copy(a, 4 + j, (*chip, c), sibling)
                fwd.start()
                passed.append(fwd)
        for a in range(n):
            copy(a, 0, sibling, me).wait_recv()
            for j, chip in enumerate(chips):
                copy(a, 4 + j, (*chip, 1 - c), me).wait_recv()
        for cp in first + passed:
            cp.wait_send()
        for cp in mine:
            cp.wait()

    return pl.pallas_call(
        body, name=name, out_shape=out_shape, in_specs=[pl.BlockSpec(memory_space=pltpu.VMEM)] * n,
        out_specs=[pl.BlockSpec(memory_space=pl.ANY)] * n,
        scratch_shapes=[pltpu.SemaphoreType.DMA((7 * n,)), pltpu.SemaphoreType.DMA((7 * n,)), pltpu.SemaphoreType.DMA((n,))],
    )(*shards)


def _peers(x, y, c):
    flip = lambda v, f: 1 - v if f else v
    return [(flip(x, m & 4), flip(y, m & 2), flip(c, m & 1)) for m in range(1, N_DEV)]


def _dev_index(p):
    return 4 * p[0] + 2 * p[1] + p[2]


def _push_copies(src_refs, land_refs, send_sems, recv_sems, scatter, receive):
    x, y, c = _mesh_pos()
    me = _dev_index((x, y, c))
    copies = []
    for a, (src, land) in enumerate(zip(src_refs, land_refs)):
        for k, p in enumerate(_peers(x, y, c)):
            copies.append(pltpu.make_async_remote_copy(
                src_ref=src.at[_dev_index(p)] if scatter else src, dst_ref=land.at[_dev_index(p) if receive else me],
                send_sem=send_sems.at[7 * a + k], recv_sem=recv_sems.at[7 * a + k], device_id=p, device_id_type=MESH))
    return copies


_HBM = pl.BlockSpec(memory_space=pltpu.HBM)
_SEM = pl.BlockSpec(memory_space=pltpu.SEMAPHORE)
_EFFECT = pltpu.SideEffectType.DATAFLOW_SIDE_EFFECTING


def _pushes_start(srcs, lands, scatter, name):
    n = len(srcs)

    def body(*refs):
        src_refs, land_refs = refs[:n], refs[n:2 * n]
        send_sems, recv_sems = refs[2 * n], refs[2 * n + 1]
        token = refs[-1]
        for cp in _push_copies(src_refs, land_refs, send_sems, recv_sems, scatter, receive=False):
            cp.start()
        token[...] = jnp.zeros_like(token)

    hbm = lambda a: pltpu.HBM(a.shape, a.dtype)
    sems = pltpu.SemaphoreType.DMA((7 * n,))
    outs = pl.pallas_call(
        body, name=name,
        out_shape=(sems, sems, *[hbm(a) for a in srcs], *[hbm(a) for a in lands], jax.ShapeDtypeStruct((8, 128), F32)),
        in_specs=[_HBM] * (2 * n), out_specs=(_SEM, _SEM, *[_HBM] * (2 * n), pl.BlockSpec(memory_space=pltpu.VMEM)),
        input_output_aliases={i: 2 + i for i in range(2 * n)},
        compiler_params=pltpu.CompilerParams(has_side_effects=_EFFECT),
    )(*[pltpu.with_memory_space_constraint(a, pltpu.HBM) for a in (*srcs, *lands)])
    return (outs[0], outs[1], outs[2:2 + n], outs[2 + n:2 + 2 * n], scatter), outs[-1]


def _pushes_wait(handle, after, name):
    send_sems, recv_sems, srcs, lands, scatter = handle
    n = len(srcs)
    after = after if isinstance(after, (tuple, list)) else (after,)

    def body(*refs):
        src_refs, land_refs = refs[:n], refs[n:2 * n]
        for cp in _push_copies(src_refs, land_refs, refs[2 * n], refs[2 * n + 1], scatter, receive=True):
            cp.wait_send()
            cp.wait_recv()

    hbm = lambda a: pltpu.HBM(a.shape, a.dtype)
    outs = pl.pallas_call(
        body, name=name, out_shape=tuple(hbm(a) for a in (*srcs, *lands)),
        in_specs=[_HBM] * (2 * n) + [_SEM, _SEM] + [pl.BlockSpec(memory_space=pl.ANY)] * len(after),
        out_specs=tuple([_HBM] * (2 * n)), input_output_aliases={i: i for i in range(2 * n)},
        compiler_params=pltpu.CompilerParams(has_side_effects=_EFFECT),
    )(*srcs, *lands, send_sems, recv_sems, *after)
    return outs[:n], outs[n:]


def _landing_zones(srcs, behind, name):
    n, nb = len(srcs), len(behind)

    def body(*refs):
        src_refs, land_refs, bufs, sems = refs[:n], refs[n + nb:2 * n + nb], refs[2 * n + nb:3 * n + nb], refs[3 * n + nb]
        me = _dev_index(_mesh_pos())
        load = [pltpu.make_async_copy(src, buf, sems.at[a]) for a, (src, buf) in enumerate(zip(src_refs, bufs))]
        store = [pltpu.make_async_copy(buf, land.at[me], sems.at[a]) for a, (buf, land) in enumerate(zip(bufs, land_refs))]
        for cp in load:
            cp.start()
        for ld, st in zip(load, store):
            ld.wait()
            st.start()
        for cp in store:
            cp.wait()

    any_spec = pl.BlockSpec(memory_space=pl.ANY)
    return pl.pallas_call(
        body, name=name, out_shape=[jax.ShapeDtypeStruct((N_DEV,) + s.shape, s.dtype) for s in srcs],
        in_specs=[any_spec] * (n + nb), out_specs=[any_spec] * n,
        scratch_shapes=[pltpu.VMEM(s.shape, s.dtype) for s in srcs] + [pltpu.SemaphoreType.DMA((n,))],
        compiler_params=pltpu.CompilerParams(vmem_limit_bytes=V7X_VMEM_LIMIT),
    )(*srcs, *behind)


def _ffn_forward(x, mod, norm_g, w, tag):
    sh, sc, gate = mod
    h = _modnorm(x, norm_g, sc, sh, f"{tag}_norm")
    u = _ffn_up(h, w["up_t"], f"{tag}_up")
    act, z = _ffn_act(u, w["dw_w"], w["dw_b"], f"{tag}_act")
    y, x_new = _matmul(act, w["down"], "nn", BF16, f"{tag}_down", resid=(x, gate))
    return x_new, (x, h, u, z, act, y)


def _behind(row, token):
    return row if token is None else row + token[0:1, 0:1]


def _ffn_backward(dx_new, dy, d_gate, saved, mod, norm_g, w, tag, emit, below):
    x, h, u, z, act, _ = saved
    _, sc, _ = mod
    d_down = _matmul_tn_acc(act, dy, f"{tag}_down_dw")
    dact = _matmul(dy, w["down"], "nt", BF16, f"{tag}_down_dx")
    du, d_dw_w, d_dw_b = _ffn_act_bwd(u, z, dact, w["dw_w"], f"{tag}_act_bwd")
    d_up_t = _matmul_tn_acc(du, h, f"{tag}_up_dw").reshape(2 * FFN_DIM, -1)
    token = emit([d_up_t, d_down])
    dh = _ffn_up_dx(du, w["up_t"], f"{tag}_up_dx")
    dx, d_w, d_sh, *dy_below = _modnorm_bwd(x, dh, norm_g, _behind(sc, token), dx_new, below, f"{tag}_norm_bwd")
    return (dx, *dy_below), dict(dw_w=d_dw_w.transpose(1, 0, 2).reshape(FFN_CONV_WIDTH, 2 * FFN_DIM),
                    dw_b=d_dw_b.reshape(1, 2 * FFN_DIM), norm_g=d_w * (1.0 + sc), sh=d_sh, sc=d_w * norm_g, gate=d_gate)


def _mixer_forward(x, mod, norm_g, w, rope, tag):
    sh, sc, gate = mod
    h = _modnorm(x, norm_g, sc, sh, f"{tag}_norm")
    z = _matmul(h, w["w_in_t"], "nt", BF16, f"{tag}_in")
    ya = _gmlp_fwd(z, w["gain"], w["wtril"], w["bias_exp"], f"{tag}_gmlp")
    q, k, v = _qk_prep(z, rope[0], rope[1], w["gq"], w["gk"], w["seg"], f"{tag}_qk")
    outs, lses = zip(*[_attn_fwd(q[b], k[b], v[b], dil, f"{tag}_attn_d{dil}") for b, dil in enumerate(DILATIONS)])
    yb, lse, cat = _attn_merge(outs, lses, ya, f"{tag}_merge")
    y, x_new = _matmul(cat, w["w_out"], "nn", BF16, f"{tag}_out", resid=(x, gate))
    return x_new, (x, h, z, q, k, v, yb, lse, cat, y)


def _mixer_backward(dx_new, dy, d_gate, saved, mod, norm_g, w, rope, tag, emit, below):
    x, h, z, q, k, v, yb, lse, cat, _ = saved
    _, sc, _ = mod
    d_w_out = _matmul_tn_acc(cat, dy, f"{tag}_out_dw")
    dcat = _matmul(dy, w["w_out"], "nt", BF16, f"{tag}_out_dx")
    dz_a, d_sp_w, d_gain, d_bias_exp = _gmlp_bwd(z, dcat, w["gain"], w["wtril"], w["wtril_t"], w["bias_exp"], f"{tag}_gmlp_bwd")
    dyb = _subseq_views(dcat, A_WIDTH // B_WIDTH, f"{tag}_dyb_views")
    dqs, dks, dvs = zip(*[_attn_bwd(q[b], k[b], v[b], dyb[b], yb[b], lse[b], dil, f"{tag}_attn_bwd_d{dil}")
                          for b, dil in enumerate(DILATIONS)])
    dz_qkv, d_gq, d_gk = _qk_prep_bwd(z, dqs, dks, dvs, rope[0], rope[1], w["gq"], w["gk"], w["seg"], f"{tag}_qk_bwd")
    dz = jnp.concatenate([dz_a, dz_qkv], axis=1)
    d_w_in_t = _matmul_tn_acc(dz, h, f"{tag}_in_dw")
    token = emit([d_w_in_t, d_w_out])
    dh = _matmul(dz, w["w_in_t"], "nn", F32, f"{tag}_in_dx")
    dx, d_w, d_sh, *dy_below = _modnorm_bwd(x, dh, norm_g, _behind(sc, token), dx_new, below, f"{tag}_norm_bwd")
    return (dx, *dy_below), dict(
        vnorm_g=d_gain.reshape(A_GROUPS, GROUP_DIM), spatial_w=d_sp_w,
        spatial_b=d_bias_exp.reshape(CHUNK, A_GROUPS, GROUP_DIM).sum(-1).T,
        q_norm_g=d_gq.reshape(HEADS, HEAD_DIM).sum(0), k_norm_g=d_gk.reshape(HEADS, HEAD_DIM).sum(0),
        norm_g=d_w * (1.0 + sc), sh=d_sh, sc=d_w * norm_g, gate=d_gate)


def _conformer_forward(x, mod, norm_g, w, tag):
    sh, sc, gate = mod
    h = _modnorm(x, norm_g, sc, sh, f"{tag}_norm")
    p = _matmul(h, w["pw1_t"], "nt", BF16, f"{tag}_pw1", bias=w["pw1_b"])
    s, dc = _conformer_mid(p, w["dw_w"], w["dw_b"], w["ln_g"], w["ln_b"], f"{tag}_mid")
    y, x_new = _matmul(s, w["pw2"], "nn", BF16, f"{tag}_pw2", bias=w["pw2_b"], resid=(x, gate))
    return x_new, (x, h, p, dc, s, y)


def _conformer_backward(dx_new, dy, d_gate, saved, mod, norm_g, w, tag, emit, below):
    x, h, p, dc, s, _ = saved
    _, sc, _ = mod
    d_pw2 = _matmul_tn_acc(s, dy, f"{tag}_pw2_dw")
    d_pw2_b = _colsum_call(dy, f"{tag}_pw2_db")
    ds = _matmul(dy, w["pw2"], "nt", BF16, f"{tag}_pw2_dx")
    ddc, d_dw_w, d_dw_b, d_ln_g, d_ln_b = _conformer_mid_bwd(p, dc, ds, w["ln_g"], w["ln_b"], f"{tag}_mid_bwd")
    dp, d_pw1_b = _conformer_glu_bwd(p, ddc, w["dw_w"], f"{tag}_glu_bwd")
    d_pw1_t = _matmul_tn_acc(dp, h, f"{tag}_pw1_dw")
    token = emit([d_pw1_t, d_pw2])
    dh = _matmul(dp, w["pw1_t"], "nn", F32, f"{tag}_pw1_dx")
    dx, d_w, d_sh, *dy_below = _modnorm_bwd(x, dh, norm_g, _behind(sc, token), dx_new, below, f"{tag}_norm_bwd")
    return (dx, *dy_below), dict(pw1_b=d_pw1_b, dw_w=d_dw_w, dw_b=d_dw_b, ln_g=d_ln_g, ln_b=d_ln_b, pw2_b=d_pw2_b, norm_g=d_w * (1.0 + sc), sh=d_sh, sc=d_w * norm_g, gate=d_gate)


def _local_step(x, target, pos, mod, norm_mix_g, norm_ffn_g, mixer_w, conv_w, ffn_w, fetch, emit):
    d = D_MODEL
    inv_freq = 1.0 / (ROPE_THETA ** (jnp.arange(0, HEAD_DIM, 2, dtype=F32) / HEAD_DIM))
    inv_freq = jnp.tile(inv_freq, 2 * HEADS)[None, :]
    sign = jnp.tile(jnp.concatenate([-jnp.ones(HEAD_DIM // 2, F32), jnp.ones(HEAD_DIM // 2, F32)]), HEADS)[None, :]
    rope = _rope_tables(pos, inv_freq, sign, "rope_tables")
    mods = [[mod[l:l + 1, i * d:(i + 1) * d] for i in range(6)] for l in range(2)]
    mix = [(m[0], m[1], m[2]) for m in mods]
    ffn = [(m[3], m[4], m[5]) for m in mods]
    gm = [norm_mix_g[l:l + 1] for l in range(2)]
    gf = [norm_ffn_g[l:l + 1] for l in range(2)]

    mixer_w = {**mixer_w, **fetch("l0_mix", x)}
    x1, s_mix = _mixer_forward(x, mix[0], gm[0], mixer_w, rope, "l0_mix")
    ffn_w0 = {**ffn_w[0], **fetch("l0_ffn", x1)}
    x2, s_ffn0 = _ffn_forward(x1, ffn[0], gf[0], ffn_w0, "l0_ffn")
    conv_w = {**conv_w, **fetch("l1_conv", x2)}
    x3, s_conv = _conformer_forward(x2, mix[1], gm[1], conv_w, "l1_conv")
    ffn_w1 = {**ffn_w[1], **fetch("l1_ffn", x3)}
    x4, s_ffn1 = _ffn_forward(x3, ffn[1], gf[1], ffn_w1, "l1_ffn")
    below = lambda saved, m: (saved[-1], m[2])
    dx, loss, dy, dg = _loss_head(x4, target, below(s_ffn1, ffn[1]), "loss_head")
    (dx, dy, dg), g_ffn1 = _ffn_backward(dx, dy, dg, s_ffn1, ffn[1], gf[1], ffn_w1, "l1_ffn",
                                         functools.partial(emit, "l1_ffn"), below(s_conv, mix[1]))
    (dx, dy, dg), g_conv = _conformer_backward(dx, dy, dg, s_conv, mix[1], gm[1], conv_w, "l1_conv",
                                               functools.partial(emit, "l1_conv"), below(s_ffn0, ffn[0]))
    (dx, dy, dg), g_ffn0 = _ffn_backward(dx, dy, dg, s_ffn0, ffn[0], gf[0], ffn_w0, "l0_ffn",
                                         functools.partial(emit, "l0_ffn"), below(s_mix, mix[0]))
    (dx,), g_mix = _mixer_backward(dx, dy, dg, s_mix, mix[0], gm[0], mixer_w, rope, "l0_mix",
                                   functools.partial(emit, "l0_mix"), None)
    blocks = [g_mix, g_ffn0, g_conv, g_ffn1]
    dmod = jnp.stack([jnp.concatenate([a["sh"], a["sc"], a["gate"], b["sh"], b["sc"], b["gate"]], axis=1)[0]
                      for a, b in ((g_mix, g_ffn0), (g_conv, g_ffn1))])
    return loss, dx, dmod, blocks


def _pack(arrs, rows=8):
    flat = jnp.concatenate([a.reshape(-1).astype(F32) for a in arrs])
    n = flat.shape[0]
    cols = -(-n // (rows * 128)) * 128
    return jnp.pad(flat, (0, rows * cols - n)).reshape(rows, cols)


def _unpack(flat, shapes):
    out, off = [], 0
    for shp in shapes:
        n = math.prod(shp)
        out.append(flat[..., off:off + n].reshape(flat.shape[:-1] + tuple(shp)))
        off += n
    return out


def _take_block(a, idx, size, axis):
    return lax.dynamic_slice_in_dim(a, idx * size, size, axis)


def kernel(x, c, positions, ada_w, ada_b, norm_mix_g, norm_ffn_g, ab_w_in, a_vnorm_g, a_spatial_w, a_spatial_b, b_q_norm_g, b_k_norm_g, ab_w_out, conv_pw1_w, conv_pw1_b, conv_dw_w, conv_dw_b, conv_ln_g, conv_ln_b, conv_pw2_w, conv_pw2_b, ffn_up_w, ffn_dw_w, ffn_dw_b, ffn_down_w, loss_target, m_ada_w, m_ada_b, m_norm_mix_g, m_norm_ffn_g, m_ab_w_in, m_a_vnorm_g, m_a_spatial_w, m_a_spatial_b, m_b_q_norm_g, m_b_k_norm_g, m_ab_w_out, m_conv_pw1_w, m_conv_pw1_b, m_conv_dw_w, m_conv_dw_b, m_conv_ln_g, m_conv_ln_b, m_conv_pw2_w, m_conv_pw2_b, m_ffn_up_w, m_ffn_dw_w, m_ffn_dw_b, m_ffn_down_w, v_ada_w, v_ada_b, v_norm_mix_g, v_norm_ffn_g, v_ab_w_in, v_a_vnorm_g, v_a_spatial_w, v_a_spatial_b, v_b_q_norm_g, v_b_k_norm_g, v_ab_w_out, v_conv_pw1_w, v_conv_pw1_b, v_conv_dw_w, v_conv_dw_b, v_conv_ln_g, v_conv_ln_b, v_conv_pw2_w, v_conv_pw2_b, v_ffn_up_w, v_ffn_dw_w, v_ffn_dw_b, v_ffn_down_w):
    weights = dict(ada_w=ada_w, ada_b=ada_b, norm_mix_g=norm_mix_g, norm_ffn_g=norm_ffn_g, ab_w_in=ab_w_in, a_vnorm_g=a_vnorm_g, a_spatial_w=a_spatial_w, a_spatial_b=a_spatial_b, b_q_norm_g=b_q_norm_g, b_k_norm_g=b_k_norm_g, ab_w_out=ab_w_out, conv_pw1_w=conv_pw1_w, conv_pw1_b=conv_pw1_b, conv_dw_w=conv_dw_w, conv_dw_b=conv_dw_b, conv_ln_g=conv_ln_g, conv_ln_b=conv_ln_b, conv_pw2_w=conv_pw2_w, conv_pw2_b=conv_pw2_b, ffn_up_w=ffn_up_w, ffn_dw_w=ffn_dw_w, ffn_dw_b=ffn_dw_b, ffn_down_w=ffn_down_w)
    mom1 = dict(ada_w=m_ada_w, ada_b=m_ada_b, norm_mix_g=m_norm_mix_g, norm_ffn_g=m_norm_ffn_g, ab_w_in=m_ab_w_in, a_vnorm_g=m_a_vnorm_g, a_spatial_w=m_a_spatial_w, a_spatial_b=m_a_spatial_b, b_q_norm_g=m_b_q_norm_g, b_k_norm_g=m_b_k_norm_g, ab_w_out=m_ab_w_out, conv_pw1_w=m_conv_pw1_w, conv_pw1_b=m_conv_pw1_b, conv_dw_w=m_conv_dw_w, conv_dw_b=m_conv_dw_b, conv_ln_g=m_conv_ln_g, conv_ln_b=m_conv_ln_b, conv_pw2_w=m_conv_pw2_w, conv_pw2_b=m_conv_pw2_b, ffn_up_w=m_ffn_up_w, ffn_dw_w=m_ffn_dw_w, ffn_dw_b=m_ffn_dw_b, ffn_down_w=m_ffn_down_w)
    mom2 = dict(ada_w=v_ada_w, ada_b=v_ada_b, norm_mix_g=v_norm_mix_g, norm_ffn_g=v_norm_ffn_g, ab_w_in=v_ab_w_in, a_vnorm_g=v_a_vnorm_g, a_spatial_w=v_a_spatial_w, a_spatial_b=v_a_spatial_b, b_q_norm_g=v_b_q_norm_g, b_k_norm_g=v_b_k_norm_g, ab_w_out=v_ab_w_out, conv_pw1_w=v_conv_pw1_w, conv_pw1_b=v_conv_pw1_b, conv_dw_w=v_conv_dw_w, conv_dw_b=v_conv_dw_b, conv_ln_g=v_conv_ln_g, conv_ln_b=v_conv_ln_b, conv_pw2_w=v_conv_pw2_w, conv_pw2_b=v_conv_pw2_b, ffn_up_w=v_ffn_up_w, ffn_dw_w=v_ffn_dw_w, ffn_dw_b=v_ffn_dw_b, ffn_down_w=v_ffn_down_w)
    order = list(weights)
    d, f2 = D_MODEL, 2 * FFN_DIM
    t = x.shape[1]
    me = 4 * lax.axis_index("x") + 2 * lax.axis_index("y") + lax.axis_index("c")
    for window, dil in PATTERNS:
        assert window // dil == Q_BLOCK and t % (dil * Q_BLOCK) == 0

    small_in = [c[0], conv_pw1_b[0], conv_dw_w[0], conv_dw_b[0], conv_ln_g[0], conv_ln_b[0], conv_pw2_b[0], ffn_dw_w]
    g1 = _all_gather_vmem(_pack(small_in, rows=8), "gather_small").reshape(N_DEV, -1)
    c_all, pw1_b, dw_w, dw_b, ln_g, ln_b, pw2_b, fdw_w = _unpack(g1, [a.shape for a in small_in])
    pw1_b, dw_b, ln_g, ln_b, pw2_b = [a.reshape(1, -1) for a in (pw1_b, dw_b, ln_g, ln_b, pw2_b)]
    dw_w = dw_w.transpose(1, 0, 2).reshape(CONV_WIDTH, d)
    fdw_w = fdw_w.transpose(1, 2, 0, 3).reshape(2, FFN_CONV_WIDTH, f2)

    c16 = jnp.pad(c_all, ((0, 2 * N_DEV - c_all.shape[0]), (0, 0)))
    part = jnp.concatenate([_ada_fwd(c16, ada_w[l], f"ada_fwd{l}")[:N_DEV] for l in range(2)], axis=1)
    g2 = _all_gather_vmem(part, "gather_mod").reshape(N_DEV, N_DEV, 2, -1)
    mod = lax.dynamic_index_in_dim(g2, me, axis=1, keepdims=False).transpose(1, 0, 2).reshape(2, 6 * d) + ada_b

    stages = dict(l0_mix=[ab_w_in[0].T, ab_w_out[0]], l0_ffn=[ffn_up_w[0].T, ffn_down_w[0]],
                  l1_conv=[conv_pw1_w[0].T, conv_pw2_w[0]], l1_ffn=[ffn_up_w[1].T, ffn_down_w[1]])
    stages = {k: [s.astype(BF16) for s in v] for k, v in stages.items()}
    names = dict(l0_mix=("w_in_t", "w_out"), l0_ffn=("up_t", "down"), l1_conv=("pw1_t", "pw2"), l1_ffn=("up_t", "down"))
    ready = {"l0_mix": [a.reshape(-1, d) for a in _all_gather_hbm(stages["l0_mix"], "gather_mixer_weights")]}
    behind = (*ready["l0_mix"], mod)
    arriving = {}
    for stage, group in (("l0_ffn", ("l0_ffn",)), ("l1_conv", ("l1_conv", "l1_ffn"))):
        srcs = [s for g in group for s in stages[g]]
        arriving[stage], token = _pushes_start(
            srcs, _landing_zones(srcs, behind, f"gather_{stage}_zones"), False, f"gather_{stage}_start")
        behind = (token,)
        mod = mod + token[0:1, 0:1]

    def fetch(stage, after):
        if stage in arriving:
            full = [a.reshape(-1, d) for a in _pushes_wait(arriving[stage], after, f"gather_{stage}_wait")[1]]
            ready[stage] = full[:2]
            if stage == "l1_conv":
                ready["l1_ffn"] = full[2:]
        return dict(zip(names[stage], ready[stage]))

    causal = jnp.tril(jnp.ones((CHUNK, CHUNK), bool))
    wtril = jnp.where(causal[None], a_spatial_w[0], 0.0)
    mixer_w = dict(
        gain=a_vnorm_g[0].reshape(1, A_WIDTH), wtril=wtril.astype(BF16),
        wtril_t=wtril.transpose(0, 2, 1).astype(BF16),
        bias_exp=jnp.repeat(a_spatial_b[0].T, GROUP_DIM, axis=1),
        gq=jnp.tile(b_q_norm_g[0], HEADS)[None, :], gk=jnp.tile(b_k_norm_g[0], HEADS)[None, :],
        seg=jnp.kron(jnp.eye(HEADS, dtype=BF16), jnp.ones((HEAD_DIM, HEAD_DIM), BF16)))
    conv_w = dict(pw1_b=pw1_b, dw_w=dw_w, dw_b=dw_b, ln_g=ln_g, ln_b=ln_b, pw2_b=pw2_b)
    ffn_w = [dict(dw_w=fdw_w[l].reshape(FFN_CONV_WIDTH, 2, FFN_DIM).transpose(1, 0, 2), dw_b=ffn_dw_b[l].reshape(2, 1, FFN_DIM))
             for l in range(2)]

    leaving = {}

    def emit(stage, grads):
        blocks = [g.reshape(N_DEV, g.shape[0] // N_DEV, d) for g in grads]
        leaving[stage], token = _pushes_start(
            blocks, [lax.empty(b.shape, b.dtype) for b in blocks], True, f"reduce_{stage}_start")
        return token

    loss, dx, dmod, (g_mix, g_ffn0, g_conv, g_ffn1) = _local_step(
        x[0], loss_target[0], positions[0].astype(F32)[:, None], mod, norm_mix_g, norm_ffn_g, mixer_w, conv_w, ffn_w,
        fetch, emit)

    me_op = me.astype(jnp.int32).reshape(1)

    def reduced(stage, after):
        blocks, lands = _pushes_wait(leaving[stage], after, f"reduce_{stage}_wait")
        return [_sum_with_own(b, a, me_op, f"reduce_{stage}_sum{i}") for i, (b, a) in enumerate(zip(blocks, lands))]

    (r_up_t1, r_down1), (r_pw1_t, r_pw2), (r_up_t0, r_down0) = [reduced(s, dx) for s in ("l1_ffn", "l1_conv", "l0_ffn")]

    small_g = [
        dmod, jnp.concatenate([g_mix["norm_g"], g_conv["norm_g"]]), jnp.concatenate([g_ffn0["norm_g"], g_ffn1["norm_g"]]),
        g_mix["vnorm_g"], g_mix["spatial_w"], g_mix["spatial_b"], g_mix["q_norm_g"], g_mix["k_norm_g"],
        g_conv["pw1_b"], g_conv["dw_w"], g_conv["dw_b"], g_conv["ln_g"], g_conv["ln_b"], g_conv["pw2_b"],
        jnp.stack([g_ffn0["dw_w"], g_ffn1["dw_w"]]), jnp.concatenate([g_ffn0["dw_b"], g_ffn1["dw_b"]])]
    packed = _pack(small_g, rows=8)
    small_leaving, token = _pushes_start([packed], [lax.empty((N_DEV,) + packed.shape, F32)], False, "gather_small_grads_start")

    grads = dict(conv_pw2_w=r_pw2[None], ffn_down_w=jnp.stack([r_down0, r_down1]))
    grads_t = dict(conv_pw1_w=r_pw1_t[None], ffn_up_w=jnp.stack([r_up_t0, r_up_t1]))
    flip = lambda a: jnp.swapaxes(a, 1, 2)
    delta, new_m, new_v = {}, {}, {}

    def update(name, behind):
        if name in grads_t:
            grads[name] = flip(grads_t[name])
            res = _adamw(flip(weights[name]), grads_t[name], flip(mom1[name]), flip(mom2[name]), behind, f"adamw_{name}")
            delta[name], new_m[name], new_v[name] = [flip(r) for r in res]
        else:
            delta[name], new_m[name], new_v[name] = _adamw(
                weights[name], grads[name], mom1[name], mom2[name], behind, f"adamw_{name}")

    for name in ("conv_pw1_w", "conv_pw2_w", "ffn_up_w", "ffn_down_w"):
        update(name, token)
    r_in_t, r_out = reduced("l0_mix", new_v["ffn_down_w"])
    grads_t["ab_w_in"], grads["ab_w_out"] = r_in_t[None], r_out[None]
    update("ab_w_in", token)
    update("ab_w_out", token)

    (packed,), (landed,) = _pushes_wait(small_leaving, tuple(new_v.values()), "gather_small_grads_wait")
    total = _sum_in_device_order(packed, landed, me_op, "sum_small_grads")
    (s_dmod, s_mix_g, s_ffn_g, s_vnorm, s_sp_w, s_sp_b, s_gq, s_gk, s_pw1_b, s_dw_w, s_dw_b, s_ln_g, s_ln_b,
     s_pw2_b, s_fdw_w, s_fdw_b) = _unpack(total.reshape(-1), [a.shape for a in small_g])
    dmod_all = lax.dynamic_update_slice(
        landed.reshape(N_DEV, -1)[:, :dmod.size].reshape((N_DEV,) + dmod.shape), dmod[None], (me, 0, 0))
    n_ada = ada_w.shape[2]
    dmod16 = jnp.pad(_take_block(dmod_all, me, n_ada, 2), ((0, N_DEV), (0, 0), (0, 0)))
    grads.update(
        ada_w=jnp.stack([_ada_bwd(c16, dmod16[:, l], f"ada_bwd{l}") for l in range(2)]),
        ada_b=s_dmod, norm_mix_g=s_mix_g, norm_ffn_g=s_ffn_g,
        a_vnorm_g=s_vnorm[None], a_spatial_w=s_sp_w[None], a_spatial_b=s_sp_b[None], b_q_norm_g=s_gq[None],
        b_k_norm_g=s_gk[None],
        conv_pw1_b=_take_block(s_pw1_b, me, conv_pw1_b.shape[1], 1),
        conv_dw_w=_take_block(s_dw_w, me, conv_dw_w.shape[2], 1)[None],
        conv_dw_b=_take_block(s_dw_b, me, conv_dw_b.shape[1], 1), conv_ln_g=_take_block(s_ln_g, me, conv_ln_g.shape[1], 1),
        conv_ln_b=_take_block(s_ln_b, me, conv_ln_b.shape[1], 1),
        conv_pw2_b=_take_block(s_pw2_b, me, conv_pw2_b.shape[1], 1),
        ffn_dw_w=_take_block(s_fdw_w, me, ffn_dw_w.shape[2], 2), ffn_dw_b=s_fdw_b)
    update("ada_w", None)
    large = ("ada_w", "conv_pw1_w", "conv_pw2_w", "ffn_up_w", "ffn_down_w", "ab_w_in", "ab_w_out")
    small = [n for n in order if n not in large]
    res = _adamw_small(*[[src[n] for n in small] for src in (weights, grads, mom1, mom2)], "adamw_small")
    for dst, arrs in zip((delta, new_m, new_v), res):
        dst.update(zip(small, arrs))

    loss = lax.psum(loss[0, 0], ("x", "y", "c"))
    return (loss, dx[None], *[grads[n] for n in order], *[delta[n] for n in order],
            *[new_m[n] for n in order], *[new_v[n] for n in order])
```

```python
import functools
import math

import jax
import jax.numpy as jnp
from jax import lax
from jax.experimental import pallas as pl
from jax.experimental.pallas import tpu as pltpu

F32 = jnp.float32
BF16 = jnp.bfloat16
MESH = pl.DeviceIdType.MESH

D_MODEL = 1024
A_WIDTH = 512
A_GROUPS = 4
GROUP_DIM = 128
CHUNK = 128
B_WIDTH = 512
HEADS = 8
HEAD_DIM = 64
PATTERNS = ((128, 1), (512, 4), (2048, 16))
Q_BLOCK = 128
ROPE_THETA = 10000.0
AB_IN = 2560
CONV_WIDTH = 31
FFN_DIM = 2816
FFN_CONV_WIDTH = 3
EPS = 1e-6
NEG = -1e30
N_DEV = 8
ADAM_LR, ADAM_B1, ADAM_B2, ADAM_EPS, ADAM_WD, ADAM_STEP = 0.001, 0.9, 0.999, 1e-08, 0.01, 10

V7X_VMEM_LIMIT = 56 * 2**20
FFN_HALO = 16
CONV_HALO = 32

_NN = (((1,), (0,)), ((), ()))
_NT = (((1,), (1,)), ((), ()))
_TN = (((0,), (0,)), ((), ()))


def _tile(n, prefs=(512, 256, 128)):
    for t in prefs:
        if n % t == 0:
            return t
    return n


def _row_tile(n, cap=512):
    best = n
    for t in range(8, min(n, cap) + 1, 8):
        if n % t == 0:
            best = t
    return best if best <= cap else n


def _params(*sem):
    return pltpu.CompilerParams(dimension_semantics=sem, vmem_limit_bytes=V7X_VMEM_LIMIT)


def _dot(a, b, dims):
    return lax.dot_general(a, b, dims, preferred_element_type=F32)


def _sigmoid(x):
    return 1.0 / (1.0 + jnp.exp(-x))


def _gelu(x):
    return 0.5 * x * (1.0 + lax.erf(x * (2.0 ** -0.5)))


def _gelu_grad(x):
    return 0.5 * (1.0 + lax.erf(x * (2.0 ** -0.5))) + x * jnp.exp(-0.5 * x * x) * (1.0 / math.sqrt(2.0 * math.pi))


def _colsum(v):
    return jnp.sum(v, axis=0, keepdims=True)


MATMUL_VMEM_BUDGET = 40 * 2**20


def _matmul_tiles(m, n, k, out_bytes, with_resid):
    def options(dim):
        opts = [t for t in (1024, 512, 256, 128) if dim % t == 0]
        return opts + [dim] if dim <= 4096 and dim not in opts else opts

    best = None
    for tm in options(m):
        for tn in options(n):
            need = 4 * (tm * k + k * tn) + tm * tn * (4 + 2 * out_bytes) + (24 * tm * tn if with_resid else 0)
            if need <= MATMUL_VMEM_BUDGET and (best is None or tm * tn / (tm + tn) > best[0]):
                best = (tm * tn / (tm + tn), tm, tn)
    return best[1], best[2]


def _matmul_tn_acc(a, b, name, tk=1024):
    squeeze = a.ndim == 2
    a3 = a[None] if squeeze else a
    p_, t, m = a3.shape
    n = b.shape[1]
    nk = t // tk

    def body(a_ref, b_ref, o_ref, acc_ref):
        kt = pl.program_id(1)

        @pl.when(kt == 0)
        def _():
            acc_ref[...] = jnp.zeros_like(acc_ref)

        acc_ref[...] += _dot(a_ref[...], b_ref[...], _TN)

        @pl.when(kt == nk - 1)
        def _():
            o_ref[...] = acc_ref[...].astype(BF16)

    out = pl.pallas_call(
        body, name=name, grid=(p_, nk),
        in_specs=[pl.BlockSpec((None, tk, m), lambda p, kt: (p, kt, 0)), pl.BlockSpec((tk, n), lambda p, kt: (kt, 0))],
        out_specs=pl.BlockSpec((None, m, n), lambda p, kt: (p, 0, 0)), out_shape=jax.ShapeDtypeStruct((p_, m, n), BF16),
        scratch_shapes=[pltpu.VMEM((m, n), F32)], compiler_params=_params("parallel", "arbitrary"),
    )(a3, b)
    return out[0] if squeeze else out


def _matmul(a, b, mode, out_dtype, name, bias=None, resid=None):
    if mode == "nn":
        (m, k), (_, n) = a.shape, b.shape
    elif mode == "nt":
        (m, k), (n, _) = a.shape, b.shape
    else:
        (k, m), (_, n) = a.shape, b.shape
    tm, tn = _matmul_tiles(m, n, k, jnp.dtype(out_dtype).itemsize, resid is not None)
    dims = {"nn": _NN, "nt": _NT, "tn": _TN}[mode]
    a_spec = pl.BlockSpec((k, tm), lambda i, j: (0, i)) if mode == "tn" else pl.BlockSpec((tm, k), lambda i, j: (i, 0))
    b_spec = pl.BlockSpec((tn, k), lambda i, j: (j, 0)) if mode == "nt" else pl.BlockSpec((k, tn), lambda i, j: (0, j))
    in_specs, args = [a_spec, b_spec], [a, b]
    row_spec = pl.BlockSpec((1, tn), lambda i, j: (0, j))
    tile_spec = pl.BlockSpec((tm, tn), lambda i, j: (i, j))
    if bias is not None:
        in_specs.append(row_spec)
        args.append(bias)
    if resid is not None:
        in_specs += [tile_spec, row_spec]
        args += list(resid)
    out_shape = [jax.ShapeDtypeStruct((m, n), out_dtype)]
    out_specs = [tile_spec]
    if resid is not None:
        out_shape.append(jax.ShapeDtypeStruct((m, n), F32))
        out_specs.append(tile_spec)

    def body(*refs):
        a_ref, b_ref = refs[0], refs[1]
        pos = 2
        acc = _dot(a_ref[...], b_ref[...], dims)
        if bias is not None:
            acc = acc + refs[pos][...]
            pos += 1
        if resid is not None:
            x_ref, g_ref = refs[pos], refs[pos + 1]
            pos += 2
        refs[pos][...] = acc.astype(out_dtype)
        if resid is not None:
            refs[pos + 1][...] = x_ref[...] + g_ref[...] * acc

    outs = pl.pallas_call(
        body, name=name, grid=(m // tm, n // tn), in_specs=in_specs, out_specs=out_specs, out_shape=out_shape,
        compiler_params=_params("parallel", "parallel"),
    )(*args)
    return outs if resid is not None else outs[0]


NORM_TM = (1024, 512, 256, 128)


def _modnorm(x, g, sc, sh, name):
    t, d = x.shape
    tm = _tile(t, NORM_TM)
    row = pl.BlockSpec((1, d), lambda i: (0, 0))
    blk = pl.BlockSpec((tm, d), lambda i: (i, 0))

    def body(x_ref, g_ref, sc_ref, sh_ref, o_ref):
        x = x_ref[...]
        r = lax.rsqrt(jnp.mean(x * x, axis=-1, keepdims=True) + EPS)
        o_ref[...] = ((x * r) * g_ref[...] * (1.0 + sc_ref[...]) + sh_ref[...]).astype(BF16)

    return pl.pallas_call(
        body, name=name, grid=(t // tm,), in_specs=[blk, row, row, row], out_specs=blk,
        out_shape=jax.ShapeDtypeStruct((t, d), BF16), compiler_params=_params("parallel"),
    )(x, g, sc, sh)


def _gate_bwd_tile(dx, y_ref, gate_ref, dy_ref, dgate_ref, first):
    @pl.when(first)
    def _():
        dgate_ref[...] = jnp.zeros_like(dgate_ref)

    dy_ref[...] = (dx * gate_ref[...]).astype(BF16)
    dgate_ref[...] += _colsum(dx * y_ref[...].astype(F32))


def _modnorm_bwd(x, dh, g, sc, dres, below, name):
    t, d = x.shape
    tm = _tile(t, NORM_TM)
    row = pl.BlockSpec((1, d), lambda i: (0, 0))
    blk = pl.BlockSpec((tm, d), lambda i: (i, 0))

    def body(x_ref, dh_ref, g_ref, sc_ref, dres_ref, *rest):
        dx_ref, dw_ref, dsh_ref = rest[-5:-2] if below else rest
        first = pl.program_id(0) == 0

        @pl.when(first)
        def _():
            dw_ref[...] = jnp.zeros_like(dw_ref)
            dsh_ref[...] = jnp.zeros_like(dsh_ref)

        x = x_ref[...]
        dh = dh_ref[...].astype(F32)
        r = lax.rsqrt(jnp.mean(x * x, axis=-1, keepdims=True) + EPS)
        xn = x * r
        dxn = dh * (g_ref[...] * (1.0 + sc_ref[...]))
        dx = dres_ref[...] + r * (dxn - xn * jnp.mean(dxn * xn, axis=-1, keepdims=True))
        dx_ref[...] = dx
        dw_ref[...] += _colsum(dh * xn)
        dsh_ref[...] += _colsum(dh)
        if below:
            _gate_bwd_tile(dx, rest[0], rest[1], rest[-2], rest[-1], first)

    row_out = jax.ShapeDtypeStruct((1, d), F32)
    return pl.pallas_call(
        body, name=name, grid=(t // tm,), in_specs=[blk, blk, row, row, blk] + ([blk, row] if below else []),
        out_specs=[blk, row, row] + ([blk, row] if below else []),
        out_shape=[jax.ShapeDtypeStruct((t, d), F32), row_out, row_out]
        + ([jax.ShapeDtypeStruct((t, d), BF16), row_out] if below else []),
        compiler_params=_params("arbitrary"),
    )(x, dh, g, sc, dres, *(below or ()))


def _loss_head(y, target, below, name):
    t, d = y.shape
    tm = _tile(t, NORM_TM)
    blk = pl.BlockSpec((tm, d), lambda i: (i, 0))
    row = pl.BlockSpec((1, d), lambda i: (0, 0))
    one = pl.BlockSpec((1, 1), lambda i: (0, 0))
    steps = t // tm

    def body(y_ref, t_ref, yb_ref, gate_ref, dx_ref, loss_ref, dy_ref, dgate_ref, acc_ref):
        first = pl.program_id(0) == 0

        @pl.when(first)
        def _():
            acc_ref[...] = jnp.zeros_like(acc_ref)

        e = y_ref[...] - t_ref[...]
        dx = e * (1.0 / d)
        dx_ref[...] = dx
        acc_ref[...] += _colsum(e * e)
        _gate_bwd_tile(dx, yb_ref, gate_ref, dy_ref, dgate_ref, first)

        @pl.when(pl.program_id(0) == steps - 1)
        def _():
            loss_ref[...] = jnp.sum(acc_ref[...], axis=1, keepdims=True) * (0.5 / d)

    return pl.pallas_call(
        body, name=name, grid=(steps,), in_specs=[blk, blk, blk, row], out_specs=[blk, one, blk, row],
        out_shape=[jax.ShapeDtypeStruct((t, d), F32), jax.ShapeDtypeStruct((1, 1), F32),
                   jax.ShapeDtypeStruct((t, d), BF16), jax.ShapeDtypeStruct((1, d), F32)],
        scratch_shapes=[pltpu.VMEM((1, d), F32)], compiler_params=_params("arbitrary"),
    )(y, target, *below)


GMLP_TM = 512


def _group_norm(vg, gain):
    mu = jnp.mean(vg, axis=-1, keepdims=True)
    xc = vg - mu
    rstd = lax.rsqrt(jnp.mean(xc * xc, axis=-1, keepdims=True) + EPS)
    xhat = xc * rstd
    return xhat, rstd, xhat * gain


def _gmlp_fwd(z, gain, wtril, bias_exp, name):
    t = z.shape[0]
    tm = _tile(t, (GMLP_TM,))
    zu = pl.BlockSpec((tm, A_WIDTH), lambda i: (i, 0))
    zv = pl.BlockSpec((tm, A_WIDTH), lambda i: (i, 1))
    full2 = lambda shp: pl.BlockSpec(shp, lambda i: (0, 0))
    w_spec = pl.BlockSpec((A_GROUPS, CHUNK, CHUNK), lambda i: (0, 0, 0))

    def body(zu_ref, zv_ref, gain_ref, w_ref, b_ref, ya_ref):
        for c in range(tm // CHUNK):
            rows = slice(c * CHUNK, (c + 1) * CHUNK)
            ua = _gelu(zu_ref[rows, :].astype(F32))
            vg = _gelu(zv_ref[rows, :].astype(F32))
            for g in range(A_GROUPS):
                sl = slice(g * GROUP_DIM, (g + 1) * GROUP_DIM)
                _, _, vn = _group_norm(vg[:, sl], gain_ref[:, sl])
                f = _dot(w_ref[g], vn.astype(BF16), _NN) + b_ref[:, sl]
                ya_ref[rows, sl] = (ua[:, sl] * f).astype(BF16)

    return pl.pallas_call(
        body, name=name, grid=(t // tm,),
        in_specs=[zu, zv, full2((1, A_WIDTH)), w_spec, full2((CHUNK, A_WIDTH))], out_specs=zu,
        out_shape=jax.ShapeDtypeStruct((t, A_WIDTH + B_WIDTH), BF16), compiler_params=_params("parallel"),
    )(z, z, gain, wtril, bias_exp)


def _gmlp_bwd(z, dcat, gain, wtril, wtril_t, bias_exp, name):
    t = z.shape[0]
    tm = _tile(t, (GMLP_TM,))
    zu = pl.BlockSpec((tm, A_WIDTH), lambda i: (i, 0))
    zv = pl.BlockSpec((tm, A_WIDTH), lambda i: (i, 1))
    full2 = lambda shp: pl.BlockSpec(shp, lambda i: (0, 0))
    w_spec = pl.BlockSpec((A_GROUPS, CHUNK, CHUNK), lambda i: (0, 0, 0))
    dz_spec = pl.BlockSpec((tm, 2 * A_WIDTH), lambda i: (i, 0))

    def body(zu_ref, zv_ref, dya_ref, gain_ref, w_ref, wt_ref, b_ref, dz_ref, dw_ref, dgain_ref, dbias_ref):
        @pl.when(pl.program_id(0) == 0)
        def _():
            dw_ref[...] = jnp.zeros_like(dw_ref)
            dgain_ref[...] = jnp.zeros_like(dgain_ref)
            dbias_ref[...] = jnp.zeros_like(dbias_ref)

        row = lax.broadcasted_iota(jnp.int32, (CHUNK, CHUNK), 0)
        col = lax.broadcasted_iota(jnp.int32, (CHUNK, CHUNK), 1)
        for c in range(tm // CHUNK):
            rows = slice(c * CHUNK, (c + 1) * CHUNK)
            zu_v = zu_ref[rows, :].astype(F32)
            zv_v = zv_ref[rows, :].astype(F32)
            dya = dya_ref[rows, :].astype(F32)
            ua = _gelu(zu_v)
            vg = _gelu(zv_v)
            for g in range(A_GROUPS):
                sl = slice(g * GROUP_DIM, (g + 1) * GROUP_DIM)
                gain_g = gain_ref[:, sl]
                xhat, rstd, vn = _group_norm(vg[:, sl], gain_g)
                vn16 = vn.astype(BF16)
                f = _dot(w_ref[g], vn16, _NN) + b_ref[:, sl]
                df = dya[:, sl] * ua[:, sl]
                df16 = df.astype(BF16)
                dz_ref[rows, sl] = (dya[:, sl] * f * _gelu_grad(zu_v[:, sl])).astype(BF16)
                dw_ref[g] += jnp.where(row >= col, _dot(df16, vn16, _NT), 0.0)
                dvn = _dot(wt_ref[g], df16, _NN)
                dgain_ref[:, sl] += _colsum(dvn * xhat)
                dxh = dvn * gain_g
                dvg = rstd * (dxh - jnp.mean(dxh, axis=-1, keepdims=True) - xhat * jnp.mean(dxh * xhat, axis=-1, keepdims=True))
                dz_ref[rows, A_WIDTH + g * GROUP_DIM:A_WIDTH + (g + 1) * GROUP_DIM] = (dvg * _gelu_grad(zv_v[:, sl])).astype(BF16)
                dbias_ref[:, sl] += df

    return pl.pallas_call(
        body, name=name, grid=(t // tm,),
        in_specs=[zu, zv, zu, full2((1, A_WIDTH)), w_spec, w_spec, full2((CHUNK, A_WIDTH))],
        out_specs=[dz_spec, w_spec, full2((1, A_WIDTH)), full2((CHUNK, A_WIDTH))],
        out_shape=[jax.ShapeDtypeStruct((t, 2 * A_WIDTH), BF16), jax.ShapeDtypeStruct((A_GROUPS, CHUNK, CHUNK), F32),
                   jax.ShapeDtypeStruct((1, A_WIDTH), F32), jax.ShapeDtypeStruct((CHUNK, A_WIDTH), F32)],
        compiler_params=_params("arbitrary"),
    )(z, z, dcat, gain, wtril, wtril_t, bias_exp)


def _rope_tables(pos, inv_freq, sign, name):
    t = pos.shape[0]
    tm = _tile(t)
    row = pl.BlockSpec((1, B_WIDTH), lambda i: (0, 0))
    blk = pl.BlockSpec((tm, B_WIDTH), lambda i: (i, 0))

    def body(pos_ref, f_ref, s_ref, cos_ref, sin_ref):
        ang = pos_ref[...] * f_ref[:, 0:LANES]
        cos_ref[...] = jnp.tile(jnp.cos(ang), (1, B_WIDTH // LANES))
        sin_ref[...] = jnp.tile(jnp.sin(ang) * s_ref[:, 0:LANES], (1, B_WIDTH // LANES))

    return pl.pallas_call(
        body, name=name, grid=(t // tm,), in_specs=[pl.BlockSpec((tm, 1), lambda i: (i, 0)), row, row],
        out_specs=[blk, blk], out_shape=[jax.ShapeDtypeStruct((t, B_WIDTH), F32)] * 2,
        compiler_params=_params("parallel"),
    )(pos, inv_freq, sign)


def _head_sum(v, seg):
    hi = v.astype(BF16)
    lo = (v - hi.astype(F32)).astype(BF16)
    return _dot(hi, seg, _NN) + _dot(lo, seg, _NN)


def _swap_halves(v):
    lane = lax.broadcasted_iota(jnp.int32, v.shape, 1)
    return jnp.where((lane & (HEAD_DIM - 1)) < HEAD_DIM // 2,pltpu.roll(v, B_WIDTH - HEAD_DIM // 2, 1), pltpu.roll(v, HEAD_DIM // 2, 1))


DILATIONS = tuple(dil for _, dil in PATTERNS)
SUBSEQ_TM = 512
LANES = 128


def _subseq_shape(t, dil):
    return (t // dil, dil * B_WIDTH)


def _subseq_spec(tm, dil):
    return pl.BlockSpec((tm // dil, dil * B_WIDTH), lambda i: (i, 0))


def _to_subseq(x, scr_ref, dil):
    if dil == 1:
        return x
    tm, w = x.shape
    for c in range(w // LANES):
        scr_ref[c * tm:(c + 1) * tm, :] = x[:, c * LANES:(c + 1) * LANES]
    return jnp.concatenate([scr_ref[pl.ds(c * tm + r, tm // dil, stride=dil), :]
                            for r in range(dil) for c in range(w // LANES)], axis=1)


def _from_subseq(y, scr_ref, dil):
    if dil == 1:
        return y
    n, w = y.shape[0], y.shape[1] // dil
    tm = n * dil
    for r in range(dil):
        for c in range(w // LANES):
            scr_ref[pl.ds(c * tm + r, n, stride=dil), :] = y[:, r * w + c * LANES:r * w + (c + 1) * LANES]
    return jnp.concatenate([scr_ref[c * tm:(c + 1) * tm, :] for c in range(w // LANES)], axis=1)


def _subseq_scratch(tm):
    return pltpu.VMEM((B_WIDTH // LANES * tm, LANES), F32)


def _qk_prep(z, cos_t, sin_t, gq, gk, seg, name):
    t = z.shape[0]
    tm = _tile(t, (SUBSEQ_TM,))
    col = lambda c: pl.BlockSpec((tm, B_WIDTH), lambda i: (i, c))
    row = pl.BlockSpec((1, B_WIDTH), lambda i: (0, 0))
    blk = col(0)
    nd = len(DILATIONS)

    def body(q_ref, k_ref, v_ref, cos_ref, sin_ref, gq_ref, gk_ref, seg_ref, *rest):
        out_refs, scr_ref = rest[:-1], rest[-1]

        def norm_rot(x, g):
            r = lax.rsqrt(_head_sum(x * x, seg_ref[...]) * (1.0 / HEAD_DIM) + EPS)
            xn = x * r * g
            return xn * cos_ref[...] + _swap_halves(xn) * sin_ref[...]

        vals = (norm_rot(q_ref[...].astype(F32), gq_ref[...]), norm_rot(k_ref[...].astype(F32), gk_ref[...]),
                v_ref[...].astype(F32))
        for a, val in enumerate(vals):
            for b, dil in enumerate(DILATIONS):
                out_refs[a * nd + b][...] = _to_subseq(val, scr_ref, dil).astype(BF16)

    outs = pl.pallas_call(
        body, name=name, grid=(t // tm,),
        in_specs=[col(2), col(3), col(4), blk, blk, row, row, pl.BlockSpec((B_WIDTH, B_WIDTH), lambda i: (0, 0))],
        out_specs=[_subseq_spec(tm, dil) for _ in range(3) for dil in DILATIONS],
        out_shape=[jax.ShapeDtypeStruct(_subseq_shape(t, dil), BF16) for _ in range(3) for dil in DILATIONS],
        scratch_shapes=[_subseq_scratch(tm)], compiler_params=_params("parallel"),
    )(z, z, z, cos_t, sin_t, gq, gk, seg)
    return outs[:nd], outs[nd:2 * nd], outs[2 * nd:]


def _qk_prep_bwd(z, dqs, dks, dvs, cos_t, sin_t, gq, gk, seg, name):
    t = z.shape[0]
    tm = _tile(t, (SUBSEQ_TM,))
    col = lambda c: pl.BlockSpec((tm, B_WIDTH), lambda i: (i, c))
    row = pl.BlockSpec((1, B_WIDTH), lambda i: (0, 0))
    blk = col(0)
    nb = len(DILATIONS)
    subs = [_subseq_spec(tm, dil) for dil in DILATIONS]

    def body(*refs):
        q_ref, k_ref = refs[0], refs[1]
        dq_refs, dk_refs, dv_refs = refs[2:2 + nb], refs[2 + nb:2 + 2 * nb], refs[2 + 2 * nb:2 + 3 * nb]
        cos_ref, sin_ref, gq_ref, gk_ref, seg_ref, dz_ref, dgq_ref, dgk_ref, scr_ref = refs[2 + 3 * nb:]

        @pl.when(pl.program_id(0) == 0)
        def _():
            dgq_ref[...] = jnp.zeros_like(dgq_ref)
            dgk_ref[...] = jnp.zeros_like(dgk_ref)

        def total(d_refs):
            return sum(_from_subseq(r_[...], scr_ref, dil) for r_, dil in zip(d_refs, DILATIONS))

        def back(x, d_refs, g, dg_ref):
            dout = total(d_refs)
            dy = dout * cos_ref[...] + _swap_halves(dout * sin_ref[...])
            r = lax.rsqrt(_head_sum(x * x, seg_ref[...]) * (1.0 / HEAD_DIM) + EPS)
            xn = x * r
            dg_ref[...] += _colsum(dy * xn)
            dxn = dy * g
            return r * (dxn - xn * (_head_sum(dxn * xn, seg_ref[...]) * (1.0 / HEAD_DIM)))

        dz_ref[:, 0:B_WIDTH] = back(q_ref[...].astype(F32), dq_refs, gq_ref[...], dgq_ref).astype(BF16)
        dz_ref[:, B_WIDTH:2 * B_WIDTH] = back(k_ref[...].astype(F32), dk_refs, gk_ref[...], dgk_ref).astype(BF16)
        dz_ref[:, 2 * B_WIDTH:3 * B_WIDTH] = total(dv_refs).astype(BF16)

    return pl.pallas_call(
        body, name=name, grid=(t // tm,),
        in_specs=[col(2), col(3)] + subs * 3 + [blk, blk, row, row, pl.BlockSpec((B_WIDTH, B_WIDTH), lambda i: (0, 0))],
        out_specs=[pl.BlockSpec((tm, 3 * B_WIDTH), lambda i: (i, 0)), row, row],
        out_shape=[jax.ShapeDtypeStruct((t, 3 * B_WIDTH), BF16), jax.ShapeDtypeStruct((1, B_WIDTH), F32),
                   jax.ShapeDtypeStruct((1, B_WIDTH), F32)],
        scratch_shapes=[_subseq_scratch(tm)], compiler_params=_params("arbitrary"),
    )(z, z, *dqs, *dks, *dvs, cos_t, sin_t, gq, gk, seg)


def _subseq_views(x, col, name):
    t = x.shape[0]
    tm = _tile(t, (SUBSEQ_TM,))

    def body(x_ref, *rest):
        out_refs, scr_ref = rest[:-1], rest[-1]
        val = x_ref[...].astype(F32)
        for o_ref, dil in zip(out_refs, DILATIONS):
            o_ref[...] = _to_subseq(val, scr_ref, dil).astype(o_ref.dtype)

    return pl.pallas_call(
        body, name=name, grid=(t // tm,), in_specs=[pl.BlockSpec((tm, B_WIDTH), lambda i: (i, col))],
        out_specs=[_subseq_spec(tm, dil) for dil in DILATIONS],
        out_shape=[jax.ShapeDtypeStruct(_subseq_shape(t, dil), x.dtype) for dil in DILATIONS],
        scratch_shapes=[_subseq_scratch(tm)], compiler_params=_params("parallel"),
    )(x)


def _attn_fwd(q, k, v, dil, name):
    t = q.shape[0] * dil
    nb = t // dil // Q_BLOCK
    cur = pl.BlockSpec((Q_BLOCK, B_WIDTH), lambda r, i: (i, r))
    prev = pl.BlockSpec((Q_BLOCK, B_WIDTH), lambda r, i: (jnp.maximum(i - 1, 0), r))

    def body(q_ref, kp_ref, kc_ref, vp_ref, vc_ref, o_ref, lse_ref):
        i = pl.program_id(1)
        q = q_ref[...]
        kk = jnp.concatenate([kp_ref[...], kc_ref[...]], axis=0)
        vv = jnp.concatenate([vp_ref[...], vc_ref[...]], axis=0)
        a = lax.broadcasted_iota(jnp.int32, (Q_BLOCK, 2 * Q_BLOCK), 0)
        j = lax.broadcasted_iota(jnp.int32, (Q_BLOCK, 2 * Q_BLOCK), 1)
        dist = a + Q_BLOCK - j
        mask = (dist >= 0) & (dist <= Q_BLOCK) & ((j >= Q_BLOCK) | (i > 0))
        sls = [slice(h * HEAD_DIM, (h + 1) * HEAD_DIM) for h in range(HEADS)]
        scores = [_dot(q[:, sl], kk[:, sl], _NT) for sl in sls]
        ps, dens = [], []
        for sl, s in zip(sls, scores):
            s = jnp.where(mask, s * (HEAD_DIM ** -0.5), NEG)
            m = jnp.max(s, axis=-1, keepdims=True)
            p = jnp.exp(s - m)
            den = jnp.sum(p, axis=-1, keepdims=True)
            ps.append(p.astype(BF16))
            dens.append(den)
            lse_ref[:, sl] = jnp.broadcast_to(m + jnp.log(den), (Q_BLOCK, HEAD_DIM))
        for sl, p, den in zip(sls, ps, dens):
            o_ref[:, sl] = _dot(p, vv[:, sl], _NN) / den

    return pl.pallas_call(
        body, name=name, grid=(dil, nb), in_specs=[cur, prev, cur, prev, cur], out_specs=[cur, cur],
        out_shape=[jax.ShapeDtypeStruct(_subseq_shape(t, dil), F32)] * 2,
        compiler_params=_params("parallel", "parallel"),
    )(q, k, k, v, v)


def _attn_merge(outs, lses, cat, name):
    nb = len(DILATIONS)
    t = cat.shape[0]
    tm = _tile(t, (SUBSEQ_TM,))
    subs = [_subseq_spec(tm, dil) for dil in DILATIONS]

    def body(*refs):
        o_refs, l_refs = refs[:nb], refs[nb:2 * nb]
        yb_refs, lse_refs, cat_ref, scr_ref = refs[2 * nb + 1:3 * nb + 1], refs[3 * nb + 1:4 * nb + 1], refs[4 * nb + 1], refs[4 * nb + 2]
        ls = [_from_subseq(r[...], scr_ref, dil) for r, dil in zip(l_refs, DILATIONS)]
        m = functools.reduce(jnp.maximum, ls)
        tot = m + jnp.log(sum(jnp.exp(l - m) for l in ls))
        yb = sum(jnp.exp(l - tot) * _from_subseq(o[...], scr_ref, dil) for l, o, dil in zip(ls, o_refs, DILATIONS))
        cat_ref[...] = yb.astype(BF16)
        yb = yb.astype(BF16).astype(F32)
        for yb_ref, lse_ref, dil in zip(yb_refs, lse_refs, DILATIONS):
            yb_ref[...] = _to_subseq(yb, scr_ref, dil).astype(BF16)
            lse_ref[...] = _to_subseq(tot, scr_ref, dil)

    outs_ = pl.pallas_call(
        body, name=name, grid=(t // tm,), in_specs=subs * 2 + [pl.BlockSpec(memory_space=pl.ANY)],
        out_specs=subs * 2 + [pl.BlockSpec((tm, B_WIDTH), lambda i: (i, A_WIDTH // B_WIDTH))],
        out_shape=[jax.ShapeDtypeStruct(_subseq_shape(t, dil), BF16) for dil in DILATIONS]
        + [jax.ShapeDtypeStruct(_subseq_shape(t, dil), F32) for dil in DILATIONS] + [jax.ShapeDtypeStruct(cat.shape, BF16)],
        input_output_aliases={2 * nb: 2 * nb}, scratch_shapes=[_subseq_scratch(tm)], compiler_params=_params("parallel"),
    )(*outs, *lses, cat)
    return outs_[:nb], outs_[nb:2 * nb], outs_[2 * nb]


def _attn_bwd(q, k, v, do, o, lse, dil, name):
    t = q.shape[0] * dil
    nb = t // dil // Q_BLOCK
    cur = pl.BlockSpec((Q_BLOCK, B_WIDTH), lambda r, i: (i, r))
    prev = pl.BlockSpec((Q_BLOCK, B_WIDTH), lambda r, i: (jnp.maximum(i - 1, 0), r))
    scale = HEAD_DIM ** -0.5

    def body(q_ref, kp_ref, kc_ref, vp_ref, vc_ref, do_ref, o_ref, lse_ref, dq_ref, dk_ref, dv_ref,
             ck_ref, cv_ref, tk_ref, tv_ref):
        i = pl.program_id(1)

        @pl.when(i == 0)
        def _():
            ck_ref[...] = jnp.zeros_like(ck_ref)
            cv_ref[...] = jnp.zeros_like(cv_ref)

        q = q_ref[...]
        kk = jnp.concatenate([kp_ref[...], kc_ref[...]], axis=0)
        vv = jnp.concatenate([vp_ref[...], vc_ref[...]], axis=0)
        do = do_ref[...]
        dof = do.astype(F32)
        of = o_ref[...].astype(F32)
        a = lax.broadcasted_iota(jnp.int32, (Q_BLOCK, 2 * Q_BLOCK), 0)
        j = lax.broadcasted_iota(jnp.int32, (Q_BLOCK, 2 * Q_BLOCK), 1)
        dist = a + Q_BLOCK - j
        mask = (dist >= 0) & (dist <= Q_BLOCK) & ((j >= Q_BLOCK) | (i > 0))
        sls = [slice(h * HEAD_DIM, (h + 1) * HEAD_DIM) for h in range(HEADS)]
        scores = [_dot(q[:, sl], kk[:, sl], _NT) for sl in sls]
        dps = [_dot(do[:, sl], vv[:, sl], _NT) for sl in sls]
        ps, dss = [], []
        for sl, s, dp in zip(sls, scores, dps):
            p = jnp.exp(jnp.where(mask, s * scale, NEG) - lse_ref[:, sl.start:sl.start + 1])
            delta = jnp.sum(dof[:, sl] * of[:, sl], axis=-1, keepdims=True)
            dss.append((p * (dp - delta) * scale).astype(BF16))
            ps.append(p.astype(BF16))
        for sl, p, ds in zip(sls, ps, dss):
            dq_ref[:, sl] = _dot(ds, kk[:, sl], _NN)
            dv_t = _dot(do[:, sl], p, _TN)
            dk_t = _dot(q[:, sl], ds, _TN)
            tk_ref[sl, :] = ck_ref[sl, :] + dk_t[:, :Q_BLOCK]
            tv_ref[sl, :] = cv_ref[sl, :] + dv_t[:, :Q_BLOCK]
            ck_ref[sl, :] = dk_t[:, Q_BLOCK:]
            cv_ref[sl, :] = dv_t[:, Q_BLOCK:]

        @pl.when(i >= 1)
        def _():
            rows = pl.ds(pl.multiple_of((i - 1) * Q_BLOCK, Q_BLOCK), Q_BLOCK)
            dk_ref[rows, :] = tk_ref[...].T
            dv_ref[rows, :] = tv_ref[...].T

        @pl.when(i == nb - 1)
        def _():
            rows = pl.ds((nb - 1) * Q_BLOCK, Q_BLOCK)
            dk_ref[rows, :] = ck_ref[...].T
            dv_ref[rows, :] = cv_ref[...].T

    whole = pl.BlockSpec((t // dil, B_WIDTH), lambda r, i: (0, r))
    return pl.pallas_call(
        body, name=name, grid=(dil, nb), in_specs=[cur, prev, cur, prev, cur, cur, cur, cur],
        out_specs=[cur, whole, whole], out_shape=[jax.ShapeDtypeStruct(_subseq_shape(t, dil), F32)] * 3,
        scratch_shapes=[pltpu.VMEM((B_WIDTH, Q_BLOCK), F32)] * 4,
        compiler_params=_params("parallel", "arbitrary"),
    )(q, k, k, v, v, do, o, lse)


FFN_TN = 256
FFN_ACT_TM = (4096, 2048, 1024, 512, 256, 128)
FFN_FWD_CHUNK = 256
FFN_BWD_CHUNK = 128


def _ffn_up(h, up_t, name):
    t, k = h.shape
    tm = _tile(t)

    def body(h_ref, w_ref, o_ref):
        o_ref[...] = _dot(h_ref[...], w_ref[...], _NT).astype(BF16)

    return pl.pallas_call(
        body, name=name, grid=(2, t // tm),
        in_specs=[pl.BlockSpec((tm, k), lambda p, i: (i, 0)), pl.BlockSpec((None, FFN_DIM, k), lambda p, i: (p, 0, 0))],
        out_specs=pl.BlockSpec((None, tm, FFN_DIM), lambda p, i: (p, i, 0)),
        out_shape=jax.ShapeDtypeStruct((2, t, FFN_DIM), BF16), compiler_params=_params("parallel", "parallel"),
    )(h, up_t.reshape(2, FFN_DIM, k))


def _ffn_up_dx(du, up_t, name):
    t = du.shape[1]
    k = up_t.shape[1]
    tm = _tile(t)

    def body(a_ref, b_ref, o_ref):
        o_ref[...] = _dot(a_ref[0], b_ref[0], _NN) + _dot(a_ref[1], b_ref[1], _NN)

    return pl.pallas_call(
        body, name=name, grid=(t // tm,),
        in_specs=[pl.BlockSpec((2, tm, FFN_DIM), lambda i: (0, i, 0)), pl.BlockSpec((2, FFN_DIM, k), lambda i: (0, 0, 0))],
        out_specs=pl.BlockSpec((tm, k), lambda i: (i, 0)), out_shape=jax.ShapeDtypeStruct((t, k), F32),
        compiler_params=_params("parallel"),
    )(du, up_t.reshape(2, FFN_DIM, k))


def _ffn_conv(win, w_ref, b_ref, p):
    x = win.astype(F32)
    x0, x1, x2 = x[FFN_HALO:], pltpu.roll(x, 1, 0)[FFN_HALO:], pltpu.roll(x, 2, 0)[FFN_HALO:]
    return b_ref[p] + w_ref[p, 2:3, :] * x0 + w_ref[p, 1:2, :] * x1 + w_ref[p, 0:1, :] * x2


def _zero_if(cond, v):
    return jnp.where(cond, 0, v).astype(v.dtype)


def _ffn_act(u, dw_w, dw_b, name):
    t = u.shape[1]
    tm = _tile(t, FFN_ACT_TM)
    chunk = min(FFN_FWD_CHUNK, tm)
    hb = tm // FFN_HALO
    main = pl.BlockSpec((2, tm, FFN_TN), lambda i, j: (0, i, j))
    halo = pl.BlockSpec((2, FFN_HALO, FFN_TN), lambda i, j: (0, jnp.maximum(i * hb - 1, 0), j))
    wsp = pl.BlockSpec((2, FFN_CONV_WIDTH, FFN_TN), lambda i, j: (0, 0, j))
    bsp = pl.BlockSpec((2, 1, FFN_TN), lambda i, j: (0, 0, j))

    def body(u_ref, uh_ref, w_ref, b_ref, o_ref, z_ref):
        first = pl.program_id(0) == 0

        def emit(rows, wins):
            za, zb = _ffn_conv(wins[0], w_ref, b_ref, 0), _ffn_conv(wins[1], w_ref, b_ref, 1)
            o_ref[rows, :] = (za * _sigmoid(za) * zb).astype(BF16)
            z_ref[0, rows, :] = za.astype(BF16)
            z_ref[1, rows, :] = zb.astype(BF16)

        emit(pl.ds(0, chunk), [jnp.concatenate([_zero_if(first, uh_ref[p]), u_ref[p, 0:chunk, :]], axis=0) for p in range(2)])

        def step(c, carry):
            s = pl.multiple_of(c * chunk, chunk)
            emit(pl.ds(s, chunk), [u_ref[p, pl.ds(s - FFN_HALO, chunk + FFN_HALO), :] for p in range(2)])
            return carry

        lax.fori_loop(1, tm // chunk, step, 0)

    return pl.pallas_call(
        body, name=name, grid=(t // tm, FFN_DIM // FFN_TN), in_specs=[main, halo, wsp, bsp],
        out_specs=[pl.BlockSpec((tm, FFN_TN), lambda i, j: (i, j)), main],
        out_shape=[jax.ShapeDtypeStruct((t, FFN_DIM), BF16), jax.ShapeDtypeStruct((2, t, FFN_DIM), BF16)],
        compiler_params=_params("parallel", "parallel"),
    )(u, u, dw_w, dw_b)


def _fold8(v):
    return jnp.sum(v.reshape(v.shape[0] // 8, 8, v.shape[1]), axis=0)


def _ffn_act_bwd(u, z, dact, dw_w, name):
    t = u.shape[1]
    tm = _tile(t, FFN_ACT_TM)
    chunk = min(FFN_BWD_CHUNK, tm // 2)
    halo = FFN_HALO
    hb = tm // halo
    nt = t // tm
    last_halo = t // halo - 1
    next_i = lambda i: jnp.minimum((i + 1) * hb, last_halo)
    main = pl.BlockSpec((2, tm, FFN_TN), lambda j, i: (0, i, j))
    nxt = pl.BlockSpec((2, halo, FFN_TN), lambda j, i: (0, next_i(i), j))
    wsp = pl.BlockSpec((2, FFN_CONV_WIDTH, FFN_TN), lambda j, i: (0, 0, j))
    bsp = pl.BlockSpec((2, 1, FFN_TN), lambda j, i: (0, 0, j))

    def body(u_ref, z_ref, zn_ref, da_ref, dan_ref, w_ref, du_ref, dw_ref, db_ref, acc_ref):
        i = pl.program_id(1)
        last = i == nt - 1
        acc_ref[...] = jnp.zeros_like(acc_ref)

        def emit(rows, zs, dact):
            n = chunk + halo
            za, zb, dact = zs[0].astype(F32), zs[1].astype(F32), dact.astype(F32)
            sg = _sigmoid(za)
            dzs = (dact * zb * (sg * (1.0 + za * (1.0 - sg))), dact * (za * sg))
            for p, dz in enumerate(dzs):
                ahead = (dz[:chunk], pltpu.roll(dz, n - 1, 0)[:chunk], pltpu.roll(dz, n - 2, 0)[:chunk])
                um = u_ref[p, rows, :].astype(F32)
                acc_ref[p, FFN_CONV_WIDTH] += _fold8(ahead[0])
                du = None
                for j, dzj in enumerate(ahead):
                    k = FFN_CONV_WIDTH - 1 - j
                    acc_ref[p, k] += _fold8(dzj * um)
                    term = w_ref[p, k:k + 1, :] * dzj
                    du = term if du is None else du + term
                du_ref[p, rows, :] = du.astype(BF16)

        def step(c, carry):
            s = pl.multiple_of(c * chunk, chunk)
            emit(pl.ds(s, chunk), [z_ref[p, pl.ds(s, chunk + halo), :] for p in range(2)], da_ref[pl.ds(s, chunk + halo), :])
            return carry

        lax.fori_loop(0, tm // chunk - 1, step, 0)
        s = tm - chunk
        emit(pl.ds(s, chunk),
             [jnp.concatenate([z_ref[p, s:tm, :], zn_ref[p]], axis=0) for p in range(2)],
             jnp.concatenate([da_ref[s:tm, :], _zero_if(last, dan_ref[...])], axis=0))

        @pl.when(i == 0)
        def _():
            dw_ref[...] = jnp.zeros_like(dw_ref)
            db_ref[...] = jnp.zeros_like(db_ref)

        for p in range(2):
            for k in range(FFN_CONV_WIDTH):
                dw_ref[p, k:k + 1, :] += _colsum(acc_ref[p, k])
            db_ref[p] += _colsum(acc_ref[p, FFN_CONV_WIDTH])

    return pl.pallas_call(
        body, name=name, grid=(FFN_DIM // FFN_TN, nt),
        in_specs=[main, main, nxt, pl.BlockSpec((tm, FFN_TN), lambda j, i: (i, j)),
                  pl.BlockSpec((halo, FFN_TN), lambda j, i: (next_i(i), j)), wsp],
        out_specs=[main, wsp, bsp],
        out_shape=[jax.ShapeDtypeStruct((2, t, FFN_DIM), BF16), jax.ShapeDtypeStruct((2, FFN_CONV_WIDTH, FFN_DIM), F32),
                   jax.ShapeDtypeStruct((2, 1, FFN_DIM), F32)],
        scratch_shapes=[pltpu.VMEM((2, FFN_CONV_WIDTH + 1, 8, FFN_TN), F32)],
        compiler_params=_params("parallel", "arbitrary"),
    )(u, z, z, dact, dact, dw_w)


CONV_TM = 1024
CONV_ROWS = 128
CONV_FWD_ROWS = 256
CONV_LANES = 128
CONV_NORM_ROWS = 32


def _glu_window(pa_ref, pah_ref, pg_ref, pgh_ref, scr_ref, first):
    ah, gh = pah_ref[...].astype(F32), pgh_ref[...].astype(F32)
    scr_ref[0:CONV_HALO, :] = jnp.where(first, 0.0, ah * _sigmoid(gh))
    scr_ref[CONV_HALO:, :] = pa_ref[...].astype(F32) * _sigmoid(pg_ref[...].astype(F32))


def _tap_slabs(win, rows, ahead):
    n = win.shape[0]
    for s in range(8):
        ws = win if s == 0 else pltpu.roll(win, n - s if ahead else s, 0)
        for q in range(CONV_HALO // 8):
            o = 8 * q + s
            if o < CONV_WIDTH:
                start = 8 * q if ahead else CONV_HALO - 8 * q
                yield CONV_WIDTH - 1 - o, ws[start:start + rows]


def _conformer_specs(t):
    tm = _tile(t, (CONV_TM, 128))
    hb = tm // CONV_HALO
    d = D_MODEL
    main = lambda c: pl.BlockSpec((tm, d), lambda i: (i, c))
    halo = lambda c: pl.BlockSpec((CONV_HALO, d), lambda i: (jnp.maximum(i * hb - 1, 0), c))
    row = pl.BlockSpec((1, d), lambda i: (0, 0))
    wsp = pl.BlockSpec((CONV_WIDTH, d), lambda i: (0, 0))
    return tm, main, halo, row, wsp


def _conformer_mid(p, dw_w, dw_b, ln_g, ln_b, name):
    t = p.shape[0]
    tm, main, halo, row, wsp = _conformer_specs(t)
    d, lanes, rows = D_MODEL, CONV_LANES, min(CONV_FWD_ROWS, tm)

    def body(pa_ref, pah_ref, pg_ref, pgh_ref, w_ref, b_ref, g_ref, lb_ref, o_ref, dc_ref, scr_ref):
        _glu_window(pa_ref, pah_ref, pg_ref, pgh_ref, scr_ref, pl.program_id(0) == 0)
        for c in range(d // lanes):
            ls = slice(c * lanes, (c + 1) * lanes)

            def taps(r, carry, ls=ls):
                r0 = pl.multiple_of(r * rows, rows)
                acc = jnp.broadcast_to(b_ref[:, ls], (rows, lanes))
                for k, slab in _tap_slabs(scr_ref[pl.ds(r0, rows + CONV_HALO), ls], rows, False):
                    acc = acc + w_ref[k:k + 1, ls] * slab
                dc_ref[pl.ds(r0, rows), ls] = acc
                return carry

            lax.fori_loop(0, tm // rows, taps, 0)

        def norm(r, carry):
            r0 = pl.multiple_of(r * CONV_NORM_ROWS, CONV_NORM_ROWS)
            dc = dc_ref[pl.ds(r0, CONV_NORM_ROWS), :]
            xc = dc - jnp.mean(dc, axis=-1, keepdims=True)
            ln = xc * lax.rsqrt(jnp.mean(xc * xc, axis=-1, keepdims=True) + EPS) * g_ref[...] + lb_ref[...]
            o_ref[pl.ds(r0, CONV_NORM_ROWS), :] = (ln * _sigmoid(ln)).astype(BF16)
            return carry

        lax.fori_loop(0, tm // CONV_NORM_ROWS,norm, 0)

    return pl.pallas_call(
        body, name=name, grid=(t // tm,), in_specs=[main(0), halo(0), main(1), halo(1), wsp, row, row, row],
        out_specs=[main(0), main(0)], out_shape=[jax.ShapeDtypeStruct((t, d), BF16), jax.ShapeDtypeStruct((t, d), F32)],
        scratch_shapes=[pltpu.VMEM((tm + CONV_HALO, d), F32)], compiler_params=_params("parallel"),
    )(p, p, p, p, dw_w, dw_b, ln_g, ln_b)


def _conformer_mid_bwd(p, dc, ds, ln_g, ln_b, name):
    t = p.shape[0]
    tm, main, halo, row, wsp = _conformer_specs(t)
    d, nt = D_MODEL, t // tm
    rows, lanes = CONV_ROWS, CONV_LANES

    def body(pa_ref, pah_ref, pg_ref, pgh_ref, dc_ref, ds_ref, g_ref, lb_ref,
             ddc_ref, dw_ref, db_ref, dg_ref, dlb_ref, scr_ref, wacc_ref, racc_ref):
        i = pl.program_id(0)

        @pl.when(i == 0)
        def _():
            wacc_ref[...] = jnp.zeros_like(wacc_ref)
            racc_ref[...] = jnp.zeros_like(racc_ref)

        _glu_window(pa_ref, pah_ref, pg_ref, pgh_ref, scr_ref, i == 0)

        def norm_bwd(r, carry):
            r0 = pl.multiple_of(r * CONV_NORM_ROWS, CONV_NORM_ROWS)
            dcv = dc_ref[pl.ds(r0, CONV_NORM_ROWS), :]
            xc = dcv - jnp.mean(dcv, axis=-1, keepdims=True)
            rstd = lax.rsqrt(jnp.mean(xc * xc, axis=-1, keepdims=True) + EPS)
            xhat = xc * rstd
            ln = xhat * g_ref[...] + lb_ref[...]
            sg = _sigmoid(ln)
            dln = ds_ref[pl.ds(r0, CONV_NORM_ROWS), :].astype(F32) * (sg * (1.0 + ln * (1.0 - sg)))
            dxh = dln * g_ref[...]
            ddc = rstd * (dxh - jnp.mean(dxh, axis=-1, keepdims=True) - xhat * jnp.mean(dxh * xhat, axis=-1, keepdims=True))
            ddc_ref[pl.ds(r0, CONV_NORM_ROWS), :] = ddc
            racc_ref[0] += _fold8(dln * xhat)
            racc_ref[1] += _fold8(dln)
            racc_ref[2] += _fold8(ddc)
            return carry

        lax.fori_loop(0, tm // CONV_NORM_ROWS,norm_bwd, 0)

        for c in range(d // lanes):
            ls = slice(c * lanes, (c + 1) * lanes)

            def taps(r, carry, ls=ls):
                r0 = pl.multiple_of(r * rows, rows)
                ddc = ddc_ref[pl.ds(r0, rows), ls]
                for k, slab in _tap_slabs(scr_ref[pl.ds(r0, rows + CONV_HALO), ls], rows, False):
                    wacc_ref[k, :, ls] += _fold8(ddc * slab)
                return carry

            lax.fori_loop(0, tm // rows, taps, 0)

        @pl.when(i == nt - 1)
        def _():
            for k in range(CONV_WIDTH):
                dw_ref[k:k + 1, :] = _colsum(wacc_ref[k])
            dg_ref[...] = _colsum(racc_ref[0])
            dlb_ref[...] = _colsum(racc_ref[1])
            db_ref[...] = _colsum(racc_ref[2])

    return pl.pallas_call(
        body, name=name, grid=(nt,), in_specs=[main(0), halo(0), main(1), halo(1), main(0), main(0), row, row],
        out_specs=[main(0), wsp, row, row, row],
        out_shape=[jax.ShapeDtypeStruct((t, d), F32), jax.ShapeDtypeStruct((CONV_WIDTH, d), F32)]
        + [jax.ShapeDtypeStruct((1, d), F32)] * 3,
        scratch_shapes=[pltpu.VMEM((tm + CONV_HALO, d), F32), pltpu.VMEM((CONV_WIDTH, 8, d), F32), pltpu.VMEM((3, 8, d), F32)],
        compiler_params=_params("arbitrary"),
    )(p, p, p, p, dc, ds, ln_g, ln_b)


def _conformer_glu_bwd(p, ddc, dw_w, name):
    t = p.shape[0]
    d = D_MODEL
    tm = _tile(t, (CONV_TM, 128))
    hb = tm // CONV_HALO
    nt = t // tm
    last_halo = t // CONV_HALO - 1
    rows, lanes = CONV_ROWS, CONV_LANES
    col = lambda c: pl.BlockSpec((tm, d), lambda i: (i, c))
    nxt = pl.BlockSpec((CONV_HALO, d), lambda i: (jnp.minimum((i + 1) * hb, last_halo), 0))

    def body(pa_ref, pg_ref, ddc_ref, ddcn_ref, w_ref, dp_ref, db_ref, scr_ref, acc_ref):
        i = pl.program_id(0)

        @pl.when(i == 0)
        def _():
            acc_ref[...] = jnp.zeros_like(acc_ref)

        scr_ref[0:tm, :] = ddc_ref[...]
        scr_ref[tm:, :] = _zero_if(i == nt - 1, ddcn_ref[...])
        for c in range(d // lanes):
            ls = slice(c * lanes, (c + 1) * lanes)
            gs = slice(d + c * lanes, d + (c + 1) * lanes)

            def taps(r, carry, ls=ls, gs=gs):
                r0 = pl.multiple_of(r * rows, rows)
                dglu = None
                for k, slab in _tap_slabs(scr_ref[pl.ds(r0, rows + CONV_HALO), ls], rows, True):
                    term = w_ref[k:k + 1, ls] * slab
                    dglu = term if dglu is None else dglu + term
                a = pa_ref[pl.ds(r0, rows), ls].astype(F32)
                sg = _sigmoid(pg_ref[pl.ds(r0, rows), ls].astype(F32))
                da = (dglu * sg).astype(BF16)
                dg = (dglu * a * sg * (1.0 - sg)).astype(BF16)
                dp_ref[pl.ds(r0, rows), ls] = da
                dp_ref[pl.ds(r0, rows), gs] = dg
                acc_ref[:, ls] += _fold8(da.astype(F32))
                acc_ref[:, gs] += _fold8(dg.astype(F32))
                return carry

            lax.fori_loop(0, tm // rows, taps, 0)

        @pl.when(i == nt - 1)
        def _():
            db_ref[...] = _colsum(acc_ref[...])

    return pl.pallas_call(
        body, name=name, grid=(nt,),
        in_specs=[col(0), col(1), col(0), nxt, pl.BlockSpec((CONV_WIDTH, d), lambda i: (0, 0))],
        out_specs=[pl.BlockSpec((tm, 2 * d), lambda i: (i, 0)), pl.BlockSpec((1, 2 * d), lambda i: (0, 0))],
        out_shape=[jax.ShapeDtypeStruct((t, 2 * d), BF16), jax.ShapeDtypeStruct((1, 2 * d), F32)],
        scratch_shapes=[pltpu.VMEM((tm + CONV_HALO, d), F32), pltpu.VMEM((8, 2 * d), F32)],
        compiler_params=_params("arbitrary"),
    )(p, p, ddc, ddc, dw_w)


def _colsum_call(a, name):
    t, n = a.shape
    tm = _tile(t)

    def body(a_ref, o_ref):
        @pl.when(pl.program_id(0) == 0)
        def _():
            o_ref[...] = jnp.zeros_like(o_ref)

        o_ref[...] += _colsum(a_ref[...].astype(F32))

    return pl.pallas_call(
        body, name=name, grid=(t // tm,), in_specs=[pl.BlockSpec((tm, n), lambda i: (i, 0))],
        out_specs=pl.BlockSpec((1, n), lambda i: (0, 0)), out_shape=jax.ShapeDtypeStruct((1, n), F32),
        compiler_params=_params("arbitrary"),
    )(a)


def _ada_fwd(c_all, w, name):
    rows, d = c_all.shape
    n = w.shape[1]
    tn = _tile(n, (256, 128))

    def body(c_ref, w_ref, o_ref):
        c = c_ref[...]
        o_ref[...] = _dot((c * _sigmoid(c)).astype(BF16), w_ref[...].astype(BF16), _NN)

    return pl.pallas_call(
        body, name=name, grid=(n // tn,),
        in_specs=[pl.BlockSpec((rows, d), lambda j: (0, 0)), pl.BlockSpec((d, tn), lambda j: (0, j))],
        out_specs=pl.BlockSpec((rows, tn), lambda j: (0, j)), out_shape=jax.ShapeDtypeStruct((rows, n), F32),
        compiler_params=_params("parallel"),
    )(c_all, w)


def _ada_bwd(c_all, dmod, name):
    rows, d = c_all.shape
    n = dmod.shape[1]
    tn = _tile(n, (256, 128))

    def body(c_ref, g_ref, o_ref):
        c = c_ref[...]
        o_ref[...] = _dot((c * _sigmoid(c)).astype(BF16), g_ref[...].astype(BF16), _TN)

    return pl.pallas_call(
        body, name=name, grid=(n // tn,),
        in_specs=[pl.BlockSpec((rows, d), lambda j: (0, 0)), pl.BlockSpec((rows, tn), lambda j: (0, j))],
        out_specs=pl.BlockSpec((d, tn), lambda j: (0, j)), out_shape=jax.ShapeDtypeStruct((d, n), F32),
        compiler_params=_params("parallel"),
    )(c_all, dmod)


def _sum_in_device_order(own, land, me, name):
    s, r, c = land.shape
    tr = _row_tile(r, 256)
    slot = lambda k: pl.BlockSpec((None, tr, c), lambda i, me_ref: (jnp.where(me_ref[0] == k, (k + 1) % s, k), i, 0))
    own_spec = pl.BlockSpec((tr, c), lambda i, me_ref: (i, 0))

    def body(me_ref, own_ref, *refs):
        o_ref = refs[-1]
        acc = None
        for k, ref in enumerate(refs[:-1]):
            term = jnp.where(me_ref[0] == k, own_ref[...], ref[...]).astype(F32)
            acc = term if acc is None else acc + term
        o_ref[...] = acc

    return pl.pallas_call(
        body, name=name, out_shape=jax.ShapeDtypeStruct((r, c), F32),
        grid_spec=pltpu.PrefetchScalarGridSpec(
            num_scalar_prefetch=1, grid=(r // tr,), in_specs=[own_spec] + [slot(k) for k in range(s)], out_specs=own_spec),
        compiler_params=_params("parallel"),
    )(me, own, *[land] * s)


def _sum_with_own(blocks, land, me, name):
    s, r, c = land.shape
    tr = _row_tile(r, 256)
    slot = lambda k: pl.BlockSpec((None, tr, c), lambda i, me_ref: ((me_ref[0] + k) % s, i, 0))

    def body(me_ref, own_ref, *refs):
        o_ref = refs[-1]
        acc = own_ref[...].astype(F32)
        for ref in refs[:-1]:
            acc = acc + ref[...].astype(F32)
        o_ref[...] = acc

    return pl.pallas_call(
        body, name=name, out_shape=jax.ShapeDtypeStruct((r, c), F32),
        grid_spec=pltpu.PrefetchScalarGridSpec(
            num_scalar_prefetch=1, grid=(r // tr,), in_specs=[slot(0)] + [slot(k) for k in range(1, s)],
            out_specs=pl.BlockSpec((tr, c), lambda i, me_ref: (i, 0))),
        compiler_params=_params("parallel"),
    )(me, blocks, *[land] * (s - 1))


def _adamw_update(w, g, m, v):
    nm = ADAM_B1 * m + (1.0 - ADAM_B1) * g
    nv = ADAM_B2 * v + (1.0 - ADAM_B2) * (g * g)
    m_hat = nm * (1.0 / (1.0 - ADAM_B1 ** ADAM_STEP))
    v_hat = nv * (1.0 / (1.0 - ADAM_B2 ** ADAM_STEP))
    return -ADAM_LR * (m_hat / (jnp.sqrt(v_hat) + ADAM_EPS) + ADAM_WD * w), nm, nv


def _adamw(w, g, m, v, behind, name):
    l, r, c = w.shape
    tr = _row_tile(r, 256)
    blk = pl.BlockSpec((None, tr, c), lambda k, i: (k, i, 0))
    order = [] if behind is None else [behind]

    def body(w_ref, g_ref, m_ref, v_ref, *rest):
        d_ref, nm_ref, nv_ref = rest[-3:]
        d_ref[...], nm_ref[...], nv_ref[...] = _adamw_update(w_ref[...], g_ref[...], m_ref[...], v_ref[...])

    return pl.pallas_call(
        body, name=name, grid=(l, r // tr), in_specs=[blk] * 4 + [pl.BlockSpec(memory_space=pl.ANY)] * len(order),
        out_specs=[blk] * 3, out_shape=[jax.ShapeDtypeStruct(w.shape, F32)] * 3,
        compiler_params=_params("parallel", "parallel"),
    )(w, g, m, v, *order)


def _adamw_small(ws, gs, ms, vs, name):
    n = len(ws)
    two_d = lambda a: a.reshape(-1, a.shape[-1])

    def body(*refs):
        ins, outs = refs[:4 * n], refs[4 * n:]
        for a in range(n):
            outs[a][...], outs[n + a][...], outs[2 * n + a][...] = _adamw_update(*[ins[k * n + a][...] for k in range(4)])

    res = pl.pallas_call(
        body, name=name, out_shape=[jax.ShapeDtypeStruct(two_d(w).shape, F32) for w in ws] * 3,
    )(*[two_d(a) for a in (*ws, *gs, *ms, *vs)])
    return [[res[k * n + a].reshape(ws[a].shape) for a in range(n)] for k in range(3)]


def _mesh_pos():
    return lax.axis_index("x"), lax.axis_index("y"), lax.axis_index("c")


def _all_gather_vmem(x_shard, name):
    m_per, n = x_shard.shape

    def body(x_ref, out_ref, send_sems, recv_sems, local_sem):
        x, y, c = _mesh_pos()
        me, sibling = (x, y, c), (x, y, 1 - c)
        chips = [(1 - x, y), (x, 1 - y), (1 - x, 1 - y)]

        def rows(px, py, pc):
            return out_ref.at[pl.ds((4 * px + 2 * py + pc) * m_per, m_per), :]

        def copy(k, block, to, src=None):
            return pltpu.make_async_remote_copy(
                src_ref=rows(*block) if src is None else src, dst_ref=rows(*block),
                send_sem=send_sems.at[k], recv_sem=recv_sems.at[k], device_id=to, device_id_type=MESH)

        mine = pltpu.make_async_copy(x_ref, rows(*me), local_sem)
        mine.start()
        first = [copy(0, me, sibling, src=x_ref)]
        first += [copy(1 + j, me, (*chip, c), src=x_ref) for j, chip in enumerate(chips)]
        for cp in first:
            cp.start()
        passed = [copy(4 + j, (*chip, c), sibling) for j, chip in enumerate(chips)]
        for j, chip in enumerate(chips):
            copy(1 + j, (*chip, c), me).wait_recv()
            passed[j].start()
        copy(0, sibling, me).wait_recv()
        for j, chip in enumerate(chips):
            copy(4 + j, (*chip, 1 - c), me).wait_recv()
        for cp in first + passed:
            cp.wait_send()
        mine.wait()

    return pl.pallas_call(
        body, name=name, out_shape=jax.ShapeDtypeStruct((N_DEV * m_per, n), x_shard.dtype),
        in_specs=[pl.BlockSpec(memory_space=pltpu.VMEM)], out_specs=pl.BlockSpec(memory_space=pltpu.VMEM),
        scratch_shapes=[pltpu.SemaphoreType.DMA((7,)), pltpu.SemaphoreType.DMA((7,)), pltpu.SemaphoreType.DMA],
    )(x_shard)


def _all_gather_hbm(shards, name):
    n = len(shards)
    out_shape = [jax.ShapeDtypeStruct((N_DEV,) + s.shape, s.dtype) for s in shards]

    def body(*refs):
        x_refs, out_refs = refs[:n], refs[n:2 * n]
        send_sems, recv_sems, local_sems = refs[2 * n:]
        x, y, c = _mesh_pos()
        me, sibling = (x, y, c), (x, y, 1 - c)
        chips = [(1 - x, y), (x, 1 - y), (1 - x, 1 - y)]

        def blk(a, p):
            return out_refs[a].at[4 * p[0] + 2 * p[1] + p[2]]

        def copy(a, k, block, to, src=None):
            return pltpu.make_async_remote_copy(
                src_ref=blk(a, block) if src is None else src, dst_ref=blk(a, block),
                send_sem=send_sems.at[7 * a + k], recv_sem=recv_sems.at[7 * a + k], device_id=to, device_id_type=MESH)

        mine = [pltpu.make_async_copy(x_refs[a], blk(a, me), local_sems.at[a]) for a in range(n)]
        for cp in mine:
            cp.start()
        first = []
        for a in range(n):
            first.append(copy(a, 0, me, sibling, src=x_refs[a]))
            first += [copy(a, 1 + j, me, (*chip, c), src=x_refs[a]) for j, chip in enumerate(chips)]
        for cp in first:
            cp.start()
        passed = []
        for j, chip in enumerate(chips):
            for a in range(n):
                copy(a, 1 + j, (*chip, c), me).wait_recv()
                fwd = copy(a, 4 + j, (*chip, c), sibling)
                fwd.start()
                passed.append(fwd)
        for a in range(n):
            copy(a, 0, sibling, me).wait_recv()
            for j, chip in enumerate(chips):
                copy(a, 4 + j, (*chip, 1 - c), me).wait_recv()
        for cp in first + passed:
            cp.wait_send()
        for cp in mine:
            cp.wait()

    return pl.pallas_call(
        body, name=name, out_shape=out_shape, in_specs=[pl.BlockSpec(memory_space=pltpu.VMEM)] * n,
        out_specs=[pl.BlockSpec(memory_space=pl.ANY)] * n,
        scratch_shapes=[pltpu.SemaphoreType.DMA((7 * n,)), pltpu.SemaphoreType.DMA((7 * n,)), pltpu.SemaphoreType.DMA((n,))],
    )(*shards)


def _peers(x, y, c):
    flip = lambda v, f: 1 - v if f else v
    return [(flip(x, m & 4), flip(y, m & 2), flip(c, m & 1)) for m in range(1, N_DEV)]


def _dev_index(p):
    return 4 * p[0] + 2 * p[1] + p[2]


def _push_copies(src_refs, land_refs, send_sems, recv_sems, scatter, receive):
    x, y, c = _mesh_pos()
    me = _dev_index((x, y, c))
    copies = []
    for a, (src, land) in enumerate(zip(src_refs, land_refs)):
        for k, p in enumerate(_peers(x, y, c)):
            copies.append(pltpu.make_async_remote_copy(
                src_ref=src.at[_dev_index(p)] if scatter else src, dst_ref=land.at[_dev_index(p) if receive else me],
                send_sem=send_sems.at[7 * a + k], recv_sem=recv_sems.at[7 * a + k], device_id=p, device_id_type=MESH))
    return copies


_HBM = pl.BlockSpec(memory_space=pltpu.HBM)
_SEM = pl.BlockSpec(memory_space=pltpu.SEMAPHORE)
_EFFECT = pltpu.SideEffectType.DATAFLOW_SIDE_EFFECTING


def _pushes_start(srcs, lands, scatter, name):
    n = len(srcs)

    def body(*refs):
        src_refs, land_refs = refs[:n], refs[n:2 * n]
        send_sems, recv_sems = refs[2 * n], refs[2 * n + 1]
        token = refs[-1]
        for cp in _push_copies(src_refs, land_refs, send_sems, recv_sems, scatter, receive=False):
            cp.start()
        token[...] = jnp.zeros_like(token)

    hbm = lambda a: pltpu.HBM(a.shape, a.dtype)
    sems = pltpu.SemaphoreType.DMA((7 * n,))
    outs = pl.pallas_call(
        body, name=name,
        out_shape=(sems, sems, *[hbm(a) for a in srcs], *[hbm(a) for a in lands], jax.ShapeDtypeStruct((8, 128), F32)),
        in_specs=[_HBM] * (2 * n), out_specs=(_SEM, _SEM, *[_HBM] * (2 * n), pl.BlockSpec(memory_space=pltpu.VMEM)),
        input_output_aliases={i: 2 + i for i in range(2 * n)},
        compiler_params=pltpu.CompilerParams(has_side_effects=_EFFECT),
    )(*[pltpu.with_memory_space_constraint(a, pltpu.HBM) for a in (*srcs, *lands)])
    return (outs[0], outs[1], outs[2:2 + n], outs[2 + n:2 + 2 * n], scatter), outs[-1]


def _pushes_wait(handle, after, name):
    send_sems, recv_sems, srcs, lands, scatter = handle
    n = len(srcs)
    after = after if isinstance(after, (tuple, list)) else (after,)

    def body(*refs):
        src_refs, land_refs = refs[:n], refs[n:2 * n]
        for cp in _push_copies(src_refs, land_refs, refs[2 * n], refs[2 * n + 1], scatter, receive=True):
            cp.wait_send()
            cp.wait_recv()

    hbm = lambda a: pltpu.HBM(a.shape, a.dtype)
    outs = pl.pallas_call(
        body, name=name, out_shape=tuple(hbm(a) for a in (*srcs, *lands)),
        in_specs=[_HBM] * (2 * n) + [_SEM, _SEM] + [pl.BlockSpec(memory_space=pl.ANY)] * len(after),
        out_specs=tuple([_HBM] * (2 * n)), input_output_aliases={i: i for i in range(2 * n)},
        compiler_params=pltpu.CompilerParams(has_side_effects=_EFFECT),
    )(*srcs, *lands, send_sems, recv_sems, *after)
    return outs[:n], outs[n:]


def _landing_zones(srcs, behind, name):
    n, nb = len(srcs), len(behind)

    def body(*refs):
        src_refs, land_refs, bufs, sems = refs[:n], refs[n + nb:2 * n + nb], refs[2 * n + nb:3 * n + nb], refs[3 * n + nb]
        me = _dev_index(_mesh_pos())
        load = [pltpu.make_async_copy(src, buf, sems.at[a]) for a, (src, buf) in enumerate(zip(src_refs, bufs))]
        store = [pltpu.make_async_copy(buf, land.at[me], sems.at[a]) for a, (buf, land) in enumerate(zip(bufs, land_refs))]
        for cp in load:
            cp.start()
        for ld, st in zip(load, store):
            ld.wait()
            st.start()
        for cp in store:
            cp.wait()

    any_spec = pl.BlockSpec(memory_space=pl.ANY)
    return pl.pallas_call(
        body, name=name, out_shape=[jax.ShapeDtypeStruct((N_DEV,) + s.shape, s.dtype) for s in srcs],
        in_specs=[any_spec] * (n + nb), out_specs=[any_spec] * n,
        scratch_shapes=[pltpu.VMEM(s.shape, s.dtype) for s in srcs] + [pltpu.SemaphoreType.DMA((n,))],
        compiler_params=pltpu.CompilerParams(vmem_limit_bytes=V7X_VMEM_LIMIT),
    )(*srcs, *behind)


def _ffn_forward(x, mod, norm_g, w, tag):
    sh, sc, gate = mod
    h = _modnorm(x, norm_g, sc, sh, f"{tag}_norm")
    u = _ffn_up(h, w["up_t"], f"{tag}_up")
    act, z = _ffn_act(u, w["dw_w"], w["dw_b"], f"{tag}_act")
    y, x_new = _matmul(act, w["down"], "nn", BF16, f"{tag}_down", resid=(x, gate))
    return x_new, (x, h, u, z, act, y)


def _behind(row, token):
    return row if token is None else row + token[0:1, 0:1]


def _ffn_backward(dx_new, dy, d_gate, saved, mod, norm_g, w, tag, emit, below):
    x, h, u, z, act, _ = saved
    _, sc, _ = mod
    d_down = _matmul_tn_acc(act, dy, f"{tag}_down_dw")
    dact = _matmul(dy, w["down"], "nt", BF16, f"{tag}_down_dx")
    du, d_dw_w, d_dw_b = _ffn_act_bwd(u, z, dact, w["dw_w"], f"{tag}_act_bwd")
    d_up_t = _matmul_tn_acc(du, h, f"{tag}_up_dw").reshape(2 * FFN_DIM, -1)
    token = emit([d_up_t, d_down])
    dh = _ffn_up_dx(du, w["up_t"], f"{tag}_up_dx")
    dx, d_w, d_sh, *dy_below = _modnorm_bwd(x, dh, norm_g, _behind(sc, token), dx_new, below, f"{tag}_norm_bwd")
    return (dx, *dy_below), dict(dw_w=d_dw_w.transpose(1, 0, 2).reshape(FFN_CONV_WIDTH, 2 * FFN_DIM),
                    dw_b=d_dw_b.reshape(1, 2 * FFN_DIM), norm_g=d_w * (1.0 + sc), sh=d_sh, sc=d_w * norm_g, gate=d_gate)


def _mixer_forward(x, mod, norm_g, w, rope, tag):
    sh, sc, gate = mod
    h = _modnorm(x, norm_g, sc, sh, f"{tag}_norm")
    z = _matmul(h, w["w_in_t"], "nt", BF16, f"{tag}_in")
    ya = _gmlp_fwd(z, w["gain"], w["wtril"], w["bias_exp"], f"{tag}_gmlp")
    q, k, v = _qk_prep(z, rope[0], rope[1], w["gq"], w["gk"], w["seg"], f"{tag}_qk")
    outs, lses = zip(*[_attn_fwd(q[b], k[b], v[b], dil, f"{tag}_attn_d{dil}") for b, dil in enumerate(DILATIONS)])
    yb, lse, cat = _attn_merge(outs, lses, ya, f"{tag}_merge")
    y, x_new = _matmul(cat, w["w_out"], "nn", BF16, f"{tag}_out", resid=(x, gate))
    return x_new, (x, h, z, q, k, v, yb, lse, cat, y)


def _mixer_backward(dx_new, dy, d_gate, saved, mod, norm_g, w, rope, tag, emit, below):
    x, h, z, q, k, v, yb, lse, cat, _ = saved
    _, sc, _ = mod
    d_w_out = _matmul_tn_acc(cat, dy, f"{tag}_out_dw")
    dcat = _matmul(dy, w["w_out"], "nt", BF16, f"{tag}_out_dx")
    dz_a, d_sp_w, d_gain, d_bias_exp = _gmlp_bwd(z, dcat, w["gain"], w["wtril"], w["wtril_t"], w["bias_exp"], f"{tag}_gmlp_bwd")
    dyb = _subseq_views(dcat, A_WIDTH // B_WIDTH, f"{tag}_dyb_views")
    dqs, dks, dvs = zip(*[_attn_bwd(q[b], k[b], v[b], dyb[b], yb[b], lse[b], dil, f"{tag}_attn_bwd_d{dil}")
                          for b, dil in enumerate(DILATIONS)])
    dz_qkv, d_gq, d_gk = _qk_prep_bwd(z, dqs, dks, dvs, rope[0], rope[1], w["gq"], w["gk"], w["seg"], f"{tag}_qk_bwd")
    dz = jnp.concatenate([dz_a, dz_qkv], axis=1)
    d_w_in_t = _matmul_tn_acc(dz, h, f"{tag}_in_dw")
    token = emit([d_w_in_t, d_w_out])
    dh = _matmul(dz, w["w_in_t"], "nn", F32, f"{tag}_in_dx")
    dx, d_w, d_sh, *dy_below = _modnorm_bwd(x, dh, norm_g, _behind(sc, token), dx_new, below, f"{tag}_norm_bwd")
    return (dx, *dy_below), dict(
        vnorm_g=d_gain.reshape(A_GROUPS, GROUP_DIM), spatial_w=d_sp_w,
        spatial_b=d_bias_exp.reshape(CHUNK, A_GROUPS, GROUP_DIM).sum(-1).T,
        q_norm_g=d_gq.reshape(HEADS, HEAD_DIM).sum(0), k_norm_g=d_gk.reshape(HEADS, HEAD_DIM).sum(0),
        norm_g=d_w * (1.0 + sc), sh=d_sh, sc=d_w * norm_g, gate=d_gate)


def _conformer_forward(x, mod, norm_g, w, tag):
    sh, sc, gate = mod
    h = _modnorm(x, norm_g, sc, sh, f"{tag}_norm")
    p = _matmul(h, w["pw1_t"], "nt", BF16, f"{tag}_pw1", bias=w["pw1_b"])
    s, dc = _conformer_mid(p, w["dw_w"], w["dw_b"], w["ln_g"], w["ln_b"], f"{tag}_mid")
    y, x_new = _matmul(s, w["pw2"], "nn", BF16, f"{tag}_pw2", bias=w["pw2_b"], resid=(x, gate))
    return x_new, (x, h, p, dc, s, y)


def _conformer_backward(dx_new, dy, d_gate, saved, mod, norm_g, w, tag, emit, below):
    x, h, p, dc, s, _ = saved
    _, sc, _ = mod
    d_pw2 = _matmul_tn_acc(s, dy, f"{tag}_pw2_dw")
    d_pw2_b = _colsum_call(dy, f"{tag}_pw2_db")
    ds = _matmul(dy, w["pw2"], "nt", BF16, f"{tag}_pw2_dx")
    ddc, d_dw_w, d_dw_b, d_ln_g, d_ln_b = _conformer_mid_bwd(p, dc, ds, w["ln_g"], w["ln_b"], f"{tag}_mid_bwd")
    dp, d_pw1_b = _conformer_glu_bwd(p, ddc, w["dw_w"], f"{tag}_glu_bwd")
    d_pw1_t = _matmul_tn_acc(dp, h, f"{tag}_pw1_dw")
    token = emit([d_pw1_t, d_pw2])
    dh = _matmul(dp, w["pw1_t"], "nn", F32, f"{tag}_pw1_dx")
    dx, d_w, d_sh, *dy_below = _modnorm_bwd(x, dh, norm_g, _behind(sc, token), dx_new, below, f"{tag}_norm_bwd")
    return (dx, *dy_below), dict(pw1_b=d_pw1_b, dw_w=d_dw_w, dw_b=d_dw_b, ln_g=d_ln_g, ln_b=d_ln_b, pw2_b=d_pw2_b, norm_g=d_w * (1.0 + sc), sh=d_sh, sc=d_w * norm_g, gate=d_gate)


def _local_step(x, target, pos, mod, norm_mix_g, norm_ffn_g, mixer_w, conv_w, ffn_w, fetch, emit):
    d = D_MODEL
    inv_freq = 1.0 / (ROPE_THETA ** (jnp.arange(0, HEAD_DIM, 2, dtype=F32) / HEAD_DIM))
    inv_freq = jnp.tile(inv_freq, 2 * HEADS)[None, :]
    sign = jnp.tile(jnp.concatenate([-jnp.ones(HEAD_DIM // 2, F32), jnp.ones(HEAD_DIM // 2, F32)]), HEADS)[None, :]
    rope = _rope_tables(pos, inv_freq, sign, "rope_tables")
    mods = [[mod[l:l + 1, i * d:(i + 1) * d] for i in range(6)] for l in range(2)]
    mix = [(m[0], m[1], m[2]) for m in mods]
    ffn = [(m[3], m[4], m[5]) for m in mods]
    gm = [norm_mix_g[l:l + 1] for l in range(2)]
    gf = [norm_ffn_g[l:l + 1] for l in range(2)]

    mixer_w = {**mixer_w, **fetch("l0_mix", x)}
    x1, s_mix = _mixer_forward(x, mix[0], gm[0], mixer_w, rope, "l0_mix")
    ffn_w0 = {**ffn_w[0], **fetch("l0_ffn", x1)}
    x2, s_ffn0 = _ffn_forward(x1, ffn[0], gf[0], ffn_w0, "l0_ffn")
    conv_w = {**conv_w, **fetch("l1_conv", x2)}
    x3, s_conv = _conformer_forward(x2, mix[1], gm[1], conv_w, "l1_conv")
    ffn_w1 = {**ffn_w[1], **fetch("l1_ffn", x3)}
    x4, s_ffn1 = _ffn_forward(x3, ffn[1], gf[1], ffn_w1, "l1_ffn")
    below = lambda saved, m: (saved[-1], m[2])
    dx, loss, dy, dg = _loss_head(x4, target, below(s_ffn1, ffn[1]), "loss_head")
    (dx, dy, dg), g_ffn1 = _ffn_backward(dx, dy, dg, s_ffn1, ffn[1], gf[1], ffn_w1, "l1_ffn",
                                         functools.partial(emit, "l1_ffn"), below(s_conv, mix[1]))
    (dx, dy, dg), g_conv = _conformer_backward(dx, dy, dg, s_conv, mix[1], gm[1], conv_w, "l1_conv",
                                               functools.partial(emit, "l1_conv"), below(s_ffn0, ffn[0]))
    (dx, dy, dg), g_ffn0 = _ffn_backward(dx, dy, dg, s_ffn0, ffn[0], gf[0], ffn_w0, "l0_ffn",
                                         functools.partial(emit, "l0_ffn"), below(s_mix, mix[0]))
    (dx,), g_mix = _mixer_backward(dx, dy, dg, s_mix, mix[0], gm[0], mixer_w, rope, "l0_mix",
                                   functools.partial(emit, "l0_mix"), None)
    blocks = [g_mix, g_ffn0, g_conv, g_ffn1]
    dmod = jnp.stack([jnp.concatenate([a["sh"], a["sc"], a["gate"], b["sh"], b["sc"], b["gate"]], axis=1)[0]
                      for a, b in ((g_mix, g_ffn0), (g_conv, g_ffn1))])
    return loss, dx, dmod, blocks


def _pack(arrs, rows=8):
    flat = jnp.concatenate([a.reshape(-1).astype(F32) for a in arrs])
    n = flat.shape[0]
    cols = -(-n // (rows * 128)) * 128
    return jnp.pad(flat, (0, rows * cols - n)).reshape(rows, cols)


def _unpack(flat, shapes):
    out, off = [], 0
    for shp in shapes:
        n = math.prod(shp)
        out.append(flat[..., off:off + n].reshape(flat.shape[:-1] + tuple(shp)))
        off += n
    return out


def _take_block(a, idx, size, axis):
    return lax.dynamic_slice_in_dim(a, idx * size, size, axis)


def kernel(x, c, positions, ada_w, ada_b, norm_mix_g, norm_ffn_g, ab_w_in, a_vnorm_g, a_spatial_w, a_spatial_b, b_q_norm_g, b_k_norm_g, ab_w_out, conv_pw1_w, conv_pw1_b, conv_dw_w, conv_dw_b, conv_ln_g, conv_ln_b, conv_pw2_w, conv_pw2_b, ffn_up_w, ffn_dw_w, ffn_dw_b, ffn_down_w, loss_target, m_ada_w, m_ada_b, m_norm_mix_g, m_norm_ffn_g, m_ab_w_in, m_a_vnorm_g, m_a_spatial_w, m_a_spatial_b, m_b_q_norm_g, m_b_k_norm_g, m_ab_w_out, m_conv_pw1_w, m_conv_pw1_b, m_conv_dw_w, m_conv_dw_b, m_conv_ln_g, m_conv_ln_b, m_conv_pw2_w, m_conv_pw2_b, m_ffn_up_w, m_ffn_dw_w, m_ffn_dw_b, m_ffn_down_w, v_ada_w, v_ada_b, v_norm_mix_g, v_norm_ffn_g, v_ab_w_in, v_a_vnorm_g, v_a_spatial_w, v_a_spatial_b, v_b_q_norm_g, v_b_k_norm_g, v_ab_w_out, v_conv_pw1_w, v_conv_pw1_b, v_conv_dw_w, v_conv_dw_b, v_conv_ln_g, v_conv_ln_b, v_conv_pw2_w, v_conv_pw2_b, v_ffn_up_w, v_ffn_dw_w, v_ffn_dw_b, v_ffn_down_w):
    weights = dict(ada_w=ada_w, ada_b=ada_b, norm_mix_g=norm_mix_g, norm_ffn_g=norm_ffn_g, ab_w_in=ab_w_in, a_vnorm_g=a_vnorm_g, a_spatial_w=a_spatial_w, a_spatial_b=a_spatial_b, b_q_norm_g=b_q_norm_g, b_k_norm_g=b_k_norm_g, ab_w_out=ab_w_out, conv_pw1_w=conv_pw1_w, conv_pw1_b=conv_pw1_b, conv_dw_w=conv_dw_w, conv_dw_b=conv_dw_b, conv_ln_g=conv_ln_g, conv_ln_b=conv_ln_b, conv_pw2_w=conv_pw2_w, conv_pw2_b=conv_pw2_b, ffn_up_w=ffn_up_w, ffn_dw_w=ffn_dw_w, ffn_dw_b=ffn_dw_b, ffn_down_w=ffn_down_w)
    mom1 = dict(ada_w=m_ada_w, ada_b=m_ada_b, norm_mix_g=m_norm_mix_g, norm_ffn_g=m_norm_ffn_g, ab_w_in=m_ab_w_in, a_vnorm_g=m_a_vnorm_g, a_spatial_w=m_a_spatial_w, a_spatial_b=m_a_spatial_b, b_q_norm_g=m_b_q_norm_g, b_k_norm_g=m_b_k_norm_g, ab_w_out=m_ab_w_out, conv_pw1_w=m_conv_pw1_w, conv_pw1_b=m_conv_pw1_b, conv_dw_w=m_conv_dw_w, conv_dw_b=m_conv_dw_b, conv_ln_g=m_conv_ln_g, conv_ln_b=m_conv_ln_b, conv_pw2_w=m_conv_pw2_w, conv_pw2_b=m_conv_pw2_b, ffn_up_w=m_ffn_up_w, ffn_dw_w=m_ffn_dw_w, ffn_dw_b=m_ffn_dw_b, ffn_down_w=m_ffn_down_w)
    mom2 = dict(ada_w=v_ada_w, ada_b=v_ada_b, norm_mix_g=v_norm_mix_g, norm_ffn_g=v_norm_ffn_g, ab_w_in=v_ab_w_in, a_vnorm_g=v_a_vnorm_g, a_spatial_w=v_a_spatial_w, a_spatial_b=v_a_spatial_b, b_q_norm_g=v_b_q_norm_g, b_k_norm_g=v_b_k_norm_g, ab_w_out=v_ab_w_out, conv_pw1_w=v_conv_pw1_w, conv_pw1_b=v_conv_pw1_b, conv_dw_w=v_conv_dw_w, conv_dw_b=v_conv_dw_b, conv_ln_g=v_conv_ln_g, conv_ln_b=v_conv_ln_b, conv_pw2_w=v_conv_pw2_w, conv_pw2_b=v_conv_pw2_b, ffn_up_w=v_ffn_up_w, ffn_dw_w=v_ffn_dw_w, ffn_dw_b=v_ffn_dw_b, ffn_down_w=v_ffn_down_w)
    order = list(weights)
    d, f2 = D_MODEL, 2 * FFN_DIM
    t = x.shape[1]
    me = 4 * lax.axis_index("x") + 2 * lax.axis_index("y") + lax.axis_index("c")
    for window, dil in PATTERNS:
        assert window // dil == Q_BLOCK and t % (dil * Q_BLOCK) == 0

    small_in = [c[0], conv_pw1_b[0], conv_dw_w[0], conv_dw_b[0], conv_ln_g[0], conv_ln_b[0], conv_pw2_b[0], ffn_dw_w]
    g1 = _all_gather_vmem(_pack(small_in, rows=8), "gather_small").reshape(N_DEV, -1)
    c_all, pw1_b, dw_w, dw_b, ln_g, ln_b, pw2_b, fdw_w = _unpack(g1, [a.shape for a in small_in])
    pw1_b, dw_b, ln_g, ln_b, pw2_b = [a.reshape(1, -1) for a in (pw1_b, dw_b, ln_g, ln_b, pw2_b)]
    dw_w = dw_w.transpose(1, 0, 2).reshape(CONV_WIDTH, d)
    fdw_w = fdw_w.transpose(1, 2, 0, 3).reshape(2, FFN_CONV_WIDTH, f2)

    c16 = jnp.pad(c_all, ((0, 2 * N_DEV - c_all.shape[0]), (0, 0)))
    part = jnp.concatenate([_ada_fwd(c16, ada_w[l], f"ada_fwd{l}")[:N_DEV] for l in range(2)], axis=1)
    g2 = _all_gather_vmem(part, "gather_mod").reshape(N_DEV, N_DEV, 2, -1)
    mod = lax.dynamic_index_in_dim(g2, me, axis=1, keepdims=False).transpose(1, 0, 2).reshape(2, 6 * d) + ada_b

    stages = dict(l0_mix=[ab_w_in[0].T, ab_w_out[0]], l0_ffn=[ffn_up_w[0].T, ffn_down_w[0]],
                  l1_conv=[conv_pw1_w[0].T, conv_pw2_w[0]], l1_ffn=[ffn_up_w[1].T, ffn_down_w[1]])
    stages = {k: [s.astype(BF16) for s in v] for k, v in stages.items()}
    names = dict(l0_mix=("w_in_t", "w_out"), l0_ffn=("up_t", "down"), l1_conv=("pw1_t", "pw2"), l1_ffn=("up_t", "down"))
    ready = {"l0_mix": [a.reshape(-1, d) for a in _all_gather_hbm(stages["l0_mix"], "gather_mixer_weights")]}
    behind = (*ready["l0_mix"], mod)
    arriving = {}
    for stage, group in (("l0_ffn", ("l0_ffn",)), ("l1_conv", ("l1_conv", "l1_ffn"))):
        srcs = [s for g in group for s in stages[g]]
        arriving[stage], token = _pushes_start(
            srcs, _landing_zones(srcs, behind, f"gather_{stage}_zones"), False, f"gather_{stage}_start")
        behind = (token,)
        mod = mod + token[0:1, 0:1]

    def fetch(stage, after):
        if stage in arriving:
            full = [a.reshape(-1, d) for a in _pushes_wait(arriving[stage], after, f"gather_{stage}_wait")[1]]
            ready[stage] = full[:2]
            if stage == "l1_conv":
                ready["l1_ffn"] = full[2:]
        return dict(zip(names[stage], ready[stage]))

    causal = jnp.tril(jnp.ones((CHUNK, CHUNK), bool))
    wtril = jnp.where(causal[None], a_spatial_w[0], 0.0)
    mixer_w = dict(
        gain=a_vnorm_g[0].reshape(1, A_WIDTH), wtril=wtril.astype(BF16),
        wtril_t=wtril.transpose(0, 2, 1).astype(BF16),
        bias_exp=jnp.repeat(a_spatial_b[0].T, GROUP_DIM, axis=1),
        gq=jnp.tile(b_q_norm_g[0], HEADS)[None, :], gk=jnp.tile(b_k_norm_g[0], HEADS)[None, :],
        seg=jnp.kron(jnp.eye(HEADS, dtype=BF16), jnp.ones((HEAD_DIM, HEAD_DIM), BF16)))
    conv_w = dict(pw1_b=pw1_b, dw_w=dw_w, dw_b=dw_b, ln_g=ln_g, ln_b=ln_b, pw2_b=pw2_b)
    ffn_w = [dict(dw_w=fdw_w[l].reshape(FFN_CONV_WIDTH, 2, FFN_DIM).transpose(1, 0, 2), dw_b=ffn_dw_b[l].reshape(2, 1, FFN_DIM))
             for l in range(2)]

    leaving = {}

    def emit(stage, grads):
        blocks = [g.reshape(N_DEV, g.shape[0] // N_DEV, d) for g in grads]
        leaving[stage], token = _pushes_start(
            blocks, [lax.empty(b.shape, b.dtype) for b in blocks], True, f"reduce_{stage}_start")
        return token

    loss, dx, dmod, (g_mix, g_ffn0, g_conv, g_ffn1) = _local_step(
        x[0], loss_target[0], positions[0].astype(F32)[:, None], mod, norm_mix_g, norm_ffn_g, mixer_w, conv_w, ffn_w,
        fetch, emit)

    me_op = me.astype(jnp.int32).reshape(1)

    def reduced(stage, after):
        blocks, lands = _pushes_wait(leaving[stage], after, f"reduce_{stage}_wait")
        return [_sum_with_own(b, a, me_op, f"reduce_{stage}_sum{i}") for i, (b, a) in enumerate(zip(blocks, lands))]

    (r_up_t1, r_down1), (r_pw1_t, r_pw2), (r_up_t0, r_down0) = [reduced(s, dx) for s in ("l1_ffn", "l1_conv", "l0_ffn")]

    small_g = [
        dmod, jnp.concatenate([g_mix["norm_g"], g_conv["norm_g"]]), jnp.concatenate([g_ffn0["norm_g"], g_ffn1["norm_g"]]),
        g_mix["vnorm_g"], g_mix["spatial_w"], g_mix["spatial_b"], g_mix["q_norm_g"], g_mix["k_norm_g"],
        g_conv["pw1_b"], g_conv["dw_w"], g_conv["dw_b"], g_conv["ln_g"], g_conv["ln_b"], g_conv["pw2_b"],
        jnp.stack([g_ffn0["dw_w"], g_ffn1["dw_w"]]), jnp.concatenate([g_ffn0["dw_b"], g_ffn1["dw_b"]])]
    packed = _pack(small_g, rows=8)
    small_leaving, token = _pushes_start([packed], [lax.empty((N_DEV,) + packed.shape, F32)], False, "gather_small_grads_start")

    grads = dict(conv_pw2_w=r_pw2[None], ffn_down_w=jnp.stack([r_down0, r_down1]))
    grads_t = dict(conv_pw1_w=r_pw1_t[None], ffn_up_w=jnp.stack([r_up_t0, r_up_t1]))
    flip = lambda a: jnp.swapaxes(a, 1, 2)
    delta, new_m, new_v = {}, {}, {}

    def update(name, behind):
        if name in grads_t:
            grads[name] = flip(grads_t[name])
            res = _adamw(flip(weights[name]), grads_t[name], flip(mom1[name]), flip(mom2[name]), behind, f"adamw_{name}")
            delta[name], new_m[name], new_v[name] = [flip(r) for r in res]
        else:
            delta[name], new_m[name], new_v[name] = _adamw(
                weights[name], grads[name], mom1[name], mom2[name], behind, f"adamw_{name}")

    for name in ("conv_pw1_w", "conv_pw2_w", "ffn_up_w", "ffn_down_w"):
        update(name, token)
    r_in_t, r_out = reduced("l0_mix", new_v["ffn_down_w"])
    grads_t["ab_w_in"], grads["ab_w_out"] = r_in_t[None], r_out[None]
    update("ab_w_in", token)
    update("ab_w_out", token)

    (packed,), (landed,) = _pushes_wait(small_leaving, tuple(new_v.values()), "gather_small_grads_wait")
    total = _sum_in_device_order(packed, landed, me_op, "sum_small_grads")
    (s_dmod, s_mix_g, s_ffn_g, s_vnorm, s_sp_w, s_sp_b, s_gq, s_gk, s_pw1_b, s_dw_w, s_dw_b, s_ln_g, s_ln_b,
     s_pw2_b, s_fdw_w, s_fdw_b) = _unpack(total.reshape(-1), [a.shape for a in small_g])
    dmod_all = lax.dynamic_update_slice(
        landed.reshape(N_DEV, -1)[:, :dmod.size].reshape((N_DEV,) + dmod.shape), dmod[None], (me, 0, 0))
    n_ada = ada_w.shape[2]
    dmod16 = jnp.pad(_take_block(dmod_all, me, n_ada, 2), ((0, N_DEV), (0, 0), (0, 0)))
    grads.update(
        ada_w=jnp.stack([_ada_bwd(c16, dmod16[:, l], f"ada_bwd{l}") for l in range(2)]),
        ada_b=s_dmod, norm_mix_g=s_mix_g, norm_ffn_g=s_ffn_g,
        a_vnorm_g=s_vnorm[None], a_spatial_w=s_sp_w[None], a_spatial_b=s_sp_b[None], b_q_norm_g=s_gq[None],
        b_k_norm_g=s_gk[None],
        conv_pw1_b=_take_block(s_pw1_b, me, conv_pw1_b.shape[1], 1),
        conv_dw_w=_take_block(s_dw_w, me, conv_dw_w.shape[2], 1)[None],
        conv_dw_b=_take_block(s_dw_b, me, conv_dw_b.shape[1], 1), conv_ln_g=_take_block(s_ln_g, me, conv_ln_g.shape[1], 1),
        conv_ln_b=_take_block(s_ln_b, me, conv_ln_b.shape[1], 1),
        conv_pw2_b=_take_block(s_pw2_b, me, conv_pw2_b.shape[1], 1),
        ffn_dw_w=_take_block(s_fdw_w, me, ffn_dw_w.shape[2], 2), ffn_dw_b=s_fdw_b)
    update("ada_w", None)
    large = ("ada_w", "conv_pw1_w", "conv_pw2_w", "ffn_up_w", "ffn_down_w", "ab_w_in", "ab_w_out")
    small = [n for n in order if n not in large]
    res = _adamw_small(*[[src[n] for n in small] for src in (weights, grads, mom1, mom2)], "adamw_small")
    for dst, arrs in zip((delta, new_m, new_v), res):
        dst.update(zip(small, arrs))

    loss = lax.psum(loss[0, 0], ("x", "y", "c"))
    return (loss, dx[None], *[grads[n] for n in order], *[delta[n] for n in order],
            *[new_m[n] for n in order], *[new_v[n] for n in order])
```

```python
import functools
import math

import jax
import jax.numpy as jnp
from jax import lax
from jax.experimental import pallas as pl
from jax.experimental.pallas import tpu as pltpu

F32 = jnp.float32
BF16 = jnp.bfloat16
MESH = pl.DeviceIdType.MESH

D_MODEL = 1024
A_WIDTH = 512
A_GROUPS = 4
GROUP_DIM = 128
CHUNK = 128
B_WIDTH = 512
HEADS = 8
HEAD_DIM = 64
PATTERNS = ((128, 1), (512, 4), (2048, 16))
Q_BLOCK = 128
ROPE_THETA = 10000.0
AB_IN = 2560
CONV_WIDTH = 31
FFN_DIM = 2816
FFN_CONV_WIDTH = 3
EPS = 1e-6
NEG = -1e30
N_DEV = 8
ADAM_LR, ADAM_B1, ADAM_B2, ADAM_EPS, ADAM_WD, ADAM_STEP = 0.001, 0.9, 0.999, 1e-08, 0.01, 10

V7X_VMEM_LIMIT = 56 * 2**20
FFN_HALO = 16
CONV_HALO = 32

_NN = (((1,), (0,)), ((), ()))
_NT = (((1,), (1,)), ((), ()))
_TN = (((0,), (0,)), ((), ()))


def _tile(n, prefs=(512, 256, 128)):
    for t in prefs:
        if n % t == 0:
            return t
    return n


def _row_tile(n, cap=512):
    best = n
    for t in range(8, min(n, cap) + 1, 8):
        if n % t == 0:
            best = t
    return best if best <= cap else n


def _params(*sem):
    return pltpu.CompilerParams(dimension_semantics=sem, vmem_limit_bytes=V7X_VMEM_LIMIT)


def _dot(a, b, dims):
    return lax.dot_general(a, b, dims, preferred_element_type=F32)


def _sigmoid(x):
    return 1.0 / (1.0 + jnp.exp(-x))


def _gelu(x):
    return 0.5 * x * (1.0 + lax.erf(x * (2.0 ** -0.5)))


def _gelu_grad(x):
    return 0.5 * (1.0 + lax.erf(x * (2.0 ** -0.5))) + x * jnp.exp(-0.5 * x * x) * (1.0 / math.sqrt(2.0 * math.pi))


def _colsum(v):
    return jnp.sum(v, axis=0, keepdims=True)


MATMUL_VMEM_BUDGET = 40 * 2**20


def _matmul_tiles(m, n, k, out_bytes, with_resid):
    def options(dim):
        opts = [t for t in (1024, 512, 256, 128) if dim % t == 0]
        return opts + [dim] if dim <= 4096 and dim not in opts else opts

    best = None
    for tm in options(m):
        for tn in options(n):
            need = 4 * (tm * k + k * tn) + tm * tn * (4 + 2 * out_bytes) + (24 * tm * tn if with_resid else 0)
            if need <= MATMUL_VMEM_BUDGET and (best is None or tm * tn / (tm + tn) > best[0]):
                best = (tm * tn / (tm + tn), tm, tn)
    return best[1], best[2]


def _matmul_tn_acc(a, b, name, tk=1024):
    squeeze = a.ndim == 2
    a3 = a[None] if squeeze else a
    p_, t, m = a3.shape
    n = b.shape[1]
    nk = t // tk

    def body(a_ref, b_ref, o_ref, acc_ref):
        kt = pl.program_id(1)

        @pl.when(kt == 0)
        def _():
            acc_ref[...] = jnp.zeros_like(acc_ref)

        acc_ref[...] += _dot(a_ref[...], b_ref[...], _TN)

        @pl.when(kt == nk - 1)
        def _():
            o_ref[...] = acc_ref[...].astype(BF16)

    out = pl.pallas_call(
        body, name=name, grid=(p_, nk),
        in_specs=[pl.BlockSpec((None, tk, m), lambda p, kt: (p, kt, 0)), pl.BlockSpec((tk, n), lambda p, kt: (kt, 0))],
        out_specs=pl.BlockSpec((None, m, n), lambda p, kt: (p, 0, 0)), out_shape=jax.ShapeDtypeStruct((p_, m, n), BF16),
        scratch_shapes=[pltpu.VMEM((m, n), F32)], compiler_params=_params("parallel", "arbitrary"),
    )(a3, b)
    return out[0] if squeeze else out


def _matmul(a, b, mode, out_dtype, name, bias=None, resid=None):
    if mode == "nn":
        (m, k), (_, n) = a.shape, b.shape
    elif mode == "nt":
        (m, k), (n, _) = a.shape, b.shape
    else:
        (k, m), (_, n) = a.shape, b.shape
    tm, tn = _matmul_tiles(m, n, k, jnp.dtype(out_dtype).itemsize, resid is not None)
    dims = {"nn": _NN, "nt": _NT, "tn": _TN}[mode]
    a_spec = pl.BlockSpec((k, tm), lambda i, j: (0, i)) if mode == "tn" else pl.BlockSpec((tm, k), lambda i, j: (i, 0))
    b_spec = pl.BlockSpec((tn, k), lambda i, j: (j, 0)) if mode == "nt" else pl.BlockSpec((k, tn), lambda i, j: (0, j))
    in_specs, args = [a_spec, b_spec], [a, b]
    row_spec = pl.BlockSpec((1, tn), lambda i, j: (0, j))
    tile_spec = pl.BlockSpec((tm, tn), lambda i, j: (i, j))
    if bias is not None:
        in_specs.append(row_spec)
        args.append(bias)
    if resid is not None:
        in_specs += [tile_spec, row_spec]
        args += list(resid)
    out_shape = [jax.ShapeDtypeStruct((m, n), out_dtype)]
    out_specs = [tile_spec]
    if resid is not None:
        out_shape.append(jax.ShapeDtypeStruct((m, n), F32))
        out_specs.append(tile_spec)

    def body(*refs):
        a_ref, b_ref = refs[0], refs[1]
        pos = 2
        acc = _dot(a_ref[...], b_ref[...], dims)
        if bias is not None:
            acc = acc + refs[pos][...]
            pos += 1
        if resid is not None:
            x_ref, g_ref = refs[pos], refs[pos + 1]
            pos += 2
        refs[pos][...] = acc.astype(out_dtype)
        if resid is not None:
            refs[pos + 1][...] = x_ref[...] + g_ref[...] * acc

    outs = pl.pallas_call(
        body, name=name, grid=(m // tm, n // tn), in_specs=in_specs, out_specs=out_specs, out_shape=out_shape,
        compiler_params=_params("parallel", "parallel"),
    )(*args)
    return outs if resid is not None else outs[0]


NORM_TM = (1024, 512, 256, 128)


def _modnorm(x, g, sc, sh, name):
    t, d = x.shape
    tm = _tile(t, NORM_TM)
    row = pl.BlockSpec((1, d), lambda i: (0, 0))
    blk = pl.BlockSpec((tm, d), lambda i: (i, 0))

    def body(x_ref, g_ref, sc_ref, sh_ref, o_ref):
        x = x_ref[...]
        r = lax.rsqrt(jnp.mean(x * x, axis=-1, keepdims=True) + EPS)
        o_ref[...] = ((x * r) * g_ref[...] * (1.0 + sc_ref[...]) + sh_ref[...]).astype(BF16)

    return pl.pallas_call(
        body, name=name, grid=(t // tm,), in_specs=[blk, row, row, row], out_specs=blk,
        out_shape=jax.ShapeDtypeStruct((t, d), BF16), compiler_params=_params("parallel"),
    )(x, g, sc, sh)


def _gate_bwd_tile(dx, y_ref, gate_ref, dy_ref, dgate_ref, first):
    @pl.when(first)
    def _():
        dgate_ref[...] = jnp.zeros_like(dgate_ref)

    dy_ref[...] = (dx * gate_ref[...]).astype(BF16)
    dgate_ref[...] += _colsum(dx * y_ref[...].astype(F32))


def _modnorm_bwd(x, dh, g, sc, dres, below, name):
    t, d = x.shape
    tm = _tile(t, NORM_TM)
    row = pl.BlockSpec((1, d), lambda i: (0, 0))
    blk = pl.BlockSpec((tm, d), lambda i: (i, 0))

    def body(x_ref, dh_ref, g_ref, sc_ref, dres_ref, *rest):
        dx_ref, dw_ref, dsh_ref = rest[-5:-2] if below else rest
        first = pl.program_id(0) == 0

        @pl.when(first)
        def _():
            dw_ref[...] = jnp.zeros_like(dw_ref)
            dsh_ref[...] = jnp.zeros_like(dsh_ref)

        x = x_ref[...]
        dh = dh_ref[...].astype(F32)
        r = lax.rsqrt(jnp.mean(x * x, axis=-1, keepdims=True) + EPS)
        xn = x * r
        dxn = dh * (g_ref[...] * (1.0 + sc_ref[...]))
        dx = dres_ref[...] + r * (dxn - xn * jnp.mean(dxn * xn, axis=-1, keepdims=True))
        dx_ref[...] = dx
        dw_ref[...] += _colsum(dh * xn)
        dsh_ref[...] += _colsum(dh)
        if below:
            _gate_bwd_tile(dx, rest[0], rest[1], rest[-2], rest[-1], first)

    row_out = jax.ShapeDtypeStruct((1, d), F32)
    return pl.pallas_call(
        body, name=name, grid=(t // tm,), in_specs=[blk, blk, row, row, blk] + ([blk, row] if below else []),
        out_specs=[blk, row, row] + ([blk, row] if below else []),
        out_shape=[jax.ShapeDtypeStruct((t, d), F32), row_out, row_out]
        + ([jax.ShapeDtypeStruct((t, d), BF16), row_out] if below else []),
        compiler_params=_params("arbitrary"),
    )(x, dh, g, sc, dres, *(below or ()))


def _loss_head(y, target, below, name):
    t, d = y.shape
    tm = _tile(t, NORM_TM)
    blk = pl.BlockSpec((tm, d), lambda i: (i, 0))
    row = pl.BlockSpec((1, d), lambda i: (0, 0))
    one = pl.BlockSpec((1, 1), lambda i: (0, 0))
    steps = t // tm

    def body(y_ref, t_ref, yb_ref, gate_ref, dx_ref, loss_ref, dy_ref, dgate_ref, acc_ref):
        first = pl.program_id(0) == 0

        @pl.when(first)
        def _():
            acc_ref[...] = jnp.zeros_like(acc_ref)

        e = y_ref[...] - t_ref[...]
        dx = e * (1.0 / d)
        dx_ref[...] = dx
        acc_ref[...] += _colsum(e * e)
        _gate_bwd_tile(dx, yb_ref, gate_ref, dy_ref, dgate_ref, first)

        @pl.when(pl.program_id(0) == steps - 1)
        def _():
            loss_ref[...] = jnp.sum(acc_ref[...], axis=1, keepdims=True) * (0.5 / d)

    return pl.pallas_call(
        body, name=name, grid=(steps,), in_specs=[blk, blk, blk, row], out_specs=[blk, one, blk, row],
        out_shape=[jax.ShapeDtypeStruct((t, d), F32), jax.ShapeDtypeStruct((1, 1), F32),
                   jax.ShapeDtypeStruct((t, d), BF16), jax.ShapeDtypeStruct((1, d), F32)],
        scratch_shapes=[pltpu.VMEM((1, d), F32)], compiler_params=_params("arbitrary"),
    )(y, target, *below)


GMLP_TM = 512


def _group_norm(vg, gain):
    mu = jnp.mean(vg, axis=-1, keepdims=True)
    xc = vg - mu
    rstd = lax.rsqrt(jnp.mean(xc * xc, axis=-1, keepdims=True) + EPS)
    xhat = xc * rstd
    return xhat, rstd, xhat * gain


def _gmlp_fwd(z, gain, wtril, bias_exp, name):
    t = z.shape[0]
    tm = _tile(t, (GMLP_TM,))
    zu = pl.BlockSpec((tm, A_WIDTH), lambda i: (i, 0))
    zv = pl.BlockSpec((tm, A_WIDTH), lambda i: (i, 1))
    full2 = lambda shp: pl.BlockSpec(shp, lambda i: (0, 0))
    w_spec = pl.BlockSpec((A_GROUPS, CHUNK, CHUNK), lambda i: (0, 0, 0))

    def body(zu_ref, zv_ref, gain_ref, w_ref, b_ref, ya_ref):
        for c in range(tm // CHUNK):
            rows = slice(c * CHUNK, (c + 1) * CHUNK)
            ua = _gelu(zu_ref[rows, :].astype(F32))
            vg = _gelu(zv_ref[rows, :].astype(F32))
            for g in range(A_GROUPS):
                sl = slice(g * GROUP_DIM, (g + 1) * GROUP_DIM)
                _, _, vn = _group_norm(vg[:, sl], gain_ref[:, sl])
                f = _dot(w_ref[g], vn.astype(BF16), _NN) + b_ref[:, sl]
                ya_ref[rows, sl] = (ua[:, sl] * f).astype(BF16)

    return pl.pallas_call(
        body, name=name, grid=(t // tm,),
        in_specs=[zu, zv, full2((1, A_WIDTH)), w_spec, full2((CHUNK, A_WIDTH))], out_specs=zu,
        out_shape=jax.ShapeDtypeStruct((t, A_WIDTH + B_WIDTH), BF16), compiler_params=_params("parallel"),
    )(z, z, gain, wtril, bias_exp)


def _gmlp_bwd(z, dcat, gain, wtril, wtril_t, bias_exp, name):
    t = z.shape[0]
    tm = _tile(t, (GMLP_TM,))
    zu = pl.BlockSpec((tm, A_WIDTH), lambda i: (i, 0))
    zv = pl.BlockSpec((tm, A_WIDTH), lambda i: (i, 1))
    full2 = lambda shp: pl.BlockSpec(shp, lambda i: (0, 0))
    w_spec = pl.BlockSpec((A_GROUPS, CHUNK, CHUNK), lambda i: (0, 0, 0))
    dz_spec = pl.BlockSpec((tm, 2 * A_WIDTH), lambda i: (i, 0))

    def body(zu_ref, zv_ref, dya_ref, gain_ref, w_ref, wt_ref, b_ref, dz_ref, dw_ref, dgain_ref, dbias_ref):
        @pl.when(pl.program_id(0) == 0)
        def _():
            dw_ref[...] = jnp.zeros_like(dw_ref)
            dgain_ref[...] = jnp.zeros_like(dgain_ref)
            dbias_ref[...] = jnp.zeros_like(dbias_ref)

        row = lax.broadcasted_iota(jnp.int32, (CHUNK, CHUNK), 0)
        col = lax.broadcasted_iota(jnp.int32, (CHUNK, CHUNK), 1)
        for c in range(tm // CHUNK):
            rows = slice(c * CHUNK, (c + 1) * CHUNK)
            zu_v = zu_ref[rows, :].astype(F32)
            zv_v = zv_ref[rows, :].astype(F32)
            dya = dya_ref[rows, :].astype(F32)
            ua = _gelu(zu_v)
            vg = _gelu(zv_v)
            for g in range(A_GROUPS):
                sl = slice(g * GROUP_DIM, (g + 1) * GROUP_DIM)
                gain_g = gain_ref[:, sl]
                xhat, rstd, vn = _group_norm(vg[:, sl], gain_g)
                vn16 = vn.astype(BF16)
                f = _dot(w_ref[g], vn16, _NN) + b_ref[:, sl]
                df = dya[:, sl] * ua[:, sl]
                df16 = df.astype(BF16)
                dz_ref[rows, sl] = (dya[:, sl] * f * _gelu_grad(zu_v[:, sl])).astype(BF16)
                dw_ref[g] += jnp.where(row >= col, _dot(df16, vn16, _NT), 0.0)
                dvn = _dot(wt_ref[g], df16, _NN)
                dgain_ref[:, sl] += _colsum(dvn * xhat)
                dxh = dvn * gain_g
                dvg = rstd * (dxh - jnp.mean(dxh, axis=-1, keepdims=True) - xhat * jnp.mean(dxh * xhat, axis=-1, keepdims=True))
                dz_ref[rows, A_WIDTH + g * GROUP_DIM:A_WIDTH + (g + 1) * GROUP_DIM] = (dvg * _gelu_grad(zv_v[:, sl])).astype(BF16)
                dbias_ref[:, sl] += df

    return pl.pallas_call(
        body, name=name, grid=(t // tm,),
        in_specs=[zu, zv, zu, full2((1, A_WIDTH)), w_spec, w_spec, full2((CHUNK, A_WIDTH))],
        out_specs=[dz_spec, w_spec, full2((1, A_WIDTH)), full2((CHUNK, A_WIDTH))],
        out_shape=[jax.ShapeDtypeStruct((t, 2 * A_WIDTH), BF16), jax.ShapeDtypeStruct((A_GROUPS, CHUNK, CHUNK), F32),
                   jax.ShapeDtypeStruct((1, A_WIDTH), F32), jax.ShapeDtypeStruct((CHUNK, A_WIDTH), F32)],
        compiler_params=_params("arbitrary"),
    )(z, z, dcat, gain, wtril, wtril_t, bias_exp)


def _rope_tables(pos, inv_freq, sign, name):
    t = pos.shape[0]
    tm = _tile(t)
    row = pl.BlockSpec((1, B_WIDTH), lambda i: (0, 0))
    blk = pl.BlockSpec((tm, B_WIDTH), lambda i: (i, 0))

    def body(pos_ref, f_ref, s_ref, cos_ref, sin_ref):
        ang = pos_ref[...] * f_ref[:, 0:LANES]
        cos_ref[...] = jnp.tile(jnp.cos(ang), (1, B_WIDTH // LANES))
        sin_ref[...] = jnp.tile(jnp.sin(ang) * s_ref[:, 0:LANES], (1, B_WIDTH // LANES))

    return pl.pallas_call(
        body, name=name, grid=(t // tm,), in_specs=[pl.BlockSpec((tm, 1), lambda i: (i, 0)), row, row],
        out_specs=[blk, blk], out_shape=[jax.ShapeDtypeStruct((t, B_WIDTH), F32)] * 2,
        compiler_params=_params("parallel"),
    )(pos, inv_freq, sign)


def _head_sum(v, seg):
    hi = v.astype(BF16)
    lo = (v - hi.astype(F32)).astype(BF16)
    return _dot(hi, seg, _NN) + _dot(lo, seg, _NN)


def _swap_halves(v):
    lane = lax.broadcasted_iota(jnp.int32, v.shape, 1)
    return jnp.where((lane & (HEAD_DIM - 1)) < HEAD_DIM // 2,pltpu.roll(v, B_WIDTH - HEAD_DIM // 2, 1), pltpu.roll(v, HEAD_DIM // 2, 1))


DILATIONS = tuple(dil for _, dil in PATTERNS)
SUBSEQ_TM = 512
LANES = 128


def _subseq_shape(t, dil):
    return (t // dil, dil * B_WIDTH)


def _subseq_spec(tm, dil):
    return pl.BlockSpec((tm // dil, dil * B_WIDTH), lambda i: (i, 0))


def _to_subseq(x, scr_ref, dil):
    if dil == 1:
        return x
    tm, w = x.shape
    for c in range(w // LANES):
        scr_ref[c * tm:(c + 1) * tm, :] = x[:, c * LANES:(c + 1) * LANES]
    return jnp.concatenate([scr_ref[pl.ds(c * tm + r, tm // dil, stride=dil), :]
                            for r in range(dil) for c in range(w // LANES)], axis=1)


def _from_subseq(y, scr_ref, dil):
    if dil == 1:
        return y
    n, w = y.shape[0], y.shape[1] // dil
    tm = n * dil
    for r in range(dil):
        for c in range(w // LANES):
            scr_ref[pl.ds(c * tm + r, n, stride=dil), :] = y[:, r * w + c * LANES:r * w + (c + 1) * LANES]
    return jnp.concatenate([scr_ref[c * tm:(c + 1) * tm, :] for c in range(w // LANES)], axis=1)


def _subseq_scratch(tm):
    return pltpu.VMEM((B_WIDTH // LANES * tm, LANES), F32)


def _qk_prep(z, cos_t, sin_t, gq, gk, seg, name):
    t = z.shape[0]
    tm = _tile(t, (SUBSEQ_TM,))
    col = lambda c: pl.BlockSpec((tm, B_WIDTH), lambda i: (i, c))
    row = pl.BlockSpec((1, B_WIDTH), lambda i: (0, 0))
    blk = col(0)
    nd = len(DILATIONS)

    def body(q_ref, k_ref, v_ref, cos_ref, sin_ref, gq_ref, gk_ref, seg_ref, *rest):
        out_refs, scr_ref = rest[:-1], rest[-1]

        def norm_rot(x, g):
            r = lax.rsqrt(_head_sum(x * x, seg_ref[...]) * (1.0 / HEAD_DIM) + EPS)
            xn = x * r * g
            return xn * cos_ref[...] + _swap_halves(xn) * sin_ref[...]

        vals = (norm_rot(q_ref[...].astype(F32), gq_ref[...]), norm_rot(k_ref[...].astype(F32), gk_ref[...]),
                v_ref[...].astype(F32))
        for a, val in enumerate(vals):
            for b, dil in enumerate(DILATIONS):
                out_refs[a * nd + b][...] = _to_subseq(val, scr_ref, dil).astype(BF16)

    outs = pl.pallas_call(
        body, name=name, grid=(t // tm,),
        in_specs=[col(2), col(3), col(4), blk, blk, row, row, pl.BlockSpec((B_WIDTH, B_WIDTH), lambda i: (0, 0))],
        out_specs=[_subseq_spec(tm, dil) for _ in range(3) for dil in DILATIONS],
        out_shape=[jax.ShapeDtypeStruct(_subseq_shape(t, dil), BF16) for _ in range(3) for dil in DILATIONS],
        scratch_shapes=[_subseq_scratch(tm)], compiler_params=_params("parallel"),
    )(z, z, z, cos_t, sin_t, gq, gk, seg)
    return outs[:nd], outs[nd:2 * nd], outs[2 * nd:]


def _qk_prep_bwd(z, dqs, dks, dvs, cos_t, sin_t, gq, gk, seg, name):
    t = z.shape[0]
    tm = _tile(t, (SUBSEQ_TM,))
    col = lambda c: pl.BlockSpec((tm, B_WIDTH), lambda i: (i, c))
    row = pl.BlockSpec((1, B_WIDTH), lambda i: (0, 0))
    blk = col(0)
    nb = len(DILATIONS)
    subs = [_subseq_spec(tm, dil) for dil in DILATIONS]

    def body(*refs):
        q_ref, k_ref = refs[0], refs[1]
        dq_refs, dk_refs, dv_refs = refs[2:2 + nb], refs[2 + nb:2 + 2 * nb], refs[2 + 2 * nb:2 + 3 * nb]
        cos_ref, sin_ref, gq_ref, gk_ref, seg_ref, dz_ref, dgq_ref, dgk_ref, scr_ref = refs[2 + 3 * nb:]

        @pl.when(pl.program_id(0) == 0)
        def _():
            dgq_ref[...] = jnp.zeros_like(dgq_ref)
            dgk_ref[...] = jnp.zeros_like(dgk_ref)

        def total(d_refs):
            return sum(_from_subseq(r_[...], scr_ref, dil) for r_, dil in zip(d_refs, DILATIONS))

        def back(x, d_refs, g, dg_ref):
            dout = total(d_refs)
            dy = dout * cos_ref[...] + _swap_halves(dout * sin_ref[...])
            r = lax.rsqrt(_head_sum(x * x, seg_ref[...]) * (1.0 / HEAD_DIM) + EPS)
            xn = x * r
            dg_ref[...] += _colsum(dy * xn)
            dxn = dy * g
            return r * (dxn - xn * (_head_sum(dxn * xn, seg_ref[...]) * (1.0 / HEAD_DIM)))

        dz_ref[:, 0:B_WIDTH] = back(q_ref[...].astype(F32), dq_refs, gq_ref[...], dgq_ref).astype(BF16)
        dz_ref[:, B_WIDTH:2 * B_WIDTH] = back(k_ref[...].astype(F32), dk_refs, gk_ref[...], dgk_ref).astype(BF16)
        dz_ref[:, 2 * B_WIDTH:3 * B_WIDTH] = total(dv_refs).astype(BF16)

    return pl.pallas_call(
        body, name=name, grid=(t // tm,),
        in_specs=[col(2), col(3)] + subs * 3 + [blk, blk, row, row, pl.BlockSpec((B_WIDTH, B_WIDTH), lambda i: (0, 0))],
        out_specs=[pl.BlockSpec((tm, 3 * B_WIDTH), lambda i: (i, 0)), row, row],
        out_shape=[jax.ShapeDtypeStruct((t, 3 * B_WIDTH), BF16), jax.ShapeDtypeStruct((1, B_WIDTH), F32),
                   jax.ShapeDtypeStruct((1, B_WIDTH), F32)],
        scratch_shapes=[_subseq_scratch(tm)], compiler_params=_params("arbitrary"),
    )(z, z, *dqs, *dks, *dvs, cos_t, sin_t, gq, gk, seg)


def _subseq_views(x, col, name):
    t = x.shape[0]
    tm = _tile(t, (SUBSEQ_TM,))

    def body(x_ref, *rest):
        out_refs, scr_ref = rest[:-1], rest[-1]
        val = x_ref[...].astype(F32)
        for o_ref, dil in zip(out_refs, DILATIONS):
            o_ref[...] = _to_subseq(val, scr_ref, dil).astype(o_ref.dtype)

    return pl.pallas_call(
        body, name=name, grid=(t // tm,), in_specs=[pl.BlockSpec((tm, B_WIDTH), lambda i: (i, col))],
        out_specs=[_subseq_spec(tm, dil) for dil in DILATIONS],
        out_shape=[jax.ShapeDtypeStruct(_subseq_shape(t, dil), x.dtype) for dil in DILATIONS],
        scratch_shapes=[_subseq_scratch(tm)], compiler_params=_params("parallel"),
    )(x)


ATTN_FWD_BLOCKS = 1
ATTN_BWD_BLOCKS = 2


def _attn_step_specs(nb, want):
    ns = want if nb % want == 0 else 1
    cur = pl.BlockSpec((ns * Q_BLOCK, B_WIDTH), lambda r, i: (i, r))
    prev = pl.BlockSpec((Q_BLOCK, B_WIDTH), lambda r, i: (jnp.maximum(ns * i - 1, 0), r))
    return ns, cur, prev


def _attn_fwd(q, k, v, dil, name):
    t = q.shape[0] * dil
    nb = t // dil // Q_BLOCK
    ns, cur, prev = _attn_step_specs(nb, ATTN_FWD_BLOCKS)

    def body(q_ref, kp_ref, kc_ref, vp_ref, vc_ref, o_ref, lse_ref):
        i = pl.program_id(1)
        a = lax.broadcasted_iota(jnp.int32, (Q_BLOCK, 2 * Q_BLOCK), 0)
        j = lax.broadcasted_iota(jnp.int32, (Q_BLOCK, 2 * Q_BLOCK), 1)
        dist = a + Q_BLOCK - j
        band = (dist >= 0) & (dist <= Q_BLOCK)
        sls = [slice(h * HEAD_DIM, (h + 1) * HEAD_DIM) for h in range(HEADS)]
        for sb in range(ns):
            rows = slice(sb * Q_BLOCK, (sb + 1) * Q_BLOCK)
            before = slice((sb - 1) * Q_BLOCK, sb * Q_BLOCK)
            q = q_ref[rows, :]
            kk = jnp.concatenate([kp_ref[...] if sb == 0 else kc_ref[before, :], kc_ref[rows, :]], axis=0)
            vv = jnp.concatenate([vp_ref[...] if sb == 0 else vc_ref[before, :], vc_ref[rows, :]], axis=0)
            mask = band & ((j >= Q_BLOCK) | (i > 0)) if sb == 0 else band
            scores = [_dot(q[:, sl], kk[:, sl], _NT) for sl in sls]
            ps, dens = [], []
            for sl, s in zip(sls, scores):
                s = jnp.where(mask, s * (HEAD_DIM ** -0.5), NEG)
                m = jnp.max(s, axis=-1, keepdims=True)
                p = jnp.exp(s - m)
                den = jnp.sum(p, axis=-1, keepdims=True)
                ps.append(p.astype(BF16))
                dens.append(den)
                lse_ref[rows, sl] = jnp.broadcast_to(m + jnp.log(den), (Q_BLOCK, HEAD_DIM))
            for sl, p, den in zip(sls, ps, dens):
                o_ref[rows, sl] = _dot(p, vv[:, sl], _NN) / den

    return pl.pallas_call(
        body, name=name, grid=(dil, nb // ns), in_specs=[cur, prev, cur, prev, cur], out_specs=[cur, cur],
        out_shape=[jax.ShapeDtypeStruct(_subseq_shape(t, dil), F32)] * 2,
        compiler_params=_params("parallel", "parallel"),
    )(q, k, k, v, v)


def _attn_merge(outs, lses, cat, name):
    nb = len(DILATIONS)
    t = cat.shape[0]
    tm = _tile(t, (SUBSEQ_TM,))
    subs = [_subseq_spec(tm, dil) for dil in DILATIONS]

    def body(*refs):
        o_refs, l_refs = refs[:nb], refs[nb:2 * nb]
        yb_refs, lse_refs, cat_ref, scr_ref = refs[2 * nb + 1:3 * nb + 1], refs[3 * nb + 1:4 * nb + 1], refs[4 * nb + 1], refs[4 * nb + 2]
        ls = [_from_subseq(r[...], scr_ref, dil) for r, dil in zip(l_refs, DILATIONS)]
        m = functools.reduce(jnp.maximum, ls)
        tot = m + jnp.log(sum(jnp.exp(l - m) for l in ls))
        yb = sum(jnp.exp(l - tot) * _from_subseq(o[...], scr_ref, dil) for l, o, dil in zip(ls, o_refs, DILATIONS))
        cat_ref[...] = yb.astype(BF16)
        yb = yb.astype(BF16).astype(F32)
        for yb_ref, lse_ref, dil in zip(yb_refs, lse_refs, DILATIONS):
            yb_ref[...] = _to_subseq(yb, scr_ref, dil).astype(BF16)
            lse_ref[...] = _to_subseq(tot, scr_ref, dil)

    outs_ = pl.pallas_call(
        body, name=name, grid=(t // tm,), in_specs=subs * 2 + [pl.BlockSpec(memory_space=pl.ANY)],
        out_specs=subs * 2 + [pl.BlockSpec((tm, B_WIDTH), lambda i: (i, A_WIDTH // B_WIDTH))],
        out_shape=[jax.ShapeDtypeStruct(_subseq_shape(t, dil), BF16) for dil in DILATIONS]
        + [jax.ShapeDtypeStruct(_subseq_shape(t, dil), F32) for dil in DILATIONS] + [jax.ShapeDtypeStruct(cat.shape, BF16)],
        input_output_aliases={2 * nb: 2 * nb}, scratch_shapes=[_subseq_scratch(tm)], compiler_params=_params("parallel"),
    )(*outs, *lses, cat)
    return outs_[:nb], outs_[nb:2 * nb], outs_[2 * nb]


def _attn_bwd(q, k, v, do, o, lse, dil, name):
    t = q.shape[0] * dil
    nb = t // dil // Q_BLOCK
    ns, cur, prev = _attn_step_specs(nb, ATTN_BWD_BLOCKS)
    scale = HEAD_DIM ** -0.5

    def body(q_ref, kp_ref, kc_ref, vp_ref, vc_ref, do_ref, o_ref, lse_ref, dq_ref, dk_ref, dv_ref,
             ck_ref, cv_ref, tk_ref, tv_ref):
        step = pl.program_id(1)

        @pl.when(step == 0)
        def _():
            ck_ref[...] = jnp.zeros_like(ck_ref)
            cv_ref[...] = jnp.zeros_like(cv_ref)

        a = lax.broadcasted_iota(jnp.int32, (Q_BLOCK, 2 * Q_BLOCK), 0)
        j = lax.broadcasted_iota(jnp.int32, (Q_BLOCK, 2 * Q_BLOCK), 1)
        dist = a + Q_BLOCK - j
        band = (dist >= 0) & (dist <= Q_BLOCK)
        sls = [slice(h * HEAD_DIM, (h + 1) * HEAD_DIM) for h in range(HEADS)]
        for sb in range(ns):
            i = ns * step + sb
            rows = slice(sb * Q_BLOCK, (sb + 1) * Q_BLOCK)
            before = slice((sb - 1) * Q_BLOCK, sb * Q_BLOCK)
            q = q_ref[rows, :]
            kk = jnp.concatenate([kp_ref[...] if sb == 0 else kc_ref[before, :], kc_ref[rows, :]], axis=0)
            vv = jnp.concatenate([vp_ref[...] if sb == 0 else vc_ref[before, :], vc_ref[rows, :]], axis=0)
            do = do_ref[rows, :]
            dof = do.astype(F32)
            of = o_ref[rows, :].astype(F32)
            mask = band & ((j >= Q_BLOCK) | (step > 0)) if sb == 0 else band
            scores = [_dot(q[:, sl], kk[:, sl], _NT) for sl in sls]
            dps = [_dot(do[:, sl], vv[:, sl], _NT) for sl in sls]
            ps, dss = [], []
            for sl, s, dp in zip(sls, scores, dps):
                p = jnp.exp(jnp.where(mask, s * scale, NEG) - lse_ref[rows, sl.start:sl.start + 1])
                delta = jnp.sum(dof[:, sl] * of[:, sl], axis=-1, keepdims=True)
                dss.append((p * (dp - delta) * scale).astype(BF16))
                ps.append(p.astype(BF16))
            for sl, p, ds in zip(sls, ps, dss):
                dq_ref[rows, sl] = _dot(ds, kk[:, sl], _NN)
                dv_t = _dot(do[:, sl], p, _TN)
                dk_t = _dot(q[:, sl], ds, _TN)
                tk_ref[sl, :] = ck_ref[sl, :] + dk_t[:, :Q_BLOCK]
                tv_ref[sl, :] = cv_ref[sl, :] + dv_t[:, :Q_BLOCK]
                ck_ref[sl, :] = dk_t[:, Q_BLOCK:]
                cv_ref[sl, :] = dv_t[:, Q_BLOCK:]

            @pl.when(i >= 1)
            def _():
                done = pl.ds(pl.multiple_of((i - 1) * Q_BLOCK, Q_BLOCK), Q_BLOCK)
                dk_ref[done, :] = tk_ref[...].T
                dv_ref[done, :] = tv_ref[...].T

        @pl.when(step == nb // ns - 1)
        def _():
            done = pl.ds((nb - 1) * Q_BLOCK, Q_BLOCK)
            dk_ref[done, :] = ck_ref[...].T
            dv_ref[done, :] = cv_ref[...].T

    whole = pl.BlockSpec((t // dil, B_WIDTH), lambda r, i: (0, r))
    return pl.pallas_call(
        body, name=name, grid=(dil, nb // ns), in_specs=[cur, prev, cur, prev, cur, cur, cur, cur],
        out_specs=[cur, whole, whole], out_shape=[jax.ShapeDtypeStruct(_subseq_shape(t, dil), F32)] * 3,
        scratch_shapes=[pltpu.VMEM((B_WIDTH, Q_BLOCK), F32)] * 4,
        compiler_params=_params("parallel", "arbitrary"),
    )(q, k, k, v, v, do, o, lse)


FFN_TN = 256
FFN_ACT_TM = (4096, 2048, 1024, 512, 256, 128)
FFN_FWD_CHUNK = 256
FFN_BWD_CHUNK = 128


def _ffn_up(h, up_t, name):
    t, k = h.shape
    tm = _tile(t)

    def body(h_ref, w_ref, o_ref):
        o_ref[...] = _dot(h_ref[...], w_ref[...], _NT).astype(BF16)

    return pl.pallas_call(
        body, name=name, grid=(2, t // tm),
        in_specs=[pl.BlockSpec((tm, k), lambda p, i: (i, 0)), pl.BlockSpec((None, FFN_DIM, k), lambda p, i: (p, 0, 0))],
        out_specs=pl.BlockSpec((None, tm, FFN_DIM), lambda p, i: (p, i, 0)),
        out_shape=jax.ShapeDtypeStruct((2, t, FFN_DIM), BF16), compiler_params=_params("parallel", "parallel"),
    )(h, up_t.reshape(2, FFN_DIM, k))


def _ffn_up_dx(du, up_t, name):
    t = du.shape[1]
    k = up_t.shape[1]
    tm = _tile(t)

    def body(a_ref, b_ref, o_ref):
        o_ref[...] = _dot(a_ref[0], b_ref[0], _NN) + _dot(a_ref[1], b_ref[1], _NN)

    return pl.pallas_call(
        body, name=name, grid=(t // tm,),
        in_specs=[pl.BlockSpec((2, tm, FFN_DIM), lambda i: (0, i, 0)), pl.BlockSpec((2, FFN_DIM, k), lambda i: (0, 0, 0))],
        out_specs=pl.BlockSpec((tm, k), lambda i: (i, 0)), out_shape=jax.ShapeDtypeStruct((t, k), F32),
        compiler_params=_params("parallel"),
    )(du, up_t.reshape(2, FFN_DIM, k))


def _ffn_conv(win, w_ref, b_ref, p):
    x = win.astype(F32)
    x0, x1, x2 = x[FFN_HALO:], pltpu.roll(x, 1, 0)[FFN_HALO:], pltpu.roll(x, 2, 0)[FFN_HALO:]
    return b_ref[p] + w_ref[p, 2:3, :] * x0 + w_ref[p, 1:2, :] * x1 + w_ref[p, 0:1, :] * x2


def _zero_if(cond, v):
    return jnp.where(cond, 0, v).astype(v.dtype)


def _ffn_act(u, dw_w, dw_b, name):
    t = u.shape[1]
    tm = _tile(t, FFN_ACT_TM)
    chunk = min(FFN_FWD_CHUNK, tm)
    hb = tm // FFN_HALO
    main = pl.BlockSpec((2, tm, FFN_TN), lambda i, j: (0, i, j))
    halo = pl.BlockSpec((2, FFN_HALO, FFN_TN), lambda i, j: (0, jnp.maximum(i * hb - 1, 0), j))
    wsp = pl.BlockSpec((2, FFN_CONV_WIDTH, FFN_TN), lambda i, j: (0, 0, j))
    bsp = pl.BlockSpec((2, 1, FFN_TN), lambda i, j: (0, 0, j))

    def body(u_ref, uh_ref, w_ref, b_ref, o_ref, z_ref):
        first = pl.program_id(0) == 0

        def emit(rows, wins):
            za, zb = _ffn_conv(wins[0], w_ref, b_ref, 0), _ffn_conv(wins[1], w_ref, b_ref, 1)
            o_ref[rows, :] = (za * _sigmoid(za) * zb).astype(BF16)
            z_ref[0, rows, :] = za.astype(BF16)
            z_ref[1, rows, :] = zb.astype(BF16)

        emit(pl.ds(0, chunk), [jnp.concatenate([_zero_if(first, uh_ref[p]), u_ref[p, 0:chunk, :]], axis=0) for p in range(2)])

        def step(c, carry):
            s = pl.multiple_of(c * chunk, chunk)
            emit(pl.ds(s, chunk), [u_ref[p, pl.ds(s - FFN_HALO, chunk + FFN_HALO), :] for p in range(2)])
            return carry

        lax.fori_loop(1, tm // chunk, step, 0)

    return pl.pallas_call(
        body, name=name, grid=(t // tm, FFN_DIM // FFN_TN), in_specs=[main, halo, wsp, bsp],
        out_specs=[pl.BlockSpec((tm, FFN_TN), lambda i, j: (i, j)), main],
        out_shape=[jax.ShapeDtypeStruct((t, FFN_DIM), BF16), jax.ShapeDtypeStruct((2, t, FFN_DIM), BF16)],
        compiler_params=_params("parallel", "parallel"),
    )(u, u, dw_w, dw_b)


def _fold8(v):
    return jnp.sum(v.reshape(v.shape[0] // 8, 8, v.shape[1]), axis=0)


def _ffn_act_bwd(u, z, dact, dw_w, name):
    t = u.shape[1]
    tm = _tile(t, FFN_ACT_TM)
    chunk = min(FFN_BWD_CHUNK, tm // 2)
    halo = FFN_HALO
    hb = tm // halo
    nt = t // tm
    last_halo = t // halo - 1
    next_i = lambda i: jnp.minimum((i + 1) * hb, last_halo)
    main = pl.BlockSpec((2, tm, FFN_TN), lambda j, i: (0, i, j))
    nxt = pl.BlockSpec((2, halo, FFN_TN), lambda j, i: (0, next_i(i), j))
    wsp = pl.BlockSpec((2, FFN_CONV_WIDTH, FFN_TN), lambda j, i: (0, 0, j))
    bsp = pl.BlockSpec((2, 1, FFN_TN), lambda j, i: (0, 0, j))

    def body(u_ref, z_ref, zn_ref, da_ref, dan_ref, w_ref, du_ref, dw_ref, db_ref, acc_ref):
        i = pl.program_id(1)
        last = i == nt - 1
        acc_ref[...] = jnp.zeros_like(acc_ref)

        def emit(rows, zs, dact):
            n = chunk + halo
            za, zb, dact = zs[0].astype(F32), zs[1].astype(F32), dact.astype(F32)
            sg = _sigmoid(za)
            dzs = (dact * zb * (sg * (1.0 + za * (1.0 - sg))), dact * (za * sg))
            for p, dz in enumerate(dzs):
                ahead = (dz[:chunk], pltpu.roll(dz, n - 1, 0)[:chunk], pltpu.roll(dz, n - 2, 0)[:chunk])
                um = u_ref[p, rows, :].astype(F32)
                acc_ref[p, FFN_CONV_WIDTH] += _fold8(ahead[0])
                du = None
                for j, dzj in enumerate(ahead):
                    k = FFN_CONV_WIDTH - 1 - j
                    acc_ref[p, k] += _fold8(dzj * um)
                    term = w_ref[p, k:k + 1, :] * dzj
                    du = term if du is None else du + term
                du_ref[p, rows, :] = du.astype(BF16)

        def step(c, carry):
            s = pl.multiple_of(c * chunk, chunk)
            emit(pl.ds(s, chunk), [z_ref[p, pl.ds(s, chunk + halo), :] for p in range(2)], da_ref[pl.ds(s, chunk + halo), :])
            return carry

        lax.fori_loop(0, tm // chunk - 1, step, 0)
        s = tm - chunk
        emit(pl.ds(s, chunk),
             [jnp.concatenate([z_ref[p, s:tm, :], zn_ref[p]], axis=0) for p in range(2)],
             jnp.concatenate([da_ref[s:tm, :], _zero_if(last, dan_ref[...])], axis=0))

        @pl.when(i == 0)
        def _():
            dw_ref[...] = jnp.zeros_like(dw_ref)
            db_ref[...] = jnp.zeros_like(db_ref)

        for p in range(2):
            for k in range(FFN_CONV_WIDTH):
                dw_ref[p, k:k + 1, :] += _colsum(acc_ref[p, k])
            db_ref[p] += _colsum(acc_ref[p, FFN_CONV_WIDTH])

    return pl.pallas_call(
        body, name=name, grid=(FFN_DIM // FFN_TN, nt),
        in_specs=[main, main, nxt, pl.BlockSpec((tm, FFN_TN), lambda j, i: (i, j)),
                  pl.BlockSpec((halo, FFN_TN), lambda j, i: (next_i(i), j)), wsp],
        out_specs=[main, wsp, bsp],
        out_shape=[jax.ShapeDtypeStruct((2, t, FFN_DIM), BF16), jax.ShapeDtypeStruct((2, FFN_CONV_WIDTH, FFN_DIM), F32),
                   jax.ShapeDtypeStruct((2, 1, FFN_DIM), F32)],
        scratch_shapes=[pltpu.VMEM((2, FFN_CONV_WIDTH + 1, 8, FFN_TN), F32)],
        compiler_params=_params("parallel", "arbitrary"),
    )(u, z, z, dact, dact, dw_w)


CONV_TM = 256
CONV_ROWS = 128
CONV_FWD_ROWS = 256
CONV_LANES = 128
CONV_NORM_ROWS = 32


def _glu_window(pa_ref, pah_ref, pg_ref, pgh_ref, scr_ref, first):
    ah, gh = pah_ref[...].astype(F32), pgh_ref[...].astype(F32)
    scr_ref[0:CONV_HALO, :] = jnp.where(first, 0.0, ah * _sigmoid(gh))
    scr_ref[CONV_HALO:, :] = pa_ref[...].astype(F32) * _sigmoid(pg_ref[...].astype(F32))


def _tap_slabs(win, rows, ahead):
    n = win.shape[0]
    for s in range(8):
        ws = win if s == 0 else pltpu.roll(win, n - s if ahead else s, 0)
        for q in range(CONV_HALO // 8):
            o = 8 * q + s
            if o < CONV_WIDTH:
                start = 8 * q if ahead else CONV_HALO - 8 * q
                yield CONV_WIDTH - 1 - o, ws[start:start + rows]


def _conformer_specs(t):
    tm = _tile(t, (CONV_TM, 128))
    hb = tm // CONV_HALO
    d = D_MODEL
    main = lambda c: pl.BlockSpec((tm, d), lambda i: (i, c))
    halo = lambda c: pl.BlockSpec((CONV_HALO, d), lambda i: (jnp.maximum(i * hb - 1, 0), c))
    row = pl.BlockSpec((1, d), lambda i: (0, 0))
    wsp = pl.BlockSpec((CONV_WIDTH, d), lambda i: (0, 0))
    return tm, main, halo, row, wsp


def _conformer_mid(p, dw_w, dw_b, ln_g, ln_b, name):
    t = p.shape[0]
    tm, main, halo, row, wsp = _conformer_specs(t)
    d, lanes, rows = D_MODEL, CONV_LANES, min(CONV_FWD_ROWS, tm)

    def body(pa_ref, pah_ref, pg_ref, pgh_ref, w_ref, b_ref, g_ref, lb_ref, o_ref, dc_ref, scr_ref):
        _glu_window(pa_ref, pah_ref, pg_ref, pgh_ref, scr_ref, pl.program_id(0) == 0)
        for c in range(d // lanes):
            ls = slice(c * lanes, (c + 1) * lanes)

            def taps(r, carry, ls=ls):
                r0 = pl.multiple_of(r * rows, rows)
                acc = jnp.broadcast_to(b_ref[:, ls], (rows, lanes))
                for k, slab in _tap_slabs(scr_ref[pl.ds(r0, rows + CONV_HALO), ls], rows, False):
                    acc = acc + w_ref[k:k + 1, ls] * slab
                dc_ref[pl.ds(r0, rows), ls] = acc
                return carry

            lax.fori_loop(0, tm // rows, taps, 0)

        def norm(r, carry):
            r0 = pl.multiple_of(r * CONV_NORM_ROWS, CONV_NORM_ROWS)
            dc = dc_ref[pl.ds(r0, CONV_NORM_ROWS), :]
            xc = dc - jnp.mean(dc, axis=-1, keepdims=True)
            ln = xc * lax.rsqrt(jnp.mean(xc * xc, axis=-1, keepdims=True) + EPS) * g_ref[...] + lb_ref[...]
            o_ref[pl.ds(r0, CONV_NORM_ROWS), :] = (ln * _sigmoid(ln)).astype(BF16)
            return carry

        lax.fori_loop(0, tm // CONV_NORM_ROWS,norm, 0)

    return pl.pallas_call(
        body, name=name, grid=(t // tm,), in_specs=[main(0), halo(0), main(1), halo(1), wsp, row, row, row],
        out_specs=[main(0), main(0)], out_shape=[jax.ShapeDtypeStruct((t, d), BF16), jax.ShapeDtypeStruct((t, d), F32)],
        scratch_shapes=[pltpu.VMEM((tm + CONV_HALO, d), F32)], compiler_params=_params("parallel"),
    )(p, p, p, p, dw_w, dw_b, ln_g, ln_b)


def _conformer_mid_bwd(p, dc, ds, ln_g, ln_b, name):
    t = p.shape[0]
    tm, main, halo, row, wsp = _conformer_specs(t)
    d, nt = D_MODEL, t // tm
    rows, lanes = CONV_ROWS, CONV_LANES

    def body(pa_ref, pah_ref, pg_ref, pgh_ref, dc_ref, ds_ref, g_ref, lb_ref,
             ddc_ref, dw_ref, db_ref, dg_ref, dlb_ref, scr_ref, wacc_ref, racc_ref):
        i = pl.program_id(0)

        @pl.when(i == 0)
        def _():
            wacc_ref[...] = jnp.zeros_like(wacc_ref)
            racc_ref[...] = jnp.zeros_like(racc_ref)

        _glu_window(pa_ref, pah_ref, pg_ref, pgh_ref, scr_ref, i == 0)

        def norm_bwd(r, carry):
            r0 = pl.multiple_of(r * CONV_NORM_ROWS, CONV_NORM_ROWS)
            dcv = dc_ref[pl.ds(r0, CONV_NORM_ROWS), :]
            xc = dcv - jnp.mean(dcv, axis=-1, keepdims=True)
            rstd = lax.rsqrt(jnp.mean(xc * xc, axis=-1, keepdims=True) + EPS)
            xhat = xc * rstd
            ln = xhat * g_ref[...] + lb_ref[...]
            sg = _sigmoid(ln)
            dln = ds_ref[pl.ds(r0, CONV_NORM_ROWS), :].astype(F32) * (sg * (1.0 + ln * (1.0 - sg)))
            dxh = dln * g_ref[...]
            ddc = rstd * (dxh - jnp.mean(dxh, axis=-1, keepdims=True) - xhat * jnp.mean(dxh * xhat, axis=-1, keepdims=True))
            ddc_ref[pl.ds(r0, CONV_NORM_ROWS), :] = ddc
            racc_ref[0] += _fold8(dln * xhat)
            racc_ref[1] += _fold8(dln)
            racc_ref[2] += _fold8(ddc)
            return carry

        lax.fori_loop(0, tm // CONV_NORM_ROWS,norm_bwd, 0)

        for c in range(d // lanes):
            ls = slice(c * lanes, (c + 1) * lanes)

            def taps(r, carry, ls=ls):
                r0 = pl.multiple_of(r * rows, rows)
                ddc = ddc_ref[pl.ds(r0, rows), ls]
                for k, slab in _tap_slabs(scr_ref[pl.ds(r0, rows + CONV_HALO), ls], rows, False):
                    wacc_ref[k, :, ls] += _fold8(ddc * slab)
                return carry

            lax.fori_loop(0, tm // rows, taps, 0)

        @pl.when(i == nt - 1)
        def _():
            for k in range(CONV_WIDTH):
                dw_ref[k:k + 1, :] = _colsum(wacc_ref[k])
            dg_ref[...] = _colsum(racc_ref[0])
            dlb_ref[...] = _colsum(racc_ref[1])
            db_ref[...] = _colsum(racc_ref[2])

    return pl.pallas_call(
        body, name=name, grid=(nt,), in_specs=[main(0), halo(0), main(1), halo(1), main(0), main(0), row, row],
        out_specs=[main(0), wsp, row, row, row],
        out_shape=[jax.ShapeDtypeStruct((t, d), F32), jax.ShapeDtypeStruct((CONV_WIDTH, d), F32)]
        + [jax.ShapeDtypeStruct((1, d), F32)] * 3,
        scratch_shapes=[pltpu.VMEM((tm + CONV_HALO, d), F32), pltpu.VMEM((CONV_WIDTH, 8, d), F32), pltpu.VMEM((3, 8, d), F32)],
        compiler_params=_params("arbitrary"),
    )(p, p, p, p, dc, ds, ln_g, ln_b)


def _conformer_glu_bwd(p, ddc, dw_w, name):
    t = p.shape[0]
    d = D_MODEL
    tm = _tile(t, (CONV_TM, 128))
    hb = tm // CONV_HALO
    nt = t // tm
    last_halo = t // CONV_HALO - 1
    rows, lanes = CONV_ROWS, CONV_LANES
    col = lambda c: pl.BlockSpec((tm, d), lambda i: (i, c))
    nxt = pl.BlockSpec((CONV_HALO, d), lambda i: (jnp.minimum((i + 1) * hb, last_halo), 0))

    def body(pa_ref, pg_ref, ddc_ref, ddcn_ref, w_ref, dp_ref, db_ref, scr_ref, acc_ref):
        i = pl.program_id(0)

        @pl.when(i == 0)
        def _():
            acc_ref[...] = jnp.zeros_like(acc_ref)

        scr_ref[0:tm, :] = ddc_ref[...]
        scr_ref[tm:, :] = _zero_if(i == nt - 1, ddcn_ref[...])
        for c in range(d // lanes):
            ls = slice(c * lanes, (c + 1) * lanes)
            gs = slice(d + c * lanes, d + (c + 1) * lanes)

            def taps(r, carry, ls=ls, gs=gs):
                r0 = pl.multiple_of(r * rows, rows)
                dglu = None
                for k, slab in _tap_slabs(scr_ref[pl.ds(r0, rows + CONV_HALO), ls], rows, True):
                    term = w_ref[k:k + 1, ls] * slab
                    dglu = term if dglu is None else dglu + term
                a = pa_ref[pl.ds(r0, rows), ls].astype(F32)
                sg = _sigmoid(pg_ref[pl.ds(r0, rows), ls].astype(F32))
                da = (dglu * sg).astype(BF16)
                dg = (dglu * a * sg * (1.0 - sg)).astype(BF16)
                dp_ref[pl.ds(r0, rows), ls] = da
                dp_ref[pl.ds(r0, rows), gs] = dg
                acc_ref[:, ls] += _fold8(da.astype(F32))
                acc_ref[:, gs] += _fold8(dg.astype(F32))
                return carry

            lax.fori_loop(0, tm // rows, taps, 0)

        @pl.when(i == nt - 1)
        def _():
            db_ref[...] = _colsum(acc_ref[...])

    return pl.pallas_call(
        body, name=name, grid=(nt,),
        in_specs=[col(0), col(1), col(0), nxt, pl.BlockSpec((CONV_WIDTH, d), lambda i: (0, 0))],
        out_specs=[pl.BlockSpec((tm, 2 * d), lambda i: (i, 0)), pl.BlockSpec((1, 2 * d), lambda i: (0, 0))],
        out_shape=[jax.ShapeDtypeStruct((t, 2 * d), BF16), jax.ShapeDtypeStruct((1, 2 * d), F32)],
        scratch_shapes=[pltpu.VMEM((tm + CONV_HALO, d), F32), pltpu.VMEM((8, 2 * d), F32)],
        compiler_params=_params("arbitrary"),
    )(p, p, ddc, ddc, dw_w)


def _colsum_call(a, name):
    t, n = a.shape
    tm = _tile(t)

    def body(a_ref, o_ref):
        @pl.when(pl.program_id(0) == 0)
        def _():
            o_ref[...] = jnp.zeros_like(o_ref)

        o_ref[...] += _colsum(a_ref[...].astype(F32))

    return pl.pallas_call(
        body, name=name, grid=(t // tm,), in_specs=[pl.BlockSpec((tm, n), lambda i: (i, 0))],
        out_specs=pl.BlockSpec((1, n), lambda i: (0, 0)), out_shape=jax.ShapeDtypeStruct((1, n), F32),
        compiler_params=_params("arbitrary"),
    )(a)


def _ada_fwd(c_all, w, name):
    rows, d = c_all.shape
    n = w.shape[1]
    tn = _tile(n, (256, 128))

    def body(c_ref, w_ref, o_ref):
        c = c_ref[...]
        o_ref[...] = _dot((c * _sigmoid(c)).astype(BF16), w_ref[...].astype(BF16), _NN)

    return pl.pallas_call(
        body, name=name, grid=(n // tn,),
        in_specs=[pl.BlockSpec((rows, d), lambda j: (0, 0)), pl.BlockSpec((d, tn), lambda j: (0, j))],
        out_specs=pl.BlockSpec((rows, tn), lambda j: (0, j)), out_shape=jax.ShapeDtypeStruct((rows, n), F32),
        compiler_params=_params("parallel"),
    )(c_all, w)


def _ada_bwd(c_all, dmod, name):
    rows, d = c_all.shape
    n = dmod.shape[1]
    tn = _tile(n, (256, 128))

    def body(c_ref, g_ref, o_ref):
        c = c_ref[...]
        o_ref[...] = _dot((c * _sigmoid(c)).astype(BF16), g_ref[...].astype(BF16), _TN)

    return pl.pallas_call(
        body, name=name, grid=(n // tn,),
        in_specs=[pl.BlockSpec((rows, d), lambda j: (0, 0)), pl.BlockSpec((rows, tn), lambda j: (0, j))],
        out_specs=pl.BlockSpec((d, tn), lambda j: (0, j)), out_shape=jax.ShapeDtypeStruct((d, n), F32),
        compiler_params=_params("parallel"),
    )(c_all, dmod)


def _sum_in_device_order(own, land, me, name):
    s, r, c = land.shape
    tr = _row_tile(r, 256)
    slot = lambda k: pl.BlockSpec((None, tr, c), lambda i, me_ref: (jnp.where(me_ref[0] == k, (k + 1) % s, k), i, 0))
    own_spec = pl.BlockSpec((tr, c), lambda i, me_ref: (i, 0))

    def body(me_ref, own_ref, *refs):
        o_ref = refs[-1]
        acc = None
        for k, ref in enumerate(refs[:-1]):
            term = jnp.where(me_ref[0] == k, own_ref[...], ref[...]).astype(F32)
            acc = term if acc is None else acc + term
        o_ref[...] = acc

    return pl.pallas_call(
        body, name=name, out_shape=jax.ShapeDtypeStruct((r, c), F32),
        grid_spec=pltpu.PrefetchScalarGridSpec(
            num_scalar_prefetch=1, grid=(r // tr,), in_specs=[own_spec] + [slot(k) for k in range(s)], out_specs=own_spec),
        compiler_params=_params("parallel"),
    )(me, own, *[land] * s)


def _sum_with_own(blocks, land, me, name):
    s, r, c = land.shape
    tr = _row_tile(r, 256)
    slot = lambda k: pl.BlockSpec((None, tr, c), lambda i, me_ref: ((me_ref[0] + k) % s, i, 0))

    def body(me_ref, own_ref, *refs):
        o_ref = refs[-1]
        acc = own_ref[...].astype(F32)
        for ref in refs[:-1]:
            acc = acc + ref[...].astype(F32)
        o_ref[...] = acc

    return pl.pallas_call(
        body, name=name, out_shape=jax.ShapeDtypeStruct((r, c), F32),
        grid_spec=pltpu.PrefetchScalarGridSpec(
            num_scalar_prefetch=1, grid=(r // tr,), in_specs=[slot(0)] + [slot(k) for k in range(1, s)],
            out_specs=pl.BlockSpec((tr, c), lambda i, me_ref: (i, 0))),
        compiler_params=_params("parallel"),
    )(me, blocks, *[land] * (s - 1))


def _adamw_update(w, g, m, v):
    nm = ADAM_B1 * m + (1.0 - ADAM_B1) * g
    nv = ADAM_B2 * v + (1.0 - ADAM_B2) * (g * g)
    m_hat = nm * (1.0 / (1.0 - ADAM_B1 ** ADAM_STEP))
    v_hat = nv * (1.0 / (1.0 - ADAM_B2 ** ADAM_STEP))
    return -ADAM_LR * (m_hat / (jnp.sqrt(v_hat) + ADAM_EPS) + ADAM_WD * w), nm, nv


def _adamw(w, g, m, v, behind, name):
    l, r, c = w.shape
    tr = _row_tile(r, 256)
    blk = pl.BlockSpec((None, tr, c), lambda k, i: (k, i, 0))
    order = [] if behind is None else [behind]

    def body(w_ref, g_ref, m_ref, v_ref, *rest):
        d_ref, nm_ref, nv_ref = rest[-3:]
        d_ref[...], nm_ref[...], nv_ref[...] = _adamw_update(w_ref[...], g_ref[...], m_ref[...], v_ref[...])

    return pl.pallas_call(
        body, name=name, grid=(l, r // tr), in_specs=[blk] * 4 + [pl.BlockSpec(memory_space=pl.ANY)] * len(order),
        out_specs=[blk] * 3, out_shape=[jax.ShapeDtypeStruct(w.shape, F32)] * 3,
        compiler_params=_params("parallel", "parallel"),
    )(w, g, m, v, *order)


def _adamw_small(ws, gs, ms, vs, name):
    n = len(ws)
    two_d = lambda a: a.reshape(-1, a.shape[-1])

    def body(*refs):
        ins, outs = refs[:4 * n], refs[4 * n:]
        for a in range(n):
            outs[a][...], outs[n + a][...], outs[2 * n + a][...] = _adamw_update(*[ins[k * n + a][...] for k in range(4)])

    res = pl.pallas_call(
        body, name=name, out_shape=[jax.ShapeDtypeStruct(two_d(w).shape, F32) for w in ws] * 3,
    )(*[two_d(a) for a in (*ws, *gs, *ms, *vs)])
    return [[res[k * n + a].reshape(ws[a].shape) for a in range(n)] for k in range(3)]


def _mesh_pos():
    return lax.axis_index("x"), lax.axis_index("y"), lax.axis_index("c")


def _all_gather_vmem(x_shard, name):
    m_per, n = x_shard.shape

    def body(x_ref, out_ref, send_sems, recv_sems, local_sem):
        x, y, c = _mesh_pos()
        me, sibling = (x, y, c), (x, y, 1 - c)
        chips = [(1 - x, y), (x, 1 - y), (1 - x, 1 - y)]

        def rows(px, py, pc):
            return out_ref.at[pl.ds((4 * px + 2 * py + pc) * m_per, m_per), :]

        def copy(k, block, to, src=None):
            return pltpu.make_async_remote_copy(
                src_ref=rows(*block) if src is None else src, dst_ref=rows(*block),
                send_sem=send_sems.at[k], recv_sem=recv_sems.at[k], device_id=to, device_id_type=MESH)

        mine = pltpu.make_async_copy(x_ref, rows(*me), local_sem)
        mine.start()
        first = [copy(0, me, sibling, src=x_ref)]
        first += [copy(1 + j, me, (*chip, c), src=x_ref) for j, chip in enumerate(chips)]
        for cp in first:
            cp.start()
        passed = [copy(4 + j, (*chip, c), sibling) for j, chip in enumerate(chips)]
        for j, chip in enumerate(chips):
            copy(1 + j, (*chip, c), me).wait_recv()
            passed[j].start()
        copy(0, sibling, me).wait_recv()
        for j, chip in enumerate(chips):
            copy(4 + j, (*chip, 1 - c), me).wait_recv()
        for cp in first + passed:
            cp.wait_send()
        mine.wait()

    return pl.pallas_call(
        body, name=name, out_shape=jax.ShapeDtypeStruct((N_DEV * m_per, n), x_shard.dtype),
        in_specs=[pl.BlockSpec(memory_space=pltpu.VMEM)], out_specs=pl.BlockSpec(memory_space=pltpu.VMEM),
        scratch_shapes=[pltpu.SemaphoreType.DMA((7,)), pltpu.SemaphoreType.DMA((7,)), pltpu.SemaphoreType.DMA],
    )(x_shard)


def _all_gather_hbm(shards, name):
    n = len(shards)
    out_shape = [jax.ShapeDtypeStruct((N_DEV,) + s.shape, s.dtype) for s in shards]

    def body(*refs):
        x_refs, out_refs = refs[:n], refs[n:2 * n]
        send_sems, recv_sems, local_sems = refs[2 * n:]
        x, y, c = _mesh_pos()
        me, sibling = (x, y, c), (x, y, 1 - c)
        chips = [(1 - x, y), (x, 1 - y), (1 - x, 1 - y)]

        def blk(a, p):
            return out_refs[a].at[4 * p[0] + 2 * p[1] + p[2]]

        def copy(a, k, block, to, src=None):
            return pltpu.make_async_remote_copy(
                src_ref=blk(a, block) if src is None else src, dst_ref=blk(a, block),
                send_sem=send_sems.at[7 * a + k], recv_sem=recv_sems.at[7 * a + k], device_id=to, device_id_type=MESH)

        mine = [pltpu.make_async_copy(x_refs[a], blk(a, me), local_sems.at[a]) for a in range(n)]
        for cp in mine:
            cp.start()
        first = []
        for a in range(n):
            first.append(copy(a, 0, me, sibling, src=x_refs[a]))
            first += [copy(a, 1 + j, me, (*chip, c), src=x_refs[a]) for j, chip in enumerate(chips)]
        for cp in first:
            cp.start()
        passed = []
        for j, chip in enumerate(chips):
            for a in range(n):
                copy(a, 1 + j, (*chip, c), me).wait_recv()
                fwd = copy(a, 4 + j, (*chip, c), sibling)
                fwd.start()
                passed.append(fwd)
        for a in range(n):
            copy(a, 0, sibling, me).wait_recv()
            for j, chip in enumerate(chips):
                copy(a, 4 + j, (*chip, 1 - c), me).wait_recv()
        for cp in first + passed:
            cp.wait_send()
        for cp in mine:
            cp.wait()

    return pl.pallas_call(
        body, name=name, out_shape=out_shape, in_specs=[pl.BlockSpec(memory_space=pltpu.VMEM)] * n,
        out_specs=[pl.BlockSpec(memory_space=pl.ANY)] * n,
        scratch_shapes=[pltpu.SemaphoreType.DMA((7 * n,)), pltpu.SemaphoreType.DMA((7 * n,)), pltpu.SemaphoreType.DMA((n,))],
    )(*shards)


def _peers(x, y, c):
    flip = lambda v, f: 1 - v if f else v
    return [(flip(x, m & 4), flip(y, m & 2), flip(c, m & 1)) for m in range(1, N_DEV)]


def _dev_index(p):
    return 4 * p[0] + 2 * p[1] + p[2]


def _push_copies(src_refs, land_refs, send_sems, recv_sems, scatter, receive):
    x, y, c = _mesh_pos()
    me = _dev_index((x, y, c))
    copies = []
    for a, (src, land) in enumerate(zip(src_refs, land_refs)):
        for k, p in enumerate(_peers(x, y, c)):
            copies.append(pltpu.make_async_remote_copy(
                src_ref=src.at[_dev_index(p)] if scatter else src, dst_ref=land.at[_dev_index(p) if receive else me],
                send_sem=send_sems.at[7 * a + k], recv_sem=recv_sems.at[7 * a + k], device_id=p, device_id_type=MESH))
    return copies


_HBM = pl.BlockSpec(memory_space=pltpu.HBM)
_SEM = pl.BlockSpec(memory_space=pltpu.SEMAPHORE)
_EFFECT = pltpu.SideEffectType.DATAFLOW_SIDE_EFFECTING


def _pushes_start(srcs, lands, scatter, name):
    n = len(srcs)

    def body(*refs):
        src_refs, land_refs = refs[:n], refs[n:2 * n]
        send_sems, recv_sems = refs[2 * n], refs[2 * n + 1]
        token = refs[-1]
        for cp in _push_copies(src_refs, land_refs, send_sems, recv_sems, scatter, receive=False):
            cp.start()
        token[...] = jnp.zeros_like(token)

    hbm = lambda a: pltpu.HBM(a.shape, a.dtype)
    sems = pltpu.SemaphoreType.DMA((7 * n,))
    outs = pl.pallas_call(
        body, name=name,
        out_shape=(sems, sems, *[hbm(a) for a in srcs], *[hbm(a) for a in lands], jax.ShapeDtypeStruct((8, 128), F32)),
        in_specs=[_HBM] * (2 * n), out_specs=(_SEM, _SEM, *[_HBM] * (2 * n), pl.BlockSpec(memory_space=pltpu.VMEM)),
        input_output_aliases={i: 2 + i for i in range(2 * n)},
        compiler_params=pltpu.CompilerParams(has_side_effects=_EFFECT),
    )(*[pltpu.with_memory_space_constraint(a, pltpu.HBM) for a in (*srcs, *lands)])
    return (outs[0], outs[1], outs[2:2 + n], outs[2 + n:2 + 2 * n], scatter), outs[-1]


def _pushes_wait(handle, after, name):
    send_sems, recv_sems, srcs, lands, scatter = handle
    n = len(srcs)
    after = after if isinstance(after, (tuple, list)) else (after,)

    def body(*refs):
        src_refs, land_refs = refs[:n], refs[n:2 * n]
        for cp in _push_copies(src_refs, land_refs, refs[2 * n], refs[2 * n + 1], scatter, receive=True):
            cp.wait_send()
            cp.wait_recv()

    hbm = lambda a: pltpu.HBM(a.shape, a.dtype)
    outs = pl.pallas_call(
        body, name=name, out_shape=tuple(hbm(a) for a in (*srcs, *lands)),
        in_specs=[_HBM] * (2 * n) + [_SEM, _SEM] + [pl.BlockSpec(memory_space=pl.ANY)] * len(after),
        out_specs=tuple([_HBM] * (2 * n)), input_output_aliases={i: i for i in range(2 * n)},
        compiler_params=pltpu.CompilerParams(has_side_effects=_EFFECT),
    )(*srcs, *lands, send_sems, recv_sems, *after)
    return outs[:n], outs[n:]


def _landing_zones(srcs, behind, name):
    n, nb = len(srcs), len(behind)

    def body(*refs):
        src_refs, land_refs, bufs, sems = refs[:n], refs[n + nb:2 * n + nb], refs[2 * n + nb:3 * n + nb], refs[3 * n + nb]
        me = _dev_index(_mesh_pos())
        load = [pltpu.make_async_copy(src, buf, sems.at[a]) for a, (src, buf) in enumerate(zip(src_refs, bufs))]
        store = [pltpu.make_async_copy(buf, land.at[me], sems.at[a]) for a, (buf, land) in enumerate(zip(bufs, land_refs))]
        for cp in load:
            cp.start()
        for ld, st in zip(load, store):
            ld.wait()
            st.start()
        for cp in store:
            cp.wait()

    any_spec = pl.BlockSpec(memory_space=pl.ANY)
    return pl.pallas_call(
        body, name=name, out_shape=[jax.ShapeDtypeStruct((N_DEV,) + s.shape, s.dtype) for s in srcs],
        in_specs=[any_spec] * (n + nb), out_specs=[any_spec] * n,
        scratch_shapes=[pltpu.VMEM(s.shape, s.dtype) for s in srcs] + [pltpu.SemaphoreType.DMA((n,))],
        compiler_params=pltpu.CompilerParams(vmem_limit_bytes=V7X_VMEM_LIMIT),
    )(*srcs, *behind)


def _ffn_forward(x, mod, norm_g, w, tag):
    sh, sc, gate = mod
    h = _modnorm(x, norm_g, sc, sh, f"{tag}_norm")
    u = _ffn_up(h, w["up_t"], f"{tag}_up")
    act, z = _ffn_act(u, w["dw_w"], w["dw_b"], f"{tag}_act")
    y, x_new = _matmul(act, w["down"], "nn", BF16, f"{tag}_down", resid=(x, gate))
    return x_new, (x, h, u, z, act, y)


def _behind(row, token):
    return row if token is None else row + token[0:1, 0:1]


def _ffn_backward(dx_new, dy, d_gate, saved, mod, norm_g, w, tag, emit, below):
    x, h, u, z, act, _ = saved
    _, sc, _ = mod
    d_down = _matmul_tn_acc(act, dy, f"{tag}_down_dw")
    dact = _matmul(dy, w["down"], "nt", BF16, f"{tag}_down_dx")
    du, d_dw_w, d_dw_b = _ffn_act_bwd(u, z, dact, w["dw_w"], f"{tag}_act_bwd")
    d_up_t = _matmul_tn_acc(du, h, f"{tag}_up_dw").reshape(2 * FFN_DIM, -1)
    token = emit([d_up_t, d_down])
    dh = _ffn_up_dx(du, w["up_t"], f"{tag}_up_dx")
    dx, d_w, d_sh, *dy_below = _modnorm_bwd(x, dh, norm_g, _behind(sc, token), dx_new, below, f"{tag}_norm_bwd")
    return (dx, *dy_below), dict(dw_w=d_dw_w.transpose(1, 0, 2).reshape(FFN_CONV_WIDTH, 2 * FFN_DIM),
                    dw_b=d_dw_b.reshape(1, 2 * FFN_DIM), norm_g=d_w * (1.0 + sc), sh=d_sh, sc=d_w * norm_g, gate=d_gate)


def _mixer_forward(x, mod, norm_g, w, rope, tag):
    sh, sc, gate = mod
    h = _modnorm(x, norm_g, sc, sh, f"{tag}_norm")
    z = _matmul(h, w["w_in_t"], "nt", BF16, f"{tag}_in")
    ya = _gmlp_fwd(z, w["gain"], w["wtril"], w["bias_exp"], f"{tag}_gmlp")
    q, k, v = _qk_prep(z, rope[0], rope[1], w["gq"], w["gk"], w["seg"], f"{tag}_qk")
    outs, lses = zip(*[_attn_fwd(q[b], k[b], v[b], dil, f"{tag}_attn_d{dil}") for b, dil in enumerate(DILATIONS)])
    yb, lse, cat = _attn_merge(outs, lses, ya, f"{tag}_merge")
    y, x_new = _matmul(cat, w["w_out"], "nn", BF16, f"{tag}_out", resid=(x, gate))
    return x_new, (x, h, z, q, k, v, yb, lse, cat, y)


def _mixer_backward(dx_new, dy, d_gate, saved, mod, norm_g, w, rope, tag, emit, below):
    x, h, z, q, k, v, yb, lse, cat, _ = saved
    _, sc, _ = mod
    d_w_out = _matmul_tn_acc(cat, dy, f"{tag}_out_dw")
    dcat = _matmul(dy, w["w_out"], "nt", BF16, f"{tag}_out_dx")
    dz_a, d_sp_w, d_gain, d_bias_exp = _gmlp_bwd(z, dcat, w["gain"], w["wtril"], w["wtril_t"], w["bias_exp"], f"{tag}_gmlp_bwd")
    dyb = _subseq_views(dcat, A_WIDTH // B_WIDTH, f"{tag}_dyb_views")
    dqs, dks, dvs = zip(*[_attn_bwd(q[b], k[b], v[b], dyb[b], yb[b], lse[b], dil, f"{tag}_attn_bwd_d{dil}")
                          for b, dil in enumerate(DILATIONS)])
    dz_qkv, d_gq, d_gk = _qk_prep_bwd(z, dqs, dks, dvs, rope[0], rope[1], w["gq"], w["gk"], w["seg"], f"{tag}_qk_bwd")
    dz = jnp.concatenate([dz_a, dz_qkv], axis=1)
    d_w_in_t = _matmul_tn_acc(dz, h, f"{tag}_in_dw")
    token = emit([d_w_in_t, d_w_out])
    dh = _matmul(dz, w["w_in_t"], "nn", F32, f"{tag}_in_dx")
    dx, d_w, d_sh, *dy_below = _modnorm_bwd(x, dh, norm_g, _behind(sc, token), dx_new, below, f"{tag}_norm_bwd")
    return (dx, *dy_below), dict(
        vnorm_g=d_gain.reshape(A_GROUPS, GROUP_DIM), spatial_w=d_sp_w,
        spatial_b=d_bias_exp.reshape(CHUNK, A_GROUPS, GROUP_DIM).sum(-1).T,
        q_norm_g=d_gq.reshape(HEADS, HEAD_DIM).sum(0), k_norm_g=d_gk.reshape(HEADS, HEAD_DIM).sum(0),
        norm_g=d_w * (1.0 + sc), sh=d_sh, sc=d_w * norm_g, gate=d_gate)


def _conformer_forward(x, mod, norm_g, w, tag):
    sh, sc, gate = mod
    h = _modnorm(x, norm_g, sc, sh, f"{tag}_norm")
    p = _matmul(h, w["pw1_t"], "nt", BF16, f"{tag}_pw1", bias=w["pw1_b"])
    s, dc = _conformer_mid(p, w["dw_w"], w["dw_b"], w["ln_g"], w["ln_b"], f"{tag}_mid")
    y, x_new = _matmul(s, w["pw2"], "nn", BF16, f"{tag}_pw2", bias=w["pw2_b"], resid=(x, gate))
    return x_new, (x, h, p, dc, s, y)


def _conformer_backward(dx_new, dy, d_gate, saved, mod, norm_g, w, tag, emit, below):
    x, h, p, dc, s, _ = saved
    _, sc, _ = mod
    d_pw2 = _matmul_tn_acc(s, dy, f"{tag}_pw2_dw")
    d_pw2_b = _colsum_call(dy, f"{tag}_pw2_db")
    ds = _matmul(dy, w["pw2"], "nt", BF16, f"{tag}_pw2_dx")
    ddc, d_dw_w, d_dw_b, d_ln_g, d_ln_b = _conformer_mid_bwd(p, dc, ds, w["ln_g"], w["ln_b"], f"{tag}_mid_bwd")
    dp, d_pw1_b = _conformer_glu_bwd(p, ddc, w["dw_w"], f"{tag}_glu_bwd")
    d_pw1_t = _matmul_tn_acc(dp, h, f"{tag}_pw1_dw")
    token = emit([d_pw1_t, d_pw2])
    dh = _matmul(dp, w["pw1_t"], "nn", F32, f"{tag}_pw1_dx")
    dx, d_w, d_sh, *dy_below = _modnorm_bwd(x, dh, norm_g, _behind(sc, token), dx_new, below, f"{tag}_norm_bwd")
    return (dx, *dy_below), dict(pw1_b=d_pw1_b, dw_w=d_dw_w, dw_b=d_dw_b, ln_g=d_ln_g, ln_b=d_ln_b, pw2_b=d_pw2_b, norm_g=d_w * (1.0 + sc), sh=d_sh, sc=d_w * norm_g, gate=d_gate)


def _local_step(x, target, pos, mod, norm_mix_g, norm_ffn_g, mixer_w, conv_w, ffn_w, fetch, emit):
    d = D_MODEL
    inv_freq = 1.0 / (ROPE_THETA ** (jnp.arange(0, HEAD_DIM, 2, dtype=F32) / HEAD_DIM))
    inv_freq = jnp.tile(inv_freq, 2 * HEADS)[None, :]
    sign = jnp.tile(jnp.concatenate([-jnp.ones(HEAD_DIM // 2, F32), jnp.ones(HEAD_DIM // 2, F32)]), HEADS)[None, :]
    rope = _rope_tables(pos, inv_freq, sign, "rope_tables")
    mods = [[mod[l:l + 1, i * d:(i + 1) * d] for i in range(6)] for l in range(2)]
    mix = [(m[0], m[1], m[2]) for m in mods]
    ffn = [(m[3], m[4], m[5]) for m in mods]
    gm = [norm_mix_g[l:l + 1] for l in range(2)]
    gf = [norm_ffn_g[l:l + 1] for l in range(2)]

    mixer_w = {**mixer_w, **fetch("l0_mix", x)}
    x1, s_mix = _mixer_forward(x, mix[0], gm[0], mixer_w, rope, "l0_mix")
    ffn_w0 = {**ffn_w[0], **fetch("l0_ffn", x1)}
    x2, s_ffn0 = _ffn_forward(x1, ffn[0], gf[0], ffn_w0, "l0_ffn")
    conv_w = {**conv_w, **fetch("l1_conv", x2)}
    x3, s_conv = _conformer_forward(x2, mix[1], gm[1], conv_w, "l1_conv")
    ffn_w1 = {**ffn_w[1], **fetch("l1_ffn", x3)}
    x4, s_ffn1 = _ffn_forward(x3, ffn[1], gf[1], ffn_w1, "l1_ffn")
    below = lambda saved, m: (saved[-1], m[2])
    dx, loss, dy, dg = _loss_head(x4, target, below(s_ffn1, ffn[1]), "loss_head")
    (dx, dy, dg), g_ffn1 = _ffn_backward(dx, dy, dg, s_ffn1, ffn[1], gf[1], ffn_w1, "l1_ffn",
                                         functools.partial(emit, "l1_ffn"), below(s_conv, mix[1]))
    (dx, dy, dg), g_conv = _conformer_backward(dx, dy, dg, s_conv, mix[1], gm[1], conv_w, "l1_conv",
                                               functools.partial(emit, "l1_conv"), below(s_ffn0, ffn[0]))
    (dx, dy, dg), g_ffn0 = _ffn_backward(dx, dy, dg, s_ffn0, ffn[0], gf[0], ffn_w0, "l0_ffn",
                                         functools.partial(emit, "l0_ffn"), below(s_mix, mix[0]))
    (dx,), g_mix = _mixer_backward(dx, dy, dg, s_mix, mix[0], gm[0], mixer_w, rope, "l0_mix",
                                   functools.partial(emit, "l0_mix"), None)
    blocks = [g_mix, g_ffn0, g_conv, g_ffn1]
    dmod = jnp.stack([jnp.concatenate([a["sh"], a["sc"], a["gate"], b["sh"], b["sc"], b["gate"]], axis=1)[0]
                      for a, b in ((g_mix, g_ffn0), (g_conv, g_ffn1))])
    return loss, dx, dmod, blocks


def _pack(arrs, rows=8):
    flat = jnp.concatenate([a.reshape(-1).astype(F32) for a in arrs])
    n = flat.shape[0]
    cols = -(-n // (rows * 128)) * 128
    return jnp.pad(flat, (0, rows * cols - n)).reshape(rows, cols)


def _unpack(flat, shapes):
    out, off = [], 0
    for shp in shapes:
        n = math.prod(shp)
        out.append(flat[..., off:off + n].reshape(flat.shape[:-1] + tuple(shp)))
        off += n
    return out


def _take_block(a, idx, size, axis):
    return lax.dynamic_slice_in_dim(a, idx * size, size, axis)


def kernel(x, c, positions, ada_w, ada_b, norm_mix_g, norm_ffn_g, ab_w_in, a_vnorm_g, a_spatial_w, a_spatial_b, b_q_norm_g, b_k_norm_g, ab_w_out, conv_pw1_w, conv_pw1_b, conv_dw_w, conv_dw_b, conv_ln_g, conv_ln_b, conv_pw2_w, conv_pw2_b, ffn_up_w, ffn_dw_w, ffn_dw_b, ffn_down_w, loss_target, m_ada_w, m_ada_b, m_norm_mix_g, m_norm_ffn_g, m_ab_w_in, m_a_vnorm_g, m_a_spatial_w, m_a_spatial_b, m_b_q_norm_g, m_b_k_norm_g, m_ab_w_out, m_conv_pw1_w, m_conv_pw1_b, m_conv_dw_w, m_conv_dw_b, m_conv_ln_g, m_conv_ln_b, m_conv_pw2_w, m_conv_pw2_b, m_ffn_up_w, m_ffn_dw_w, m_ffn_dw_b, m_ffn_down_w, v_ada_w, v_ada_b, v_norm_mix_g, v_norm_ffn_g, v_ab_w_in, v_a_vnorm_g, v_a_spatial_w, v_a_spatial_b, v_b_q_norm_g, v_b_k_norm_g, v_ab_w_out, v_conv_pw1_w, v_conv_pw1_b, v_conv_dw_w, v_conv_dw_b, v_conv_ln_g, v_conv_ln_b, v_conv_pw2_w, v_conv_pw2_b, v_ffn_up_w, v_ffn_dw_w, v_ffn_dw_b, v_ffn_down_w):
    weights = dict(ada_w=ada_w, ada_b=ada_b, norm_mix_g=norm_mix_g, norm_ffn_g=norm_ffn_g, ab_w_in=ab_w_in, a_vnorm_g=a_vnorm_g, a_spatial_w=a_spatial_w, a_spatial_b=a_spatial_b, b_q_norm_g=b_q_norm_g, b_k_norm_g=b_k_norm_g, ab_w_out=ab_w_out, conv_pw1_w=conv_pw1_w, conv_pw1_b=conv_pw1_b, conv_dw_w=conv_dw_w, conv_dw_b=conv_dw_b, conv_ln_g=conv_ln_g, conv_ln_b=conv_ln_b, conv_pw2_w=conv_pw2_w, conv_pw2_b=conv_pw2_b, ffn_up_w=ffn_up_w, ffn_dw_w=ffn_dw_w, ffn_dw_b=ffn_dw_b, ffn_down_w=ffn_down_w)
    mom1 = dict(ada_w=m_ada_w, ada_b=m_ada_b, norm_mix_g=m_norm_mix_g, norm_ffn_g=m_norm_ffn_g, ab_w_in=m_ab_w_in, a_vnorm_g=m_a_vnorm_g, a_spatial_w=m_a_spatial_w, a_spatial_b=m_a_spatial_b, b_q_norm_g=m_b_q_norm_g, b_k_norm_g=m_b_k_norm_g, ab_w_out=m_ab_w_out, conv_pw1_w=m_conv_pw1_w, conv_pw1_b=m_conv_pw1_b, conv_dw_w=m_conv_dw_w, conv_dw_b=m_conv_dw_b, conv_ln_g=m_conv_ln_g, conv_ln_b=m_conv_ln_b, conv_pw2_w=m_conv_pw2_w, conv_pw2_b=m_conv_pw2_b, ffn_up_w=m_ffn_up_w, ffn_dw_w=m_ffn_dw_w, ffn_dw_b=m_ffn_dw_b, ffn_down_w=m_ffn_down_w)
    mom2 = dict(ada_w=v_ada_w, ada_b=v_ada_b, norm_mix_g=v_norm_mix_g, norm_ffn_g=v_norm_ffn_g, ab_w_in=v_ab_w_in, a_vnorm_g=v_a_vnorm_g, a_spatial_w=v_a_spatial_w, a_spatial_b=v_a_spatial_b, b_q_norm_g=v_b_q_norm_g, b_k_norm_g=v_b_k_norm_g, ab_w_out=v_ab_w_out, conv_pw1_w=v_conv_pw1_w, conv_pw1_b=v_conv_pw1_b, conv_dw_w=v_conv_dw_w, conv_dw_b=v_conv_dw_b, conv_ln_g=v_conv_ln_g, conv_ln_b=v_conv_ln_b, conv_pw2_w=v_conv_pw2_w, conv_pw2_b=v_conv_pw2_b, ffn_up_w=v_ffn_up_w, ffn_dw_w=v_ffn_dw_w, ffn_dw_b=v_ffn_dw_b, ffn_down_w=v_ffn_down_w)
    order = list(weights)
    d, f2 = D_MODEL, 2 * FFN_DIM
    t = x.shape[1]
    me = 4 * lax.axis_index("x") + 2 * lax.axis_index("y") + lax.axis_index("c")
    for window, dil in PATTERNS:
        assert window // dil == Q_BLOCK and t % (dil * Q_BLOCK) == 0

    small_in = [c[0], conv_pw1_b[0], conv_dw_w[0], conv_dw_b[0], conv_ln_g[0], conv_ln_b[0], conv_pw2_b[0], ffn_dw_w]
    g1 = _all_gather_vmem(_pack(small_in, rows=8), "gather_small").reshape(N_DEV, -1)
    c_all, pw1_b, dw_w, dw_b, ln_g, ln_b, pw2_b, fdw_w = _unpack(g1, [a.shape for a in small_in])
    pw1_b, dw_b, ln_g, ln_b, pw2_b = [a.reshape(1, -1) for a in (pw1_b, dw_b, ln_g, ln_b, pw2_b)]
    dw_w = dw_w.transpose(1, 0, 2).reshape(CONV_WIDTH, d)
    fdw_w = fdw_w.transpose(1, 2, 0, 3).reshape(2, FFN_CONV_WIDTH, f2)

    c16 = jnp.pad(c_all, ((0, 2 * N_DEV - c_all.shape[0]), (0, 0)))
    part = jnp.concatenate([_ada_fwd(c16, ada_w[l], f"ada_fwd{l}")[:N_DEV] for l in range(2)], axis=1)
    g2 = _all_gather_vmem(part, "gather_mod").reshape(N_DEV, N_DEV, 2, -1)
    mod = lax.dynamic_index_in_dim(g2, me, axis=1, keepdims=False).transpose(1, 0, 2).reshape(2, 6 * d) + ada_b

    stages = dict(l0_mix=[ab_w_in[0].T, ab_w_out[0]], l0_ffn=[ffn_up_w[0].T, ffn_down_w[0]],
                  l1_conv=[conv_pw1_w[0].T, conv_pw2_w[0]], l1_ffn=[ffn_up_w[1].T, ffn_down_w[1]])
    stages = {k: [s.astype(BF16) for s in v] for k, v in stages.items()}
    names = dict(l0_mix=("w_in_t", "w_out"), l0_ffn=("up_t", "down"), l1_conv=("pw1_t", "pw2"), l1_ffn=("up_t", "down"))
    ready = {"l0_mix": [a.reshape(-1, d) for a in _all_gather_hbm(stages["l0_mix"], "gather_mixer_weights")]}
    behind = (*ready["l0_mix"], mod)
    arriving = {}
    for stage, group in (("l0_ffn", ("l0_ffn",)), ("l1_conv", ("l1_conv", "l1_ffn"))):
        srcs = [s for g in group for s in stages[g]]
        arriving[stage], token = _pushes_start(
            srcs, _landing_zones(srcs, behind, f"gather_{stage}_zones"), False, f"gather_{stage}_start")
        behind = (token,)
        mod = mod + token[0:1, 0:1]

    def fetch(stage, after):
        if stage in arriving:
            full = [a.reshape(-1, d) for a in _pushes_wait(arriving[stage], after, f"gather_{stage}_wait")[1]]
            ready[stage] = full[:2]
            if stage == "l1_conv":
                ready["l1_ffn"] = full[2:]
        return dict(zip(names[stage], ready[stage]))

    causal = jnp.tril(jnp.ones((CHUNK, CHUNK), bool))
    wtril = jnp.where(causal[None], a_spatial_w[0], 0.0)
    mixer_w = dict(
        gain=a_vnorm_g[0].reshape(1, A_WIDTH), wtril=wtril.astype(BF16),
        wtril_t=wtril.transpose(0, 2, 1).astype(BF16),
        bias_exp=jnp.repeat(a_spatial_b[0].T, GROUP_DIM, axis=1),
        gq=jnp.tile(b_q_norm_g[0], HEADS)[None, :], gk=jnp.tile(b_k_norm_g[0], HEADS)[None, :],
        seg=jnp.kron(jnp.eye(HEADS, dtype=BF16), jnp.ones((HEAD_DIM, HEAD_DIM), BF16)))
    conv_w = dict(pw1_b=pw1_b, dw_w=dw_w, dw_b=dw_b, ln_g=ln_g, ln_b=ln_b, pw2_b=pw2_b)
    ffn_w = [dict(dw_w=fdw_w[l].reshape(FFN_CONV_WIDTH, 2, FFN_DIM).transpose(1, 0, 2), dw_b=ffn_dw_b[l].reshape(2, 1, FFN_DIM))
             for l in range(2)]

    leaving = {}

    def emit(stage, grads):
        blocks = [g.reshape(N_DEV, g.shape[0] // N_DEV, d) for g in grads]
        leaving[stage], token = _pushes_start(
            blocks, [lax.empty(b.shape, b.dtype) for b in blocks], True, f"reduce_{stage}_start")
        return token

    loss, dx, dmod, (g_mix, g_ffn0, g_conv, g_ffn1) = _local_step(
        x[0], loss_target[0], positions[0].astype(F32)[:, None], mod, norm_mix_g, norm_ffn_g, mixer_w, conv_w, ffn_w,
        fetch, emit)

    me_op = me.astype(jnp.int32).reshape(1)

    def reduced(stage, after):
        blocks, lands = _pushes_wait(leaving[stage], after, f"reduce_{stage}_wait")
        return [_sum_with_own(b, a, me_op, f"reduce_{stage}_sum{i}") for i, (b, a) in enumerate(zip(blocks, lands))]

    (r_up_t1, r_down1), (r_pw1_t, r_pw2), (r_up_t0, r_down0) = [reduced(s, dx) for s in ("l1_ffn", "l1_conv", "l0_ffn")]

    small_g = [
        dmod, jnp.concatenate([g_mix["norm_g"], g_conv["norm_g"]]), jnp.concatenate([g_ffn0["norm_g"], g_ffn1["norm_g"]]),
        g_mix["vnorm_g"], g_mix["spatial_w"], g_mix["spatial_b"], g_mix["q_norm_g"], g_mix["k_norm_g"],
        g_conv["pw1_b"], g_conv["dw_w"], g_conv["dw_b"], g_conv["ln_g"], g_conv["ln_b"], g_conv["pw2_b"],
        jnp.stack([g_ffn0["dw_w"], g_ffn1["dw_w"]]), jnp.concatenate([g_ffn0["dw_b"], g_ffn1["dw_b"]])]
    packed = _pack(small_g, rows=8)
    small_leaving, token = _pushes_start([packed], [lax.empty((N_DEV,) + packed.shape, F32)], False, "gather_small_grads_start")

    grads = dict(conv_pw2_w=r_pw2[None], ffn_down_w=jnp.stack([r_down0, r_down1]))
    grads_t = dict(conv_pw1_w=r_pw1_t[None], ffn_up_w=jnp.stack([r_up_t0, r_up_t1]))
    flip = lambda a: jnp.swapaxes(a, 1, 2)
    delta, new_m, new_v = {}, {}, {}

    def update(name, behind):
        if name in grads_t:
            grads[name] = flip(grads_t[name])
            res = _adamw(flip(weights[name]), grads_t[name], flip(mom1[name]), flip(mom2[name]), behind, f"adamw_{name}")
            delta[name], new_m[name], new_v[name] = [flip(r) for r in res]
        else:
            delta[name], new_m[name], new_v[name] = _adamw(
                weights[name], grads[name], mom1[name], mom2[name], behind, f"adamw_{name}")

    for name in ("conv_pw1_w", "conv_pw2_w", "ffn_up_w", "ffn_down_w"):
        update(name, token)
    r_in_t, r_out = reduced("l0_mix", new_v["ffn_down_w"])
    grads_t["ab_w_in"], grads["ab_w_out"] = r_in_t[None], r_out[None]
    update("ab_w_in", token)
    update("ab_w_out", token)

    (packed,), (landed,) = _pushes_wait(small_leaving, tuple(new_v.values()), "gather_small_grads_wait")
    total = _sum_in_device_order(packed, landed, me_op, "sum_small_grads")
    (s_dmod, s_mix_g, s_ffn_g, s_vnorm, s_sp_w, s_sp_b, s_gq, s_gk, s_pw1_b, s_dw_w, s_dw_b, s_ln_g, s_ln_b,
     s_pw2_b, s_fdw_w, s_fdw_b) = _unpack(total.reshape(-1), [a.shape for a in small_g])
    dmod_all = lax.dynamic_update_slice(
        landed.reshape(N_DEV, -1)[:, :dmod.size].reshape((N_DEV,) + dmod.shape), dmod[None], (me, 0, 0))
    n_ada = ada_w.shape[2]
    dmod16 = jnp.pad(_take_block(dmod_all, me, n_ada, 2), ((0, N_DEV), (0, 0), (0, 0)))
    grads.update(
        ada_w=jnp.stack([_ada_bwd(c16, dmod16[:, l], f"ada_bwd{l}") for l in range(2)]),
        ada_b=s_dmod, norm_mix_g=s_mix_g, norm_ffn_g=s_ffn_g,
        a_vnorm_g=s_vnorm[None], a_spatial_w=s_sp_w[None], a_spatial_b=s_sp_b[None], b_q_norm_g=s_gq[None],
        b_k_norm_g=s_gk[None],
        conv_pw1_b=_take_block(s_pw1_b, me, conv_pw1_b.shape[1], 1),
        conv_dw_w=_take_block(s_dw_w, me, conv_dw_w.shape[2], 1)[None],
        conv_dw_b=_take_block(s_dw_b, me, conv_dw_b.shape[1], 1), conv_ln_g=_take_block(s_ln_g, me, conv_ln_g.shape[1], 1),
        conv_ln_b=_take_block(s_ln_b, me, conv_ln_b.shape[1], 1),
        conv_pw2_b=_take_block(s_pw2_b, me, conv_pw2_b.shape[1], 1),
        ffn_dw_w=_take_block(s_fdw_w, me, ffn_dw_w.shape[2], 2), ffn_dw_b=s_fdw_b)
    update("ada_w", None)
    large = ("ada_w", "conv_pw1_w", "conv_pw2_w", "ffn_up_w", "ffn_down_w", "ab_w_in", "ab_w_out")
    small = [n for n in order if n not in large]
    res = _adamw_small(*[[src[n] for n in small] for src in (weights, grads, mom1, mom2)], "adamw_small")
    for dst, arrs in zip((delta, new_m, new_v), res):
        dst.update(zip(small, arrs))

    loss = lax.psum(loss[0, 0], ("x", "y", "c"))
    return (loss, dx[None], *[grads[n] for n in order], *[delta[n] for n in order],
            *[new_m[n] for n in order], *[new_v[n] for n in order])
```

```python
import functools
import math

import jax
import jax.numpy as jnp
from jax import lax
from jax.experimental import pallas as pl
from jax.experimental.pallas import tpu as pltpu

F32 = jnp.float32
BF16 = jnp.bfloat16
MESH = pl.DeviceIdType.MESH

D_MODEL = 1024
A_WIDTH = 512
A_GROUPS = 4
GROUP_DIM = 128
CHUNK = 128
B_WIDTH = 512
HEADS = 8
HEAD_DIM = 64
PATTERNS = ((128, 1), (512, 4), (2048, 16))
Q_BLOCK = 128
ROPE_THETA = 10000.0
AB_IN = 2560
CONV_WIDTH = 31
FFN_DIM = 2816
FFN_CONV_WIDTH = 3
EPS = 1e-6
NEG = -1e30
N_DEV = 8
ADAM_LR, ADAM_B1, ADAM_B2, ADAM_EPS, ADAM_WD, ADAM_STEP = 0.001, 0.9, 0.999, 1e-08, 0.01, 10

V7X_VMEM_LIMIT = 56 * 2**20
FFN_HALO = 16
CONV_HALO = 32

_NN = (((1,), (0,)), ((), ()))
_NT = (((1,), (1,)), ((), ()))
_TN = (((0,), (0,)), ((), ()))


def _tile(n, prefs=(512, 256, 128)):
    for t in prefs:
        if n % t == 0:
            return t
    return n


def _row_tile(n, cap=512):
    best = n
    for t in range(8, min(n, cap) + 1, 8):
        if n % t == 0:
            best = t
    return best if best <= cap else n


def _params(*sem):
    return pltpu.CompilerParams(dimension_semantics=sem, vmem_limit_bytes=V7X_VMEM_LIMIT)


def _dot(a, b, dims):
    return lax.dot_general(a, b, dims, preferred_element_type=F32)


def _sigmoid(x):
    return 1.0 / (1.0 + jnp.exp(-x))


def _gelu(x):
    return 0.5 * x * (1.0 + lax.erf(x * (2.0 ** -0.5)))


def _gelu_grad(x):
    return 0.5 * (1.0 + lax.erf(x * (2.0 ** -0.5))) + x * jnp.exp(-0.5 * x * x) * (1.0 / math.sqrt(2.0 * math.pi))


def _colsum(v):
    return jnp.sum(v, axis=0, keepdims=True)


MATMUL_VMEM_BUDGET = 40 * 2**20


def _matmul_tiles(m, n, k, out_bytes, with_resid):
    def options(dim):
        opts = [t for t in (1024, 512, 256, 128) if dim % t == 0]
        return opts + [dim] if dim <= 4096 and dim not in opts else opts

    best = None
    for tm in options(m):
        for tn in options(n):
            need = 4 * (tm * k + k * tn) + tm * tn * (4 + 2 * out_bytes) + (24 * tm * tn if with_resid else 0)
            if need <= MATMUL_VMEM_BUDGET and (best is None or tm * tn / (tm + tn) > best[0]):
                best = (tm * tn / (tm + tn), tm, tn)
    return best[1], best[2]


def _matmul_tn_acc(a, b, name, tk=1024):
    squeeze = a.ndim == 2
    a3 = a[None] if squeeze else a
    p_, t, m = a3.shape
    n = b.shape[1]
    nk = t // tk

    def body(a_ref, b_ref, o_ref, acc_ref):
        kt = pl.program_id(1)

        @pl.when(kt == 0)
        def _():
            acc_ref[...] = jnp.zeros_like(acc_ref)

        acc_ref[...] += _dot(a_ref[...], b_ref[...], _TN)

        @pl.when(kt == nk - 1)
        def _():
            o_ref[...] = acc_ref[...].astype(BF16)

    out = pl.pallas_call(
        body, name=name, grid=(p_, nk),
        in_specs=[pl.BlockSpec((None, tk, m), lambda p, kt: (p, kt, 0)), pl.BlockSpec((tk, n), lambda p, kt: (kt, 0))],
        out_specs=pl.BlockSpec((None, m, n), lambda p, kt: (p, 0, 0)), out_shape=jax.ShapeDtypeStruct((p_, m, n), BF16),
        scratch_shapes=[pltpu.VMEM((m, n), F32)], compiler_params=_params("parallel", "arbitrary"),
    )(a3, b)
    return out[0] if squeeze else out


def _matmul(a, b, mode, out_dtype, name, bias=None, resid=None):
    if mode == "nn":
        (m, k), (_, n) = a.shape, b.shape
    elif mode == "nt":
        (m, k), (n, _) = a.shape, b.shape
    else:
        (k, m), (_, n) = a.shape, b.shape
    tm, tn = _matmul_tiles(m, n, k, jnp.dtype(out_dtype).itemsize, resid is not None)
    dims = {"nn": _NN, "nt": _NT, "tn": _TN}[mode]
    a_spec = pl.BlockSpec((k, tm), lambda i, j: (0, i)) if mode == "tn" else pl.BlockSpec((tm, k), lambda i, j: (i, 0))
    b_spec = pl.BlockSpec((tn, k), lambda i, j: (j, 0)) if mode == "nt" else pl.BlockSpec((k, tn), lambda i, j: (0, j))
    in_specs, args = [a_spec, b_spec], [a, b]
    row_spec = pl.BlockSpec((1, tn), lambda i, j: (0, j))
    tile_spec = pl.BlockSpec((tm, tn), lambda i, j: (i, j))
    if bias is not None:
        in_specs.append(row_spec)
        args.append(bias)
    if resid is not None:
        in_specs += [tile_spec, row_spec]
        args += list(resid)
    out_shape = [jax.ShapeDtypeStruct((m, n), out_dtype)]
    out_specs = [tile_spec]
    if resid is not None:
        out_shape.append(jax.ShapeDtypeStruct((m, n), F32))
        out_specs.append(tile_spec)

    def body(*refs):
        a_ref, b_ref = refs[0], refs[1]
        pos = 2
        acc = _dot(a_ref[...], b_ref[...], dims)
        if bias is not None:
            acc = acc + refs[pos][...]
            pos += 1
        if resid is not None:
            x_ref, g_ref = refs[pos], refs[pos + 1]
            pos += 2
        refs[pos][...] = acc.astype(out_dtype)
        if resid is not None:
            refs[pos + 1][...] = x_ref[...] + g_ref[...] * acc

    outs = pl.pallas_call(
        body, name=name, grid=(m // tm, n // tn), in_specs=in_specs, out_specs=out_specs, out_shape=out_shape,
        compiler_params=_params("parallel", "parallel"),
    )(*args)
    return outs if resid is not None else outs[0]


NORM_TM = (1024, 512, 256, 128)


def _modnorm(x, g, sc, sh, name):
    t, d = x.shape
    tm = _tile(t, NORM_TM)
    row = pl.BlockSpec((1, d), lambda i: (0, 0))
    blk = pl.BlockSpec((tm, d), lambda i: (i, 0))

    def body(x_ref, g_ref, sc_ref, sh_ref, o_ref):
        x = x_ref[...]
        r = lax.rsqrt(jnp.mean(x * x, axis=-1, keepdims=True) + EPS)
        o_ref[...] = ((x * r) * g_ref[...] * (1.0 + sc_ref[...]) + sh_ref[...]).astype(BF16)

    return pl.pallas_call(
        body, name=name, grid=(t // tm,), in_specs=[blk, row, row, row], out_specs=blk,
        out_shape=jax.ShapeDtypeStruct((t, d), BF16), compiler_params=_params("parallel"),
    )(x, g, sc, sh)


def _gate_bwd_tile(dx, y_ref, gate_ref, dy_ref, dgate_ref, first):
    @pl.when(first)
    def _():
        dgate_ref[...] = jnp.zeros_like(dgate_ref)

    dy_ref[...] = (dx * gate_ref[...]).astype(BF16)
    dgate_ref[...] += _colsum(dx * y_ref[...].astype(F32))


def _modnorm_bwd(x, dh, g, sc, dres, below, name):
    t, d = x.shape
    tm = _tile(t, NORM_TM)
    row = pl.BlockSpec((1, d), lambda i: (0, 0))
    blk = pl.BlockSpec((tm, d), lambda i: (i, 0))

    def body(x_ref, dh_ref, g_ref, sc_ref, dres_ref, *rest):
        dx_ref, dw_ref, dsh_ref = rest[-5:-2] if below else rest
        first = pl.program_id(0) == 0

        @pl.when(first)
        def _():
            dw_ref[...] = jnp.zeros_like(dw_ref)
            dsh_ref[...] = jnp.zeros_like(dsh_ref)

        x = x_ref[...]
        dh = dh_ref[...].astype(F32)
        r = lax.rsqrt(jnp.mean(x * x, axis=-1, keepdims=True) + EPS)
        xn = x * r
        dxn = dh * (g_ref[...] * (1.0 + sc_ref[...]))
        dx = dres_ref[...] + r * (dxn - xn * jnp.mean(dxn * xn, axis=-1, keepdims=True))
        dx_ref[...] = dx
        dw_ref[...] += _colsum(dh * xn)
        dsh_ref[...] += _colsum(dh)
        if below:
            _gate_bwd_tile(dx, rest[0], rest[1], rest[-2], rest[-1], first)

    row_out = jax.ShapeDtypeStruct((1, d), F32)
    return pl.pallas_call(
        body, name=name, grid=(t // tm,), in_specs=[blk, blk, row, row, blk] + ([blk, row] if below else []),
        out_specs=[blk, row, row] + ([blk, row] if below else []),
        out_shape=[jax.ShapeDtypeStruct((t, d), F32), row_out, row_out]
        + ([jax.ShapeDtypeStruct((t, d), BF16), row_out] if below else []),
        compiler_params=_params("arbitrary"),
    )(x, dh, g, sc, dres, *(below or ()))


def _loss_head(y, target, below, name):
    t, d = y.shape
    tm = _tile(t, NORM_TM)
    blk = pl.BlockSpec((tm, d), lambda i: (i, 0))
    row = pl.BlockSpec((1, d), lambda i: (0, 0))
    one = pl.BlockSpec((1, 1), lambda i: (0, 0))
    steps = t // tm

    def body(y_ref, t_ref, yb_ref, gate_ref, dx_ref, loss_ref, dy_ref, dgate_ref, acc_ref):
        first = pl.program_id(0) == 0

        @pl.when(first)
        def _():
            acc_ref[...] = jnp.zeros_like(acc_ref)

        e = y_ref[...] - t_ref[...]
        dx = e * (1.0 / d)
        dx_ref[...] = dx
        acc_ref[...] += _colsum(e * e)
        _gate_bwd_tile(dx, yb_ref, gate_ref, dy_ref, dgate_ref, first)

        @pl.when(pl.program_id(0) == steps - 1)
        def _():
            loss_ref[...] = jnp.sum(acc_ref[...], axis=1, keepdims=True) * (0.5 / d)

    return pl.pallas_call(
        body, name=name, grid=(steps,), in_specs=[blk, blk, blk, row], out_specs=[blk, one, blk, row],
        out_shape=[jax.ShapeDtypeStruct((t, d), F32), jax.ShapeDtypeStruct((1, 1), F32),
                   jax.ShapeDtypeStruct((t, d), BF16), jax.ShapeDtypeStruct((1, d), F32)],
        scratch_shapes=[pltpu.VMEM((1, d), F32)], compiler_params=_params("arbitrary"),
    )(y, target, *below)


GMLP_TM = 512


def _group_norm(vg, gain):
    mu = jnp.mean(vg, axis=-1, keepdims=True)
    xc = vg - mu
    rstd = lax.rsqrt(jnp.mean(xc * xc, axis=-1, keepdims=True) + EPS)
    xhat = xc * rstd
    return xhat, rstd, xhat * gain


def _gmlp_fwd(z, gain, wtril, bias_exp, name):
    t = z.shape[0]
    tm = _tile(t, (GMLP_TM,))
    zu = pl.BlockSpec((tm, A_WIDTH), lambda i: (i, 0))
    zv = pl.BlockSpec((tm, A_WIDTH), lambda i: (i, 1))
    full2 = lambda shp: pl.BlockSpec(shp, lambda i: (0, 0))
    w_spec = pl.BlockSpec((A_GROUPS, CHUNK, CHUNK), lambda i: (0, 0, 0))

    def body(zu_ref, zv_ref, gain_ref, w_ref, b_ref, ya_ref):
        for c in range(tm // CHUNK):
            rows = slice(c * CHUNK, (c + 1) * CHUNK)
            ua = _gelu(zu_ref[rows, :].astype(F32))
            vg = _gelu(zv_ref[rows, :].astype(F32))
            for g in range(A_GROUPS):
                sl = slice(g * GROUP_DIM, (g + 1) * GROUP_DIM)
                _, _, vn = _group_norm(vg[:, sl], gain_ref[:, sl])
                f = _dot(w_ref[g], vn.astype(BF16), _NN) + b_ref[:, sl]
                ya_ref[rows, sl] = (ua[:, sl] * f).astype(BF16)

    return pl.pallas_call(
        body, name=name, grid=(t // tm,),
        in_specs=[zu, zv, full2((1, A_WIDTH)), w_spec, full2((CHUNK, A_WIDTH))], out_specs=zu,
        out_shape=jax.ShapeDtypeStruct((t, A_WIDTH + B_WIDTH), BF16), compiler_params=_params("parallel"),
    )(z, z, gain, wtril, bias_exp)


def _gmlp_bwd(z, dcat, gain, wtril, wtril_t, bias_exp, name):
    t = z.shape[0]
    tm = _tile(t, (GMLP_TM,))
    zu = pl.BlockSpec((tm, A_WIDTH), lambda i: (i, 0))
    zv = pl.BlockSpec((tm, A_WIDTH), lambda i: (i, 1))
    full2 = lambda shp: pl.BlockSpec(shp, lambda i: (0, 0))
    w_spec = pl.BlockSpec((A_GROUPS, CHUNK, CHUNK), lambda i: (0, 0, 0))
    dz_spec = pl.BlockSpec((tm, 2 * A_WIDTH), lambda i: (i, 0))

    def body(zu_ref, zv_ref, dya_ref, gain_ref, w_ref, wt_ref, b_ref, dz_ref, dw_ref, dgain_ref, dbias_ref):
        @pl.when(pl.program_id(0) == 0)
        def _():
            dw_ref[...] = jnp.zeros_like(dw_ref)
            dgain_ref[...] = jnp.zeros_like(dgain_ref)
            dbias_ref[...] = jnp.zeros_like(dbias_ref)

        row = lax.broadcasted_iota(jnp.int32, (CHUNK, CHUNK), 0)
        col = lax.broadcasted_iota(jnp.int32, (CHUNK, CHUNK), 1)
        for c in range(tm // CHUNK):
            rows = slice(c * CHUNK, (c + 1) * CHUNK)
            zu_v = zu_ref[rows, :].astype(F32)
            zv_v = zv_ref[rows, :].astype(F32)
            dya = dya_ref[rows, :].astype(F32)
            ua = _gelu(zu_v)
            vg = _gelu(zv_v)
            for g in range(A_GROUPS):
                sl = slice(g * GROUP_DIM, (g + 1) * GROUP_DIM)
                gain_g = gain_ref[:, sl]
                xhat, rstd, vn = _group_norm(vg[:, sl], gain_g)
                vn16 = vn.astype(BF16)
                f = _dot(w_ref[g], vn16, _NN) + b_ref[:, sl]
                df = dya[:, sl] * ua[:, sl]
                df16 = df.astype(BF16)
                dz_ref[rows, sl] = (dya[:, sl] * f * _gelu_grad(zu_v[:, sl])).astype(BF16)
                dw_ref[g] += jnp.where(row >= col, _dot(df16, vn16, _NT), 0.0)
                dvn = _dot(wt_ref[g], df16, _NN)
                dgain_ref[:, sl] += _colsum(dvn * xhat)
                dxh = dvn * gain_g
                dvg = rstd * (dxh - jnp.mean(dxh, axis=-1, keepdims=True) - xhat * jnp.mean(dxh * xhat, axis=-1, keepdims=True))
                dz_ref[rows, A_WIDTH + g * GROUP_DIM:A_WIDTH + (g + 1) * GROUP_DIM] = (dvg * _gelu_grad(zv_v[:, sl])).astype(BF16)
                dbias_ref[:, sl] += df

    return pl.pallas_call(
        body, name=name, grid=(t // tm,),
        in_specs=[zu, zv, zu, full2((1, A_WIDTH)), w_spec, w_spec, full2((CHUNK, A_WIDTH))],
        out_specs=[dz_spec, w_spec, full2((1, A_WIDTH)), full2((CHUNK, A_WIDTH))],
        out_shape=[jax.ShapeDtypeStruct((t, 2 * A_WIDTH), BF16), jax.ShapeDtypeStruct((A_GROUPS, CHUNK, CHUNK), F32),
                   jax.ShapeDtypeStruct((1, A_WIDTH), F32), jax.ShapeDtypeStruct((CHUNK, A_WIDTH), F32)],
        compiler_params=_params("arbitrary"),
    )(z, z, dcat, gain, wtril, wtril_t, bias_exp)


def _rope_tables(pos, inv_freq, sign, name):
    t = pos.shape[0]
    tm = _tile(t)
    row = pl.BlockSpec((1, B_WIDTH), lambda i: (0, 0))
    blk = pl.BlockSpec((tm, B_WIDTH), lambda i: (i, 0))

    def body(pos_ref, f_ref, s_ref, cos_ref, sin_ref):
        ang = pos_ref[...] * f_ref[:, 0:LANES]
        cos_ref[...] = jnp.tile(jnp.cos(ang), (1, B_WIDTH // LANES))
        sin_ref[...] = jnp.tile(jnp.sin(ang) * s_ref[:, 0:LANES], (1, B_WIDTH // LANES))

    return pl.pallas_call(
        body, name=name, grid=(t // tm,), in_specs=[pl.BlockSpec((tm, 1), lambda i: (i, 0)), row, row],
        out_specs=[blk, blk], out_shape=[jax.ShapeDtypeStruct((t, B_WIDTH), F32)] * 2,
        compiler_params=_params("parallel"),
    )(pos, inv_freq, sign)


def _head_sum(v, seg):
    hi = v.astype(BF16)
    lo = (v - hi.astype(F32)).astype(BF16)
    return _dot(hi, seg, _NN) + _dot(lo, seg, _NN)


def _swap_halves(v):
    lane = lax.broadcasted_iota(jnp.int32, v.shape, 1)
    return jnp.where((lane & (HEAD_DIM - 1)) < HEAD_DIM // 2,pltpu.roll(v, B_WIDTH - HEAD_DIM // 2, 1), pltpu.roll(v, HEAD_DIM // 2, 1))


DILATIONS = tuple(dil for _, dil in PATTERNS)
SUBSEQ_TM = 512
LANES = 128


def _subseq_shape(t, dil):
    return (t // dil, dil * B_WIDTH)


def _subseq_spec(tm, dil):
    return pl.BlockSpec((tm // dil, dil * B_WIDTH), lambda i: (i, 0))


def _to_subseq(x, scr_ref, dil):
    if dil == 1:
        return x
    tm, w = x.shape
    for c in range(w // LANES):
        scr_ref[c * tm:(c + 1) * tm, :] = x[:, c * LANES:(c + 1) * LANES]
    return jnp.concatenate([scr_ref[pl.ds(c * tm + r, tm // dil, stride=dil), :]
                            for r in range(dil) for c in range(w // LANES)], axis=1)


def _from_subseq(y, scr_ref, dil):
    if dil == 1:
        return y
    n, w = y.shape[0], y.shape[1] // dil
    tm = n * dil
    for r in range(dil):
        for c in range(w // LANES):
            scr_ref[pl.ds(c * tm + r, n, stride=dil), :] = y[:, r * w + c * LANES:r * w + (c + 1) * LANES]
    return jnp.concatenate([scr_ref[c * tm:(c + 1) * tm, :] for c in range(w // LANES)], axis=1)


def _subseq_scratch(tm):
    return pltpu.VMEM((B_WIDTH // LANES * tm, LANES), F32)


def _qk_prep(z, cos_t, sin_t, gq, gk, seg, name):
    t = z.shape[0]
    tm = _tile(t, (SUBSEQ_TM,))
    col = lambda c: pl.BlockSpec((tm, B_WIDTH), lambda i: (i, c))
    row = pl.BlockSpec((1, B_WIDTH), lambda i: (0, 0))
    blk = col(0)
    nd = len(DILATIONS)

    def body(q_ref, k_ref, v_ref, cos_ref, sin_ref, gq_ref, gk_ref, seg_ref, *rest):
        out_refs, scr_ref = rest[:-1], rest[-1]

        def norm_rot(x, g):
            r = lax.rsqrt(_head_sum(x * x, seg_ref[...]) * (1.0 / HEAD_DIM) + EPS)
            xn = x * r * g
            return xn * cos_ref[...] + _swap_halves(xn) * sin_ref[...]

        vals = (norm_rot(q_ref[...].astype(F32), gq_ref[...]), norm_rot(k_ref[...].astype(F32), gk_ref[...]),
                v_ref[...].astype(F32))
        for a, val in enumerate(vals):
            for b, dil in enumerate(DILATIONS):
                out_refs[a * nd + b][...] = _to_subseq(val, scr_ref, dil).astype(BF16)

    outs = pl.pallas_call(
        body, name=name, grid=(t // tm,),
        in_specs=[col(2), col(3), col(4), blk, blk, row, row, pl.BlockSpec((B_WIDTH, B_WIDTH), lambda i: (0, 0))],
        out_specs=[_subseq_spec(tm, dil) for _ in range(3) for dil in DILATIONS],
        out_shape=[jax.ShapeDtypeStruct(_subseq_shape(t, dil), BF16) for _ in range(3) for dil in DILATIONS],
        scratch_shapes=[_subseq_scratch(tm)], compiler_params=_params("parallel"),
    )(z, z, z, cos_t, sin_t, gq, gk, seg)
    return outs[:nd], outs[nd:2 * nd], outs[2 * nd:]


def _qk_prep_bwd(z, dqs, dks, dvs, cos_t, sin_t, gq, gk, seg, name):
    t = z.shape[0]
    tm = _tile(t, (SUBSEQ_TM,))
    col = lambda c: pl.BlockSpec((tm, B_WIDTH), lambda i: (i, c))
    row = pl.BlockSpec((1, B_WIDTH), lambda i: (0, 0))
    blk = col(0)
    nb = len(DILATIONS)
    subs = [_subseq_spec(tm, dil) for dil in DILATIONS]

    def body(*refs):
        q_ref, k_ref = refs[0], refs[1]
        dq_refs, dk_refs, dv_refs = refs[2:2 + nb], refs[2 + nb:2 + 2 * nb], refs[2 + 2 * nb:2 + 3 * nb]
        cos_ref, sin_ref, gq_ref, gk_ref, seg_ref, dz_ref, dgq_ref, dgk_ref, scr_ref = refs[2 + 3 * nb:]

        @pl.when(pl.program_id(0) == 0)
        def _():
            dgq_ref[...] = jnp.zeros_like(dgq_ref)
            dgk_ref[...] = jnp.zeros_like(dgk_ref)

        def total(d_refs):
            return sum(_from_subseq(r_[...], scr_ref, dil) for r_, dil in zip(d_refs, DILATIONS))

        def back(x, d_refs, g, dg_ref):
            dout = total(d_refs)
            dy = dout * cos_ref[...] + _swap_halves(dout * sin_ref[...])
            r = lax.rsqrt(_head_sum(x * x, seg_ref[...]) * (1.0 / HEAD_DIM) + EPS)
            xn = x * r
            dg_ref[...] += _colsum(dy * xn)
            dxn = dy * g
            return r * (dxn - xn * (_head_sum(dxn * xn, seg_ref[...]) * (1.0 / HEAD_DIM)))

        dz_ref[:, 0:B_WIDTH] = back(q_ref[...].astype(F32), dq_refs, gq_ref[...], dgq_ref).astype(BF16)
        dz_ref[:, B_WIDTH:2 * B_WIDTH] = back(k_ref[...].astype(F32), dk_refs, gk_ref[...], dgk_ref).astype(BF16)
        dz_ref[:, 2 * B_WIDTH:3 * B_WIDTH] = total(dv_refs).astype(BF16)

    return pl.pallas_call(
        body, name=name, grid=(t // tm,),
        in_specs=[col(2), col(3)] + subs * 3 + [blk, blk, row, row, pl.BlockSpec((B_WIDTH, B_WIDTH), lambda i: (0, 0))],
        out_specs=[pl.BlockSpec((tm, 3 * B_WIDTH), lambda i: (i, 0)), row, row],
        out_shape=[jax.ShapeDtypeStruct((t, 3 * B_WIDTH), BF16), jax.ShapeDtypeStruct((1, B_WIDTH), F32),
                   jax.ShapeDtypeStruct((1, B_WIDTH), F32)],
        scratch_shapes=[_subseq_scratch(tm)], compiler_params=_params("arbitrary"),
    )(z, z, *dqs, *dks, *dvs, cos_t, sin_t, gq, gk, seg)


def _subseq_views(x, col, name):
    t = x.shape[0]
    tm = _tile(t, (SUBSEQ_TM,))

    def body(x_ref, *rest):
        out_refs, scr_ref = rest[:-1], rest[-1]
        val = x_ref[...].astype(F32)
        for o_ref, dil in zip(out_refs, DILATIONS):
            o_ref[...] = _to_subseq(val, scr_ref, dil).astype(o_ref.dtype)

    return pl.pallas_call(
        body, name=name, grid=(t // tm,), in_specs=[pl.BlockSpec((tm, B_WIDTH), lambda i: (i, col))],
        out_specs=[_subseq_spec(tm, dil) for dil in DILATIONS],
        out_shape=[jax.ShapeDtypeStruct(_subseq_shape(t, dil), x.dtype) for dil in DILATIONS],
        scratch_shapes=[_subseq_scratch(tm)], compiler_params=_params("parallel"),
    )(x)


ATTN_FWD_BLOCKS = 1
ATTN_BWD_BLOCKS = 2


def _attn_step_specs(nb, want):
    ns = want if nb % want == 0 else 1
    cur = pl.BlockSpec((ns * Q_BLOCK, B_WIDTH), lambda r, i: (i, r))
    prev = pl.BlockSpec((Q_BLOCK, B_WIDTH), lambda r, i: (jnp.maximum(ns * i - 1, 0), r))
    return ns, cur, prev


def _attn_fwd(q, k, v, dil, name):
    t = q.shape[0] * dil
    nb = t // dil // Q_BLOCK
    ns, cur, prev = _attn_step_specs(nb, ATTN_FWD_BLOCKS)

    def body(q_ref, kp_ref, kc_ref, vp_ref, vc_ref, o_ref, lse_ref):
        i = pl.program_id(1)
        a = lax.broadcasted_iota(jnp.int32, (Q_BLOCK, 2 * Q_BLOCK), 0)
        j = lax.broadcasted_iota(jnp.int32, (Q_BLOCK, 2 * Q_BLOCK), 1)
        dist = a + Q_BLOCK - j
        band = (dist >= 0) & (dist <= Q_BLOCK)
        sls = [slice(h * HEAD_DIM, (h + 1) * HEAD_DIM) for h in range(HEADS)]
        for sb in range(ns):
            rows = slice(sb * Q_BLOCK, (sb + 1) * Q_BLOCK)
            before = slice((sb - 1) * Q_BLOCK, sb * Q_BLOCK)
            q = q_ref[rows, :]
            kk = jnp.concatenate([kp_ref[...] if sb == 0 else kc_ref[before, :], kc_ref[rows, :]], axis=0)
            vv = jnp.concatenate([vp_ref[...] if sb == 0 else vc_ref[before, :], vc_ref[rows, :]], axis=0)
            mask = band & ((j >= Q_BLOCK) | (i > 0)) if sb == 0 else band
            scores = [_dot(q[:, sl], kk[:, sl], _NT) for sl in sls]
            ps, dens = [], []
            for sl, s in zip(sls, scores):
                s = jnp.where(mask, s * (HEAD_DIM ** -0.5), NEG)
                m = jnp.max(s, axis=-1, keepdims=True)
                p = jnp.exp(s - m)
                den = jnp.sum(p, axis=-1, keepdims=True)
                ps.append(p.astype(BF16))
                dens.append(den)
                lse_ref[rows, sl] = jnp.broadcast_to(m + jnp.log(den), (Q_BLOCK, HEAD_DIM))
            for sl, p, den in zip(sls, ps, dens):
                o_ref[rows, sl] = _dot(p, vv[:, sl], _NN) / den

    return pl.pallas_call(
        body, name=name, grid=(dil, nb // ns), in_specs=[cur, prev, cur, prev, cur], out_specs=[cur, cur],
        out_shape=[jax.ShapeDtypeStruct(_subseq_shape(t, dil), F32)] * 2,
        compiler_params=_params("parallel", "parallel"),
    )(q, k, k, v, v)


def _attn_merge(outs, lses, cat, name):
    nb = len(DILATIONS)
    t = cat.shape[0]
    tm = _tile(t, (SUBSEQ_TM,))
    subs = [_subseq_spec(tm, dil) for dil in DILATIONS]

    def body(*refs):
        o_refs, l_refs = refs[:nb], refs[nb:2 * nb]
        yb_refs, lse_refs, cat_ref, scr_ref = refs[2 * nb + 1:3 * nb + 1], refs[3 * nb + 1:4 * nb + 1], refs[4 * nb + 1], refs[4 * nb + 2]
        ls = [_from_subseq(r[...], scr_ref, dil) for r, dil in zip(l_refs, DILATIONS)]
        m = functools.reduce(jnp.maximum, ls)
        tot = m + jnp.log(sum(jnp.exp(l - m) for l in ls))
        yb = sum(jnp.exp(l - tot) * _from_subseq(o[...], scr_ref, dil) for l, o, dil in zip(ls, o_refs, DILATIONS))
        cat_ref[...] = yb.astype(BF16)
        yb = yb.astype(BF16).astype(F32)
        for yb_ref, lse_ref, dil in zip(yb_refs, lse_refs, DILATIONS):
            yb_ref[...] = _to_subseq(yb, scr_ref, dil).astype(BF16)
            lse_ref[...] = _to_subseq(tot, scr_ref, dil)

    outs_ = pl.pallas_call(
        body, name=name, grid=(t // tm,), in_specs=subs * 2 + [pl.BlockSpec(memory_space=pl.ANY)],
        out_specs=subs * 2 + [pl.BlockSpec((tm, B_WIDTH), lambda i: (i, A_WIDTH // B_WIDTH))],
        out_shape=[jax.ShapeDtypeStruct(_subseq_shape(t, dil), BF16) for dil in DILATIONS]
        + [jax.ShapeDtypeStruct(_subseq_shape(t, dil), F32) for dil in DILATIONS] + [jax.ShapeDtypeStruct(cat.shape, BF16)],
        input_output_aliases={2 * nb: 2 * nb}, scratch_shapes=[_subseq_scratch(tm)], compiler_params=_params("parallel"),
    )(*outs, *lses, cat)
    return outs_[:nb], outs_[nb:2 * nb], outs_[2 * nb]


def _attn_bwd(q, k, v, do, o, lse, dil, name):
    t = q.shape[0] * dil
    nb = t // dil // Q_BLOCK
    ns, cur, prev = _attn_step_specs(nb, ATTN_BWD_BLOCKS)
    scale = HEAD_DIM ** -0.5

    def body(q_ref, kp_ref, kc_ref, vp_ref, vc_ref, do_ref, o_ref, lse_ref, dq_ref, dk_ref, dv_ref,
             ck_ref, cv_ref, tk_ref, tv_ref):
        step = pl.program_id(1)

        @pl.when(step == 0)
        def _():
            ck_ref[...] = jnp.zeros_like(ck_ref)
            cv_ref[...] = jnp.zeros_like(cv_ref)

        a = lax.broadcasted_iota(jnp.int32, (Q_BLOCK, 2 * Q_BLOCK), 0)
        j = lax.broadcasted_iota(jnp.int32, (Q_BLOCK, 2 * Q_BLOCK), 1)
        dist = a + Q_BLOCK - j
        band = (dist >= 0) & (dist <= Q_BLOCK)
        sls = [slice(h * HEAD_DIM, (h + 1) * HEAD_DIM) for h in range(HEADS)]
        for sb in range(ns):
            i = ns * step + sb
            rows = slice(sb * Q_BLOCK, (sb + 1) * Q_BLOCK)
            before = slice((sb - 1) * Q_BLOCK, sb * Q_BLOCK)
            q = q_ref[rows, :]
            kk = jnp.concatenate([kp_ref[...] if sb == 0 else kc_ref[before, :], kc_ref[rows, :]], axis=0)
            vv = jnp.concatenate([vp_ref[...] if sb == 0 else vc_ref[before, :], vc_ref[rows, :]], axis=0)
            do = do_ref[rows, :]
            dof = do.astype(F32)
            of = o_ref[rows, :].astype(F32)
            mask = band & ((j >= Q_BLOCK) | (step > 0)) if sb == 0 else band
            scores = [_dot(q[:, sl], kk[:, sl], _NT) for sl in sls]
            dps = [_dot(do[:, sl], vv[:, sl], _NT) for sl in sls]
            ps, dss = [], []
            for sl, s, dp in zip(sls, scores, dps):
                p = jnp.exp(jnp.where(mask, s * scale, NEG) - lse_ref[rows, sl.start:sl.start + 1])
                delta = jnp.sum(dof[:, sl] * of[:, sl], axis=-1, keepdims=True)
                dss.append((p * (dp - delta) * scale).astype(BF16))
                ps.append(p.astype(BF16))
            for sl, p, ds in zip(sls, ps, dss):
                dq_ref[rows, sl] = _dot(ds, kk[:, sl], _NN)
                dv_t = _dot(do[:, sl], p, _TN)
                dk_t = _dot(q[:, sl], ds, _TN)
                tk_ref[sl, :] = ck_ref[sl, :] + dk_t[:, :Q_BLOCK]
                tv_ref[sl, :] = cv_ref[sl, :] + dv_t[:, :Q_BLOCK]
                ck_ref[sl, :] = dk_t[:, Q_BLOCK:]
                cv_ref[sl, :] = dv_t[:, Q_BLOCK:]

            @pl.when(i >= 1)
            def _():
                done = pl.ds(pl.multiple_of((i - 1) * Q_BLOCK, Q_BLOCK), Q_BLOCK)
                dk_ref[done, :] = tk_ref[...].T
                dv_ref[done, :] = tv_ref[...].T

        @pl.when(step == nb // ns - 1)
        def _():
            done = pl.ds((nb - 1) * Q_BLOCK, Q_BLOCK)
            dk_ref[done, :] = ck_ref[...].T
            dv_ref[done, :] = cv_ref[...].T

    whole = pl.BlockSpec((t // dil, B_WIDTH), lambda r, i: (0, r))
    return pl.pallas_call(
        body, name=name, grid=(dil, nb // ns), in_specs=[cur, prev, cur, prev, cur, cur, cur, cur],
        out_specs=[cur, whole, whole], out_shape=[jax.ShapeDtypeStruct(_subseq_shape(t, dil), F32)] * 3,
        scratch_shapes=[pltpu.VMEM((B_WIDTH, Q_BLOCK), F32)] * 4,
        compiler_params=_params("parallel", "arbitrary"),
    )(q, k, k, v, v, do, o, lse)


FFN_TN = 256
FFN_ACT_TM = (4096, 2048, 1024, 512, 256, 128)
FFN_FWD_CHUNK = 256
FFN_BWD_CHUNK = 128


def _ffn_up(h, up_t, name):
    t, k = h.shape
    tm = _tile(t)

    def body(h_ref, w_ref, o_ref):
        o_ref[...] = _dot(h_ref[...], w_ref[...], _NT).astype(BF16)

    return pl.pallas_call(
        body, name=name, grid=(2, t // tm),
        in_specs=[pl.BlockSpec((tm, k), lambda p, i: (i, 0)), pl.BlockSpec((None, FFN_DIM, k), lambda p, i: (p, 0, 0))],
        out_specs=pl.BlockSpec((None, tm, FFN_DIM), lambda p, i: (p, i, 0)),
        out_shape=jax.ShapeDtypeStruct((2, t, FFN_DIM), BF16), compiler_params=_params("parallel", "parallel"),
    )(h, up_t.reshape(2, FFN_DIM, k))


def _ffn_up_dx(du, up_t, name):
    t = du.shape[1]
    k = up_t.shape[1]
    tm = _tile(t)

    def body(a_ref, b_ref, o_ref):
        o_ref[...] = _dot(a_ref[0], b_ref[0], _NN) + _dot(a_ref[1], b_ref[1], _NN)

    return pl.pallas_call(
        body, name=name, grid=(t // tm,),
        in_specs=[pl.BlockSpec((2, tm, FFN_DIM), lambda i: (0, i, 0)), pl.BlockSpec((2, FFN_DIM, k), lambda i: (0, 0, 0))],
        out_specs=pl.BlockSpec((tm, k), lambda i: (i, 0)), out_shape=jax.ShapeDtypeStruct((t, k), F32),
        compiler_params=_params("parallel"),
    )(du, up_t.reshape(2, FFN_DIM, k))


def _ffn_conv(win, w_ref, b_ref, p):
    x = win.astype(F32)
    x0, x1, x2 = x[FFN_HALO:], pltpu.roll(x, 1, 0)[FFN_HALO:], pltpu.roll(x, 2, 0)[FFN_HALO:]
    return b_ref[p] + w_ref[p, 2:3, :] * x0 + w_ref[p, 1:2, :] * x1 + w_ref[p, 0:1, :] * x2


def _zero_if(cond, v):
    return jnp.where(cond, 0, v).astype(v.dtype)


def _ffn_act(u, dw_w, dw_b, name):
    t = u.shape[1]
    tm = _tile(t, FFN_ACT_TM)
    chunk = min(FFN_FWD_CHUNK, tm)
    hb = tm // FFN_HALO
    main = pl.BlockSpec((2, tm, FFN_TN), lambda i, j: (0, i, j))
    halo = pl.BlockSpec((2, FFN_HALO, FFN_TN), lambda i, j: (0, jnp.maximum(i * hb - 1, 0), j))
    wsp = pl.BlockSpec((2, FFN_CONV_WIDTH, FFN_TN), lambda i, j: (0, 0, j))
    bsp = pl.BlockSpec((2, 1, FFN_TN), lambda i, j: (0, 0, j))

    def body(u_ref, uh_ref, w_ref, b_ref, o_ref, z_ref):
        first = pl.program_id(0) == 0

        def emit(rows, wins):
            za, zb = _ffn_conv(wins[0], w_ref, b_ref, 0), _ffn_conv(wins[1], w_ref, b_ref, 1)
            o_ref[rows, :] = (za * _sigmoid(za) * zb).astype(BF16)
            z_ref[0, rows, :] = za.astype(BF16)
            z_ref[1, rows, :] = zb.astype(BF16)

        emit(pl.ds(0, chunk), [jnp.concatenate([_zero_if(first, uh_ref[p]), u_ref[p, 0:chunk, :]], axis=0) for p in range(2)])

        def step(c, carry):
            s = pl.multiple_of(c * chunk, chunk)
            emit(pl.ds(s, chunk), [u_ref[p, pl.ds(s - FFN_HALO, chunk + FFN_HALO), :] for p in range(2)])
            return carry

        lax.fori_loop(1, tm // chunk, step, 0)

    return pl.pallas_call(
        body, name=name, grid=(t // tm, FFN_DIM // FFN_TN), in_specs=[main, halo, wsp, bsp],
        out_specs=[pl.BlockSpec((tm, FFN_TN), lambda i, j: (i, j)), main],
        out_shape=[jax.ShapeDtypeStruct((t, FFN_DIM), BF16), jax.ShapeDtypeStruct((2, t, FFN_DIM), BF16)],
        compiler_params=_params("parallel", "parallel"),
    )(u, u, dw_w, dw_b)


def _fold8(v):
    return jnp.sum(v.reshape(v.shape[0] // 8, 8, v.shape[1]), axis=0)


def _ffn_act_bwd(u, z, dact, dw_w, name):
    t = u.shape[1]
    tm = _tile(t, FFN_ACT_TM)
    chunk = min(FFN_BWD_CHUNK, tm // 2)
    halo = FFN_HALO
    hb = tm // halo
    nt = t // tm
    last_halo = t // halo - 1
    next_i = lambda i: jnp.minimum((i + 1) * hb, last_halo)
    main = pl.BlockSpec((2, tm, FFN_TN), lambda j, i: (0, i, j))
    nxt = pl.BlockSpec((2, halo, FFN_TN), lambda j, i: (0, next_i(i), j))
    wsp = pl.BlockSpec((2, FFN_CONV_WIDTH, FFN_TN), lambda j, i: (0, 0, j))
    bsp = pl.BlockSpec((2, 1, FFN_TN), lambda j, i: (0, 0, j))

    def body(u_ref, z_ref, zn_ref, da_ref, dan_ref, w_ref, du_ref, dw_ref, db_ref, acc_ref):
        i = pl.program_id(1)
        last = i == nt - 1
        acc_ref[...] = jnp.zeros_like(acc_ref)

        def emit(rows, zs, dact):
            n = chunk + halo
            za, zb, dact = zs[0].astype(F32), zs[1].astype(F32), dact.astype(F32)
            sg = _sigmoid(za)
            dzs = (dact * zb * (sg * (1.0 + za * (1.0 - sg))), dact * (za * sg))
            for p, dz in enumerate(dzs):
                ahead = (dz[:chunk], pltpu.roll(dz, n - 1, 0)[:chunk], pltpu.roll(dz, n - 2, 0)[:chunk])
                um = u_ref[p, rows, :].astype(F32)
                acc_ref[p, FFN_CONV_WIDTH] += _fold8(ahead[0])
                du = None
                for j, dzj in enumerate(ahead):
                    k = FFN_CONV_WIDTH - 1 - j
                    acc_ref[p, k] += _fold8(dzj * um)
                    term = w_ref[p, k:k + 1, :] * dzj
                    du = term if du is None else du + term
                du_ref[p, rows, :] = du.astype(BF16)

        def step(c, carry):
            s = pl.multiple_of(c * chunk, chunk)
            emit(pl.ds(s, chunk), [z_ref[p, pl.ds(s, chunk + halo), :] for p in range(2)], da_ref[pl.ds(s, chunk + halo), :])
            return carry

        lax.fori_loop(0, tm // chunk - 1, step, 0)
        s = tm - chunk
        emit(pl.ds(s, chunk),
             [jnp.concatenate([z_ref[p, s:tm, :], zn_ref[p]], axis=0) for p in range(2)],
             jnp.concatenate([da_ref[s:tm, :], _zero_if(last, dan_ref[...])], axis=0))

        @pl.when(i == 0)
        def _():
            dw_ref[...] = jnp.zeros_like(dw_ref)
            db_ref[...] = jnp.zeros_like(db_ref)

        for p in range(2):
            for k in range(FFN_CONV_WIDTH):
                dw_ref[p, k:k + 1, :] += _colsum(acc_ref[p, k])
            db_ref[p] += _colsum(acc_ref[p, FFN_CONV_WIDTH])

    return pl.pallas_call(
        body, name=name, grid=(FFN_DIM // FFN_TN, nt),
        in_specs=[main, main, nxt, pl.BlockSpec((tm, FFN_TN), lambda j, i: (i, j)),
                  pl.BlockSpec((halo, FFN_TN), lambda j, i: (next_i(i), j)), wsp],
        out_specs=[main, wsp, bsp],
        out_shape=[jax.ShapeDtypeStruct((2, t, FFN_DIM), BF16), jax.ShapeDtypeStruct((2, FFN_CONV_WIDTH, FFN_DIM), F32),
                   jax.ShapeDtypeStruct((2, 1, FFN_DIM), F32)],
        scratch_shapes=[pltpu.VMEM((2, FFN_CONV_WIDTH + 1, 8, FFN_TN), F32)],
        compiler_params=_params("parallel", "arbitrary"),
    )(u, z, z, dact, dact, dw_w)


CONV_TM = 256
CONV_ROWS = 128
CONV_FWD_ROWS = 256
CONV_LANES = 128
CONV_NORM_ROWS = 32


def _glu_window(pa_ref, pah_ref, pg_ref, pgh_ref, scr_ref, first):
    ah, gh = pah_ref[...].astype(F32), pgh_ref[...].astype(F32)
    scr_ref[0:CONV_HALO, :] = jnp.where(first, 0.0, ah * _sigmoid(gh))
    scr_ref[CONV_HALO:, :] = pa_ref[...].astype(F32) * _sigmoid(pg_ref[...].astype(F32))


def _tap_slabs(win, rows, ahead):
    n = win.shape[0]
    for s in range(8):
        ws = win if s == 0 else pltpu.roll(win, n - s if ahead else s, 0)
        for q in range(CONV_HALO // 8):
            o = 8 * q + s
            if o < CONV_WIDTH:
                start = 8 * q if ahead else CONV_HALO - 8 * q
                yield CONV_WIDTH - 1 - o, ws[start:start + rows]


def _conformer_specs(t):
    tm = _tile(t, (CONV_TM, 128))
    hb = tm // CONV_HALO
    d = D_MODEL
    main = lambda c: pl.BlockSpec((tm, d), lambda i: (i, c))
    halo = lambda c: pl.BlockSpec((CONV_HALO, d), lambda i: (jnp.maximum(i * hb - 1, 0), c))
    row = pl.BlockSpec((1, d), lambda i: (0, 0))
    wsp = pl.BlockSpec((CONV_WIDTH, d), lambda i: (0, 0))
    return tm, main, halo, row, wsp


def _conformer_mid(p, dw_w, dw_b, ln_g, ln_b, name):
    t = p.shape[0]
    tm, main, halo, row, wsp = _conformer_specs(t)
    d, lanes, rows = D_MODEL, CONV_LANES, min(CONV_FWD_ROWS, tm)

    def body(pa_ref, pah_ref, pg_ref, pgh_ref, w_ref, b_ref, g_ref, lb_ref, o_ref, dc_ref, scr_ref):
        _glu_window(pa_ref, pah_ref, pg_ref, pgh_ref, scr_ref, pl.program_id(0) == 0)
        for c in range(d // lanes):
            ls = slice(c * lanes, (c + 1) * lanes)

            def taps(r, carry, ls=ls):
                r0 = pl.multiple_of(r * rows, rows)
                acc = jnp.broadcast_to(b_ref[:, ls], (rows, lanes))
                for k, slab in _tap_slabs(scr_ref[pl.ds(r0, rows + CONV_HALO), ls], rows, False):
                    acc = acc + w_ref[k:k + 1, ls] * slab
                dc_ref[pl.ds(r0, rows), ls] = acc
                return carry

            lax.fori_loop(0, tm // rows, taps, 0)

        def norm(r, carry):
            r0 = pl.multiple_of(r * CONV_NORM_ROWS, CONV_NORM_ROWS)
            dc = dc_ref[pl.ds(r0, CONV_NORM_ROWS), :]
            xc = dc - jnp.mean(dc, axis=-1, keepdims=True)
            ln = xc * lax.rsqrt(jnp.mean(xc * xc, axis=-1, keepdims=True) + EPS) * g_ref[...] + lb_ref[...]
            o_ref[pl.ds(r0, CONV_NORM_ROWS), :] = (ln * _sigmoid(ln)).astype(BF16)
            return carry

        lax.fori_loop(0, tm // CONV_NORM_ROWS,norm, 0)

    return pl.pallas_call(
        body, name=name, grid=(t // tm,), in_specs=[main(0), halo(0), main(1), halo(1), wsp, row, row, row],
        out_specs=[main(0), main(0)], out_shape=[jax.ShapeDtypeStruct((t, d), BF16), jax.ShapeDtypeStruct((t, d), F32)],
        scratch_shapes=[pltpu.VMEM((tm + CONV_HALO, d), F32)], compiler_params=_params("parallel"),
    )(p, p, p, p, dw_w, dw_b, ln_g, ln_b)


def _conformer_mid_bwd(p, dc, ds, ln_g, ln_b, name):
    t = p.shape[0]
    tm, main, halo, row, wsp = _conformer_specs(t)
    d, nt = D_MODEL, t // tm
    rows, lanes = CONV_ROWS, CONV_LANES

    def body(pa_ref, pah_ref, pg_ref, pgh_ref, dc_ref, ds_ref, g_ref, lb_ref,
             ddc_ref, dw_ref, db_ref, dg_ref, dlb_ref, scr_ref, wacc_ref, racc_ref):
        i = pl.program_id(0)

        @pl.when(i == 0)
        def _():
            wacc_ref[...] = jnp.zeros_like(wacc_ref)
            racc_ref[...] = jnp.zeros_like(racc_ref)

        _glu_window(pa_ref, pah_ref, pg_ref, pgh_ref, scr_ref, i == 0)

        def norm_bwd(r, carry):
            r0 = pl.multiple_of(r * CONV_NORM_ROWS, CONV_NORM_ROWS)
            dcv = dc_ref[pl.ds(r0, CONV_NORM_ROWS), :]
            xc = dcv - jnp.mean(dcv, axis=-1, keepdims=True)
            rstd = lax.rsqrt(jnp.mean(xc * xc, axis=-1, keepdims=True) + EPS)
            xhat = xc * rstd
            ln = xhat * g_ref[...] + lb_ref[...]
            sg = _sigmoid(ln)
            dln = ds_ref[pl.ds(r0, CONV_NORM_ROWS), :].astype(F32) * (sg * (1.0 + ln * (1.0 - sg)))
            dxh = dln * g_ref[...]
            ddc = rstd * (dxh - jnp.mean(dxh, axis=-1, keepdims=True) - xhat * jnp.mean(dxh * xhat, axis=-1, keepdims=True))
            ddc_ref[pl.ds(r0, CONV_NORM_ROWS), :] = ddc
            racc_ref[0] += _fold8(dln * xhat)
            racc_ref[1] += _fold8(dln)
            racc_ref[2] += _fold8(ddc)
            return carry

        lax.fori_loop(0, tm // CONV_NORM_ROWS,norm_bwd, 0)

        for c in range(d // lanes):
            ls = slice(c * lanes, (c + 1) * lanes)

            def taps(r, carry, ls=ls):
                r0 = pl.multiple_of(r * rows, rows)
                ddc = ddc_ref[pl.ds(r0, rows), ls]
                for k, slab in _tap_slabs(scr_ref[pl.ds(r0, rows + CONV_HALO), ls], rows, False):
                    wacc_ref[k, :, ls] += _fold8(ddc * slab)
                return carry

            lax.fori_loop(0, tm // rows, taps, 0)

        @pl.when(i == nt - 1)
        def _():
            for k in range(CONV_WIDTH):
                dw_ref[k:k + 1, :] = _colsum(wacc_ref[k])
            dg_ref[...] = _colsum(racc_ref[0])
            dlb_ref[...] = _colsum(racc_ref[1])
            db_ref[...] = _colsum(racc_ref[2])

    return pl.pallas_call(
        body, name=name, grid=(nt,), in_specs=[main(0), halo(0), main(1), halo(1), main(0), main(0), row, row],
        out_specs=[main(0), wsp, row, row, row],
        out_shape=[jax.ShapeDtypeStruct((t, d), F32), jax.ShapeDtypeStruct((CONV_WIDTH, d), F32)]
        + [jax.ShapeDtypeStruct((1, d), F32)] * 3,
        scratch_shapes=[pltpu.VMEM((tm + CONV_HALO, d), F32), pltpu.VMEM((CONV_WIDTH, 8, d), F32), pltpu.VMEM((3, 8, d), F32)],
        compiler_params=_params("arbitrary"),
    )(p, p, p, p, dc, ds, ln_g, ln_b)


def _conformer_glu_bwd(p, ddc, dw_w, name):
    t = p.shape[0]
    d = D_MODEL
    tm = _tile(t, (CONV_TM, 128))
    hb = tm // CONV_HALO
    nt = t // tm
    last_halo = t // CONV_HALO - 1
    rows, lanes = CONV_ROWS, CONV_LANES
    col = lambda c: pl.BlockSpec((tm, d), lambda i: (i, c))
    nxt = pl.BlockSpec((CONV_HALO, d), lambda i: (jnp.minimum((i + 1) * hb, last_halo), 0))

    def body(pa_ref, pg_ref, ddc_ref, ddcn_ref, w_ref, dp_ref, db_ref, scr_ref, acc_ref):
        i = pl.program_id(0)

        @pl.when(i == 0)
        def _():
            acc_ref[...] = jnp.zeros_like(acc_ref)

        scr_ref[0:tm, :] = ddc_ref[...]
        scr_ref[tm:, :] = _zero_if(i == nt - 1, ddcn_ref[...])
        for c in range(d // lanes):
            ls = slice(c * lanes, (c + 1) * lanes)
            gs = slice(d + c * lanes, d + (c + 1) * lanes)

            def taps(r, carry, ls=ls, gs=gs):
                r0 = pl.multiple_of(r * rows, rows)
                dglu = None
                for k, slab in _tap_slabs(scr_ref[pl.ds(r0, rows + CONV_HALO), ls], rows, True):
                    term = w_ref[k:k + 1, ls] * slab
                    dglu = term if dglu is None else dglu + term
                a = pa_ref[pl.ds(r0, rows), ls].astype(F32)
                sg = _sigmoid(pg_ref[pl.ds(r0, rows), ls].astype(F32))
                da = (dglu * sg).astype(BF16)
                dg = (dglu * a * sg * (1.0 - sg)).astype(BF16)
                dp_ref[pl.ds(r0, rows), ls] = da
                dp_ref[pl.ds(r0, rows), gs] = dg
                acc_ref[:, ls] += _fold8(da.astype(F32))
                acc_ref[:, gs] += _fold8(dg.astype(F32))
                return carry

            lax.fori_loop(0, tm // rows, taps, 0)

        @pl.when(i == nt - 1)
        def _():
            db_ref[...] = _colsum(acc_ref[...])

    return pl.pallas_call(
        body, name=name, grid=(nt,),
        in_specs=[col(0), col(1), col(0), nxt, pl.BlockSpec((CONV_WIDTH, d), lambda i: (0, 0))],
        out_specs=[pl.BlockSpec((tm, 2 * d), lambda i: (i, 0)), pl.BlockSpec((1, 2 * d), lambda i: (0, 0))],
        out_shape=[jax.ShapeDtypeStruct((t, 2 * d), BF16), jax.ShapeDtypeStruct((1, 2 * d), F32)],
        scratch_shapes=[pltpu.VMEM((tm + CONV_HALO, d), F32), pltpu.VMEM((8, 2 * d), F32)],
        compiler_params=_params("arbitrary"),
    )(p, p, ddc, ddc, dw_w)


def _colsum_call(a, name):
    t, n = a.shape
    tm = _tile(t)

    def body(a_ref, o_ref):
        @pl.when(pl.program_id(0) == 0)
        def _():
            o_ref[...] = jnp.zeros_like(o_ref)

        o_ref[...] += _colsum(a_ref[...].astype(F32))

    return pl.pallas_call(
        body, name=name, grid=(t // tm,), in_specs=[pl.BlockSpec((tm, n), lambda i: (i, 0))],
        out_specs=pl.BlockSpec((1, n), lambda i: (0, 0)), out_shape=jax.ShapeDtypeStruct((1, n), F32),
        compiler_params=_params("arbitrary"),
    )(a)


def _ada_fwd(c_all, w, name):
    rows, d = c_all.shape
    n = w.shape[1]
    tn = _tile(n, (256, 128))

    def body(c_ref, w_ref, o_ref):
        c = c_ref[...]
        o_ref[...] = _dot((c * _sigmoid(c)).astype(BF16), w_ref[...].astype(BF16), _NN)

    return pl.pallas_call(
        body, name=name, grid=(n // tn,),
        in_specs=[pl.BlockSpec((rows, d), lambda j: (0, 0)), pl.BlockSpec((d, tn), lambda j: (0, j))],
        out_specs=pl.BlockSpec((rows, tn), lambda j: (0, j)), out_shape=jax.ShapeDtypeStruct((rows, n), F32),
        compiler_params=_params("parallel"),
    )(c_all, w)


def _ada_bwd(c_all, dmod, name):
    rows, d = c_all.shape
    n = dmod.shape[1]
    tn = _tile(n, (256, 128))

    def body(c_ref, g_ref, o_ref):
        c = c_ref[...]
        o_ref[...] = _dot((c * _sigmoid(c)).astype(BF16), g_ref[...].astype(BF16), _TN)

    return pl.pallas_call(
        body, name=name, grid=(n // tn,),
        in_specs=[pl.BlockSpec((rows, d), lambda j: (0, 0)), pl.BlockSpec((rows, tn), lambda j: (0, j))],
        out_specs=pl.BlockSpec((d, tn), lambda j: (0, j)), out_shape=jax.ShapeDtypeStruct((d, n), F32),
        compiler_params=_params("parallel"),
    )(c_all, dmod)


def _sum_in_device_order(own, land, me, name):
    s, r, c = land.shape
    tr = _row_tile(r, 256)
    slot = lambda k: pl.BlockSpec((None, tr, c), lambda i, me_ref: (jnp.where(me_ref[0] == k, (k + 1) % s, k), i, 0))
    own_spec = pl.BlockSpec((tr, c), lambda i, me_ref: (i, 0))

    def body(me_ref, own_ref, *refs):
        o_ref = refs[-1]
        acc = None
        for k, ref in enumerate(refs[:-1]):
            term = jnp.where(me_ref[0] == k, own_ref[...], ref[...]).astype(F32)
            acc = term if acc is None else acc + term
        o_ref[...] = acc

    return pl.pallas_call(
        body, name=name, out_shape=jax.ShapeDtypeStruct((r, c), F32),
        grid_spec=pltpu.PrefetchScalarGridSpec(
            num_scalar_prefetch=1, grid=(r // tr,), in_specs=[own_spec] + [slot(k) for k in range(s)], out_specs=own_spec),
        compiler_params=_params("parallel"),
    )(me, own, *[land] * s)


def _sum_with_own(blocks, land, me, name, layer=0, layers=1, into=None):
    s, r, c = land.shape
    tr = _row_tile(r, 256)
    slot = lambda k: pl.BlockSpec((None, tr, c), lambda i, me_ref: ((me_ref[0] + k) % s, i, 0))
    stacked = [] if into is None else [into]

    def body(me_ref, own_ref, *refs):
        o_ref = refs[-1]
        acc = own_ref[...].astype(F32)
        for ref in refs[:s - 1]:
            acc = acc + ref[...].astype(F32)
        o_ref[...] = acc

    return pl.pallas_call(
        body, name=name, out_shape=jax.ShapeDtypeStruct((layers, r, c), F32),
        grid_spec=pltpu.PrefetchScalarGridSpec(
            num_scalar_prefetch=1, grid=(r // tr,),
            in_specs=[slot(k) for k in range(s)] + [pl.BlockSpec(memory_space=pl.ANY)] * len(stacked),
            out_specs=pl.BlockSpec((None, tr, c), lambda i, me_ref: (layer, i, 0))),
        input_output_aliases={s + 1: 0} if stacked else {},
        compiler_params=_params("parallel"),
    )(me, blocks, *[land] * (s - 1), *stacked)


def _adamw_update(w, g, m, v):
    nm = ADAM_B1 * m + (1.0 - ADAM_B1) * g
    nv = ADAM_B2 * v + (1.0 - ADAM_B2) * (g * g)
    m_hat = nm * (1.0 / (1.0 - ADAM_B1 ** ADAM_STEP))
    v_hat = nv * (1.0 / (1.0 - ADAM_B2 ** ADAM_STEP))
    return -ADAM_LR * (m_hat / (jnp.sqrt(v_hat) + ADAM_EPS) + ADAM_WD * w), nm, nv


def _adamw(w, g, m, v, behind, name):
    l, r, c = w.shape
    tr = _row_tile(r, 256)
    blk = pl.BlockSpec((None, tr, c), lambda k, i: (k, i, 0))
    order = [] if behind is None else [behind]

    def body(w_ref, g_ref, m_ref, v_ref, *rest):
        d_ref, nm_ref, nv_ref = rest[-3:]
        d_ref[...], nm_ref[...], nv_ref[...] = _adamw_update(w_ref[...], g_ref[...], m_ref[...], v_ref[...])

    return pl.pallas_call(
        body, name=name, grid=(l, r // tr), in_specs=[blk] * 4 + [pl.BlockSpec(memory_space=pl.ANY)] * len(order),
        out_specs=[blk] * 3, out_shape=[jax.ShapeDtypeStruct(w.shape, F32)] * 3,
        compiler_params=_params("parallel", "parallel"),
    )(w, g, m, v, *order)


def _adamw_small(ws, gs, ms, vs, name):
    n = len(ws)
    two_d = lambda a: a.reshape(-1, a.shape[-1])

    def body(*refs):
        ins, outs = refs[:4 * n], refs[4 * n:]
        for a in range(n):
            outs[a][...], outs[n + a][...], outs[2 * n + a][...] = _adamw_update(*[ins[k * n + a][...] for k in range(4)])

    res = pl.pallas_call(
        body, name=name, out_shape=[jax.ShapeDtypeStruct(two_d(w).shape, F32) for w in ws] * 3,
    )(*[two_d(a) for a in (*ws, *gs, *ms, *vs)])
    return [[res[k * n + a].reshape(ws[a].shape) for a in range(n)] for k in range(3)]


def _mesh_pos():
    return lax.axis_index("x"), lax.axis_index("y"), lax.axis_index("c")


def _all_gather_vmem(x_shard, name):
    m_per, n = x_shard.shape

    def body(x_ref, out_ref, send_sems, recv_sems, local_sem):
        x, y, c = _mesh_pos()
        me, sibling = (x, y, c), (x, y, 1 - c)
        chips = [(1 - x, y), (x, 1 - y), (1 - x, 1 - y)]

        def rows(px, py, pc):
            return out_ref.at[pl.ds((4 * px + 2 * py + pc) * m_per, m_per), :]

        def copy(k, block, to, src=None):
            return pltpu.make_async_remote_copy(
                src_ref=rows(*block) if src is None else src, dst_ref=rows(*block),
                send_sem=send_sems.at[k], recv_sem=recv_sems.at[k], device_id=to, device_id_type=MESH)

        mine = pltpu.make_async_copy(x_ref, rows(*me), local_sem)
        mine.start()
        first = [copy(0, me, sibling, src=x_ref)]
        first += [copy(1 + j, me, (*chip, c), src=x_ref) for j, chip in enumerate(chips)]
        for cp in first:
            cp.start()
        passed = [copy(4 + j, (*chip, c), sibling) for j, chip in enumerate(chips)]
        for j, chip in enumerate(chips):
            copy(1 + j, (*chip, c), me).wait_recv()
            passed[j].start()
        copy(0, sibling, me).wait_recv()
        for j, chip in enumerate(chips):
            copy(4 + j, (*chip, 1 - c), me).wait_recv()
        for cp in first + passed:
            cp.wait_send()
        mine.wait()

    return pl.pallas_call(
        body, name=name, out_shape=jax.ShapeDtypeStruct((N_DEV * m_per, n), x_shard.dtype),
        in_specs=[pl.BlockSpec(memory_space=pltpu.VMEM)], out_specs=pl.BlockSpec(memory_space=pltpu.VMEM),
        scratch_shapes=[pltpu.SemaphoreType.DMA((7,)), pltpu.SemaphoreType.DMA((7,)), pltpu.SemaphoreType.DMA],
    )(x_shard)


def _all_gather_hbm(shards, name):
    n = len(shards)
    out_shape = [jax.ShapeDtypeStruct((N_DEV,) + s.shape, s.dtype) for s in shards]

    def body(*refs):
        x_refs, out_refs = refs[:n], refs[n:2 * n]
        send_sems, recv_sems, local_sems = refs[2 * n:]
        x, y, c = _mesh_pos()
        me, sibling = (x, y, c), (x, y, 1 - c)
        chips = [(1 - x, y), (x, 1 - y), (1 - x, 1 - y)]

        def blk(a, p):
            return out_refs[a].at[4 * p[0] + 2 * p[1] + p[2]]

        def copy(a, k, block, to, src=None):
            return pltpu.make_async_remote_copy(
                src_ref=blk(a, block) if src is None else src, dst_ref=blk(a, block),
                send_sem=send_sems.at[7 * a + k], recv_sem=recv_sems.at[7 * a + k], device_id=to, device_id_type=MESH)

        mine = [pltpu.make_async_copy(x_refs[a], blk(a, me), local_sems.at[a]) for a in range(n)]
        for cp in mine:
            cp.start()
        first = []
        for a in range(n):
            first.append(copy(a, 0, me, sibling, src=x_refs[a]))
            first += [copy(a, 1 + j, me, (*chip, c), src=x_refs[a]) for j, chip in enumerate(chips)]
        for cp in first:
            cp.start()
        passed = []
        for j, chip in enumerate(chips):
            for a in range(n):
                copy(a, 1 + j, (*chip, c), me).wait_recv()
                fwd = copy(a, 4 + j, (*chip, c), sibling)
                fwd.start()
                passed.append(fwd)
        for a in range(n):
            copy(a, 0, sibling, me).wait_recv()
            for j, chip in enumerate(chips):
                copy(a, 4 + j, (*chip, 1 - c), me).wait_recv()
        for cp in first + passed:
            cp.wait_send()
        for cp in mine:
            cp.wait()

    return pl.pallas_call(
        body, name=name, out_shape=out_shape, in_specs=[pl.BlockSpec(memory_space=pltpu.VMEM)] * n,
        out_specs=[pl.BlockSpec(memory_space=pl.ANY)] * n,
        scratch_shapes=[pltpu.SemaphoreType.DMA((7 * n,)), pltpu.SemaphoreType.DMA((7 * n,)), pltpu.SemaphoreType.DMA((n,))],
    )(*shards)


def _peers(x, y, c):
    flip = lambda v, f: 1 - v if f else v
    return [(flip(x, m & 4), flip(y, m & 2), flip(c, m & 1)) for m in range(1, N_DEV)]


def _dev_index(p):
    return 4 * p[0] + 2 * p[1] + p[2]


def _push_copies(src_refs, land_refs, send_sems, recv_sems, scatter, receive):
    x, y, c = _mesh_pos()
    me = _dev_index((x, y, c))
    copies = []
    for a, (src, land) in enumerate(zip(src_refs, land_refs)):
        for k, p in enumerate(_peers(x, y, c)):
            copies.append(pltpu.make_async_remote_copy(
                src_ref=src.at[_dev_index(p)] if scatter else src, dst_ref=land.at[_dev_index(p) if receive else me],
                send_sem=send_sems.at[7 * a + k], recv_sem=recv_sems.at[7 * a + k], device_id=p, device_id_type=MESH))
    return copies


_HBM = pl.BlockSpec(memory_space=pltpu.HBM)
_SEM = pl.BlockSpec(memory_space=pltpu.SEMAPHORE)
_EFFECT = pltpu.SideEffectType.DATAFLOW_SIDE_EFFECTING


def _pushes_start(srcs, lands, scatter, name):
    n = len(srcs)

    def body(*refs):
        src_refs, land_refs = refs[:n], refs[n:2 * n]
        send_sems, recv_sems = refs[2 * n], refs[2 * n + 1]
        token = refs[-1]
        for cp in _push_copies(src_refs, land_refs, send_sems, recv_sems, scatter, receive=False):
            cp.start()
        token[...] = jnp.zeros_like(token)

    hbm = lambda a: pltpu.HBM(a.shape, a.dtype)
    sems = pltpu.SemaphoreType.DMA((7 * n,))
    outs = pl.pallas_call(
        body, name=name,
        out_shape=(sems, sems, *[hbm(a) for a in srcs], *[hbm(a) for a in lands], jax.ShapeDtypeStruct((8, 128), F32)),
        in_specs=[_HBM] * (2 * n), out_specs=(_SEM, _SEM, *[_HBM] * (2 * n), pl.BlockSpec(memory_space=pltpu.VMEM)),
        input_output_aliases={i: 2 + i for i in range(2 * n)},
        compiler_params=pltpu.CompilerParams(has_side_effects=_EFFECT),
    )(*[pltpu.with_memory_space_constraint(a, pltpu.HBM) for a in (*srcs, *lands)])
    return (outs[0], outs[1], outs[2:2 + n], outs[2 + n:2 + 2 * n], scatter), outs[-1]


def _pushes_wait(handle, after, name):
    send_sems, recv_sems, srcs, lands, scatter = handle
    n = len(srcs)
    after = after if isinstance(after, (tuple, list)) else (after,)

    def body(*refs):
        src_refs, land_refs = refs[:n], refs[n:2 * n]
        for cp in _push_copies(src_refs, land_refs, refs[2 * n], refs[2 * n + 1], scatter, receive=True):
            cp.wait_send()
            cp.wait_recv()

    hbm = lambda a: pltpu.HBM(a.shape, a.dtype)
    outs = pl.pallas_call(
        body, name=name, out_shape=tuple(hbm(a) for a in (*srcs, *lands)),
        in_specs=[_HBM] * (2 * n) + [_SEM, _SEM] + [pl.BlockSpec(memory_space=pl.ANY)] * len(after),
        out_specs=tuple([_HBM] * (2 * n)), input_output_aliases={i: i for i in range(2 * n)},
        compiler_params=pltpu.CompilerParams(has_side_effects=_EFFECT),
    )(*srcs, *lands, send_sems, recv_sems, *after)
    return outs[:n], outs[n:]


def _landing_zones(srcs, behind, name):
    n, nb = len(srcs), len(behind)

    def body(*refs):
        src_refs, land_refs, bufs, sems = refs[:n], refs[n + nb:2 * n + nb], refs[2 * n + nb:3 * n + nb], refs[3 * n + nb]
        me = _dev_index(_mesh_pos())
        load = [pltpu.make_async_copy(src, buf, sems.at[a]) for a, (src, buf) in enumerate(zip(src_refs, bufs))]
        store = [pltpu.make_async_copy(buf, land.at[me], sems.at[a]) for a, (buf, land) in enumerate(zip(bufs, land_refs))]
        for cp in load:
            cp.start()
        for ld, st in zip(load, store):
            ld.wait()
            st.start()
        for cp in store:
            cp.wait()

    any_spec = pl.BlockSpec(memory_space=pl.ANY)
    return pl.pallas_call(
        body, name=name, out_shape=[jax.ShapeDtypeStruct((N_DEV,) + s.shape, s.dtype) for s in srcs],
        in_specs=[any_spec] * (n + nb), out_specs=[any_spec] * n,
        scratch_shapes=[pltpu.VMEM(s.shape, s.dtype) for s in srcs] + [pltpu.SemaphoreType.DMA((n,))],
        compiler_params=pltpu.CompilerParams(vmem_limit_bytes=V7X_VMEM_LIMIT),
    )(*srcs, *behind)


def _ffn_forward(x, mod, norm_g, w, tag):
    sh, sc, gate = mod
    h = _modnorm(x, norm_g, sc, sh, f"{tag}_norm")
    u = _ffn_up(h, w["up_t"], f"{tag}_up")
    act, z = _ffn_act(u, w["dw_w"], w["dw_b"], f"{tag}_act")
    y, x_new = _matmul(act, w["down"], "nn", BF16, f"{tag}_down", resid=(x, gate))
    return x_new, (x, h, u, z, act, y)


def _behind(row, token):
    return row if token is None else row + token[0:1, 0:1]


def _ffn_backward(dx_new, dy, d_gate, saved, mod, norm_g, w, tag, emit, below):
    x, h, u, z, act, _ = saved
    _, sc, _ = mod
    d_down = _matmul_tn_acc(act, dy, f"{tag}_down_dw")
    dact = _matmul(dy, w["down"], "nt", BF16, f"{tag}_down_dx")
    du, d_dw_w, d_dw_b = _ffn_act_bwd(u, z, dact, w["dw_w"], f"{tag}_act_bwd")
    d_up_t = _matmul_tn_acc(du, h, f"{tag}_up_dw").reshape(2 * FFN_DIM, -1)
    token = emit([d_up_t, d_down])
    dh = _ffn_up_dx(du, w["up_t"], f"{tag}_up_dx")
    dx, d_w, d_sh, *dy_below = _modnorm_bwd(x, dh, norm_g, _behind(sc, token), dx_new, below, f"{tag}_norm_bwd")
    return (dx, *dy_below), dict(dw_w=d_dw_w.transpose(1, 0, 2).reshape(FFN_CONV_WIDTH, 2 * FFN_DIM),
                    dw_b=d_dw_b.reshape(1, 2 * FFN_DIM), norm_g=d_w * (1.0 + sc), sh=d_sh, sc=d_w * norm_g, gate=d_gate)


def _mixer_forward(x, mod, norm_g, w, rope, tag):
    sh, sc, gate = mod
    h = _modnorm(x, norm_g, sc, sh, f"{tag}_norm")
    z = _matmul(h, w["w_in_t"], "nt", BF16, f"{tag}_in")
    ya = _gmlp_fwd(z, w["gain"], w["wtril"], w["bias_exp"], f"{tag}_gmlp")
    q, k, v = _qk_prep(z, rope[0], rope[1], w["gq"], w["gk"], w["seg"], f"{tag}_qk")
    outs, lses = zip(*[_attn_fwd(q[b], k[b], v[b], dil, f"{tag}_attn_d{dil}") for b, dil in enumerate(DILATIONS)])
    yb, lse, cat = _attn_merge(outs, lses, ya, f"{tag}_merge")
    y, x_new = _matmul(cat, w["w_out"], "nn", BF16, f"{tag}_out", resid=(x, gate))
    return x_new, (x, h, z, q, k, v, yb, lse, cat, y)


def _mixer_backward(dx_new, dy, d_gate, saved, mod, norm_g, w, rope, tag, emit, below):
    x, h, z, q, k, v, yb, lse, cat, _ = saved
    _, sc, _ = mod
    d_w_out = _matmul_tn_acc(cat, dy, f"{tag}_out_dw")
    dcat = _matmul(dy, w["w_out"], "nt", BF16, f"{tag}_out_dx")
    dz_a, d_sp_w, d_gain, d_bias_exp = _gmlp_bwd(z, dcat, w["gain"], w["wtril"], w["wtril_t"], w["bias_exp"], f"{tag}_gmlp_bwd")
    dyb = _subseq_views(dcat, A_WIDTH // B_WIDTH, f"{tag}_dyb_views")
    dqs, dks, dvs = zip(*[_attn_bwd(q[b], k[b], v[b], dyb[b], yb[b], lse[b], dil, f"{tag}_attn_bwd_d{dil}")
                          for b, dil in enumerate(DILATIONS)])
    dz_qkv, d_gq, d_gk = _qk_prep_bwd(z, dqs, dks, dvs, rope[0], rope[1], w["gq"], w["gk"], w["seg"], f"{tag}_qk_bwd")
    dz = jnp.concatenate([dz_a, dz_qkv], axis=1)
    d_w_in_t = _matmul_tn_acc(dz, h, f"{tag}_in_dw")
    token = emit([d_w_in_t, d_w_out])
    dh = _matmul(dz, w["w_in_t"], "nn", F32, f"{tag}_in_dx")
    dx, d_w, d_sh, *dy_below = _modnorm_bwd(x, dh, norm_g, _behind(sc, token), dx_new, below, f"{tag}_norm_bwd")
    return (dx, *dy_below), dict(
        vnorm_g=d_gain.reshape(A_GROUPS, GROUP_DIM), spatial_w=d_sp_w,
        spatial_b=d_bias_exp.reshape(CHUNK, A_GROUPS, GROUP_DIM).sum(-1).T,
        q_norm_g=d_gq.reshape(HEADS, HEAD_DIM).sum(0), k_norm_g=d_gk.reshape(HEADS, HEAD_DIM).sum(0),
        norm_g=d_w * (1.0 + sc), sh=d_sh, sc=d_w * norm_g, gate=d_gate)


def _conformer_forward(x, mod, norm_g, w, tag):
    sh, sc, gate = mod
    h = _modnorm(x, norm_g, sc, sh, f"{tag}_norm")
    p = _matmul(h, w["pw1_t"], "nt", BF16, f"{tag}_pw1", bias=w["pw1_b"])
    s, dc = _conformer_mid(p, w["dw_w"], w["dw_b"], w["ln_g"], w["ln_b"], f"{tag}_mid")
    y, x_new = _matmul(s, w["pw2"], "nn", BF16, f"{tag}_pw2", bias=w["pw2_b"], resid=(x, gate))
    return x_new, (x, h, p, dc, s, y)


def _conformer_backward(dx_new, dy, d_gate, saved, mod, norm_g, w, tag, emit, below):
    x, h, p, dc, s, _ = saved
    _, sc, _ = mod
    d_pw2 = _matmul_tn_acc(s, dy, f"{tag}_pw2_dw")
    d_pw2_b = _colsum_call(dy, f"{tag}_pw2_db")
    ds = _matmul(dy, w["pw2"], "nt", BF16, f"{tag}_pw2_dx")
    ddc, d_dw_w, d_dw_b, d_ln_g, d_ln_b = _conformer_mid_bwd(p, dc, ds, w["ln_g"], w["ln_b"], f"{tag}_mid_bwd")
    dp, d_pw1_b = _conformer_glu_bwd(p, ddc, w["dw_w"], f"{tag}_glu_bwd")
    d_pw1_t = _matmul_tn_acc(dp, h, f"{tag}_pw1_dw")
    token = emit([d_pw1_t, d_pw2])
    dh = _matmul(dp, w["pw1_t"], "nn", F32, f"{tag}_pw1_dx")
    dx, d_w, d_sh, *dy_below = _modnorm_bwd(x, dh, norm_g, _behind(sc, token), dx_new, below, f"{tag}_norm_bwd")
    return (dx, *dy_below), dict(pw1_b=d_pw1_b, dw_w=d_dw_w, dw_b=d_dw_b, ln_g=d_ln_g, ln_b=d_ln_b, pw2_b=d_pw2_b, norm_g=d_w * (1.0 + sc), sh=d_sh, sc=d_w * norm_g, gate=d_gate)


def _local_step(x, target, pos, mod, norm_mix_g, norm_ffn_g, mixer_w, conv_w, ffn_w, fetch, emit):
    d = D_MODEL
    inv_freq = 1.0 / (ROPE_THETA ** (jnp.arange(0, HEAD_DIM, 2, dtype=F32) / HEAD_DIM))
    inv_freq = jnp.tile(inv_freq, 2 * HEADS)[None, :]
    sign = jnp.tile(jnp.concatenate([-jnp.ones(HEAD_DIM // 2, F32), jnp.ones(HEAD_DIM // 2, F32)]), HEADS)[None, :]
    rope = _rope_tables(pos, inv_freq, sign, "rope_tables")
    mods = [[mod[l:l + 1, i * d:(i + 1) * d] for i in range(6)] for l in range(2)]
    mix = [(m[0], m[1], m[2]) for m in mods]
    ffn = [(m[3], m[4], m[5]) for m in mods]
    gm = [norm_mix_g[l:l + 1] for l in range(2)]
    gf = [norm_ffn_g[l:l + 1] for l in range(2)]

    mixer_w = {**mixer_w, **fetch("l0_mix", x)}
    x1, s_mix = _mixer_forward(x, mix[0], gm[0], mixer_w, rope, "l0_mix")
    ffn_w0 = {**ffn_w[0], **fetch("l0_ffn", x1)}
    x2, s_ffn0 = _ffn_forward(x1, ffn[0], gf[0], ffn_w0, "l0_ffn")
    conv_w = {**conv_w, **fetch("l1_conv", x2)}
    x3, s_conv = _conformer_forward(x2, mix[1], gm[1], conv_w, "l1_conv")
    ffn_w1 = {**ffn_w[1], **fetch("l1_ffn", x3)}
    x4, s_ffn1 = _ffn_forward(x3, ffn[1], gf[1], ffn_w1, "l1_ffn")
    below = lambda saved, m: (saved[-1], m[2])
    dx, loss, dy, dg = _loss_head(x4, target, below(s_ffn1, ffn[1]), "loss_head")
    (dx, dy, dg), g_ffn1 = _ffn_backward(dx, dy, dg, s_ffn1, ffn[1], gf[1], ffn_w1, "l1_ffn",
                                         functools.partial(emit, "l1_ffn"), below(s_conv, mix[1]))
    (dx, dy, dg), g_conv = _conformer_backward(dx, dy, dg, s_conv, mix[1], gm[1], conv_w, "l1_conv",
                                               functools.partial(emit, "l1_conv"), below(s_ffn0, ffn[0]))
    (dx, dy, dg), g_ffn0 = _ffn_backward(dx, dy, dg, s_ffn0, ffn[0], gf[0], ffn_w0, "l0_ffn",
                                         functools.partial(emit, "l0_ffn"), below(s_mix, mix[0]))
    (dx,), g_mix = _mixer_backward(dx, dy, dg, s_mix, mix[0], gm[0], mixer_w, rope, "l0_mix",
                                   functools.partial(emit, "l0_mix"), None)
    blocks = [g_mix, g_ffn0, g_conv, g_ffn1]
    dmod = jnp.stack([jnp.concatenate([a["sh"], a["sc"], a["gate"], b["sh"], b["sc"], b["gate"]], axis=1)[0]
                      for a, b in ((g_mix, g_ffn0), (g_conv, g_ffn1))])
    return loss, dx, dmod, blocks


def _pack(arrs, rows=8):
    flat = jnp.concatenate([a.reshape(-1).astype(F32) for a in arrs])
    n = flat.shape[0]
    cols = -(-n // (rows * 128)) * 128
    return jnp.pad(flat, (0, rows * cols - n)).reshape(rows, cols)


def _unpack(flat, shapes):
    out, off = [], 0
    for shp in shapes:
        n = math.prod(shp)
        out.append(flat[..., off:off + n].reshape(flat.shape[:-1] + tuple(shp)))
        off += n
    return out


def _take_block(a, idx, size, axis):
    return lax.dynamic_slice_in_dim(a, idx * size, size, axis)


def kernel(x, c, positions, ada_w, ada_b, norm_mix_g, norm_ffn_g, ab_w_in, a_vnorm_g, a_spatial_w, a_spatial_b, b_q_norm_g, b_k_norm_g, ab_w_out, conv_pw1_w, conv_pw1_b, conv_dw_w, conv_dw_b, conv_ln_g, conv_ln_b, conv_pw2_w, conv_pw2_b, ffn_up_w, ffn_dw_w, ffn_dw_b, ffn_down_w, loss_target, m_ada_w, m_ada_b, m_norm_mix_g, m_norm_ffn_g, m_ab_w_in, m_a_vnorm_g, m_a_spatial_w, m_a_spatial_b, m_b_q_norm_g, m_b_k_norm_g, m_ab_w_out, m_conv_pw1_w, m_conv_pw1_b, m_conv_dw_w, m_conv_dw_b, m_conv_ln_g, m_conv_ln_b, m_conv_pw2_w, m_conv_pw2_b, m_ffn_up_w, m_ffn_dw_w, m_ffn_dw_b, m_ffn_down_w, v_ada_w, v_ada_b, v_norm_mix_g, v_norm_ffn_g, v_ab_w_in, v_a_vnorm_g, v_a_spatial_w, v_a_spatial_b, v_b_q_norm_g, v_b_k_norm_g, v_ab_w_out, v_conv_pw1_w, v_conv_pw1_b, v_conv_dw_w, v_conv_dw_b, v_conv_ln_g, v_conv_ln_b, v_conv_pw2_w, v_conv_pw2_b, v_ffn_up_w, v_ffn_dw_w, v_ffn_dw_b, v_ffn_down_w):
    weights = dict(ada_w=ada_w, ada_b=ada_b, norm_mix_g=norm_mix_g, norm_ffn_g=norm_ffn_g, ab_w_in=ab_w_in, a_vnorm_g=a_vnorm_g, a_spatial_w=a_spatial_w, a_spatial_b=a_spatial_b, b_q_norm_g=b_q_norm_g, b_k_norm_g=b_k_norm_g, ab_w_out=ab_w_out, conv_pw1_w=conv_pw1_w, conv_pw1_b=conv_pw1_b, conv_dw_w=conv_dw_w, conv_dw_b=conv_dw_b, conv_ln_g=conv_ln_g, conv_ln_b=conv_ln_b, conv_pw2_w=conv_pw2_w, conv_pw2_b=conv_pw2_b, ffn_up_w=ffn_up_w, ffn_dw_w=ffn_dw_w, ffn_dw_b=ffn_dw_b, ffn_down_w=ffn_down_w)
    mom1 = dict(ada_w=m_ada_w, ada_b=m_ada_b, norm_mix_g=m_norm_mix_g, norm_ffn_g=m_norm_ffn_g, ab_w_in=m_ab_w_in, a_vnorm_g=m_a_vnorm_g, a_spatial_w=m_a_spatial_w, a_spatial_b=m_a_spatial_b, b_q_norm_g=m_b_q_norm_g, b_k_norm_g=m_b_k_norm_g, ab_w_out=m_ab_w_out, conv_pw1_w=m_conv_pw1_w, conv_pw1_b=m_conv_pw1_b, conv_dw_w=m_conv_dw_w, conv_dw_b=m_conv_dw_b, conv_ln_g=m_conv_ln_g, conv_ln_b=m_conv_ln_b, conv_pw2_w=m_conv_pw2_w, conv_pw2_b=m_conv_pw2_b, ffn_up_w=m_ffn_up_w, ffn_dw_w=m_ffn_dw_w, ffn_dw_b=m_ffn_dw_b, ffn_down_w=m_ffn_down_w)
    mom2 = dict(ada_w=v_ada_w, ada_b=v_ada_b, norm_mix_g=v_norm_mix_g, norm_ffn_g=v_norm_ffn_g, ab_w_in=v_ab_w_in, a_vnorm_g=v_a_vnorm_g, a_spatial_w=v_a_spatial_w, a_spatial_b=v_a_spatial_b, b_q_norm_g=v_b_q_norm_g, b_k_norm_g=v_b_k_norm_g, ab_w_out=v_ab_w_out, conv_pw1_w=v_conv_pw1_w, conv_pw1_b=v_conv_pw1_b, conv_dw_w=v_conv_dw_w, conv_dw_b=v_conv_dw_b, conv_ln_g=v_conv_ln_g, conv_ln_b=v_conv_ln_b, conv_pw2_w=v_conv_pw2_w, conv_pw2_b=v_conv_pw2_b, ffn_up_w=v_ffn_up_w, ffn_dw_w=v_ffn_dw_w, ffn_dw_b=v_ffn_dw_b, ffn_down_w=v_ffn_down_w)
    order = list(weights)
    d, f2 = D_MODEL, 2 * FFN_DIM
    t = x.shape[1]
    me = 4 * lax.axis_index("x") + 2 * lax.axis_index("y") + lax.axis_index("c")
    for window, dil in PATTERNS:
        assert window // dil == Q_BLOCK and t % (dil * Q_BLOCK) == 0

    small_in = [c[0], conv_pw1_b[0], conv_dw_w[0], conv_dw_b[0], conv_ln_g[0], conv_ln_b[0], conv_pw2_b[0], ffn_dw_w]
    g1 = _all_gather_vmem(_pack(small_in, rows=8), "gather_small").reshape(N_DEV, -1)
    c_all, pw1_b, dw_w, dw_b, ln_g, ln_b, pw2_b, fdw_w = _unpack(g1, [a.shape for a in small_in])
    pw1_b, dw_b, ln_g, ln_b, pw2_b = [a.reshape(1, -1) for a in (pw1_b, dw_b, ln_g, ln_b, pw2_b)]
    dw_w = dw_w.transpose(1, 0, 2).reshape(CONV_WIDTH, d)
    fdw_w = fdw_w.transpose(1, 2, 0, 3).reshape(2, FFN_CONV_WIDTH, f2)

    c16 = jnp.pad(c_all, ((0, 2 * N_DEV - c_all.shape[0]), (0, 0)))
    part = jnp.concatenate([_ada_fwd(c16, ada_w[l], f"ada_fwd{l}")[:N_DEV] for l in range(2)], axis=1)
    g2 = _all_gather_vmem(part, "gather_mod").reshape(N_DEV, N_DEV, 2, -1)
    mod = lax.dynamic_index_in_dim(g2, me, axis=1, keepdims=False).transpose(1, 0, 2).reshape(2, 6 * d) + ada_b

    stages = dict(l0_mix=[ab_w_in[0].T, ab_w_out[0]], l0_ffn=[ffn_up_w[0].T, ffn_down_w[0]],
                  l1_conv=[conv_pw1_w[0].T, conv_pw2_w[0]], l1_ffn=[ffn_up_w[1].T, ffn_down_w[1]])
    stages = {k: [s.astype(BF16) for s in v] for k, v in stages.items()}
    names = dict(l0_mix=("w_in_t", "w_out"), l0_ffn=("up_t", "down"), l1_conv=("pw1_t", "pw2"), l1_ffn=("up_t", "down"))
    ready = {"l0_mix": [a.reshape(-1, d) for a in _all_gather_hbm(stages["l0_mix"], "gather_mixer_weights")]}
    behind = (*ready["l0_mix"], mod)
    arriving = {}
    for stage, group in (("l0_ffn", ("l0_ffn",)), ("l1_conv", ("l1_conv", "l1_ffn"))):
        srcs = [s for g in group for s in stages[g]]
        arriving[stage], token = _pushes_start(
            srcs, _landing_zones(srcs, behind, f"gather_{stage}_zones"), False, f"gather_{stage}_start")
        behind = (token,)
        mod = mod + token[0:1, 0:1]

    def fetch(stage, after):
        if stage in arriving:
            full = [a.reshape(-1, d) for a in _pushes_wait(arriving[stage], after, f"gather_{stage}_wait")[1]]
            ready[stage] = full[:2]
            if stage == "l1_conv":
                ready["l1_ffn"] = full[2:]
        return dict(zip(names[stage], ready[stage]))

    causal = jnp.tril(jnp.ones((CHUNK, CHUNK), bool))
    wtril = jnp.where(causal[None], a_spatial_w[0], 0.0)
    mixer_w = dict(
        gain=a_vnorm_g[0].reshape(1, A_WIDTH), wtril=wtril.astype(BF16),
        wtril_t=wtril.transpose(0, 2, 1).astype(BF16),
        bias_exp=jnp.repeat(a_spatial_b[0].T, GROUP_DIM, axis=1),
        gq=jnp.tile(b_q_norm_g[0], HEADS)[None, :], gk=jnp.tile(b_k_norm_g[0], HEADS)[None, :],
        seg=jnp.kron(jnp.eye(HEADS, dtype=BF16), jnp.ones((HEAD_DIM, HEAD_DIM), BF16)))
    conv_w = dict(pw1_b=pw1_b, dw_w=dw_w, dw_b=dw_b, ln_g=ln_g, ln_b=ln_b, pw2_b=pw2_b)
    ffn_w = [dict(dw_w=fdw_w[l].reshape(FFN_CONV_WIDTH, 2, FFN_DIM).transpose(1, 0, 2), dw_b=ffn_dw_b[l].reshape(2, 1, FFN_DIM))
             for l in range(2)]

    leaving = {}

    def emit(stage, grads):
        blocks = [g.reshape(N_DEV, g.shape[0] // N_DEV, d) for g in grads]
        leaving[stage], token = _pushes_start(
            blocks, [lax.empty(b.shape, b.dtype) for b in blocks], True, f"reduce_{stage}_start")
        return token

    loss, dx, dmod, (g_mix, g_ffn0, g_conv, g_ffn1) = _local_step(
        x[0], loss_target[0], positions[0].astype(F32)[:, None], mod, norm_mix_g, norm_ffn_g, mixer_w, conv_w, ffn_w,
        fetch, emit)

    me_op = me.astype(jnp.int32).reshape(1)

    def reduced(stage, after, layer=0, layers=1, into=(None, None)):
        blocks, lands = _pushes_wait(leaving[stage], after, f"reduce_{stage}_wait")
        return [_sum_with_own(b, a, me_op, f"reduce_{stage}_sum{i}", layer, layers, into[i])
                for i, (b, a) in enumerate(zip(blocks, lands))]

    r_ffn = reduced("l1_ffn", dx, 1, 2)
    r_pw1_t, r_pw2 = reduced("l1_conv", dx)
    r_up_t, r_down = reduced("l0_ffn", dx, 0, 2, r_ffn)

    small_g = [
        dmod, jnp.concatenate([g_mix["norm_g"], g_conv["norm_g"]]), jnp.concatenate([g_ffn0["norm_g"], g_ffn1["norm_g"]]),
        g_mix["vnorm_g"], g_mix["spatial_w"], g_mix["spatial_b"], g_mix["q_norm_g"], g_mix["k_norm_g"],
        g_conv["pw1_b"], g_conv["dw_w"], g_conv["dw_b"], g_conv["ln_g"], g_conv["ln_b"], g_conv["pw2_b"],
        jnp.stack([g_ffn0["dw_w"], g_ffn1["dw_w"]]), jnp.concatenate([g_ffn0["dw_b"], g_ffn1["dw_b"]]), loss]
    packed = _pack(small_g, rows=8)
    small_leaving, token = _pushes_start([packed], [lax.empty((N_DEV,) + packed.shape, F32)], False, "gather_small_grads_start")

    grads = dict(conv_pw2_w=r_pw2, ffn_down_w=r_down)
    grads_t = dict(conv_pw1_w=r_pw1_t, ffn_up_w=r_up_t)
    flip = lambda a: jnp.swapaxes(a, 1, 2)
    delta, new_m, new_v = {}, {}, {}

    def update(name, behind):
        if name in grads_t:
            grads[name] = flip(grads_t[name])
            res = _adamw(flip(weights[name]), grads_t[name], flip(mom1[name]), flip(mom2[name]), behind, f"adamw_{name}")
            delta[name], new_m[name], new_v[name] = [flip(r) for r in res]
        else:
            delta[name], new_m[name], new_v[name] = _adamw(
                weights[name], grads[name], mom1[name], mom2[name], behind, f"adamw_{name}")

    for name in ("conv_pw1_w", "conv_pw2_w", "ffn_up_w", "ffn_down_w"):
        update(name, token)
    r_in_t, r_out = reduced("l0_mix", new_v["ffn_down_w"])
    grads_t["ab_w_in"], grads["ab_w_out"] = r_in_t, r_out
    update("ab_w_in", token)
    update("ab_w_out", token)

    (packed,), (landed,) = _pushes_wait(small_leaving, tuple(new_v.values()), "gather_small_grads_wait")
    total = _sum_in_device_order(packed, landed, me_op, "sum_small_grads")
    (s_dmod, s_mix_g, s_ffn_g, s_vnorm, s_sp_w, s_sp_b, s_gq, s_gk, s_pw1_b, s_dw_w, s_dw_b, s_ln_g, s_ln_b,
     s_pw2_b, s_fdw_w, s_fdw_b, s_loss) = _unpack(total.reshape(-1), [a.shape for a in small_g])
    dmod_all = lax.dynamic_update_slice(
        landed.reshape(N_DEV, -1)[:, :dmod.size].reshape((N_DEV,) + dmod.shape), dmod[None], (me, 0, 0))
    n_ada = ada_w.shape[2]
    dmod16 = jnp.pad(_take_block(dmod_all, me, n_ada, 2), ((0, N_DEV), (0, 0), (0, 0)))
    grads.update(
        ada_w=jnp.stack([_ada_bwd(c16, dmod16[:, l], f"ada_bwd{l}") for l in range(2)]),
        ada_b=s_dmod, norm_mix_g=s_mix_g, norm_ffn_g=s_ffn_g,
        a_vnorm_g=s_vnorm[None], a_spatial_w=s_sp_w[None], a_spatial_b=s_sp_b[None], b_q_norm_g=s_gq[None],
        b_k_norm_g=s_gk[None],
        conv_pw1_b=_take_block(s_pw1_b, me, conv_pw1_b.shape[1], 1),
        conv_dw_w=_take_block(s_dw_w, me, conv_dw_w.shape[2], 1)[None],
        conv_dw_b=_take_block(s_dw_b, me, conv_dw_b.shape[1], 1), conv_ln_g=_take_block(s_ln_g, me, conv_ln_g.shape[1], 1),
        conv_ln_b=_take_block(s_ln_b, me, conv_ln_b.shape[1], 1),
        conv_pw2_b=_take_block(s_pw2_b, me, conv_pw2_b.shape[1], 1),
        ffn_dw_w=_take_block(s_fdw_w, me, ffn_dw_w.shape[2], 2), ffn_dw_b=s_fdw_b)
    update("ada_w", None)
    large = ("ada_w", "conv_pw1_w", "conv_pw2_w", "ffn_up_w", "ffn_down_w", "ab_w_in", "ab_w_out")
    small = [n for n in order if n not in large]
    res = _adamw_small(*[[src[n] for n in small] for src in (weights, grads, mom1, mom2)], "adamw_small")
    for dst, arrs in zip((delta, new_m, new_v), res):
        dst.update(zip(small, arrs))

    return (s_loss[0, 0], dx[None], *[grads[n] for n in order], *[delta[n] for n in order],
            *[new_m[n] for n in order], *[new_v[n] for n in order])
```

```python
import functools
import math

import jax
import jax.numpy as jnp
from jax import lax
from jax.experimental import pallas as pl
from jax.experimental.pallas import tpu as pltpu

F32 = jnp.float32
BF16 = jnp.bfloat16
MESH = pl.DeviceIdType.MESH

D_MODEL = 1024
A_WIDTH = 512
A_GROUPS = 4
GROUP_DIM = 128
CHUNK = 128
B_WIDTH = 512
HEADS = 8
HEAD_DIM = 64
PATTERNS = ((128, 1), (512, 4), (2048, 16))
Q_BLOCK = 128
ROPE_THETA = 10000.0
AB_IN = 2560
CONV_WIDTH = 31
FFN_DIM = 2816
FFN_CONV_WIDTH = 3
EPS = 1e-6
NEG = -1e30
N_DEV = 8
ADAM_LR, ADAM_B1, ADAM_B2, ADAM_EPS, ADAM_WD, ADAM_STEP = 0.001, 0.9, 0.999, 1e-08, 0.01, 10

V7X_VMEM_LIMIT = 56 * 2**20
FFN_HALO = 16
CONV_HALO = 32

_NN = (((1,), (0,)), ((), ()))
_NT = (((1,), (1,)), ((), ()))
_TN = (((0,), (0,)), ((), ()))


def _tile(n, prefs=(512, 256, 128)):
    for t in prefs:
        if n % t == 0:
            return t
    return n


def _row_tile(n, cap=512):
    best = n
    for t in range(8, min(n, cap) + 1, 8):
        if n % t == 0:
            best = t
    return best if best <= cap else n


def _params(*sem):
    return pltpu.CompilerParams(dimension_semantics=sem, vmem_limit_bytes=V7X_VMEM_LIMIT)


def _dot(a, b, dims):
    return lax.dot_general(a, b, dims, preferred_element_type=F32)


def _sigmoid(x):
    return 1.0 / (1.0 + jnp.exp(-x))


def _gelu(x):
    return 0.5 * x * (1.0 + lax.erf(x * (2.0 ** -0.5)))


def _gelu_grad(x):
    return 0.5 * (1.0 + lax.erf(x * (2.0 ** -0.5))) + x * jnp.exp(-0.5 * x * x) * (1.0 / math.sqrt(2.0 * math.pi))


def _colsum(v):
    return jnp.sum(v, axis=0, keepdims=True)


MATMUL_VMEM_BUDGET = 40 * 2**20


def _matmul_tiles(m, n, k, out_bytes, with_resid):
    def options(dim):
        opts = [t for t in (1024, 512, 256, 128) if dim % t == 0]
        return opts + [dim] if dim <= 4096 and dim not in opts else opts

    best = None
    for tm in options(m):
        for tn in options(n):
            need = 4 * (tm * k + k * tn) + tm * tn * (4 + 2 * out_bytes) + (24 * tm * tn if with_resid else 0)
            if need <= MATMUL_VMEM_BUDGET and (best is None or tm * tn / (tm + tn) > best[0]):
                best = (tm * tn / (tm + tn), tm, tn)
    return best[1], best[2]


def _matmul_tn_acc(a, b, name, tk=1024):
    squeeze = a.ndim == 2
    a3 = a[None] if squeeze else a
    p_, t, m = a3.shape
    n = b.shape[1]
    nk = t // tk

    def body(a_ref, b_ref, o_ref, acc_ref):
        kt = pl.program_id(1)

        @pl.when(kt == 0)
        def _():
            acc_ref[...] = jnp.zeros_like(acc_ref)

        acc_ref[...] += _dot(a_ref[...], b_ref[...], _TN)

        @pl.when(kt == nk - 1)
        def _():
            o_ref[...] = acc_ref[...].astype(BF16)

    out = pl.pallas_call(
        body, name=name, grid=(p_, nk),
        in_specs=[pl.BlockSpec((None, tk, m), lambda p, kt: (p, kt, 0)), pl.BlockSpec((tk, n), lambda p, kt: (kt, 0))],
        out_specs=pl.BlockSpec((None, m, n), lambda p, kt: (p, 0, 0)), out_shape=jax.ShapeDtypeStruct((p_, m, n), BF16),
        scratch_shapes=[pltpu.VMEM((m, n), F32)], compiler_params=_params("parallel", "arbitrary"),
    )(a3, b)
    return out[0] if squeeze else out


def _matmul(a, b, mode, out_dtype, name, bias=None, resid=None):
    if mode == "nn":
        (m, k), (_, n) = a.shape, b.shape
    elif mode == "nt":
        (m, k), (n, _) = a.shape, b.shape
    else:
        (k, m), (_, n) = a.shape, b.shape
    tm, tn = _matmul_tiles(m, n, k, jnp.dtype(out_dtype).itemsize, resid is not None)
    dims = {"nn": _NN, "nt": _NT, "tn": _TN}[mode]
    a_spec = pl.BlockSpec((k, tm), lambda i, j: (0, i)) if mode == "tn" else pl.BlockSpec((tm, k), lambda i, j: (i, 0))
    b_spec = pl.BlockSpec((tn, k), lambda i, j: (j, 0)) if mode == "nt" else pl.BlockSpec((k, tn), lambda i, j: (0, j))
    in_specs, args = [a_spec, b_spec], [a, b]
    row_spec = pl.BlockSpec((1, tn), lambda i, j: (0, j))
    tile_spec = pl.BlockSpec((tm, tn), lambda i, j: (i, j))
    if bias is not None:
        in_specs.append(row_spec)
        args.append(bias)
    if resid is not None:
        in_specs += [tile_spec, row_spec]
        args += list(resid)
    out_shape = [jax.ShapeDtypeStruct((m, n), out_dtype)]
    out_specs = [tile_spec]
    if resid is not None:
        out_shape.append(jax.ShapeDtypeStruct((m, n), F32))
        out_specs.append(tile_spec)

    def body(*refs):
        a_ref, b_ref = refs[0], refs[1]
        pos = 2
        acc = _dot(a_ref[...], b_ref[...], dims)
        if bias is not None:
            acc = acc + refs[pos][...]
            pos += 1
        if resid is not None:
            x_ref, g_ref = refs[pos], refs[pos + 1]
            pos += 2
        refs[pos][...] = acc.astype(out_dtype)
        if resid is not None:
            refs[pos + 1][...] = x_ref[...] + g_ref[...] * acc

    outs = pl.pallas_call(
        body, name=name, grid=(m // tm, n // tn), in_specs=in_specs, out_specs=out_specs, out_shape=out_shape,
        compiler_params=_params("parallel", "parallel"),
    )(*args)
    return outs if resid is not None else outs[0]


NORM_TM = (1024, 512, 256, 128)


def _modnorm(x, g, sc, sh, name):
    t, d = x.shape
    tm = _tile(t, NORM_TM)
    row = pl.BlockSpec((1, d), lambda i: (0, 0))
    blk = pl.BlockSpec((tm, d), lambda i: (i, 0))

    def body(x_ref, g_ref, sc_ref, sh_ref, o_ref):
        x = x_ref[...]
        r = lax.rsqrt(jnp.mean(x * x, axis=-1, keepdims=True) + EPS)
        o_ref[...] = ((x * r) * g_ref[...] * (1.0 + sc_ref[...]) + sh_ref[...]).astype(BF16)

    return pl.pallas_call(
        body, name=name, grid=(t // tm,), in_specs=[blk, row, row, row], out_specs=blk,
        out_shape=jax.ShapeDtypeStruct((t, d), BF16), compiler_params=_params("parallel"),
    )(x, g, sc, sh)


def _gate_bwd_tile(dx, y_ref, gate_ref, dy_ref, dgate_ref, first):
    @pl.when(first)
    def _():
        dgate_ref[...] = jnp.zeros_like(dgate_ref)

    dy_ref[...] = (dx * gate_ref[...]).astype(BF16)
    dgate_ref[...] += _colsum(dx * y_ref[...].astype(F32))


def _modnorm_bwd(x, dh, g, sc, dres, below, name):
    t, d = x.shape
    tm = _tile(t, NORM_TM)
    row = pl.BlockSpec((1, d), lambda i: (0, 0))
    blk = pl.BlockSpec((tm, d), lambda i: (i, 0))

    def body(x_ref, dh_ref, g_ref, sc_ref, dres_ref, *rest):
        dx_ref, dw_ref, dsh_ref = rest[-5:-2] if below else rest
        first = pl.program_id(0) == 0

        @pl.when(first)
        def _():
            dw_ref[...] = jnp.zeros_like(dw_ref)
            dsh_ref[...] = jnp.zeros_like(dsh_ref)

        x = x_ref[...]
        dh = dh_ref[...].astype(F32)
        r = lax.rsqrt(jnp.mean(x * x, axis=-1, keepdims=True) + EPS)
        xn = x * r
        dxn = dh * (g_ref[...] * (1.0 + sc_ref[...]))
        dx = dres_ref[...] + r * (dxn - xn * jnp.mean(dxn * xn, axis=-1, keepdims=True))
        dx_ref[...] = dx
        dw_ref[...] += _colsum(dh * xn)
        dsh_ref[...] += _colsum(dh)
        if below:
            _gate_bwd_tile(dx, rest[0], rest[1], rest[-2], rest[-1], first)

    row_out = jax.ShapeDtypeStruct((1, d), F32)
    return pl.pallas_call(
        body, name=name, grid=(t // tm,), in_specs=[blk, blk, row, row, blk] + ([blk, row] if below else []),
        out_specs=[blk, row, row] + ([blk, row] if below else []),
        out_shape=[jax.ShapeDtypeStruct((t, d), F32), row_out, row_out]
        + ([jax.ShapeDtypeStruct((t, d), BF16), row_out] if below else []),
        compiler_params=_params("arbitrary"),
    )(x, dh, g, sc, dres, *(below or ()))


def _loss_head(y, target, below, name):
    t, d = y.shape
    tm = _tile(t, NORM_TM)
    blk = pl.BlockSpec((tm, d), lambda i: (i, 0))
    row = pl.BlockSpec((1, d), lambda i: (0, 0))
    one = pl.BlockSpec((1, 1), lambda i: (0, 0))
    steps = t // tm

    def body(y_ref, t_ref, yb_ref, gate_ref, dx_ref, loss_ref, dy_ref, dgate_ref, acc_ref):
        first = pl.program_id(0) == 0

        @pl.when(first)
        def _():
            acc_ref[...] = jnp.zeros_like(acc_ref)

        e = y_ref[...] - t_ref[...]
        dx = e * (1.0 / d)
        dx_ref[...] = dx
        acc_ref[...] += _colsum(e * e)
        _gate_bwd_tile(dx, yb_ref, gate_ref, dy_ref, dgate_ref, first)

        @pl.when(pl.program_id(0) == steps - 1)
        def _():
            loss_ref[...] = jnp.sum(acc_ref[...], axis=1, keepdims=True) * (0.5 / d)

    return pl.pallas_call(
        body, name=name, grid=(steps,), in_specs=[blk, blk, blk, row], out_specs=[blk, one, blk, row],
        out_shape=[jax.ShapeDtypeStruct((t, d), F32), jax.ShapeDtypeStruct((1, 1), F32),
                   jax.ShapeDtypeStruct((t, d), BF16), jax.ShapeDtypeStruct((1, d), F32)],
        scratch_shapes=[pltpu.VMEM((1, d), F32)], compiler_params=_params("arbitrary"),
    )(y, target, *below)


GMLP_TM = 512


def _group_norm(vg, gain):
    mu = jnp.mean(vg, axis=-1, keepdims=True)
    xc = vg - mu
    rstd = lax.rsqrt(jnp.mean(xc * xc, axis=-1, keepdims=True) + EPS)
    xhat = xc * rstd
    return xhat, rstd, xhat * gain


def _gmlp_fwd(z, gain, wtril, bias_exp, name):
    t = z.shape[0]
    tm = _tile(t, (GMLP_TM,))
    zu = pl.BlockSpec((tm, A_WIDTH), lambda i: (i, 0))
    zv = pl.BlockSpec((tm, A_WIDTH), lambda i: (i, 1))
    full2 = lambda shp: pl.BlockSpec(shp, lambda i: (0, 0))
    w_spec = pl.BlockSpec((A_GROUPS, CHUNK, CHUNK), lambda i: (0, 0, 0))

    def body(zu_ref, zv_ref, gain_ref, w_ref, b_ref, ya_ref):
        for c in range(tm // CHUNK):
            rows = slice(c * CHUNK, (c + 1) * CHUNK)
            ua = _gelu(zu_ref[rows, :].astype(F32))
            vg = _gelu(zv_ref[rows, :].astype(F32))
            for g in range(A_GROUPS):
                sl = slice(g * GROUP_DIM, (g + 1) * GROUP_DIM)
                _, _, vn = _group_norm(vg[:, sl], gain_ref[:, sl])
                f = _dot(w_ref[g], vn.astype(BF16), _NN) + b_ref[:, sl]
                ya_ref[rows, sl] = (ua[:, sl] * f).astype(BF16)

    return pl.pallas_call(
        body, name=name, grid=(t // tm,),
        in_specs=[zu, zv, full2((1, A_WIDTH)), w_spec, full2((CHUNK, A_WIDTH))], out_specs=zu,
        out_shape=jax.ShapeDtypeStruct((t, A_WIDTH + B_WIDTH), BF16), compiler_params=_params("parallel"),
    )(z, z, gain, wtril, bias_exp)


def _gmlp_bwd(z, dcat, gain, wtril, wtril_t, bias_exp, name):
    t = z.shape[0]
    tm = _tile(t, (GMLP_TM,))
    zu = pl.BlockSpec((tm, A_WIDTH), lambda i: (i, 0))
    zv = pl.BlockSpec((tm, A_WIDTH), lambda i: (i, 1))
    full2 = lambda shp: pl.BlockSpec(shp, lambda i: (0, 0))
    w_spec = pl.BlockSpec((A_GROUPS, CHUNK, CHUNK), lambda i: (0, 0, 0))
    dz_spec = pl.BlockSpec((tm, 2 * A_WIDTH), lambda i: (i, 0))

    def body(zu_ref, zv_ref, dya_ref, gain_ref, w_ref, wt_ref, b_ref, dz_ref, dw_ref, dgain_ref, dbias_ref):
        @pl.when(pl.program_id(0) == 0)
        def _():
            dw_ref[...] = jnp.zeros_like(dw_ref)
            dgain_ref[...] = jnp.zeros_like(dgain_ref)
            dbias_ref[...] = jnp.zeros_like(dbias_ref)

        row = lax.broadcasted_iota(jnp.int32, (CHUNK, CHUNK), 0)
        col = lax.broadcasted_iota(jnp.int32, (CHUNK, CHUNK), 1)
        for c in range(tm // CHUNK):
            rows = slice(c * CHUNK, (c + 1) * CHUNK)
            zu_v = zu_ref[rows, :].astype(F32)
            zv_v = zv_ref[rows, :].astype(F32)
            dya = dya_ref[rows, :].astype(F32)
            ua = _gelu(zu_v)
            vg = _gelu(zv_v)
            for g in range(A_GROUPS):
                sl = slice(g * GROUP_DIM, (g + 1) * GROUP_DIM)
                gain_g = gain_ref[:, sl]
                xhat, rstd, vn = _group_norm(vg[:, sl], gain_g)
                vn16 = vn.astype(BF16)
                f = _dot(w_ref[g], vn16, _NN) + b_ref[:, sl]
                df = dya[:, sl] * ua[:, sl]
                df16 = df.astype(BF16)
                dz_ref[rows, sl] = (dya[:, sl] * f * _gelu_grad(zu_v[:, sl])).astype(BF16)
                dw_ref[g] += jnp.where(row >= col, _dot(df16, vn16, _NT), 0.0)
                dvn = _dot(wt_ref[g], df16, _NN)
                dgain_ref[:, sl] += _colsum(dvn * xhat)
                dxh = dvn * gain_g
                dvg = rstd * (dxh - jnp.mean(dxh, axis=-1, keepdims=True) - xhat * jnp.mean(dxh * xhat, axis=-1, keepdims=True))
                dz_ref[rows, A_WIDTH + g * GROUP_DIM:A_WIDTH + (g + 1) * GROUP_DIM] = (dvg * _gelu_grad(zv_v[:, sl])).astype(BF16)
                dbias_ref[:, sl] += df

    return pl.pallas_call(
        body, name=name, grid=(t // tm,),
        in_specs=[zu, zv, zu, full2((1, A_WIDTH)), w_spec, w_spec, full2((CHUNK, A_WIDTH))],
        out_specs=[dz_spec, w_spec, full2((1, A_WIDTH)), full2((CHUNK, A_WIDTH))],
        out_shape=[jax.ShapeDtypeStruct((t, 2 * A_WIDTH), BF16), jax.ShapeDtypeStruct((A_GROUPS, CHUNK, CHUNK), F32),
                   jax.ShapeDtypeStruct((1, A_WIDTH), F32), jax.ShapeDtypeStruct((CHUNK, A_WIDTH), F32)],
        compiler_params=_params("arbitrary"),
    )(z, z, dcat, gain, wtril, wtril_t, bias_exp)


def _rope_tables(pos, inv_freq, sign, name):
    t = pos.shape[0]
    tm = _tile(t)
    row = pl.BlockSpec((1, B_WIDTH), lambda i: (0, 0))
    blk = pl.BlockSpec((tm, B_WIDTH), lambda i: (i, 0))

    def body(pos_ref, f_ref, s_ref, cos_ref, sin_ref):
        ang = pos_ref[...] * f_ref[:, 0:LANES]
        cos_ref[...] = jnp.tile(jnp.cos(ang), (1, B_WIDTH // LANES))
        sin_ref[...] = jnp.tile(jnp.sin(ang) * s_ref[:, 0:LANES], (1, B_WIDTH // LANES))

    return pl.pallas_call(
        body, name=name, grid=(t // tm,), in_specs=[pl.BlockSpec((tm, 1), lambda i: (i, 0)), row, row],
        out_specs=[blk, blk], out_shape=[jax.ShapeDtypeStruct((t, B_WIDTH), F32)] * 2,
        compiler_params=_params("parallel"),
    )(pos, inv_freq, sign)


def _head_sum(v, seg):
    hi = v.astype(BF16)
    lo = (v - hi.astype(F32)).astype(BF16)
    return _dot(hi, seg, _NN) + _dot(lo, seg, _NN)


def _swap_halves(v):
    lane = lax.broadcasted_iota(jnp.int32, v.shape, 1)
    return jnp.where((lane & (HEAD_DIM - 1)) < HEAD_DIM // 2,pltpu.roll(v, B_WIDTH - HEAD_DIM // 2, 1), pltpu.roll(v, HEAD_DIM // 2, 1))


DILATIONS = tuple(dil for _, dil in PATTERNS)
SUBSEQ_TM = 512
LANES = 128


def _subseq_shape(t, dil):
    return (t // dil, dil * B_WIDTH)


def _subseq_spec(tm, dil):
    return pl.BlockSpec((tm // dil, dil * B_WIDTH), lambda i: (i, 0))


def _to_subseq(x, scr_ref, dil):
    if dil == 1:
        return x
    tm, w = x.shape
    for c in range(w // LANES):
        scr_ref[c * tm:(c + 1) * tm, :] = x[:, c * LANES:(c + 1) * LANES]
    return jnp.concatenate([scr_ref[pl.ds(c * tm + r, tm // dil, stride=dil), :]
                            for r in range(dil) for c in range(w // LANES)], axis=1)


def _from_subseq(y, scr_ref, dil):
    if dil == 1:
        return y
    n, w = y.shape[0], y.shape[1] // dil
    tm = n * dil
    for r in range(dil):
        for c in range(w // LANES):
            scr_ref[pl.ds(c * tm + r, n, stride=dil), :] = y[:, r * w + c * LANES:r * w + (c + 1) * LANES]
    return jnp.concatenate([scr_ref[c * tm:(c + 1) * tm, :] for c in range(w // LANES)], axis=1)


def _subseq_scratch(tm):
    return pltpu.VMEM((B_WIDTH // LANES * tm, LANES), F32)


def _qk_prep(z, cos_t, sin_t, gq, gk, seg, name):
    t = z.shape[0]
    tm = _tile(t, (SUBSEQ_TM,))
    col = lambda c: pl.BlockSpec((tm, B_WIDTH), lambda i: (i, c))
    row = pl.BlockSpec((1, B_WIDTH), lambda i: (0, 0))
    blk = col(0)
    nd = len(DILATIONS)

    def body(q_ref, k_ref, v_ref, cos_ref, sin_ref, gq_ref, gk_ref, seg_ref, *rest):
        out_refs, scr_ref = rest[:-1], rest[-1]

        def norm_rot(x, g):
            r = lax.rsqrt(_head_sum(x * x, seg_ref[...]) * (1.0 / HEAD_DIM) + EPS)
            xn = x * r * g
            return xn * cos_ref[...] + _swap_halves(xn) * sin_ref[...]

        vals = (norm_rot(q_ref[...].astype(F32), gq_ref[...]), norm_rot(k_ref[...].astype(F32), gk_ref[...]),
                v_ref[...].astype(F32))
        for a, val in enumerate(vals):
            for b, dil in enumerate(DILATIONS):
                out_refs[a * nd + b][...] = _to_subseq(val, scr_ref, dil).astype(BF16)

    outs = pl.pallas_call(
        body, name=name, grid=(t // tm,),
        in_specs=[col(2), col(3), col(4), blk, blk, row, row, pl.BlockSpec((B_WIDTH, B_WIDTH), lambda i: (0, 0))],
        out_specs=[_subseq_spec(tm, dil) for _ in range(3) for dil in DILATIONS],
        out_shape=[jax.ShapeDtypeStruct(_subseq_shape(t, dil), BF16) for _ in range(3) for dil in DILATIONS],
        scratch_shapes=[_subseq_scratch(tm)], compiler_params=_params("parallel"),
    )(z, z, z, cos_t, sin_t, gq, gk, seg)
    return outs[:nd], outs[nd:2 * nd], outs[2 * nd:]


def _qk_prep_bwd(z, dqs, dks, dvs, cos_t, sin_t, gq, gk, seg, name):
    t = z.shape[0]
    tm = _tile(t, (SUBSEQ_TM,))
    col = lambda c: pl.BlockSpec((tm, B_WIDTH), lambda i: (i, c))
    row = pl.BlockSpec((1, B_WIDTH), lambda i: (0, 0))
    blk = col(0)
    nb = len(DILATIONS)
    subs = [_subseq_spec(tm, dil) for dil in DILATIONS]

    def body(*refs):
        q_ref, k_ref = refs[0], refs[1]
        dq_refs, dk_refs, dv_refs = refs[2:2 + nb], refs[2 + nb:2 + 2 * nb], refs[2 + 2 * nb:2 + 3 * nb]
        cos_ref, sin_ref, gq_ref, gk_ref, seg_ref, dz_ref, dgq_ref, dgk_ref, scr_ref = refs[2 + 3 * nb:]

        @pl.when(pl.program_id(0) == 0)
        def _():
            dgq_ref[...] = jnp.zeros_like(dgq_ref)
            dgk_ref[...] = jnp.zeros_like(dgk_ref)

        def total(d_refs):
            return sum(_from_subseq(r_[...], scr_ref, dil) for r_, dil in zip(d_refs, DILATIONS))

        def back(x, d_refs, g, dg_ref):
            dout = total(d_refs)
            dy = dout * cos_ref[...] + _swap_halves(dout * sin_ref[...])
            r = lax.rsqrt(_head_sum(x * x, seg_ref[...]) * (1.0 / HEAD_DIM) + EPS)
            xn = x * r
            dg_ref[...] += _colsum(dy * xn)
            dxn = dy * g
            return r * (dxn - xn * (_head_sum(dxn * xn, seg_ref[...]) * (1.0 / HEAD_DIM)))

        dz_ref[:, 0:B_WIDTH] = back(q_ref[...].astype(F32), dq_refs, gq_ref[...], dgq_ref).astype(BF16)
        dz_ref[:, B_WIDTH:2 * B_WIDTH] = back(k_ref[...].astype(F32), dk_refs, gk_ref[...], dgk_ref).astype(BF16)
        dz_ref[:, 2 * B_WIDTH:3 * B_WIDTH] = total(dv_refs).astype(BF16)

    return pl.pallas_call(
        body, name=name, grid=(t // tm,),
        in_specs=[col(2), col(3)] + subs * 3 + [blk, blk, row, row, pl.BlockSpec((B_WIDTH, B_WIDTH), lambda i: (0, 0))],
        out_specs=[pl.BlockSpec((tm, 3 * B_WIDTH), lambda i: (i, 0)), row, row],
        out_shape=[jax.ShapeDtypeStruct((t, 3 * B_WIDTH), BF16), jax.ShapeDtypeStruct((1, B_WIDTH), F32),
                   jax.ShapeDtypeStruct((1, B_WIDTH), F32)],
        scratch_shapes=[_subseq_scratch(tm)], compiler_params=_params("arbitrary"),
    )(z, z, *dqs, *dks, *dvs, cos_t, sin_t, gq, gk, seg)


def _subseq_views(x, col, name):
    t = x.shape[0]
    tm = _tile(t, (SUBSEQ_TM,))

    def body(x_ref, *rest):
        out_refs, scr_ref = rest[:-1], rest[-1]
        val = x_ref[...].astype(F32)
        for o_ref, dil in zip(out_refs, DILATIONS):
            o_ref[...] = _to_subseq(val, scr_ref, dil).astype(o_ref.dtype)

    return pl.pallas_call(
        body, name=name, grid=(t // tm,), in_specs=[pl.BlockSpec((tm, B_WIDTH), lambda i: (i, col))],
        out_specs=[_subseq_spec(tm, dil) for dil in DILATIONS],
        out_shape=[jax.ShapeDtypeStruct(_subseq_shape(t, dil), x.dtype) for dil in DILATIONS],
        scratch_shapes=[_subseq_scratch(tm)], compiler_params=_params("parallel"),
    )(x)


ATTN_FWD_BLOCKS = 1
ATTN_BWD_BLOCKS = 2


def _attn_step_specs(nb, want):
    ns = want if nb % want == 0 else 1
    cur = pl.BlockSpec((ns * Q_BLOCK, B_WIDTH), lambda r, i: (i, r))
    prev = pl.BlockSpec((Q_BLOCK, B_WIDTH), lambda r, i: (jnp.maximum(ns * i - 1, 0), r))
    return ns, cur, prev


def _attn_fwd(q, k, v, dil, name):
    t = q.shape[0] * dil
    nb = t // dil // Q_BLOCK
    ns, cur, prev = _attn_step_specs(nb, ATTN_FWD_BLOCKS)

    def body(q_ref, kp_ref, kc_ref, vp_ref, vc_ref, o_ref, lse_ref):
        i = pl.program_id(1)
        a = lax.broadcasted_iota(jnp.int32, (Q_BLOCK, 2 * Q_BLOCK), 0)
        j = lax.broadcasted_iota(jnp.int32, (Q_BLOCK, 2 * Q_BLOCK), 1)
        dist = a + Q_BLOCK - j
        band = (dist >= 0) & (dist <= Q_BLOCK)
        sls = [slice(h * HEAD_DIM, (h + 1) * HEAD_DIM) for h in range(HEADS)]
        for sb in range(ns):
            rows = slice(sb * Q_BLOCK, (sb + 1) * Q_BLOCK)
            before = slice((sb - 1) * Q_BLOCK, sb * Q_BLOCK)
            q = q_ref[rows, :]
            kk = jnp.concatenate([kp_ref[...] if sb == 0 else kc_ref[before, :], kc_ref[rows, :]], axis=0)
            vv = jnp.concatenate([vp_ref[...] if sb == 0 else vc_ref[before, :], vc_ref[rows, :]], axis=0)
            mask = band & ((j >= Q_BLOCK) | (i > 0)) if sb == 0 else band
            scores = [_dot(q[:, sl], kk[:, sl], _NT) for sl in sls]
            ps, dens = [], []
            for sl, s in zip(sls, scores):
                s = jnp.where(mask, s * (HEAD_DIM ** -0.5), NEG)
                m = jnp.max(s, axis=-1, keepdims=True)
                p = jnp.exp(s - m)
                den = jnp.sum(p, axis=-1, keepdims=True)
                ps.append(p.astype(BF16))
                dens.append(den)
                lse_ref[rows, sl] = jnp.broadcast_to(m + jnp.log(den), (Q_BLOCK, HEAD_DIM))
            for sl, p, den in zip(sls, ps, dens):
                o_ref[rows, sl] = _dot(p, vv[:, sl], _NN) / den

    return pl.pallas_call(
        body, name=name, grid=(dil, nb // ns), in_specs=[cur, prev, cur, prev, cur], out_specs=[cur, cur],
        out_shape=[jax.ShapeDtypeStruct(_subseq_shape(t, dil), F32)] * 2,
        compiler_params=_params("parallel", "parallel"),
    )(q, k, k, v, v)


def _attn_merge(outs, lses, cat, name):
    nb = len(DILATIONS)
    t = cat.shape[0]
    tm = _tile(t, (SUBSEQ_TM,))
    subs = [_subseq_spec(tm, dil) for dil in DILATIONS]

    def body(*refs):
        o_refs, l_refs = refs[:nb], refs[nb:2 * nb]
        yb_refs, lse_refs, cat_ref, scr_ref = refs[2 * nb + 1:3 * nb + 1], refs[3 * nb + 1:4 * nb + 1], refs[4 * nb + 1], refs[4 * nb + 2]
        ls = [_from_subseq(r[...], scr_ref, dil) for r, dil in zip(l_refs, DILATIONS)]
        m = functools.reduce(jnp.maximum, ls)
        tot = m + jnp.log(sum(jnp.exp(l - m) for l in ls))
        yb = sum(jnp.exp(l - tot) * _from_subseq(o[...], scr_ref, dil) for l, o, dil in zip(ls, o_refs, DILATIONS))
        cat_ref[...] = yb.astype(BF16)
        yb = yb.astype(BF16).astype(F32)
        for yb_ref, lse_ref, dil in zip(yb_refs, lse_refs, DILATIONS):
            yb_ref[...] = _to_subseq(yb, scr_ref, dil).astype(BF16)
            lse_ref[...] = _to_subseq(tot, scr_ref, dil)

    outs_ = pl.pallas_call(
        body, name=name, grid=(t // tm,), in_specs=subs * 2 + [pl.BlockSpec(memory_space=pl.ANY)],
        out_specs=subs * 2 + [pl.BlockSpec((tm, B_WIDTH), lambda i: (i, A_WIDTH // B_WIDTH))],
        out_shape=[jax.ShapeDtypeStruct(_subseq_shape(t, dil), BF16) for dil in DILATIONS]
        + [jax.ShapeDtypeStruct(_subseq_shape(t, dil), F32) for dil in DILATIONS] + [jax.ShapeDtypeStruct(cat.shape, BF16)],
        input_output_aliases={2 * nb: 2 * nb}, scratch_shapes=[_subseq_scratch(tm)], compiler_params=_params("parallel"),
    )(*outs, *lses, cat)
    return outs_[:nb], outs_[nb:2 * nb], outs_[2 * nb]


def _attn_bwd(q, k, v, do, o, lse, dil, name):
    t = q.shape[0] * dil
    nb = t // dil // Q_BLOCK
    ns, cur, prev = _attn_step_specs(nb, ATTN_BWD_BLOCKS)
    scale = HEAD_DIM ** -0.5

    def body(q_ref, kp_ref, kc_ref, vp_ref, vc_ref, do_ref, o_ref, lse_ref, dq_ref, dk_ref, dv_ref,
             ck_ref, cv_ref, tk_ref, tv_ref):
        step = pl.program_id(1)

        @pl.when(step == 0)
        def _():
            ck_ref[...] = jnp.zeros_like(ck_ref)
            cv_ref[...] = jnp.zeros_like(cv_ref)

        a = lax.broadcasted_iota(jnp.int32, (Q_BLOCK, 2 * Q_BLOCK), 0)
        j = lax.broadcasted_iota(jnp.int32, (Q_BLOCK, 2 * Q_BLOCK), 1)
        dist = a + Q_BLOCK - j
        band = (dist >= 0) & (dist <= Q_BLOCK)
        sls = [slice(h * HEAD_DIM, (h + 1) * HEAD_DIM) for h in range(HEADS)]
        for sb in range(ns):
            i = ns * step + sb
            rows = slice(sb * Q_BLOCK, (sb + 1) * Q_BLOCK)
            before = slice((sb - 1) * Q_BLOCK, sb * Q_BLOCK)
            q = q_ref[rows, :]
            kk = jnp.concatenate([kp_ref[...] if sb == 0 else kc_ref[before, :], kc_ref[rows, :]], axis=0)
            vv = jnp.concatenate([vp_ref[...] if sb == 0 else vc_ref[before, :], vc_ref[rows, :]], axis=0)
            do = do_ref[rows, :]
            dof = do.astype(F32)
            of = o_ref[rows, :].astype(F32)
            mask = band & ((j >= Q_BLOCK) | (step > 0)) if sb == 0 else band
            scores = [_dot(q[:, sl], kk[:, sl], _NT) for sl in sls]
            dps = [_dot(do[:, sl], vv[:, sl], _NT) for sl in sls]
            ps, dss = [], []
            for sl, s, dp in zip(sls, scores, dps):
                p = jnp.exp(jnp.where(mask, s * scale, NEG) - lse_ref[rows, sl.start:sl.start + 1])
                delta = jnp.sum(dof[:, sl] * of[:, sl], axis=-1, keepdims=True)
                dss.append((p * (dp - delta) * scale).astype(BF16))
                ps.append(p.astype(BF16))
            for sl, p, ds in zip(sls, ps, dss):
                dq_ref[rows, sl] = _dot(ds, kk[:, sl], _NN)
                dv_t = _dot(do[:, sl], p, _TN)
                dk_t = _dot(q[:, sl], ds, _TN)
                tk_ref[sl, :] = ck_ref[sl, :] + dk_t[:, :Q_BLOCK]
                tv_ref[sl, :] = cv_ref[sl, :] + dv_t[:, :Q_BLOCK]
                ck_ref[sl, :] = dk_t[:, Q_BLOCK:]
                cv_ref[sl, :] = dv_t[:, Q_BLOCK:]

            @pl.when(i >= 1)
            def _():
                done = pl.ds(pl.multiple_of((i - 1) * Q_BLOCK, Q_BLOCK), Q_BLOCK)
                dk_ref[done, :] = tk_ref[...].T
                dv_ref[done, :] = tv_ref[...].T

        @pl.when(step == nb // ns - 1)
        def _():
            done = pl.ds((nb - 1) * Q_BLOCK, Q_BLOCK)
            dk_ref[done, :] = ck_ref[...].T
            dv_ref[done, :] = cv_ref[...].T

    whole = pl.BlockSpec((t // dil, B_WIDTH), lambda r, i: (0, r))
    return pl.pallas_call(
        body, name=name, grid=(dil, nb // ns), in_specs=[cur, prev, cur, prev, cur, cur, cur, cur],
        out_specs=[cur, whole, whole], out_shape=[jax.ShapeDtypeStruct(_subseq_shape(t, dil), F32)] * 3,
        scratch_shapes=[pltpu.VMEM((B_WIDTH, Q_BLOCK), F32)] * 4,
        compiler_params=_params("parallel", "arbitrary"),
    )(q, k, k, v, v, do, o, lse)


FFN_TN = 256
FFN_ACT_TM = (4096, 2048, 1024, 512, 256, 128)
FFN_FWD_CHUNK = 256
FFN_BWD_CHUNK = 128


def _ffn_up(h, up_t, name):
    t, k = h.shape
    tm = _tile(t)

    def body(h_ref, w_ref, o_ref):
        o_ref[...] = _dot(h_ref[...], w_ref[...], _NT).astype(BF16)

    return pl.pallas_call(
        body, name=name, grid=(2, t // tm),
        in_specs=[pl.BlockSpec((tm, k), lambda p, i: (i, 0)), pl.BlockSpec((None, FFN_DIM, k), lambda p, i: (p, 0, 0))],
        out_specs=pl.BlockSpec((None, tm, FFN_DIM), lambda p, i: (p, i, 0)),
        out_shape=jax.ShapeDtypeStruct((2, t, FFN_DIM), BF16), compiler_params=_params("parallel", "parallel"),
    )(h, up_t.reshape(2, FFN_DIM, k))


def _ffn_up_dx(du, up_t, name):
    t = du.shape[1]
    k = up_t.shape[1]
    tm = _tile(t)

    def body(a_ref, b_ref, o_ref):
        o_ref[...] = _dot(a_ref[0], b_ref[0], _NN) + _dot(a_ref[1], b_ref[1], _NN)

    return pl.pallas_call(
        body, name=name, grid=(t // tm,),
        in_specs=[pl.BlockSpec((2, tm, FFN_DIM), lambda i: (0, i, 0)), pl.BlockSpec((2, FFN_DIM, k), lambda i: (0, 0, 0))],
        out_specs=pl.BlockSpec((tm, k), lambda i: (i, 0)), out_shape=jax.ShapeDtypeStruct((t, k), F32),
        compiler_params=_params("parallel"),
    )(du, up_t.reshape(2, FFN_DIM, k))


def _ffn_conv(win, w_ref, b_ref, p):
    x = win.astype(F32)
    x0, x1, x2 = x[FFN_HALO:], pltpu.roll(x, 1, 0)[FFN_HALO:], pltpu.roll(x, 2, 0)[FFN_HALO:]
    return b_ref[p] + w_ref[p, 2:3, :] * x0 + w_ref[p, 1:2, :] * x1 + w_ref[p, 0:1, :] * x2


def _zero_if(cond, v):
    return jnp.where(cond, 0, v).astype(v.dtype)


def _ffn_act(u, dw_w, dw_b, name):
    t = u.shape[1]
    tm = _tile(t, FFN_ACT_TM)
    chunk = min(FFN_FWD_CHUNK, tm)
    hb = tm // FFN_HALO
    main = pl.BlockSpec((2, tm, FFN_TN), lambda i, j: (0, i, j))
    halo = pl.BlockSpec((2, FFN_HALO, FFN_TN), lambda i, j: (0, jnp.maximum(i * hb - 1, 0), j))
    wsp = pl.BlockSpec((2, FFN_CONV_WIDTH, FFN_TN), lambda i, j: (0, 0, j))
    bsp = pl.BlockSpec((2, 1, FFN_TN), lambda i, j: (0, 0, j))

    def body(u_ref, uh_ref, w_ref, b_ref, o_ref, z_ref):
        first = pl.program_id(0) == 0

        def emit(rows, wins):
            za, zb = _ffn_conv(wins[0], w_ref, b_ref, 0), _ffn_conv(wins[1], w_ref, b_ref, 1)
            o_ref[rows, :] = (za * _sigmoid(za) * zb).astype(BF16)
            z_ref[0, rows, :] = za.astype(BF16)
            z_ref[1, rows, :] = zb.astype(BF16)

        emit(pl.ds(0, chunk), [jnp.concatenate([_zero_if(first, uh_ref[p]), u_ref[p, 0:chunk, :]], axis=0) for p in range(2)])

        def step(c, carry):
            s = pl.multiple_of(c * chunk, chunk)
            emit(pl.ds(s, chunk), [u_ref[p, pl.ds(s - FFN_HALO, chunk + FFN_HALO), :] for p in range(2)])
            return carry

        lax.fori_loop(1, tm // chunk, step, 0)

    return pl.pallas_call(
        body, name=name, grid=(t // tm, FFN_DIM // FFN_TN), in_specs=[main, halo, wsp, bsp],
        out_specs=[pl.BlockSpec((tm, FFN_TN), lambda i, j: (i, j)), main],
        out_shape=[jax.ShapeDtypeStruct((t, FFN_DIM), BF16), jax.ShapeDtypeStruct((2, t, FFN_DIM), BF16)],
        compiler_params=_params("parallel", "parallel"),
    )(u, u, dw_w, dw_b)


def _fold8(v):
    return jnp.sum(v.reshape(v.shape[0] // 8, 8, v.shape[1]), axis=0)


def _ffn_act_bwd(u, z, dact, dw_w, name):
    t = u.shape[1]
    tm = _tile(t, FFN_ACT_TM)
    chunk = min(FFN_BWD_CHUNK, tm // 2)
    halo = FFN_HALO
    hb = tm // halo
    nt = t // tm
    last_halo = t // halo - 1
    next_i = lambda i: jnp.minimum((i + 1) * hb, last_halo)
    main = pl.BlockSpec((2, tm, FFN_TN), lambda j, i: (0, i, j))
    nxt = pl.BlockSpec((2, halo, FFN_TN), lambda j, i: (0, next_i(i), j))
    wsp = pl.BlockSpec((2, FFN_CONV_WIDTH, FFN_TN), lambda j, i: (0, 0, j))
    bsp = pl.BlockSpec((2, 1, FFN_TN), lambda j, i: (0, 0, j))

    def body(u_ref, z_ref, zn_ref, da_ref, dan_ref, w_ref, du_ref, dw_ref, db_ref, acc_ref):
        i = pl.program_id(1)
        last = i == nt - 1
        acc_ref[...] = jnp.zeros_like(acc_ref)

        def emit(rows, zs, dact):
            n = chunk + halo
            za, zb, dact = zs[0].astype(F32), zs[1].astype(F32), dact.astype(F32)
            sg = _sigmoid(za)
            dzs = (dact * zb * (sg * (1.0 + za * (1.0 - sg))), dact * (za * sg))
            for p, dz in enumerate(dzs):
                ahead = (dz[:chunk], pltpu.roll(dz, n - 1, 0)[:chunk], pltpu.roll(dz, n - 2, 0)[:chunk])
                um = u_ref[p, rows, :].astype(F32)
                acc_ref[p, FFN_CONV_WIDTH] += _fold8(ahead[0])
                du = None
                for j, dzj in enumerate(ahead):
                    k = FFN_CONV_WIDTH - 1 - j
                    acc_ref[p, k] += _fold8(dzj * um)
                    term = w_ref[p, k:k + 1, :] * dzj
                    du = term if du is None else du + term
                du_ref[p, rows, :] = du.astype(BF16)

        def step(c, carry):
            s = pl.multiple_of(c * chunk, chunk)
            emit(pl.ds(s, chunk), [z_ref[p, pl.ds(s, chunk + halo), :] for p in range(2)], da_ref[pl.ds(s, chunk + halo), :])
            return carry

        lax.fori_loop(0, tm // chunk - 1, step, 0)
        s = tm - chunk
        emit(pl.ds(s, chunk),
             [jnp.concatenate([z_ref[p, s:tm, :], zn_ref[p]], axis=0) for p in range(2)],
             jnp.concatenate([da_ref[s:tm, :], _zero_if(last, dan_ref[...])], axis=0))

        @pl.when(i == 0)
        def _():
            dw_ref[...] = jnp.zeros_like(dw_ref)
            db_ref[...] = jnp.zeros_like(db_ref)

        for p in range(2):
            for k in range(FFN_CONV_WIDTH):
                dw_ref[p, k:k + 1, :] += _colsum(acc_ref[p, k])
            db_ref[p] += _colsum(acc_ref[p, FFN_CONV_WIDTH])

    return pl.pallas_call(
        body, name=name, grid=(FFN_DIM // FFN_TN, nt),
        in_specs=[main, main, nxt, pl.BlockSpec((tm, FFN_TN), lambda j, i: (i, j)),
                  pl.BlockSpec((halo, FFN_TN), lambda j, i: (next_i(i), j)), wsp],
        out_specs=[main, wsp, bsp],
        out_shape=[jax.ShapeDtypeStruct((2, t, FFN_DIM), BF16), jax.ShapeDtypeStruct((2, FFN_CONV_WIDTH, FFN_DIM), F32),
                   jax.ShapeDtypeStruct((2, 1, FFN_DIM), F32)],
        scratch_shapes=[pltpu.VMEM((2, FFN_CONV_WIDTH + 1, 8, FFN_TN), F32)],
        compiler_params=_params("parallel", "arbitrary"),
    )(u, z, z, dact, dact, dw_w)


CONV_TM = 256
CONV_ROWS = 128
CONV_FWD_ROWS = 256
CONV_LANES = 128
CONV_NORM_ROWS = 32


def _glu_window(pa_ref, pah_ref, pg_ref, pgh_ref, scr_ref, first):
    ah, gh = pah_ref[...].astype(F32), pgh_ref[...].astype(F32)
    scr_ref[0:CONV_HALO, :] = jnp.where(first, 0.0, ah * _sigmoid(gh))
    scr_ref[CONV_HALO:, :] = pa_ref[...].astype(F32) * _sigmoid(pg_ref[...].astype(F32))


def _tap_slabs(win, rows, ahead):
    n = win.shape[0]
    for s in range(8):
        ws = win if s == 0 else pltpu.roll(win, n - s if ahead else s, 0)
        for q in range(CONV_HALO // 8):
            o = 8 * q + s
            if o < CONV_WIDTH:
                start = 8 * q if ahead else CONV_HALO - 8 * q
                yield CONV_WIDTH - 1 - o, ws[start:start + rows]


def _conformer_specs(t):
    tm = _tile(t, (CONV_TM, 128))
    hb = tm // CONV_HALO
    d = D_MODEL
    main = lambda c: pl.BlockSpec((tm, d), lambda i: (i, c))
    halo = lambda c: pl.BlockSpec((CONV_HALO, d), lambda i: (jnp.maximum(i * hb - 1, 0), c))
    row = pl.BlockSpec((1, d), lambda i: (0, 0))
    wsp = pl.BlockSpec((CONV_WIDTH, d), lambda i: (0, 0))
    return tm, main, halo, row, wsp


def _conformer_mid(p, dw_w, dw_b, ln_g, ln_b, name):
    t = p.shape[0]
    tm, main, halo, row, wsp = _conformer_specs(t)
    d, lanes, rows = D_MODEL, CONV_LANES, min(CONV_FWD_ROWS, tm)

    def body(pa_ref, pah_ref, pg_ref, pgh_ref, w_ref, b_ref, g_ref, lb_ref, o_ref, dc_ref, scr_ref):
        _glu_window(pa_ref, pah_ref, pg_ref, pgh_ref, scr_ref, pl.program_id(0) == 0)
        for c in range(d // lanes):
            ls = slice(c * lanes, (c + 1) * lanes)

            def taps(r, carry, ls=ls):
                r0 = pl.multiple_of(r * rows, rows)
                acc = jnp.broadcast_to(b_ref[:, ls], (rows, lanes))
                for k, slab in _tap_slabs(scr_ref[pl.ds(r0, rows + CONV_HALO), ls], rows, False):
                    acc = acc + w_ref[k:k + 1, ls] * slab
                dc_ref[pl.ds(r0, rows), ls] = acc
                return carry

            lax.fori_loop(0, tm // rows, taps, 0)

        def norm(r, carry):
            r0 = pl.multiple_of(r * CONV_NORM_ROWS, CONV_NORM_ROWS)
            dc = dc_ref[pl.ds(r0, CONV_NORM_ROWS), :]
            xc = dc - jnp.mean(dc, axis=-1, keepdims=True)
            ln = xc * lax.rsqrt(jnp.mean(xc * xc, axis=-1, keepdims=True) + EPS) * g_ref[...] + lb_ref[...]
            o_ref[pl.ds(r0, CONV_NORM_ROWS), :] = (ln * _sigmoid(ln)).astype(BF16)
            return carry

        lax.fori_loop(0, tm // CONV_NORM_ROWS,norm, 0)

    return pl.pallas_call(
        body, name=name, grid=(t // tm,), in_specs=[main(0), halo(0), main(1), halo(1), wsp, row, row, row],
        out_specs=[main(0), main(0)], out_shape=[jax.ShapeDtypeStruct((t, d), BF16), jax.ShapeDtypeStruct((t, d), F32)],
        scratch_shapes=[pltpu.VMEM((tm + CONV_HALO, d), F32)], compiler_params=_params("parallel"),
    )(p, p, p, p, dw_w, dw_b, ln_g, ln_b)


def _conformer_mid_bwd(p, dc, ds, ln_g, ln_b, name):
    t = p.shape[0]
    tm, main, halo, row, wsp = _conformer_specs(t)
    d, nt = D_MODEL, t // tm
    rows, lanes = CONV_ROWS, CONV_LANES

    def body(pa_ref, pah_ref, pg_ref, pgh_ref, dc_ref, ds_ref, g_ref, lb_ref,
             ddc_ref, dw_ref, db_ref, dg_ref, dlb_ref, scr_ref, wacc_ref, racc_ref):
        i = pl.program_id(0)

        @pl.when(i == 0)
        def _():
            wacc_ref[...] = jnp.zeros_like(wacc_ref)
            racc_ref[...] = jnp.zeros_like(racc_ref)

        _glu_window(pa_ref, pah_ref, pg_ref, pgh_ref, scr_ref, i == 0)

        def norm_bwd(r, carry):
            r0 = pl.multiple_of(r * CONV_NORM_ROWS, CONV_NORM_ROWS)
            dcv = dc_ref[pl.ds(r0, CONV_NORM_ROWS), :]
            xc = dcv - jnp.mean(dcv, axis=-1, keepdims=True)
            rstd = lax.rsqrt(jnp.mean(xc * xc, axis=-1, keepdims=True) + EPS)
            xhat = xc * rstd
            ln = xhat * g_ref[...] + lb_ref[...]
            sg = _sigmoid(ln)
            dln = ds_ref[pl.ds(r0, CONV_NORM_ROWS), :].astype(F32) * (sg * (1.0 + ln * (1.0 - sg)))
            dxh = dln * g_ref[...]
            ddc = rstd * (dxh - jnp.mean(dxh, axis=-1, keepdims=True) - xhat * jnp.mean(dxh * xhat, axis=-1, keepdims=True))
            ddc_ref[pl.ds(r0, CONV_NORM_ROWS), :] = ddc
            racc_ref[0] += _fold8(dln * xhat)
            racc_ref[1] += _fold8(dln)
            racc_ref[2] += _fold8(ddc)
            return carry

        lax.fori_loop(0, tm // CONV_NORM_ROWS,norm_bwd, 0)

        for c in range(d // lanes):
            ls = slice(c * lanes, (c + 1) * lanes)

            def taps(r, carry, ls=ls):
                r0 = pl.multiple_of(r * rows, rows)
                ddc = ddc_ref[pl.ds(r0, rows), ls]
                for k, slab in _tap_slabs(scr_ref[pl.ds(r0, rows + CONV_HALO), ls], rows, False):
                    wacc_ref[k, :, ls] += _fold8(ddc * slab)
                return carry

            lax.fori_loop(0, tm // rows, taps, 0)

        @pl.when(i == nt - 1)
        def _():
            for k in range(CONV_WIDTH):
                dw_ref[k:k + 1, :] = _colsum(wacc_ref[k])
            dg_ref[...] = _colsum(racc_ref[0])
            dlb_ref[...] = _colsum(racc_ref[1])
            db_ref[...] = _colsum(racc_ref[2])

    return pl.pallas_call(
        body, name=name, grid=(nt,), in_specs=[main(0), halo(0), main(1), halo(1), main(0), main(0), row, row],
        out_specs=[main(0), wsp, row, row, row],
        out_shape=[jax.ShapeDtypeStruct((t, d), F32), jax.ShapeDtypeStruct((CONV_WIDTH, d), F32)]
        + [jax.ShapeDtypeStruct((1, d), F32)] * 3,
        scratch_shapes=[pltpu.VMEM((tm + CONV_HALO, d), F32), pltpu.VMEM((CONV_WIDTH, 8, d), F32), pltpu.VMEM((3, 8, d), F32)],
        compiler_params=_params("arbitrary"),
    )(p, p, p, p, dc, ds, ln_g, ln_b)


def _conformer_glu_bwd(p, ddc, dw_w, name):
    t = p.shape[0]
    d = D_MODEL
    tm = _tile(t, (CONV_TM, 128))
    hb = tm // CONV_HALO
    nt = t // tm
    last_halo = t // CONV_HALO - 1
    rows, lanes = CONV_ROWS, CONV_LANES
    col = lambda c: pl.BlockSpec((tm, d), lambda i: (i, c))
    nxt = pl.BlockSpec((CONV_HALO, d), lambda i: (jnp.minimum((i + 1) * hb, last_halo), 0))

    def body(pa_ref, pg_ref, ddc_ref, ddcn_ref, w_ref, dp_ref, db_ref, scr_ref, acc_ref):
        i = pl.program_id(0)

        @pl.when(i == 0)
        def _():
            acc_ref[...] = jnp.zeros_like(acc_ref)

        scr_ref[0:tm, :] = ddc_ref[...]
        scr_ref[tm:, :] = _zero_if(i == nt - 1, ddcn_ref[...])
        for c in range(d // lanes):
            ls = slice(c * lanes, (c + 1) * lanes)
            gs = slice(d + c * lanes, d + (c + 1) * lanes)

            def taps(r, carry, ls=ls, gs=gs):
                r0 = pl.multiple_of(r * rows, rows)
                dglu = None
                for k, slab in _tap_slabs(scr_ref[pl.ds(r0, rows + CONV_HALO), ls], rows, True):
                    term = w_ref[k:k + 1, ls] * slab
                    dglu = term if dglu is None else dglu + term
                a = pa_ref[pl.ds(r0, rows), ls].astype(F32)
                sg = _sigmoid(pg_ref[pl.ds(r0, rows), ls].astype(F32))
                da = (dglu * sg).astype(BF16)
                dg = (dglu * a * sg * (1.0 - sg)).astype(BF16)
                dp_ref[pl.ds(r0, rows), ls] = da
                dp_ref[pl.ds(r0, rows), gs] = dg
                acc_ref[:, ls] += _fold8(da.astype(F32))
                acc_ref[:, gs] += _fold8(dg.astype(F32))
                return carry

            lax.fori_loop(0, tm // rows, taps, 0)

        @pl.when(i == nt - 1)
        def _():
            db_ref[...] = _colsum(acc_ref[...])

    return pl.pallas_call(
        body, name=name, grid=(nt,),
        in_specs=[col(0), col(1), col(0), nxt, pl.BlockSpec((CONV_WIDTH, d), lambda i: (0, 0))],
        out_specs=[pl.BlockSpec((tm, 2 * d), lambda i: (i, 0)), pl.BlockSpec((1, 2 * d), lambda i: (0, 0))],
        out_shape=[jax.ShapeDtypeStruct((t, 2 * d), BF16), jax.ShapeDtypeStruct((1, 2 * d), F32)],
        scratch_shapes=[pltpu.VMEM((tm + CONV_HALO, d), F32), pltpu.VMEM((8, 2 * d), F32)],
        compiler_params=_params("arbitrary"),
    )(p, p, ddc, ddc, dw_w)


def _colsum_call(a, name):
    t, n = a.shape
    tm = _tile(t)

    def body(a_ref, o_ref):
        @pl.when(pl.program_id(0) == 0)
        def _():
            o_ref[...] = jnp.zeros_like(o_ref)

        o_ref[...] += _colsum(a_ref[...].astype(F32))

    return pl.pallas_call(
        body, name=name, grid=(t // tm,), in_specs=[pl.BlockSpec((tm, n), lambda i: (i, 0))],
        out_specs=pl.BlockSpec((1, n), lambda i: (0, 0)), out_shape=jax.ShapeDtypeStruct((1, n), F32),
        compiler_params=_params("arbitrary"),
    )(a)


def _ada_fwd(c_all, w, name):
    rows, d = c_all.shape
    n = w.shape[1]
    tn = _tile(n, (256, 128))

    def body(c_ref, w_ref, o_ref):
        c = c_ref[...]
        o_ref[...] = _dot((c * _sigmoid(c)).astype(BF16), w_ref[...].astype(BF16), _NN)

    return pl.pallas_call(
        body, name=name, grid=(n // tn,),
        in_specs=[pl.BlockSpec((rows, d), lambda j: (0, 0)), pl.BlockSpec((d, tn), lambda j: (0, j))],
        out_specs=pl.BlockSpec((rows, tn), lambda j: (0, j)), out_shape=jax.ShapeDtypeStruct((rows, n), F32),
        compiler_params=_params("parallel"),
    )(c_all, w)


def _ada_bwd(c_all, dmod, name):
    rows, d = c_all.shape
    layers, _, n = dmod.shape
    tn = _tile(n, (256, 128))

    def body(c_ref, g_ref, o_ref):
        c = c_ref[...]
        o_ref[...] = _dot((c * _sigmoid(c)).astype(BF16), g_ref[...].astype(BF16), _TN)

    return pl.pallas_call(
        body, name=name, grid=(layers, n // tn),
        in_specs=[pl.BlockSpec((rows, d), lambda l, j: (0, 0)), pl.BlockSpec((None, rows, tn), lambda l, j: (l, 0, j))],
        out_specs=pl.BlockSpec((None, d, tn), lambda l, j: (l, 0, j)), out_shape=jax.ShapeDtypeStruct((layers, d, n), F32),
        compiler_params=_params("parallel", "parallel"),
    )(c_all, dmod)


def _sum_in_device_order(own, land, me, name):
    s, r, c = land.shape
    tr = _row_tile(r, 256)
    slot = lambda k: pl.BlockSpec((None, tr, c), lambda i, me_ref: (jnp.where(me_ref[0] == k, (k + 1) % s, k), i, 0))
    own_spec = pl.BlockSpec((tr, c), lambda i, me_ref: (i, 0))

    def body(me_ref, own_ref, *refs):
        o_ref = refs[-1]
        acc = None
        for k, ref in enumerate(refs[:-1]):
            term = jnp.where(me_ref[0] == k, own_ref[...], ref[...]).astype(F32)
            acc = term if acc is None else acc + term
        o_ref[...] = acc

    return pl.pallas_call(
        body, name=name, out_shape=jax.ShapeDtypeStruct((r, c), F32),
        grid_spec=pltpu.PrefetchScalarGridSpec(
            num_scalar_prefetch=1, grid=(r // tr,), in_specs=[own_spec] + [slot(k) for k in range(s)], out_specs=own_spec),
        compiler_params=_params("parallel"),
    )(me, own, *[land] * s)


def _sum_with_own(blocks, land, me, name, layer=0, layers=1, into=None):
    s, r, c = land.shape
    tr = _row_tile(r, 256)
    slot = lambda k: pl.BlockSpec((None, tr, c), lambda i, me_ref: ((me_ref[0] + k) % s, i, 0))
    stacked = [] if into is None else [into]

    def body(me_ref, own_ref, *refs):
        o_ref = refs[-1]
        acc = own_ref[...].astype(F32)
        for ref in refs[:s - 1]:
            acc = acc + ref[...].astype(F32)
        o_ref[...] = acc

    return pl.pallas_call(
        body, name=name, out_shape=jax.ShapeDtypeStruct((layers, r, c), F32),
        grid_spec=pltpu.PrefetchScalarGridSpec(
            num_scalar_prefetch=1, grid=(r // tr,),
            in_specs=[slot(k) for k in range(s)] + [pl.BlockSpec(memory_space=pl.ANY)] * len(stacked),
            out_specs=pl.BlockSpec((None, tr, c), lambda i, me_ref: (layer, i, 0))),
        input_output_aliases={s + 1: 0} if stacked else {},
        compiler_params=_params("parallel"),
    )(me, blocks, *[land] * (s - 1), *stacked)


def _adamw_update(w, g, m, v):
    nm = ADAM_B1 * m + (1.0 - ADAM_B1) * g
    nv = ADAM_B2 * v + (1.0 - ADAM_B2) * (g * g)
    m_hat = nm * (1.0 / (1.0 - ADAM_B1 ** ADAM_STEP))
    v_hat = nv * (1.0 / (1.0 - ADAM_B2 ** ADAM_STEP))
    return -ADAM_LR * (m_hat / (jnp.sqrt(v_hat) + ADAM_EPS) + ADAM_WD * w), nm, nv


def _adamw(w, g, m, v, behind, name):
    l, r, c = w.shape
    tr = _row_tile(r, 256)
    blk = pl.BlockSpec((None, tr, c), lambda k, i: (k, i, 0))
    order = [] if behind is None else [behind]

    def body(w_ref, g_ref, m_ref, v_ref, *rest):
        d_ref, nm_ref, nv_ref = rest[-3:]
        d_ref[...], nm_ref[...], nv_ref[...] = _adamw_update(w_ref[...], g_ref[...], m_ref[...], v_ref[...])

    return pl.pallas_call(
        body, name=name, grid=(l, r // tr), in_specs=[blk] * 4 + [pl.BlockSpec(memory_space=pl.ANY)] * len(order),
        out_specs=[blk] * 3, out_shape=[jax.ShapeDtypeStruct(w.shape, F32)] * 3,
        compiler_params=_params("parallel", "parallel"),
    )(w, g, m, v, *order)


def _adamw_small(ws, gs, ms, vs, name):
    n = len(ws)
    two_d = lambda a: a.reshape(-1, a.shape[-1])

    def body(*refs):
        ins, outs = refs[:4 * n], refs[4 * n:]
        for a in range(n):
            outs[a][...], outs[n + a][...], outs[2 * n + a][...] = _adamw_update(*[ins[k * n + a][...] for k in range(4)])

    res = pl.pallas_call(
        body, name=name, out_shape=[jax.ShapeDtypeStruct(two_d(w).shape, F32) for w in ws] * 3,
    )(*[two_d(a) for a in (*ws, *gs, *ms, *vs)])
    return [[res[k * n + a].reshape(ws[a].shape) for a in range(n)] for k in range(3)]


def _mesh_pos():
    return lax.axis_index("x"), lax.axis_index("y"), lax.axis_index("c")


def _all_gather_vmem(x_shard, name):
    m_per, n = x_shard.shape

    def body(x_ref, out_ref, send_sems, recv_sems, local_sem):
        x, y, c = _mesh_pos()
        me, sibling = (x, y, c), (x, y, 1 - c)
        chips = [(1 - x, y), (x, 1 - y), (1 - x, 1 - y)]

        def rows(px, py, pc):
            return out_ref.at[pl.ds((4 * px + 2 * py + pc) * m_per, m_per), :]

        def copy(k, block, to, src=None):
            return pltpu.make_async_remote_copy(
                src_ref=rows(*block) if src is None else src, dst_ref=rows(*block),
                send_sem=send_sems.at[k], recv_sem=recv_sems.at[k], device_id=to, device_id_type=MESH)

        mine = pltpu.make_async_copy(x_ref, rows(*me), local_sem)
        mine.start()
        first = [copy(0, me, sibling, src=x_ref)]
        first += [copy(1 + j, me, (*chip, c), src=x_ref) for j, chip in enumerate(chips)]
        for cp in first:
            cp.start()
        passed = [copy(4 + j, (*chip, c), sibling) for j, chip in enumerate(chips)]
        for j, chip in enumerate(chips):
            copy(1 + j, (*chip, c), me).wait_recv()
            passed[j].start()
        copy(0, sibling, me).wait_recv()
        for j, chip in enumerate(chips):
            copy(4 + j, (*chip, 1 - c), me).wait_recv()
        for cp in first + passed:
            cp.wait_send()
        mine.wait()

    return pl.pallas_call(
        body, name=name, out_shape=jax.ShapeDtypeStruct((N_DEV * m_per, n), x_shard.dtype),
        in_specs=[pl.BlockSpec(memory_space=pltpu.VMEM)], out_specs=pl.BlockSpec(memory_space=pltpu.VMEM),
        scratch_shapes=[pltpu.SemaphoreType.DMA((7,)), pltpu.SemaphoreType.DMA((7,)), pltpu.SemaphoreType.DMA],
    )(x_shard)


def _all_gather_hbm(shards, name):
    n = len(shards)
    out_shape = [jax.ShapeDtypeStruct((N_DEV,) + s.shape, s.dtype) for s in shards]

    def body(*refs):
        x_refs, out_refs = refs[:n], refs[n:2 * n]
        send_sems, recv_sems, local_sems = refs[2 * n:]
        x, y, c = _mesh_pos()
        me, sibling = (x, y, c), (x, y, 1 - c)
        chips = [(1 - x, y), (x, 1 - y), (1 - x, 1 - y)]

        def blk(a, p):
            return out_refs[a].at[4 * p[0] + 2 * p[1] + p[2]]

        def copy(a, k, block, to, src=None):
            return pltpu.make_async_remote_copy(
                src_ref=blk(a, block) if src is None else src, dst_ref=blk(a, block),
                send_sem=send_sems.at[7 * a + k], recv_sem=recv_sems.at[7 * a + k], device_id=to, device_id_type=MESH)

        mine = [pltpu.make_async_copy(x_refs[a], blk(a, me), local_sems.at[a]) for a in range(n)]
        for cp in mine:
            cp.start()
        first = []
        for a in range(n):
            first.append(copy(a, 0, me, sibling, src=x_refs[a]))
            first += [copy(a, 1 + j, me, (*chip, c), src=x_refs[a]) for j, chip in enumerate(chips)]
        for cp in first:
            cp.start()
        passed = []
        for j, chip in enumerate(chips):
            for a in range(n):
                copy(a, 1 + j, (*chip, c), me).wait_recv()
                fwd = copy(a, 4 + j, (*chip, c), sibling)
                fwd.start()
                passed.append(fwd)
        for a in range(n):
            copy(a, 0, sibling, me).wait_recv()
            for j, chip in enumerate(chips):
                copy(a, 4 + j, (*chip, 1 - c), me).wait_recv()
        for cp in first + passed:
            cp.wait_send()
        for cp in mine:
            cp.wait()

    return pl.pallas_call(
        body, name=name, out_shape=out_shape, in_specs=[pl.BlockSpec(memory_space=pltpu.VMEM)] * n,
        out_specs=[pl.BlockSpec(memory_space=pl.ANY)] * n,
        scratch_shapes=[pltpu.SemaphoreType.DMA((7 * n,)), pltpu.SemaphoreType.DMA((7 * n,)), pltpu.SemaphoreType.DMA((n,))],
    )(*shards)


def _peers(x, y, c):
    flip = lambda v, f: 1 - v if f else v
    return [(flip(x, m & 4), flip(y, m & 2), flip(c, m & 1)) for m in range(1, N_DEV)]


def _dev_index(p):
    return 4 * p[0] + 2 * p[1] + p[2]


def _push_copies(src_refs, land_refs, send_sems, recv_sems, scatter, receive):
    x, y, c = _mesh_pos()
    me = _dev_index((x, y, c))
    copies = []
    for a, (src, land) in enumerate(zip(src_refs, land_refs)):
        for k, p in enumerate(_peers(x, y, c)):
            copies.append(pltpu.make_async_remote_copy(
                src_ref=src.at[_dev_index(p)] if scatter else src, dst_ref=land.at[_dev_index(p) if receive else me],
                send_sem=send_sems.at[7 * a + k], recv_sem=recv_sems.at[7 * a + k], device_id=p, device_id_type=MESH))
    return copies


_HBM = pl.BlockSpec(memory_space=pltpu.HBM)
_SEM = pl.BlockSpec(memory_space=pltpu.SEMAPHORE)
_EFFECT = pltpu.SideEffectType.DATAFLOW_SIDE_EFFECTING


def _pushes_start(srcs, lands, scatter, name):
    n = len(srcs)

    def body(*refs):
        src_refs, land_refs = refs[:n], refs[n:2 * n]
        send_sems, recv_sems = refs[2 * n], refs[2 * n + 1]
        token = refs[-1]
        for cp in _push_copies(src_refs, land_refs, send_sems, recv_sems, scatter, receive=False):
            cp.start()
        token[...] = jnp.zeros_like(token)

    hbm = lambda a: pltpu.HBM(a.shape, a.dtype)
    sems = pltpu.SemaphoreType.DMA((7 * n,))
    outs = pl.pallas_call(
        body, name=name,
        out_shape=(sems, sems, *[hbm(a) for a in srcs], *[hbm(a) for a in lands], jax.ShapeDtypeStruct((8, 128), F32)),
        in_specs=[_HBM] * (2 * n), out_specs=(_SEM, _SEM, *[_HBM] * (2 * n), pl.BlockSpec(memory_space=pltpu.VMEM)),
        input_output_aliases={i: 2 + i for i in range(2 * n)},
        compiler_params=pltpu.CompilerParams(has_side_effects=_EFFECT),
    )(*[pltpu.with_memory_space_constraint(a, pltpu.HBM) for a in (*srcs, *lands)])
    return (outs[0], outs[1], outs[2:2 + n], outs[2 + n:2 + 2 * n], scatter), outs[-1]


def _pushes_wait(handle, after, name):
    send_sems, recv_sems, srcs, lands, scatter = handle
    n = len(srcs)
    after = after if isinstance(after, (tuple, list)) else (after,)

    def body(*refs):
        src_refs, land_refs = refs[:n], refs[n:2 * n]
        for cp in _push_copies(src_refs, land_refs, refs[2 * n], refs[2 * n + 1], scatter, receive=True):
            cp.wait_send()
            cp.wait_recv()

    hbm = lambda a: pltpu.HBM(a.shape, a.dtype)
    outs = pl.pallas_call(
        body, name=name, out_shape=tuple(hbm(a) for a in (*srcs, *lands)),
        in_specs=[_HBM] * (2 * n) + [_SEM, _SEM] + [pl.BlockSpec(memory_space=pl.ANY)] * len(after),
        out_specs=tuple([_HBM] * (2 * n)), input_output_aliases={i: i for i in range(2 * n)},
        compiler_params=pltpu.CompilerParams(has_side_effects=_EFFECT),
    )(*srcs, *lands, send_sems, recv_sems, *after)
    return outs[:n], outs[n:]


def _landing_zones(srcs, behind, name):
    n, nb = len(srcs), len(behind)

    def body(*refs):
        src_refs, land_refs, bufs, sems = refs[:n], refs[n + nb:2 * n + nb], refs[2 * n + nb:3 * n + nb], refs[3 * n + nb]
        me = _dev_index(_mesh_pos())
        load = [pltpu.make_async_copy(src, buf, sems.at[a]) for a, (src, buf) in enumerate(zip(src_refs, bufs))]
        store = [pltpu.make_async_copy(buf, land.at[me], sems.at[a]) for a, (buf, land) in enumerate(zip(bufs, land_refs))]
        for cp in load:
            cp.start()
        for ld, st in zip(load, store):
            ld.wait()
            st.start()
        for cp in store:
            cp.wait()

    any_spec = pl.BlockSpec(memory_space=pl.ANY)
    return pl.pallas_call(
        body, name=name, out_shape=[jax.ShapeDtypeStruct((N_DEV,) + s.shape, s.dtype) for s in srcs],
        in_specs=[any_spec] * (n + nb), out_specs=[any_spec] * n,
        scratch_shapes=[pltpu.VMEM(s.shape, s.dtype) for s in srcs] + [pltpu.SemaphoreType.DMA((n,))],
        compiler_params=pltpu.CompilerParams(vmem_limit_bytes=V7X_VMEM_LIMIT),
    )(*srcs, *behind)


def _ffn_forward(x, mod, norm_g, w, tag):
    sh, sc, gate = mod
    h = _modnorm(x, norm_g, sc, sh, f"{tag}_norm")
    u = _ffn_up(h, w["up_t"], f"{tag}_up")
    act, z = _ffn_act(u, w["dw_w"], w["dw_b"], f"{tag}_act")
    y, x_new = _matmul(act, w["down"], "nn", BF16, f"{tag}_down", resid=(x, gate))
    return x_new, (x, h, u, z, act, y)


def _behind(row, token):
    return row if token is None else row + token[0:1, 0:1]


def _ffn_backward(dx_new, dy, d_gate, saved, mod, norm_g, w, tag, emit, below):
    x, h, u, z, act, _ = saved
    _, sc, _ = mod
    d_down = _matmul_tn_acc(act, dy, f"{tag}_down_dw")
    dact = _matmul(dy, w["down"], "nt", BF16, f"{tag}_down_dx")
    du, d_dw_w, d_dw_b = _ffn_act_bwd(u, z, dact, w["dw_w"], f"{tag}_act_bwd")
    d_up_t = _matmul_tn_acc(du, h, f"{tag}_up_dw").reshape(2 * FFN_DIM, -1)
    token = emit([d_up_t, d_down])
    dh = _ffn_up_dx(du, w["up_t"], f"{tag}_up_dx")
    dx, d_w, d_sh, *dy_below = _modnorm_bwd(x, dh, norm_g, _behind(sc, token), dx_new, below, f"{tag}_norm_bwd")
    return (dx, *dy_below), dict(dw_w=d_dw_w.transpose(1, 0, 2).reshape(FFN_CONV_WIDTH, 2 * FFN_DIM),
                    dw_b=d_dw_b.reshape(1, 2 * FFN_DIM), norm_g=d_w * (1.0 + sc), sh=d_sh, sc=d_w * norm_g, gate=d_gate)


def _mixer_forward(x, mod, norm_g, w, rope, tag):
    sh, sc, gate = mod
    h = _modnorm(x, norm_g, sc, sh, f"{tag}_norm")
    z = _matmul(h, w["w_in_t"], "nt", BF16, f"{tag}_in")
    ya = _gmlp_fwd(z, w["gain"], w["wtril"], w["bias_exp"], f"{tag}_gmlp")
    q, k, v = _qk_prep(z, rope[0], rope[1], w["gq"], w["gk"], w["seg"], f"{tag}_qk")
    outs, lses = zip(*[_attn_fwd(q[b], k[b], v[b], dil, f"{tag}_attn_d{dil}") for b, dil in enumerate(DILATIONS)])
    yb, lse, cat = _attn_merge(outs, lses, ya, f"{tag}_merge")
    y, x_new = _matmul(cat, w["w_out"], "nn", BF16, f"{tag}_out", resid=(x, gate))
    return x_new, (x, h, z, q, k, v, yb, lse, cat, y)


def _mixer_backward(dx_new, dy, d_gate, saved, mod, norm_g, w, rope, tag, emit, below):
    x, h, z, q, k, v, yb, lse, cat, _ = saved
    _, sc, _ = mod
    d_w_out = _matmul_tn_acc(cat, dy, f"{tag}_out_dw")
    dcat = _matmul(dy, w["w_out"], "nt", BF16, f"{tag}_out_dx")
    dz_a, d_sp_w, d_gain, d_bias_exp = _gmlp_bwd(z, dcat, w["gain"], w["wtril"], w["wtril_t"], w["bias_exp"], f"{tag}_gmlp_bwd")
    dyb = _subseq_views(dcat, A_WIDTH // B_WIDTH, f"{tag}_dyb_views")
    dqs, dks, dvs = zip(*[_attn_bwd(q[b], k[b], v[b], dyb[b], yb[b], lse[b], dil, f"{tag}_attn_bwd_d{dil}")
                          for b, dil in enumerate(DILATIONS)])
    dz_qkv, d_gq, d_gk = _qk_prep_bwd(z, dqs, dks, dvs, rope[0], rope[1], w["gq"], w["gk"], w["seg"], f"{tag}_qk_bwd")
    dz = jnp.concatenate([dz_a, dz_qkv], axis=1)
    d_w_in_t = _matmul_tn_acc(dz, h, f"{tag}_in_dw")
    token = emit([d_w_in_t, d_w_out])
    dh = _matmul(dz, w["w_in_t"], "nn", F32, f"{tag}_in_dx")
    dx, d_w, d_sh, *dy_below = _modnorm_bwd(x, dh, norm_g, _behind(sc, token), dx_new, below, f"{tag}_norm_bwd")
    return (dx, *dy_below), dict(
        vnorm_g=d_gain.reshape(A_GROUPS, GROUP_DIM), spatial_w=d_sp_w,
        spatial_b=d_bias_exp.reshape(CHUNK, A_GROUPS, GROUP_DIM).sum(-1).T,
        q_norm_g=d_gq.reshape(HEADS, HEAD_DIM).sum(0), k_norm_g=d_gk.reshape(HEADS, HEAD_DIM).sum(0),
        norm_g=d_w * (1.0 + sc), sh=d_sh, sc=d_w * norm_g, gate=d_gate)


def _conformer_forward(x, mod, norm_g, w, tag):
    sh, sc, gate = mod
    h = _modnorm(x, norm_g, sc, sh, f"{tag}_norm")
    p = _matmul(h, w["pw1_t"], "nt", BF16, f"{tag}_pw1", bias=w["pw1_b"])
    s, dc = _conformer_mid(p, w["dw_w"], w["dw_b"], w["ln_g"], w["ln_b"], f"{tag}_mid")
    y, x_new = _matmul(s, w["pw2"], "nn", BF16, f"{tag}_pw2", bias=w["pw2_b"], resid=(x, gate))
    return x_new, (x, h, p, dc, s, y)


def _conformer_backward(dx_new, dy, d_gate, saved, mod, norm_g, w, tag, emit, below):
    x, h, p, dc, s, _ = saved
    _, sc, _ = mod
    d_pw2 = _matmul_tn_acc(s, dy, f"{tag}_pw2_dw")
    d_pw2_b = _colsum_call(dy, f"{tag}_pw2_db")
    ds = _matmul(dy, w["pw2"], "nt", BF16, f"{tag}_pw2_dx")
    ddc, d_dw_w, d_dw_b, d_ln_g, d_ln_b = _conformer_mid_bwd(p, dc, ds, w["ln_g"], w["ln_b"], f"{tag}_mid_bwd")
    dp, d_pw1_b = _conformer_glu_bwd(p, ddc, w["dw_w"], f"{tag}_glu_bwd")
    d_pw1_t = _matmul_tn_acc(dp, h, f"{tag}_pw1_dw")
    token = emit([d_pw1_t, d_pw2])
    dh = _matmul(dp, w["pw1_t"], "nn", F32, f"{tag}_pw1_dx")
    dx, d_w, d_sh, *dy_below = _modnorm_bwd(x, dh, norm_g, _behind(sc, token), dx_new, below, f"{tag}_norm_bwd")
    return (dx, *dy_below), dict(pw1_b=d_pw1_b, dw_w=d_dw_w, dw_b=d_dw_b, ln_g=d_ln_g, ln_b=d_ln_b, pw2_b=d_pw2_b, norm_g=d_w * (1.0 + sc), sh=d_sh, sc=d_w * norm_g, gate=d_gate)


def _local_step(x, target, pos, mod, norm_mix_g, norm_ffn_g, mixer_w, conv_w, ffn_w, fetch, emit):
    d = D_MODEL
    inv_freq = 1.0 / (ROPE_THETA ** (jnp.arange(0, HEAD_DIM, 2, dtype=F32) / HEAD_DIM))
    inv_freq = jnp.tile(inv_freq, 2 * HEADS)[None, :]
    sign = jnp.tile(jnp.concatenate([-jnp.ones(HEAD_DIM // 2, F32), jnp.ones(HEAD_DIM // 2, F32)]), HEADS)[None, :]
    rope = _rope_tables(pos, inv_freq, sign, "rope_tables")
    mods = [[mod[l:l + 1, i * d:(i + 1) * d] for i in range(6)] for l in range(2)]
    mix = [(m[0], m[1], m[2]) for m in mods]
    ffn = [(m[3], m[4], m[5]) for m in mods]
    gm = [norm_mix_g[l:l + 1] for l in range(2)]
    gf = [norm_ffn_g[l:l + 1] for l in range(2)]

    mixer_w = {**mixer_w, **fetch("l0_mix", x)}
    x1, s_mix = _mixer_forward(x, mix[0], gm[0], mixer_w, rope, "l0_mix")
    ffn_w0 = {**ffn_w[0], **fetch("l0_ffn", x1)}
    x2, s_ffn0 = _ffn_forward(x1, ffn[0], gf[0], ffn_w0, "l0_ffn")
    conv_w = {**conv_w, **fetch("l1_conv", x2)}
    x3, s_conv = _conformer_forward(x2, mix[1], gm[1], conv_w, "l1_conv")
    ffn_w1 = {**ffn_w[1], **fetch("l1_ffn", x3)}
    x4, s_ffn1 = _ffn_forward(x3, ffn[1], gf[1], ffn_w1, "l1_ffn")
    below = lambda saved, m: (saved[-1], m[2])
    dx, loss, dy, dg = _loss_head(x4, target, below(s_ffn1, ffn[1]), "loss_head")
    (dx, dy, dg), g_ffn1 = _ffn_backward(dx, dy, dg, s_ffn1, ffn[1], gf[1], ffn_w1, "l1_ffn",
                                         functools.partial(emit, "l1_ffn"), below(s_conv, mix[1]))
    (dx, dy, dg), g_conv = _conformer_backward(dx, dy, dg, s_conv, mix[1], gm[1], conv_w, "l1_conv",
                                               functools.partial(emit, "l1_conv"), below(s_ffn0, ffn[0]))
    (dx, dy, dg), g_ffn0 = _ffn_backward(dx, dy, dg, s_ffn0, ffn[0], gf[0], ffn_w0, "l0_ffn",
                                         functools.partial(emit, "l0_ffn"), below(s_mix, mix[0]))
    (dx,), g_mix = _mixer_backward(dx, dy, dg, s_mix, mix[0], gm[0], mixer_w, rope, "l0_mix",
                                   functools.partial(emit, "l0_mix"), None)
    blocks = [g_mix, g_ffn0, g_conv, g_ffn1]
    dmod = jnp.stack([jnp.concatenate([a["sh"], a["sc"], a["gate"], b["sh"], b["sc"], b["gate"]], axis=1)[0]
                      for a, b in ((g_mix, g_ffn0), (g_conv, g_ffn1))])
    return loss, dx, dmod, blocks


def _pack(arrs, rows=8):
    flat = jnp.concatenate([a.reshape(-1).astype(F32) for a in arrs])
    n = flat.shape[0]
    cols = -(-n // (rows * 128)) * 128
    return jnp.pad(flat, (0, rows * cols - n)).reshape(rows, cols)


def _unpack(flat, shapes):
    out, off = [], 0
    for shp in shapes:
        n = math.prod(shp)
        out.append(flat[..., off:off + n].reshape(flat.shape[:-1] + tuple(shp)))
        off += n
    return out


def _take_block(a, idx, size, axis):
    return lax.dynamic_slice_in_dim(a, idx * size, size, axis)


def kernel(x, c, positions, ada_w, ada_b, norm_mix_g, norm_ffn_g, ab_w_in, a_vnorm_g, a_spatial_w, a_spatial_b, b_q_norm_g, b_k_norm_g, ab_w_out, conv_pw1_w, conv_pw1_b, conv_dw_w, conv_dw_b, conv_ln_g, conv_ln_b, conv_pw2_w, conv_pw2_b, ffn_up_w, ffn_dw_w, ffn_dw_b, ffn_down_w, loss_target, m_ada_w, m_ada_b, m_norm_mix_g, m_norm_ffn_g, m_ab_w_in, m_a_vnorm_g, m_a_spatial_w, m_a_spatial_b, m_b_q_norm_g, m_b_k_norm_g, m_ab_w_out, m_conv_pw1_w, m_conv_pw1_b, m_conv_dw_w, m_conv_dw_b, m_conv_ln_g, m_conv_ln_b, m_conv_pw2_w, m_conv_pw2_b, m_ffn_up_w, m_ffn_dw_w, m_ffn_dw_b, m_ffn_down_w, v_ada_w, v_ada_b, v_norm_mix_g, v_norm_ffn_g, v_ab_w_in, v_a_vnorm_g, v_a_spatial_w, v_a_spatial_b, v_b_q_norm_g, v_b_k_norm_g, v_ab_w_out, v_conv_pw1_w, v_conv_pw1_b, v_conv_dw_w, v_conv_dw_b, v_conv_ln_g, v_conv_ln_b, v_conv_pw2_w, v_conv_pw2_b, v_ffn_up_w, v_ffn_dw_w, v_ffn_dw_b, v_ffn_down_w):
    weights = dict(ada_w=ada_w, ada_b=ada_b, norm_mix_g=norm_mix_g, norm_ffn_g=norm_ffn_g, ab_w_in=ab_w_in, a_vnorm_g=a_vnorm_g, a_spatial_w=a_spatial_w, a_spatial_b=a_spatial_b, b_q_norm_g=b_q_norm_g, b_k_norm_g=b_k_norm_g, ab_w_out=ab_w_out, conv_pw1_w=conv_pw1_w, conv_pw1_b=conv_pw1_b, conv_dw_w=conv_dw_w, conv_dw_b=conv_dw_b, conv_ln_g=conv_ln_g, conv_ln_b=conv_ln_b, conv_pw2_w=conv_pw2_w, conv_pw2_b=conv_pw2_b, ffn_up_w=ffn_up_w, ffn_dw_w=ffn_dw_w, ffn_dw_b=ffn_dw_b, ffn_down_w=ffn_down_w)
    mom1 = dict(ada_w=m_ada_w, ada_b=m_ada_b, norm_mix_g=m_norm_mix_g, norm_ffn_g=m_norm_ffn_g, ab_w_in=m_ab_w_in, a_vnorm_g=m_a_vnorm_g, a_spatial_w=m_a_spatial_w, a_spatial_b=m_a_spatial_b, b_q_norm_g=m_b_q_norm_g, b_k_norm_g=m_b_k_norm_g, ab_w_out=m_ab_w_out, conv_pw1_w=m_conv_pw1_w, conv_pw1_b=m_conv_pw1_b, conv_dw_w=m_conv_dw_w, conv_dw_b=m_conv_dw_b, conv_ln_g=m_conv_ln_g, conv_ln_b=m_conv_ln_b, conv_pw2_w=m_conv_pw2_w, conv_pw2_b=m_conv_pw2_b, ffn_up_w=m_ffn_up_w, ffn_dw_w=m_ffn_dw_w, ffn_dw_b=m_ffn_dw_b, ffn_down_w=m_ffn_down_w)
    mom2 = dict(ada_w=v_ada_w, ada_b=v_ada_b, norm_mix_g=v_norm_mix_g, norm_ffn_g=v_norm_ffn_g, ab_w_in=v_ab_w_in, a_vnorm_g=v_a_vnorm_g, a_spatial_w=v_a_spatial_w, a_spatial_b=v_a_spatial_b, b_q_norm_g=v_b_q_norm_g, b_k_norm_g=v_b_k_norm_g, ab_w_out=v_ab_w_out, conv_pw1_w=v_conv_pw1_w, conv_pw1_b=v_conv_pw1_b, conv_dw_w=v_conv_dw_w, conv_dw_b=v_conv_dw_b, conv_ln_g=v_conv_ln_g, conv_ln_b=v_conv_ln_b, conv_pw2_w=v_conv_pw2_w, conv_pw2_b=v_conv_pw2_b, ffn_up_w=v_ffn_up_w, ffn_dw_w=v_ffn_dw_w, ffn_dw_b=v_ffn_dw_b, ffn_down_w=v_ffn_down_w)
    order = list(weights)
    d, f2 = D_MODEL, 2 * FFN_DIM
    t = x.shape[1]
    me = 4 * lax.axis_index("x") + 2 * lax.axis_index("y") + lax.axis_index("c")
    for window, dil in PATTERNS:
        assert window // dil == Q_BLOCK and t % (dil * Q_BLOCK) == 0

    small_in = [c[0], conv_pw1_b[0], conv_dw_w[0], conv_dw_b[0], conv_ln_g[0], conv_ln_b[0], conv_pw2_b[0], ffn_dw_w]
    g1 = _all_gather_vmem(_pack(small_in, rows=8), "gather_small").reshape(N_DEV, -1)
    c_all, pw1_b, dw_w, dw_b, ln_g, ln_b, pw2_b, fdw_w = _unpack(g1, [a.shape for a in small_in])
    pw1_b, dw_b, ln_g, ln_b, pw2_b = [a.reshape(1, -1) for a in (pw1_b, dw_b, ln_g, ln_b, pw2_b)]
    dw_w = dw_w.transpose(1, 0, 2).reshape(CONV_WIDTH, d)
    fdw_w = fdw_w.transpose(1, 2, 0, 3).reshape(2, FFN_CONV_WIDTH, f2)

    c16 = jnp.pad(c_all, ((0, 2 * N_DEV - c_all.shape[0]), (0, 0)))
    part = jnp.concatenate([_ada_fwd(c16, ada_w[l], f"ada_fwd{l}")[:N_DEV] for l in range(2)], axis=1)
    g2 = _all_gather_vmem(part, "gather_mod").reshape(N_DEV, N_DEV, 2, -1)
    mod = lax.dynamic_index_in_dim(g2, me, axis=1, keepdims=False).transpose(1, 0, 2).reshape(2, 6 * d) + ada_b

    stages = dict(l0_mix=[ab_w_in[0].T, ab_w_out[0]], l0_ffn=[ffn_up_w[0].T, ffn_down_w[0]],
                  l1_conv=[conv_pw1_w[0].T, conv_pw2_w[0]], l1_ffn=[ffn_up_w[1].T, ffn_down_w[1]])
    stages = {k: [s.astype(BF16) for s in v] for k, v in stages.items()}
    names = dict(l0_mix=("w_in_t", "w_out"), l0_ffn=("up_t", "down"), l1_conv=("pw1_t", "pw2"), l1_ffn=("up_t", "down"))
    ready = {"l0_mix": [a.reshape(-1, d) for a in _all_gather_hbm(stages["l0_mix"], "gather_mixer_weights")]}
    behind = (*ready["l0_mix"], mod)
    arriving = {}
    for stage, group in (("l0_ffn", ("l0_ffn",)), ("l1_conv", ("l1_conv", "l1_ffn"))):
        srcs = [s for g in group for s in stages[g]]
        arriving[stage], token = _pushes_start(
            srcs, _landing_zones(srcs, behind, f"gather_{stage}_zones"), False, f"gather_{stage}_start")
        behind = (token,)
        mod = mod + token[0:1, 0:1]

    def fetch(stage, after):
        if stage in arriving:
            full = [a.reshape(-1, d) for a in _pushes_wait(arriving[stage], after, f"gather_{stage}_wait")[1]]
            ready[stage] = full[:2]
            if stage == "l1_conv":
                ready["l1_ffn"] = full[2:]
        return dict(zip(names[stage], ready[stage]))

    causal = jnp.tril(jnp.ones((CHUNK, CHUNK), bool))
    wtril = jnp.where(causal[None], a_spatial_w[0], 0.0)
    mixer_w = dict(
        gain=a_vnorm_g[0].reshape(1, A_WIDTH), wtril=wtril.astype(BF16),
        wtril_t=wtril.transpose(0, 2, 1).astype(BF16),
        bias_exp=jnp.repeat(a_spatial_b[0].T, GROUP_DIM, axis=1),
        gq=jnp.tile(b_q_norm_g[0], HEADS)[None, :], gk=jnp.tile(b_k_norm_g[0], HEADS)[None, :],
        seg=jnp.kron(jnp.eye(HEADS, dtype=BF16), jnp.ones((HEAD_DIM, HEAD_DIM), BF16)))
    conv_w = dict(pw1_b=pw1_b, dw_w=dw_w, dw_b=dw_b, ln_g=ln_g, ln_b=ln_b, pw2_b=pw2_b)
    ffn_w = [dict(dw_w=fdw_w[l].reshape(FFN_CONV_WIDTH, 2, FFN_DIM).transpose(1, 0, 2), dw_b=ffn_dw_b[l].reshape(2, 1, FFN_DIM))
             for l in range(2)]

    leaving = {}

    def emit(stage, grads):
        blocks = [g.reshape(N_DEV, g.shape[0] // N_DEV, d) for g in grads]
        leaving[stage], token = _pushes_start(
            blocks, [lax.empty(b.shape, b.dtype) for b in blocks], True, f"reduce_{stage}_start")
        return token

    loss, dx, dmod, (g_mix, g_ffn0, g_conv, g_ffn1) = _local_step(
        x[0], loss_target[0], positions[0].astype(F32)[:, None], mod, norm_mix_g, norm_ffn_g, mixer_w, conv_w, ffn_w,
        fetch, emit)

    me_op = me.astype(jnp.int32).reshape(1)

    def reduced(stage, after, layer=0, layers=1, into=(None, None)):
        blocks, lands = _pushes_wait(leaving[stage], after, f"reduce_{stage}_wait")
        return [_sum_with_own(b, a, me_op, f"reduce_{stage}_sum{i}", layer, layers, into[i])
                for i, (b, a) in enumerate(zip(blocks, lands))]

    r_ffn = reduced("l1_ffn", dx, 1, 2)
    r_pw1_t, r_pw2 = reduced("l1_conv", dx)
    r_up_t, r_down = reduced("l0_ffn", dx, 0, 2, r_ffn)

    small_g = [
        dmod, jnp.concatenate([g_mix["norm_g"], g_conv["norm_g"]]), jnp.concatenate([g_ffn0["norm_g"], g_ffn1["norm_g"]]),
        g_mix["vnorm_g"], g_mix["spatial_w"], g_mix["spatial_b"], g_mix["q_norm_g"], g_mix["k_norm_g"],
        g_conv["pw1_b"], g_conv["dw_w"], g_conv["dw_b"], g_conv["ln_g"], g_conv["ln_b"], g_conv["pw2_b"],
        jnp.stack([g_ffn0["dw_w"], g_ffn1["dw_w"]]), jnp.concatenate([g_ffn0["dw_b"], g_ffn1["dw_b"]]), loss]
    packed = _pack(small_g, rows=8)
    small_leaving, token = _pushes_start([packed], [lax.empty((N_DEV,) + packed.shape, F32)], False, "gather_small_grads_start")

    grads = dict(conv_pw2_w=r_pw2, ffn_down_w=r_down)
    grads_t = dict(conv_pw1_w=r_pw1_t, ffn_up_w=r_up_t)
    flip = lambda a: jnp.swapaxes(a, 1, 2)
    delta, new_m, new_v = {}, {}, {}

    def update(name, behind):
        if name in grads_t:
            grads[name] = flip(grads_t[name])
            res = _adamw(flip(weights[name]), grads_t[name], flip(mom1[name]), flip(mom2[name]), behind, f"adamw_{name}")
            delta[name], new_m[name], new_v[name] = [flip(r) for r in res]
        else:
            delta[name], new_m[name], new_v[name] = _adamw(
                weights[name], grads[name], mom1[name], mom2[name], behind, f"adamw_{name}")

    for name in ("conv_pw1_w", "conv_pw2_w", "ffn_up_w", "ffn_down_w"):
        update(name, token)
    r_in_t, r_out = reduced("l0_mix", new_v["ffn_down_w"])
    grads_t["ab_w_in"], grads["ab_w_out"] = r_in_t, r_out
    update("ab_w_in", token)
    update("ab_w_out", token)

    (packed,), (landed,) = _pushes_wait(small_leaving, tuple(new_v.values()), "gather_small_grads_wait")
    total = _sum_in_device_order(packed, landed, me_op, "sum_small_grads")
    (s_dmod, s_mix_g, s_ffn_g, s_vnorm, s_sp_w, s_sp_b, s_gq, s_gk, s_pw1_b, s_dw_w, s_dw_b, s_ln_g, s_ln_b,
     s_pw2_b, s_fdw_w, s_fdw_b, s_loss) = _unpack(total.reshape(-1), [a.shape for a in small_g])
    dmod_all = lax.dynamic_update_slice(
        landed.reshape(N_DEV, -1)[:, :dmod.size].reshape((N_DEV,) + dmod.shape), dmod[None], (me, 0, 0))
    n_ada = ada_w.shape[2]
    dmod16 = jnp.pad(_take_block(dmod_all, me, n_ada, 2), ((0, N_DEV), (0, 0), (0, 0)))
    grads.update(
        ada_w=_ada_bwd(c16, dmod16.transpose(1, 0, 2), "ada_bwd"),
        ada_b=s_dmod, norm_mix_g=s_mix_g, norm_ffn_g=s_ffn_g,
        a_vnorm_g=s_vnorm[None], a_spatial_w=s_sp_w[None], a_spatial_b=s_sp_b[None], b_q_norm_g=s_gq[None],
        b_k_norm_g=s_gk[None],
        conv_pw1_b=_take_block(s_pw1_b, me, conv_pw1_b.shape[1], 1),
        conv_dw_w=_take_block(s_dw_w, me, conv_dw_w.shape[2], 1)[None],
        conv_dw_b=_take_block(s_dw_b, me, conv_dw_b.shape[1], 1), conv_ln_g=_take_block(s_ln_g, me, conv_ln_g.shape[1], 1),
        conv_ln_b=_take_block(s_ln_b, me, conv_ln_b.shape[1], 1),
        conv_pw2_b=_take_block(s_pw2_b, me, conv_pw2_b.shape[1], 1),
        ffn_dw_w=_take_block(s_fdw_w, me, ffn_dw_w.shape[2], 2), ffn_dw_b=s_fdw_b)
    update("ada_w", None)
    large = ("ada_w", "conv_pw1_w", "conv_pw2_w", "ffn_up_w", "ffn_down_w", "ab_w_in", "ab_w_out")
    small = [n for n in order if n not in large]
    res = _adamw_small(*[[src[n] for n in small] for src in (weights, grads, mom1, mom2)], "adamw_small")
    for dst, arrs in zip((delta, new_m, new_v), res):
        dst.update(zip(small, arrs))

    return (s_loss[0, 0], dx[None], *[grads[n] for n in order], *[delta[n] for n in order],
            *[new_m[n] for n in order], *[new_v[n] for n in order])
```

```python
import functools
import math

import jax
import jax.numpy as jnp
from jax import lax
from jax.experimental import pallas as pl
from jax.experimental.pallas import tpu as pltpu

F32 = jnp.float32
BF16 = jnp.bfloat16
MESH = pl.DeviceIdType.MESH

D_MODEL = 1024
A_WIDTH = 512
A_GROUPS = 4
GROUP_DIM = 128
CHUNK = 128
B_WIDTH = 512
HEADS = 8
HEAD_DIM = 64
PATTERNS = ((128, 1), (512, 4), (2048, 16))
Q_BLOCK = 128
ROPE_THETA = 10000.0
AB_IN = 2560
CONV_WIDTH = 31
FFN_DIM = 2816
FFN_CONV_WIDTH = 3
EPS = 1e-6
NEG = -1e30
N_DEV = 8
ADAM_LR, ADAM_B1, ADAM_B2, ADAM_EPS, ADAM_WD, ADAM_STEP = 0.001, 0.9, 0.999, 1e-08, 0.01, 10

V7X_VMEM_LIMIT = 56 * 2**20
FFN_HALO = 16
CONV_HALO = 32

_NN = (((1,), (0,)), ((), ()))
_NT = (((1,), (1,)), ((), ()))
_TN = (((0,), (0,)), ((), ()))


def _tile(n, prefs=(512, 256, 128)):
    for t in prefs:
        if n % t == 0:
            return t
    return n


def _row_tile(n, cap=512):
    best = n
    for t in range(8, min(n, cap) + 1, 8):
        if n % t == 0:
            best = t
    return best if best <= cap else n


def _params(*sem):
    return pltpu.CompilerParams(dimension_semantics=sem, vmem_limit_bytes=V7X_VMEM_LIMIT)


def _dot(a, b, dims):
    return lax.dot_general(a, b, dims, preferred_element_type=F32)


def _sigmoid(x):
    return 1.0 / (1.0 + jnp.exp(-x))


def _gelu(x):
    return 0.5 * x * (1.0 + lax.erf(x * (2.0 ** -0.5)))


def _gelu_grad(x):
    return 0.5 * (1.0 + lax.erf(x * (2.0 ** -0.5))) + x * jnp.exp(-0.5 * x * x) * (1.0 / math.sqrt(2.0 * math.pi))


def _colsum(v):
    return jnp.sum(v, axis=0, keepdims=True)


MATMUL_VMEM_BUDGET = 40 * 2**20


def _matmul_tiles(m, n, k, out_bytes, with_resid):
    def options(dim):
        opts = [t for t in (1024, 512, 256, 128) if dim % t == 0]
        return opts + [dim] if dim <= 4096 and dim not in opts else opts

    best = None
    for tm in options(m):
        for tn in options(n):
            need = 4 * (tm * k + k * tn) + tm * tn * (4 + 2 * out_bytes) + (24 * tm * tn if with_resid else 0)
            if need <= MATMUL_VMEM_BUDGET and (best is None or tm * tn / (tm + tn) > best[0]):
                best = (tm * tn / (tm + tn), tm, tn)
    return best[1], best[2]


def _matmul_tn_acc(a, b, name, tk=1024):
    squeeze = a.ndim == 2
    a3 = a[None] if squeeze else a
    p_, t, m = a3.shape
    n = b.shape[1]
    nk = t // tk

    def body(a_ref, b_ref, o_ref, acc_ref):
        kt = pl.program_id(1)

        @pl.when(kt == 0)
        def _():
            acc_ref[...] = jnp.zeros_like(acc_ref)

        acc_ref[...] += _dot(a_ref[...], b_ref[...], _TN)

        @pl.when(kt == nk - 1)
        def _():
            o_ref[...] = acc_ref[...].astype(BF16)

    out = pl.pallas_call(
        body, name=name, grid=(p_, nk),
        in_specs=[pl.BlockSpec((None, tk, m), lambda p, kt: (p, kt, 0)), pl.BlockSpec((tk, n), lambda p, kt: (kt, 0))],
        out_specs=pl.BlockSpec((None, m, n), lambda p, kt: (p, 0, 0)), out_shape=jax.ShapeDtypeStruct((p_, m, n), BF16),
        scratch_shapes=[pltpu.VMEM((m, n), F32)], compiler_params=_params("parallel", "arbitrary"),
    )(a3, b)
    return out[0] if squeeze else out


def _matmul(a, b, mode, out_dtype, name, bias=None, resid=None):
    if mode == "nn":
        (m, k), (_, n) = a.shape, b.shape
    elif mode == "nt":
        (m, k), (n, _) = a.shape, b.shape
    else:
        (k, m), (_, n) = a.shape, b.shape
    tm, tn = _matmul_tiles(m, n, k, jnp.dtype(out_dtype).itemsize, resid is not None)
    dims = {"nn": _NN, "nt": _NT, "tn": _TN}[mode]
    a_spec = pl.BlockSpec((k, tm), lambda i, j: (0, i)) if mode == "tn" else pl.BlockSpec((tm, k), lambda i, j: (i, 0))
    b_spec = pl.BlockSpec((tn, k), lambda i, j: (j, 0)) if mode == "nt" else pl.BlockSpec((k, tn), lambda i, j: (0, j))
    in_specs, args = [a_spec, b_spec], [a, b]
    row_spec = pl.BlockSpec((1, tn), lambda i, j: (0, j))
    tile_spec = pl.BlockSpec((tm, tn), lambda i, j: (i, j))
    if bias is not None:
        in_specs.append(row_spec)
        args.append(bias)
    if resid is not None:
        in_specs += [tile_spec, row_spec]
        args += list(resid)
    out_shape = [jax.ShapeDtypeStruct((m, n), out_dtype)]
    out_specs = [tile_spec]
    if resid is not None:
        out_shape.append(jax.ShapeDtypeStruct((m, n), F32))
        out_specs.append(tile_spec)

    def body(*refs):
        a_ref, b_ref = refs[0], refs[1]
        pos = 2
        acc = _dot(a_ref[...], b_ref[...], dims)
        if bias is not None:
            acc = acc + refs[pos][...]
            pos += 1
        if resid is not None:
            x_ref, g_ref = refs[pos], refs[pos + 1]
            pos += 2
        refs[pos][...] = acc.astype(out_dtype)
        if resid is not None:
            refs[pos + 1][...] = x_ref[...] + g_ref[...] * acc

    outs = pl.pallas_call(
        body, name=name, grid=(m // tm, n // tn), in_specs=in_specs, out_specs=out_specs, out_shape=out_shape,
        compiler_params=_params("parallel", "parallel"),
    )(*args)
    return outs if resid is not None else outs[0]


NORM_TM = (1024, 512, 256, 128)


def _modnorm(x, g, sc, sh, name):
    t, d = x.shape
    tm = _tile(t, NORM_TM)
    row = pl.BlockSpec((1, d), lambda i: (0, 0))
    blk = pl.BlockSpec((tm, d), lambda i: (i, 0))

    def body(x_ref, g_ref, sc_ref, sh_ref, o_ref):
        x = x_ref[...]
        r = lax.rsqrt(jnp.mean(x * x, axis=-1, keepdims=True) + EPS)
        o_ref[...] = ((x * r) * g_ref[...] * (1.0 + sc_ref[...]) + sh_ref[...]).astype(BF16)

    return pl.pallas_call(
        body, name=name, grid=(t // tm,), in_specs=[blk, row, row, row], out_specs=blk,
        out_shape=jax.ShapeDtypeStruct((t, d), BF16), compiler_params=_params("parallel"),
    )(x, g, sc, sh)


def _gate_bwd_tile(dx, y_ref, gate_ref, dy_ref, dgate_ref, first):
    @pl.when(first)
    def _():
        dgate_ref[...] = jnp.zeros_like(dgate_ref)

    dy_ref[...] = (dx * gate_ref[...]).astype(BF16)
    dgate_ref[...] += _colsum(dx * y_ref[...].astype(F32))


def _modnorm_bwd(x, dh, g, sc, dres, below, name):
    t, d = x.shape
    tm = _tile(t, NORM_TM)
    row = pl.BlockSpec((1, d), lambda i: (0, 0))
    blk = pl.BlockSpec((tm, d), lambda i: (i, 0))

    def body(x_ref, dh_ref, g_ref, sc_ref, dres_ref, *rest):
        dx_ref, dw_ref, dsh_ref = rest[-5:-2] if below else rest
        first = pl.program_id(0) == 0

        @pl.when(first)
        def _():
            dw_ref[...] = jnp.zeros_like(dw_ref)
            dsh_ref[...] = jnp.zeros_like(dsh_ref)

        x = x_ref[...]
        dh = dh_ref[...].astype(F32)
        r = lax.rsqrt(jnp.mean(x * x, axis=-1, keepdims=True) + EPS)
        xn = x * r
        dxn = dh * (g_ref[...] * (1.0 + sc_ref[...]))
        dx = dres_ref[...] + r * (dxn - xn * jnp.mean(dxn * xn, axis=-1, keepdims=True))
        dx_ref[...] = dx
        dw_ref[...] += _colsum(dh * xn)
        dsh_ref[...] += _colsum(dh)
        if below:
            _gate_bwd_tile(dx, rest[0], rest[1], rest[-2], rest[-1], first)

    row_out = jax.ShapeDtypeStruct((1, d), F32)
    return pl.pallas_call(
        body, name=name, grid=(t // tm,), in_specs=[blk, blk, row, row, blk] + ([blk, row] if below else []),
        out_specs=[blk, row, row] + ([blk, row] if below else []),
        out_shape=[jax.ShapeDtypeStruct((t, d), F32), row_out, row_out]
        + ([jax.ShapeDtypeStruct((t, d), BF16), row_out] if below else []),
        compiler_params=_params("arbitrary"),
    )(x, dh, g, sc, dres, *(below or ()))


def _loss_head(y, target, below, name):
    t, d = y.shape
    tm = _tile(t, NORM_TM)
    blk = pl.BlockSpec((tm, d), lambda i: (i, 0))
    row = pl.BlockSpec((1, d), lambda i: (0, 0))
    one = pl.BlockSpec((1, 1), lambda i: (0, 0))
    steps = t // tm

    def body(y_ref, t_ref, yb_ref, gate_ref, dx_ref, loss_ref, dy_ref, dgate_ref, acc_ref):
        first = pl.program_id(0) == 0

        @pl.when(first)
        def _():
            acc_ref[...] = jnp.zeros_like(acc_ref)

        e = y_ref[...] - t_ref[...]
        dx = e * (1.0 / d)
        dx_ref[...] = dx
        acc_ref[...] += _colsum(e * e)
        _gate_bwd_tile(dx, yb_ref, gate_ref, dy_ref, dgate_ref, first)

        @pl.when(pl.program_id(0) == steps - 1)
        def _():
            loss_ref[...] = jnp.sum(acc_ref[...], axis=1, keepdims=True) * (0.5 / d)

    return pl.pallas_call(
        body, name=name, grid=(steps,), in_specs=[blk, blk, blk, row], out_specs=[blk, one, blk, row],
        out_shape=[jax.ShapeDtypeStruct((t, d), F32), jax.ShapeDtypeStruct((1, 1), F32),
                   jax.ShapeDtypeStruct((t, d), BF16), jax.ShapeDtypeStruct((1, d), F32)],
        scratch_shapes=[pltpu.VMEM((1, d), F32)], compiler_params=_params("arbitrary"),
    )(y, target, *below)


GMLP_TM = 512


def _group_norm(vg, gain):
    mu = jnp.mean(vg, axis=-1, keepdims=True)
    xc = vg - mu
    rstd = lax.rsqrt(jnp.mean(xc * xc, axis=-1, keepdims=True) + EPS)
    xhat = xc * rstd
    return xhat, rstd, xhat * gain


def _gmlp_fwd(z, gain, wtril, bias_exp, name):
    t = z.shape[0]
    tm = _tile(t, (GMLP_TM,))
    zu = pl.BlockSpec((tm, A_WIDTH), lambda i: (i, 0))
    zv = pl.BlockSpec((tm, A_WIDTH), lambda i: (i, 1))
    full2 = lambda shp: pl.BlockSpec(shp, lambda i: (0, 0))
    w_spec = pl.BlockSpec((A_GROUPS, CHUNK, CHUNK), lambda i: (0, 0, 0))

    def body(zu_ref, zv_ref, gain_ref, w_ref, b_ref, ya_ref):
        for c in range(tm // CHUNK):
            rows = slice(c * CHUNK, (c + 1) * CHUNK)
            ua = _gelu(zu_ref[rows, :].astype(F32))
            vg = _gelu(zv_ref[rows, :].astype(F32))
            for g in range(A_GROUPS):
                sl = slice(g * GROUP_DIM, (g + 1) * GROUP_DIM)
                _, _, vn = _group_norm(vg[:, sl], gain_ref[:, sl])
                f = _dot(w_ref[g], vn.astype(BF16), _NN) + b_ref[:, sl]
                ya_ref[rows, sl] = (ua[:, sl] * f).astype(BF16)

    return pl.pallas_call(
        body, name=name, grid=(t // tm,),
        in_specs=[zu, zv, full2((1, A_WIDTH)), w_spec, full2((CHUNK, A_WIDTH))], out_specs=zu,
        out_shape=jax.ShapeDtypeStruct((t, A_WIDTH + B_WIDTH), BF16), compiler_params=_params("parallel"),
    )(z, z, gain, wtril, bias_exp)


def _gmlp_bwd(z, dcat, gain, wtril, wtril_t, bias_exp, dz, name):
    t = z.shape[0]
    tm = _tile(t, (GMLP_TM,))
    zu = pl.BlockSpec((tm, A_WIDTH), lambda i: (i, 0))
    zv = pl.BlockSpec((tm, A_WIDTH), lambda i: (i, 1))
    full2 = lambda shp: pl.BlockSpec(shp, lambda i: (0, 0))
    w_spec = pl.BlockSpec((A_GROUPS, CHUNK, CHUNK), lambda i: (0, 0, 0))
    dz_spec = pl.BlockSpec((tm, 2 * A_WIDTH), lambda i: (i, 0))

    def body(zu_ref, zv_ref, dya_ref, gain_ref, w_ref, wt_ref, b_ref, _, dz_ref, dw_ref, dgain_ref, dbias_ref):
        @pl.when(pl.program_id(0) == 0)
        def _():
            dw_ref[...] = jnp.zeros_like(dw_ref)
            dgain_ref[...] = jnp.zeros_like(dgain_ref)
            dbias_ref[...] = jnp.zeros_like(dbias_ref)

        row = lax.broadcasted_iota(jnp.int32, (CHUNK, CHUNK), 0)
        col = lax.broadcasted_iota(jnp.int32, (CHUNK, CHUNK), 1)
        for c in range(tm // CHUNK):
            rows = slice(c * CHUNK, (c + 1) * CHUNK)
            zu_v = zu_ref[rows, :].astype(F32)
            zv_v = zv_ref[rows, :].astype(F32)
            dya = dya_ref[rows, :].astype(F32)
            ua = _gelu(zu_v)
            vg = _gelu(zv_v)
            for g in range(A_GROUPS):
                sl = slice(g * GROUP_DIM, (g + 1) * GROUP_DIM)
                gain_g = gain_ref[:, sl]
                xhat, rstd, vn = _group_norm(vg[:, sl], gain_g)
                vn16 = vn.astype(BF16)
                f = _dot(w_ref[g], vn16, _NN) + b_ref[:, sl]
                df = dya[:, sl] * ua[:, sl]
                df16 = df.astype(BF16)
                dz_ref[rows, sl] = (dya[:, sl] * f * _gelu_grad(zu_v[:, sl])).astype(BF16)
                dw_ref[g] += jnp.where(row >= col, _dot(df16, vn16, _NT), 0.0)
                dvn = _dot(wt_ref[g], df16, _NN)
                dgain_ref[:, sl] += _colsum(dvn * xhat)
                dxh = dvn * gain_g
                dvg = rstd * (dxh - jnp.mean(dxh, axis=-1, keepdims=True) - xhat * jnp.mean(dxh * xhat, axis=-1, keepdims=True))
                dz_ref[rows, A_WIDTH + g * GROUP_DIM:A_WIDTH + (g + 1) * GROUP_DIM] = (dvg * _gelu_grad(zv_v[:, sl])).astype(BF16)
                dbias_ref[:, sl] += df

    return pl.pallas_call(
        body, name=name, grid=(t // tm,),
        in_specs=[zu, zv, zu, full2((1, A_WIDTH)), w_spec, w_spec, full2((CHUNK, A_WIDTH)), pl.BlockSpec(memory_space=pl.ANY)],
        out_specs=[dz_spec, w_spec, full2((1, A_WIDTH)), full2((CHUNK, A_WIDTH))],
        out_shape=[jax.ShapeDtypeStruct(dz.shape, BF16), jax.ShapeDtypeStruct((A_GROUPS, CHUNK, CHUNK), F32),
                   jax.ShapeDtypeStruct((1, A_WIDTH), F32), jax.ShapeDtypeStruct((CHUNK, A_WIDTH), F32)],
        input_output_aliases={7: 0}, compiler_params=_params("arbitrary"),
    )(z, z, dcat, gain, wtril, wtril_t, bias_exp, dz)


def _rope_tables(pos, inv_freq, sign, name):
    t = pos.shape[0]
    tm = _tile(t)
    row = pl.BlockSpec((1, B_WIDTH), lambda i: (0, 0))
    blk = pl.BlockSpec((tm, B_WIDTH), lambda i: (i, 0))

    def body(pos_ref, f_ref, s_ref, cos_ref, sin_ref):
        ang = pos_ref[...] * f_ref[:, 0:LANES]
        cos_ref[...] = jnp.tile(jnp.cos(ang), (1, B_WIDTH // LANES))
        sin_ref[...] = jnp.tile(jnp.sin(ang) * s_ref[:, 0:LANES], (1, B_WIDTH // LANES))

    return pl.pallas_call(
        body, name=name, grid=(t // tm,), in_specs=[pl.BlockSpec((tm, 1), lambda i: (i, 0)), row, row],
        out_specs=[blk, blk], out_shape=[jax.ShapeDtypeStruct((t, B_WIDTH), F32)] * 2,
        compiler_params=_params("parallel"),
    )(pos, inv_freq, sign)


def _head_sum(v, seg):
    hi = v.astype(BF16)
    lo = (v - hi.astype(F32)).astype(BF16)
    return _dot(hi, seg, _NN) + _dot(lo, seg, _NN)


def _swap_halves(v):
    lane = lax.broadcasted_iota(jnp.int32, v.shape, 1)
    return jnp.where((lane & (HEAD_DIM - 1)) < HEAD_DIM // 2,pltpu.roll(v, B_WIDTH - HEAD_DIM // 2, 1), pltpu.roll(v, HEAD_DIM // 2, 1))


DILATIONS = tuple(dil for _, dil in PATTERNS)
SUBSEQ_TM = 512
LANES = 128


def _subseq_shape(t, dil):
    return (t // dil, dil * B_WIDTH)


def _subseq_spec(tm, dil):
    return pl.BlockSpec((tm // dil, dil * B_WIDTH), lambda i: (i, 0))


def _to_subseq(x, scr_ref, dil):
    if dil == 1:
        return x
    tm, w = x.shape
    for c in range(w // LANES):
        scr_ref[c * tm:(c + 1) * tm, :] = x[:, c * LANES:(c + 1) * LANES]
    return jnp.concatenate([scr_ref[pl.ds(c * tm + r, tm // dil, stride=dil), :]
                            for r in range(dil) for c in range(w // LANES)], axis=1)


def _from_subseq(y, scr_ref, dil):
    if dil == 1:
        return y
    n, w = y.shape[0], y.shape[1] // dil
    tm = n * dil
    for r in range(dil):
        for c in range(w // LANES):
            scr_ref[pl.ds(c * tm + r, n, stride=dil), :] = y[:, r * w + c * LANES:r * w + (c + 1) * LANES]
    return jnp.concatenate([scr_ref[c * tm:(c + 1) * tm, :] for c in range(w // LANES)], axis=1)


def _subseq_scratch(tm):
    return pltpu.VMEM((B_WIDTH // LANES * tm, LANES), F32)


def _qk_prep(z, cos_t, sin_t, gq, gk, seg, name):
    t = z.shape[0]
    tm = _tile(t, (SUBSEQ_TM,))
    col = lambda c: pl.BlockSpec((tm, B_WIDTH), lambda i: (i, c))
    row = pl.BlockSpec((1, B_WIDTH), lambda i: (0, 0))
    blk = col(0)
    nd = len(DILATIONS)

    def body(q_ref, k_ref, v_ref, cos_ref, sin_ref, gq_ref, gk_ref, seg_ref, *rest):
        out_refs, scr_ref = rest[:-1], rest[-1]

        def norm_rot(x, g):
            r = lax.rsqrt(_head_sum(x * x, seg_ref[...]) * (1.0 / HEAD_DIM) + EPS)
            xn = x * r * g
            return xn * cos_ref[...] + _swap_halves(xn) * sin_ref[...]

        vals = (norm_rot(q_ref[...].astype(F32), gq_ref[...]), norm_rot(k_ref[...].astype(F32), gk_ref[...]),
                v_ref[...].astype(F32))
        for a, val in enumerate(vals):
            for b, dil in enumerate(DILATIONS):
                out_refs[a * nd + b][...] = _to_subseq(val, scr_ref, dil).astype(BF16)

    outs = pl.pallas_call(
        body, name=name, grid=(t // tm,),
        in_specs=[col(2), col(3), col(4), blk, blk, row, row, pl.BlockSpec((B_WIDTH, B_WIDTH), lambda i: (0, 0))],
        out_specs=[_subseq_spec(tm, dil) for _ in range(3) for dil in DILATIONS],
        out_shape=[jax.ShapeDtypeStruct(_subseq_shape(t, dil), BF16) for _ in range(3) for dil in DILATIONS],
        scratch_shapes=[_subseq_scratch(tm)], compiler_params=_params("parallel"),
    )(z, z, z, cos_t, sin_t, gq, gk, seg)
    return outs[:nd], outs[nd:2 * nd], outs[2 * nd:]


def _qk_prep_bwd(z, dqs, dks, dvs, cos_t, sin_t, gq, gk, seg, name):
    t = z.shape[0]
    tm = _tile(t, (SUBSEQ_TM,))
    a = 2 * A_WIDTH
    col = lambda c: pl.BlockSpec((tm, B_WIDTH), lambda i: (i, c))
    row = pl.BlockSpec((1, B_WIDTH), lambda i: (0, 0))
    blk = col(0)
    nb = len(DILATIONS)
    subs = [_subseq_spec(tm, dil) for dil in DILATIONS]

    def body(*refs):
        q_ref, k_ref = refs[0], refs[1]
        dq_refs, dk_refs, dv_refs = refs[2:2 + nb], refs[2 + nb:2 + 2 * nb], refs[2 + 2 * nb:2 + 3 * nb]
        cos_ref, sin_ref, gq_ref, gk_ref, seg_ref, dz_ref, dgq_ref, dgk_ref, scr_ref = refs[2 + 3 * nb:]

        @pl.when(pl.program_id(0) == 0)
        def _():
            dgq_ref[...] = jnp.zeros_like(dgq_ref)
            dgk_ref[...] = jnp.zeros_like(dgk_ref)

        def total(d_refs):
            return sum(_from_subseq(r_[...], scr_ref, dil) for r_, dil in zip(d_refs, DILATIONS))

        def back(x, d_refs, g, dg_ref):
            dout = total(d_refs)
            dy = dout * cos_ref[...] + _swap_halves(dout * sin_ref[...])
            r = lax.rsqrt(_head_sum(x * x, seg_ref[...]) * (1.0 / HEAD_DIM) + EPS)
            xn = x * r
            dg_ref[...] += _colsum(dy * xn)
            dxn = dy * g
            return r * (dxn - xn * (_head_sum(dxn * xn, seg_ref[...]) * (1.0 / HEAD_DIM)))

        dz_ref[:, a:a + B_WIDTH] = back(q_ref[...].astype(F32), dq_refs, gq_ref[...], dgq_ref).astype(BF16)
        dz_ref[:, a + B_WIDTH:a + 2 * B_WIDTH] = back(k_ref[...].astype(F32), dk_refs, gk_ref[...], dgk_ref).astype(BF16)
        dz_ref[:, a + 2 * B_WIDTH:a + 3 * B_WIDTH] = total(dv_refs).astype(BF16)

    return pl.pallas_call(
        body, name=name, grid=(t // tm,),
        in_specs=[col(2), col(3)] + subs * 3 + [blk, blk, row, row, pl.BlockSpec((B_WIDTH, B_WIDTH), lambda i: (0, 0))],
        out_specs=[pl.BlockSpec((tm, a + 3 * B_WIDTH), lambda i: (i, 0)), row, row],
        out_shape=[jax.ShapeDtypeStruct((t, a + 3 * B_WIDTH), BF16), jax.ShapeDtypeStruct((1, B_WIDTH), F32),
                   jax.ShapeDtypeStruct((1, B_WIDTH), F32)],
        scratch_shapes=[_subseq_scratch(tm)], compiler_params=_params("arbitrary"),
    )(z, z, *dqs, *dks, *dvs, cos_t, sin_t, gq, gk, seg)


def _subseq_views(x, col, name):
    t = x.shape[0]
    tm = _tile(t, (SUBSEQ_TM,))

    def body(x_ref, *rest):
        out_refs, scr_ref = rest[:-1], rest[-1]
        val = x_ref[...].astype(F32)
        for o_ref, dil in zip(out_refs, DILATIONS):
            o_ref[...] = _to_subseq(val, scr_ref, dil).astype(o_ref.dtype)

    return pl.pallas_call(
        body, name=name, grid=(t // tm,), in_specs=[pl.BlockSpec((tm, B_WIDTH), lambda i: (i, col))],
        out_specs=[_subseq_spec(tm, dil) for dil in DILATIONS],
        out_shape=[jax.ShapeDtypeStruct(_subseq_shape(t, dil), x.dtype) for dil in DILATIONS],
        scratch_shapes=[_subseq_scratch(tm)], compiler_params=_params("parallel"),
    )(x)


ATTN_FWD_BLOCKS = 1
ATTN_BWD_BLOCKS = 2


def _attn_step_specs(nb, want):
    ns = want if nb % want == 0 else 1
    cur = pl.BlockSpec((ns * Q_BLOCK, B_WIDTH), lambda r, i: (i, r))
    prev = pl.BlockSpec((Q_BLOCK, B_WIDTH), lambda r, i: (jnp.maximum(ns * i - 1, 0), r))
    return ns, cur, prev


def _attn_fwd(q, k, v, dil, name):
    t = q.shape[0] * dil
    nb = t // dil // Q_BLOCK
    ns, cur, prev = _attn_step_specs(nb, ATTN_FWD_BLOCKS)

    def body(q_ref, kp_ref, kc_ref, vp_ref, vc_ref, o_ref, lse_ref):
        i = pl.program_id(1)
        a = lax.broadcasted_iota(jnp.int32, (Q_BLOCK, 2 * Q_BLOCK), 0)
        j = lax.broadcasted_iota(jnp.int32, (Q_BLOCK, 2 * Q_BLOCK), 1)
        dist = a + Q_BLOCK - j
        band = (dist >= 0) & (dist <= Q_BLOCK)
        sls = [slice(h * HEAD_DIM, (h + 1) * HEAD_DIM) for h in range(HEADS)]
        for sb in range(ns):
            rows = slice(sb * Q_BLOCK, (sb + 1) * Q_BLOCK)
            before = slice((sb - 1) * Q_BLOCK, sb * Q_BLOCK)
            q = q_ref[rows, :]
            kk = jnp.concatenate([kp_ref[...] if sb == 0 else kc_ref[before, :], kc_ref[rows, :]], axis=0)
            vv = jnp.concatenate([vp_ref[...] if sb == 0 else vc_ref[before, :], vc_ref[rows, :]], axis=0)
            mask = band & ((j >= Q_BLOCK) | (i > 0)) if sb == 0 else band
            scores = [_dot(q[:, sl], kk[:, sl], _NT) for sl in sls]
            ps, dens = [], []
            for sl, s in zip(sls, scores):
                s = jnp.where(mask, s * (HEAD_DIM ** -0.5), NEG)
                m = jnp.max(s, axis=-1, keepdims=True)
                p = jnp.exp(s - m)
                den = jnp.sum(p, axis=-1, keepdims=True)
                ps.append(p.astype(BF16))
                dens.append(den)
                lse_ref[rows, sl] = jnp.broadcast_to(m + jnp.log(den), (Q_BLOCK, HEAD_DIM))
            for sl, p, den in zip(sls, ps, dens):
                o_ref[rows, sl] = _dot(p, vv[:, sl], _NN) / den

    return pl.pallas_call(
        body, name=name, grid=(dil, nb // ns), in_specs=[cur, prev, cur, prev, cur], out_specs=[cur, cur],
        out_shape=[jax.ShapeDtypeStruct(_subseq_shape(t, dil), F32)] * 2,
        compiler_params=_params("parallel", "parallel"),
    )(q, k, k, v, v)


def _attn_merge(outs, lses, cat, name):
    nb = len(DILATIONS)
    t = cat.shape[0]
    tm = _tile(t, (SUBSEQ_TM,))
    subs = [_subseq_spec(tm, dil) for dil in DILATIONS]

    def body(*refs):
        o_refs, l_refs = refs[:nb], refs[nb:2 * nb]
        yb_refs, lse_refs, cat_ref, scr_ref = refs[2 * nb + 1:3 * nb + 1], refs[3 * nb + 1:4 * nb + 1], refs[4 * nb + 1], refs[4 * nb + 2]
        ls = [_from_subseq(r[...], scr_ref, dil) for r, dil in zip(l_refs, DILATIONS)]
        m = functools.reduce(jnp.maximum, ls)
        tot = m + jnp.log(sum(jnp.exp(l - m) for l in ls))
        yb = sum(jnp.exp(l - tot) * _from_subseq(o[...], scr_ref, dil) for l, o, dil in zip(ls, o_refs, DILATIONS))
        cat_ref[...] = yb.astype(BF16)
        yb = yb.astype(BF16).astype(F32)
        for yb_ref, lse_ref, dil in zip(yb_refs, lse_refs, DILATIONS):
            yb_ref[...] = _to_subseq(yb, scr_ref, dil).astype(BF16)
            lse_ref[...] = _to_subseq(tot, scr_ref, dil)

    outs_ = pl.pallas_call(
        body, name=name, grid=(t // tm,), in_specs=subs * 2 + [pl.BlockSpec(memory_space=pl.ANY)],
        out_specs=subs * 2 + [pl.BlockSpec((tm, B_WIDTH), lambda i: (i, A_WIDTH // B_WIDTH))],
        out_shape=[jax.ShapeDtypeStruct(_subseq_shape(t, dil), BF16) for dil in DILATIONS]
        + [jax.ShapeDtypeStruct(_subseq_shape(t, dil), F32) for dil in DILATIONS] + [jax.ShapeDtypeStruct(cat.shape, BF16)],
        input_output_aliases={2 * nb: 2 * nb}, scratch_shapes=[_subseq_scratch(tm)], compiler_params=_params("parallel"),
    )(*outs, *lses, cat)
    return outs_[:nb], outs_[nb:2 * nb], outs_[2 * nb]


def _attn_bwd(q, k, v, do, o, lse, dil, name):
    t = q.shape[0] * dil
    nb = t // dil // Q_BLOCK
    ns, cur, prev = _attn_step_specs(nb, ATTN_BWD_BLOCKS)
    scale = HEAD_DIM ** -0.5

    def body(q_ref, kp_ref, kc_ref, vp_ref, vc_ref, do_ref, o_ref, lse_ref, dq_ref, dk_ref, dv_ref,
             ck_ref, cv_ref, tk_ref, tv_ref):
        step = pl.program_id(1)

        @pl.when(step == 0)
        def _():
            ck_ref[...] = jnp.zeros_like(ck_ref)
            cv_ref[...] = jnp.zeros_like(cv_ref)

        a = lax.broadcasted_iota(jnp.int32, (Q_BLOCK, 2 * Q_BLOCK), 0)
        j = lax.broadcasted_iota(jnp.int32, (Q_BLOCK, 2 * Q_BLOCK), 1)
        dist = a + Q_BLOCK - j
        band = (dist >= 0) & (dist <= Q_BLOCK)
        sls = [slice(h * HEAD_DIM, (h + 1) * HEAD_DIM) for h in range(HEADS)]
        for sb in range(ns):
            i = ns * step + sb
            rows = slice(sb * Q_BLOCK, (sb + 1) * Q_BLOCK)
            before = slice((sb - 1) * Q_BLOCK, sb * Q_BLOCK)
            q = q_ref[rows, :]
            kk = jnp.concatenate([kp_ref[...] if sb == 0 else kc_ref[before, :], kc_ref[rows, :]], axis=0)
            vv = jnp.concatenate([vp_ref[...] if sb == 0 else vc_ref[before, :], vc_ref[rows, :]], axis=0)
            do = do_ref[rows, :]
            dof = do.astype(F32)
            of = o_ref[rows, :].astype(F32)
            mask = band & ((j >= Q_BLOCK) | (step > 0)) if sb == 0 else band
            scores = [_dot(q[:, sl], kk[:, sl], _NT) for sl in sls]
            dps = [_dot(do[:, sl], vv[:, sl], _NT) for sl in sls]
            ps, dss = [], []
            for sl, s, dp in zip(sls, scores, dps):
                p = jnp.exp(jnp.where(mask, s * scale, NEG) - lse_ref[rows, sl.start:sl.start + 1])
                delta = jnp.sum(dof[:, sl] * of[:, sl], axis=-1, keepdims=True)
                dss.append((p * (dp - delta) * scale).astype(BF16))
                ps.append(p.astype(BF16))
            for sl, p, ds in zip(sls, ps, dss):
                dq_ref[rows, sl] = _dot(ds, kk[:, sl], _NN)
                dv_t = _dot(do[:, sl], p, _TN)
                dk_t = _dot(q[:, sl], ds, _TN)
                tk_ref[sl, :] = ck_ref[sl, :] + dk_t[:, :Q_BLOCK]
                tv_ref[sl, :] = cv_ref[sl, :] + dv_t[:, :Q_BLOCK]
                ck_ref[sl, :] = dk_t[:, Q_BLOCK:]
                cv_ref[sl, :] = dv_t[:, Q_BLOCK:]

            @pl.when(i >= 1)
            def _():
                done = pl.ds(pl.multiple_of((i - 1) * Q_BLOCK, Q_BLOCK), Q_BLOCK)
                dk_ref[done, :] = tk_ref[...].T
                dv_ref[done, :] = tv_ref[...].T

        @pl.when(step == nb // ns - 1)
        def _():
            done = pl.ds((nb - 1) * Q_BLOCK, Q_BLOCK)
            dk_ref[done, :] = ck_ref[...].T
            dv_ref[done, :] = cv_ref[...].T

    whole = pl.BlockSpec((t // dil, B_WIDTH), lambda r, i: (0, r))
    return pl.pallas_call(
        body, name=name, grid=(dil, nb // ns), in_specs=[cur, prev, cur, prev, cur, cur, cur, cur],
        out_specs=[cur, whole, whole], out_shape=[jax.ShapeDtypeStruct(_subseq_shape(t, dil), F32)] * 3,
        scratch_shapes=[pltpu.VMEM((B_WIDTH, Q_BLOCK), F32)] * 4,
        compiler_params=_params("parallel", "arbitrary"),
    )(q, k, k, v, v, do, o, lse)


FFN_TN = 256
FFN_ACT_TM = (4096, 2048, 1024, 512, 256, 128)
FFN_FWD_CHUNK = 256
FFN_BWD_CHUNK = 128


def _ffn_up(h, up_t, name):
    t, k = h.shape
    tm = _tile(t)

    def body(h_ref, w_ref, o_ref):
        o_ref[...] = _dot(h_ref[...], w_ref[...], _NT).astype(BF16)

    return pl.pallas_call(
        body, name=name, grid=(2, t // tm),
        in_specs=[pl.BlockSpec((tm, k), lambda p, i: (i, 0)), pl.BlockSpec((None, FFN_DIM, k), lambda p, i: (p, 0, 0))],
        out_specs=pl.BlockSpec((None, tm, FFN_DIM), lambda p, i: (p, i, 0)),
        out_shape=jax.ShapeDtypeStruct((2, t, FFN_DIM), BF16), compiler_params=_params("parallel", "parallel"),
    )(h, up_t.reshape(2, FFN_DIM, k))


def _ffn_up_dx(du, up_t, name):
    t = du.shape[1]
    k = up_t.shape[1]
    tm = _tile(t)

    def body(a_ref, b_ref, o_ref):
        o_ref[...] = _dot(a_ref[0], b_ref[0], _NN) + _dot(a_ref[1], b_ref[1], _NN)

    return pl.pallas_call(
        body, name=name, grid=(t // tm,),
        in_specs=[pl.BlockSpec((2, tm, FFN_DIM), lambda i: (0, i, 0)), pl.BlockSpec((2, FFN_DIM, k), lambda i: (0, 0, 0))],
        out_specs=pl.BlockSpec((tm, k), lambda i: (i, 0)), out_shape=jax.ShapeDtypeStruct((t, k), F32),
        compiler_params=_params("parallel"),
    )(du, up_t.reshape(2, FFN_DIM, k))


def _ffn_conv(win, w_ref, b_ref, p):
    x = win.astype(F32)
    x0, x1, x2 = x[FFN_HALO:], pltpu.roll(x, 1, 0)[FFN_HALO:], pltpu.roll(x, 2, 0)[FFN_HALO:]
    return b_ref[p] + w_ref[p, 2:3, :] * x0 + w_ref[p, 1:2, :] * x1 + w_ref[p, 0:1, :] * x2


def _zero_if(cond, v):
    return jnp.where(cond, 0, v).astype(v.dtype)


def _ffn_act(u, dw_w, dw_b, name):
    t = u.shape[1]
    tm = _tile(t, FFN_ACT_TM)
    chunk = min(FFN_FWD_CHUNK, tm)
    hb = tm // FFN_HALO
    main = pl.BlockSpec((2, tm, FFN_TN), lambda i, j: (0, i, j))
    halo = pl.BlockSpec((2, FFN_HALO, FFN_TN), lambda i, j: (0, jnp.maximum(i * hb - 1, 0), j))
    wsp = pl.BlockSpec((2, FFN_CONV_WIDTH, FFN_TN), lambda i, j: (0, 0, j))
    bsp = pl.BlockSpec((2, 1, FFN_TN), lambda i, j: (0, 0, j))

    def body(u_ref, uh_ref, w_ref, b_ref, o_ref, z_ref):
        first = pl.program_id(0) == 0

        def emit(rows, wins):
            za, zb = _ffn_conv(wins[0], w_ref, b_ref, 0), _ffn_conv(wins[1], w_ref, b_ref, 1)
            o_ref[rows, :] = (za * _sigmoid(za) * zb).astype(BF16)
            z_ref[0, rows, :] = za.astype(BF16)
            z_ref[1, rows, :] = zb.astype(BF16)

        emit(pl.ds(0, chunk), [jnp.concatenate([_zero_if(first, uh_ref[p]), u_ref[p, 0:chunk, :]], axis=0) for p in range(2)])

        def step(c, carry):
            s = pl.multiple_of(c * chunk, chunk)
            emit(pl.ds(s, chunk), [u_ref[p, pl.ds(s - FFN_HALO, chunk + FFN_HALO), :] for p in range(2)])
            return carry

        lax.fori_loop(1, tm // chunk, step, 0)

    return pl.pallas_call(
        body, name=name, grid=(t // tm, FFN_DIM // FFN_TN), in_specs=[main, halo, wsp, bsp],
        out_specs=[pl.BlockSpec((tm, FFN_TN), lambda i, j: (i, j)), main],
        out_shape=[jax.ShapeDtypeStruct((t, FFN_DIM), BF16), jax.ShapeDtypeStruct((2, t, FFN_DIM), BF16)],
        compiler_params=_params("parallel", "parallel"),
    )(u, u, dw_w, dw_b)


def _fold8(v):
    return jnp.sum(v.reshape(v.shape[0] // 8, 8, v.shape[1]), axis=0)


def _ffn_act_bwd(u, z, dact, dw_w, name):
    t = u.shape[1]
    tm = _tile(t, FFN_ACT_TM)
    chunk = min(FFN_BWD_CHUNK, tm // 2)
    halo = FFN_HALO
    hb = tm // halo
    nt = t // tm
    last_halo = t // halo - 1
    next_i = lambda i: jnp.minimum((i + 1) * hb, last_halo)
    main = pl.BlockSpec((2, tm, FFN_TN), lambda j, i: (0, i, j))
    nxt = pl.BlockSpec((2, halo, FFN_TN), lambda j, i: (0, next_i(i), j))
    wsp = pl.BlockSpec((2, FFN_CONV_WIDTH, FFN_TN), lambda j, i: (0, 0, j))
    bsp = pl.BlockSpec((2, 1, FFN_TN), lambda j, i: (0, 0, j))

    def body(u_ref, z_ref, zn_ref, da_ref, dan_ref, w_ref, du_ref, dw_ref, db_ref, acc_ref):
        i = pl.program_id(1)
        last = i == nt - 1
        acc_ref[...] = jnp.zeros_like(acc_ref)

        def emit(rows, zs, dact):
            n = chunk + halo
            za, zb, dact = zs[0].astype(F32), zs[1].astype(F32), dact.astype(F32)
            sg = _sigmoid(za)
            dzs = (dact * zb * (sg * (1.0 + za * (1.0 - sg))), dact * (za * sg))
            for p, dz in enumerate(dzs):
                ahead = (dz[:chunk], pltpu.roll(dz, n - 1, 0)[:chunk], pltpu.roll(dz, n - 2, 0)[:chunk])
                um = u_ref[p, rows, :].astype(F32)
                acc_ref[p, FFN_CONV_WIDTH] += _fold8(ahead[0])
                du = None
                for j, dzj in enumerate(ahead):
                    k = FFN_CONV_WIDTH - 1 - j
                    acc_ref[p, k] += _fold8(dzj * um)
                    term = w_ref[p, k:k + 1, :] * dzj
                    du = term if du is None else du + term
                du_ref[p, rows, :] = du.astype(BF16)

        def step(c, carry):
            s = pl.multiple_of(c * chunk, chunk)
            emit(pl.ds(s, chunk), [z_ref[p, pl.ds(s, chunk + halo), :] for p in range(2)], da_ref[pl.ds(s, chunk + halo), :])
            return carry

        lax.fori_loop(0, tm // chunk - 1, step, 0)
        s = tm - chunk
        emit(pl.ds(s, chunk),
             [jnp.concatenate([z_ref[p, s:tm, :], zn_ref[p]], axis=0) for p in range(2)],
             jnp.concatenate([da_ref[s:tm, :], _zero_if(last, dan_ref[...])], axis=0))

        @pl.when(i == 0)
        def _():
            dw_ref[...] = jnp.zeros_like(dw_ref)
            db_ref[...] = jnp.zeros_like(db_ref)

        for p in range(2):
            for k in range(FFN_CONV_WIDTH):
                dw_ref[p, k:k + 1, :] += _colsum(acc_ref[p, k])
            db_ref[p] += _colsum(acc_ref[p, FFN_CONV_WIDTH])

    return pl.pallas_call(
        body, name=name, grid=(FFN_DIM // FFN_TN, nt),
        in_specs=[main, main, nxt, pl.BlockSpec((tm, FFN_TN), lambda j, i: (i, j)),
                  pl.BlockSpec((halo, FFN_TN), lambda j, i: (next_i(i), j)), wsp],
        out_specs=[main, wsp, bsp],
        out_shape=[jax.ShapeDtypeStruct((2, t, FFN_DIM), BF16), jax.ShapeDtypeStruct((2, FFN_CONV_WIDTH, FFN_DIM), F32),
                   jax.ShapeDtypeStruct((2, 1, FFN_DIM), F32)],
        scratch_shapes=[pltpu.VMEM((2, FFN_CONV_WIDTH + 1, 8, FFN_TN), F32)],
        compiler_params=_params("parallel", "arbitrary"),
    )(u, z, z, dact, dact, dw_w)


CONV_TM = 256
CONV_ROWS = 128
CONV_FWD_ROWS = 256
CONV_LANES = 128
CONV_NORM_ROWS = 32


def _glu_window(pa_ref, pah_ref, pg_ref, pgh_ref, scr_ref, first):
    ah, gh = pah_ref[...].astype(F32), pgh_ref[...].astype(F32)
    scr_ref[0:CONV_HALO, :] = jnp.where(first, 0.0, ah * _sigmoid(gh))
    scr_ref[CONV_HALO:, :] = pa_ref[...].astype(F32) * _sigmoid(pg_ref[...].astype(F32))


def _tap_slabs(win, rows, ahead):
    n = win.shape[0]
    for s in range(8):
        ws = win if s == 0 else pltpu.roll(win, n - s if ahead else s, 0)
        for q in range(CONV_HALO // 8):
            o = 8 * q + s
            if o < CONV_WIDTH:
                start = 8 * q if ahead else CONV_HALO - 8 * q
                yield CONV_WIDTH - 1 - o, ws[start:start + rows]


def _conformer_specs(t):
    tm = _tile(t, (CONV_TM, 128))
    hb = tm // CONV_HALO
    d = D_MODEL
    main = lambda c: pl.BlockSpec((tm, d), lambda i: (i, c))
    halo = lambda c: pl.BlockSpec((CONV_HALO, d), lambda i: (jnp.maximum(i * hb - 1, 0), c))
    row = pl.BlockSpec((1, d), lambda i: (0, 0))
    wsp = pl.BlockSpec((CONV_WIDTH, d), lambda i: (0, 0))
    return tm, main, halo, row, wsp


def _conformer_mid(p, dw_w, dw_b, ln_g, ln_b, name):
    t = p.shape[0]
    tm, main, halo, row, wsp = _conformer_specs(t)
    d, lanes, rows = D_MODEL, CONV_LANES, min(CONV_FWD_ROWS, tm)

    def body(pa_ref, pah_ref, pg_ref, pgh_ref, w_ref, b_ref, g_ref, lb_ref, o_ref, dc_ref, scr_ref):
        _glu_window(pa_ref, pah_ref, pg_ref, pgh_ref, scr_ref, pl.program_id(0) == 0)
        for c in range(d // lanes):
            ls = slice(c * lanes, (c + 1) * lanes)

            def taps(r, carry, ls=ls):
                r0 = pl.multiple_of(r * rows, rows)
                acc = jnp.broadcast_to(b_ref[:, ls], (rows, lanes))
                for k, slab in _tap_slabs(scr_ref[pl.ds(r0, rows + CONV_HALO), ls], rows, False):
                    acc = acc + w_ref[k:k + 1, ls] * slab
                dc_ref[pl.ds(r0, rows), ls] = acc
                return carry

            lax.fori_loop(0, tm // rows, taps, 0)

        def norm(r, carry):
            r0 = pl.multiple_of(r * CONV_NORM_ROWS, CONV_NORM_ROWS)
            dc = dc_ref[pl.ds(r0, CONV_NORM_ROWS), :]
            xc = dc - jnp.mean(dc, axis=-1, keepdims=True)
            ln = xc * lax.rsqrt(jnp.mean(xc * xc, axis=-1, keepdims=True) + EPS) * g_ref[...] + lb_ref[...]
            o_ref[pl.ds(r0, CONV_NORM_ROWS), :] = (ln * _sigmoid(ln)).astype(BF16)
            return carry

        lax.fori_loop(0, tm // CONV_NORM_ROWS,norm, 0)

    return pl.pallas_call(
        body, name=name, grid=(t // tm,), in_specs=[main(0), halo(0), main(1), halo(1), wsp, row, row, row],
        out_specs=[main(0), main(0)], out_shape=[jax.ShapeDtypeStruct((t, d), BF16), jax.ShapeDtypeStruct((t, d), F32)],
        scratch_shapes=[pltpu.VMEM((tm + CONV_HALO, d), F32)], compiler_params=_params("parallel"),
    )(p, p, p, p, dw_w, dw_b, ln_g, ln_b)


def _conformer_mid_bwd(p, dc, ds, ln_g, ln_b, name):
    t = p.shape[0]
    tm, main, halo, row, wsp = _conformer_specs(t)
    d, nt = D_MODEL, t // tm
    rows, lanes = CONV_ROWS, CONV_LANES

    def body(pa_ref, pah_ref, pg_ref, pgh_ref, dc_ref, ds_ref, g_ref, lb_ref,
             ddc_ref, dw_ref, db_ref, dg_ref, dlb_ref, scr_ref, wacc_ref, racc_ref):
        i = pl.program_id(0)

        @pl.when(i == 0)
        def _():
            wacc_ref[...] = jnp.zeros_like(wacc_ref)
            racc_ref[...] = jnp.zeros_like(racc_ref)

        _glu_window(pa_ref, pah_ref, pg_ref, pgh_ref, scr_ref, i == 0)

        def norm_bwd(r, carry):
            r0 = pl.multiple_of(r * CONV_NORM_ROWS, CONV_NORM_ROWS)
            dcv = dc_ref[pl.ds(r0, CONV_NORM_ROWS), :]
            xc = dcv - jnp.mean(dcv, axis=-1, keepdims=True)
            rstd = lax.rsqrt(jnp.mean(xc * xc, axis=-1, keepdims=True) + EPS)
            xhat = xc * rstd
            ln = xhat * g_ref[...] + lb_ref[...]
            sg = _sigmoid(ln)
            dln = ds_ref[pl.ds(r0, CONV_NORM_ROWS), :].astype(F32) * (sg * (1.0 + ln * (1.0 - sg)))
            dxh = dln * g_ref[...]
            ddc = rstd * (dxh - jnp.mean(dxh, axis=-1, keepdims=True) - xhat * jnp.mean(dxh * xhat, axis=-1, keepdims=True))
            ddc_ref[pl.ds(r0, CONV_NORM_ROWS), :] = ddc
            racc_ref[0] += _fold8(dln * xhat)
            racc_ref[1] += _fold8(dln)
            racc_ref[2] += _fold8(ddc)
            return carry

        lax.fori_loop(0, tm // CONV_NORM_ROWS,norm_bwd, 0)

        for c in range(d // lanes):
            ls = slice(c * lanes, (c + 1) * lanes)

            def taps(r, carry, ls=ls):
                r0 = pl.multiple_of(r * rows, rows)
                ddc = ddc_ref[pl.ds(r0, rows), ls]
                for k, slab in _tap_slabs(scr_ref[pl.ds(r0, rows + CONV_HALO), ls], rows, False):
                    wacc_ref[k, :, ls] += _fold8(ddc * slab)
                return carry

            lax.fori_loop(0, tm // rows, taps, 0)

        @pl.when(i == nt - 1)
        def _():
            for k in range(CONV_WIDTH):
                dw_ref[k:k + 1, :] = _colsum(wacc_ref[k])
            dg_ref[...] = _colsum(racc_ref[0])
            dlb_ref[...] = _colsum(racc_ref[1])
            db_ref[...] = _colsum(racc_ref[2])

    return pl.pallas_call(
        body, name=name, grid=(nt,), in_specs=[main(0), halo(0), main(1), halo(1), main(0), main(0), row, row],
        out_specs=[main(0), wsp, row, row, row],
        out_shape=[jax.ShapeDtypeStruct((t, d), F32), jax.ShapeDtypeStruct((CONV_WIDTH, d), F32)]
        + [jax.ShapeDtypeStruct((1, d), F32)] * 3,
        scratch_shapes=[pltpu.VMEM((tm + CONV_HALO, d), F32), pltpu.VMEM((CONV_WIDTH, 8, d), F32), pltpu.VMEM((3, 8, d), F32)],
        compiler_params=_params("arbitrary"),
    )(p, p, p, p, dc, ds, ln_g, ln_b)


def _conformer_glu_bwd(p, ddc, dw_w, name):
    t = p.shape[0]
    d = D_MODEL
    tm = _tile(t, (CONV_TM, 128))
    hb = tm // CONV_HALO
    nt = t // tm
    last_halo = t // CONV_HALO - 1
    rows, lanes = CONV_ROWS, CONV_LANES
    col = lambda c: pl.BlockSpec((tm, d), lambda i: (i, c))
    nxt = pl.BlockSpec((CONV_HALO, d), lambda i: (jnp.minimum((i + 1) * hb, last_halo), 0))

    def body(pa_ref, pg_ref, ddc_ref, ddcn_ref, w_ref, dp_ref, db_ref, scr_ref, acc_ref):
        i = pl.program_id(0)

        @pl.when(i == 0)
        def _():
            acc_ref[...] = jnp.zeros_like(acc_ref)

        scr_ref[0:tm, :] = ddc_ref[...]
        scr_ref[tm:, :] = _zero_if(i == nt - 1, ddcn_ref[...])
        for c in range(d // lanes):
            ls = slice(c * lanes, (c + 1) * lanes)
            gs = slice(d + c * lanes, d + (c + 1) * lanes)

            def taps(r, carry, ls=ls, gs=gs):
                r0 = pl.multiple_of(r * rows, rows)
                dglu = None
                for k, slab in _tap_slabs(scr_ref[pl.ds(r0, rows + CONV_HALO), ls], rows, True):
                    term = w_ref[k:k + 1, ls] * slab
                    dglu = term if dglu is None else dglu + term
                a = pa_ref[pl.ds(r0, rows), ls].astype(F32)
                sg = _sigmoid(pg_ref[pl.ds(r0, rows), ls].astype(F32))
                da = (dglu * sg).astype(BF16)
                dg = (dglu * a * sg * (1.0 - sg)).astype(BF16)
                dp_ref[pl.ds(r0, rows), ls] = da
                dp_ref[pl.ds(r0, rows), gs] = dg
                acc_ref[:, ls] += _fold8(da.astype(F32))
                acc_ref[:, gs] += _fold8(dg.astype(F32))
                return carry

            lax.fori_loop(0, tm // rows, taps, 0)

        @pl.when(i == nt - 1)
        def _():
            db_ref[...] = _colsum(acc_ref[...])

    return pl.pallas_call(
        body, name=name, grid=(nt,),
        in_specs=[col(0), col(1), col(0), nxt, pl.BlockSpec((CONV_WIDTH, d), lambda i: (0, 0))],
        out_specs=[pl.BlockSpec((tm, 2 * d), lambda i: (i, 0)), pl.BlockSpec((1, 2 * d), lambda i: (0, 0))],
        out_shape=[jax.ShapeDtypeStruct((t, 2 * d), BF16), jax.ShapeDtypeStruct((1, 2 * d), F32)],
        scratch_shapes=[pltpu.VMEM((tm + CONV_HALO, d), F32), pltpu.VMEM((8, 2 * d), F32)],
        compiler_params=_params("arbitrary"),
    )(p, p, ddc, ddc, dw_w)


def _colsum_call(a, name):
    t, n = a.shape
    tm = _tile(t)

    def body(a_ref, o_ref):
        @pl.when(pl.program_id(0) == 0)
        def _():
            o_ref[...] = jnp.zeros_like(o_ref)

        o_ref[...] += _colsum(a_ref[...].astype(F32))

    return pl.pallas_call(
        body, name=name, grid=(t // tm,), in_specs=[pl.BlockSpec((tm, n), lambda i: (i, 0))],
        out_specs=pl.BlockSpec((1, n), lambda i: (0, 0)), out_shape=jax.ShapeDtypeStruct((1, n), F32),
        compiler_params=_params("arbitrary"),
    )(a)


def _ada_fwd(c_all, w, name):
    rows, d = c_all.shape
    n = w.shape[1]
    tn = _tile(n, (256, 128))

    def body(c_ref, w_ref, o_ref):
        c = c_ref[...]
        o_ref[...] = _dot((c * _sigmoid(c)).astype(BF16), w_ref[...].astype(BF16), _NN)

    return pl.pallas_call(
        body, name=name, grid=(n // tn,),
        in_specs=[pl.BlockSpec((rows, d), lambda j: (0, 0)), pl.BlockSpec((d, tn), lambda j: (0, j))],
        out_specs=pl.BlockSpec((rows, tn), lambda j: (0, j)), out_shape=jax.ShapeDtypeStruct((rows, n), F32),
        compiler_params=_params("parallel"),
    )(c_all, w)


def _ada_bwd(c_all, dmod, name):
    rows, d = c_all.shape
    layers, _, n = dmod.shape
    tn = _tile(n, (256, 128))

    def body(c_ref, g_ref, o_ref):
        c = c_ref[...]
        o_ref[...] = _dot((c * _sigmoid(c)).astype(BF16), g_ref[...].astype(BF16), _TN)

    return pl.pallas_call(
        body, name=name, grid=(layers, n // tn),
        in_specs=[pl.BlockSpec((rows, d), lambda l, j: (0, 0)), pl.BlockSpec((None, rows, tn), lambda l, j: (l, 0, j))],
        out_specs=pl.BlockSpec((None, d, tn), lambda l, j: (l, 0, j)), out_shape=jax.ShapeDtypeStruct((layers, d, n), F32),
        compiler_params=_params("parallel", "parallel"),
    )(c_all, dmod)


def _sum_in_device_order(own, land, me, name):
    s, r, c = land.shape
    tr = _row_tile(r, 256)
    slot = lambda k: pl.BlockSpec((None, tr, c), lambda i, me_ref: (jnp.where(me_ref[0] == k, (k + 1) % s, k), i, 0))
    own_spec = pl.BlockSpec((tr, c), lambda i, me_ref: (i, 0))

    def body(me_ref, own_ref, *refs):
        o_ref = refs[-1]
        acc = None
        for k, ref in enumerate(refs[:-1]):
            term = jnp.where(me_ref[0] == k, own_ref[...], ref[...]).astype(F32)
            acc = term if acc is None else acc + term
        o_ref[...] = acc

    return pl.pallas_call(
        body, name=name, out_shape=jax.ShapeDtypeStruct((r, c), F32),
        grid_spec=pltpu.PrefetchScalarGridSpec(
            num_scalar_prefetch=1, grid=(r // tr,), in_specs=[own_spec] + [slot(k) for k in range(s)], out_specs=own_spec),
        compiler_params=_params("parallel"),
    )(me, own, *[land] * s)


def _sum_with_own(blocks, land, me, name, layer=0, layers=1, into=None):
    s, r, c = land.shape
    tr = _row_tile(r, 256)
    slot = lambda k: pl.BlockSpec((None, tr, c), lambda i, me_ref: ((me_ref[0] + k) % s, i, 0))
    stacked = [] if into is None else [into]

    def body(me_ref, own_ref, *refs):
        o_ref = refs[-1]
        acc = own_ref[...].astype(F32)
        for ref in refs[:s - 1]:
            acc = acc + ref[...].astype(F32)
        o_ref[...] = acc

    return pl.pallas_call(
        body, name=name, out_shape=jax.ShapeDtypeStruct((layers, r, c), F32),
        grid_spec=pltpu.PrefetchScalarGridSpec(
            num_scalar_prefetch=1, grid=(r // tr,),
            in_specs=[slot(k) for k in range(s)] + [pl.BlockSpec(memory_space=pl.ANY)] * len(stacked),
            out_specs=pl.BlockSpec((None, tr, c), lambda i, me_ref: (layer, i, 0))),
        input_output_aliases={s + 1: 0} if stacked else {},
        compiler_params=_params("parallel"),
    )(me, blocks, *[land] * (s - 1), *stacked)


def _adamw_update(w, g, m, v):
    nm = ADAM_B1 * m + (1.0 - ADAM_B1) * g
    nv = ADAM_B2 * v + (1.0 - ADAM_B2) * (g * g)
    m_hat = nm * (1.0 / (1.0 - ADAM_B1 ** ADAM_STEP))
    v_hat = nv * (1.0 / (1.0 - ADAM_B2 ** ADAM_STEP))
    return -ADAM_LR * (m_hat / (jnp.sqrt(v_hat) + ADAM_EPS) + ADAM_WD * w), nm, nv


def _adamw(w, g, m, v, behind, name):
    l, r, c = w.shape
    tr = _row_tile(r, 256)
    blk = pl.BlockSpec((None, tr, c), lambda k, i: (k, i, 0))
    order = [] if behind is None else [behind]

    def body(w_ref, g_ref, m_ref, v_ref, *rest):
        d_ref, nm_ref, nv_ref = rest[-3:]
        d_ref[...], nm_ref[...], nv_ref[...] = _adamw_update(w_ref[...], g_ref[...], m_ref[...], v_ref[...])

    return pl.pallas_call(
        body, name=name, grid=(l, r // tr), in_specs=[blk] * 4 + [pl.BlockSpec(memory_space=pl.ANY)] * len(order),
        out_specs=[blk] * 3, out_shape=[jax.ShapeDtypeStruct(w.shape, F32)] * 3,
        compiler_params=_params("parallel", "parallel"),
    )(w, g, m, v, *order)


def _adamw_small(ws, gs, ms, vs, name):
    n = len(ws)
    two_d = lambda a: a.reshape(-1, a.shape[-1])

    def body(*refs):
        ins, outs = refs[:4 * n], refs[4 * n:]
        for a in range(n):
            outs[a][...], outs[n + a][...], outs[2 * n + a][...] = _adamw_update(*[ins[k * n + a][...] for k in range(4)])

    res = pl.pallas_call(
        body, name=name, out_shape=[jax.ShapeDtypeStruct(two_d(w).shape, F32) for w in ws] * 3,
    )(*[two_d(a) for a in (*ws, *gs, *ms, *vs)])
    return [[res[k * n + a].reshape(ws[a].shape) for a in range(n)] for k in range(3)]


def _mesh_pos():
    return lax.axis_index("x"), lax.axis_index("y"), lax.axis_index("c")


def _all_gather_vmem(x_shard, name):
    m_per, n = x_shard.shape

    def body(x_ref, out_ref, send_sems, recv_sems, local_sem):
        x, y, c = _mesh_pos()
        me, sibling = (x, y, c), (x, y, 1 - c)
        chips = [(1 - x, y), (x, 1 - y), (1 - x, 1 - y)]

        def rows(px, py, pc):
            return out_ref.at[pl.ds((4 * px + 2 * py + pc) * m_per, m_per), :]

        def copy(k, block, to, src=None):
            return pltpu.make_async_remote_copy(
                src_ref=rows(*block) if src is None else src, dst_ref=rows(*block),
                send_sem=send_sems.at[k], recv_sem=recv_sems.at[k], device_id=to, device_id_type=MESH)

        mine = pltpu.make_async_copy(x_ref, rows(*me), local_sem)
        mine.start()
        first = [copy(0, me, sibling, src=x_ref)]
        first += [copy(1 + j, me, (*chip, c), src=x_ref) for j, chip in enumerate(chips)]
        for cp in first:
            cp.start()
        passed = [copy(4 + j, (*chip, c), sibling) for j, chip in enumerate(chips)]
        for j, chip in enumerate(chips):
            copy(1 + j, (*chip, c), me).wait_recv()
            passed[j].start()
        copy(0, sibling, me).wait_recv()
        for j, chip in enumerate(chips):
            copy(4 + j, (*chip, 1 - c), me).wait_recv()
        for cp in first + passed:
            cp.wait_send()
        mine.wait()

    return pl.pallas_call(
        body, name=name, out_shape=jax.ShapeDtypeStruct((N_DEV * m_per, n), x_shard.dtype),
        in_specs=[pl.BlockSpec(memory_space=pltpu.VMEM)], out_specs=pl.BlockSpec(memory_space=pltpu.VMEM),
        scratch_shapes=[pltpu.SemaphoreType.DMA((7,)), pltpu.SemaphoreType.DMA((7,)), pltpu.SemaphoreType.DMA],
    )(x_shard)


def _all_gather_hbm(shards, name):
    n = len(shards)
    out_shape = [jax.ShapeDtypeStruct((N_DEV,) + s.shape, s.dtype) for s in shards]

    def body(*refs):
        x_refs, out_refs = refs[:n], refs[n:2 * n]
        send_sems, recv_sems, local_sems = refs[2 * n:]
        x, y, c = _mesh_pos()
        me, sibling = (x, y, c), (x, y, 1 - c)
        chips = [(1 - x, y), (x, 1 - y), (1 - x, 1 - y)]

        def blk(a, p):
            return out_refs[a].at[4 * p[0] + 2 * p[1] + p[2]]

        def copy(a, k, block, to, src=None):
            return pltpu.make_async_remote_copy(
                src_ref=blk(a, block) if src is None else src, dst_ref=blk(a, block),
                send_sem=send_sems.at[7 * a + k], recv_sem=recv_sems.at[7 * a + k], device_id=to, device_id_type=MESH)

        mine = [pltpu.make_async_copy(x_refs[a], blk(a, me), local_sems.at[a]) for a in range(n)]
        for cp in mine:
            cp.start()
        first = []
        for a in range(n):
            first.append(copy(a, 0, me, sibling, src=x_refs[a]))
            first += [copy(a, 1 + j, me, (*chip, c), src=x_refs[a]) for j, chip in enumerate(chips)]
        for cp in first:
            cp.start()
        passed = []
        for j, chip in enumerate(chips):
            for a in range(n):
                copy(a, 1 + j, (*chip, c), me).wait_recv()
                fwd = copy(a, 4 + j, (*chip, c), sibling)
                fwd.start()
                passed.append(fwd)
        for a in range(n):
            copy(a, 0, sibling, me).wait_recv()
            for j, chip in enumerate(chips):
                copy(a, 4 + j, (*chip, 1 - c), me).wait_recv()
        for cp in first + passed:
            cp.wait_send()
        for cp in mine:
            cp.wait()

    return pl.pallas_call(
        body, name=name, out_shape=out_shape, in_specs=[pl.BlockSpec(memory_space=pltpu.VMEM)] * n,
        out_specs=[pl.BlockSpec(memory_space=pl.ANY)] * n,
        scratch_shapes=[pltpu.SemaphoreType.DMA((7 * n,)), pltpu.SemaphoreType.DMA((7 * n,)), pltpu.SemaphoreType.DMA((n,))],
    )(*shards)


def _peers(x, y, c):
    flip = lambda v, f: 1 - v if f else v
    return [(flip(x, m & 4), flip(y, m & 2), flip(c, m & 1)) for m in range(1, N_DEV)]


def _dev_index(p):
    return 4 * p[0] + 2 * p[1] + p[2]


def _push_copies(src_refs, land_refs, send_sems, recv_sems, scatter, receive):
    x, y, c = _mesh_pos()
    me = _dev_index((x, y, c))
    copies = []
    for a, (src, land) in enumerate(zip(src_refs, land_refs)):
        for k, p in enumerate(_peers(x, y, c)):
            copies.append(pltpu.make_async_remote_copy(
                src_ref=src.at[_dev_index(p)] if scatter else src, dst_ref=land.at[_dev_index(p) if receive else me],
                send_sem=send_sems.at[7 * a + k], recv_sem=recv_sems.at[7 * a + k], device_id=p, device_id_type=MESH))
    return copies


_HBM = pl.BlockSpec(memory_space=pltpu.HBM)
_SEM = pl.BlockSpec(memory_space=pltpu.SEMAPHORE)
_EFFECT = pltpu.SideEffectType.DATAFLOW_SIDE_EFFECTING


def _pushes_start(srcs, lands, scatter, name):
    n = len(srcs)

    def body(*refs):
        src_refs, land_refs = refs[:n], refs[n:2 * n]
        send_sems, recv_sems = refs[2 * n], refs[2 * n + 1]
        token = refs[-1]
        for cp in _push_copies(src_refs, land_refs, send_sems, recv_sems, scatter, receive=False):
            cp.start()
        token[...] = jnp.zeros_like(token)

    hbm = lambda a: pltpu.HBM(a.shape, a.dtype)
    sems = pltpu.SemaphoreType.DMA((7 * n,))
    outs = pl.pallas_call(
        body, name=name,
        out_shape=(sems, sems, *[hbm(a) for a in srcs], *[hbm(a) for a in lands], jax.ShapeDtypeStruct((8, 128), F32)),
        in_specs=[_HBM] * (2 * n), out_specs=(_SEM, _SEM, *[_HBM] * (2 * n), pl.BlockSpec(memory_space=pltpu.VMEM)),
        input_output_aliases={i: 2 + i for i in range(2 * n)},
        compiler_params=pltpu.CompilerParams(has_side_effects=_EFFECT),
    )(*[pltpu.with_memory_space_constraint(a, pltpu.HBM) for a in (*srcs, *lands)])
    return (outs[0], outs[1], outs[2:2 + n], outs[2 + n:2 + 2 * n], scatter), outs[-1]


def _pushes_wait(handle, after, name):
    send_sems, recv_sems, srcs, lands, scatter = handle
    n = len(srcs)
    after = after if isinstance(after, (tuple, list)) else (after,)

    def body(*refs):
        src_refs, land_refs = refs[:n], refs[n:2 * n]
        for cp in _push_copies(src_refs, land_refs, refs[2 * n], refs[2 * n + 1], scatter, receive=True):
            cp.wait_send()
            cp.wait_recv()

    hbm = lambda a: pltpu.HBM(a.shape, a.dtype)
    outs = pl.pallas_call(
        body, name=name, out_shape=tuple(hbm(a) for a in (*srcs, *lands)),
        in_specs=[_HBM] * (2 * n) + [_SEM, _SEM] + [pl.BlockSpec(memory_space=pl.ANY)] * len(after),
        out_specs=tuple([_HBM] * (2 * n)), input_output_aliases={i: i for i in range(2 * n)},
        compiler_params=pltpu.CompilerParams(has_side_effects=_EFFECT),
    )(*srcs, *lands, send_sems, recv_sems, *after)
    return outs[:n], outs[n:]


def _landing_zones(srcs, behind, name):
    n, nb = len(srcs), len(behind)

    def body(*refs):
        src_refs, land_refs, bufs, sems = refs[:n], refs[n + nb:2 * n + nb], refs[2 * n + nb:3 * n + nb], refs[3 * n + nb]
        me = _dev_index(_mesh_pos())
        load = [pltpu.make_async_copy(src, buf, sems.at[a]) for a, (src, buf) in enumerate(zip(src_refs, bufs))]
        store = [pltpu.make_async_copy(buf, land.at[me], sems.at[a]) for a, (buf, land) in enumerate(zip(bufs, land_refs))]
        for cp in load:
            cp.start()
        for ld, st in zip(load, store):
            ld.wait()
            st.start()
        for cp in store:
            cp.wait()

    any_spec = pl.BlockSpec(memory_space=pl.ANY)
    return pl.pallas_call(
        body, name=name, out_shape=[jax.ShapeDtypeStruct((N_DEV,) + s.shape, s.dtype) for s in srcs],
        in_specs=[any_spec] * (n + nb), out_specs=[any_spec] * n,
        scratch_shapes=[pltpu.VMEM(s.shape, s.dtype) for s in srcs] + [pltpu.SemaphoreType.DMA((n,))],
        compiler_params=pltpu.CompilerParams(vmem_limit_bytes=V7X_VMEM_LIMIT),
    )(*srcs, *behind)


def _ffn_forward(x, mod, norm_g, w, tag):
    sh, sc, gate = mod
    h = _modnorm(x, norm_g, sc, sh, f"{tag}_norm")
    u = _ffn_up(h, w["up_t"], f"{tag}_up")
    act, z = _ffn_act(u, w["dw_w"], w["dw_b"], f"{tag}_act")
    y, x_new = _matmul(act, w["down"], "nn", BF16, f"{tag}_down", resid=(x, gate))
    return x_new, (x, h, u, z, act, y)


def _behind(row, token):
    return row if token is None else row + token[0:1, 0:1]


def _ffn_backward(dx_new, dy, d_gate, saved, mod, norm_g, w, tag, emit, below):
    x, h, u, z, act, _ = saved
    _, sc, _ = mod
    d_down = _matmul_tn_acc(act, dy, f"{tag}_down_dw")
    dact = _matmul(dy, w["down"], "nt", BF16, f"{tag}_down_dx")
    du, d_dw_w, d_dw_b = _ffn_act_bwd(u, z, dact, w["dw_w"], f"{tag}_act_bwd")
    d_up_t = _matmul_tn_acc(du, h, f"{tag}_up_dw").reshape(2 * FFN_DIM, -1)
    token = emit([d_up_t, d_down])
    dh = _ffn_up_dx(du, w["up_t"], f"{tag}_up_dx")
    dx, d_w, d_sh, *dy_below = _modnorm_bwd(x, dh, norm_g, _behind(sc, token), dx_new, below, f"{tag}_norm_bwd")
    return (dx, *dy_below), dict(dw_w=d_dw_w.transpose(1, 0, 2).reshape(FFN_CONV_WIDTH, 2 * FFN_DIM),
                    dw_b=d_dw_b.reshape(1, 2 * FFN_DIM), norm_g=d_w * (1.0 + sc), sh=d_sh, sc=d_w * norm_g, gate=d_gate)


def _mixer_forward(x, mod, norm_g, w, rope, tag):
    sh, sc, gate = mod
    h = _modnorm(x, norm_g, sc, sh, f"{tag}_norm")
    z = _matmul(h, w["w_in_t"], "nt", BF16, f"{tag}_in")
    ya = _gmlp_fwd(z, w["gain"], w["wtril"], w["bias_exp"], f"{tag}_gmlp")
    q, k, v = _qk_prep(z, rope[0], rope[1], w["gq"], w["gk"], w["seg"], f"{tag}_qk")
    outs, lses = zip(*[_attn_fwd(q[b], k[b], v[b], dil, f"{tag}_attn_d{dil}") for b, dil in enumerate(DILATIONS)])
    yb, lse, cat = _attn_merge(outs, lses, ya, f"{tag}_merge")
    y, x_new = _matmul(cat, w["w_out"], "nn", BF16, f"{tag}_out", resid=(x, gate))
    return x_new, (x, h, z, q, k, v, yb, lse, cat, y)


def _mixer_backward(dx_new, dy, d_gate, saved, mod, norm_g, w, rope, tag, emit, below):
    x, h, z, q, k, v, yb, lse, cat, _ = saved
    _, sc, _ = mod
    d_w_out = _matmul_tn_acc(cat, dy, f"{tag}_out_dw")
    dcat = _matmul(dy, w["w_out"], "nt", BF16, f"{tag}_out_dx")
    dyb = _subseq_views(dcat, A_WIDTH // B_WIDTH, f"{tag}_dyb_views")
    dqs, dks, dvs = zip(*[_attn_bwd(q[b], k[b], v[b], dyb[b], yb[b], lse[b], dil, f"{tag}_attn_bwd_d{dil}")
                          for b, dil in enumerate(DILATIONS)])
    dz, d_gq, d_gk = _qk_prep_bwd(z, dqs, dks, dvs, rope[0], rope[1], w["gq"], w["gk"], w["seg"], f"{tag}_qk_bwd")
    dz, d_sp_w, d_gain, d_bias_exp = _gmlp_bwd(
        z, dcat, w["gain"], w["wtril"], w["wtril_t"], w["bias_exp"], dz, f"{tag}_gmlp_bwd")
    d_w_in_t = _matmul_tn_acc(dz, h, f"{tag}_in_dw")
    token = emit([d_w_in_t, d_w_out])
    dh = _matmul(dz, w["w_in_t"], "nn", F32, f"{tag}_in_dx")
    dx, d_w, d_sh, *dy_below = _modnorm_bwd(x, dh, norm_g, _behind(sc, token), dx_new, below, f"{tag}_norm_bwd")
    return (dx, *dy_below), dict(
        vnorm_g=d_gain.reshape(A_GROUPS, GROUP_DIM), spatial_w=d_sp_w,
        spatial_b=d_bias_exp.reshape(CHUNK, A_GROUPS, GROUP_DIM).sum(-1).T,
        q_norm_g=d_gq.reshape(HEADS, HEAD_DIM).sum(0), k_norm_g=d_gk.reshape(HEADS, HEAD_DIM).sum(0),
        norm_g=d_w * (1.0 + sc), sh=d_sh, sc=d_w * norm_g, gate=d_gate)


def _conformer_forward(x, mod, norm_g, w, tag):
    sh, sc, gate = mod
    h = _modnorm(x, norm_g, sc, sh, f"{tag}_norm")
    p = _matmul(h, w["pw1_t"], "nt", BF16, f"{tag}_pw1", bias=w["pw1_b"])
    s, dc = _conformer_mid(p, w["dw_w"], w["dw_b"], w["ln_g"], w["ln_b"], f"{tag}_mid")
    y, x_new = _matmul(s, w["pw2"], "nn", BF16, f"{tag}_pw2", bias=w["pw2_b"], resid=(x, gate))
    return x_new, (x, h, p, dc, s, y)


def _conformer_backward(dx_new, dy, d_gate, saved, mod, norm_g, w, tag, emit, below):
    x, h, p, dc, s, _ = saved
    _, sc, _ = mod
    d_pw2 = _matmul_tn_acc(s, dy, f"{tag}_pw2_dw")
    d_pw2_b = _colsum_call(dy, f"{tag}_pw2_db")
    ds = _matmul(dy, w["pw2"], "nt", BF16, f"{tag}_pw2_dx")
    ddc, d_dw_w, d_dw_b, d_ln_g, d_ln_b = _conformer_mid_bwd(p, dc, ds, w["ln_g"], w["ln_b"], f"{tag}_mid_bwd")
    dp, d_pw1_b = _conformer_glu_bwd(p, ddc, w["dw_w"], f"{tag}_glu_bwd")
    d_pw1_t = _matmul_tn_acc(dp, h, f"{tag}_pw1_dw")
    token = emit([d_pw1_t, d_pw2])
    dh = _matmul(dp, w["pw1_t"], "nn", F32, f"{tag}_pw1_dx")
    dx, d_w, d_sh, *dy_below = _modnorm_bwd(x, dh, norm_g, _behind(sc, token), dx_new, below, f"{tag}_norm_bwd")
    return (dx, *dy_below), dict(pw1_b=d_pw1_b, dw_w=d_dw_w, dw_b=d_dw_b, ln_g=d_ln_g, ln_b=d_ln_b, pw2_b=d_pw2_b, norm_g=d_w * (1.0 + sc), sh=d_sh, sc=d_w * norm_g, gate=d_gate)


def _local_step(x, target, pos, mod, norm_mix_g, norm_ffn_g, mixer_w, conv_w, ffn_w, fetch, emit):
    d = D_MODEL
    inv_freq = 1.0 / (ROPE_THETA ** (jnp.arange(0, HEAD_DIM, 2, dtype=F32) / HEAD_DIM))
    inv_freq = jnp.tile(inv_freq, 2 * HEADS)[None, :]
    sign = jnp.tile(jnp.concatenate([-jnp.ones(HEAD_DIM // 2, F32), jnp.ones(HEAD_DIM // 2, F32)]), HEADS)[None, :]
    rope = _rope_tables(pos, inv_freq, sign, "rope_tables")
    mods = [[mod[l:l + 1, i * d:(i + 1) * d] for i in range(6)] for l in range(2)]
    mix = [(m[0], m[1], m[2]) for m in mods]
    ffn = [(m[3], m[4], m[5]) for m in mods]
    gm = [norm_mix_g[l:l + 1] for l in range(2)]
    gf = [norm_ffn_g[l:l + 1] for l in range(2)]

    mixer_w = {**mixer_w, **fetch("l0_mix", x)}
    x1, s_mix = _mixer_forward(x, mix[0], gm[0], mixer_w, rope, "l0_mix")
    ffn_w0 = {**ffn_w[0], **fetch("l0_ffn", x1)}
    x2, s_ffn0 = _ffn_forward(x1, ffn[0], gf[0], ffn_w0, "l0_ffn")
    conv_w = {**conv_w, **fetch("l1_conv", x2)}
    x3, s_conv = _conformer_forward(x2, mix[1], gm[1], conv_w, "l1_conv")
    ffn_w1 = {**ffn_w[1], **fetch("l1_ffn", x3)}
    x4, s_ffn1 = _ffn_forward(x3, ffn[1], gf[1], ffn_w1, "l1_ffn")
    below = lambda saved, m: (saved[-1], m[2])
    dx, loss, dy, dg = _loss_head(x4, target, below(s_ffn1, ffn[1]), "loss_head")
    (dx, dy, dg), g_ffn1 = _ffn_backward(dx, dy, dg, s_ffn1, ffn[1], gf[1], ffn_w1, "l1_ffn",
                                         functools.partial(emit, "l1_ffn"), below(s_conv, mix[1]))
    (dx, dy, dg), g_conv = _conformer_backward(dx, dy, dg, s_conv, mix[1], gm[1], conv_w, "l1_conv",
                                               functools.partial(emit, "l1_conv"), below(s_ffn0, ffn[0]))
    (dx, dy, dg), g_ffn0 = _ffn_backward(dx, dy, dg, s_ffn0, ffn[0], gf[0], ffn_w0, "l0_ffn",
                                         functools.partial(emit, "l0_ffn"), below(s_mix, mix[0]))
    (dx,), g_mix = _mixer_backward(dx, dy, dg, s_mix, mix[0], gm[0], mixer_w, rope, "l0_mix",
                                   functools.partial(emit, "l0_mix"), None)
    blocks = [g_mix, g_ffn0, g_conv, g_ffn1]
    dmod = jnp.stack([jnp.concatenate([a["sh"], a["sc"], a["gate"], b["sh"], b["sc"], b["gate"]], axis=1)[0]
                      for a, b in ((g_mix, g_ffn0), (g_conv, g_ffn1))])
    return loss, dx, dmod, blocks


def _pack(arrs, rows=8):
    flat = jnp.concatenate([a.reshape(-1).astype(F32) for a in arrs])
    n = flat.shape[0]
    cols = -(-n // (rows * 128)) * 128
    return jnp.pad(flat, (0, rows * cols - n)).reshape(rows, cols)


def _unpack(flat, shapes):
    out, off = [], 0
    for shp in shapes:
        n = math.prod(shp)
        out.append(flat[..., off:off + n].reshape(flat.shape[:-1] + tuple(shp)))
        off += n
    return out


def _take_block(a, idx, size, axis):
    return lax.dynamic_slice_in_dim(a, idx * size, size, axis)


def kernel(x, c, positions, ada_w, ada_b, norm_mix_g, norm_ffn_g, ab_w_in, a_vnorm_g, a_spatial_w, a_spatial_b, b_q_norm_g, b_k_norm_g, ab_w_out, conv_pw1_w, conv_pw1_b, conv_dw_w, conv_dw_b, conv_ln_g, conv_ln_b, conv_pw2_w, conv_pw2_b, ffn_up_w, ffn_dw_w, ffn_dw_b, ffn_down_w, loss_target, m_ada_w, m_ada_b, m_norm_mix_g, m_norm_ffn_g, m_ab_w_in, m_a_vnorm_g, m_a_spatial_w, m_a_spatial_b, m_b_q_norm_g, m_b_k_norm_g, m_ab_w_out, m_conv_pw1_w, m_conv_pw1_b, m_conv_dw_w, m_conv_dw_b, m_conv_ln_g, m_conv_ln_b, m_conv_pw2_w, m_conv_pw2_b, m_ffn_up_w, m_ffn_dw_w, m_ffn_dw_b, m_ffn_down_w, v_ada_w, v_ada_b, v_norm_mix_g, v_norm_ffn_g, v_ab_w_in, v_a_vnorm_g, v_a_spatial_w, v_a_spatial_b, v_b_q_norm_g, v_b_k_norm_g, v_ab_w_out, v_conv_pw1_w, v_conv_pw1_b, v_conv_dw_w, v_conv_dw_b, v_conv_ln_g, v_conv_ln_b, v_conv_pw2_w, v_conv_pw2_b, v_ffn_up_w, v_ffn_dw_w, v_ffn_dw_b, v_ffn_down_w):
    weights = dict(ada_w=ada_w, ada_b=ada_b, norm_mix_g=norm_mix_g, norm_ffn_g=norm_ffn_g, ab_w_in=ab_w_in, a_vnorm_g=a_vnorm_g, a_spatial_w=a_spatial_w, a_spatial_b=a_spatial_b, b_q_norm_g=b_q_norm_g, b_k_norm_g=b_k_norm_g, ab_w_out=ab_w_out, conv_pw1_w=conv_pw1_w, conv_pw1_b=conv_pw1_b, conv_dw_w=conv_dw_w, conv_dw_b=conv_dw_b, conv_ln_g=conv_ln_g, conv_ln_b=conv_ln_b, conv_pw2_w=conv_pw2_w, conv_pw2_b=conv_pw2_b, ffn_up_w=ffn_up_w, ffn_dw_w=ffn_dw_w, ffn_dw_b=ffn_dw_b, ffn_down_w=ffn_down_w)
    mom1 = dict(ada_w=m_ada_w, ada_b=m_ada_b, norm_mix_g=m_norm_mix_g, norm_ffn_g=m_norm_ffn_g, ab_w_in=m_ab_w_in, a_vnorm_g=m_a_vnorm_g, a_spatial_w=m_a_spatial_w, a_spatial_b=m_a_spatial_b, b_q_norm_g=m_b_q_norm_g, b_k_norm_g=m_b_k_norm_g, ab_w_out=m_ab_w_out, conv_pw1_w=m_conv_pw1_w, conv_pw1_b=m_conv_pw1_b, conv_dw_w=m_conv_dw_w, conv_dw_b=m_conv_dw_b, conv_ln_g=m_conv_ln_g, conv_ln_b=m_conv_ln_b, conv_pw2_w=m_conv_pw2_w, conv_pw2_b=m_conv_pw2_b, ffn_up_w=m_ffn_up_w, ffn_dw_w=m_ffn_dw_w, ffn_dw_b=m_ffn_dw_b, ffn_down_w=m_ffn_down_w)
    mom2 = dict(ada_w=v_ada_w, ada_b=v_ada_b, norm_mix_g=v_norm_mix_g, norm_ffn_g=v_norm_ffn_g, ab_w_in=v_ab_w_in, a_vnorm_g=v_a_vnorm_g, a_spatial_w=v_a_spatial_w, a_spatial_b=v_a_spatial_b, b_q_norm_g=v_b_q_norm_g, b_k_norm_g=v_b_k_norm_g, ab_w_out=v_ab_w_out, conv_pw1_w=v_conv_pw1_w, conv_pw1_b=v_conv_pw1_b, conv_dw_w=v_conv_dw_w, conv_dw_b=v_conv_dw_b, conv_ln_g=v_conv_ln_g, conv_ln_b=v_conv_ln_b, conv_pw2_w=v_conv_pw2_w, conv_pw2_b=v_conv_pw2_b, ffn_up_w=v_ffn_up_w, ffn_dw_w=v_ffn_dw_w, ffn_dw_b=v_ffn_dw_b, ffn_down_w=v_ffn_down_w)
    order = list(weights)
    d, f2 = D_MODEL, 2 * FFN_DIM
    t = x.shape[1]
    me = 4 * lax.axis_index("x") + 2 * lax.axis_index("y") + lax.axis_index("c")
    for window, dil in PATTERNS:
        assert window // dil == Q_BLOCK and t % (dil * Q_BLOCK) == 0

    small_in = [c[0], conv_pw1_b[0], conv_dw_w[0], conv_dw_b[0], conv_ln_g[0], conv_ln_b[0], conv_pw2_b[0], ffn_dw_w]
    g1 = _all_gather_vmem(_pack(small_in, rows=8), "gather_small").reshape(N_DEV, -1)
    c_all, pw1_b, dw_w, dw_b, ln_g, ln_b, pw2_b, fdw_w = _unpack(g1, [a.shape for a in small_in])
    pw1_b, dw_b, ln_g, ln_b, pw2_b = [a.reshape(1, -1) for a in (pw1_b, dw_b, ln_g, ln_b, pw2_b)]
    dw_w = dw_w.transpose(1, 0, 2).reshape(CONV_WIDTH, d)
    fdw_w = fdw_w.transpose(1, 2, 0, 3).reshape(2, FFN_CONV_WIDTH, f2)

    c16 = jnp.pad(c_all, ((0, 2 * N_DEV - c_all.shape[0]), (0, 0)))
    part = jnp.concatenate([_ada_fwd(c16, ada_w[l], f"ada_fwd{l}")[:N_DEV] for l in range(2)], axis=1)
    g2 = _all_gather_vmem(part, "gather_mod").reshape(N_DEV, N_DEV, 2, -1)
    mod = lax.dynamic_index_in_dim(g2, me, axis=1, keepdims=False).transpose(1, 0, 2).reshape(2, 6 * d) + ada_b

    stages = dict(l0_mix=[ab_w_in[0].T, ab_w_out[0]], l0_ffn=[ffn_up_w[0].T, ffn_down_w[0]],
                  l1_conv=[conv_pw1_w[0].T, conv_pw2_w[0]], l1_ffn=[ffn_up_w[1].T, ffn_down_w[1]])
    stages = {k: [s.astype(BF16) for s in v] for k, v in stages.items()}
    names = dict(l0_mix=("w_in_t", "w_out"), l0_ffn=("up_t", "down"), l1_conv=("pw1_t", "pw2"), l1_ffn=("up_t", "down"))
    ready = {"l0_mix": [a.reshape(-1, d) for a in _all_gather_hbm(stages["l0_mix"], "gather_mixer_weights")]}
    behind = (*ready["l0_mix"], mod)
    arriving = {}
    for stage, group in (("l0_ffn", ("l0_ffn",)), ("l1_conv", ("l1_conv", "l1_ffn"))):
        srcs = [s for g in group for s in stages[g]]
        arriving[stage], token = _pushes_start(
            srcs, _landing_zones(srcs, behind, f"gather_{stage}_zones"), False, f"gather_{stage}_start")
        behind = (token,)
        mod = mod + token[0:1, 0:1]

    def fetch(stage, after):
        if stage in arriving:
            full = [a.reshape(-1, d) for a in _pushes_wait(arriving[stage], after, f"gather_{stage}_wait")[1]]
            ready[stage] = full[:2]
            if stage == "l1_conv":
                ready["l1_ffn"] = full[2:]
        return dict(zip(names[stage], ready[stage]))

    causal = jnp.tril(jnp.ones((CHUNK, CHUNK), bool))
    wtril = jnp.where(causal[None], a_spatial_w[0], 0.0)
    mixer_w = dict(
        gain=a_vnorm_g[0].reshape(1, A_WIDTH), wtril=wtril.astype(BF16),
        wtril_t=wtril.transpose(0, 2, 1).astype(BF16),
        bias_exp=jnp.repeat(a_spatial_b[0].T, GROUP_DIM, axis=1),
        gq=jnp.tile(b_q_norm_g[0], HEADS)[None, :], gk=jnp.tile(b_k_norm_g[0], HEADS)[None, :],
        seg=jnp.kron(jnp.eye(HEADS, dtype=BF16), jnp.ones((HEAD_DIM, HEAD_DIM), BF16)))
    conv_w = dict(pw1_b=pw1_b, dw_w=dw_w, dw_b=dw_b, ln_g=ln_g, ln_b=ln_b, pw2_b=pw2_b)
    ffn_w = [dict(dw_w=fdw_w[l].reshape(FFN_CONV_WIDTH, 2, FFN_DIM).transpose(1, 0, 2), dw_b=ffn_dw_b[l].reshape(2, 1, FFN_DIM))
             for l in range(2)]

    leaving = {}

    def emit(stage, grads):
        blocks = [g.reshape(N_DEV, g.shape[0] // N_DEV, d) for g in grads]
        leaving[stage], token = _pushes_start(
            blocks, [lax.empty(b.shape, b.dtype) for b in blocks], True, f"reduce_{stage}_start")
        return token

    loss, dx, dmod, (g_mix, g_ffn0, g_conv, g_ffn1) = _local_step(
        x[0], loss_target[0], positions[0].astype(F32)[:, None], mod, norm_mix_g, norm_ffn_g, mixer_w, conv_w, ffn_w,
        fetch, emit)

    me_op = me.astype(jnp.int32).reshape(1)

    def reduced(stage, after, layer=0, layers=1, into=(None, None)):
        blocks, lands = _pushes_wait(leaving[stage], after, f"reduce_{stage}_wait")
        return [_sum_with_own(b, a, me_op, f"reduce_{stage}_sum{i}", layer, layers, into[i])
                for i, (b, a) in enumerate(zip(blocks, lands))]

    r_ffn = reduced("l1_ffn", dx, 1, 2)
    r_pw1_t, r_pw2 = reduced("l1_conv", dx)
    r_up_t, r_down = reduced("l0_ffn", dx, 0, 2, r_ffn)

    small_g = [
        dmod, jnp.concatenate([g_mix["norm_g"], g_conv["norm_g"]]), jnp.concatenate([g_ffn0["norm_g"], g_ffn1["norm_g"]]),
        g_mix["vnorm_g"], g_mix["spatial_w"], g_mix["spatial_b"], g_mix["q_norm_g"], g_mix["k_norm_g"],
        g_conv["pw1_b"], g_conv["dw_w"], g_conv["dw_b"], g_conv["ln_g"], g_conv["ln_b"], g_conv["pw2_b"],
        jnp.stack([g_ffn0["dw_w"], g_ffn1["dw_w"]]), jnp.concatenate([g_ffn0["dw_b"], g_ffn1["dw_b"]]), loss]
    packed = _pack(small_g, rows=8)
    small_leaving, token = _pushes_start([packed], [lax.empty((N_DEV,) + packed.shape, F32)], False, "gather_small_grads_start")

    grads = dict(conv_pw2_w=r_pw2, ffn_down_w=r_down)
    grads_t = dict(conv_pw1_w=r_pw1_t, ffn_up_w=r_up_t)
    flip = lambda a: jnp.swapaxes(a, 1, 2)
    delta, new_m, new_v = {}, {}, {}

    def update(name, behind):
        if name in grads_t:
            grads[name] = flip(grads_t[name])
            res = _adamw(flip(weights[name]), grads_t[name], flip(mom1[name]), flip(mom2[name]), behind, f"adamw_{name}")
            delta[name], new_m[name], new_v[name] = [flip(r) for r in res]
        else:
            delta[name], new_m[name], new_v[name] = _adamw(
                weights[name], grads[name], mom1[name], mom2[name], behind, f"adamw_{name}")

    for name in ("conv_pw1_w", "conv_pw2_w", "ffn_up_w", "ffn_down_w"):
        update(name, token)
    r_in_t, r_out = reduced("l0_mix", new_v["ffn_down_w"])
    grads_t["ab_w_in"], grads["ab_w_out"] = r_in_t, r_out
    update("ab_w_in", token)
    update("ab_w_out", token)

    (packed,), (landed,) = _pushes_wait(small_leaving, tuple(new_v.values()), "gather_small_grads_wait")
    total = _sum_in_device_order(packed, landed, me_op, "sum_small_grads")
    (s_dmod, s_mix_g, s_ffn_g, s_vnorm, s_sp_w, s_sp_b, s_gq, s_gk, s_pw1_b, s_dw_w, s_dw_b, s_ln_g, s_ln_b,
     s_pw2_b, s_fdw_w, s_fdw_b, s_loss) = _unpack(total.reshape(-1), [a.shape for a in small_g])
    dmod_all = lax.dynamic_update_slice(
        landed.reshape(N_DEV, -1)[:, :dmod.size].reshape((N_DEV,) + dmod.shape), dmod[None], (me, 0, 0))
    n_ada = ada_w.shape[2]
    dmod16 = jnp.pad(_take_block(dmod_all, me, n_ada, 2), ((0, N_DEV), (0, 0), (0, 0)))
    grads.update(
        ada_w=_ada_bwd(c16, dmod16.transpose(1, 0, 2), "ada_bwd"),
        ada_b=s_dmod, norm_mix_g=s_mix_g, norm_ffn_g=s_ffn_g,
        a_vnorm_g=s_vnorm[None], a_spatial_w=s_sp_w[None], a_spatial_b=s_sp_b[None], b_q_norm_g=s_gq[None],
        b_k_norm_g=s_gk[None],
        conv_pw1_b=_take_block(s_pw1_b, me, conv_pw1_b.shape[1], 1),
        conv_dw_w=_take_block(s_dw_w, me, conv_dw_w.shape[2], 1)[None],
        conv_dw_b=_take_block(s_dw_b, me, conv_dw_b.shape[1], 1), conv_ln_g=_take_block(s_ln_g, me, conv_ln_g.shape[1], 1),
        conv_ln_b=_take_block(s_ln_b, me, conv_ln_b.shape[1], 1),
        conv_pw2_b=_take_block(s_pw2_b, me, conv_pw2_b.shape[1], 1),
        ffn_dw_w=_take_block(s_fdw_w, me, ffn_dw_w.shape[2], 2), ffn_dw_b=s_fdw_b)
    update("ada_w", None)
    large = ("ada_w", "conv_pw1_w", "conv_pw2_w", "ffn_up_w", "ffn_down_w", "ab_w_in", "ab_w_out")
    small = [n for n in order if n not in large]
    res = _adamw_small(*[[src[n] for n in small] for src in (weights, grads, mom1, mom2)], "adamw_small")
    for dst, arrs in zip((delta, new_m, new_v), res):
        dst.update(zip(small, arrs))

    return (s_loss[0, 0], dx[None], *[grads[n] for n in order], *[delta[n] for n in order],
            *[new_m[n] for n in order], *[new_v[n] for n in order])
```

```python
import functools
import math

import jax
import jax.numpy as jnp
from jax import lax
from jax.experimental import pallas as pl
from jax.experimental.pallas import tpu as pltpu

F32 = jnp.float32
BF16 = jnp.bfloat16
MESH = pl.DeviceIdType.MESH

D_MODEL = 1024
A_WIDTH = 512
A_GROUPS = 4
GROUP_DIM = 128
CHUNK = 128
B_WIDTH = 512
HEADS = 8
HEAD_DIM = 64
PATTERNS = ((128, 1), (512, 4), (2048, 16))
Q_BLOCK = 128
ROPE_THETA = 10000.0
AB_IN = 2560
CONV_WIDTH = 31
FFN_DIM = 2816
FFN_CONV_WIDTH = 3
EPS = 1e-6
NEG = -1e30
N_DEV = 8
ADAM_LR, ADAM_B1, ADAM_B2, ADAM_EPS, ADAM_WD, ADAM_STEP = 0.001, 0.9, 0.999, 1e-08, 0.01, 10

V7X_VMEM_LIMIT = 56 * 2**20
FFN_HALO = 16
CONV_HALO = 32

_NN = (((1,), (0,)), ((), ()))
_NT = (((1,), (1,)), ((), ()))
_TN = (((0,), (0,)), ((), ()))


def _tile(n, prefs=(512, 256, 128)):
    for t in prefs:
        if n % t == 0:
            return t
    return n


def _row_tile(n, cap=512):
    best = n
    for t in range(8, min(n, cap) + 1, 8):
        if n % t == 0:
            best = t
    return best if best <= cap else n


def _params(*sem):
    return pltpu.CompilerParams(dimension_semantics=sem, vmem_limit_bytes=V7X_VMEM_LIMIT)


def _dot(a, b, dims):
    return lax.dot_general(a, b, dims, preferred_element_type=F32)


def _sigmoid(x):
    return 1.0 / (1.0 + jnp.exp(-x))


def _gelu(x):
    return 0.5 * x * (1.0 + lax.erf(x * (2.0 ** -0.5)))


def _gelu_grad(x):
    return 0.5 * (1.0 + lax.erf(x * (2.0 ** -0.5))) + x * jnp.exp(-0.5 * x * x) * (1.0 / math.sqrt(2.0 * math.pi))


def _colsum(v):
    return jnp.sum(v, axis=0, keepdims=True)


MATMUL_VMEM_BUDGET = 40 * 2**20


def _matmul_tiles(m, n, k, out_bytes, with_resid):
    def options(dim):
        opts = [t for t in (1024, 512, 256, 128) if dim % t == 0]
        return opts + [dim] if dim <= 4096 and dim not in opts else opts

    best = None
    for tm in options(m):
        for tn in options(n):
            need = 4 * (tm * k + k * tn) + tm * tn * (4 + 2 * out_bytes) + (24 * tm * tn if with_resid else 0)
            if need <= MATMUL_VMEM_BUDGET and (best is None or tm * tn / (tm + tn) > best[0]):
                best = (tm * tn / (tm + tn), tm, tn)
    return best[1], best[2]


def _matmul_tn_acc(a, b, name, tk=1024):
    squeeze = a.ndim == 2
    a3 = a[None] if squeeze else a
    p_, t, m = a3.shape
    n = b.shape[1]
    nk = t // tk

    def body(a_ref, b_ref, o_ref, acc_ref):
        kt = pl.program_id(1)

        @pl.when(kt == 0)
        def _():
            acc_ref[...] = jnp.zeros_like(acc_ref)

        acc_ref[...] += _dot(a_ref[...], b_ref[...], _TN)

        @pl.when(kt == nk - 1)
        def _():
            o_ref[...] = acc_ref[...].astype(BF16)

    out = pl.pallas_call(
        body, name=name, grid=(p_, nk),
        in_specs=[pl.BlockSpec((None, tk, m), lambda p, kt: (p, kt, 0)), pl.BlockSpec((tk, n), lambda p, kt: (kt, 0))],
        out_specs=pl.BlockSpec((None, m, n), lambda p, kt: (p, 0, 0)), out_shape=jax.ShapeDtypeStruct((p_, m, n), BF16),
        scratch_shapes=[pltpu.VMEM((m, n), F32)], compiler_params=_params("parallel", "arbitrary"),
    )(a3, b)
    return out[0] if squeeze else out


def _matmul(a, b, mode, out_dtype, name, bias=None, resid=None):
    if mode == "nn":
        (m, k), (_, n) = a.shape, b.shape
    elif mode == "nt":
        (m, k), (n, _) = a.shape, b.shape
    else:
        (k, m), (_, n) = a.shape, b.shape
    tm, tn = _matmul_tiles(m, n, k, jnp.dtype(out_dtype).itemsize, resid is not None)
    dims = {"nn": _NN, "nt": _NT, "tn": _TN}[mode]
    a_spec = pl.BlockSpec((k, tm), lambda i, j: (0, i)) if mode == "tn" else pl.BlockSpec((tm, k), lambda i, j: (i, 0))
    b_spec = pl.BlockSpec((tn, k), lambda i, j: (j, 0)) if mode == "nt" else pl.BlockSpec((k, tn), lambda i, j: (0, j))
    in_specs, args = [a_spec, b_spec], [a, b]
    row_spec = pl.BlockSpec((1, tn), lambda i, j: (0, j))
    tile_spec = pl.BlockSpec((tm, tn), lambda i, j: (i, j))
    if bias is not None:
        in_specs.append(row_spec)
        args.append(bias)
    if resid is not None:
        in_specs += [tile_spec, row_spec]
        args += list(resid)
    out_shape = [jax.ShapeDtypeStruct((m, n), out_dtype)]
    out_specs = [tile_spec]
    if resid is not None:
        out_shape.append(jax.ShapeDtypeStruct((m, n), F32))
        out_specs.append(tile_spec)

    def body(*refs):
        a_ref, b_ref = refs[0], refs[1]
        pos = 2
        acc = _dot(a_ref[...], b_ref[...], dims)
        if bias is not None:
            acc = acc + refs[pos][...]
            pos += 1
        if resid is not None:
            x_ref, g_ref = refs[pos], refs[pos + 1]
            pos += 2
        refs[pos][...] = acc.astype(out_dtype)
        if resid is not None:
            refs[pos + 1][...] = x_ref[...] + g_ref[...] * acc

    outs = pl.pallas_call(
        body, name=name, grid=(m // tm, n // tn), in_specs=in_specs, out_specs=out_specs, out_shape=out_shape,
        compiler_params=_params("parallel", "parallel"),
    )(*args)
    return outs if resid is not None else outs[0]


NORM_TM = (1024, 512, 256, 128)


def _modnorm(x, g, sc, sh, name):
    t, d = x.shape
    tm = _tile(t, NORM_TM)
    row = pl.BlockSpec((1, d), lambda i: (0, 0))
    blk = pl.BlockSpec((tm, d), lambda i: (i, 0))

    def body(x_ref, g_ref, sc_ref, sh_ref, o_ref):
        x = x_ref[...]
        r = lax.rsqrt(jnp.mean(x * x, axis=-1, keepdims=True) + EPS)
        o_ref[...] = ((x * r) * g_ref[...] * (1.0 + sc_ref[...]) + sh_ref[...]).astype(BF16)

    return pl.pallas_call(
        body, name=name, grid=(t // tm,), in_specs=[blk, row, row, row], out_specs=blk,
        out_shape=jax.ShapeDtypeStruct((t, d), BF16), compiler_params=_params("parallel"),
    )(x, g, sc, sh)


def _gate_bwd_tile(dx, y_ref, gate_ref, dy_ref, dgate_ref, first):
    @pl.when(first)
    def _():
        dgate_ref[...] = jnp.zeros_like(dgate_ref)

    dy_ref[...] = (dx * gate_ref[...]).astype(BF16)
    dgate_ref[...] += _colsum(dx * y_ref[...].astype(F32))


def _modnorm_bwd(x, dh, g, sc, dres, below, name):
    t, d = x.shape
    tm = _tile(t, NORM_TM)
    row = pl.BlockSpec((1, d), lambda i: (0, 0))
    blk = pl.BlockSpec((tm, d), lambda i: (i, 0))

    def body(x_ref, dh_ref, g_ref, sc_ref, dres_ref, *rest):
        dx_ref, dw_ref, dsh_ref = rest[-5:-2] if below else rest
        first = pl.program_id(0) == 0

        @pl.when(first)
        def _():
            dw_ref[...] = jnp.zeros_like(dw_ref)
            dsh_ref[...] = jnp.zeros_like(dsh_ref)

        x = x_ref[...]
        dh = dh_ref[...].astype(F32)
        r = lax.rsqrt(jnp.mean(x * x, axis=-1, keepdims=True) + EPS)
        xn = x * r
        dxn = dh * (g_ref[...] * (1.0 + sc_ref[...]))
        dx = dres_ref[...] + r * (dxn - xn * jnp.mean(dxn * xn, axis=-1, keepdims=True))
        dx_ref[...] = dx
        dw_ref[...] += _colsum(dh * xn)
        dsh_ref[...] += _colsum(dh)
        if below:
            _gate_bwd_tile(dx, rest[0], rest[1], rest[-2], rest[-1], first)

    row_out = jax.ShapeDtypeStruct((1, d), F32)
    return pl.pallas_call(
        body, name=name, grid=(t // tm,), in_specs=[blk, blk, row, row, blk] + ([blk, row] if below else []),
        out_specs=[blk, row, row] + ([blk, row] if below else []),
        out_shape=[jax.ShapeDtypeStruct((t, d), F32), row_out, row_out]
        + ([jax.ShapeDtypeStruct((t, d), BF16), row_out] if below else []),
        compiler_params=_params("arbitrary"),
    )(x, dh, g, sc, dres, *(below or ()))


def _loss_head(y, target, below, name):
    t, d = y.shape
    tm = _tile(t, NORM_TM)
    blk = pl.BlockSpec((tm, d), lambda i: (i, 0))
    row = pl.BlockSpec((1, d), lambda i: (0, 0))
    one = pl.BlockSpec((1, 1), lambda i: (0, 0))
    steps = t // tm

    def body(y_ref, t_ref, yb_ref, gate_ref, dx_ref, loss_ref, dy_ref, dgate_ref, acc_ref):
        first = pl.program_id(0) == 0

        @pl.when(first)
        def _():
            acc_ref[...] = jnp.zeros_like(acc_ref)

        e = y_ref[...] - t_ref[...]
        dx = e * (1.0 / d)
        dx_ref[...] = dx
        acc_ref[...] += _colsum(e * e)
        _gate_bwd_tile(dx, yb_ref, gate_ref, dy_ref, dgate_ref, first)

        @pl.when(pl.program_id(0) == steps - 1)
        def _():
            loss_ref[...] = jnp.sum(acc_ref[...], axis=1, keepdims=True) * (0.5 / d)

    return pl.pallas_call(
        body, name=name, grid=(steps,), in_specs=[blk, blk, blk, row], out_specs=[blk, one, blk, row],
        out_shape=[jax.ShapeDtypeStruct((t, d), F32), jax.ShapeDtypeStruct((1, 1), F32),
                   jax.ShapeDtypeStruct((t, d), BF16), jax.ShapeDtypeStruct((1, d), F32)],
        scratch_shapes=[pltpu.VMEM((1, d), F32)], compiler_params=_params("arbitrary"),
    )(y, target, *below)


GMLP_TM = 512


def _group_norm(vg, gain):
    mu = jnp.mean(vg, axis=-1, keepdims=True)
    xc = vg - mu
    rstd = lax.rsqrt(jnp.mean(xc * xc, axis=-1, keepdims=True) + EPS)
    xhat = xc * rstd
    return xhat, rstd, xhat * gain


def _gmlp_fwd(z, gain, wtril, bias_exp, name):
    t = z.shape[0]
    tm = _tile(t, (GMLP_TM,))
    zu = pl.BlockSpec((tm, A_WIDTH), lambda i: (i, 0))
    zv = pl.BlockSpec((tm, A_WIDTH), lambda i: (i, 1))
    full2 = lambda shp: pl.BlockSpec(shp, lambda i: (0, 0))
    w_spec = pl.BlockSpec((A_GROUPS, CHUNK, CHUNK), lambda i: (0, 0, 0))

    def body(zu_ref, zv_ref, gain_ref, w_ref, b_ref, ya_ref):
        for c in range(tm // CHUNK):
            rows = slice(c * CHUNK, (c + 1) * CHUNK)
            ua = _gelu(zu_ref[rows, :].astype(F32))
            vg = _gelu(zv_ref[rows, :].astype(F32))
            for g in range(A_GROUPS):
                sl = slice(g * GROUP_DIM, (g + 1) * GROUP_DIM)
                _, _, vn = _group_norm(vg[:, sl], gain_ref[:, sl])
                f = _dot(w_ref[g], vn.astype(BF16), _NN) + b_ref[:, sl]
                ya_ref[rows, sl] = (ua[:, sl] * f).astype(BF16)

    return pl.pallas_call(
        body, name=name, grid=(t // tm,),
        in_specs=[zu, zv, full2((1, A_WIDTH)), w_spec, full2((CHUNK, A_WIDTH))], out_specs=zu,
        out_shape=jax.ShapeDtypeStruct((t, A_WIDTH + B_WIDTH), BF16), compiler_params=_params("parallel"),
    )(z, z, gain, wtril, bias_exp)


def _gmlp_bwd(z, dcat, gain, wtril, wtril_t, bias_exp, dz, name):
    t = z.shape[0]
    tm = _tile(t, (GMLP_TM,))
    zu = pl.BlockSpec((tm, A_WIDTH), lambda i: (i, 0))
    zv = pl.BlockSpec((tm, A_WIDTH), lambda i: (i, 1))
    full2 = lambda shp: pl.BlockSpec(shp, lambda i: (0, 0))
    w_spec = pl.BlockSpec((A_GROUPS, CHUNK, CHUNK), lambda i: (0, 0, 0))
    dz_spec = pl.BlockSpec((tm, 2 * A_WIDTH), lambda i: (i, 0))

    def body(zu_ref, zv_ref, dya_ref, gain_ref, w_ref, wt_ref, b_ref, _, dz_ref, dw_ref, dgain_ref, dbias_ref):
        @pl.when(pl.program_id(0) == 0)
        def _():
            dw_ref[...] = jnp.zeros_like(dw_ref)
            dgain_ref[...] = jnp.zeros_like(dgain_ref)
            dbias_ref[...] = jnp.zeros_like(dbias_ref)

        row = lax.broadcasted_iota(jnp.int32, (CHUNK, CHUNK), 0)
        col = lax.broadcasted_iota(jnp.int32, (CHUNK, CHUNK), 1)
        for c in range(tm // CHUNK):
            rows = slice(c * CHUNK, (c + 1) * CHUNK)
            zu_v = zu_ref[rows, :].astype(F32)
            zv_v = zv_ref[rows, :].astype(F32)
            dya = dya_ref[rows, :].astype(F32)
            ua = _gelu(zu_v)
            vg = _gelu(zv_v)
            for g in range(A_GROUPS):
                sl = slice(g * GROUP_DIM, (g + 1) * GROUP_DIM)
                gain_g = gain_ref[:, sl]
                xhat, rstd, vn = _group_norm(vg[:, sl], gain_g)
                vn16 = vn.astype(BF16)
                f = _dot(w_ref[g], vn16, _NN) + b_ref[:, sl]
                df = dya[:, sl] * ua[:, sl]
                df16 = df.astype(BF16)
                dz_ref[rows, sl] = (dya[:, sl] * f * _gelu_grad(zu_v[:, sl])).astype(BF16)
                dw_ref[g] += jnp.where(row >= col, _dot(df16, vn16, _NT), 0.0)
                dvn = _dot(wt_ref[g], df16, _NN)
                dgain_ref[:, sl] += _colsum(dvn * xhat)
                dxh = dvn * gain_g
                dvg = rstd * (dxh - jnp.mean(dxh, axis=-1, keepdims=True) - xhat * jnp.mean(dxh * xhat, axis=-1, keepdims=True))
                dz_ref[rows, A_WIDTH + g * GROUP_DIM:A_WIDTH + (g + 1) * GROUP_DIM] = (dvg * _gelu_grad(zv_v[:, sl])).astype(BF16)
                dbias_ref[:, sl] += df

    return pl.pallas_call(
        body, name=name, grid=(t // tm,),
        in_specs=[zu, zv, zu, full2((1, A_WIDTH)), w_spec, w_spec, full2((CHUNK, A_WIDTH)), pl.BlockSpec(memory_space=pl.ANY)],
        out_specs=[dz_spec, w_spec, full2((1, A_WIDTH)), full2((CHUNK, A_WIDTH))],
        out_shape=[jax.ShapeDtypeStruct(dz.shape, BF16), jax.ShapeDtypeStruct((A_GROUPS, CHUNK, CHUNK), F32),
                   jax.ShapeDtypeStruct((1, A_WIDTH), F32), jax.ShapeDtypeStruct((CHUNK, A_WIDTH), F32)],
        input_output_aliases={7: 0}, compiler_params=_params("arbitrary"),
    )(z, z, dcat, gain, wtril, wtril_t, bias_exp, dz)


def _rope_tables(pos, inv_freq, sign, name):
    t = pos.shape[0]
    tm = _tile(t)
    row = pl.BlockSpec((1, B_WIDTH), lambda i: (0, 0))
    blk = pl.BlockSpec((tm, B_WIDTH), lambda i: (i, 0))

    def body(pos_ref, f_ref, s_ref, cos_ref, sin_ref):
        ang = pos_ref[...] * f_ref[:, 0:LANES]
        cos_ref[...] = jnp.tile(jnp.cos(ang), (1, B_WIDTH // LANES))
        sin_ref[...] = jnp.tile(jnp.sin(ang) * s_ref[:, 0:LANES], (1, B_WIDTH // LANES))

    return pl.pallas_call(
        body, name=name, grid=(t // tm,), in_specs=[pl.BlockSpec((tm, 1), lambda i: (i, 0)), row, row],
        out_specs=[blk, blk], out_shape=[jax.ShapeDtypeStruct((t, B_WIDTH), F32)] * 2,
        compiler_params=_params("parallel"),
    )(pos, inv_freq, sign)


def _head_sum(v, seg):
    hi = v.astype(BF16)
    lo = (v - hi.astype(F32)).astype(BF16)
    return _dot(hi, seg, _NN) + _dot(lo, seg, _NN)


def _swap_halves(v):
    lane = lax.broadcasted_iota(jnp.int32, v.shape, 1)
    return jnp.where((lane & (HEAD_DIM - 1)) < HEAD_DIM // 2,pltpu.roll(v, B_WIDTH - HEAD_DIM // 2, 1), pltpu.roll(v, HEAD_DIM // 2, 1))


DILATIONS = tuple(dil for _, dil in PATTERNS)
SUBSEQ_TM = 512
LANES = 128


def _subseq_shape(t, dil):
    return (t // dil, dil * B_WIDTH)


def _subseq_spec(tm, dil):
    return pl.BlockSpec((tm // dil, dil * B_WIDTH), lambda i: (i, 0))


def _to_subseq(x, scr_ref, dil):
    if dil == 1:
        return x
    tm, w = x.shape
    for c in range(w // LANES):
        scr_ref[c * tm:(c + 1) * tm, :] = x[:, c * LANES:(c + 1) * LANES]
    return jnp.concatenate([scr_ref[pl.ds(c * tm + r, tm // dil, stride=dil), :]
                            for r in range(dil) for c in range(w // LANES)], axis=1)


def _from_subseq(y, scr_ref, dil):
    if dil == 1:
        return y
    n, w = y.shape[0], y.shape[1] // dil
    tm = n * dil
    for r in range(dil):
        for c in range(w // LANES):
            scr_ref[pl.ds(c * tm + r, n, stride=dil), :] = y[:, r * w + c * LANES:r * w + (c + 1) * LANES]
    return jnp.concatenate([scr_ref[c * tm:(c + 1) * tm, :] for c in range(w // LANES)], axis=1)


def _subseq_scratch(tm):
    return pltpu.VMEM((B_WIDTH // LANES * tm, LANES), F32)


def _qk_prep(z, cos_t, sin_t, gq, gk, seg, name):
    t = z.shape[0]
    tm = _tile(t, (SUBSEQ_TM,))
    col = lambda c: pl.BlockSpec((tm, B_WIDTH), lambda i: (i, c))
    row = pl.BlockSpec((1, B_WIDTH), lambda i: (0, 0))
    blk = col(0)
    nd = len(DILATIONS)

    def body(q_ref, k_ref, v_ref, cos_ref, sin_ref, gq_ref, gk_ref, seg_ref, *rest):
        out_refs, scr_ref = rest[:-1], rest[-1]

        def norm_rot(x, g):
            r = lax.rsqrt(_head_sum(x * x, seg_ref[...]) * (1.0 / HEAD_DIM) + EPS)
            xn = x * r * g
            return xn * cos_ref[...] + _swap_halves(xn) * sin_ref[...]

        vals = (norm_rot(q_ref[...].astype(F32), gq_ref[...]), norm_rot(k_ref[...].astype(F32), gk_ref[...]),
                v_ref[...].astype(F32))
        for a, val in enumerate(vals):
            for b, dil in enumerate(DILATIONS):
                out_refs[a * nd + b][...] = _to_subseq(val, scr_ref, dil).astype(BF16)

    outs = pl.pallas_call(
        body, name=name, grid=(t // tm,),
        in_specs=[col(2), col(3), col(4), blk, blk, row, row, pl.BlockSpec((B_WIDTH, B_WIDTH), lambda i: (0, 0))],
        out_specs=[_subseq_spec(tm, dil) for _ in range(3) for dil in DILATIONS],
        out_shape=[jax.ShapeDtypeStruct(_subseq_shape(t, dil), BF16) for _ in range(3) for dil in DILATIONS],
        scratch_shapes=[_subseq_scratch(tm)], compiler_params=_params("parallel"),
    )(z, z, z, cos_t, sin_t, gq, gk, seg)
    return outs[:nd], outs[nd:2 * nd], outs[2 * nd:]


def _qk_prep_bwd(z, dqs, dks, dvs, cos_t, sin_t, gq, gk, seg, name):
    t = z.shape[0]
    tm = _tile(t, (SUBSEQ_TM,))
    a = 2 * A_WIDTH
    col = lambda c: pl.BlockSpec((tm, B_WIDTH), lambda i: (i, c))
    row = pl.BlockSpec((1, B_WIDTH), lambda i: (0, 0))
    blk = col(0)
    nb = len(DILATIONS)
    subs = [_subseq_spec(tm, dil) for dil in DILATIONS]

    def body(*refs):
        q_ref, k_ref = refs[0], refs[1]
        dq_refs, dk_refs, dv_refs = refs[2:2 + nb], refs[2 + nb:2 + 2 * nb], refs[2 + 2 * nb:2 + 3 * nb]
        cos_ref, sin_ref, gq_ref, gk_ref, seg_ref, dz_ref, dgq_ref, dgk_ref, scr_ref = refs[2 + 3 * nb:]

        @pl.when(pl.program_id(0) == 0)
        def _():
            dgq_ref[...] = jnp.zeros_like(dgq_ref)
            dgk_ref[...] = jnp.zeros_like(dgk_ref)

        def total(d_refs):
            return sum(_from_subseq(r_[...], scr_ref, dil) for r_, dil in zip(d_refs, DILATIONS))

        def back(x, d_refs, g, dg_ref):
            dout = total(d_refs)
            dy = dout * cos_ref[...] + _swap_halves(dout * sin_ref[...])
            r = lax.rsqrt(_head_sum(x * x, seg_ref[...]) * (1.0 / HEAD_DIM) + EPS)
            xn = x * r
            dg_ref[...] += _colsum(dy * xn)
            dxn = dy * g
            return r * (dxn - xn * (_head_sum(dxn * xn, seg_ref[...]) * (1.0 / HEAD_DIM)))

        dz_ref[:, a:a + B_WIDTH] = back(q_ref[...].astype(F32), dq_refs, gq_ref[...], dgq_ref).astype(BF16)
        dz_ref[:, a + B_WIDTH:a + 2 * B_WIDTH] = back(k_ref[...].astype(F32), dk_refs, gk_ref[...], dgk_ref).astype(BF16)
        dz_ref[:, a + 2 * B_WIDTH:a + 3 * B_WIDTH] = total(dv_refs).astype(BF16)

    return pl.pallas_call(
        body, name=name, grid=(t // tm,),
        in_specs=[col(2), col(3)] + subs * 3 + [blk, blk, row, row, pl.BlockSpec((B_WIDTH, B_WIDTH), lambda i: (0, 0))],
        out_specs=[pl.BlockSpec((tm, a + 3 * B_WIDTH), lambda i: (i, 0)), row, row],
        out_shape=[jax.ShapeDtypeStruct((t, a + 3 * B_WIDTH), BF16), jax.ShapeDtypeStruct((1, B_WIDTH), F32),
                   jax.ShapeDtypeStruct((1, B_WIDTH), F32)],
        scratch_shapes=[_subseq_scratch(tm)], compiler_params=_params("arbitrary"),
    )(z, z, *dqs, *dks, *dvs, cos_t, sin_t, gq, gk, seg)


def _subseq_views(x, col, name):
    t = x.shape[0]
    tm = _tile(t, (SUBSEQ_TM,))

    def body(x_ref, *rest):
        out_refs, scr_ref = rest[:-1], rest[-1]
        val = x_ref[...].astype(F32)
        for o_ref, dil in zip(out_refs, DILATIONS):
            o_ref[...] = _to_subseq(val, scr_ref, dil).astype(o_ref.dtype)

    return pl.pallas_call(
        body, name=name, grid=(t // tm,), in_specs=[pl.BlockSpec((tm, B_WIDTH), lambda i: (i, col))],
        out_specs=[_subseq_spec(tm, dil) for dil in DILATIONS],
        out_shape=[jax.ShapeDtypeStruct(_subseq_shape(t, dil), x.dtype) for dil in DILATIONS],
        scratch_shapes=[_subseq_scratch(tm)], compiler_params=_params("parallel"),
    )(x)


ATTN_FWD_BLOCKS = 1
ATTN_BWD_BLOCKS = 2


def _attn_step_specs(nb, want):
    ns = want if nb % want == 0 else 1
    cur = pl.BlockSpec((ns * Q_BLOCK, B_WIDTH), lambda r, i: (i, r))
    prev = pl.BlockSpec((Q_BLOCK, B_WIDTH), lambda r, i: (jnp.maximum(ns * i - 1, 0), r))
    return ns, cur, prev


def _attn_fwd(q, k, v, dil, name):
    t = q.shape[0] * dil
    nb = t // dil // Q_BLOCK
    ns, cur, prev = _attn_step_specs(nb, ATTN_FWD_BLOCKS)

    def body(q_ref, kp_ref, kc_ref, vp_ref, vc_ref, o_ref, lse_ref):
        i = pl.program_id(1)
        a = lax.broadcasted_iota(jnp.int32, (Q_BLOCK, 2 * Q_BLOCK), 0)
        j = lax.broadcasted_iota(jnp.int32, (Q_BLOCK, 2 * Q_BLOCK), 1)
        dist = a + Q_BLOCK - j
        band = (dist >= 0) & (dist <= Q_BLOCK)
        sls = [slice(h * HEAD_DIM, (h + 1) * HEAD_DIM) for h in range(HEADS)]
        for sb in range(ns):
            rows = slice(sb * Q_BLOCK, (sb + 1) * Q_BLOCK)
            before = slice((sb - 1) * Q_BLOCK, sb * Q_BLOCK)
            q = q_ref[rows, :]
            kk = jnp.concatenate([kp_ref[...] if sb == 0 else kc_ref[before, :], kc_ref[rows, :]], axis=0)
            vv = jnp.concatenate([vp_ref[...] if sb == 0 else vc_ref[before, :], vc_ref[rows, :]], axis=0)
            mask = band & ((j >= Q_BLOCK) | (i > 0)) if sb == 0 else band
            scores = [_dot(q[:, sl], kk[:, sl], _NT) for sl in sls]
            ps, dens = [], []
            for sl, s in zip(sls, scores):
                s = jnp.where(mask, s * (HEAD_DIM ** -0.5), NEG)
                m = jnp.max(s, axis=-1, keepdims=True)
                p = jnp.exp(s - m)
                den = jnp.sum(p, axis=-1, keepdims=True)
                ps.append(p.astype(BF16))
                dens.append(den)
                lse_ref[rows, sl] = jnp.broadcast_to(m + jnp.log(den), (Q_BLOCK, HEAD_DIM))
            for sl, p, den in zip(sls, ps, dens):
                o_ref[rows, sl] = _dot(p, vv[:, sl], _NN) / den

    return pl.pallas_call(
        body, name=name, grid=(dil, nb // ns), in_specs=[cur, prev, cur, prev, cur], out_specs=[cur, cur],
        out_shape=[jax.ShapeDtypeStruct(_subseq_shape(t, dil), F32)] * 2,
        compiler_params=_params("parallel", "parallel"),
    )(q, k, k, v, v)


def _attn_merge(outs, lses, cat, name):
    nb = len(DILATIONS)
    t = cat.shape[0]
    tm = _tile(t, (SUBSEQ_TM,))
    subs = [_subseq_spec(tm, dil) for dil in DILATIONS]

    def body(*refs):
        o_refs, l_refs = refs[:nb], refs[nb:2 * nb]
        yb_refs, lse_refs, cat_ref, scr_ref = refs[2 * nb + 1:3 * nb + 1], refs[3 * nb + 1:4 * nb + 1], refs[4 * nb + 1], refs[4 * nb + 2]
        ls = [_from_subseq(r[...], scr_ref, dil) for r, dil in zip(l_refs, DILATIONS)]
        m = functools.reduce(jnp.maximum, ls)
        tot = m + jnp.log(sum(jnp.exp(l - m) for l in ls))
        yb = sum(jnp.exp(l - tot) * _from_subseq(o[...], scr_ref, dil) for l, o, dil in zip(ls, o_refs, DILATIONS))
        cat_ref[...] = yb.astype(BF16)
        yb = yb.astype(BF16).astype(F32)
        for yb_ref, lse_ref, dil in zip(yb_refs, lse_refs, DILATIONS):
            yb_ref[...] = _to_subseq(yb, scr_ref, dil).astype(BF16)
            lse_ref[...] = _to_subseq(tot, scr_ref, dil)

    outs_ = pl.pallas_call(
        body, name=name, grid=(t // tm,), in_specs=subs * 2 + [pl.BlockSpec(memory_space=pl.ANY)],
        out_specs=subs * 2 + [pl.BlockSpec((tm, B_WIDTH), lambda i: (i, A_WIDTH // B_WIDTH))],
        out_shape=[jax.ShapeDtypeStruct(_subseq_shape(t, dil), BF16) for dil in DILATIONS]
        + [jax.ShapeDtypeStruct(_subseq_shape(t, dil), F32) for dil in DILATIONS] + [jax.ShapeDtypeStruct(cat.shape, BF16)],
        input_output_aliases={2 * nb: 2 * nb}, scratch_shapes=[_subseq_scratch(tm)], compiler_params=_params("parallel"),
    )(*outs, *lses, cat)
    return outs_[:nb], outs_[nb:2 * nb], outs_[2 * nb]


def _attn_bwd(q, k, v, do, o, lse, dil, name):
    t = q.shape[0] * dil
    nb = t // dil // Q_BLOCK
    ns, cur, prev = _attn_step_specs(nb, ATTN_BWD_BLOCKS)
    scale = HEAD_DIM ** -0.5

    def body(q_ref, kp_ref, kc_ref, vp_ref, vc_ref, do_ref, o_ref, lse_ref, dq_ref, dk_ref, dv_ref,
             ck_ref, cv_ref, tk_ref, tv_ref):
        step = pl.program_id(1)

        @pl.when(step == 0)
        def _():
            ck_ref[...] = jnp.zeros_like(ck_ref)
            cv_ref[...] = jnp.zeros_like(cv_ref)

        a = lax.broadcasted_iota(jnp.int32, (Q_BLOCK, 2 * Q_BLOCK), 0)
        j = lax.broadcasted_iota(jnp.int32, (Q_BLOCK, 2 * Q_BLOCK), 1)
        dist = a + Q_BLOCK - j
        band = (dist >= 0) & (dist <= Q_BLOCK)
        sls = [slice(h * HEAD_DIM, (h + 1) * HEAD_DIM) for h in range(HEADS)]
        for sb in range(ns):
            i = ns * step + sb
            rows = slice(sb * Q_BLOCK, (sb + 1) * Q_BLOCK)
            before = slice((sb - 1) * Q_BLOCK, sb * Q_BLOCK)
            q = q_ref[rows, :]
            kk = jnp.concatenate([kp_ref[...] if sb == 0 else kc_ref[before, :], kc_ref[rows, :]], axis=0)
            vv = jnp.concatenate([vp_ref[...] if sb == 0 else vc_ref[before, :], vc_ref[rows, :]], axis=0)
            do = do_ref[rows, :]
            dof = do.astype(F32)
            of = o_ref[rows, :].astype(F32)
            mask = band & ((j >= Q_BLOCK) | (step > 0)) if sb == 0 else band
            scores = [_dot(q[:, sl], kk[:, sl], _NT) for sl in sls]
            dps = [_dot(do[:, sl], vv[:, sl], _NT) for sl in sls]
            ps, dss = [], []
            for sl, s, dp in zip(sls, scores, dps):
                p = jnp.exp(jnp.where(mask, s * scale, NEG) - lse_ref[rows, sl.start:sl.start + 1])
                delta = jnp.sum(dof[:, sl] * of[:, sl], axis=-1, keepdims=True)
                dss.append((p * (dp - delta) * scale).astype(BF16))
                ps.append(p.astype(BF16))
            for sl, p, ds in zip(sls, ps, dss):
                dq_ref[rows, sl] = _dot(ds, kk[:, sl], _NN)
                dv_t = _dot(do[:, sl], p, _TN)
                dk_t = _dot(q[:, sl], ds, _TN)
                tk_ref[sl, :] = ck_ref[sl, :] + dk_t[:, :Q_BLOCK]
                tv_ref[sl, :] = cv_ref[sl, :] + dv_t[:, :Q_BLOCK]
                ck_ref[sl, :] = dk_t[:, Q_BLOCK:]
                cv_ref[sl, :] = dv_t[:, Q_BLOCK:]

            @pl.when(i >= 1)
            def _():
                done = pl.ds(pl.multiple_of((i - 1) * Q_BLOCK, Q_BLOCK), Q_BLOCK)
                dk_ref[done, :] = tk_ref[...].T
                dv_ref[done, :] = tv_ref[...].T

        @pl.when(step == nb // ns - 1)
        def _():
            done = pl.ds((nb - 1) * Q_BLOCK, Q_BLOCK)
            dk_ref[done, :] = ck_ref[...].T
            dv_ref[done, :] = cv_ref[...].T

    whole = pl.BlockSpec((t // dil, B_WIDTH), lambda r, i: (0, r))
    return pl.pallas_call(
        body, name=name, grid=(dil, nb // ns), in_specs=[cur, prev, cur, prev, cur, cur, cur, cur],
        out_specs=[cur, whole, whole], out_shape=[jax.ShapeDtypeStruct(_subseq_shape(t, dil), F32)] * 3,
        scratch_shapes=[pltpu.VMEM((B_WIDTH, Q_BLOCK), F32)] * 4,
        compiler_params=_params("parallel", "arbitrary"),
    )(q, k, k, v, v, do, o, lse)


FFN_TN = 256
FFN_ACT_TM = (4096, 2048, 1024, 512, 256, 128)
FFN_FWD_CHUNK = 256
FFN_BWD_CHUNK = 128


def _ffn_up(h, up_t, name):
    t, k = h.shape
    tm = _tile(t)

    def body(h_ref, w_ref, o_ref):
        o_ref[...] = _dot(h_ref[...], w_ref[...], _NT).astype(BF16)

    return pl.pallas_call(
        body, name=name, grid=(2, t // tm),
        in_specs=[pl.BlockSpec((tm, k), lambda p, i: (i, 0)), pl.BlockSpec((None, FFN_DIM, k), lambda p, i: (p, 0, 0))],
        out_specs=pl.BlockSpec((None, tm, FFN_DIM), lambda p, i: (p, i, 0)),
        out_shape=jax.ShapeDtypeStruct((2, t, FFN_DIM), BF16), compiler_params=_params("parallel", "parallel"),
    )(h, up_t.reshape(2, FFN_DIM, k))


def _ffn_up_dx(du, up_t, name):
    t = du.shape[1]
    k = up_t.shape[1]
    tm = _tile(t)

    def body(a_ref, b_ref, o_ref):
        o_ref[...] = _dot(a_ref[0], b_ref[0], _NN) + _dot(a_ref[1], b_ref[1], _NN)

    return pl.pallas_call(
        body, name=name, grid=(t // tm,),
        in_specs=[pl.BlockSpec((2, tm, FFN_DIM), lambda i: (0, i, 0)), pl.BlockSpec((2, FFN_DIM, k), lambda i: (0, 0, 0))],
        out_specs=pl.BlockSpec((tm, k), lambda i: (i, 0)), out_shape=jax.ShapeDtypeStruct((t, k), F32),
        compiler_params=_params("parallel"),
    )(du, up_t.reshape(2, FFN_DIM, k))


def _ffn_conv(win, w_ref, b_ref, p):
    x = win.astype(F32)
    x0, x1, x2 = x[FFN_HALO:], pltpu.roll(x, 1, 0)[FFN_HALO:], pltpu.roll(x, 2, 0)[FFN_HALO:]
    return b_ref[p] + w_ref[p, 2:3, :] * x0 + w_ref[p, 1:2, :] * x1 + w_ref[p, 0:1, :] * x2


def _zero_if(cond, v):
    return jnp.where(cond, 0, v).astype(v.dtype)


def _ffn_act(u, dw_w, dw_b, name):
    t = u.shape[1]
    tm = _tile(t, FFN_ACT_TM)
    chunk = min(FFN_FWD_CHUNK, tm)
    hb = tm // FFN_HALO
    main = pl.BlockSpec((2, tm, FFN_TN), lambda i, j: (0, i, j))
    halo = pl.BlockSpec((2, FFN_HALO, FFN_TN), lambda i, j: (0, jnp.maximum(i * hb - 1, 0), j))
    wsp = pl.BlockSpec((2, FFN_CONV_WIDTH, FFN_TN), lambda i, j: (0, 0, j))
    bsp = pl.BlockSpec((2, 1, FFN_TN), lambda i, j: (0, 0, j))

    def body(u_ref, uh_ref, w_ref, b_ref, o_ref, z_ref):
        first = pl.program_id(0) == 0

        def emit(rows, wins):
            za, zb = _ffn_conv(wins[0], w_ref, b_ref, 0), _ffn_conv(wins[1], w_ref, b_ref, 1)
            o_ref[rows, :] = (za * _sigmoid(za) * zb).astype(BF16)
            z_ref[0, rows, :] = za.astype(BF16)
            z_ref[1, rows, :] = zb.astype(BF16)

        emit(pl.ds(0, chunk), [jnp.concatenate([_zero_if(first, uh_ref[p]), u_ref[p, 0:chunk, :]], axis=0) for p in range(2)])

        def step(c, carry):
            s = pl.multiple_of(c * chunk, chunk)
            emit(pl.ds(s, chunk), [u_ref[p, pl.ds(s - FFN_HALO, chunk + FFN_HALO), :] for p in range(2)])
            return carry

        lax.fori_loop(1, tm // chunk, step, 0)

    return pl.pallas_call(
        body, name=name, grid=(t // tm, FFN_DIM // FFN_TN), in_specs=[main, halo, wsp, bsp],
        out_specs=[pl.BlockSpec((tm, FFN_TN), lambda i, j: (i, j)), main],
        out_shape=[jax.ShapeDtypeStruct((t, FFN_DIM), BF16), jax.ShapeDtypeStruct((2, t, FFN_DIM), BF16)],
        compiler_params=_params("parallel", "parallel"),
    )(u, u, dw_w, dw_b)


def _fold8(v):
    return jnp.sum(v.reshape(v.shape[0] // 8, 8, v.shape[1]), axis=0)


def _ffn_act_bwd(u, z, dact, dw_w, name):
    t = u.shape[1]
    tm = _tile(t, FFN_ACT_TM)
    chunk = min(FFN_BWD_CHUNK, tm // 2)
    halo = FFN_HALO
    hb = tm // halo
    nt = t // tm
    last_halo = t // halo - 1
    next_i = lambda i: jnp.minimum((i + 1) * hb, last_halo)
    main = pl.BlockSpec((2, tm, FFN_TN), lambda j, i: (0, i, j))
    nxt = pl.BlockSpec((2, halo, FFN_TN), lambda j, i: (0, next_i(i), j))
    wsp = pl.BlockSpec((2, FFN_CONV_WIDTH, FFN_TN), lambda j, i: (0, 0, j))
    bsp = pl.BlockSpec((2, 1, FFN_TN), lambda j, i: (0, 0, j))

    def body(u_ref, z_ref, zn_ref, da_ref, dan_ref, w_ref, du_ref, dw_ref, db_ref, acc_ref):
        i = pl.program_id(1)
        last = i == nt - 1
        acc_ref[...] = jnp.zeros_like(acc_ref)

        def emit(rows, zs, dact):
            n = chunk + halo
            za, zb, dact = zs[0].astype(F32), zs[1].astype(F32), dact.astype(F32)
            sg = _sigmoid(za)
            dzs = (dact * zb * (sg * (1.0 + za * (1.0 - sg))), dact * (za * sg))
            for p, dz in enumerate(dzs):
                ahead = (dz[:chunk], pltpu.roll(dz, n - 1, 0)[:chunk], pltpu.roll(dz, n - 2, 0)[:chunk])
                um = u_ref[p, rows, :].astype(F32)
                acc_ref[p, FFN_CONV_WIDTH] += _fold8(ahead[0])
                du = None
                for j, dzj in enumerate(ahead):
                    k = FFN_CONV_WIDTH - 1 - j
                    acc_ref[p, k] += _fold8(dzj * um)
                    term = w_ref[p, k:k + 1, :] * dzj
                    du = term if du is None else du + term
                du_ref[p, rows, :] = du.astype(BF16)

        def step(c, carry):
            s = pl.multiple_of(c * chunk, chunk)
            emit(pl.ds(s, chunk), [z_ref[p, pl.ds(s, chunk + halo), :] for p in range(2)], da_ref[pl.ds(s, chunk + halo), :])
            return carry

        lax.fori_loop(0, tm // chunk - 1, step, 0)
        s = tm - chunk
        emit(pl.ds(s, chunk),
             [jnp.concatenate([z_ref[p, s:tm, :], zn_ref[p]], axis=0) for p in range(2)],
             jnp.concatenate([da_ref[s:tm, :], _zero_if(last, dan_ref[...])], axis=0))

        @pl.when(i == 0)
        def _():
            dw_ref[...] = jnp.zeros_like(dw_ref)
            db_ref[...] = jnp.zeros_like(db_ref)

        for p in range(2):
            for k in range(FFN_CONV_WIDTH):
                dw_ref[p, k:k + 1, :] += _colsum(acc_ref[p, k])
            db_ref[p] += _colsum(acc_ref[p, FFN_CONV_WIDTH])

    return pl.pallas_call(
        body, name=name, grid=(FFN_DIM // FFN_TN, nt),
        in_specs=[main, main, nxt, pl.BlockSpec((tm, FFN_TN), lambda j, i: (i, j)),
                  pl.BlockSpec((halo, FFN_TN), lambda j, i: (next_i(i), j)), wsp],
        out_specs=[main, wsp, bsp],
        out_shape=[jax.ShapeDtypeStruct((2, t, FFN_DIM), BF16), jax.ShapeDtypeStruct((2, FFN_CONV_WIDTH, FFN_DIM), F32),
                   jax.ShapeDtypeStruct((2, 1, FFN_DIM), F32)],
        scratch_shapes=[pltpu.VMEM((2, FFN_CONV_WIDTH + 1, 8, FFN_TN), F32)],
        compiler_params=_params("parallel", "arbitrary"),
    )(u, z, z, dact, dact, dw_w)


CONV_TM = 256
CONV_ROWS = 128
CONV_FWD_ROWS = 256
CONV_LANES = 128
CONV_NORM_ROWS = 32


def _glu_window(pa_ref, pah_ref, pg_ref, pgh_ref, scr_ref, first):
    ah, gh = pah_ref[...].astype(F32), pgh_ref[...].astype(F32)
    scr_ref[0:CONV_HALO, :] = jnp.where(first, 0.0, ah * _sigmoid(gh))
    scr_ref[CONV_HALO:, :] = pa_ref[...].astype(F32) * _sigmoid(pg_ref[...].astype(F32))


def _tap_slabs(win, rows, ahead):
    n = win.shape[0]
    for s in range(8):
        ws = win if s == 0 else pltpu.roll(win, n - s if ahead else s, 0)
        for q in range(CONV_HALO // 8):
            o = 8 * q + s
            if o < CONV_WIDTH:
                start = 8 * q if ahead else CONV_HALO - 8 * q
                yield CONV_WIDTH - 1 - o, ws[start:start + rows]


def _conformer_specs(t):
    tm = _tile(t, (CONV_TM, 128))
    hb = tm // CONV_HALO
    d = D_MODEL
    main = lambda c: pl.BlockSpec((tm, d), lambda i: (i, c))
    halo = lambda c: pl.BlockSpec((CONV_HALO, d), lambda i: (jnp.maximum(i * hb - 1, 0), c))
    row = pl.BlockSpec((1, d), lambda i: (0, 0))
    wsp = pl.BlockSpec((CONV_WIDTH, d), lambda i: (0, 0))
    return tm, main, halo, row, wsp


def _conformer_mid(p, dw_w, dw_b, ln_g, ln_b, name):
    t = p.shape[0]
    tm, main, halo, row, wsp = _conformer_specs(t)
    d, lanes, rows = D_MODEL, CONV_LANES, min(CONV_FWD_ROWS, tm)

    def body(pa_ref, pah_ref, pg_ref, pgh_ref, w_ref, b_ref, g_ref, lb_ref, o_ref, dc_ref, scr_ref):
        _glu_window(pa_ref, pah_ref, pg_ref, pgh_ref, scr_ref, pl.program_id(0) == 0)
        for c in range(d // lanes):
            ls = slice(c * lanes, (c + 1) * lanes)

            def taps(r, carry, ls=ls):
                r0 = pl.multiple_of(r * rows, rows)
                acc = jnp.broadcast_to(b_ref[:, ls], (rows, lanes))
                for k, slab in _tap_slabs(scr_ref[pl.ds(r0, rows + CONV_HALO), ls], rows, False):
                    acc = acc + w_ref[k:k + 1, ls] * slab
                dc_ref[pl.ds(r0, rows), ls] = acc
                return carry

            lax.fori_loop(0, tm // rows, taps, 0)

        def norm(r, carry):
            r0 = pl.multiple_of(r * CONV_NORM_ROWS, CONV_NORM_ROWS)
            dc = dc_ref[pl.ds(r0, CONV_NORM_ROWS), :]
            xc = dc - jnp.mean(dc, axis=-1, keepdims=True)
            ln = xc * lax.rsqrt(jnp.mean(xc * xc, axis=-1, keepdims=True) + EPS) * g_ref[...] + lb_ref[...]
            o_ref[pl.ds(r0, CONV_NORM_ROWS), :] = (ln * _sigmoid(ln)).astype(BF16)
            return carry

        lax.fori_loop(0, tm // CONV_NORM_ROWS,norm, 0)

    return pl.pallas_call(
        body, name=name, grid=(t // tm,), in_specs=[main(0), halo(0), main(1), halo(1), wsp, row, row, row],
        out_specs=[main(0), main(0)], out_shape=[jax.ShapeDtypeStruct((t, d), BF16), jax.ShapeDtypeStruct((t, d), F32)],
        scratch_shapes=[pltpu.VMEM((tm + CONV_HALO, d), F32)], compiler_params=_params("parallel"),
    )(p, p, p, p, dw_w, dw_b, ln_g, ln_b)


def _conformer_mid_bwd(p, dc, ds, ln_g, ln_b, name):
    t = p.shape[0]
    tm, main, halo, row, wsp = _conformer_specs(t)
    d, nt = D_MODEL, t // tm
    rows, lanes = CONV_ROWS, CONV_LANES

    def body(pa_ref, pah_ref, pg_ref, pgh_ref, dc_ref, ds_ref, g_ref, lb_ref,
             ddc_ref, dw_ref, db_ref, dg_ref, dlb_ref, scr_ref, wacc_ref, racc_ref):
        i = pl.program_id(0)

        @pl.when(i == 0)
        def _():
            wacc_ref[...] = jnp.zeros_like(wacc_ref)
            racc_ref[...] = jnp.zeros_like(racc_ref)

        _glu_window(pa_ref, pah_ref, pg_ref, pgh_ref, scr_ref, i == 0)

        def norm_bwd(r, carry):
            r0 = pl.multiple_of(r * CONV_NORM_ROWS, CONV_NORM_ROWS)
            dcv = dc_ref[pl.ds(r0, CONV_NORM_ROWS), :]
            xc = dcv - jnp.mean(dcv, axis=-1, keepdims=True)
            rstd = lax.rsqrt(jnp.mean(xc * xc, axis=-1, keepdims=True) + EPS)
            xhat = xc * rstd
            ln = xhat * g_ref[...] + lb_ref[...]
            sg = _sigmoid(ln)
            dln = ds_ref[pl.ds(r0, CONV_NORM_ROWS), :].astype(F32) * (sg * (1.0 + ln * (1.0 - sg)))
            dxh = dln * g_ref[...]
            ddc = rstd * (dxh - jnp.mean(dxh, axis=-1, keepdims=True) - xhat * jnp.mean(dxh * xhat, axis=-1, keepdims=True))
            ddc_ref[pl.ds(r0, CONV_NORM_ROWS), :] = ddc
            racc_ref[0] += _fold8(dln * xhat)
            racc_ref[1] += _fold8(dln)
            racc_ref[2] += _fold8(ddc)
            return carry

        lax.fori_loop(0, tm // CONV_NORM_ROWS,norm_bwd, 0)

        for c in range(d // lanes):
            ls = slice(c * lanes, (c + 1) * lanes)

            def taps(r, carry, ls=ls):
                r0 = pl.multiple_of(r * rows, rows)
                ddc = ddc_ref[pl.ds(r0, rows), ls]
                for k, slab in _tap_slabs(scr_ref[pl.ds(r0, rows + CONV_HALO), ls], rows, False):
                    wacc_ref[k, :, ls] += _fold8(ddc * slab)
                return carry

            lax.fori_loop(0, tm // rows, taps, 0)

        @pl.when(i == nt - 1)
        def _():
            for k in range(CONV_WIDTH):
                dw_ref[k:k + 1, :] = _colsum(wacc_ref[k])
            dg_ref[...] = _colsum(racc_ref[0])
            dlb_ref[...] = _colsum(racc_ref[1])
            db_ref[...] = _colsum(racc_ref[2])

    return pl.pallas_call(
        body, name=name, grid=(nt,), in_specs=[main(0), halo(0), main(1), halo(1), main(0), main(0), row, row],
        out_specs=[main(0), wsp, row, row, row],
        out_shape=[jax.ShapeDtypeStruct((t, d), F32), jax.ShapeDtypeStruct((CONV_WIDTH, d), F32)]
        + [jax.ShapeDtypeStruct((1, d), F32)] * 3,
        scratch_shapes=[pltpu.VMEM((tm + CONV_HALO, d), F32), pltpu.VMEM((CONV_WIDTH, 8, d), F32), pltpu.VMEM((3, 8, d), F32)],
        compiler_params=_params("arbitrary"),
    )(p, p, p, p, dc, ds, ln_g, ln_b)


def _conformer_glu_bwd(p, ddc, dw_w, name):
    t = p.shape[0]
    d = D_MODEL
    tm = _tile(t, (CONV_TM, 128))
    hb = tm // CONV_HALO
    nt = t // tm
    last_halo = t // CONV_HALO - 1
    rows, lanes = CONV_ROWS, CONV_LANES
    col = lambda c: pl.BlockSpec((tm, d), lambda i: (i, c))
    nxt = pl.BlockSpec((CONV_HALO, d), lambda i: (jnp.minimum((i + 1) * hb, last_halo), 0))

    def body(pa_ref, pg_ref, ddc_ref, ddcn_ref, w_ref, dp_ref, db_ref, scr_ref, acc_ref):
        i = pl.program_id(0)

        @pl.when(i == 0)
        def _():
            acc_ref[...] = jnp.zeros_like(acc_ref)

        scr_ref[0:tm, :] = ddc_ref[...]
        scr_ref[tm:, :] = _zero_if(i == nt - 1, ddcn_ref[...])
        for c in range(d // lanes):
            ls = slice(c * lanes, (c + 1) * lanes)
            gs = slice(d + c * lanes, d + (c + 1) * lanes)

            def taps(r, carry, ls=ls, gs=gs):
                r0 = pl.multiple_of(r * rows, rows)
                dglu = None
                for k, slab in _tap_slabs(scr_ref[pl.ds(r0, rows + CONV_HALO), ls], rows, True):
                    term = w_ref[k:k + 1, ls] * slab
                    dglu = term if dglu is None else dglu + term
                a = pa_ref[pl.ds(r0, rows), ls].astype(F32)
                sg = _sigmoid(pg_ref[pl.ds(r0, rows), ls].astype(F32))
                da = (dglu * sg).astype(BF16)
                dg = (dglu * a * sg * (1.0 - sg)).astype(BF16)
                dp_ref[pl.ds(r0, rows), ls] = da
                dp_ref[pl.ds(r0, rows), gs] = dg
                acc_ref[:, ls] += _fold8(da.astype(F32))
                acc_ref[:, gs] += _fold8(dg.astype(F32))
                return carry

            lax.fori_loop(0, tm // rows, taps, 0)

        @pl.when(i == nt - 1)
        def _():
            db_ref[...] = _colsum(acc_ref[...])

    return pl.pallas_call(
        body, name=name, grid=(nt,),
        in_specs=[col(0), col(1), col(0), nxt, pl.BlockSpec((CONV_WIDTH, d), lambda i: (0, 0))],
        out_specs=[pl.BlockSpec((tm, 2 * d), lambda i: (i, 0)), pl.BlockSpec((1, 2 * d), lambda i: (0, 0))],
        out_shape=[jax.ShapeDtypeStruct((t, 2 * d), BF16), jax.ShapeDtypeStruct((1, 2 * d), F32)],
        scratch_shapes=[pltpu.VMEM((tm + CONV_HALO, d), F32), pltpu.VMEM((8, 2 * d), F32)],
        compiler_params=_params("arbitrary"),
    )(p, p, ddc, ddc, dw_w)


def _colsum_call(a, name):
    t, n = a.shape
    tm = _tile(t)

    def body(a_ref, o_ref):
        @pl.when(pl.program_id(0) == 0)
        def _():
            o_ref[...] = jnp.zeros_like(o_ref)

        o_ref[...] += _colsum(a_ref[...].astype(F32))

    return pl.pallas_call(
        body, name=name, grid=(t // tm,), in_specs=[pl.BlockSpec((tm, n), lambda i: (i, 0))],
        out_specs=pl.BlockSpec((1, n), lambda i: (0, 0)), out_shape=jax.ShapeDtypeStruct((1, n), F32),
        compiler_params=_params("arbitrary"),
    )(a)


def _ada_fwd(c_all, w, name):
    rows, d = c_all.shape
    layers, _, n = w.shape
    tn = _tile(n, (256, 128))

    def body(c_ref, w_ref, o_ref):
        c = c_ref[...]
        o_ref[...] = _dot((c * _sigmoid(c)).astype(BF16), w_ref[...].astype(BF16), _NN)

    return pl.pallas_call(
        body, name=name, grid=(layers, n // tn),
        in_specs=[pl.BlockSpec((rows, d), lambda l, j: (0, 0)), pl.BlockSpec((None, d, tn), lambda l, j: (l, 0, j))],
        out_specs=pl.BlockSpec((None, rows, tn), lambda l, j: (l, 0, j)), out_shape=jax.ShapeDtypeStruct((layers, rows, n), F32),
        compiler_params=_params("parallel", "parallel"),
    )(c_all, w)


def _ada_bwd(c_all, dmod, name):
    rows, d = c_all.shape
    layers, _, n = dmod.shape
    tn = _tile(n, (256, 128))

    def body(c_ref, g_ref, o_ref):
        c = c_ref[...]
        o_ref[...] = _dot((c * _sigmoid(c)).astype(BF16), g_ref[...].astype(BF16), _TN)

    return pl.pallas_call(
        body, name=name, grid=(layers, n // tn),
        in_specs=[pl.BlockSpec((rows, d), lambda l, j: (0, 0)), pl.BlockSpec((None, rows, tn), lambda l, j: (l, 0, j))],
        out_specs=pl.BlockSpec((None, d, tn), lambda l, j: (l, 0, j)), out_shape=jax.ShapeDtypeStruct((layers, d, n), F32),
        compiler_params=_params("parallel", "parallel"),
    )(c_all, dmod)


def _sum_in_device_order(own, land, me, name):
    s, r, c = land.shape
    tr = _row_tile(r, 256)
    slot = lambda k: pl.BlockSpec((None, tr, c), lambda i, me_ref: (jnp.where(me_ref[0] == k, (k + 1) % s, k), i, 0))
    own_spec = pl.BlockSpec((tr, c), lambda i, me_ref: (i, 0))

    def body(me_ref, own_ref, *refs):
        o_ref = refs[-1]
        acc = None
        for k, ref in enumerate(refs[:-1]):
            term = jnp.where(me_ref[0] == k, own_ref[...], ref[...]).astype(F32)
            acc = term if acc is None else acc + term
        o_ref[...] = acc

    return pl.pallas_call(
        body, name=name, out_shape=jax.ShapeDtypeStruct((r, c), F32),
        grid_spec=pltpu.PrefetchScalarGridSpec(
            num_scalar_prefetch=1, grid=(r // tr,), in_specs=[own_spec] + [slot(k) for k in range(s)], out_specs=own_spec),
        compiler_params=_params("parallel"),
    )(me, own, *[land] * s)


def _sum_with_own(blocks, land, me, name, layer=0, layers=1, into=None):
    s, r, c = land.shape
    tr = _row_tile(r, 256)
    slot = lambda k: pl.BlockSpec((None, tr, c), lambda i, me_ref: ((me_ref[0] + k) % s, i, 0))
    stacked = [] if into is None else [into]

    def body(me_ref, own_ref, *refs):
        o_ref = refs[-1]
        acc = own_ref[...].astype(F32)
        for ref in refs[:s - 1]:
            acc = acc + ref[...].astype(F32)
        o_ref[...] = acc

    return pl.pallas_call(
        body, name=name, out_shape=jax.ShapeDtypeStruct((layers, r, c), F32),
        grid_spec=pltpu.PrefetchScalarGridSpec(
            num_scalar_prefetch=1, grid=(r // tr,),
            in_specs=[slot(k) for k in range(s)] + [pl.BlockSpec(memory_space=pl.ANY)] * len(stacked),
            out_specs=pl.BlockSpec((None, tr, c), lambda i, me_ref: (layer, i, 0))),
        input_output_aliases={s + 1: 0} if stacked else {},
        compiler_params=_params("parallel"),
    )(me, blocks, *[land] * (s - 1), *stacked)


def _adamw_update(w, g, m, v):
    nm = ADAM_B1 * m + (1.0 - ADAM_B1) * g
    nv = ADAM_B2 * v + (1.0 - ADAM_B2) * (g * g)
    m_hat = nm * (1.0 / (1.0 - ADAM_B1 ** ADAM_STEP))
    v_hat = nv * (1.0 / (1.0 - ADAM_B2 ** ADAM_STEP))
    return -ADAM_LR * (m_hat / (jnp.sqrt(v_hat) + ADAM_EPS) + ADAM_WD * w), nm, nv


def _adamw(w, g, m, v, behind, name):
    l, r, c = w.shape
    tr = _row_tile(r, 256)
    blk = pl.BlockSpec((None, tr, c), lambda k, i: (k, i, 0))
    order = [] if behind is None else [behind]

    def body(w_ref, g_ref, m_ref, v_ref, *rest):
        d_ref, nm_ref, nv_ref = rest[-3:]
        d_ref[...], nm_ref[...], nv_ref[...] = _adamw_update(w_ref[...], g_ref[...], m_ref[...], v_ref[...])

    return pl.pallas_call(
        body, name=name, grid=(l, r // tr), in_specs=[blk] * 4 + [pl.BlockSpec(memory_space=pl.ANY)] * len(order),
        out_specs=[blk] * 3, out_shape=[jax.ShapeDtypeStruct(w.shape, F32)] * 3,
        compiler_params=_params("parallel", "parallel"),
    )(w, g, m, v, *order)


def _adamw_small(ws, gs, ms, vs, name):
    n = len(ws)
    two_d = lambda a: a.reshape(-1, a.shape[-1])

    def body(*refs):
        ins, outs = refs[:4 * n], refs[4 * n:]
        for a in range(n):
            outs[a][...], outs[n + a][...], outs[2 * n + a][...] = _adamw_update(*[ins[k * n + a][...] for k in range(4)])

    res = pl.pallas_call(
        body, name=name, out_shape=[jax.ShapeDtypeStruct(two_d(w).shape, F32) for w in ws] * 3,
    )(*[two_d(a) for a in (*ws, *gs, *ms, *vs)])
    return [[res[k * n + a].reshape(ws[a].shape) for a in range(n)] for k in range(3)]


def _mesh_pos():
    return lax.axis_index("x"), lax.axis_index("y"), lax.axis_index("c")


def _all_gather_vmem(x_shard, name):
    m_per, n = x_shard.shape

    def body(x_ref, out_ref, send_sems, recv_sems, local_sem):
        x, y, c = _mesh_pos()
        me, sibling = (x, y, c), (x, y, 1 - c)
        chips = [(1 - x, y), (x, 1 - y), (1 - x, 1 - y)]

        def rows(px, py, pc):
            return out_ref.at[pl.ds((4 * px + 2 * py + pc) * m_per, m_per), :]

        def copy(k, block, to, src=None):
            return pltpu.make_async_remote_copy(
                src_ref=rows(*block) if src is None else src, dst_ref=rows(*block),
                send_sem=send_sems.at[k], recv_sem=recv_sems.at[k], device_id=to, device_id_type=MESH)

        mine = pltpu.make_async_copy(x_ref, rows(*me), local_sem)
        mine.start()
        first = [copy(0, me, sibling, src=x_ref)]
        first += [copy(1 + j, me, (*chip, c), src=x_ref) for j, chip in enumerate(chips)]
        for cp in first:
            cp.start()
        passed = [copy(4 + j, (*chip, c), sibling) for j, chip in enumerate(chips)]
        for j, chip in enumerate(chips):
            copy(1 + j, (*chip, c), me).wait_recv()
            passed[j].start()
        copy(0, sibling, me).wait_recv()
        for j, chip in enumerate(chips):
            copy(4 + j, (*chip, 1 - c), me).wait_recv()
        for cp in first + passed:
            cp.wait_send()
        mine.wait()

    return pl.pallas_call(
        body, name=name, out_shape=jax.ShapeDtypeStruct((N_DEV * m_per, n), x_shard.dtype),
        in_specs=[pl.BlockSpec(memory_space=pltpu.VMEM)], out_specs=pl.BlockSpec(memory_space=pltpu.VMEM),
        scratch_shapes=[pltpu.SemaphoreType.DMA((7,)), pltpu.SemaphoreType.DMA((7,)), pltpu.SemaphoreType.DMA],
    )(x_shard)


def _all_gather_hbm(shards, name):
    n = len(shards)
    out_shape = [jax.ShapeDtypeStruct((N_DEV,) + s.shape, s.dtype) for s in shards]

    def body(*refs):
        x_refs, out_refs = refs[:n], refs[n:2 * n]
        send_sems, recv_sems, local_sems = refs[2 * n:]
        x, y, c = _mesh_pos()
        me, sibling = (x, y, c), (x, y, 1 - c)
        chips = [(1 - x, y), (x, 1 - y), (1 - x, 1 - y)]

        def blk(a, p):
            return out_refs[a].at[4 * p[0] + 2 * p[1] + p[2]]

        def copy(a, k, block, to, src=None):
            return pltpu.make_async_remote_copy(
                src_ref=blk(a, block) if src is None else src, dst_ref=blk(a, block),
                send_sem=send_sems.at[7 * a + k], recv_sem=recv_sems.at[7 * a + k], device_id=to, device_id_type=MESH)

        mine = [pltpu.make_async_copy(x_refs[a], blk(a, me), local_sems.at[a]) for a in range(n)]
        for cp in mine:
            cp.start()
        first = []
        for a in range(n):
            first.append(copy(a, 0, me, sibling, src=x_refs[a]))
            first += [copy(a, 1 + j, me, (*chip, c), src=x_refs[a]) for j, chip in enumerate(chips)]
        for cp in first:
            cp.start()
        passed = []
        for j, chip in enumerate(chips):
            for a in range(n):
                copy(a, 1 + j, (*chip, c), me).wait_recv()
                fwd = copy(a, 4 + j, (*chip, c), sibling)
                fwd.start()
                passed.append(fwd)
        for a in range(n):
            copy(a, 0, sibling, me).wait_recv()
            for j, chip in enumerate(chips):
                copy(a, 4 + j, (*chip, 1 - c), me).wait_recv()
        for cp in first + passed:
            cp.wait_send()
        for cp in mine:
            cp.wait()

    return pl.pallas_call(
        body, name=name, out_shape=out_shape, in_specs=[pl.BlockSpec(memory_space=pltpu.VMEM)] * n,
        out_specs=[pl.BlockSpec(memory_space=pl.ANY)] * n,
        scratch_shapes=[pltpu.SemaphoreType.DMA((7 * n,)), pltpu.SemaphoreType.DMA((7 * n,)), pltpu.SemaphoreType.DMA((n,))],
    )(*shards)


def _peers(x, y, c):
    flip = lambda v, f: 1 - v if f else v
    return [(flip(x, m & 4), flip(y, m & 2), flip(c, m & 1)) for m in range(1, N_DEV)]


def _dev_index(p):
    return 4 * p[0] + 2 * p[1] + p[2]


def _push_copies(src_refs, land_refs, send_sems, recv_sems, scatter, receive):
    x, y, c = _mesh_pos()
    me = _dev_index((x, y, c))
    copies = []
    for a, (src, land) in enumerate(zip(src_refs, land_refs)):
        for k, p in enumerate(_peers(x, y, c)):
            copies.append(pltpu.make_async_remote_copy(
                src_ref=src.at[_dev_index(p)] if scatter else src, dst_ref=land.at[_dev_index(p) if receive else me],
                send_sem=send_sems.at[7 * a + k], recv_sem=recv_sems.at[7 * a + k], device_id=p, device_id_type=MESH))
    return copies


_HBM = pl.BlockSpec(memory_space=pltpu.HBM)
_SEM = pl.BlockSpec(memory_space=pltpu.SEMAPHORE)
_EFFECT = pltpu.SideEffectType.DATAFLOW_SIDE_EFFECTING


def _pushes_start(srcs, lands, scatter, name):
    n = len(srcs)

    def body(*refs):
        src_refs, land_refs = refs[:n], refs[n:2 * n]
        send_sems, recv_sems = refs[2 * n], refs[2 * n + 1]
        token = refs[-1]
        for cp in _push_copies(src_refs, land_refs, send_sems, recv_sems, scatter, receive=False):
            cp.start()
        token[...] = jnp.zeros_like(token)

    hbm = lambda a: pltpu.HBM(a.shape, a.dtype)
    sems = pltpu.SemaphoreType.DMA((7 * n,))
    outs = pl.pallas_call(
        body, name=name,
        out_shape=(sems, sems, *[hbm(a) for a in srcs], *[hbm(a) for a in lands], jax.ShapeDtypeStruct((8, 128), F32)),
        in_specs=[_HBM] * (2 * n), out_specs=(_SEM, _SEM, *[_HBM] * (2 * n), pl.BlockSpec(memory_space=pltpu.VMEM)),
        input_output_aliases={i: 2 + i for i in range(2 * n)},
        compiler_params=pltpu.CompilerParams(has_side_effects=_EFFECT),
    )(*[pltpu.with_memory_space_constraint(a, pltpu.HBM) for a in (*srcs, *lands)])
    return (outs[0], outs[1], outs[2:2 + n], outs[2 + n:2 + 2 * n], scatter), outs[-1]


def _pushes_wait(handle, after, name):
    send_sems, recv_sems, srcs, lands, scatter = handle
    n = len(srcs)
    after = after if isinstance(after, (tuple, list)) else (after,)

    def body(*refs):
        src_refs, land_refs = refs[:n], refs[n:2 * n]
        for cp in _push_copies(src_refs, land_refs, refs[2 * n], refs[2 * n + 1], scatter, receive=True):
            cp.wait_send()
            cp.wait_recv()

    hbm = lambda a: pltpu.HBM(a.shape, a.dtype)
    outs = pl.pallas_call(
        body, name=name, out_shape=tuple(hbm(a) for a in (*srcs, *lands)),
        in_specs=[_HBM] * (2 * n) + [_SEM, _SEM] + [pl.BlockSpec(memory_space=pl.ANY)] * len(after),
        out_specs=tuple([_HBM] * (2 * n)), input_output_aliases={i: i for i in range(2 * n)},
        compiler_params=pltpu.CompilerParams(has_side_effects=_EFFECT),
    )(*srcs, *lands, send_sems, recv_sems, *after)
    return outs[:n], outs[n:]


def _landing_zones(srcs, behind, name):
    n, nb = len(srcs), len(behind)

    def body(*refs):
        src_refs, land_refs, bufs, sems = refs[:n], refs[n + nb:2 * n + nb], refs[2 * n + nb:3 * n + nb], refs[3 * n + nb]
        me = _dev_index(_mesh_pos())
        load = [pltpu.make_async_copy(src, buf, sems.at[a]) for a, (src, buf) in enumerate(zip(src_refs, bufs))]
        store = [pltpu.make_async_copy(buf, land.at[me], sems.at[a]) for a, (buf, land) in enumerate(zip(bufs, land_refs))]
        for cp in load:
            cp.start()
        for ld, st in zip(load, store):
            ld.wait()
            st.start()
        for cp in store:
            cp.wait()

    any_spec = pl.BlockSpec(memory_space=pl.ANY)
    return pl.pallas_call(
        body, name=name, out_shape=[jax.ShapeDtypeStruct((N_DEV,) + s.shape, s.dtype) for s in srcs],
        in_specs=[any_spec] * (n + nb), out_specs=[any_spec] * n,
        scratch_shapes=[pltpu.VMEM(s.shape, s.dtype) for s in srcs] + [pltpu.SemaphoreType.DMA((n,))],
        compiler_params=pltpu.CompilerParams(vmem_limit_bytes=V7X_VMEM_LIMIT),
    )(*srcs, *behind)


def _ffn_forward(x, mod, norm_g, w, tag):
    sh, sc, gate = mod
    h = _modnorm(x, norm_g, sc, sh, f"{tag}_norm")
    u = _ffn_up(h, w["up_t"], f"{tag}_up")
    act, z = _ffn_act(u, w["dw_w"], w["dw_b"], f"{tag}_act")
    y, x_new = _matmul(act, w["down"], "nn", BF16, f"{tag}_down", resid=(x, gate))
    return x_new, (x, h, u, z, act, y)


def _behind(row, token):
    return row if token is None else row + token[0:1, 0:1]


def _ffn_backward(dx_new, dy, d_gate, saved, mod, norm_g, w, tag, emit, below):
    x, h, u, z, act, _ = saved
    _, sc, _ = mod
    d_down = _matmul_tn_acc(act, dy, f"{tag}_down_dw")
    dact = _matmul(dy, w["down"], "nt", BF16, f"{tag}_down_dx")
    du, d_dw_w, d_dw_b = _ffn_act_bwd(u, z, dact, w["dw_w"], f"{tag}_act_bwd")
    d_up_t = _matmul_tn_acc(du, h, f"{tag}_up_dw").reshape(2 * FFN_DIM, -1)
    token = emit([d_up_t, d_down])
    dh = _ffn_up_dx(du, w["up_t"], f"{tag}_up_dx")
    dx, d_w, d_sh, *dy_below = _modnorm_bwd(x, dh, norm_g, _behind(sc, token), dx_new, below, f"{tag}_norm_bwd")
    return (dx, *dy_below), dict(dw_w=d_dw_w.transpose(1, 0, 2).reshape(FFN_CONV_WIDTH, 2 * FFN_DIM),
                    dw_b=d_dw_b.reshape(1, 2 * FFN_DIM), norm_g=d_w * (1.0 + sc), sh=d_sh, sc=d_w * norm_g, gate=d_gate)


def _mixer_forward(x, mod, norm_g, w, rope, tag):
    sh, sc, gate = mod
    h = _modnorm(x, norm_g, sc, sh, f"{tag}_norm")
    z = _matmul(h, w["w_in_t"], "nt", BF16, f"{tag}_in")
    ya = _gmlp_fwd(z, w["gain"], w["wtril"], w["bias_exp"], f"{tag}_gmlp")
    q, k, v = _qk_prep(z, rope[0], rope[1], w["gq"], w["gk"], w["seg"], f"{tag}_qk")
    outs, lses = zip(*[_attn_fwd(q[b], k[b], v[b], dil, f"{tag}_attn_d{dil}") for b, dil in enumerate(DILATIONS)])
    yb, lse, cat = _attn_merge(outs, lses, ya, f"{tag}_merge")
    y, x_new = _matmul(cat, w["w_out"], "nn", BF16, f"{tag}_out", resid=(x, gate))
    return x_new, (x, h, z, q, k, v, yb, lse, cat, y)


def _mixer_backward(dx_new, dy, d_gate, saved, mod, norm_g, w, rope, tag, emit, below):
    x, h, z, q, k, v, yb, lse, cat, _ = saved
    _, sc, _ = mod
    d_w_out = _matmul_tn_acc(cat, dy, f"{tag}_out_dw")
    dcat = _matmul(dy, w["w_out"], "nt", BF16, f"{tag}_out_dx")
    dyb = _subseq_views(dcat, A_WIDTH // B_WIDTH, f"{tag}_dyb_views")
    dqs, dks, dvs = zip(*[_attn_bwd(q[b], k[b], v[b], dyb[b], yb[b], lse[b], dil, f"{tag}_attn_bwd_d{dil}")
                          for b, dil in enumerate(DILATIONS)])
    dz, d_gq, d_gk = _qk_prep_bwd(z, dqs, dks, dvs, rope[0], rope[1], w["gq"], w["gk"], w["seg"], f"{tag}_qk_bwd")
    dz, d_sp_w, d_gain, d_bias_exp = _gmlp_bwd(
        z, dcat, w["gain"], w["wtril"], w["wtril_t"], w["bias_exp"], dz, f"{tag}_gmlp_bwd")
    d_w_in_t = _matmul_tn_acc(dz, h, f"{tag}_in_dw")
    token = emit([d_w_in_t, d_w_out])
    dh = _matmul(dz, w["w_in_t"], "nn", F32, f"{tag}_in_dx")
    dx, d_w, d_sh, *dy_below = _modnorm_bwd(x, dh, norm_g, _behind(sc, token), dx_new, below, f"{tag}_norm_bwd")
    return (dx, *dy_below), dict(
        vnorm_g=d_gain.reshape(A_GROUPS, GROUP_DIM), spatial_w=d_sp_w,
        spatial_b=d_bias_exp.reshape(CHUNK, A_GROUPS, GROUP_DIM).sum(-1).T,
        q_norm_g=d_gq.reshape(HEADS, HEAD_DIM).sum(0), k_norm_g=d_gk.reshape(HEADS, HEAD_DIM).sum(0),
        norm_g=d_w * (1.0 + sc), sh=d_sh, sc=d_w * norm_g, gate=d_gate)


def _conformer_forward(x, mod, norm_g, w, tag):
    sh, sc, gate = mod
    h = _modnorm(x, norm_g, sc, sh, f"{tag}_norm")
    p = _matmul(h, w["pw1_t"], "nt", BF16, f"{tag}_pw1", bias=w["pw1_b"])
    s, dc = _conformer_mid(p, w["dw_w"], w["dw_b"], w["ln_g"], w["ln_b"], f"{tag}_mid")
    y, x_new = _matmul(s, w["pw2"], "nn", BF16, f"{tag}_pw2", bias=w["pw2_b"], resid=(x, gate))
    return x_new, (x, h, p, dc, s, y)


def _conformer_backward(dx_new, dy, d_gate, saved, mod, norm_g, w, tag, emit, below):
    x, h, p, dc, s, _ = saved
    _, sc, _ = mod
    d_pw2 = _matmul_tn_acc(s, dy, f"{tag}_pw2_dw")
    d_pw2_b = _colsum_call(dy, f"{tag}_pw2_db")
    ds = _matmul(dy, w["pw2"], "nt", BF16, f"{tag}_pw2_dx")
    ddc, d_dw_w, d_dw_b, d_ln_g, d_ln_b = _conformer_mid_bwd(p, dc, ds, w["ln_g"], w["ln_b"], f"{tag}_mid_bwd")
    dp, d_pw1_b = _conformer_glu_bwd(p, ddc, w["dw_w"], f"{tag}_glu_bwd")
    d_pw1_t = _matmul_tn_acc(dp, h, f"{tag}_pw1_dw")
    token = emit([d_pw1_t, d_pw2])
    dh = _matmul(dp, w["pw1_t"], "nn", F32, f"{tag}_pw1_dx")
    dx, d_w, d_sh, *dy_below = _modnorm_bwd(x, dh, norm_g, _behind(sc, token), dx_new, below, f"{tag}_norm_bwd")
    return (dx, *dy_below), dict(pw1_b=d_pw1_b, dw_w=d_dw_w, dw_b=d_dw_b, ln_g=d_ln_g, ln_b=d_ln_b, pw2_b=d_pw2_b, norm_g=d_w * (1.0 + sc), sh=d_sh, sc=d_w * norm_g, gate=d_gate)


def _local_step(x, target, pos, mod, norm_mix_g, norm_ffn_g, mixer_w, conv_w, ffn_w, fetch, emit):
    d = D_MODEL
    inv_freq = 1.0 / (ROPE_THETA ** (jnp.arange(0, HEAD_DIM, 2, dtype=F32) / HEAD_DIM))
    inv_freq = jnp.tile(inv_freq, 2 * HEADS)[None, :]
    sign = jnp.tile(jnp.concatenate([-jnp.ones(HEAD_DIM // 2, F32), jnp.ones(HEAD_DIM // 2, F32)]), HEADS)[None, :]
    rope = _rope_tables(pos, inv_freq, sign, "rope_tables")
    mods = [[mod[l:l + 1, i * d:(i + 1) * d] for i in range(6)] for l in range(2)]
    mix = [(m[0], m[1], m[2]) for m in mods]
    ffn = [(m[3], m[4], m[5]) for m in mods]
    gm = [norm_mix_g[l:l + 1] for l in range(2)]
    gf = [norm_ffn_g[l:l + 1] for l in range(2)]

    mixer_w = {**mixer_w, **fetch("l0_mix", x)}
    x1, s_mix = _mixer_forward(x, mix[0], gm[0], mixer_w, rope, "l0_mix")
    ffn_w0 = {**ffn_w[0], **fetch("l0_ffn", x1)}
    x2, s_ffn0 = _ffn_forward(x1, ffn[0], gf[0], ffn_w0, "l0_ffn")
    conv_w = {**conv_w, **fetch("l1_conv", x2)}
    x3, s_conv = _conformer_forward(x2, mix[1], gm[1], conv_w, "l1_conv")
    ffn_w1 = {**ffn_w[1], **fetch("l1_ffn", x3)}
    x4, s_ffn1 = _ffn_forward(x3, ffn[1], gf[1], ffn_w1, "l1_ffn")
    below = lambda saved, m: (saved[-1], m[2])
    dx, loss, dy, dg = _loss_head(x4, target, below(s_ffn1, ffn[1]), "loss_head")
    (dx, dy, dg), g_ffn1 = _ffn_backward(dx, dy, dg, s_ffn1, ffn[1], gf[1], ffn_w1, "l1_ffn",
                                         functools.partial(emit, "l1_ffn"), below(s_conv, mix[1]))
    (dx, dy, dg), g_conv = _conformer_backward(dx, dy, dg, s_conv, mix[1], gm[1], conv_w, "l1_conv",
                                               functools.partial(emit, "l1_conv"), below(s_ffn0, ffn[0]))
    (dx, dy, dg), g_ffn0 = _ffn_backward(dx, dy, dg, s_ffn0, ffn[0], gf[0], ffn_w0, "l0_ffn",
                                         functools.partial(emit, "l0_ffn"), below(s_mix, mix[0]))
    (dx,), g_mix = _mixer_backward(dx, dy, dg, s_mix, mix[0], gm[0], mixer_w, rope, "l0_mix",
                                   functools.partial(emit, "l0_mix"), None)
    blocks = [g_mix, g_ffn0, g_conv, g_ffn1]
    dmod = jnp.stack([jnp.concatenate([a["sh"], a["sc"], a["gate"], b["sh"], b["sc"], b["gate"]], axis=1)[0]
                      for a, b in ((g_mix, g_ffn0), (g_conv, g_ffn1))])
    return loss, dx, dmod, blocks


def _pack(arrs, rows=8):
    flat = jnp.concatenate([a.reshape(-1).astype(F32) for a in arrs])
    n = flat.shape[0]
    cols = -(-n // (rows * 128)) * 128
    return jnp.pad(flat, (0, rows * cols - n)).reshape(rows, cols)


def _unpack(flat, shapes):
    out, off = [], 0
    for shp in shapes:
        n = math.prod(shp)
        out.append(flat[..., off:off + n].reshape(flat.shape[:-1] + tuple(shp)))
        off += n
    return out


def _take_block(a, idx, size, axis):
    return lax.dynamic_slice_in_dim(a, idx * size, size, axis)


def kernel(x, c, positions, ada_w, ada_b, norm_mix_g, norm_ffn_g, ab_w_in, a_vnorm_g, a_spatial_w, a_spatial_b, b_q_norm_g, b_k_norm_g, ab_w_out, conv_pw1_w, conv_pw1_b, conv_dw_w, conv_dw_b, conv_ln_g, conv_ln_b, conv_pw2_w, conv_pw2_b, ffn_up_w, ffn_dw_w, ffn_dw_b, ffn_down_w, loss_target, m_ada_w, m_ada_b, m_norm_mix_g, m_norm_ffn_g, m_ab_w_in, m_a_vnorm_g, m_a_spatial_w, m_a_spatial_b, m_b_q_norm_g, m_b_k_norm_g, m_ab_w_out, m_conv_pw1_w, m_conv_pw1_b, m_conv_dw_w, m_conv_dw_b, m_conv_ln_g, m_conv_ln_b, m_conv_pw2_w, m_conv_pw2_b, m_ffn_up_w, m_ffn_dw_w, m_ffn_dw_b, m_ffn_down_w, v_ada_w, v_ada_b, v_norm_mix_g, v_norm_ffn_g, v_ab_w_in, v_a_vnorm_g, v_a_spatial_w, v_a_spatial_b, v_b_q_norm_g, v_b_k_norm_g, v_ab_w_out, v_conv_pw1_w, v_conv_pw1_b, v_conv_dw_w, v_conv_dw_b, v_conv_ln_g, v_conv_ln_b, v_conv_pw2_w, v_conv_pw2_b, v_ffn_up_w, v_ffn_dw_w, v_ffn_dw_b, v_ffn_down_w):
    weights = dict(ada_w=ada_w, ada_b=ada_b, norm_mix_g=norm_mix_g, norm_ffn_g=norm_ffn_g, ab_w_in=ab_w_in, a_vnorm_g=a_vnorm_g, a_spatial_w=a_spatial_w, a_spatial_b=a_spatial_b, b_q_norm_g=b_q_norm_g, b_k_norm_g=b_k_norm_g, ab_w_out=ab_w_out, conv_pw1_w=conv_pw1_w, conv_pw1_b=conv_pw1_b, conv_dw_w=conv_dw_w, conv_dw_b=conv_dw_b, conv_ln_g=conv_ln_g, conv_ln_b=conv_ln_b, conv_pw2_w=conv_pw2_w, conv_pw2_b=conv_pw2_b, ffn_up_w=ffn_up_w, ffn_dw_w=ffn_dw_w, ffn_dw_b=ffn_dw_b, ffn_down_w=ffn_down_w)
    mom1 = dict(ada_w=m_ada_w, ada_b=m_ada_b, norm_mix_g=m_norm_mix_g, norm_ffn_g=m_norm_ffn_g, ab_w_in=m_ab_w_in, a_vnorm_g=m_a_vnorm_g, a_spatial_w=m_a_spatial_w, a_spatial_b=m_a_spatial_b, b_q_norm_g=m_b_q_norm_g, b_k_norm_g=m_b_k_norm_g, ab_w_out=m_ab_w_out, conv_pw1_w=m_conv_pw1_w, conv_pw1_b=m_conv_pw1_b, conv_dw_w=m_conv_dw_w, conv_dw_b=m_conv_dw_b, conv_ln_g=m_conv_ln_g, conv_ln_b=m_conv_ln_b, conv_pw2_w=m_conv_pw2_w, conv_pw2_b=m_conv_pw2_b, ffn_up_w=m_ffn_up_w, ffn_dw_w=m_ffn_dw_w, ffn_dw_b=m_ffn_dw_b, ffn_down_w=m_ffn_down_w)
    mom2 = dict(ada_w=v_ada_w, ada_b=v_ada_b, norm_mix_g=v_norm_mix_g, norm_ffn_g=v_norm_ffn_g, ab_w_in=v_ab_w_in, a_vnorm_g=v_a_vnorm_g, a_spatial_w=v_a_spatial_w, a_spatial_b=v_a_spatial_b, b_q_norm_g=v_b_q_norm_g, b_k_norm_g=v_b_k_norm_g, ab_w_out=v_ab_w_out, conv_pw1_w=v_conv_pw1_w, conv_pw1_b=v_conv_pw1_b, conv_dw_w=v_conv_dw_w, conv_dw_b=v_conv_dw_b, conv_ln_g=v_conv_ln_g, conv_ln_b=v_conv_ln_b, conv_pw2_w=v_conv_pw2_w, conv_pw2_b=v_conv_pw2_b, ffn_up_w=v_ffn_up_w, ffn_dw_w=v_ffn_dw_w, ffn_dw_b=v_ffn_dw_b, ffn_down_w=v_ffn_down_w)
    order = list(weights)
    d, f2 = D_MODEL, 2 * FFN_DIM
    t = x.shape[1]
    me = 4 * lax.axis_index("x") + 2 * lax.axis_index("y") + lax.axis_index("c")
    for window, dil in PATTERNS:
        assert window // dil == Q_BLOCK and t % (dil * Q_BLOCK) == 0

    small_in = [c[0], conv_pw1_b[0], conv_dw_w[0], conv_dw_b[0], conv_ln_g[0], conv_ln_b[0], conv_pw2_b[0], ffn_dw_w]
    g1 = _all_gather_vmem(_pack(small_in, rows=8), "gather_small").reshape(N_DEV, -1)
    c_all, pw1_b, dw_w, dw_b, ln_g, ln_b, pw2_b, fdw_w = _unpack(g1, [a.shape for a in small_in])
    pw1_b, dw_b, ln_g, ln_b, pw2_b = [a.reshape(1, -1) for a in (pw1_b, dw_b, ln_g, ln_b, pw2_b)]
    dw_w = dw_w.transpose(1, 0, 2).reshape(CONV_WIDTH, d)
    fdw_w = fdw_w.transpose(1, 2, 0, 3).reshape(2, FFN_CONV_WIDTH, f2)

    c16 = jnp.pad(c_all, ((0, 2 * N_DEV - c_all.shape[0]), (0, 0)))
    part = _ada_fwd(c16, ada_w, "ada_fwd")[:, :N_DEV].transpose(1, 0, 2).reshape(N_DEV, -1)
    g2 = _all_gather_vmem(part, "gather_mod").reshape(N_DEV, N_DEV, 2, -1)
    mod = lax.dynamic_index_in_dim(g2, me, axis=1, keepdims=False).transpose(1, 0, 2).reshape(2, 6 * d) + ada_b

    stages = dict(l0_mix=[ab_w_in[0].T, ab_w_out[0]], l0_ffn=[ffn_up_w[0].T, ffn_down_w[0]],
                  l1_conv=[conv_pw1_w[0].T, conv_pw2_w[0]], l1_ffn=[ffn_up_w[1].T, ffn_down_w[1]])
    stages = {k: [s.astype(BF16) for s in v] for k, v in stages.items()}
    names = dict(l0_mix=("w_in_t", "w_out"), l0_ffn=("up_t", "down"), l1_conv=("pw1_t", "pw2"), l1_ffn=("up_t", "down"))
    ready = {"l0_mix": [a.reshape(-1, d) for a in _all_gather_hbm(stages["l0_mix"], "gather_mixer_weights")]}
    behind = (*ready["l0_mix"], mod)
    arriving = {}
    for stage, group in (("l0_ffn", ("l0_ffn",)), ("l1_conv", ("l1_conv", "l1_ffn"))):
        srcs = [s for g in group for s in stages[g]]
        arriving[stage], token = _pushes_start(
            srcs, _landing_zones(srcs, behind, f"gather_{stage}_zones"), False, f"gather_{stage}_start")
        behind = (token,)
        mod = mod + token[0:1, 0:1]

    def fetch(stage, after):
        if stage in arriving:
            full = [a.reshape(-1, d) for a in _pushes_wait(arriving[stage], after, f"gather_{stage}_wait")[1]]
            ready[stage] = full[:2]
            if stage == "l1_conv":
                ready["l1_ffn"] = full[2:]
        return dict(zip(names[stage], ready[stage]))

    causal = jnp.tril(jnp.ones((CHUNK, CHUNK), bool))
    wtril = jnp.where(causal[None], a_spatial_w[0], 0.0)
    mixer_w = dict(
        gain=a_vnorm_g[0].reshape(1, A_WIDTH), wtril=wtril.astype(BF16),
        wtril_t=wtril.transpose(0, 2, 1).astype(BF16),
        bias_exp=jnp.repeat(a_spatial_b[0].T, GROUP_DIM, axis=1),
        gq=jnp.tile(b_q_norm_g[0], HEADS)[None, :], gk=jnp.tile(b_k_norm_g[0], HEADS)[None, :],
        seg=jnp.kron(jnp.eye(HEADS, dtype=BF16), jnp.ones((HEAD_DIM, HEAD_DIM), BF16)))
    conv_w = dict(pw1_b=pw1_b, dw_w=dw_w, dw_b=dw_b, ln_g=ln_g, ln_b=ln_b, pw2_b=pw2_b)
    ffn_w = [dict(dw_w=fdw_w[l].reshape(FFN_CONV_WIDTH, 2, FFN_DIM).transpose(1, 0, 2), dw_b=ffn_dw_b[l].reshape(2, 1, FFN_DIM))
             for l in range(2)]

    leaving = {}

    def emit(stage, grads):
        blocks = [g.reshape(N_DEV, g.shape[0] // N_DEV, d) for g in grads]
        leaving[stage], token = _pushes_start(
            blocks, [lax.empty(b.shape, b.dtype) for b in blocks], True, f"reduce_{stage}_start")
        return token

    loss, dx, dmod, (g_mix, g_ffn0, g_conv, g_ffn1) = _local_step(
        x[0], loss_target[0], positions[0].astype(F32)[:, None], mod, norm_mix_g, norm_ffn_g, mixer_w, conv_w, ffn_w,
        fetch, emit)

    me_op = me.astype(jnp.int32).reshape(1)

    def reduced(stage, after, layer=0, layers=1, into=(None, None)):
        blocks, lands = _pushes_wait(leaving[stage], after, f"reduce_{stage}_wait")
        return [_sum_with_own(b, a, me_op, f"reduce_{stage}_sum{i}", layer, layers, into[i])
                for i, (b, a) in enumerate(zip(blocks, lands))]

    r_ffn = reduced("l1_ffn", dx, 1, 2)
    r_pw1_t, r_pw2 = reduced("l1_conv", dx)
    r_up_t, r_down = reduced("l0_ffn", dx, 0, 2, r_ffn)

    small_g = [
        dmod, jnp.concatenate([g_mix["norm_g"], g_conv["norm_g"]]), jnp.concatenate([g_ffn0["norm_g"], g_ffn1["norm_g"]]),
        g_mix["vnorm_g"], g_mix["spatial_w"], g_mix["spatial_b"], g_mix["q_norm_g"], g_mix["k_norm_g"],
        g_conv["pw1_b"], g_conv["dw_w"], g_conv["dw_b"], g_conv["ln_g"], g_conv["ln_b"], g_conv["pw2_b"],
        jnp.stack([g_ffn0["dw_w"], g_ffn1["dw_w"]]), jnp.concatenate([g_ffn0["dw_b"], g_ffn1["dw_b"]]), loss]
    packed = _pack(small_g, rows=8)
    small_leaving, token = _pushes_start([packed], [lax.empty((N_DEV,) + packed.shape, F32)], False, "gather_small_grads_start")

    grads = dict(conv_pw2_w=r_pw2, ffn_down_w=r_down)
    grads_t = dict(conv_pw1_w=r_pw1_t, ffn_up_w=r_up_t)
    flip = lambda a: jnp.swapaxes(a, 1, 2)
    delta, new_m, new_v = {}, {}, {}

    def update(name, behind):
        if name in grads_t:
            grads[name] = flip(grads_t[name])
            res = _adamw(flip(weights[name]), grads_t[name], flip(mom1[name]), flip(mom2[name]), behind, f"adamw_{name}")
            delta[name], new_m[name], new_v[name] = [flip(r) for r in res]
        else:
            delta[name], new_m[name], new_v[name] = _adamw(
                weights[name], grads[name], mom1[name], mom2[name], behind, f"adamw_{name}")

    for name in ("conv_pw1_w", "conv_pw2_w", "ffn_up_w", "ffn_down_w"):
        update(name, token)
    r_in_t, r_out = reduced("l0_mix", new_v["ffn_down_w"])
    grads_t["ab_w_in"], grads["ab_w_out"] = r_in_t, r_out
    update("ab_w_in", token)
    update("ab_w_out", token)

    (packed,), (landed,) = _pushes_wait(small_leaving, tuple(new_v.values()), "gather_small_grads_wait")
    total = _sum_in_device_order(packed, landed, me_op, "sum_small_grads")
    (s_dmod, s_mix_g, s_ffn_g, s_vnorm, s_sp_w, s_sp_b, s_gq, s_gk, s_pw1_b, s_dw_w, s_dw_b, s_ln_g, s_ln_b,
     s_pw2_b, s_fdw_w, s_fdw_b, s_loss) = _unpack(total.reshape(-1), [a.shape for a in small_g])
    dmod_all = lax.dynamic_update_slice(
        landed.reshape(N_DEV, -1)[:, :dmod.size].reshape((N_DEV,) + dmod.shape), dmod[None], (me, 0, 0))
    n_ada = ada_w.shape[2]
    dmod16 = jnp.pad(_take_block(dmod_all, me, n_ada, 2), ((0, N_DEV), (0, 0), (0, 0)))
    grads.update(
        ada_w=_ada_bwd(c16, dmod16.transpose(1, 0, 2), "ada_bwd"),
        ada_b=s_dmod, norm_mix_g=s_mix_g, norm_ffn_g=s_ffn_g,
        a_vnorm_g=s_vnorm[None], a_spatial_w=s_sp_w[None], a_spatial_b=s_sp_b[None], b_q_norm_g=s_gq[None],
        b_k_norm_g=s_gk[None],
        conv_pw1_b=_take_block(s_pw1_b, me, conv_pw1_b.shape[1], 1),
        conv_dw_w=_take_block(s_dw_w, me, conv_dw_w.shape[2], 1)[None],
        conv_dw_b=_take_block(s_dw_b, me, conv_dw_b.shape[1], 1), conv_ln_g=_take_block(s_ln_g, me, conv_ln_g.shape[1], 1),
        conv_ln_b=_take_block(s_ln_b, me, conv_ln_b.shape[1], 1),
        conv_pw2_b=_take_block(s_pw2_b, me, conv_pw2_b.shape[1], 1),
        ffn_dw_w=_take_block(s_fdw_w, me, ffn_dw_w.shape[2], 2), ffn_dw_b=s_fdw_b)
    update("ada_w", None)
    large = ("ada_w", "conv_pw1_w", "conv_pw2_w", "ffn_up_w", "ffn_down_w", "ab_w_in", "ab_w_out")
    small = [n for n in order if n not in large]
    res = _adamw_small(*[[src[n] for n in small] for src in (weights, grads, mom1, mom2)], "adamw_small")
    for dst, arrs in zip((delta, new_m, new_v), res):
        dst.update(zip(small, arrs))

    return (s_loss[0, 0], dx[None], *[grads[n] for n in order], *[delta[n] for n in order],
            *[new_m[n] for n in order], *[new_v[n] for n in order])
```
